```python
import jax, jax.numpy as jnp
from jax import lax
import numpy as np

D_MODEL = 1024
BATCH = 8
SEQ = 4096
DEPTH = 2

CHUNK = 64
N_MIXERS = 2
N_POOL_LAYERS = (DEPTH + N_MIXERS - 1) // N_MIXERS
N_GLA_LAYERS = DEPTH // N_MIXERS

POOL_WIDTH = D_MODEL
POOL_WINDOWS = (2, 4, 8, 16)
POOL_GROUPS = len(POOL_WINDOWS)
POOL_GROUP_DIM = POOL_WIDTH // POOL_GROUPS

GLA_HEADS = 4
GLA_KEY_WIDTH = D_MODEL // 2
GLA_VALUE_WIDTH = D_MODEL
GLA_HEAD_K = GLA_KEY_WIDTH // GLA_HEADS
GLA_HEAD_V = GLA_VALUE_WIDTH // GLA_HEADS
GLA_GATE_RANK = 16
GLA_GATE_NORMALIZER = 16.0
GLA_IN_WIDTH = 2 * GLA_KEY_WIDTH + 2 * GLA_VALUE_WIDTH + GLA_GATE_RANK

RMS_EPS = 1e-6

kernel_name = "hybrid_pool_gla_streaming_trunk"


def rms_norm(x, w):
    xf = x.astype(jnp.float32)
    y = xf * lax.rsqrt(jnp.mean(xf * xf, axis=-1, keepdims=True) + RMS_EPS)
    return (y * w.astype(jnp.float32)).astype(x.dtype)


def trailing_mean(u, window):
    seq = u.shape[1]
    cs = jnp.cumsum(u.astype(jnp.float32), axis=1)
    lagged = jnp.pad(cs, ((0, 0), (window, 0), (0, 0)))[:, :seq]
    count = jnp.minimum(jnp.arange(1, seq + 1), window).astype(jnp.float32)
    return ((cs - lagged) / count[None, :, None]).astype(u.dtype)


def pool_mixer(h, in_w, group_w, group_b, scale, out_w):
    b, s, _ = h.shape
    u, gate = jnp.split(h @ in_w, 2, axis=-1)
    ug = u.reshape(b, s, POOL_GROUPS, POOL_GROUP_DIM)
    pooled = jnp.stack([trailing_mean(ug[:, :, g], w) for g, w in enumerate(POOL_WINDOWS)],
                       axis=2) - ug
    mixed = jnp.einsum('bsgc,gcd->bsgd', pooled, group_w) + group_b
    y = mixed.reshape(b, s, POOL_WIDTH) * scale * jax.nn.silu(gate)
    return y @ out_w


def gla_mixer(h, in_w, gk_w, gk_b, head_norm_w, out_w):
    b, s, _ = h.shape
    n_chunks = s // CHUNK
    f32 = jnp.float32
    proj = h @ in_w
    q, k, v, gate, gk_low = jnp.split(
        proj, [GLA_KEY_WIDTH, 2 * GLA_KEY_WIDTH, 2 * GLA_KEY_WIDTH + GLA_VALUE_WIDTH,
               2 * GLA_KEY_WIDTH + 2 * GLA_VALUE_WIDTH], axis=-1)
    log_g = jax.nn.log_sigmoid((gk_low @ gk_w + gk_b).astype(f32)) / GLA_GATE_NORMALIZER

    def to_chunks(t, d):
        return t.astype(f32).reshape(b, n_chunks, CHUNK, GLA_HEADS, d)

    qc = to_chunks(q, GLA_HEAD_K) * (GLA_HEAD_K ** -0.5)
    kc = to_chunks(k, GLA_HEAD_K)
    vc = to_chunks(v, GLA_HEAD_V)
    cum = jnp.cumsum(to_chunks(log_g, GLA_HEAD_K), axis=2)
    cum_last = cum[:, :, -1:]
    e_pos, e_neg = jnp.exp(cum), jnp.exp(-cum)

    fwd = jnp.einsum('bnthk,bnshk->bnhts', qc * e_pos, kc * e_neg)
    bwd = jnp.einsum('bnthk,bnshk->bnhts', qc * e_neg, kc * e_pos)
    idx = jnp.arange(CHUNK)
    lower = idx[:, None] >= idx[None, :]
    scores = jnp.where(lower, fwd, bwd)
    o_intra = jnp.einsum('bnhts,bnshv->bnthv', scores, vc)

    q_dec = qc * e_pos
    k_dec = kc * jnp.exp(cum_last - cum)
    chunk_decay = jnp.exp(cum_last[:, :, 0])

    def step(state, inp):
        q_n, k_n, v_n, d_n = inp
        o_n = jnp.einsum('bthk,bhkv->bthv', q_n, state)
        state = state * d_n[..., None] + jnp.einsum('bthk,bthv->bhkv', k_n, v_n)
        return state, o_n

    xs = (jnp.moveaxis(q_dec, 1, 0), jnp.moveaxis(k_dec, 1, 0),
          jnp.moveaxis(vc, 1, 0), jnp.moveaxis(chunk_decay, 1, 0))
    state0 = jnp.zeros((b, GLA_HEADS, GLA_HEAD_K, GLA_HEAD_V), f32)
    _, o_inter = lax.scan(step, state0, xs)
    o = (o_intra + jnp.moveaxis(o_inter, 0, 1)).reshape(b, s, GLA_HEADS, GLA_HEAD_V)

    o = o * lax.rsqrt(jnp.mean(o * o, axis=-1, keepdims=True) + RMS_EPS) * head_norm_w.astype(f32)
    y = o.reshape(b, s, GLA_VALUE_WIDTH).astype(h.dtype) * jax.nn.silu(gate)
    return y @ out_w


def _fwd_setup_inputs(seed: int = 0) -> dict:
    key = jax.random.key(seed)
    ks = jax.random.split(key, 16)
    nrm = jax.random.normal
    f32 = jnp.float32
    NP, NG = N_POOL_LAYERS, N_GLA_LAYERS
    return {
        "x": nrm(ks[0], (BATCH, SEQ, D_MODEL), f32),
        "norm_w": 1.0 + 0.02 * nrm(ks[1], (DEPTH, D_MODEL), f32),
        "pool_in_w": nrm(ks[2], (NP, D_MODEL, 2 * POOL_WIDTH), f32) * D_MODEL ** -0.5,
        "pool_group_w": nrm(ks[3], (NP, POOL_GROUPS, POOL_GROUP_DIM, POOL_GROUP_DIM), f32) * POOL_GROUP_DIM ** -0.5,
        "pool_group_b": 0.02 * nrm(ks[4], (NP, POOL_GROUPS, POOL_GROUP_DIM), f32),
        "pool_scale": 1.0 + 0.1 * nrm(ks[5], (NP, POOL_WIDTH), f32),
        "pool_out_w": nrm(ks[6], (NP, POOL_WIDTH, D_MODEL), f32) * POOL_WIDTH ** -0.5,
        "gla_in_w": nrm(ks[7], (NG, D_MODEL, GLA_IN_WIDTH), f32) * D_MODEL ** -0.5,
        "gla_gk_w": nrm(ks[8], (NG, GLA_GATE_RANK, GLA_KEY_WIDTH), f32) * GLA_GATE_RANK ** -0.5,
        "gla_gk_b": 0.02 * nrm(ks[9], (NG, GLA_KEY_WIDTH), f32),
        "gla_head_norm_w": 1.0 + 0.02 * nrm(ks[10], (NG, GLA_HEAD_V), f32),
        "gla_out_w": nrm(ks[11], (NG, GLA_VALUE_WIDTH, D_MODEL), f32) * GLA_VALUE_WIDTH ** -0.5,
        "final_norm_w": 1.0 + 0.02 * nrm(ks[12], (D_MODEL,), f32),
    }


def _fwd_reference(x, norm_w, pool_in_w, pool_group_w, pool_group_b, pool_scale, pool_out_w,
              gla_in_w, gla_gk_w, gla_gk_b, gla_head_norm_w, gla_out_w, final_norm_w):
    h = x
    for i in range(DEPTH):
        normed = rms_norm(h, norm_w[i])
        j = i // N_MIXERS
        if i % N_MIXERS == 0:
            h = h + pool_mixer(normed, pool_in_w[j], pool_group_w[j], pool_group_b[j],
                               pool_scale[j], pool_out_w[j])
        else:
            h = h + gla_mixer(normed, gla_in_w[j], gla_gk_w[j], gla_gk_b[j],
                              gla_head_norm_w[j], gla_out_w[j])
    return rms_norm(h, final_norm_w)


import jax as _jax
import jax.numpy as _jnp

TWIN_FORMAT = 'train_step'
FWD_PARAMS = ['x', 'norm_w', 'pool_in_w', 'pool_group_w', 'pool_group_b', 'pool_scale', 'pool_out_w', 'gla_in_w', 'gla_gk_w', 'gla_gk_b', 'gla_head_norm_w', 'gla_out_w', 'final_norm_w']
TWIN_WEIGHTS = ['norm_w', 'pool_in_w', 'pool_group_w', 'pool_group_b', 'pool_scale', 'pool_out_w', 'gla_in_w', 'gla_gk_w', 'gla_gk_b', 'gla_head_norm_w', 'gla_out_w', 'final_norm_w']
TWIN_DIFF_INPUT = 'x'
TWIN_INPUTS = ['x', 'norm_w', 'pool_in_w', 'pool_group_w', 'pool_group_b', 'pool_scale', 'pool_out_w', 'gla_in_w', 'gla_gk_w', 'gla_gk_b', 'gla_head_norm_w', 'gla_out_w', 'final_norm_w', 'loss_target', 'm_norm_w', 'm_pool_in_w', 'm_pool_group_w', 'm_pool_group_b', 'm_pool_scale', 'm_pool_out_w', 'm_gla_in_w', 'm_gla_gk_w', 'm_gla_gk_b', 'm_gla_head_norm_w', 'm_gla_out_w', 'm_final_norm_w', 'v_norm_w', 'v_pool_in_w', 'v_pool_group_w', 'v_pool_group_b', 'v_pool_scale', 'v_pool_out_w', 'v_gla_in_w', 'v_gla_gk_w', 'v_gla_gk_b', 'v_gla_head_norm_w', 'v_gla_out_w', 'v_final_norm_w']
TWIN_OUTPUTS = ['loss', 'grad_x', 'grad_norm_w', 'grad_pool_in_w', 'grad_pool_group_w', 'grad_pool_group_b', 'grad_pool_scale', 'grad_pool_out_w', 'grad_gla_in_w', 'grad_gla_gk_w', 'grad_gla_gk_b', 'grad_gla_head_norm_w', 'grad_gla_out_w', 'grad_final_norm_w', 'delta_norm_w', 'delta_pool_in_w', 'delta_pool_group_w', 'delta_pool_group_b', 'delta_pool_scale', 'delta_pool_out_w', 'delta_gla_in_w', 'delta_gla_gk_w', 'delta_gla_gk_b', 'delta_gla_head_norm_w', 'delta_gla_out_w', 'delta_final_norm_w', 'new_m_norm_w', 'new_m_pool_in_w', 'new_m_pool_group_w', 'new_m_pool_group_b', 'new_m_pool_scale', 'new_m_pool_out_w', 'new_m_gla_in_w', 'new_m_gla_gk_w', 'new_m_gla_gk_b', 'new_m_gla_head_norm_w', 'new_m_gla_out_w', 'new_m_final_norm_w', 'new_v_norm_w', 'new_v_pool_in_w', 'new_v_pool_group_w', 'new_v_pool_group_b', 'new_v_pool_scale', 'new_v_pool_out_w', 'new_v_gla_in_w', 'new_v_gla_gk_w', 'new_v_gla_gk_b', 'new_v_gla_head_norm_w', 'new_v_gla_out_w', 'new_v_final_norm_w']
TWIN_LEAF_KINDS = {'loss': 'loss', 'grad_x': 'grad_x', 'grad_norm_w': 'grad_w', 'grad_pool_in_w': 'grad_w', 'grad_pool_group_w': 'grad_w', 'grad_pool_group_b': 'grad_w', 'grad_pool_scale': 'grad_w', 'grad_pool_out_w': 'grad_w', 'grad_gla_in_w': 'grad_w', 'grad_gla_gk_w': 'grad_w', 'grad_gla_gk_b': 'grad_w', 'grad_gla_head_norm_w': 'grad_w', 'grad_gla_out_w': 'grad_w', 'grad_final_norm_w': 'grad_w', 'delta_norm_w': 'delta_w', 'delta_pool_in_w': 'delta_w', 'delta_pool_group_w': 'delta_w', 'delta_pool_group_b': 'delta_w', 'delta_pool_scale': 'delta_w', 'delta_pool_out_w': 'delta_w', 'delta_gla_in_w': 'delta_w', 'delta_gla_gk_w': 'delta_w', 'delta_gla_gk_b': 'delta_w', 'delta_gla_head_norm_w': 'delta_w', 'delta_gla_out_w': 'delta_w', 'delta_final_norm_w': 'delta_w', 'new_m_norm_w': 'new_m', 'new_m_pool_in_w': 'new_m', 'new_m_pool_group_w': 'new_m', 'new_m_pool_group_b': 'new_m', 'new_m_pool_scale': 'new_m', 'new_m_pool_out_w': 'new_m', 'new_m_gla_in_w': 'new_m', 'new_m_gla_gk_w': 'new_m', 'new_m_gla_gk_b': 'new_m', 'new_m_gla_head_norm_w': 'new_m', 'new_m_gla_out_w': 'new_m', 'new_m_final_norm_w': 'new_m', 'new_v_norm_w': 'new_v', 'new_v_pool_in_w': 'new_v', 'new_v_pool_group_w': 'new_v', 'new_v_pool_group_b': 'new_v', 'new_v_pool_scale': 'new_v', 'new_v_pool_out_w': 'new_v', 'new_v_gla_in_w': 'new_v', 'new_v_gla_gk_w': 'new_v', 'new_v_gla_gk_b': 'new_v', 'new_v_gla_head_norm_w': 'new_v', 'new_v_gla_out_w': 'new_v', 'new_v_final_norm_w': 'new_v'}


def _forward(args):
    return _fwd_reference(*[args[k] for k in FWD_PARAMS])


def _output_shape():
    def fwd():
        inp = _fwd_setup_inputs(0)
        return _fwd_reference(*[inp[k] for k in FWD_PARAMS])
    out = _jax.eval_shape(fwd)
    return out.shape, out.dtype

N_MICROBATCH = 1
ADAM_LR = 0.001
ADAM_B1 = 0.9
ADAM_B2 = 0.999
ADAM_EPS = 1e-08
ADAM_WD = 0.01
ADAM_STEP = 10
PER_EXAMPLE_BATCH_AXIS = {'x': 0, 'loss_target': 0}
SHARED_INPUTS = []
_WEIGHT_DTYPES = {'norm_w': _jnp.float32, 'pool_in_w': _jnp.float32, 'pool_group_w': _jnp.float32, 'pool_group_b': _jnp.float32, 'pool_scale': _jnp.float32, 'pool_out_w': _jnp.float32, 'gla_in_w': _jnp.float32, 'gla_gk_w': _jnp.float32, 'gla_gk_b': _jnp.float32, 'gla_head_norm_w': _jnp.float32, 'gla_out_w': _jnp.float32, 'final_norm_w': _jnp.float32}
MOMENT_SCALE = {'norm_w': 1.636694e-01, 'pool_in_w': 1.078315e-01, 'pool_group_w': 1.064100e-01, 'pool_group_b': 1.225970e-01, 'pool_scale': 1.077958e-01, 'pool_out_w': 1.066297e-01, 'gla_in_w': 9.744800e-02, 'gla_gk_w': 1.282390e-02, 'gla_gk_b': 4.842410e-02, 'gla_head_norm_w': 1.852066e-01, 'gla_out_w': 8.284120e-02, 'final_norm_w': 3.202744e+01}


def _to_microbatches(a, axis):
    t = _jnp.moveaxis(a, axis, 0)
    t = t.reshape((N_MICROBATCH, t.shape[0] // N_MICROBATCH) + t.shape[1:])
    return _jnp.moveaxis(t, 1, axis + 1)


def setup_inputs(seed: int = 0) -> dict:
    inp = _fwd_setup_inputs(seed)
    key = _jax.random.fold_in(_jax.random.key(seed), 7919)
    shape, _ = _output_shape()
    out = dict(inp)
    out["loss_target"] = _jax.random.normal(_jax.random.fold_in(key, 0), shape, _jnp.float32)
    for i, name in enumerate(TWIN_WEIGHTS):
        w = inp[name].astype(_jnp.float32)
        if MOMENT_SCALE is None:
            s = _jnp.sqrt(_jnp.mean(_jnp.square(w)) + 1e-30)
        else:
            s = MOMENT_SCALE[name]
        km, kv = _jax.random.split(_jax.random.fold_in(key, i + 1))
        out[name] = w
        out["m_" + name] = s * _jax.random.normal(km, w.shape, _jnp.float32)
        out["v_" + name] = (s * s) * _jax.random.uniform(kv, w.shape, _jnp.float32, 0.5, 1.5)
    if N_MICROBATCH > 1:
        for name, axis in PER_EXAMPLE_BATCH_AXIS.items():
            out[name] = _to_microbatches(out[name], axis)
    return {'x': out['x'], 'norm_w': out['norm_w'], 'pool_in_w': out['pool_in_w'], 'pool_group_w': out['pool_group_w'], 'pool_group_b': out['pool_group_b'], 'pool_scale': out['pool_scale'], 'pool_out_w': out['pool_out_w'], 'gla_in_w': out['gla_in_w'], 'gla_gk_w': out['gla_gk_w'], 'gla_gk_b': out['gla_gk_b'], 'gla_head_norm_w': out['gla_head_norm_w'], 'gla_out_w': out['gla_out_w'], 'final_norm_w': out['final_norm_w'], 'loss_target': out['loss_target'], 'm_norm_w': out['m_norm_w'], 'm_pool_in_w': out['m_pool_in_w'], 'm_pool_group_w': out['m_pool_group_w'], 'm_pool_group_b': out['m_pool_group_b'], 'm_pool_scale': out['m_pool_scale'], 'm_pool_out_w': out['m_pool_out_w'], 'm_gla_in_w': out['m_gla_in_w'], 'm_gla_gk_w': out['m_gla_gk_w'], 'm_gla_gk_b': out['m_gla_gk_b'], 'm_gla_head_norm_w': out['m_gla_head_norm_w'], 'm_gla_out_w': out['m_gla_out_w'], 'm_final_norm_w': out['m_final_norm_w'], 'v_norm_w': out['v_norm_w'], 'v_pool_in_w': out['v_pool_in_w'], 'v_pool_group_w': out['v_pool_group_w'], 'v_pool_group_b': out['v_pool_group_b'], 'v_pool_scale': out['v_pool_scale'], 'v_pool_out_w': out['v_pool_out_w'], 'v_gla_in_w': out['v_gla_in_w'], 'v_gla_gk_w': out['v_gla_gk_w'], 'v_gla_gk_b': out['v_gla_gk_b'], 'v_gla_head_norm_w': out['v_gla_head_norm_w'], 'v_gla_out_w': out['v_gla_out_w'], 'v_final_norm_w': out['v_final_norm_w']}


def _loss(weights, diff, rest, loss_target):
    with _jax.named_scope("forward"):
        args = {**rest, TWIN_DIFF_INPUT: diff, **{k: w.astype(_WEIGHT_DTYPES[k]) for k, w in weights.items()}}
        y = _forward(args)
    with _jax.named_scope("loss_head"):
        err = _jnp.square(y.astype(_jnp.float32) - loss_target)
        return 0.5 * _jnp.sum(_jnp.mean(err, axis=-1)) if err.ndim else 0.5 * err


def _adamw(w, g, m, v):
    m = ADAM_B1 * m + (1.0 - ADAM_B1) * g
    v = ADAM_B2 * v + (1.0 - ADAM_B2) * _jnp.square(g)
    m_hat = m / (1.0 - ADAM_B1 ** ADAM_STEP)
    v_hat = v / (1.0 - ADAM_B2 ** ADAM_STEP)
    delta = -ADAM_LR * (m_hat / (_jnp.sqrt(v_hat) + ADAM_EPS) + ADAM_WD * w)
    return delta, m, v


def reference(x, norm_w, pool_in_w, pool_group_w, pool_group_b, pool_scale, pool_out_w, gla_in_w, gla_gk_w, gla_gk_b, gla_head_norm_w, gla_out_w, final_norm_w, loss_target, m_norm_w, m_pool_in_w, m_pool_group_w, m_pool_group_b, m_pool_scale, m_pool_out_w, m_gla_in_w, m_gla_gk_w, m_gla_gk_b, m_gla_head_norm_w, m_gla_out_w, m_final_norm_w, v_norm_w, v_pool_in_w, v_pool_group_w, v_pool_group_b, v_pool_scale, v_pool_out_w, v_gla_in_w, v_gla_gk_w, v_gla_gk_b, v_gla_head_norm_w, v_gla_out_w, v_final_norm_w):
    given = dict(x=x, norm_w=norm_w, pool_in_w=pool_in_w, pool_group_w=pool_group_w, pool_group_b=pool_group_b, pool_scale=pool_scale, pool_out_w=pool_out_w, gla_in_w=gla_in_w, gla_gk_w=gla_gk_w, gla_gk_b=gla_gk_b, gla_head_norm_w=gla_head_norm_w, gla_out_w=gla_out_w, final_norm_w=final_norm_w, loss_target=loss_target, m_norm_w=m_norm_w, m_pool_in_w=m_pool_in_w, m_pool_group_w=m_pool_group_w, m_pool_group_b=m_pool_group_b, m_pool_scale=m_pool_scale, m_pool_out_w=m_pool_out_w, m_gla_in_w=m_gla_in_w, m_gla_gk_w=m_gla_gk_w, m_gla_gk_b=m_gla_gk_b, m_gla_head_norm_w=m_gla_head_norm_w, m_gla_out_w=m_gla_out_w, m_final_norm_w=m_final_norm_w, v_norm_w=v_norm_w, v_pool_in_w=v_pool_in_w, v_pool_group_w=v_pool_group_w, v_pool_group_b=v_pool_group_b, v_pool_scale=v_pool_scale, v_pool_out_w=v_pool_out_w, v_gla_in_w=v_gla_in_w, v_gla_gk_w=v_gla_gk_w, v_gla_gk_b=v_gla_gk_b, v_gla_head_norm_w=v_gla_head_norm_w, v_gla_out_w=v_gla_out_w, v_final_norm_w=v_final_norm_w)
    weights = {n: given[n] for n in TWIN_WEIGHTS}
    shared = {n: given[n] for n in SHARED_INPUTS}
    per_example = {n: given[n] for n in ['x']}
    grad_fn = _jax.value_and_grad(_loss, argnums=(0, 1))

    def one_microbatch(ex, loss_target):
        ex = dict(ex)
        diff = ex.pop(TWIN_DIFF_INPUT)
        return grad_fn(weights, diff, {**shared, **ex}, loss_target)

    if N_MICROBATCH == 1:
        loss, (grad_w, grad_x) = one_microbatch(per_example, given["loss_target"])
    else:
        def body(carry, xs):
            loss_sum, grad_sum = carry
            l_k, (gw_k, gx_k) = one_microbatch(xs[0], xs[1])
            with _jax.named_scope("update"):
                return (loss_sum + l_k, _jax.tree.map(_jnp.add, grad_sum, gw_k)), gx_k

        init = (_jnp.zeros((), _jnp.float32), _jax.tree.map(_jnp.zeros_like, weights))
        (loss, grad_w), grad_x = _jax.lax.scan(body, init, (per_example, given["loss_target"]))
    with _jax.named_scope("update"):
        delta_w, new_m, new_v = {}, {}, {}
        for n in TWIN_WEIGHTS:
            delta_w[n], new_m[n], new_v[n] = _adamw(weights[n], grad_w[n], given["m_" + n], given["v_" + n])
    return (loss, grad_x, *[grad_w[n] for n in TWIN_WEIGHTS], *[delta_w[n] for n in TWIN_WEIGHTS],
            *[new_m[n] for n in TWIN_WEIGHTS], *[new_v[n] for n in TWIN_WEIGHTS])
```

```python
import functools

import jax
import jax.numpy as jnp
from jax import lax
from jax.experimental import pallas as pl
from jax.experimental.pallas import tpu as pltpu

F32 = jnp.float32
BF16 = jnp.bfloat16
MESH = pl.DeviceIdType.MESH

N_DEV = 8
D_MODEL = 1024
POOL_WIDTH = 1024
POOL_GROUPS = 4
POOL_GROUP_DIM = 256
POOL_HALO = 16
GLA_HEADS = 4
GLA_HEAD_K = 128
GLA_HEAD_V = 256
GLA_KEY_WIDTH = 512
GLA_VALUE_WIDTH = 1024
GLA_GATE_RANK = 16
GLA_IN_WIDTH = 3088
GLA_IN_PAD = 3200
GLA_LOW_PAD = 128
CHUNK = 64
GATE_NORMALIZER = 16.0
RMS_EPS = 1e-6
Q_SCALE = GLA_HEAD_K ** -0.5

ADAM_LR = 0.001
ADAM_B1 = 0.9
ADAM_B2 = 0.999
ADAM_EPS = 1e-08
ADAM_WD = 0.01
ADAM_STEP = 10

LANES = 128
BF16_ROWS = 16
VMEM_LIMIT = 56 * 1024 * 1024
ROW_TILE = 256


def _dot_nn(a, b):
    return lax.dot_general(a, b, (((1,), (0,)), ((), ())), preferred_element_type=F32)


def _dot_nt(a, b):
    return lax.dot_general(a, b, (((1,), (1,)), ((), ())), preferred_element_type=F32)


def _dot_tn(a, b):
    return lax.dot_general(a, b, (((0,), (0,)), ((), ())), preferred_element_type=F32)


def _rms(x):
    rstd = lax.rsqrt(jnp.mean(x * x, axis=-1, keepdims=True) + RMS_EPS)
    return x * rstd, rstd


def _rms_bwd(dxhat, xhat, rstd):
    return rstd * (dxhat - xhat * jnp.mean(dxhat * xhat, axis=-1, keepdims=True))


def _sigmoid(x):
    return 1.0 / (1.0 + jnp.exp(-x))


def _params(sem=("arbitrary",)):
    return pltpu.CompilerParams(dimension_semantics=sem, vmem_limit_bytes=VMEM_LIMIT)


def _full(shape):
    return pl.BlockSpec(shape, lambda i: (0,) * len(shape))


def _window_sums(ext, forward):
    n = ext.shape[0]
    outs = []
    for g in range(POOL_GROUPS):
        s = ext[:, g * POOL_GROUP_DIM:(g + 1) * POOL_GROUP_DIM]
        for k in range(g + 1):
            shift = (1 << k) if forward else n - (1 << k)
            s = s + pltpu.roll(s, shift, axis=0)
        outs.append(s[:n - POOL_HALO])
    return outs


def _inv_count(row0, tm):
    row = row0 + lax.broadcasted_iota(jnp.int32, (tm, 1), 0)
    return [1.0 / jnp.minimum(row + 1, 2 << g).astype(F32) for g in range(POOL_GROUPS)]


def _pool_mix(u, u_prev, row0, gw_ref, gb):
    tm = u.shape[0]
    sums = _window_sums(jnp.concatenate([u, u_prev], axis=0), True)
    inv = _inv_count(row0, tm)
    pooled, mixed = [], []
    for g in range(POOL_GROUPS):
        ug = u[:, g * POOL_GROUP_DIM:(g + 1) * POOL_GROUP_DIM]
        pg = (sums[g] * inv[g] - ug).astype(BF16)
        pooled.append(pg)
        mixed.append(_dot_nn(pg, gw_ref[g]))
    return pooled, jnp.concatenate(mixed, axis=1) + gb


def _pool_fwd_call(x, nw, w_in, gw, gb, sc, w_out):
    seq = x.shape[0]
    tm = ROW_TILE
    nt = seq // tm

    def body(x_ref, nw_ref, win_ref, gw_ref, gb_ref, sc_ref, wout_ref, h_ref, p_ref, halo_ref):
        i = pl.program_id(0)

        @pl.when(i == 0)
        def _():
            halo_ref[...] = jnp.zeros_like(halo_ref)

        xt = x_ref[...]
        xhat, _ = _rms(xt)
        n = (xhat * nw_ref[...]).astype(BF16)
        p = _dot_nn(n, win_ref[...])
        p_ref[...] = p
        u = p[:, :POOL_WIDTH]
        gate = p[:, POOL_WIDTH:]
        _, mixed = _pool_mix(u, halo_ref[...], i * tm, gw_ref, gb_ref[...])
        halo_ref[...] = u[tm - POOL_HALO:, :]
        y = (mixed * sc_ref[...] * (gate * _sigmoid(gate))).astype(BF16)
        h_ref[...] = xt + _dot_nn(y, wout_ref[...])

    return pl.pallas_call(
        body, name="pool_fwd", grid=(nt,),
        in_specs=[pl.BlockSpec((tm, D_MODEL), lambda i: (i, 0)), _full((1, D_MODEL)),
                  _full((D_MODEL, 2 * POOL_WIDTH)), _full((POOL_GROUPS, POOL_GROUP_DIM, POOL_GROUP_DIM)),
                  _full((1, POOL_WIDTH)), _full((1, POOL_WIDTH)), _full((POOL_WIDTH, D_MODEL))],
        out_specs=[pl.BlockSpec((tm, D_MODEL), lambda i: (i, 0)),
                   pl.BlockSpec((tm, 2 * POOL_WIDTH), lambda i: (i, 0))],
        out_shape=[jax.ShapeDtypeStruct((seq, D_MODEL), F32),
                   jax.ShapeDtypeStruct((seq, 2 * POOL_WIDTH), F32)],
        scratch_shapes=[pltpu.VMEM((POOL_HALO, POOL_WIDTH), F32)],
        compiler_params=_params(),
    )(x, nw, w_in, gw, gb, sc, w_out)


def _pool_bwd_call(dh, p, gw, gb, sc, w_out):
    seq = dh.shape[0]
    tm = ROW_TILE
    nt = seq // tm
    halo_blocks = tm // POOL_HALO

    def body(dh_ref, p_ref, pprev_ref, gw_ref, gb_ref, sc_ref, wout_ref,
             dp_ref, dwout_hbm, dgw_ref, dgb_ref, dsc_ref, carry_ref, dwout_acc):
        i = pl.program_id(0)
        t = nt - 1 - i

        @pl.when(i == 0)
        def _():
            carry_ref[...] = jnp.zeros_like(carry_ref)
            dwout_acc[...] = jnp.zeros_like(dwout_acc)
            dgw_ref[...] = jnp.zeros_like(dgw_ref)
            dgb_ref[...] = jnp.zeros_like(dgb_ref)
            dsc_ref[...] = jnp.zeros_like(dsc_ref)

        p = p_ref[...]
        u = p[:, :POOL_WIDTH]
        gate = p[:, POOL_WIDTH:]
        u_prev = jnp.where(t > 0, pprev_ref[:, :POOL_WIDTH], 0.0)
        pooled, mixed = _pool_mix(u, u_prev, t * tm, gw_ref, gb_ref[...])
        sg = _sigmoid(gate)
        silu = gate * sg
        sc = sc_ref[...]
        dhb = dh_ref[...].astype(BF16)
        y = (mixed * sc * silu).astype(BF16)
        dwout_acc[...] += _dot_tn(y, dhb)
        dy = _dot_nt(dhb, wout_ref[...])
        dmixed = dy * sc * silu
        dsc_ref[...] += jnp.sum(dy * mixed * silu, axis=0, keepdims=True)
        dgate = dy * mixed * sc * (sg * (1.0 + gate * (1.0 - sg)))
        dgb_ref[...] += jnp.sum(dmixed, axis=0, keepdims=True)
        inv = _inv_count(t * tm, tm)
        dpooled, scaled = [], []
        for g in range(POOL_GROUPS):
            dmg = dmixed[:, g * POOL_GROUP_DIM:(g + 1) * POOL_GROUP_DIM].astype(BF16)
            dgw_ref[g] += _dot_tn(pooled[g], dmg)
            dpg = _dot_nt(dmg, gw_ref[g])
            dpooled.append(dpg)
            scaled.append(dpg * inv[g])
        r = jnp.concatenate(scaled, axis=1)
        sums = _window_sums(jnp.concatenate([r, carry_ref[...]], axis=0), False)
        carry_ref[...] = r[:POOL_HALO, :]
        du = jnp.concatenate([sums[g] - dpooled[g] for g in range(POOL_GROUPS)], axis=1)
        dp_ref[...] = jnp.concatenate([du, dgate], axis=1).astype(BF16)

        @pl.when(i == nt - 1)
        def _():
            pltpu.sync_copy(dwout_acc, dwout_hbm)

    rev = lambda i: (nt - 1 - i, 0)
    return pl.pallas_call(
        body, name="pool_bwd", grid=(nt,),
        in_specs=[pl.BlockSpec((tm, D_MODEL), rev), pl.BlockSpec((tm, 2 * POOL_WIDTH), rev),
                  pl.BlockSpec((POOL_HALO, 2 * POOL_WIDTH),
                               lambda i: (jnp.maximum((nt - 1 - i) * halo_blocks - 1, 0), 0)),
                  _full((POOL_GROUPS, POOL_GROUP_DIM, POOL_GROUP_DIM)), _full((1, POOL_WIDTH)),
                  _full((1, POOL_WIDTH)), _full((POOL_WIDTH, D_MODEL))],
        out_specs=[pl.BlockSpec((tm, 2 * POOL_WIDTH), rev), pl.BlockSpec(memory_space=pl.ANY),
                   _full((POOL_GROUPS, POOL_GROUP_DIM, POOL_GROUP_DIM)), _full((1, POOL_WIDTH)),
                   _full((1, POOL_WIDTH))],
        out_shape=[jax.ShapeDtypeStruct((seq, 2 * POOL_WIDTH), BF16),
                   jax.ShapeDtypeStruct((POOL_WIDTH, D_MODEL), F32),
                   jax.ShapeDtypeStruct((POOL_GROUPS, POOL_GROUP_DIM, POOL_GROUP_DIM), F32),
                   jax.ShapeDtypeStruct((1, POOL_WIDTH), F32), jax.ShapeDtypeStruct((1, POOL_WIDTH), F32)],
        scratch_shapes=[pltpu.VMEM((POOL_HALO, POOL_WIDTH), F32), pltpu.VMEM((POOL_WIDTH, D_MODEL), F32)],
        compiler_params=_params(),
    )(dh, p, p, gw, gb, sc, w_out)


def _inproj_bwd_call(name, dproj, h_in, nw, w_in, dres):
    seq = h_in.shape[0]
    width = w_in.shape[1]
    tm = ROW_TILE
    nt = seq // tm

    def body(dproj_ref, h_ref, nw_ref, win_ref, dres_ref, dh_ref, dw_hbm, dnw_ref, dw_acc):
        i = pl.program_id(0)

        @pl.when(i == 0)
        def _():
            dw_acc[...] = jnp.zeros_like(dw_acc)
            dnw_ref[...] = jnp.zeros_like(dnw_ref)

        xhat, rstd = _rms(h_ref[...])
        nw_row = nw_ref[...]
        n = (xhat * nw_row).astype(BF16)
        dpb = dproj_ref[...]
        dw_acc[...] += _dot_tn(n, dpb)
        dn = _dot_nt(dpb, win_ref[...])
        dnw_ref[...] += jnp.sum(dn * xhat, axis=0, keepdims=True)
        dh_ref[...] = _rms_bwd(dn * nw_row, xhat, rstd) + dres_ref[...]

        @pl.when(i == nt - 1)
        def _():
            pltpu.sync_copy(dw_acc, dw_hbm)

    row = lambda i: (i, 0)
    return pl.pallas_call(
        body, name=name, grid=(nt,),
        in_specs=[pl.BlockSpec((tm, width), row), pl.BlockSpec((tm, D_MODEL), row), _full((1, D_MODEL)),
                  _full((D_MODEL, width)), pl.BlockSpec((tm, D_MODEL), row)],
        out_specs=[pl.BlockSpec((tm, D_MODEL), row), pl.BlockSpec(memory_space=pl.ANY), _full((1, D_MODEL))],
        out_shape=[jax.ShapeDtypeStruct((seq, D_MODEL), F32), jax.ShapeDtypeStruct((D_MODEL, width), F32),
                   jax.ShapeDtypeStruct((1, D_MODEL), F32)],
        scratch_shapes=[pltpu.VMEM((D_MODEL, width), F32)],
        compiler_params=_params(),
    )(dproj, h_in, nw, w_in, dres)


def _chunk_scan(x, reverse):
    n = x.shape[0]
    pos = lax.broadcasted_iota(jnp.int32, (n, 1), 0) & (CHUNK - 1)
    k = 1
    while k < CHUNK:
        if reverse:
            x = x + jnp.where(pos < CHUNK - k, pltpu.roll(x, n - k, axis=0), 0.0)
        else:
            x = x + jnp.where(pos >= k, pltpu.roll(x, k, axis=0), 0.0)
        k *= 2
    return x


class _GlaTile:
    def __init__(self, proj, gkw_ref, gkb):
        tm = proj.shape[0]
        self.q = proj[:, :GLA_KEY_WIDTH] * Q_SCALE
        self.k = proj[:, GLA_KEY_WIDTH:2 * GLA_KEY_WIDTH]
        self.v = proj[:, 2 * GLA_KEY_WIDTH:2 * GLA_KEY_WIDTH + GLA_VALUE_WIDTH]
        self.gate = proj[:, 2 * GLA_KEY_WIDTH + GLA_VALUE_WIDTH:2 * GLA_KEY_WIDTH + 2 * GLA_VALUE_WIDTH]
        self.low_b = proj[:, 2 * GLA_KEY_WIDTH + 2 * GLA_VALUE_WIDTH:].astype(BF16)
        self.z = _dot_nn(self.low_b, gkw_ref[...]) + gkb
        log_g = (jnp.minimum(self.z, 0.0) - jnp.log(1.0 + jnp.exp(-jnp.abs(self.z)))) / GATE_NORMALIZER
        self.c = _chunk_scan(log_g, False)
        is_last = lax.broadcasted_iota(jnp.int32, (CHUNK, 1), 0) == CHUNK - 1
        last = [jnp.sum(jnp.where(is_last, self.c[j * CHUNK:(j + 1) * CHUNK, :], 0.0), axis=0, keepdims=True)
                for j in range(tm // CHUNK)]
        self.c_last = last
        c_last_rows = jnp.concatenate([jnp.broadcast_to(r, (CHUNK, GLA_KEY_WIDTH)) for r in last], axis=0)
        self.e_pos = jnp.exp(self.c)
        self.e_neg = jnp.exp(-self.c)
        self.e_rest = jnp.exp(c_last_rows - self.c)
        self.a = self.q * self.e_pos
        self.b = self.k * self.e_neg
        self.cn = self.q * self.e_neg
        self.dp = self.k * self.e_pos
        self.kd = self.k * self.e_rest
        self.a_b, self.b_b, self.cn_b, self.dp_b, self.kd_b, self.v_b = (
            t.astype(BF16) for t in (self.a, self.b, self.cn, self.dp, self.kd, self.v))
        idx_t = lax.broadcasted_iota(jnp.int32, (CHUNK, CHUNK), 0)
        idx_s = lax.broadcasted_iota(jnp.int32, (CHUNK, CHUNK), 1)
        self.lower = idx_t >= idx_s

    @staticmethod
    def rows(j):
        return slice(j * CHUNK, (j + 1) * CHUNK)

    @staticmethod
    def kcols(h):
        return slice(h * GLA_HEAD_K, (h + 1) * GLA_HEAD_K)

    @staticmethod
    def vcols(h):
        return slice(h * GLA_HEAD_V, (h + 1) * GLA_HEAD_V)

    def scores(self, j, h):
        r, kc = self.rows(j), self.kcols(h)
        fwd = _dot_nt(self.a_b[r, kc], self.b_b[r, kc])
        bwd = _dot_nt(self.cn_b[r, kc], self.dp_b[r, kc])
        return jnp.where(self.lower, fwd, bwd).astype(BF16)


def _gla_fwd_call(h1, nw, w_in, gkw, gkb, hw, w_out, wf, target):
    seq = h1.shape[0]
    tm = ROW_TILE
    nt = seq // tm
    cpt = tm // CHUNK
    n_chunks = seq // CHUNK

    def body(h_ref, nw_ref, win_ref, gkw_ref, gkb_ref, hw_ref, wout_ref, wf_ref, tgt_ref,
             dh2_ref, proj_ref, o_ref, st_ref, loss_ref, dwf_ref, state_ref):
        i = pl.program_id(0)

        @pl.when(i == 0)
        def _():
            state_ref[...] = jnp.zeros_like(state_ref)
            loss_ref[...] = jnp.zeros_like(loss_ref)
            dwf_ref[...] = jnp.zeros_like(dwf_ref)

        ht = h_ref[...]
        xhat, _ = _rms(ht)
        n = (xhat * nw_ref[...]).astype(BF16)
        proj = _dot_nn(n, win_ref[...])
        proj_ref[...] = proj
        g = _GlaTile(proj, gkw_ref, gkb_ref[...])
        o_rows = []
        for j in range(cpt):
            r = g.rows(j)
            o_heads = []
            for h in range(GLA_HEADS):
                kc, vc = g.kcols(h), g.vcols(h)
                srows = slice(h * GLA_HEAD_V, (h + 1) * GLA_HEAD_V)
                state = state_ref[srows, :]
                st_ref[j, srows, :] = state
                o_heads.append(_dot_nn(g.scores(j, h), g.v_b[r, vc]) + _dot_nt(g.a_b[r, kc], state.astype(BF16)))
                decay = jnp.exp(g.c_last[j][:, kc])
                state_ref[srows, :] = state * decay + _dot_tn(g.v_b[r, vc], g.kd_b[r, kc])
            o_rows.append(jnp.concatenate(o_heads, axis=1))
        o = jnp.concatenate(o_rows, axis=0)
        o_ref[...] = o
        hw_row = hw_ref[...]
        on = jnp.concatenate([_rms(o[:, g.vcols(h)])[0] for h in range(GLA_HEADS)], axis=1) * hw_row
        y = (on * (g.gate * _sigmoid(g.gate))).astype(BF16)
        h2 = ht + _dot_nn(y, wout_ref[...])
        xhat2, rstd2 = _rms(h2)
        wf_row = wf_ref[...]
        err = xhat2 * wf_row - tgt_ref[...]
        loss_ref[...] += 0.5 * jnp.sum(err * err) / D_MODEL
        dout = err * (1.0 / D_MODEL)
        dwf_ref[...] += jnp.sum(dout * xhat2, axis=0, keepdims=True)
        dh2_ref[...] = _rms_bwd(dout * wf_row, xhat2, rstd2)

    row = lambda i: (i, 0)
    return pl.pallas_call(
        body, name="gla_fwd", grid=(nt,),
        in_specs=[pl.BlockSpec((tm, D_MODEL), row), _full((1, D_MODEL)), _full((D_MODEL, GLA_IN_PAD)),
                  _full((GLA_LOW_PAD, GLA_KEY_WIDTH)), _full((1, GLA_KEY_WIDTH)), _full((1, GLA_VALUE_WIDTH)),
                  _full((GLA_VALUE_WIDTH, D_MODEL)), _full((1, D_MODEL)), pl.BlockSpec((tm, D_MODEL), row)],
        out_specs=[pl.BlockSpec((tm, D_MODEL), row), pl.BlockSpec((tm, GLA_IN_PAD), row),
                   pl.BlockSpec((tm, GLA_VALUE_WIDTH), row),
                   pl.BlockSpec((cpt, GLA_VALUE_WIDTH, GLA_HEAD_K), lambda i: (i, 0, 0)),
                   _full((8, LANES)), _full((1, D_MODEL))],
        out_shape=[jax.ShapeDtypeStruct((seq, D_MODEL), F32), jax.ShapeDtypeStruct((seq, GLA_IN_PAD), F32),
                   jax.ShapeDtypeStruct((seq, GLA_VALUE_WIDTH), F32),
                   jax.ShapeDtypeStruct((n_chunks, GLA_VALUE_WIDTH, GLA_HEAD_K), F32),
                   jax.ShapeDtypeStruct((8, LANES), F32), jax.ShapeDtypeStruct((1, D_MODEL), F32)],
        scratch_shapes=[pltpu.VMEM((GLA_VALUE_WIDTH, GLA_HEAD_K), F32)],
        compiler_params=_params(),
    )(h1, nw, w_in, gkw, gkb, hw, w_out, wf, target)


def _gla_bwd_call(dh2, proj, o, states, gkw, gkb, hw, w_out):
    seq = dh2.shape[0]
    tm = ROW_TILE
    nt = seq // tm
    cpt = tm // CHUNK

    def body(dh_ref, proj_ref, o_ref, st_ref, gkw_ref, gkb_ref, hw_ref, wout_ref,
             dproj_ref, dwout_hbm, dhw_ref, dgkw_ref, dgkb_ref, dstate_ref, dwout_acc):
        i = pl.program_id(0)

        @pl.when(i == 0)
        def _():
            dstate_ref[...] = jnp.zeros_like(dstate_ref)
            dwout_acc[...] = jnp.zeros_like(dwout_acc)
            dhw_ref[...] = jnp.zeros_like(dhw_ref)
            dgkw_ref[...] = jnp.zeros_like(dgkw_ref)
            dgkb_ref[...] = jnp.zeros_like(dgkb_ref)

        g = _GlaTile(proj_ref[...], gkw_ref, gkb_ref[...])
        dhb = dh_ref[...].astype(BF16)
        o = o_ref[...]
        hw_row = hw_ref[...]
        dy = _dot_nt(dhb, wout_ref[...])
        sg = _sigmoid(g.gate)
        silu = g.gate * sg
        don = dy * silu
        on_parts, do_parts, dhw_parts = [], [], []
        for h in range(GLA_HEADS):
            vc = g.vcols(h)
            xh, rs = _rms(o[:, vc])
            on_parts.append(xh * hw_row[:, vc])
            dhw_parts.append(jnp.sum(don[:, vc] * xh, axis=0, keepdims=True))
            do_parts.append(_rms_bwd(don[:, vc] * hw_row[:, vc], xh, rs))
        on = jnp.concatenate(on_parts, axis=1)
        dwout_acc[...] += _dot_tn((on * silu).astype(BF16), dhb)
        dhw_ref[...] += jnp.concatenate(dhw_parts, axis=1)
        dgate = dy * on * (sg * (1.0 + g.gate * (1.0 - sg)))
        do_b = jnp.concatenate(do_parts, axis=1).astype(BF16)

        last_row = lax.broadcasted_iota(jnp.int32, (CHUNK, 1), 0) == CHUNK - 1
        dq_rows, dk_rows, dv_rows, dc_rows = [None] * cpt, [None] * cpt, [None] * cpt, [None] * cpt
        for j in reversed(range(cpt)):
            r = g.rows(j)
            dq_h, dk_h, dv_h, dc_h = [], [], [], []
            for h in range(GLA_HEADS):
                kc, vc = g.kcols(h), g.vcols(h)
                srows = slice(h * GLA_HEAD_V, (h + 1) * GLA_HEAD_V)
                state = st_ref[j, srows, :]
                dstate = dstate_ref[srows, :]
                dstate_b = dstate.astype(BF16)
                do_c = do_b[r, vc]
                scores = g.scores(j, h)
                dscores = _dot_nt(do_c, g.v_b[r, vc])
                dfwd = jnp.where(g.lower, dscores, 0.0).astype(BF16)
                dbwd = jnp.where(g.lower, 0.0, dscores).astype(BF16)
                dv_h.append(_dot_tn(scores, do_c) + _dot_nt(g.kd_b[r, kc], dstate_b))
                da = _dot_nn(dfwd, g.b_b[r, kc]) + _dot_nn(do_c, state.astype(BF16))
                db = _dot_tn(dfwd, g.a_b[r, kc])
                dcn = _dot_nn(dbwd, g.dp_b[r, kc])
                ddp = _dot_tn(dbwd, g.cn_b[r, kc])
                dkd = _dot_nn(g.v_b[r, vc], dstate_b)
                decay = jnp.exp(g.c_last[j][:, kc])
                dstate_ref[srows, :] = _dot_tn(do_c, g.a_b[r, kc]) + dstate * decay
                kd_c = g.kd[r, kc]
                dkd_kd = dkd * kd_c
                dc_last = (jnp.sum(dkd_kd, axis=0, keepdims=True)
                           + decay * jnp.sum(state * dstate, axis=0, keepdims=True))
                dq_h.append(Q_SCALE * (da * g.e_pos[r, kc] + dcn * g.e_neg[r, kc]))
                dk_h.append(db * g.e_neg[r, kc] + ddp * g.e_pos[r, kc] + dkd * g.e_rest[r, kc])
                dc = (da * g.a[r, kc] - db * g.b[r, kc] - dcn * g.cn[r, kc] + ddp * g.dp[r, kc] - dkd_kd)
                dc_h.append(dc + jnp.where(last_row, dc_last, 0.0))
            dq_rows[j] = jnp.concatenate(dq_h, axis=1)
            dk_rows[j] = jnp.concatenate(dk_h, axis=1)
            dv_rows[j] = jnp.concatenate(dv_h, axis=1)
            dc_rows[j] = jnp.concatenate(dc_h, axis=1)
        dq = jnp.concatenate(dq_rows, axis=0)
        dk = jnp.concatenate(dk_rows, axis=0)
        dv = jnp.concatenate(dv_rows, axis=0)
        dlog_g = _chunk_scan(jnp.concatenate(dc_rows, axis=0), True)
        dz = dlog_g * (1.0 / GATE_NORMALIZER) * (1.0 - _sigmoid(g.z))
        dzb = dz.astype(BF16)
        dgkb_ref[...] += jnp.sum(dz, axis=0, keepdims=True)
        dgkw_ref[...] += _dot_tn(g.low_b, dzb)
        dlow = _dot_nt(dzb, gkw_ref[...])
        dproj_ref[...] = jnp.concatenate([dq, dk, dv, dgate, dlow], axis=1).astype(BF16)

        @pl.when(i == nt - 1)
        def _():
            pltpu.sync_copy(dwout_acc, dwout_hbm)

    rev = lambda i: (nt - 1 - i, 0)
    return pl.pallas_call(
        body, name="gla_bwd", grid=(nt,),
        in_specs=[pl.BlockSpec((tm, D_MODEL), rev), pl.BlockSpec((tm, GLA_IN_PAD), rev),
                  pl.BlockSpec((tm, GLA_VALUE_WIDTH), rev),
                  pl.BlockSpec((cpt, GLA_VALUE_WIDTH, GLA_HEAD_K), lambda i: (nt - 1 - i, 0, 0)),
                  _full((GLA_LOW_PAD, GLA_KEY_WIDTH)), _full((1, GLA_KEY_WIDTH)), _full((1, GLA_VALUE_WIDTH)),
                  _full((GLA_VALUE_WIDTH, D_MODEL))],
        out_specs=[pl.BlockSpec((tm, GLA_IN_PAD), rev), pl.BlockSpec(memory_space=pl.ANY),
                   _full((1, GLA_VALUE_WIDTH)), _full((GLA_LOW_PAD, GLA_KEY_WIDTH)), _full((1, GLA_KEY_WIDTH))],
        out_shape=[jax.ShapeDtypeStruct((seq, GLA_IN_PAD), BF16), jax.ShapeDtypeStruct((GLA_VALUE_WIDTH, D_MODEL), F32),
                   jax.ShapeDtypeStruct((1, GLA_VALUE_WIDTH), F32), jax.ShapeDtypeStruct((GLA_LOW_PAD, GLA_KEY_WIDTH), F32),
                   jax.ShapeDtypeStruct((1, GLA_KEY_WIDTH), F32)],
        scratch_shapes=[pltpu.VMEM((GLA_VALUE_WIDTH, GLA_HEAD_K), F32), pltpu.VMEM((GLA_VALUE_WIDTH, D_MODEL), F32)],
        compiler_params=_params(),
    )(dh2, proj, o, states, gkw, gkb, hw, w_out)


def _local_step(x, target, w):
    h1, p = _pool_fwd_call(x, w["nw0"], w["pool_in"], w["pool_gw"], w["pool_gb"], w["pool_sc"], w["pool_out"])
    dh2, proj, o, states, loss, dwf = _gla_fwd_call(
        h1, w["nw1"], w["gla_in"], w["gla_gkw"], w["gla_gkb"], w["gla_hw"], w["gla_out"], w["wf"], target)
    dproj, d_gla_out, dhw, dgkw, dgkb = _gla_bwd_call(
        dh2, proj, o, states, w["gla_gkw"], w["gla_gkb"], w["gla_hw"], w["gla_out"])
    dh1, d_gla_in, dnw1 = _inproj_bwd_call("gla_in_bwd", dproj, h1, w["nw1"], w["gla_in"], dh2)
    dp, d_pool_out, dgw, dgb, dsc = _pool_bwd_call(dh1, p, w["pool_gw"], w["pool_gb"], w["pool_sc"], w["pool_out"])
    dx, d_pool_in, dnw0 = _inproj_bwd_call("pool_in_bwd", dp, x, w["nw0"], w["pool_in"], dh1)
    grads = {
        "norm_w": jnp.concatenate([dnw0, dnw1], axis=0),
        "pool_in_w": d_pool_in,
        "pool_group_w": dgw,
        "pool_group_b": dgb.reshape(POOL_GROUPS, POOL_GROUP_DIM),
        "pool_scale": dsc,
        "pool_out_w": d_pool_out,
        "gla_in_w": d_gla_in[:, :GLA_IN_WIDTH],
        "gla_gk_w": dgkw[:GLA_GATE_RANK],
        "gla_gk_b": dgkb,
        "gla_head_norm_w": dhw.reshape(GLA_HEADS, GLA_HEAD_V).sum(axis=0, keepdims=True),
        "gla_out_w": d_gla_out,
        "final_norm_w": dwf.reshape(D_MODEL),
    }
    return loss[0, 0], dx, grads


SHARDED = ("pool_in_w", "pool_group_w", "pool_group_b", "pool_out_w", "gla_in_w", "gla_gk_w", "gla_gk_b",
           "gla_head_norm_w", "gla_out_w")
REPLICATED = ("norm_w", "pool_scale", "final_norm_w")
SHARD_SHAPES = {
    "pool_in_w": (1, 1024, 256), "pool_group_w": (1, 4, 32, 256), "pool_group_b": (1, 4, 32),
    "pool_out_w": (1, 128, 1024), "gla_in_w": (1, 1024, 386), "gla_gk_w": (1, 16, 64), "gla_gk_b": (1, 64),
    "gla_head_norm_w": (1, 32), "gla_out_w": (1, 128, 1024),
}
MATMUL_WEIGHTS = ("pool_in_w", "pool_group_w", "pool_out_w", "gla_in_w", "gla_gk_w", "gla_out_w")
SMALL_F32 = ("pool_group_b", "gla_gk_b", "gla_head_norm_w")


def _size(shape):
    n = 1
    for s in shape:
        n *= s
    return n


def _rows(name):
    r = -(-_size(SHARD_SHAPES[name]) // LANES)
    return -(-r // BF16_ROWS) * BF16_ROWS


ADAM_BLOCK = 512
ADAM_ROWS = -(-sum(_rows(n) for n in SHARDED) // ADAM_BLOCK) * ADAM_BLOCK
REP_ROWS = 8
REP_AS_BF16_ROWS = REP_ROWS * D_MODEL * 2 // LANES
GATHER_ROWS = sum(_rows(n) for n in MATMUL_WEIGHTS) + BF16_ROWS


def _slab(a, rows, lead=()):
    flat = a.reshape(lead + (-1,))
    pad = rows * LANES - flat.shape[-1]
    if pad:
        flat = jnp.pad(flat, [(0, 0)] * len(lead) + [(0, pad)])
    return flat.reshape(lead + (rows, LANES))


def _pack(parts, names, total_rows, lead=()):
    slabs = [_slab(parts[n], _rows(n), lead) for n in names]
    used = sum(_rows(n) for n in names)
    if total_rows > used:
        slabs.append(jnp.zeros(lead + (total_rows - used, LANES), slabs[0].dtype))
    return jnp.concatenate(slabs, axis=len(lead))


def _unpack(slab, names, lead=()):
    out, r0 = {}, 0
    for n in names:
        r = _rows(n)
        part = slab[..., r0:r0 + r, :].reshape(lead + (-1,))[..., :_size(SHARD_SHAPES[n])]
        out[n] = part.reshape(lead + SHARD_SHAPES[n])
        r0 += r
    return out


def _f32_bits_as_bf16(a, lead=()):
    return lax.bitcast_convert_type(a, BF16).reshape(lead + (-1,))


def _bf16_bits_as_f32(a, lead=()):
    return lax.bitcast_convert_type(a.reshape(lead + (-1, 2)), F32)


def _position():
    x, y, c = lax.axis_index("x"), lax.axis_index("y"), lax.axis_index("c")
    return x, y, c


def _all_gather_call(shard):
    rows = shard.shape[0]

    def body(x_ref, out_ref, send_sems, recv_sems, local_sem):
        x, y, c = _position()
        me, sibling = (x, y, c), (x, y, 1 - c)
        chips = [(1 - x, y), (x, 1 - y), (1 - x, 1 - y)]

        def slot(px, py, pc):
            return out_ref.at[4 * px + 2 * py + pc]

        def copy(k, block, to, src=None):
            return pltpu.make_async_remote_copy(
                src_ref=slot(*block) if src is None else src, dst_ref=slot(*block),
                send_sem=send_sems.at[k], recv_sem=recv_sems.at[k], device_id=to, device_id_type=MESH)

        mine = pltpu.make_async_copy(x_ref, slot(*me), local_sem)
        mine.start()
        first = [copy(0, me, sibling, src=x_ref)]
        first += [copy(1 + j, me, (*chip, c), src=x_ref) for j, chip in enumerate(chips)]
        for cp in first:
            cp.start()
        passed = [copy(4 + j, (*chip, c), sibling) for j, chip in enumerate(chips)]
        for j, chip in enumerate(chips):
            copy(1 + j, (*chip, c), me).wait_recv()
            passed[j].start()
        copy(0, sibling, me).wait_recv()
        for j, chip in enumerate(chips):
            copy(4 + j, (*chip, 1 - c), me).wait_recv()
        for cp in first + passed:
            cp.wait_send()
        mine.wait()

    return pl.pallas_call(
        body, name="weights_all_gather",
        in_specs=[pl.BlockSpec(memory_space=pl.ANY)], out_specs=pl.BlockSpec(memory_space=pl.ANY),
        out_shape=jax.ShapeDtypeStruct((N_DEV, rows, LANES), shard.dtype),
        scratch_shapes=[pltpu.SemaphoreType.DMA((7,)), pltpu.SemaphoreType.DMA((7,)), pltpu.SemaphoreType.DMA],
    )(shard)


def _all_to_all_call(send):
    rows = send.shape[1]

    def body(send_ref, land_ref, send_sems, recv_sems, local_sem):
        x, y, c = _position()
        me = 4 * x + 2 * y + c
        mine = pltpu.make_async_copy(send_ref.at[me], land_ref.at[me], local_sem)
        mine.start()
        copies = []
        for k in range(1, N_DEV):
            px, py, pc = x ^ (k >> 2), y ^ ((k >> 1) & 1), c ^ (k & 1)
            copies.append(pltpu.make_async_remote_copy(
                src_ref=send_ref.at[4 * px + 2 * py + pc], dst_ref=land_ref.at[me],
                send_sem=send_sems.at[k - 1], recv_sem=recv_sems.at[k - 1],
                device_id=(px, py, pc), device_id_type=MESH))
        for cp in copies:
            cp.start()
        for cp in copies:
            cp.wait_recv()
        for cp in copies:
            cp.wait_send()
        mine.wait()

    return pl.pallas_call(
        body, name="grads_all_to_all",
        in_specs=[pl.BlockSpec(memory_space=pl.ANY)], out_specs=pl.BlockSpec(memory_space=pl.ANY),
        out_shape=jax.ShapeDtypeStruct((N_DEV, rows, LANES), send.dtype),
        scratch_shapes=[pltpu.SemaphoreType.DMA((7,)), pltpu.SemaphoreType.DMA((7,)), pltpu.SemaphoreType.DMA],
    )(send)


def _adamw(w, g, m, v):
    m = ADAM_B1 * m + (1.0 - ADAM_B1) * g
    v = ADAM_B2 * v + (1.0 - ADAM_B2) * (g * g)
    m_hat = m / (1.0 - ADAM_B1 ** ADAM_STEP)
    v_hat = v / (1.0 - ADAM_B2 ** ADAM_STEP)
    delta = -ADAM_LR * (m_hat / (jnp.sqrt(v_hat) + ADAM_EPS) + ADAM_WD * w)
    return delta, m, v


def _adamw_call(name, parts, w, m, v, block_rows):
    rows, lanes = w.shape
    nb = rows // block_rows

    def body(parts_ref, w_ref, m_ref, v_ref, g_ref, delta_ref, m_out, v_out):
        g = parts_ref[0].astype(F32)
        for s in range(1, N_DEV):
            g = g + parts_ref[s].astype(F32)
        delta, m_new, v_new = _adamw(w_ref[...], g, m_ref[...], v_ref[...])
        g_ref[...] = g
        delta_ref[...] = delta
        m_out[...] = m_new
        v_out[...] = v_new

    blk = pl.BlockSpec((block_rows, lanes), lambda i: (i, 0))
    return pl.pallas_call(
        body, name=name, grid=(nb,),
        in_specs=[pl.BlockSpec((N_DEV, block_rows, lanes), lambda i: (0, i, 0)), blk, blk, blk],
        out_specs=[blk, blk, blk, blk],
        out_shape=[jax.ShapeDtypeStruct((rows, lanes), F32)] * 4,
        compiler_params=_params(("parallel",)),
    )(parts, w, m, v)


def kernel(x, norm_w, pool_in_w, pool_group_w, pool_group_b, pool_scale, pool_out_w, gla_in_w, gla_gk_w, gla_gk_b, gla_head_norm_w, gla_out_w, final_norm_w, loss_target, m_norm_w, m_pool_in_w, m_pool_group_w, m_pool_group_b, m_pool_scale, m_pool_out_w, m_gla_in_w, m_gla_gk_w, m_gla_gk_b, m_gla_head_norm_w, m_gla_out_w, m_final_norm_w, v_norm_w, v_pool_in_w, v_pool_group_w, v_pool_group_b, v_pool_scale, v_pool_out_w, v_gla_in_w, v_gla_gk_w, v_gla_gk_b, v_gla_head_norm_w, v_gla_out_w, v_final_norm_w):
    w_shard = dict(pool_in_w=pool_in_w, pool_group_w=pool_group_w, pool_group_b=pool_group_b, pool_out_w=pool_out_w,
                   gla_in_w=gla_in_w, gla_gk_w=gla_gk_w, gla_gk_b=gla_gk_b, gla_head_norm_w=gla_head_norm_w,
                   gla_out_w=gla_out_w)
    m_shard = dict(pool_in_w=m_pool_in_w, pool_group_w=m_pool_group_w, pool_group_b=m_pool_group_b,
                   pool_out_w=m_pool_out_w, gla_in_w=m_gla_in_w, gla_gk_w=m_gla_gk_w, gla_gk_b=m_gla_gk_b,
                   gla_head_norm_w=m_gla_head_norm_w, gla_out_w=m_gla_out_w)
    v_shard = dict(pool_in_w=v_pool_in_w, pool_group_w=v_pool_group_w, pool_group_b=v_pool_group_b,
                   pool_out_w=v_pool_out_w, gla_in_w=v_gla_in_w, gla_gk_w=v_gla_gk_w, gla_gk_b=v_gla_gk_b,
                   gla_head_norm_w=v_gla_head_norm_w, gla_out_w=v_gla_out_w)

    small = jnp.concatenate([w_shard[n].reshape(-1) for n in SMALL_F32])
    small = jnp.pad(small, (0, BF16_ROWS * LANES // 2 - small.shape[0]))
    gather_in = jnp.concatenate(
        [_pack({n: w_shard[n].astype(BF16) for n in MATMUL_WEIGHTS}, MATMUL_WEIGHTS, GATHER_ROWS - BF16_ROWS),
         _f32_bits_as_bf16(small).reshape(BF16_ROWS, LANES)], axis=0)
    gathered = _all_gather_call(gather_in)
    wg = _unpack(gathered[:, :GATHER_ROWS - BF16_ROWS], MATMUL_WEIGHTS, (N_DEV,))
    small_all = _bf16_bits_as_f32(gathered[:, GATHER_ROWS - BF16_ROWS:], (N_DEV,))
    gb_all = small_all[:, :128].reshape(N_DEV, POOL_GROUPS, 32)
    gkb_all = small_all[:, 128:192]
    hw_all = small_all[:, 192:224]
    gla_in_full = jnp.transpose(wg["gla_in_w"][:, 0], (1, 0, 2)).reshape(D_MODEL, GLA_IN_WIDTH)
    gkw_full = jnp.transpose(wg["gla_gk_w"][:, 0], (1, 0, 2)).reshape(GLA_GATE_RANK, GLA_KEY_WIDTH)
    hw_full = hw_all.reshape(1, GLA_HEAD_V)
    full = {
        "nw0": norm_w[0:1], "nw1": norm_w[1:2], "wf": final_norm_w.reshape(1, D_MODEL),
        "pool_in": jnp.transpose(wg["pool_in_w"][:, 0], (1, 0, 2)).reshape(D_MODEL, 2 * POOL_WIDTH),
        "pool_gw": jnp.transpose(wg["pool_group_w"][:, 0], (1, 0, 2, 3)).reshape(
            POOL_GROUPS, POOL_GROUP_DIM, POOL_GROUP_DIM),
        "pool_gb": jnp.transpose(gb_all, (1, 0, 2)).reshape(1, POOL_WIDTH),
        "pool_sc": pool_scale,
        "pool_out": wg["pool_out_w"][:, 0].reshape(POOL_WIDTH, D_MODEL),
        "gla_in": jnp.pad(gla_in_full, ((0, 0), (0, GLA_IN_PAD - GLA_IN_WIDTH))),
        "gla_gkw": jnp.pad(gkw_full, ((0, GLA_LOW_PAD - GLA_GATE_RANK), (0, 0))),
        "gla_gkb": gkb_all.reshape(1, GLA_KEY_WIDTH),
        "gla_hw": jnp.tile(hw_full, (1, GLA_HEADS)),
        "gla_out": wg["gla_out_w"][:, 0].reshape(GLA_VALUE_WIDTH, D_MODEL),
    }

    loss_part, grad_x, g = _local_step(x[0], loss_target[0], full)

    per_dev = {
        "pool_in_w": jnp.transpose(g["pool_in_w"].reshape(D_MODEL, N_DEV, 256), (1, 0, 2)),
        "pool_group_w": jnp.transpose(g["pool_group_w"].reshape(POOL_GROUPS, N_DEV, 32, POOL_GROUP_DIM), (1, 0, 2, 3)),
        "pool_group_b": jnp.transpose(g["pool_group_b"].reshape(POOL_GROUPS, N_DEV, 32), (1, 0, 2)),
        "pool_out_w": g["pool_out_w"].reshape(N_DEV, 128, D_MODEL),
        "gla_in_w": jnp.transpose(g["gla_in_w"].reshape(D_MODEL, N_DEV, 386), (1, 0, 2)),
        "gla_gk_w": jnp.transpose(g["gla_gk_w"].reshape(GLA_GATE_RANK, N_DEV, 64), (1, 0, 2)),
        "gla_gk_b": g["gla_gk_b"].reshape(N_DEV, 64),
        "gla_head_norm_w": g["gla_head_norm_w"].reshape(N_DEV, 32),
        "gla_out_w": g["gla_out_w"].reshape(N_DEV, 128, D_MODEL),
    }
    rep = jnp.concatenate([
        g["norm_w"], g["pool_scale"], g["final_norm_w"].reshape(1, D_MODEL),
        jnp.pad(loss_part.reshape(1, 1), ((0, 0), (0, D_MODEL - 1))),
        jnp.zeros((REP_ROWS - 5, D_MODEL), F32)], axis=0)
    rep_bits = _f32_bits_as_bf16(rep.reshape(-1)).reshape(REP_AS_BF16_ROWS, LANES)
    send = jnp.concatenate([
        _pack({n: a.astype(BF16) for n, a in per_dev.items()}, SHARDED, ADAM_ROWS, (N_DEV,)),
        jnp.broadcast_to(rep_bits, (N_DEV, REP_AS_BF16_ROWS, LANES))], axis=1)
    landed = _all_to_all_call(send)

    g_s, delta_s, m_s, v_s = _adamw_call(
        "adamw_sharded", landed, _pack(w_shard, SHARDED, ADAM_ROWS), _pack(m_shard, SHARDED, ADAM_ROWS),
        _pack(v_shard, SHARDED, ADAM_ROWS), ADAM_BLOCK)
    rep_parts = _bf16_bits_as_f32(landed[:, ADAM_ROWS:], (N_DEV,)).reshape(N_DEV, REP_ROWS, D_MODEL)

    def rep_slab(nw, sc, fw):
        return jnp.concatenate([nw, sc, fw.reshape(1, D_MODEL), jnp.zeros((REP_ROWS - 4, D_MODEL), F32)], axis=0)

    g_r, delta_r, m_r, v_r = _adamw_call(
        "adamw_replicated", rep_parts, rep_slab(norm_w, pool_scale, final_norm_w),
        rep_slab(m_norm_w, m_pool_scale, m_final_norm_w), rep_slab(v_norm_w, v_pool_scale, v_final_norm_w), REP_ROWS)

    def outputs(sharded_slab, rep_slab_):
        s = _unpack(sharded_slab, SHARDED)
        return [rep_slab_[0:2], s["pool_in_w"], s["pool_group_w"], s["pool_group_b"], rep_slab_[2:3], s["pool_out_w"],
                s["gla_in_w"], s["gla_gk_w"], s["gla_gk_b"], s["gla_head_norm_w"], s["gla_out_w"], rep_slab_[3]]

    loss = g_r[4, 0]
    return (loss, grad_x[None], *outputs(g_s, g_r), *outputs(delta_s, delta_r), *outputs(m_s, m_r),
            *outputs(v_s, v_r))
```

```python
import functools

import jax
import jax.numpy as jnp
from jax import lax
from jax.experimental import pallas as pl
from jax.experimental.pallas import tpu as pltpu

F32 = jnp.float32
BF16 = jnp.bfloat16
MESH = pl.DeviceIdType.MESH

N_DEV = 8
D_MODEL = 1024
POOL_WIDTH = 1024
POOL_GROUPS = 4
POOL_GROUP_DIM = 256
POOL_HALO = 16
GLA_HEADS = 4
GLA_HEAD_K = 128
GLA_HEAD_V = 256
GLA_KEY_WIDTH = 512
GLA_VALUE_WIDTH = 1024
GLA_GATE_RANK = 16
GLA_IN_WIDTH = 3088
GLA_IN_PAD = 3200
GLA_LOW_PAD = 128
CHUNK = 64
GATE_NORMALIZER = 16.0
RMS_EPS = 1e-6
Q_SCALE = GLA_HEAD_K ** -0.5

ADAM_LR = 0.001
ADAM_B1 = 0.9
ADAM_B2 = 0.999
ADAM_EPS = 1e-08
ADAM_WD = 0.01
ADAM_STEP = 10

LANES = 128
BF16_ROWS = 16
VMEM_LIMIT = 56 * 1024 * 1024
ROW_TILE = 256


def _dot_nn(a, b):
    return lax.dot_general(a, b, (((1,), (0,)), ((), ())), preferred_element_type=F32)


def _dot_nt(a, b):
    return lax.dot_general(a, b, (((1,), (1,)), ((), ())), preferred_element_type=F32)


def _dot_tn(a, b):
    return lax.dot_general(a, b, (((0,), (0,)), ((), ())), preferred_element_type=F32)


def _rms(x):
    rstd = lax.rsqrt(jnp.mean(x * x, axis=-1, keepdims=True) + RMS_EPS)
    return x * rstd, rstd


def _rms_bwd(dxhat, xhat, rstd):
    return rstd * (dxhat - xhat * jnp.mean(dxhat * xhat, axis=-1, keepdims=True))


def _sigmoid(x):
    return 1.0 / (1.0 + jnp.exp(-x))


def _params(sem=("arbitrary",)):
    return pltpu.CompilerParams(dimension_semantics=sem, vmem_limit_bytes=VMEM_LIMIT)


def _full(shape):
    return pl.BlockSpec(shape, lambda i: (0,) * len(shape))


def _window_sums(ext, forward):
    n = ext.shape[0]
    outs = []
    for g in range(POOL_GROUPS):
        s = ext[:, g * POOL_GROUP_DIM:(g + 1) * POOL_GROUP_DIM]
        for k in range(g + 1):
            shift = (1 << k) if forward else n - (1 << k)
            s = s + pltpu.roll(s, shift, axis=0)
        outs.append(s[:n - POOL_HALO])
    return outs


def _inv_count(row0, tm):
    row = row0 + lax.broadcasted_iota(jnp.int32, (tm, 1), 0)
    return [1.0 / jnp.minimum(row + 1, 2 << g).astype(F32) for g in range(POOL_GROUPS)]


def _pool_mix(u, u_prev, row0, gw_ref, gb):
    tm = u.shape[0]
    sums = _window_sums(jnp.concatenate([u, u_prev], axis=0), True)
    inv = _inv_count(row0, tm)
    pooled, mixed = [], []
    for g in range(POOL_GROUPS):
        ug = u[:, g * POOL_GROUP_DIM:(g + 1) * POOL_GROUP_DIM]
        pg = (sums[g] * inv[g] - ug).astype(BF16)
        pooled.append(pg)
        mixed.append(_dot_nn(pg, gw_ref[g]))
    return pooled, jnp.concatenate(mixed, axis=1) + gb


def _pool_fwd_call(x, nw, w_in, gw, gb, sc, w_out):
    seq = x.shape[0]
    tm = ROW_TILE
    nt = seq // tm

    def body(x_ref, nw_ref, win_ref, gw_ref, gb_ref, sc_ref, wout_ref, h_ref, p_ref, halo_ref):
        i = pl.program_id(0)

        @pl.when(i == 0)
        def _():
            halo_ref[...] = jnp.zeros_like(halo_ref)

        xt = x_ref[...]
        xhat, _ = _rms(xt)
        n = (xhat * nw_ref[...]).astype(BF16)
        p = _dot_nn(n, win_ref[...])
        p_ref[...] = p
        u = p[:, :POOL_WIDTH]
        gate = p[:, POOL_WIDTH:]
        _, mixed = _pool_mix(u, halo_ref[...], i * tm, gw_ref, gb_ref[...])
        halo_ref[...] = u[tm - POOL_HALO:, :]
        y = (mixed * sc_ref[...] * (gate * _sigmoid(gate))).astype(BF16)
        h_ref[...] = xt + _dot_nn(y, wout_ref[...])

    return pl.pallas_call(
        body, name="pool_fwd", grid=(nt,),
        in_specs=[pl.BlockSpec((tm, D_MODEL), lambda i: (i, 0)), _full((1, D_MODEL)),
                  _full((D_MODEL, 2 * POOL_WIDTH)), _full((POOL_GROUPS, POOL_GROUP_DIM, POOL_GROUP_DIM)),
                  _full((1, POOL_WIDTH)), _full((1, POOL_WIDTH)), _full((POOL_WIDTH, D_MODEL))],
        out_specs=[pl.BlockSpec((tm, D_MODEL), lambda i: (i, 0)),
                   pl.BlockSpec((tm, 2 * POOL_WIDTH), lambda i: (i, 0))],
        out_shape=[jax.ShapeDtypeStruct((seq, D_MODEL), F32),
                   jax.ShapeDtypeStruct((seq, 2 * POOL_WIDTH), F32)],
        scratch_shapes=[pltpu.VMEM((POOL_HALO, POOL_WIDTH), F32)],
        compiler_params=_params(),
    )(x, nw, w_in, gw, gb, sc, w_out)


def _pool_bwd_call(dh, p, gw, gb, sc, w_out):
    seq = dh.shape[0]
    tm = ROW_TILE
    nt = seq // tm
    halo_blocks = tm // POOL_HALO

    def body(dh_ref, p_ref, pprev_ref, gw_ref, gb_ref, sc_ref, wout_ref,
             dp_ref, dwout_hbm, dgw_hbm, dgb_ref, dsc_ref, carry_ref, dwout_acc, dgw_acc, dwout_stage, dgw_stage):
        i = pl.program_id(0)
        t = nt - 1 - i

        @pl.when(i == 0)
        def _():
            carry_ref[...] = jnp.zeros_like(carry_ref)
            dwout_acc[...] = jnp.zeros_like(dwout_acc)
            dgw_acc[...] = jnp.zeros_like(dgw_acc)
            dgb_ref[...] = jnp.zeros_like(dgb_ref)
            dsc_ref[...] = jnp.zeros_like(dsc_ref)

        p = p_ref[...]
        u = p[:, :POOL_WIDTH]
        gate = p[:, POOL_WIDTH:]
        u_prev = jnp.where(t > 0, pprev_ref[:, :POOL_WIDTH], 0.0)
        pooled, mixed = _pool_mix(u, u_prev, t * tm, gw_ref, gb_ref[...])
        sg = _sigmoid(gate)
        silu = gate * sg
        sc = sc_ref[...]
        dhb = dh_ref[...].astype(BF16)
        y = (mixed * sc * silu).astype(BF16)
        dwout_acc[...] += _dot_tn(y, dhb)
        dy = _dot_nt(dhb, wout_ref[...])
        dmixed = dy * sc * silu
        dsc_ref[...] += jnp.sum(dy * mixed * silu, axis=0, keepdims=True)
        dgate = dy * mixed * sc * (sg * (1.0 + gate * (1.0 - sg)))
        dgb_ref[...] += jnp.sum(dmixed, axis=0, keepdims=True)
        inv = _inv_count(t * tm, tm)
        dpooled, scaled = [], []
        for g in range(POOL_GROUPS):
            dmg = dmixed[:, g * POOL_GROUP_DIM:(g + 1) * POOL_GROUP_DIM].astype(BF16)
            dgw_acc[g] += _dot_tn(pooled[g], dmg)
            dpg = _dot_nt(dmg, gw_ref[g])
            dpooled.append(dpg)
            scaled.append(dpg * inv[g])
        r = jnp.concatenate(scaled, axis=1)
        sums = _window_sums(jnp.concatenate([r, carry_ref[...]], axis=0), False)
        carry_ref[...] = r[:POOL_HALO, :]
        du = jnp.concatenate([sums[g] - dpooled[g] for g in range(POOL_GROUPS)], axis=1)
        dp_ref[...] = jnp.concatenate([du, dgate], axis=1).astype(BF16)

        @pl.when(i == nt - 1)
        def _():
            dwout_stage[...] = dwout_acc[...].astype(BF16)
            dgw_stage[...] = dgw_acc[...].astype(BF16)
            pltpu.sync_copy(dwout_stage, dwout_hbm)
            pltpu.sync_copy(dgw_stage, dgw_hbm)

    rev = lambda i: (nt - 1 - i, 0)
    return pl.pallas_call(
        body, name="pool_bwd", grid=(nt,),
        in_specs=[pl.BlockSpec((tm, D_MODEL), rev), pl.BlockSpec((tm, 2 * POOL_WIDTH), rev),
                  pl.BlockSpec((POOL_HALO, 2 * POOL_WIDTH),
                               lambda i: (jnp.maximum((nt - 1 - i) * halo_blocks - 1, 0), 0)),
                  _full((POOL_GROUPS, POOL_GROUP_DIM, POOL_GROUP_DIM)), _full((1, POOL_WIDTH)),
                  _full((1, POOL_WIDTH)), _full((POOL_WIDTH, D_MODEL))],
        out_specs=[pl.BlockSpec((tm, 2 * POOL_WIDTH), rev), pl.BlockSpec(memory_space=pl.ANY),
                   pl.BlockSpec(memory_space=pl.ANY), _full((1, POOL_WIDTH)), _full((1, POOL_WIDTH))],
        out_shape=[jax.ShapeDtypeStruct((seq, 2 * POOL_WIDTH), BF16),
                   jax.ShapeDtypeStruct((POOL_WIDTH, D_MODEL), BF16),
                   jax.ShapeDtypeStruct((POOL_GROUPS, POOL_GROUP_DIM, POOL_GROUP_DIM), BF16),
                   jax.ShapeDtypeStruct((1, POOL_WIDTH), F32), jax.ShapeDtypeStruct((1, POOL_WIDTH), F32)],
        scratch_shapes=[pltpu.VMEM((POOL_HALO, POOL_WIDTH), F32), pltpu.VMEM((POOL_WIDTH, D_MODEL), F32),
                        pltpu.VMEM((POOL_GROUPS, POOL_GROUP_DIM, POOL_GROUP_DIM), F32),
                        pltpu.VMEM((POOL_WIDTH, D_MODEL), BF16),
                        pltpu.VMEM((POOL_GROUPS, POOL_GROUP_DIM, POOL_GROUP_DIM), BF16)],
        compiler_params=_params(),
    )(dh, p, p, gw, gb, sc, w_out)


def _inproj_bwd_call(name, dproj, h_in, nw, w_in, dres):
    seq = h_in.shape[0]
    width = w_in.shape[1]
    tm = ROW_TILE
    nt = seq // tm

    def body(dproj_ref, h_ref, nw_ref, win_ref, dres_ref, dh_ref, dw_hbm, dnw_ref, dw_acc, dw_stage):
        i = pl.program_id(0)

        @pl.when(i == 0)
        def _():
            dw_acc[...] = jnp.zeros_like(dw_acc)
            dnw_ref[...] = jnp.zeros_like(dnw_ref)

        xhat, rstd = _rms(h_ref[...])
        nw_row = nw_ref[...]
        n = (xhat * nw_row).astype(BF16)
        dpb = dproj_ref[...]
        dw_acc[...] += _dot_tn(n, dpb)
        dn = _dot_nt(dpb, win_ref[...])
        dnw_ref[...] += jnp.sum(dn * xhat, axis=0, keepdims=True)
        dh_ref[...] = _rms_bwd(dn * nw_row, xhat, rstd) + dres_ref[...]

        @pl.when(i == nt - 1)
        def _():
            dw_stage[...] = dw_acc[...].astype(BF16)
            pltpu.sync_copy(dw_stage, dw_hbm)

    row = lambda i: (i, 0)
    return pl.pallas_call(
        body, name=name, grid=(nt,),
        in_specs=[pl.BlockSpec((tm, width), row), pl.BlockSpec((tm, D_MODEL), row), _full((1, D_MODEL)),
                  _full((D_MODEL, width)), pl.BlockSpec((tm, D_MODEL), row)],
        out_specs=[pl.BlockSpec((tm, D_MODEL), row), pl.BlockSpec(memory_space=pl.ANY), _full((1, D_MODEL))],
        out_shape=[jax.ShapeDtypeStruct((seq, D_MODEL), F32), jax.ShapeDtypeStruct((D_MODEL, width), BF16),
                   jax.ShapeDtypeStruct((1, D_MODEL), F32)],
        scratch_shapes=[pltpu.VMEM((D_MODEL, width), F32), pltpu.VMEM((D_MODEL, width), BF16)],
        compiler_params=_params(),
    )(dproj, h_in, nw, w_in, dres)


def _chunk_scan(x, reverse):
    n = x.shape[0]
    pos = lax.broadcasted_iota(jnp.int32, (n, 1), 0) & (CHUNK - 1)
    k = 1
    while k < CHUNK:
        if reverse:
            x = x + jnp.where(pos < CHUNK - k, pltpu.roll(x, n - k, axis=0), 0.0)
        else:
            x = x + jnp.where(pos >= k, pltpu.roll(x, k, axis=0), 0.0)
        k *= 2
    return x


class _GlaTile:
    def __init__(self, proj, gkw_ref, gkb):
        tm = proj.shape[0]
        self.q = proj[:, :GLA_KEY_WIDTH] * Q_SCALE
        self.k = proj[:, GLA_KEY_WIDTH:2 * GLA_KEY_WIDTH]
        self.v = proj[:, 2 * GLA_KEY_WIDTH:2 * GLA_KEY_WIDTH + GLA_VALUE_WIDTH]
        self.gate = proj[:, 2 * GLA_KEY_WIDTH + GLA_VALUE_WIDTH:2 * GLA_KEY_WIDTH + 2 * GLA_VALUE_WIDTH]
        self.low_b = proj[:, 2 * GLA_KEY_WIDTH + 2 * GLA_VALUE_WIDTH:].astype(BF16)
        self.z = _dot_nn(self.low_b, gkw_ref[...]) + gkb
        log_g = (jnp.minimum(self.z, 0.0) - jnp.log(1.0 + jnp.exp(-jnp.abs(self.z)))) / GATE_NORMALIZER
        self.c = _chunk_scan(log_g, False)
        is_last = lax.broadcasted_iota(jnp.int32, (CHUNK, 1), 0) == CHUNK - 1
        last = [jnp.sum(jnp.where(is_last, self.c[j * CHUNK:(j + 1) * CHUNK, :], 0.0), axis=0, keepdims=True)
                for j in range(tm // CHUNK)]
        self.c_last = last
        c_last_rows = jnp.concatenate([jnp.broadcast_to(r, (CHUNK, GLA_KEY_WIDTH)) for r in last], axis=0)
        self.e_pos = jnp.exp(self.c)
        self.e_neg = jnp.exp(-self.c)
        self.e_rest = jnp.exp(c_last_rows - self.c)
        self.a = self.q * self.e_pos
        self.b = self.k * self.e_neg
        self.cn = self.q * self.e_neg
        self.dp = self.k * self.e_pos
        self.kd = self.k * self.e_rest
        self.a_b, self.b_b, self.cn_b, self.dp_b, self.kd_b, self.v_b = (
            t.astype(BF16) for t in (self.a, self.b, self.cn, self.dp, self.kd, self.v))
        idx_t = lax.broadcasted_iota(jnp.int32, (CHUNK, CHUNK), 0)
        idx_s = lax.broadcasted_iota(jnp.int32, (CHUNK, CHUNK), 1)
        self.lower = idx_t >= idx_s

    @staticmethod
    def rows(j):
        return slice(j * CHUNK, (j + 1) * CHUNK)

    @staticmethod
    def kcols(h):
        return slice(h * GLA_HEAD_K, (h + 1) * GLA_HEAD_K)

    @staticmethod
    def vcols(h):
        return slice(h * GLA_HEAD_V, (h + 1) * GLA_HEAD_V)

    def scores(self, j, h):
        r, kc = self.rows(j), self.kcols(h)
        fwd = _dot_nt(self.a_b[r, kc], self.b_b[r, kc])
        bwd = _dot_nt(self.cn_b[r, kc], self.dp_b[r, kc])
        return jnp.where(self.lower, fwd, bwd).astype(BF16)


def _gla_fwd_call(h1, nw, w_in, gkw, gkb, hw, w_out, wf, target):
    seq = h1.shape[0]
    tm = ROW_TILE
    nt = seq // tm
    cpt = tm // CHUNK
    n_chunks = seq // CHUNK

    def body(h_ref, nw_ref, win_ref, gkw_ref, gkb_ref, hw_ref, wout_ref, wf_ref, tgt_ref,
             dh2_ref, proj_ref, o_ref, st_ref, loss_ref, dwf_ref, state_ref):
        i = pl.program_id(0)

        @pl.when(i == 0)
        def _():
            state_ref[...] = jnp.zeros_like(state_ref)
            loss_ref[...] = jnp.zeros_like(loss_ref)
            dwf_ref[...] = jnp.zeros_like(dwf_ref)

        ht = h_ref[...]
        xhat, _ = _rms(ht)
        n = (xhat * nw_ref[...]).astype(BF16)
        proj = _dot_nn(n, win_ref[...])
        proj_ref[...] = proj
        g = _GlaTile(proj, gkw_ref, gkb_ref[...])
        o_rows = []
        for j in range(cpt):
            r = g.rows(j)
            o_heads = []
            for h in range(GLA_HEADS):
                kc, vc = g.kcols(h), g.vcols(h)
                srows = slice(h * GLA_HEAD_V, (h + 1) * GLA_HEAD_V)
                state = state_ref[srows, :]
                st_ref[j, srows, :] = state
                o_heads.append(_dot_nn(g.scores(j, h), g.v_b[r, vc]) + _dot_nt(g.a_b[r, kc], state.astype(BF16)))
                decay = jnp.exp(g.c_last[j][:, kc])
                state_ref[srows, :] = state * decay + _dot_tn(g.v_b[r, vc], g.kd_b[r, kc])
            o_rows.append(jnp.concatenate(o_heads, axis=1))
        o = jnp.concatenate(o_rows, axis=0)
        o_ref[...] = o
        hw_row = hw_ref[...]
        on = jnp.concatenate([_rms(o[:, g.vcols(h)])[0] for h in range(GLA_HEADS)], axis=1) * hw_row
        y = (on * (g.gate * _sigmoid(g.gate))).astype(BF16)
        h2 = ht + _dot_nn(y, wout_ref[...])
        xhat2, rstd2 = _rms(h2)
        wf_row = wf_ref[...]
        err = xhat2 * wf_row - tgt_ref[...]
        loss_ref[...] += 0.5 * jnp.sum(err * err) / D_MODEL
        dout = err * (1.0 / D_MODEL)
        dwf_ref[...] += jnp.sum(dout * xhat2, axis=0, keepdims=True)
        dh2_ref[...] = _rms_bwd(dout * wf_row, xhat2, rstd2)

    row = lambda i: (i, 0)
    return pl.pallas_call(
        body, name="gla_fwd", grid=(nt,),
        in_specs=[pl.BlockSpec((tm, D_MODEL), row), _full((1, D_MODEL)), _full((D_MODEL, GLA_IN_PAD)),
                  _full((GLA_LOW_PAD, GLA_KEY_WIDTH)), _full((1, GLA_KEY_WIDTH)), _full((1, GLA_VALUE_WIDTH)),
                  _full((GLA_VALUE_WIDTH, D_MODEL)), _full((1, D_MODEL)), pl.BlockSpec((tm, D_MODEL), row)],
        out_specs=[pl.BlockSpec((tm, D_MODEL), row), pl.BlockSpec((tm, GLA_IN_PAD), row),
                   pl.BlockSpec((tm, GLA_VALUE_WIDTH), row),
                   pl.BlockSpec((cpt, GLA_VALUE_WIDTH, GLA_HEAD_K), lambda i: (i, 0, 0)),
                   _full((8, LANES)), _full((1, D_MODEL))],
        out_shape=[jax.ShapeDtypeStruct((seq, D_MODEL), F32), jax.ShapeDtypeStruct((seq, GLA_IN_PAD), F32),
                   jax.ShapeDtypeStruct((seq, GLA_VALUE_WIDTH), F32),
                   jax.ShapeDtypeStruct((n_chunks, GLA_VALUE_WIDTH, GLA_HEAD_K), F32),
                   jax.ShapeDtypeStruct((8, LANES), F32), jax.ShapeDtypeStruct((1, D_MODEL), F32)],
        scratch_shapes=[pltpu.VMEM((GLA_VALUE_WIDTH, GLA_HEAD_K), F32)],
        compiler_params=_params(),
    )(h1, nw, w_in, gkw, gkb, hw, w_out, wf, target)


def _gla_bwd_call(dh2, proj, o, states, gkw, gkb, hw, w_out):
    seq = dh2.shape[0]
    tm = ROW_TILE
    nt = seq // tm
    cpt = tm // CHUNK

    def body(dh_ref, proj_ref, o_ref, st_ref, gkw_ref, gkb_ref, hw_ref, wout_ref,
             dproj_ref, dwout_hbm, dhw_ref, dgkw_ref, dgkb_ref, dstate_ref, dwout_acc, dwout_stage):
        i = pl.program_id(0)

        @pl.when(i == 0)
        def _():
            dstate_ref[...] = jnp.zeros_like(dstate_ref)
            dwout_acc[...] = jnp.zeros_like(dwout_acc)
            dhw_ref[...] = jnp.zeros_like(dhw_ref)
            dgkw_ref[...] = jnp.zeros_like(dgkw_ref)
            dgkb_ref[...] = jnp.zeros_like(dgkb_ref)

        g = _GlaTile(proj_ref[...], gkw_ref, gkb_ref[...])
        dhb = dh_ref[...].astype(BF16)
        o = o_ref[...]
        hw_row = hw_ref[...]
        dy = _dot_nt(dhb, wout_ref[...])
        sg = _sigmoid(g.gate)
        silu = g.gate * sg
        don = dy * silu
        on_parts, do_parts, dhw_parts = [], [], []
        for h in range(GLA_HEADS):
            vc = g.vcols(h)
            xh, rs = _rms(o[:, vc])
            on_parts.append(xh * hw_row[:, vc])
            dhw_parts.append(jnp.sum(don[:, vc] * xh, axis=0, keepdims=True))
            do_parts.append(_rms_bwd(don[:, vc] * hw_row[:, vc], xh, rs))
        on = jnp.concatenate(on_parts, axis=1)
        dwout_acc[...] += _dot_tn((on * silu).astype(BF16), dhb)
        dhw_ref[...] += jnp.concatenate(dhw_parts, axis=1)
        dgate = dy * on * (sg * (1.0 + g.gate * (1.0 - sg)))
        do_b = jnp.concatenate(do_parts, axis=1).astype(BF16)

        last_row = lax.broadcasted_iota(jnp.int32, (CHUNK, 1), 0) == CHUNK - 1
        dq_rows, dk_rows, dv_rows, dc_rows = [None] * cpt, [None] * cpt, [None] * cpt, [None] * cpt
        for j in reversed(range(cpt)):
            r = g.rows(j)
            dq_h, dk_h, dv_h, dc_h = [], [], [], []
            for h in range(GLA_HEADS):
                kc, vc = g.kcols(h), g.vcols(h)
                srows = slice(h * GLA_HEAD_V, (h + 1) * GLA_HEAD_V)
                state = st_ref[j, srows, :]
                dstate = dstate_ref[srows, :]
                dstate_b = dstate.astype(BF16)
                do_c = do_b[r, vc]
                scores = g.scores(j, h)
                dscores = _dot_nt(do_c, g.v_b[r, vc])
                dfwd = jnp.where(g.lower, dscores, 0.0).astype(BF16)
                dbwd = jnp.where(g.lower, 0.0, dscores).astype(BF16)
                dv_h.append(_dot_tn(scores, do_c) + _dot_nt(g.kd_b[r, kc], dstate_b))
                da = _dot_nn(dfwd, g.b_b[r, kc]) + _dot_nn(do_c, state.astype(BF16))
                db = _dot_tn(dfwd, g.a_b[r, kc])
                dcn = _dot_nn(dbwd, g.dp_b[r, kc])
                ddp = _dot_tn(dbwd, g.cn_b[r, kc])
                dkd = _dot_nn(g.v_b[r, vc], dstate_b)
                decay = jnp.exp(g.c_last[j][:, kc])
                dstate_ref[srows, :] = _dot_tn(do_c, g.a_b[r, kc]) + dstate * decay
                kd_c = g.kd[r, kc]
                dkd_kd = dkd * kd_c
                dc_last = (jnp.sum(dkd_kd, axis=0, keepdims=True)
                           + decay * jnp.sum(state * dstate, axis=0, keepdims=True))
                dq_h.append(Q_SCALE * (da * g.e_pos[r, kc] + dcn * g.e_neg[r, kc]))
                dk_h.append(db * g.e_neg[r, kc] + ddp * g.e_pos[r, kc] + dkd * g.e_rest[r, kc])
                dc = (da * g.a[r, kc] - db * g.b[r, kc] - dcn * g.cn[r, kc] + ddp * g.dp[r, kc] - dkd_kd)
                dc_h.append(dc + jnp.where(last_row, dc_last, 0.0))
            dq_rows[j] = jnp.concatenate(dq_h, axis=1)
            dk_rows[j] = jnp.concatenate(dk_h, axis=1)
            dv_rows[j] = jnp.concatenate(dv_h, axis=1)
            dc_rows[j] = jnp.concatenate(dc_h, axis=1)
        dq = jnp.concatenate(dq_rows, axis=0)
        dk = jnp.concatenate(dk_rows, axis=0)
        dv = jnp.concatenate(dv_rows, axis=0)
        dlog_g = _chunk_scan(jnp.concatenate(dc_rows, axis=0), True)
        dz = dlog_g * (1.0 / GATE_NORMALIZER) * (1.0 - _sigmoid(g.z))
        dzb = dz.astype(BF16)
        dgkb_ref[...] += jnp.sum(dz, axis=0, keepdims=True)
        dgkw_ref[...] += _dot_tn(g.low_b, dzb)
        dlow = _dot_nt(dzb, gkw_ref[...])
        dproj_ref[...] = jnp.concatenate([dq, dk, dv, dgate, dlow], axis=1).astype(BF16)

        @pl.when(i == nt - 1)
        def _():
            dwout_stage[...] = dwout_acc[...].astype(BF16)
            pltpu.sync_copy(dwout_stage, dwout_hbm)

    rev = lambda i: (nt - 1 - i, 0)
    return pl.pallas_call(
        body, name="gla_bwd", grid=(nt,),
        in_specs=[pl.BlockSpec((tm, D_MODEL), rev), pl.BlockSpec((tm, GLA_IN_PAD), rev),
                  pl.BlockSpec((tm, GLA_VALUE_WIDTH), rev),
                  pl.BlockSpec((cpt, GLA_VALUE_WIDTH, GLA_HEAD_K), lambda i: (nt - 1 - i, 0, 0)),
                  _full((GLA_LOW_PAD, GLA_KEY_WIDTH)), _full((1, GLA_KEY_WIDTH)), _full((1, GLA_VALUE_WIDTH)),
                  _full((GLA_VALUE_WIDTH, D_MODEL))],
        out_specs=[pl.BlockSpec((tm, GLA_IN_PAD), rev), pl.BlockSpec(memory_space=pl.ANY),
                   _full((1, GLA_VALUE_WIDTH)), _full((GLA_LOW_PAD, GLA_KEY_WIDTH)), _full((1, GLA_KEY_WIDTH))],
        out_shape=[jax.ShapeDtypeStruct((seq, GLA_IN_PAD), BF16), jax.ShapeDtypeStruct((GLA_VALUE_WIDTH, D_MODEL), BF16),
                   jax.ShapeDtypeStruct((1, GLA_VALUE_WIDTH), F32), jax.ShapeDtypeStruct((GLA_LOW_PAD, GLA_KEY_WIDTH), F32),
                   jax.ShapeDtypeStruct((1, GLA_KEY_WIDTH), F32)],
        scratch_shapes=[pltpu.VMEM((GLA_VALUE_WIDTH, GLA_HEAD_K), F32), pltpu.VMEM((GLA_VALUE_WIDTH, D_MODEL), F32),
                        pltpu.VMEM((GLA_VALUE_WIDTH, D_MODEL), BF16)],
        compiler_params=_params(),
    )(dh2, proj, o, states, gkw, gkb, hw, w_out)


def _local_step(x, target, w):
    h1, p = _pool_fwd_call(x, w["nw0"], w["pool_in"], w["pool_gw"], w["pool_gb"], w["pool_sc"], w["pool_out"])
    dh2, proj, o, states, loss, dwf = _gla_fwd_call(
        h1, w["nw1"], w["gla_in"], w["gla_gkw"], w["gla_gkb"], w["gla_hw"], w["gla_out"], w["wf"], target)
    dproj, d_gla_out, dhw, dgkw, dgkb = _gla_bwd_call(
        dh2, proj, o, states, w["gla_gkw"], w["gla_gkb"], w["gla_hw"], w["gla_out"])
    dh1, d_gla_in, dnw1 = _inproj_bwd_call("gla_in_bwd", dproj, h1, w["nw1"], w["gla_in"], dh2)
    dp, d_pool_out, dgw, dgb, dsc = _pool_bwd_call(dh1, p, w["pool_gw"], w["pool_gb"], w["pool_sc"], w["pool_out"])
    dx, d_pool_in, dnw0 = _inproj_bwd_call("pool_in_bwd", dp, x, w["nw0"], w["pool_in"], dh1)
    big = {"pool_in_w": d_pool_in, "pool_group_w": dgw, "pool_out_w": d_pool_out, "gla_in_w": d_gla_in,
           "gla_out_w": d_gla_out}
    small = {"norm_w0": dnw0, "norm_w1": dnw1, "pool_scale": dsc, "final_norm_w": dwf, "loss": loss,
             "pool_group_b": dgb, "gla_gk_w": dgkw, "gla_gk_b": dgkb, "gla_head_norm_w": dhw}
    return dx, big, small


def _position():
    return lax.axis_index("x"), lax.axis_index("y"), lax.axis_index("c")


def _lead_slot(ref, d):
    return ref.at[d]


def _row_slot(rows):
    return lambda ref, d: ref.at[pl.ds(pl.multiple_of(d * rows, rows), rows)]


def _dim1_slot(size):
    return lambda ref, d: ref.at[:, pl.ds(pl.multiple_of(d * size, size), size)]


def _all_gather_call(shards, full_shapes, slots):
    n = len(shards)

    def body(*refs):
        x_refs, out_refs = refs[:n], refs[n:2 * n]
        send_sems, recv_sems, local_sems = refs[2 * n:]
        x, y, c = _position()
        me, sibling = (x, y, c), (x, y, 1 - c)
        chips = [(1 - x, y), (x, 1 - y), (1 - x, 1 - y)]

        def copy(a, k, block, to, from_input=False):
            part = slots[a](out_refs[a], 4 * block[0] + 2 * block[1] + block[2])
            return pltpu.make_async_remote_copy(
                src_ref=x_refs[a] if from_input else part, dst_ref=part,
                send_sem=send_sems.at[a, k], recv_sem=recv_sems.at[a, k], device_id=to, device_id_type=MESH)

        mine = [pltpu.make_async_copy(x_refs[a], slots[a](out_refs[a], 4 * x + 2 * y + c), local_sems.at[a])
                for a in range(n)]
        for cp in mine:
            cp.start()
        first = [copy(a, 0, me, sibling, True) for a in range(n)]
        first += [copy(a, 1 + j, me, (*chip, c), True) for j, chip in enumerate(chips) for a in range(n)]
        for cp in first:
            cp.start()
        passed = []
        for j, chip in enumerate(chips):
            for a in range(n):
                copy(a, 1 + j, (*chip, c), me).wait_recv()
                passed.append(copy(a, 4 + j, (*chip, c), sibling))
                passed[-1].start()
        for a in range(n):
            copy(a, 0, sibling, me).wait_recv()
        for j, chip in enumerate(chips):
            for a in range(n):
                copy(a, 4 + j, (*chip, 1 - c), me).wait_recv()
        for cp in first + passed:
            cp.wait_send()
        for cp in mine:
            cp.wait()

    hbm = pl.BlockSpec(memory_space=pl.ANY)
    return pl.pallas_call(
        body, name="weights_all_gather", in_specs=[hbm] * n, out_specs=[hbm] * n,
        out_shape=[jax.ShapeDtypeStruct(s, x.dtype) for s, x in zip(full_shapes, shards)],
        scratch_shapes=[pltpu.SemaphoreType.DMA((n, 7)), pltpu.SemaphoreType.DMA((n, 7)),
                        pltpu.SemaphoreType.DMA((n,))],
    )(*shards)


def _all_to_all_call(sends, slots, part_shapes):
    n = len(sends)

    def body(*refs):
        send_refs, land_refs = refs[:n], refs[n:2 * n]
        send_sems, recv_sems, local_sems = refs[2 * n:]
        x, y, c = _position()
        me = 4 * x + 2 * y + c
        mine = [pltpu.make_async_copy(slots[a](send_refs[a], me), land_refs[a].at[me], local_sems.at[a])
                for a in range(n)]
        for cp in mine:
            cp.start()
        copies = []
        for k in range(1, N_DEV):
            px, py, pc = x ^ (k >> 2), y ^ ((k >> 1) & 1), c ^ (k & 1)
            for a in range(n):
                copies.append(pltpu.make_async_remote_copy(
                    src_ref=slots[a](send_refs[a], 4 * px + 2 * py + pc), dst_ref=land_refs[a].at[me],
                    send_sem=send_sems.at[a, k - 1], recv_sem=recv_sems.at[a, k - 1],
                    device_id=(px, py, pc), device_id_type=MESH))
        for cp in copies:
            cp.start()
        for cp in copies:
            cp.wait_recv()
        for cp in copies:
            cp.wait_send()
        for cp in mine:
            cp.wait()

    hbm = pl.BlockSpec(memory_space=pl.ANY)
    return pl.pallas_call(
        body, name="grads_all_to_all", in_specs=[hbm] * n, out_specs=[hbm] * n,
        out_shape=[jax.ShapeDtypeStruct((N_DEV,) + tuple(s), x.dtype) for s, x in zip(part_shapes, sends)],
        scratch_shapes=[pltpu.SemaphoreType.DMA((n, 7)), pltpu.SemaphoreType.DMA((n, 7)),
                        pltpu.SemaphoreType.DMA((n,))],
    )(*sends)


def _adamw(w, g, m, v):
    m = ADAM_B1 * m + (1.0 - ADAM_B1) * g
    v = ADAM_B2 * v + (1.0 - ADAM_B2) * (g * g)
    m_hat = m / (1.0 - ADAM_B1 ** ADAM_STEP)
    v_hat = v / (1.0 - ADAM_B2 ** ADAM_STEP)
    delta = -ADAM_LR * (m_hat / (jnp.sqrt(v_hat) + ADAM_EPS) + ADAM_WD * w)
    return delta, m, v


def _sum_parts(parts_ref, index=()):
    g = parts_ref[(0,) + index].astype(F32)
    for s in range(1, N_DEV):
        g = g + parts_ref[(s,) + index].astype(F32)
    return g


def _adamw_call(name, parts, w, m, v, block_rows):
    rows, cols = w.shape
    nb = rows // block_rows

    def body(parts_ref, w_ref, m_ref, v_ref, g_ref, delta_ref, m_out, v_out):
        g = _sum_parts(parts_ref)
        delta, m_new, v_new = _adamw(w_ref[...], g, m_ref[...], v_ref[...])
        g_ref[...] = g
        delta_ref[...] = delta
        m_out[...] = m_new
        v_out[...] = v_new

    blk = pl.BlockSpec((block_rows, cols), lambda i: (i, 0))
    return pl.pallas_call(
        body, name=name, grid=(nb,),
        in_specs=[pl.BlockSpec((N_DEV, block_rows, cols), lambda i: (0, i, 0)), blk, blk, blk],
        out_specs=[blk, blk, blk, blk],
        out_shape=[jax.ShapeDtypeStruct((rows, cols), F32)] * 4,
        compiler_params=_params(("parallel",)),
    )(parts, w, m, v)


WIDE_ROWS = 8
NARROW_ROWS = 40
NARROW_GKW_ROW = 8
NARROW_GKB_ROW = 24
NARROW_HW_ROW = 32
GROUP_SHARD = POOL_GROUP_DIM // N_DEV
KEY_SHARD = GLA_KEY_WIDTH // N_DEV
HEAD_V_SHARD = GLA_HEAD_V // N_DEV


def _small_adamw_call(wide, narrow, w, m, v):
    names = ("norm_w", "pool_scale", "final_norm_w", "pool_group_b", "gla_gk_w", "gla_gk_b", "gla_head_norm_w")
    where = {
        "norm_w": (0, slice(0, 2), slice(None)),
        "pool_scale": (0, slice(2, 3), slice(None)),
        "final_norm_w": (0, slice(3, 4), slice(None)),
        "pool_group_b": (1, slice(0, POOL_GROUPS), slice(0, GROUP_SHARD)),
        "gla_gk_w": (1, slice(NARROW_GKW_ROW, NARROW_GKW_ROW + GLA_GATE_RANK), slice(0, KEY_SHARD)),
        "gla_gk_b": (1, slice(NARROW_GKB_ROW, NARROW_GKB_ROW + 1), slice(0, KEY_SHARD)),
        "gla_head_norm_w": (1, slice(NARROW_HW_ROW, NARROW_HW_ROW + 1), slice(0, HEAD_V_SHARD)),
    }
    k = len(names)

    def body(*refs):
        parts = refs[0:2]
        w_refs, m_refs, v_refs = refs[2:2 + k], refs[2 + k:2 + 2 * k], refs[2 + 2 * k:2 + 3 * k]
        outs = refs[2 + 3 * k:]
        loss_ref = outs[0]
        loss_ref[...] = _sum_parts(parts[0], (slice(4, 5), slice(0, 1)))
        for i, name in enumerate(names):
            buf, rows, cols = where[name]
            g = _sum_parts(parts[buf], (rows, cols))
            delta, m_new, v_new = _adamw(w_refs[i][...], g, m_refs[i][...], v_refs[i][...])
            outs[1 + i][...] = g
            outs[1 + k + i][...] = delta
            outs[1 + 2 * k + i][...] = m_new
            outs[1 + 3 * k + i][...] = v_new

    vmem = pl.BlockSpec(memory_space=pltpu.VMEM)
    shapes = [jax.ShapeDtypeStruct(w[n].shape, F32) for n in names]
    res = pl.pallas_call(
        body, name="adamw_small", in_specs=[vmem] * (2 + 3 * k), out_specs=[vmem] * (1 + 4 * k),
        out_shape=[jax.ShapeDtypeStruct((1, 1), F32)] + shapes * 4,
    )(wide, narrow, *[w[n] for n in names], *[m[n] for n in names], *[v[n] for n in names])
    unzip = lambda j: dict(zip(names, res[1 + j * k:1 + (j + 1) * k]))
    return res[0], unzip(0), unzip(1), unzip(2), unzip(3)


def kernel(x, norm_w, pool_in_w, pool_group_w, pool_group_b, pool_scale, pool_out_w, gla_in_w, gla_gk_w, gla_gk_b, gla_head_norm_w, gla_out_w, final_norm_w, loss_target, m_norm_w, m_pool_in_w, m_pool_group_w, m_pool_group_b, m_pool_scale, m_pool_out_w, m_gla_in_w, m_gla_gk_w, m_gla_gk_b, m_gla_head_norm_w, m_gla_out_w, m_final_norm_w, v_norm_w, v_pool_in_w, v_pool_group_w, v_pool_group_b, v_pool_scale, v_pool_out_w, v_gla_in_w, v_gla_gk_w, v_gla_gk_b, v_gla_head_norm_w, v_gla_out_w, v_final_norm_w):
    w = dict(norm_w=norm_w, pool_in_w=pool_in_w, pool_group_w=pool_group_w, pool_group_b=pool_group_b,
             pool_scale=pool_scale, pool_out_w=pool_out_w, gla_in_w=gla_in_w, gla_gk_w=gla_gk_w, gla_gk_b=gla_gk_b,
             gla_head_norm_w=gla_head_norm_w, gla_out_w=gla_out_w, final_norm_w=final_norm_w)
    m = dict(norm_w=m_norm_w, pool_in_w=m_pool_in_w, pool_group_w=m_pool_group_w, pool_group_b=m_pool_group_b,
             pool_scale=m_pool_scale, pool_out_w=m_pool_out_w, gla_in_w=m_gla_in_w, gla_gk_w=m_gla_gk_w,
             gla_gk_b=m_gla_gk_b, gla_head_norm_w=m_gla_head_norm_w, gla_out_w=m_gla_out_w,
             final_norm_w=m_final_norm_w)
    v = dict(norm_w=v_norm_w, pool_in_w=v_pool_in_w, pool_group_w=v_pool_group_w, pool_group_b=v_pool_group_b,
             pool_scale=v_pool_scale, pool_out_w=v_pool_out_w, gla_in_w=v_gla_in_w, gla_gk_w=v_gla_gk_w,
             gla_gk_b=v_gla_gk_b, gla_head_norm_w=v_gla_head_norm_w, gla_out_w=v_gla_out_w,
             final_norm_w=v_final_norm_w)
    col_shard = GLA_IN_WIDTH // N_DEV
    row_shard = D_MODEL // N_DEV

    def lanes(a):
        return jnp.pad(a, [(0, 0)] * (a.ndim - 1) + [(0, LANES - a.shape[-1])])

    small_in = jnp.concatenate([lanes(pool_group_b[0]), lanes(gla_gk_b), lanes(gla_head_norm_w),
                                jnp.zeros((2, LANES), F32)], axis=0)
    shards = [pool_in_w[0].astype(BF16), pool_group_w[0].astype(BF16), pool_out_w[0].astype(BF16),
              gla_in_w[0].astype(BF16), gla_gk_w[0].astype(BF16), gla_out_w[0].astype(BF16), small_in]
    pool_in, pool_gw, pool_out, gla_in_parts, gkw_parts, gla_out, small_all = _all_gather_call(
        shards,
        [(D_MODEL, 2 * POOL_WIDTH), (POOL_GROUPS, POOL_GROUP_DIM, POOL_GROUP_DIM), (POOL_WIDTH, D_MODEL),
         (N_DEV, D_MODEL, col_shard), (N_DEV, GLA_GATE_RANK, KEY_SHARD), (GLA_VALUE_WIDTH, D_MODEL),
         (N_DEV, 8, LANES)],
        [_dim1_slot(2 * POOL_WIDTH // N_DEV), _dim1_slot(GROUP_SHARD), _row_slot(row_shard), _lead_slot, _lead_slot,
         _row_slot(row_shard), _lead_slot])
    gla_in_full = jnp.transpose(gla_in_parts, (1, 0, 2)).reshape(D_MODEL, GLA_IN_WIDTH)
    gkw_full = jnp.transpose(gkw_parts, (1, 0, 2)).reshape(GLA_GATE_RANK, GLA_KEY_WIDTH)
    full = {
        "nw0": norm_w[0:1], "nw1": norm_w[1:2], "wf": final_norm_w.reshape(1, D_MODEL),
        "pool_in": pool_in, "pool_gw": pool_gw, "pool_sc": pool_scale, "pool_out": pool_out,
        "pool_gb": jnp.transpose(small_all[:, 0:POOL_GROUPS, :GROUP_SHARD], (1, 0, 2)).reshape(1, POOL_WIDTH),
        "gla_in": jnp.pad(gla_in_full, ((0, 0), (0, GLA_IN_PAD - GLA_IN_WIDTH))),
        "gla_gkw": jnp.pad(gkw_full, ((0, GLA_LOW_PAD - GLA_GATE_RANK), (0, 0))),
        "gla_gkb": small_all[:, POOL_GROUPS, :KEY_SHARD].reshape(1, GLA_KEY_WIDTH),
        "gla_hw": jnp.tile(small_all[:, POOL_GROUPS + 1, :HEAD_V_SHARD].reshape(1, GLA_HEAD_V), (1, GLA_HEADS)),
        "gla_out": gla_out,
    }

    grad_x, big, small = _local_step(x[0], loss_target[0], full)

    gla_in_send = jnp.transpose(big["gla_in_w"][:, :GLA_IN_WIDTH].reshape(D_MODEL, N_DEV, col_shard), (1, 0, 2))
    wide = jnp.concatenate([
        small["norm_w0"], small["norm_w1"], small["pool_scale"], small["final_norm_w"],
        jnp.pad(small["loss"][0:1, 0:1], ((0, 0), (0, D_MODEL - 1))),
        jnp.zeros((WIDE_ROWS - 5, D_MODEL), F32)], axis=0)

    def rows8(a):
        return jnp.pad(lanes(a), ((0, 0), (0, -a.shape[1] % 8), (0, 0)))

    narrow = jnp.concatenate([
        rows8(jnp.transpose(small["pool_group_b"].reshape(POOL_GROUPS, N_DEV, GROUP_SHARD), (1, 0, 2))),
        rows8(jnp.transpose(small["gla_gk_w"][:GLA_GATE_RANK].reshape(GLA_GATE_RANK, N_DEV, KEY_SHARD), (1, 0, 2))),
        rows8(small["gla_gk_b"].reshape(N_DEV, 1, KEY_SHARD)),
        rows8(small["gla_head_norm_w"].reshape(GLA_HEADS, GLA_HEAD_V).sum(axis=0).reshape(N_DEV, 1, HEAD_V_SHARD)),
    ], axis=1)
    landed = _all_to_all_call(
        [big["pool_in_w"], big["pool_group_w"], big["pool_out_w"], gla_in_send, big["gla_out_w"], wide, narrow],
        [_dim1_slot(2 * POOL_WIDTH // N_DEV), _dim1_slot(GROUP_SHARD), _row_slot(row_shard), _lead_slot,
         _row_slot(row_shard), lambda ref, d: ref, _lead_slot],
        [(D_MODEL, 2 * POOL_WIDTH // N_DEV), (POOL_GROUPS, GROUP_SHARD, POOL_GROUP_DIM), (row_shard, D_MODEL),
         (D_MODEL, col_shard), (row_shard, D_MODEL), (WIDE_ROWS, D_MODEL), (NARROW_ROWS, LANES)])

    res = {}
    for idx, (name, rows, cols, block) in enumerate([
            ("pool_in_w", D_MODEL, 2 * POOL_WIDTH // N_DEV, 256), ("pool_group_w", POOL_GROUPS * GROUP_SHARD, POOL_GROUP_DIM, 128),
            ("pool_out_w", row_shard, D_MODEL, 128), ("gla_in_w", D_MODEL, col_shard, 256),
            ("gla_out_w", row_shard, D_MODEL, 128)]):
        outs = _adamw_call("adamw_" + name, landed[idx].reshape(N_DEV, rows, cols), w[name].reshape(rows, cols),
                           m[name].reshape(rows, cols), v[name].reshape(rows, cols), block)
        res[name] = [t.reshape(w[name].shape) for t in outs]
    small_shapes = {"norm_w": (2, D_MODEL), "pool_scale": (1, D_MODEL), "final_norm_w": (1, D_MODEL),
                    "pool_group_b": (POOL_GROUPS, GROUP_SHARD), "gla_gk_w": (GLA_GATE_RANK, KEY_SHARD),
                    "gla_gk_b": (1, KEY_SHARD), "gla_head_norm_w": (1, HEAD_V_SHARD)}
    as_small = lambda t: {n: t[n].reshape(s) for n, s in small_shapes.items()}
    loss, *small_outs = _small_adamw_call(landed[5], landed[6], as_small(w), as_small(m), as_small(v))
    for name in small_shapes:
        res[name] = [t[name].reshape(w[name].shape) for t in small_outs]
    order = ("norm_w", "pool_in_w", "pool_group_w", "pool_group_b", "pool_scale", "pool_out_w", "gla_in_w",
             "gla_gk_w", "gla_gk_b", "gla_head_norm_w", "gla_out_w", "final_norm_w")
    return (loss.reshape(()), grad_x[None], *[res[n][0] for n in order], *[res[n][1] for n in order],
            *[res[n][2] for n in order], *[res[n][3] for n in order])
```

```python
import functools

import jax
import jax.numpy as jnp
from jax import lax
from jax.experimental import pallas as pl
from jax.experimental.pallas import tpu as pltpu

F32 = jnp.float32
BF16 = jnp.bfloat16
MESH = pl.DeviceIdType.MESH

N_DEV = 8
D_MODEL = 1024
POOL_WIDTH = 1024
POOL_GROUPS = 4
POOL_GROUP_DIM = 256
POOL_HALO = 16
GLA_HEADS = 4
GLA_HEAD_K = 128
GLA_HEAD_V = 256
GLA_KEY_WIDTH = 512
GLA_VALUE_WIDTH = 1024
GLA_GATE_RANK = 16
GLA_IN_WIDTH = 3088
GLA_IN_PAD = 3200
GLA_LOW_PAD = 128
CHUNK = 64
GATE_NORMALIZER = 16.0
RMS_EPS = 1e-6
Q_SCALE = GLA_HEAD_K ** -0.5

ADAM_LR = 0.001
ADAM_B1 = 0.9
ADAM_B2 = 0.999
ADAM_EPS = 1e-08
ADAM_WD = 0.01
ADAM_STEP = 10

LANES = 128
BF16_ROWS = 16
VMEM_LIMIT = 56 * 1024 * 1024
ROW_TILE = 256


def _dot_nn(a, b):
    return lax.dot_general(a, b, (((1,), (0,)), ((), ())), preferred_element_type=F32)


def _dot_nt(a, b):
    return lax.dot_general(a, b, (((1,), (1,)), ((), ())), preferred_element_type=F32)


def _dot_tn(a, b):
    return lax.dot_general(a, b, (((0,), (0,)), ((), ())), preferred_element_type=F32)


def _rms(x):
    rstd = lax.rsqrt(jnp.mean(x * x, axis=-1, keepdims=True) + RMS_EPS)
    return x * rstd, rstd


def _rms_bwd(dxhat, xhat, rstd):
    return rstd * (dxhat - xhat * jnp.mean(dxhat * xhat, axis=-1, keepdims=True))


def _sigmoid(x):
    return 1.0 / (1.0 + jnp.exp(-x))


def _params(sem=("arbitrary",)):
    return pltpu.CompilerParams(dimension_semantics=sem, vmem_limit_bytes=VMEM_LIMIT)


def _full(shape):
    return pl.BlockSpec(shape, lambda i: (0,) * len(shape))


def _window_sums(ext, forward):
    n = ext.shape[0]
    outs = []
    for g in range(POOL_GROUPS):
        s = ext[:, g * POOL_GROUP_DIM:(g + 1) * POOL_GROUP_DIM]
        for k in range(g + 1):
            shift = (1 << k) if forward else n - (1 << k)
            s = s + pltpu.roll(s, shift, axis=0)
        outs.append(s[:n - POOL_HALO])
    return outs


def _inv_count(row0, tm):
    row = row0 + lax.broadcasted_iota(jnp.int32, (tm, 1), 0)
    return [1.0 / jnp.minimum(row + 1, 2 << g).astype(F32) for g in range(POOL_GROUPS)]


def _pool_mix(u, u_prev, row0, gw_ref, gb):
    tm = u.shape[0]
    sums = _window_sums(jnp.concatenate([u, u_prev], axis=0), True)
    inv = _inv_count(row0, tm)
    pooled, mixed = [], []
    for g in range(POOL_GROUPS):
        ug = u[:, g * POOL_GROUP_DIM:(g + 1) * POOL_GROUP_DIM]
        pg = (sums[g] * inv[g] - ug).astype(BF16)
        pooled.append(pg)
        mixed.append(_dot_nn(pg, gw_ref[g]))
    return pooled, jnp.concatenate(mixed, axis=1) + gb


def _pool_fwd_call(x, nw, w_in, gw, gb, sc, w_out, rider=None):
    seq = x.shape[0]
    tm = ROW_TILE
    nt = seq // tm

    def main(x_ref, nw_ref, win_ref, gw_ref, gb_ref, sc_ref, wout_ref, h_ref, p_ref, halo_ref):
        i = pl.program_id(0)

        @pl.when(i == 0)
        def _():
            halo_ref[...] = jnp.zeros_like(halo_ref)

        xt = x_ref[...]
        xhat, _ = _rms(xt)
        n = (xhat * nw_ref[...]).astype(BF16)
        p = _dot_nn(n, win_ref[...])
        p_ref[...] = p
        u = p[:, :POOL_WIDTH]
        gate = p[:, POOL_WIDTH:]
        _, mixed = _pool_mix(u, halo_ref[...], i * tm, gw_ref, gb_ref[...])
        halo_ref[...] = u[tm - POOL_HALO:, :]
        y = (mixed * sc_ref[...] * (gate * _sigmoid(gate))).astype(BF16)
        h_ref[...] = xt + _dot_nn(y, wout_ref[...])

    def body(*refs):
        own, comm = _split_refs(refs, 7, 2, 1, rider)
        _ride_before(comm, pl.program_id(0), nt)
        main(*own)
        _ride_after(comm, pl.program_id(0), nt)

    return pl.pallas_call(
        body, name="pool_fwd", grid=(nt,),
        in_specs=_extend([pl.BlockSpec((tm, D_MODEL), lambda i: (i, 0)), _full((1, D_MODEL)),
                          _full((D_MODEL, 2 * POOL_WIDTH)), _full((POOL_GROUPS, POOL_GROUP_DIM, POOL_GROUP_DIM)),
                          _full((1, POOL_WIDTH)), _full((1, POOL_WIDTH)), _full((POOL_WIDTH, D_MODEL))],
                         rider, "in_specs"),
        out_specs=_extend([pl.BlockSpec((tm, D_MODEL), lambda i: (i, 0)),
                           pl.BlockSpec((tm, 2 * POOL_WIDTH), lambda i: (i, 0))], rider, "out_specs"),
        out_shape=_extend([jax.ShapeDtypeStruct((seq, D_MODEL), F32),
                           jax.ShapeDtypeStruct((seq, 2 * POOL_WIDTH), F32)], rider, "out_shape"),
        scratch_shapes=_extend([pltpu.VMEM((POOL_HALO, POOL_WIDTH), F32)], rider, "scratch"),
        compiler_params=_params(),
    )(x, nw, w_in, gw, gb, sc, w_out, *_extend([], rider, "arrays"))


def _pool_bwd_call(dh, p, gw, gb, sc, w_out, rider=None):
    seq = dh.shape[0]
    tm = ROW_TILE
    nt = seq // tm
    halo_blocks = tm // POOL_HALO

    def main(dh_ref, p_ref, pprev_ref, gw_ref, gb_ref, sc_ref, wout_ref,
             dp_ref, dwout_hbm, dgw_hbm, dgb_ref, dsc_ref, carry_ref, dwout_acc, dgw_acc, dwout_stage, dgw_stage):
        i = pl.program_id(0)
        t = nt - 1 - i

        @pl.when(i == 0)
        def _():
            carry_ref[...] = jnp.zeros_like(carry_ref)
            dwout_acc[...] = jnp.zeros_like(dwout_acc)
            dgw_acc[...] = jnp.zeros_like(dgw_acc)
            dgb_ref[...] = jnp.zeros_like(dgb_ref)
            dsc_ref[...] = jnp.zeros_like(dsc_ref)

        p = p_ref[...]
        u = p[:, :POOL_WIDTH]
        gate = p[:, POOL_WIDTH:]
        u_prev = jnp.where(t > 0, pprev_ref[:, :POOL_WIDTH], 0.0)
        pooled, mixed = _pool_mix(u, u_prev, t * tm, gw_ref, gb_ref[...])
        sg = _sigmoid(gate)
        silu = gate * sg
        sc = sc_ref[...]
        dhb = dh_ref[...].astype(BF16)
        y = (mixed * sc * silu).astype(BF16)
        dwout_acc[...] += _dot_tn(y, dhb)
        dy = _dot_nt(dhb, wout_ref[...])
        dmixed = dy * sc * silu
        dsc_ref[...] += jnp.sum(dy * mixed * silu, axis=0, keepdims=True)
        dgate = dy * mixed * sc * (sg * (1.0 + gate * (1.0 - sg)))
        dgb_ref[...] += jnp.sum(dmixed, axis=0, keepdims=True)
        inv = _inv_count(t * tm, tm)
        dpooled, scaled = [], []
        for g in range(POOL_GROUPS):
            dmg = dmixed[:, g * POOL_GROUP_DIM:(g + 1) * POOL_GROUP_DIM].astype(BF16)
            dgw_acc[g] += _dot_tn(pooled[g], dmg)
            dpg = _dot_nt(dmg, gw_ref[g])
            dpooled.append(dpg)
            scaled.append(dpg * inv[g])
        r = jnp.concatenate(scaled, axis=1)
        sums = _window_sums(jnp.concatenate([r, carry_ref[...]], axis=0), False)
        carry_ref[...] = r[:POOL_HALO, :]
        du = jnp.concatenate([sums[g] - dpooled[g] for g in range(POOL_GROUPS)], axis=1)
        dp_ref[...] = jnp.concatenate([du, dgate], axis=1).astype(BF16)

        @pl.when(i == nt - 1)
        def _():
            dwout_stage[...] = dwout_acc[...].astype(BF16)
            dgw_stage[...] = dgw_acc[...].astype(BF16)
            pltpu.sync_copy(dwout_stage, dwout_hbm)
            pltpu.sync_copy(dgw_stage, dgw_hbm)

    def body(*refs):
        own, comm = _split_refs(refs, 7, 5, 5, rider)
        _ride_before(comm, pl.program_id(0), nt)
        main(*own)
        _ride_after(comm, pl.program_id(0), nt)

    rev = lambda i: (nt - 1 - i, 0)
    return pl.pallas_call(
        body, name="pool_bwd", grid=(nt,),
        in_specs=_extend([pl.BlockSpec((tm, D_MODEL), rev), pl.BlockSpec((tm, 2 * POOL_WIDTH), rev),
                          pl.BlockSpec((POOL_HALO, 2 * POOL_WIDTH),
                                       lambda i: (jnp.maximum((nt - 1 - i) * halo_blocks - 1, 0), 0)),
                          _full((POOL_GROUPS, POOL_GROUP_DIM, POOL_GROUP_DIM)), _full((1, POOL_WIDTH)),
                          _full((1, POOL_WIDTH)), _full((POOL_WIDTH, D_MODEL))], rider, "in_specs"),
        out_specs=_extend([pl.BlockSpec((tm, 2 * POOL_WIDTH), rev), pl.BlockSpec(memory_space=pl.ANY),
                           pl.BlockSpec(memory_space=pl.ANY), _full((1, POOL_WIDTH)), _full((1, POOL_WIDTH))],
                          rider, "out_specs"),
        out_shape=_extend([jax.ShapeDtypeStruct((seq, 2 * POOL_WIDTH), BF16),
                           jax.ShapeDtypeStruct((POOL_WIDTH, D_MODEL), BF16),
                           jax.ShapeDtypeStruct((POOL_GROUPS, POOL_GROUP_DIM, POOL_GROUP_DIM), BF16),
                           jax.ShapeDtypeStruct((1, POOL_WIDTH), F32), jax.ShapeDtypeStruct((1, POOL_WIDTH), F32)],
                          rider, "out_shape"),
        scratch_shapes=_extend([pltpu.VMEM((POOL_HALO, POOL_WIDTH), F32), pltpu.VMEM((POOL_WIDTH, D_MODEL), F32),
                                pltpu.VMEM((POOL_GROUPS, POOL_GROUP_DIM, POOL_GROUP_DIM), F32),
                                pltpu.VMEM((POOL_WIDTH, D_MODEL), BF16),
                                pltpu.VMEM((POOL_GROUPS, POOL_GROUP_DIM, POOL_GROUP_DIM), BF16)], rider, "scratch"),
        compiler_params=_params(),
    )(dh, p, p, gw, gb, sc, w_out, *_extend([], rider, "arrays"))


def _inproj_bwd_call(name, dproj, h_in, nw, w_in, dres, rider=None):
    seq = h_in.shape[0]
    width = w_in.shape[1]
    tm = ROW_TILE
    nt = seq // tm

    def main(dproj_ref, h_ref, nw_ref, win_ref, dres_ref, dh_ref, dw_hbm, dnw_ref, dw_acc, dw_stage):
        i = pl.program_id(0)

        @pl.when(i == 0)
        def _():
            dw_acc[...] = jnp.zeros_like(dw_acc)
            dnw_ref[...] = jnp.zeros_like(dnw_ref)

        xhat, rstd = _rms(h_ref[...])
        nw_row = nw_ref[...]
        n = (xhat * nw_row).astype(BF16)
        dpb = dproj_ref[...]
        dw_acc[...] += _dot_tn(n, dpb)
        dn = _dot_nt(dpb, win_ref[...])
        dnw_ref[...] += jnp.sum(dn * xhat, axis=0, keepdims=True)
        dh_ref[...] = _rms_bwd(dn * nw_row, xhat, rstd) + dres_ref[...]

        @pl.when(i == nt - 1)
        def _():
            dw_stage[...] = dw_acc[...].astype(BF16)
            pltpu.sync_copy(dw_stage, dw_hbm)

    def body(*refs):
        own, comm = _split_refs(refs, 5, 3, 2, rider)
        _ride_before(comm, pl.program_id(0), nt)
        main(*own)
        _ride_after(comm, pl.program_id(0), nt)

    row = lambda i: (i, 0)
    return pl.pallas_call(
        body, name=name, grid=(nt,),
        in_specs=_extend([pl.BlockSpec((tm, width), row), pl.BlockSpec((tm, D_MODEL), row), _full((1, D_MODEL)),
                          _full((D_MODEL, width)), pl.BlockSpec((tm, D_MODEL), row)], rider, "in_specs"),
        out_specs=_extend([pl.BlockSpec((tm, D_MODEL), row), pl.BlockSpec(memory_space=pl.ANY),
                           _full((1, D_MODEL))], rider, "out_specs"),
        out_shape=_extend([jax.ShapeDtypeStruct((seq, D_MODEL), F32), jax.ShapeDtypeStruct((D_MODEL, width), BF16),
                           jax.ShapeDtypeStruct((1, D_MODEL), F32)], rider, "out_shape"),
        scratch_shapes=_extend([pltpu.VMEM((D_MODEL, width), F32), pltpu.VMEM((D_MODEL, width), BF16)],
                               rider, "scratch"),
        compiler_params=_params(),
    )(dproj, h_in, nw, w_in, dres, *_extend([], rider, "arrays"))


def _chunk_scan(x, reverse):
    n = x.shape[0]
    pos = lax.broadcasted_iota(jnp.int32, (n, 1), 0) & (CHUNK - 1)
    k = 1
    while k < CHUNK:
        if reverse:
            x = x + jnp.where(pos < CHUNK - k, pltpu.roll(x, n - k, axis=0), 0.0)
        else:
            x = x + jnp.where(pos >= k, pltpu.roll(x, k, axis=0), 0.0)
        k *= 2
    return x


class _GlaTile:
    def __init__(self, proj, gkw_ref, gkb):
        tm = proj.shape[0]
        self.q = proj[:, :GLA_KEY_WIDTH] * Q_SCALE
        self.k = proj[:, GLA_KEY_WIDTH:2 * GLA_KEY_WIDTH]
        self.v = proj[:, 2 * GLA_KEY_WIDTH:2 * GLA_KEY_WIDTH + GLA_VALUE_WIDTH]
        self.gate = proj[:, 2 * GLA_KEY_WIDTH + GLA_VALUE_WIDTH:2 * GLA_KEY_WIDTH + 2 * GLA_VALUE_WIDTH]
        self.low_b = proj[:, 2 * GLA_KEY_WIDTH + 2 * GLA_VALUE_WIDTH:].astype(BF16)
        self.z = _dot_nn(self.low_b, gkw_ref[...]) + gkb
        log_g = (jnp.minimum(self.z, 0.0) - jnp.log(1.0 + jnp.exp(-jnp.abs(self.z)))) / GATE_NORMALIZER
        self.c = _chunk_scan(log_g, False)
        is_last = lax.broadcasted_iota(jnp.int32, (CHUNK, 1), 0) == CHUNK - 1
        last = [jnp.sum(jnp.where(is_last, self.c[j * CHUNK:(j + 1) * CHUNK, :], 0.0), axis=0, keepdims=True)
                for j in range(tm // CHUNK)]
        self.c_last = last
        c_last_rows = jnp.concatenate([jnp.broadcast_to(r, (CHUNK, GLA_KEY_WIDTH)) for r in last], axis=0)
        self.e_pos = jnp.exp(self.c)
        self.e_neg = jnp.exp(-self.c)
        self.e_rest = jnp.exp(c_last_rows - self.c)
        self.a = self.q * self.e_pos
        self.b = self.k * self.e_neg
        self.cn = self.q * self.e_neg
        self.dp = self.k * self.e_pos
        self.kd = self.k * self.e_rest
        self.a_b, self.b_b, self.cn_b, self.dp_b, self.kd_b, self.v_b = (
            t.astype(BF16) for t in (self.a, self.b, self.cn, self.dp, self.kd, self.v))
        idx_t = lax.broadcasted_iota(jnp.int32, (CHUNK, CHUNK), 0)
        idx_s = lax.broadcasted_iota(jnp.int32, (CHUNK, CHUNK), 1)
        self.lower = idx_t >= idx_s

    @staticmethod
    def rows(j):
        return slice(j * CHUNK, (j + 1) * CHUNK)

    @staticmethod
    def kcols(h):
        return slice(h * GLA_HEAD_K, (h + 1) * GLA_HEAD_K)

    @staticmethod
    def vcols(h):
        return slice(h * GLA_HEAD_V, (h + 1) * GLA_HEAD_V)

    def scores(self, j, h):
        r, kc = self.rows(j), self.kcols(h)
        fwd = _dot_nt(self.a_b[r, kc], self.b_b[r, kc])
        bwd = _dot_nt(self.cn_b[r, kc], self.dp_b[r, kc])
        return jnp.where(self.lower, fwd, bwd).astype(BF16)


def _gla_fwd_call(h1, nw, w_in, gkw, gkb, hw, w_out, wf, target):
    seq = h1.shape[0]
    tm = ROW_TILE
    nt = seq // tm
    cpt = tm // CHUNK
    n_chunks = seq // CHUNK

    def body(h_ref, nw_ref, win_ref, gkw_ref, gkb_ref, hw_ref, wout_ref, wf_ref, tgt_ref,
             dh2_ref, proj_ref, o_ref, st_ref, loss_ref, dwf_ref, state_ref):
        i = pl.program_id(0)

        @pl.when(i == 0)
        def _():
            state_ref[...] = jnp.zeros_like(state_ref)
            loss_ref[...] = jnp.zeros_like(loss_ref)
            dwf_ref[...] = jnp.zeros_like(dwf_ref)

        ht = h_ref[...]
        xhat, _ = _rms(ht)
        n = (xhat * nw_ref[...]).astype(BF16)
        proj = _dot_nn(n, win_ref[...])
        proj_ref[...] = proj
        g = _GlaTile(proj, gkw_ref, gkb_ref[...])
        o_rows = []
        for j in range(cpt):
            r = g.rows(j)
            o_heads = []
            for h in range(GLA_HEADS):
                kc, vc = g.kcols(h), g.vcols(h)
                srows = slice(h * GLA_HEAD_V, (h + 1) * GLA_HEAD_V)
                state = state_ref[srows, :]
                st_ref[j, srows, :] = state
                o_heads.append(_dot_nn(g.scores(j, h), g.v_b[r, vc]) + _dot_nt(g.a_b[r, kc], state.astype(BF16)))
                decay = jnp.exp(g.c_last[j][:, kc])
                state_ref[srows, :] = state * decay + _dot_tn(g.v_b[r, vc], g.kd_b[r, kc])
            o_rows.append(jnp.concatenate(o_heads, axis=1))
        o = jnp.concatenate(o_rows, axis=0)
        o_ref[...] = o
        hw_row = hw_ref[...]
        on = jnp.concatenate([_rms(o[:, g.vcols(h)])[0] for h in range(GLA_HEADS)], axis=1) * hw_row
        y = (on * (g.gate * _sigmoid(g.gate))).astype(BF16)
        h2 = ht + _dot_nn(y, wout_ref[...])
        xhat2, rstd2 = _rms(h2)
        wf_row = wf_ref[...]
        err = xhat2 * wf_row - tgt_ref[...]
        loss_ref[...] += 0.5 * jnp.sum(err * err) / D_MODEL
        dout = err * (1.0 / D_MODEL)
        dwf_ref[...] += jnp.sum(dout * xhat2, axis=0, keepdims=True)
        dh2_ref[...] = _rms_bwd(dout * wf_row, xhat2, rstd2)

    row = lambda i: (i, 0)
    return pl.pallas_call(
        body, name="gla_fwd", grid=(nt,),
        in_specs=[pl.BlockSpec((tm, D_MODEL), row), _full((1, D_MODEL)), _full((D_MODEL, GLA_IN_PAD)),
                  _full((GLA_LOW_PAD, GLA_KEY_WIDTH)), _full((1, GLA_KEY_WIDTH)), _full((1, GLA_VALUE_WIDTH)),
                  _full((GLA_VALUE_WIDTH, D_MODEL)), _full((1, D_MODEL)), pl.BlockSpec((tm, D_MODEL), row)],
        out_specs=[pl.BlockSpec((tm, D_MODEL), row), pl.BlockSpec((tm, GLA_IN_PAD), row),
                   pl.BlockSpec((tm, GLA_VALUE_WIDTH), row),
                   pl.BlockSpec((cpt, GLA_VALUE_WIDTH, GLA_HEAD_K), lambda i: (i, 0, 0)),
                   _full((8, LANES)), _full((1, D_MODEL))],
        out_shape=[jax.ShapeDtypeStruct((seq, D_MODEL), F32), jax.ShapeDtypeStruct((seq, GLA_IN_PAD), F32),
                   jax.ShapeDtypeStruct((seq, GLA_VALUE_WIDTH), F32),
                   jax.ShapeDtypeStruct((n_chunks, GLA_VALUE_WIDTH, GLA_HEAD_K), F32),
                   jax.ShapeDtypeStruct((8, LANES), F32), jax.ShapeDtypeStruct((1, D_MODEL), F32)],
        scratch_shapes=[pltpu.VMEM((GLA_VALUE_WIDTH, GLA_HEAD_K), F32)],
        compiler_params=_params(),
    )(h1, nw, w_in, gkw, gkb, hw, w_out, wf, target)


def _gla_bwd_call(dh2, proj, o, states, gkw, gkb, hw, w_out):
    seq = dh2.shape[0]
    tm = ROW_TILE
    nt = seq // tm
    cpt = tm // CHUNK

    def body(dh_ref, proj_ref, o_ref, st_ref, gkw_ref, gkb_ref, hw_ref, wout_ref,
             dproj_ref, dwout_hbm, dhw_ref, dgkw_ref, dgkb_ref, dstate_ref, dwout_acc, dwout_stage):
        i = pl.program_id(0)

        @pl.when(i == 0)
        def _():
            dstate_ref[...] = jnp.zeros_like(dstate_ref)
            dwout_acc[...] = jnp.zeros_like(dwout_acc)
            dhw_ref[...] = jnp.zeros_like(dhw_ref)
            dgkw_ref[...] = jnp.zeros_like(dgkw_ref)
            dgkb_ref[...] = jnp.zeros_like(dgkb_ref)

        g = _GlaTile(proj_ref[...], gkw_ref, gkb_ref[...])
        dhb = dh_ref[...].astype(BF16)
        o = o_ref[...]
        hw_row = hw_ref[...]
        dy = _dot_nt(dhb, wout_ref[...])
        sg = _sigmoid(g.gate)
        silu = g.gate * sg
        don = dy * silu
        on_parts, do_parts, dhw_parts = [], [], []
        for h in range(GLA_HEADS):
            vc = g.vcols(h)
            xh, rs = _rms(o[:, vc])
            on_parts.append(xh * hw_row[:, vc])
            dhw_parts.append(jnp.sum(don[:, vc] * xh, axis=0, keepdims=True))
            do_parts.append(_rms_bwd(don[:, vc] * hw_row[:, vc], xh, rs))
        on = jnp.concatenate(on_parts, axis=1)
        dwout_acc[...] += _dot_tn((on * silu).astype(BF16), dhb)
        dhw_ref[...] += jnp.concatenate(dhw_parts, axis=1)
        dgate = dy * on * (sg * (1.0 + g.gate * (1.0 - sg)))
        do_b = jnp.concatenate(do_parts, axis=1).astype(BF16)

        last_row = lax.broadcasted_iota(jnp.int32, (CHUNK, 1), 0) == CHUNK - 1
        dq_rows, dk_rows, dv_rows, dc_rows = [None] * cpt, [None] * cpt, [None] * cpt, [None] * cpt
        for j in reversed(range(cpt)):
            r = g.rows(j)
            dq_h, dk_h, dv_h, dc_h = [], [], [], []
            for h in range(GLA_HEADS):
                kc, vc = g.kcols(h), g.vcols(h)
                srows = slice(h * GLA_HEAD_V, (h + 1) * GLA_HEAD_V)
                state = st_ref[j, srows, :]
                dstate = dstate_ref[srows, :]
                dstate_b = dstate.astype(BF16)
                do_c = do_b[r, vc]
                scores = g.scores(j, h)
                dscores = _dot_nt(do_c, g.v_b[r, vc])
                dfwd = jnp.where(g.lower, dscores, 0.0).astype(BF16)
                dbwd = jnp.where(g.lower, 0.0, dscores).astype(BF16)
                dv_h.append(_dot_tn(scores, do_c) + _dot_nt(g.kd_b[r, kc], dstate_b))
                da = _dot_nn(dfwd, g.b_b[r, kc]) + _dot_nn(do_c, state.astype(BF16))
                db = _dot_tn(dfwd, g.a_b[r, kc])
                dcn = _dot_nn(dbwd, g.dp_b[r, kc])
                ddp = _dot_tn(dbwd, g.cn_b[r, kc])
                dkd = _dot_nn(g.v_b[r, vc], dstate_b)
                decay = jnp.exp(g.c_last[j][:, kc])
                dstate_ref[srows, :] = _dot_tn(do_c, g.a_b[r, kc]) + dstate * decay
                kd_c = g.kd[r, kc]
                dkd_kd = dkd * kd_c
                dc_last = (jnp.sum(dkd_kd, axis=0, keepdims=True)
                           + decay * jnp.sum(state * dstate, axis=0, keepdims=True))
                dq_h.append(Q_SCALE * (da * g.e_pos[r, kc] + dcn * g.e_neg[r, kc]))
                dk_h.append(db * g.e_neg[r, kc] + ddp * g.e_pos[r, kc] + dkd * g.e_rest[r, kc])
                dc = (da * g.a[r, kc] - db * g.b[r, kc] - dcn * g.cn[r, kc] + ddp * g.dp[r, kc] - dkd_kd)
                dc_h.append(dc + jnp.where(last_row, dc_last, 0.0))
            dq_rows[j] = jnp.concatenate(dq_h, axis=1)
            dk_rows[j] = jnp.concatenate(dk_h, axis=1)
            dv_rows[j] = jnp.concatenate(dv_h, axis=1)
            dc_rows[j] = jnp.concatenate(dc_h, axis=1)
        dq = jnp.concatenate(dq_rows, axis=0)
        dk = jnp.concatenate(dk_rows, axis=0)
        dv = jnp.concatenate(dv_rows, axis=0)
        dlog_g = _chunk_scan(jnp.concatenate(dc_rows, axis=0), True)
        dz = dlog_g * (1.0 / GATE_NORMALIZER) * (1.0 - _sigmoid(g.z))
        dzb = dz.astype(BF16)
        dgkb_ref[...] += jnp.sum(dz, axis=0, keepdims=True)
        dgkw_ref[...] += _dot_tn(g.low_b, dzb)
        dlow = _dot_nt(dzb, gkw_ref[...])
        dproj_ref[...] = jnp.concatenate([dq, dk, dv, dgate, dlow], axis=1).astype(BF16)

        @pl.when(i == nt - 1)
        def _():
            dwout_stage[...] = dwout_acc[...].astype(BF16)
            pltpu.sync_copy(dwout_stage, dwout_hbm)

    rev = lambda i: (nt - 1 - i, 0)
    return pl.pallas_call(
        body, name="gla_bwd", grid=(nt,),
        in_specs=[pl.BlockSpec((tm, D_MODEL), rev), pl.BlockSpec((tm, GLA_IN_PAD), rev),
                  pl.BlockSpec((tm, GLA_VALUE_WIDTH), rev),
                  pl.BlockSpec((cpt, GLA_VALUE_WIDTH, GLA_HEAD_K), lambda i: (nt - 1 - i, 0, 0)),
                  _full((GLA_LOW_PAD, GLA_KEY_WIDTH)), _full((1, GLA_KEY_WIDTH)), _full((1, GLA_VALUE_WIDTH)),
                  _full((GLA_VALUE_WIDTH, D_MODEL))],
        out_specs=[pl.BlockSpec((tm, GLA_IN_PAD), rev), pl.BlockSpec(memory_space=pl.ANY),
                   _full((1, GLA_VALUE_WIDTH)), _full((GLA_LOW_PAD, GLA_KEY_WIDTH)), _full((1, GLA_KEY_WIDTH))],
        out_shape=[jax.ShapeDtypeStruct((seq, GLA_IN_PAD), BF16), jax.ShapeDtypeStruct((GLA_VALUE_WIDTH, D_MODEL), BF16),
                   jax.ShapeDtypeStruct((1, GLA_VALUE_WIDTH), F32), jax.ShapeDtypeStruct((GLA_LOW_PAD, GLA_KEY_WIDTH), F32),
                   jax.ShapeDtypeStruct((1, GLA_KEY_WIDTH), F32)],
        scratch_shapes=[pltpu.VMEM((GLA_VALUE_WIDTH, GLA_HEAD_K), F32), pltpu.VMEM((GLA_VALUE_WIDTH, D_MODEL), F32),
                        pltpu.VMEM((GLA_VALUE_WIDTH, D_MODEL), BF16)],
        compiler_params=_params(),
    )(dh2, proj, o, states, gkw, gkb, hw, w_out)


def _position():
    return lax.axis_index("x"), lax.axis_index("y"), lax.axis_index("c")


def _lead_slot(ref, d):
    return ref.at[d]


def _row_slot(rows):
    return lambda ref, d: ref.at[pl.ds(pl.multiple_of(d * rows, rows), rows)]


def _dim1_slot(size):
    return lambda ref, d: ref.at[:, pl.ds(pl.multiple_of(d * size, size), size)]


class _Gather:
    def __init__(self, in_refs, out_refs, slots, send_sems, recv_sems, local_sems):
        self.in_refs, self.out_refs, self.slots = in_refs, out_refs, slots
        self.send_sems, self.recv_sems, self.local_sems = send_sems, recv_sems, local_sems
        self.n = len(in_refs)
        x, y, c = _position()
        self.c = c
        self.me, self.sibling = (x, y, c), (x, y, 1 - c)
        self.chips = [(1 - x, y), (x, 1 - y), (1 - x, 1 - y)]

    def _copy(self, a, k, block, to, from_input=False):
        part = self.slots[a](self.out_refs[a], 4 * block[0] + 2 * block[1] + block[2])
        return pltpu.make_async_remote_copy(
            src_ref=self.in_refs[a] if from_input else part, dst_ref=part,
            send_sem=self.send_sems.at[a, k], recv_sem=self.recv_sems.at[a, k], device_id=to, device_id_type=MESH)

    def _mine(self):
        return [pltpu.make_async_copy(self.in_refs[a], self.slots[a](self.out_refs[a], 4 * self.me[0] + 2 * self.me[1]
                                                                    + self.me[2]), self.local_sems.at[a])
                for a in range(self.n)]

    def _first(self):
        first = [self._copy(a, 0, self.me, self.sibling, True) for a in range(self.n)]
        return first + [self._copy(a, 1 + j, self.me, (*chip, self.c), True)
                        for j, chip in enumerate(self.chips) for a in range(self.n)]

    def _passed(self):
        return [self._copy(a, 4 + j, (*chip, self.c), self.sibling)
                for j, chip in enumerate(self.chips) for a in range(self.n)]

    def start(self):
        for cp in self._mine() + self._first():
            cp.start()

    def forward(self):
        passed = self._passed()
        for j, chip in enumerate(self.chips):
            for a in range(self.n):
                self._copy(a, 1 + j, (*chip, self.c), self.me).wait_recv()
                passed[j * self.n + a].start()

    def finish(self):
        for a in range(self.n):
            self._copy(a, 0, self.sibling, self.me).wait_recv()
        for j, chip in enumerate(self.chips):
            for a in range(self.n):
                self._copy(a, 4 + j, (*chip, 1 - self.c), self.me).wait_recv()
        for cp in self._first() + self._passed():
            cp.wait_send()
        for cp in self._mine():
            cp.wait()


class _Exchange:
    def __init__(self, in_refs, out_refs, slots, send_sems, recv_sems, local_sems):
        self.in_refs, self.out_refs, self.slots = in_refs, out_refs, slots
        self.send_sems, self.recv_sems, self.local_sems = send_sems, recv_sems, local_sems
        self.n = len(in_refs)
        self.pos = _position()

    def _copies(self):
        x, y, c = self.pos
        me = 4 * x + 2 * y + c
        mine = [pltpu.make_async_copy(self.slots[a](self.in_refs[a], me), self.out_refs[a].at[me],
                                      self.local_sems.at[a]) for a in range(self.n)]
        remote = []
        for k in range(1, N_DEV):
            px, py, pc = x ^ (k >> 2), y ^ ((k >> 1) & 1), c ^ (k & 1)
            for a in range(self.n):
                remote.append(pltpu.make_async_remote_copy(
                    src_ref=self.slots[a](self.in_refs[a], 4 * px + 2 * py + pc), dst_ref=self.out_refs[a].at[me],
                    send_sem=self.send_sems.at[a, k - 1], recv_sem=self.recv_sems.at[a, k - 1],
                    device_id=(px, py, pc), device_id_type=MESH))
        return mine, remote

    def start(self):
        mine, remote = self._copies()
        for cp in mine + remote:
            cp.start()

    def forward(self):
        pass

    def finish(self):
        mine, remote = self._copies()
        for cp in remote:
            cp.wait_recv()
        for cp in remote:
            cp.wait_send()
        for cp in mine:
            cp.wait()


class _Rider:
    def __init__(self, kind, arrays, out_shapes, slots):
        self.kind, self.arrays, self.slots = kind, list(arrays), slots
        self.n = len(self.arrays)
        hbm = pl.BlockSpec(memory_space=pl.ANY)
        self.in_specs = [hbm] * self.n
        self.out_specs = [hbm] * self.n
        self.out_shape = [jax.ShapeDtypeStruct(tuple(s), a.dtype) for s, a in zip(out_shapes, self.arrays)]
        self.scratch = [pltpu.SemaphoreType.DMA((self.n, 7)), pltpu.SemaphoreType.DMA((self.n, 7)),
                        pltpu.SemaphoreType.DMA((self.n,))]

    def bind(self, in_refs, out_refs, sems):
        return self.kind(in_refs, out_refs, self.slots, *sems)


def _gather_rider(shards, full_shapes, slots):
    return _Rider(_Gather, shards, full_shapes, slots)


def _exchange_rider(sends, part_shapes, slots):
    return _Rider(_Exchange, sends, [(N_DEV,) + tuple(s) for s in part_shapes], slots)


def _split_refs(refs, n_in, n_out, n_scratch, rider):
    k = rider.n if rider is not None else 0
    ins, r_ins = refs[:n_in], refs[n_in:n_in + k]
    outs, r_outs = refs[n_in + k:n_in + k + n_out], refs[n_in + k + n_out:n_in + 2 * k + n_out]
    rest = refs[n_in + 2 * k + n_out:]
    scratch, sems = rest[:n_scratch], rest[n_scratch:]
    comm = rider.bind(r_ins, r_outs, sems) if rider is not None else None
    return ins + outs + scratch, comm


def _ride_before(comm, i, nt):
    if comm is not None:
        pl.when(i == 0)(comm.start)
        pl.when(i == nt // 2)(comm.forward)


def _ride_after(comm, i, nt):
    if comm is not None:
        pl.when(i == nt - 1)(comm.finish)


def _extend(specs, rider, field):
    return list(specs) + (getattr(rider, field) if rider is not None else [])


def _comm_call(name, rider):
    def body(*refs):
        _, comm = _split_refs(refs, 0, 0, 0, rider)
        comm.start()
        comm.forward()
        comm.finish()

    return pl.pallas_call(body, name=name, in_specs=rider.in_specs, out_specs=rider.out_specs,
                          out_shape=rider.out_shape, scratch_shapes=rider.scratch)(*rider.arrays)


def _adamw(w, g, m, v):
    m = ADAM_B1 * m + (1.0 - ADAM_B1) * g
    v = ADAM_B2 * v + (1.0 - ADAM_B2) * (g * g)
    m_hat = m / (1.0 - ADAM_B1 ** ADAM_STEP)
    v_hat = v / (1.0 - ADAM_B2 ** ADAM_STEP)
    delta = -ADAM_LR * (m_hat / (jnp.sqrt(v_hat) + ADAM_EPS) + ADAM_WD * w)
    return delta, m, v


def _sum_parts(parts_ref, index=()):
    g = parts_ref[(0,) + index].astype(F32)
    for s in range(1, N_DEV):
        g = g + parts_ref[(s,) + index].astype(F32)
    return g


def _adamw_call(name, parts, w, m, v, block_rows):
    rows, cols = w.shape
    nb = rows // block_rows

    def body(parts_ref, w_ref, m_ref, v_ref, g_ref, delta_ref, m_out, v_out):
        g = _sum_parts(parts_ref)
        delta, m_new, v_new = _adamw(w_ref[...], g, m_ref[...], v_ref[...])
        g_ref[...] = g
        delta_ref[...] = delta
        m_out[...] = m_new
        v_out[...] = v_new

    blk = pl.BlockSpec((block_rows, cols), lambda i: (i, 0))
    return pl.pallas_call(
        body, name=name, grid=(nb,),
        in_specs=[pl.BlockSpec((N_DEV, block_rows, cols), lambda i: (0, i, 0)), blk, blk, blk],
        out_specs=[blk, blk, blk, blk],
        out_shape=[jax.ShapeDtypeStruct((rows, cols), F32)] * 4,
        compiler_params=_params(("parallel",)),
    )(parts, w, m, v)


WIDE_ROWS = 8
NARROW_ROWS = 40
NARROW_GKW_ROW = 8
NARROW_GKB_ROW = 24
NARROW_HW_ROW = 32
GROUP_SHARD = POOL_GROUP_DIM // N_DEV
KEY_SHARD = GLA_KEY_WIDTH // N_DEV
HEAD_V_SHARD = GLA_HEAD_V // N_DEV


def _small_adamw_call(wide, narrow, w, m, v):
    names = ("norm_w", "pool_scale", "final_norm_w", "pool_group_b", "gla_gk_w", "gla_gk_b", "gla_head_norm_w")
    where = {
        "norm_w": (0, slice(0, 2), slice(None)),
        "pool_scale": (0, slice(2, 3), slice(None)),
        "final_norm_w": (0, slice(3, 4), slice(None)),
        "pool_group_b": (1, slice(0, POOL_GROUPS), slice(0, GROUP_SHARD)),
        "gla_gk_w": (1, slice(NARROW_GKW_ROW, NARROW_GKW_ROW + GLA_GATE_RANK), slice(0, KEY_SHARD)),
        "gla_gk_b": (1, slice(NARROW_GKB_ROW, NARROW_GKB_ROW + 1), slice(0, KEY_SHARD)),
        "gla_head_norm_w": (1, slice(NARROW_HW_ROW, NARROW_HW_ROW + 1), slice(0, HEAD_V_SHARD)),
    }
    k = len(names)

    def body(*refs):
        parts = refs[0:2]
        w_refs, m_refs, v_refs = refs[2:2 + k], refs[2 + k:2 + 2 * k], refs[2 + 2 * k:2 + 3 * k]
        outs = refs[2 + 3 * k:]
        loss_ref = outs[0]
        loss_ref[...] = _sum_parts(parts[0], (slice(4, 5), slice(0, 1)))
        for i, name in enumerate(names):
            buf, rows, cols = where[name]
            g = _sum_parts(parts[buf], (rows, cols))
            delta, m_new, v_new = _adamw(w_refs[i][...], g, m_refs[i][...], v_refs[i][...])
            outs[1 + i][...] = g
            outs[1 + k + i][...] = delta
            outs[1 + 2 * k + i][...] = m_new
            outs[1 + 3 * k + i][...] = v_new

    vmem = pl.BlockSpec(memory_space=pltpu.VMEM)
    shapes = [jax.ShapeDtypeStruct(w[n].shape, F32) for n in names]
    res = pl.pallas_call(
        body, name="adamw_small", in_specs=[vmem] * (2 + 3 * k), out_specs=[vmem] * (1 + 4 * k),
        out_shape=[jax.ShapeDtypeStruct((1, 1), F32)] + shapes * 4,
    )(wide, narrow, *[w[n] for n in names], *[m[n] for n in names], *[v[n] for n in names])
    unzip = lambda j: dict(zip(names, res[1 + j * k:1 + (j + 1) * k]))
    return res[0], unzip(0), unzip(1), unzip(2), unzip(3)


def kernel(x, norm_w, pool_in_w, pool_group_w, pool_group_b, pool_scale, pool_out_w, gla_in_w, gla_gk_w, gla_gk_b, gla_head_norm_w, gla_out_w, final_norm_w, loss_target, m_norm_w, m_pool_in_w, m_pool_group_w, m_pool_group_b, m_pool_scale, m_pool_out_w, m_gla_in_w, m_gla_gk_w, m_gla_gk_b, m_gla_head_norm_w, m_gla_out_w, m_final_norm_w, v_norm_w, v_pool_in_w, v_pool_group_w, v_pool_group_b, v_pool_scale, v_pool_out_w, v_gla_in_w, v_gla_gk_w, v_gla_gk_b, v_gla_head_norm_w, v_gla_out_w, v_final_norm_w):
    w = dict(norm_w=norm_w, pool_in_w=pool_in_w, pool_group_w=pool_group_w, pool_group_b=pool_group_b,
             pool_scale=pool_scale, pool_out_w=pool_out_w, gla_in_w=gla_in_w, gla_gk_w=gla_gk_w, gla_gk_b=gla_gk_b,
             gla_head_norm_w=gla_head_norm_w, gla_out_w=gla_out_w, final_norm_w=final_norm_w)
    m = dict(norm_w=m_norm_w, pool_in_w=m_pool_in_w, pool_group_w=m_pool_group_w, pool_group_b=m_pool_group_b,
             pool_scale=m_pool_scale, pool_out_w=m_pool_out_w, gla_in_w=m_gla_in_w, gla_gk_w=m_gla_gk_w,
             gla_gk_b=m_gla_gk_b, gla_head_norm_w=m_gla_head_norm_w, gla_out_w=m_gla_out_w,
             final_norm_w=m_final_norm_w)
    v = dict(norm_w=v_norm_w, pool_in_w=v_pool_in_w, pool_group_w=v_pool_group_w, pool_group_b=v_pool_group_b,
             pool_scale=v_pool_scale, pool_out_w=v_pool_out_w, gla_in_w=v_gla_in_w, gla_gk_w=v_gla_gk_w,
             gla_gk_b=v_gla_gk_b, gla_head_norm_w=v_gla_head_norm_w, gla_out_w=v_gla_out_w,
             final_norm_w=v_final_norm_w)
    col_shard = GLA_IN_WIDTH // N_DEV
    row_shard = D_MODEL // N_DEV

    def lanes(a):
        return jnp.pad(a, [(0, 0)] * (a.ndim - 1) + [(0, LANES - a.shape[-1])])

    small_in = jnp.concatenate([lanes(pool_group_b[0]), lanes(gla_gk_b), lanes(gla_head_norm_w),
                                jnp.zeros((2, LANES), F32)], axis=0)
    in_cols = 2 * POOL_WIDTH // N_DEV
    pool_in, pool_gw, pool_out, small_all = _comm_call("pool_weights_all_gather", _gather_rider(
        [pool_in_w[0].astype(BF16), pool_group_w[0].astype(BF16), pool_out_w[0].astype(BF16), small_in],
        [(D_MODEL, 2 * POOL_WIDTH), (POOL_GROUPS, POOL_GROUP_DIM, POOL_GROUP_DIM), (POOL_WIDTH, D_MODEL),
         (N_DEV, 8, LANES)],
        [_dim1_slot(in_cols), _dim1_slot(GROUP_SHARD), _row_slot(row_shard), _lead_slot]))
    pool_gb = jnp.transpose(small_all[:, 0:POOL_GROUPS, :GROUP_SHARD], (1, 0, 2)).reshape(1, POOL_WIDTH)
    gla_gkb = small_all[:, POOL_GROUPS, :KEY_SHARD].reshape(1, GLA_KEY_WIDTH)
    gla_hw = jnp.tile(small_all[:, POOL_GROUPS + 1, :HEAD_V_SHARD].reshape(1, GLA_HEAD_V), (1, GLA_HEADS))
    nw0, nw1, wf = norm_w[0:1], norm_w[1:2], final_norm_w.reshape(1, D_MODEL)
    xs, target = x[0], loss_target[0]

    h1, p, gla_in_parts, gkw_parts, gla_out = _pool_fwd_call(
        xs, nw0, pool_in, pool_gw, pool_gb, pool_scale, pool_out, _gather_rider(
            [gla_in_w[0].astype(BF16), gla_gk_w[0].astype(BF16), gla_out_w[0].astype(BF16)],
            [(N_DEV, D_MODEL, col_shard), (N_DEV, GLA_GATE_RANK, KEY_SHARD), (GLA_VALUE_WIDTH, D_MODEL)],
            [_lead_slot, _lead_slot, _row_slot(row_shard)]))
    gla_in = jnp.pad(jnp.transpose(gla_in_parts, (1, 0, 2)).reshape(D_MODEL, GLA_IN_WIDTH),
                     ((0, 0), (0, GLA_IN_PAD - GLA_IN_WIDTH)))
    gla_gkw = jnp.pad(jnp.transpose(gkw_parts, (1, 0, 2)).reshape(GLA_GATE_RANK, GLA_KEY_WIDTH),
                      ((0, GLA_LOW_PAD - GLA_GATE_RANK), (0, 0)))
    dh2, proj, o, states, loss_part, dwf = _gla_fwd_call(h1, nw1, gla_in, gla_gkw, gla_gkb, gla_hw, gla_out, wf, target)

    dproj, d_gla_out, dhw, dgkw, dgkb = _gla_bwd_call(dh2, proj, o, states, gla_gkw, gla_gkb, gla_hw, gla_out)
    dh1, d_gla_in, dnw1 = _inproj_bwd_call("gla_in_bwd", dproj, h1, nw1, gla_in, dh2)
    gla_in_send = jnp.transpose(d_gla_in[:, :GLA_IN_WIDTH].reshape(D_MODEL, N_DEV, col_shard), (1, 0, 2))
    dp, d_pool_out, dgw, dgb, dsc, landed_gla_in = _pool_bwd_call(
        dh1, p, pool_gw, pool_gb, pool_scale, pool_out,
        _exchange_rider([gla_in_send], [(D_MODEL, col_shard)], [_lead_slot]))
    grad_x, d_pool_in, dnw0, landed_gla_out, landed_pool_out, landed_gw = _inproj_bwd_call(
        "pool_in_bwd", dp, xs, nw0, pool_in, dh1,
        _exchange_rider([d_gla_out, d_pool_out, dgw],
                        [(row_shard, D_MODEL), (row_shard, D_MODEL), (POOL_GROUPS, GROUP_SHARD, POOL_GROUP_DIM)],
                        [_row_slot(row_shard), _row_slot(row_shard), _dim1_slot(GROUP_SHARD)]))

    wide = jnp.concatenate([
        dnw0, dnw1, dsc, dwf, jnp.pad(loss_part[0:1, 0:1], ((0, 0), (0, D_MODEL - 1))),
        jnp.zeros((WIDE_ROWS - 5, D_MODEL), F32)], axis=0)

    def rows8(a):
        return jnp.pad(lanes(a), ((0, 0), (0, -a.shape[1] % 8), (0, 0)))

    narrow = jnp.concatenate([
        rows8(jnp.transpose(dgb.reshape(POOL_GROUPS, N_DEV, GROUP_SHARD), (1, 0, 2))),
        rows8(jnp.transpose(dgkw[:GLA_GATE_RANK].reshape(GLA_GATE_RANK, N_DEV, KEY_SHARD), (1, 0, 2))),
        rows8(dgkb.reshape(N_DEV, 1, KEY_SHARD)),
        rows8(dhw.reshape(GLA_HEADS, GLA_HEAD_V).sum(axis=0).reshape(N_DEV, 1, HEAD_V_SHARD)),
    ], axis=1)
    landed_pool_in, landed_wide, landed_narrow = _comm_call("grads_all_to_all", _exchange_rider(
        [d_pool_in, wide, narrow], [(D_MODEL, in_cols), (WIDE_ROWS, D_MODEL), (NARROW_ROWS, LANES)],
        [_dim1_slot(in_cols), lambda ref, d: ref, _lead_slot]))

    res = {}
    for name, parts, rows, cols, block in [
            ("pool_in_w", landed_pool_in, D_MODEL, in_cols, 256),
            ("pool_group_w", landed_gw, POOL_GROUPS * GROUP_SHARD, POOL_GROUP_DIM, 128),
            ("pool_out_w", landed_pool_out, row_shard, D_MODEL, 128),
            ("gla_in_w", landed_gla_in, D_MODEL, col_shard, 256),
            ("gla_out_w", landed_gla_out, row_shard, D_MODEL, 128)]:
        outs = _adamw_call("adamw_" + name, parts.reshape(N_DEV, rows, cols), w[name].reshape(rows, cols),
                           m[name].reshape(rows, cols), v[name].reshape(rows, cols), block)
        res[name] = [t.reshape(w[name].shape) for t in outs]
    small_shapes = {"norm_w": (2, D_MODEL), "pool_scale": (1, D_MODEL), "final_norm_w": (1, D_MODEL),
                    "pool_group_b": (POOL_GROUPS, GROUP_SHARD), "gla_gk_w": (GLA_GATE_RANK, KEY_SHARD),
                    "gla_gk_b": (1, KEY_SHARD), "gla_head_norm_w": (1, HEAD_V_SHARD)}
    as_small = lambda t: {n: t[n].reshape(s) for n, s in small_shapes.items()}
    loss, *small_outs = _small_adamw_call(landed_wide, landed_narrow, as_small(w), as_small(m), as_small(v))
    for name in small_shapes:
        res[name] = [t[name].reshape(w[name].shape) for t in small_outs]
    order = ("norm_w", "pool_in_w", "pool_group_w", "pool_group_b", "pool_scale", "pool_out_w", "gla_in_w",
             "gla_gk_w", "gla_gk_b", "gla_head_norm_w", "gla_out_w", "final_norm_w")
    return (loss.reshape(()), grad_x[None], *[res[n][0] for n in order], *[res[n][1] for n in order],
            *[res[n][2] for n in order], *[res[n][3] for n in order])
```

```python
import functools

import jax
import jax.numpy as jnp
from jax import lax
from jax.experimental import pallas as pl
from jax.experimental.pallas import tpu as pltpu

F32 = jnp.float32
BF16 = jnp.bfloat16
MESH = pl.DeviceIdType.MESH

N_DEV = 8
D_MODEL = 1024
POOL_WIDTH = 1024
POOL_GROUPS = 4
POOL_GROUP_DIM = 256
POOL_HALO = 16
GLA_HEADS = 4
GLA_HEAD_K = 128
GLA_HEAD_V = 256
GLA_KEY_WIDTH = 512
GLA_VALUE_WIDTH = 1024
GLA_GATE_RANK = 16
GLA_IN_WIDTH = 3088
GLA_IN_PAD = 3200
GLA_LOW_PAD = 128
CHUNK = 64
GATE_NORMALIZER = 16.0
RMS_EPS = 1e-6
Q_SCALE = GLA_HEAD_K ** -0.5

ADAM_LR = 0.001
ADAM_B1 = 0.9
ADAM_B2 = 0.999
ADAM_EPS = 1e-08
ADAM_WD = 0.01
ADAM_STEP = 10

LANES = 128
BF16_ROWS = 16
VMEM_LIMIT = 56 * 1024 * 1024
ROW_TILE = 256
GLA_IN_COLS_WITH_POOL_BWD = 768


def _dot_nn(a, b):
    return lax.dot_general(a, b, (((1,), (0,)), ((), ())), preferred_element_type=F32)


def _dot_nt(a, b):
    return lax.dot_general(a, b, (((1,), (1,)), ((), ())), preferred_element_type=F32)


def _dot_tn(a, b):
    return lax.dot_general(a, b, (((0,), (0,)), ((), ())), preferred_element_type=F32)


def _rms(x):
    rstd = lax.rsqrt(jnp.mean(x * x, axis=-1, keepdims=True) + RMS_EPS)
    return x * rstd, rstd


def _rms_bwd(dxhat, xhat, rstd):
    return rstd * (dxhat - xhat * jnp.mean(dxhat * xhat, axis=-1, keepdims=True))


def _sigmoid(x):
    return 1.0 / (1.0 + jnp.exp(-x))


def _params(sem=("arbitrary",)):
    return pltpu.CompilerParams(dimension_semantics=sem, vmem_limit_bytes=VMEM_LIMIT)


def _full(shape):
    return pl.BlockSpec(shape, lambda i: (0,) * len(shape))


def _window_sums(ext, forward):
    n = ext.shape[0]
    outs = []
    for g in range(POOL_GROUPS):
        s = ext[:, g * POOL_GROUP_DIM:(g + 1) * POOL_GROUP_DIM]
        for k in range(g + 1):
            shift = (1 << k) if forward else n - (1 << k)
            s = s + pltpu.roll(s, shift, axis=0)
        outs.append(s[:n - POOL_HALO])
    return outs


def _inv_count(row0, tm):
    row = row0 + lax.broadcasted_iota(jnp.int32, (tm, 1), 0)
    return [1.0 / jnp.minimum(row + 1, 2 << g).astype(F32) for g in range(POOL_GROUPS)]


def _pool_mix(u, u_prev, row0, gw_ref, gb):
    tm = u.shape[0]
    sums = _window_sums(jnp.concatenate([u, u_prev], axis=0), True)
    inv = _inv_count(row0, tm)
    pooled, mixed = [], []
    for g in range(POOL_GROUPS):
        ug = u[:, g * POOL_GROUP_DIM:(g + 1) * POOL_GROUP_DIM]
        pg = (sums[g] * inv[g] - ug).astype(BF16)
        pooled.append(pg)
        mixed.append(_dot_nn(pg, gw_ref[g]))
    return pooled, jnp.concatenate(mixed, axis=1) + gb


def _pool_fwd_call(x, nw, w_in, gw, gb, sc, w_out, rider=None):
    seq = x.shape[0]
    tm = ROW_TILE
    nt = seq // tm

    def main(x_ref, nw_ref, win_ref, gw_ref, gb_ref, sc_ref, wout_ref, h_ref, p_ref, halo_ref):
        i = pl.program_id(0)

        @pl.when(i == 0)
        def _():
            halo_ref[...] = jnp.zeros_like(halo_ref)

        xt = x_ref[...]
        xhat, _ = _rms(xt)
        n = (xhat * nw_ref[...]).astype(BF16)
        p = _dot_nn(n, win_ref[...])
        p_ref[...] = p
        u = p[:, :POOL_WIDTH]
        gate = p[:, POOL_WIDTH:]
        _, mixed = _pool_mix(u, halo_ref[...], i * tm, gw_ref, gb_ref[...])
        halo_ref[...] = u[tm - POOL_HALO:, :]
        y = (mixed * sc_ref[...] * (gate * _sigmoid(gate))).astype(BF16)
        h_ref[...] = xt + _dot_nn(y, wout_ref[...])

    def body(*refs):
        own, comm = _split_refs(refs, 7, 2, 1, rider)
        _ride_before(comm, pl.program_id(0), nt)
        main(*own)
        _ride_after(comm, pl.program_id(0), nt)

    return pl.pallas_call(
        body, name="pool_fwd", grid=(nt,),
        in_specs=_extend([pl.BlockSpec((tm, D_MODEL), lambda i: (i, 0)), _full((1, D_MODEL)),
                          _full((D_MODEL, 2 * POOL_WIDTH)), _full((POOL_GROUPS, POOL_GROUP_DIM, POOL_GROUP_DIM)),
                          _full((1, POOL_WIDTH)), _full((1, POOL_WIDTH)), _full((POOL_WIDTH, D_MODEL))],
                         rider, "in_specs"),
        out_specs=_extend([pl.BlockSpec((tm, D_MODEL), lambda i: (i, 0)),
                           pl.BlockSpec((tm, 2 * POOL_WIDTH), lambda i: (i, 0))], rider, "out_specs"),
        out_shape=_extend([jax.ShapeDtypeStruct((seq, D_MODEL), F32),
                           jax.ShapeDtypeStruct((seq, 2 * POOL_WIDTH), F32)], rider, "out_shape"),
        scratch_shapes=_extend([pltpu.VMEM((POOL_HALO, POOL_WIDTH), F32)], rider, "scratch"),
        compiler_params=_params(),
    )(x, nw, w_in, gw, gb, sc, w_out, *_extend([], rider, "arrays"))


def _pool_bwd_call(dh, p, gw, gb, sc, w_out, rider=None):
    seq = dh.shape[0]
    tm = ROW_TILE
    nt = seq // tm
    halo_blocks = tm // POOL_HALO

    def main(dh_ref, p_ref, pprev_ref, gw_ref, gb_ref, sc_ref, wout_ref,
             dp_ref, dwout_hbm, dgw_hbm, dgb_ref, dsc_ref, carry_ref, dwout_acc, dgw_acc, dwout_stage, dgw_stage):
        i = pl.program_id(0)
        t = nt - 1 - i

        @pl.when(i == 0)
        def _():
            carry_ref[...] = jnp.zeros_like(carry_ref)
            dwout_acc[...] = jnp.zeros_like(dwout_acc)
            dgw_acc[...] = jnp.zeros_like(dgw_acc)
            dgb_ref[...] = jnp.zeros_like(dgb_ref)
            dsc_ref[...] = jnp.zeros_like(dsc_ref)

        p = p_ref[...]
        u = p[:, :POOL_WIDTH]
        gate = p[:, POOL_WIDTH:]
        u_prev = jnp.where(t > 0, pprev_ref[:, :POOL_WIDTH], 0.0)
        pooled, mixed = _pool_mix(u, u_prev, t * tm, gw_ref, gb_ref[...])
        sg = _sigmoid(gate)
        silu = gate * sg
        sc = sc_ref[...]
        dhb = dh_ref[...].astype(BF16)
        y = (mixed * sc * silu).astype(BF16)
        dwout_acc[...] += _dot_tn(y, dhb)
        dy = _dot_nt(dhb, wout_ref[...])
        dmixed = dy * sc * silu
        dsc_ref[...] += jnp.sum(dy * mixed * silu, axis=0, keepdims=True)
        dgate = dy * mixed * sc * (sg * (1.0 + gate * (1.0 - sg)))
        dgb_ref[...] += jnp.sum(dmixed, axis=0, keepdims=True)
        inv = _inv_count(t * tm, tm)
        dpooled, scaled = [], []
        for g in range(POOL_GROUPS):
            dmg = dmixed[:, g * POOL_GROUP_DIM:(g + 1) * POOL_GROUP_DIM].astype(BF16)
            dgw_acc[g] += _dot_tn(pooled[g], dmg)
            dpg = _dot_nt(dmg, gw_ref[g])
            dpooled.append(dpg)
            scaled.append(dpg * inv[g])
        r = jnp.concatenate(scaled, axis=1)
        sums = _window_sums(jnp.concatenate([r, carry_ref[...]], axis=0), False)
        carry_ref[...] = r[:POOL_HALO, :]
        du = jnp.concatenate([sums[g] - dpooled[g] for g in range(POOL_GROUPS)], axis=1)
        dp_ref[...] = jnp.concatenate([du, dgate], axis=1).astype(BF16)

        @pl.when(i == nt - 1)
        def _():
            dwout_stage[...] = dwout_acc[...].astype(BF16)
            dgw_stage[...] = dgw_acc[...].astype(BF16)
            pltpu.sync_copy(dwout_stage, dwout_hbm)
            pltpu.sync_copy(dgw_stage, dgw_hbm)

    def body(*refs):
        own, comm = _split_refs(refs, 7, 5, 5, rider)
        _ride_before(comm, pl.program_id(0), nt)
        main(*own)
        _ride_after(comm, pl.program_id(0), nt)

    rev = lambda i: (nt - 1 - i, 0)
    return pl.pallas_call(
        body, name="pool_bwd", grid=(nt,),
        in_specs=_extend([pl.BlockSpec((tm, D_MODEL), rev), pl.BlockSpec((tm, 2 * POOL_WIDTH), rev),
                          pl.BlockSpec((POOL_HALO, 2 * POOL_WIDTH),
                                       lambda i: (jnp.maximum((nt - 1 - i) * halo_blocks - 1, 0), 0)),
                          _full((POOL_GROUPS, POOL_GROUP_DIM, POOL_GROUP_DIM)), _full((1, POOL_WIDTH)),
                          _full((1, POOL_WIDTH)), _full((POOL_WIDTH, D_MODEL))], rider, "in_specs"),
        out_specs=_extend([pl.BlockSpec((tm, 2 * POOL_WIDTH), rev), pl.BlockSpec(memory_space=pl.ANY),
                           pl.BlockSpec(memory_space=pl.ANY), _full((1, POOL_WIDTH)), _full((1, POOL_WIDTH))],
                          rider, "out_specs"),
        out_shape=_extend([jax.ShapeDtypeStruct((seq, 2 * POOL_WIDTH), BF16),
                           jax.ShapeDtypeStruct((POOL_WIDTH, D_MODEL), BF16),
                           jax.ShapeDtypeStruct((POOL_GROUPS, POOL_GROUP_DIM, POOL_GROUP_DIM), BF16),
                           jax.ShapeDtypeStruct((1, POOL_WIDTH), F32), jax.ShapeDtypeStruct((1, POOL_WIDTH), F32)],
                          rider, "out_shape"),
        scratch_shapes=_extend([pltpu.VMEM((POOL_HALO, POOL_WIDTH), F32), pltpu.VMEM((POOL_WIDTH, D_MODEL), F32),
                                pltpu.VMEM((POOL_GROUPS, POOL_GROUP_DIM, POOL_GROUP_DIM), F32),
                                pltpu.VMEM((POOL_WIDTH, D_MODEL), BF16),
                                pltpu.VMEM((POOL_GROUPS, POOL_GROUP_DIM, POOL_GROUP_DIM), BF16)], rider, "scratch"),
        compiler_params=_params(),
    )(dh, p, p, gw, gb, sc, w_out, *_extend([], rider, "arrays"))


def _inproj_bwd_call(name, dproj, h_in, nw, w_in, dres, rider=None, transposed=False):
    seq = h_in.shape[0]
    width = dproj.shape[1]
    w_shape = tuple(w_in.shape)
    tm = ROW_TILE
    nt = seq // tm

    def main(dproj_ref, h_ref, nw_ref, win_ref, dres_ref, dh_ref, dw_hbm, dnw_ref, dw_acc, dw_stage):
        i = pl.program_id(0)

        @pl.when(i == 0)
        def _():
            dw_acc[...] = jnp.zeros_like(dw_acc)
            dnw_ref[...] = jnp.zeros_like(dnw_ref)

        xhat, rstd = _rms(h_ref[...])
        nw_row = nw_ref[...]
        n = (xhat * nw_row).astype(BF16)
        dpb = dproj_ref[...]
        if transposed:
            dw_acc[...] += _dot_tn(dpb, n)
            dn = _dot_nn(dpb, win_ref[...])
        else:
            dw_acc[...] += _dot_tn(n, dpb)
            dn = _dot_nt(dpb, win_ref[...])
        dnw_ref[...] += jnp.sum(dn * xhat, axis=0, keepdims=True)
        dh_ref[...] = _rms_bwd(dn * nw_row, xhat, rstd) + dres_ref[...]

        @pl.when(i == nt - 1)
        def _():
            dw_stage[...] = dw_acc[...].astype(BF16)
            pltpu.sync_copy(dw_stage, dw_hbm)

    def body(*refs):
        own, comm = _split_refs(refs, 5, 3, 2, rider)
        _ride_before(comm, pl.program_id(0), nt)
        main(*own)
        _ride_after(comm, pl.program_id(0), nt)

    row = lambda i: (i, 0)
    return pl.pallas_call(
        body, name=name, grid=(nt,),
        in_specs=_extend([pl.BlockSpec((tm, width), row), pl.BlockSpec((tm, D_MODEL), row), _full((1, D_MODEL)),
                          _full(w_shape), pl.BlockSpec((tm, D_MODEL), row)], rider, "in_specs"),
        out_specs=_extend([pl.BlockSpec((tm, D_MODEL), row), pl.BlockSpec(memory_space=pl.ANY),
                           _full((1, D_MODEL))], rider, "out_specs"),
        out_shape=_extend([jax.ShapeDtypeStruct((seq, D_MODEL), F32), jax.ShapeDtypeStruct(w_shape, BF16),
                           jax.ShapeDtypeStruct((1, D_MODEL), F32)], rider, "out_shape"),
        scratch_shapes=_extend([pltpu.VMEM(w_shape, F32), pltpu.VMEM(w_shape, BF16)], rider, "scratch"),
        compiler_params=_params(),
    )(dproj, h_in, nw, w_in, dres, *_extend([], rider, "arrays"))


def _chunk_scan(x, reverse):
    n = x.shape[0]
    pos = lax.broadcasted_iota(jnp.int32, (n, 1), 0) & (CHUNK - 1)
    k = 1
    while k < CHUNK:
        if reverse:
            x = x + jnp.where(pos < CHUNK - k, pltpu.roll(x, n - k, axis=0), 0.0)
        else:
            x = x + jnp.where(pos >= k, pltpu.roll(x, k, axis=0), 0.0)
        k *= 2
    return x


class _GlaTile:
    def __init__(self, proj, gkw_ref, gkb):
        tm = proj.shape[0]
        self.q = proj[:, :GLA_KEY_WIDTH] * Q_SCALE
        self.k = proj[:, GLA_KEY_WIDTH:2 * GLA_KEY_WIDTH]
        self.v = proj[:, 2 * GLA_KEY_WIDTH:2 * GLA_KEY_WIDTH + GLA_VALUE_WIDTH]
        self.gate = proj[:, 2 * GLA_KEY_WIDTH + GLA_VALUE_WIDTH:2 * GLA_KEY_WIDTH + 2 * GLA_VALUE_WIDTH]
        self.low_b = proj[:, 2 * GLA_KEY_WIDTH + 2 * GLA_VALUE_WIDTH:].astype(BF16)
        self.z = _dot_nn(self.low_b, gkw_ref[...]) + gkb
        log_g = (jnp.minimum(self.z, 0.0) - jnp.log(1.0 + jnp.exp(-jnp.abs(self.z)))) / GATE_NORMALIZER
        self.c = _chunk_scan(log_g, False)
        is_last = lax.broadcasted_iota(jnp.int32, (CHUNK, 1), 0) == CHUNK - 1
        last = [jnp.sum(jnp.where(is_last, self.c[j * CHUNK:(j + 1) * CHUNK, :], 0.0), axis=0, keepdims=True)
                for j in range(tm // CHUNK)]
        self.c_last = last
        c_last_rows = jnp.concatenate([jnp.broadcast_to(r, (CHUNK, GLA_KEY_WIDTH)) for r in last], axis=0)
        self.e_pos = jnp.exp(self.c)
        self.e_neg = jnp.exp(-self.c)
        self.e_rest = jnp.exp(c_last_rows - self.c)
        self.a = self.q * self.e_pos
        self.b = self.k * self.e_neg
        self.cn = self.q * self.e_neg
        self.dp = self.k * self.e_pos
        self.kd = self.k * self.e_rest
        self.a_b, self.b_b, self.cn_b, self.dp_b, self.kd_b, self.v_b = (
            t.astype(BF16) for t in (self.a, self.b, self.cn, self.dp, self.kd, self.v))
        idx_t = lax.broadcasted_iota(jnp.int32, (CHUNK, CHUNK), 0)
        idx_s = lax.broadcasted_iota(jnp.int32, (CHUNK, CHUNK), 1)
        self.lower = idx_t >= idx_s

    @staticmethod
    def rows(j):
        return slice(j * CHUNK, (j + 1) * CHUNK)

    @staticmethod
    def kcols(h):
        return slice(h * GLA_HEAD_K, (h + 1) * GLA_HEAD_K)

    @staticmethod
    def vcols(h):
        return slice(h * GLA_HEAD_V, (h + 1) * GLA_HEAD_V)

    def scores(self, j, h):
        r, kc = self.rows(j), self.kcols(h)
        fwd = _dot_nt(self.a_b[r, kc], self.b_b[r, kc])
        bwd = _dot_nt(self.cn_b[r, kc], self.dp_b[r, kc])
        return jnp.where(self.lower, fwd, bwd).astype(BF16)


def _gla_fwd_call(h1, nw, w_in, gkw, gkb, hw, w_out, wf, target):
    seq = h1.shape[0]
    tm = ROW_TILE
    nt = seq // tm
    cpt = tm // CHUNK
    n_chunks = seq // CHUNK

    def body(h_ref, nw_ref, win_ref, gkw_ref, gkb_ref, hw_ref, wout_ref, wf_ref, tgt_ref,
             dh2_ref, proj_ref, o_ref, st_ref, loss_ref, dwf_ref, state_ref):
        i = pl.program_id(0)

        @pl.when(i == 0)
        def _():
            state_ref[...] = jnp.zeros_like(state_ref)
            loss_ref[...] = jnp.zeros_like(loss_ref)
            dwf_ref[...] = jnp.zeros_like(dwf_ref)

        ht = h_ref[...]
        xhat, _ = _rms(ht)
        n = (xhat * nw_ref[...]).astype(BF16)
        proj = _dot_nt(n, win_ref[...])
        proj_ref[...] = proj
        g = _GlaTile(proj, gkw_ref, gkb_ref[...])
        o_rows = []
        for j in range(cpt):
            r = g.rows(j)
            o_heads = []
            for h in range(GLA_HEADS):
                kc, vc = g.kcols(h), g.vcols(h)
                srows = slice(h * GLA_HEAD_V, (h + 1) * GLA_HEAD_V)
                state = state_ref[srows, :]
                st_ref[j, srows, :] = state
                o_heads.append(_dot_nn(g.scores(j, h), g.v_b[r, vc]) + _dot_nt(g.a_b[r, kc], state.astype(BF16)))
                decay = jnp.exp(g.c_last[j][:, kc])
                state_ref[srows, :] = state * decay + _dot_tn(g.v_b[r, vc], g.kd_b[r, kc])
            o_rows.append(jnp.concatenate(o_heads, axis=1))
        o = jnp.concatenate(o_rows, axis=0)
        o_ref[...] = o
        hw_row = hw_ref[...]
        on = jnp.concatenate([_rms(o[:, g.vcols(h)])[0] for h in range(GLA_HEADS)], axis=1) * hw_row
        y = (on * (g.gate * _sigmoid(g.gate))).astype(BF16)
        h2 = ht + _dot_nn(y, wout_ref[...])
        xhat2, rstd2 = _rms(h2)
        wf_row = wf_ref[...]
        err = xhat2 * wf_row - tgt_ref[...]
        loss_ref[...] += 0.5 * jnp.sum(err * err) / D_MODEL
        dout = err * (1.0 / D_MODEL)
        dwf_ref[...] += jnp.sum(dout * xhat2, axis=0, keepdims=True)
        dh2_ref[...] = _rms_bwd(dout * wf_row, xhat2, rstd2)

    row = lambda i: (i, 0)
    return pl.pallas_call(
        body, name="gla_fwd", grid=(nt,),
        in_specs=[pl.BlockSpec((tm, D_MODEL), row), _full((1, D_MODEL)), _full((GLA_IN_PAD, D_MODEL)),
                  _full((GLA_LOW_PAD, GLA_KEY_WIDTH)), _full((1, GLA_KEY_WIDTH)), _full((1, GLA_VALUE_WIDTH)),
                  _full((GLA_VALUE_WIDTH, D_MODEL)), _full((1, D_MODEL)), pl.BlockSpec((tm, D_MODEL), row)],
        out_specs=[pl.BlockSpec((tm, D_MODEL), row), pl.BlockSpec((tm, GLA_IN_PAD), row),
                   pl.BlockSpec((tm, GLA_VALUE_WIDTH), row),
                   pl.BlockSpec((cpt, GLA_VALUE_WIDTH, GLA_HEAD_K), lambda i: (i, 0, 0)),
                   _full((8, LANES)), _full((1, D_MODEL))],
        out_shape=[jax.ShapeDtypeStruct((seq, D_MODEL), F32), jax.ShapeDtypeStruct((seq, GLA_IN_PAD), F32),
                   jax.ShapeDtypeStruct((seq, GLA_VALUE_WIDTH), F32),
                   jax.ShapeDtypeStruct((n_chunks, GLA_VALUE_WIDTH, GLA_HEAD_K), F32),
                   jax.ShapeDtypeStruct((8, LANES), F32), jax.ShapeDtypeStruct((1, D_MODEL), F32)],
        scratch_shapes=[pltpu.VMEM((GLA_VALUE_WIDTH, GLA_HEAD_K), F32)],
        compiler_params=_params(),
    )(h1, nw, w_in, gkw, gkb, hw, w_out, wf, target)


def _gla_bwd_call(dh2, proj, o, states, gkw, gkb, hw, w_out):
    seq = dh2.shape[0]
    tm = ROW_TILE
    nt = seq // tm
    cpt = tm // CHUNK

    def body(dh_ref, proj_ref, o_ref, st_ref, gkw_ref, gkb_ref, hw_ref, wout_ref,
             dproj_ref, dwout_hbm, dhw_ref, dgkw_ref, dgkb_ref, dstate_ref, dwout_acc, dwout_stage):
        i = pl.program_id(0)

        @pl.when(i == 0)
        def _():
            dstate_ref[...] = jnp.zeros_like(dstate_ref)
            dwout_acc[...] = jnp.zeros_like(dwout_acc)
            dhw_ref[...] = jnp.zeros_like(dhw_ref)
            dgkw_ref[...] = jnp.zeros_like(dgkw_ref)
            dgkb_ref[...] = jnp.zeros_like(dgkb_ref)

        g = _GlaTile(proj_ref[...], gkw_ref, gkb_ref[...])
        dhb = dh_ref[...].astype(BF16)
        o = o_ref[...]
        hw_row = hw_ref[...]
        dy = _dot_nt(dhb, wout_ref[...])
        sg = _sigmoid(g.gate)
        silu = g.gate * sg
        don = dy * silu
        on_parts, do_parts, dhw_parts = [], [], []
        for h in range(GLA_HEADS):
            vc = g.vcols(h)
            xh, rs = _rms(o[:, vc])
            on_parts.append(xh * hw_row[:, vc])
            dhw_parts.append(jnp.sum(don[:, vc] * xh, axis=0, keepdims=True))
            do_parts.append(_rms_bwd(don[:, vc] * hw_row[:, vc], xh, rs))
        on = jnp.concatenate(on_parts, axis=1)
        dwout_acc[...] += _dot_tn((on * silu).astype(BF16), dhb)
        dhw_ref[...] += jnp.concatenate(dhw_parts, axis=1)
        dgate = dy * on * (sg * (1.0 + g.gate * (1.0 - sg)))
        do_b = jnp.concatenate(do_parts, axis=1).astype(BF16)

        last_row = lax.broadcasted_iota(jnp.int32, (CHUNK, 1), 0) == CHUNK - 1
        dq_rows, dk_rows, dv_rows, dc_rows = [None] * cpt, [None] * cpt, [None] * cpt, [None] * cpt
        for j in reversed(range(cpt)):
            r = g.rows(j)
            dq_h, dk_h, dv_h, dc_h = [], [], [], []
            for h in range(GLA_HEADS):
                kc, vc = g.kcols(h), g.vcols(h)
                srows = slice(h * GLA_HEAD_V, (h + 1) * GLA_HEAD_V)
                state = st_ref[j, srows, :]
                dstate = dstate_ref[srows, :]
                dstate_b = dstate.astype(BF16)
                do_c = do_b[r, vc]
                scores = g.scores(j, h)
                dscores = _dot_nt(do_c, g.v_b[r, vc])
                dfwd = jnp.where(g.lower, dscores, 0.0).astype(BF16)
                dbwd = jnp.where(g.lower, 0.0, dscores).astype(BF16)
                dv_h.append(_dot_tn(scores, do_c) + _dot_nt(g.kd_b[r, kc], dstate_b))
                da = _dot_nn(dfwd, g.b_b[r, kc]) + _dot_nn(do_c, state.astype(BF16))
                db = _dot_tn(dfwd, g.a_b[r, kc])
                dcn = _dot_nn(dbwd, g.dp_b[r, kc])
                ddp = _dot_tn(dbwd, g.cn_b[r, kc])
                dkd = _dot_nn(g.v_b[r, vc], dstate_b)
                decay = jnp.exp(g.c_last[j][:, kc])
                dstate_ref[srows, :] = _dot_tn(do_c, g.a_b[r, kc]) + dstate * decay
                kd_c = g.kd[r, kc]
                dkd_kd = dkd * kd_c
                dc_last = (jnp.sum(dkd_kd, axis=0, keepdims=True)
                           + decay * jnp.sum(state * dstate, axis=0, keepdims=True))
                dq_h.append(Q_SCALE * (da * g.e_pos[r, kc] + dcn * g.e_neg[r, kc]))
                dk_h.append(db * g.e_neg[r, kc] + ddp * g.e_pos[r, kc] + dkd * g.e_rest[r, kc])
                dc = (da * g.a[r, kc] - db * g.b[r, kc] - dcn * g.cn[r, kc] + ddp * g.dp[r, kc] - dkd_kd)
                dc_h.append(dc + jnp.where(last_row, dc_last, 0.0))
            dq_rows[j] = jnp.concatenate(dq_h, axis=1)
            dk_rows[j] = jnp.concatenate(dk_h, axis=1)
            dv_rows[j] = jnp.concatenate(dv_h, axis=1)
            dc_rows[j] = jnp.concatenate(dc_h, axis=1)
        dq = jnp.concatenate(dq_rows, axis=0)
        dk = jnp.concatenate(dk_rows, axis=0)
        dv = jnp.concatenate(dv_rows, axis=0)
        dlog_g = _chunk_scan(jnp.concatenate(dc_rows, axis=0), True)
        dz = dlog_g * (1.0 / GATE_NORMALIZER) * (1.0 - _sigmoid(g.z))
        dzb = dz.astype(BF16)
        dgkb_ref[...] += jnp.sum(dz, axis=0, keepdims=True)
        dgkw_ref[...] += _dot_tn(g.low_b, dzb)
        dlow = _dot_nt(dzb, gkw_ref[...])
        dproj_ref[...] = jnp.concatenate([dq, dk, dv, dgate, dlow], axis=1).astype(BF16)

        @pl.when(i == nt - 1)
        def _():
            dwout_stage[...] = dwout_acc[...].astype(BF16)
            pltpu.sync_copy(dwout_stage, dwout_hbm)

    rev = lambda i: (nt - 1 - i, 0)
    return pl.pallas_call(
        body, name="gla_bwd", grid=(nt,),
        in_specs=[pl.BlockSpec((tm, D_MODEL), rev), pl.BlockSpec((tm, GLA_IN_PAD), rev),
                  pl.BlockSpec((tm, GLA_VALUE_WIDTH), rev),
                  pl.BlockSpec((cpt, GLA_VALUE_WIDTH, GLA_HEAD_K), lambda i: (nt - 1 - i, 0, 0)),
                  _full((GLA_LOW_PAD, GLA_KEY_WIDTH)), _full((1, GLA_KEY_WIDTH)), _full((1, GLA_VALUE_WIDTH)),
                  _full((GLA_VALUE_WIDTH, D_MODEL))],
        out_specs=[pl.BlockSpec((tm, GLA_IN_PAD), rev), pl.BlockSpec(memory_space=pl.ANY),
                   _full((1, GLA_VALUE_WIDTH)), _full((GLA_LOW_PAD, GLA_KEY_WIDTH)), _full((1, GLA_KEY_WIDTH))],
        out_shape=[jax.ShapeDtypeStruct((seq, GLA_IN_PAD), BF16), jax.ShapeDtypeStruct((GLA_VALUE_WIDTH, D_MODEL), BF16),
                   jax.ShapeDtypeStruct((1, GLA_VALUE_WIDTH), F32), jax.ShapeDtypeStruct((GLA_LOW_PAD, GLA_KEY_WIDTH), F32),
                   jax.ShapeDtypeStruct((1, GLA_KEY_WIDTH), F32)],
        scratch_shapes=[pltpu.VMEM((GLA_VALUE_WIDTH, GLA_HEAD_K), F32), pltpu.VMEM((GLA_VALUE_WIDTH, D_MODEL), F32),
                        pltpu.VMEM((GLA_VALUE_WIDTH, D_MODEL), BF16)],
        compiler_params=_params(),
    )(dh2, proj, o, states, gkw, gkb, hw, w_out)


def _position():
    return lax.axis_index("x"), lax.axis_index("y"), lax.axis_index("c")


def _lead_slot(ref, d):
    return ref.at[d]


def _row_slot(rows):
    return lambda ref, d: ref.at[pl.ds(pl.multiple_of(d * rows, rows), rows)]


def _dim1_slot(size):
    return lambda ref, d: ref.at[:, pl.ds(pl.multiple_of(d * size, size), size)]


class _Gather:
    def __init__(self, in_refs, out_refs, slots, send_sems, recv_sems, local_sems):
        self.in_refs, self.out_refs, self.slots = in_refs, out_refs, slots
        self.send_sems, self.recv_sems, self.local_sems = send_sems, recv_sems, local_sems
        self.n = len(in_refs)
        x, y, c = _position()
        self.c = c
        self.me, self.sibling = (x, y, c), (x, y, 1 - c)
        self.chips = [(1 - x, y), (x, 1 - y), (1 - x, 1 - y)]

    def _copy(self, a, k, block, to, from_input=False):
        part = self.slots[a](self.out_refs[a], 4 * block[0] + 2 * block[1] + block[2])
        return pltpu.make_async_remote_copy(
            src_ref=self.in_refs[a] if from_input else part, dst_ref=part,
            send_sem=self.send_sems.at[a, k], recv_sem=self.recv_sems.at[a, k], device_id=to, device_id_type=MESH)

    def _mine(self):
        return [pltpu.make_async_copy(self.in_refs[a], self.slots[a](self.out_refs[a], 4 * self.me[0] + 2 * self.me[1]
                                                                    + self.me[2]), self.local_sems.at[a])
                for a in range(self.n)]

    def _first(self):
        first = [self._copy(a, 0, self.me, self.sibling, True) for a in range(self.n)]
        return first + [self._copy(a, 1 + j, self.me, (*chip, self.c), True)
                        for j, chip in enumerate(self.chips) for a in range(self.n)]

    def _passed(self):
        return [self._copy(a, 4 + j, (*chip, self.c), self.sibling)
                for j, chip in enumerate(self.chips) for a in range(self.n)]

    def start(self):
        for cp in self._mine() + self._first():
            cp.start()

    def forward(self):
        passed = self._passed()
        for j, chip in enumerate(self.chips):
            for a in range(self.n):
                self._copy(a, 1 + j, (*chip, self.c), self.me).wait_recv()
                passed[j * self.n + a].start()

    def finish(self):
        for a in range(self.n):
            self._copy(a, 0, self.sibling, self.me).wait_recv()
        for j, chip in enumerate(self.chips):
            for a in range(self.n):
                self._copy(a, 4 + j, (*chip, 1 - self.c), self.me).wait_recv()
        for cp in self._first() + self._passed():
            cp.wait_send()
        for cp in self._mine():
            cp.wait()


class _Exchange:
    def __init__(self, in_refs, out_refs, slots, send_sems, recv_sems, local_sems):
        self.in_refs, self.out_refs, self.slots = in_refs, out_refs, slots
        self.send_sems, self.recv_sems, self.local_sems = send_sems, recv_sems, local_sems
        self.n = len(in_refs)
        self.pos = _position()

    def _copies(self):
        x, y, c = self.pos
        me = 4 * x + 2 * y + c
        mine = [pltpu.make_async_copy(self.slots[a](self.in_refs[a], me), self.out_refs[a].at[me],
                                      self.local_sems.at[a]) for a in range(self.n)]
        remote = []
        for k in range(1, N_DEV):
            px, py, pc = x ^ (k >> 2), y ^ ((k >> 1) & 1), c ^ (k & 1)
            for a in range(self.n):
                remote.append(pltpu.make_async_remote_copy(
                    src_ref=self.slots[a](self.in_refs[a], 4 * px + 2 * py + pc), dst_ref=self.out_refs[a].at[me],
                    send_sem=self.send_sems.at[a, k - 1], recv_sem=self.recv_sems.at[a, k - 1],
                    device_id=(px, py, pc), device_id_type=MESH))
        return mine, remote

    def start(self):
        mine, remote = self._copies()
        for cp in mine + remote:
            cp.start()

    def forward(self):
        pass

    def finish(self):
        mine, remote = self._copies()
        for cp in remote:
            cp.wait_recv()
        for cp in remote:
            cp.wait_send()
        for cp in mine:
            cp.wait()


class _Rider:
    def __init__(self, kind, arrays, out_shapes, slots):
        self.kind, self.arrays, self.slots = kind, list(arrays), slots
        self.n = len(self.arrays)
        hbm = pl.BlockSpec(memory_space=pl.ANY)
        self.in_specs = [hbm] * self.n
        self.out_specs = [hbm] * self.n
        self.out_shape = [jax.ShapeDtypeStruct(tuple(s), a.dtype) for s, a in zip(out_shapes, self.arrays)]
        self.scratch = [pltpu.SemaphoreType.DMA((self.n, 7)), pltpu.SemaphoreType.DMA((self.n, 7)),
                        pltpu.SemaphoreType.DMA((self.n,))]

    def bind(self, in_refs, out_refs, sems):
        return self.kind(in_refs, out_refs, self.slots, *sems)


def _gather_rider(shards, full_shapes, slots):
    return _Rider(_Gather, shards, full_shapes, slots)


def _exchange_rider(sends, part_shapes, slots):
    return _Rider(_Exchange, sends, [(N_DEV,) + tuple(s) for s in part_shapes], slots)


def _split_refs(refs, n_in, n_out, n_scratch, rider):
    k = rider.n if rider is not None else 0
    ins, r_ins = refs[:n_in], refs[n_in:n_in + k]
    outs, r_outs = refs[n_in + k:n_in + k + n_out], refs[n_in + k + n_out:n_in + 2 * k + n_out]
    rest = refs[n_in + 2 * k + n_out:]
    scratch, sems = rest[:n_scratch], rest[n_scratch:]
    comm = rider.bind(r_ins, r_outs, sems) if rider is not None else None
    return ins + outs + scratch, comm


def _ride_before(comm, i, nt):
    if comm is not None:
        pl.when(i == 0)(comm.start)
        pl.when(i == nt - 1)(comm.forward)


def _ride_after(comm, i, nt):
    if comm is not None:
        pl.when(i == nt - 1)(comm.finish)


def _extend(specs, rider, field):
    return list(specs) + (getattr(rider, field) if rider is not None else [])


def _comm_call(name, rider):
    def body(*refs):
        _, comm = _split_refs(refs, 0, 0, 0, rider)
        comm.start()
        comm.forward()
        comm.finish()

    return pl.pallas_call(body, name=name, in_specs=rider.in_specs, out_specs=rider.out_specs,
                          out_shape=rider.out_shape, scratch_shapes=rider.scratch)(*rider.arrays)


def _adamw(w, g, m, v):
    m = ADAM_B1 * m + (1.0 - ADAM_B1) * g
    v = ADAM_B2 * v + (1.0 - ADAM_B2) * (g * g)
    m_hat = m / (1.0 - ADAM_B1 ** ADAM_STEP)
    v_hat = v / (1.0 - ADAM_B2 ** ADAM_STEP)
    delta = -ADAM_LR * (m_hat / (jnp.sqrt(v_hat) + ADAM_EPS) + ADAM_WD * w)
    return delta, m, v


def _sum_parts(parts_ref, index=()):
    g = parts_ref[(0,) + index].astype(F32)
    for s in range(1, N_DEV):
        g = g + parts_ref[(s,) + index].astype(F32)
    return g


def _adamw_call(name, parts, w, m, v, block_rows):
    rows, cols = w.shape
    nb = rows // block_rows

    def body(parts_ref, w_ref, m_ref, v_ref, g_ref, delta_ref, m_out, v_out):
        g = _sum_parts(parts_ref)
        delta, m_new, v_new = _adamw(w_ref[...], g, m_ref[...], v_ref[...])
        g_ref[...] = g
        delta_ref[...] = delta
        m_out[...] = m_new
        v_out[...] = v_new

    blk = pl.BlockSpec((block_rows, cols), lambda i: (i, 0))
    return pl.pallas_call(
        body, name=name, grid=(nb,),
        in_specs=[pl.BlockSpec((N_DEV, block_rows, cols), lambda i: (0, i, 0)), blk, blk, blk],
        out_specs=[blk, blk, blk, blk],
        out_shape=[jax.ShapeDtypeStruct((rows, cols), F32)] * 4,
        compiler_params=_params(("parallel",)),
    )(parts, w, m, v)


def _adamw_cols_call(name, parts_a, parts_b, w, m, v, block_cols):
    rows, cols = w.shape
    na, nb = parts_a.shape[2] // block_cols, parts_b.shape[2] // block_cols

    def body(pa_ref, pb_ref, w_ref, m_ref, v_ref, g_ref, delta_ref, m_out, v_out):
        def update(parts_ref):
            g = _sum_parts(parts_ref)
            delta, m_new, v_new = _adamw(w_ref[...], g, m_ref[...], v_ref[...])
            g_ref[...] = g
            delta_ref[...] = delta
            m_out[...] = m_new
            v_out[...] = v_new

        pl.when(pl.program_id(0) < na)(functools.partial(update, pa_ref))
        pl.when(pl.program_id(0) >= na)(functools.partial(update, pb_ref))

    blk = pl.BlockSpec((rows, block_cols), lambda i: (0, i))
    return pl.pallas_call(
        body, name=name, grid=(na + nb,),
        in_specs=[pl.BlockSpec((N_DEV, rows, block_cols), lambda i: (0, 0, jnp.minimum(i, na - 1))),
                  pl.BlockSpec((N_DEV, rows, block_cols), lambda i: (0, 0, jnp.maximum(i - na, 0))), blk, blk, blk],
        out_specs=[blk, blk, blk, blk],
        out_shape=[jax.ShapeDtypeStruct((rows, cols), F32)] * 4,
        compiler_params=_params(("arbitrary",)),
    )(parts_a, parts_b, w, m, v)


WIDE_ROWS = 8
NARROW_ROWS = 40
NARROW_GKW_ROW = 8
NARROW_GKB_ROW = 24
NARROW_HW_ROW = 32
GROUP_SHARD = POOL_GROUP_DIM // N_DEV
KEY_SHARD = GLA_KEY_WIDTH // N_DEV
HEAD_V_SHARD = GLA_HEAD_V // N_DEV


def _small_adamw_call(wide, narrow, w, m, v):
    names = ("norm_w", "pool_scale", "final_norm_w", "pool_group_b", "gla_gk_w", "gla_gk_b", "gla_head_norm_w")
    where = {
        "norm_w": (0, slice(0, 2), slice(None)),
        "pool_scale": (0, slice(2, 3), slice(None)),
        "final_norm_w": (0, slice(3, 4), slice(None)),
        "pool_group_b": (1, slice(0, POOL_GROUPS), slice(0, GROUP_SHARD)),
        "gla_gk_w": (1, slice(NARROW_GKW_ROW, NARROW_GKW_ROW + GLA_GATE_RANK), slice(0, KEY_SHARD)),
        "gla_gk_b": (1, slice(NARROW_GKB_ROW, NARROW_GKB_ROW + 1), slice(0, KEY_SHARD)),
        "gla_head_norm_w": (1, slice(NARROW_HW_ROW, NARROW_HW_ROW + 1), slice(0, HEAD_V_SHARD)),
    }
    k = len(names)

    def body(*refs):
        parts = refs[0:2]
        w_refs, m_refs, v_refs = refs[2:2 + k], refs[2 + k:2 + 2 * k], refs[2 + 2 * k:2 + 3 * k]
        outs = refs[2 + 3 * k:]
        loss_ref = outs[0]
        loss_ref[...] = _sum_parts(parts[0], (slice(4, 5), slice(0, 1)))
        for i, name in enumerate(names):
            buf, rows, cols = where[name]
            g = _sum_parts(parts[buf], (rows, cols))
            delta, m_new, v_new = _adamw(w_refs[i][...], g, m_refs[i][...], v_refs[i][...])
            outs[1 + i][...] = g
            outs[1 + k + i][...] = delta
            outs[1 + 2 * k + i][...] = m_new
            outs[1 + 3 * k + i][...] = v_new

    vmem = pl.BlockSpec(memory_space=pltpu.VMEM)
    shapes = [jax.ShapeDtypeStruct(w[n].shape, F32) for n in names]
    res = pl.pallas_call(
        body, name="adamw_small", in_specs=[vmem] * (2 + 3 * k), out_specs=[vmem] * (1 + 4 * k),
        out_shape=[jax.ShapeDtypeStruct((1, 1), F32)] + shapes * 4,
    )(wide, narrow, *[w[n] for n in names], *[m[n] for n in names], *[v[n] for n in names])
    unzip = lambda j: dict(zip(names, res[1 + j * k:1 + (j + 1) * k]))
    return res[0], unzip(0), unzip(1), unzip(2), unzip(3)


def kernel(x, norm_w, pool_in_w, pool_group_w, pool_group_b, pool_scale, pool_out_w, gla_in_w, gla_gk_w, gla_gk_b, gla_head_norm_w, gla_out_w, final_norm_w, loss_target, m_norm_w, m_pool_in_w, m_pool_group_w, m_pool_group_b, m_pool_scale, m_pool_out_w, m_gla_in_w, m_gla_gk_w, m_gla_gk_b, m_gla_head_norm_w, m_gla_out_w, m_final_norm_w, v_norm_w, v_pool_in_w, v_pool_group_w, v_pool_group_b, v_pool_scale, v_pool_out_w, v_gla_in_w, v_gla_gk_w, v_gla_gk_b, v_gla_head_norm_w, v_gla_out_w, v_final_norm_w):
    w = dict(norm_w=norm_w, pool_in_w=pool_in_w, pool_group_w=pool_group_w, pool_group_b=pool_group_b,
             pool_scale=pool_scale, pool_out_w=pool_out_w, gla_in_w=gla_in_w, gla_gk_w=gla_gk_w, gla_gk_b=gla_gk_b,
             gla_head_norm_w=gla_head_norm_w, gla_out_w=gla_out_w, final_norm_w=final_norm_w)
    m = dict(norm_w=m_norm_w, pool_in_w=m_pool_in_w, pool_group_w=m_pool_group_w, pool_group_b=m_pool_group_b,
             pool_scale=m_pool_scale, pool_out_w=m_pool_out_w, gla_in_w=m_gla_in_w, gla_gk_w=m_gla_gk_w,
             gla_gk_b=m_gla_gk_b, gla_head_norm_w=m_gla_head_norm_w, gla_out_w=m_gla_out_w,
             final_norm_w=m_final_norm_w)
    v = dict(norm_w=v_norm_w, pool_in_w=v_pool_in_w, pool_group_w=v_pool_group_w, pool_group_b=v_pool_group_b,
             pool_scale=v_pool_scale, pool_out_w=v_pool_out_w, gla_in_w=v_gla_in_w, gla_gk_w=v_gla_gk_w,
             gla_gk_b=v_gla_gk_b, gla_head_norm_w=v_gla_head_norm_w, gla_out_w=v_gla_out_w,
             final_norm_w=v_final_norm_w)
    col_shard = GLA_IN_WIDTH // N_DEV
    row_shard = D_MODEL // N_DEV

    def lanes(a):
        return jnp.pad(a, [(0, 0)] * (a.ndim - 1) + [(0, LANES - a.shape[-1])])

    small_in = jnp.concatenate([lanes(pool_group_b[0]), lanes(gla_gk_b), lanes(gla_head_norm_w),
                                jnp.zeros((2, LANES), F32)], axis=0)
    in_cols = 2 * POOL_WIDTH // N_DEV
    pool_in, pool_gw, pool_out, small_all = _comm_call("pool_weights_all_gather", _gather_rider(
        [pool_in_w[0].astype(BF16), pool_group_w[0].astype(BF16), pool_out_w[0].astype(BF16), small_in],
        [(D_MODEL, 2 * POOL_WIDTH), (POOL_GROUPS, POOL_GROUP_DIM, POOL_GROUP_DIM), (POOL_WIDTH, D_MODEL),
         (N_DEV, 8, LANES)],
        [_dim1_slot(in_cols), _dim1_slot(GROUP_SHARD), _row_slot(row_shard), _lead_slot]))
    pool_gb = jnp.transpose(small_all[:, 0:POOL_GROUPS, :GROUP_SHARD], (1, 0, 2)).reshape(1, POOL_WIDTH)
    gla_gkb = small_all[:, POOL_GROUPS, :KEY_SHARD].reshape(1, GLA_KEY_WIDTH)
    gla_hw = jnp.tile(small_all[:, POOL_GROUPS + 1, :HEAD_V_SHARD].reshape(1, GLA_HEAD_V), (1, GLA_HEADS))
    nw0, nw1, wf = norm_w[0:1], norm_w[1:2], final_norm_w.reshape(1, D_MODEL)
    xs, target = x[0], loss_target[0]

    h1, p, gla_in_parts, gkw_parts, gla_out = _pool_fwd_call(
        xs, nw0, pool_in, pool_gw, pool_gb, pool_scale, pool_out, _gather_rider(
            [jnp.transpose(gla_in_w[0]).astype(BF16), gla_gk_w[0].astype(BF16), gla_out_w[0].astype(BF16)],
            [(N_DEV, col_shard, D_MODEL), (N_DEV, GLA_GATE_RANK, KEY_SHARD), (GLA_VALUE_WIDTH, D_MODEL)],
            [_lead_slot, _lead_slot, _row_slot(row_shard)]))
    gla_in = jnp.pad(gla_in_parts.reshape(GLA_IN_WIDTH, D_MODEL), ((0, GLA_IN_PAD - GLA_IN_WIDTH), (0, 0)))
    gla_gkw = jnp.pad(jnp.transpose(gkw_parts, (1, 0, 2)).reshape(GLA_GATE_RANK, GLA_KEY_WIDTH),
                      ((0, GLA_LOW_PAD - GLA_GATE_RANK), (0, 0)))
    dh2, proj, o, states, loss_part, dwf = _gla_fwd_call(h1, nw1, gla_in, gla_gkw, gla_gkb, gla_hw, gla_out, wf, target)

    dproj, d_gla_out, dhw, dgkw, dgkb = _gla_bwd_call(dh2, proj, o, states, gla_gkw, gla_gkb, gla_hw, gla_out)
    dh1, d_gla_in, dnw1, landed_gla_out = _inproj_bwd_call(
        "gla_in_bwd", dproj, h1, nw1, gla_in, dh2,
        _exchange_rider([d_gla_out], [(row_shard, D_MODEL)], [_row_slot(row_shard)]), transposed=True)
    gla_in_send = d_gla_in[:GLA_IN_WIDTH].reshape(N_DEV, col_shard, D_MODEL)
    cols_a = GLA_IN_COLS_WITH_POOL_BWD
    dp, d_pool_out, dgw, dgb, dsc, landed_gla_in_a = _pool_bwd_call(
        dh1, p, pool_gw, pool_gb, pool_scale, pool_out,
        _exchange_rider([gla_in_send], [(col_shard, cols_a)], [lambda ref, d: ref.at[d, :, pl.ds(0, cols_a)]]))
    grad_x, d_pool_in, dnw0, landed_gla_in_b, landed_pool_out, landed_gw = _inproj_bwd_call(
        "pool_in_bwd", dp, xs, nw0, pool_in, dh1,
        _exchange_rider([gla_in_send, d_pool_out, dgw],
                        [(col_shard, D_MODEL - cols_a), (row_shard, D_MODEL),
                         (POOL_GROUPS, GROUP_SHARD, POOL_GROUP_DIM)],
                        [lambda ref, d: ref.at[d, :, pl.ds(cols_a, D_MODEL - cols_a)], _row_slot(row_shard),
                         _dim1_slot(GROUP_SHARD)]))

    wide = jnp.concatenate([
        dnw0, dnw1, dsc, dwf, jnp.pad(loss_part[0:1, 0:1], ((0, 0), (0, D_MODEL - 1))),
        jnp.zeros((WIDE_ROWS - 5, D_MODEL), F32)], axis=0)

    def rows8(a):
        return jnp.pad(lanes(a), ((0, 0), (0, -a.shape[1] % 8), (0, 0)))

    narrow = jnp.concatenate([
        rows8(jnp.transpose(dgb.reshape(POOL_GROUPS, N_DEV, GROUP_SHARD), (1, 0, 2))),
        rows8(jnp.transpose(dgkw[:GLA_GATE_RANK].reshape(GLA_GATE_RANK, N_DEV, KEY_SHARD), (1, 0, 2))),
        rows8(dgkb.reshape(N_DEV, 1, KEY_SHARD)),
        rows8(dhw.reshape(GLA_HEADS, GLA_HEAD_V).sum(axis=0).reshape(N_DEV, 1, HEAD_V_SHARD)),
    ], axis=1)
    landed_pool_in, landed_wide, landed_narrow = _comm_call("grads_all_to_all", _exchange_rider(
        [d_pool_in, wide, narrow], [(D_MODEL, in_cols), (WIDE_ROWS, D_MODEL), (NARROW_ROWS, LANES)],
        [_dim1_slot(in_cols), lambda ref, d: ref, _lead_slot]))

    res = {}
    for name, parts, rows, cols, block in [
            ("pool_in_w", landed_pool_in, D_MODEL, in_cols, 256),
            ("pool_group_w", landed_gw, POOL_GROUPS * GROUP_SHARD, POOL_GROUP_DIM, 128),
            ("pool_out_w", landed_pool_out, row_shard, D_MODEL, 128),
            ("gla_out_w", landed_gla_out, row_shard, D_MODEL, 128)]:
        outs = _adamw_call("adamw_" + name, parts.reshape(N_DEV, rows, cols), w[name].reshape(rows, cols),
                           m[name].reshape(rows, cols), v[name].reshape(rows, cols), block)
        res[name] = [t.reshape(w[name].shape) for t in outs]
    outs = _adamw_cols_call("adamw_gla_in_w", landed_gla_in_a, landed_gla_in_b, jnp.transpose(gla_in_w[0]),
                            jnp.transpose(m_gla_in_w[0]), jnp.transpose(v_gla_in_w[0]), 256)
    res["gla_in_w"] = [jnp.transpose(t)[None] for t in outs]
    small_shapes = {"norm_w": (2, D_MODEL), "pool_scale": (1, D_MODEL), "final_norm_w": (1, D_MODEL),
                    "pool_group_b": (POOL_GROUPS, GROUP_SHARD), "gla_gk_w": (GLA_GATE_RANK, KEY_SHARD),
                    "gla_gk_b": (1, KEY_SHARD), "gla_head_norm_w": (1, HEAD_V_SHARD)}
    as_small = lambda t: {n: t[n].reshape(s) for n, s in small_shapes.items()}
    loss, *small_outs = _small_adamw_call(landed_wide, landed_narrow, as_small(w), as_small(m), as_small(v))
    for name in small_shapes:
        res[name] = [t[name].reshape(w[name].shape) for t in small_outs]
    order = ("norm_w", "pool_in_w", "pool_group_w", "pool_group_b", "pool_scale", "pool_out_w", "gla_in_w",
             "gla_gk_w", "gla_gk_b", "gla_head_norm_w", "gla_out_w", "final_norm_w")
    return (loss.reshape(()), grad_x[None], *[res[n][0] for n in order], *[res[n][1] for n in order],
            *[res[n][2] for n in order], *[res[n][3] for n in order])
```

```python
import functools

import jax
import jax.numpy as jnp
from jax import lax
from jax.experimental import pallas as pl
from jax.experimental.pallas import tpu as pltpu

F32 = jnp.float32
BF16 = jnp.bfloat16
MESH = pl.DeviceIdType.MESH

N_DEV = 8
D_MODEL = 1024
POOL_WIDTH = 1024
POOL_GROUPS = 4
POOL_GROUP_DIM = 256
POOL_HALO = 16
GLA_HEADS = 4
GLA_HEAD_K = 128
GLA_HEAD_V = 256
GLA_KEY_WIDTH = 512
GLA_VALUE_WIDTH = 1024
GLA_GATE_RANK = 16
GLA_IN_WIDTH = 3088
GLA_IN_PAD = 3200
GLA_LOW_PAD = 128
CHUNK = 64
GATE_NORMALIZER = 16.0
RMS_EPS = 1e-6
Q_SCALE = GLA_HEAD_K ** -0.5

ADAM_LR = 0.001
ADAM_B1 = 0.9
ADAM_B2 = 0.999
ADAM_EPS = 1e-08
ADAM_WD = 0.01
ADAM_STEP = 10

LANES = 128
BF16_ROWS = 16
VMEM_LIMIT = 56 * 1024 * 1024
ROW_TILE = 256
MATMUL_ROW_TILE = 512
GLA_IN_COLS_WITH_POOL_BWD = 768


def _dot_nn(a, b):
    return lax.dot_general(a, b, (((1,), (0,)), ((), ())), preferred_element_type=F32)


def _dot_nt(a, b):
    return lax.dot_general(a, b, (((1,), (1,)), ((), ())), preferred_element_type=F32)


def _dot_tn(a, b):
    return lax.dot_general(a, b, (((0,), (0,)), ((), ())), preferred_element_type=F32)


def _rms(x):
    rstd = lax.rsqrt(jnp.mean(x * x, axis=-1, keepdims=True) + RMS_EPS)
    return x * rstd, rstd


def _rms_bwd(dxhat, xhat, rstd):
    return rstd * (dxhat - xhat * jnp.mean(dxhat * xhat, axis=-1, keepdims=True))


def _sigmoid(x):
    return 1.0 / (1.0 + jnp.exp(-x))


def _params(sem=("arbitrary",)):
    return pltpu.CompilerParams(dimension_semantics=sem, vmem_limit_bytes=VMEM_LIMIT)


def _full(shape):
    return pl.BlockSpec(shape, lambda i: (0,) * len(shape))


def _const(shape):
    return pl.BlockSpec(shape, lambda i: (0,) * len(shape), pipeline_mode=pl.Buffered(1))


def _window_sums(ext, forward):
    n = ext.shape[0]
    outs = []
    for g in range(POOL_GROUPS):
        s = ext[:, g * POOL_GROUP_DIM:(g + 1) * POOL_GROUP_DIM]
        for k in range(g + 1):
            shift = (1 << k) if forward else n - (1 << k)
            s = s + pltpu.roll(s, shift, axis=0)
        outs.append(s[:n - POOL_HALO])
    return outs


def _inv_count(row0, tm):
    row = row0 + lax.broadcasted_iota(jnp.int32, (tm, 1), 0)
    return [1.0 / jnp.minimum(row + 1, 2 << g).astype(F32) for g in range(POOL_GROUPS)]


def _pool_mix(u, u_prev, row0, gw_ref, gb):
    tm = u.shape[0]
    sums = _window_sums(jnp.concatenate([u, u_prev], axis=0), True)
    inv = _inv_count(row0, tm)
    pooled, mixed = [], []
    for g in range(POOL_GROUPS):
        ug = u[:, g * POOL_GROUP_DIM:(g + 1) * POOL_GROUP_DIM]
        pg = (sums[g] * inv[g] - ug).astype(BF16)
        pooled.append(pg)
        mixed.append(_dot_nn(pg, gw_ref[g]))
    return pooled, jnp.concatenate(mixed, axis=1) + gb


def _pool_fwd_call(x, nw, w_in, gw, gb, sc, w_out, rider=None):
    seq = x.shape[0]
    tm = min(MATMUL_ROW_TILE, seq)
    nt = seq // tm

    def main(x_ref, nw_ref, win_ref, gw_ref, gb_ref, sc_ref, wout_ref, h_ref, p_ref, halo_ref):
        i = pl.program_id(0)

        @pl.when(i == 0)
        def _():
            halo_ref[...] = jnp.zeros_like(halo_ref)

        xt = x_ref[...]
        xhat, _ = _rms(xt)
        n = (xhat * nw_ref[...]).astype(BF16)
        p = _dot_nn(n, win_ref[...])
        p_ref[...] = p
        u = p[:, :POOL_WIDTH]
        gate = p[:, POOL_WIDTH:]
        _, mixed = _pool_mix(u, halo_ref[...], i * tm, gw_ref, gb_ref[...])
        halo_ref[...] = u[tm - POOL_HALO:, :]
        y = (mixed * sc_ref[...] * (gate * _sigmoid(gate))).astype(BF16)
        h_ref[...] = xt + _dot_nn(y, wout_ref[...])

    def body(*refs):
        own, comm = _split_refs(refs, 7, 2, 1, rider)
        _ride_before(comm, pl.program_id(0), nt)
        main(*own)
        _ride_after(comm, pl.program_id(0), nt)

    return pl.pallas_call(
        body, name="pool_fwd", grid=(nt,),
        in_specs=_extend([pl.BlockSpec((tm, D_MODEL), lambda i: (i, 0)), _const((1, D_MODEL)),
                          _const((D_MODEL, 2 * POOL_WIDTH)), _const((POOL_GROUPS, POOL_GROUP_DIM, POOL_GROUP_DIM)),
                          _const((1, POOL_WIDTH)), _const((1, POOL_WIDTH)), _const((POOL_WIDTH, D_MODEL))],
                         rider, "in_specs"),
        out_specs=_extend([pl.BlockSpec((tm, D_MODEL), lambda i: (i, 0)),
                           pl.BlockSpec((tm, 2 * POOL_WIDTH), lambda i: (i, 0))], rider, "out_specs"),
        out_shape=_extend([jax.ShapeDtypeStruct((seq, D_MODEL), F32),
                           jax.ShapeDtypeStruct((seq, 2 * POOL_WIDTH), F32)], rider, "out_shape"),
        scratch_shapes=_extend([pltpu.VMEM((POOL_HALO, POOL_WIDTH), F32)], rider, "scratch"),
        compiler_params=_params(),
    )(x, nw, w_in, gw, gb, sc, w_out, *_extend([], rider, "arrays"))


def _pool_bwd_call(dh, p, gw, gb, sc, w_out, rider=None):
    seq = dh.shape[0]
    tm = min(MATMUL_ROW_TILE, seq)
    nt = seq // tm
    halo_blocks = tm // POOL_HALO

    def main(dh_ref, p_ref, pprev_ref, gw_ref, gb_ref, sc_ref, wout_ref,
             dp_ref, dwout_hbm, dgw_hbm, dgb_ref, dsc_ref, carry_ref, dwout_acc, dgw_acc, dwout_stage, dgw_stage):
        i = pl.program_id(0)
        t = nt - 1 - i

        @pl.when(i == 0)
        def _():
            carry_ref[...] = jnp.zeros_like(carry_ref)
            dwout_acc[...] = jnp.zeros_like(dwout_acc)
            dgw_acc[...] = jnp.zeros_like(dgw_acc)
            dgb_ref[...] = jnp.zeros_like(dgb_ref)
            dsc_ref[...] = jnp.zeros_like(dsc_ref)

        p = p_ref[...]
        u = p[:, :POOL_WIDTH]
        gate = p[:, POOL_WIDTH:]
        u_prev = jnp.where(t > 0, pprev_ref[:, :POOL_WIDTH], 0.0)
        pooled, mixed = _pool_mix(u, u_prev, t * tm, gw_ref, gb_ref[...])
        sg = _sigmoid(gate)
        silu = gate * sg
        sc = sc_ref[...]
        dhb = dh_ref[...].astype(BF16)
        y = (mixed * sc * silu).astype(BF16)
        dwout_acc[...] += _dot_tn(y, dhb)
        dy = _dot_nt(dhb, wout_ref[...])
        dmixed = dy * sc * silu
        dsc_ref[...] += jnp.sum(dy * mixed * silu, axis=0, keepdims=True)
        dgate = dy * mixed * sc * (sg * (1.0 + gate * (1.0 - sg)))
        dgb_ref[...] += jnp.sum(dmixed, axis=0, keepdims=True)
        inv = _inv_count(t * tm, tm)
        dpooled, scaled = [], []
        for g in range(POOL_GROUPS):
            dmg = dmixed[:, g * POOL_GROUP_DIM:(g + 1) * POOL_GROUP_DIM].astype(BF16)
            dgw_acc[g] += _dot_tn(pooled[g], dmg)
            dpg = _dot_nt(dmg, gw_ref[g])
            dpooled.append(dpg)
            scaled.append(dpg * inv[g])
        r = jnp.concatenate(scaled, axis=1)
        sums = _window_sums(jnp.concatenate([r, carry_ref[...]], axis=0), False)
        carry_ref[...] = r[:POOL_HALO, :]
        du = jnp.concatenate([sums[g] - dpooled[g] for g in range(POOL_GROUPS)], axis=1)
        dp_ref[...] = jnp.concatenate([du, dgate], axis=1).astype(BF16)

        @pl.when(i == nt - 1)
        def _():
            dwout_stage[...] = dwout_acc[...].astype(BF16)
            dgw_stage[...] = dgw_acc[...].astype(BF16)
            pltpu.sync_copy(dwout_stage, dwout_hbm)
            pltpu.sync_copy(dgw_stage, dgw_hbm)

    def body(*refs):
        own, comm = _split_refs(refs, 7, 5, 5, rider)
        _ride_before(comm, pl.program_id(0), nt)
        main(*own)
        _ride_after(comm, pl.program_id(0), nt)

    rev = lambda i: (nt - 1 - i, 0)
    return pl.pallas_call(
        body, name="pool_bwd", grid=(nt,),
        in_specs=_extend([pl.BlockSpec((tm, D_MODEL), rev), pl.BlockSpec((tm, 2 * POOL_WIDTH), rev),
                          pl.BlockSpec((POOL_HALO, 2 * POOL_WIDTH),
                                       lambda i: (jnp.maximum((nt - 1 - i) * halo_blocks - 1, 0), 0)),
                          _const((POOL_GROUPS, POOL_GROUP_DIM, POOL_GROUP_DIM)), _const((1, POOL_WIDTH)),
                          _const((1, POOL_WIDTH)), _const((POOL_WIDTH, D_MODEL))], rider, "in_specs"),
        out_specs=_extend([pl.BlockSpec((tm, 2 * POOL_WIDTH), rev), pl.BlockSpec(memory_space=pl.ANY),
                           pl.BlockSpec(memory_space=pl.ANY), _full((1, POOL_WIDTH)), _full((1, POOL_WIDTH))],
                          rider, "out_specs"),
        out_shape=_extend([jax.ShapeDtypeStruct((seq, 2 * POOL_WIDTH), BF16),
                           jax.ShapeDtypeStruct((POOL_WIDTH, D_MODEL), BF16),
                           jax.ShapeDtypeStruct((POOL_GROUPS, POOL_GROUP_DIM, POOL_GROUP_DIM), BF16),
                           jax.ShapeDtypeStruct((1, POOL_WIDTH), F32), jax.ShapeDtypeStruct((1, POOL_WIDTH), F32)],
                          rider, "out_shape"),
        scratch_shapes=_extend([pltpu.VMEM((POOL_HALO, POOL_WIDTH), F32), pltpu.VMEM((POOL_WIDTH, D_MODEL), F32),
                                pltpu.VMEM((POOL_GROUPS, POOL_GROUP_DIM, POOL_GROUP_DIM), F32),
                                pltpu.VMEM((POOL_WIDTH, D_MODEL), BF16),
                                pltpu.VMEM((POOL_GROUPS, POOL_GROUP_DIM, POOL_GROUP_DIM), BF16)], rider, "scratch"),
        compiler_params=_params(),
    )(dh, p, p, gw, gb, sc, w_out, *_extend([], rider, "arrays"))


def _inproj_bwd_call(name, dproj, h_in, nw, w_in, dres, rider=None, transposed=False):
    seq = h_in.shape[0]
    width = dproj.shape[1]
    w_shape = tuple(w_in.shape)
    tm = min(MATMUL_ROW_TILE, seq)
    nt = seq // tm

    def main(dproj_ref, h_ref, nw_ref, win_ref, dres_ref, dh_ref, dw_hbm, dnw_ref, dw_acc, dw_stage):
        i = pl.program_id(0)

        @pl.when(i == 0)
        def _():
            dw_acc[...] = jnp.zeros_like(dw_acc)
            dnw_ref[...] = jnp.zeros_like(dnw_ref)

        xhat, rstd = _rms(h_ref[...])
        nw_row = nw_ref[...]
        n = (xhat * nw_row).astype(BF16)
        dpb = dproj_ref[...]
        if transposed:
            dw_acc[...] += _dot_tn(dpb, n)
            dn = _dot_nn(dpb, win_ref[...])
        else:
            dw_acc[...] += _dot_tn(n, dpb)
            dn = _dot_nt(dpb, win_ref[...])
        dnw_ref[...] += jnp.sum(dn * xhat, axis=0, keepdims=True)
        dh_ref[...] = _rms_bwd(dn * nw_row, xhat, rstd) + dres_ref[...]

        @pl.when(i == nt - 1)
        def _():
            dw_stage[...] = dw_acc[...].astype(BF16)
            pltpu.sync_copy(dw_stage, dw_hbm)

    def body(*refs):
        own, comm = _split_refs(refs, 5, 3, 2, rider)
        _ride_before(comm, pl.program_id(0), nt)
        main(*own)
        _ride_after(comm, pl.program_id(0), nt)

    row = lambda i: (i, 0)
    return pl.pallas_call(
        body, name=name, grid=(nt,),
        in_specs=_extend([pl.BlockSpec((tm, width), row), pl.BlockSpec((tm, D_MODEL), row), _const((1, D_MODEL)),
                          _const(w_shape), pl.BlockSpec((tm, D_MODEL), row)], rider, "in_specs"),
        out_specs=_extend([pl.BlockSpec((tm, D_MODEL), row), pl.BlockSpec(memory_space=pl.ANY),
                           _full((1, D_MODEL))], rider, "out_specs"),
        out_shape=_extend([jax.ShapeDtypeStruct((seq, D_MODEL), F32), jax.ShapeDtypeStruct(w_shape, BF16),
                           jax.ShapeDtypeStruct((1, D_MODEL), F32)], rider, "out_shape"),
        scratch_shapes=_extend([pltpu.VMEM(w_shape, F32), pltpu.VMEM(w_shape, BF16)], rider, "scratch"),
        compiler_params=_params(),
    )(dproj, h_in, nw, w_in, dres, *_extend([], rider, "arrays"))


def _chunk_scan(x, reverse):
    n = x.shape[0]
    pos = lax.broadcasted_iota(jnp.int32, (n, 1), 0) & (CHUNK - 1)
    k = 1
    while k < CHUNK:
        if reverse:
            x = x + jnp.where(pos < CHUNK - k, pltpu.roll(x, n - k, axis=0), 0.0)
        else:
            x = x + jnp.where(pos >= k, pltpu.roll(x, k, axis=0), 0.0)
        k *= 2
    return x


class _GlaTile:
    def __init__(self, proj, gkw_ref, gkb):
        tm = proj.shape[0]
        self.q = proj[:, :GLA_KEY_WIDTH] * Q_SCALE
        self.k = proj[:, GLA_KEY_WIDTH:2 * GLA_KEY_WIDTH]
        self.v = proj[:, 2 * GLA_KEY_WIDTH:2 * GLA_KEY_WIDTH + GLA_VALUE_WIDTH]
        self.gate = proj[:, 2 * GLA_KEY_WIDTH + GLA_VALUE_WIDTH:2 * GLA_KEY_WIDTH + 2 * GLA_VALUE_WIDTH]
        self.low_b = proj[:, 2 * GLA_KEY_WIDTH + 2 * GLA_VALUE_WIDTH:].astype(BF16)
        self.z = _dot_nn(self.low_b, gkw_ref[...]) + gkb
        log_g = (jnp.minimum(self.z, 0.0) - jnp.log(1.0 + jnp.exp(-jnp.abs(self.z)))) / GATE_NORMALIZER
        self.c = _chunk_scan(log_g, False)
        is_last = lax.broadcasted_iota(jnp.int32, (CHUNK, 1), 0) == CHUNK - 1
        last = [jnp.sum(jnp.where(is_last, self.c[j * CHUNK:(j + 1) * CHUNK, :], 0.0), axis=0, keepdims=True)
                for j in range(tm // CHUNK)]
        self.c_last = last
        c_last_rows = jnp.concatenate([jnp.broadcast_to(r, (CHUNK, GLA_KEY_WIDTH)) for r in last], axis=0)
        self.e_pos = jnp.exp(self.c)
        self.e_neg = jnp.exp(-self.c)
        self.e_rest = jnp.exp(c_last_rows - self.c)
        self.a = self.q * self.e_pos
        self.b = self.k * self.e_neg
        self.cn = self.q * self.e_neg
        self.dp = self.k * self.e_pos
        self.kd = self.k * self.e_rest
        self.a_b, self.b_b, self.cn_b, self.dp_b, self.kd_b, self.v_b = (
            t.astype(BF16) for t in (self.a, self.b, self.cn, self.dp, self.kd, self.v))
        idx_t = lax.broadcasted_iota(jnp.int32, (CHUNK, CHUNK), 0)
        idx_s = lax.broadcasted_iota(jnp.int32, (CHUNK, CHUNK), 1)
        self.lower = idx_t >= idx_s

    @staticmethod
    def rows(j):
        return slice(j * CHUNK, (j + 1) * CHUNK)

    @staticmethod
    def kcols(h):
        return slice(h * GLA_HEAD_K, (h + 1) * GLA_HEAD_K)

    @staticmethod
    def vcols(h):
        return slice(h * GLA_HEAD_V, (h + 1) * GLA_HEAD_V)

    def scores(self, j, h):
        r, kc = self.rows(j), self.kcols(h)
        fwd = _dot_nt(self.a_b[r, kc], self.b_b[r, kc])
        bwd = _dot_nt(self.cn_b[r, kc], self.dp_b[r, kc])
        return jnp.where(self.lower, fwd, bwd).astype(BF16)


def _gla_fwd_call(h1, nw, w_in, gkw, gkb, hw, w_out, wf, target):
    seq = h1.shape[0]
    tm = ROW_TILE
    nt = seq // tm
    cpt = tm // CHUNK
    n_chunks = seq // CHUNK

    def body(h_ref, nw_ref, win_ref, gkw_ref, gkb_ref, hw_ref, wout_ref, wf_ref, tgt_ref,
             dh2_ref, proj_ref, o_ref, st_ref, loss_ref, dwf_ref, state_ref):
        i = pl.program_id(0)

        @pl.when(i == 0)
        def _():
            state_ref[...] = jnp.zeros_like(state_ref)
            loss_ref[...] = jnp.zeros_like(loss_ref)
            dwf_ref[...] = jnp.zeros_like(dwf_ref)

        ht = h_ref[...]
        xhat, _ = _rms(ht)
        n = (xhat * nw_ref[...]).astype(BF16)
        proj = _dot_nt(n, win_ref[...])
        proj_ref[...] = proj
        g = _GlaTile(proj, gkw_ref, gkb_ref[...])
        o_rows = []
        for j in range(cpt):
            r = g.rows(j)
            o_heads = []
            for h in range(GLA_HEADS):
                kc, vc = g.kcols(h), g.vcols(h)
                srows = slice(h * GLA_HEAD_V, (h + 1) * GLA_HEAD_V)
                state = state_ref[srows, :]
                st_ref[j, srows, :] = state
                o_heads.append(_dot_nn(g.scores(j, h), g.v_b[r, vc]) + _dot_nt(g.a_b[r, kc], state.astype(BF16)))
                decay = jnp.exp(g.c_last[j][:, kc])
                state_ref[srows, :] = state * decay + _dot_tn(g.v_b[r, vc], g.kd_b[r, kc])
            o_rows.append(jnp.concatenate(o_heads, axis=1))
        o = jnp.concatenate(o_rows, axis=0)
        o_ref[...] = o
        hw_row = hw_ref[...]
        on = jnp.concatenate([_rms(o[:, g.vcols(h)])[0] for h in range(GLA_HEADS)], axis=1) * hw_row
        y = (on * (g.gate * _sigmoid(g.gate))).astype(BF16)
        h2 = ht + _dot_nn(y, wout_ref[...])
        xhat2, rstd2 = _rms(h2)
        wf_row = wf_ref[...]
        err = xhat2 * wf_row - tgt_ref[...]
        loss_ref[...] += 0.5 * jnp.sum(err * err) / D_MODEL
        dout = err * (1.0 / D_MODEL)
        dwf_ref[...] += jnp.sum(dout * xhat2, axis=0, keepdims=True)
        dh2_ref[...] = _rms_bwd(dout * wf_row, xhat2, rstd2)

    row = lambda i: (i, 0)
    return pl.pallas_call(
        body, name="gla_fwd", grid=(nt,),
        in_specs=[pl.BlockSpec((tm, D_MODEL), row), _const((1, D_MODEL)), _const((GLA_IN_PAD, D_MODEL)),
                  _const((GLA_LOW_PAD, GLA_KEY_WIDTH)), _const((1, GLA_KEY_WIDTH)), _const((1, GLA_VALUE_WIDTH)),
                  _const((GLA_VALUE_WIDTH, D_MODEL)), _const((1, D_MODEL)), pl.BlockSpec((tm, D_MODEL), row)],
        out_specs=[pl.BlockSpec((tm, D_MODEL), row), pl.BlockSpec((tm, GLA_IN_PAD), row),
                   pl.BlockSpec((tm, GLA_VALUE_WIDTH), row),
                   pl.BlockSpec((cpt, GLA_VALUE_WIDTH, GLA_HEAD_K), lambda i: (i, 0, 0)),
                   _full((8, LANES)), _full((1, D_MODEL))],
        out_shape=[jax.ShapeDtypeStruct((seq, D_MODEL), F32), jax.ShapeDtypeStruct((seq, GLA_IN_PAD), F32),
                   jax.ShapeDtypeStruct((seq, GLA_VALUE_WIDTH), F32),
                   jax.ShapeDtypeStruct((n_chunks, GLA_VALUE_WIDTH, GLA_HEAD_K), F32),
                   jax.ShapeDtypeStruct((8, LANES), F32), jax.ShapeDtypeStruct((1, D_MODEL), F32)],
        scratch_shapes=[pltpu.VMEM((GLA_VALUE_WIDTH, GLA_HEAD_K), F32)],
        compiler_params=_params(),
    )(h1, nw, w_in, gkw, gkb, hw, w_out, wf, target)


def _gla_bwd_call(dh2, proj, o, states, gkw, gkb, hw, w_out):
    seq = dh2.shape[0]
    tm = ROW_TILE
    nt = seq // tm
    cpt = tm // CHUNK

    def body(dh_ref, proj_ref, o_ref, st_ref, gkw_ref, gkb_ref, hw_ref, wout_ref,
             dproj_ref, dwout_hbm, dhw_ref, dgkw_ref, dgkb_ref, dstate_ref, dwout_acc, dwout_stage):
        i = pl.program_id(0)

        @pl.when(i == 0)
        def _():
            dstate_ref[...] = jnp.zeros_like(dstate_ref)
            dwout_acc[...] = jnp.zeros_like(dwout_acc)
            dhw_ref[...] = jnp.zeros_like(dhw_ref)
            dgkw_ref[...] = jnp.zeros_like(dgkw_ref)
            dgkb_ref[...] = jnp.zeros_like(dgkb_ref)

        g = _GlaTile(proj_ref[...], gkw_ref, gkb_ref[...])
        dhb = dh_ref[...].astype(BF16)
        o = o_ref[...]
        hw_row = hw_ref[...]
        dy = _dot_nt(dhb, wout_ref[...])
        sg = _sigmoid(g.gate)
        silu = g.gate * sg
        don = dy * silu
        on_parts, do_parts, dhw_parts = [], [], []
        for h in range(GLA_HEADS):
            vc = g.vcols(h)
            xh, rs = _rms(o[:, vc])
            on_parts.append(xh * hw_row[:, vc])
            dhw_parts.append(jnp.sum(don[:, vc] * xh, axis=0, keepdims=True))
            do_parts.append(_rms_bwd(don[:, vc] * hw_row[:, vc], xh, rs))
        on = jnp.concatenate(on_parts, axis=1)
        dwout_acc[...] += _dot_tn((on * silu).astype(BF16), dhb)
        dhw_ref[...] += jnp.concatenate(dhw_parts, axis=1)
        dgate = dy * on * (sg * (1.0 + g.gate * (1.0 - sg)))
        do_b = jnp.concatenate(do_parts, axis=1).astype(BF16)

        last_row = lax.broadcasted_iota(jnp.int32, (CHUNK, 1), 0) == CHUNK - 1
        dq_rows, dk_rows, dv_rows, dc_rows = [None] * cpt, [None] * cpt, [None] * cpt, [None] * cpt
        for j in reversed(range(cpt)):
            r = g.rows(j)
            dq_h, dk_h, dv_h, dc_h = [], [], [], []
            for h in range(GLA_HEADS):
                kc, vc = g.kcols(h), g.vcols(h)
                srows = slice(h * GLA_HEAD_V, (h + 1) * GLA_HEAD_V)
                state = st_ref[j, srows, :]
                dstate = dstate_ref[srows, :]
                dstate_b = dstate.astype(BF16)
                do_c = do_b[r, vc]
                scores = g.scores(j, h)
                dscores = _dot_nt(do_c, g.v_b[r, vc])
                dfwd = jnp.where(g.lower, dscores, 0.0).astype(BF16)
                dbwd = jnp.where(g.lower, 0.0, dscores).astype(BF16)
                dv_h.append(_dot_tn(scores, do_c) + _dot_nt(g.kd_b[r, kc], dstate_b))
                da = _dot_nn(dfwd, g.b_b[r, kc]) + _dot_nn(do_c, state.astype(BF16))
                db = _dot_tn(dfwd, g.a_b[r, kc])
                dcn = _dot_nn(dbwd, g.dp_b[r, kc])
                ddp = _dot_tn(dbwd, g.cn_b[r, kc])
                dkd = _dot_nn(g.v_b[r, vc], dstate_b)
                decay = jnp.exp(g.c_last[j][:, kc])
                dstate_ref[srows, :] = _dot_tn(do_c, g.a_b[r, kc]) + dstate * decay
                kd_c = g.kd[r, kc]
                dkd_kd = dkd * kd_c
                dc_last = (jnp.sum(dkd_kd, axis=0, keepdims=True)
                           + decay * jnp.sum(state * dstate, axis=0, keepdims=True))
                dq_h.append(Q_SCALE * (da * g.e_pos[r, kc] + dcn * g.e_neg[r, kc]))
                dk_h.append(db * g.e_neg[r, kc] + ddp * g.e_pos[r, kc] + dkd * g.e_rest[r, kc])
                dc = (da * g.a[r, kc] - db * g.b[r, kc] - dcn * g.cn[r, kc] + ddp * g.dp[r, kc] - dkd_kd)
                dc_h.append(dc + jnp.where(last_row, dc_last, 0.0))
            dq_rows[j] = jnp.concatenate(dq_h, axis=1)
            dk_rows[j] = jnp.concatenate(dk_h, axis=1)
            dv_rows[j] = jnp.concatenate(dv_h, axis=1)
            dc_rows[j] = jnp.concatenate(dc_h, axis=1)
        dq = jnp.concatenate(dq_rows, axis=0)
        dk = jnp.concatenate(dk_rows, axis=0)
        dv = jnp.concatenate(dv_rows, axis=0)
        dlog_g = _chunk_scan(jnp.concatenate(dc_rows, axis=0), True)
        dz = dlog_g * (1.0 / GATE_NORMALIZER) * (1.0 - _sigmoid(g.z))
        dzb = dz.astype(BF16)
        dgkb_ref[...] += jnp.sum(dz, axis=0, keepdims=True)
        dgkw_ref[...] += _dot_tn(g.low_b, dzb)
        dlow = _dot_nt(dzb, gkw_ref[...])
        dproj_ref[...] = jnp.concatenate([dq, dk, dv, dgate, dlow], axis=1).astype(BF16)

        @pl.when(i == nt - 1)
        def _():
            dwout_stage[...] = dwout_acc[...].astype(BF16)
            pltpu.sync_copy(dwout_stage, dwout_hbm)

    rev = lambda i: (nt - 1 - i, 0)
    return pl.pallas_call(
        body, name="gla_bwd", grid=(nt,),
        in_specs=[pl.BlockSpec((tm, D_MODEL), rev), pl.BlockSpec((tm, GLA_IN_PAD), rev),
                  pl.BlockSpec((tm, GLA_VALUE_WIDTH), rev),
                  pl.BlockSpec((cpt, GLA_VALUE_WIDTH, GLA_HEAD_K), lambda i: (nt - 1 - i, 0, 0)),
                  _const((GLA_LOW_PAD, GLA_KEY_WIDTH)), _const((1, GLA_KEY_WIDTH)), _const((1, GLA_VALUE_WIDTH)),
                  _const((GLA_VALUE_WIDTH, D_MODEL))],
        out_specs=[pl.BlockSpec((tm, GLA_IN_PAD), rev), pl.BlockSpec(memory_space=pl.ANY),
                   _full((1, GLA_VALUE_WIDTH)), _full((GLA_LOW_PAD, GLA_KEY_WIDTH)), _full((1, GLA_KEY_WIDTH))],
        out_shape=[jax.ShapeDtypeStruct((seq, GLA_IN_PAD), BF16), jax.ShapeDtypeStruct((GLA_VALUE_WIDTH, D_MODEL), BF16),
                   jax.ShapeDtypeStruct((1, GLA_VALUE_WIDTH), F32), jax.ShapeDtypeStruct((GLA_LOW_PAD, GLA_KEY_WIDTH), F32),
                   jax.ShapeDtypeStruct((1, GLA_KEY_WIDTH), F32)],
        scratch_shapes=[pltpu.VMEM((GLA_VALUE_WIDTH, GLA_HEAD_K), F32), pltpu.VMEM((GLA_VALUE_WIDTH, D_MODEL), F32),
                        pltpu.VMEM((GLA_VALUE_WIDTH, D_MODEL), BF16)],
        compiler_params=_params(),
    )(dh2, proj, o, states, gkw, gkb, hw, w_out)


def _position():
    return lax.axis_index("x"), lax.axis_index("y"), lax.axis_index("c")


def _lead_slot(ref, d):
    return ref.at[d]


def _row_slot(rows):
    return lambda ref, d: ref.at[pl.ds(pl.multiple_of(d * rows, rows), rows)]


def _dim1_slot(size):
    return lambda ref, d: ref.at[:, pl.ds(pl.multiple_of(d * size, size), size)]


class _Gather:
    def __init__(self, in_refs, out_refs, slots, send_sems, recv_sems, local_sems):
        self.in_refs, self.out_refs, self.slots = in_refs, out_refs, slots
        self.send_sems, self.recv_sems, self.local_sems = send_sems, recv_sems, local_sems
        self.n = len(in_refs)
        x, y, c = _position()
        self.c = c
        self.me, self.sibling = (x, y, c), (x, y, 1 - c)
        self.chips = [(1 - x, y), (x, 1 - y), (1 - x, 1 - y)]

    def _copy(self, a, k, block, to, from_input=False):
        part = self.slots[a](self.out_refs[a], 4 * block[0] + 2 * block[1] + block[2])
        return pltpu.make_async_remote_copy(
            src_ref=self.in_refs[a] if from_input else part, dst_ref=part,
            send_sem=self.send_sems.at[a, k], recv_sem=self.recv_sems.at[a, k], device_id=to, device_id_type=MESH)

    def _mine(self):
        return [pltpu.make_async_copy(self.in_refs[a], self.slots[a](self.out_refs[a], 4 * self.me[0] + 2 * self.me[1]
                                                                    + self.me[2]), self.local_sems.at[a])
                for a in range(self.n)]

    def _first(self):
        first = [self._copy(a, 0, self.me, self.sibling, True) for a in range(self.n)]
        return first + [self._copy(a, 1 + j, self.me, (*chip, self.c), True)
                        for j, chip in enumerate(self.chips) for a in range(self.n)]

    def _passed(self):
        return [self._copy(a, 4 + j, (*chip, self.c), self.sibling)
                for j, chip in enumerate(self.chips) for a in range(self.n)]

    def start(self):
        for cp in self._mine() + self._first():
            cp.start()

    def forward(self):
        passed = self._passed()
        for j, chip in enumerate(self.chips):
            for a in range(self.n):
                self._copy(a, 1 + j, (*chip, self.c), self.me).wait_recv()
                passed[j * self.n + a].start()

    def finish(self):
        for a in range(self.n):
            self._copy(a, 0, self.sibling, self.me).wait_recv()
        for j, chip in enumerate(self.chips):
            for a in range(self.n):
                self._copy(a, 4 + j, (*chip, 1 - self.c), self.me).wait_recv()
        for cp in self._first() + self._passed():
            cp.wait_send()
        for cp in self._mine():
            cp.wait()


class _Exchange:
    def __init__(self, in_refs, out_refs, slots, send_sems, recv_sems, local_sems):
        self.in_refs, self.out_refs, self.slots = in_refs, out_refs, slots
        self.send_sems, self.recv_sems, self.local_sems = send_sems, recv_sems, local_sems
        self.n = len(in_refs)
        self.pos = _position()

    def _copies(self):
        x, y, c = self.pos
        me = 4 * x + 2 * y + c
        mine = [pltpu.make_async_copy(self.slots[a](self.in_refs[a], me), self.out_refs[a].at[me],
                                      self.local_sems.at[a]) for a in range(self.n)]
        remote = []
        for k in range(1, N_DEV):
            px, py, pc = x ^ (k >> 2), y ^ ((k >> 1) & 1), c ^ (k & 1)
            for a in range(self.n):
                remote.append(pltpu.make_async_remote_copy(
                    src_ref=self.slots[a](self.in_refs[a], 4 * px + 2 * py + pc), dst_ref=self.out_refs[a].at[me],
                    send_sem=self.send_sems.at[a, k - 1], recv_sem=self.recv_sems.at[a, k - 1],
                    device_id=(px, py, pc), device_id_type=MESH))
        return mine, remote

    def start(self):
        mine, remote = self._copies()
        for cp in mine + remote:
            cp.start()

    def forward(self):
        pass

    def finish(self):
        mine, remote = self._copies()
        for cp in remote:
            cp.wait_recv()
        for cp in remote:
            cp.wait_send()
        for cp in mine:
            cp.wait()


class _Rider:
    def __init__(self, kind, arrays, out_shapes, slots):
        self.kind, self.arrays, self.slots = kind, list(arrays), slots
        self.n = len(self.arrays)
        hbm = pl.BlockSpec(memory_space=pl.ANY)
        self.in_specs = [hbm] * self.n
        self.out_specs = [hbm] * self.n
        self.out_shape = [jax.ShapeDtypeStruct(tuple(s), a.dtype) for s, a in zip(out_shapes, self.arrays)]
        self.scratch = [pltpu.SemaphoreType.DMA((self.n, 7)), pltpu.SemaphoreType.DMA((self.n, 7)),
                        pltpu.SemaphoreType.DMA((self.n,))]

    def bind(self, in_refs, out_refs, sems):
        return self.kind(in_refs, out_refs, self.slots, *sems)


def _gather_rider(shards, full_shapes, slots):
    return _Rider(_Gather, shards, full_shapes, slots)


def _exchange_rider(sends, part_shapes, slots):
    return _Rider(_Exchange, sends, [(N_DEV,) + tuple(s) for s in part_shapes], slots)


def _split_refs(refs, n_in, n_out, n_scratch, rider):
    k = rider.n if rider is not None else 0
    ins, r_ins = refs[:n_in], refs[n_in:n_in + k]
    outs, r_outs = refs[n_in + k:n_in + k + n_out], refs[n_in + k + n_out:n_in + 2 * k + n_out]
    rest = refs[n_in + 2 * k + n_out:]
    scratch, sems = rest[:n_scratch], rest[n_scratch:]
    comm = rider.bind(r_ins, r_outs, sems) if rider is not None else None
    return ins + outs + scratch, comm


def _ride_before(comm, i, nt):
    if comm is not None:
        pl.when(i == 0)(comm.start)
        pl.when(i == nt - 1)(comm.forward)


def _ride_after(comm, i, nt):
    if comm is not None:
        pl.when(i == nt - 1)(comm.finish)


def _extend(specs, rider, field):
    return list(specs) + (getattr(rider, field) if rider is not None else [])


def _comm_call(name, rider):
    def body(*refs):
        _, comm = _split_refs(refs, 0, 0, 0, rider)
        comm.start()
        comm.forward()
        comm.finish()

    return pl.pallas_call(body, name=name, in_specs=rider.in_specs, out_specs=rider.out_specs,
                          out_shape=rider.out_shape, scratch_shapes=rider.scratch)(*rider.arrays)


def _adamw(w, g, m, v):
    m = ADAM_B1 * m + (1.0 - ADAM_B1) * g
    v = ADAM_B2 * v + (1.0 - ADAM_B2) * (g * g)
    m_hat = m / (1.0 - ADAM_B1 ** ADAM_STEP)
    v_hat = v / (1.0 - ADAM_B2 ** ADAM_STEP)
    delta = -ADAM_LR * (m_hat / (jnp.sqrt(v_hat) + ADAM_EPS) + ADAM_WD * w)
    return delta, m, v


def _sum_parts(parts_ref, index=()):
    g = parts_ref[(0,) + index].astype(F32)
    for s in range(1, N_DEV):
        g = g + parts_ref[(s,) + index].astype(F32)
    return g


def _adamw_call(name, parts, w, m, v, block_rows):
    rows, cols = w.shape
    nb = rows // block_rows

    def body(parts_ref, w_ref, m_ref, v_ref, g_ref, delta_ref, m_out, v_out):
        g = _sum_parts(parts_ref)
        delta, m_new, v_new = _adamw(w_ref[...], g, m_ref[...], v_ref[...])
        g_ref[...] = g
        delta_ref[...] = delta
        m_out[...] = m_new
        v_out[...] = v_new

    blk = pl.BlockSpec((block_rows, cols), lambda i: (i, 0))
    return pl.pallas_call(
        body, name=name, grid=(nb,),
        in_specs=[pl.BlockSpec((N_DEV, block_rows, cols), lambda i: (0, i, 0)), blk, blk, blk],
        out_specs=[blk, blk, blk, blk],
        out_shape=[jax.ShapeDtypeStruct((rows, cols), F32)] * 4,
        compiler_params=_params(("parallel",)),
    )(parts, w, m, v)


def _adamw_cols_call(name, parts_a, parts_b, w, m, v, block_cols):
    rows, cols = w.shape
    na, nb = parts_a.shape[2] // block_cols, parts_b.shape[2] // block_cols

    def body(pa_ref, pb_ref, w_ref, m_ref, v_ref, g_ref, delta_ref, m_out, v_out):
        def update(parts_ref):
            g = _sum_parts(parts_ref)
            delta, m_new, v_new = _adamw(w_ref[...], g, m_ref[...], v_ref[...])
            g_ref[...] = g
            delta_ref[...] = delta
            m_out[...] = m_new
            v_out[...] = v_new

        pl.when(pl.program_id(0) < na)(functools.partial(update, pa_ref))
        pl.when(pl.program_id(0) >= na)(functools.partial(update, pb_ref))

    blk = pl.BlockSpec((rows, block_cols), lambda i: (0, i))
    return pl.pallas_call(
        body, name=name, grid=(na + nb,),
        in_specs=[pl.BlockSpec((N_DEV, rows, block_cols), lambda i: (0, 0, jnp.minimum(i, na - 1))),
                  pl.BlockSpec((N_DEV, rows, block_cols), lambda i: (0, 0, jnp.maximum(i - na, 0))), blk, blk, blk],
        out_specs=[blk, blk, blk, blk],
        out_shape=[jax.ShapeDtypeStruct((rows, cols), F32)] * 4,
        compiler_params=_params(("arbitrary",)),
    )(parts_a, parts_b, w, m, v)


WIDE_ROWS = 8
NARROW_ROWS = 40
NARROW_GKW_ROW = 8
NARROW_GKB_ROW = 24
NARROW_HW_ROW = 32
GROUP_SHARD = POOL_GROUP_DIM // N_DEV
KEY_SHARD = GLA_KEY_WIDTH // N_DEV
HEAD_V_SHARD = GLA_HEAD_V // N_DEV


def _small_adamw_call(wide, narrow, w, m, v):
    names = ("norm_w", "pool_scale", "final_norm_w", "pool_group_b", "gla_gk_w", "gla_gk_b", "gla_head_norm_w")
    where = {
        "norm_w": (0, slice(0, 2), slice(None)),
        "pool_scale": (0, slice(2, 3), slice(None)),
        "final_norm_w": (0, slice(3, 4), slice(None)),
        "pool_group_b": (1, slice(0, POOL_GROUPS), slice(0, GROUP_SHARD)),
        "gla_gk_w": (1, slice(NARROW_GKW_ROW, NARROW_GKW_ROW + GLA_GATE_RANK), slice(0, KEY_SHARD)),
        "gla_gk_b": (1, slice(NARROW_GKB_ROW, NARROW_GKB_ROW + 1), slice(0, KEY_SHARD)),
        "gla_head_norm_w": (1, slice(NARROW_HW_ROW, NARROW_HW_ROW + 1), slice(0, HEAD_V_SHARD)),
    }
    k = len(names)

    def body(*refs):
        parts = refs[0:2]
        w_refs, m_refs, v_refs = refs[2:2 + k], refs[2 + k:2 + 2 * k], refs[2 + 2 * k:2 + 3 * k]
        outs = refs[2 + 3 * k:]
        loss_ref = outs[0]
        loss_ref[...] = _sum_parts(parts[0], (slice(4, 5), slice(0, 1)))
        for i, name in enumerate(names):
            buf, rows, cols = where[name]
            g = _sum_parts(parts[buf], (rows, cols))
            delta, m_new, v_new = _adamw(w_refs[i][...], g, m_refs[i][...], v_refs[i][...])
            outs[1 + i][...] = g
            outs[1 + k + i][...] = delta
            outs[1 + 2 * k + i][...] = m_new
            outs[1 + 3 * k + i][...] = v_new

    vmem = pl.BlockSpec(memory_space=pltpu.VMEM)
    shapes = [jax.ShapeDtypeStruct(w[n].shape, F32) for n in names]
    res = pl.pallas_call(
        body, name="adamw_small", in_specs=[vmem] * (2 + 3 * k), out_specs=[vmem] * (1 + 4 * k),
        out_shape=[jax.ShapeDtypeStruct((1, 1), F32)] + shapes * 4,
    )(wide, narrow, *[w[n] for n in names], *[m[n] for n in names], *[v[n] for n in names])
    unzip = lambda j: dict(zip(names, res[1 + j * k:1 + (j + 1) * k]))
    return res[0], unzip(0), unzip(1), unzip(2), unzip(3)


def kernel(x, norm_w, pool_in_w, pool_group_w, pool_group_b, pool_scale, pool_out_w, gla_in_w, gla_gk_w, gla_gk_b, gla_head_norm_w, gla_out_w, final_norm_w, loss_target, m_norm_w, m_pool_in_w, m_pool_group_w, m_pool_group_b, m_pool_scale, m_pool_out_w, m_gla_in_w, m_gla_gk_w, m_gla_gk_b, m_gla_head_norm_w, m_gla_out_w, m_final_norm_w, v_norm_w, v_pool_in_w, v_pool_group_w, v_pool_group_b, v_pool_scale, v_pool_out_w, v_gla_in_w, v_gla_gk_w, v_gla_gk_b, v_gla_head_norm_w, v_gla_out_w, v_final_norm_w):
    w = dict(norm_w=norm_w, pool_in_w=pool_in_w, pool_group_w=pool_group_w, pool_group_b=pool_group_b,
             pool_scale=pool_scale, pool_out_w=pool_out_w, gla_in_w=gla_in_w, gla_gk_w=gla_gk_w, gla_gk_b=gla_gk_b,
             gla_head_norm_w=gla_head_norm_w, gla_out_w=gla_out_w, final_norm_w=final_norm_w)
    m = dict(norm_w=m_norm_w, pool_in_w=m_pool_in_w, pool_group_w=m_pool_group_w, pool_group_b=m_pool_group_b,
             pool_scale=m_pool_scale, pool_out_w=m_pool_out_w, gla_in_w=m_gla_in_w, gla_gk_w=m_gla_gk_w,
             gla_gk_b=m_gla_gk_b, gla_head_norm_w=m_gla_head_norm_w, gla_out_w=m_gla_out_w,
             final_norm_w=m_final_norm_w)
    v = dict(norm_w=v_norm_w, pool_in_w=v_pool_in_w, pool_group_w=v_pool_group_w, pool_group_b=v_pool_group_b,
             pool_scale=v_pool_scale, pool_out_w=v_pool_out_w, gla_in_w=v_gla_in_w, gla_gk_w=v_gla_gk_w,
             gla_gk_b=v_gla_gk_b, gla_head_norm_w=v_gla_head_norm_w, gla_out_w=v_gla_out_w,
             final_norm_w=v_final_norm_w)
    col_shard = GLA_IN_WIDTH // N_DEV
    row_shard = D_MODEL // N_DEV

    def lanes(a):
        return jnp.pad(a, [(0, 0)] * (a.ndim - 1) + [(0, LANES - a.shape[-1])])

    small_in = jnp.concatenate([lanes(pool_group_b[0]), lanes(gla_gk_b), lanes(gla_head_norm_w),
                                jnp.zeros((2, LANES), F32)], axis=0)
    in_cols = 2 * POOL_WIDTH // N_DEV
    pool_in, pool_gw, pool_out, small_all = _comm_call("pool_weights_all_gather", _gather_rider(
        [pool_in_w[0].astype(BF16), pool_group_w[0].astype(BF16), pool_out_w[0].astype(BF16), small_in],
        [(D_MODEL, 2 * POOL_WIDTH), (POOL_GROUPS, POOL_GROUP_DIM, POOL_GROUP_DIM), (POOL_WIDTH, D_MODEL),
         (N_DEV, 8, LANES)],
        [_dim1_slot(in_cols), _dim1_slot(GROUP_SHARD), _row_slot(row_shard), _lead_slot]))
    pool_gb = jnp.transpose(small_all[:, 0:POOL_GROUPS, :GROUP_SHARD], (1, 0, 2)).reshape(1, POOL_WIDTH)
    gla_gkb = small_all[:, POOL_GROUPS, :KEY_SHARD].reshape(1, GLA_KEY_WIDTH)
    gla_hw = jnp.tile(small_all[:, POOL_GROUPS + 1, :HEAD_V_SHARD].reshape(1, GLA_HEAD_V), (1, GLA_HEADS))
    nw0, nw1, wf = norm_w[0:1], norm_w[1:2], final_norm_w.reshape(1, D_MODEL)
    xs, target = x[0], loss_target[0]

    h1, p, gla_in_parts, gkw_parts, gla_out = _pool_fwd_call(
        xs, nw0, pool_in, pool_gw, pool_gb, pool_scale, pool_out, _gather_rider(
            [jnp.transpose(gla_in_w[0]).astype(BF16), gla_gk_w[0].astype(BF16), gla_out_w[0].astype(BF16)],
            [(N_DEV, col_shard, D_MODEL), (N_DEV, GLA_GATE_RANK, KEY_SHARD), (GLA_VALUE_WIDTH, D_MODEL)],
            [_lead_slot, _lead_slot, _row_slot(row_shard)]))
    gla_in = jnp.pad(gla_in_parts.reshape(GLA_IN_WIDTH, D_MODEL), ((0, GLA_IN_PAD - GLA_IN_WIDTH), (0, 0)))
    gla_gkw = jnp.pad(jnp.transpose(gkw_parts, (1, 0, 2)).reshape(GLA_GATE_RANK, GLA_KEY_WIDTH),
                      ((0, GLA_LOW_PAD - GLA_GATE_RANK), (0, 0)))
    dh2, proj, o, states, loss_part, dwf = _gla_fwd_call(h1, nw1, gla_in, gla_gkw, gla_gkb, gla_hw, gla_out, wf, target)

    dproj, d_gla_out, dhw, dgkw, dgkb = _gla_bwd_call(dh2, proj, o, states, gla_gkw, gla_gkb, gla_hw, gla_out)
    dh1, d_gla_in, dnw1, landed_gla_out = _inproj_bwd_call(
        "gla_in_bwd", dproj, h1, nw1, gla_in, dh2,
        _exchange_rider([d_gla_out], [(row_shard, D_MODEL)], [_row_slot(row_shard)]), transposed=True)
    gla_in_send = d_gla_in[:GLA_IN_WIDTH].reshape(N_DEV, col_shard, D_MODEL)
    cols_a = GLA_IN_COLS_WITH_POOL_BWD
    dp, d_pool_out, dgw, dgb, dsc, landed_gla_in_a = _pool_bwd_call(
        dh1, p, pool_gw, pool_gb, pool_scale, pool_out,
        _exchange_rider([gla_in_send], [(col_shard, cols_a)], [lambda ref, d: ref.at[d, :, pl.ds(0, cols_a)]]))
    grad_x, d_pool_in, dnw0, landed_gla_in_b, landed_pool_out, landed_gw = _inproj_bwd_call(
        "pool_in_bwd", dp, xs, nw0, pool_in, dh1,
        _exchange_rider([gla_in_send, d_pool_out, dgw],
                        [(col_shard, D_MODEL - cols_a), (row_shard, D_MODEL),
                         (POOL_GROUPS, GROUP_SHARD, POOL_GROUP_DIM)],
                        [lambda ref, d: ref.at[d, :, pl.ds(cols_a, D_MODEL - cols_a)], _row_slot(row_shard),
                         _dim1_slot(GROUP_SHARD)]))

    wide = jnp.concatenate([
        dnw0, dnw1, dsc, dwf, jnp.pad(loss_part[0:1, 0:1], ((0, 0), (0, D_MODEL - 1))),
        jnp.zeros((WIDE_ROWS - 5, D_MODEL), F32)], axis=0)

    def rows8(a):
        return jnp.pad(lanes(a), ((0, 0), (0, -a.shape[1] % 8), (0, 0)))

    narrow = jnp.concatenate([
        rows8(jnp.transpose(dgb.reshape(POOL_GROUPS, N_DEV, GROUP_SHARD), (1, 0, 2))),
        rows8(jnp.transpose(dgkw[:GLA_GATE_RANK].reshape(GLA_GATE_RANK, N_DEV, KEY_SHARD), (1, 0, 2))),
        rows8(dgkb.reshape(N_DEV, 1, KEY_SHARD)),
        rows8(dhw.reshape(GLA_HEADS, GLA_HEAD_V).sum(axis=0).reshape(N_DEV, 1, HEAD_V_SHARD)),
    ], axis=1)
    landed_pool_in, landed_wide, landed_narrow = _comm_call("grads_all_to_all", _exchange_rider(
        [d_pool_in, wide, narrow], [(D_MODEL, in_cols), (WIDE_ROWS, D_MODEL), (NARROW_ROWS, LANES)],
        [_dim1_slot(in_cols), lambda ref, d: ref, _lead_slot]))

    res = {}
    for name, parts, rows, cols, block in [
            ("pool_in_w", landed_pool_in, D_MODEL, in_cols, 256),
            ("pool_group_w", landed_gw, POOL_GROUPS * GROUP_SHARD, POOL_GROUP_DIM, 128),
            ("pool_out_w", landed_pool_out, row_shard, D_MODEL, 128),
            ("gla_out_w", landed_gla_out, row_shard, D_MODEL, 128)]:
        outs = _adamw_call("adamw_" + name, parts.reshape(N_DEV, rows, cols), w[name].reshape(rows, cols),
                           m[name].reshape(rows, cols), v[name].reshape(rows, cols), block)
        res[name] = [t.reshape(w[name].shape) for t in outs]
    outs = _adamw_cols_call("adamw_gla_in_w", landed_gla_in_a, landed_gla_in_b, jnp.transpose(gla_in_w[0]),
                            jnp.transpose(m_gla_in_w[0]), jnp.transpose(v_gla_in_w[0]), 256)
    res["gla_in_w"] = [jnp.transpose(t)[None] for t in outs]
    small_shapes = {"norm_w": (2, D_MODEL), "pool_scale": (1, D_MODEL), "final_norm_w": (1, D_MODEL),
                    "pool_group_b": (POOL_GROUPS, GROUP_SHARD), "gla_gk_w": (GLA_GATE_RANK, KEY_SHARD),
                    "gla_gk_b": (1, KEY_SHARD), "gla_head_norm_w": (1, HEAD_V_SHARD)}
    as_small = lambda t: {n: t[n].reshape(s) for n, s in small_shapes.items()}
    loss, *small_outs = _small_adamw_call(landed_wide, landed_narrow, as_small(w), as_small(m), as_small(v))
    for name in small_shapes:
        res[name] = [t[name].reshape(w[name].shape) for t in small_outs]
    order = ("norm_w", "pool_in_w", "pool_group_w", "pool_group_b", "pool_scale", "pool_out_w", "gla_in_w",
             "gla_gk_w", "gla_gk_b", "gla_head_norm_w", "gla_out_w", "final_norm_w")
    return (loss.reshape(()), grad_x[None], *[res[n][0] for n in order], *[res[n][1] for n in order],
            *[res[n][2] for n in order], *[res[n][3] for n in order])
```

```python
import functools

import jax
import jax.numpy as jnp
from jax import lax
from jax.experimental import pallas as pl
from jax.experimental.pallas import tpu as pltpu

F32 = jnp.float32
BF16 = jnp.bfloat16
MESH = pl.DeviceIdType.MESH

N_DEV = 8
D_MODEL = 1024
POOL_WIDTH = 1024
POOL_GROUPS = 4
POOL_GROUP_DIM = 256
POOL_HALO = 16
GLA_HEADS = 4
GLA_HEAD_K = 128
GLA_HEAD_V = 256
GLA_KEY_WIDTH = 512
GLA_VALUE_WIDTH = 1024
GLA_GATE_RANK = 16
GLA_IN_WIDTH = 3088
GLA_IN_PAD = 3200
GLA_LOW_PAD = 128
GLA_QKVG_WIDTH = 3072
CHUNK = 64
GATE_NORMALIZER = 16.0
RMS_EPS = 1e-6
Q_SCALE = GLA_HEAD_K ** -0.5

ADAM_LR = 0.001
ADAM_B1 = 0.9
ADAM_B2 = 0.999
ADAM_EPS = 1e-08
ADAM_WD = 0.01
ADAM_STEP = 10

LANES = 128
BF16_ROWS = 16
VMEM_LIMIT = 56 * 1024 * 1024
ROW_TILE = 256
MATMUL_ROW_TILE = 512
GLA_IN_COLS_WITH_POOL_BWD = 768


def _dot_nn(a, b):
    return lax.dot_general(a, b, (((1,), (0,)), ((), ())), preferred_element_type=F32)


def _dot_nt(a, b):
    return lax.dot_general(a, b, (((1,), (1,)), ((), ())), preferred_element_type=F32)


def _dot_tn(a, b):
    return lax.dot_general(a, b, (((0,), (0,)), ((), ())), preferred_element_type=F32)


def _rms(x):
    rstd = lax.rsqrt(jnp.mean(x * x, axis=-1, keepdims=True) + RMS_EPS)
    return x * rstd, rstd


def _rms_bwd(dxhat, xhat, rstd):
    return rstd * (dxhat - xhat * jnp.mean(dxhat * xhat, axis=-1, keepdims=True))


def _sigmoid(x):
    return 1.0 / (1.0 + jnp.exp(-x))


def _params(sem=("arbitrary",)):
    return pltpu.CompilerParams(dimension_semantics=sem, vmem_limit_bytes=VMEM_LIMIT)


def _full(shape):
    return pl.BlockSpec(shape, lambda i: (0,) * len(shape))


def _const(shape):
    return pl.BlockSpec(shape, lambda i: (0,) * len(shape), pipeline_mode=pl.Buffered(1))


def _window_sums(ext, forward):
    n = ext.shape[0]
    outs = []
    for g in range(POOL_GROUPS):
        s = ext[:, g * POOL_GROUP_DIM:(g + 1) * POOL_GROUP_DIM]
        for k in range(g + 1):
            shift = (1 << k) if forward else n - (1 << k)
            s = s + pltpu.roll(s, shift, axis=0)
        outs.append(s[:n - POOL_HALO])
    return outs


def _inv_count(row0, tm):
    row = row0 + lax.broadcasted_iota(jnp.int32, (tm, 1), 0)
    return [1.0 / jnp.minimum(row + 1, 2 << g).astype(F32) for g in range(POOL_GROUPS)]


def _pool_mix(u, u_prev, row0, gw_ref, gb):
    tm = u.shape[0]
    sums = _window_sums(jnp.concatenate([u, u_prev], axis=0), True)
    inv = _inv_count(row0, tm)
    pooled, mixed = [], []
    for g in range(POOL_GROUPS):
        ug = u[:, g * POOL_GROUP_DIM:(g + 1) * POOL_GROUP_DIM]
        pg = (sums[g] * inv[g] - ug).astype(BF16)
        pooled.append(pg)
        mixed.append(_dot_nn(pg, gw_ref[g]))
    return pooled, jnp.concatenate(mixed, axis=1) + gb


def _pool_fwd_call(x, nw, w_in, gw, gb, sc, w_out, rider=None):
    seq = x.shape[0]
    tm = min(MATMUL_ROW_TILE, seq)
    nt = seq // tm

    def main(x_ref, nw_ref, win_ref, gw_ref, gb_ref, sc_ref, wout_ref, h_ref, p_ref, halo_ref):
        i = pl.program_id(0)

        @pl.when(i == 0)
        def _():
            halo_ref[...] = jnp.zeros_like(halo_ref)

        xt = x_ref[...]
        xhat, _ = _rms(xt)
        n = (xhat * nw_ref[...]).astype(BF16)
        p = _dot_nn(n, win_ref[...])
        p_ref[...] = p
        u = p[:, :POOL_WIDTH]
        gate = p[:, POOL_WIDTH:]
        _, mixed = _pool_mix(u, halo_ref[...], i * tm, gw_ref, gb_ref[...])
        halo_ref[...] = u[tm - POOL_HALO:, :]
        y = (mixed * sc_ref[...] * (gate * _sigmoid(gate))).astype(BF16)
        h_ref[...] = xt + _dot_nn(y, wout_ref[...])

    def body(*refs):
        own, comm = _split_refs(refs, 7, 2, 1, rider)
        _ride_before(comm, pl.program_id(0), nt)
        main(*own)
        _ride_after(comm, pl.program_id(0), nt)

    return pl.pallas_call(
        body, name="pool_fwd", grid=(nt,),
        in_specs=_extend([pl.BlockSpec((tm, D_MODEL), lambda i: (i, 0)), _const((1, D_MODEL)),
                          _const((D_MODEL, 2 * POOL_WIDTH)), _const((POOL_GROUPS, POOL_GROUP_DIM, POOL_GROUP_DIM)),
                          _const((1, POOL_WIDTH)), _const((1, POOL_WIDTH)), _const((POOL_WIDTH, D_MODEL))],
                         rider, "in_specs"),
        out_specs=_extend([pl.BlockSpec((tm, D_MODEL), lambda i: (i, 0)),
                           pl.BlockSpec((tm, 2 * POOL_WIDTH), lambda i: (i, 0))], rider, "out_specs"),
        out_shape=_extend([jax.ShapeDtypeStruct((seq, D_MODEL), F32),
                           jax.ShapeDtypeStruct((seq, 2 * POOL_WIDTH), F32)], rider, "out_shape"),
        scratch_shapes=_extend([pltpu.VMEM((POOL_HALO, POOL_WIDTH), F32)], rider, "scratch"),
        compiler_params=_params(),
    )(x, nw, w_in, gw, gb, sc, w_out, *_extend([], rider, "arrays"))


def _pool_bwd_call(dh, p, gw, gb, sc, w_out, rider=None):
    seq = dh.shape[0]
    tm = min(MATMUL_ROW_TILE, seq)
    nt = seq // tm
    halo_blocks = tm // POOL_HALO

    def main(dh_ref, p_ref, pprev_ref, gw_ref, gb_ref, sc_ref, wout_ref,
             dp_ref, dwout_hbm, dgw_hbm, dgb_ref, dsc_ref, carry_ref, dwout_acc, dgw_acc, dwout_stage, dgw_stage):
        i = pl.program_id(0)
        t = nt - 1 - i

        @pl.when(i == 0)
        def _():
            carry_ref[...] = jnp.zeros_like(carry_ref)
            dwout_acc[...] = jnp.zeros_like(dwout_acc)
            dgw_acc[...] = jnp.zeros_like(dgw_acc)
            dgb_ref[...] = jnp.zeros_like(dgb_ref)
            dsc_ref[...] = jnp.zeros_like(dsc_ref)

        p = p_ref[...]
        u = p[:, :POOL_WIDTH]
        gate = p[:, POOL_WIDTH:]
        u_prev = jnp.where(t > 0, pprev_ref[:, :POOL_WIDTH], 0.0)
        pooled, mixed = _pool_mix(u, u_prev, t * tm, gw_ref, gb_ref[...])
        sg = _sigmoid(gate)
        silu = gate * sg
        sc = sc_ref[...]
        dhb = dh_ref[...].astype(BF16)
        y = (mixed * sc * silu).astype(BF16)
        dwout_acc[...] += _dot_tn(y, dhb)
        dy = _dot_nt(dhb, wout_ref[...])
        dmixed = dy * sc * silu
        dsc_ref[...] += jnp.sum(dy * mixed * silu, axis=0, keepdims=True)
        dgate = dy * mixed * sc * (sg * (1.0 + gate * (1.0 - sg)))
        dgb_ref[...] += jnp.sum(dmixed, axis=0, keepdims=True)
        inv = _inv_count(t * tm, tm)
        dpooled, scaled = [], []
        for g in range(POOL_GROUPS):
            dmg = dmixed[:, g * POOL_GROUP_DIM:(g + 1) * POOL_GROUP_DIM].astype(BF16)
            dgw_acc[g] += _dot_tn(pooled[g], dmg)
            dpg = _dot_nt(dmg, gw_ref[g])
            dpooled.append(dpg)
            scaled.append(dpg * inv[g])
        r = jnp.concatenate(scaled, axis=1)
        sums = _window_sums(jnp.concatenate([r, carry_ref[...]], axis=0), False)
        carry_ref[...] = r[:POOL_HALO, :]
        du = jnp.concatenate([sums[g] - dpooled[g] for g in range(POOL_GROUPS)], axis=1)
        dp_ref[...] = jnp.concatenate([du, dgate], axis=1).astype(BF16)

        @pl.when(i == nt - 1)
        def _():
            dwout_stage[...] = dwout_acc[...].astype(BF16)
            dgw_stage[...] = dgw_acc[...].astype(BF16)
            pltpu.sync_copy(dwout_stage, dwout_hbm)
            pltpu.sync_copy(dgw_stage, dgw_hbm)

    def body(*refs):
        own, comm = _split_refs(refs, 7, 5, 5, rider)
        _ride_before(comm, pl.program_id(0), nt)
        main(*own)
        _ride_after(comm, pl.program_id(0), nt)

    rev = lambda i: (nt - 1 - i, 0)
    return pl.pallas_call(
        body, name="pool_bwd", grid=(nt,),
        in_specs=_extend([pl.BlockSpec((tm, D_MODEL), rev), pl.BlockSpec((tm, 2 * POOL_WIDTH), rev),
                          pl.BlockSpec((POOL_HALO, 2 * POOL_WIDTH),
                                       lambda i: (jnp.maximum((nt - 1 - i) * halo_blocks - 1, 0), 0)),
                          _const((POOL_GROUPS, POOL_GROUP_DIM, POOL_GROUP_DIM)), _const((1, POOL_WIDTH)),
                          _const((1, POOL_WIDTH)), _const((POOL_WIDTH, D_MODEL))], rider, "in_specs"),
        out_specs=_extend([pl.BlockSpec((tm, 2 * POOL_WIDTH), rev), pl.BlockSpec(memory_space=pl.ANY),
                           pl.BlockSpec(memory_space=pl.ANY), _full((1, POOL_WIDTH)), _full((1, POOL_WIDTH))],
                          rider, "out_specs"),
        out_shape=_extend([jax.ShapeDtypeStruct((seq, 2 * POOL_WIDTH), BF16),
                           jax.ShapeDtypeStruct((POOL_WIDTH, D_MODEL), BF16),
                           jax.ShapeDtypeStruct((POOL_GROUPS, POOL_GROUP_DIM, POOL_GROUP_DIM), BF16),
                           jax.ShapeDtypeStruct((1, POOL_WIDTH), F32), jax.ShapeDtypeStruct((1, POOL_WIDTH), F32)],
                          rider, "out_shape"),
        scratch_shapes=_extend([pltpu.VMEM((POOL_HALO, POOL_WIDTH), F32), pltpu.VMEM((POOL_WIDTH, D_MODEL), F32),
                                pltpu.VMEM((POOL_GROUPS, POOL_GROUP_DIM, POOL_GROUP_DIM), F32),
                                pltpu.VMEM((POOL_WIDTH, D_MODEL), BF16),
                                pltpu.VMEM((POOL_GROUPS, POOL_GROUP_DIM, POOL_GROUP_DIM), BF16)], rider, "scratch"),
        compiler_params=_params(),
    )(dh, p, p, gw, gb, sc, w_out, *_extend([], rider, "arrays"))


def _inproj_bwd_call(name, dproj, h_in, nw, w_in, dres, rider=None, transposed=False):
    seq = h_in.shape[0]
    width = dproj.shape[1]
    w_shape = tuple(w_in.shape)
    tm = min(MATMUL_ROW_TILE, seq)
    nt = seq // tm

    def main(dproj_ref, h_ref, nw_ref, win_ref, dres_ref, dh_ref, dw_hbm, dnw_ref, dw_acc, dw_stage):
        i = pl.program_id(0)

        @pl.when(i == 0)
        def _():
            dw_acc[...] = jnp.zeros_like(dw_acc)
            dnw_ref[...] = jnp.zeros_like(dnw_ref)

        xhat, rstd = _rms(h_ref[...])
        nw_row = nw_ref[...]
        n = (xhat * nw_row).astype(BF16)
        dpb = dproj_ref[...]
        if transposed:
            dw_acc[...] += _dot_tn(dpb, n)
            dn = _dot_nn(dpb, win_ref[...])
        else:
            dw_acc[...] += _dot_tn(n, dpb)
            dn = _dot_nt(dpb, win_ref[...])
        dnw_ref[...] += jnp.sum(dn * xhat, axis=0, keepdims=True)
        dh_ref[...] = _rms_bwd(dn * nw_row, xhat, rstd) + dres_ref[...]

        @pl.when(i == nt - 1)
        def _():
            dw_stage[...] = dw_acc[...].astype(BF16)
            pltpu.sync_copy(dw_stage, dw_hbm)

    def body(*refs):
        own, comm = _split_refs(refs, 5, 3, 2, rider)
        _ride_before(comm, pl.program_id(0), nt)
        main(*own)
        _ride_after(comm, pl.program_id(0), nt)

    row = lambda i: (i, 0)
    return pl.pallas_call(
        body, name=name, grid=(nt,),
        in_specs=_extend([pl.BlockSpec((tm, width), row), pl.BlockSpec((tm, D_MODEL), row), _const((1, D_MODEL)),
                          _const(w_shape), pl.BlockSpec((tm, D_MODEL), row)], rider, "in_specs"),
        out_specs=_extend([pl.BlockSpec((tm, D_MODEL), row), pl.BlockSpec(memory_space=pl.ANY),
                           _full((1, D_MODEL))], rider, "out_specs"),
        out_shape=_extend([jax.ShapeDtypeStruct((seq, D_MODEL), F32), jax.ShapeDtypeStruct(w_shape, BF16),
                           jax.ShapeDtypeStruct((1, D_MODEL), F32)], rider, "out_shape"),
        scratch_shapes=_extend([pltpu.VMEM(w_shape, F32), pltpu.VMEM(w_shape, BF16)], rider, "scratch"),
        compiler_params=_params(),
    )(dproj, h_in, nw, w_in, dres, *_extend([], rider, "arrays"))


def _chunk_scan(x, reverse):
    n = x.shape[0]
    pos = lax.broadcasted_iota(jnp.int32, (n, 1), 0) & (CHUNK - 1)
    k = 1
    while k < CHUNK:
        if reverse:
            x = x + jnp.where(pos < CHUNK - k, pltpu.roll(x, n - k, axis=0), 0.0)
        else:
            x = x + jnp.where(pos >= k, pltpu.roll(x, k, axis=0), 0.0)
        k *= 2
    return x


class _GlaTile:
    def __init__(self, q, k, v, gate, low, gkw_ref, gkb):
        tm = q.shape[0]
        self.q = q * Q_SCALE
        self.k, self.v, self.gate = k, v, gate
        self.low_b = low.astype(BF16)
        self.z = _dot_nn(self.low_b, gkw_ref[...]) + gkb
        log_g = (jnp.minimum(self.z, 0.0) - jnp.log(1.0 + jnp.exp(-jnp.abs(self.z)))) / GATE_NORMALIZER
        self.c = _chunk_scan(log_g, False)
        is_last = lax.broadcasted_iota(jnp.int32, (CHUNK, 1), 0) == CHUNK - 1
        last = [jnp.sum(jnp.where(is_last, self.c[j * CHUNK:(j + 1) * CHUNK, :], 0.0), axis=0, keepdims=True)
                for j in range(tm // CHUNK)]
        self.c_last = last
        c_last_rows = jnp.concatenate([jnp.broadcast_to(r, (CHUNK, GLA_KEY_WIDTH)) for r in last], axis=0)
        self.e_pos = jnp.exp(self.c)
        self.e_neg = jnp.exp(-self.c)
        self.e_rest = jnp.exp(c_last_rows - self.c)
        self.a = self.q * self.e_pos
        self.b = self.k * self.e_neg
        self.cn = self.q * self.e_neg
        self.dp = self.k * self.e_pos
        self.kd = self.k * self.e_rest
        self.a_b, self.b_b, self.cn_b, self.dp_b, self.kd_b, self.v_b = (
            t.astype(BF16) for t in (self.a, self.b, self.cn, self.dp, self.kd, self.v))
        idx_t = lax.broadcasted_iota(jnp.int32, (tm, tm), 0)
        idx_s = lax.broadcasted_iota(jnp.int32, (tm, tm), 1)
        same_chunk = (idx_t ^ idx_s) < CHUNK
        self.lower = same_chunk & (idx_t >= idx_s)
        self.upper = same_chunk & (idx_t < idx_s)

    @staticmethod
    def rows(j):
        return slice(j * CHUNK, (j + 1) * CHUNK)

    @staticmethod
    def kcols(h):
        return slice(h * GLA_HEAD_K, (h + 1) * GLA_HEAD_K)

    @staticmethod
    def vcols(h):
        return slice(h * GLA_HEAD_V, (h + 1) * GLA_HEAD_V)

    def scores(self, h):
        kc = self.kcols(h)
        fwd = _dot_nt(self.a_b[:, kc], self.b_b[:, kc])
        bwd = _dot_nt(self.cn_b[:, kc], self.dp_b[:, kc])
        return jnp.where(self.lower, fwd, jnp.where(self.upper, bwd, 0.0)).astype(BF16)


def _gla_fwd_call(h1, nw, w_in, gkw, gkb, hw, w_out, wf, target):
    seq = h1.shape[0]
    tm = ROW_TILE
    nt = seq // tm
    cpt = tm // CHUNK
    n_chunks = seq // CHUNK

    def body(h_ref, nw_ref, win_ref, gkw_ref, gkb_ref, hw_ref, wout_ref, wf_ref, tgt_ref,
             dh2_ref, proj_ref, o_ref, st_ref, loss_ref, dwf_ref, state_ref):
        i = pl.program_id(0)

        @pl.when(i == 0)
        def _():
            state_ref[...] = jnp.zeros_like(state_ref)
            loss_ref[...] = jnp.zeros_like(loss_ref)
            dwf_ref[...] = jnp.zeros_like(dwf_ref)

        ht = h_ref[...]
        xhat, _ = _rms(ht)
        n = (xhat * nw_ref[...]).astype(BF16)
        sections = {}
        for name, lo, hi in (("low", GLA_QKVG_WIDTH, GLA_IN_PAD), ("qk", 0, 2 * GLA_KEY_WIDTH),
                             ("v", 2 * GLA_KEY_WIDTH, GLA_QKVG_WIDTH - GLA_VALUE_WIDTH),
                             ("gate", GLA_QKVG_WIDTH - GLA_VALUE_WIDTH, GLA_QKVG_WIDTH)):
            sections[name] = _dot_nt(n, win_ref[lo:hi, :])
            proj_ref[:, lo:hi] = sections[name]
        g = _GlaTile(sections["qk"][:, :GLA_KEY_WIDTH], sections["qk"][:, GLA_KEY_WIDTH:], sections["v"],
                     sections["gate"], sections["low"], gkw_ref, gkb_ref[...])
        o_heads = []
        for h in range(GLA_HEADS):
            kc, vc = g.kcols(h), g.vcols(h)
            srows = slice(h * GLA_HEAD_V, (h + 1) * GLA_HEAD_V)
            o_intra = _dot_nn(g.scores(h), g.v_b[:, vc])
            state = state_ref[srows, :]
            o_rows = []
            for j in range(cpt):
                r = g.rows(j)
                st_ref[j, srows, :] = state
                o_rows.append(o_intra[r] + _dot_nt(g.a_b[r, kc], state.astype(BF16)))
                decay = jnp.exp(g.c_last[j][:, kc])
                state = state * decay + _dot_tn(g.v_b[r, vc], g.kd_b[r, kc])
            state_ref[srows, :] = state
            o_heads.append(jnp.concatenate(o_rows, axis=0))
        o = jnp.concatenate(o_heads, axis=1)
        o_ref[...] = o
        hw_row = hw_ref[...]
        on = jnp.concatenate([_rms(o[:, g.vcols(h)])[0] for h in range(GLA_HEADS)], axis=1) * hw_row
        y = (on * (g.gate * _sigmoid(g.gate))).astype(BF16)
        h2 = ht + _dot_nn(y, wout_ref[...])
        xhat2, rstd2 = _rms(h2)
        wf_row = wf_ref[...]
        err = xhat2 * wf_row - tgt_ref[...]
        loss_ref[...] += 0.5 * jnp.sum(err * err) / D_MODEL
        dout = err * (1.0 / D_MODEL)
        dwf_ref[...] += jnp.sum(dout * xhat2, axis=0, keepdims=True)
        dh2_ref[...] = _rms_bwd(dout * wf_row, xhat2, rstd2)

    row = lambda i: (i, 0)
    return pl.pallas_call(
        body, name="gla_fwd", grid=(nt,),
        in_specs=[pl.BlockSpec((tm, D_MODEL), row), _const((1, D_MODEL)), _const((GLA_IN_PAD, D_MODEL)),
                  _const((GLA_LOW_PAD, GLA_KEY_WIDTH)), _const((1, GLA_KEY_WIDTH)), _const((1, GLA_VALUE_WIDTH)),
                  _const((GLA_VALUE_WIDTH, D_MODEL)), _const((1, D_MODEL)), pl.BlockSpec((tm, D_MODEL), row)],
        out_specs=[pl.BlockSpec((tm, D_MODEL), row), pl.BlockSpec((tm, GLA_IN_PAD), row),
                   pl.BlockSpec((tm, GLA_VALUE_WIDTH), row),
                   pl.BlockSpec((cpt, GLA_VALUE_WIDTH, GLA_HEAD_K), lambda i: (i, 0, 0)),
                   _full((8, LANES)), _full((1, D_MODEL))],
        out_shape=[jax.ShapeDtypeStruct((seq, D_MODEL), F32), jax.ShapeDtypeStruct((seq, GLA_IN_PAD), F32),
                   jax.ShapeDtypeStruct((seq, GLA_VALUE_WIDTH), F32),
                   jax.ShapeDtypeStruct((n_chunks, GLA_VALUE_WIDTH, GLA_HEAD_K), F32),
                   jax.ShapeDtypeStruct((8, LANES), F32), jax.ShapeDtypeStruct((1, D_MODEL), F32)],
        scratch_shapes=[pltpu.VMEM((GLA_VALUE_WIDTH, GLA_HEAD_K), F32)],
        compiler_params=_params(),
    )(h1, nw, w_in, gkw, gkb, hw, w_out, wf, target)


def _gla_bwd_call(dh2, proj, o, states, gkw, gkb, hw, w_out):
    seq = dh2.shape[0]
    tm = ROW_TILE
    nt = seq // tm
    cpt = tm // CHUNK

    def body(dh_ref, proj_ref, o_ref, st_ref, gkw_ref, gkb_ref, hw_ref, wout_ref,
             dproj_ref, dwout_hbm, dhw_ref, dgkw_ref, dgkb_ref, dstate_ref, dwout_acc, dwout_stage):
        i = pl.program_id(0)

        @pl.when(i == 0)
        def _():
            dstate_ref[...] = jnp.zeros_like(dstate_ref)
            dwout_acc[...] = jnp.zeros_like(dwout_acc)
            dhw_ref[...] = jnp.zeros_like(dhw_ref)
            dgkw_ref[...] = jnp.zeros_like(dgkw_ref)
            dgkb_ref[...] = jnp.zeros_like(dgkb_ref)

        g = _GlaTile(proj_ref[:, :GLA_KEY_WIDTH], proj_ref[:, GLA_KEY_WIDTH:2 * GLA_KEY_WIDTH],
                     proj_ref[:, 2 * GLA_KEY_WIDTH:GLA_QKVG_WIDTH - GLA_VALUE_WIDTH],
                     proj_ref[:, GLA_QKVG_WIDTH - GLA_VALUE_WIDTH:GLA_QKVG_WIDTH], proj_ref[:, GLA_QKVG_WIDTH:],
                     gkw_ref, gkb_ref[...])
        dhb = dh_ref[...].astype(BF16)
        o = o_ref[...]
        hw_row = hw_ref[...]
        dy = _dot_nt(dhb, wout_ref[...])
        sg = _sigmoid(g.gate)
        silu = g.gate * sg
        don = dy * silu
        on_parts, do_parts, dhw_parts = [], [], []
        for h in range(GLA_HEADS):
            vc = g.vcols(h)
            xh, rs = _rms(o[:, vc])
            on_parts.append(xh * hw_row[:, vc])
            dhw_parts.append(jnp.sum(don[:, vc] * xh, axis=0, keepdims=True))
            do_parts.append(_rms_bwd(don[:, vc] * hw_row[:, vc], xh, rs))
        on = jnp.concatenate(on_parts, axis=1)
        dwout_acc[...] += _dot_tn((on * silu).astype(BF16), dhb)
        dhw_ref[...] += jnp.concatenate(dhw_parts, axis=1)
        dgate = dy * on * (sg * (1.0 + g.gate * (1.0 - sg)))
        do_b = jnp.concatenate(do_parts, axis=1).astype(BF16)

        last_row = lax.broadcasted_iota(jnp.int32, (CHUNK, 1), 0) == CHUNK - 1
        dq_h, dk_h, dv_h, dc_h = [], [], [], []
        for h in range(GLA_HEADS):
            kc, vc = g.kcols(h), g.vcols(h)
            srows = slice(h * GLA_HEAD_V, (h + 1) * GLA_HEAD_V)
            scores = g.scores(h)
            dscores = _dot_nt(do_b[:, vc], g.v_b[:, vc])
            dfwd = jnp.where(g.lower, dscores, 0.0).astype(BF16)
            dbwd = jnp.where(g.upper, dscores, 0.0).astype(BF16)
            dv_intra = _dot_tn(scores, do_b[:, vc])
            da_intra = _dot_nn(dfwd, g.b_b[:, kc])
            db = _dot_tn(dfwd, g.a_b[:, kc])
            dcn = _dot_nn(dbwd, g.dp_b[:, kc])
            ddp = _dot_tn(dbwd, g.cn_b[:, kc])
            dstate = dstate_ref[srows, :]
            da_rows, dkd_rows, dv_rows, dcl_rows = [None] * cpt, [None] * cpt, [None] * cpt, [None] * cpt
            for j in reversed(range(cpt)):
                r = g.rows(j)
                state = st_ref[j, srows, :]
                dstate_b = dstate.astype(BF16)
                do_c = do_b[r, vc]
                dv_rows[j] = dv_intra[r] + _dot_nt(g.kd_b[r, kc], dstate_b)
                da_rows[j] = da_intra[r] + _dot_nn(do_c, state.astype(BF16))
                dkd = _dot_nn(g.v_b[r, vc], dstate_b)
                dkd_rows[j] = dkd
                decay = jnp.exp(g.c_last[j][:, kc])
                dc_last = (jnp.sum(dkd * g.kd[r, kc], axis=0, keepdims=True)
                           + decay * jnp.sum(state * dstate, axis=0, keepdims=True))
                dcl_rows[j] = jnp.where(last_row, dc_last, 0.0)
                dstate = _dot_tn(do_c, g.a_b[r, kc]) + dstate * decay
            dstate_ref[srows, :] = dstate
            da = jnp.concatenate(da_rows, axis=0)
            dkd = jnp.concatenate(dkd_rows, axis=0)
            dv_h.append(jnp.concatenate(dv_rows, axis=0))
            dq_h.append(Q_SCALE * (da * g.e_pos[:, kc] + dcn * g.e_neg[:, kc]))
            dk_h.append(db * g.e_neg[:, kc] + ddp * g.e_pos[:, kc] + dkd * g.e_rest[:, kc])
            dc_h.append(da * g.a[:, kc] - db * g.b[:, kc] - dcn * g.cn[:, kc] + ddp * g.dp[:, kc]
                        - dkd * g.kd[:, kc] + jnp.concatenate(dcl_rows, axis=0))
        dq = jnp.concatenate(dq_h, axis=1)
        dk = jnp.concatenate(dk_h, axis=1)
        dv = jnp.concatenate(dv_h, axis=1)
        dlog_g = _chunk_scan(jnp.concatenate(dc_h, axis=1), True)
        dz = dlog_g * (1.0 / GATE_NORMALIZER) * (1.0 - _sigmoid(g.z))
        dzb = dz.astype(BF16)
        dgkb_ref[...] += jnp.sum(dz, axis=0, keepdims=True)
        dgkw_ref[...] += _dot_tn(g.low_b, dzb)
        dlow = _dot_nt(dzb, gkw_ref[...])
        dproj_ref[...] = jnp.concatenate([dq, dk, dv, dgate, dlow], axis=1).astype(BF16)

        @pl.when(i == nt - 1)
        def _():
            dwout_stage[...] = dwout_acc[...].astype(BF16)
            pltpu.sync_copy(dwout_stage, dwout_hbm)

    rev = lambda i: (nt - 1 - i, 0)
    return pl.pallas_call(
        body, name="gla_bwd", grid=(nt,),
        in_specs=[pl.BlockSpec((tm, D_MODEL), rev), pl.BlockSpec((tm, GLA_IN_PAD), rev),
                  pl.BlockSpec((tm, GLA_VALUE_WIDTH), rev),
                  pl.BlockSpec((cpt, GLA_VALUE_WIDTH, GLA_HEAD_K), lambda i: (nt - 1 - i, 0, 0)),
                  _const((GLA_LOW_PAD, GLA_KEY_WIDTH)), _const((1, GLA_KEY_WIDTH)), _const((1, GLA_VALUE_WIDTH)),
                  _const((GLA_VALUE_WIDTH, D_MODEL))],
        out_specs=[pl.BlockSpec((tm, GLA_IN_PAD), rev), pl.BlockSpec(memory_space=pl.ANY),
                   _full((1, GLA_VALUE_WIDTH)), _full((GLA_LOW_PAD, GLA_KEY_WIDTH)), _full((1, GLA_KEY_WIDTH))],
        out_shape=[jax.ShapeDtypeStruct((seq, GLA_IN_PAD), BF16), jax.ShapeDtypeStruct((GLA_VALUE_WIDTH, D_MODEL), BF16),
                   jax.ShapeDtypeStruct((1, GLA_VALUE_WIDTH), F32), jax.ShapeDtypeStruct((GLA_LOW_PAD, GLA_KEY_WIDTH), F32),
                   jax.ShapeDtypeStruct((1, GLA_KEY_WIDTH), F32)],
        scratch_shapes=[pltpu.VMEM((GLA_VALUE_WIDTH, GLA_HEAD_K), F32), pltpu.VMEM((GLA_VALUE_WIDTH, D_MODEL), F32),
                        pltpu.VMEM((GLA_VALUE_WIDTH, D_MODEL), BF16)],
        compiler_params=_params(),
    )(dh2, proj, o, states, gkw, gkb, hw, w_out)


def _position():
    return lax.axis_index("x"), lax.axis_index("y"), lax.axis_index("c")


def _lead_slot(ref, d):
    return ref.at[d]


def _row_slot(rows):
    return lambda ref, d: ref.at[pl.ds(pl.multiple_of(d * rows, rows), rows)]


def _dim1_slot(size):
    return lambda ref, d: ref.at[:, pl.ds(pl.multiple_of(d * size, size), size)]


class _Gather:
    def __init__(self, in_refs, out_refs, slots, send_sems, recv_sems, local_sems):
        self.in_refs, self.out_refs, self.slots = in_refs, out_refs, slots
        self.send_sems, self.recv_sems, self.local_sems = send_sems, recv_sems, local_sems
        self.n = len(in_refs)
        x, y, c = _position()
        self.c = c
        self.me, self.sibling = (x, y, c), (x, y, 1 - c)
        self.chips = [(1 - x, y), (x, 1 - y), (1 - x, 1 - y)]

    def _copy(self, a, k, block, to, from_input=False):
        part = self.slots[a](self.out_refs[a], 4 * block[0] + 2 * block[1] + block[2])
        return pltpu.make_async_remote_copy(
            src_ref=self.in_refs[a] if from_input else part, dst_ref=part,
            send_sem=self.send_sems.at[a, k], recv_sem=self.recv_sems.at[a, k], device_id=to, device_id_type=MESH)

    def _mine(self):
        return [pltpu.make_async_copy(self.in_refs[a], self.slots[a](self.out_refs[a], 4 * self.me[0] + 2 * self.me[1]
                                                                    + self.me[2]), self.local_sems.at[a])
                for a in range(self.n)]

    def _first(self):
        first = [self._copy(a, 0, self.me, self.sibling, True) for a in range(self.n)]
        return first + [self._copy(a, 1 + j, self.me, (*chip, self.c), True)
                        for j, chip in enumerate(self.chips) for a in range(self.n)]

    def _passed(self):
        return [self._copy(a, 4 + j, (*chip, self.c), self.sibling)
                for j, chip in enumerate(self.chips) for a in range(self.n)]

    def start(self):
        for cp in self._mine() + self._first():
            cp.start()

    def forward(self):
        passed = self._passed()
        for j, chip in enumerate(self.chips):
            for a in range(self.n):
                self._copy(a, 1 + j, (*chip, self.c), self.me).wait_recv()
                passed[j * self.n + a].start()

    def finish(self):
        for a in range(self.n):
            self._copy(a, 0, self.sibling, self.me).wait_recv()
        for j, chip in enumerate(self.chips):
            for a in range(self.n):
                self._copy(a, 4 + j, (*chip, 1 - self.c), self.me).wait_recv()
        for cp in self._first() + self._passed():
            cp.wait_send()
        for cp in self._mine():
            cp.wait()


class _Exchange:
    def __init__(self, in_refs, out_refs, slots, send_sems, recv_sems, local_sems):
        self.in_refs, self.out_refs, self.slots = in_refs, out_refs, slots
        self.send_sems, self.recv_sems, self.local_sems = send_sems, recv_sems, local_sems
        self.n = len(in_refs)
        self.pos = _position()

    def _copies(self):
        x, y, c = self.pos
        me = 4 * x + 2 * y + c
        mine = [pltpu.make_async_copy(self.slots[a](self.in_refs[a], me), self.out_refs[a].at[me],
                                      self.local_sems.at[a]) for a in range(self.n)]
        remote = []
        for k in range(1, N_DEV):
            px, py, pc = x ^ (k >> 2), y ^ ((k >> 1) & 1), c ^ (k & 1)
            for a in range(self.n):
                remote.append(pltpu.make_async_remote_copy(
                    src_ref=self.slots[a](self.in_refs[a], 4 * px + 2 * py + pc), dst_ref=self.out_refs[a].at[me],
                    send_sem=self.send_sems.at[a, k - 1], recv_sem=self.recv_sems.at[a, k - 1],
                    device_id=(px, py, pc), device_id_type=MESH))
        return mine, remote

    def start(self):
        mine, remote = self._copies()
        for cp in mine + remote:
            cp.start()

    def forward(self):
        pass

    def finish(self):
        mine, remote = self._copies()
        for cp in remote:
            cp.wait_recv()
        for cp in remote:
            cp.wait_send()
        for cp in mine:
            cp.wait()


class _Rider:
    def __init__(self, kind, arrays, out_shapes, slots):
        self.kind, self.arrays, self.slots = kind, list(arrays), slots
        self.n = len(self.arrays)
        hbm = pl.BlockSpec(memory_space=pl.ANY)
        self.in_specs = [hbm] * self.n
        self.out_specs = [hbm] * self.n
        self.out_shape = [jax.ShapeDtypeStruct(tuple(s), a.dtype) for s, a in zip(out_shapes, self.arrays)]
        self.scratch = [pltpu.SemaphoreType.DMA((self.n, 7)), pltpu.SemaphoreType.DMA((self.n, 7)),
                        pltpu.SemaphoreType.DMA((self.n,))]

    def bind(self, in_refs, out_refs, sems):
        return self.kind(in_refs, out_refs, self.slots, *sems)


def _gather_rider(shards, full_shapes, slots):
    return _Rider(_Gather, shards, full_shapes, slots)


def _exchange_rider(sends, part_shapes, slots):
    return _Rider(_Exchange, sends, [(N_DEV,) + tuple(s) for s in part_shapes], slots)


def _split_refs(refs, n_in, n_out, n_scratch, rider):
    k = rider.n if rider is not None else 0
    ins, r_ins = refs[:n_in], refs[n_in:n_in + k]
    outs, r_outs = refs[n_in + k:n_in + k + n_out], refs[n_in + k + n_out:n_in + 2 * k + n_out]
    rest = refs[n_in + 2 * k + n_out:]
    scratch, sems = rest[:n_scratch], rest[n_scratch:]
    comm = rider.bind(r_ins, r_outs, sems) if rider is not None else None
    return ins + outs + scratch, comm


def _ride_before(comm, i, nt):
    if comm is not None:
        pl.when(i == 0)(comm.start)
        pl.when(i == nt - 1)(comm.forward)


def _ride_after(comm, i, nt):
    if comm is not None:
        pl.when(i == nt - 1)(comm.finish)


def _extend(specs, rider, field):
    return list(specs) + (getattr(rider, field) if rider is not None else [])


def _comm_call(name, rider):
    def body(*refs):
        _, comm = _split_refs(refs, 0, 0, 0, rider)
        comm.start()
        comm.forward()
        comm.finish()

    return pl.pallas_call(body, name=name, in_specs=rider.in_specs, out_specs=rider.out_specs,
                          out_shape=rider.out_shape, scratch_shapes=rider.scratch)(*rider.arrays)


def _adamw(w, g, m, v):
    m = ADAM_B1 * m + (1.0 - ADAM_B1) * g
    v = ADAM_B2 * v + (1.0 - ADAM_B2) * (g * g)
    m_hat = m / (1.0 - ADAM_B1 ** ADAM_STEP)
    v_hat = v / (1.0 - ADAM_B2 ** ADAM_STEP)
    delta = -ADAM_LR * (m_hat / (jnp.sqrt(v_hat) + ADAM_EPS) + ADAM_WD * w)
    return delta, m, v


def _sum_parts(parts_ref, index=()):
    g = parts_ref[(0,) + index].astype(F32)
    for s in range(1, N_DEV):
        g = g + parts_ref[(s,) + index].astype(F32)
    return g


def _adamw_call(name, parts, w, m, v, block_rows):
    rows, cols = w.shape
    nb = rows // block_rows

    def body(parts_ref, w_ref, m_ref, v_ref, g_ref, delta_ref, m_out, v_out):
        g = _sum_parts(parts_ref)
        delta, m_new, v_new = _adamw(w_ref[...], g, m_ref[...], v_ref[...])
        g_ref[...] = g
        delta_ref[...] = delta
        m_out[...] = m_new
        v_out[...] = v_new

    blk = pl.BlockSpec((block_rows, cols), lambda i: (i, 0))
    return pl.pallas_call(
        body, name=name, grid=(nb,),
        in_specs=[pl.BlockSpec((N_DEV, block_rows, cols), lambda i: (0, i, 0)), blk, blk, blk],
        out_specs=[blk, blk, blk, blk],
        out_shape=[jax.ShapeDtypeStruct((rows, cols), F32)] * 4,
        compiler_params=_params(("parallel",)),
    )(parts, w, m, v)


def _adamw_cols_call(name, parts_a, parts_b, w, m, v, block_cols):
    rows, cols = w.shape
    na, nb = parts_a.shape[2] // block_cols, parts_b.shape[2] // block_cols

    def body(pa_ref, pb_ref, w_ref, m_ref, v_ref, g_ref, delta_ref, m_out, v_out):
        def update(parts_ref):
            g = _sum_parts(parts_ref)
            delta, m_new, v_new = _adamw(w_ref[...], g, m_ref[...], v_ref[...])
            g_ref[...] = g
            delta_ref[...] = delta
            m_out[...] = m_new
            v_out[...] = v_new

        pl.when(pl.program_id(0) < na)(functools.partial(update, pa_ref))
        pl.when(pl.program_id(0) >= na)(functools.partial(update, pb_ref))

    blk = pl.BlockSpec((rows, block_cols), lambda i: (0, i))
    return pl.pallas_call(
        body, name=name, grid=(na + nb,),
        in_specs=[pl.BlockSpec((N_DEV, rows, block_cols), lambda i: (0, 0, jnp.minimum(i, na - 1))),
                  pl.BlockSpec((N_DEV, rows, block_cols), lambda i: (0, 0, jnp.maximum(i - na, 0))), blk, blk, blk],
        out_specs=[blk, blk, blk, blk],
        out_shape=[jax.ShapeDtypeStruct((rows, cols), F32)] * 4,
        compiler_params=_params(("arbitrary",)),
    )(parts_a, parts_b, w, m, v)


WIDE_ROWS = 8
NARROW_ROWS = 40
NARROW_GKW_ROW = 8
NARROW_GKB_ROW = 24
NARROW_HW_ROW = 32
GROUP_SHARD = POOL_GROUP_DIM // N_DEV
KEY_SHARD = GLA_KEY_WIDTH // N_DEV
HEAD_V_SHARD = GLA_HEAD_V // N_DEV


def _small_adamw_call(wide, narrow, w, m, v):
    names = ("norm_w", "pool_scale", "final_norm_w", "pool_group_b", "gla_gk_w", "gla_gk_b", "gla_head_norm_w")
    where = {
        "norm_w": (0, slice(0, 2), slice(None)),
        "pool_scale": (0, slice(2, 3), slice(None)),
        "final_norm_w": (0, slice(3, 4), slice(None)),
        "pool_group_b": (1, slice(0, POOL_GROUPS), slice(0, GROUP_SHARD)),
        "gla_gk_w": (1, slice(NARROW_GKW_ROW, NARROW_GKW_ROW + GLA_GATE_RANK), slice(0, KEY_SHARD)),
        "gla_gk_b": (1, slice(NARROW_GKB_ROW, NARROW_GKB_ROW + 1), slice(0, KEY_SHARD)),
        "gla_head_norm_w": (1, slice(NARROW_HW_ROW, NARROW_HW_ROW + 1), slice(0, HEAD_V_SHARD)),
    }
    k = len(names)

    def body(*refs):
        parts = refs[0:2]
        w_refs, m_refs, v_refs = refs[2:2 + k], refs[2 + k:2 + 2 * k], refs[2 + 2 * k:2 + 3 * k]
        outs = refs[2 + 3 * k:]
        loss_ref = outs[0]
        loss_ref[...] = _sum_parts(parts[0], (slice(4, 5), slice(0, 1)))
        for i, name in enumerate(names):
            buf, rows, cols = where[name]
            g = _sum_parts(parts[buf], (rows, cols))
            delta, m_new, v_new = _adamw(w_refs[i][...], g, m_refs[i][...], v_refs[i][...])
            outs[1 + i][...] = g
            outs[1 + k + i][...] = delta
            outs[1 + 2 * k + i][...] = m_new
            outs[1 + 3 * k + i][...] = v_new

    vmem = pl.BlockSpec(memory_space=pltpu.VMEM)
    shapes = [jax.ShapeDtypeStruct(w[n].shape, F32) for n in names]
    res = pl.pallas_call(
        body, name="adamw_small", in_specs=[vmem] * (2 + 3 * k), out_specs=[vmem] * (1 + 4 * k),
        out_shape=[jax.ShapeDtypeStruct((1, 1), F32)] + shapes * 4,
    )(wide, narrow, *[w[n] for n in names], *[m[n] for n in names], *[v[n] for n in names])
    unzip = lambda j: dict(zip(names, res[1 + j * k:1 + (j + 1) * k]))
    return res[0], unzip(0), unzip(1), unzip(2), unzip(3)


def kernel(x, norm_w, pool_in_w, pool_group_w, pool_group_b, pool_scale, pool_out_w, gla_in_w, gla_gk_w, gla_gk_b, gla_head_norm_w, gla_out_w, final_norm_w, loss_target, m_norm_w, m_pool_in_w, m_pool_group_w, m_pool_group_b, m_pool_scale, m_pool_out_w, m_gla_in_w, m_gla_gk_w, m_gla_gk_b, m_gla_head_norm_w, m_gla_out_w, m_final_norm_w, v_norm_w, v_pool_in_w, v_pool_group_w, v_pool_group_b, v_pool_scale, v_pool_out_w, v_gla_in_w, v_gla_gk_w, v_gla_gk_b, v_gla_head_norm_w, v_gla_out_w, v_final_norm_w):
    w = dict(norm_w=norm_w, pool_in_w=pool_in_w, pool_group_w=pool_group_w, pool_group_b=pool_group_b,
             pool_scale=pool_scale, pool_out_w=pool_out_w, gla_in_w=gla_in_w, gla_gk_w=gla_gk_w, gla_gk_b=gla_gk_b,
             gla_head_norm_w=gla_head_norm_w, gla_out_w=gla_out_w, final_norm_w=final_norm_w)
    m = dict(norm_w=m_norm_w, pool_in_w=m_pool_in_w, pool_group_w=m_pool_group_w, pool_group_b=m_pool_group_b,
             pool_scale=m_pool_scale, pool_out_w=m_pool_out_w, gla_in_w=m_gla_in_w, gla_gk_w=m_gla_gk_w,
             gla_gk_b=m_gla_gk_b, gla_head_norm_w=m_gla_head_norm_w, gla_out_w=m_gla_out_w,
             final_norm_w=m_final_norm_w)
    v = dict(norm_w=v_norm_w, pool_in_w=v_pool_in_w, pool_group_w=v_pool_group_w, pool_group_b=v_pool_group_b,
             pool_scale=v_pool_scale, pool_out_w=v_pool_out_w, gla_in_w=v_gla_in_w, gla_gk_w=v_gla_gk_w,
             gla_gk_b=v_gla_gk_b, gla_head_norm_w=v_gla_head_norm_w, gla_out_w=v_gla_out_w,
             final_norm_w=v_final_norm_w)
    col_shard = GLA_IN_WIDTH // N_DEV
    row_shard = D_MODEL // N_DEV

    def lanes(a):
        return jnp.pad(a, [(0, 0)] * (a.ndim - 1) + [(0, LANES - a.shape[-1])])

    small_in = jnp.concatenate([lanes(pool_group_b[0]), lanes(gla_gk_b), lanes(gla_head_norm_w),
                                jnp.zeros((2, LANES), F32)], axis=0)
    in_cols = 2 * POOL_WIDTH // N_DEV
    pool_in, pool_gw, pool_out, small_all = _comm_call("pool_weights_all_gather", _gather_rider(
        [pool_in_w[0].astype(BF16), pool_group_w[0].astype(BF16), pool_out_w[0].astype(BF16), small_in],
        [(D_MODEL, 2 * POOL_WIDTH), (POOL_GROUPS, POOL_GROUP_DIM, POOL_GROUP_DIM), (POOL_WIDTH, D_MODEL),
         (N_DEV, 8, LANES)],
        [_dim1_slot(in_cols), _dim1_slot(GROUP_SHARD), _row_slot(row_shard), _lead_slot]))
    pool_gb = jnp.transpose(small_all[:, 0:POOL_GROUPS, :GROUP_SHARD], (1, 0, 2)).reshape(1, POOL_WIDTH)
    gla_gkb = small_all[:, POOL_GROUPS, :KEY_SHARD].reshape(1, GLA_KEY_WIDTH)
    gla_hw = jnp.tile(small_all[:, POOL_GROUPS + 1, :HEAD_V_SHARD].reshape(1, GLA_HEAD_V), (1, GLA_HEADS))
    nw0, nw1, wf = norm_w[0:1], norm_w[1:2], final_norm_w.reshape(1, D_MODEL)
    xs, target = x[0], loss_target[0]

    h1, p, gla_in_parts, gkw_parts, gla_out = _pool_fwd_call(
        xs, nw0, pool_in, pool_gw, pool_gb, pool_scale, pool_out, _gather_rider(
            [jnp.transpose(gla_in_w[0]).astype(BF16), gla_gk_w[0].astype(BF16), gla_out_w[0].astype(BF16)],
            [(N_DEV, col_shard, D_MODEL), (N_DEV, GLA_GATE_RANK, KEY_SHARD), (GLA_VALUE_WIDTH, D_MODEL)],
            [_lead_slot, _lead_slot, _row_slot(row_shard)]))
    gla_in = jnp.pad(gla_in_parts.reshape(GLA_IN_WIDTH, D_MODEL), ((0, GLA_IN_PAD - GLA_IN_WIDTH), (0, 0)))
    gla_gkw = jnp.pad(jnp.transpose(gkw_parts, (1, 0, 2)).reshape(GLA_GATE_RANK, GLA_KEY_WIDTH),
                      ((0, GLA_LOW_PAD - GLA_GATE_RANK), (0, 0)))
    dh2, proj, o, states, loss_part, dwf = _gla_fwd_call(h1, nw1, gla_in, gla_gkw, gla_gkb, gla_hw, gla_out, wf, target)

    dproj, d_gla_out, dhw, dgkw, dgkb = _gla_bwd_call(dh2, proj, o, states, gla_gkw, gla_gkb, gla_hw, gla_out)
    dh1, d_gla_in, dnw1, landed_gla_out = _inproj_bwd_call(
        "gla_in_bwd", dproj, h1, nw1, gla_in, dh2,
        _exchange_rider([d_gla_out], [(row_shard, D_MODEL)], [_row_slot(row_shard)]), transposed=True)
    gla_in_send = d_gla_in[:GLA_IN_WIDTH].reshape(N_DEV, col_shard, D_MODEL)
    cols_a = GLA_IN_COLS_WITH_POOL_BWD
    dp, d_pool_out, dgw, dgb, dsc, landed_gla_in_a = _pool_bwd_call(
        dh1, p, pool_gw, pool_gb, pool_scale, pool_out,
        _exchange_rider([gla_in_send], [(col_shard, cols_a)], [lambda ref, d: ref.at[d, :, pl.ds(0, cols_a)]]))
    grad_x, d_pool_in, dnw0, landed_gla_in_b, landed_pool_out, landed_gw = _inproj_bwd_call(
        "pool_in_bwd", dp, xs, nw0, pool_in, dh1,
        _exchange_rider([gla_in_send, d_pool_out, dgw],
                        [(col_shard, D_MODEL - cols_a), (row_shard, D_MODEL),
                         (POOL_GROUPS, GROUP_SHARD, POOL_GROUP_DIM)],
                        [lambda ref, d: ref.at[d, :, pl.ds(cols_a, D_MODEL - cols_a)], _row_slot(row_shard),
                         _dim1_slot(GROUP_SHARD)]))

    wide = jnp.concatenate([
        dnw0, dnw1, dsc, dwf, jnp.pad(loss_part[0:1, 0:1], ((0, 0), (0, D_MODEL - 1))),
        jnp.zeros((WIDE_ROWS - 5, D_MODEL), F32)], axis=0)

    def rows8(a):
        return jnp.pad(lanes(a), ((0, 0), (0, -a.shape[1] % 8), (0, 0)))

    narrow = jnp.concatenate([
        rows8(jnp.transpose(dgb.reshape(POOL_GROUPS, N_DEV, GROUP_SHARD), (1, 0, 2))),
        rows8(jnp.transpose(dgkw[:GLA_GATE_RANK].reshape(GLA_GATE_RANK, N_DEV, KEY_SHARD), (1, 0, 2))),
        rows8(dgkb.reshape(N_DEV, 1, KEY_SHARD)),
        rows8(dhw.reshape(GLA_HEADS, GLA_HEAD_V).sum(axis=0).reshape(N_DEV, 1, HEAD_V_SHARD)),
    ], axis=1)
    landed_pool_in, landed_wide, landed_narrow = _comm_call("grads_all_to_all", _exchange_rider(
        [d_pool_in, wide, narrow], [(D_MODEL, in_cols), (WIDE_ROWS, D_MODEL), (NARROW_ROWS, LANES)],
        [_dim1_slot(in_cols), lambda ref, d: ref, _lead_slot]))

    res = {}
    for name, parts, rows, cols, block in [
            ("pool_in_w", landed_pool_in, D_MODEL, in_cols, 256),
            ("pool_group_w", landed_gw, POOL_GROUPS * GROUP_SHARD, POOL_GROUP_DIM, 128),
            ("pool_out_w", landed_pool_out, row_shard, D_MODEL, 128),
            ("gla_out_w", landed_gla_out, row_shard, D_MODEL, 128)]:
        outs = _adamw_call("adamw_" + name, parts.reshape(N_DEV, rows, cols), w[name].reshape(rows, cols),
                           m[name].reshape(rows, cols), v[name].reshape(rows, cols), block)
        res[name] = [t.reshape(w[name].shape) for t in outs]
    outs = _adamw_cols_call("adamw_gla_in_w", landed_gla_in_a, landed_gla_in_b, jnp.transpose(gla_in_w[0]),
                            jnp.transpose(m_gla_in_w[0]), jnp.transpose(v_gla_in_w[0]), 256)
    res["gla_in_w"] = [jnp.transpose(t)[None] for t in outs]
    small_shapes = {"norm_w": (2, D_MODEL), "pool_scale": (1, D_MODEL), "final_norm_w": (1, D_MODEL),
                    "pool_group_b": (POOL_GROUPS, GROUP_SHARD), "gla_gk_w": (GLA_GATE_RANK, KEY_SHARD),
                    "gla_gk_b": (1, KEY_SHARD), "gla_head_norm_w": (1, HEAD_V_SHARD)}
    as_small = lambda t: {n: t[n].reshape(s) for n, s in small_shapes.items()}
    loss, *small_outs = _small_adamw_call(landed_wide, landed_narrow, as_small(w), as_small(m), as_small(v))
    for name in small_shapes:
        res[name] = [t[name].reshape(w[name].shape) for t in small_outs]
    order = ("norm_w", "pool_in_w", "pool_group_w", "pool_group_b", "pool_scale", "pool_out_w", "gla_in_w",
             "gla_gk_w", "gla_gk_b", "gla_head_norm_w", "gla_out_w", "final_norm_w")
    return (loss.reshape(()), grad_x[None], *[res[n][0] for n in order], *[res[n][1] for n in order],
            *[res[n][2] for n in order], *[res[n][3] for n in order])
```

```python
import functools

import jax
import jax.numpy as jnp
from jax import lax
from jax.experimental import pallas as pl
from jax.experimental.pallas import tpu as pltpu

F32 = jnp.float32
BF16 = jnp.bfloat16
MESH = pl.DeviceIdType.MESH

N_DEV = 8
D_MODEL = 1024
POOL_WIDTH = 1024
POOL_GROUPS = 4
POOL_GROUP_DIM = 256
POOL_HALO = 16
GLA_HEADS = 4
GLA_HEAD_K = 128
GLA_HEAD_V = 256
GLA_KEY_WIDTH = 512
GLA_VALUE_WIDTH = 1024
GLA_GATE_RANK = 16
GLA_IN_WIDTH = 3088
GLA_IN_PAD = 3200
GLA_LOW_PAD = 128
GLA_QKVG_WIDTH = 3072
CHUNK = 64
GATE_NORMALIZER = 16.0
RMS_EPS = 1e-6
Q_SCALE = GLA_HEAD_K ** -0.5

ADAM_LR = 0.001
ADAM_B1 = 0.9
ADAM_B2 = 0.999
ADAM_EPS = 1e-08
ADAM_WD = 0.01
ADAM_STEP = 10

LANES = 128
BF16_ROWS = 16
VMEM_LIMIT = 56 * 1024 * 1024
ROW_TILE = 256
MATMUL_ROW_TILE = 512
GLA_IN_COLS_WITH_POOL_BWD = 768


def _dot_nn(a, b):
    return lax.dot_general(a, b, (((1,), (0,)), ((), ())), preferred_element_type=F32)


def _dot_nt(a, b):
    return lax.dot_general(a, b, (((1,), (1,)), ((), ())), preferred_element_type=F32)


def _dot_tn(a, b):
    return lax.dot_general(a, b, (((0,), (0,)), ((), ())), preferred_element_type=F32)


def _rms(x):
    rstd = lax.rsqrt(jnp.mean(x * x, axis=-1, keepdims=True) + RMS_EPS)
    return x * rstd, rstd


def _rms_bwd(dxhat, xhat, rstd):
    return rstd * (dxhat - xhat * jnp.mean(dxhat * xhat, axis=-1, keepdims=True))


def _sigmoid(x):
    return 1.0 / (1.0 + jnp.exp(-x))


def _params(sem=("arbitrary",)):
    return pltpu.CompilerParams(dimension_semantics=sem, vmem_limit_bytes=VMEM_LIMIT)


def _full(shape):
    return pl.BlockSpec(shape, lambda i: (0,) * len(shape))


def _const(shape):
    return pl.BlockSpec(shape, lambda i: (0,) * len(shape), pipeline_mode=pl.Buffered(1))


def _window_sums(ext, forward):
    n = ext.shape[0]
    outs = []
    for g in range(POOL_GROUPS):
        s = ext[:, g * POOL_GROUP_DIM:(g + 1) * POOL_GROUP_DIM]
        for k in range(g + 1):
            shift = (1 << k) if forward else n - (1 << k)
            s = s + pltpu.roll(s, shift, axis=0)
        outs.append(s[:n - POOL_HALO])
    return outs


def _inv_count(row0, tm):
    row = row0 + lax.broadcasted_iota(jnp.int32, (tm, 1), 0)
    return [1.0 / jnp.minimum(row + 1, 2 << g).astype(F32) for g in range(POOL_GROUPS)]


def _pool_mix(u, u_prev, row0, gw_ref, gb):
    tm = u.shape[0]
    sums = _window_sums(jnp.concatenate([u, u_prev], axis=0), True)
    inv = _inv_count(row0, tm)
    pooled, mixed = [], []
    for g in range(POOL_GROUPS):
        ug = u[:, g * POOL_GROUP_DIM:(g + 1) * POOL_GROUP_DIM]
        pg = (sums[g] * inv[g] - ug).astype(BF16)
        pooled.append(pg)
        mixed.append(_dot_nn(pg, gw_ref[g]))
    return pooled, jnp.concatenate(mixed, axis=1) + gb


def _pool_fwd_call(x, nw, w_in, gw, gb, sc, w_out, rider=None):
    seq = x.shape[0]
    tm = min(MATMUL_ROW_TILE, seq)
    nt = seq // tm

    def main(x_ref, nw_ref, win_ref, gw_ref, gb_ref, sc_ref, wout_ref, h_ref, p_ref, halo_ref):
        i = pl.program_id(0)

        @pl.when(i == 0)
        def _():
            halo_ref[...] = jnp.zeros_like(halo_ref)

        xt = x_ref[...]
        xhat, _ = _rms(xt)
        n = (xhat * nw_ref[...]).astype(BF16)
        p = _dot_nn(n, win_ref[...])
        p_ref[...] = p
        u = p[:, :POOL_WIDTH]
        gate = p[:, POOL_WIDTH:]
        _, mixed = _pool_mix(u, halo_ref[...], i * tm, gw_ref, gb_ref[...])
        halo_ref[...] = u[tm - POOL_HALO:, :]
        y = (mixed * sc_ref[...] * (gate * _sigmoid(gate))).astype(BF16)
        h_ref[...] = xt + _dot_nn(y, wout_ref[...])

    def body(*refs):
        own, comm = _split_refs(refs, 7, 2, 1, rider)
        _ride_before(comm, pl.program_id(0), nt)
        main(*own)
        _ride_after(comm, pl.program_id(0), nt)

    return pl.pallas_call(
        body, name="pool_fwd", grid=(nt,),
        in_specs=_extend([pl.BlockSpec((tm, D_MODEL), lambda i: (i, 0)), _const((1, D_MODEL)),
                          _const((D_MODEL, 2 * POOL_WIDTH)), _const((POOL_GROUPS, POOL_GROUP_DIM, POOL_GROUP_DIM)),
                          _const((1, POOL_WIDTH)), _const((1, POOL_WIDTH)), _const((POOL_WIDTH, D_MODEL))],
                         rider, "in_specs"),
        out_specs=_extend([pl.BlockSpec((tm, D_MODEL), lambda i: (i, 0)),
                           pl.BlockSpec((tm, 2 * POOL_WIDTH), lambda i: (i, 0))], rider, "out_specs"),
        out_shape=_extend([jax.ShapeDtypeStruct((seq, D_MODEL), F32),
                           jax.ShapeDtypeStruct((seq, 2 * POOL_WIDTH), F32)], rider, "out_shape"),
        scratch_shapes=_extend([pltpu.VMEM((POOL_HALO, POOL_WIDTH), F32)], rider, "scratch"),
        compiler_params=_params(),
    )(x, nw, w_in, gw, gb, sc, w_out, *_extend([], rider, "arrays"))


def _pool_bwd_call(dh, p, gw, gb, sc, w_out, rider=None):
    seq = dh.shape[0]
    tm = min(MATMUL_ROW_TILE, seq)
    nt = seq // tm
    halo_blocks = tm // POOL_HALO

    def main(dh_ref, p_ref, pprev_ref, gw_ref, gb_ref, sc_ref, wout_ref,
             dp_ref, dwout_hbm, dgw_hbm, dgb_ref, dsc_ref, carry_ref, dwout_acc, dgw_acc, dwout_stage, dgw_stage):
        i = pl.program_id(0)
        t = nt - 1 - i

        @pl.when(i == 0)
        def _():
            carry_ref[...] = jnp.zeros_like(carry_ref)
            dwout_acc[...] = jnp.zeros_like(dwout_acc)
            dgw_acc[...] = jnp.zeros_like(dgw_acc)
            dgb_ref[...] = jnp.zeros_like(dgb_ref)
            dsc_ref[...] = jnp.zeros_like(dsc_ref)

        p = p_ref[...]
        u = p[:, :POOL_WIDTH]
        gate = p[:, POOL_WIDTH:]
        u_prev = jnp.where(t > 0, pprev_ref[:, :POOL_WIDTH], 0.0)
        pooled, mixed = _pool_mix(u, u_prev, t * tm, gw_ref, gb_ref[...])
        sg = _sigmoid(gate)
        silu = gate * sg
        sc = sc_ref[...]
        dhb = dh_ref[...].astype(BF16)
        y = (mixed * sc * silu).astype(BF16)
        dwout_acc[...] += _dot_tn(y, dhb)
        dy = _dot_nt(dhb, wout_ref[...])
        dmixed = dy * sc * silu
        dsc_ref[...] += jnp.sum(dy * mixed * silu, axis=0, keepdims=True)
        dgate = dy * mixed * sc * (sg * (1.0 + gate * (1.0 - sg)))
        dgb_ref[...] += jnp.sum(dmixed, axis=0, keepdims=True)
        inv = _inv_count(t * tm, tm)
        dpooled, scaled = [], []
        for g in range(POOL_GROUPS):
            dmg = dmixed[:, g * POOL_GROUP_DIM:(g + 1) * POOL_GROUP_DIM].astype(BF16)
            dgw_acc[g] += _dot_tn(pooled[g], dmg)
            dpg = _dot_nt(dmg, gw_ref[g])
            dpooled.append(dpg)
            scaled.append(dpg * inv[g])
        r = jnp.concatenate(scaled, axis=1)
        sums = _window_sums(jnp.concatenate([r, carry_ref[...]], axis=0), False)
        carry_ref[...] = r[:POOL_HALO, :]
        du = jnp.concatenate([sums[g] - dpooled[g] for g in range(POOL_GROUPS)], axis=1)
        dp_ref[...] = jnp.concatenate([du, dgate], axis=1).astype(BF16)

        @pl.when(i == nt - 1)
        def _():
            dwout_stage[...] = dwout_acc[...].astype(BF16)
            dgw_stage[...] = dgw_acc[...].astype(BF16)
            pltpu.sync_copy(dwout_stage, dwout_hbm)
            pltpu.sync_copy(dgw_stage, dgw_hbm)

    def body(*refs):
        own, comm = _split_refs(refs, 7, 5, 5, rider)
        _ride_before(comm, pl.program_id(0), nt)
        main(*own)
        _ride_after(comm, pl.program_id(0), nt)

    rev = lambda i: (nt - 1 - i, 0)
    return pl.pallas_call(
        body, name="pool_bwd", grid=(nt,),
        in_specs=_extend([pl.BlockSpec((tm, D_MODEL), rev), pl.BlockSpec((tm, 2 * POOL_WIDTH), rev),
                          pl.BlockSpec((POOL_HALO, 2 * POOL_WIDTH),
                                       lambda i: (jnp.maximum((nt - 1 - i) * halo_blocks - 1, 0), 0)),
                          _const((POOL_GROUPS, POOL_GROUP_DIM, POOL_GROUP_DIM)), _const((1, POOL_WIDTH)),
                          _const((1, POOL_WIDTH)), _const((POOL_WIDTH, D_MODEL))], rider, "in_specs"),
        out_specs=_extend([pl.BlockSpec((tm, 2 * POOL_WIDTH), rev), pl.BlockSpec(memory_space=pl.ANY),
                           pl.BlockSpec(memory_space=pl.ANY), _full((1, POOL_WIDTH)), _full((1, POOL_WIDTH))],
                          rider, "out_specs"),
        out_shape=_extend([jax.ShapeDtypeStruct((seq, 2 * POOL_WIDTH), BF16),
                           jax.ShapeDtypeStruct((POOL_WIDTH, D_MODEL), BF16),
                           jax.ShapeDtypeStruct((POOL_GROUPS, POOL_GROUP_DIM, POOL_GROUP_DIM), BF16),
                           jax.ShapeDtypeStruct((1, POOL_WIDTH), F32), jax.ShapeDtypeStruct((1, POOL_WIDTH), F32)],
                          rider, "out_shape"),
        scratch_shapes=_extend([pltpu.VMEM((POOL_HALO, POOL_WIDTH), F32), pltpu.VMEM((POOL_WIDTH, D_MODEL), F32),
                                pltpu.VMEM((POOL_GROUPS, POOL_GROUP_DIM, POOL_GROUP_DIM), F32),
                                pltpu.VMEM((POOL_WIDTH, D_MODEL), BF16),
                                pltpu.VMEM((POOL_GROUPS, POOL_GROUP_DIM, POOL_GROUP_DIM), BF16)], rider, "scratch"),
        compiler_params=_params(),
    )(dh, p, p, gw, gb, sc, w_out, *_extend([], rider, "arrays"))


def _inproj_bwd_call(name, dproj, h_in, nw, w_in, dres, rider=None, transposed=False):
    seq = h_in.shape[0]
    width = dproj.shape[1]
    w_shape = tuple(w_in.shape)
    tm = min(MATMUL_ROW_TILE, seq)
    nt = seq // tm

    def main(dproj_ref, h_ref, nw_ref, win_ref, dres_ref, dh_ref, dw_hbm, dnw_ref, dw_acc, dw_stage):
        i = pl.program_id(0)

        @pl.when(i == 0)
        def _():
            dw_acc[...] = jnp.zeros_like(dw_acc)
            dnw_ref[...] = jnp.zeros_like(dnw_ref)

        xhat, rstd = _rms(h_ref[...])
        nw_row = nw_ref[...]
        n = (xhat * nw_row).astype(BF16)
        dpb = dproj_ref[...]
        if transposed:
            dw_acc[...] += _dot_tn(dpb, n)
            dn = _dot_nn(dpb, win_ref[...])
        else:
            dw_acc[...] += _dot_tn(n, dpb)
            dn = _dot_nt(dpb, win_ref[...])
        dnw_ref[...] += jnp.sum(dn * xhat, axis=0, keepdims=True)
        dh_ref[...] = _rms_bwd(dn * nw_row, xhat, rstd) + dres_ref[...]

        @pl.when(i == nt - 1)
        def _():
            dw_stage[...] = dw_acc[...].astype(BF16)
            pltpu.sync_copy(dw_stage, dw_hbm)

    def body(*refs):
        own, comm = _split_refs(refs, 5, 3, 2, rider)
        _ride_before(comm, pl.program_id(0), nt)
        main(*own)
        _ride_after(comm, pl.program_id(0), nt)

    row = lambda i: (i, 0)
    return pl.pallas_call(
        body, name=name, grid=(nt,),
        in_specs=_extend([pl.BlockSpec((tm, width), row), pl.BlockSpec((tm, D_MODEL), row), _const((1, D_MODEL)),
                          _const(w_shape), pl.BlockSpec((tm, D_MODEL), row)], rider, "in_specs"),
        out_specs=_extend([pl.BlockSpec((tm, D_MODEL), row), pl.BlockSpec(memory_space=pl.ANY),
                           _full((1, D_MODEL))], rider, "out_specs"),
        out_shape=_extend([jax.ShapeDtypeStruct((seq, D_MODEL), F32), jax.ShapeDtypeStruct(w_shape, BF16),
                           jax.ShapeDtypeStruct((1, D_MODEL), F32)], rider, "out_shape"),
        scratch_shapes=_extend([pltpu.VMEM(w_shape, F32), pltpu.VMEM(w_shape, BF16)], rider, "scratch"),
        compiler_params=_params(),
    )(dproj, h_in, nw, w_in, dres, *_extend([], rider, "arrays"))


def _chunk_scan(x, reverse):
    n = x.shape[0]
    pos = lax.broadcasted_iota(jnp.int32, (n, 1), 0) & (CHUNK - 1)
    k = 1
    while k < CHUNK:
        if reverse:
            x = x + jnp.where(pos < CHUNK - k, pltpu.roll(x, n - k, axis=0), 0.0)
        else:
            x = x + jnp.where(pos >= k, pltpu.roll(x, k, axis=0), 0.0)
        k *= 2
    return x


class _GlaTile:
    def __init__(self, q, k, v, gate, low, gkw_ref, gkb):
        tm = q.shape[0]
        self.q = q * Q_SCALE
        self.k, self.v, self.gate = k, v, gate
        self.low_b = low.astype(BF16)
        self.z = _dot_nn(self.low_b, gkw_ref[...]) + gkb
        log_g = (jnp.minimum(self.z, 0.0) - jnp.log(1.0 + jnp.exp(-jnp.abs(self.z)))) / GATE_NORMALIZER
        self.c = _chunk_scan(log_g, False)
        is_last = lax.broadcasted_iota(jnp.int32, (CHUNK, 1), 0) == CHUNK - 1
        last = [jnp.sum(jnp.where(is_last, self.c[j * CHUNK:(j + 1) * CHUNK, :], 0.0), axis=0, keepdims=True)
                for j in range(tm // CHUNK)]
        self.c_last = last
        c_last_rows = jnp.concatenate([jnp.broadcast_to(r, (CHUNK, GLA_KEY_WIDTH)) for r in last], axis=0)
        self.e_pos = jnp.exp(self.c)
        self.e_neg = jnp.exp(-self.c)
        self.e_rest = jnp.exp(c_last_rows - self.c)
        self.a_b = (self.q * self.e_pos).astype(BF16)
        self.b_b = (self.k * self.e_neg).astype(BF16)
        self.cn_b = (self.q * self.e_neg).astype(BF16)
        self.dp_b = (self.k * self.e_pos).astype(BF16)
        self.kd_b = (self.k * self.e_rest).astype(BF16)
        self.v_b = self.v.astype(BF16)
        idx_t = lax.broadcasted_iota(jnp.int32, (tm, tm), 0)
        idx_s = lax.broadcasted_iota(jnp.int32, (tm, tm), 1)
        same_chunk = (idx_t ^ idx_s) < CHUNK
        self.lower = same_chunk & (idx_t >= idx_s)
        self.upper = same_chunk & (idx_t < idx_s)

    @staticmethod
    def rows(j):
        return slice(j * CHUNK, (j + 1) * CHUNK)

    @staticmethod
    def kcols(h):
        return slice(h * GLA_HEAD_K, (h + 1) * GLA_HEAD_K)

    @staticmethod
    def vcols(h):
        return slice(h * GLA_HEAD_V, (h + 1) * GLA_HEAD_V)

    def scores(self, h):
        kc = self.kcols(h)
        fwd = _dot_nt(self.a_b[:, kc], self.b_b[:, kc])
        bwd = _dot_nt(self.cn_b[:, kc], self.dp_b[:, kc])
        return jnp.where(self.lower, fwd, jnp.where(self.upper, bwd, 0.0)).astype(BF16)


def _gla_fwd_call(h1, nw, w_in, gkw, gkb, hw, w_out, wf, target):
    seq = h1.shape[0]
    tm = ROW_TILE
    nt = seq // tm
    cpt = tm // CHUNK
    n_chunks = seq // CHUNK

    def body(h_ref, nw_ref, win_ref, gkw_ref, gkb_ref, hw_ref, wout_ref, wf_ref, tgt_ref,
             dh2_ref, proj_ref, o_ref, st_ref, loss_ref, dwf_ref, state_ref):
        i = pl.program_id(0)

        @pl.when(i == 0)
        def _():
            state_ref[...] = jnp.zeros_like(state_ref)
            loss_ref[...] = jnp.zeros_like(loss_ref)
            dwf_ref[...] = jnp.zeros_like(dwf_ref)

        ht = h_ref[...]
        xhat, _ = _rms(ht)
        n = (xhat * nw_ref[...]).astype(BF16)
        sections = {}
        for name, lo, hi in (("low", GLA_QKVG_WIDTH, GLA_IN_PAD), ("qk", 0, 2 * GLA_KEY_WIDTH),
                             ("v", 2 * GLA_KEY_WIDTH, GLA_QKVG_WIDTH - GLA_VALUE_WIDTH),
                             ("gate", GLA_QKVG_WIDTH - GLA_VALUE_WIDTH, GLA_QKVG_WIDTH)):
            sections[name] = _dot_nt(n, win_ref[lo:hi, :])
            proj_ref[:, lo:hi] = sections[name]
        g = _GlaTile(sections["qk"][:, :GLA_KEY_WIDTH], sections["qk"][:, GLA_KEY_WIDTH:], sections["v"],
                     sections["gate"], sections["low"], gkw_ref, gkb_ref[...])
        o_heads = []
        for h in range(GLA_HEADS):
            kc, vc = g.kcols(h), g.vcols(h)
            srows = slice(h * GLA_HEAD_V, (h + 1) * GLA_HEAD_V)
            o_intra = _dot_nn(g.scores(h), g.v_b[:, vc])
            state = state_ref[srows, :]
            o_rows = []
            for j in range(cpt):
                r = g.rows(j)
                st_ref[j, srows, :] = state
                o_rows.append(o_intra[r] + _dot_nt(g.a_b[r, kc], state.astype(BF16)))
                decay = jnp.exp(g.c_last[j][:, kc])
                state = state * decay + _dot_tn(g.v_b[r, vc], g.kd_b[r, kc])
            state_ref[srows, :] = state
            o_heads.append(jnp.concatenate(o_rows, axis=0))
        o = jnp.concatenate(o_heads, axis=1)
        o_ref[...] = o
        hw_row = hw_ref[...]
        on = jnp.concatenate([_rms(o[:, g.vcols(h)])[0] for h in range(GLA_HEADS)], axis=1) * hw_row
        y = (on * (g.gate * _sigmoid(g.gate))).astype(BF16)
        h2 = ht + _dot_nn(y, wout_ref[...])
        xhat2, rstd2 = _rms(h2)
        wf_row = wf_ref[...]
        err = xhat2 * wf_row - tgt_ref[...]
        loss_ref[...] += 0.5 * jnp.sum(err * err) / D_MODEL
        dout = err * (1.0 / D_MODEL)
        dwf_ref[...] += jnp.sum(dout * xhat2, axis=0, keepdims=True)
        dh2_ref[...] = _rms_bwd(dout * wf_row, xhat2, rstd2)

    row = lambda i: (i, 0)
    return pl.pallas_call(
        body, name="gla_fwd", grid=(nt,),
        in_specs=[pl.BlockSpec((tm, D_MODEL), row), _const((1, D_MODEL)), _const((GLA_IN_PAD, D_MODEL)),
                  _const((GLA_LOW_PAD, GLA_KEY_WIDTH)), _const((1, GLA_KEY_WIDTH)), _const((1, GLA_VALUE_WIDTH)),
                  _const((GLA_VALUE_WIDTH, D_MODEL)), _const((1, D_MODEL)), pl.BlockSpec((tm, D_MODEL), row)],
        out_specs=[pl.BlockSpec((tm, D_MODEL), row), pl.BlockSpec((tm, GLA_IN_PAD), row),
                   pl.BlockSpec((tm, GLA_VALUE_WIDTH), row),
                   pl.BlockSpec((cpt, GLA_VALUE_WIDTH, GLA_HEAD_K), lambda i: (i, 0, 0)),
                   _full((8, LANES)), _full((1, D_MODEL))],
        out_shape=[jax.ShapeDtypeStruct((seq, D_MODEL), F32), jax.ShapeDtypeStruct((seq, GLA_IN_PAD), F32),
                   jax.ShapeDtypeStruct((seq, GLA_VALUE_WIDTH), F32),
                   jax.ShapeDtypeStruct((n_chunks, GLA_VALUE_WIDTH, GLA_HEAD_K), F32),
                   jax.ShapeDtypeStruct((8, LANES), F32), jax.ShapeDtypeStruct((1, D_MODEL), F32)],
        scratch_shapes=[pltpu.VMEM((GLA_VALUE_WIDTH, GLA_HEAD_K), F32)],
        compiler_params=_params(),
    )(h1, nw, w_in, gkw, gkb, hw, w_out, wf, target)


def _gla_bwd_call(dh2, proj, o, states, gkw, gkb, hw, w_out):
    seq = dh2.shape[0]
    tm = ROW_TILE
    nt = seq // tm
    cpt = tm // CHUNK

    def body(dh_ref, proj_ref, o_ref, st_ref, gkw_ref, gkb_ref, hw_ref, wout_ref,
             dproj_ref, dwout_hbm, dhw_ref, dgkw_ref, dgkb_ref, dstate_ref, dwout_acc, dwout_stage):
        i = pl.program_id(0)

        @pl.when(i == 0)
        def _():
            dstate_ref[...] = jnp.zeros_like(dstate_ref)
            dwout_acc[...] = jnp.zeros_like(dwout_acc)
            dhw_ref[...] = jnp.zeros_like(dhw_ref)
            dgkw_ref[...] = jnp.zeros_like(dgkw_ref)
            dgkb_ref[...] = jnp.zeros_like(dgkb_ref)

        g = _GlaTile(proj_ref[:, :GLA_KEY_WIDTH], proj_ref[:, GLA_KEY_WIDTH:2 * GLA_KEY_WIDTH],
                     proj_ref[:, 2 * GLA_KEY_WIDTH:GLA_QKVG_WIDTH - GLA_VALUE_WIDTH],
                     proj_ref[:, GLA_QKVG_WIDTH - GLA_VALUE_WIDTH:GLA_QKVG_WIDTH], proj_ref[:, GLA_QKVG_WIDTH:],
                     gkw_ref, gkb_ref[...])
        dhb = dh_ref[...].astype(BF16)
        o = o_ref[...]
        hw_row = hw_ref[...]
        dy = _dot_nt(dhb, wout_ref[...])
        sg = _sigmoid(g.gate)
        silu = g.gate * sg
        don = dy * silu
        on_parts, do_parts, dhw_parts = [], [], []
        for h in range(GLA_HEADS):
            vc = g.vcols(h)
            xh, rs = _rms(o[:, vc])
            on_parts.append(xh * hw_row[:, vc])
            dhw_parts.append(jnp.sum(don[:, vc] * xh, axis=0, keepdims=True))
            do_parts.append(_rms_bwd(don[:, vc] * hw_row[:, vc], xh, rs))
        on = jnp.concatenate(on_parts, axis=1)
        dwout_acc[...] += _dot_tn((on * silu).astype(BF16), dhb)
        dhw_ref[...] += jnp.concatenate(dhw_parts, axis=1)
        dgate = dy * on * (sg * (1.0 + g.gate * (1.0 - sg)))
        do_b = jnp.concatenate(do_parts, axis=1).astype(BF16)

        last_row = lax.broadcasted_iota(jnp.int32, (CHUNK, 1), 0) == CHUNK - 1
        dq_h, dk_h, dv_h, dc_h = [], [], [], []
        for h in range(GLA_HEADS):
            kc, vc = g.kcols(h), g.vcols(h)
            srows = slice(h * GLA_HEAD_V, (h + 1) * GLA_HEAD_V)
            scores = g.scores(h)
            dscores = _dot_nt(do_b[:, vc], g.v_b[:, vc])
            dfwd = jnp.where(g.lower, dscores, 0.0).astype(BF16)
            dbwd = jnp.where(g.upper, dscores, 0.0).astype(BF16)
            dv_intra = _dot_tn(scores, do_b[:, vc])
            da_intra = _dot_nn(dfwd, g.b_b[:, kc])
            db = _dot_tn(dfwd, g.a_b[:, kc])
            dcn = _dot_nn(dbwd, g.dp_b[:, kc])
            ddp = _dot_tn(dbwd, g.cn_b[:, kc])
            dstate = dstate_ref[srows, :]
            da_rows, dkd_rows, dv_rows, dcl_rows = [None] * cpt, [None] * cpt, [None] * cpt, [None] * cpt
            for j in reversed(range(cpt)):
                r = g.rows(j)
                state = st_ref[j, srows, :]
                dstate_b = dstate.astype(BF16)
                do_c = do_b[r, vc]
                dv_rows[j] = dv_intra[r] + _dot_nt(g.kd_b[r, kc], dstate_b)
                da_rows[j] = da_intra[r] + _dot_nn(do_c, state.astype(BF16))
                dkd = _dot_nn(g.v_b[r, vc], dstate_b) * g.e_rest[r, kc]
                dkd_rows[j] = dkd
                decay = jnp.exp(g.c_last[j][:, kc])
                dc_last = (jnp.sum(dkd * g.k[r, kc], axis=0, keepdims=True)
                           + decay * jnp.sum(state * dstate, axis=0, keepdims=True))
                dcl_rows[j] = jnp.where(last_row, dc_last, 0.0)
                dstate = _dot_tn(do_c, g.a_b[r, kc]) + dstate * decay
            dstate_ref[srows, :] = dstate
            da = jnp.concatenate(da_rows, axis=0)
            dkd = jnp.concatenate(dkd_rows, axis=0)
            dv_h.append(jnp.concatenate(dv_rows, axis=0))
            q_up, q_down = da * g.e_pos[:, kc], dcn * g.e_neg[:, kc]
            k_up, k_down = ddp * g.e_pos[:, kc], db * g.e_neg[:, kc] + dkd
            dq_h.append(Q_SCALE * (q_up + q_down))
            dk_h.append(k_up + k_down)
            dc_h.append(g.q[:, kc] * (q_up - q_down) + g.k[:, kc] * (k_up - k_down)
                        + jnp.concatenate(dcl_rows, axis=0))
        dq = jnp.concatenate(dq_h, axis=1)
        dk = jnp.concatenate(dk_h, axis=1)
        dv = jnp.concatenate(dv_h, axis=1)
        dlog_g = _chunk_scan(jnp.concatenate(dc_h, axis=1), True)
        dz = dlog_g * (1.0 / GATE_NORMALIZER) * (1.0 - _sigmoid(g.z))
        dzb = dz.astype(BF16)
        dgkb_ref[...] += jnp.sum(dz, axis=0, keepdims=True)
        dgkw_ref[...] += _dot_tn(g.low_b, dzb)
        dlow = _dot_nt(dzb, gkw_ref[...])
        dproj_ref[...] = jnp.concatenate([dq, dk, dv, dgate, dlow], axis=1).astype(BF16)

        @pl.when(i == nt - 1)
        def _():
            dwout_stage[...] = dwout_acc[...].astype(BF16)
            pltpu.sync_copy(dwout_stage, dwout_hbm)

    rev = lambda i: (nt - 1 - i, 0)
    return pl.pallas_call(
        body, name="gla_bwd", grid=(nt,),
        in_specs=[pl.BlockSpec((tm, D_MODEL), rev), pl.BlockSpec((tm, GLA_IN_PAD), rev),
                  pl.BlockSpec((tm, GLA_VALUE_WIDTH), rev),
                  pl.BlockSpec((cpt, GLA_VALUE_WIDTH, GLA_HEAD_K), lambda i: (nt - 1 - i, 0, 0)),
                  _const((GLA_LOW_PAD, GLA_KEY_WIDTH)), _const((1, GLA_KEY_WIDTH)), _const((1, GLA_VALUE_WIDTH)),
                  _const((GLA_VALUE_WIDTH, D_MODEL))],
        out_specs=[pl.BlockSpec((tm, GLA_IN_PAD), rev), pl.BlockSpec(memory_space=pl.ANY),
                   _full((1, GLA_VALUE_WIDTH)), _full((GLA_LOW_PAD, GLA_KEY_WIDTH)), _full((1, GLA_KEY_WIDTH))],
        out_shape=[jax.ShapeDtypeStruct((seq, GLA_IN_PAD), BF16), jax.ShapeDtypeStruct((GLA_VALUE_WIDTH, D_MODEL), BF16),
                   jax.ShapeDtypeStruct((1, GLA_VALUE_WIDTH), F32), jax.ShapeDtypeStruct((GLA_LOW_PAD, GLA_KEY_WIDTH), F32),
                   jax.ShapeDtypeStruct((1, GLA_KEY_WIDTH), F32)],
        scratch_shapes=[pltpu.VMEM((GLA_VALUE_WIDTH, GLA_HEAD_K), F32), pltpu.VMEM((GLA_VALUE_WIDTH, D_MODEL), F32),
                        pltpu.VMEM((GLA_VALUE_WIDTH, D_MODEL), BF16)],
        compiler_params=_params(),
    )(dh2, proj, o, states, gkw, gkb, hw, w_out)


def _position():
    return lax.axis_index("x"), lax.axis_index("y"), lax.axis_index("c")


def _lead_slot(ref, d):
    return ref.at[d]


def _row_slot(rows):
    return lambda ref, d: ref.at[pl.ds(pl.multiple_of(d * rows, rows), rows)]


def _dim1_slot(size):
    return lambda ref, d: ref.at[:, pl.ds(pl.multiple_of(d * size, size), size)]


class _Gather:
    def __init__(self, in_refs, out_refs, slots, send_sems, recv_sems, local_sems):
        self.in_refs, self.out_refs, self.slots = in_refs, out_refs, slots
        self.send_sems, self.recv_sems, self.local_sems = send_sems, recv_sems, local_sems
        self.n = len(in_refs)
        x, y, c = _position()
        self.c = c
        self.me, self.sibling = (x, y, c), (x, y, 1 - c)
        self.chips = [(1 - x, y), (x, 1 - y), (1 - x, 1 - y)]

    def _copy(self, a, k, block, to, from_input=False):
        part = self.slots[a](self.out_refs[a], 4 * block[0] + 2 * block[1] + block[2])
        return pltpu.make_async_remote_copy(
            src_ref=self.in_refs[a] if from_input else part, dst_ref=part,
            send_sem=self.send_sems.at[a, k], recv_sem=self.recv_sems.at[a, k], device_id=to, device_id_type=MESH)

    def _mine(self):
        return [pltpu.make_async_copy(self.in_refs[a], self.slots[a](self.out_refs[a], 4 * self.me[0] + 2 * self.me[1]
                                                                    + self.me[2]), self.local_sems.at[a])
                for a in range(self.n)]

    def _first(self):
        first = [self._copy(a, 0, self.me, self.sibling, True) for a in range(self.n)]
        return first + [self._copy(a, 1 + j, self.me, (*chip, self.c), True)
                        for j, chip in enumerate(self.chips) for a in range(self.n)]

    def _passed(self):
        return [self._copy(a, 4 + j, (*chip, self.c), self.sibling)
                for j, chip in enumerate(self.chips) for a in range(self.n)]

    def start(self):
        for cp in self._mine() + self._first():
            cp.start()

    def forward(self):
        passed = self._passed()
        for j, chip in enumerate(self.chips):
            for a in range(self.n):
                self._copy(a, 1 + j, (*chip, self.c), self.me).wait_recv()
                passed[j * self.n + a].start()

    def finish(self):
        for a in range(self.n):
            self._copy(a, 0, self.sibling, self.me).wait_recv()
        for j, chip in enumerate(self.chips):
            for a in range(self.n):
                self._copy(a, 4 + j, (*chip, 1 - self.c), self.me).wait_recv()
        for cp in self._first() + self._passed():
            cp.wait_send()
        for cp in self._mine():
            cp.wait()


class _Exchange:
    def __init__(self, in_refs, out_refs, slots, send_sems, recv_sems, local_sems):
        self.in_refs, self.out_refs, self.slots = in_refs, out_refs, slots
        self.send_sems, self.recv_sems, self.local_sems = send_sems, recv_sems, local_sems
        self.n = len(in_refs)
        self.pos = _position()

    def _copies(self):
        x, y, c = self.pos
        me = 4 * x + 2 * y + c
        mine = [pltpu.make_async_copy(self.slots[a](self.in_refs[a], me), self.out_refs[a].at[me],
                                      self.local_sems.at[a]) for a in range(self.n)]
        remote = []
        for k in range(1, N_DEV):
            px, py, pc = x ^ (k >> 2), y ^ ((k >> 1) & 1), c ^ (k & 1)
            for a in range(self.n):
                remote.append(pltpu.make_async_remote_copy(
                    src_ref=self.slots[a](self.in_refs[a], 4 * px + 2 * py + pc), dst_ref=self.out_refs[a].at[me],
                    send_sem=self.send_sems.at[a, k - 1], recv_sem=self.recv_sems.at[a, k - 1],
                    device_id=(px, py, pc), device_id_type=MESH))
        return mine, remote

    def start(self):
        mine, remote = self._copies()
        for cp in mine + remote:
            cp.start()

    def forward(self):
        pass

    def finish(self):
        mine, remote = self._copies()
        for cp in remote:
            cp.wait_recv()
        for cp in remote:
            cp.wait_send()
        for cp in mine:
            cp.wait()


class _Rider:
    def __init__(self, kind, arrays, out_shapes, slots):
        self.kind, self.arrays, self.slots = kind, list(arrays), slots
        self.n = len(self.arrays)
        hbm = pl.BlockSpec(memory_space=pl.ANY)
        self.in_specs = [hbm] * self.n
        self.out_specs = [hbm] * self.n
        self.out_shape = [jax.ShapeDtypeStruct(tuple(s), a.dtype) for s, a in zip(out_shapes, self.arrays)]
        self.scratch = [pltpu.SemaphoreType.DMA((self.n, 7)), pltpu.SemaphoreType.DMA((self.n, 7)),
                        pltpu.SemaphoreType.DMA((self.n,))]

    def bind(self, in_refs, out_refs, sems):
        return self.kind(in_refs, out_refs, self.slots, *sems)


def _gather_rider(shards, full_shapes, slots):
    return _Rider(_Gather, shards, full_shapes, slots)


def _exchange_rider(sends, part_shapes, slots):
    return _Rider(_Exchange, sends, [(N_DEV,) + tuple(s) for s in part_shapes], slots)


def _split_refs(refs, n_in, n_out, n_scratch, rider):
    k = rider.n if rider is not None else 0
    ins, r_ins = refs[:n_in], refs[n_in:n_in + k]
    outs, r_outs = refs[n_in + k:n_in + k + n_out], refs[n_in + k + n_out:n_in + 2 * k + n_out]
    rest = refs[n_in + 2 * k + n_out:]
    scratch, sems = rest[:n_scratch], rest[n_scratch:]
    comm = rider.bind(r_ins, r_outs, sems) if rider is not None else None
    return ins + outs + scratch, comm


def _ride_before(comm, i, nt):
    if comm is not None:
        pl.when(i == 0)(comm.start)
        pl.when(i == nt - 1)(comm.forward)


def _ride_after(comm, i, nt):
    if comm is not None:
        pl.when(i == nt - 1)(comm.finish)


def _extend(specs, rider, field):
    return list(specs) + (getattr(rider, field) if rider is not None else [])


def _comm_call(name, rider):
    def body(*refs):
        _, comm = _split_refs(refs, 0, 0, 0, rider)
        comm.start()
        comm.forward()
        comm.finish()

    return pl.pallas_call(body, name=name, in_specs=rider.in_specs, out_specs=rider.out_specs,
                          out_shape=rider.out_shape, scratch_shapes=rider.scratch)(*rider.arrays)


N_CHIPS = 4


def _reduce_scatter_call(grad, small_sends, small_part_shapes, small_slots):
    rows, cols = grad.shape[0], grad.shape[1] // N_DEV
    n_small = len(small_sends)

    def body(*refs):
        g_ref, small_in = refs[0], refs[1:1 + n_small]
        chip_ref, small_out = refs[1 + n_small], refs[2 + n_small:2 + 2 * n_small]
        (own_buf, recv_buf, part_buf, small_send, small_recv, small_local,
         swap_send, swap_recv, local_sems, chip_send, chip_recv) = refs[2 + 2 * n_small:]
        x, y, c = _position()
        my_chip = 2 * x + y
        small = _Exchange(small_in, small_out, small_slots, small_send, small_recv, small_local)
        small.start()

        def block(chip, core):
            return g_ref.at[:, pl.ds(pl.multiple_of((2 * chip + core) * cols, cols), cols)]

        swap = [pltpu.make_async_remote_copy(
            src_ref=block(q, 1 - c), dst_ref=recv_buf.at[q], send_sem=swap_send.at[q], recv_sem=swap_recv.at[q],
            device_id=(x, y, 1 - c), device_id_type=MESH) for q in range(N_CHIPS)]
        mine = [pltpu.make_async_copy(block(q, c), own_buf.at[q], local_sems.at[q]) for q in range(N_CHIPS)]
        for cp in swap + mine:
            cp.start()
        for q in range(N_CHIPS):
            mine[q].wait()
            swap[q].wait_recv()
            part_buf[q] = (own_buf[q].astype(F32) + recv_buf[q].astype(F32)).astype(BF16)
        to_chips = []
        for k in range(1, N_CHIPS):
            px, py = x ^ (k >> 1), y ^ (k & 1)
            to_chips.append(pltpu.make_async_remote_copy(
                src_ref=part_buf.at[2 * px + py], dst_ref=chip_ref.at[my_chip],
                send_sem=chip_send.at[k - 1], recv_sem=chip_recv.at[k - 1],
                device_id=(px, py, c), device_id_type=MESH))
        own = pltpu.make_async_copy(part_buf.at[my_chip], chip_ref.at[my_chip], local_sems.at[N_CHIPS])
        for cp in to_chips + [own]:
            cp.start()
        for cp in to_chips:
            cp.wait_recv()
        for cp in to_chips + swap:
            cp.wait_send()
        own.wait()
        small.finish()

    hbm = pl.BlockSpec(memory_space=pl.ANY)
    half = pltpu.VMEM((N_CHIPS, rows, cols), grad.dtype)
    return pl.pallas_call(
        body, name="grads_reduce_scatter", in_specs=[hbm] * (1 + n_small), out_specs=[hbm] * (1 + n_small),
        out_shape=[jax.ShapeDtypeStruct((N_CHIPS, rows, cols), grad.dtype)]
        + [jax.ShapeDtypeStruct((N_DEV,) + tuple(s), a.dtype) for s, a in zip(small_part_shapes, small_sends)],
        scratch_shapes=[half, half, half,
                        pltpu.SemaphoreType.DMA((n_small, 7)), pltpu.SemaphoreType.DMA((n_small, 7)),
                        pltpu.SemaphoreType.DMA((n_small,)),
                        pltpu.SemaphoreType.DMA((N_CHIPS,)), pltpu.SemaphoreType.DMA((N_CHIPS,)),
                        pltpu.SemaphoreType.DMA((N_CHIPS + 1,)),
                        pltpu.SemaphoreType.DMA((N_CHIPS - 1,)), pltpu.SemaphoreType.DMA((N_CHIPS - 1,))],
        compiler_params=pltpu.CompilerParams(vmem_limit_bytes=VMEM_LIMIT),
    )(grad, *small_sends)


def _adamw(w, g, m, v):
    m = ADAM_B1 * m + (1.0 - ADAM_B1) * g
    v = ADAM_B2 * v + (1.0 - ADAM_B2) * (g * g)
    m_hat = m / (1.0 - ADAM_B1 ** ADAM_STEP)
    v_hat = v / (1.0 - ADAM_B2 ** ADAM_STEP)
    delta = -ADAM_LR * (m_hat / (jnp.sqrt(v_hat) + ADAM_EPS) + ADAM_WD * w)
    return delta, m, v


def _sum_parts(parts_ref, index=()):
    g = parts_ref[(0,) + index].astype(F32)
    for s in range(1, parts_ref.shape[0]):
        g = g + parts_ref[(s,) + index].astype(F32)
    return g


def _adamw_call(name, parts, w, m, v, block_rows):
    rows, cols = w.shape
    nb = rows // block_rows
    senders = parts.shape[0]

    def body(parts_ref, w_ref, m_ref, v_ref, g_ref, delta_ref, m_out, v_out):
        g = _sum_parts(parts_ref)
        delta, m_new, v_new = _adamw(w_ref[...], g, m_ref[...], v_ref[...])
        g_ref[...] = g
        delta_ref[...] = delta
        m_out[...] = m_new
        v_out[...] = v_new

    blk = pl.BlockSpec((block_rows, cols), lambda i: (i, 0))
    return pl.pallas_call(
        body, name=name, grid=(nb,),
        in_specs=[pl.BlockSpec((senders, block_rows, cols), lambda i: (0, i, 0)), blk, blk, blk],
        out_specs=[blk, blk, blk, blk],
        out_shape=[jax.ShapeDtypeStruct((rows, cols), F32)] * 4,
        compiler_params=_params(("parallel",)),
    )(parts, w, m, v)


def _adamw_cols_call(name, parts_a, parts_b, w, m, v, block_cols):
    rows, cols = w.shape
    na, nb = parts_a.shape[2] // block_cols, parts_b.shape[2] // block_cols

    def body(pa_ref, pb_ref, w_ref, m_ref, v_ref, g_ref, delta_ref, m_out, v_out):
        def update(parts_ref):
            g = _sum_parts(parts_ref)
            delta, m_new, v_new = _adamw(w_ref[...], g, m_ref[...], v_ref[...])
            g_ref[...] = g
            delta_ref[...] = delta
            m_out[...] = m_new
            v_out[...] = v_new

        pl.when(pl.program_id(0) < na)(functools.partial(update, pa_ref))
        pl.when(pl.program_id(0) >= na)(functools.partial(update, pb_ref))

    blk = pl.BlockSpec((rows, block_cols), lambda i: (0, i))
    return pl.pallas_call(
        body, name=name, grid=(na + nb,),
        in_specs=[pl.BlockSpec((N_DEV, rows, block_cols), lambda i: (0, 0, jnp.minimum(i, na - 1))),
                  pl.BlockSpec((N_DEV, rows, block_cols), lambda i: (0, 0, jnp.maximum(i - na, 0))), blk, blk, blk],
        out_specs=[blk, blk, blk, blk],
        out_shape=[jax.ShapeDtypeStruct((rows, cols), F32)] * 4,
        compiler_params=_params(("arbitrary",)),
    )(parts_a, parts_b, w, m, v)


WIDE_ROWS = 8
NARROW_ROWS = 40
NARROW_GKW_ROW = 8
NARROW_GKB_ROW = 24
NARROW_HW_ROW = 32
GROUP_SHARD = POOL_GROUP_DIM // N_DEV
KEY_SHARD = GLA_KEY_WIDTH // N_DEV
HEAD_V_SHARD = GLA_HEAD_V // N_DEV


def _small_adamw_call(wide, narrow, w, m, v):
    names = ("norm_w", "pool_scale", "final_norm_w", "pool_group_b", "gla_gk_w", "gla_gk_b", "gla_head_norm_w")
    where = {
        "norm_w": (0, slice(0, 2), slice(None)),
        "pool_scale": (0, slice(2, 3), slice(None)),
        "final_norm_w": (0, slice(3, 4), slice(None)),
        "pool_group_b": (1, slice(0, POOL_GROUPS), slice(0, GROUP_SHARD)),
        "gla_gk_w": (1, slice(NARROW_GKW_ROW, NARROW_GKW_ROW + GLA_GATE_RANK), slice(0, KEY_SHARD)),
        "gla_gk_b": (1, slice(NARROW_GKB_ROW, NARROW_GKB_ROW + 1), slice(0, KEY_SHARD)),
        "gla_head_norm_w": (1, slice(NARROW_HW_ROW, NARROW_HW_ROW + 1), slice(0, HEAD_V_SHARD)),
    }
    k = len(names)

    def body(*refs):
        parts = refs[0:2]
        w_refs, m_refs, v_refs = refs[2:2 + k], refs[2 + k:2 + 2 * k], refs[2 + 2 * k:2 + 3 * k]
        outs = refs[2 + 3 * k:]
        loss_ref = outs[0]
        loss_ref[...] = _sum_parts(parts[0], (slice(4, 5), slice(0, 1)))
        for i, name in enumerate(names):
            buf, rows, cols = where[name]
            g = _sum_parts(parts[buf], (rows, cols))
            delta, m_new, v_new = _adamw(w_refs[i][...], g, m_refs[i][...], v_refs[i][...])
            outs[1 + i][...] = g
            outs[1 + k + i][...] = delta
            outs[1 + 2 * k + i][...] = m_new
            outs[1 + 3 * k + i][...] = v_new

    vmem = pl.BlockSpec(memory_space=pltpu.VMEM)
    shapes = [jax.ShapeDtypeStruct(w[n].shape, F32) for n in names]
    res = pl.pallas_call(
        body, name="adamw_small", in_specs=[vmem] * (2 + 3 * k), out_specs=[vmem] * (1 + 4 * k),
        out_shape=[jax.ShapeDtypeStruct((1, 1), F32)] + shapes * 4,
    )(wide, narrow, *[w[n] for n in names], *[m[n] for n in names], *[v[n] for n in names])
    unzip = lambda j: dict(zip(names, res[1 + j * k:1 + (j + 1) * k]))
    return res[0], unzip(0), unzip(1), unzip(2), unzip(3)


def kernel(x, norm_w, pool_in_w, pool_group_w, pool_group_b, pool_scale, pool_out_w, gla_in_w, gla_gk_w, gla_gk_b, gla_head_norm_w, gla_out_w, final_norm_w, loss_target, m_norm_w, m_pool_in_w, m_pool_group_w, m_pool_group_b, m_pool_scale, m_pool_out_w, m_gla_in_w, m_gla_gk_w, m_gla_gk_b, m_gla_head_norm_w, m_gla_out_w, m_final_norm_w, v_norm_w, v_pool_in_w, v_pool_group_w, v_pool_group_b, v_pool_scale, v_pool_out_w, v_gla_in_w, v_gla_gk_w, v_gla_gk_b, v_gla_head_norm_w, v_gla_out_w, v_final_norm_w):
    w = dict(norm_w=norm_w, pool_in_w=pool_in_w, pool_group_w=pool_group_w, pool_group_b=pool_group_b,
             pool_scale=pool_scale, pool_out_w=pool_out_w, gla_in_w=gla_in_w, gla_gk_w=gla_gk_w, gla_gk_b=gla_gk_b,
             gla_head_norm_w=gla_head_norm_w, gla_out_w=gla_out_w, final_norm_w=final_norm_w)
    m = dict(norm_w=m_norm_w, pool_in_w=m_pool_in_w, pool_group_w=m_pool_group_w, pool_group_b=m_pool_group_b,
             pool_scale=m_pool_scale, pool_out_w=m_pool_out_w, gla_in_w=m_gla_in_w, gla_gk_w=m_gla_gk_w,
             gla_gk_b=m_gla_gk_b, gla_head_norm_w=m_gla_head_norm_w, gla_out_w=m_gla_out_w,
             final_norm_w=m_final_norm_w)
    v = dict(norm_w=v_norm_w, pool_in_w=v_pool_in_w, pool_group_w=v_pool_group_w, pool_group_b=v_pool_group_b,
             pool_scale=v_pool_scale, pool_out_w=v_pool_out_w, gla_in_w=v_gla_in_w, gla_gk_w=v_gla_gk_w,
             gla_gk_b=v_gla_gk_b, gla_head_norm_w=v_gla_head_norm_w, gla_out_w=v_gla_out_w,
             final_norm_w=v_final_norm_w)
    col_shard = GLA_IN_WIDTH // N_DEV
    row_shard = D_MODEL // N_DEV

    def lanes(a):
        return jnp.pad(a, [(0, 0)] * (a.ndim - 1) + [(0, LANES - a.shape[-1])])

    small_in = jnp.concatenate([lanes(pool_group_b[0]), lanes(gla_gk_b), lanes(gla_head_norm_w),
                                jnp.zeros((2, LANES), F32)], axis=0)
    in_cols = 2 * POOL_WIDTH // N_DEV
    pool_in, pool_gw, pool_out, small_all = _comm_call("pool_weights_all_gather", _gather_rider(
        [pool_in_w[0].astype(BF16), pool_group_w[0].astype(BF16), pool_out_w[0].astype(BF16), small_in],
        [(D_MODEL, 2 * POOL_WIDTH), (POOL_GROUPS, POOL_GROUP_DIM, POOL_GROUP_DIM), (POOL_WIDTH, D_MODEL),
         (N_DEV, 8, LANES)],
        [_dim1_slot(in_cols), _dim1_slot(GROUP_SHARD), _row_slot(row_shard), _lead_slot]))
    pool_gb = jnp.transpose(small_all[:, 0:POOL_GROUPS, :GROUP_SHARD], (1, 0, 2)).reshape(1, POOL_WIDTH)
    gla_gkb = small_all[:, POOL_GROUPS, :KEY_SHARD].reshape(1, GLA_KEY_WIDTH)
    gla_hw = jnp.tile(small_all[:, POOL_GROUPS + 1, :HEAD_V_SHARD].reshape(1, GLA_HEAD_V), (1, GLA_HEADS))
    nw0, nw1, wf = norm_w[0:1], norm_w[1:2], final_norm_w.reshape(1, D_MODEL)
    xs, target = x[0], loss_target[0]

    h1, p, gla_in_parts, gkw_parts, gla_out = _pool_fwd_call(
        xs, nw0, pool_in, pool_gw, pool_gb, pool_scale, pool_out, _gather_rider(
            [jnp.transpose(gla_in_w[0]).astype(BF16), gla_gk_w[0].astype(BF16), gla_out_w[0].astype(BF16)],
            [(N_DEV, col_shard, D_MODEL), (N_DEV, GLA_GATE_RANK, KEY_SHARD), (GLA_VALUE_WIDTH, D_MODEL)],
            [_lead_slot, _lead_slot, _row_slot(row_shard)]))
    gla_in = jnp.concatenate([gla_in_parts[d] for d in range(N_DEV)]
                             + [jnp.zeros((GLA_IN_PAD - GLA_IN_WIDTH, D_MODEL), BF16)], axis=0)
    gla_gkw = jnp.pad(jnp.transpose(gkw_parts, (1, 0, 2)).reshape(GLA_GATE_RANK, GLA_KEY_WIDTH),
                      ((0, GLA_LOW_PAD - GLA_GATE_RANK), (0, 0)))
    dh2, proj, o, states, loss_part, dwf = _gla_fwd_call(h1, nw1, gla_in, gla_gkw, gla_gkb, gla_hw, gla_out, wf, target)

    dproj, d_gla_out, dhw, dgkw, dgkb = _gla_bwd_call(dh2, proj, o, states, gla_gkw, gla_gkb, gla_hw, gla_out)
    dh1, d_gla_in, dnw1, landed_gla_out = _inproj_bwd_call(
        "gla_in_bwd", dproj, h1, nw1, gla_in, dh2,
        _exchange_rider([d_gla_out], [(row_shard, D_MODEL)], [_row_slot(row_shard)]), transposed=True)
    gla_in_send = jnp.stack([d_gla_in[d * col_shard:(d + 1) * col_shard] for d in range(N_DEV)])
    cols_a = GLA_IN_COLS_WITH_POOL_BWD
    dp, d_pool_out, dgw, dgb, dsc, landed_gla_in_a = _pool_bwd_call(
        dh1, p, pool_gw, pool_gb, pool_scale, pool_out,
        _exchange_rider([gla_in_send], [(col_shard, cols_a)], [lambda ref, d: ref.at[d, :, pl.ds(0, cols_a)]]))
    grad_x, d_pool_in, dnw0, landed_gla_in_b, landed_pool_out, landed_gw = _inproj_bwd_call(
        "pool_in_bwd", dp, xs, nw0, pool_in, dh1,
        _exchange_rider([gla_in_send, d_pool_out, dgw],
                        [(col_shard, D_MODEL - cols_a), (row_shard, D_MODEL),
                         (POOL_GROUPS, GROUP_SHARD, POOL_GROUP_DIM)],
                        [lambda ref, d: ref.at[d, :, pl.ds(cols_a, D_MODEL - cols_a)], _row_slot(row_shard),
                         _dim1_slot(GROUP_SHARD)]))

    wide = jnp.concatenate([
        dnw0, dnw1, dsc, dwf, jnp.pad(loss_part[0:1, 0:1], ((0, 0), (0, D_MODEL - 1))),
        jnp.zeros((WIDE_ROWS - 5, D_MODEL), F32)], axis=0)

    def rows8(a):
        return jnp.pad(lanes(a), ((0, 0), (0, -a.shape[1] % 8), (0, 0)))

    narrow = jnp.concatenate([
        rows8(jnp.transpose(dgb.reshape(POOL_GROUPS, N_DEV, GROUP_SHARD), (1, 0, 2))),
        rows8(jnp.transpose(dgkw[:GLA_GATE_RANK].reshape(GLA_GATE_RANK, N_DEV, KEY_SHARD), (1, 0, 2))),
        rows8(dgkb.reshape(N_DEV, 1, KEY_SHARD)),
        rows8(dhw.reshape(GLA_HEADS, GLA_HEAD_V).sum(axis=0).reshape(N_DEV, 1, HEAD_V_SHARD)),
    ], axis=1)
    landed_pool_in, landed_wide, landed_narrow = _reduce_scatter_call(
        d_pool_in, [wide, narrow], [(WIDE_ROWS, D_MODEL), (NARROW_ROWS, LANES)], [lambda ref, d: ref, _lead_slot])

    res = {}
    for name, parts, rows, cols, block in [
            ("pool_in_w", landed_pool_in, D_MODEL, in_cols, 256),
            ("pool_group_w", landed_gw, POOL_GROUPS * GROUP_SHARD, POOL_GROUP_DIM, 128),
            ("pool_out_w", landed_pool_out, row_shard, D_MODEL, 128),
            ("gla_out_w", landed_gla_out, row_shard, D_MODEL, 128)]:
        outs = _adamw_call("adamw_" + name, parts.reshape(parts.shape[0], rows, cols), w[name].reshape(rows, cols),
                           m[name].reshape(rows, cols), v[name].reshape(rows, cols), block)
        res[name] = [t.reshape(w[name].shape) for t in outs]
    outs = _adamw_cols_call("adamw_gla_in_w", landed_gla_in_a, landed_gla_in_b, jnp.transpose(gla_in_w[0]),
                            jnp.transpose(m_gla_in_w[0]), jnp.transpose(v_gla_in_w[0]), 256)
    res["gla_in_w"] = [jnp.transpose(t)[None] for t in outs]
    small_shapes = {"norm_w": (2, D_MODEL), "pool_scale": (1, D_MODEL), "final_norm_w": (1, D_MODEL),
                    "pool_group_b": (POOL_GROUPS, GROUP_SHARD), "gla_gk_w": (GLA_GATE_RANK, KEY_SHARD),
                    "gla_gk_b": (1, KEY_SHARD), "gla_head_norm_w": (1, HEAD_V_SHARD)}
    as_small = lambda t: {n: t[n].reshape(s) for n, s in small_shapes.items()}
    loss, *small_outs = _small_adamw_call(landed_wide, landed_narrow, as_small(w), as_small(m), as_small(v))
    for name in small_shapes:
        res[name] = [t[name].reshape(w[name].shape) for t in small_outs]
    order = ("norm_w", "pool_in_w", "pool_group_w", "pool_group_b", "pool_scale", "pool_out_w", "gla_in_w",
             "gla_gk_w", "gla_gk_b", "gla_head_norm_w", "gla_out_w", "final_norm_w")
    return (loss.reshape(()), grad_x[None], *[res[n][0] for n in order], *[res[n][1] for n in order],
            *[res[n][2] for n in order], *[res[n][3] for n in order])
```

```python
import functools

import jax
import jax.numpy as jnp
from jax import lax
from jax.experimental import pallas as pl
from jax.experimental.pallas import tpu as pltpu

F32 = jnp.float32
BF16 = jnp.bfloat16
MESH = pl.DeviceIdType.MESH

N_DEV = 8
D_MODEL = 1024
POOL_WIDTH = 1024
POOL_GROUPS = 4
POOL_GROUP_DIM = 256
POOL_HALO = 16
GLA_HEADS = 4
GLA_HEAD_K = 128
GLA_HEAD_V = 256
GLA_KEY_WIDTH = 512
GLA_VALUE_WIDTH = 1024
GLA_GATE_RANK = 16
GLA_IN_WIDTH = 3088
GLA_IN_PAD = 3200
GLA_LOW_PAD = 128
GLA_QKVG_WIDTH = 3072
CHUNK = 64
GATE_NORMALIZER = 16.0
RMS_EPS = 1e-6
Q_SCALE = GLA_HEAD_K ** -0.5

ADAM_LR = 0.001
ADAM_B1 = 0.9
ADAM_B2 = 0.999
ADAM_EPS = 1e-08
ADAM_WD = 0.01
ADAM_STEP = 10

LANES = 128
BF16_ROWS = 16
VMEM_LIMIT = 56 * 1024 * 1024
ROW_TILE = 256
MATMUL_ROW_TILE = 512
GLA_IN_SLABS_WITH_POOL_BWD = 145


def _dot_nn(a, b):
    return lax.dot_general(a, b, (((1,), (0,)), ((), ())), preferred_element_type=F32)


def _dot_nt(a, b):
    return lax.dot_general(a, b, (((1,), (1,)), ((), ())), preferred_element_type=F32)


def _dot_tn(a, b):
    return lax.dot_general(a, b, (((0,), (0,)), ((), ())), preferred_element_type=F32)


def _rms(x):
    rstd = lax.rsqrt(jnp.mean(x * x, axis=-1, keepdims=True) + RMS_EPS)
    return x * rstd, rstd


def _rms_bwd(dxhat, xhat, rstd):
    return rstd * (dxhat - xhat * jnp.mean(dxhat * xhat, axis=-1, keepdims=True))


def _sigmoid(x):
    return 1.0 / (1.0 + jnp.exp(-x))


def _params(sem=("arbitrary",)):
    return pltpu.CompilerParams(dimension_semantics=sem, vmem_limit_bytes=VMEM_LIMIT)


def _full(shape):
    return pl.BlockSpec(shape, lambda i: (0,) * len(shape))


def _const(shape):
    return pl.BlockSpec(shape, lambda i: (0,) * len(shape), pipeline_mode=pl.Buffered(1))


def _window_sums(ext, forward):
    n = ext.shape[0]
    outs = []
    for g in range(POOL_GROUPS):
        s = ext[:, g * POOL_GROUP_DIM:(g + 1) * POOL_GROUP_DIM]
        for k in range(g + 1):
            shift = (1 << k) if forward else n - (1 << k)
            s = s + pltpu.roll(s, shift, axis=0)
        outs.append(s[:n - POOL_HALO])
    return outs


def _inv_count(row0, tm):
    row = row0 + lax.broadcasted_iota(jnp.int32, (tm, 1), 0)
    return [1.0 / jnp.minimum(row + 1, 2 << g).astype(F32) for g in range(POOL_GROUPS)]


def _pool_mix(u, u_prev, row0, gw_ref, gb):
    tm = u.shape[0]
    sums = _window_sums(jnp.concatenate([u, u_prev], axis=0), True)
    inv = _inv_count(row0, tm)
    pooled, mixed = [], []
    for g in range(POOL_GROUPS):
        ug = u[:, g * POOL_GROUP_DIM:(g + 1) * POOL_GROUP_DIM]
        pg = (sums[g] * inv[g] - ug).astype(BF16)
        pooled.append(pg)
        mixed.append(_dot_nn(pg, gw_ref[g]))
    return pooled, jnp.concatenate(mixed, axis=1) + gb


def _pool_fwd_call(x, nw, w_in, gw, gb, sc, w_out, rider=None):
    seq = x.shape[0]
    tm = min(MATMUL_ROW_TILE, seq)
    nt = seq // tm

    def main(x_ref, nw_ref, win_ref, gw_ref, gb_ref, sc_ref, wout_ref, h_ref, p_ref, halo_ref):
        i = pl.program_id(0)

        @pl.when(i == 0)
        def _():
            halo_ref[...] = jnp.zeros_like(halo_ref)

        xt = x_ref[...]
        xhat, _ = _rms(xt)
        n = (xhat * nw_ref[...]).astype(BF16)
        p = _dot_nn(n, win_ref[...])
        p_ref[...] = p
        u = p[:, :POOL_WIDTH]
        gate = p[:, POOL_WIDTH:]
        _, mixed = _pool_mix(u, halo_ref[...], i * tm, gw_ref, gb_ref[...])
        halo_ref[...] = u[tm - POOL_HALO:, :]
        y = (mixed * sc_ref[...] * (gate * _sigmoid(gate))).astype(BF16)
        h_ref[...] = xt + _dot_nn(y, wout_ref[...])

    def body(*refs):
        own, comm = _split_refs(refs, 7, 2, 1, rider)
        _ride_before(comm, pl.program_id(0), nt)
        main(*own)
        _ride_after(comm, pl.program_id(0), nt)

    return pl.pallas_call(
        body, name="pool_fwd", grid=(nt,),
        in_specs=_extend([pl.BlockSpec((tm, D_MODEL), lambda i: (i, 0)), _const((1, D_MODEL)),
                          _const((D_MODEL, 2 * POOL_WIDTH)), _const((POOL_GROUPS, POOL_GROUP_DIM, POOL_GROUP_DIM)),
                          _const((1, POOL_WIDTH)), _const((1, POOL_WIDTH)), _const((POOL_WIDTH, D_MODEL))],
                         rider, "in_specs"),
        out_specs=_extend([pl.BlockSpec((tm, D_MODEL), lambda i: (i, 0)),
                           pl.BlockSpec((tm, 2 * POOL_WIDTH), lambda i: (i, 0))], rider, "out_specs"),
        out_shape=_extend([jax.ShapeDtypeStruct((seq, D_MODEL), F32),
                           jax.ShapeDtypeStruct((seq, 2 * POOL_WIDTH), F32)], rider, "out_shape"),
        scratch_shapes=_extend([pltpu.VMEM((POOL_HALO, POOL_WIDTH), F32)], rider, "scratch"),
        compiler_params=_params(),
    )(x, nw, w_in, gw, gb, sc, w_out, *_extend([], rider, "arrays"))


def _pool_bwd_call(dh, p, gw, gb, sc, w_out, rider=None):
    seq = dh.shape[0]
    tm = min(MATMUL_ROW_TILE, seq)
    nt = seq // tm
    halo_blocks = tm // POOL_HALO

    def main(dh_ref, p_ref, pprev_ref, gw_ref, gb_ref, sc_ref, wout_ref,
             dp_ref, dwout_hbm, dgw_hbm, dgb_ref, dsc_ref, carry_ref, dwout_acc, dgw_acc, dwout_stage, dgw_stage):
        i = pl.program_id(0)
        t = nt - 1 - i

        @pl.when(i == 0)
        def _():
            carry_ref[...] = jnp.zeros_like(carry_ref)
            dwout_acc[...] = jnp.zeros_like(dwout_acc)
            dgw_acc[...] = jnp.zeros_like(dgw_acc)
            dgb_ref[...] = jnp.zeros_like(dgb_ref)
            dsc_ref[...] = jnp.zeros_like(dsc_ref)

        p = p_ref[...]
        u = p[:, :POOL_WIDTH]
        gate = p[:, POOL_WIDTH:]
        u_prev = jnp.where(t > 0, pprev_ref[:, :POOL_WIDTH], 0.0)
        pooled, mixed = _pool_mix(u, u_prev, t * tm, gw_ref, gb_ref[...])
        sg = _sigmoid(gate)
        silu = gate * sg
        sc = sc_ref[...]
        dhb = dh_ref[...].astype(BF16)
        y = (mixed * sc * silu).astype(BF16)
        dwout_acc[...] += _dot_tn(y, dhb)
        dy = _dot_nt(dhb, wout_ref[...])
        dmixed = dy * sc * silu
        dsc_ref[...] += jnp.sum(dy * mixed * silu, axis=0, keepdims=True)
        dgate = dy * mixed * sc * (sg * (1.0 + gate * (1.0 - sg)))
        dgb_ref[...] += jnp.sum(dmixed, axis=0, keepdims=True)
        inv = _inv_count(t * tm, tm)
        dpooled, scaled = [], []
        for g in range(POOL_GROUPS):
            dmg = dmixed[:, g * POOL_GROUP_DIM:(g + 1) * POOL_GROUP_DIM].astype(BF16)
            dgw_acc[g] += _dot_tn(pooled[g], dmg)
            dpg = _dot_nt(dmg, gw_ref[g])
            dpooled.append(dpg)
            scaled.append(dpg * inv[g])
        r = jnp.concatenate(scaled, axis=1)
        sums = _window_sums(jnp.concatenate([r, carry_ref[...]], axis=0), False)
        carry_ref[...] = r[:POOL_HALO, :]
        du = jnp.concatenate([sums[g] - dpooled[g] for g in range(POOL_GROUPS)], axis=1)
        dp_ref[...] = jnp.concatenate([du, dgate], axis=1).astype(BF16)

        @pl.when(i == nt - 1)
        def _():
            dwout_stage[...] = dwout_acc[...].astype(BF16)
            dgw_stage[...] = dgw_acc[...].astype(BF16)
            pltpu.sync_copy(dwout_stage, dwout_hbm)
            pltpu.sync_copy(dgw_stage, dgw_hbm)

    def body(*refs):
        own, comm = _split_refs(refs, 7, 5, 5, rider)
        _ride_before(comm, pl.program_id(0), nt)
        main(*own)
        _ride_after(comm, pl.program_id(0), nt)

    rev = lambda i: (nt - 1 - i, 0)
    return pl.pallas_call(
        body, name="pool_bwd", grid=(nt,),
        in_specs=_extend([pl.BlockSpec((tm, D_MODEL), rev), pl.BlockSpec((tm, 2 * POOL_WIDTH), rev),
                          pl.BlockSpec((POOL_HALO, 2 * POOL_WIDTH),
                                       lambda i: (jnp.maximum((nt - 1 - i) * halo_blocks - 1, 0), 0)),
                          _const((POOL_GROUPS, POOL_GROUP_DIM, POOL_GROUP_DIM)), _const((1, POOL_WIDTH)),
                          _const((1, POOL_WIDTH)), _const((POOL_WIDTH, D_MODEL))], rider, "in_specs"),
        out_specs=_extend([pl.BlockSpec((tm, 2 * POOL_WIDTH), rev), pl.BlockSpec(memory_space=pl.ANY),
                           pl.BlockSpec(memory_space=pl.ANY), _full((1, POOL_WIDTH)), _full((1, POOL_WIDTH))],
                          rider, "out_specs"),
        out_shape=_extend([jax.ShapeDtypeStruct((seq, 2 * POOL_WIDTH), BF16),
                           jax.ShapeDtypeStruct((POOL_WIDTH, D_MODEL), BF16),
                           jax.ShapeDtypeStruct((POOL_GROUPS, POOL_GROUP_DIM, POOL_GROUP_DIM), BF16),
                           jax.ShapeDtypeStruct((1, POOL_WIDTH), F32), jax.ShapeDtypeStruct((1, POOL_WIDTH), F32)],
                          rider, "out_shape"),
        scratch_shapes=_extend([pltpu.VMEM((POOL_HALO, POOL_WIDTH), F32), pltpu.VMEM((POOL_WIDTH, D_MODEL), F32),
                                pltpu.VMEM((POOL_GROUPS, POOL_GROUP_DIM, POOL_GROUP_DIM), F32),
                                pltpu.VMEM((POOL_WIDTH, D_MODEL), BF16),
                                pltpu.VMEM((POOL_GROUPS, POOL_GROUP_DIM, POOL_GROUP_DIM), BF16)], rider, "scratch"),
        compiler_params=_params(),
    )(dh, p, p, gw, gb, sc, w_out, *_extend([], rider, "arrays"))


def _inproj_bwd_call(name, dproj, h_in, nw, w_in, dres, rider=None, transposed=False):
    seq = h_in.shape[0]
    width = dproj.shape[1]
    w_shape = tuple(w_in.shape)
    tm = min(MATMUL_ROW_TILE, seq)
    nt = seq // tm

    def main(dproj_ref, h_ref, nw_ref, win_ref, dres_ref, dh_ref, dw_hbm, dnw_ref, dw_acc, dw_stage):
        i = pl.program_id(0)

        @pl.when(i == 0)
        def _():
            dw_acc[...] = jnp.zeros_like(dw_acc)
            dnw_ref[...] = jnp.zeros_like(dnw_ref)

        xhat, rstd = _rms(h_ref[...])
        nw_row = nw_ref[...]
        n = (xhat * nw_row).astype(BF16)
        dpb = dproj_ref[...]
        if transposed:
            dw_acc[...] += _dot_tn(dpb, n)
            dn = _dot_nn(dpb, win_ref[...])
        else:
            dw_acc[...] += _dot_tn(n, dpb)
            dn = _dot_nt(dpb, win_ref[...])
        dnw_ref[...] += jnp.sum(dn * xhat, axis=0, keepdims=True)
        dh_ref[...] = _rms_bwd(dn * nw_row, xhat, rstd) + dres_ref[...]

        @pl.when(i == nt - 1)
        def _():
            dw_stage[...] = dw_acc[...].astype(BF16)
            pltpu.sync_copy(dw_stage, dw_hbm)

    def body(*refs):
        own, comm = _split_refs(refs, 5, 3, 2, rider)
        _ride_before(comm, pl.program_id(0), nt)
        main(*own)
        _ride_after(comm, pl.program_id(0), nt)

    row = lambda i: (i, 0)
    return pl.pallas_call(
        body, name=name, grid=(nt,),
        in_specs=_extend([pl.BlockSpec((tm, width), row), pl.BlockSpec((tm, D_MODEL), row), _const((1, D_MODEL)),
                          _const(w_shape), pl.BlockSpec((tm, D_MODEL), row)], rider, "in_specs"),
        out_specs=_extend([pl.BlockSpec((tm, D_MODEL), row), pl.BlockSpec(memory_space=pl.ANY),
                           _full((1, D_MODEL))], rider, "out_specs"),
        out_shape=_extend([jax.ShapeDtypeStruct((seq, D_MODEL), F32), jax.ShapeDtypeStruct(w_shape, BF16),
                           jax.ShapeDtypeStruct((1, D_MODEL), F32)], rider, "out_shape"),
        scratch_shapes=_extend([pltpu.VMEM(w_shape, F32), pltpu.VMEM(w_shape, BF16)], rider, "scratch"),
        compiler_params=_params(),
    )(dproj, h_in, nw, w_in, dres, *_extend([], rider, "arrays"))


def _chunk_scan(x, reverse):
    n = x.shape[0]
    pos = lax.broadcasted_iota(jnp.int32, (n, 1), 0) & (CHUNK - 1)
    k = 1
    while k < CHUNK:
        if reverse:
            x = x + jnp.where(pos < CHUNK - k, pltpu.roll(x, n - k, axis=0), 0.0)
        else:
            x = x + jnp.where(pos >= k, pltpu.roll(x, k, axis=0), 0.0)
        k *= 2
    return x


class _GlaTile:
    def __init__(self, q, k, v, gate, low, gkw_ref, gkb):
        tm = q.shape[0]
        self.q = q * Q_SCALE
        self.k, self.v, self.gate = k, v, gate
        self.low_b = low.astype(BF16)
        self.z = _dot_nn(self.low_b, gkw_ref[...]) + gkb
        log_g = (jnp.minimum(self.z, 0.0) - jnp.log(1.0 + jnp.exp(-jnp.abs(self.z)))) / GATE_NORMALIZER
        self.c = _chunk_scan(log_g, False)
        is_last = lax.broadcasted_iota(jnp.int32, (CHUNK, 1), 0) == CHUNK - 1
        last = [jnp.sum(jnp.where(is_last, self.c[j * CHUNK:(j + 1) * CHUNK, :], 0.0), axis=0, keepdims=True)
                for j in range(tm // CHUNK)]
        self.c_last = last
        c_last_rows = jnp.concatenate([jnp.broadcast_to(r, (CHUNK, GLA_KEY_WIDTH)) for r in last], axis=0)
        self.e_pos = jnp.exp(self.c)
        self.e_neg = jnp.exp(-self.c)
        self.e_rest = jnp.exp(c_last_rows - self.c)
        self.a_b = (self.q * self.e_pos).astype(BF16)
        self.b_b = (self.k * self.e_neg).astype(BF16)
        self.cn_b = (self.q * self.e_neg).astype(BF16)
        self.dp_b = (self.k * self.e_pos).astype(BF16)
        self.kd_b = (self.k * self.e_rest).astype(BF16)
        self.v_b = self.v.astype(BF16)
        idx_t = lax.broadcasted_iota(jnp.int32, (tm, tm), 0)
        idx_s = lax.broadcasted_iota(jnp.int32, (tm, tm), 1)
        same_chunk = (idx_t ^ idx_s) < CHUNK
        self.lower = same_chunk & (idx_t >= idx_s)
        self.upper = same_chunk & (idx_t < idx_s)

    @staticmethod
    def rows(j):
        return slice(j * CHUNK, (j + 1) * CHUNK)

    @staticmethod
    def kcols(h):
        return slice(h * GLA_HEAD_K, (h + 1) * GLA_HEAD_K)

    @staticmethod
    def vcols(h):
        return slice(h * GLA_HEAD_V, (h + 1) * GLA_HEAD_V)

    def scores(self, h):
        kc = self.kcols(h)
        fwd = _dot_nt(self.a_b[:, kc], self.b_b[:, kc])
        bwd = _dot_nt(self.cn_b[:, kc], self.dp_b[:, kc])
        return jnp.where(self.lower, fwd, jnp.where(self.upper, bwd, 0.0)).astype(BF16)


def _gla_fwd_call(h1, nw, w_in, gkw, gkb, hw, w_out, wf, target):
    seq = h1.shape[0]
    tm = ROW_TILE
    nt = seq // tm
    cpt = tm // CHUNK
    n_chunks = seq // CHUNK

    def body(h_ref, nw_ref, win_ref, gkw_ref, gkb_ref, hw_ref, wout_ref, wf_ref, tgt_ref,
             dh2_ref, proj_ref, o_ref, st_ref, loss_ref, dwf_ref, state_ref):
        i = pl.program_id(0)

        @pl.when(i == 0)
        def _():
            state_ref[...] = jnp.zeros_like(state_ref)
            loss_ref[...] = jnp.zeros_like(loss_ref)
            dwf_ref[...] = jnp.zeros_like(dwf_ref)

        ht = h_ref[...]
        xhat, _ = _rms(ht)
        n = (xhat * nw_ref[...]).astype(BF16)
        sections = {}
        for name, lo, hi in (("low", GLA_QKVG_WIDTH, GLA_IN_PAD), ("qk", 0, 2 * GLA_KEY_WIDTH),
                             ("v", 2 * GLA_KEY_WIDTH, GLA_QKVG_WIDTH - GLA_VALUE_WIDTH),
                             ("gate", GLA_QKVG_WIDTH - GLA_VALUE_WIDTH, GLA_QKVG_WIDTH)):
            sections[name] = _dot_nt(n, win_ref[lo:hi, :])
            proj_ref[:, lo:hi] = sections[name]
        g = _GlaTile(sections["qk"][:, :GLA_KEY_WIDTH], sections["qk"][:, GLA_KEY_WIDTH:], sections["v"],
                     sections["gate"], sections["low"], gkw_ref, gkb_ref[...])
        o_heads = []
        for h in range(GLA_HEADS):
            kc, vc = g.kcols(h), g.vcols(h)
            srows = slice(h * GLA_HEAD_V, (h + 1) * GLA_HEAD_V)
            o_intra = _dot_nn(g.scores(h), g.v_b[:, vc])
            state = state_ref[srows, :]
            o_rows = []
            for j in range(cpt):
                r = g.rows(j)
                st_ref[j, srows, :] = state
                o_rows.append(o_intra[r] + _dot_nt(g.a_b[r, kc], state.astype(BF16)))
                decay = jnp.exp(g.c_last[j][:, kc])
                state = state * decay + _dot_tn(g.v_b[r, vc], g.kd_b[r, kc])
            state_ref[srows, :] = state
            o_heads.append(jnp.concatenate(o_rows, axis=0))
        o = jnp.concatenate(o_heads, axis=1)
        o_ref[...] = o
        hw_row = hw_ref[...]
        on = jnp.concatenate([_rms(o[:, g.vcols(h)])[0] for h in range(GLA_HEADS)], axis=1) * hw_row
        y = (on * (g.gate * _sigmoid(g.gate))).astype(BF16)
        h2 = ht + _dot_nn(y, wout_ref[...])
        xhat2, rstd2 = _rms(h2)
        wf_row = wf_ref[...]
        err = xhat2 * wf_row - tgt_ref[...]
        loss_ref[...] += 0.5 * jnp.sum(err * err) / D_MODEL
        dout = err * (1.0 / D_MODEL)
        dwf_ref[...] += jnp.sum(dout * xhat2, axis=0, keepdims=True)
        dh2_ref[...] = _rms_bwd(dout * wf_row, xhat2, rstd2)

    row = lambda i: (i, 0)
    return pl.pallas_call(
        body, name="gla_fwd", grid=(nt,),
        in_specs=[pl.BlockSpec((tm, D_MODEL), row), _const((1, D_MODEL)), _const((GLA_IN_PAD, D_MODEL)),
                  _const((GLA_LOW_PAD, GLA_KEY_WIDTH)), _const((1, GLA_KEY_WIDTH)), _const((1, GLA_VALUE_WIDTH)),
                  _const((GLA_VALUE_WIDTH, D_MODEL)), _const((1, D_MODEL)), pl.BlockSpec((tm, D_MODEL), row)],
        out_specs=[pl.BlockSpec((tm, D_MODEL), row), pl.BlockSpec((tm, GLA_IN_PAD), row),
                   pl.BlockSpec((tm, GLA_VALUE_WIDTH), row),
                   pl.BlockSpec((cpt, GLA_VALUE_WIDTH, GLA_HEAD_K), lambda i: (i, 0, 0)),
                   _full((8, LANES)), _full((1, D_MODEL))],
        out_shape=[jax.ShapeDtypeStruct((seq, D_MODEL), F32), jax.ShapeDtypeStruct((seq, GLA_IN_PAD), F32),
                   jax.ShapeDtypeStruct((seq, GLA_VALUE_WIDTH), F32),
                   jax.ShapeDtypeStruct((n_chunks, GLA_VALUE_WIDTH, GLA_HEAD_K), F32),
                   jax.ShapeDtypeStruct((8, LANES), F32), jax.ShapeDtypeStruct((1, D_MODEL), F32)],
        scratch_shapes=[pltpu.VMEM((GLA_VALUE_WIDTH, GLA_HEAD_K), F32)],
        compiler_params=_params(),
    )(h1, nw, w_in, gkw, gkb, hw, w_out, wf, target)


def _gla_bwd_call(dh2, proj, o, states, gkw, gkb, hw, w_out):
    seq = dh2.shape[0]
    tm = ROW_TILE
    nt = seq // tm
    cpt = tm // CHUNK

    def body(dh_ref, proj_ref, o_ref, st_ref, gkw_ref, gkb_ref, hw_ref, wout_ref,
             dproj_ref, dwout_hbm, dhw_ref, dgkw_ref, dgkb_ref, dstate_ref, dwout_acc, dwout_stage):
        i = pl.program_id(0)

        @pl.when(i == 0)
        def _():
            dstate_ref[...] = jnp.zeros_like(dstate_ref)
            dwout_acc[...] = jnp.zeros_like(dwout_acc)
            dhw_ref[...] = jnp.zeros_like(dhw_ref)
            dgkw_ref[...] = jnp.zeros_like(dgkw_ref)
            dgkb_ref[...] = jnp.zeros_like(dgkb_ref)

        g = _GlaTile(proj_ref[:, :GLA_KEY_WIDTH], proj_ref[:, GLA_KEY_WIDTH:2 * GLA_KEY_WIDTH],
                     proj_ref[:, 2 * GLA_KEY_WIDTH:GLA_QKVG_WIDTH - GLA_VALUE_WIDTH],
                     proj_ref[:, GLA_QKVG_WIDTH - GLA_VALUE_WIDTH:GLA_QKVG_WIDTH], proj_ref[:, GLA_QKVG_WIDTH:],
                     gkw_ref, gkb_ref[...])
        dhb = dh_ref[...].astype(BF16)
        o = o_ref[...]
        hw_row = hw_ref[...]
        dy = _dot_nt(dhb, wout_ref[...])
        sg = _sigmoid(g.gate)
        silu = g.gate * sg
        don = dy * silu
        on_parts, do_parts, dhw_parts = [], [], []
        for h in range(GLA_HEADS):
            vc = g.vcols(h)
            xh, rs = _rms(o[:, vc])
            on_parts.append(xh * hw_row[:, vc])
            dhw_parts.append(jnp.sum(don[:, vc] * xh, axis=0, keepdims=True))
            do_parts.append(_rms_bwd(don[:, vc] * hw_row[:, vc], xh, rs))
        on = jnp.concatenate(on_parts, axis=1)
        dwout_acc[...] += _dot_tn((on * silu).astype(BF16), dhb)
        dhw_ref[...] += jnp.concatenate(dhw_parts, axis=1)
        dgate = dy * on * (sg * (1.0 + g.gate * (1.0 - sg)))
        do_b = jnp.concatenate(do_parts, axis=1).astype(BF16)

        last_row = lax.broadcasted_iota(jnp.int32, (CHUNK, 1), 0) == CHUNK - 1
        dq_h, dk_h, dv_h, dc_h = [], [], [], []
        for h in range(GLA_HEADS):
            kc, vc = g.kcols(h), g.vcols(h)
            srows = slice(h * GLA_HEAD_V, (h + 1) * GLA_HEAD_V)
            scores = g.scores(h)
            dscores = _dot_nt(do_b[:, vc], g.v_b[:, vc])
            dfwd = jnp.where(g.lower, dscores, 0.0).astype(BF16)
            dbwd = jnp.where(g.upper, dscores, 0.0).astype(BF16)
            dv_intra = _dot_tn(scores, do_b[:, vc])
            da_intra = _dot_nn(dfwd, g.b_b[:, kc])
            db = _dot_tn(dfwd, g.a_b[:, kc])
            dcn = _dot_nn(dbwd, g.dp_b[:, kc])
            ddp = _dot_tn(dbwd, g.cn_b[:, kc])
            dstate = dstate_ref[srows, :]
            da_rows, dkd_rows, dv_rows, dcl_rows = [None] * cpt, [None] * cpt, [None] * cpt, [None] * cpt
            for j in reversed(range(cpt)):
                r = g.rows(j)
                state = st_ref[j, srows, :]
                dstate_b = dstate.astype(BF16)
                do_c = do_b[r, vc]
                dv_rows[j] = dv_intra[r] + _dot_nt(g.kd_b[r, kc], dstate_b)
                da_rows[j] = da_intra[r] + _dot_nn(do_c, state.astype(BF16))
                dkd = _dot_nn(g.v_b[r, vc], dstate_b) * g.e_rest[r, kc]
                dkd_rows[j] = dkd
                decay = jnp.exp(g.c_last[j][:, kc])
                dc_last = (jnp.sum(dkd * g.k[r, kc], axis=0, keepdims=True)
                           + decay * jnp.sum(state * dstate, axis=0, keepdims=True))
                dcl_rows[j] = jnp.where(last_row, dc_last, 0.0)
                dstate = _dot_tn(do_c, g.a_b[r, kc]) + dstate * decay
            dstate_ref[srows, :] = dstate
            da = jnp.concatenate(da_rows, axis=0)
            dkd = jnp.concatenate(dkd_rows, axis=0)
            dv_h.append(jnp.concatenate(dv_rows, axis=0))
            q_up, q_down = da * g.e_pos[:, kc], dcn * g.e_neg[:, kc]
            k_up, k_down = ddp * g.e_pos[:, kc], db * g.e_neg[:, kc] + dkd
            dq_h.append(Q_SCALE * (q_up + q_down))
            dk_h.append(k_up + k_down)
            dc_h.append(g.q[:, kc] * (q_up - q_down) + g.k[:, kc] * (k_up - k_down)
                        + jnp.concatenate(dcl_rows, axis=0))
        dq = jnp.concatenate(dq_h, axis=1)
        dk = jnp.concatenate(dk_h, axis=1)
        dv = jnp.concatenate(dv_h, axis=1)
        dlog_g = _chunk_scan(jnp.concatenate(dc_h, axis=1), True)
        dz = dlog_g * (1.0 / GATE_NORMALIZER) * (1.0 - _sigmoid(g.z))
        dzb = dz.astype(BF16)
        dgkb_ref[...] += jnp.sum(dz, axis=0, keepdims=True)
        dgkw_ref[...] += _dot_tn(g.low_b, dzb)
        dlow = _dot_nt(dzb, gkw_ref[...])
        dproj_ref[...] = jnp.concatenate([dq, dk, dv, dgate, dlow], axis=1).astype(BF16)

        @pl.when(i == nt - 1)
        def _():
            dwout_stage[...] = dwout_acc[...].astype(BF16)
            pltpu.sync_copy(dwout_stage, dwout_hbm)

    rev = lambda i: (nt - 1 - i, 0)
    return pl.pallas_call(
        body, name="gla_bwd", grid=(nt,),
        in_specs=[pl.BlockSpec((tm, D_MODEL), rev), pl.BlockSpec((tm, GLA_IN_PAD), rev),
                  pl.BlockSpec((tm, GLA_VALUE_WIDTH), rev),
                  pl.BlockSpec((cpt, GLA_VALUE_WIDTH, GLA_HEAD_K), lambda i: (nt - 1 - i, 0, 0)),
                  _const((GLA_LOW_PAD, GLA_KEY_WIDTH)), _const((1, GLA_KEY_WIDTH)), _const((1, GLA_VALUE_WIDTH)),
                  _const((GLA_VALUE_WIDTH, D_MODEL))],
        out_specs=[pl.BlockSpec((tm, GLA_IN_PAD), rev), pl.BlockSpec(memory_space=pl.ANY),
                   _full((1, GLA_VALUE_WIDTH)), _full((GLA_LOW_PAD, GLA_KEY_WIDTH)), _full((1, GLA_KEY_WIDTH))],
        out_shape=[jax.ShapeDtypeStruct((seq, GLA_IN_PAD), BF16), jax.ShapeDtypeStruct((GLA_VALUE_WIDTH, D_MODEL), BF16),
                   jax.ShapeDtypeStruct((1, GLA_VALUE_WIDTH), F32), jax.ShapeDtypeStruct((GLA_LOW_PAD, GLA_KEY_WIDTH), F32),
                   jax.ShapeDtypeStruct((1, GLA_KEY_WIDTH), F32)],
        scratch_shapes=[pltpu.VMEM((GLA_VALUE_WIDTH, GLA_HEAD_K), F32), pltpu.VMEM((GLA_VALUE_WIDTH, D_MODEL), F32),
                        pltpu.VMEM((GLA_VALUE_WIDTH, D_MODEL), BF16)],
        compiler_params=_params(),
    )(dh2, proj, o, states, gkw, gkb, hw, w_out)


def _position():
    return lax.axis_index("x"), lax.axis_index("y"), lax.axis_index("c")


def _lead_slot(ref, d):
    return ref.at[d]


def _row_slot(rows):
    return lambda ref, d: ref.at[pl.ds(pl.multiple_of(d * rows, rows), rows)]


def _dim1_slot(size):
    return lambda ref, d: ref.at[:, pl.ds(pl.multiple_of(d * size, size), size)]


class _Gather:
    def __init__(self, in_refs, out_refs, slots, send_sems, recv_sems, local_sems):
        self.in_refs, self.out_refs, self.slots = in_refs, out_refs, slots
        self.send_sems, self.recv_sems, self.local_sems = send_sems, recv_sems, local_sems
        self.n = len(in_refs)
        x, y, c = _position()
        self.c = c
        self.me, self.sibling = (x, y, c), (x, y, 1 - c)
        self.chips = [(1 - x, y), (x, 1 - y), (1 - x, 1 - y)]

    def _copy(self, a, k, block, to, from_input=False):
        part = self.slots[a](self.out_refs[a], 4 * block[0] + 2 * block[1] + block[2])
        return pltpu.make_async_remote_copy(
            src_ref=self.in_refs[a] if from_input else part, dst_ref=part,
            send_sem=self.send_sems.at[a, k], recv_sem=self.recv_sems.at[a, k], device_id=to, device_id_type=MESH)

    def _mine(self):
        return [pltpu.make_async_copy(self.in_refs[a], self.slots[a](self.out_refs[a], 4 * self.me[0] + 2 * self.me[1]
                                                                    + self.me[2]), self.local_sems.at[a])
                for a in range(self.n)]

    def _first(self):
        first = [self._copy(a, 0, self.me, self.sibling, True) for a in range(self.n)]
        return first + [self._copy(a, 1 + j, self.me, (*chip, self.c), True)
                        for j, chip in enumerate(self.chips) for a in range(self.n)]

    def _passed(self):
        return [self._copy(a, 4 + j, (*chip, self.c), self.sibling)
                for j, chip in enumerate(self.chips) for a in range(self.n)]

    def start(self):
        for cp in self._mine() + self._first():
            cp.start()

    def forward(self):
        passed = self._passed()
        for j, chip in enumerate(self.chips):
            for a in range(self.n):
                self._copy(a, 1 + j, (*chip, self.c), self.me).wait_recv()
                passed[j * self.n + a].start()

    def finish(self):
        for a in range(self.n):
            self._copy(a, 0, self.sibling, self.me).wait_recv()
        for j, chip in enumerate(self.chips):
            for a in range(self.n):
                self._copy(a, 4 + j, (*chip, 1 - self.c), self.me).wait_recv()
        for cp in self._first() + self._passed():
            cp.wait_send()
        for cp in self._mine():
            cp.wait()


class _Exchange:
    def __init__(self, in_refs, out_refs, slots, send_sems, recv_sems, local_sems):
        self.in_refs, self.out_refs, self.slots = in_refs, out_refs, slots
        self.send_sems, self.recv_sems, self.local_sems = send_sems, recv_sems, local_sems
        self.n = len(in_refs)
        self.pos = _position()

    def _copies(self):
        x, y, c = self.pos
        me = 4 * x + 2 * y + c
        mine = [pltpu.make_async_copy(self.slots[a](self.in_refs[a], me), self.out_refs[a].at[me],
                                      self.local_sems.at[a]) for a in range(self.n)]
        remote = []
        for k in range(1, N_DEV):
            px, py, pc = x ^ (k >> 2), y ^ ((k >> 1) & 1), c ^ (k & 1)
            for a in range(self.n):
                remote.append(pltpu.make_async_remote_copy(
                    src_ref=self.slots[a](self.in_refs[a], 4 * px + 2 * py + pc), dst_ref=self.out_refs[a].at[me],
                    send_sem=self.send_sems.at[a, k - 1], recv_sem=self.recv_sems.at[a, k - 1],
                    device_id=(px, py, pc), device_id_type=MESH))
        return mine, remote

    def start(self):
        mine, remote = self._copies()
        for cp in mine + remote:
            cp.start()

    def forward(self):
        pass

    def finish(self):
        mine, remote = self._copies()
        for cp in remote:
            cp.wait_recv()
        for cp in remote:
            cp.wait_send()
        for cp in mine:
            cp.wait()


class _Rider:
    def __init__(self, kind, arrays, out_shapes, slots):
        self.kind, self.arrays, self.slots = kind, list(arrays), slots
        self.n = len(self.arrays)
        hbm = pl.BlockSpec(memory_space=pl.ANY)
        self.in_specs = [hbm] * self.n
        self.out_specs = [hbm] * self.n
        self.out_shape = [jax.ShapeDtypeStruct(tuple(s), a.dtype) for s, a in zip(out_shapes, self.arrays)]
        self.scratch = [pltpu.SemaphoreType.DMA((self.n, 7)), pltpu.SemaphoreType.DMA((self.n, 7)),
                        pltpu.SemaphoreType.DMA((self.n,))]

    def bind(self, in_refs, out_refs, sems):
        return self.kind(in_refs, out_refs, self.slots, *sems)


def _gather_rider(shards, full_shapes, slots):
    return _Rider(_Gather, shards, full_shapes, slots)


def _exchange_rider(sends, part_shapes, slots):
    return _Rider(_Exchange, sends, [(N_DEV,) + tuple(s) for s in part_shapes], slots)


def _split_refs(refs, n_in, n_out, n_scratch, rider):
    k = rider.n if rider is not None else 0
    ins, r_ins = refs[:n_in], refs[n_in:n_in + k]
    outs, r_outs = refs[n_in + k:n_in + k + n_out], refs[n_in + k + n_out:n_in + 2 * k + n_out]
    rest = refs[n_in + 2 * k + n_out:]
    scratch, sems = rest[:n_scratch], rest[n_scratch:]
    comm = rider.bind(r_ins, r_outs, sems) if rider is not None else None
    return ins + outs + scratch, comm


def _ride_before(comm, i, nt):
    if comm is not None:
        pl.when(i == 0)(comm.start)
        pl.when(i == nt - 1)(comm.forward)


def _ride_after(comm, i, nt):
    if comm is not None:
        pl.when(i == nt - 1)(comm.finish)


def _extend(specs, rider, field):
    return list(specs) + (getattr(rider, field) if rider is not None else [])


def _comm_call(name, rider):
    def body(*refs):
        _, comm = _split_refs(refs, 0, 0, 0, rider)
        comm.start()
        comm.forward()
        comm.finish()

    return pl.pallas_call(body, name=name, in_specs=rider.in_specs, out_specs=rider.out_specs,
                          out_shape=rider.out_shape, scratch_shapes=rider.scratch)(*rider.arrays)


N_CHIPS = 4


def _reduce_scatter_call(grad, small_sends, small_part_shapes, small_slots):
    rows, cols = grad.shape[0], grad.shape[1] // N_DEV
    n_small = len(small_sends)

    def body(*refs):
        g_ref, small_in = refs[0], refs[1:1 + n_small]
        chip_ref, small_out = refs[1 + n_small], refs[2 + n_small:2 + 2 * n_small]
        (own_buf, recv_buf, part_buf, small_send, small_recv, small_local,
         swap_send, swap_recv, local_sems, chip_send, chip_recv) = refs[2 + 2 * n_small:]
        x, y, c = _position()
        my_chip = 2 * x + y
        small = _Exchange(small_in, small_out, small_slots, small_send, small_recv, small_local)
        small.start()

        def block(chip, core):
            return g_ref.at[:, pl.ds(pl.multiple_of((2 * chip + core) * cols, cols), cols)]

        swap = [pltpu.make_async_remote_copy(
            src_ref=block(q, 1 - c), dst_ref=recv_buf.at[q], send_sem=swap_send.at[q], recv_sem=swap_recv.at[q],
            device_id=(x, y, 1 - c), device_id_type=MESH) for q in range(N_CHIPS)]
        mine = [pltpu.make_async_copy(block(q, c), own_buf.at[q], local_sems.at[q]) for q in range(N_CHIPS)]
        for cp in swap + mine:
            cp.start()
        for q in range(N_CHIPS):
            mine[q].wait()
            swap[q].wait_recv()
            part_buf[q] = (own_buf[q].astype(F32) + recv_buf[q].astype(F32)).astype(BF16)
        to_chips = []
        for k in range(1, N_CHIPS):
            px, py = x ^ (k >> 1), y ^ (k & 1)
            to_chips.append(pltpu.make_async_remote_copy(
                src_ref=part_buf.at[2 * px + py], dst_ref=chip_ref.at[my_chip],
                send_sem=chip_send.at[k - 1], recv_sem=chip_recv.at[k - 1],
                device_id=(px, py, c), device_id_type=MESH))
        own = pltpu.make_async_copy(part_buf.at[my_chip], chip_ref.at[my_chip], local_sems.at[N_CHIPS])
        for cp in to_chips + [own]:
            cp.start()
        for cp in to_chips:
            cp.wait_recv()
        for cp in to_chips + swap:
            cp.wait_send()
        own.wait()
        small.finish()

    hbm = pl.BlockSpec(memory_space=pl.ANY)
    half = pltpu.VMEM((N_CHIPS, rows, cols), grad.dtype)
    return pl.pallas_call(
        body, name="grads_reduce_scatter", in_specs=[hbm] * (1 + n_small), out_specs=[hbm] * (1 + n_small),
        out_shape=[jax.ShapeDtypeStruct((N_CHIPS, rows, cols), grad.dtype)]
        + [jax.ShapeDtypeStruct((N_DEV,) + tuple(s), a.dtype) for s, a in zip(small_part_shapes, small_sends)],
        scratch_shapes=[half, half, half,
                        pltpu.SemaphoreType.DMA((n_small, 7)), pltpu.SemaphoreType.DMA((n_small, 7)),
                        pltpu.SemaphoreType.DMA((n_small,)),
                        pltpu.SemaphoreType.DMA((N_CHIPS,)), pltpu.SemaphoreType.DMA((N_CHIPS,)),
                        pltpu.SemaphoreType.DMA((N_CHIPS + 1,)),
                        pltpu.SemaphoreType.DMA((N_CHIPS - 1,)), pltpu.SemaphoreType.DMA((N_CHIPS - 1,))],
        compiler_params=pltpu.CompilerParams(vmem_limit_bytes=VMEM_LIMIT),
    )(grad, *small_sends)


def _adamw(w, g, m, v):
    m = ADAM_B1 * m + (1.0 - ADAM_B1) * g
    v = ADAM_B2 * v + (1.0 - ADAM_B2) * (g * g)
    m_hat = m / (1.0 - ADAM_B1 ** ADAM_STEP)
    v_hat = v / (1.0 - ADAM_B2 ** ADAM_STEP)
    delta = -ADAM_LR * (m_hat / (jnp.sqrt(v_hat) + ADAM_EPS) + ADAM_WD * w)
    return delta, m, v


def _sum_parts(parts_ref, index=()):
    g = parts_ref[(0,) + index].astype(F32)
    for s in range(1, parts_ref.shape[0]):
        g = g + parts_ref[(s,) + index].astype(F32)
    return g


def _adamw_call(name, parts, w, m, v, block_rows):
    rows, cols = w.shape
    nb = rows // block_rows
    senders = parts.shape[0]

    def body(parts_ref, w_ref, m_ref, v_ref, g_ref, delta_ref, m_out, v_out):
        g = _sum_parts(parts_ref)
        delta, m_new, v_new = _adamw(w_ref[...], g, m_ref[...], v_ref[...])
        g_ref[...] = g
        delta_ref[...] = delta
        m_out[...] = m_new
        v_out[...] = v_new

    blk = pl.BlockSpec((block_rows, cols), lambda i: (i, 0))
    return pl.pallas_call(
        body, name=name, grid=(nb,),
        in_specs=[pl.BlockSpec((senders, block_rows, cols), lambda i: (0, i, 0)), blk, blk, blk],
        out_specs=[blk, blk, blk, blk],
        out_shape=[jax.ShapeDtypeStruct((rows, cols), F32)] * 4,
        compiler_params=_params(("parallel",)),
    )(parts, w, m, v)


def _adamw_slabs_call(name, parts_a, parts_b, w, m, v):
    slabs_a = parts_a.shape[1]

    def body(pa_ref, pb_ref, w_ref, m_ref, v_ref, g_ref, delta_ref, m_out, v_out):
        for parts_ref, part in ((pa_ref, slice(0, slabs_a)), (pb_ref, slice(slabs_a, None))):
            g = _sum_parts(parts_ref)
            delta, m_new, v_new = _adamw(w_ref[part], g, m_ref[part], v_ref[part])
            g_ref[part] = g
            delta_ref[part] = delta
            m_out[part] = m_new
            v_out[part] = v_new

    vmem = pl.BlockSpec(memory_space=pltpu.VMEM)
    return pl.pallas_call(
        body, name=name, in_specs=[vmem] * 5, out_specs=[vmem] * 4,
        out_shape=[jax.ShapeDtypeStruct(w.shape, F32)] * 4,
        compiler_params=pltpu.CompilerParams(vmem_limit_bytes=VMEM_LIMIT),
    )(parts_a, parts_b, w, m, v)


WIDE_ROWS = 8
NARROW_ROWS = 40
NARROW_GKW_ROW = 8
NARROW_GKB_ROW = 24
NARROW_HW_ROW = 32
GROUP_SHARD = POOL_GROUP_DIM // N_DEV
KEY_SHARD = GLA_KEY_WIDTH // N_DEV
HEAD_V_SHARD = GLA_HEAD_V // N_DEV


def _small_adamw_call(wide, narrow, w, m, v):
    names = ("norm_w", "pool_scale", "final_norm_w", "pool_group_b", "gla_gk_w", "gla_gk_b", "gla_head_norm_w")
    where = {
        "norm_w": (0, slice(0, 2), slice(None)),
        "pool_scale": (0, slice(2, 3), slice(None)),
        "final_norm_w": (0, slice(3, 4), slice(None)),
        "pool_group_b": (1, slice(0, POOL_GROUPS), slice(0, GROUP_SHARD)),
        "gla_gk_w": (1, slice(NARROW_GKW_ROW, NARROW_GKW_ROW + GLA_GATE_RANK), slice(0, KEY_SHARD)),
        "gla_gk_b": (1, slice(NARROW_GKB_ROW, NARROW_GKB_ROW + 1), slice(0, KEY_SHARD)),
        "gla_head_norm_w": (1, slice(NARROW_HW_ROW, NARROW_HW_ROW + 1), slice(0, HEAD_V_SHARD)),
    }
    k = len(names)

    def body(*refs):
        parts = refs[0:2]
        w_refs, m_refs, v_refs = refs[2:2 + k], refs[2 + k:2 + 2 * k], refs[2 + 2 * k:2 + 3 * k]
        outs = refs[2 + 3 * k:]
        loss_ref = outs[0]
        loss_ref[...] = _sum_parts(parts[0], (slice(4, 5), slice(0, 1)))
        for i, name in enumerate(names):
            buf, rows, cols = where[name]
            g = _sum_parts(parts[buf], (rows, cols))
            delta, m_new, v_new = _adamw(w_refs[i][...], g, m_refs[i][...], v_refs[i][...])
            outs[1 + i][...] = g
            outs[1 + k + i][...] = delta
            outs[1 + 2 * k + i][...] = m_new
            outs[1 + 3 * k + i][...] = v_new

    vmem = pl.BlockSpec(memory_space=pltpu.VMEM)
    shapes = [jax.ShapeDtypeStruct(w[n].shape, F32) for n in names]
    res = pl.pallas_call(
        body, name="adamw_small", in_specs=[vmem] * (2 + 3 * k), out_specs=[vmem] * (1 + 4 * k),
        out_shape=[jax.ShapeDtypeStruct((1, 1), F32)] + shapes * 4,
    )(wide, narrow, *[w[n] for n in names], *[m[n] for n in names], *[v[n] for n in names])
    unzip = lambda j: dict(zip(names, res[1 + j * k:1 + (j + 1) * k]))
    return res[0], unzip(0), unzip(1), unzip(2), unzip(3)


def kernel(x, norm_w, pool_in_w, pool_group_w, pool_group_b, pool_scale, pool_out_w, gla_in_w, gla_gk_w, gla_gk_b, gla_head_norm_w, gla_out_w, final_norm_w, loss_target, m_norm_w, m_pool_in_w, m_pool_group_w, m_pool_group_b, m_pool_scale, m_pool_out_w, m_gla_in_w, m_gla_gk_w, m_gla_gk_b, m_gla_head_norm_w, m_gla_out_w, m_final_norm_w, v_norm_w, v_pool_in_w, v_pool_group_w, v_pool_group_b, v_pool_scale, v_pool_out_w, v_gla_in_w, v_gla_gk_w, v_gla_gk_b, v_gla_head_norm_w, v_gla_out_w, v_final_norm_w):
    w = dict(norm_w=norm_w, pool_in_w=pool_in_w, pool_group_w=pool_group_w, pool_group_b=pool_group_b,
             pool_scale=pool_scale, pool_out_w=pool_out_w, gla_in_w=gla_in_w, gla_gk_w=gla_gk_w, gla_gk_b=gla_gk_b,
             gla_head_norm_w=gla_head_norm_w, gla_out_w=gla_out_w, final_norm_w=final_norm_w)
    m = dict(norm_w=m_norm_w, pool_in_w=m_pool_in_w, pool_group_w=m_pool_group_w, pool_group_b=m_pool_group_b,
             pool_scale=m_pool_scale, pool_out_w=m_pool_out_w, gla_in_w=m_gla_in_w, gla_gk_w=m_gla_gk_w,
             gla_gk_b=m_gla_gk_b, gla_head_norm_w=m_gla_head_norm_w, gla_out_w=m_gla_out_w,
             final_norm_w=m_final_norm_w)
    v = dict(norm_w=v_norm_w, pool_in_w=v_pool_in_w, pool_group_w=v_pool_group_w, pool_group_b=v_pool_group_b,
             pool_scale=v_pool_scale, pool_out_w=v_pool_out_w, gla_in_w=v_gla_in_w, gla_gk_w=v_gla_gk_w,
             gla_gk_b=v_gla_gk_b, gla_head_norm_w=v_gla_head_norm_w, gla_out_w=v_gla_out_w,
             final_norm_w=v_final_norm_w)
    col_shard = GLA_IN_WIDTH // N_DEV
    row_shard = D_MODEL // N_DEV

    def lanes(a):
        return jnp.pad(a, [(0, 0)] * (a.ndim - 1) + [(0, LANES - a.shape[-1])])

    small_in = jnp.concatenate([lanes(pool_group_b[0]), lanes(gla_gk_b), lanes(gla_head_norm_w),
                                jnp.zeros((2, LANES), F32)], axis=0)
    in_cols = 2 * POOL_WIDTH // N_DEV
    pool_in, pool_gw, pool_out, small_all = _comm_call("pool_weights_all_gather", _gather_rider(
        [pool_in_w[0].astype(BF16), pool_group_w[0].astype(BF16), pool_out_w[0].astype(BF16), small_in],
        [(D_MODEL, 2 * POOL_WIDTH), (POOL_GROUPS, POOL_GROUP_DIM, POOL_GROUP_DIM), (POOL_WIDTH, D_MODEL),
         (N_DEV, 8, LANES)],
        [_dim1_slot(in_cols), _dim1_slot(GROUP_SHARD), _row_slot(row_shard), _lead_slot]))
    pool_gb = jnp.transpose(small_all[:, 0:POOL_GROUPS, :GROUP_SHARD], (1, 0, 2)).reshape(1, POOL_WIDTH)
    gla_gkb = small_all[:, POOL_GROUPS, :KEY_SHARD].reshape(1, GLA_KEY_WIDTH)
    gla_hw = jnp.tile(small_all[:, POOL_GROUPS + 1, :HEAD_V_SHARD].reshape(1, GLA_HEAD_V), (1, GLA_HEADS))
    nw0, nw1, wf = norm_w[0:1], norm_w[1:2], final_norm_w.reshape(1, D_MODEL)
    xs, target = x[0], loss_target[0]

    h1, p, gla_in_parts, gkw_parts, gla_out = _pool_fwd_call(
        xs, nw0, pool_in, pool_gw, pool_gb, pool_scale, pool_out, _gather_rider(
            [jnp.transpose(gla_in_w[0]).astype(BF16), gla_gk_w[0].astype(BF16), gla_out_w[0].astype(BF16)],
            [(N_DEV, col_shard, D_MODEL), (N_DEV, GLA_GATE_RANK, KEY_SHARD), (GLA_VALUE_WIDTH, D_MODEL)],
            [_lead_slot, _lead_slot, _row_slot(row_shard)]))
    gla_in = jnp.pad(gla_in_parts.reshape(GLA_IN_WIDTH, D_MODEL), ((0, GLA_IN_PAD - GLA_IN_WIDTH), (0, 0)))
    gla_gkw = jnp.pad(jnp.transpose(gkw_parts, (1, 0, 2)).reshape(GLA_GATE_RANK, GLA_KEY_WIDTH),
                      ((0, GLA_LOW_PAD - GLA_GATE_RANK), (0, 0)))
    dh2, proj, o, states, loss_part, dwf = _gla_fwd_call(h1, nw1, gla_in, gla_gkw, gla_gkb, gla_hw, gla_out, wf, target)

    dproj, d_gla_out, dhw, dgkw, dgkb = _gla_bwd_call(dh2, proj, o, states, gla_gkw, gla_gkb, gla_hw, gla_out)
    dh1, d_gla_in, dnw1, landed_gla_out = _inproj_bwd_call(
        "gla_in_bwd", dproj, h1, nw1, gla_in, dh2,
        _exchange_rider([d_gla_out], [(row_shard, D_MODEL)], [_row_slot(row_shard)]), transposed=True)
    slabs = col_shard * D_MODEL // (BF16_ROWS * LANES)
    slabs_a = GLA_IN_SLABS_WITH_POOL_BWD
    gla_in_send = jnp.stack([d_gla_in[d * col_shard:(d + 1) * col_shard] for d in range(N_DEV)]).reshape(
        N_DEV, slabs, BF16_ROWS, LANES)
    dp, d_pool_out, dgw, dgb, dsc, landed_gla_in_a = _pool_bwd_call(
        dh1, p, pool_gw, pool_gb, pool_scale, pool_out,
        _exchange_rider([gla_in_send], [(slabs_a, BF16_ROWS, LANES)], [lambda ref, d: ref.at[d, pl.ds(0, slabs_a)]]))
    grad_x, d_pool_in, dnw0, landed_gla_in_b, landed_pool_out, landed_gw = _inproj_bwd_call(
        "pool_in_bwd", dp, xs, nw0, pool_in, dh1,
        _exchange_rider([gla_in_send, d_pool_out, dgw],
                        [(slabs - slabs_a, BF16_ROWS, LANES), (row_shard, D_MODEL),
                         (POOL_GROUPS, GROUP_SHARD, POOL_GROUP_DIM)],
                        [lambda ref, d: ref.at[d, pl.ds(slabs_a, slabs - slabs_a)], _row_slot(row_shard),
                         _dim1_slot(GROUP_SHARD)]))

    wide = jnp.concatenate([
        dnw0, dnw1, dsc, dwf, jnp.pad(loss_part[0:1, 0:1], ((0, 0), (0, D_MODEL - 1))),
        jnp.zeros((WIDE_ROWS - 5, D_MODEL), F32)], axis=0)

    def rows8(a):
        return jnp.pad(lanes(a), ((0, 0), (0, -a.shape[1] % 8), (0, 0)))

    narrow = jnp.concatenate([
        rows8(jnp.transpose(dgb.reshape(POOL_GROUPS, N_DEV, GROUP_SHARD), (1, 0, 2))),
        rows8(jnp.transpose(dgkw[:GLA_GATE_RANK].reshape(GLA_GATE_RANK, N_DEV, KEY_SHARD), (1, 0, 2))),
        rows8(dgkb.reshape(N_DEV, 1, KEY_SHARD)),
        rows8(dhw.reshape(GLA_HEADS, GLA_HEAD_V).sum(axis=0).reshape(N_DEV, 1, HEAD_V_SHARD)),
    ], axis=1)
    landed_pool_in, landed_wide, landed_narrow = _reduce_scatter_call(
        d_pool_in, [wide, narrow], [(WIDE_ROWS, D_MODEL), (NARROW_ROWS, LANES)], [lambda ref, d: ref, _lead_slot])

    res = {}
    for name, parts, rows, cols, block in [
            ("pool_in_w", landed_pool_in, D_MODEL, in_cols, 256),
            ("pool_group_w", landed_gw, POOL_GROUPS * GROUP_SHARD, POOL_GROUP_DIM, 128),
            ("pool_out_w", landed_pool_out, row_shard, D_MODEL, 128),
            ("gla_out_w", landed_gla_out, row_shard, D_MODEL, 128)]:
        outs = _adamw_call("adamw_" + name, parts.reshape(parts.shape[0], rows, cols), w[name].reshape(rows, cols),
                           m[name].reshape(rows, cols), v[name].reshape(rows, cols), block)
        res[name] = [t.reshape(w[name].shape) for t in outs]
    as_slabs = lambda t: jnp.transpose(t[0]).reshape(slabs, BF16_ROWS, LANES)
    outs = _adamw_slabs_call("adamw_gla_in_w", landed_gla_in_a, landed_gla_in_b, as_slabs(gla_in_w),
                             as_slabs(m_gla_in_w), as_slabs(v_gla_in_w))
    res["gla_in_w"] = [jnp.transpose(t.reshape(col_shard, D_MODEL))[None] for t in outs]
    small_shapes = {"norm_w": (2, D_MODEL), "pool_scale": (1, D_MODEL), "final_norm_w": (1, D_MODEL),
                    "pool_group_b": (POOL_GROUPS, GROUP_SHARD), "gla_gk_w": (GLA_GATE_RANK, KEY_SHARD),
                    "gla_gk_b": (1, KEY_SHARD), "gla_head_norm_w": (1, HEAD_V_SHARD)}
    as_small = lambda t: {n: t[n].reshape(s) for n, s in small_shapes.items()}
    loss, *small_outs = _small_adamw_call(landed_wide, landed_narrow, as_small(w), as_small(m), as_small(v))
    for name in small_shapes:
        res[name] = [t[name].reshape(w[name].shape) for t in small_outs]
    order = ("norm_w", "pool_in_w", "pool_group_w", "pool_group_b", "pool_scale", "pool_out_w", "gla_in_w",
             "gla_gk_w", "gla_gk_b", "gla_head_norm_w", "gla_out_w", "final_norm_w")
    return (loss.reshape(()), grad_x[None], *[res[n][0] for n in order], *[res[n][1] for n in order],
            *[res[n][2] for n in order], *[res[n][3] for n in order])
```

```python
import jax
import jax.numpy as jnp
from jax import lax
from jax.experimental import pallas as pl
from jax.experimental.pallas import tpu as pltpu

F32 = jnp.float32
BF16 = jnp.bfloat16
MESH = pl.DeviceIdType.MESH

N_DEV = 8
D_MODEL = 1024
POOL_WIDTH = 1024
POOL_GROUPS = 4
POOL_GROUP_DIM = 256
POOL_HALO = 16
GLA_HEADS = 4
GLA_HEAD_K = 128
GLA_HEAD_V = 256
GLA_KEY_WIDTH = 512
GLA_VALUE_WIDTH = 1024
GLA_GATE_RANK = 16
GLA_IN_WIDTH = 3088
GLA_IN_PAD = 3200
GLA_LOW_PAD = 128
GLA_QKVG_WIDTH = 3072
CHUNK = 64
GATE_NORMALIZER = 16.0
RMS_EPS = 1e-6
Q_SCALE = GLA_HEAD_K ** -0.5

ADAM_LR = 0.001
ADAM_B1 = 0.9
ADAM_B2 = 0.999
ADAM_EPS = 1e-08
ADAM_WD = 0.01
ADAM_STEP = 10

LANES = 128
BF16_ROWS = 16
VMEM_LIMIT = 56 * 1024 * 1024
ROW_TILE = 256
MATMUL_ROW_TILE = 512
GLA_IN_SLABS_WITH_POOL_BWD = 145


def _dot_nn(a, b):
    return lax.dot_general(a, b, (((1,), (0,)), ((), ())), preferred_element_type=F32)


def _dot_nt(a, b):
    return lax.dot_general(a, b, (((1,), (1,)), ((), ())), preferred_element_type=F32)


def _dot_tn(a, b):
    return lax.dot_general(a, b, (((0,), (0,)), ((), ())), preferred_element_type=F32)


def _rms(x):
    rstd = lax.rsqrt(jnp.mean(x * x, axis=-1, keepdims=True) + RMS_EPS)
    return x * rstd, rstd


def _rms_bwd(dxhat, xhat, rstd):
    return rstd * (dxhat - xhat * jnp.mean(dxhat * xhat, axis=-1, keepdims=True))


def _sigmoid(x):
    return 1.0 / (1.0 + jnp.exp(-x))


def _params(sem=("arbitrary",)):
    return pltpu.CompilerParams(dimension_semantics=sem, vmem_limit_bytes=VMEM_LIMIT)


def _full(shape):
    return pl.BlockSpec(shape, lambda i: (0,) * len(shape))


def _const(shape):
    return pl.BlockSpec(shape, lambda i: (0,) * len(shape), pipeline_mode=pl.Buffered(1))


def _window_sums(ext, forward):
    n = ext.shape[0]
    outs = []
    for g in range(POOL_GROUPS):
        s = ext[:, g * POOL_GROUP_DIM:(g + 1) * POOL_GROUP_DIM]
        for k in range(g + 1):
            shift = (1 << k) if forward else n - (1 << k)
            s = s + pltpu.roll(s, shift, axis=0)
        outs.append(s[:n - POOL_HALO])
    return outs


def _inv_count(row0, tm):
    row = row0 + lax.broadcasted_iota(jnp.int32, (tm, 1), 0)
    return [1.0 / jnp.minimum(row + 1, 2 << g).astype(F32) for g in range(POOL_GROUPS)]


def _pool_mix(u, u_prev, row0, gw_ref, gb):
    tm = u.shape[0]
    sums = _window_sums(jnp.concatenate([u, u_prev], axis=0), True)
    inv = _inv_count(row0, tm)
    pooled, mixed = [], []
    for g in range(POOL_GROUPS):
        ug = u[:, g * POOL_GROUP_DIM:(g + 1) * POOL_GROUP_DIM]
        pg = (sums[g] * inv[g] - ug).astype(BF16)
        pooled.append(pg)
        mixed.append(_dot_nn(pg, gw_ref[g]))
    return pooled, jnp.concatenate(mixed, axis=1) + gb


def _pool_fwd_call(x, nw, w_in, gw, gb, sc, w_out, rider=None):
    seq = x.shape[0]
    tm = min(MATMUL_ROW_TILE, seq)
    nt = seq // tm

    def main(x_ref, nw_ref, win_ref, gw_ref, gb_ref, sc_ref, wout_ref, h_ref, p_ref, halo_ref):
        i = pl.program_id(0)

        @pl.when(i == 0)
        def _():
            halo_ref[...] = jnp.zeros_like(halo_ref)

        xt = x_ref[...]
        xhat, _ = _rms(xt)
        n = (xhat * nw_ref[...]).astype(BF16)
        p = _dot_nn(n, win_ref[...])
        p_ref[...] = p
        u = p[:, :POOL_WIDTH]
        gate = p[:, POOL_WIDTH:]
        _, mixed = _pool_mix(u, halo_ref[...], i * tm, gw_ref, gb_ref[...])
        halo_ref[...] = u[tm - POOL_HALO:, :]
        y = (mixed * sc_ref[...] * (gate * _sigmoid(gate))).astype(BF16)
        h_ref[...] = xt + _dot_nn(y, wout_ref[...])

    def body(*refs):
        own, comm = _split_refs(refs, 7, 2, 1, rider)
        _ride_before(comm, pl.program_id(0), nt)
        main(*own)
        _ride_after(comm, pl.program_id(0), nt)

    return pl.pallas_call(
        body, name="pool_fwd", grid=(nt,),
        in_specs=_extend([pl.BlockSpec((tm, D_MODEL), lambda i: (i, 0)), _const((1, D_MODEL)),
                          _const((D_MODEL, 2 * POOL_WIDTH)), _const((POOL_GROUPS, POOL_GROUP_DIM, POOL_GROUP_DIM)),
                          _const((1, POOL_WIDTH)), _const((1, POOL_WIDTH)), _const((POOL_WIDTH, D_MODEL))],
                         rider, "in_specs"),
        out_specs=_extend([pl.BlockSpec((tm, D_MODEL), lambda i: (i, 0)),
                           pl.BlockSpec((tm, 2 * POOL_WIDTH), lambda i: (i, 0))], rider, "out_specs"),
        out_shape=_extend([jax.ShapeDtypeStruct((seq, D_MODEL), F32),
                           jax.ShapeDtypeStruct((seq, 2 * POOL_WIDTH), F32)], rider, "out_shape"),
        scratch_shapes=_extend([pltpu.VMEM((POOL_HALO, POOL_WIDTH), F32)], rider, "scratch"),
        compiler_params=_params(),
    )(x, nw, w_in, gw, gb, sc, w_out, *_extend([], rider, "arrays"))


def _pool_bwd_call(dh, p, gw, gb, sc, w_out, rider=None):
    seq = dh.shape[0]
    tm = min(MATMUL_ROW_TILE, seq)
    nt = seq // tm
    halo_blocks = tm // POOL_HALO

    def main(dh_ref, p_ref, pprev_ref, gw_ref, gb_ref, sc_ref, wout_ref,
             dp_ref, dwout_hbm, dgw_hbm, dgb_ref, dsc_ref, carry_ref, dwout_acc, dgw_acc, dwout_stage, dgw_stage):
        i = pl.program_id(0)
        t = nt - 1 - i

        @pl.when(i == 0)
        def _():
            carry_ref[...] = jnp.zeros_like(carry_ref)
            dwout_acc[...] = jnp.zeros_like(dwout_acc)
            dgw_acc[...] = jnp.zeros_like(dgw_acc)
            dgb_ref[...] = jnp.zeros_like(dgb_ref)
            dsc_ref[...] = jnp.zeros_like(dsc_ref)

        p = p_ref[...]
        u = p[:, :POOL_WIDTH]
        gate = p[:, POOL_WIDTH:]
        u_prev = jnp.where(t > 0, pprev_ref[:, :POOL_WIDTH], 0.0)
        pooled, mixed = _pool_mix(u, u_prev, t * tm, gw_ref, gb_ref[...])
        sg = _sigmoid(gate)
        silu = gate * sg
        sc = sc_ref[...]
        dhb = dh_ref[...].astype(BF16)
        y = (mixed * sc * silu).astype(BF16)
        dwout_acc[...] += _dot_tn(y, dhb)
        dy = _dot_nt(dhb, wout_ref[...])
        dmixed = dy * sc * silu
        dsc_ref[...] += jnp.sum(dy * mixed * silu, axis=0, keepdims=True)
        dgate = dy * mixed * sc * (sg * (1.0 + gate * (1.0 - sg)))
        dgb_ref[...] += jnp.sum(dmixed, axis=0, keepdims=True)
        inv = _inv_count(t * tm, tm)
        dpooled, scaled = [], []
        for g in range(POOL_GROUPS):
            dmg = dmixed[:, g * POOL_GROUP_DIM:(g + 1) * POOL_GROUP_DIM].astype(BF16)
            dgw_acc[g] += _dot_tn(pooled[g], dmg)
            dpg = _dot_nt(dmg, gw_ref[g])
            dpooled.append(dpg)
            scaled.append(dpg * inv[g])
        r = jnp.concatenate(scaled, axis=1)
        sums = _window_sums(jnp.concatenate([r, carry_ref[...]], axis=0), False)
        carry_ref[...] = r[:POOL_HALO, :]
        du = jnp.concatenate([sums[g] - dpooled[g] for g in range(POOL_GROUPS)], axis=1)
        dp_ref[...] = jnp.concatenate([du, dgate], axis=1).astype(BF16)

        @pl.when(i == nt - 1)
        def _():
            dwout_stage[...] = dwout_acc[...].astype(BF16)
            dgw_stage[...] = dgw_acc[...].astype(BF16)
            pltpu.sync_copy(dwout_stage, dwout_hbm)
            pltpu.sync_copy(dgw_stage, dgw_hbm)

    def body(*refs):
        own, comm = _split_refs(refs, 7, 5, 5, rider)
        _ride_before(comm, pl.program_id(0), nt)
        main(*own)
        _ride_after(comm, pl.program_id(0), nt)

    rev = lambda i: (nt - 1 - i, 0)
    return pl.pallas_call(
        body, name="pool_bwd", grid=(nt,),
        in_specs=_extend([pl.BlockSpec((tm, D_MODEL), rev), pl.BlockSpec((tm, 2 * POOL_WIDTH), rev),
                          pl.BlockSpec((POOL_HALO, 2 * POOL_WIDTH),
                                       lambda i: (jnp.maximum((nt - 1 - i) * halo_blocks - 1, 0), 0)),
                          _const((POOL_GROUPS, POOL_GROUP_DIM, POOL_GROUP_DIM)), _const((1, POOL_WIDTH)),
                          _const((1, POOL_WIDTH)), _const((POOL_WIDTH, D_MODEL))], rider, "in_specs"),
        out_specs=_extend([pl.BlockSpec((tm, 2 * POOL_WIDTH), rev), pl.BlockSpec(memory_space=pl.ANY),
                           pl.BlockSpec(memory_space=pl.ANY), _full((1, POOL_WIDTH)), _full((1, POOL_WIDTH))],
                          rider, "out_specs"),
        out_shape=_extend([jax.ShapeDtypeStruct((seq, 2 * POOL_WIDTH), BF16),
                           jax.ShapeDtypeStruct((POOL_WIDTH, D_MODEL), BF16),
                           jax.ShapeDtypeStruct((POOL_GROUPS, POOL_GROUP_DIM, POOL_GROUP_DIM), BF16),
                           jax.ShapeDtypeStruct((1, POOL_WIDTH), F32), jax.ShapeDtypeStruct((1, POOL_WIDTH), F32)],
                          rider, "out_shape"),
        scratch_shapes=_extend([pltpu.VMEM((POOL_HALO, POOL_WIDTH), F32), pltpu.VMEM((POOL_WIDTH, D_MODEL), F32),
                                pltpu.VMEM((POOL_GROUPS, POOL_GROUP_DIM, POOL_GROUP_DIM), F32),
                                pltpu.VMEM((POOL_WIDTH, D_MODEL), BF16),
                                pltpu.VMEM((POOL_GROUPS, POOL_GROUP_DIM, POOL_GROUP_DIM), BF16)], rider, "scratch"),
        compiler_params=_params(),
    )(dh, p, p, gw, gb, sc, w_out, *_extend([], rider, "arrays"))


def _rows_then_zeros(ref, lo, hi, rows):
    part = ref[lo:hi, :]
    return jnp.concatenate([part, jnp.zeros((rows - (hi - lo), part.shape[1]), part.dtype)], axis=0)


def _inproj_bwd_call(name, dproj, h_in, nw, w_in, dres, rider=None, transposed=False):
    seq = h_in.shape[0]
    width = dproj.shape[1]
    w_shape = tuple(w_in.shape)
    acc_shape = (width, D_MODEL) if transposed else w_shape
    whole = w_shape[0] // LANES * LANES
    tm = min(MATMUL_ROW_TILE, seq)
    nt = seq // tm

    def main(dproj_ref, h_ref, nw_ref, win_ref, dres_ref, dh_ref, dw_hbm, dnw_ref, dw_acc, dw_stage):
        i = pl.program_id(0)

        @pl.when(i == 0)
        def _():
            dw_acc[...] = jnp.zeros_like(dw_acc)
            dnw_ref[...] = jnp.zeros_like(dnw_ref)

        xhat, rstd = _rms(h_ref[...])
        nw_row = nw_ref[...]
        n = (xhat * nw_row).astype(BF16)
        dpb = dproj_ref[...]
        if transposed:
            dw_acc[...] += _dot_tn(dpb, n)
            dn = _dot_nn(dpb[:, :whole], win_ref[0:whole, :])
            if whole < w_shape[0]:
                dn = dn + _dot_nn(dpb[:, whole:], _rows_then_zeros(win_ref, whole, w_shape[0], width - whole))
        else:
            dw_acc[...] += _dot_tn(n, dpb)
            dn = _dot_nt(dpb, win_ref[...])
        dnw_ref[...] += jnp.sum(dn * xhat, axis=0, keepdims=True)
        dh_ref[...] = _rms_bwd(dn * nw_row, xhat, rstd) + dres_ref[...]

        @pl.when(i == nt - 1)
        def _():
            dw_stage[...] = dw_acc[...].astype(BF16)
            pltpu.sync_copy(dw_stage.at[pl.ds(0, w_shape[0])], dw_hbm)

    def body(*refs):
        own, comm = _split_refs(refs, 5, 3, 2, rider)
        _ride_before(comm, pl.program_id(0), nt)
        main(*own)
        _ride_after(comm, pl.program_id(0), nt)

    row = lambda i: (i, 0)
    return pl.pallas_call(
        body, name=name, grid=(nt,),
        in_specs=_extend([pl.BlockSpec((tm, width), row), pl.BlockSpec((tm, D_MODEL), row), _const((1, D_MODEL)),
                          _const(w_shape), pl.BlockSpec((tm, D_MODEL), row)], rider, "in_specs"),
        out_specs=_extend([pl.BlockSpec((tm, D_MODEL), row), pl.BlockSpec(memory_space=pl.ANY),
                           _full((1, D_MODEL))], rider, "out_specs"),
        out_shape=_extend([jax.ShapeDtypeStruct((seq, D_MODEL), F32), jax.ShapeDtypeStruct(w_shape, BF16),
                           jax.ShapeDtypeStruct((1, D_MODEL), F32)], rider, "out_shape"),
        scratch_shapes=_extend([pltpu.VMEM(acc_shape, F32), pltpu.VMEM(acc_shape, BF16)], rider, "scratch"),
        compiler_params=_params(),
    )(dproj, h_in, nw, w_in, dres, *_extend([], rider, "arrays"))


def _chunk_scan(x, reverse):
    n = x.shape[0]
    pos = lax.broadcasted_iota(jnp.int32, (n, 1), 0) & (CHUNK - 1)
    k = 1
    while k < CHUNK:
        if reverse:
            x = x + jnp.where(pos < CHUNK - k, pltpu.roll(x, n - k, axis=0), 0.0)
        else:
            x = x + jnp.where(pos >= k, pltpu.roll(x, k, axis=0), 0.0)
        k *= 2
    return x


class _GlaTile:
    def __init__(self, q, k, v, gate, low, gkw_ref, gkb):
        tm = q.shape[0]
        self.q = q * Q_SCALE
        self.k, self.v, self.gate = k, v, gate
        self.low_b = low.astype(BF16)
        self.z = _dot_nn(self.low_b, gkw_ref[...]) + gkb
        log_g = (jnp.minimum(self.z, 0.0) - jnp.log(1.0 + jnp.exp(-jnp.abs(self.z)))) / GATE_NORMALIZER
        self.c = _chunk_scan(log_g, False)
        is_last = lax.broadcasted_iota(jnp.int32, (CHUNK, 1), 0) == CHUNK - 1
        last = [jnp.sum(jnp.where(is_last, self.c[j * CHUNK:(j + 1) * CHUNK, :], 0.0), axis=0, keepdims=True)
                for j in range(tm // CHUNK)]
        self.c_last = last
        c_last_rows = jnp.concatenate([jnp.broadcast_to(r, (CHUNK, GLA_KEY_WIDTH)) for r in last], axis=0)
        self.e_pos = jnp.exp(self.c)
        self.e_neg = jnp.exp(-self.c)
        self.e_rest = jnp.exp(c_last_rows - self.c)
        self.a_b = (self.q * self.e_pos).astype(BF16)
        self.b_b = (self.k * self.e_neg).astype(BF16)
        self.cn_b = (self.q * self.e_neg).astype(BF16)
        self.dp_b = (self.k * self.e_pos).astype(BF16)
        self.kd_b = (self.k * self.e_rest).astype(BF16)
        self.v_b = self.v.astype(BF16)
        idx_t = lax.broadcasted_iota(jnp.int32, (tm, tm), 0)
        idx_s = lax.broadcasted_iota(jnp.int32, (tm, tm), 1)
        same_chunk = (idx_t ^ idx_s) < CHUNK
        self.lower = same_chunk & (idx_t >= idx_s)
        self.upper = same_chunk & (idx_t < idx_s)

    @staticmethod
    def rows(j):
        return slice(j * CHUNK, (j + 1) * CHUNK)

    @staticmethod
    def kcols(h):
        return slice(h * GLA_HEAD_K, (h + 1) * GLA_HEAD_K)

    @staticmethod
    def vcols(h):
        return slice(h * GLA_HEAD_V, (h + 1) * GLA_HEAD_V)

    def scores(self, h):
        kc = self.kcols(h)
        fwd = _dot_nt(self.a_b[:, kc], self.b_b[:, kc])
        bwd = _dot_nt(self.cn_b[:, kc], self.dp_b[:, kc])
        return jnp.where(self.lower, fwd, jnp.where(self.upper, bwd, 0.0)).astype(BF16)


def _gla_fwd_call(h1, nw, w_in, gkw, gkb, hw, w_out, wf, target):
    seq = h1.shape[0]
    tm = ROW_TILE
    nt = seq // tm
    cpt = tm // CHUNK
    n_chunks = seq // CHUNK

    def body(h_ref, nw_ref, win_ref, gkw_ref, gkb_ref, hw_ref, wout_ref, wf_ref, tgt_ref,
             dh2_ref, proj_ref, o_ref, st_ref, loss_ref, dwf_ref, state_ref):
        i = pl.program_id(0)

        @pl.when(i == 0)
        def _():
            state_ref[...] = jnp.zeros_like(state_ref)
            loss_ref[...] = jnp.zeros_like(loss_ref)
            dwf_ref[...] = jnp.zeros_like(dwf_ref)

        ht = h_ref[...]
        xhat, _ = _rms(ht)
        n = (xhat * nw_ref[...]).astype(BF16)
        sections = {}
        for name, lo, hi in (("low", GLA_QKVG_WIDTH, GLA_IN_PAD), ("qk", 0, 2 * GLA_KEY_WIDTH),
                             ("v", 2 * GLA_KEY_WIDTH, GLA_QKVG_WIDTH - GLA_VALUE_WIDTH),
                             ("gate", GLA_QKVG_WIDTH - GLA_VALUE_WIDTH, GLA_QKVG_WIDTH)):
            rows = (win_ref[lo:hi, :] if hi <= GLA_IN_WIDTH
                    else _rows_then_zeros(win_ref, lo, GLA_IN_WIDTH, hi - lo))
            sections[name] = _dot_nt(n, rows)
            proj_ref[:, lo:hi] = sections[name]
        g = _GlaTile(sections["qk"][:, :GLA_KEY_WIDTH], sections["qk"][:, GLA_KEY_WIDTH:], sections["v"],
                     sections["gate"], sections["low"], gkw_ref, gkb_ref[...])
        o_heads = []
        for h in range(GLA_HEADS):
            kc, vc = g.kcols(h), g.vcols(h)
            srows = slice(h * GLA_HEAD_V, (h + 1) * GLA_HEAD_V)
            o_intra = _dot_nn(g.scores(h), g.v_b[:, vc])
            state = state_ref[srows, :]
            o_rows = []
            for j in range(cpt):
                r = g.rows(j)
                st_ref[j, srows, :] = state
                o_rows.append(o_intra[r] + _dot_nt(g.a_b[r, kc], state.astype(BF16)))
                decay = jnp.exp(g.c_last[j][:, kc])
                state = state * decay + _dot_tn(g.v_b[r, vc], g.kd_b[r, kc])
            state_ref[srows, :] = state
            o_heads.append(jnp.concatenate(o_rows, axis=0))
        o = jnp.concatenate(o_heads, axis=1)
        o_ref[...] = o
        hw_row = hw_ref[...]
        on = jnp.concatenate([_rms(o[:, g.vcols(h)])[0] for h in range(GLA_HEADS)], axis=1) * hw_row
        y = (on * (g.gate * _sigmoid(g.gate))).astype(BF16)
        h2 = ht + _dot_nn(y, wout_ref[...])
        xhat2, rstd2 = _rms(h2)
        wf_row = wf_ref[...]
        err = xhat2 * wf_row - tgt_ref[...]
        loss_ref[...] += 0.5 * jnp.sum(err * err) / D_MODEL
        dout = err * (1.0 / D_MODEL)
        dwf_ref[...] += jnp.sum(dout * xhat2, axis=0, keepdims=True)
        dh2_ref[...] = _rms_bwd(dout * wf_row, xhat2, rstd2)

    row = lambda i: (i, 0)
    return pl.pallas_call(
        body, name="gla_fwd", grid=(nt,),
        in_specs=[pl.BlockSpec((tm, D_MODEL), row), _const((1, D_MODEL)), _const((GLA_IN_WIDTH, D_MODEL)),
                  _const((GLA_LOW_PAD, GLA_KEY_WIDTH)), _const((1, GLA_KEY_WIDTH)), _const((1, GLA_VALUE_WIDTH)),
                  _const((GLA_VALUE_WIDTH, D_MODEL)), _const((1, D_MODEL)), pl.BlockSpec((tm, D_MODEL), row)],
        out_specs=[pl.BlockSpec((tm, D_MODEL), row), pl.BlockSpec((tm, GLA_IN_PAD), row),
                   pl.BlockSpec((tm, GLA_VALUE_WIDTH), row),
                   pl.BlockSpec((cpt, GLA_VALUE_WIDTH, GLA_HEAD_K), lambda i: (i, 0, 0)),
                   _full((8, LANES)), _full((1, D_MODEL))],
        out_shape=[jax.ShapeDtypeStruct((seq, D_MODEL), F32), jax.ShapeDtypeStruct((seq, GLA_IN_PAD), F32),
                   jax.ShapeDtypeStruct((seq, GLA_VALUE_WIDTH), F32),
                   jax.ShapeDtypeStruct((n_chunks, GLA_VALUE_WIDTH, GLA_HEAD_K), F32),
                   jax.ShapeDtypeStruct((8, LANES), F32), jax.ShapeDtypeStruct((1, D_MODEL), F32)],
        scratch_shapes=[pltpu.VMEM((GLA_VALUE_WIDTH, GLA_HEAD_K), F32)],
        compiler_params=_params(),
    )(h1, nw, w_in, gkw, gkb, hw, w_out, wf, target)


def _gla_bwd_call(dh2, proj, o, states, gkw, gkb, hw, w_out):
    seq = dh2.shape[0]
    tm = ROW_TILE
    nt = seq // tm
    cpt = tm // CHUNK

    def body(dh_ref, proj_ref, o_ref, st_ref, gkw_ref, gkb_ref, hw_ref, wout_ref,
             dproj_ref, dwout_hbm, dhw_ref, dgkw_ref, dgkb_ref, dstate_ref, dwout_acc, dwout_stage):
        i = pl.program_id(0)

        @pl.when(i == 0)
        def _():
            dstate_ref[...] = jnp.zeros_like(dstate_ref)
            dwout_acc[...] = jnp.zeros_like(dwout_acc)
            dhw_ref[...] = jnp.zeros_like(dhw_ref)
            dgkw_ref[...] = jnp.zeros_like(dgkw_ref)
            dgkb_ref[...] = jnp.zeros_like(dgkb_ref)

        g = _GlaTile(proj_ref[:, :GLA_KEY_WIDTH], proj_ref[:, GLA_KEY_WIDTH:2 * GLA_KEY_WIDTH],
                     proj_ref[:, 2 * GLA_KEY_WIDTH:GLA_QKVG_WIDTH - GLA_VALUE_WIDTH],
                     proj_ref[:, GLA_QKVG_WIDTH - GLA_VALUE_WIDTH:GLA_QKVG_WIDTH], proj_ref[:, GLA_QKVG_WIDTH:],
                     gkw_ref, gkb_ref[...])
        dhb = dh_ref[...].astype(BF16)
        o = o_ref[...]
        hw_row = hw_ref[...]
        dy = _dot_nt(dhb, wout_ref[...])
        sg = _sigmoid(g.gate)
        silu = g.gate * sg
        don = dy * silu
        on_parts, do_parts, dhw_parts = [], [], []
        for h in range(GLA_HEADS):
            vc = g.vcols(h)
            xh, rs = _rms(o[:, vc])
            on_parts.append(xh * hw_row[:, vc])
            dhw_parts.append(jnp.sum(don[:, vc] * xh, axis=0, keepdims=True))
            do_parts.append(_rms_bwd(don[:, vc] * hw_row[:, vc], xh, rs))
        on = jnp.concatenate(on_parts, axis=1)
        dwout_acc[...] += _dot_tn((on * silu).astype(BF16), dhb)
        dhw_ref[...] += jnp.concatenate(dhw_parts, axis=1)
        dgate = dy * on * (sg * (1.0 + g.gate * (1.0 - sg)))
        do_b = jnp.concatenate(do_parts, axis=1).astype(BF16)

        last_row = lax.broadcasted_iota(jnp.int32, (CHUNK, 1), 0) == CHUNK - 1
        dq_h, dk_h, dv_h, dc_h = [], [], [], []
        for h in range(GLA_HEADS):
            kc, vc = g.kcols(h), g.vcols(h)
            srows = slice(h * GLA_HEAD_V, (h + 1) * GLA_HEAD_V)
            scores = g.scores(h)
            dscores = _dot_nt(do_b[:, vc], g.v_b[:, vc])
            dfwd = jnp.where(g.lower, dscores, 0.0).astype(BF16)
            dbwd = jnp.where(g.upper, dscores, 0.0).astype(BF16)
            dv_intra = _dot_tn(scores, do_b[:, vc])
            da_intra = _dot_nn(dfwd, g.b_b[:, kc])
            db = _dot_tn(dfwd, g.a_b[:, kc])
            dcn = _dot_nn(dbwd, g.dp_b[:, kc])
            ddp = _dot_tn(dbwd, g.cn_b[:, kc])
            dstate = dstate_ref[srows, :]
            da_rows, dkd_rows, dv_rows, dcl_rows = [None] * cpt, [None] * cpt, [None] * cpt, [None] * cpt
            for j in reversed(range(cpt)):
                r = g.rows(j)
                state = st_ref[j, srows, :]
                dstate_b = dstate.astype(BF16)
                do_c = do_b[r, vc]
                dv_rows[j] = dv_intra[r] + _dot_nt(g.kd_b[r, kc], dstate_b)
                da_rows[j] = da_intra[r] + _dot_nn(do_c, state.astype(BF16))
                dkd = _dot_nn(g.v_b[r, vc], dstate_b) * g.e_rest[r, kc]
                dkd_rows[j] = dkd
                decay = jnp.exp(g.c_last[j][:, kc])
                dc_last = (jnp.sum(dkd * g.k[r, kc], axis=0, keepdims=True)
                           + decay * jnp.sum(state * dstate, axis=0, keepdims=True))
                dcl_rows[j] = jnp.where(last_row, dc_last, 0.0)
                dstate = _dot_tn(do_c, g.a_b[r, kc]) + dstate * decay
            dstate_ref[srows, :] = dstate
            da = jnp.concatenate(da_rows, axis=0)
            dkd = jnp.concatenate(dkd_rows, axis=0)
            dv_h.append(jnp.concatenate(dv_rows, axis=0))
            q_up, q_down = da * g.e_pos[:, kc], dcn * g.e_neg[:, kc]
            k_up, k_down = ddp * g.e_pos[:, kc], db * g.e_neg[:, kc] + dkd
            dq_h.append(Q_SCALE * (q_up + q_down))
            dk_h.append(k_up + k_down)
            dc_h.append(g.q[:, kc] * (q_up - q_down) + g.k[:, kc] * (k_up - k_down)
                        + jnp.concatenate(dcl_rows, axis=0))
        dq = jnp.concatenate(dq_h, axis=1)
        dk = jnp.concatenate(dk_h, axis=1)
        dv = jnp.concatenate(dv_h, axis=1)
        dlog_g = _chunk_scan(jnp.concatenate(dc_h, axis=1), True)
        dz = dlog_g * (1.0 / GATE_NORMALIZER) * (1.0 - _sigmoid(g.z))
        dzb = dz.astype(BF16)
        dgkb_ref[...] += jnp.sum(dz, axis=0, keepdims=True)
        dgkw_ref[...] += _dot_tn(g.low_b, dzb)
        dlow = _dot_nt(dzb, gkw_ref[...])
        dproj_ref[...] = jnp.concatenate([dq, dk, dv, dgate, dlow], axis=1).astype(BF16)

        @pl.when(i == nt - 1)
        def _():
            dwout_stage[...] = dwout_acc[...].astype(BF16)
            pltpu.sync_copy(dwout_stage, dwout_hbm)

    rev = lambda i: (nt - 1 - i, 0)
    return pl.pallas_call(
        body, name="gla_bwd", grid=(nt,),
        in_specs=[pl.BlockSpec((tm, D_MODEL), rev), pl.BlockSpec((tm, GLA_IN_PAD), rev),
                  pl.BlockSpec((tm, GLA_VALUE_WIDTH), rev),
                  pl.BlockSpec((cpt, GLA_VALUE_WIDTH, GLA_HEAD_K), lambda i: (nt - 1 - i, 0, 0)),
                  _const((GLA_LOW_PAD, GLA_KEY_WIDTH)), _const((1, GLA_KEY_WIDTH)), _const((1, GLA_VALUE_WIDTH)),
                  _const((GLA_VALUE_WIDTH, D_MODEL))],
        out_specs=[pl.BlockSpec((tm, GLA_IN_PAD), rev), pl.BlockSpec(memory_space=pl.ANY),
                   _full((1, GLA_VALUE_WIDTH)), _full((GLA_LOW_PAD, GLA_KEY_WIDTH)), _full((1, GLA_KEY_WIDTH))],
        out_shape=[jax.ShapeDtypeStruct((seq, GLA_IN_PAD), BF16), jax.ShapeDtypeStruct((GLA_VALUE_WIDTH, D_MODEL), BF16),
                   jax.ShapeDtypeStruct((1, GLA_VALUE_WIDTH), F32), jax.ShapeDtypeStruct((GLA_LOW_PAD, GLA_KEY_WIDTH), F32),
                   jax.ShapeDtypeStruct((1, GLA_KEY_WIDTH), F32)],
        scratch_shapes=[pltpu.VMEM((GLA_VALUE_WIDTH, GLA_HEAD_K), F32), pltpu.VMEM((GLA_VALUE_WIDTH, D_MODEL), F32),
                        pltpu.VMEM((GLA_VALUE_WIDTH, D_MODEL), BF16)],
        compiler_params=_params(),
    )(dh2, proj, o, states, gkw, gkb, hw, w_out)


def _position():
    return lax.axis_index("x"), lax.axis_index("y"), lax.axis_index("c")


def _lead_slot(ref, d):
    return ref.at[d]


def _row_slot(rows):
    return lambda ref, d: ref.at[pl.ds(pl.multiple_of(d * rows, rows), rows)]


def _dim1_slot(size):
    return lambda ref, d: ref.at[:, pl.ds(pl.multiple_of(d * size, size), size)]


class _Gather:
    def __init__(self, in_refs, out_refs, slots, send_sems, recv_sems, local_sems):
        self.in_refs, self.out_refs, self.slots = in_refs, out_refs, slots
        self.send_sems, self.recv_sems, self.local_sems = send_sems, recv_sems, local_sems
        self.n = len(in_refs)
        x, y, c = _position()
        self.c = c
        self.me, self.sibling = (x, y, c), (x, y, 1 - c)
        self.chips = [(1 - x, y), (x, 1 - y), (1 - x, 1 - y)]

    def _copy(self, a, k, block, to, from_input=False):
        part = self.slots[a](self.out_refs[a], 4 * block[0] + 2 * block[1] + block[2])
        return pltpu.make_async_remote_copy(
            src_ref=self.in_refs[a] if from_input else part, dst_ref=part,
            send_sem=self.send_sems.at[a, k], recv_sem=self.recv_sems.at[a, k], device_id=to, device_id_type=MESH)

    def _mine(self):
        return [pltpu.make_async_copy(self.in_refs[a], self.slots[a](self.out_refs[a], 4 * self.me[0] + 2 * self.me[1]
                                                                    + self.me[2]), self.local_sems.at[a])
                for a in range(self.n)]

    def _first(self):
        first = [self._copy(a, 0, self.me, self.sibling, True) for a in range(self.n)]
        return first + [self._copy(a, 1 + j, self.me, (*chip, self.c), True)
                        for j, chip in enumerate(self.chips) for a in range(self.n)]

    def _passed(self):
        return [self._copy(a, 4 + j, (*chip, self.c), self.sibling)
                for j, chip in enumerate(self.chips) for a in range(self.n)]

    def start(self):
        for cp in self._mine() + self._first():
            cp.start()

    def forward(self):
        passed = self._passed()
        for j, chip in enumerate(self.chips):
            for a in range(self.n):
                self._copy(a, 1 + j, (*chip, self.c), self.me).wait_recv()
                passed[j * self.n + a].start()

    def finish(self):
        for a in range(self.n):
            self._copy(a, 0, self.sibling, self.me).wait_recv()
        for j, chip in enumerate(self.chips):
            for a in range(self.n):
                self._copy(a, 4 + j, (*chip, 1 - self.c), self.me).wait_recv()
        for cp in self._first() + self._passed():
            cp.wait_send()
        for cp in self._mine():
            cp.wait()


class _Exchange:
    def __init__(self, in_refs, out_refs, slots, send_sems, recv_sems, local_sems):
        self.in_refs, self.out_refs, self.slots = in_refs, out_refs, slots
        self.send_sems, self.recv_sems, self.local_sems = send_sems, recv_sems, local_sems
        self.n = len(in_refs)
        self.pos = _position()

    def _copies(self):
        x, y, c = self.pos
        me = 4 * x + 2 * y + c
        mine = [pltpu.make_async_copy(self.slots[a](self.in_refs[a], me), self.out_refs[a].at[me],
                                      self.local_sems.at[a]) for a in range(self.n)]
        remote = []
        for k in range(1, N_DEV):
            px, py, pc = x ^ (k >> 2), y ^ ((k >> 1) & 1), c ^ (k & 1)
            for a in range(self.n):
                remote.append(pltpu.make_async_remote_copy(
                    src_ref=self.slots[a](self.in_refs[a], 4 * px + 2 * py + pc), dst_ref=self.out_refs[a].at[me],
                    send_sem=self.send_sems.at[a, k - 1], recv_sem=self.recv_sems.at[a, k - 1],
                    device_id=(px, py, pc), device_id_type=MESH))
        return mine, remote

    def start(self):
        mine, remote = self._copies()
        for cp in mine + remote:
            cp.start()

    def forward(self):
        pass

    def finish(self):
        mine, remote = self._copies()
        for cp in remote:
            cp.wait_recv()
        for cp in remote:
            cp.wait_send()
        for cp in mine:
            cp.wait()


class _Rider:
    def __init__(self, kind, arrays, out_shapes, slots):
        self.kind, self.arrays, self.slots = kind, list(arrays), slots
        self.n = len(self.arrays)
        hbm = pl.BlockSpec(memory_space=pl.ANY)
        self.in_specs = [hbm] * self.n
        self.out_specs = [hbm] * self.n
        self.out_shape = [jax.ShapeDtypeStruct(tuple(s), a.dtype) for s, a in zip(out_shapes, self.arrays)]
        self.scratch = [pltpu.SemaphoreType.DMA((self.n, 7)), pltpu.SemaphoreType.DMA((self.n, 7)),
                        pltpu.SemaphoreType.DMA((self.n,))]

    def bind(self, in_refs, out_refs, sems):
        return self.kind(in_refs, out_refs, self.slots, *sems)


def _gather_rider(shards, full_shapes, slots):
    return _Rider(_Gather, shards, full_shapes, slots)


def _exchange_rider(sends, part_shapes, slots):
    return _Rider(_Exchange, sends, [(N_DEV,) + tuple(s) for s in part_shapes], slots)


def _split_refs(refs, n_in, n_out, n_scratch, rider):
    k = rider.n if rider is not None else 0
    ins, r_ins = refs[:n_in], refs[n_in:n_in + k]
    outs, r_outs = refs[n_in + k:n_in + k + n_out], refs[n_in + k + n_out:n_in + 2 * k + n_out]
    rest = refs[n_in + 2 * k + n_out:]
    scratch, sems = rest[:n_scratch], rest[n_scratch:]
    comm = rider.bind(r_ins, r_outs, sems) if rider is not None else None
    return ins + outs + scratch, comm


def _ride_before(comm, i, nt):
    if comm is not None:
        pl.when(i == 0)(comm.start)
        pl.when(i == nt - 1)(comm.forward)


def _ride_after(comm, i, nt):
    if comm is not None:
        pl.when(i == nt - 1)(comm.finish)


def _extend(specs, rider, field):
    return list(specs) + (getattr(rider, field) if rider is not None else [])


def _comm_call(name, rider):
    def body(*refs):
        _, comm = _split_refs(refs, 0, 0, 0, rider)
        comm.start()
        comm.forward()
        comm.finish()

    return pl.pallas_call(body, name=name, in_specs=rider.in_specs, out_specs=rider.out_specs,
                          out_shape=rider.out_shape, scratch_shapes=rider.scratch)(*rider.arrays)


N_CHIPS = 4


def _reduce_scatter_call(grad, small_sends, small_part_shapes, small_slots):
    rows, cols = grad.shape[0], grad.shape[1] // N_DEV
    n_small = len(small_sends)

    def body(*refs):
        g_ref, small_in = refs[0], refs[1:1 + n_small]
        chip_ref, small_out = refs[1 + n_small], refs[2 + n_small:2 + 2 * n_small]
        (own_buf, recv_buf, part_buf, small_send, small_recv, small_local,
         swap_send, swap_recv, local_sems, chip_send, chip_recv) = refs[2 + 2 * n_small:]
        x, y, c = _position()
        my_chip = 2 * x + y
        small = _Exchange(small_in, small_out, small_slots, small_send, small_recv, small_local)
        small.start()

        def block(chip, core):
            return g_ref.at[:, pl.ds(pl.multiple_of((2 * chip + core) * cols, cols), cols)]

        swap = [pltpu.make_async_remote_copy(
            src_ref=block(q, 1 - c), dst_ref=recv_buf.at[q], send_sem=swap_send.at[q], recv_sem=swap_recv.at[q],
            device_id=(x, y, 1 - c), device_id_type=MESH) for q in range(N_CHIPS)]
        mine = [pltpu.make_async_copy(block(q, c), own_buf.at[q], local_sems.at[q]) for q in range(N_CHIPS)]
        for cp in swap + mine:
            cp.start()
        for q in range(N_CHIPS):
            mine[q].wait()
            swap[q].wait_recv()
            part_buf[q] = (own_buf[q].astype(F32) + recv_buf[q].astype(F32)).astype(BF16)
        to_chips = []
        for k in range(1, N_CHIPS):
            px, py = x ^ (k >> 1), y ^ (k & 1)
            to_chips.append(pltpu.make_async_remote_copy(
                src_ref=part_buf.at[2 * px + py], dst_ref=chip_ref.at[my_chip],
                send_sem=chip_send.at[k - 1], recv_sem=chip_recv.at[k - 1],
                device_id=(px, py, c), device_id_type=MESH))
        own = pltpu.make_async_copy(part_buf.at[my_chip], chip_ref.at[my_chip], local_sems.at[N_CHIPS])
        for cp in to_chips + [own]:
            cp.start()
        for cp in to_chips:
            cp.wait_recv()
        for cp in to_chips + swap:
            cp.wait_send()
        own.wait()
        small.finish()

    hbm = pl.BlockSpec(memory_space=pl.ANY)
    half = pltpu.VMEM((N_CHIPS, rows, cols), grad.dtype)
    return pl.pallas_call(
        body, name="grads_reduce_scatter", in_specs=[hbm] * (1 + n_small), out_specs=[hbm] * (1 + n_small),
        out_shape=[jax.ShapeDtypeStruct((N_CHIPS, rows, cols), grad.dtype)]
        + [jax.ShapeDtypeStruct((N_DEV,) + tuple(s), a.dtype) for s, a in zip(small_part_shapes, small_sends)],
        scratch_shapes=[half, half, half,
                        pltpu.SemaphoreType.DMA((n_small, 7)), pltpu.SemaphoreType.DMA((n_small, 7)),
                        pltpu.SemaphoreType.DMA((n_small,)),
                        pltpu.SemaphoreType.DMA((N_CHIPS,)), pltpu.SemaphoreType.DMA((N_CHIPS,)),
                        pltpu.SemaphoreType.DMA((N_CHIPS + 1,)),
                        pltpu.SemaphoreType.DMA((N_CHIPS - 1,)), pltpu.SemaphoreType.DMA((N_CHIPS - 1,))],
        compiler_params=pltpu.CompilerParams(vmem_limit_bytes=VMEM_LIMIT),
    )(grad, *small_sends)


def _adamw(w, g, m, v):
    m = ADAM_B1 * m + (1.0 - ADAM_B1) * g
    v = ADAM_B2 * v + (1.0 - ADAM_B2) * (g * g)
    m_hat = m / (1.0 - ADAM_B1 ** ADAM_STEP)
    v_hat = v / (1.0 - ADAM_B2 ** ADAM_STEP)
    delta = -ADAM_LR * (m_hat / (jnp.sqrt(v_hat) + ADAM_EPS) + ADAM_WD * w)
    return delta, m, v


def _sum_parts(parts_ref, index=()):
    g = parts_ref[(0,) + index].astype(F32)
    for s in range(1, parts_ref.shape[0]):
        g = g + parts_ref[(s,) + index].astype(F32)
    return g


def _adamw_call(name, parts, w, m, v, block_rows):
    rows, cols = w.shape
    nb = rows // block_rows
    senders = parts.shape[0]

    def body(parts_ref, w_ref, m_ref, v_ref, g_ref, delta_ref, m_out, v_out):
        g = _sum_parts(parts_ref)
        delta, m_new, v_new = _adamw(w_ref[...], g, m_ref[...], v_ref[...])
        g_ref[...] = g
        delta_ref[...] = delta
        m_out[...] = m_new
        v_out[...] = v_new

    blk = pl.BlockSpec((block_rows, cols), lambda i: (i, 0))
    return pl.pallas_call(
        body, name=name, grid=(nb,),
        in_specs=[pl.BlockSpec((senders, block_rows, cols), lambda i: (0, i, 0)), blk, blk, blk],
        out_specs=[blk, blk, blk, blk],
        out_shape=[jax.ShapeDtypeStruct((rows, cols), F32)] * 4,
        compiler_params=_params(("parallel",)),
    )(parts, w, m, v)


def _adamw_slabs_call(name, parts_a, parts_b, w, m, v):
    slabs_a = parts_a.shape[1]

    def body(pa_ref, pb_ref, w_ref, m_ref, v_ref, g_ref, delta_ref, m_out, v_out):
        for parts_ref, part in ((pa_ref, slice(0, slabs_a)), (pb_ref, slice(slabs_a, None))):
            g = _sum_parts(parts_ref)
            delta, m_new, v_new = _adamw(w_ref[part], g, m_ref[part], v_ref[part])
            g_ref[part] = g
            delta_ref[part] = delta
            m_out[part] = m_new
            v_out[part] = v_new

    vmem = pl.BlockSpec(memory_space=pltpu.VMEM)
    return pl.pallas_call(
        body, name=name, in_specs=[vmem] * 5, out_specs=[vmem] * 4,
        out_shape=[jax.ShapeDtypeStruct(w.shape, F32)] * 4,
        compiler_params=pltpu.CompilerParams(vmem_limit_bytes=VMEM_LIMIT),
    )(parts_a, parts_b, w, m, v)


WIDE_ROWS = 8
NARROW_ROWS = 40
NARROW_GKW_ROW = 8
NARROW_GKB_ROW = 24
NARROW_HW_ROW = 32
GROUP_SHARD = POOL_GROUP_DIM // N_DEV
KEY_SHARD = GLA_KEY_WIDTH // N_DEV
HEAD_V_SHARD = GLA_HEAD_V // N_DEV


def _small_adamw_call(wide, narrow, w, m, v):
    names = ("norm_w", "pool_scale", "final_norm_w", "pool_group_b", "gla_gk_w", "gla_gk_b", "gla_head_norm_w")
    where = {
        "norm_w": (0, slice(0, 2), slice(None)),
        "pool_scale": (0, slice(2, 3), slice(None)),
        "final_norm_w": (0, slice(3, 4), slice(None)),
        "pool_group_b": (1, slice(0, POOL_GROUPS), slice(0, GROUP_SHARD)),
        "gla_gk_w": (1, slice(NARROW_GKW_ROW, NARROW_GKW_ROW + GLA_GATE_RANK), slice(0, KEY_SHARD)),
        "gla_gk_b": (1, slice(NARROW_GKB_ROW, NARROW_GKB_ROW + 1), slice(0, KEY_SHARD)),
        "gla_head_norm_w": (1, slice(NARROW_HW_ROW, NARROW_HW_ROW + 1), slice(0, HEAD_V_SHARD)),
    }
    k = len(names)

    def body(*refs):
        parts = refs[0:2]
        w_refs, m_refs, v_refs = refs[2:2 + k], refs[2 + k:2 + 2 * k], refs[2 + 2 * k:2 + 3 * k]
        outs = refs[2 + 3 * k:]
        loss_ref = outs[0]
        loss_ref[...] = _sum_parts(parts[0], (slice(4, 5), slice(0, 1)))
        for i, name in enumerate(names):
            buf, rows, cols = where[name]
            g = _sum_parts(parts[buf], (rows, cols))
            delta, m_new, v_new = _adamw(w_refs[i][...], g, m_refs[i][...], v_refs[i][...])
            outs[1 + i][...] = g
            outs[1 + k + i][...] = delta
            outs[1 + 2 * k + i][...] = m_new
            outs[1 + 3 * k + i][...] = v_new

    vmem = pl.BlockSpec(memory_space=pltpu.VMEM)
    shapes = [jax.ShapeDtypeStruct(w[n].shape, F32) for n in names]
    res = pl.pallas_call(
        body, name="adamw_small", in_specs=[vmem] * (2 + 3 * k), out_specs=[vmem] * (1 + 4 * k),
        out_shape=[jax.ShapeDtypeStruct((1, 1), F32)] + shapes * 4,
    )(wide, narrow, *[w[n] for n in names], *[m[n] for n in names], *[v[n] for n in names])
    unzip = lambda j: dict(zip(names, res[1 + j * k:1 + (j + 1) * k]))
    return res[0], unzip(0), unzip(1), unzip(2), unzip(3)


def kernel(x, norm_w, pool_in_w, pool_group_w, pool_group_b, pool_scale, pool_out_w, gla_in_w, gla_gk_w, gla_gk_b, gla_head_norm_w, gla_out_w, final_norm_w, loss_target, m_norm_w, m_pool_in_w, m_pool_group_w, m_pool_group_b, m_pool_scale, m_pool_out_w, m_gla_in_w, m_gla_gk_w, m_gla_gk_b, m_gla_head_norm_w, m_gla_out_w, m_final_norm_w, v_norm_w, v_pool_in_w, v_pool_group_w, v_pool_group_b, v_pool_scale, v_pool_out_w, v_gla_in_w, v_gla_gk_w, v_gla_gk_b, v_gla_head_norm_w, v_gla_out_w, v_final_norm_w):
    w = dict(norm_w=norm_w, pool_in_w=pool_in_w, pool_group_w=pool_group_w, pool_group_b=pool_group_b,
             pool_scale=pool_scale, pool_out_w=pool_out_w, gla_in_w=gla_in_w, gla_gk_w=gla_gk_w, gla_gk_b=gla_gk_b,
             gla_head_norm_w=gla_head_norm_w, gla_out_w=gla_out_w, final_norm_w=final_norm_w)
    m = dict(norm_w=m_norm_w, pool_in_w=m_pool_in_w, pool_group_w=m_pool_group_w, pool_group_b=m_pool_group_b,
             pool_scale=m_pool_scale, pool_out_w=m_pool_out_w, gla_in_w=m_gla_in_w, gla_gk_w=m_gla_gk_w,
             gla_gk_b=m_gla_gk_b, gla_head_norm_w=m_gla_head_norm_w, gla_out_w=m_gla_out_w,
             final_norm_w=m_final_norm_w)
    v = dict(norm_w=v_norm_w, pool_in_w=v_pool_in_w, pool_group_w=v_pool_group_w, pool_group_b=v_pool_group_b,
             pool_scale=v_pool_scale, pool_out_w=v_pool_out_w, gla_in_w=v_gla_in_w, gla_gk_w=v_gla_gk_w,
             gla_gk_b=v_gla_gk_b, gla_head_norm_w=v_gla_head_norm_w, gla_out_w=v_gla_out_w,
             final_norm_w=v_final_norm_w)
    col_shard = GLA_IN_WIDTH // N_DEV
    row_shard = D_MODEL // N_DEV

    def lanes(a):
        return jnp.pad(a, [(0, 0)] * (a.ndim - 1) + [(0, LANES - a.shape[-1])])

    small_in = jnp.concatenate([lanes(pool_group_b[0]), lanes(gla_gk_b), lanes(gla_head_norm_w),
                                jnp.zeros((2, LANES), F32)], axis=0)
    in_cols = 2 * POOL_WIDTH // N_DEV
    pool_in, pool_gw, pool_out, small_all = _comm_call("pool_weights_all_gather", _gather_rider(
        [pool_in_w[0].astype(BF16), pool_group_w[0].astype(BF16), pool_out_w[0].astype(BF16), small_in],
        [(D_MODEL, 2 * POOL_WIDTH), (POOL_GROUPS, POOL_GROUP_DIM, POOL_GROUP_DIM), (POOL_WIDTH, D_MODEL),
         (N_DEV, 8, LANES)],
        [_dim1_slot(in_cols), _dim1_slot(GROUP_SHARD), _row_slot(row_shard), _lead_slot]))
    pool_gb = jnp.transpose(small_all[:, 0:POOL_GROUPS, :GROUP_SHARD], (1, 0, 2)).reshape(1, POOL_WIDTH)
    gla_gkb = small_all[:, POOL_GROUPS, :KEY_SHARD].reshape(1, GLA_KEY_WIDTH)
    gla_hw = jnp.tile(small_all[:, POOL_GROUPS + 1, :HEAD_V_SHARD].reshape(1, GLA_HEAD_V), (1, GLA_HEADS))
    nw0, nw1, wf = norm_w[0:1], norm_w[1:2], final_norm_w.reshape(1, D_MODEL)
    xs, target = x[0], loss_target[0]

    h1, p, gla_in_parts, gkw_parts, gla_out = _pool_fwd_call(
        xs, nw0, pool_in, pool_gw, pool_gb, pool_scale, pool_out, _gather_rider(
            [jnp.transpose(gla_in_w[0]).astype(BF16), gla_gk_w[0].astype(BF16), gla_out_w[0].astype(BF16)],
            [(N_DEV, col_shard, D_MODEL), (N_DEV, GLA_GATE_RANK, KEY_SHARD), (GLA_VALUE_WIDTH, D_MODEL)],
            [_lead_slot, _lead_slot, _row_slot(row_shard)]))
    gla_in = gla_in_parts.reshape(GLA_IN_WIDTH, D_MODEL)
    gla_gkw = jnp.pad(jnp.transpose(gkw_parts, (1, 0, 2)).reshape(GLA_GATE_RANK, GLA_KEY_WIDTH),
                      ((0, GLA_LOW_PAD - GLA_GATE_RANK), (0, 0)))
    dh2, proj, o, states, loss_part, dwf = _gla_fwd_call(h1, nw1, gla_in, gla_gkw, gla_gkb, gla_hw, gla_out, wf, target)

    dproj, d_gla_out, dhw, dgkw, dgkb = _gla_bwd_call(dh2, proj, o, states, gla_gkw, gla_gkb, gla_hw, gla_out)
    dh1, d_gla_in, dnw1, landed_gla_out = _inproj_bwd_call(
        "gla_in_bwd", dproj, h1, nw1, gla_in, dh2,
        _exchange_rider([d_gla_out], [(row_shard, D_MODEL)], [_row_slot(row_shard)]), transposed=True)
    slabs = col_shard * D_MODEL // (BF16_ROWS * LANES)
    slabs_a = GLA_IN_SLABS_WITH_POOL_BWD
    gla_in_send = d_gla_in.reshape(N_DEV, slabs, BF16_ROWS, LANES)
    dp, d_pool_out, dgw, dgb, dsc, landed_gla_in_a = _pool_bwd_call(
        dh1, p, pool_gw, pool_gb, pool_scale, pool_out,
        _exchange_rider([gla_in_send], [(slabs_a, BF16_ROWS, LANES)], [lambda ref, d: ref.at[d, pl.ds(0, slabs_a)]]))
    grad_x, d_pool_in, dnw0, landed_gla_in_b, landed_pool_out, landed_gw = _inproj_bwd_call(
        "pool_in_bwd", dp, xs, nw0, pool_in, dh1,
        _exchange_rider([gla_in_send, d_pool_out, dgw],
                        [(slabs - slabs_a, BF16_ROWS, LANES), (row_shard, D_MODEL),
                         (POOL_GROUPS, GROUP_SHARD, POOL_GROUP_DIM)],
                        [lambda ref, d: ref.at[d, pl.ds(slabs_a, slabs - slabs_a)], _row_slot(row_shard),
                         _dim1_slot(GROUP_SHARD)]))

    wide = jnp.concatenate([
        dnw0, dnw1, dsc, dwf, jnp.pad(loss_part[0:1, 0:1], ((0, 0), (0, D_MODEL - 1))),
        jnp.zeros((WIDE_ROWS - 5, D_MODEL), F32)], axis=0)

    def rows8(a):
        return jnp.pad(lanes(a), ((0, 0), (0, -a.shape[1] % 8), (0, 0)))

    narrow = jnp.concatenate([
        rows8(jnp.transpose(dgb.reshape(POOL_GROUPS, N_DEV, GROUP_SHARD), (1, 0, 2))),
        rows8(jnp.transpose(dgkw[:GLA_GATE_RANK].reshape(GLA_GATE_RANK, N_DEV, KEY_SHARD), (1, 0, 2))),
        rows8(dgkb.reshape(N_DEV, 1, KEY_SHARD)),
        rows8(dhw.reshape(GLA_HEADS, GLA_HEAD_V).sum(axis=0).reshape(N_DEV, 1, HEAD_V_SHARD)),
    ], axis=1)
    landed_pool_in, landed_wide, landed_narrow = _reduce_scatter_call(
        d_pool_in, [wide, narrow], [(WIDE_ROWS, D_MODEL), (NARROW_ROWS, LANES)], [lambda ref, d: ref, _lead_slot])

    res = {}
    for name, parts, rows, cols, block in [
            ("pool_in_w", landed_pool_in, D_MODEL, in_cols, 256),
            ("pool_group_w", landed_gw, POOL_GROUPS * GROUP_SHARD, POOL_GROUP_DIM, 128),
            ("pool_out_w", landed_pool_out, row_shard, D_MODEL, 128),
            ("gla_out_w", landed_gla_out, row_shard, D_MODEL, 128)]:
        outs = _adamw_call("adamw_" + name, parts.reshape(parts.shape[0], rows, cols), w[name].reshape(rows, cols),
                           m[name].reshape(rows, cols), v[name].reshape(rows, cols), block)
        res[name] = [t.reshape(w[name].shape) for t in outs]
    as_slabs = lambda t: jnp.transpose(t[0]).reshape(slabs, BF16_ROWS, LANES)
    outs = _adamw_slabs_call("adamw_gla_in_w", landed_gla_in_a, landed_gla_in_b, as_slabs(gla_in_w),
                             as_slabs(m_gla_in_w), as_slabs(v_gla_in_w))
    res["gla_in_w"] = [jnp.transpose(t.reshape(col_shard, D_MODEL))[None] for t in outs]
    small_shapes = {"norm_w": (2, D_MODEL), "pool_scale": (1, D_MODEL), "final_norm_w": (1, D_MODEL),
                    "pool_group_b": (POOL_GROUPS, GROUP_SHARD), "gla_gk_w": (GLA_GATE_RANK, KEY_SHARD),
                    "gla_gk_b": (1, KEY_SHARD), "gla_head_norm_w": (1, HEAD_V_SHARD)}
    as_small = lambda t: {n: t[n].reshape(s) for n, s in small_shapes.items()}
    loss, *small_outs = _small_adamw_call(landed_wide, landed_narrow, as_small(w), as_small(m), as_small(v))
    for name in small_shapes:
        res[name] = [t[name].reshape(w[name].shape) for t in small_outs]
    order = ("norm_w", "pool_in_w", "pool_group_w", "pool_group_b", "pool_scale", "pool_out_w", "gla_in_w",
             "gla_gk_w", "gla_gk_b", "gla_head_norm_w", "gla_out_w", "final_norm_w")
    return (loss.reshape(()), grad_x[None], *[res[n][0] for n in order], *[res[n][1] for n in order],
            *[res[n][2] for n in order], *[res[n][3] for n in order])
```

```python
import jax
import jax.numpy as jnp
from jax import lax
from jax.experimental import pallas as pl
from jax.experimental.pallas import tpu as pltpu

F32 = jnp.float32
BF16 = jnp.bfloat16
MESH = pl.DeviceIdType.MESH

N_DEV = 8
D_MODEL = 1024
POOL_WIDTH = 1024
POOL_GROUPS = 4
POOL_GROUP_DIM = 256
POOL_HALO = 16
GLA_HEADS = 4
GLA_HEAD_K = 128
GLA_HEAD_V = 256
GLA_KEY_WIDTH = 512
GLA_VALUE_WIDTH = 1024
GLA_GATE_RANK = 16
GLA_IN_WIDTH = 3088
GLA_IN_PAD = 3200
GLA_LOW_PAD = 128
GLA_QKVG_WIDTH = 3072
CHUNK = 64
GATE_NORMALIZER = 16.0
RMS_EPS = 1e-6
Q_SCALE = GLA_HEAD_K ** -0.5

ADAM_LR = 0.001
ADAM_B1 = 0.9
ADAM_B2 = 0.999
ADAM_EPS = 1e-08
ADAM_WD = 0.01
ADAM_STEP = 10

LANES = 128
BF16_ROWS = 16
VMEM_LIMIT = 56 * 1024 * 1024
ROW_TILE = 256
MATMUL_ROW_TILE = 512
TWO_LEVEL_ADD_STEP = 2


def _dot_nn(a, b):
    return lax.dot_general(a, b, (((1,), (0,)), ((), ())), preferred_element_type=F32)


def _dot_nt(a, b):
    return lax.dot_general(a, b, (((1,), (1,)), ((), ())), preferred_element_type=F32)


def _dot_tn(a, b):
    return lax.dot_general(a, b, (((0,), (0,)), ((), ())), preferred_element_type=F32)


def _rms(x):
    rstd = lax.rsqrt(jnp.mean(x * x, axis=-1, keepdims=True) + RMS_EPS)
    return x * rstd, rstd


def _rms_bwd(dxhat, xhat, rstd):
    return rstd * (dxhat - xhat * jnp.mean(dxhat * xhat, axis=-1, keepdims=True))


def _sigmoid(x):
    return 1.0 / (1.0 + jnp.exp(-x))


def _params(sem=("arbitrary",)):
    return pltpu.CompilerParams(dimension_semantics=sem, vmem_limit_bytes=VMEM_LIMIT)


def _full(shape):
    return pl.BlockSpec(shape, lambda i: (0,) * len(shape))


def _const(shape):
    return pl.BlockSpec(shape, lambda i: (0,) * len(shape), pipeline_mode=pl.Buffered(1))


def _window_sums(ext, forward):
    n = ext.shape[0]
    outs = []
    for g in range(POOL_GROUPS):
        s = ext[:, g * POOL_GROUP_DIM:(g + 1) * POOL_GROUP_DIM]
        for k in range(g + 1):
            shift = (1 << k) if forward else n - (1 << k)
            s = s + pltpu.roll(s, shift, axis=0)
        outs.append(s[:n - POOL_HALO])
    return outs


def _inv_count(row0, tm):
    row = row0 + lax.broadcasted_iota(jnp.int32, (tm, 1), 0)
    return [1.0 / jnp.minimum(row + 1, 2 << g).astype(F32) for g in range(POOL_GROUPS)]


def _pool_mix(u, u_prev, row0, gw_ref, gb):
    tm = u.shape[0]
    sums = _window_sums(jnp.concatenate([u, u_prev], axis=0), True)
    inv = _inv_count(row0, tm)
    pooled, mixed = [], []
    for g in range(POOL_GROUPS):
        ug = u[:, g * POOL_GROUP_DIM:(g + 1) * POOL_GROUP_DIM]
        pg = (sums[g] * inv[g] - ug).astype(BF16)
        pooled.append(pg)
        mixed.append(_dot_nn(pg, gw_ref[g]))
    return pooled, jnp.concatenate(mixed, axis=1) + gb


def _pool_fwd_call(x, nw, w_in, gw, gb, sc, w_out, rider=None):
    seq = x.shape[0]
    tm = min(MATMUL_ROW_TILE, seq)
    nt = seq // tm

    def main(x_ref, nw_ref, win_ref, gw_ref, gb_ref, sc_ref, wout_ref, h_ref, p_ref, halo_ref):
        i = pl.program_id(0)

        @pl.when(i == 0)
        def _():
            halo_ref[...] = jnp.zeros_like(halo_ref)

        xt = x_ref[...]
        xhat, _ = _rms(xt)
        n = (xhat * nw_ref[...]).astype(BF16)
        p = _dot_nn(n, win_ref[...])
        p_ref[...] = p
        u = p[:, :POOL_WIDTH]
        gate = p[:, POOL_WIDTH:]
        _, mixed = _pool_mix(u, halo_ref[...], i * tm, gw_ref, gb_ref[...])
        halo_ref[...] = u[tm - POOL_HALO:, :]
        y = (mixed * sc_ref[...] * (gate * _sigmoid(gate))).astype(BF16)
        h_ref[...] = xt + _dot_nn(y, wout_ref[...])

    def body(*refs):
        own, comm = _split_refs(refs, 7, 2, 1, rider)
        _ride_before(comm, pl.program_id(0), nt)
        main(*own)
        _ride_after(comm, pl.program_id(0), nt)

    return pl.pallas_call(
        body, name="pool_fwd", grid=(nt,),
        in_specs=_extend([pl.BlockSpec((tm, D_MODEL), lambda i: (i, 0)), _const((1, D_MODEL)),
                          _const((D_MODEL, 2 * POOL_WIDTH)), _const((POOL_GROUPS, POOL_GROUP_DIM, POOL_GROUP_DIM)),
                          _const((1, POOL_WIDTH)), _const((1, POOL_WIDTH)), _const((POOL_WIDTH, D_MODEL))],
                         rider, "in_specs"),
        out_specs=_extend([pl.BlockSpec((tm, D_MODEL), lambda i: (i, 0)),
                           pl.BlockSpec((tm, 2 * POOL_WIDTH), lambda i: (i, 0))], rider, "out_specs"),
        out_shape=_extend([jax.ShapeDtypeStruct((seq, D_MODEL), F32),
                           jax.ShapeDtypeStruct((seq, 2 * POOL_WIDTH), F32)], rider, "out_shape"),
        scratch_shapes=_extend([pltpu.VMEM((POOL_HALO, POOL_WIDTH), F32)], rider, "scratch"),
        compiler_params=_params(),
    )(x, nw, w_in, gw, gb, sc, w_out, *_extend([], rider, "arrays"))


def _pool_bwd_call(dh, p, gw, gb, sc, w_out, rider=None):
    seq = dh.shape[0]
    tm = min(MATMUL_ROW_TILE, seq)
    nt = seq // tm
    halo_blocks = tm // POOL_HALO

    def main(dh_ref, p_ref, pprev_ref, gw_ref, gb_ref, sc_ref, wout_ref,
             dp_ref, dwout_hbm, dgw_hbm, dgb_ref, dsc_ref, carry_ref, dwout_acc, dgw_acc, dwout_stage, dgw_stage):
        i = pl.program_id(0)
        t = nt - 1 - i

        @pl.when(i == 0)
        def _():
            carry_ref[...] = jnp.zeros_like(carry_ref)
            dwout_acc[...] = jnp.zeros_like(dwout_acc)
            dgw_acc[...] = jnp.zeros_like(dgw_acc)
            dgb_ref[...] = jnp.zeros_like(dgb_ref)
            dsc_ref[...] = jnp.zeros_like(dsc_ref)

        dhb = dh_ref[...].astype(BF16)
        dy = _dot_nt(dhb, wout_ref[...])
        p = p_ref[...]
        u = p[:, :POOL_WIDTH]
        gate = p[:, POOL_WIDTH:]
        u_prev = jnp.where(t > 0, pprev_ref[:, :POOL_WIDTH], 0.0)
        pooled, mixed = _pool_mix(u, u_prev, t * tm, gw_ref, gb_ref[...])
        sg = _sigmoid(gate)
        silu = gate * sg
        sc = sc_ref[...]
        y = (mixed * sc * silu).astype(BF16)
        dwout_acc[...] += _dot_tn(y, dhb)
        dmixed = dy * sc * silu
        dsc_ref[...] += jnp.sum(dy * mixed * silu, axis=0, keepdims=True)
        dgate = dy * mixed * sc * (sg * (1.0 + gate * (1.0 - sg)))
        dgb_ref[...] += jnp.sum(dmixed, axis=0, keepdims=True)
        inv = _inv_count(t * tm, tm)
        dpooled, scaled = [], []
        for g in range(POOL_GROUPS):
            dmg = dmixed[:, g * POOL_GROUP_DIM:(g + 1) * POOL_GROUP_DIM].astype(BF16)
            dgw_acc[g] += _dot_tn(pooled[g], dmg)
            dpg = _dot_nt(dmg, gw_ref[g])
            dpooled.append(dpg)
            scaled.append(dpg * inv[g])
        r = jnp.concatenate(scaled, axis=1)
        sums = _window_sums(jnp.concatenate([r, carry_ref[...]], axis=0), False)
        carry_ref[...] = r[:POOL_HALO, :]
        du = jnp.concatenate([sums[g] - dpooled[g] for g in range(POOL_GROUPS)], axis=1)
        dp_ref[...] = jnp.concatenate([du, dgate], axis=1).astype(BF16)

        @pl.when(i == nt - 1)
        def _():
            dwout_stage[...] = dwout_acc[...].astype(BF16)
            dgw_stage[...] = dgw_acc[...].astype(BF16)
            pltpu.sync_copy(dwout_stage, dwout_hbm)
            pltpu.sync_copy(dgw_stage, dgw_hbm)

    def body(*refs):
        own, comm = _split_refs(refs, 7, 5, 5, rider)
        _ride_before(comm, pl.program_id(0), nt)
        main(*own)
        _ride_after(comm, pl.program_id(0), nt)

    rev = lambda i: (nt - 1 - i, 0)
    return pl.pallas_call(
        body, name="pool_bwd", grid=(nt,),
        in_specs=_extend([pl.BlockSpec((tm, D_MODEL), rev), pl.BlockSpec((tm, 2 * POOL_WIDTH), rev),
                          pl.BlockSpec((POOL_HALO, 2 * POOL_WIDTH),
                                       lambda i: (jnp.maximum((nt - 1 - i) * halo_blocks - 1, 0), 0)),
                          _const((POOL_GROUPS, POOL_GROUP_DIM, POOL_GROUP_DIM)), _const((1, POOL_WIDTH)),
                          _const((1, POOL_WIDTH)), _const((POOL_WIDTH, D_MODEL))], rider, "in_specs"),
        out_specs=_extend([pl.BlockSpec((tm, 2 * POOL_WIDTH), rev), pl.BlockSpec(memory_space=pl.ANY),
                           pl.BlockSpec(memory_space=pl.ANY), _full((1, POOL_WIDTH)), _full((1, POOL_WIDTH))],
                          rider, "out_specs"),
        out_shape=_extend([jax.ShapeDtypeStruct((seq, 2 * POOL_WIDTH), BF16),
                           jax.ShapeDtypeStruct((POOL_WIDTH, D_MODEL), BF16),
                           jax.ShapeDtypeStruct((POOL_GROUPS, POOL_GROUP_DIM, POOL_GROUP_DIM), BF16),
                           jax.ShapeDtypeStruct((1, POOL_WIDTH), F32), jax.ShapeDtypeStruct((1, POOL_WIDTH), F32)],
                          rider, "out_shape"),
        scratch_shapes=_extend([pltpu.VMEM((POOL_HALO, POOL_WIDTH), F32), pltpu.VMEM((POOL_WIDTH, D_MODEL), F32),
                                pltpu.VMEM((POOL_GROUPS, POOL_GROUP_DIM, POOL_GROUP_DIM), F32),
                                pltpu.VMEM((POOL_WIDTH, D_MODEL), BF16),
                                pltpu.VMEM((POOL_GROUPS, POOL_GROUP_DIM, POOL_GROUP_DIM), BF16)], rider, "scratch"),
        compiler_params=_params(),
    )(dh, p, p, gw, gb, sc, w_out, *_extend([], rider, "arrays"))


def _rows_then_zeros(ref, lo, hi, rows):
    part = ref[lo:hi, :]
    return jnp.concatenate([part, jnp.zeros((rows - (hi - lo), part.shape[1]), part.dtype)], axis=0)


def _inproj_bwd_call(name, dproj, h_in, nw, w_in, dres, rider=None, transposed=False):
    seq = h_in.shape[0]
    width = dproj.shape[1]
    w_shape = tuple(w_in.shape)
    acc_shape = (width, D_MODEL) if transposed else w_shape
    whole = w_shape[0] // LANES * LANES
    tm = min(MATMUL_ROW_TILE, seq)
    nt = seq // tm

    def main(dproj_ref, h_ref, nw_ref, win_ref, dres_ref, dh_ref, dw_hbm, dnw_ref, dw_acc, dw_stage):
        i = pl.program_id(0)

        @pl.when(i == 0)
        def _():
            dw_acc[...] = jnp.zeros_like(dw_acc)
            dnw_ref[...] = jnp.zeros_like(dnw_ref)

        dpb = dproj_ref[...]
        if transposed:
            dn = _dot_nn(dpb[:, :whole], win_ref[0:whole, :])
            if whole < w_shape[0]:
                dn = dn + _dot_nn(dpb[:, whole:], _rows_then_zeros(win_ref, whole, w_shape[0], width - whole))
        else:
            dn = _dot_nt(dpb, win_ref[...])
        xhat, rstd = _rms(h_ref[...])
        nw_row = nw_ref[...]
        n = (xhat * nw_row).astype(BF16)
        dw_acc[...] += _dot_tn(dpb, n) if transposed else _dot_tn(n, dpb)
        dnw_ref[...] += jnp.sum(dn * xhat, axis=0, keepdims=True)
        dh_ref[...] = _rms_bwd(dn * nw_row, xhat, rstd) + dres_ref[...]

        @pl.when(i == nt - 1)
        def _():
            dw_stage[...] = dw_acc[...].astype(BF16)
            pltpu.sync_copy(dw_stage.at[pl.ds(0, w_shape[0])], dw_hbm)

    def body(*refs):
        own, comm = _split_refs(refs, 5, 3, 2, rider)
        _ride_before(comm, pl.program_id(0), nt)
        main(*own)
        _ride_after(comm, pl.program_id(0), nt)

    row = lambda i: (i, 0)
    return pl.pallas_call(
        body, name=name, grid=(nt,),
        in_specs=_extend([pl.BlockSpec((tm, width), row), pl.BlockSpec((tm, D_MODEL), row), _const((1, D_MODEL)),
                          _const(w_shape), pl.BlockSpec((tm, D_MODEL), row)], rider, "in_specs"),
        out_specs=_extend([pl.BlockSpec((tm, D_MODEL), row), pl.BlockSpec(memory_space=pl.ANY),
                           _full((1, D_MODEL))], rider, "out_specs"),
        out_shape=_extend([jax.ShapeDtypeStruct((seq, D_MODEL), F32), jax.ShapeDtypeStruct(w_shape, BF16),
                           jax.ShapeDtypeStruct((1, D_MODEL), F32)], rider, "out_shape"),
        scratch_shapes=_extend([pltpu.VMEM(acc_shape, F32), pltpu.VMEM(acc_shape, BF16)], rider, "scratch"),
        compiler_params=_params(),
    )(dproj, h_in, nw, w_in, dres, *_extend([], rider, "arrays"))


def _chunk_scan(x, reverse):
    n = x.shape[0]
    pos = lax.broadcasted_iota(jnp.int32, (n, 1), 0) & (CHUNK - 1)
    k = 1
    while k < CHUNK:
        if reverse:
            x = x + jnp.where(pos < CHUNK - k, pltpu.roll(x, n - k, axis=0), 0.0)
        else:
            x = x + jnp.where(pos >= k, pltpu.roll(x, k, axis=0), 0.0)
        k *= 2
    return x


class _GlaTile:
    def __init__(self, q, k, v, gate, low, gkw_ref, gkb):
        tm = q.shape[0]
        self.q = q * Q_SCALE
        self.k, self.v, self.gate = k, v, gate
        self.low_b = low.astype(BF16)
        self.z = _dot_nn(self.low_b, gkw_ref[...]) + gkb
        log_g = (jnp.minimum(self.z, 0.0) - jnp.log(1.0 + jnp.exp(-jnp.abs(self.z)))) / GATE_NORMALIZER
        self.c = _chunk_scan(log_g, False)
        is_last = lax.broadcasted_iota(jnp.int32, (CHUNK, 1), 0) == CHUNK - 1
        last = [jnp.sum(jnp.where(is_last, self.c[j * CHUNK:(j + 1) * CHUNK, :], 0.0), axis=0, keepdims=True)
                for j in range(tm // CHUNK)]
        self.c_last = last
        c_last_rows = jnp.concatenate([jnp.broadcast_to(r, (CHUNK, GLA_KEY_WIDTH)) for r in last], axis=0)
        self.e_pos = jnp.exp(self.c)
        self.e_neg = jnp.exp(-self.c)
        self.e_rest = jnp.exp(c_last_rows - self.c)
        self.a_b = (self.q * self.e_pos).astype(BF16)
        self.b_b = (self.k * self.e_neg).astype(BF16)
        self.cn_b = (self.q * self.e_neg).astype(BF16)
        self.dp_b = (self.k * self.e_pos).astype(BF16)
        self.kd_b = (self.k * self.e_rest).astype(BF16)
        self.v_b = self.v.astype(BF16)
        idx_t = lax.broadcasted_iota(jnp.int32, (tm, tm), 0)
        idx_s = lax.broadcasted_iota(jnp.int32, (tm, tm), 1)
        same_chunk = (idx_t ^ idx_s) < CHUNK
        self.lower = same_chunk & (idx_t >= idx_s)
        self.upper = same_chunk & (idx_t < idx_s)

    @staticmethod
    def rows(j):
        return slice(j * CHUNK, (j + 1) * CHUNK)

    @staticmethod
    def kcols(h):
        return slice(h * GLA_HEAD_K, (h + 1) * GLA_HEAD_K)

    @staticmethod
    def vcols(h):
        return slice(h * GLA_HEAD_V, (h + 1) * GLA_HEAD_V)

    def scores(self, h):
        kc = self.kcols(h)
        fwd = _dot_nt(self.a_b[:, kc], self.b_b[:, kc])
        bwd = _dot_nt(self.cn_b[:, kc], self.dp_b[:, kc])
        return jnp.where(self.lower, fwd, jnp.where(self.upper, bwd, 0.0)).astype(BF16)


def _gla_fwd_call(h1, nw, w_in, gkw, gkb, hw, w_out, wf, target):
    seq = h1.shape[0]
    tm = ROW_TILE
    nt = seq // tm
    cpt = tm // CHUNK
    n_chunks = seq // CHUNK

    def body(h_ref, nw_ref, win_ref, gkw_ref, gkb_ref, hw_ref, wout_ref, wf_ref, tgt_ref,
             dh2_ref, proj_ref, o_ref, st_ref, loss_ref, dwf_ref, state_ref):
        i = pl.program_id(0)

        @pl.when(i == 0)
        def _():
            state_ref[...] = jnp.zeros_like(state_ref)
            loss_ref[...] = jnp.zeros_like(loss_ref)
            dwf_ref[...] = jnp.zeros_like(dwf_ref)

        ht = h_ref[...]
        xhat, _ = _rms(ht)
        n = (xhat * nw_ref[...]).astype(BF16)
        sections = {}
        for name, lo, hi in (("low", GLA_QKVG_WIDTH, GLA_IN_PAD), ("qk", 0, 2 * GLA_KEY_WIDTH),
                             ("v", 2 * GLA_KEY_WIDTH, GLA_QKVG_WIDTH - GLA_VALUE_WIDTH),
                             ("gate", GLA_QKVG_WIDTH - GLA_VALUE_WIDTH, GLA_QKVG_WIDTH)):
            rows = (win_ref[lo:hi, :] if hi <= GLA_IN_WIDTH
                    else _rows_then_zeros(win_ref, lo, GLA_IN_WIDTH, hi - lo))
            sections[name] = _dot_nt(n, rows)
            proj_ref[:, lo:hi] = sections[name]
        g = _GlaTile(sections["qk"][:, :GLA_KEY_WIDTH], sections["qk"][:, GLA_KEY_WIDTH:], sections["v"],
                     sections["gate"], sections["low"], gkw_ref, gkb_ref[...])
        o_heads = []
        for h in range(GLA_HEADS):
            kc, vc = g.kcols(h), g.vcols(h)
            srows = slice(h * GLA_HEAD_V, (h + 1) * GLA_HEAD_V)
            o_intra = _dot_nn(g.scores(h), g.v_b[:, vc])
            state = state_ref[srows, :]
            o_rows = []
            for j in range(cpt):
                r = g.rows(j)
                st_ref[j, srows, :] = state
                o_rows.append(o_intra[r] + _dot_nt(g.a_b[r, kc], state.astype(BF16)))
                decay = jnp.exp(g.c_last[j][:, kc])
                state = state * decay + _dot_tn(g.v_b[r, vc], g.kd_b[r, kc])
            state_ref[srows, :] = state
            o_heads.append(jnp.concatenate(o_rows, axis=0))
        o = jnp.concatenate(o_heads, axis=1)
        o_ref[...] = o
        hw_row = hw_ref[...]
        on = jnp.concatenate([_rms(o[:, g.vcols(h)])[0] for h in range(GLA_HEADS)], axis=1) * hw_row
        y = (on * (g.gate * _sigmoid(g.gate))).astype(BF16)
        h2 = ht + _dot_nn(y, wout_ref[...])
        xhat2, rstd2 = _rms(h2)
        wf_row = wf_ref[...]
        err = xhat2 * wf_row - tgt_ref[...]
        loss_ref[...] += 0.5 * jnp.sum(err * err) / D_MODEL
        dout = err * (1.0 / D_MODEL)
        dwf_ref[...] += jnp.sum(dout * xhat2, axis=0, keepdims=True)
        dh2_ref[...] = _rms_bwd(dout * wf_row, xhat2, rstd2)

    row = lambda i: (i, 0)
    return pl.pallas_call(
        body, name="gla_fwd", grid=(nt,),
        in_specs=[pl.BlockSpec((tm, D_MODEL), row), _const((1, D_MODEL)), _const((GLA_IN_WIDTH, D_MODEL)),
                  _const((GLA_LOW_PAD, GLA_KEY_WIDTH)), _const((1, GLA_KEY_WIDTH)), _const((1, GLA_VALUE_WIDTH)),
                  _const((GLA_VALUE_WIDTH, D_MODEL)), _const((1, D_MODEL)), pl.BlockSpec((tm, D_MODEL), row)],
        out_specs=[pl.BlockSpec((tm, D_MODEL), row), pl.BlockSpec((tm, GLA_IN_PAD), row),
                   pl.BlockSpec((tm, GLA_VALUE_WIDTH), row),
                   pl.BlockSpec((cpt, GLA_VALUE_WIDTH, GLA_HEAD_K), lambda i: (i, 0, 0)),
                   _full((8, LANES)), _full((1, D_MODEL))],
        out_shape=[jax.ShapeDtypeStruct((seq, D_MODEL), F32), jax.ShapeDtypeStruct((seq, GLA_IN_PAD), F32),
                   jax.ShapeDtypeStruct((seq, GLA_VALUE_WIDTH), F32),
                   jax.ShapeDtypeStruct((n_chunks, GLA_VALUE_WIDTH, GLA_HEAD_K), F32),
                   jax.ShapeDtypeStruct((8, LANES), F32), jax.ShapeDtypeStruct((1, D_MODEL), F32)],
        scratch_shapes=[pltpu.VMEM((GLA_VALUE_WIDTH, GLA_HEAD_K), F32)],
        compiler_params=_params(),
    )(h1, nw, w_in, gkw, gkb, hw, w_out, wf, target)


def _gla_bwd_call(dh2, proj, o, states, gkw, gkb, hw, w_out):
    seq = dh2.shape[0]
    tm = ROW_TILE
    nt = seq // tm
    cpt = tm // CHUNK

    def body(dh_ref, proj_ref, o_ref, st_ref, gkw_ref, gkb_ref, hw_ref, wout_ref,
             dproj_ref, dwout_hbm, dhw_ref, dgkw_ref, dgkb_ref, dstate_ref, dwout_acc, dwout_stage):
        i = pl.program_id(0)

        @pl.when(i == 0)
        def _():
            dstate_ref[...] = jnp.zeros_like(dstate_ref)
            dwout_acc[...] = jnp.zeros_like(dwout_acc)
            dhw_ref[...] = jnp.zeros_like(dhw_ref)
            dgkw_ref[...] = jnp.zeros_like(dgkw_ref)
            dgkb_ref[...] = jnp.zeros_like(dgkb_ref)

        dhb = dh_ref[...].astype(BF16)
        dy = _dot_nt(dhb, wout_ref[...])
        g = _GlaTile(proj_ref[:, :GLA_KEY_WIDTH], proj_ref[:, GLA_KEY_WIDTH:2 * GLA_KEY_WIDTH],
                     proj_ref[:, 2 * GLA_KEY_WIDTH:GLA_QKVG_WIDTH - GLA_VALUE_WIDTH],
                     proj_ref[:, GLA_QKVG_WIDTH - GLA_VALUE_WIDTH:GLA_QKVG_WIDTH], proj_ref[:, GLA_QKVG_WIDTH:],
                     gkw_ref, gkb_ref[...])
        o = o_ref[...]
        hw_row = hw_ref[...]
        sg = _sigmoid(g.gate)
        silu = g.gate * sg
        don = dy * silu
        on_parts, do_parts, dhw_parts = [], [], []
        for h in range(GLA_HEADS):
            vc = g.vcols(h)
            xh, rs = _rms(o[:, vc])
            on_parts.append(xh * hw_row[:, vc])
            dhw_parts.append(jnp.sum(don[:, vc] * xh, axis=0, keepdims=True))
            do_parts.append(_rms_bwd(don[:, vc] * hw_row[:, vc], xh, rs))
        on = jnp.concatenate(on_parts, axis=1)
        dwout_acc[...] += _dot_tn((on * silu).astype(BF16), dhb)
        dhw_ref[...] += jnp.concatenate(dhw_parts, axis=1)
        dgate = dy * on * (sg * (1.0 + g.gate * (1.0 - sg)))
        do_b = jnp.concatenate(do_parts, axis=1).astype(BF16)

        last_row = lax.broadcasted_iota(jnp.int32, (CHUNK, 1), 0) == CHUNK - 1
        dq_h, dk_h, dv_h, dc_h = [], [], [], []
        for h in range(GLA_HEADS):
            kc, vc = g.kcols(h), g.vcols(h)
            srows = slice(h * GLA_HEAD_V, (h + 1) * GLA_HEAD_V)
            scores = g.scores(h)
            dscores = _dot_nt(do_b[:, vc], g.v_b[:, vc])
            dfwd = jnp.where(g.lower, dscores, 0.0).astype(BF16)
            dbwd = jnp.where(g.upper, dscores, 0.0).astype(BF16)
            dv_intra = _dot_tn(scores, do_b[:, vc])
            da_intra = _dot_nn(dfwd, g.b_b[:, kc])
            db = _dot_tn(dfwd, g.a_b[:, kc])
            dcn = _dot_nn(dbwd, g.dp_b[:, kc])
            ddp = _dot_tn(dbwd, g.cn_b[:, kc])
            dstate = dstate_ref[srows, :]
            da_rows, dkd_rows, dv_rows, dcl_rows = [None] * cpt, [None] * cpt, [None] * cpt, [None] * cpt
            for j in reversed(range(cpt)):
                r = g.rows(j)
                state = st_ref[j, srows, :]
                dstate_b = dstate.astype(BF16)
                do_c = do_b[r, vc]
                dv_rows[j] = dv_intra[r] + _dot_nt(g.kd_b[r, kc], dstate_b)
                da_rows[j] = da_intra[r] + _dot_nn(do_c, state.astype(BF16))
                dkd = _dot_nn(g.v_b[r, vc], dstate_b) * g.e_rest[r, kc]
                dkd_rows[j] = dkd
                decay = jnp.exp(g.c_last[j][:, kc])
                dc_last = (jnp.sum(dkd * g.k[r, kc], axis=0, keepdims=True)
                           + decay * jnp.sum(state * dstate, axis=0, keepdims=True))
                dcl_rows[j] = jnp.where(last_row, dc_last, 0.0)
                dstate = _dot_tn(do_c, g.a_b[r, kc]) + dstate * decay
            dstate_ref[srows, :] = dstate
            da = jnp.concatenate(da_rows, axis=0)
            dkd = jnp.concatenate(dkd_rows, axis=0)
            dv_h.append(jnp.concatenate(dv_rows, axis=0))
            q_up, q_down = da * g.e_pos[:, kc], dcn * g.e_neg[:, kc]
            k_up, k_down = ddp * g.e_pos[:, kc], db * g.e_neg[:, kc] + dkd
            dq_h.append(Q_SCALE * (q_up + q_down))
            dk_h.append(k_up + k_down)
            dc_h.append(g.q[:, kc] * (q_up - q_down) + g.k[:, kc] * (k_up - k_down)
                        + jnp.concatenate(dcl_rows, axis=0))
        dq = jnp.concatenate(dq_h, axis=1)
        dk = jnp.concatenate(dk_h, axis=1)
        dv = jnp.concatenate(dv_h, axis=1)
        dlog_g = _chunk_scan(jnp.concatenate(dc_h, axis=1), True)
        dz = dlog_g * (1.0 / GATE_NORMALIZER) * (1.0 - _sigmoid(g.z))
        dzb = dz.astype(BF16)
        dgkb_ref[...] += jnp.sum(dz, axis=0, keepdims=True)
        dgkw_ref[...] += _dot_tn(g.low_b, dzb)
        dlow = _dot_nt(dzb, gkw_ref[...])
        dproj_ref[...] = jnp.concatenate([dq, dk, dv, dgate, dlow], axis=1).astype(BF16)

        @pl.when(i == nt - 1)
        def _():
            dwout_stage[...] = dwout_acc[...].astype(BF16)
            pltpu.sync_copy(dwout_stage, dwout_hbm)

    rev = lambda i: (nt - 1 - i, 0)
    return pl.pallas_call(
        body, name="gla_bwd", grid=(nt,),
        in_specs=[pl.BlockSpec((tm, D_MODEL), rev), pl.BlockSpec((tm, GLA_IN_PAD), rev),
                  pl.BlockSpec((tm, GLA_VALUE_WIDTH), rev),
                  pl.BlockSpec((cpt, GLA_VALUE_WIDTH, GLA_HEAD_K), lambda i: (nt - 1 - i, 0, 0)),
                  _const((GLA_LOW_PAD, GLA_KEY_WIDTH)), _const((1, GLA_KEY_WIDTH)), _const((1, GLA_VALUE_WIDTH)),
                  _const((GLA_VALUE_WIDTH, D_MODEL))],
        out_specs=[pl.BlockSpec((tm, GLA_IN_PAD), rev), pl.BlockSpec(memory_space=pl.ANY),
                   _full((1, GLA_VALUE_WIDTH)), _full((GLA_LOW_PAD, GLA_KEY_WIDTH)), _full((1, GLA_KEY_WIDTH))],
        out_shape=[jax.ShapeDtypeStruct((seq, GLA_IN_PAD), BF16), jax.ShapeDtypeStruct((GLA_VALUE_WIDTH, D_MODEL), BF16),
                   jax.ShapeDtypeStruct((1, GLA_VALUE_WIDTH), F32), jax.ShapeDtypeStruct((GLA_LOW_PAD, GLA_KEY_WIDTH), F32),
                   jax.ShapeDtypeStruct((1, GLA_KEY_WIDTH), F32)],
        scratch_shapes=[pltpu.VMEM((GLA_VALUE_WIDTH, GLA_HEAD_K), F32), pltpu.VMEM((GLA_VALUE_WIDTH, D_MODEL), F32),
                        pltpu.VMEM((GLA_VALUE_WIDTH, D_MODEL), BF16)],
        compiler_params=_params(),
    )(dh2, proj, o, states, gkw, gkb, hw, w_out)


def _position():
    return lax.axis_index("x"), lax.axis_index("y"), lax.axis_index("c")


def _lead_slot(ref, d):
    return ref.at[d]


def _row_slot(rows):
    return lambda ref, d: ref.at[pl.ds(pl.multiple_of(d * rows, rows), rows)]


def _dim1_slot(size):
    return lambda ref, d: ref.at[:, pl.ds(pl.multiple_of(d * size, size), size)]


class _Gather:
    def __init__(self, in_refs, out_refs, slots, send_sems, recv_sems, local_sems):
        self.in_refs, self.out_refs, self.slots = in_refs, out_refs, slots
        self.send_sems, self.recv_sems, self.local_sems = send_sems, recv_sems, local_sems
        self.n = len(in_refs)
        x, y, c = _position()
        self.c = c
        self.me, self.sibling = (x, y, c), (x, y, 1 - c)
        self.chips = [(1 - x, y), (x, 1 - y), (1 - x, 1 - y)]

    def _copy(self, a, k, block, to, from_input=False):
        part = self.slots[a](self.out_refs[a], 4 * block[0] + 2 * block[1] + block[2])
        return pltpu.make_async_remote_copy(
            src_ref=self.in_refs[a] if from_input else part, dst_ref=part,
            send_sem=self.send_sems.at[a, k], recv_sem=self.recv_sems.at[a, k], device_id=to, device_id_type=MESH)

    def _mine(self):
        return [pltpu.make_async_copy(self.in_refs[a], self.slots[a](self.out_refs[a], 4 * self.me[0] + 2 * self.me[1]
                                                                    + self.me[2]), self.local_sems.at[a])
                for a in range(self.n)]

    def _first(self):
        first = [self._copy(a, 0, self.me, self.sibling, True) for a in range(self.n)]
        return first + [self._copy(a, 1 + j, self.me, (*chip, self.c), True)
                        for j, chip in enumerate(self.chips) for a in range(self.n)]

    def _passed(self):
        return [self._copy(a, 4 + j, (*chip, self.c), self.sibling)
                for j, chip in enumerate(self.chips) for a in range(self.n)]

    def start(self):
        for cp in self._mine() + self._first():
            cp.start()

    def forward(self):
        passed = self._passed()
        for j, chip in enumerate(self.chips):
            for a in range(self.n):
                self._copy(a, 1 + j, (*chip, self.c), self.me).wait_recv()
                passed[j * self.n + a].start()

    def finish(self):
        for a in range(self.n):
            self._copy(a, 0, self.sibling, self.me).wait_recv()
        for j, chip in enumerate(self.chips):
            for a in range(self.n):
                self._copy(a, 4 + j, (*chip, 1 - self.c), self.me).wait_recv()
        for cp in self._first() + self._passed():
            cp.wait_send()
        for cp in self._mine():
            cp.wait()


class _Exchange:
    def __init__(self, in_refs, out_refs, slots, send_sems, recv_sems, local_sems):
        self.in_refs, self.out_refs, self.slots = in_refs, out_refs, slots
        self.send_sems, self.recv_sems, self.local_sems = send_sems, recv_sems, local_sems
        self.n = len(in_refs)
        self.pos = _position()

    def _copies(self):
        x, y, c = self.pos
        me = 4 * x + 2 * y + c
        mine = [pltpu.make_async_copy(self.slots[a](self.in_refs[a], me), self.out_refs[a].at[me],
                                      self.local_sems.at[a]) for a in range(self.n)]
        remote = []
        for k in range(1, N_DEV):
            px, py, pc = x ^ (k >> 2), y ^ ((k >> 1) & 1), c ^ (k & 1)
            for a in range(self.n):
                remote.append(pltpu.make_async_remote_copy(
                    src_ref=self.slots[a](self.in_refs[a], 4 * px + 2 * py + pc), dst_ref=self.out_refs[a].at[me],
                    send_sem=self.send_sems.at[a, k - 1], recv_sem=self.recv_sems.at[a, k - 1],
                    device_id=(px, py, pc), device_id_type=MESH))
        return mine, remote

    def start(self):
        mine, remote = self._copies()
        for cp in mine + remote:
            cp.start()

    def forward(self):
        pass

    def finish(self):
        mine, remote = self._copies()
        for cp in remote:
            cp.wait_recv()
        for cp in remote:
            cp.wait_send()
        for cp in mine:
            cp.wait()


class _Rider:
    def __init__(self, kind, arrays, out_shapes, slots, scratch=None, forward_step=None):
        self.kind, self.arrays, self.slots = kind, list(arrays), slots
        self.n = len(self.arrays)
        hbm = pl.BlockSpec(memory_space=pl.ANY)
        self.in_specs = [hbm] * self.n
        self.out_specs = [hbm] * self.n
        self.out_shape = [jax.ShapeDtypeStruct(tuple(s), a.dtype) for s, a in zip(out_shapes, self.arrays)]
        self.scratch = scratch if scratch is not None else [
            pltpu.SemaphoreType.DMA((self.n, 7)), pltpu.SemaphoreType.DMA((self.n, 7)),
            pltpu.SemaphoreType.DMA((self.n,))]
        self.forward_step = forward_step

    def bind(self, in_refs, out_refs, scratch):
        return self.kind(in_refs, out_refs, self.slots, *scratch)


def _gather_rider(shards, full_shapes, slots):
    return _Rider(_Gather, shards, full_shapes, slots)


def _exchange_rider(sends, part_shapes, slots):
    return _Rider(_Exchange, sends, [(N_DEV,) + tuple(s) for s in part_shapes], slots)


def _split_refs(refs, n_in, n_out, n_scratch, rider):
    k = rider.n if rider is not None else 0
    ins, r_ins = refs[:n_in], refs[n_in:n_in + k]
    outs, r_outs = refs[n_in + k:n_in + k + n_out], refs[n_in + k + n_out:n_in + 2 * k + n_out]
    rest = refs[n_in + 2 * k + n_out:]
    scratch, r_scratch = rest[:n_scratch], rest[n_scratch:]
    comm = rider.bind(r_ins, r_outs, r_scratch) if rider is not None else None
    if comm is not None:
        comm.forward_step = rider.forward_step
    return ins + outs + scratch, comm


def _ride_before(comm, i, nt):
    if comm is not None:
        pl.when(i == 0)(comm.start)
        pl.when(i == (nt - 1 if comm.forward_step is None else min(comm.forward_step, nt - 1)))(comm.forward)


def _ride_after(comm, i, nt):
    if comm is not None:
        pl.when(i == nt - 1)(comm.finish)


def _extend(specs, rider, field):
    return list(specs) + (getattr(rider, field) if rider is not None else [])


def _comm_call(name, rider):
    def body(*refs):
        _, comm = _split_refs(refs, 0, 0, 0, rider)
        comm.start()
        comm.forward()
        comm.finish()

    return pl.pallas_call(body, name=name, in_specs=rider.in_specs, out_specs=rider.out_specs,
                          out_shape=rider.out_shape, scratch_shapes=rider.scratch,
                          compiler_params=pltpu.CompilerParams(vmem_limit_bytes=VMEM_LIMIT))(*rider.arrays)


N_CHIPS = 4


class _TwoLevel:
    def __init__(self, in_refs, out_refs, slots, *scratch):
        self.in_refs, self.out_refs, self.slots = in_refs, out_refs, slots
        self.n = n = len(in_refs)
        self.own_bufs, self.recv_bufs = scratch[:n], scratch[n:2 * n]
        self.swap_send, self.swap_recv, self.local_sems, self.chip_send, self.chip_recv = scratch[2 * n:]
        self.pos = _position()

    def _swap(self):
        x, y, c = self.pos
        return [pltpu.make_async_remote_copy(
            src_ref=self.slots[a](self.in_refs[a], 2 * q + 1 - c), dst_ref=self.recv_bufs[a].at[q],
            send_sem=self.swap_send.at[a, q], recv_sem=self.swap_recv.at[a, q],
            device_id=(x, y, 1 - c), device_id_type=MESH) for a in range(self.n) for q in range(N_CHIPS)]

    def _mine(self):
        c = self.pos[2]
        return [pltpu.make_async_copy(self.slots[a](self.in_refs[a], 2 * q + c), self.own_bufs[a].at[q],
                                      self.local_sems.at[a, q]) for a in range(self.n) for q in range(N_CHIPS)]

    def _to_chips(self):
        x, y, c = self.pos
        copies = []
        for k in range(1, N_CHIPS):
            px, py = x ^ (k >> 1), y ^ (k & 1)
            copies += [pltpu.make_async_remote_copy(
                src_ref=self.own_bufs[a].at[2 * px + py], dst_ref=self.out_refs[a].at[2 * x + y],
                send_sem=self.chip_send.at[a, k - 1], recv_sem=self.chip_recv.at[a, k - 1],
                device_id=(px, py, c), device_id_type=MESH) for a in range(self.n)]
        return copies

    def _own(self):
        x, y, _ = self.pos
        return [pltpu.make_async_copy(self.own_bufs[a].at[2 * x + y], self.out_refs[a].at[2 * x + y],
                                      self.local_sems.at[a, N_CHIPS]) for a in range(self.n)]

    def start(self):
        for cp in self._swap() + self._mine():
            cp.start()

    def forward(self):
        swap, mine = self._swap(), self._mine()
        for a in range(self.n):
            for q in range(N_CHIPS):
                mine[a * N_CHIPS + q].wait()
                swap[a * N_CHIPS + q].wait_recv()
                self.own_bufs[a][q] = (self.own_bufs[a][q].astype(F32)
                                       + self.recv_bufs[a][q].astype(F32)).astype(BF16)
        for cp in self._to_chips() + self._own():
            cp.start()

    def finish(self):
        to_chips = self._to_chips()
        for cp in to_chips:
            cp.wait_recv()
        for cp in to_chips + self._swap():
            cp.wait_send()
        for cp in self._own():
            cp.wait()


def _two_level_rider(sends, part_shapes, slots, forward_step=None):
    n = len(sends)
    bufs = [pltpu.VMEM((N_CHIPS,) + tuple(s), a.dtype) for s, a in zip(part_shapes, sends)]
    scratch = bufs + bufs + [pltpu.SemaphoreType.DMA((n, N_CHIPS)), pltpu.SemaphoreType.DMA((n, N_CHIPS)),
                             pltpu.SemaphoreType.DMA((n, N_CHIPS + 1)), pltpu.SemaphoreType.DMA((n, N_CHIPS - 1)),
                             pltpu.SemaphoreType.DMA((n, N_CHIPS - 1))]
    return _Rider(_TwoLevel, sends, [(N_CHIPS,) + tuple(s) for s in part_shapes], slots, scratch, forward_step)


class _Joined:
    def __init__(self, first, second):
        self.first, self.second = first, second

    def start(self):
        self.first.start()
        self.second.start()

    def forward(self):
        self.first.forward()
        self.second.forward()

    def finish(self):
        self.first.finish()
        self.second.finish()


class _JoinedRider:
    def __init__(self, first, second):
        self.first, self.second = first, second
        self.n = first.n + second.n
        self.arrays = first.arrays + second.arrays
        self.in_specs = first.in_specs + second.in_specs
        self.out_specs = first.out_specs + second.out_specs
        self.out_shape = first.out_shape + second.out_shape
        self.scratch = first.scratch + second.scratch
        self.forward_step = first.forward_step

    def bind(self, in_refs, out_refs, scratch):
        k, s = self.first.n, len(self.first.scratch)
        return _Joined(self.first.bind(in_refs[:k], out_refs[:k], scratch[:s]),
                       self.second.bind(in_refs[k:], out_refs[k:], scratch[s:]))


def _adamw(w, g, m, v):
    m = ADAM_B1 * m + (1.0 - ADAM_B1) * g
    v = ADAM_B2 * v + (1.0 - ADAM_B2) * (g * g)
    m_hat = m / (1.0 - ADAM_B1 ** ADAM_STEP)
    v_hat = v / (1.0 - ADAM_B2 ** ADAM_STEP)
    delta = -ADAM_LR * (m_hat / (jnp.sqrt(v_hat) + ADAM_EPS) + ADAM_WD * w)
    return delta, m, v


def _sum_parts(parts_ref, index=()):
    g = parts_ref[(0,) + index].astype(F32)
    for s in range(1, parts_ref.shape[0]):
        g = g + parts_ref[(s,) + index].astype(F32)
    return g


def _adamw_call(name, parts, w, m, v, block_rows):
    rows, cols = w.shape
    nb = rows // block_rows
    senders = parts.shape[0]

    def body(parts_ref, w_ref, m_ref, v_ref, g_ref, delta_ref, m_out, v_out):
        g = _sum_parts(parts_ref)
        delta, m_new, v_new = _adamw(w_ref[...], g, m_ref[...], v_ref[...])
        g_ref[...] = g
        delta_ref[...] = delta
        m_out[...] = m_new
        v_out[...] = v_new

    blk = pl.BlockSpec((block_rows, cols), lambda i: (i, 0))
    return pl.pallas_call(
        body, name=name, grid=(nb,),
        in_specs=[pl.BlockSpec((senders, block_rows, cols), lambda i: (0, i, 0)), blk, blk, blk],
        out_specs=[blk, blk, blk, blk],
        out_shape=[jax.ShapeDtypeStruct((rows, cols), F32)] * 4,
        compiler_params=_params(("parallel",)),
    )(parts, w, m, v)


def _adamw_slabs_call(name, parts, w, m, v):
    def body(parts_ref, w_ref, m_ref, v_ref, g_ref, delta_ref, m_out, v_out):
        g = _sum_parts(parts_ref)
        delta, m_new, v_new = _adamw(w_ref[...], g, m_ref[...], v_ref[...])
        g_ref[...] = g
        delta_ref[...] = delta
        m_out[...] = m_new
        v_out[...] = v_new

    vmem = pl.BlockSpec(memory_space=pltpu.VMEM)
    return pl.pallas_call(
        body, name=name, in_specs=[vmem] * 4, out_specs=[vmem] * 4,
        out_shape=[jax.ShapeDtypeStruct(w.shape, F32)] * 4,
        compiler_params=pltpu.CompilerParams(vmem_limit_bytes=VMEM_LIMIT),
    )(parts, w, m, v)


WIDE_ROWS = 8
NARROW_ROWS = 40
NARROW_GKW_ROW = 8
NARROW_GKB_ROW = 24
NARROW_HW_ROW = 32
GROUP_SHARD = POOL_GROUP_DIM // N_DEV
KEY_SHARD = GLA_KEY_WIDTH // N_DEV
HEAD_V_SHARD = GLA_HEAD_V // N_DEV


def _small_adamw_call(wide, narrow, w, m, v):
    names = ("norm_w", "pool_scale", "final_norm_w", "pool_group_b", "gla_gk_w", "gla_gk_b", "gla_head_norm_w")
    where = {
        "norm_w": (0, slice(0, 2), slice(None)),
        "pool_scale": (0, slice(2, 3), slice(None)),
        "final_norm_w": (0, slice(3, 4), slice(None)),
        "pool_group_b": (1, slice(0, POOL_GROUPS), slice(0, GROUP_SHARD)),
        "gla_gk_w": (1, slice(NARROW_GKW_ROW, NARROW_GKW_ROW + GLA_GATE_RANK), slice(0, KEY_SHARD)),
        "gla_gk_b": (1, slice(NARROW_GKB_ROW, NARROW_GKB_ROW + 1), slice(0, KEY_SHARD)),
        "gla_head_norm_w": (1, slice(NARROW_HW_ROW, NARROW_HW_ROW + 1), slice(0, HEAD_V_SHARD)),
    }
    k = len(names)

    def body(*refs):
        parts = refs[0:2]
        w_refs, m_refs, v_refs = refs[2:2 + k], refs[2 + k:2 + 2 * k], refs[2 + 2 * k:2 + 3 * k]
        outs = refs[2 + 3 * k:]
        loss_ref = outs[0]
        loss_ref[...] = _sum_parts(parts[0], (slice(4, 5), slice(0, 1)))
        for i, name in enumerate(names):
            buf, rows, cols = where[name]
            g = _sum_parts(parts[buf], (rows, cols))
            delta, m_new, v_new = _adamw(w_refs[i][...], g, m_refs[i][...], v_refs[i][...])
            outs[1 + i][...] = g
            outs[1 + k + i][...] = delta
            outs[1 + 2 * k + i][...] = m_new
            outs[1 + 3 * k + i][...] = v_new

    vmem = pl.BlockSpec(memory_space=pltpu.VMEM)
    shapes = [jax.ShapeDtypeStruct(w[n].shape, F32) for n in names]
    res = pl.pallas_call(
        body, name="adamw_small", in_specs=[vmem] * (2 + 3 * k), out_specs=[vmem] * (1 + 4 * k),
        out_shape=[jax.ShapeDtypeStruct((1, 1), F32)] + shapes * 4,
    )(wide, narrow, *[w[n] for n in names], *[m[n] for n in names], *[v[n] for n in names])
    unzip = lambda j: dict(zip(names, res[1 + j * k:1 + (j + 1) * k]))
    return res[0], unzip(0), unzip(1), unzip(2), unzip(3)


def kernel(x, norm_w, pool_in_w, pool_group_w, pool_group_b, pool_scale, pool_out_w, gla_in_w, gla_gk_w, gla_gk_b, gla_head_norm_w, gla_out_w, final_norm_w, loss_target, m_norm_w, m_pool_in_w, m_pool_group_w, m_pool_group_b, m_pool_scale, m_pool_out_w, m_gla_in_w, m_gla_gk_w, m_gla_gk_b, m_gla_head_norm_w, m_gla_out_w, m_final_norm_w, v_norm_w, v_pool_in_w, v_pool_group_w, v_pool_group_b, v_pool_scale, v_pool_out_w, v_gla_in_w, v_gla_gk_w, v_gla_gk_b, v_gla_head_norm_w, v_gla_out_w, v_final_norm_w):
    w = dict(norm_w=norm_w, pool_in_w=pool_in_w, pool_group_w=pool_group_w, pool_group_b=pool_group_b,
             pool_scale=pool_scale, pool_out_w=pool_out_w, gla_in_w=gla_in_w, gla_gk_w=gla_gk_w, gla_gk_b=gla_gk_b,
             gla_head_norm_w=gla_head_norm_w, gla_out_w=gla_out_w, final_norm_w=final_norm_w)
    m = dict(norm_w=m_norm_w, pool_in_w=m_pool_in_w, pool_group_w=m_pool_group_w, pool_group_b=m_pool_group_b,
             pool_scale=m_pool_scale, pool_out_w=m_pool_out_w, gla_in_w=m_gla_in_w, gla_gk_w=m_gla_gk_w,
             gla_gk_b=m_gla_gk_b, gla_head_norm_w=m_gla_head_norm_w, gla_out_w=m_gla_out_w,
             final_norm_w=m_final_norm_w)
    v = dict(norm_w=v_norm_w, pool_in_w=v_pool_in_w, pool_group_w=v_pool_group_w, pool_group_b=v_pool_group_b,
             pool_scale=v_pool_scale, pool_out_w=v_pool_out_w, gla_in_w=v_gla_in_w, gla_gk_w=v_gla_gk_w,
             gla_gk_b=v_gla_gk_b, gla_head_norm_w=v_gla_head_norm_w, gla_out_w=v_gla_out_w,
             final_norm_w=v_final_norm_w)
    col_shard = GLA_IN_WIDTH // N_DEV
    row_shard = D_MODEL // N_DEV

    def lanes(a):
        return jnp.pad(a, [(0, 0)] * (a.ndim - 1) + [(0, LANES - a.shape[-1])])

    small_in = jnp.concatenate([lanes(pool_group_b[0]), lanes(gla_gk_b), lanes(gla_head_norm_w),
                                jnp.zeros((2, LANES), F32)], axis=0)
    in_cols = 2 * POOL_WIDTH // N_DEV
    pool_in, pool_gw, pool_out, small_all = _comm_call("pool_weights_all_gather", _gather_rider(
        [pool_in_w[0].astype(BF16), pool_group_w[0].astype(BF16), pool_out_w[0].astype(BF16), small_in],
        [(D_MODEL, 2 * POOL_WIDTH), (POOL_GROUPS, POOL_GROUP_DIM, POOL_GROUP_DIM), (POOL_WIDTH, D_MODEL),
         (N_DEV, 8, LANES)],
        [_dim1_slot(in_cols), _dim1_slot(GROUP_SHARD), _row_slot(row_shard), _lead_slot]))
    pool_gb = jnp.transpose(small_all[:, 0:POOL_GROUPS, :GROUP_SHARD], (1, 0, 2)).reshape(1, POOL_WIDTH)
    gla_gkb = small_all[:, POOL_GROUPS, :KEY_SHARD].reshape(1, GLA_KEY_WIDTH)
    gla_hw = jnp.tile(small_all[:, POOL_GROUPS + 1, :HEAD_V_SHARD].reshape(1, GLA_HEAD_V), (1, GLA_HEADS))
    nw0, nw1, wf = norm_w[0:1], norm_w[1:2], final_norm_w.reshape(1, D_MODEL)
    xs, target = x[0], loss_target[0]

    h1, p, gla_in_parts, gkw_parts, gla_out = _pool_fwd_call(
        xs, nw0, pool_in, pool_gw, pool_gb, pool_scale, pool_out, _gather_rider(
            [jnp.transpose(gla_in_w[0]).astype(BF16), gla_gk_w[0].astype(BF16), gla_out_w[0].astype(BF16)],
            [(N_DEV, col_shard, D_MODEL), (N_DEV, GLA_GATE_RANK, KEY_SHARD), (GLA_VALUE_WIDTH, D_MODEL)],
            [_lead_slot, _lead_slot, _row_slot(row_shard)]))
    gla_in = gla_in_parts.reshape(GLA_IN_WIDTH, D_MODEL)
    gla_gkw = jnp.pad(jnp.transpose(gkw_parts, (1, 0, 2)).reshape(GLA_GATE_RANK, GLA_KEY_WIDTH),
                      ((0, GLA_LOW_PAD - GLA_GATE_RANK), (0, 0)))
    dh2, proj, o, states, loss_part, dwf = _gla_fwd_call(h1, nw1, gla_in, gla_gkw, gla_gkb, gla_hw, gla_out, wf, target)

    dproj, d_gla_out, dhw, dgkw, dgkb = _gla_bwd_call(dh2, proj, o, states, gla_gkw, gla_gkb, gla_hw, gla_out)
    dh1, d_gla_in, dnw1, landed_gla_out = _inproj_bwd_call(
        "gla_in_bwd", dproj, h1, nw1, gla_in, dh2,
        _exchange_rider([d_gla_out], [(row_shard, D_MODEL)], [_row_slot(row_shard)]), transposed=True)
    slabs = col_shard * D_MODEL // (BF16_ROWS * LANES)
    gla_in_send = d_gla_in.reshape(N_DEV, slabs, BF16_ROWS, LANES)
    dp, d_pool_out, dgw, dgb, dsc, landed_gla_in = _pool_bwd_call(
        dh1, p, pool_gw, pool_gb, pool_scale, pool_out,
        _two_level_rider([gla_in_send], [(slabs, BF16_ROWS, LANES)], [_lead_slot], TWO_LEVEL_ADD_STEP))
    grad_x, d_pool_in, dnw0, landed_pool_out, landed_gw = _inproj_bwd_call(
        "pool_in_bwd", dp, xs, nw0, pool_in, dh1,
        _two_level_rider([d_pool_out, dgw], [(row_shard, D_MODEL), (POOL_GROUPS, GROUP_SHARD, POOL_GROUP_DIM)],
                         [_row_slot(row_shard), _dim1_slot(GROUP_SHARD)], TWO_LEVEL_ADD_STEP))

    wide = jnp.concatenate([
        dnw0, dnw1, dsc, dwf, jnp.pad(loss_part[0:1, 0:1], ((0, 0), (0, D_MODEL - 1))),
        jnp.zeros((WIDE_ROWS - 5, D_MODEL), F32)], axis=0)

    def rows8(a):
        return jnp.pad(lanes(a), ((0, 0), (0, -a.shape[1] % 8), (0, 0)))

    narrow = jnp.concatenate([
        rows8(jnp.transpose(dgb.reshape(POOL_GROUPS, N_DEV, GROUP_SHARD), (1, 0, 2))),
        rows8(jnp.transpose(dgkw[:GLA_GATE_RANK].reshape(GLA_GATE_RANK, N_DEV, KEY_SHARD), (1, 0, 2))),
        rows8(dgkb.reshape(N_DEV, 1, KEY_SHARD)),
        rows8(dhw.reshape(GLA_HEADS, GLA_HEAD_V).sum(axis=0).reshape(N_DEV, 1, HEAD_V_SHARD)),
    ], axis=1)
    landed_pool_in, landed_wide, landed_narrow = _comm_call("grads_reduce_scatter", _JoinedRider(
        _two_level_rider([d_pool_in], [(D_MODEL, in_cols)], [_dim1_slot(in_cols)]),
        _exchange_rider([wide, narrow], [(WIDE_ROWS, D_MODEL), (NARROW_ROWS, LANES)],
                        [lambda ref, d: ref, _lead_slot])))

    res = {}
    for name, parts, rows, cols, block in [
            ("pool_in_w", landed_pool_in, D_MODEL, in_cols, 256),
            ("pool_group_w", landed_gw, POOL_GROUPS * GROUP_SHARD, POOL_GROUP_DIM, 128),
            ("pool_out_w", landed_pool_out, row_shard, D_MODEL, 128),
            ("gla_out_w", landed_gla_out, row_shard, D_MODEL, 128)]:
        outs = _adamw_call("adamw_" + name, parts.reshape(parts.shape[0], rows, cols), w[name].reshape(rows, cols),
                           m[name].reshape(rows, cols), v[name].reshape(rows, cols), block)
        res[name] = [t.reshape(w[name].shape) for t in outs]
    as_slabs = lambda t: jnp.transpose(t[0]).reshape(slabs, BF16_ROWS, LANES)
    outs = _adamw_slabs_call("adamw_gla_in_w", landed_gla_in, as_slabs(gla_in_w),
                             as_slabs(m_gla_in_w), as_slabs(v_gla_in_w))
    res["gla_in_w"] = [jnp.transpose(t.reshape(col_shard, D_MODEL))[None] for t in outs]
    small_shapes = {"norm_w": (2, D_MODEL), "pool_scale": (1, D_MODEL), "final_norm_w": (1, D_MODEL),
                    "pool_group_b": (POOL_GROUPS, GROUP_SHARD), "gla_gk_w": (GLA_GATE_RANK, KEY_SHARD),
                    "gla_gk_b": (1, KEY_SHARD), "gla_head_norm_w": (1, HEAD_V_SHARD)}
    as_small = lambda t: {n: t[n].reshape(s) for n, s in small_shapes.items()}
    loss, *small_outs = _small_adamw_call(landed_wide, landed_narrow, as_small(w), as_small(m), as_small(v))
    for name in small_shapes:
        res[name] = [t[name].reshape(w[name].shape) for t in small_outs]
    order = ("norm_w", "pool_in_w", "pool_group_w", "pool_group_b", "pool_scale", "pool_out_w", "gla_in_w",
             "gla_gk_w", "gla_gk_b", "gla_head_norm_w", "gla_out_w", "final_norm_w")
    return (loss.reshape(()), grad_x[None], *[res[n][0] for n in order], *[res[n][1] for n in order],
            *[res[n][2] for n in order], *[res[n][3] for n in order])
```

```python
import jax
import jax.numpy as jnp
from jax import lax
from jax.experimental import pallas as pl
from jax.experimental.pallas import tpu as pltpu

F32 = jnp.float32
BF16 = jnp.bfloat16
MESH = pl.DeviceIdType.MESH

N_DEV = 8
D_MODEL = 1024
POOL_WIDTH = 1024
POOL_GROUPS = 4
POOL_GROUP_DIM = 256
POOL_HALO = 16
GLA_HEADS = 4
GLA_HEAD_K = 128
GLA_HEAD_V = 256
GLA_KEY_WIDTH = 512
GLA_VALUE_WIDTH = 1024
GLA_GATE_RANK = 16
GLA_IN_WIDTH = 3088
GLA_IN_PAD = 3200
GLA_LOW_PAD = 128
GLA_QKVG_WIDTH = 3072
CHUNK = 64
GATE_NORMALIZER = 16.0
RMS_EPS = 1e-6
Q_SCALE = GLA_HEAD_K ** -0.5

ADAM_LR = 0.001
ADAM_B1 = 0.9
ADAM_B2 = 0.999
ADAM_EPS = 1e-08
ADAM_WD = 0.01
ADAM_STEP = 10

LANES = 128
BF16_ROWS = 16
VMEM_LIMIT = 56 * 1024 * 1024
ROW_TILE = 256
MATMUL_ROW_TILE = 512
TWO_LEVEL_ADD_STEP = 1


def _dot_nn(a, b):
    return lax.dot_general(a, b, (((1,), (0,)), ((), ())), preferred_element_type=F32)


def _dot_nt(a, b):
    return lax.dot_general(a, b, (((1,), (1,)), ((), ())), preferred_element_type=F32)


def _dot_tn(a, b):
    return lax.dot_general(a, b, (((0,), (0,)), ((), ())), preferred_element_type=F32)


def _rms(x):
    rstd = lax.rsqrt(jnp.mean(x * x, axis=-1, keepdims=True) + RMS_EPS)
    return x * rstd, rstd


def _rms_bwd(dxhat, xhat, rstd):
    return rstd * (dxhat - xhat * jnp.mean(dxhat * xhat, axis=-1, keepdims=True))


def _sigmoid(x):
    return 1.0 / (1.0 + jnp.exp(-x))


def _params(sem=("arbitrary",)):
    return pltpu.CompilerParams(dimension_semantics=sem, vmem_limit_bytes=VMEM_LIMIT)


def _full(shape):
    return pl.BlockSpec(shape, lambda i: (0,) * len(shape))


def _const(shape):
    return pl.BlockSpec(shape, lambda i: (0,) * len(shape), pipeline_mode=pl.Buffered(1))


def _window_sums(ext, forward):
    n = ext.shape[0]
    outs = []
    for g in range(POOL_GROUPS):
        s = ext[:, g * POOL_GROUP_DIM:(g + 1) * POOL_GROUP_DIM]
        for k in range(g + 1):
            shift = (1 << k) if forward else n - (1 << k)
            s = s + pltpu.roll(s, shift, axis=0)
        outs.append(s[:n - POOL_HALO])
    return outs


def _inv_count(row0, tm):
    row = row0 + lax.broadcasted_iota(jnp.int32, (tm, 1), 0)
    return [1.0 / jnp.minimum(row + 1, 2 << g).astype(F32) for g in range(POOL_GROUPS)]


def _pool_mix(u, u_prev, row0, gw_ref, gb):
    tm = u.shape[0]
    sums = _window_sums(jnp.concatenate([u, u_prev], axis=0), True)
    inv = _inv_count(row0, tm)
    pooled, mixed = [], []
    for g in range(POOL_GROUPS):
        ug = u[:, g * POOL_GROUP_DIM:(g + 1) * POOL_GROUP_DIM]
        pg = (sums[g] * inv[g] - ug).astype(BF16)
        pooled.append(pg)
        mixed.append(_dot_nn(pg, gw_ref[g]))
    return pooled, jnp.concatenate(mixed, axis=1) + gb


def _pool_fwd_call(x, nw, w_in, gw, gb, sc, w_out, rider=None):
    seq = x.shape[0]
    tm = min(MATMUL_ROW_TILE, seq)
    nt = seq // tm

    def main(x_ref, nw_ref, win_ref, gw_ref, gb_ref, sc_ref, wout_ref, h_ref, p_ref, halo_ref):
        i = pl.program_id(0)

        @pl.when(i == 0)
        def _():
            halo_ref[...] = jnp.zeros_like(halo_ref)

        xt = x_ref[...]
        xhat, _ = _rms(xt)
        n = (xhat * nw_ref[...]).astype(BF16)
        p = _dot_nn(n, win_ref[...])
        p_ref[...] = p
        u = p[:, :POOL_WIDTH]
        gate = p[:, POOL_WIDTH:]
        _, mixed = _pool_mix(u, halo_ref[...], i * tm, gw_ref, gb_ref[...])
        halo_ref[...] = u[tm - POOL_HALO:, :]
        y = (mixed * sc_ref[...] * (gate * _sigmoid(gate))).astype(BF16)
        h_ref[...] = xt + _dot_nn(y, wout_ref[...])

    def body(*refs):
        own, comm = _split_refs(refs, 7, 2, 1, rider)
        _ride_before(comm, pl.program_id(0), nt)
        main(*own)
        _ride_after(comm, pl.program_id(0), nt)

    return pl.pallas_call(
        body, name="pool_fwd", grid=(nt,),
        in_specs=_extend([pl.BlockSpec((tm, D_MODEL), lambda i: (i, 0)), _const((1, D_MODEL)),
                          _const((D_MODEL, 2 * POOL_WIDTH)), _const((POOL_GROUPS, POOL_GROUP_DIM, POOL_GROUP_DIM)),
                          _const((1, POOL_WIDTH)), _const((1, POOL_WIDTH)), _const((POOL_WIDTH, D_MODEL))],
                         rider, "in_specs"),
        out_specs=_extend([pl.BlockSpec((tm, D_MODEL), lambda i: (i, 0)),
                           pl.BlockSpec((tm, 2 * POOL_WIDTH), lambda i: (i, 0))], rider, "out_specs"),
        out_shape=_extend([jax.ShapeDtypeStruct((seq, D_MODEL), F32),
                           jax.ShapeDtypeStruct((seq, 2 * POOL_WIDTH), F32)], rider, "out_shape"),
        scratch_shapes=_extend([pltpu.VMEM((POOL_HALO, POOL_WIDTH), F32)], rider, "scratch"),
        compiler_params=_params(),
    )(x, nw, w_in, gw, gb, sc, w_out, *_extend([], rider, "arrays"))


def _pool_bwd_call(dh, p, gw, gb, sc, w_out, rider=None):
    seq = dh.shape[0]
    tm = min(MATMUL_ROW_TILE, seq)
    nt = seq // tm
    halo_blocks = tm // POOL_HALO

    def main(dh_ref, p_ref, pprev_ref, gw_ref, gb_ref, sc_ref, wout_ref,
             dp_ref, dwout_hbm, dgw_hbm, dgb_ref, dsc_ref, carry_ref, dwout_acc, dgw_acc, dwout_stage, dgw_stage):
        i = pl.program_id(0)
        t = nt - 1 - i

        @pl.when(i == 0)
        def _():
            carry_ref[...] = jnp.zeros_like(carry_ref)
            dwout_acc[...] = jnp.zeros_like(dwout_acc)
            dgw_acc[...] = jnp.zeros_like(dgw_acc)
            dgb_ref[...] = jnp.zeros_like(dgb_ref)
            dsc_ref[...] = jnp.zeros_like(dsc_ref)

        dhb = dh_ref[...].astype(BF16)
        dy = _dot_nt(dhb, wout_ref[...])
        p = p_ref[...]
        u = p[:, :POOL_WIDTH]
        gate = p[:, POOL_WIDTH:]
        u_prev = jnp.where(t > 0, pprev_ref[:, :POOL_WIDTH], 0.0)
        pooled, mixed = _pool_mix(u, u_prev, t * tm, gw_ref, gb_ref[...])
        sg = _sigmoid(gate)
        silu = gate * sg
        sc = sc_ref[...]
        y = (mixed * sc * silu).astype(BF16)
        dwout_acc[...] += _dot_tn(y, dhb)
        dmixed = dy * sc * silu
        dsc_ref[...] += jnp.sum(dy * mixed * silu, axis=0, keepdims=True)
        dgate = dy * mixed * sc * (sg * (1.0 + gate * (1.0 - sg)))
        dgb_ref[...] += jnp.sum(dmixed, axis=0, keepdims=True)
        inv = _inv_count(t * tm, tm)
        dpooled, scaled = [], []
        for g in range(POOL_GROUPS):
            dmg = dmixed[:, g * POOL_GROUP_DIM:(g + 1) * POOL_GROUP_DIM].astype(BF16)
            dgw_acc[g] += _dot_tn(pooled[g], dmg)
            dpg = _dot_nt(dmg, gw_ref[g])
            dpooled.append(dpg)
            scaled.append(dpg * inv[g])
        r = jnp.concatenate(scaled, axis=1)
        sums = _window_sums(jnp.concatenate([r, carry_ref[...]], axis=0), False)
        carry_ref[...] = r[:POOL_HALO, :]
        du = jnp.concatenate([sums[g] - dpooled[g] for g in range(POOL_GROUPS)], axis=1)
        dp_ref[...] = jnp.concatenate([du, dgate], axis=1).astype(BF16)

        @pl.when(i == nt - 1)
        def _():
            dwout_stage[...] = dwout_acc[...].astype(BF16)
            dgw_stage[...] = dgw_acc[...].astype(BF16)
            pltpu.sync_copy(dwout_stage, dwout_hbm)
            pltpu.sync_copy(dgw_stage, dgw_hbm)

    def body(*refs):
        own, comm = _split_refs(refs, 7, 5, 5, rider)
        _ride_before(comm, pl.program_id(0), nt)
        main(*own)
        _ride_after(comm, pl.program_id(0), nt)

    rev = lambda i: (nt - 1 - i, 0)
    return pl.pallas_call(
        body, name="pool_bwd", grid=(nt,),
        in_specs=_extend([pl.BlockSpec((tm, D_MODEL), rev), pl.BlockSpec((tm, 2 * POOL_WIDTH), rev),
                          pl.BlockSpec((POOL_HALO, 2 * POOL_WIDTH),
                                       lambda i: (jnp.maximum((nt - 1 - i) * halo_blocks - 1, 0), 0)),
                          _const((POOL_GROUPS, POOL_GROUP_DIM, POOL_GROUP_DIM)), _const((1, POOL_WIDTH)),
                          _const((1, POOL_WIDTH)), _const((POOL_WIDTH, D_MODEL))], rider, "in_specs"),
        out_specs=_extend([pl.BlockSpec((tm, 2 * POOL_WIDTH), rev), pl.BlockSpec(memory_space=pl.ANY),
                           pl.BlockSpec(memory_space=pl.ANY), _full((1, POOL_WIDTH)), _full((1, POOL_WIDTH))],
                          rider, "out_specs"),
        out_shape=_extend([jax.ShapeDtypeStruct((seq, 2 * POOL_WIDTH), BF16),
                           jax.ShapeDtypeStruct((POOL_WIDTH, D_MODEL), BF16),
                           jax.ShapeDtypeStruct((POOL_GROUPS, POOL_GROUP_DIM, POOL_GROUP_DIM), BF16),
                           jax.ShapeDtypeStruct((1, POOL_WIDTH), F32), jax.ShapeDtypeStruct((1, POOL_WIDTH), F32)],
                          rider, "out_shape"),
        scratch_shapes=_extend([pltpu.VMEM((POOL_HALO, POOL_WIDTH), F32), pltpu.VMEM((POOL_WIDTH, D_MODEL), F32),
                                pltpu.VMEM((POOL_GROUPS, POOL_GROUP_DIM, POOL_GROUP_DIM), F32),
                                pltpu.VMEM((POOL_WIDTH, D_MODEL), BF16),
                                pltpu.VMEM((POOL_GROUPS, POOL_GROUP_DIM, POOL_GROUP_DIM), BF16)], rider, "scratch"),
        compiler_params=_params(),
    )(dh, p, p, gw, gb, sc, w_out, *_extend([], rider, "arrays"))


def _rows_then_zeros(ref, lo, hi, rows):
    part = ref[lo:hi, :]
    return jnp.concatenate([part, jnp.zeros((rows - (hi - lo), part.shape[1]), part.dtype)], axis=0)


def _inproj_bwd_call(name, dproj, h_in, nw, w_in, dres, rider=None, transposed=False):
    seq = h_in.shape[0]
    width = dproj.shape[1]
    w_shape = tuple(w_in.shape)
    acc_shape = (width, D_MODEL) if transposed else w_shape
    whole = w_shape[0] // LANES * LANES
    tm = min(MATMUL_ROW_TILE, seq)
    nt = seq // tm

    def main(dproj_ref, h_ref, nw_ref, win_ref, dres_ref, dh_ref, dw_hbm, dnw_ref, dw_acc, dw_stage):
        i = pl.program_id(0)

        @pl.when(i == 0)
        def _():
            dw_acc[...] = jnp.zeros_like(dw_acc)
            dnw_ref[...] = jnp.zeros_like(dnw_ref)

        dpb = dproj_ref[...]
        if transposed:
            dn = _dot_nn(dpb[:, :whole], win_ref[0:whole, :])
            if whole < w_shape[0]:
                dn = dn + _dot_nn(dpb[:, whole:], _rows_then_zeros(win_ref, whole, w_shape[0], width - whole))
        else:
            dn = _dot_nt(dpb, win_ref[...])
        xhat, rstd = _rms(h_ref[...])
        nw_row = nw_ref[...]
        n = (xhat * nw_row).astype(BF16)
        dw_acc[...] += _dot_tn(dpb, n) if transposed else _dot_tn(n, dpb)
        dnw_ref[...] += jnp.sum(dn * xhat, axis=0, keepdims=True)
        dh_ref[...] = _rms_bwd(dn * nw_row, xhat, rstd) + dres_ref[...]

        @pl.when(i == nt - 1)
        def _():
            dw_stage[...] = dw_acc[...].astype(BF16)
            pltpu.sync_copy(dw_stage.at[pl.ds(0, w_shape[0])], dw_hbm)

    def body(*refs):
        own, comm = _split_refs(refs, 5, 3, 2, rider)
        _ride_before(comm, pl.program_id(0), nt)
        main(*own)
        _ride_after(comm, pl.program_id(0), nt)

    row = lambda i: (i, 0)
    return pl.pallas_call(
        body, name=name, grid=(nt,),
        in_specs=_extend([pl.BlockSpec((tm, width), row), pl.BlockSpec((tm, D_MODEL), row), _const((1, D_MODEL)),
                          _const(w_shape), pl.BlockSpec((tm, D_MODEL), row)], rider, "in_specs"),
        out_specs=_extend([pl.BlockSpec((tm, D_MODEL), row), pl.BlockSpec(memory_space=pl.ANY),
                           _full((1, D_MODEL))], rider, "out_specs"),
        out_shape=_extend([jax.ShapeDtypeStruct((seq, D_MODEL), F32), jax.ShapeDtypeStruct(w_shape, BF16),
                           jax.ShapeDtypeStruct((1, D_MODEL), F32)], rider, "out_shape"),
        scratch_shapes=_extend([pltpu.VMEM(acc_shape, F32), pltpu.VMEM(acc_shape, BF16)], rider, "scratch"),
        compiler_params=_params(),
    )(dproj, h_in, nw, w_in, dres, *_extend([], rider, "arrays"))


def _chunk_scan(x, reverse):
    n = x.shape[0]
    idx_t = lax.broadcasted_iota(jnp.int32, (n, n), 0)
    idx_s = lax.broadcasted_iota(jnp.int32, (n, n), 1)
    taken = ((idx_t ^ idx_s) < CHUNK) & ((idx_s >= idx_t) if reverse else (idx_s <= idx_t))
    ones = jnp.where(taken, 1.0, 0.0).astype(BF16)
    high = x.astype(BF16)
    rest = x - high.astype(F32)
    mid = rest.astype(BF16)
    low = (rest - mid.astype(F32)).astype(BF16)
    return _dot_nn(ones, high) + _dot_nn(ones, mid) + _dot_nn(ones, low)


class _GlaTile:
    def __init__(self, q, k, v, gate, low, gkw_ref, gkb):
        tm = q.shape[0]
        self.q = q * Q_SCALE
        self.k, self.v, self.gate = k, v, gate
        self.low_b = low.astype(BF16)
        self.z = _dot_nn(self.low_b, gkw_ref[...]) + gkb
        log_g = (jnp.minimum(self.z, 0.0) - jnp.log(1.0 + jnp.exp(-jnp.abs(self.z)))) / GATE_NORMALIZER
        self.c = _chunk_scan(log_g, False)
        is_last = lax.broadcasted_iota(jnp.int32, (CHUNK, 1), 0) == CHUNK - 1
        last = [jnp.sum(jnp.where(is_last, self.c[j * CHUNK:(j + 1) * CHUNK, :], 0.0), axis=0, keepdims=True)
                for j in range(tm // CHUNK)]
        self.c_last = last
        c_last_rows = jnp.concatenate([jnp.broadcast_to(r, (CHUNK, GLA_KEY_WIDTH)) for r in last], axis=0)
        self.e_pos = jnp.exp(self.c)
        self.e_neg = jnp.exp(-self.c)
        self.e_rest = jnp.exp(c_last_rows - self.c)
        self.a_b = (self.q * self.e_pos).astype(BF16)
        self.b_b = (self.k * self.e_neg).astype(BF16)
        self.cn_b = (self.q * self.e_neg).astype(BF16)
        self.dp_b = (self.k * self.e_pos).astype(BF16)
        self.kd_b = (self.k * self.e_rest).astype(BF16)
        self.v_b = self.v.astype(BF16)
        idx_t = lax.broadcasted_iota(jnp.int32, (tm, tm), 0)
        idx_s = lax.broadcasted_iota(jnp.int32, (tm, tm), 1)
        same_chunk = (idx_t ^ idx_s) < CHUNK
        self.lower = same_chunk & (idx_t >= idx_s)
        self.upper = same_chunk & (idx_t < idx_s)

    @staticmethod
    def rows(j):
        return slice(j * CHUNK, (j + 1) * CHUNK)

    @staticmethod
    def kcols(h):
        return slice(h * GLA_HEAD_K, (h + 1) * GLA_HEAD_K)

    @staticmethod
    def vcols(h):
        return slice(h * GLA_HEAD_V, (h + 1) * GLA_HEAD_V)

    def scores(self, h):
        kc = self.kcols(h)
        fwd = _dot_nt(self.a_b[:, kc], self.b_b[:, kc])
        bwd = _dot_nt(self.cn_b[:, kc], self.dp_b[:, kc])
        return jnp.where(self.lower, fwd, jnp.where(self.upper, bwd, 0.0)).astype(BF16)


def _gla_fwd_call(h1, nw, w_in, gkw, gkb, hw, w_out, wf, target):
    seq = h1.shape[0]
    tm = ROW_TILE
    nt = seq // tm
    cpt = tm // CHUNK
    n_chunks = seq // CHUNK

    def body(h_ref, nw_ref, win_ref, gkw_ref, gkb_ref, hw_ref, wout_ref, wf_ref, tgt_ref,
             dh2_ref, proj_ref, o_ref, st_ref, loss_ref, dwf_ref, state_ref):
        i = pl.program_id(0)

        @pl.when(i == 0)
        def _():
            state_ref[...] = jnp.zeros_like(state_ref)
            loss_ref[...] = jnp.zeros_like(loss_ref)
            dwf_ref[...] = jnp.zeros_like(dwf_ref)

        ht = h_ref[...]
        xhat, _ = _rms(ht)
        n = (xhat * nw_ref[...]).astype(BF16)
        sections = {}
        for name, lo, hi in (("low", GLA_QKVG_WIDTH, GLA_IN_PAD), ("qk", 0, 2 * GLA_KEY_WIDTH),
                             ("v", 2 * GLA_KEY_WIDTH, GLA_QKVG_WIDTH - GLA_VALUE_WIDTH),
                             ("gate", GLA_QKVG_WIDTH - GLA_VALUE_WIDTH, GLA_QKVG_WIDTH)):
            rows = (win_ref[lo:hi, :] if hi <= GLA_IN_WIDTH
                    else _rows_then_zeros(win_ref, lo, GLA_IN_WIDTH, hi - lo))
            sections[name] = _dot_nt(n, rows)
            proj_ref[:, lo:hi] = sections[name]
        g = _GlaTile(sections["qk"][:, :GLA_KEY_WIDTH], sections["qk"][:, GLA_KEY_WIDTH:], sections["v"],
                     sections["gate"], sections["low"], gkw_ref, gkb_ref[...])
        o_heads = []
        for h in range(GLA_HEADS):
            kc, vc = g.kcols(h), g.vcols(h)
            srows = slice(h * GLA_HEAD_V, (h + 1) * GLA_HEAD_V)
            o_intra = _dot_nn(g.scores(h), g.v_b[:, vc])
            state = state_ref[srows, :]
            o_rows = []
            for j in range(cpt):
                r = g.rows(j)
                st_ref[j, srows, :] = state
                o_rows.append(o_intra[r] + _dot_nt(g.a_b[r, kc], state.astype(BF16)))
                decay = jnp.exp(g.c_last[j][:, kc])
                state = state * decay + _dot_tn(g.v_b[r, vc], g.kd_b[r, kc])
            state_ref[srows, :] = state
            o_heads.append(jnp.concatenate(o_rows, axis=0))
        o = jnp.concatenate(o_heads, axis=1)
        o_ref[...] = o
        hw_row = hw_ref[...]
        on = jnp.concatenate([_rms(o[:, g.vcols(h)])[0] for h in range(GLA_HEADS)], axis=1) * hw_row
        y = (on * (g.gate * _sigmoid(g.gate))).astype(BF16)
        h2 = ht + _dot_nn(y, wout_ref[...])
        xhat2, rstd2 = _rms(h2)
        wf_row = wf_ref[...]
        err = xhat2 * wf_row - tgt_ref[...]
        loss_ref[...] += 0.5 * jnp.sum(err * err) / D_MODEL
        dout = err * (1.0 / D_MODEL)
        dwf_ref[...] += jnp.sum(dout * xhat2, axis=0, keepdims=True)
        dh2_ref[...] = _rms_bwd(dout * wf_row, xhat2, rstd2)

    row = lambda i: (i, 0)
    return pl.pallas_call(
        body, name="gla_fwd", grid=(nt,),
        in_specs=[pl.BlockSpec((tm, D_MODEL), row), _const((1, D_MODEL)), _const((GLA_IN_WIDTH, D_MODEL)),
                  _const((GLA_LOW_PAD, GLA_KEY_WIDTH)), _const((1, GLA_KEY_WIDTH)), _const((1, GLA_VALUE_WIDTH)),
                  _const((GLA_VALUE_WIDTH, D_MODEL)), _const((1, D_MODEL)), pl.BlockSpec((tm, D_MODEL), row)],
        out_specs=[pl.BlockSpec((tm, D_MODEL), row), pl.BlockSpec((tm, GLA_IN_PAD), row),
                   pl.BlockSpec((tm, GLA_VALUE_WIDTH), row),
                   pl.BlockSpec((cpt, GLA_VALUE_WIDTH, GLA_HEAD_K), lambda i: (i, 0, 0)),
                   _full((8, LANES)), _full((1, D_MODEL))],
        out_shape=[jax.ShapeDtypeStruct((seq, D_MODEL), F32), jax.ShapeDtypeStruct((seq, GLA_IN_PAD), F32),
                   jax.ShapeDtypeStruct((seq, GLA_VALUE_WIDTH), F32),
                   jax.ShapeDtypeStruct((n_chunks, GLA_VALUE_WIDTH, GLA_HEAD_K), F32),
                   jax.ShapeDtypeStruct((8, LANES), F32), jax.ShapeDtypeStruct((1, D_MODEL), F32)],
        scratch_shapes=[pltpu.VMEM((GLA_VALUE_WIDTH, GLA_HEAD_K), F32)],
        compiler_params=_params(),
    )(h1, nw, w_in, gkw, gkb, hw, w_out, wf, target)


def _gla_bwd_call(dh2, proj, o, states, gkw, gkb, hw, w_out):
    seq = dh2.shape[0]
    tm = ROW_TILE
    nt = seq // tm
    cpt = tm // CHUNK

    def body(dh_ref, proj_ref, o_ref, st_ref, gkw_ref, gkb_ref, hw_ref, wout_ref,
             dproj_ref, dwout_hbm, dhw_ref, dgkw_ref, dgkb_ref, dstate_ref, dwout_acc, dwout_stage):
        i = pl.program_id(0)

        @pl.when(i == 0)
        def _():
            dstate_ref[...] = jnp.zeros_like(dstate_ref)
            dwout_acc[...] = jnp.zeros_like(dwout_acc)
            dhw_ref[...] = jnp.zeros_like(dhw_ref)
            dgkw_ref[...] = jnp.zeros_like(dgkw_ref)
            dgkb_ref[...] = jnp.zeros_like(dgkb_ref)

        dhb = dh_ref[...].astype(BF16)
        dy = _dot_nt(dhb, wout_ref[...])
        g = _GlaTile(proj_ref[:, :GLA_KEY_WIDTH], proj_ref[:, GLA_KEY_WIDTH:2 * GLA_KEY_WIDTH],
                     proj_ref[:, 2 * GLA_KEY_WIDTH:GLA_QKVG_WIDTH - GLA_VALUE_WIDTH],
                     proj_ref[:, GLA_QKVG_WIDTH - GLA_VALUE_WIDTH:GLA_QKVG_WIDTH], proj_ref[:, GLA_QKVG_WIDTH:],
                     gkw_ref, gkb_ref[...])
        o = o_ref[...]
        hw_row = hw_ref[...]
        sg = _sigmoid(g.gate)
        silu = g.gate * sg
        don = dy * silu
        on_parts, do_parts, dhw_parts = [], [], []
        for h in range(GLA_HEADS):
            vc = g.vcols(h)
            xh, rs = _rms(o[:, vc])
            on_parts.append(xh * hw_row[:, vc])
            dhw_parts.append(jnp.sum(don[:, vc] * xh, axis=0, keepdims=True))
            do_parts.append(_rms_bwd(don[:, vc] * hw_row[:, vc], xh, rs))
        on = jnp.concatenate(on_parts, axis=1)
        dwout_acc[...] += _dot_tn((on * silu).astype(BF16), dhb)
        dhw_ref[...] += jnp.concatenate(dhw_parts, axis=1)
        dgate = dy * on * (sg * (1.0 + g.gate * (1.0 - sg)))
        do_b = jnp.concatenate(do_parts, axis=1).astype(BF16)

        last_row = lax.broadcasted_iota(jnp.int32, (CHUNK, 1), 0) == CHUNK - 1
        dq_h, dk_h, dv_h, dc_h = [], [], [], []
        for h in range(GLA_HEADS):
            kc, vc = g.kcols(h), g.vcols(h)
            srows = slice(h * GLA_HEAD_V, (h + 1) * GLA_HEAD_V)
            scores = g.scores(h)
            dscores = _dot_nt(do_b[:, vc], g.v_b[:, vc])
            dfwd = jnp.where(g.lower, dscores, 0.0).astype(BF16)
            dbwd = jnp.where(g.upper, dscores, 0.0).astype(BF16)
            dv_intra = _dot_tn(scores, do_b[:, vc])
            da_intra = _dot_nn(dfwd, g.b_b[:, kc])
            db = _dot_tn(dfwd, g.a_b[:, kc])
            dcn = _dot_nn(dbwd, g.dp_b[:, kc])
            ddp = _dot_tn(dbwd, g.cn_b[:, kc])
            dstate = dstate_ref[srows, :]
            da_rows, dkd_rows, dv_rows, dcl_rows = [None] * cpt, [None] * cpt, [None] * cpt, [None] * cpt
            for j in reversed(range(cpt)):
                r = g.rows(j)
                state = st_ref[j, srows, :]
                dstate_b = dstate.astype(BF16)
                do_c = do_b[r, vc]
                dv_rows[j] = dv_intra[r] + _dot_nt(g.kd_b[r, kc], dstate_b)
                da_rows[j] = da_intra[r] + _dot_nn(do_c, state.astype(BF16))
                dkd = _dot_nn(g.v_b[r, vc], dstate_b) * g.e_rest[r, kc]
                dkd_rows[j] = dkd
                decay = jnp.exp(g.c_last[j][:, kc])
                dc_last = (jnp.sum(dkd * g.k[r, kc], axis=0, keepdims=True)
                           + decay * jnp.sum(state * dstate, axis=0, keepdims=True))
                dcl_rows[j] = jnp.where(last_row, dc_last, 0.0)
                dstate = _dot_tn(do_c, g.a_b[r, kc]) + dstate * decay
            dstate_ref[srows, :] = dstate
            da = jnp.concatenate(da_rows, axis=0)
            dkd = jnp.concatenate(dkd_rows, axis=0)
            dv_h.append(jnp.concatenate(dv_rows, axis=0))
            q_up, q_down = da * g.e_pos[:, kc], dcn * g.e_neg[:, kc]
            k_up, k_down = ddp * g.e_pos[:, kc], db * g.e_neg[:, kc] + dkd
            dq_h.append(Q_SCALE * (q_up + q_down))
            dk_h.append(k_up + k_down)
            dc_h.append(g.q[:, kc] * (q_up - q_down) + g.k[:, kc] * (k_up - k_down)
                        + jnp.concatenate(dcl_rows, axis=0))
        dq = jnp.concatenate(dq_h, axis=1)
        dk = jnp.concatenate(dk_h, axis=1)
        dv = jnp.concatenate(dv_h, axis=1)
        dlog_g = _chunk_scan(jnp.concatenate(dc_h, axis=1), True)
        dz = dlog_g * (1.0 / GATE_NORMALIZER) * (1.0 - _sigmoid(g.z))
        dzb = dz.astype(BF16)
        dgkb_ref[...] += jnp.sum(dz, axis=0, keepdims=True)
        dgkw_ref[...] += _dot_tn(g.low_b, dzb)
        dlow = _dot_nt(dzb, gkw_ref[...])
        dproj_ref[...] = jnp.concatenate([dq, dk, dv, dgate, dlow], axis=1).astype(BF16)

        @pl.when(i == nt - 1)
        def _():
            dwout_stage[...] = dwout_acc[...].astype(BF16)
            pltpu.sync_copy(dwout_stage, dwout_hbm)

    rev = lambda i: (nt - 1 - i, 0)
    return pl.pallas_call(
        body, name="gla_bwd", grid=(nt,),
        in_specs=[pl.BlockSpec((tm, D_MODEL), rev), pl.BlockSpec((tm, GLA_IN_PAD), rev),
                  pl.BlockSpec((tm, GLA_VALUE_WIDTH), rev),
                  pl.BlockSpec((cpt, GLA_VALUE_WIDTH, GLA_HEAD_K), lambda i: (nt - 1 - i, 0, 0)),
                  _const((GLA_LOW_PAD, GLA_KEY_WIDTH)), _const((1, GLA_KEY_WIDTH)), _const((1, GLA_VALUE_WIDTH)),
                  _const((GLA_VALUE_WIDTH, D_MODEL))],
        out_specs=[pl.BlockSpec((tm, GLA_IN_PAD), rev), pl.BlockSpec(memory_space=pl.ANY),
                   _full((1, GLA_VALUE_WIDTH)), _full((GLA_LOW_PAD, GLA_KEY_WIDTH)), _full((1, GLA_KEY_WIDTH))],
        out_shape=[jax.ShapeDtypeStruct((seq, GLA_IN_PAD), BF16), jax.ShapeDtypeStruct((GLA_VALUE_WIDTH, D_MODEL), BF16),
                   jax.ShapeDtypeStruct((1, GLA_VALUE_WIDTH), F32), jax.ShapeDtypeStruct((GLA_LOW_PAD, GLA_KEY_WIDTH), F32),
                   jax.ShapeDtypeStruct((1, GLA_KEY_WIDTH), F32)],
        scratch_shapes=[pltpu.VMEM((GLA_VALUE_WIDTH, GLA_HEAD_K), F32), pltpu.VMEM((GLA_VALUE_WIDTH, D_MODEL), F32),
                        pltpu.VMEM((GLA_VALUE_WIDTH, D_MODEL), BF16)],
        compiler_params=_params(),
    )(dh2, proj, o, states, gkw, gkb, hw, w_out)


def _position():
    return lax.axis_index("x"), lax.axis_index("y"), lax.axis_index("c")


def _lead_slot(ref, d):
    return ref.at[d]


def _row_slot(rows):
    return lambda ref, d: ref.at[pl.ds(pl.multiple_of(d * rows, rows), rows)]


def _dim1_slot(size):
    return lambda ref, d: ref.at[:, pl.ds(pl.multiple_of(d * size, size), size)]


class _Gather:
    def __init__(self, in_refs, out_refs, slots, send_sems, recv_sems, local_sems):
        self.in_refs, self.out_refs, self.slots = in_refs, out_refs, slots
        self.send_sems, self.recv_sems, self.local_sems = send_sems, recv_sems, local_sems
        self.n = len(in_refs)
        x, y, c = _position()
        self.c = c
        self.me, self.sibling = (x, y, c), (x, y, 1 - c)
        self.chips = [(1 - x, y), (x, 1 - y), (1 - x, 1 - y)]

    def _copy(self, a, k, block, to, from_input=False):
        part = self.slots[a](self.out_refs[a], 4 * block[0] + 2 * block[1] + block[2])
        return pltpu.make_async_remote_copy(
            src_ref=self.in_refs[a] if from_input else part, dst_ref=part,
            send_sem=self.send_sems.at[a, k], recv_sem=self.recv_sems.at[a, k], device_id=to, device_id_type=MESH)

    def _mine(self):
        return [pltpu.make_async_copy(self.in_refs[a], self.slots[a](self.out_refs[a], 4 * self.me[0] + 2 * self.me[1]
                                                                    + self.me[2]), self.local_sems.at[a])
                for a in range(self.n)]

    def _first(self):
        first = [self._copy(a, 0, self.me, self.sibling, True) for a in range(self.n)]
        return first + [self._copy(a, 1 + j, self.me, (*chip, self.c), True)
                        for j, chip in enumerate(self.chips) for a in range(self.n)]

    def _passed(self):
        return [self._copy(a, 4 + j, (*chip, self.c), self.sibling)
                for j, chip in enumerate(self.chips) for a in range(self.n)]

    def start(self):
        for cp in self._mine() + self._first():
            cp.start()

    def forward(self):
        passed = self._passed()
        for j, chip in enumerate(self.chips):
            for a in range(self.n):
                self._copy(a, 1 + j, (*chip, self.c), self.me).wait_recv()
                passed[j * self.n + a].start()

    def finish(self):
        for a in range(self.n):
            self._copy(a, 0, self.sibling, self.me).wait_recv()
        for j, chip in enumerate(self.chips):
            for a in range(self.n):
                self._copy(a, 4 + j, (*chip, 1 - self.c), self.me).wait_recv()
        for cp in self._first() + self._passed():
            cp.wait_send()
        for cp in self._mine():
            cp.wait()


class _Exchange:
    def __init__(self, in_refs, out_refs, slots, send_sems, recv_sems, local_sems):
        self.in_refs, self.out_refs, self.slots = in_refs, out_refs, slots
        self.send_sems, self.recv_sems, self.local_sems = send_sems, recv_sems, local_sems
        self.n = len(in_refs)
        self.pos = _position()

    def _copies(self):
        x, y, c = self.pos
        me = 4 * x + 2 * y + c
        mine = [pltpu.make_async_copy(self.slots[a](self.in_refs[a], me), self.out_refs[a].at[me],
                                      self.local_sems.at[a]) for a in range(self.n)]
        remote = []
        for k in range(1, N_DEV):
            px, py, pc = x ^ (k >> 2), y ^ ((k >> 1) & 1), c ^ (k & 1)
            for a in range(self.n):
                remote.append(pltpu.make_async_remote_copy(
                    src_ref=self.slots[a](self.in_refs[a], 4 * px + 2 * py + pc), dst_ref=self.out_refs[a].at[me],
                    send_sem=self.send_sems.at[a, k - 1], recv_sem=self.recv_sems.at[a, k - 1],
                    device_id=(px, py, pc), device_id_type=MESH))
        return mine, remote

    def start(self):
        mine, remote = self._copies()
        for cp in mine + remote:
            cp.start()

    def forward(self):
        pass

    def finish(self):
        mine, remote = self._copies()
        for cp in remote:
            cp.wait_recv()
        for cp in remote:
            cp.wait_send()
        for cp in mine:
            cp.wait()


class _Rider:
    def __init__(self, kind, arrays, out_shapes, slots, scratch=None, forward_step=None):
        self.kind, self.arrays, self.slots = kind, list(arrays), slots
        self.n = len(self.arrays)
        hbm = pl.BlockSpec(memory_space=pl.ANY)
        self.in_specs = [hbm] * self.n
        self.out_specs = [hbm] * self.n
        self.out_shape = [jax.ShapeDtypeStruct(tuple(s), a.dtype) for s, a in zip(out_shapes, self.arrays)]
        self.scratch = scratch if scratch is not None else [
            pltpu.SemaphoreType.DMA((self.n, 7)), pltpu.SemaphoreType.DMA((self.n, 7)),
            pltpu.SemaphoreType.DMA((self.n,))]
        self.forward_step = forward_step

    def bind(self, in_refs, out_refs, scratch):
        return self.kind(in_refs, out_refs, self.slots, *scratch)


def _gather_rider(shards, full_shapes, slots):
    return _Rider(_Gather, shards, full_shapes, slots)


def _exchange_rider(sends, part_shapes, slots):
    return _Rider(_Exchange, sends, [(N_DEV,) + tuple(s) for s in part_shapes], slots)


def _split_refs(refs, n_in, n_out, n_scratch, rider):
    k = rider.n if rider is not None else 0
    ins, r_ins = refs[:n_in], refs[n_in:n_in + k]
    outs, r_outs = refs[n_in + k:n_in + k + n_out], refs[n_in + k + n_out:n_in + 2 * k + n_out]
    rest = refs[n_in + 2 * k + n_out:]
    scratch, r_scratch = rest[:n_scratch], rest[n_scratch:]
    comm = rider.bind(r_ins, r_outs, r_scratch) if rider is not None else None
    if comm is not None:
        comm.forward_step = rider.forward_step
    return ins + outs + scratch, comm


def _ride_before(comm, i, nt):
    if comm is not None:
        pl.when(i == 0)(comm.start)
        pl.when(i == (nt - 1 if comm.forward_step is None else min(comm.forward_step, nt - 1)))(comm.forward)


def _ride_after(comm, i, nt):
    if comm is not None:
        pl.when(i == nt - 1)(comm.finish)


def _extend(specs, rider, field):
    return list(specs) + (getattr(rider, field) if rider is not None else [])


def _comm_call(name, rider):
    def body(*refs):
        _, comm = _split_refs(refs, 0, 0, 0, rider)
        comm.start()
        comm.forward()
        comm.finish()

    return pl.pallas_call(body, name=name, in_specs=rider.in_specs, out_specs=rider.out_specs,
                          out_shape=rider.out_shape, scratch_shapes=rider.scratch,
                          compiler_params=pltpu.CompilerParams(vmem_limit_bytes=VMEM_LIMIT))(*rider.arrays)


N_CHIPS = 4


class _TwoLevel:
    def __init__(self, in_refs, out_refs, slots, *scratch):
        self.in_refs, self.out_refs, self.slots = in_refs, out_refs, slots
        self.n = n = len(in_refs)
        self.own_bufs, self.recv_bufs = scratch[:n], scratch[n:2 * n]
        self.swap_send, self.swap_recv, self.local_sems, self.chip_send, self.chip_recv = scratch[2 * n:]
        self.pos = _position()

    def _swap(self):
        x, y, c = self.pos
        return [pltpu.make_async_remote_copy(
            src_ref=self.slots[a](self.in_refs[a], 2 * q + 1 - c), dst_ref=self.recv_bufs[a].at[q],
            send_sem=self.swap_send.at[a, q], recv_sem=self.swap_recv.at[a, q],
            device_id=(x, y, 1 - c), device_id_type=MESH) for a in range(self.n) for q in range(N_CHIPS)]

    def _mine(self):
        c = self.pos[2]
        return [pltpu.make_async_copy(self.slots[a](self.in_refs[a], 2 * q + c), self.own_bufs[a].at[q],
                                      self.local_sems.at[a, q]) for a in range(self.n) for q in range(N_CHIPS)]

    def _to_chips(self):
        x, y, c = self.pos
        copies = []
        for k in range(1, N_CHIPS):
            px, py = x ^ (k >> 1), y ^ (k & 1)
            copies += [pltpu.make_async_remote_copy(
                src_ref=self.own_bufs[a].at[2 * px + py], dst_ref=self.out_refs[a].at[2 * x + y],
                send_sem=self.chip_send.at[a, k - 1], recv_sem=self.chip_recv.at[a, k - 1],
                device_id=(px, py, c), device_id_type=MESH) for a in range(self.n)]
        return copies

    def _own(self):
        x, y, _ = self.pos
        return [pltpu.make_async_copy(self.own_bufs[a].at[2 * x + y], self.out_refs[a].at[2 * x + y],
                                      self.local_sems.at[a, N_CHIPS]) for a in range(self.n)]

    def start(self):
        for cp in self._swap() + self._mine():
            cp.start()

    def forward(self):
        swap, mine = self._swap(), self._mine()
        for a in range(self.n):
            for q in range(N_CHIPS):
                mine[a * N_CHIPS + q].wait()
                swap[a * N_CHIPS + q].wait_recv()
                self.own_bufs[a][q] = (self.own_bufs[a][q].astype(F32)
                                       + self.recv_bufs[a][q].astype(F32)).astype(BF16)
        for cp in self._to_chips() + self._own():
            cp.start()

    def finish(self):
        to_chips = self._to_chips()
        for cp in to_chips:
            cp.wait_recv()
        for cp in to_chips + self._swap():
            cp.wait_send()
        for cp in self._own():
            cp.wait()


def _two_level_rider(sends, part_shapes, slots, forward_step=None):
    n = len(sends)
    bufs = [pltpu.VMEM((N_CHIPS,) + tuple(s), a.dtype) for s, a in zip(part_shapes, sends)]
    scratch = bufs + bufs + [pltpu.SemaphoreType.DMA((n, N_CHIPS)), pltpu.SemaphoreType.DMA((n, N_CHIPS)),
                             pltpu.SemaphoreType.DMA((n, N_CHIPS + 1)), pltpu.SemaphoreType.DMA((n, N_CHIPS - 1)),
                             pltpu.SemaphoreType.DMA((n, N_CHIPS - 1))]
    return _Rider(_TwoLevel, sends, [(N_CHIPS,) + tuple(s) for s in part_shapes], slots, scratch, forward_step)


class _Joined:
    def __init__(self, first, second):
        self.first, self.second = first, second

    def start(self):
        self.first.start()
        self.second.start()

    def forward(self):
        self.first.forward()
        self.second.forward()

    def finish(self):
        self.first.finish()
        self.second.finish()


class _JoinedRider:
    def __init__(self, first, second):
        self.first, self.second = first, second
        self.n = first.n + second.n
        self.arrays = first.arrays + second.arrays
        self.in_specs = first.in_specs + second.in_specs
        self.out_specs = first.out_specs + second.out_specs
        self.out_shape = first.out_shape + second.out_shape
        self.scratch = first.scratch + second.scratch
        self.forward_step = first.forward_step

    def bind(self, in_refs, out_refs, scratch):
        k, s = self.first.n, len(self.first.scratch)
        return _Joined(self.first.bind(in_refs[:k], out_refs[:k], scratch[:s]),
                       self.second.bind(in_refs[k:], out_refs[k:], scratch[s:]))


def _adamw(w, g, m, v):
    m = ADAM_B1 * m + (1.0 - ADAM_B1) * g
    v = ADAM_B2 * v + (1.0 - ADAM_B2) * (g * g)
    m_hat = m / (1.0 - ADAM_B1 ** ADAM_STEP)
    v_hat = v / (1.0 - ADAM_B2 ** ADAM_STEP)
    delta = -ADAM_LR * (m_hat / (jnp.sqrt(v_hat) + ADAM_EPS) + ADAM_WD * w)
    return delta, m, v


def _sum_parts(parts_ref, index=()):
    g = parts_ref[(0,) + index].astype(F32)
    for s in range(1, parts_ref.shape[0]):
        g = g + parts_ref[(s,) + index].astype(F32)
    return g


def _adamw_call(name, parts, w, m, v, block_rows):
    rows, cols = w.shape
    nb = rows // block_rows
    senders = parts.shape[0]

    def body(parts_ref, w_ref, m_ref, v_ref, g_ref, delta_ref, m_out, v_out):
        g = _sum_parts(parts_ref)
        delta, m_new, v_new = _adamw(w_ref[...], g, m_ref[...], v_ref[...])
        g_ref[...] = g
        delta_ref[...] = delta
        m_out[...] = m_new
        v_out[...] = v_new

    blk = pl.BlockSpec((block_rows, cols), lambda i: (i, 0))
    return pl.pallas_call(
        body, name=name, grid=(nb,),
        in_specs=[pl.BlockSpec((senders, block_rows, cols), lambda i: (0, i, 0)), blk, blk, blk],
        out_specs=[blk, blk, blk, blk],
        out_shape=[jax.ShapeDtypeStruct((rows, cols), F32)] * 4,
        compiler_params=_params(("parallel",)),
    )(parts, w, m, v)


def _adamw_slabs_call(name, parts, w, m, v, rider=None):
    def main(parts_ref, w_ref, m_ref, v_ref, g_ref, delta_ref, m_out, v_out):
        g = _sum_parts(parts_ref)
        delta, m_new, v_new = _adamw(w_ref[...], g, m_ref[...], v_ref[...])
        g_ref[...] = g
        delta_ref[...] = delta
        m_out[...] = m_new
        v_out[...] = v_new

    def body(*refs):
        own, comm = _split_refs(refs, 4, 4, 0, rider)
        if comm is not None:
            comm.start()
        main(*own)
        if comm is not None:
            comm.forward()
            comm.finish()

    vmem = pl.BlockSpec(memory_space=pltpu.VMEM)
    return pl.pallas_call(
        body, name=name, in_specs=_extend([vmem] * 4, rider, "in_specs"),
        out_specs=_extend([vmem] * 4, rider, "out_specs"),
        out_shape=_extend([jax.ShapeDtypeStruct(w.shape, F32)] * 4, rider, "out_shape"),
        scratch_shapes=_extend([], rider, "scratch"),
        compiler_params=pltpu.CompilerParams(vmem_limit_bytes=VMEM_LIMIT),
    )(parts, w, m, v, *_extend([], rider, "arrays"))


WIDE_ROWS = 8
NARROW_ROWS = 40
NARROW_GKW_ROW = 8
NARROW_GKB_ROW = 24
NARROW_HW_ROW = 32
GROUP_SHARD = POOL_GROUP_DIM // N_DEV
KEY_SHARD = GLA_KEY_WIDTH // N_DEV
HEAD_V_SHARD = GLA_HEAD_V // N_DEV


def _small_adamw_call(wide, narrow, w, m, v):
    names = ("norm_w", "pool_scale", "final_norm_w", "pool_group_b", "gla_gk_w", "gla_gk_b", "gla_head_norm_w")
    where = {
        "norm_w": (0, slice(0, 2), slice(None)),
        "pool_scale": (0, slice(2, 3), slice(None)),
        "final_norm_w": (0, slice(3, 4), slice(None)),
        "pool_group_b": (1, slice(0, POOL_GROUPS), slice(0, GROUP_SHARD)),
        "gla_gk_w": (1, slice(NARROW_GKW_ROW, NARROW_GKW_ROW + GLA_GATE_RANK), slice(0, KEY_SHARD)),
        "gla_gk_b": (1, slice(NARROW_GKB_ROW, NARROW_GKB_ROW + 1), slice(0, KEY_SHARD)),
        "gla_head_norm_w": (1, slice(NARROW_HW_ROW, NARROW_HW_ROW + 1), slice(0, HEAD_V_SHARD)),
    }
    k = len(names)

    def body(*refs):
        parts = refs[0:2]
        w_refs, m_refs, v_refs = refs[2:2 + k], refs[2 + k:2 + 2 * k], refs[2 + 2 * k:2 + 3 * k]
        outs = refs[2 + 3 * k:]
        loss_ref = outs[0]
        loss_ref[...] = _sum_parts(parts[0], (slice(4, 5), slice(0, 1)))
        for i, name in enumerate(names):
            buf, rows, cols = where[name]
            g = _sum_parts(parts[buf], (rows, cols))
            delta, m_new, v_new = _adamw(w_refs[i][...], g, m_refs[i][...], v_refs[i][...])
            outs[1 + i][...] = g
            outs[1 + k + i][...] = delta
            outs[1 + 2 * k + i][...] = m_new
            outs[1 + 3 * k + i][...] = v_new

    vmem = pl.BlockSpec(memory_space=pltpu.VMEM)
    shapes = [jax.ShapeDtypeStruct(w[n].shape, F32) for n in names]
    res = pl.pallas_call(
        body, name="adamw_small", in_specs=[vmem] * (2 + 3 * k), out_specs=[vmem] * (1 + 4 * k),
        out_shape=[jax.ShapeDtypeStruct((1, 1), F32)] + shapes * 4,
    )(wide, narrow, *[w[n] for n in names], *[m[n] for n in names], *[v[n] for n in names])
    unzip = lambda j: dict(zip(names, res[1 + j * k:1 + (j + 1) * k]))
    return res[0], unzip(0), unzip(1), unzip(2), unzip(3)


def kernel(x, norm_w, pool_in_w, pool_group_w, pool_group_b, pool_scale, pool_out_w, gla_in_w, gla_gk_w, gla_gk_b, gla_head_norm_w, gla_out_w, final_norm_w, loss_target, m_norm_w, m_pool_in_w, m_pool_group_w, m_pool_group_b, m_pool_scale, m_pool_out_w, m_gla_in_w, m_gla_gk_w, m_gla_gk_b, m_gla_head_norm_w, m_gla_out_w, m_final_norm_w, v_norm_w, v_pool_in_w, v_pool_group_w, v_pool_group_b, v_pool_scale, v_pool_out_w, v_gla_in_w, v_gla_gk_w, v_gla_gk_b, v_gla_head_norm_w, v_gla_out_w, v_final_norm_w):
    w = dict(norm_w=norm_w, pool_in_w=pool_in_w, pool_group_w=pool_group_w, pool_group_b=pool_group_b,
             pool_scale=pool_scale, pool_out_w=pool_out_w, gla_in_w=gla_in_w, gla_gk_w=gla_gk_w, gla_gk_b=gla_gk_b,
             gla_head_norm_w=gla_head_norm_w, gla_out_w=gla_out_w, final_norm_w=final_norm_w)
    m = dict(norm_w=m_norm_w, pool_in_w=m_pool_in_w, pool_group_w=m_pool_group_w, pool_group_b=m_pool_group_b,
             pool_scale=m_pool_scale, pool_out_w=m_pool_out_w, gla_in_w=m_gla_in_w, gla_gk_w=m_gla_gk_w,
             gla_gk_b=m_gla_gk_b, gla_head_norm_w=m_gla_head_norm_w, gla_out_w=m_gla_out_w,
             final_norm_w=m_final_norm_w)
    v = dict(norm_w=v_norm_w, pool_in_w=v_pool_in_w, pool_group_w=v_pool_group_w, pool_group_b=v_pool_group_b,
             pool_scale=v_pool_scale, pool_out_w=v_pool_out_w, gla_in_w=v_gla_in_w, gla_gk_w=v_gla_gk_w,
             gla_gk_b=v_gla_gk_b, gla_head_norm_w=v_gla_head_norm_w, gla_out_w=v_gla_out_w,
             final_norm_w=v_final_norm_w)
    col_shard = GLA_IN_WIDTH // N_DEV
    row_shard = D_MODEL // N_DEV

    def lanes(a):
        return jnp.pad(a, [(0, 0)] * (a.ndim - 1) + [(0, LANES - a.shape[-1])])

    small_in = jnp.concatenate([lanes(pool_group_b[0]), lanes(gla_gk_b), lanes(gla_head_norm_w),
                                jnp.zeros((2, LANES), F32)], axis=0)
    in_cols = 2 * POOL_WIDTH // N_DEV
    pool_in, pool_gw, pool_out, small_all = _comm_call("pool_weights_all_gather", _gather_rider(
        [pool_in_w[0].astype(BF16), pool_group_w[0].astype(BF16), pool_out_w[0].astype(BF16), small_in],
        [(D_MODEL, 2 * POOL_WIDTH), (POOL_GROUPS, POOL_GROUP_DIM, POOL_GROUP_DIM), (POOL_WIDTH, D_MODEL),
         (N_DEV, 8, LANES)],
        [_dim1_slot(in_cols), _dim1_slot(GROUP_SHARD), _row_slot(row_shard), _lead_slot]))
    pool_gb = jnp.transpose(small_all[:, 0:POOL_GROUPS, :GROUP_SHARD], (1, 0, 2)).reshape(1, POOL_WIDTH)
    gla_gkb = small_all[:, POOL_GROUPS, :KEY_SHARD].reshape(1, GLA_KEY_WIDTH)
    gla_hw = jnp.tile(small_all[:, POOL_GROUPS + 1, :HEAD_V_SHARD].reshape(1, GLA_HEAD_V), (1, GLA_HEADS))
    nw0, nw1, wf = norm_w[0:1], norm_w[1:2], final_norm_w.reshape(1, D_MODEL)
    xs, target = x[0], loss_target[0]

    h1, p, gla_in_parts, gkw_parts, gla_out = _pool_fwd_call(
        xs, nw0, pool_in, pool_gw, pool_gb, pool_scale, pool_out, _gather_rider(
            [jnp.transpose(gla_in_w[0]).astype(BF16), gla_gk_w[0].astype(BF16), gla_out_w[0].astype(BF16)],
            [(N_DEV, col_shard, D_MODEL), (N_DEV, GLA_GATE_RANK, KEY_SHARD), (GLA_VALUE_WIDTH, D_MODEL)],
            [_lead_slot, _lead_slot, _row_slot(row_shard)]))
    gla_in = gla_in_parts.reshape(GLA_IN_WIDTH, D_MODEL)
    gla_gkw = jnp.pad(jnp.transpose(gkw_parts, (1, 0, 2)).reshape(GLA_GATE_RANK, GLA_KEY_WIDTH),
                      ((0, GLA_LOW_PAD - GLA_GATE_RANK), (0, 0)))
    dh2, proj, o, states, loss_part, dwf = _gla_fwd_call(h1, nw1, gla_in, gla_gkw, gla_gkb, gla_hw, gla_out, wf, target)

    dproj, d_gla_out, dhw, dgkw, dgkb = _gla_bwd_call(dh2, proj, o, states, gla_gkw, gla_gkb, gla_hw, gla_out)
    dh1, d_gla_in, dnw1, landed_gla_out = _inproj_bwd_call(
        "gla_in_bwd", dproj, h1, nw1, gla_in, dh2,
        _exchange_rider([d_gla_out], [(row_shard, D_MODEL)], [_row_slot(row_shard)]), transposed=True)
    slabs = col_shard * D_MODEL // (BF16_ROWS * LANES)
    gla_in_send = d_gla_in.reshape(N_DEV, slabs, BF16_ROWS, LANES)
    dp, d_pool_out, dgw, dgb, dsc, landed_gla_in = _pool_bwd_call(
        dh1, p, pool_gw, pool_gb, pool_scale, pool_out,
        _two_level_rider([gla_in_send], [(slabs, BF16_ROWS, LANES)], [_lead_slot], TWO_LEVEL_ADD_STEP))
    grad_x, d_pool_in, dnw0, landed_pool_out, landed_gw = _inproj_bwd_call(
        "pool_in_bwd", dp, xs, nw0, pool_in, dh1,
        _two_level_rider([d_pool_out, dgw], [(row_shard, D_MODEL), (POOL_GROUPS, GROUP_SHARD, POOL_GROUP_DIM)],
                         [_row_slot(row_shard), _dim1_slot(GROUP_SHARD)], TWO_LEVEL_ADD_STEP))

    wide = jnp.concatenate([
        dnw0, dnw1, dsc, dwf, jnp.pad(loss_part[0:1, 0:1], ((0, 0), (0, D_MODEL - 1))),
        jnp.zeros((WIDE_ROWS - 5, D_MODEL), F32)], axis=0)

    def rows8(a):
        return jnp.pad(lanes(a), ((0, 0), (0, -a.shape[1] % 8), (0, 0)))

    narrow = jnp.concatenate([
        rows8(jnp.transpose(dgb.reshape(POOL_GROUPS, N_DEV, GROUP_SHARD), (1, 0, 2))),
        rows8(jnp.transpose(dgkw[:GLA_GATE_RANK].reshape(GLA_GATE_RANK, N_DEV, KEY_SHARD), (1, 0, 2))),
        rows8(dgkb.reshape(N_DEV, 1, KEY_SHARD)),
        rows8(dhw.reshape(GLA_HEADS, GLA_HEAD_V).sum(axis=0).reshape(N_DEV, 1, HEAD_V_SHARD)),
    ], axis=1)
    last_exchange = _JoinedRider(
        _two_level_rider([d_pool_in], [(D_MODEL, in_cols)], [_dim1_slot(in_cols)]),
        _exchange_rider([wide, narrow], [(WIDE_ROWS, D_MODEL), (NARROW_ROWS, LANES)],
                        [lambda ref, d: ref, _lead_slot]))

    res = {}
    as_slabs = lambda t: jnp.transpose(t[0]).reshape(slabs, BF16_ROWS, LANES)
    *outs, landed_pool_in, landed_wide, landed_narrow = _adamw_slabs_call(
        "adamw_gla_in_w", landed_gla_in, as_slabs(gla_in_w), as_slabs(m_gla_in_w), as_slabs(v_gla_in_w),
        last_exchange)
    res["gla_in_w"] = [jnp.transpose(t.reshape(col_shard, D_MODEL))[None] for t in outs]
    for name, parts, rows, cols, block in [
            ("pool_in_w", landed_pool_in, D_MODEL, in_cols, 256),
            ("pool_group_w", landed_gw, POOL_GROUPS * GROUP_SHARD, POOL_GROUP_DIM, 128),
            ("pool_out_w", landed_pool_out, row_shard, D_MODEL, 128),
            ("gla_out_w", landed_gla_out, row_shard, D_MODEL, 128)]:
        outs = _adamw_call("adamw_" + name, parts.reshape(parts.shape[0], rows, cols), w[name].reshape(rows, cols),
                           m[name].reshape(rows, cols), v[name].reshape(rows, cols), block)
        res[name] = [t.reshape(w[name].shape) for t in outs]
    small_shapes ={"norm_w": (2, D_MODEL), "pool_scale": (1, D_MODEL), "final_norm_w": (1, D_MODEL),
                    "pool_group_b": (POOL_GROUPS, GROUP_SHARD), "gla_gk_w": (GLA_GATE_RANK, KEY_SHARD),
                    "gla_gk_b": (1, KEY_SHARD), "gla_head_norm_w": (1, HEAD_V_SHARD)}
    as_small = lambda t: {n: t[n].reshape(s) for n, s in small_shapes.items()}
    loss, *small_outs = _small_adamw_call(landed_wide, landed_narrow, as_small(w), as_small(m), as_small(v))
    for name in small_shapes:
        res[name] = [t[name].reshape(w[name].shape) for t in small_outs]
    order = ("norm_w", "pool_in_w", "pool_group_w", "pool_group_b", "pool_scale", "pool_out_w", "gla_in_w",
             "gla_gk_w", "gla_gk_b", "gla_head_norm_w", "gla_out_w", "final_norm_w")
    return (loss.reshape(()), grad_x[None], *[res[n][0] for n in order], *[res[n][1] for n in order],
            *[res[n][2] for n in order], *[res[n][3] for n in order])
```

```python
import jax
import jax.numpy as jnp
from jax import lax
from jax.experimental import pallas as pl
from jax.experimental.pallas import tpu as pltpu

F32 = jnp.float32
BF16 = jnp.bfloat16
MESH = pl.DeviceIdType.MESH

N_DEV = 8
D_MODEL = 1024
POOL_WIDTH = 1024
POOL_GROUPS = 4
POOL_GROUP_DIM = 256
POOL_HALO = 16
GLA_HEADS = 4
GLA_HEAD_K = 128
GLA_HEAD_V = 256
GLA_KEY_WIDTH = 512
GLA_VALUE_WIDTH = 1024
GLA_GATE_RANK = 16
GLA_IN_WIDTH = 3088
GLA_IN_PAD = 3200
GLA_LOW_PAD = 128
GLA_QKVG_WIDTH = 3072
CHUNK = 64
GATE_NORMALIZER = 16.0
RMS_EPS = 1e-6
Q_SCALE = GLA_HEAD_K ** -0.5

ADAM_LR = 0.001
ADAM_B1 = 0.9
ADAM_B2 = 0.999
ADAM_EPS = 1e-08
ADAM_WD = 0.01
ADAM_STEP = 10

LANES = 128
BF16_ROWS = 16
VMEM_LIMIT = 56 * 1024 * 1024
ROW_TILE = 256
MATMUL_ROW_TILE = 512
TWO_LEVEL_ADD_STEP = 1


def _dot_nn(a, b):
    return lax.dot_general(a, b, (((1,), (0,)), ((), ())), preferred_element_type=F32)


def _dot_nt(a, b):
    return lax.dot_general(a, b, (((1,), (1,)), ((), ())), preferred_element_type=F32)


def _dot_tn(a, b):
    return lax.dot_general(a, b, (((0,), (0,)), ((), ())), preferred_element_type=F32)


def _rms(x):
    rstd = lax.rsqrt(jnp.mean(x * x, axis=-1, keepdims=True) + RMS_EPS)
    return x * rstd, rstd


def _rms_bwd(dxhat, xhat, rstd):
    return rstd * (dxhat - xhat * jnp.mean(dxhat * xhat, axis=-1, keepdims=True))


def _sigmoid(x):
    return 1.0 / (1.0 + jnp.exp(-x))


def _params(sem=("arbitrary",)):
    return pltpu.CompilerParams(dimension_semantics=sem, vmem_limit_bytes=VMEM_LIMIT)


def _full(shape):
    return pl.BlockSpec(shape, lambda i: (0,) * len(shape))


def _const(shape):
    return pl.BlockSpec(shape, lambda i: (0,) * len(shape), pipeline_mode=pl.Buffered(1))


def _window_sums(ext, forward):
    n = ext.shape[0]
    outs = []
    for g in range(POOL_GROUPS):
        s = ext[:, g * POOL_GROUP_DIM:(g + 1) * POOL_GROUP_DIM]
        for k in range(g + 1):
            shift = (1 << k) if forward else n - (1 << k)
            s = s + pltpu.roll(s, shift, axis=0)
        outs.append(s[:n - POOL_HALO])
    return outs


def _inv_count(row0, tm):
    row = row0 + lax.broadcasted_iota(jnp.int32, (tm, 1), 0)
    return [1.0 / jnp.minimum(row + 1, 2 << g).astype(F32) for g in range(POOL_GROUPS)]


def _pool_mix(u, u_prev, row0, gw_ref, gb):
    tm = u.shape[0]
    sums = _window_sums(jnp.concatenate([u, u_prev], axis=0), True)
    inv = _inv_count(row0, tm)
    pooled, mixed = [], []
    for g in range(POOL_GROUPS):
        ug = u[:, g * POOL_GROUP_DIM:(g + 1) * POOL_GROUP_DIM]
        pg = (sums[g] * inv[g] - ug).astype(BF16)
        pooled.append(pg)
        mixed.append(_dot_nn(pg, gw_ref[g]))
    return pooled, jnp.concatenate(mixed, axis=1) + gb


def _pool_fwd_call(x, nw, w_in, gw, gb, sc, w_out, rider=None):
    seq = x.shape[0]
    tm = min(MATMUL_ROW_TILE, seq)
    nt = seq // tm

    def main(x_ref, nw_ref, win_ref, gw_ref, gb_ref, sc_ref, wout_ref, h_ref, p_ref, halo_ref):
        i = pl.program_id(0)

        @pl.when(i == 0)
        def _():
            halo_ref[...] = jnp.zeros_like(halo_ref)

        xt = x_ref[...]
        xhat, _ = _rms(xt)
        n = (xhat * nw_ref[...]).astype(BF16)
        p = _dot_nn(n, win_ref[...])
        p_ref[...] = p
        u = p[:, :POOL_WIDTH]
        gate = p[:, POOL_WIDTH:]
        _, mixed = _pool_mix(u, halo_ref[...], i * tm, gw_ref, gb_ref[...])
        halo_ref[...] = u[tm - POOL_HALO:, :]
        y = (mixed * sc_ref[...] * (gate * _sigmoid(gate))).astype(BF16)
        h_ref[...] = xt + _dot_nn(y, wout_ref[...])

    def body(*refs):
        own, comm = _split_refs(refs, 7, 2, 1, rider)
        _ride_before(comm, pl.program_id(0), nt)
        main(*own)
        _ride_after(comm, pl.program_id(0), nt)

    return pl.pallas_call(
        body, name="pool_fwd", grid=(nt,),
        in_specs=_extend([pl.BlockSpec((tm, D_MODEL), lambda i: (i, 0)), _const((1, D_MODEL)),
                          _const((D_MODEL, 2 * POOL_WIDTH)), _const((POOL_GROUPS, POOL_GROUP_DIM, POOL_GROUP_DIM)),
                          _const((1, POOL_WIDTH)), _const((1, POOL_WIDTH)), _const((POOL_WIDTH, D_MODEL))],
                         rider, "in_specs"),
        out_specs=_extend([pl.BlockSpec((tm, D_MODEL), lambda i: (i, 0)),
                           pl.BlockSpec((tm, 2 * POOL_WIDTH), lambda i: (i, 0))], rider, "out_specs"),
        out_shape=_extend([jax.ShapeDtypeStruct((seq, D_MODEL), F32),
                           jax.ShapeDtypeStruct((seq, 2 * POOL_WIDTH), F32)], rider, "out_shape"),
        scratch_shapes=_extend([pltpu.VMEM((POOL_HALO, POOL_WIDTH), F32)], rider, "scratch"),
        compiler_params=_params(),
    )(x, nw, w_in, gw, gb, sc, w_out, *_extend([], rider, "arrays"))


def _pool_bwd_call(dh, p, gw, gb, sc, w_out, rider=None):
    seq = dh.shape[0]
    tm = min(MATMUL_ROW_TILE, seq)
    nt = seq // tm
    halo_blocks = tm // POOL_HALO

    def main(dh_ref, p_ref, pprev_ref, gw_ref, gb_ref, sc_ref, wout_ref,
             dp_ref, dwout_hbm, dgw_hbm, dgb_ref, dsc_ref, carry_ref, dwout_acc, dgw_acc, dwout_stage, dgw_stage):
        i = pl.program_id(0)
        t = nt - 1 - i

        @pl.when(i == 0)
        def _():
            carry_ref[...] = jnp.zeros_like(carry_ref)
            dwout_acc[...] = jnp.zeros_like(dwout_acc)
            dgw_acc[...] = jnp.zeros_like(dgw_acc)
            dgb_ref[...] = jnp.zeros_like(dgb_ref)
            dsc_ref[...] = jnp.zeros_like(dsc_ref)

        dhb = dh_ref[...].astype(BF16)
        dy = _dot_nt(dhb, wout_ref[...])
        p = p_ref[...]
        u = p[:, :POOL_WIDTH]
        gate = p[:, POOL_WIDTH:]
        u_prev = jnp.where(t > 0, pprev_ref[:, :POOL_WIDTH], 0.0)
        pooled, mixed = _pool_mix(u, u_prev, t * tm, gw_ref, gb_ref[...])
        sg = _sigmoid(gate)
        silu = gate * sg
        sc = sc_ref[...]
        y = (mixed * sc * silu).astype(BF16)
        dwout_acc[...] += _dot_tn(y, dhb)
        dmixed = dy * sc * silu
        dsc_ref[...] += jnp.sum(dy * mixed * silu, axis=0, keepdims=True)
        dgate = dy * mixed * sc * (sg * (1.0 + gate * (1.0 - sg)))
        dgb_ref[...] += jnp.sum(dmixed, axis=0, keepdims=True)
        inv = _inv_count(t * tm, tm)
        dpooled, scaled = [], []
        for g in range(POOL_GROUPS):
            dmg = dmixed[:, g * POOL_GROUP_DIM:(g + 1) * POOL_GROUP_DIM].astype(BF16)
            dgw_acc[g] += _dot_tn(pooled[g], dmg)
            dpg = _dot_nt(dmg, gw_ref[g])
            dpooled.append(dpg)
            scaled.append(dpg * inv[g])
        r = jnp.concatenate(scaled, axis=1)
        sums = _window_sums(jnp.concatenate([r, carry_ref[...]], axis=0), False)
        carry_ref[...] = r[:POOL_HALO, :]
        du = jnp.concatenate([sums[g] - dpooled[g] for g in range(POOL_GROUPS)], axis=1)
        dp_ref[...] = jnp.concatenate([du, dgate], axis=1).astype(BF16)

        @pl.when(i == nt - 1)
        def _():
            dwout_stage[...] = dwout_acc[...].astype(BF16)
            dgw_stage[...] = dgw_acc[...].astype(BF16)
            pltpu.sync_copy(dwout_stage, dwout_hbm)
            pltpu.sync_copy(dgw_stage, dgw_hbm)

    def body(*refs):
        own, comm = _split_refs(refs, 7, 5, 5, rider)
        _ride_before(comm, pl.program_id(0), nt)
        main(*own)
        _ride_after(comm, pl.program_id(0), nt)

    rev = lambda i: (nt - 1 - i, 0)
    return pl.pallas_call(
        body, name="pool_bwd", grid=(nt,),
        in_specs=_extend([pl.BlockSpec((tm, D_MODEL), rev), pl.BlockSpec((tm, 2 * POOL_WIDTH), rev),
                          pl.BlockSpec((POOL_HALO, 2 * POOL_WIDTH),
                                       lambda i: (jnp.maximum((nt - 1 - i) * halo_blocks - 1, 0), 0)),
                          _const((POOL_GROUPS, POOL_GROUP_DIM, POOL_GROUP_DIM)), _const((1, POOL_WIDTH)),
                          _const((1, POOL_WIDTH)), _const((POOL_WIDTH, D_MODEL))], rider, "in_specs"),
        out_specs=_extend([pl.BlockSpec((tm, 2 * POOL_WIDTH), rev), pl.BlockSpec(memory_space=pl.ANY),
                           pl.BlockSpec(memory_space=pl.ANY), _full((1, POOL_WIDTH)), _full((1, POOL_WIDTH))],
                          rider, "out_specs"),
        out_shape=_extend([jax.ShapeDtypeStruct((seq, 2 * POOL_WIDTH), BF16),
                           jax.ShapeDtypeStruct((POOL_WIDTH, D_MODEL), BF16),
                           jax.ShapeDtypeStruct((POOL_GROUPS, POOL_GROUP_DIM, POOL_GROUP_DIM), BF16),
                           jax.ShapeDtypeStruct((1, POOL_WIDTH), F32), jax.ShapeDtypeStruct((1, POOL_WIDTH), F32)],
                          rider, "out_shape"),
        scratch_shapes=_extend([pltpu.VMEM((POOL_HALO, POOL_WIDTH), F32), pltpu.VMEM((POOL_WIDTH, D_MODEL), F32),
                                pltpu.VMEM((POOL_GROUPS, POOL_GROUP_DIM, POOL_GROUP_DIM), F32),
                                pltpu.VMEM((POOL_WIDTH, D_MODEL), BF16),
                                pltpu.VMEM((POOL_GROUPS, POOL_GROUP_DIM, POOL_GROUP_DIM), BF16)], rider, "scratch"),
        compiler_params=_params(),
    )(dh, p, p, gw, gb, sc, w_out, *_extend([], rider, "arrays"))


def _rows_then_zeros(ref, lo, hi, rows):
    part = ref[lo:hi, :]
    return jnp.concatenate([part, jnp.zeros((rows - (hi - lo), part.shape[1]), part.dtype)], axis=0)


def _inproj_bwd_call(name, dproj, h_in, nw, w_in, dres, rider=None, transposed=False):
    seq = h_in.shape[0]
    width = dproj.shape[1]
    w_shape = tuple(w_in.shape)
    acc_shape = (width, D_MODEL) if transposed else w_shape
    whole = w_shape[0] // LANES * LANES
    tm = min(MATMUL_ROW_TILE, seq)
    nt = seq // tm

    def main(dproj_ref, h_ref, nw_ref, win_ref, dres_ref, dh_ref, dw_hbm, dnw_ref, dw_acc, dw_stage):
        i = pl.program_id(0)

        @pl.when(i == 0)
        def _():
            dw_acc[...] = jnp.zeros_like(dw_acc)
            dnw_ref[...] = jnp.zeros_like(dnw_ref)

        dpb = dproj_ref[...]
        if transposed:
            dn = _dot_nn(dpb[:, :whole], win_ref[0:whole, :])
            if whole < w_shape[0]:
                dn = dn + _dot_nn(dpb[:, whole:], _rows_then_zeros(win_ref, whole, w_shape[0], width - whole))
        else:
            dn = _dot_nt(dpb, win_ref[...])
        xhat, rstd = _rms(h_ref[...])
        nw_row = nw_ref[...]
        n = (xhat * nw_row).astype(BF16)
        dw_acc[...] += _dot_tn(dpb, n) if transposed else _dot_tn(n, dpb)
        dnw_ref[...] += jnp.sum(dn * xhat, axis=0, keepdims=True)
        dh_ref[...] = _rms_bwd(dn * nw_row, xhat, rstd) + dres_ref[...]

        @pl.when(i == nt - 1)
        def _():
            dw_stage[...] = dw_acc[...].astype(BF16)
            pltpu.sync_copy(dw_stage.at[pl.ds(0, w_shape[0])], dw_hbm)

    def body(*refs):
        own, comm = _split_refs(refs, 5, 3, 2, rider)
        _ride_before(comm, pl.program_id(0), nt)
        main(*own)
        _ride_after(comm, pl.program_id(0), nt)

    row = lambda i: (i, 0)
    return pl.pallas_call(
        body, name=name, grid=(nt,),
        in_specs=_extend([pl.BlockSpec((tm, width), row), pl.BlockSpec((tm, D_MODEL), row), _const((1, D_MODEL)),
                          _const(w_shape), pl.BlockSpec((tm, D_MODEL), row)], rider, "in_specs"),
        out_specs=_extend([pl.BlockSpec((tm, D_MODEL), row), pl.BlockSpec(memory_space=pl.ANY),
                           _full((1, D_MODEL))], rider, "out_specs"),
        out_shape=_extend([jax.ShapeDtypeStruct((seq, D_MODEL), F32), jax.ShapeDtypeStruct(w_shape, BF16),
                           jax.ShapeDtypeStruct((1, D_MODEL), F32)], rider, "out_shape"),
        scratch_shapes=_extend([pltpu.VMEM(acc_shape, F32), pltpu.VMEM(acc_shape, BF16)], rider, "scratch"),
        compiler_params=_params(),
    )(dproj, h_in, nw, w_in, dres, *_extend([], rider, "arrays"))


def _chunk_scan(x, reverse):
    n = x.shape[0]
    pos = lax.broadcasted_iota(jnp.int32, (n, 1), 0) & (CHUNK - 1)
    k = 1
    while k < CHUNK:
        if reverse:
            x = x + jnp.where(pos < CHUNK - k, pltpu.roll(x, n - k, axis=0), 0.0)
        else:
            x = x + jnp.where(pos >= k, pltpu.roll(x, k, axis=0), 0.0)
        k *= 2
    return x


def _chunk_rows(j):
    return slice(j * CHUNK, (j + 1) * CHUNK)


def _kcols(h):
    return slice(h * GLA_HEAD_K, (h + 1) * GLA_HEAD_K)


def _vcols(h):
    return slice(h * GLA_HEAD_V, (h + 1) * GLA_HEAD_V)


def _chunk_masks(tm):
    idx_t = lax.broadcasted_iota(jnp.int32, (tm, tm), 0)
    idx_s = lax.broadcasted_iota(jnp.int32, (tm, tm), 1)
    same_chunk = (idx_t ^ idx_s) < CHUNK
    return same_chunk & (idx_t >= idx_s), same_chunk & (idx_t < idx_s)


class _GlaTerms:
    def __init__(self, kc, q, k, v, low_b, gkw_ref, gkb_ref, masks):
        tm = q.shape[0]
        self.q = q * Q_SCALE
        self.k = k
        self.z = _dot_nn(low_b, gkw_ref[:, kc]) + gkb_ref[:, kc]
        log_g = (jnp.minimum(self.z, 0.0) - jnp.log(1.0 + jnp.exp(-jnp.abs(self.z)))) / GATE_NORMALIZER
        self.c = _chunk_scan(log_g, False)
        is_last = lax.broadcasted_iota(jnp.int32, (CHUNK, 1), 0) == CHUNK - 1
        self.c_last = [jnp.sum(jnp.where(is_last, self.c[_chunk_rows(j), :], 0.0), axis=0, keepdims=True)
                       for j in range(tm // CHUNK)]
        c_last_rows = jnp.concatenate([jnp.broadcast_to(r, (CHUNK, r.shape[1])) for r in self.c_last], axis=0)
        self.e_pos = jnp.exp(self.c)
        self.e_neg = jnp.exp(-self.c)
        self.e_rest = jnp.exp(c_last_rows - self.c)
        self.a_b = (self.q * self.e_pos).astype(BF16)
        self.b_b = (self.k * self.e_neg).astype(BF16)
        self.cn_b = (self.q * self.e_neg).astype(BF16)
        self.dp_b = (self.k * self.e_pos).astype(BF16)
        self.kd_b = (self.k * self.e_rest).astype(BF16)
        self.v_b = v.astype(BF16)
        self.lower, self.upper = masks

    def scores(self, kc=slice(None)):
        fwd = _dot_nt(self.a_b[:, kc], self.b_b[:, kc])
        bwd = _dot_nt(self.cn_b[:, kc], self.dp_b[:, kc])
        return jnp.where(self.lower, fwd, jnp.where(self.upper, bwd, 0.0)).astype(BF16)


def _gla_fwd_call(h1, nw, w_in, gkw, gkb, hw, w_out, wf, target):
    seq = h1.shape[0]
    tm = ROW_TILE
    nt = seq // tm
    cpt = tm // CHUNK
    n_chunks = seq // CHUNK

    def body(h_ref, nw_ref, win_ref, gkw_ref, gkb_ref, hw_ref, wout_ref, wf_ref, tgt_ref,
             dh2_ref, proj_ref, o_ref, st_ref, loss_ref, dwf_ref, state_ref):
        i = pl.program_id(0)

        @pl.when(i == 0)
        def _():
            state_ref[...] = jnp.zeros_like(state_ref)
            loss_ref[...] = jnp.zeros_like(loss_ref)
            dwf_ref[...] = jnp.zeros_like(dwf_ref)

        ht = h_ref[...]
        xhat, _ = _rms(ht)
        n = (xhat * nw_ref[...]).astype(BF16)
        sections = {}
        for name, lo, hi in (("low", GLA_QKVG_WIDTH, GLA_IN_PAD), ("qk", 0, 2 * GLA_KEY_WIDTH),
                             ("v", 2 * GLA_KEY_WIDTH, GLA_QKVG_WIDTH - GLA_VALUE_WIDTH),
                             ("gate", GLA_QKVG_WIDTH - GLA_VALUE_WIDTH, GLA_QKVG_WIDTH)):
            rows = (win_ref[lo:hi, :] if hi <= GLA_IN_WIDTH
                    else _rows_then_zeros(win_ref, lo, GLA_IN_WIDTH, hi - lo))
            sections[name] = _dot_nt(n, rows)
            proj_ref[:, lo:hi] = sections[name]
        low_b = sections["low"].astype(BF16)
        masks = _chunk_masks(tm)
        on_heads = []
        for h in range(GLA_HEADS):
            kc, vc = _kcols(h), _vcols(h)
            g = _GlaTerms(kc, sections["qk"][:, kc], sections["qk"][:, GLA_KEY_WIDTH:][:, kc], sections["v"][:, vc],
                          low_b, gkw_ref, gkb_ref, masks)
            srows = slice(h * GLA_HEAD_V, (h + 1) * GLA_HEAD_V)
            o_intra = _dot_nn(g.scores(), g.v_b)
            state = state_ref[srows, :]
            o_rows = []
            for j in range(cpt):
                r = _chunk_rows(j)
                st_ref[j, srows, :] = state
                o_rows.append(o_intra[r] + _dot_nt(g.a_b[r], state.astype(BF16)))
                decay = jnp.exp(g.c_last[j])
                state = state * decay + _dot_tn(g.v_b[r], g.kd_b[r])
            state_ref[srows, :] = state
            o_head = jnp.concatenate(o_rows, axis=0)
            o_ref[:, vc] = o_head
            on_heads.append(_rms(o_head)[0])
        gate = sections["gate"]
        on = jnp.concatenate(on_heads, axis=1) * hw_ref[...]
        y = (on * (gate * _sigmoid(gate))).astype(BF16)
        h2 = ht + _dot_nn(y, wout_ref[...])
        xhat2, rstd2 = _rms(h2)
        wf_row = wf_ref[...]
        err = xhat2 * wf_row - tgt_ref[...]
        loss_ref[...] += 0.5 * jnp.sum(err * err) / D_MODEL
        dout = err * (1.0 / D_MODEL)
        dwf_ref[...] += jnp.sum(dout * xhat2, axis=0, keepdims=True)
        dh2_ref[...] = _rms_bwd(dout * wf_row, xhat2, rstd2)

    row = lambda i: (i, 0)
    return pl.pallas_call(
        body, name="gla_fwd", grid=(nt,),
        in_specs=[pl.BlockSpec((tm, D_MODEL), row), _const((1, D_MODEL)), _const((GLA_IN_WIDTH, D_MODEL)),
                  _const((GLA_LOW_PAD, GLA_KEY_WIDTH)), _const((1, GLA_KEY_WIDTH)), _const((1, GLA_VALUE_WIDTH)),
                  _const((GLA_VALUE_WIDTH, D_MODEL)), _const((1, D_MODEL)), pl.BlockSpec((tm, D_MODEL), row)],
        out_specs=[pl.BlockSpec((tm, D_MODEL), row), pl.BlockSpec((tm, GLA_IN_PAD), row),
                   pl.BlockSpec((tm, GLA_VALUE_WIDTH), row),
                   pl.BlockSpec((cpt, GLA_VALUE_WIDTH, GLA_HEAD_K), lambda i: (i, 0, 0)),
                   _full((8, LANES)), _full((1, D_MODEL))],
        out_shape=[jax.ShapeDtypeStruct((seq, D_MODEL), F32), jax.ShapeDtypeStruct((seq, GLA_IN_PAD), F32),
                   jax.ShapeDtypeStruct((seq, GLA_VALUE_WIDTH), F32),
                   jax.ShapeDtypeStruct((n_chunks, GLA_VALUE_WIDTH, GLA_HEAD_K), F32),
                   jax.ShapeDtypeStruct((8, LANES), F32), jax.ShapeDtypeStruct((1, D_MODEL), F32)],
        scratch_shapes=[pltpu.VMEM((GLA_VALUE_WIDTH, GLA_HEAD_K), F32)],
        compiler_params=_params(),
    )(h1, nw, w_in, gkw, gkb, hw, w_out, wf, target)


def _gla_bwd_call(dh2, proj, o, states, gkw, gkb, hw, w_out):
    seq = dh2.shape[0]
    tm = ROW_TILE
    nt = seq // tm
    cpt = tm // CHUNK

    def body(dh_ref, proj_ref, o_ref, st_ref, gkw_ref, gkb_ref, hw_ref, wout_ref,
             dproj_ref, dwout_hbm, dhw_ref, dgkw_ref, dgkb_ref, dstate_ref, dwout_acc, dwout_stage):
        i = pl.program_id(0)

        @pl.when(i == 0)
        def _():
            dstate_ref[...] = jnp.zeros_like(dstate_ref)
            dwout_acc[...] = jnp.zeros_like(dwout_acc)
            dhw_ref[...] = jnp.zeros_like(dhw_ref)
            dgkw_ref[...] = jnp.zeros_like(dgkw_ref)
            dgkb_ref[...] = jnp.zeros_like(dgkb_ref)

        dhb = dh_ref[...].astype(BF16)
        dy = _dot_nt(dhb, wout_ref[...])
        v0, g0 = 2 * GLA_KEY_WIDTH, GLA_QKVG_WIDTH - GLA_VALUE_WIDTH
        gate = proj_ref[:, g0:GLA_QKVG_WIDTH]
        low_b = proj_ref[:, GLA_QKVG_WIDTH:].astype(BF16)
        o = o_ref[...]
        hw_row = hw_ref[...]
        sg = _sigmoid(gate)
        silu = gate * sg
        don = dy * silu
        on_parts, do_parts, dhw_parts = [], [], []
        for h in range(GLA_HEADS):
            vc = _vcols(h)
            xh, rs = _rms(o[:, vc])
            on_parts.append(xh * hw_row[:, vc])
            dhw_parts.append(jnp.sum(don[:, vc] * xh, axis=0, keepdims=True))
            do_parts.append(_rms_bwd(don[:, vc] * hw_row[:, vc], xh, rs).astype(BF16))
        on = jnp.concatenate(on_parts, axis=1)
        dwout_acc[...] += _dot_tn((on * silu).astype(BF16), dhb)
        dhw_ref[...] += jnp.concatenate(dhw_parts, axis=1)
        dproj_ref[:, g0:GLA_QKVG_WIDTH] = (dy * on * (sg * (1.0 + gate * (1.0 - sg)))).astype(BF16)

        last_row = lax.broadcasted_iota(jnp.int32, (CHUNK, 1), 0) == CHUNK - 1
        g = _GlaTerms(slice(0, GLA_KEY_WIDTH), proj_ref[:, :GLA_KEY_WIDTH], proj_ref[:, GLA_KEY_WIDTH:v0],
                      proj_ref[:, v0:g0], low_b, gkw_ref, gkb_ref, _chunk_masks(tm))
        dc_h = []
        for h in range(GLA_HEADS):
            kc, vc = _kcols(h), _vcols(h)
            k_cols = slice(GLA_KEY_WIDTH + kc.start, GLA_KEY_WIDTH + kc.stop)
            v_cols = slice(v0 + vc.start, v0 + vc.stop)
            do_h = do_parts[h]
            srows = slice(h * GLA_HEAD_V, (h + 1) * GLA_HEAD_V)
            scores = g.scores(kc)
            dscores = _dot_nt(do_h, g.v_b[:, vc])
            dfwd = jnp.where(g.lower, dscores, 0.0).astype(BF16)
            dbwd = jnp.where(g.upper, dscores, 0.0).astype(BF16)
            dv_intra = _dot_tn(scores, do_h)
            da_intra = _dot_nn(dfwd, g.b_b[:, kc])
            db = _dot_tn(dfwd, g.a_b[:, kc])
            dcn = _dot_nn(dbwd, g.dp_b[:, kc])
            ddp = _dot_tn(dbwd, g.cn_b[:, kc])
            dstate = dstate_ref[srows, :]
            da_rows, dkd_rows, dv_rows, dcl_rows = [None] * cpt, [None] * cpt, [None] * cpt, [None] * cpt
            for j in reversed(range(cpt)):
                r = _chunk_rows(j)
                state = st_ref[j, srows, :]
                dstate_b = dstate.astype(BF16)
                do_c = do_h[r]
                dv_rows[j] = dv_intra[r] + _dot_nt(g.kd_b[r, kc], dstate_b)
                da_rows[j] = da_intra[r] + _dot_nn(do_c, state.astype(BF16))
                dkd = _dot_nn(g.v_b[r, vc], dstate_b) * g.e_rest[r, kc]
                dkd_rows[j] = dkd
                decay = jnp.exp(g.c_last[j][:, kc])
                dc_last = (jnp.sum(dkd * g.k[r, kc], axis=0, keepdims=True)
                           + decay * jnp.sum(state * dstate, axis=0, keepdims=True))
                dcl_rows[j] = jnp.where(last_row, dc_last, 0.0)
                dstate = _dot_tn(do_c, g.a_b[r, kc]) + dstate * decay
            dstate_ref[srows, :] = dstate
            da = jnp.concatenate(da_rows, axis=0)
            dkd = jnp.concatenate(dkd_rows, axis=0)
            dproj_ref[:, v_cols] = jnp.concatenate(dv_rows, axis=0).astype(BF16)
            q_up, q_down = da * g.e_pos[:, kc], dcn * g.e_neg[:, kc]
            k_up, k_down = ddp * g.e_pos[:, kc], db * g.e_neg[:, kc] + dkd
            dproj_ref[:, kc] = (Q_SCALE * (q_up + q_down)).astype(BF16)
            dproj_ref[:, k_cols] = (k_up + k_down).astype(BF16)
            dc_h.append(g.q[:, kc] * (q_up - q_down) + g.k[:, kc] * (k_up - k_down)
                        + jnp.concatenate(dcl_rows, axis=0))
        dz = _chunk_scan(jnp.concatenate(dc_h, axis=1), True) * (1.0 / GATE_NORMALIZER) * (1.0 - _sigmoid(g.z))
        dzb = dz.astype(BF16)
        dgkb_ref[...] += jnp.sum(dz, axis=0, keepdims=True)
        dgkw_ref[...] += _dot_tn(low_b, dzb)
        dproj_ref[:, GLA_QKVG_WIDTH:] = _dot_nt(dzb, gkw_ref[...]).astype(BF16)

        @pl.when(i == nt - 1)
        def _():
            dwout_stage[...] = dwout_acc[...].astype(BF16)
            pltpu.sync_copy(dwout_stage, dwout_hbm)

    rev = lambda i: (nt - 1 - i, 0)
    return pl.pallas_call(
        body, name="gla_bwd", grid=(nt,),
        in_specs=[pl.BlockSpec((tm, D_MODEL), rev), pl.BlockSpec((tm, GLA_IN_PAD), rev),
                  pl.BlockSpec((tm, GLA_VALUE_WIDTH), rev),
                  pl.BlockSpec((cpt, GLA_VALUE_WIDTH, GLA_HEAD_K), lambda i: (nt - 1 - i, 0, 0)),
                  _const((GLA_LOW_PAD, GLA_KEY_WIDTH)), _const((1, GLA_KEY_WIDTH)), _const((1, GLA_VALUE_WIDTH)),
                  _const((GLA_VALUE_WIDTH, D_MODEL))],
        out_specs=[pl.BlockSpec((tm, GLA_IN_PAD), rev), pl.BlockSpec(memory_space=pl.ANY),
                   _full((1, GLA_VALUE_WIDTH)), _full((GLA_LOW_PAD, GLA_KEY_WIDTH)), _full((1, GLA_KEY_WIDTH))],
        out_shape=[jax.ShapeDtypeStruct((seq, GLA_IN_PAD), BF16), jax.ShapeDtypeStruct((GLA_VALUE_WIDTH, D_MODEL), BF16),
                   jax.ShapeDtypeStruct((1, GLA_VALUE_WIDTH), F32), jax.ShapeDtypeStruct((GLA_LOW_PAD, GLA_KEY_WIDTH), F32),
                   jax.ShapeDtypeStruct((1, GLA_KEY_WIDTH), F32)],
        scratch_shapes=[pltpu.VMEM((GLA_VALUE_WIDTH, GLA_HEAD_K), F32), pltpu.VMEM((GLA_VALUE_WIDTH, D_MODEL), F32),
                        pltpu.VMEM((GLA_VALUE_WIDTH, D_MODEL), BF16)],
        compiler_params=_params(),
    )(dh2, proj, o, states, gkw, gkb, hw, w_out)


def _position():
    return lax.axis_index("x"), lax.axis_index("y"), lax.axis_index("c")


def _lead_slot(ref, d):
    return ref.at[d]


def _row_slot(rows):
    return lambda ref, d: ref.at[pl.ds(pl.multiple_of(d * rows, rows), rows)]


def _dim1_slot(size):
    return lambda ref, d: ref.at[:, pl.ds(pl.multiple_of(d * size, size), size)]


class _Gather:
    def __init__(self, in_refs, out_refs, slots, send_sems, recv_sems, local_sems):
        self.in_refs, self.out_refs, self.slots = in_refs, out_refs, slots
        self.send_sems, self.recv_sems, self.local_sems = send_sems, recv_sems, local_sems
        self.n = len(in_refs)
        x, y, c = _position()
        self.c = c
        self.me, self.sibling = (x, y, c), (x, y, 1 - c)
        self.chips = [(1 - x, y), (x, 1 - y), (1 - x, 1 - y)]

    def _copy(self, a, k, block, to, from_input=False):
        part = self.slots[a](self.out_refs[a], 4 * block[0] + 2 * block[1] + block[2])
        return pltpu.make_async_remote_copy(
            src_ref=self.in_refs[a] if from_input else part, dst_ref=part,
            send_sem=self.send_sems.at[a, k], recv_sem=self.recv_sems.at[a, k], device_id=to, device_id_type=MESH)

    def _mine(self):
        return [pltpu.make_async_copy(self.in_refs[a], self.slots[a](self.out_refs[a], 4 * self.me[0] + 2 * self.me[1]
                                                                    + self.me[2]), self.local_sems.at[a])
                for a in range(self.n)]

    def _first(self):
        first = [self._copy(a, 0, self.me, self.sibling, True) for a in range(self.n)]
        return first + [self._copy(a, 1 + j, self.me, (*chip, self.c), True)
                        for j, chip in enumerate(self.chips) for a in range(self.n)]

    def _passed(self):
        return [self._copy(a, 4 + j, (*chip, self.c), self.sibling)
                for j, chip in enumerate(self.chips) for a in range(self.n)]

    def start(self):
        for cp in self._mine() + self._first():
            cp.start()

    def forward(self):
        passed = self._passed()
        for j, chip in enumerate(self.chips):
            for a in range(self.n):
                self._copy(a, 1 + j, (*chip, self.c), self.me).wait_recv()
                passed[j * self.n + a].start()

    def finish(self):
        for a in range(self.n):
            self._copy(a, 0, self.sibling, self.me).wait_recv()
        for j, chip in enumerate(self.chips):
            for a in range(self.n):
                self._copy(a, 4 + j, (*chip, 1 - self.c), self.me).wait_recv()
        for cp in self._first() + self._passed():
            cp.wait_send()
        for cp in self._mine():
            cp.wait()


class _Exchange:
    def __init__(self, in_refs, out_refs, slots, send_sems, recv_sems, local_sems):
        self.in_refs, self.out_refs, self.slots = in_refs, out_refs, slots
        self.send_sems, self.recv_sems, self.local_sems = send_sems, recv_sems, local_sems
        self.n = len(in_refs)
        self.pos = _position()

    def _copies(self):
        x, y, c = self.pos
        me = 4 * x + 2 * y + c
        mine = [pltpu.make_async_copy(self.slots[a](self.in_refs[a], me), self.out_refs[a].at[me],
                                      self.local_sems.at[a]) for a in range(self.n)]
        remote = []
        for k in range(1, N_DEV):
            px, py, pc = x ^ (k >> 2), y ^ ((k >> 1) & 1), c ^ (k & 1)
            for a in range(self.n):
                remote.append(pltpu.make_async_remote_copy(
                    src_ref=self.slots[a](self.in_refs[a], 4 * px + 2 * py + pc), dst_ref=self.out_refs[a].at[me],
                    send_sem=self.send_sems.at[a, k - 1], recv_sem=self.recv_sems.at[a, k - 1],
                    device_id=(px, py, pc), device_id_type=MESH))
        return mine, remote

    def start(self):
        mine, remote = self._copies()
        for cp in mine + remote:
            cp.start()

    def forward(self):
        pass

    def finish(self):
        mine, remote = self._copies()
        for cp in remote:
            cp.wait_recv()
        for cp in remote:
            cp.wait_send()
        for cp in mine:
            cp.wait()


class _Rider:
    def __init__(self, kind, arrays, out_shapes, slots, scratch=None, forward_step=None):
        self.kind, self.arrays, self.slots = kind, list(arrays), slots
        self.n = len(self.arrays)
        hbm = pl.BlockSpec(memory_space=pl.ANY)
        self.in_specs = [hbm] * self.n
        self.out_specs = [hbm] * self.n
        self.out_shape = [jax.ShapeDtypeStruct(tuple(s), a.dtype) for s, a in zip(out_shapes, self.arrays)]
        self.scratch = scratch if scratch is not None else [
            pltpu.SemaphoreType.DMA((self.n, 7)), pltpu.SemaphoreType.DMA((self.n, 7)),
            pltpu.SemaphoreType.DMA((self.n,))]
        self.forward_step = forward_step

    def bind(self, in_refs, out_refs, scratch):
        return self.kind(in_refs, out_refs, self.slots, *scratch)


def _gather_rider(shards, full_shapes, slots):
    return _Rider(_Gather, shards, full_shapes, slots)


def _exchange_rider(sends, part_shapes, slots):
    return _Rider(_Exchange, sends, [(N_DEV,) + tuple(s) for s in part_shapes], slots)


def _split_refs(refs, n_in, n_out, n_scratch, rider):
    k = rider.n if rider is not None else 0
    ins, r_ins = refs[:n_in], refs[n_in:n_in + k]
    outs, r_outs = refs[n_in + k:n_in + k + n_out], refs[n_in + k + n_out:n_in + 2 * k + n_out]
    rest = refs[n_in + 2 * k + n_out:]
    scratch, r_scratch = rest[:n_scratch], rest[n_scratch:]
    comm = rider.bind(r_ins, r_outs, r_scratch) if rider is not None else None
    if comm is not None:
        comm.forward_step = rider.forward_step
    return ins + outs + scratch, comm


def _ride_before(comm, i, nt):
    if comm is not None:
        pl.when(i == 0)(comm.start)
        pl.when(i == (nt - 1 if comm.forward_step is None else min(comm.forward_step, nt - 1)))(comm.forward)


def _ride_after(comm, i, nt):
    if comm is not None:
        pl.when(i == nt - 1)(comm.finish)


def _extend(specs, rider, field):
    return list(specs) + (getattr(rider, field) if rider is not None else [])


def _comm_call(name, rider):
    def body(*refs):
        _, comm = _split_refs(refs, 0, 0, 0, rider)
        comm.start()
        comm.forward()
        comm.finish()

    return pl.pallas_call(body, name=name, in_specs=rider.in_specs, out_specs=rider.out_specs,
                          out_shape=rider.out_shape, scratch_shapes=rider.scratch,
                          compiler_params=pltpu.CompilerParams(vmem_limit_bytes=VMEM_LIMIT))(*rider.arrays)


N_CHIPS = 4


class _TwoLevel:
    def __init__(self, in_refs, out_refs, slots, *scratch):
        self.in_refs, self.out_refs, self.slots = in_refs, out_refs, slots
        self.n = n = len(in_refs)
        self.own_bufs, self.recv_bufs = scratch[:n], scratch[n:2 * n]
        self.swap_send, self.swap_recv, self.local_sems, self.chip_send, self.chip_recv = scratch[2 * n:]
        self.pos = _position()

    def _swap(self):
        x, y, c = self.pos
        return [pltpu.make_async_remote_copy(
            src_ref=self.slots[a](self.in_refs[a], 2 * q + 1 - c), dst_ref=self.recv_bufs[a].at[q],
            send_sem=self.swap_send.at[a, q], recv_sem=self.swap_recv.at[a, q],
            device_id=(x, y, 1 - c), device_id_type=MESH) for a in range(self.n) for q in range(N_CHIPS)]

    def _mine(self):
        c = self.pos[2]
        return [pltpu.make_async_copy(self.slots[a](self.in_refs[a], 2 * q + c), self.own_bufs[a].at[q],
                                      self.local_sems.at[a, q]) for a in range(self.n) for q in range(N_CHIPS)]

    def _to_chips(self):
        x, y, c = self.pos
        copies = []
        for k in range(1, N_CHIPS):
            px, py = x ^ (k >> 1), y ^ (k & 1)
            copies += [pltpu.make_async_remote_copy(
                src_ref=self.own_bufs[a].at[2 * px + py], dst_ref=self.out_refs[a].at[2 * x + y],
                send_sem=self.chip_send.at[a, k - 1], recv_sem=self.chip_recv.at[a, k - 1],
                device_id=(px, py, c), device_id_type=MESH) for a in range(self.n)]
        return copies

    def _own(self):
        x, y, _ = self.pos
        return [pltpu.make_async_copy(self.own_bufs[a].at[2 * x + y], self.out_refs[a].at[2 * x + y],
                                      self.local_sems.at[a, N_CHIPS]) for a in range(self.n)]

    def start(self):
        for cp in self._swap() + self._mine():
            cp.start()

    def forward(self):
        swap, mine = self._swap(), self._mine()
        for a in range(self.n):
            for q in range(N_CHIPS):
                mine[a * N_CHIPS + q].wait()
                swap[a * N_CHIPS + q].wait_recv()
                self.own_bufs[a][q] = (self.own_bufs[a][q].astype(F32)
                                       + self.recv_bufs[a][q].astype(F32)).astype(BF16)
        for cp in self._to_chips() + self._own():
            cp.start()

    def finish(self):
        to_chips = self._to_chips()
        for cp in to_chips:
            cp.wait_recv()
        for cp in to_chips + self._swap():
            cp.wait_send()
        for cp in self._own():
            cp.wait()


def _two_level_rider(sends, part_shapes, slots, forward_step=None):
    n = len(sends)
    bufs = [pltpu.VMEM((N_CHIPS,) + tuple(s), a.dtype) for s, a in zip(part_shapes, sends)]
    scratch = bufs + bufs + [pltpu.SemaphoreType.DMA((n, N_CHIPS)), pltpu.SemaphoreType.DMA((n, N_CHIPS)),
                             pltpu.SemaphoreType.DMA((n, N_CHIPS + 1)), pltpu.SemaphoreType.DMA((n, N_CHIPS - 1)),
                             pltpu.SemaphoreType.DMA((n, N_CHIPS - 1))]
    return _Rider(_TwoLevel, sends, [(N_CHIPS,) + tuple(s) for s in part_shapes], slots, scratch, forward_step)


class _Joined:
    def __init__(self, first, second):
        self.first, self.second = first, second

    def start(self):
        self.first.start()
        self.second.start()

    def forward(self):
        self.first.forward()
        self.second.forward()

    def finish(self):
        self.first.finish()
        self.second.finish()


class _JoinedRider:
    def __init__(self, first, second):
        self.first, self.second = first, second
        self.n = first.n + second.n
        self.arrays = first.arrays + second.arrays
        self.in_specs = first.in_specs + second.in_specs
        self.out_specs = first.out_specs + second.out_specs
        self.out_shape = first.out_shape + second.out_shape
        self.scratch = first.scratch + second.scratch
        self.forward_step = first.forward_step

    def bind(self, in_refs, out_refs, scratch):
        k, s = self.first.n, len(self.first.scratch)
        return _Joined(self.first.bind(in_refs[:k], out_refs[:k], scratch[:s]),
                       self.second.bind(in_refs[k:], out_refs[k:], scratch[s:]))


def _adamw(w, g, m, v):
    m = ADAM_B1 * m + (1.0 - ADAM_B1) * g
    v = ADAM_B2 * v + (1.0 - ADAM_B2) * (g * g)
    m_hat = m / (1.0 - ADAM_B1 ** ADAM_STEP)
    v_hat = v / (1.0 - ADAM_B2 ** ADAM_STEP)
    delta = -ADAM_LR * (m_hat / (jnp.sqrt(v_hat) + ADAM_EPS) + ADAM_WD * w)
    return delta, m, v


def _sum_parts(parts_ref, index=()):
    g = parts_ref[(0,) + index].astype(F32)
    for s in range(1, parts_ref.shape[0]):
        g = g + parts_ref[(s,) + index].astype(F32)
    return g


def _adamw_call(name, parts, w, m, v, block_rows):
    rows, cols = w.shape
    nb = rows // block_rows
    senders = parts.shape[0]

    def body(parts_ref, w_ref, m_ref, v_ref, g_ref, delta_ref, m_out, v_out):
        g = _sum_parts(parts_ref)
        delta, m_new, v_new = _adamw(w_ref[...], g, m_ref[...], v_ref[...])
        g_ref[...] = g
        delta_ref[...] = delta
        m_out[...] = m_new
        v_out[...] = v_new

    blk = pl.BlockSpec((block_rows, cols), lambda i: (i, 0))
    return pl.pallas_call(
        body, name=name, grid=(nb,),
        in_specs=[pl.BlockSpec((senders, block_rows, cols), lambda i: (0, i, 0)), blk, blk, blk],
        out_specs=[blk, blk, blk, blk],
        out_shape=[jax.ShapeDtypeStruct((rows, cols), F32)] * 4,
        compiler_params=_params(("parallel",)),
    )(parts, w, m, v)


def _adamw_slabs_call(name, parts, w, m, v, rider=None):
    def main(parts_ref, w_ref, m_ref, v_ref, g_ref, delta_ref, m_out, v_out):
        g = _sum_parts(parts_ref)
        delta, m_new, v_new = _adamw(w_ref[...], g, m_ref[...], v_ref[...])
        g_ref[...] = g
        delta_ref[...] = delta
        m_out[...] = m_new
        v_out[...] = v_new

    def body(*refs):
        own, comm = _split_refs(refs, 4, 4, 0, rider)
        if comm is not None:
            comm.start()
        main(*own)
        if comm is not None:
            comm.forward()
            comm.finish()

    vmem = pl.BlockSpec(memory_space=pltpu.VMEM)
    return pl.pallas_call(
        body, name=name, in_specs=_extend([vmem] * 4, rider, "in_specs"),
        out_specs=_extend([vmem] * 4, rider, "out_specs"),
        out_shape=_extend([jax.ShapeDtypeStruct(w.shape, F32)] * 4, rider, "out_shape"),
        scratch_shapes=_extend([], rider, "scratch"),
        compiler_params=pltpu.CompilerParams(vmem_limit_bytes=VMEM_LIMIT),
    )(parts, w, m, v, *_extend([], rider, "arrays"))


WIDE_ROWS = 8
NARROW_ROWS = 40
NARROW_GKW_ROW = 8
NARROW_GKB_ROW = 24
NARROW_HW_ROW = 32
GROUP_SHARD = POOL_GROUP_DIM // N_DEV
KEY_SHARD = GLA_KEY_WIDTH // N_DEV
HEAD_V_SHARD = GLA_HEAD_V // N_DEV


def _small_adamw_call(wide, narrow, w, m, v):
    names = ("norm_w", "pool_scale", "final_norm_w", "pool_group_b", "gla_gk_w", "gla_gk_b", "gla_head_norm_w")
    where = {
        "norm_w": (0, slice(0, 2), slice(None)),
        "pool_scale": (0, slice(2, 3), slice(None)),
        "final_norm_w": (0, slice(3, 4), slice(None)),
        "pool_group_b": (1, slice(0, POOL_GROUPS), slice(0, GROUP_SHARD)),
        "gla_gk_w": (1, slice(NARROW_GKW_ROW, NARROW_GKW_ROW + GLA_GATE_RANK), slice(0, KEY_SHARD)),
        "gla_gk_b": (1, slice(NARROW_GKB_ROW, NARROW_GKB_ROW + 1), slice(0, KEY_SHARD)),
        "gla_head_norm_w": (1, slice(NARROW_HW_ROW, NARROW_HW_ROW + 1), slice(0, HEAD_V_SHARD)),
    }
    k = len(names)

    def body(*refs):
        parts = refs[0:2]
        w_refs, m_refs, v_refs = refs[2:2 + k], refs[2 + k:2 + 2 * k], refs[2 + 2 * k:2 + 3 * k]
        outs = refs[2 + 3 * k:]
        loss_ref = outs[0]
        loss_ref[...] = _sum_parts(parts[0], (slice(4, 5), slice(0, 1)))
        for i, name in enumerate(names):
            buf, rows, cols = where[name]
            g = _sum_parts(parts[buf], (rows, cols))
            delta, m_new, v_new = _adamw(w_refs[i][...], g, m_refs[i][...], v_refs[i][...])
            outs[1 + i][...] = g
            outs[1 + k + i][...] = delta
            outs[1 + 2 * k + i][...] = m_new
            outs[1 + 3 * k + i][...] = v_new

    vmem = pl.BlockSpec(memory_space=pltpu.VMEM)
    shapes = [jax.ShapeDtypeStruct(w[n].shape, F32) for n in names]
    res = pl.pallas_call(
        body, name="adamw_small", in_specs=[vmem] * (2 + 3 * k), out_specs=[vmem] * (1 + 4 * k),
        out_shape=[jax.ShapeDtypeStruct((1, 1), F32)] + shapes * 4,
    )(wide, narrow, *[w[n] for n in names], *[m[n] for n in names], *[v[n] for n in names])
    unzip = lambda j: dict(zip(names, res[1 + j * k:1 + (j + 1) * k]))
    return res[0], unzip(0), unzip(1), unzip(2), unzip(3)


def kernel(x, norm_w, pool_in_w, pool_group_w, pool_group_b, pool_scale, pool_out_w, gla_in_w, gla_gk_w, gla_gk_b, gla_head_norm_w, gla_out_w, final_norm_w, loss_target, m_norm_w, m_pool_in_w, m_pool_group_w, m_pool_group_b, m_pool_scale, m_pool_out_w, m_gla_in_w, m_gla_gk_w, m_gla_gk_b, m_gla_head_norm_w, m_gla_out_w, m_final_norm_w, v_norm_w, v_pool_in_w, v_pool_group_w, v_pool_group_b, v_pool_scale, v_pool_out_w, v_gla_in_w, v_gla_gk_w, v_gla_gk_b, v_gla_head_norm_w, v_gla_out_w, v_final_norm_w):
    w = dict(norm_w=norm_w, pool_in_w=pool_in_w, pool_group_w=pool_group_w, pool_group_b=pool_group_b,
             pool_scale=pool_scale, pool_out_w=pool_out_w, gla_in_w=gla_in_w, gla_gk_w=gla_gk_w, gla_gk_b=gla_gk_b,
             gla_head_norm_w=gla_head_norm_w, gla_out_w=gla_out_w, final_norm_w=final_norm_w)
    m = dict(norm_w=m_norm_w, pool_in_w=m_pool_in_w, pool_group_w=m_pool_group_w, pool_group_b=m_pool_group_b,
             pool_scale=m_pool_scale, pool_out_w=m_pool_out_w, gla_in_w=m_gla_in_w, gla_gk_w=m_gla_gk_w,
             gla_gk_b=m_gla_gk_b, gla_head_norm_w=m_gla_head_norm_w, gla_out_w=m_gla_out_w,
             final_norm_w=m_final_norm_w)
    v = dict(norm_w=v_norm_w, pool_in_w=v_pool_in_w, pool_group_w=v_pool_group_w, pool_group_b=v_pool_group_b,
             pool_scale=v_pool_scale, pool_out_w=v_pool_out_w, gla_in_w=v_gla_in_w, gla_gk_w=v_gla_gk_w,
             gla_gk_b=v_gla_gk_b, gla_head_norm_w=v_gla_head_norm_w, gla_out_w=v_gla_out_w,
             final_norm_w=v_final_norm_w)
    col_shard = GLA_IN_WIDTH // N_DEV
    row_shard = D_MODEL // N_DEV

    def lanes(a):
        return jnp.pad(a, [(0, 0)] * (a.ndim - 1) + [(0, LANES - a.shape[-1])])

    small_in = jnp.concatenate([lanes(pool_group_b[0]), lanes(gla_gk_b), lanes(gla_head_norm_w),
                                jnp.zeros((2, LANES), F32)], axis=0)
    in_cols = 2 * POOL_WIDTH // N_DEV
    pool_in, pool_gw, pool_out, small_all = _comm_call("pool_weights_all_gather", _gather_rider(
        [pool_in_w[0].astype(BF16), pool_group_w[0].astype(BF16), pool_out_w[0].astype(BF16), small_in],
        [(D_MODEL, 2 * POOL_WIDTH), (POOL_GROUPS, POOL_GROUP_DIM, POOL_GROUP_DIM), (POOL_WIDTH, D_MODEL),
         (N_DEV, 8, LANES)],
        [_dim1_slot(in_cols), _dim1_slot(GROUP_SHARD), _row_slot(row_shard), _lead_slot]))
    pool_gb = jnp.transpose(small_all[:, 0:POOL_GROUPS, :GROUP_SHARD], (1, 0, 2)).reshape(1, POOL_WIDTH)
    gla_gkb = small_all[:, POOL_GROUPS, :KEY_SHARD].reshape(1, GLA_KEY_WIDTH)
    gla_hw = jnp.tile(small_all[:, POOL_GROUPS + 1, :HEAD_V_SHARD].reshape(1, GLA_HEAD_V), (1, GLA_HEADS))
    nw0, nw1, wf = norm_w[0:1], norm_w[1:2], final_norm_w.reshape(1, D_MODEL)
    xs, target = x[0], loss_target[0]

    h1, p, gla_in_parts, gkw_parts, gla_out = _pool_fwd_call(
        xs, nw0, pool_in, pool_gw, pool_gb, pool_scale, pool_out, _gather_rider(
            [jnp.transpose(gla_in_w[0]).astype(BF16), gla_gk_w[0].astype(BF16), gla_out_w[0].astype(BF16)],
            [(N_DEV, col_shard, D_MODEL), (N_DEV, GLA_GATE_RANK, KEY_SHARD), (GLA_VALUE_WIDTH, D_MODEL)],
            [_lead_slot, _lead_slot, _row_slot(row_shard)]))
    gla_in = gla_in_parts.reshape(GLA_IN_WIDTH, D_MODEL)
    gla_gkw = jnp.pad(jnp.transpose(gkw_parts, (1, 0, 2)).reshape(GLA_GATE_RANK, GLA_KEY_WIDTH),
                      ((0, GLA_LOW_PAD - GLA_GATE_RANK), (0, 0)))
    dh2, proj, o, states, loss_part, dwf = _gla_fwd_call(h1, nw1, gla_in, gla_gkw, gla_gkb, gla_hw, gla_out, wf, target)

    dproj, d_gla_out, dhw, dgkw, dgkb = _gla_bwd_call(dh2, proj, o, states, gla_gkw, gla_gkb, gla_hw, gla_out)
    dh1, d_gla_in, dnw1, landed_gla_out = _inproj_bwd_call(
        "gla_in_bwd", dproj, h1, nw1, gla_in, dh2,
        _exchange_rider([d_gla_out], [(row_shard, D_MODEL)], [_row_slot(row_shard)]), transposed=True)
    slabs = col_shard * D_MODEL // (BF16_ROWS * LANES)
    gla_in_send = d_gla_in.reshape(N_DEV, slabs, BF16_ROWS, LANES)
    dp, d_pool_out, dgw, dgb, dsc, landed_gla_in = _pool_bwd_call(
        dh1, p, pool_gw, pool_gb, pool_scale, pool_out,
        _two_level_rider([gla_in_send], [(slabs, BF16_ROWS, LANES)], [_lead_slot], TWO_LEVEL_ADD_STEP))
    grad_x, d_pool_in, dnw0 = _inproj_bwd_call("pool_in_bwd", dp, xs, nw0, pool_in, dh1)

    wide = jnp.concatenate([
        dnw0, dnw1, dsc, dwf, jnp.pad(loss_part[0:1, 0:1], ((0, 0), (0, D_MODEL - 1))),
        jnp.zeros((WIDE_ROWS - 5, D_MODEL), F32)], axis=0)

    def rows8(a):
        return jnp.pad(lanes(a), ((0, 0), (0, -a.shape[1] % 8), (0, 0)))

    narrow = jnp.concatenate([
        rows8(jnp.transpose(dgb.reshape(POOL_GROUPS, N_DEV, GROUP_SHARD), (1, 0, 2))),
        rows8(jnp.transpose(dgkw[:GLA_GATE_RANK].reshape(GLA_GATE_RANK, N_DEV, KEY_SHARD), (1, 0, 2))),
        rows8(dgkb.reshape(N_DEV, 1, KEY_SHARD)),
        rows8(dhw.reshape(GLA_HEADS, GLA_HEAD_V).sum(axis=0).reshape(N_DEV, 1, HEAD_V_SHARD)),
    ], axis=1)
    last_exchange = _JoinedRider(
        _two_level_rider([d_pool_in, d_pool_out, dgw],
                         [(D_MODEL, in_cols), (row_shard, D_MODEL), (POOL_GROUPS, GROUP_SHARD, POOL_GROUP_DIM)],
                         [_dim1_slot(in_cols), _row_slot(row_shard), _dim1_slot(GROUP_SHARD)]),
        _exchange_rider([wide, narrow], [(WIDE_ROWS, D_MODEL), (NARROW_ROWS, LANES)],
                        [lambda ref, d: ref, _lead_slot]))

    res = {}
    as_slabs = lambda t: jnp.transpose(t[0]).reshape(slabs, BF16_ROWS, LANES)
    *outs, landed_pool_in, landed_pool_out, landed_gw, landed_wide, landed_narrow = _adamw_slabs_call(
        "adamw_gla_in_w", landed_gla_in, as_slabs(gla_in_w), as_slabs(m_gla_in_w), as_slabs(v_gla_in_w),
        last_exchange)
    res["gla_in_w"] = [jnp.transpose(t.reshape(col_shard, D_MODEL))[None] for t in outs]
    for name, parts, rows, cols, block in [
            ("pool_in_w", landed_pool_in, D_MODEL, in_cols, 256),
            ("pool_group_w", landed_gw, POOL_GROUPS * GROUP_SHARD, POOL_GROUP_DIM, 128),
            ("pool_out_w", landed_pool_out, row_shard, D_MODEL, 128),
            ("gla_out_w", landed_gla_out, row_shard, D_MODEL, 128)]:
        outs = _adamw_call("adamw_" + name, parts.reshape(parts.shape[0], rows, cols), w[name].reshape(rows, cols),
                           m[name].reshape(rows, cols), v[name].reshape(rows, cols), block)
        res[name] = [t.reshape(w[name].shape) for t in outs]
    small_shapes ={"norm_w": (2, D_MODEL), "pool_scale": (1, D_MODEL), "final_norm_w": (1, D_MODEL),
                    "pool_group_b": (POOL_GROUPS, GROUP_SHARD), "gla_gk_w": (GLA_GATE_RANK, KEY_SHARD),
                    "gla_gk_b": (1, KEY_SHARD), "gla_head_norm_w": (1, HEAD_V_SHARD)}
    as_small = lambda t: {n: t[n].reshape(s) for n, s in small_shapes.items()}
    loss, *small_outs = _small_adamw_call(landed_wide, landed_narrow, as_small(w), as_small(m), as_small(v))
    for name in small_shapes:
        res[name] = [t[name].reshape(w[name].shape) for t in small_outs]
    order = ("norm_w", "pool_in_w", "pool_group_w", "pool_group_b", "pool_scale", "pool_out_w", "gla_in_w",
             "gla_gk_w", "gla_gk_b", "gla_head_norm_w", "gla_out_w", "final_norm_w")
    return (loss.reshape(()), grad_x[None], *[res[n][0] for n in order], *[res[n][1] for n in order],
            *[res[n][2] for n in order], *[res[n][3] for n in order])
```

```python
import jax
import jax.numpy as jnp
from jax import lax
from jax.experimental import pallas as pl
from jax.experimental.pallas import tpu as pltpu

F32 = jnp.float32
BF16 = jnp.bfloat16
MESH = pl.DeviceIdType.MESH

N_DEV = 8
D_MODEL = 1024
POOL_WIDTH = 1024
POOL_GROUPS = 4
POOL_GROUP_DIM = 256
POOL_HALO = 16
GLA_HEADS = 4
GLA_HEAD_K = 128
GLA_HEAD_V = 256
GLA_KEY_WIDTH = 512
GLA_VALUE_WIDTH = 1024
GLA_GATE_RANK = 16
GLA_IN_WIDTH = 3088
GLA_IN_PAD = 3200
GLA_LOW_PAD = 128
GLA_QKVG_WIDTH = 3072
CHUNK = 64
GATE_NORMALIZER = 16.0
RMS_EPS = 1e-6
Q_SCALE = GLA_HEAD_K ** -0.5

ADAM_LR = 0.001
ADAM_B1 = 0.9
ADAM_B2 = 0.999
ADAM_EPS = 1e-08
ADAM_WD = 0.01
ADAM_STEP = 10

LANES = 128
BF16_ROWS = 16
VMEM_LIMIT = 56 * 1024 * 1024
ROW_TILE = 256
MATMUL_ROW_TILE = 512
TWO_LEVEL_ADD_STEP = 1


def _dot_nn(a, b):
    return lax.dot_general(a, b, (((1,), (0,)), ((), ())), preferred_element_type=F32)


def _dot_nt(a, b):
    return lax.dot_general(a, b, (((1,), (1,)), ((), ())), preferred_element_type=F32)


def _dot_tn(a, b):
    return lax.dot_general(a, b, (((0,), (0,)), ((), ())), preferred_element_type=F32)


def _rms(x):
    rstd = lax.rsqrt(jnp.mean(x * x, axis=-1, keepdims=True) + RMS_EPS)
    return x * rstd, rstd


def _rms_bwd(dxhat, xhat, rstd):
    return rstd * (dxhat - xhat * jnp.mean(dxhat * xhat, axis=-1, keepdims=True))


def _sigmoid(x):
    return 1.0 / (1.0 + jnp.exp(-x))


def _params(sem=("arbitrary",)):
    return pltpu.CompilerParams(dimension_semantics=sem, vmem_limit_bytes=VMEM_LIMIT)


def _full(shape):
    return pl.BlockSpec(shape, lambda i: (0,) * len(shape))


def _const(shape):
    return pl.BlockSpec(shape, lambda i: (0,) * len(shape), pipeline_mode=pl.Buffered(1))


def _window_sums(ext, forward):
    n = ext.shape[0]
    outs = []
    for g in range(POOL_GROUPS):
        s = ext[:, g * POOL_GROUP_DIM:(g + 1) * POOL_GROUP_DIM]
        for k in range(g + 1):
            shift = (1 << k) if forward else n - (1 << k)
            s = s + pltpu.roll(s, shift, axis=0)
        outs.append(s[:n - POOL_HALO])
    return outs


def _inv_count(row0, tm):
    row = row0 + lax.broadcasted_iota(jnp.int32, (tm, 1), 0)
    return [1.0 / jnp.minimum(row + 1, 2 << g).astype(F32) for g in range(POOL_GROUPS)]


def _pool_mix(u, u_prev, row0, gw_ref, gb):
    tm = u.shape[0]
    sums = _window_sums(jnp.concatenate([u, u_prev], axis=0), True)
    inv = _inv_count(row0, tm)
    pooled, mixed = [], []
    for g in range(POOL_GROUPS):
        ug = u[:, g * POOL_GROUP_DIM:(g + 1) * POOL_GROUP_DIM]
        pg = (sums[g] * inv[g] - ug).astype(BF16)
        pooled.append(pg)
        mixed.append(_dot_nn(pg, gw_ref[g]))
    return pooled, jnp.concatenate(mixed, axis=1) + gb


def _pool_fwd_call(x, nw, w_in, gw, gb, sc, w_out, rider=None):
    seq = x.shape[0]
    tm = min(MATMUL_ROW_TILE, seq)
    nt = seq // tm

    def main(x_ref, nw_ref, win_ref, gw_ref, gb_ref, sc_ref, wout_ref, h_ref, p_ref, halo_ref):
        i = pl.program_id(0)

        @pl.when(i == 0)
        def _():
            halo_ref[...] = jnp.zeros_like(halo_ref)

        xt = x_ref[...]
        xhat, _ = _rms(xt)
        n = (xhat * nw_ref[...]).astype(BF16)
        p = _dot_nn(n, win_ref[...])
        p_ref[...] = p
        u = p[:, :POOL_WIDTH]
        gate = p[:, POOL_WIDTH:]
        _, mixed = _pool_mix(u, halo_ref[...], i * tm, gw_ref, gb_ref[...])
        halo_ref[...] = u[tm - POOL_HALO:, :]
        y = (mixed * sc_ref[...] * (gate * _sigmoid(gate))).astype(BF16)
        h_ref[...] = xt + _dot_nn(y, wout_ref[...])

    def body(*refs):
        own, comm = _split_refs(refs, 7, 2, 1, rider)
        _ride_before(comm, pl.program_id(0), nt)
        main(*own)
        _ride_after(comm, pl.program_id(0), nt)

    return pl.pallas_call(
        body, name="pool_fwd", grid=(nt,),
        in_specs=_extend([pl.BlockSpec((tm, D_MODEL), lambda i: (i, 0)), _const((1, D_MODEL)),
                          _const((D_MODEL, 2 * POOL_WIDTH)), _const((POOL_GROUPS, POOL_GROUP_DIM, POOL_GROUP_DIM)),
                          _const((1, POOL_WIDTH)), _const((1, POOL_WIDTH)), _const((POOL_WIDTH, D_MODEL))],
                         rider, "in_specs"),
        out_specs=_extend([pl.BlockSpec((tm, D_MODEL), lambda i: (i, 0)),
                           pl.BlockSpec((tm, 2 * POOL_WIDTH), lambda i: (i, 0))], rider, "out_specs"),
        out_shape=_extend([jax.ShapeDtypeStruct((seq, D_MODEL), F32),
                           jax.ShapeDtypeStruct((seq, 2 * POOL_WIDTH), F32)], rider, "out_shape"),
        scratch_shapes=_extend([pltpu.VMEM((POOL_HALO, POOL_WIDTH), F32)], rider, "scratch"),
        compiler_params=_params(),
    )(x, nw, w_in, gw, gb, sc, w_out, *_extend([], rider, "arrays"))


def _pool_bwd_call(dh, p, gw, gb, sc, w_out, rider=None):
    seq = dh.shape[0]
    tm = min(MATMUL_ROW_TILE, seq)
    nt = seq // tm
    halo_blocks = tm // POOL_HALO

    def main(dh_ref, p_ref, pprev_ref, gw_ref, gb_ref, sc_ref, wout_ref,
             dp_ref, dwout_hbm, dgw_hbm, dgb_ref, dsc_ref, carry_ref, dwout_acc, dgw_acc, dwout_stage, dgw_stage):
        i = pl.program_id(0)
        t = nt - 1 - i

        @pl.when(i == 0)
        def _():
            carry_ref[...] = jnp.zeros_like(carry_ref)
            dwout_acc[...] = jnp.zeros_like(dwout_acc)
            dgw_acc[...] = jnp.zeros_like(dgw_acc)
            dgb_ref[...] = jnp.zeros_like(dgb_ref)
            dsc_ref[...] = jnp.zeros_like(dsc_ref)

        dhb = dh_ref[...].astype(BF16)
        dy = _dot_nt(dhb, wout_ref[...])
        p = p_ref[...]
        u = p[:, :POOL_WIDTH]
        gate = p[:, POOL_WIDTH:]
        u_prev = jnp.where(t > 0, pprev_ref[:, :POOL_WIDTH], 0.0)
        pooled, mixed = _pool_mix(u, u_prev, t * tm, gw_ref, gb_ref[...])
        sg = _sigmoid(gate)
        silu = gate * sg
        sc = sc_ref[...]
        y = (mixed * sc * silu).astype(BF16)
        dwout_acc[...] += _dot_tn(y, dhb)
        dmixed = dy * sc * silu
        dsc_ref[...] += jnp.sum(dy * mixed * silu, axis=0, keepdims=True)
        dgate = dy * mixed * sc * (sg * (1.0 + gate * (1.0 - sg)))
        dgb_ref[...] += jnp.sum(dmixed, axis=0, keepdims=True)
        inv = _inv_count(t * tm, tm)
        dpooled, scaled = [], []
        for g in range(POOL_GROUPS):
            dmg = dmixed[:, g * POOL_GROUP_DIM:(g + 1) * POOL_GROUP_DIM].astype(BF16)
            dgw_acc[g] += _dot_tn(pooled[g], dmg)
            dpg = _dot_nt(dmg, gw_ref[g])
            dpooled.append(dpg)
            scaled.append(dpg * inv[g])
        r = jnp.concatenate(scaled, axis=1)
        sums = _window_sums(jnp.concatenate([r, carry_ref[...]], axis=0), False)
        carry_ref[...] = r[:POOL_HALO, :]
        du = jnp.concatenate([sums[g] - dpooled[g] for g in range(POOL_GROUPS)], axis=1)
        dp_ref[...] = jnp.concatenate([du, dgate], axis=1).astype(BF16)

        @pl.when(i == nt - 1)
        def _():
            dwout_stage[...] = dwout_acc[...].astype(BF16)
            dgw_stage[...] = dgw_acc[...].astype(BF16)
            pltpu.sync_copy(dwout_stage, dwout_hbm)
            pltpu.sync_copy(dgw_stage, dgw_hbm)

    def body(*refs):
        own, comm = _split_refs(refs, 7, 5, 5, rider)
        _ride_before(comm, pl.program_id(0), nt)
        main(*own)
        _ride_after(comm, pl.program_id(0), nt)

    rev = lambda i: (nt - 1 - i, 0)
    return pl.pallas_call(
        body, name="pool_bwd", grid=(nt,),
        in_specs=_extend([pl.BlockSpec((tm, D_MODEL), rev), pl.BlockSpec((tm, 2 * POOL_WIDTH), rev),
                          pl.BlockSpec((POOL_HALO, 2 * POOL_WIDTH),
                                       lambda i: (jnp.maximum((nt - 1 - i) * halo_blocks - 1, 0), 0)),
                          _const((POOL_GROUPS, POOL_GROUP_DIM, POOL_GROUP_DIM)), _const((1, POOL_WIDTH)),
                          _const((1, POOL_WIDTH)), _const((POOL_WIDTH, D_MODEL))], rider, "in_specs"),
        out_specs=_extend([pl.BlockSpec((tm, 2 * POOL_WIDTH), rev), pl.BlockSpec(memory_space=pl.ANY),
                           pl.BlockSpec(memory_space=pl.ANY), _full((1, POOL_WIDTH)), _full((1, POOL_WIDTH))],
                          rider, "out_specs"),
        out_shape=_extend([jax.ShapeDtypeStruct((seq, 2 * POOL_WIDTH), BF16),
                           jax.ShapeDtypeStruct((POOL_WIDTH, D_MODEL), BF16),
                           jax.ShapeDtypeStruct((POOL_GROUPS, POOL_GROUP_DIM, POOL_GROUP_DIM), BF16),
                           jax.ShapeDtypeStruct((1, POOL_WIDTH), F32), jax.ShapeDtypeStruct((1, POOL_WIDTH), F32)],
                          rider, "out_shape"),
        scratch_shapes=_extend([pltpu.VMEM((POOL_HALO, POOL_WIDTH), F32), pltpu.VMEM((POOL_WIDTH, D_MODEL), F32),
                                pltpu.VMEM((POOL_GROUPS, POOL_GROUP_DIM, POOL_GROUP_DIM), F32),
                                pltpu.VMEM((POOL_WIDTH, D_MODEL), BF16),
                                pltpu.VMEM((POOL_GROUPS, POOL_GROUP_DIM, POOL_GROUP_DIM), BF16)], rider, "scratch"),
        compiler_params=_params(),
    )(dh, p, p, gw, gb, sc, w_out, *_extend([], rider, "arrays"))


def _rows_then_zeros(ref, lo, hi, rows):
    part = ref[lo:hi, :]
    return jnp.concatenate([part, jnp.zeros((rows - (hi - lo), part.shape[1]), part.dtype)], axis=0)


def _inproj_bwd_call(name, dproj, h_in, nw, w_in, dres, rider=None, transposed=False):
    seq = h_in.shape[0]
    width = dproj.shape[1]
    w_shape = tuple(w_in.shape)
    acc_shape = (width, D_MODEL) if transposed else w_shape
    whole = w_shape[0] // LANES * LANES
    tm = min(MATMUL_ROW_TILE, seq)
    nt = seq // tm

    def main(dproj_ref, h_ref, nw_ref, win_ref, dres_ref, dh_ref, dw_hbm, dnw_ref, dw_acc, dw_stage):
        i = pl.program_id(0)

        @pl.when(i == 0)
        def _():
            dw_acc[...] = jnp.zeros_like(dw_acc)
            dnw_ref[...] = jnp.zeros_like(dnw_ref)

        dpb = dproj_ref[...]
        if transposed:
            dn = _dot_nn(dpb[:, :whole], win_ref[0:whole, :])
            if whole < w_shape[0]:
                dn = dn + _dot_nn(dpb[:, whole:], _rows_then_zeros(win_ref, whole, w_shape[0], width - whole))
        else:
            dn = _dot_nt(dpb, win_ref[...])
        xhat, rstd = _rms(h_ref[...])
        nw_row = nw_ref[...]
        n = (xhat * nw_row).astype(BF16)
        dw_acc[...] += _dot_tn(dpb, n) if transposed else _dot_tn(n, dpb)
        dnw_ref[...] += jnp.sum(dn * xhat, axis=0, keepdims=True)
        dh_ref[...] = _rms_bwd(dn * nw_row, xhat, rstd) + dres_ref[...]

        @pl.when(i == nt - 1)
        def _():
            dw_stage[...] = dw_acc[...].astype(BF16)
            pltpu.sync_copy(dw_stage.at[pl.ds(0, w_shape[0])], dw_hbm)

    def body(*refs):
        own, comm = _split_refs(refs, 5, 3, 2, rider)
        _ride_before(comm, pl.program_id(0), nt)
        main(*own)
        _ride_after(comm, pl.program_id(0), nt)

    row = lambda i: (i, 0)
    return pl.pallas_call(
        body, name=name, grid=(nt,),
        in_specs=_extend([pl.BlockSpec((tm, width), row), pl.BlockSpec((tm, D_MODEL), row), _const((1, D_MODEL)),
                          _const(w_shape), pl.BlockSpec((tm, D_MODEL), row)], rider, "in_specs"),
        out_specs=_extend([pl.BlockSpec((tm, D_MODEL), row), pl.BlockSpec(memory_space=pl.ANY),
                           _full((1, D_MODEL))], rider, "out_specs"),
        out_shape=_extend([jax.ShapeDtypeStruct((seq, D_MODEL), F32), jax.ShapeDtypeStruct(w_shape, BF16),
                           jax.ShapeDtypeStruct((1, D_MODEL), F32)], rider, "out_shape"),
        scratch_shapes=_extend([pltpu.VMEM(acc_shape, F32), pltpu.VMEM(acc_shape, BF16)], rider, "scratch"),
        compiler_params=_params(),
    )(dproj, h_in, nw, w_in, dres, *_extend([], rider, "arrays"))


def _chunk_scan(x, reverse):
    n = x.shape[0]
    pos = lax.broadcasted_iota(jnp.int32, (n, 1), 0) & (CHUNK - 1)
    k = 1
    while k < CHUNK:
        if reverse:
            x = x + jnp.where(pos < CHUNK - k, pltpu.roll(x, n - k, axis=0), 0.0)
        else:
            x = x + jnp.where(pos >= k, pltpu.roll(x, k, axis=0), 0.0)
        k *= 2
    return x


def _chunk_rows(j):
    return slice(j * CHUNK, (j + 1) * CHUNK)


def _kcols(h):
    return slice(h * GLA_HEAD_K, (h + 1) * GLA_HEAD_K)


def _vcols(h):
    return slice(h * GLA_HEAD_V, (h + 1) * GLA_HEAD_V)


def _chunk_masks(tm):
    idx_t = lax.broadcasted_iota(jnp.int32, (tm, tm), 0)
    idx_s = lax.broadcasted_iota(jnp.int32, (tm, tm), 1)
    same_chunk = (idx_t ^ idx_s) < CHUNK
    return same_chunk & (idx_t >= idx_s), same_chunk & (idx_t < idx_s)


class _GlaTerms:
    def __init__(self, kc, q, k, v, low_b, gkw_ref, gkb_ref, masks):
        tm = q.shape[0]
        self.q = q * Q_SCALE
        self.k = k
        self.z = _dot_nn(low_b, gkw_ref[:, kc]) + gkb_ref[:, kc]
        log_g = (jnp.minimum(self.z, 0.0) - jnp.log(1.0 + jnp.exp(-jnp.abs(self.z)))) / GATE_NORMALIZER
        self.c = _chunk_scan(log_g, False)
        is_last = lax.broadcasted_iota(jnp.int32, (CHUNK, 1), 0) == CHUNK - 1
        self.c_last = [jnp.sum(jnp.where(is_last, self.c[_chunk_rows(j), :], 0.0), axis=0, keepdims=True)
                       for j in range(tm // CHUNK)]
        c_last_rows = jnp.concatenate([jnp.broadcast_to(r, (CHUNK, r.shape[1])) for r in self.c_last], axis=0)
        self.e_pos = jnp.exp(self.c)
        self.e_neg = jnp.exp(-self.c)
        self.e_rest = jnp.exp(c_last_rows - self.c)
        self.a_b = (self.q * self.e_pos).astype(BF16)
        self.b_b = (self.k * self.e_neg).astype(BF16)
        self.cn_b = (self.q * self.e_neg).astype(BF16)
        self.dp_b = (self.k * self.e_pos).astype(BF16)
        self.kd_b = (self.k * self.e_rest).astype(BF16)
        self.v_b = v.astype(BF16)
        self.lower, self.upper = masks

    def scores(self, kc=slice(None)):
        fwd = _dot_nt(self.a_b[:, kc], self.b_b[:, kc])
        bwd = _dot_nt(self.cn_b[:, kc], self.dp_b[:, kc])
        return jnp.where(self.lower, fwd, jnp.where(self.upper, bwd, 0.0)).astype(BF16)


def _gla_fwd_call(h1, nw, w_in, gkw, gkb, hw, w_out, wf, target):
    seq = h1.shape[0]
    tm = ROW_TILE
    nt = seq // tm
    cpt = tm // CHUNK
    n_chunks = seq // CHUNK

    def body(h_ref, nw_ref, win_ref, gkw_ref, gkb_ref, hw_ref, wout_ref, wf_ref, tgt_ref,
             dh2_ref, proj_ref, o_ref, st_ref, loss_ref, dwf_ref, state_ref):
        i = pl.program_id(0)

        @pl.when(i == 0)
        def _():
            state_ref[...] = jnp.zeros_like(state_ref)
            loss_ref[...] = jnp.zeros_like(loss_ref)
            dwf_ref[...] = jnp.zeros_like(dwf_ref)

        ht = h_ref[...]
        xhat, _ = _rms(ht)
        n = (xhat * nw_ref[...]).astype(BF16)
        sections = {}
        for name, lo, hi in (("low", GLA_QKVG_WIDTH, GLA_IN_PAD), ("qk", 0, 2 * GLA_KEY_WIDTH),
                             ("v", 2 * GLA_KEY_WIDTH, GLA_QKVG_WIDTH - GLA_VALUE_WIDTH),
                             ("gate", GLA_QKVG_WIDTH - GLA_VALUE_WIDTH, GLA_QKVG_WIDTH)):
            rows = (win_ref[lo:hi, :] if hi <= GLA_IN_WIDTH
                    else _rows_then_zeros(win_ref, lo, GLA_IN_WIDTH, hi - lo))
            sections[name] = _dot_nt(n, rows)
            proj_ref[:, lo:hi] = sections[name]
        low_b = sections["low"].astype(BF16)
        masks = _chunk_masks(tm)
        on_heads = []
        for h in range(GLA_HEADS):
            kc, vc = _kcols(h), _vcols(h)
            g = _GlaTerms(kc, sections["qk"][:, kc], sections["qk"][:, GLA_KEY_WIDTH:][:, kc], sections["v"][:, vc],
                          low_b, gkw_ref, gkb_ref, masks)
            srows = slice(h * GLA_HEAD_V, (h + 1) * GLA_HEAD_V)
            o_intra = _dot_nn(g.scores(), g.v_b)
            state = state_ref[srows, :]
            o_rows = []
            for j in range(cpt):
                r = _chunk_rows(j)
                st_ref[j, srows, :] = state
                o_rows.append(o_intra[r] + _dot_nt(g.a_b[r], state.astype(BF16)))
                decay = jnp.exp(g.c_last[j])
                state = state * decay + _dot_tn(g.v_b[r], g.kd_b[r])
            state_ref[srows, :] = state
            o_head = jnp.concatenate(o_rows, axis=0)
            o_ref[:, vc] = o_head
            on_heads.append(_rms(o_head)[0])
        gate = sections["gate"]
        on = jnp.concatenate(on_heads, axis=1) * hw_ref[...]
        y = (on * (gate * _sigmoid(gate))).astype(BF16)
        h2 = ht + _dot_nn(y, wout_ref[...])
        xhat2, rstd2 = _rms(h2)
        wf_row = wf_ref[...]
        err = xhat2 * wf_row - tgt_ref[...]
        loss_ref[...] += 0.5 * jnp.sum(err * err) / D_MODEL
        dout = err * (1.0 / D_MODEL)
        dwf_ref[...] += jnp.sum(dout * xhat2, axis=0, keepdims=True)
        dh2_ref[...] = _rms_bwd(dout * wf_row, xhat2, rstd2)

    row = lambda i: (i, 0)
    return pl.pallas_call(
        body, name="gla_fwd", grid=(nt,),
        in_specs=[pl.BlockSpec((tm, D_MODEL), row), _const((1, D_MODEL)), _const((GLA_IN_WIDTH, D_MODEL)),
                  _const((GLA_LOW_PAD, GLA_KEY_WIDTH)), _const((1, GLA_KEY_WIDTH)), _const((1, GLA_VALUE_WIDTH)),
                  _const((GLA_VALUE_WIDTH, D_MODEL)), _const((1, D_MODEL)), pl.BlockSpec((tm, D_MODEL), row)],
        out_specs=[pl.BlockSpec((tm, D_MODEL), row), pl.BlockSpec((tm, GLA_IN_PAD), row),
                   pl.BlockSpec((tm, GLA_VALUE_WIDTH), row),
                   pl.BlockSpec((cpt, GLA_VALUE_WIDTH, GLA_HEAD_K), lambda i: (i, 0, 0)),
                   _full((8, LANES)), _full((1, D_MODEL))],
        out_shape=[jax.ShapeDtypeStruct((seq, D_MODEL), F32), jax.ShapeDtypeStruct((seq, GLA_IN_PAD), F32),
                   jax.ShapeDtypeStruct((seq, GLA_VALUE_WIDTH), F32),
                   jax.ShapeDtypeStruct((n_chunks, GLA_VALUE_WIDTH, GLA_HEAD_K), F32),
                   jax.ShapeDtypeStruct((8, LANES), F32), jax.ShapeDtypeStruct((1, D_MODEL), F32)],
        scratch_shapes=[pltpu.VMEM((GLA_VALUE_WIDTH, GLA_HEAD_K), F32)],
        compiler_params=_params(),
    )(h1, nw, w_in, gkw, gkb, hw, w_out, wf, target)


def _gla_bwd_call(dh2, proj, o, states, gkw, gkb, hw, w_out):
    seq = dh2.shape[0]
    tm = ROW_TILE
    nt = seq // tm
    cpt = tm // CHUNK

    def body(dh_ref, proj_ref, o_ref, st_ref, gkw_ref, gkb_ref, hw_ref, wout_ref,
             dproj_ref, dwout_hbm, dhw_ref, dgkw_ref, dgkb_ref, dstate_ref, dwout_acc, dwout_stage):
        i = pl.program_id(0)

        @pl.when(i == 0)
        def _():
            dstate_ref[...] = jnp.zeros_like(dstate_ref)
            dwout_acc[...] = jnp.zeros_like(dwout_acc)
            dhw_ref[...] = jnp.zeros_like(dhw_ref)
            dgkw_ref[...] = jnp.zeros_like(dgkw_ref)
            dgkb_ref[...] = jnp.zeros_like(dgkb_ref)

        dhb = dh_ref[...].astype(BF16)
        dy = _dot_nt(dhb, wout_ref[...])
        v0, g0 = 2 * GLA_KEY_WIDTH, GLA_QKVG_WIDTH - GLA_VALUE_WIDTH
        gate = proj_ref[:, g0:GLA_QKVG_WIDTH]
        low_b = proj_ref[:, GLA_QKVG_WIDTH:].astype(BF16)
        o = o_ref[...]
        hw_row = hw_ref[...]
        sg = _sigmoid(gate)
        silu = gate * sg
        don = dy * silu
        on_parts, do_parts, dhw_parts = [], [], []
        for h in range(GLA_HEADS):
            vc = _vcols(h)
            xh, rs = _rms(o[:, vc])
            on_parts.append(xh * hw_row[:, vc])
            dhw_parts.append(jnp.sum(don[:, vc] * xh, axis=0, keepdims=True))
            do_parts.append(_rms_bwd(don[:, vc] * hw_row[:, vc], xh, rs).astype(BF16))
        on = jnp.concatenate(on_parts, axis=1)
        dwout_acc[...] += _dot_tn((on * silu).astype(BF16), dhb)
        dhw_ref[...] += jnp.concatenate(dhw_parts, axis=1)
        dproj_ref[:, g0:GLA_QKVG_WIDTH] = (dy * on * (sg * (1.0 + gate * (1.0 - sg)))).astype(BF16)

        last_row = lax.broadcasted_iota(jnp.int32, (CHUNK, 1), 0) == CHUNK - 1
        g = _GlaTerms(slice(0, GLA_KEY_WIDTH), proj_ref[:, :GLA_KEY_WIDTH], proj_ref[:, GLA_KEY_WIDTH:v0],
                      proj_ref[:, v0:g0], low_b, gkw_ref, gkb_ref, _chunk_masks(tm))
        dc_h = []
        for h in range(GLA_HEADS):
            kc, vc = _kcols(h), _vcols(h)
            k_cols = slice(GLA_KEY_WIDTH + kc.start, GLA_KEY_WIDTH + kc.stop)
            v_cols = slice(v0 + vc.start, v0 + vc.stop)
            do_h = do_parts[h]
            srows = slice(h * GLA_HEAD_V, (h + 1) * GLA_HEAD_V)
            scores = g.scores(kc)
            dscores = _dot_nt(do_h, g.v_b[:, vc])
            dfwd = jnp.where(g.lower, dscores, 0.0).astype(BF16)
            dbwd = jnp.where(g.upper, dscores, 0.0).astype(BF16)
            dv_intra = _dot_tn(scores, do_h)
            da_intra = _dot_nn(dfwd, g.b_b[:, kc])
            db = _dot_tn(dfwd, g.a_b[:, kc])
            dcn = _dot_nn(dbwd, g.dp_b[:, kc])
            ddp = _dot_tn(dbwd, g.cn_b[:, kc])
            dstate = dstate_ref[srows, :]
            da_rows, dkd_rows, dv_rows, dcl_rows = [None] * cpt, [None] * cpt, [None] * cpt, [None] * cpt
            for j in reversed(range(cpt)):
                r = _chunk_rows(j)
                state = st_ref[j, srows, :]
                dstate_b = dstate.astype(BF16)
                do_c = do_h[r]
                dv_rows[j] = dv_intra[r] + _dot_nt(g.kd_b[r, kc], dstate_b)
                da_rows[j] = da_intra[r] + _dot_nn(do_c, state.astype(BF16))
                dkd = _dot_nn(g.v_b[r, vc], dstate_b) * g.e_rest[r, kc]
                dkd_rows[j] = dkd
                decay = jnp.exp(g.c_last[j][:, kc])
                dc_last = (jnp.sum(dkd * g.k[r, kc], axis=0, keepdims=True)
                           + decay * jnp.sum(state * dstate, axis=0, keepdims=True))
                dcl_rows[j] = jnp.where(last_row, dc_last, 0.0)
                dstate = _dot_tn(do_c, g.a_b[r, kc]) + dstate * decay
            dstate_ref[srows, :] = dstate
            da = jnp.concatenate(da_rows, axis=0)
            dkd = jnp.concatenate(dkd_rows, axis=0)
            dproj_ref[:, v_cols] = jnp.concatenate(dv_rows, axis=0).astype(BF16)
            q_up, q_down = da * g.e_pos[:, kc], dcn * g.e_neg[:, kc]
            k_up, k_down = ddp * g.e_pos[:, kc], db * g.e_neg[:, kc] + dkd
            dproj_ref[:, kc] = (Q_SCALE * (q_up + q_down)).astype(BF16)
            dproj_ref[:, k_cols] = (k_up + k_down).astype(BF16)
            dc_h.append(g.q[:, kc] * (q_up - q_down) + g.k[:, kc] * (k_up - k_down)
                        + jnp.concatenate(dcl_rows, axis=0))
        dz = _chunk_scan(jnp.concatenate(dc_h, axis=1), True) * (1.0 / GATE_NORMALIZER) * (1.0 - _sigmoid(g.z))
        dzb = dz.astype(BF16)
        dgkb_ref[...] += jnp.sum(dz, axis=0, keepdims=True)
        dgkw_ref[...] += _dot_tn(low_b, dzb)
        dproj_ref[:, GLA_QKVG_WIDTH:] = _dot_nt(dzb, gkw_ref[...]).astype(BF16)

        @pl.when(i == nt - 1)
        def _():
            dwout_stage[...] = dwout_acc[...].astype(BF16)
            pltpu.sync_copy(dwout_stage, dwout_hbm)

    rev = lambda i: (nt - 1 - i, 0)
    return pl.pallas_call(
        body, name="gla_bwd", grid=(nt,),
        in_specs=[pl.BlockSpec((tm, D_MODEL), rev), pl.BlockSpec((tm, GLA_IN_PAD), rev),
                  pl.BlockSpec((tm, GLA_VALUE_WIDTH), rev),
                  pl.BlockSpec((cpt, GLA_VALUE_WIDTH, GLA_HEAD_K), lambda i: (nt - 1 - i, 0, 0)),
                  _const((GLA_LOW_PAD, GLA_KEY_WIDTH)), _const((1, GLA_KEY_WIDTH)), _const((1, GLA_VALUE_WIDTH)),
                  _const((GLA_VALUE_WIDTH, D_MODEL))],
        out_specs=[pl.BlockSpec((tm, GLA_IN_PAD), rev), pl.BlockSpec(memory_space=pl.ANY),
                   _full((1, GLA_VALUE_WIDTH)), _full((GLA_LOW_PAD, GLA_KEY_WIDTH)), _full((1, GLA_KEY_WIDTH))],
        out_shape=[jax.ShapeDtypeStruct((seq, GLA_IN_PAD), BF16), jax.ShapeDtypeStruct((GLA_VALUE_WIDTH, D_MODEL), BF16),
                   jax.ShapeDtypeStruct((1, GLA_VALUE_WIDTH), F32), jax.ShapeDtypeStruct((GLA_LOW_PAD, GLA_KEY_WIDTH), F32),
                   jax.ShapeDtypeStruct((1, GLA_KEY_WIDTH), F32)],
        scratch_shapes=[pltpu.VMEM((GLA_VALUE_WIDTH, GLA_HEAD_K), F32), pltpu.VMEM((GLA_VALUE_WIDTH, D_MODEL), F32),
                        pltpu.VMEM((GLA_VALUE_WIDTH, D_MODEL), BF16)],
        compiler_params=_params(),
    )(dh2, proj, o, states, gkw, gkb, hw, w_out)


def _position():
    return lax.axis_index("x"), lax.axis_index("y"), lax.axis_index("c")


def _lead_slot(ref, d):
    return ref.at[d]


def _row_slot(rows):
    return lambda ref, d: ref.at[pl.ds(pl.multiple_of(d * rows, rows), rows)]


def _dim1_slot(size):
    return lambda ref, d: ref.at[:, pl.ds(pl.multiple_of(d * size, size), size)]


class _Gather:
    def __init__(self, in_refs, out_refs, slots, send_sems, recv_sems, local_sems):
        self.in_refs, self.out_refs, self.slots = in_refs, out_refs, slots
        self.send_sems, self.recv_sems, self.local_sems = send_sems, recv_sems, local_sems
        self.n = len(in_refs)
        x, y, c = _position()
        self.c = c
        self.me, self.sibling = (x, y, c), (x, y, 1 - c)
        self.chips = [(1 - x, y), (x, 1 - y), (1 - x, 1 - y)]

    def _copy(self, a, k, block, to, from_input=False):
        part = self.slots[a](self.out_refs[a], 4 * block[0] + 2 * block[1] + block[2])
        return pltpu.make_async_remote_copy(
            src_ref=self.in_refs[a] if from_input else part, dst_ref=part,
            send_sem=self.send_sems.at[a, k], recv_sem=self.recv_sems.at[a, k], device_id=to, device_id_type=MESH)

    def _mine(self):
        return [pltpu.make_async_copy(self.in_refs[a], self.slots[a](self.out_refs[a], 4 * self.me[0] + 2 * self.me[1]
                                                                    + self.me[2]), self.local_sems.at[a])
                for a in range(self.n)]

    def _first(self):
        first = [self._copy(a, 0, self.me, self.sibling, True) for a in range(self.n)]
        return first + [self._copy(a, 1 + j, self.me, (*chip, self.c), True)
                        for j, chip in enumerate(self.chips) for a in range(self.n)]

    def _passed(self):
        return [self._copy(a, 4 + j, (*chip, self.c), self.sibling)
                for j, chip in enumerate(self.chips) for a in range(self.n)]

    def start(self):
        for cp in self._mine() + self._first():
            cp.start()

    def forward(self):
        passed = self._passed()
        for j, chip in enumerate(self.chips):
            for a in range(self.n):
                self._copy(a, 1 + j, (*chip, self.c), self.me).wait_recv()
                passed[j * self.n + a].start()

    def finish(self):
        for a in range(self.n):
            self._copy(a, 0, self.sibling, self.me).wait_recv()
        for j, chip in enumerate(self.chips):
            for a in range(self.n):
                self._copy(a, 4 + j, (*chip, 1 - self.c), self.me).wait_recv()
        for cp in self._first() + self._passed():
            cp.wait_send()
        for cp in self._mine():
            cp.wait()


class _Exchange:
    def __init__(self, in_refs, out_refs, slots, send_sems, recv_sems, local_sems):
        self.in_refs, self.out_refs, self.slots = in_refs, out_refs, slots
        self.send_sems, self.recv_sems, self.local_sems = send_sems, recv_sems, local_sems
        self.n = len(in_refs)
        self.pos = _position()

    def _copies(self):
        x, y, c = self.pos
        me = 4 * x + 2 * y + c
        mine = [pltpu.make_async_copy(self.slots[a](self.in_refs[a], me), self.out_refs[a].at[me],
                                      self.local_sems.at[a]) for a in range(self.n)]
        remote = []
        for k in range(1, N_DEV):
            px, py, pc = x ^ (k >> 2), y ^ ((k >> 1) & 1), c ^ (k & 1)
            for a in range(self.n):
                remote.append(pltpu.make_async_remote_copy(
                    src_ref=self.slots[a](self.in_refs[a], 4 * px + 2 * py + pc), dst_ref=self.out_refs[a].at[me],
                    send_sem=self.send_sems.at[a, k - 1], recv_sem=self.recv_sems.at[a, k - 1],
                    device_id=(px, py, pc), device_id_type=MESH))
        return mine, remote

    def start(self):
        mine, remote = self._copies()
        for cp in mine + remote:
            cp.start()

    def forward(self):
        pass

    def finish(self):
        mine, remote = self._copies()
        for cp in remote:
            cp.wait_recv()
        for cp in remote:
            cp.wait_send()
        for cp in mine:
            cp.wait()


class _Rider:
    def __init__(self, kind, arrays, out_shapes, slots, scratch=None, forward_step=None):
        self.kind, self.arrays, self.slots = kind, list(arrays), slots
        self.n = len(self.arrays)
        hbm = pl.BlockSpec(memory_space=pl.ANY)
        self.in_specs = [hbm] * self.n
        self.out_specs = [hbm] * self.n
        self.out_shape = [jax.ShapeDtypeStruct(tuple(s), a.dtype) for s, a in zip(out_shapes, self.arrays)]
        self.scratch = scratch if scratch is not None else [
            pltpu.SemaphoreType.DMA((self.n, 7)), pltpu.SemaphoreType.DMA((self.n, 7)),
            pltpu.SemaphoreType.DMA((self.n,))]
        self.forward_step = forward_step

    def bind(self, in_refs, out_refs, scratch):
        return self.kind(in_refs, out_refs, self.slots, *scratch)


def _gather_rider(shards, full_shapes, slots):
    return _Rider(_Gather, shards, full_shapes, slots)


def _exchange_rider(sends, part_shapes, slots):
    return _Rider(_Exchange, sends, [(N_DEV,) + tuple(s) for s in part_shapes], slots)


def _split_refs(refs, n_in, n_out, n_scratch, rider):
    k = rider.n if rider is not None else 0
    ins, r_ins = refs[:n_in], refs[n_in:n_in + k]
    outs, r_outs = refs[n_in + k:n_in + k + n_out], refs[n_in + k + n_out:n_in + 2 * k + n_out]
    rest = refs[n_in + 2 * k + n_out:]
    scratch, r_scratch = rest[:n_scratch], rest[n_scratch:]
    comm = rider.bind(r_ins, r_outs, r_scratch) if rider is not None else None
    if comm is not None:
        comm.forward_step = rider.forward_step
    return ins + outs + scratch, comm


def _ride_before(comm, i, nt):
    if comm is not None:
        pl.when(i == 0)(comm.start)
        pl.when(i == (nt - 1 if comm.forward_step is None else min(comm.forward_step, nt - 1)))(comm.forward)


def _ride_after(comm, i, nt):
    if comm is not None:
        pl.when(i == nt - 1)(comm.finish)


def _extend(specs, rider, field):
    return list(specs) + (getattr(rider, field) if rider is not None else [])


def _comm_call(name, rider):
    def body(*refs):
        _, comm = _split_refs(refs, 0, 0, 0, rider)
        comm.start()
        comm.forward()
        comm.finish()

    return pl.pallas_call(body, name=name, in_specs=rider.in_specs, out_specs=rider.out_specs,
                          out_shape=rider.out_shape, scratch_shapes=rider.scratch,
                          compiler_params=pltpu.CompilerParams(vmem_limit_bytes=VMEM_LIMIT))(*rider.arrays)


N_CHIPS = 4


class _TwoLevel:
    def __init__(self, in_refs, out_refs, slots, *scratch):
        self.in_refs, self.out_refs, self.slots = in_refs, out_refs, slots
        self.n = n = len(in_refs)
        self.own_bufs, self.recv_bufs = scratch[:n], scratch[n:2 * n]
        self.swap_send, self.swap_recv, self.local_sems, self.chip_send, self.chip_recv = scratch[2 * n:]
        self.pos = _position()

    def _swap(self):
        x, y, c = self.pos
        return [pltpu.make_async_remote_copy(
            src_ref=self.slots[a](self.in_refs[a], 2 * q + 1 - c), dst_ref=self.recv_bufs[a].at[q],
            send_sem=self.swap_send.at[a, q], recv_sem=self.swap_recv.at[a, q],
            device_id=(x, y, 1 - c), device_id_type=MESH) for a in range(self.n) for q in range(N_CHIPS)]

    def _mine(self):
        c = self.pos[2]
        return [pltpu.make_async_copy(self.slots[a](self.in_refs[a], 2 * q + c), self.own_bufs[a].at[q],
                                      self.local_sems.at[a, q]) for a in range(self.n) for q in range(N_CHIPS)]

    def _to_chips(self):
        x, y, c = self.pos
        copies = []
        for k in range(1, N_CHIPS):
            px, py = x ^ (k >> 1), y ^ (k & 1)
            copies += [pltpu.make_async_remote_copy(
                src_ref=self.own_bufs[a].at[2 * px + py], dst_ref=self.out_refs[a].at[2 * x + y],
                send_sem=self.chip_send.at[a, k - 1], recv_sem=self.chip_recv.at[a, k - 1],
                device_id=(px, py, c), device_id_type=MESH) for a in range(self.n)]
        return copies

    def _own(self):
        x, y, _ = self.pos
        return [pltpu.make_async_copy(self.own_bufs[a].at[2 * x + y], self.out_refs[a].at[2 * x + y],
                                      self.local_sems.at[a, N_CHIPS]) for a in range(self.n)]

    def start(self):
        for cp in self._swap() + self._mine():
            cp.start()

    def forward(self):
        swap, mine = self._swap(), self._mine()
        for a in range(self.n):
            for q in range(N_CHIPS):
                mine[a * N_CHIPS + q].wait()
                swap[a * N_CHIPS + q].wait_recv()
                self.own_bufs[a][q] = (self.own_bufs[a][q].astype(F32)
                                       + self.recv_bufs[a][q].astype(F32)).astype(BF16)
        for cp in self._to_chips() + self._own():
            cp.start()

    def finish(self):
        to_chips = self._to_chips()
        for cp in to_chips:
            cp.wait_recv()
        for cp in to_chips + self._swap():
            cp.wait_send()
        for cp in self._own():
            cp.wait()


def _two_level_rider(sends, part_shapes, slots, forward_step=None):
    n = len(sends)
    bufs = [pltpu.VMEM((N_CHIPS,) + tuple(s), a.dtype) for s, a in zip(part_shapes, sends)]
    scratch = bufs + bufs + [pltpu.SemaphoreType.DMA((n, N_CHIPS)), pltpu.SemaphoreType.DMA((n, N_CHIPS)),
                             pltpu.SemaphoreType.DMA((n, N_CHIPS + 1)), pltpu.SemaphoreType.DMA((n, N_CHIPS - 1)),
                             pltpu.SemaphoreType.DMA((n, N_CHIPS - 1))]
    return _Rider(_TwoLevel, sends, [(N_CHIPS,) + tuple(s) for s in part_shapes], slots, scratch, forward_step)


class _Joined:
    def __init__(self, first, second):
        self.first, self.second = first, second

    def start(self):
        self.first.start()
        self.second.start()

    def forward(self):
        self.first.forward()
        self.second.forward()

    def finish(self):
        self.first.finish()
        self.second.finish()


class _JoinedRider:
    def __init__(self, first, second):
        self.first, self.second = first, second
        self.n = first.n + second.n
        self.arrays = first.arrays + second.arrays
        self.in_specs = first.in_specs + second.in_specs
        self.out_specs = first.out_specs + second.out_specs
        self.out_shape = first.out_shape + second.out_shape
        self.scratch = first.scratch + second.scratch
        self.forward_step = first.forward_step

    def bind(self, in_refs, out_refs, scratch):
        k, s = self.first.n, len(self.first.scratch)
        return _Joined(self.first.bind(in_refs[:k], out_refs[:k], scratch[:s]),
                       self.second.bind(in_refs[k:], out_refs[k:], scratch[s:]))


def _adamw(w, g, m, v):
    m = ADAM_B1 * m + (1.0 - ADAM_B1) * g
    v = ADAM_B2 * v + (1.0 - ADAM_B2) * (g * g)
    m_hat = m / (1.0 - ADAM_B1 ** ADAM_STEP)
    v_hat = v / (1.0 - ADAM_B2 ** ADAM_STEP)
    delta = -ADAM_LR * (m_hat / (jnp.sqrt(v_hat) + ADAM_EPS) + ADAM_WD * w)
    return delta, m, v


def _sum_parts(parts_ref, index=()):
    g = parts_ref[(0,) + index].astype(F32)
    for s in range(1, parts_ref.shape[0]):
        g = g + parts_ref[(s,) + index].astype(F32)
    return g


def _adamw_group_call(name, groups):
    k = len(groups)

    def body(*refs):
        ins, outs = refs[:4 * k], refs[4 * k:]
        for i in range(k):
            parts_ref, w_ref, m_ref, v_ref = ins[4 * i:4 * i + 4]
            g = _sum_parts(parts_ref)
            delta, m_new, v_new = _adamw(w_ref[...], g, m_ref[...], v_ref[...])
            for out_ref, value in zip(outs[4 * i:4 * i + 4], (g, delta, m_new, v_new)):
                out_ref[...] = value

    vmem = pl.BlockSpec(memory_space=pltpu.VMEM)
    res = pl.pallas_call(
        body, name=name, in_specs=[vmem] * (4 * k), out_specs=[vmem] * (4 * k),
        out_shape=[jax.ShapeDtypeStruct(grp[1].shape, F32) for grp in groups for _ in range(4)],
        compiler_params=pltpu.CompilerParams(vmem_limit_bytes=VMEM_LIMIT),
    )(*[a for grp in groups for a in grp])
    return [res[4 * i:4 * i + 4] for i in range(k)]


def _adamw_slabs_call(name, parts, w, m, v, rider=None):
    def main(parts_ref, w_ref, m_ref, v_ref, g_ref, delta_ref, m_out, v_out):
        g = _sum_parts(parts_ref)
        delta, m_new, v_new = _adamw(w_ref[...], g, m_ref[...], v_ref[...])
        g_ref[...] = g
        delta_ref[...] = delta
        m_out[...] = m_new
        v_out[...] = v_new

    def body(*refs):
        own, comm = _split_refs(refs, 4, 4, 0, rider)
        if comm is not None:
            comm.start()
        main(*own)
        if comm is not None:
            comm.forward()
            comm.finish()

    vmem = pl.BlockSpec(memory_space=pltpu.VMEM)
    return pl.pallas_call(
        body, name=name, in_specs=_extend([vmem] * 4, rider, "in_specs"),
        out_specs=_extend([vmem] * 4, rider, "out_specs"),
        out_shape=_extend([jax.ShapeDtypeStruct(w.shape, F32)] * 4, rider, "out_shape"),
        scratch_shapes=_extend([], rider, "scratch"),
        compiler_params=pltpu.CompilerParams(vmem_limit_bytes=VMEM_LIMIT),
    )(parts, w, m, v, *_extend([], rider, "arrays"))


WIDE_ROWS = 8
NARROW_ROWS = 40
NARROW_GKW_ROW = 8
NARROW_GKB_ROW = 24
NARROW_HW_ROW = 32
GROUP_SHARD = POOL_GROUP_DIM // N_DEV
KEY_SHARD = GLA_KEY_WIDTH // N_DEV
HEAD_V_SHARD = GLA_HEAD_V // N_DEV


def _small_adamw_call(wide, narrow, w, m, v):
    names = ("norm_w", "pool_scale", "final_norm_w", "pool_group_b", "gla_gk_w", "gla_gk_b", "gla_head_norm_w")
    where = {
        "norm_w": (0, slice(0, 2), slice(None)),
        "pool_scale": (0, slice(2, 3), slice(None)),
        "final_norm_w": (0, slice(3, 4), slice(None)),
        "pool_group_b": (1, slice(0, POOL_GROUPS), slice(0, GROUP_SHARD)),
        "gla_gk_w": (1, slice(NARROW_GKW_ROW, NARROW_GKW_ROW + GLA_GATE_RANK), slice(0, KEY_SHARD)),
        "gla_gk_b": (1, slice(NARROW_GKB_ROW, NARROW_GKB_ROW + 1), slice(0, KEY_SHARD)),
        "gla_head_norm_w": (1, slice(NARROW_HW_ROW, NARROW_HW_ROW + 1), slice(0, HEAD_V_SHARD)),
    }
    k = len(names)

    def body(*refs):
        parts = refs[0:2]
        w_refs, m_refs, v_refs = refs[2:2 + k], refs[2 + k:2 + 2 * k], refs[2 + 2 * k:2 + 3 * k]
        outs = refs[2 + 3 * k:]
        loss_ref = outs[0]
        loss_ref[...] = _sum_parts(parts[0], (slice(4, 5), slice(0, 1)))
        for i, name in enumerate(names):
            buf, rows, cols = where[name]
            g = _sum_parts(parts[buf], (rows, cols))
            delta, m_new, v_new = _adamw(w_refs[i][...], g, m_refs[i][...], v_refs[i][...])
            outs[1 + i][...] = g
            outs[1 + k + i][...] = delta
            outs[1 + 2 * k + i][...] = m_new
            outs[1 + 3 * k + i][...] = v_new

    vmem = pl.BlockSpec(memory_space=pltpu.VMEM)
    shapes = [jax.ShapeDtypeStruct(w[n].shape, F32) for n in names]
    res = pl.pallas_call(
        body, name="adamw_small", in_specs=[vmem] * (2 + 3 * k), out_specs=[vmem] * (1 + 4 * k),
        out_shape=[jax.ShapeDtypeStruct((1, 1), F32)] + shapes * 4,
    )(wide, narrow, *[w[n] for n in names], *[m[n] for n in names], *[v[n] for n in names])
    unzip = lambda j: dict(zip(names, res[1 + j * k:1 + (j + 1) * k]))
    return res[0], unzip(0), unzip(1), unzip(2), unzip(3)


def kernel(x, norm_w, pool_in_w, pool_group_w, pool_group_b, pool_scale, pool_out_w, gla_in_w, gla_gk_w, gla_gk_b, gla_head_norm_w, gla_out_w, final_norm_w, loss_target, m_norm_w, m_pool_in_w, m_pool_group_w, m_pool_group_b, m_pool_scale, m_pool_out_w, m_gla_in_w, m_gla_gk_w, m_gla_gk_b, m_gla_head_norm_w, m_gla_out_w, m_final_norm_w, v_norm_w, v_pool_in_w, v_pool_group_w, v_pool_group_b, v_pool_scale, v_pool_out_w, v_gla_in_w, v_gla_gk_w, v_gla_gk_b, v_gla_head_norm_w, v_gla_out_w, v_final_norm_w):
    w = dict(norm_w=norm_w, pool_in_w=pool_in_w, pool_group_w=pool_group_w, pool_group_b=pool_group_b,
             pool_scale=pool_scale, pool_out_w=pool_out_w, gla_in_w=gla_in_w, gla_gk_w=gla_gk_w, gla_gk_b=gla_gk_b,
             gla_head_norm_w=gla_head_norm_w, gla_out_w=gla_out_w, final_norm_w=final_norm_w)
    m = dict(norm_w=m_norm_w, pool_in_w=m_pool_in_w, pool_group_w=m_pool_group_w, pool_group_b=m_pool_group_b,
             pool_scale=m_pool_scale, pool_out_w=m_pool_out_w, gla_in_w=m_gla_in_w, gla_gk_w=m_gla_gk_w,
             gla_gk_b=m_gla_gk_b, gla_head_norm_w=m_gla_head_norm_w, gla_out_w=m_gla_out_w,
             final_norm_w=m_final_norm_w)
    v = dict(norm_w=v_norm_w, pool_in_w=v_pool_in_w, pool_group_w=v_pool_group_w, pool_group_b=v_pool_group_b,
             pool_scale=v_pool_scale, pool_out_w=v_pool_out_w, gla_in_w=v_gla_in_w, gla_gk_w=v_gla_gk_w,
             gla_gk_b=v_gla_gk_b, gla_head_norm_w=v_gla_head_norm_w, gla_out_w=v_gla_out_w,
             final_norm_w=v_final_norm_w)
    col_shard = GLA_IN_WIDTH // N_DEV
    row_shard = D_MODEL // N_DEV

    def lanes(a):
        return jnp.pad(a, [(0, 0)] * (a.ndim - 1) + [(0, LANES - a.shape[-1])])

    small_in = jnp.concatenate([lanes(pool_group_b[0]), lanes(gla_gk_b), lanes(gla_head_norm_w),
                                jnp.zeros((2, LANES), F32)], axis=0)
    in_cols = 2 * POOL_WIDTH // N_DEV
    pool_in, pool_gw, pool_out, small_all = _comm_call("pool_weights_all_gather", _gather_rider(
        [pool_in_w[0].astype(BF16), pool_group_w[0].astype(BF16), pool_out_w[0].astype(BF16), small_in],
        [(D_MODEL, 2 * POOL_WIDTH), (POOL_GROUPS, POOL_GROUP_DIM, POOL_GROUP_DIM), (POOL_WIDTH, D_MODEL),
         (N_DEV, 8, LANES)],
        [_dim1_slot(in_cols), _dim1_slot(GROUP_SHARD), _row_slot(row_shard), _lead_slot]))
    pool_gb = jnp.transpose(small_all[:, 0:POOL_GROUPS, :GROUP_SHARD], (1, 0, 2)).reshape(1, POOL_WIDTH)
    gla_gkb = small_all[:, POOL_GROUPS, :KEY_SHARD].reshape(1, GLA_KEY_WIDTH)
    gla_hw = jnp.tile(small_all[:, POOL_GROUPS + 1, :HEAD_V_SHARD].reshape(1, GLA_HEAD_V), (1, GLA_HEADS))
    nw0, nw1, wf = norm_w[0:1], norm_w[1:2], final_norm_w.reshape(1, D_MODEL)
    xs, target = x[0], loss_target[0]

    h1, p, gla_in_parts, gkw_parts, gla_out = _pool_fwd_call(
        xs, nw0, pool_in, pool_gw, pool_gb, pool_scale, pool_out, _gather_rider(
            [jnp.transpose(gla_in_w[0]).astype(BF16), gla_gk_w[0].astype(BF16), gla_out_w[0].astype(BF16)],
            [(N_DEV, col_shard, D_MODEL), (N_DEV, GLA_GATE_RANK, KEY_SHARD), (GLA_VALUE_WIDTH, D_MODEL)],
            [_lead_slot, _lead_slot, _row_slot(row_shard)]))
    gla_in = gla_in_parts.reshape(GLA_IN_WIDTH, D_MODEL)
    gla_gkw = jnp.pad(jnp.transpose(gkw_parts, (1, 0, 2)).reshape(GLA_GATE_RANK, GLA_KEY_WIDTH),
                      ((0, GLA_LOW_PAD - GLA_GATE_RANK), (0, 0)))
    dh2, proj, o, states, loss_part, dwf = _gla_fwd_call(h1, nw1, gla_in, gla_gkw, gla_gkb, gla_hw, gla_out, wf, target)

    dproj, d_gla_out, dhw, dgkw, dgkb = _gla_bwd_call(dh2, proj, o, states, gla_gkw, gla_gkb, gla_hw, gla_out)
    dh1, d_gla_in, dnw1, landed_gla_out = _inproj_bwd_call(
        "gla_in_bwd", dproj, h1, nw1, gla_in, dh2,
        _exchange_rider([d_gla_out], [(row_shard, D_MODEL)], [_row_slot(row_shard)]), transposed=True)
    slabs = col_shard * D_MODEL // (BF16_ROWS * LANES)
    gla_in_send = d_gla_in.reshape(N_DEV, slabs, BF16_ROWS, LANES)
    dp, d_pool_out, dgw, dgb, dsc, landed_gla_in = _pool_bwd_call(
        dh1, p, pool_gw, pool_gb, pool_scale, pool_out,
        _two_level_rider([gla_in_send], [(slabs, BF16_ROWS, LANES)], [_lead_slot], TWO_LEVEL_ADD_STEP))
    grad_x, d_pool_in, dnw0 = _inproj_bwd_call("pool_in_bwd", dp, xs, nw0, pool_in, dh1)

    wide = jnp.concatenate([
        dnw0, dnw1, dsc, dwf, jnp.pad(loss_part[0:1, 0:1], ((0, 0), (0, D_MODEL - 1))),
        jnp.zeros((WIDE_ROWS - 5, D_MODEL), F32)], axis=0)

    def rows8(a):
        return jnp.pad(lanes(a), ((0, 0), (0, -a.shape[1] % 8), (0, 0)))

    narrow = jnp.concatenate([
        rows8(jnp.transpose(dgb.reshape(POOL_GROUPS, N_DEV, GROUP_SHARD), (1, 0, 2))),
        rows8(jnp.transpose(dgkw[:GLA_GATE_RANK].reshape(GLA_GATE_RANK, N_DEV, KEY_SHARD), (1, 0, 2))),
        rows8(dgkb.reshape(N_DEV, 1, KEY_SHARD)),
        rows8(dhw.reshape(GLA_HEADS, GLA_HEAD_V).sum(axis=0).reshape(N_DEV, 1, HEAD_V_SHARD)),
    ], axis=1)
    last_exchange = _JoinedRider(
        _two_level_rider([d_pool_in, d_pool_out, dgw],
                         [(D_MODEL, in_cols), (row_shard, D_MODEL), (POOL_GROUPS, GROUP_SHARD, POOL_GROUP_DIM)],
                         [_dim1_slot(in_cols), _row_slot(row_shard), _dim1_slot(GROUP_SHARD)]),
        _exchange_rider([wide, narrow], [(WIDE_ROWS, D_MODEL), (NARROW_ROWS, LANES)],
                        [lambda ref, d: ref, _lead_slot]))

    res = {}
    as_slabs = lambda t: jnp.transpose(t[0]).reshape(slabs, BF16_ROWS, LANES)
    *outs, landed_pool_in, landed_pool_out, landed_gw, landed_wide, landed_narrow = _adamw_slabs_call(
        "adamw_gla_in_w", landed_gla_in, as_slabs(gla_in_w), as_slabs(m_gla_in_w), as_slabs(v_gla_in_w),
        last_exchange)
    res["gla_in_w"] = [jnp.transpose(t.reshape(col_shard, D_MODEL))[None] for t in outs]
    rest = [("pool_in_w", landed_pool_in, (D_MODEL, in_cols)),
            ("pool_group_w", landed_gw, (POOL_GROUPS * GROUP_SHARD, POOL_GROUP_DIM)),
            ("pool_out_w", landed_pool_out, (row_shard, D_MODEL)), ("gla_out_w", landed_gla_out, (row_shard, D_MODEL))]
    updates = _adamw_group_call("adamw_matrices", [
        (parts.reshape((parts.shape[0],) + shape), w[name].reshape(shape), m[name].reshape(shape),
         v[name].reshape(shape)) for name, parts, shape in rest])
    for (name, _, _), outs in zip(rest, updates):
        res[name] = [t.reshape(w[name].shape) for t in outs]
    small_shapes ={"norm_w": (2, D_MODEL), "pool_scale": (1, D_MODEL), "final_norm_w": (1, D_MODEL),
                    "pool_group_b": (POOL_GROUPS, GROUP_SHARD), "gla_gk_w": (GLA_GATE_RANK, KEY_SHARD),
                    "gla_gk_b": (1, KEY_SHARD), "gla_head_norm_w": (1, HEAD_V_SHARD)}
    as_small = lambda t: {n: t[n].reshape(s) for n, s in small_shapes.items()}
    loss, *small_outs = _small_adamw_call(landed_wide, landed_narrow, as_small(w), as_small(m), as_small(v))
    for name in small_shapes:
        res[name] = [t[name].reshape(w[name].shape) for t in small_outs]
    order = ("norm_w", "pool_in_w", "pool_group_w", "pool_group_b", "pool_scale", "pool_out_w", "gla_in_w",
             "gla_gk_w", "gla_gk_b", "gla_head_norm_w", "gla_out_w", "final_norm_w")
    return (loss.reshape(()), grad_x[None], *[res[n][0] for n in order], *[res[n][1] for n in order],
            *[res[n][2] for n in order], *[res[n][3] for n in order])
```

```python
import jax
import jax.numpy as jnp
from jax import lax
from jax.experimental import pallas as pl
from jax.experimental.pallas import tpu as pltpu

F32 = jnp.float32
BF16 = jnp.bfloat16
MESH = pl.DeviceIdType.MESH

N_DEV = 8
D_MODEL = 1024
POOL_WIDTH = 1024
POOL_GROUPS = 4
POOL_GROUP_DIM = 256
POOL_HALO = 16
GLA_HEADS = 4
GLA_HEAD_K = 128
GLA_HEAD_V = 256
GLA_KEY_WIDTH = 512
GLA_VALUE_WIDTH = 1024
GLA_GATE_RANK = 16
GLA_IN_WIDTH = 3088
GLA_IN_PAD = 3200
GLA_LOW_PAD = 128
GLA_QKVG_WIDTH = 3072
CHUNK = 64
GATE_NORMALIZER = 16.0
RMS_EPS = 1e-6
Q_SCALE = GLA_HEAD_K ** -0.5

ADAM_LR = 0.001
ADAM_B1 = 0.9
ADAM_B2 = 0.999
ADAM_EPS = 1e-08
ADAM_WD = 0.01
ADAM_STEP = 10

LANES = 128
BF16_ROWS = 16
VMEM_LIMIT = 56 * 1024 * 1024
ROW_TILE = 256
MATMUL_ROW_TILE = 512
GATHER_RELAY_STEP = 5
TWO_LEVEL_ADD_STEP = 1


def _dot_nn(a, b):
    return lax.dot_general(a, b, (((1,), (0,)), ((), ())), preferred_element_type=F32)


def _dot_nt(a, b):
    return lax.dot_general(a, b, (((1,), (1,)), ((), ())), preferred_element_type=F32)


def _dot_tn(a, b):
    return lax.dot_general(a, b, (((0,), (0,)), ((), ())), preferred_element_type=F32)


def _rms(x):
    rstd = lax.rsqrt(jnp.mean(x * x, axis=-1, keepdims=True) + RMS_EPS)
    return x * rstd, rstd


def _rms_bwd(dxhat, xhat, rstd):
    return rstd * (dxhat - xhat * jnp.mean(dxhat * xhat, axis=-1, keepdims=True))


def _sigmoid(x):
    return 1.0 / (1.0 + jnp.exp(-x))


def _params(sem=("arbitrary",)):
    return pltpu.CompilerParams(dimension_semantics=sem, vmem_limit_bytes=VMEM_LIMIT)


def _full(shape):
    return pl.BlockSpec(shape, lambda i: (0,) * len(shape))


def _const(shape):
    return pl.BlockSpec(shape, lambda i: (0,) * len(shape), pipeline_mode=pl.Buffered(1))


def _window_sums(ext, forward):
    n = ext.shape[0]
    outs = []
    for g in range(POOL_GROUPS):
        s = ext[:, g * POOL_GROUP_DIM:(g + 1) * POOL_GROUP_DIM]
        for k in range(g + 1):
            shift = (1 << k) if forward else n - (1 << k)
            s = s + pltpu.roll(s, shift, axis=0)
        outs.append(s[:n - POOL_HALO])
    return outs


def _inv_count(row0, tm):
    row = row0 + lax.broadcasted_iota(jnp.int32, (tm, 1), 0)
    return [1.0 / jnp.minimum(row + 1, 2 << g).astype(F32) for g in range(POOL_GROUPS)]


def _pool_mix(u, u_prev, row0, gw_ref, gb):
    tm = u.shape[0]
    sums = _window_sums(jnp.concatenate([u, u_prev], axis=0), True)
    inv = _inv_count(row0, tm)
    pooled, mixed = [], []
    for g in range(POOL_GROUPS):
        ug = u[:, g * POOL_GROUP_DIM:(g + 1) * POOL_GROUP_DIM]
        pg = (sums[g] * inv[g] - ug).astype(BF16)
        pooled.append(pg)
        mixed.append(_dot_nn(pg, gw_ref[g]))
    return pooled, jnp.concatenate(mixed, axis=1) + gb


def _pool_fwd_call(x, nw, w_in, gw, gb, sc, w_out, rider=None):
    seq = x.shape[0]
    tm = min(MATMUL_ROW_TILE, seq)
    nt = seq // tm

    def main(x_ref, nw_ref, win_ref, gw_ref, gb_ref, sc_ref, wout_ref, h_ref, p_ref, halo_ref):
        i = pl.program_id(0)

        @pl.when(i == 0)
        def _():
            halo_ref[...] = jnp.zeros_like(halo_ref)

        xt = x_ref[...]
        xhat, _ = _rms(xt)
        n = (xhat * nw_ref[...]).astype(BF16)
        p = _dot_nn(n, win_ref[...])
        p_ref[...] = p
        u = p[:, :POOL_WIDTH]
        gate = p[:, POOL_WIDTH:]
        _, mixed = _pool_mix(u, halo_ref[...], i * tm, gw_ref, gb_ref[...])
        halo_ref[...] = u[tm - POOL_HALO:, :]
        y = (mixed * sc_ref[...] * (gate * _sigmoid(gate))).astype(BF16)
        h_ref[...] = xt + _dot_nn(y, wout_ref[...])

    def body(*refs):
        own, comm = _split_refs(refs, 7, 2, 1, rider)
        _ride_before(comm, pl.program_id(0), nt)
        main(*own)
        _ride_after(comm, pl.program_id(0), nt)

    return pl.pallas_call(
        body, name="pool_fwd", grid=(nt,),
        in_specs=_extend([pl.BlockSpec((tm, D_MODEL), lambda i: (i, 0)), _const((1, D_MODEL)),
                          _const((D_MODEL, 2 * POOL_WIDTH)), _const((POOL_GROUPS, POOL_GROUP_DIM, POOL_GROUP_DIM)),
                          _const((1, POOL_WIDTH)), _const((1, POOL_WIDTH)), _const((POOL_WIDTH, D_MODEL))],
                         rider, "in_specs"),
        out_specs=_extend([pl.BlockSpec((tm, D_MODEL), lambda i: (i, 0)),
                           pl.BlockSpec((tm, 2 * POOL_WIDTH), lambda i: (i, 0))], rider, "out_specs"),
        out_shape=_extend([jax.ShapeDtypeStruct((seq, D_MODEL), F32),
                           jax.ShapeDtypeStruct((seq, 2 * POOL_WIDTH), F32)], rider, "out_shape"),
        scratch_shapes=_extend([pltpu.VMEM((POOL_HALO, POOL_WIDTH), F32)], rider, "scratch"),
        compiler_params=_params(),
    )(x, nw, w_in, gw, gb, sc, w_out, *_extend([], rider, "arrays"))


def _pool_bwd_call(dh, p, gw, gb, sc, w_out, rider=None):
    seq = dh.shape[0]
    tm = min(MATMUL_ROW_TILE, seq)
    nt = seq // tm
    halo_blocks = tm // POOL_HALO

    def main(dh_ref, p_ref, pprev_ref, gw_ref, gb_ref, sc_ref, wout_ref,
             dp_ref, dwout_hbm, dgw_hbm, dgb_ref, dsc_ref, carry_ref, dwout_acc, dgw_acc, dwout_stage, dgw_stage):
        i = pl.program_id(0)
        t = nt - 1 - i

        @pl.when(i == 0)
        def _():
            carry_ref[...] = jnp.zeros_like(carry_ref)
            dwout_acc[...] = jnp.zeros_like(dwout_acc)
            dgw_acc[...] = jnp.zeros_like(dgw_acc)
            dgb_ref[...] = jnp.zeros_like(dgb_ref)
            dsc_ref[...] = jnp.zeros_like(dsc_ref)

        dhb = dh_ref[...].astype(BF16)
        dy = _dot_nt(dhb, wout_ref[...])
        p = p_ref[...]
        u = p[:, :POOL_WIDTH]
        gate = p[:, POOL_WIDTH:]
        u_prev = jnp.where(t > 0, pprev_ref[:, :POOL_WIDTH], 0.0)
        pooled, mixed = _pool_mix(u, u_prev, t * tm, gw_ref, gb_ref[...])
        sg = _sigmoid(gate)
        silu = gate * sg
        sc = sc_ref[...]
        y = (mixed * sc * silu).astype(BF16)
        dwout_acc[...] += _dot_tn(y, dhb)
        dmixed = dy * sc * silu
        dsc_ref[...] += jnp.sum(dy * mixed * silu, axis=0, keepdims=True)
        dgate = dy * mixed * sc * (sg * (1.0 + gate * (1.0 - sg)))
        dgb_ref[...] += jnp.sum(dmixed, axis=0, keepdims=True)
        inv = _inv_count(t * tm, tm)
        dpooled, scaled = [], []
        for g in range(POOL_GROUPS):
            dmg = dmixed[:, g * POOL_GROUP_DIM:(g + 1) * POOL_GROUP_DIM].astype(BF16)
            dgw_acc[g] += _dot_tn(pooled[g], dmg)
            dpg = _dot_nt(dmg, gw_ref[g])
            dpooled.append(dpg)
            scaled.append(dpg * inv[g])
        r = jnp.concatenate(scaled, axis=1)
        sums = _window_sums(jnp.concatenate([r, carry_ref[...]], axis=0), False)
        carry_ref[...] = r[:POOL_HALO, :]
        du = jnp.concatenate([sums[g] - dpooled[g] for g in range(POOL_GROUPS)], axis=1)
        dp_ref[...] = jnp.concatenate([du, dgate], axis=1).astype(BF16)

        @pl.when(i == nt - 1)
        def _():
            dwout_stage[...] = dwout_acc[...].astype(BF16)
            dgw_stage[...] = dgw_acc[...].astype(BF16)
            pltpu.sync_copy(dwout_stage, dwout_hbm)
            pltpu.sync_copy(dgw_stage, dgw_hbm)

    def body(*refs):
        own, comm = _split_refs(refs, 7, 5, 5, rider)
        _ride_before(comm, pl.program_id(0), nt)
        main(*own)
        _ride_after(comm, pl.program_id(0), nt)

    rev = lambda i: (nt - 1 - i, 0)
    return pl.pallas_call(
        body, name="pool_bwd", grid=(nt,),
        in_specs=_extend([pl.BlockSpec((tm, D_MODEL), rev), pl.BlockSpec((tm, 2 * POOL_WIDTH), rev),
                          pl.BlockSpec((POOL_HALO, 2 * POOL_WIDTH),
                                       lambda i: (jnp.maximum((nt - 1 - i) * halo_blocks - 1, 0), 0)),
                          _const((POOL_GROUPS, POOL_GROUP_DIM, POOL_GROUP_DIM)), _const((1, POOL_WIDTH)),
                          _const((1, POOL_WIDTH)), _const((POOL_WIDTH, D_MODEL))], rider, "in_specs"),
        out_specs=_extend([pl.BlockSpec((tm, 2 * POOL_WIDTH), rev), pl.BlockSpec(memory_space=pl.ANY),
                           pl.BlockSpec(memory_space=pl.ANY), _full((1, POOL_WIDTH)), _full((1, POOL_WIDTH))],
                          rider, "out_specs"),
        out_shape=_extend([jax.ShapeDtypeStruct((seq, 2 * POOL_WIDTH), BF16),
                           jax.ShapeDtypeStruct((POOL_WIDTH, D_MODEL), BF16),
                           jax.ShapeDtypeStruct((POOL_GROUPS, POOL_GROUP_DIM, POOL_GROUP_DIM), BF16),
                           jax.ShapeDtypeStruct((1, POOL_WIDTH), F32), jax.ShapeDtypeStruct((1, POOL_WIDTH), F32)],
                          rider, "out_shape"),
        scratch_shapes=_extend([pltpu.VMEM((POOL_HALO, POOL_WIDTH), F32), pltpu.VMEM((POOL_WIDTH, D_MODEL), F32),
                                pltpu.VMEM((POOL_GROUPS, POOL_GROUP_DIM, POOL_GROUP_DIM), F32),
                                pltpu.VMEM((POOL_WIDTH, D_MODEL), BF16),
                                pltpu.VMEM((POOL_GROUPS, POOL_GROUP_DIM, POOL_GROUP_DIM), BF16)], rider, "scratch"),
        compiler_params=_params(),
    )(dh, p, p, gw, gb, sc, w_out, *_extend([], rider, "arrays"))


def _rows_then_zeros(ref, lo, hi, rows):
    part = ref[lo:hi, :]
    return jnp.concatenate([part, jnp.zeros((rows - (hi - lo), part.shape[1]), part.dtype)], axis=0)


def _inproj_bwd_call(name, dproj, h_in, nw, w_in, dres, rider=None, transposed=False):
    seq = h_in.shape[0]
    width = dproj.shape[1]
    w_shape = tuple(w_in.shape)
    acc_shape = (width, D_MODEL) if transposed else w_shape
    whole = w_shape[0] // LANES * LANES
    tm = min(MATMUL_ROW_TILE, seq)
    nt = seq // tm

    def main(dproj_ref, h_ref, nw_ref, win_ref, dres_ref, dh_ref, dw_hbm, dnw_ref, dw_acc, dw_stage):
        i = pl.program_id(0)

        @pl.when(i == 0)
        def _():
            dw_acc[...] = jnp.zeros_like(dw_acc)
            dnw_ref[...] = jnp.zeros_like(dnw_ref)

        dpb = dproj_ref[...]
        if transposed:
            dn = _dot_nn(dpb[:, :whole], win_ref[0:whole, :])
            if whole < w_shape[0]:
                dn = dn + _dot_nn(dpb[:, whole:], _rows_then_zeros(win_ref, whole, w_shape[0], width - whole))
        else:
            dn = _dot_nt(dpb, win_ref[...])
        xhat, rstd = _rms(h_ref[...])
        nw_row = nw_ref[...]
        n = (xhat * nw_row).astype(BF16)
        dw_acc[...] += _dot_tn(dpb, n) if transposed else _dot_tn(n, dpb)
        dnw_ref[...] += jnp.sum(dn * xhat, axis=0, keepdims=True)
        dh_ref[...] = _rms_bwd(dn * nw_row, xhat, rstd) + dres_ref[...]

        @pl.when(i == nt - 1)
        def _():
            dw_stage[...] = dw_acc[...].astype(BF16)
            pltpu.sync_copy(dw_stage.at[pl.ds(0, w_shape[0])], dw_hbm)

    def body(*refs):
        own, comm = _split_refs(refs, 5, 3, 2, rider)
        _ride_before(comm, pl.program_id(0), nt)
        main(*own)
        _ride_after(comm, pl.program_id(0), nt)

    row = lambda i: (i, 0)
    return pl.pallas_call(
        body, name=name, grid=(nt,),
        in_specs=_extend([pl.BlockSpec((tm, width), row), pl.BlockSpec((tm, D_MODEL), row), _const((1, D_MODEL)),
                          _const(w_shape), pl.BlockSpec((tm, D_MODEL), row)], rider, "in_specs"),
        out_specs=_extend([pl.BlockSpec((tm, D_MODEL), row), pl.BlockSpec(memory_space=pl.ANY),
                           _full((1, D_MODEL))], rider, "out_specs"),
        out_shape=_extend([jax.ShapeDtypeStruct((seq, D_MODEL), F32), jax.ShapeDtypeStruct(w_shape, BF16),
                           jax.ShapeDtypeStruct((1, D_MODEL), F32)], rider, "out_shape"),
        scratch_shapes=_extend([pltpu.VMEM(acc_shape, F32), pltpu.VMEM(acc_shape, BF16)], rider, "scratch"),
        compiler_params=_params(),
    )(dproj, h_in, nw, w_in, dres, *_extend([], rider, "arrays"))


def _chunk_scan(x, reverse):
    n = x.shape[0]
    pos = lax.broadcasted_iota(jnp.int32, (n, 1), 0) & (CHUNK - 1)
    k = 1
    while k < CHUNK:
        if reverse:
            x = x + jnp.where(pos < CHUNK - k, pltpu.roll(x, n - k, axis=0), 0.0)
        else:
            x = x + jnp.where(pos >= k, pltpu.roll(x, k, axis=0), 0.0)
        k *= 2
    return x


def _chunk_rows(j):
    return slice(j * CHUNK, (j + 1) * CHUNK)


def _kcols(h):
    return slice(h * GLA_HEAD_K, (h + 1) * GLA_HEAD_K)


def _vcols(h):
    return slice(h * GLA_HEAD_V, (h + 1) * GLA_HEAD_V)


def _chunk_masks(tm):
    idx_t = lax.broadcasted_iota(jnp.int32, (tm, tm), 0)
    idx_s = lax.broadcasted_iota(jnp.int32, (tm, tm), 1)
    same_chunk = (idx_t ^ idx_s) < CHUNK
    return same_chunk & (idx_t >= idx_s), same_chunk & (idx_t < idx_s)


class _GlaTerms:
    def __init__(self, kc, q, k, v, low_b, gkw_ref, gkb_ref, masks):
        tm = q.shape[0]
        self.q = q * Q_SCALE
        self.k = k
        self.z = _dot_nn(low_b, gkw_ref[:, kc]) + gkb_ref[:, kc]
        log_g = (jnp.minimum(self.z, 0.0) - jnp.log(1.0 + jnp.exp(-jnp.abs(self.z)))) / GATE_NORMALIZER
        self.c = _chunk_scan(log_g, False)
        is_last = lax.broadcasted_iota(jnp.int32, (CHUNK, 1), 0) == CHUNK - 1
        self.c_last = [jnp.sum(jnp.where(is_last, self.c[_chunk_rows(j), :], 0.0), axis=0, keepdims=True)
                       for j in range(tm // CHUNK)]
        c_last_rows = jnp.concatenate([jnp.broadcast_to(r, (CHUNK, r.shape[1])) for r in self.c_last], axis=0)
        self.e_pos = jnp.exp(self.c)
        self.e_neg = jnp.exp(-self.c)
        self.e_rest = jnp.exp(c_last_rows - self.c)
        self.a_b = (self.q * self.e_pos).astype(BF16)
        self.b_b = (self.k * self.e_neg).astype(BF16)
        self.cn_b = (self.q * self.e_neg).astype(BF16)
        self.dp_b = (self.k * self.e_pos).astype(BF16)
        self.kd_b = (self.k * self.e_rest).astype(BF16)
        self.v_b = v.astype(BF16)
        self.lower, self.upper = masks

    def scores(self, kc=slice(None)):
        fwd = _dot_nt(self.a_b[:, kc], self.b_b[:, kc])
        bwd = _dot_nt(self.cn_b[:, kc], self.dp_b[:, kc])
        return jnp.where(self.lower, fwd, jnp.where(self.upper, bwd, 0.0)).astype(BF16)


def _gla_fwd_call(h1, nw, w_in, gkw, gkb, hw, w_out, wf, target):
    seq = h1.shape[0]
    tm = ROW_TILE
    nt = seq // tm
    cpt = tm // CHUNK
    n_chunks = seq // CHUNK

    def body(h_ref, nw_ref, win_ref, gkw_ref, gkb_ref, hw_ref, wout_ref, wf_ref, tgt_ref,
             dh2_ref, proj_ref, o_ref, st_ref, loss_ref, dwf_ref, state_ref):
        i = pl.program_id(0)

        @pl.when(i == 0)
        def _():
            state_ref[...] = jnp.zeros_like(state_ref)
            loss_ref[...] = jnp.zeros_like(loss_ref)
            dwf_ref[...] = jnp.zeros_like(dwf_ref)

        ht = h_ref[...]
        xhat, _ = _rms(ht)
        n = (xhat * nw_ref[...]).astype(BF16)
        sections = {}
        for name, lo, hi in (("low", GLA_QKVG_WIDTH, GLA_IN_PAD), ("qk", 0, 2 * GLA_KEY_WIDTH),
                             ("v", 2 * GLA_KEY_WIDTH, GLA_QKVG_WIDTH - GLA_VALUE_WIDTH),
                             ("gate", GLA_QKVG_WIDTH - GLA_VALUE_WIDTH, GLA_QKVG_WIDTH)):
            rows = (win_ref[lo:hi, :] if hi <= GLA_IN_WIDTH
                    else _rows_then_zeros(win_ref, lo, GLA_IN_WIDTH, hi - lo))
            sections[name] = _dot_nt(n, rows)
            proj_ref[:, lo:hi] = sections[name]
        low_b = sections["low"].astype(BF16)
        masks = _chunk_masks(tm)
        on_heads = []
        for h in range(GLA_HEADS):
            kc, vc = _kcols(h), _vcols(h)
            g = _GlaTerms(kc, sections["qk"][:, kc], sections["qk"][:, GLA_KEY_WIDTH:][:, kc], sections["v"][:, vc],
                          low_b, gkw_ref, gkb_ref, masks)
            srows = slice(h * GLA_HEAD_V, (h + 1) * GLA_HEAD_V)
            o_intra = _dot_nn(g.scores(), g.v_b)
            state = state_ref[srows, :]
            o_rows = []
            for j in range(cpt):
                r = _chunk_rows(j)
                st_ref[j, srows, :] = state
                o_rows.append(o_intra[r] + _dot_nt(g.a_b[r], state.astype(BF16)))
                decay = jnp.exp(g.c_last[j])
                state = state * decay + _dot_tn(g.v_b[r], g.kd_b[r])
            state_ref[srows, :] = state
            o_head = jnp.concatenate(o_rows, axis=0)
            o_ref[:, vc] = o_head
            on_heads.append(_rms(o_head)[0])
        gate = sections["gate"]
        on = jnp.concatenate(on_heads, axis=1) * hw_ref[...]
        y = (on * (gate * _sigmoid(gate))).astype(BF16)
        h2 = ht + _dot_nn(y, wout_ref[...])
        xhat2, rstd2 = _rms(h2)
        wf_row = wf_ref[...]
        err = xhat2 * wf_row - tgt_ref[...]
        loss_ref[...] += 0.5 * jnp.sum(err * err) / D_MODEL
        dout = err * (1.0 / D_MODEL)
        dwf_ref[...] += jnp.sum(dout * xhat2, axis=0, keepdims=True)
        dh2_ref[...] = _rms_bwd(dout * wf_row, xhat2, rstd2)

    row = lambda i: (i, 0)
    return pl.pallas_call(
        body, name="gla_fwd", grid=(nt,),
        in_specs=[pl.BlockSpec((tm, D_MODEL), row), _const((1, D_MODEL)), _const((GLA_IN_WIDTH, D_MODEL)),
                  _const((GLA_LOW_PAD, GLA_KEY_WIDTH)), _const((1, GLA_KEY_WIDTH)), _const((1, GLA_VALUE_WIDTH)),
                  _const((GLA_VALUE_WIDTH, D_MODEL)), _const((1, D_MODEL)), pl.BlockSpec((tm, D_MODEL), row)],
        out_specs=[pl.BlockSpec((tm, D_MODEL), row), pl.BlockSpec((tm, GLA_IN_PAD), row),
                   pl.BlockSpec((tm, GLA_VALUE_WIDTH), row),
                   pl.BlockSpec((cpt, GLA_VALUE_WIDTH, GLA_HEAD_K), lambda i: (i, 0, 0)),
                   _full((8, LANES)), _full((1, D_MODEL))],
        out_shape=[jax.ShapeDtypeStruct((seq, D_MODEL), F32), jax.ShapeDtypeStruct((seq, GLA_IN_PAD), F32),
                   jax.ShapeDtypeStruct((seq, GLA_VALUE_WIDTH), F32),
                   jax.ShapeDtypeStruct((n_chunks, GLA_VALUE_WIDTH, GLA_HEAD_K), F32),
                   jax.ShapeDtypeStruct((8, LANES), F32), jax.ShapeDtypeStruct((1, D_MODEL), F32)],
        scratch_shapes=[pltpu.VMEM((GLA_VALUE_WIDTH, GLA_HEAD_K), F32)],
        compiler_params=_params(),
    )(h1, nw, w_in, gkw, gkb, hw, w_out, wf, target)


def _gla_bwd_call(dh2, proj, o, states, gkw, gkb, hw, w_out):
    seq = dh2.shape[0]
    tm = ROW_TILE
    nt = seq // tm
    cpt = tm // CHUNK

    def body(dh_ref, proj_ref, o_ref, st_ref, gkw_ref, gkb_ref, hw_ref, wout_ref,
             dproj_ref, dwout_hbm, dhw_ref, dgkw_ref, dgkb_ref, dstate_ref, dwout_acc, dwout_stage):
        i = pl.program_id(0)

        @pl.when(i == 0)
        def _():
            dstate_ref[...] = jnp.zeros_like(dstate_ref)
            dwout_acc[...] = jnp.zeros_like(dwout_acc)
            dhw_ref[...] = jnp.zeros_like(dhw_ref)
            dgkw_ref[...] = jnp.zeros_like(dgkw_ref)
            dgkb_ref[...] = jnp.zeros_like(dgkb_ref)

        dhb = dh_ref[...].astype(BF16)
        dy = _dot_nt(dhb, wout_ref[...])
        v0, g0 = 2 * GLA_KEY_WIDTH, GLA_QKVG_WIDTH - GLA_VALUE_WIDTH
        gate = proj_ref[:, g0:GLA_QKVG_WIDTH]
        low_b = proj_ref[:, GLA_QKVG_WIDTH:].astype(BF16)
        o = o_ref[...]
        hw_row = hw_ref[...]
        sg = _sigmoid(gate)
        silu = gate * sg
        don = dy * silu
        on_parts, do_parts, dhw_parts = [], [], []
        for h in range(GLA_HEADS):
            vc = _vcols(h)
            xh, rs = _rms(o[:, vc])
            on_parts.append(xh * hw_row[:, vc])
            dhw_parts.append(jnp.sum(don[:, vc] * xh, axis=0, keepdims=True))
            do_parts.append(_rms_bwd(don[:, vc] * hw_row[:, vc], xh, rs).astype(BF16))
        on = jnp.concatenate(on_parts, axis=1)
        dwout_acc[...] += _dot_tn((on * silu).astype(BF16), dhb)
        dhw_ref[...] += jnp.concatenate(dhw_parts, axis=1)
        dproj_ref[:, g0:GLA_QKVG_WIDTH] = (dy * on * (sg * (1.0 + gate * (1.0 - sg)))).astype(BF16)

        last_row = lax.broadcasted_iota(jnp.int32, (CHUNK, 1), 0) == CHUNK - 1
        g = _GlaTerms(slice(0, GLA_KEY_WIDTH), proj_ref[:, :GLA_KEY_WIDTH], proj_ref[:, GLA_KEY_WIDTH:v0],
                      proj_ref[:, v0:g0], low_b, gkw_ref, gkb_ref, _chunk_masks(tm))
        dc_h = []
        for h in range(GLA_HEADS):
            kc, vc = _kcols(h), _vcols(h)
            k_cols = slice(GLA_KEY_WIDTH + kc.start, GLA_KEY_WIDTH + kc.stop)
            v_cols = slice(v0 + vc.start, v0 + vc.stop)
            do_h = do_parts[h]
            srows = slice(h * GLA_HEAD_V, (h + 1) * GLA_HEAD_V)
            scores = g.scores(kc)
            dscores = _dot_nt(do_h, g.v_b[:, vc])
            dfwd = jnp.where(g.lower, dscores, 0.0).astype(BF16)
            dbwd = jnp.where(g.upper, dscores, 0.0).astype(BF16)
            dv_intra = _dot_tn(scores, do_h)
            da_intra = _dot_nn(dfwd, g.b_b[:, kc])
            db = _dot_tn(dfwd, g.a_b[:, kc])
            dcn = _dot_nn(dbwd, g.dp_b[:, kc])
            ddp = _dot_tn(dbwd, g.cn_b[:, kc])
            dstate = dstate_ref[srows, :]
            da_rows, dkd_rows, dv_rows, dcl_rows = [None] * cpt, [None] * cpt, [None] * cpt, [None] * cpt
            for j in reversed(range(cpt)):
                r = _chunk_rows(j)
                state = st_ref[j, srows, :]
                dstate_b = dstate.astype(BF16)
                do_c = do_h[r]
                dv_rows[j] = dv_intra[r] + _dot_nt(g.kd_b[r, kc], dstate_b)
                da_rows[j] = da_intra[r] + _dot_nn(do_c, state.astype(BF16))
                dkd = _dot_nn(g.v_b[r, vc], dstate_b) * g.e_rest[r, kc]
                dkd_rows[j] = dkd
                decay = jnp.exp(g.c_last[j][:, kc])
                dc_last = (jnp.sum(dkd * g.k[r, kc], axis=0, keepdims=True)
                           + decay * jnp.sum(state * dstate, axis=0, keepdims=True))
                dcl_rows[j] = jnp.where(last_row, dc_last, 0.0)
                dstate = _dot_tn(do_c, g.a_b[r, kc]) + dstate * decay
            dstate_ref[srows, :] = dstate
            da = jnp.concatenate(da_rows, axis=0)
            dkd = jnp.concatenate(dkd_rows, axis=0)
            dproj_ref[:, v_cols] = jnp.concatenate(dv_rows, axis=0).astype(BF16)
            q_up, q_down = da * g.e_pos[:, kc], dcn * g.e_neg[:, kc]
            k_up, k_down = ddp * g.e_pos[:, kc], db * g.e_neg[:, kc] + dkd
            dproj_ref[:, kc] = (Q_SCALE * (q_up + q_down)).astype(BF16)
            dproj_ref[:, k_cols] = (k_up + k_down).astype(BF16)
            dc_h.append(g.q[:, kc] * (q_up - q_down) + g.k[:, kc] * (k_up - k_down)
                        + jnp.concatenate(dcl_rows, axis=0))
        dz = _chunk_scan(jnp.concatenate(dc_h, axis=1), True) * (1.0 / GATE_NORMALIZER) * (1.0 - _sigmoid(g.z))
        dzb = dz.astype(BF16)
        dgkb_ref[...] += jnp.sum(dz, axis=0, keepdims=True)
        dgkw_ref[...] += _dot_tn(low_b, dzb)
        dproj_ref[:, GLA_QKVG_WIDTH:] = _dot_nt(dzb, gkw_ref[...]).astype(BF16)

        @pl.when(i == nt - 1)
        def _():
            dwout_stage[...] = dwout_acc[...].astype(BF16)
            pltpu.sync_copy(dwout_stage, dwout_hbm)

    rev = lambda i: (nt - 1 - i, 0)
    return pl.pallas_call(
        body, name="gla_bwd", grid=(nt,),
        in_specs=[pl.BlockSpec((tm, D_MODEL), rev), pl.BlockSpec((tm, GLA_IN_PAD), rev),
                  pl.BlockSpec((tm, GLA_VALUE_WIDTH), rev),
                  pl.BlockSpec((cpt, GLA_VALUE_WIDTH, GLA_HEAD_K), lambda i: (nt - 1 - i, 0, 0)),
                  _const((GLA_LOW_PAD, GLA_KEY_WIDTH)), _const((1, GLA_KEY_WIDTH)), _const((1, GLA_VALUE_WIDTH)),
                  _const((GLA_VALUE_WIDTH, D_MODEL))],
        out_specs=[pl.BlockSpec((tm, GLA_IN_PAD), rev), pl.BlockSpec(memory_space=pl.ANY),
                   _full((1, GLA_VALUE_WIDTH)), _full((GLA_LOW_PAD, GLA_KEY_WIDTH)), _full((1, GLA_KEY_WIDTH))],
        out_shape=[jax.ShapeDtypeStruct((seq, GLA_IN_PAD), BF16), jax.ShapeDtypeStruct((GLA_VALUE_WIDTH, D_MODEL), BF16),
                   jax.ShapeDtypeStruct((1, GLA_VALUE_WIDTH), F32), jax.ShapeDtypeStruct((GLA_LOW_PAD, GLA_KEY_WIDTH), F32),
                   jax.ShapeDtypeStruct((1, GLA_KEY_WIDTH), F32)],
        scratch_shapes=[pltpu.VMEM((GLA_VALUE_WIDTH, GLA_HEAD_K), F32), pltpu.VMEM((GLA_VALUE_WIDTH, D_MODEL), F32),
                        pltpu.VMEM((GLA_VALUE_WIDTH, D_MODEL), BF16)],
        compiler_params=_params(),
    )(dh2, proj, o, states, gkw, gkb, hw, w_out)


def _position():
    return lax.axis_index("x"), lax.axis_index("y"), lax.axis_index("c")


def _lead_slot(ref, d):
    return ref.at[d]


def _row_slot(rows):
    return lambda ref, d: ref.at[pl.ds(pl.multiple_of(d * rows, rows), rows)]


def _dim1_slot(size):
    return lambda ref, d: ref.at[:, pl.ds(pl.multiple_of(d * size, size), size)]


class _Gather:
    def __init__(self, in_refs, out_refs, slots, send_sems, recv_sems, local_sems):
        self.in_refs, self.out_refs, self.slots = in_refs, out_refs, slots
        self.send_sems, self.recv_sems, self.local_sems = send_sems, recv_sems, local_sems
        self.n = len(in_refs)
        x, y, c = _position()
        self.c = c
        self.me, self.sibling = (x, y, c), (x, y, 1 - c)
        self.near = [(1 - x, y), (x, 1 - y)]
        self.diagonal = (1 - x, 1 - y)
        self.relay_from = (x ^ c, y ^ (1 - c))
        self.relay_to = (x ^ (1 - c), y ^ c)

    def _copy(self, a, k, block, to, from_input=False):
        part = self.slots[a](self.out_refs[a], 4 * block[0] + 2 * block[1] + block[2])
        return pltpu.make_async_remote_copy(
            src_ref=self.in_refs[a] if from_input else part, dst_ref=part,
            send_sem=self.send_sems.at[a, k], recv_sem=self.recv_sems.at[a, k], device_id=to, device_id_type=MESH)

    def _mine(self):
        return [pltpu.make_async_copy(self.in_refs[a], self.slots[a](self.out_refs[a], 4 * self.me[0] + 2 * self.me[1]
                                                                    + self.me[2]), self.local_sems.at[a])
                for a in range(self.n)]

    def _first(self):
        first = [self._copy(a, 0, self.me, self.sibling, True) for a in range(self.n)]
        return first + [self._copy(a, 1 + j, self.me, (*chip, self.c), True)
                        for j, chip in enumerate(self.near) for a in range(self.n)]

    def _relayed(self):
        return [self._copy(a, 3, (*self.relay_from, self.c), (*self.relay_to, self.c)) for a in range(self.n)]

    def _passed(self, j):
        chip = self.near[j] if j < 2 else self.diagonal
        return [self._copy(a, 4 + j, (*chip, self.c), self.sibling) for a in range(self.n)]

    def start(self):
        for cp in self._mine() + self._first():
            cp.start()

    def forward(self):
        for j, chip in enumerate(self.near):
            for a in range(self.n):
                self._copy(a, 1 + j, (*chip, self.c), self.me).wait_recv()
        for cp in self._relayed() + self._passed(0) + self._passed(1):
            cp.start()

    def finish(self):
        for a in range(self.n):
            self._copy(a, 3, (*self.diagonal, self.c), self.me).wait_recv()
        for cp in self._passed(2):
            cp.start()
        for a in range(self.n):
            self._copy(a, 0, self.sibling, self.me).wait_recv()
        for j, chip in enumerate(self.near + [self.diagonal]):
            for a in range(self.n):
                self._copy(a, 4 + j, (*chip, 1 - self.c), self.me).wait_recv()
        for cp in self._first() + self._relayed() + self._passed(0) + self._passed(1) + self._passed(2):
            cp.wait_send()
        for cp in self._mine():
            cp.wait()


class _Exchange:
    def __init__(self, in_refs, out_refs, slots, send_sems, recv_sems, local_sems):
        self.in_refs, self.out_refs, self.slots = in_refs, out_refs, slots
        self.send_sems, self.recv_sems, self.local_sems = send_sems, recv_sems, local_sems
        self.n = len(in_refs)
        self.pos = _position()

    def _copies(self):
        x, y, c = self.pos
        me = 4 * x + 2 * y + c
        mine = [pltpu.make_async_copy(self.slots[a](self.in_refs[a], me), self.out_refs[a].at[me],
                                      self.local_sems.at[a]) for a in range(self.n)]
        remote = []
        for k in range(1, N_DEV):
            px, py, pc = x ^ (k >> 2), y ^ ((k >> 1) & 1), c ^ (k & 1)
            for a in range(self.n):
                remote.append(pltpu.make_async_remote_copy(
                    src_ref=self.slots[a](self.in_refs[a], 4 * px + 2 * py + pc), dst_ref=self.out_refs[a].at[me],
                    send_sem=self.send_sems.at[a, k - 1], recv_sem=self.recv_sems.at[a, k - 1],
                    device_id=(px, py, pc), device_id_type=MESH))
        return mine, remote

    def start(self):
        mine, remote = self._copies()
        for cp in mine + remote:
            cp.start()

    def forward(self):
        pass

    def finish(self):
        mine, remote = self._copies()
        for cp in remote:
            cp.wait_recv()
        for cp in remote:
            cp.wait_send()
        for cp in mine:
            cp.wait()


class _Rider:
    def __init__(self, kind, arrays, out_shapes, slots, scratch=None, forward_step=None):
        self.kind, self.arrays, self.slots = kind, list(arrays), slots
        self.n = len(self.arrays)
        hbm = pl.BlockSpec(memory_space=pl.ANY)
        self.in_specs = [hbm] * self.n
        self.out_specs = [hbm] * self.n
        self.out_shape = [jax.ShapeDtypeStruct(tuple(s), a.dtype) for s, a in zip(out_shapes, self.arrays)]
        self.scratch = scratch if scratch is not None else [
            pltpu.SemaphoreType.DMA((self.n, 7)), pltpu.SemaphoreType.DMA((self.n, 7)),
            pltpu.SemaphoreType.DMA((self.n,))]
        self.forward_step = forward_step

    def bind(self, in_refs, out_refs, scratch):
        return self.kind(in_refs, out_refs, self.slots, *scratch)


def _gather_rider(shards, full_shapes, slots, forward_step=None):
    return _Rider(_Gather, shards, full_shapes, slots, None, forward_step)


def _exchange_rider(sends, part_shapes, slots):
    return _Rider(_Exchange, sends, [(N_DEV,) + tuple(s) for s in part_shapes], slots)


def _split_refs(refs, n_in, n_out, n_scratch, rider):
    k = rider.n if rider is not None else 0
    ins, r_ins = refs[:n_in], refs[n_in:n_in + k]
    outs, r_outs = refs[n_in + k:n_in + k + n_out], refs[n_in + k + n_out:n_in + 2 * k + n_out]
    rest = refs[n_in + 2 * k + n_out:]
    scratch, r_scratch = rest[:n_scratch], rest[n_scratch:]
    comm = rider.bind(r_ins, r_outs, r_scratch) if rider is not None else None
    if comm is not None:
        comm.forward_step = rider.forward_step
    return ins + outs + scratch, comm


def _ride_before(comm, i, nt):
    if comm is not None:
        pl.when(i == 0)(comm.start)
        pl.when(i == (nt - 1 if comm.forward_step is None else min(comm.forward_step, nt - 1)))(comm.forward)


def _ride_after(comm, i, nt):
    if comm is not None:
        pl.when(i == nt - 1)(comm.finish)


def _extend(specs, rider, field):
    return list(specs) + (getattr(rider, field) if rider is not None else [])


def _comm_call(name, rider):
    def body(*refs):
        _, comm = _split_refs(refs, 0, 0, 0, rider)
        comm.start()
        comm.forward()
        comm.finish()

    return pl.pallas_call(body, name=name, in_specs=rider.in_specs, out_specs=rider.out_specs,
                          out_shape=rider.out_shape, scratch_shapes=rider.scratch,
                          compiler_params=pltpu.CompilerParams(vmem_limit_bytes=VMEM_LIMIT))(*rider.arrays)


N_CHIPS = 4


class _TwoLevel:
    def __init__(self, in_refs, out_refs, slots, *scratch):
        self.in_refs, self.out_refs, self.slots = in_refs, out_refs, slots
        self.n = n = len(in_refs)
        self.own_bufs, self.recv_bufs = scratch[:n], scratch[n:2 * n]
        self.swap_send, self.swap_recv, self.local_sems, self.chip_send, self.chip_recv = scratch[2 * n:]
        self.pos = _position()

    def _swap(self):
        x, y, c = self.pos
        return [pltpu.make_async_remote_copy(
            src_ref=self.slots[a](self.in_refs[a], 2 * q + 1 - c), dst_ref=self.recv_bufs[a].at[q],
            send_sem=self.swap_send.at[a, q], recv_sem=self.swap_recv.at[a, q],
            device_id=(x, y, 1 - c), device_id_type=MESH) for a in range(self.n) for q in range(N_CHIPS)]

    def _mine(self):
        c = self.pos[2]
        return [pltpu.make_async_copy(self.slots[a](self.in_refs[a], 2 * q + c), self.own_bufs[a].at[q],
                                      self.local_sems.at[a, q]) for a in range(self.n) for q in range(N_CHIPS)]

    def _to_chips(self):
        x, y, c = self.pos
        copies = []
        for k in range(1, N_CHIPS):
            px, py = x ^ (k >> 1), y ^ (k & 1)
            copies += [pltpu.make_async_remote_copy(
                src_ref=self.own_bufs[a].at[2 * px + py], dst_ref=self.out_refs[a].at[2 * x + y],
                send_sem=self.chip_send.at[a, k - 1], recv_sem=self.chip_recv.at[a, k - 1],
                device_id=(px, py, c), device_id_type=MESH) for a in range(self.n)]
        return copies

    def _own(self):
        x, y, _ = self.pos
        return [pltpu.make_async_copy(self.own_bufs[a].at[2 * x + y], self.out_refs[a].at[2 * x + y],
                                      self.local_sems.at[a, N_CHIPS]) for a in range(self.n)]

    def start(self):
        for cp in self._swap() + self._mine():
            cp.start()

    def forward(self):
        swap, mine = self._swap(), self._mine()
        for a in range(self.n):
            for q in range(N_CHIPS):
                mine[a * N_CHIPS + q].wait()
                swap[a * N_CHIPS + q].wait_recv()
                self.own_bufs[a][q] = (self.own_bufs[a][q].astype(F32)
                                       + self.recv_bufs[a][q].astype(F32)).astype(BF16)
        for cp in self._to_chips() + self._own():
            cp.start()

    def finish(self):
        to_chips = self._to_chips()
        for cp in to_chips:
            cp.wait_recv()
        for cp in to_chips + self._swap():
            cp.wait_send()
        for cp in self._own():
            cp.wait()


def _two_level_rider(sends, part_shapes, slots, forward_step=None):
    n = len(sends)
    bufs = [pltpu.VMEM((N_CHIPS,) + tuple(s), a.dtype) for s, a in zip(part_shapes, sends)]
    scratch = bufs + bufs + [pltpu.SemaphoreType.DMA((n, N_CHIPS)), pltpu.SemaphoreType.DMA((n, N_CHIPS)),
                             pltpu.SemaphoreType.DMA((n, N_CHIPS + 1)), pltpu.SemaphoreType.DMA((n, N_CHIPS - 1)),
                             pltpu.SemaphoreType.DMA((n, N_CHIPS - 1))]
    return _Rider(_TwoLevel, sends, [(N_CHIPS,) + tuple(s) for s in part_shapes], slots, scratch, forward_step)


class _Joined:
    def __init__(self, first, second):
        self.first, self.second = first, second

    def start(self):
        self.first.start()
        self.second.start()

    def forward(self):
        self.first.forward()
        self.second.forward()

    def finish(self):
        self.first.finish()
        self.second.finish()


class _JoinedRider:
    def __init__(self, first, second):
        self.first, self.second = first, second
        self.n = first.n + second.n
        self.arrays = first.arrays + second.arrays
        self.in_specs = first.in_specs + second.in_specs
        self.out_specs = first.out_specs + second.out_specs
        self.out_shape = first.out_shape + second.out_shape
        self.scratch = first.scratch + second.scratch
        self.forward_step = first.forward_step

    def bind(self, in_refs, out_refs, scratch):
        k, s = self.first.n, len(self.first.scratch)
        return _Joined(self.first.bind(in_refs[:k], out_refs[:k], scratch[:s]),
                       self.second.bind(in_refs[k:], out_refs[k:], scratch[s:]))


def _adamw(w, g, m, v):
    m = ADAM_B1 * m + (1.0 - ADAM_B1) * g
    v = ADAM_B2 * v + (1.0 - ADAM_B2) * (g * g)
    m_hat = m / (1.0 - ADAM_B1 ** ADAM_STEP)
    v_hat = v / (1.0 - ADAM_B2 ** ADAM_STEP)
    delta = -ADAM_LR * (m_hat / (jnp.sqrt(v_hat) + ADAM_EPS) + ADAM_WD * w)
    return delta, m, v


def _sum_parts(parts_ref, index=()):
    g = parts_ref[(0,) + index].astype(F32)
    for s in range(1, parts_ref.shape[0]):
        g = g + parts_ref[(s,) + index].astype(F32)
    return g


def _adamw_group_call(name, groups):
    k = len(groups)

    def body(*refs):
        ins, outs = refs[:4 * k], refs[4 * k:]
        for i in range(k):
            parts_ref, w_ref, m_ref, v_ref = ins[4 * i:4 * i + 4]
            g = _sum_parts(parts_ref)
            delta, m_new, v_new = _adamw(w_ref[...], g, m_ref[...], v_ref[...])
            for out_ref, value in zip(outs[4 * i:4 * i + 4], (g, delta, m_new, v_new)):
                out_ref[...] = value

    vmem = pl.BlockSpec(memory_space=pltpu.VMEM)
    res = pl.pallas_call(
        body, name=name, in_specs=[vmem] * (4 * k), out_specs=[vmem] * (4 * k),
        out_shape=[jax.ShapeDtypeStruct(grp[1].shape, F32) for grp in groups for _ in range(4)],
        compiler_params=pltpu.CompilerParams(vmem_limit_bytes=VMEM_LIMIT),
    )(*[a for grp in groups for a in grp])
    return [res[4 * i:4 * i + 4] for i in range(k)]


def _adamw_slabs_call(name, parts, w, m, v, rider=None):
    def main(parts_ref, w_ref, m_ref, v_ref, g_ref, delta_ref, m_out, v_out):
        g = _sum_parts(parts_ref)
        delta, m_new, v_new = _adamw(w_ref[...], g, m_ref[...], v_ref[...])
        g_ref[...] = g
        delta_ref[...] = delta
        m_out[...] = m_new
        v_out[...] = v_new

    def body(*refs):
        own, comm = _split_refs(refs, 4, 4, 0, rider)
        if comm is not None:
            comm.start()
        main(*own)
        if comm is not None:
            comm.forward()
            comm.finish()

    vmem = pl.BlockSpec(memory_space=pltpu.VMEM)
    return pl.pallas_call(
        body, name=name, in_specs=_extend([vmem] * 4, rider, "in_specs"),
        out_specs=_extend([vmem] * 4, rider, "out_specs"),
        out_shape=_extend([jax.ShapeDtypeStruct(w.shape, F32)] * 4, rider, "out_shape"),
        scratch_shapes=_extend([], rider, "scratch"),
        compiler_params=pltpu.CompilerParams(vmem_limit_bytes=VMEM_LIMIT),
    )(parts, w, m, v, *_extend([], rider, "arrays"))


WIDE_ROWS = 8
NARROW_ROWS = 40
NARROW_GKW_ROW = 8
NARROW_GKB_ROW = 24
NARROW_HW_ROW = 32
GROUP_SHARD = POOL_GROUP_DIM // N_DEV
KEY_SHARD = GLA_KEY_WIDTH // N_DEV
HEAD_V_SHARD = GLA_HEAD_V // N_DEV


def _small_adamw_call(wide, narrow, w, m, v):
    names = ("norm_w", "pool_scale", "final_norm_w", "pool_group_b", "gla_gk_w", "gla_gk_b", "gla_head_norm_w")
    where = {
        "norm_w": (0, slice(0, 2), slice(None)),
        "pool_scale": (0, slice(2, 3), slice(None)),
        "final_norm_w": (0, slice(3, 4), slice(None)),
        "pool_group_b": (1, slice(0, POOL_GROUPS), slice(0, GROUP_SHARD)),
        "gla_gk_w": (1, slice(NARROW_GKW_ROW, NARROW_GKW_ROW + GLA_GATE_RANK), slice(0, KEY_SHARD)),
        "gla_gk_b": (1, slice(NARROW_GKB_ROW, NARROW_GKB_ROW + 1), slice(0, KEY_SHARD)),
        "gla_head_norm_w": (1, slice(NARROW_HW_ROW, NARROW_HW_ROW + 1), slice(0, HEAD_V_SHARD)),
    }
    k = len(names)

    def body(*refs):
        parts = refs[0:2]
        w_refs, m_refs, v_refs = refs[2:2 + k], refs[2 + k:2 + 2 * k], refs[2 + 2 * k:2 + 3 * k]
        outs = refs[2 + 3 * k:]
        loss_ref = outs[0]
        loss_ref[...] = _sum_parts(parts[0], (slice(4, 5), slice(0, 1)))
        for i, name in enumerate(names):
            buf, rows, cols = where[name]
            g = _sum_parts(parts[buf], (rows, cols))
            delta, m_new, v_new = _adamw(w_refs[i][...], g, m_refs[i][...], v_refs[i][...])
            outs[1 + i][...] = g
            outs[1 + k + i][...] = delta
            outs[1 + 2 * k + i][...] = m_new
            outs[1 + 3 * k + i][...] = v_new

    vmem = pl.BlockSpec(memory_space=pltpu.VMEM)
    shapes = [jax.ShapeDtypeStruct(w[n].shape, F32) for n in names]
    res = pl.pallas_call(
        body, name="adamw_small", in_specs=[vmem] * (2 + 3 * k), out_specs=[vmem] * (1 + 4 * k),
        out_shape=[jax.ShapeDtypeStruct((1, 1), F32)] + shapes * 4,
    )(wide, narrow, *[w[n] for n in names], *[m[n] for n in names], *[v[n] for n in names])
    unzip = lambda j: dict(zip(names, res[1 + j * k:1 + (j + 1) * k]))
    return res[0], unzip(0), unzip(1), unzip(2), unzip(3)


def kernel(x, norm_w, pool_in_w, pool_group_w, pool_group_b, pool_scale, pool_out_w, gla_in_w, gla_gk_w, gla_gk_b, gla_head_norm_w, gla_out_w, final_norm_w, loss_target, m_norm_w, m_pool_in_w, m_pool_group_w, m_pool_group_b, m_pool_scale, m_pool_out_w, m_gla_in_w, m_gla_gk_w, m_gla_gk_b, m_gla_head_norm_w, m_gla_out_w, m_final_norm_w, v_norm_w, v_pool_in_w, v_pool_group_w, v_pool_group_b, v_pool_scale, v_pool_out_w, v_gla_in_w, v_gla_gk_w, v_gla_gk_b, v_gla_head_norm_w, v_gla_out_w, v_final_norm_w):
    w = dict(norm_w=norm_w, pool_in_w=pool_in_w, pool_group_w=pool_group_w, pool_group_b=pool_group_b,
             pool_scale=pool_scale, pool_out_w=pool_out_w, gla_in_w=gla_in_w, gla_gk_w=gla_gk_w, gla_gk_b=gla_gk_b,
             gla_head_norm_w=gla_head_norm_w, gla_out_w=gla_out_w, final_norm_w=final_norm_w)
    m = dict(norm_w=m_norm_w, pool_in_w=m_pool_in_w, pool_group_w=m_pool_group_w, pool_group_b=m_pool_group_b,
             pool_scale=m_pool_scale, pool_out_w=m_pool_out_w, gla_in_w=m_gla_in_w, gla_gk_w=m_gla_gk_w,
             gla_gk_b=m_gla_gk_b, gla_head_norm_w=m_gla_head_norm_w, gla_out_w=m_gla_out_w,
             final_norm_w=m_final_norm_w)
    v = dict(norm_w=v_norm_w, pool_in_w=v_pool_in_w, pool_group_w=v_pool_group_w, pool_group_b=v_pool_group_b,
             pool_scale=v_pool_scale, pool_out_w=v_pool_out_w, gla_in_w=v_gla_in_w, gla_gk_w=v_gla_gk_w,
             gla_gk_b=v_gla_gk_b, gla_head_norm_w=v_gla_head_norm_w, gla_out_w=v_gla_out_w,
             final_norm_w=v_final_norm_w)
    col_shard = GLA_IN_WIDTH // N_DEV
    row_shard = D_MODEL // N_DEV

    def lanes(a):
        return jnp.pad(a, [(0, 0)] * (a.ndim - 1) + [(0, LANES - a.shape[-1])])

    small_in = jnp.concatenate([lanes(pool_group_b[0]), lanes(gla_gk_b), lanes(gla_head_norm_w),
                                jnp.zeros((2, LANES), F32)], axis=0)
    in_cols = 2 * POOL_WIDTH // N_DEV
    pool_in, pool_gw, pool_out, small_all = _comm_call("pool_weights_all_gather", _gather_rider(
        [pool_in_w[0].astype(BF16), pool_group_w[0].astype(BF16), pool_out_w[0].astype(BF16), small_in],
        [(D_MODEL, 2 * POOL_WIDTH), (POOL_GROUPS, POOL_GROUP_DIM, POOL_GROUP_DIM), (POOL_WIDTH, D_MODEL),
         (N_DEV, 8, LANES)],
        [_dim1_slot(in_cols), _dim1_slot(GROUP_SHARD), _row_slot(row_shard), _lead_slot]))
    pool_gb = jnp.transpose(small_all[:, 0:POOL_GROUPS, :GROUP_SHARD], (1, 0, 2)).reshape(1, POOL_WIDTH)
    gla_gkb = small_all[:, POOL_GROUPS, :KEY_SHARD].reshape(1, GLA_KEY_WIDTH)
    gla_hw = jnp.tile(small_all[:, POOL_GROUPS + 1, :HEAD_V_SHARD].reshape(1, GLA_HEAD_V), (1, GLA_HEADS))
    nw0, nw1, wf = norm_w[0:1], norm_w[1:2], final_norm_w.reshape(1, D_MODEL)
    xs, target = x[0], loss_target[0]

    h1, p, gla_in_parts, gkw_parts, gla_out = _pool_fwd_call(
        xs, nw0, pool_in, pool_gw, pool_gb, pool_scale, pool_out, _gather_rider(
            [jnp.transpose(gla_in_w[0]).astype(BF16), gla_gk_w[0].astype(BF16), gla_out_w[0].astype(BF16)],
            [(N_DEV, col_shard, D_MODEL), (N_DEV, GLA_GATE_RANK, KEY_SHARD), (GLA_VALUE_WIDTH, D_MODEL)],
            [_lead_slot, _lead_slot, _row_slot(row_shard)], GATHER_RELAY_STEP))
    gla_in = gla_in_parts.reshape(GLA_IN_WIDTH, D_MODEL)
    gla_gkw = jnp.pad(jnp.transpose(gkw_parts, (1, 0, 2)).reshape(GLA_GATE_RANK, GLA_KEY_WIDTH),
                      ((0, GLA_LOW_PAD - GLA_GATE_RANK), (0, 0)))
    dh2, proj, o, states, loss_part, dwf = _gla_fwd_call(h1, nw1, gla_in, gla_gkw, gla_gkb, gla_hw, gla_out, wf, target)

    dproj, d_gla_out, dhw, dgkw, dgkb = _gla_bwd_call(dh2, proj, o, states, gla_gkw, gla_gkb, gla_hw, gla_out)
    dh1, d_gla_in, dnw1, landed_gla_out = _inproj_bwd_call(
        "gla_in_bwd", dproj, h1, nw1, gla_in, dh2,
        _exchange_rider([d_gla_out], [(row_shard, D_MODEL)], [_row_slot(row_shard)]), transposed=True)
    slabs = col_shard * D_MODEL // (BF16_ROWS * LANES)
    gla_in_send = d_gla_in.reshape(N_DEV, slabs, BF16_ROWS, LANES)
    dp, d_pool_out, dgw, dgb, dsc, landed_gla_in = _pool_bwd_call(
        dh1, p, pool_gw, pool_gb, pool_scale, pool_out,
        _two_level_rider([gla_in_send], [(slabs, BF16_ROWS, LANES)], [_lead_slot], TWO_LEVEL_ADD_STEP))
    grad_x, d_pool_in, dnw0 = _inproj_bwd_call("pool_in_bwd", dp, xs, nw0, pool_in, dh1)

    wide = jnp.concatenate([
        dnw0, dnw1, dsc, dwf, jnp.pad(loss_part[0:1, 0:1], ((0, 0), (0, D_MODEL - 1))),
        jnp.zeros((WIDE_ROWS - 5, D_MODEL), F32)], axis=0)

    def rows8(a):
        return jnp.pad(lanes(a), ((0, 0), (0, -a.shape[1] % 8), (0, 0)))

    narrow = jnp.concatenate([
        rows8(jnp.transpose(dgb.reshape(POOL_GROUPS, N_DEV, GROUP_SHARD), (1, 0, 2))),
        rows8(jnp.transpose(dgkw[:GLA_GATE_RANK].reshape(GLA_GATE_RANK, N_DEV, KEY_SHARD), (1, 0, 2))),
        rows8(dgkb.reshape(N_DEV, 1, KEY_SHARD)),
        rows8(dhw.reshape(GLA_HEADS, GLA_HEAD_V).sum(axis=0).reshape(N_DEV, 1, HEAD_V_SHARD)),
    ], axis=1)
    last_exchange = _JoinedRider(
        _two_level_rider([d_pool_in, d_pool_out, dgw],
                         [(D_MODEL, in_cols), (row_shard, D_MODEL), (POOL_GROUPS, GROUP_SHARD, POOL_GROUP_DIM)],
                         [_dim1_slot(in_cols), _row_slot(row_shard), _dim1_slot(GROUP_SHARD)]),
        _exchange_rider([wide, narrow], [(WIDE_ROWS, D_MODEL), (NARROW_ROWS, LANES)],
                        [lambda ref, d: ref, _lead_slot]))

    res = {}
    as_slabs = lambda t: jnp.transpose(t[0]).reshape(slabs, BF16_ROWS, LANES)
    *outs, landed_pool_in, landed_pool_out, landed_gw, landed_wide, landed_narrow = _adamw_slabs_call(
        "adamw_gla_in_w", landed_gla_in, as_slabs(gla_in_w), as_slabs(m_gla_in_w), as_slabs(v_gla_in_w),
        last_exchange)
    res["gla_in_w"] = [jnp.transpose(t.reshape(col_shard, D_MODEL))[None] for t in outs]
    rest = [("pool_in_w", landed_pool_in, (D_MODEL, in_cols)),
            ("pool_group_w", landed_gw, (POOL_GROUPS * GROUP_SHARD, POOL_GROUP_DIM)),
            ("pool_out_w", landed_pool_out, (row_shard, D_MODEL)), ("gla_out_w", landed_gla_out, (row_shard, D_MODEL))]
    updates = _adamw_group_call("adamw_matrices", [
        (parts.reshape((parts.shape[0],) + shape), w[name].reshape(shape), m[name].reshape(shape),
         v[name].reshape(shape)) for name, parts, shape in rest])
    for (name, _, _), outs in zip(rest, updates):
        res[name] = [t.reshape(w[name].shape) for t in outs]
    small_shapes ={"norm_w": (2, D_MODEL), "pool_scale": (1, D_MODEL), "final_norm_w": (1, D_MODEL),
                    "pool_group_b": (POOL_GROUPS, GROUP_SHARD), "gla_gk_w": (GLA_GATE_RANK, KEY_SHARD),
                    "gla_gk_b": (1, KEY_SHARD), "gla_head_norm_w": (1, HEAD_V_SHARD)}
    as_small = lambda t: {n: t[n].reshape(s) for n, s in small_shapes.items()}
    loss, *small_outs = _small_adamw_call(landed_wide, landed_narrow, as_small(w), as_small(m), as_small(v))
    for name in small_shapes:
        res[name] = [t[name].reshape(w[name].shape) for t in small_outs]
    order = ("norm_w", "pool_in_w", "pool_group_w", "pool_group_b", "pool_scale", "pool_out_w", "gla_in_w",
             "gla_gk_w", "gla_gk_b", "gla_head_norm_w", "gla_out_w", "final_norm_w")
    return (loss.reshape(()), grad_x[None], *[res[n][0] for n in order], *[res[n][1] for n in order],
            *[res[n][2] for n in order], *[res[n][3] for n in order])
```

```python
import jax
import jax.numpy as jnp
from jax import lax
from jax.experimental import pallas as pl
from jax.experimental.pallas import tpu as pltpu

F32 = jnp.float32
BF16 = jnp.bfloat16
MESH = pl.DeviceIdType.MESH

N_DEV = 8
D_MODEL = 1024
POOL_WIDTH = 1024
POOL_GROUPS = 4
POOL_GROUP_DIM = 256
POOL_HALO = 16
GLA_HEADS = 4
GLA_HEAD_K = 128
GLA_HEAD_V = 256
GLA_KEY_WIDTH = 512
GLA_VALUE_WIDTH = 1024
GLA_GATE_RANK = 16
GLA_IN_WIDTH = 3088
GLA_IN_PAD = 3200
GLA_LOW_PAD = 128
GLA_QKVG_WIDTH = 3072
CHUNK = 64
GATE_NORMALIZER = 16.0
RMS_EPS = 1e-6
Q_SCALE = GLA_HEAD_K ** -0.5

ADAM_LR = 0.001
ADAM_B1 = 0.9
ADAM_B2 = 0.999
ADAM_EPS = 1e-08
ADAM_WD = 0.01
ADAM_STEP = 10

LANES = 128
BF16_ROWS = 16
VMEM_LIMIT = 56 * 1024 * 1024
ROW_TILE = 256
MATMUL_ROW_TILE = 512
GATHER_RELAY_STEP = 5
TWO_LEVEL_ADD_STEP = 1
TWO_LEVEL_RELAY_STEP = 4


def _dot_nn(a, b):
    return lax.dot_general(a, b, (((1,), (0,)), ((), ())), preferred_element_type=F32)


def _dot_nt(a, b):
    return lax.dot_general(a, b, (((1,), (1,)), ((), ())), preferred_element_type=F32)


def _dot_tn(a, b):
    return lax.dot_general(a, b, (((0,), (0,)), ((), ())), preferred_element_type=F32)


def _rms(x):
    rstd = lax.rsqrt(jnp.mean(x * x, axis=-1, keepdims=True) + RMS_EPS)
    return x * rstd, rstd


def _rms_bwd(dxhat, xhat, rstd):
    return rstd * (dxhat - xhat * jnp.mean(dxhat * xhat, axis=-1, keepdims=True))


def _sigmoid(x):
    return 1.0 / (1.0 + jnp.exp(-x))


def _params(sem=("arbitrary",)):
    return pltpu.CompilerParams(dimension_semantics=sem, vmem_limit_bytes=VMEM_LIMIT)


def _full(shape):
    return pl.BlockSpec(shape, lambda i: (0,) * len(shape))


def _const(shape):
    return pl.BlockSpec(shape, lambda i: (0,) * len(shape), pipeline_mode=pl.Buffered(1))


def _window_sums(ext, forward):
    n = ext.shape[0]
    outs = []
    for g in range(POOL_GROUPS):
        s = ext[:, g * POOL_GROUP_DIM:(g + 1) * POOL_GROUP_DIM]
        for k in range(g + 1):
            shift = (1 << k) if forward else n - (1 << k)
            s = s + pltpu.roll(s, shift, axis=0)
        outs.append(s[:n - POOL_HALO])
    return outs


def _inv_count(row0, tm):
    row = row0 + lax.broadcasted_iota(jnp.int32, (tm, 1), 0)
    return [1.0 / jnp.minimum(row + 1, 2 << g).astype(F32) for g in range(POOL_GROUPS)]


def _pool_mix(u, u_prev, row0, gw_ref, gb):
    tm = u.shape[0]
    sums = _window_sums(jnp.concatenate([u, u_prev], axis=0), True)
    inv = _inv_count(row0, tm)
    pooled, mixed = [], []
    for g in range(POOL_GROUPS):
        ug = u[:, g * POOL_GROUP_DIM:(g + 1) * POOL_GROUP_DIM]
        pg = (sums[g] * inv[g] - ug).astype(BF16)
        pooled.append(pg)
        mixed.append(_dot_nn(pg, gw_ref[g]))
    return pooled, jnp.concatenate(mixed, axis=1) + gb


def _pool_fwd_call(x, nw, w_in, gw, gb, sc, w_out, rider=None):
    seq = x.shape[0]
    tm = min(MATMUL_ROW_TILE, seq)
    nt = seq // tm

    def main(x_ref, nw_ref, win_ref, gw_ref, gb_ref, sc_ref, wout_ref, h_ref, p_ref, halo_ref):
        i = pl.program_id(0)

        @pl.when(i == 0)
        def _():
            halo_ref[...] = jnp.zeros_like(halo_ref)

        xt = x_ref[...]
        xhat, _ = _rms(xt)
        n = (xhat * nw_ref[...]).astype(BF16)
        p = _dot_nn(n, win_ref[...])
        p_ref[...] = p
        u = p[:, :POOL_WIDTH]
        gate = p[:, POOL_WIDTH:]
        _, mixed = _pool_mix(u, halo_ref[...], i * tm, gw_ref, gb_ref[...])
        halo_ref[...] = u[tm - POOL_HALO:, :]
        y = (mixed * sc_ref[...] * (gate * _sigmoid(gate))).astype(BF16)
        h_ref[...] = xt + _dot_nn(y, wout_ref[...])

    def body(*refs):
        own, comm = _split_refs(refs, 7, 2, 1, rider)
        _ride_before(comm, pl.program_id(0), nt)
        main(*own)
        _ride_after(comm, pl.program_id(0), nt)

    return pl.pallas_call(
        body, name="pool_fwd", grid=(nt,),
        in_specs=_extend([pl.BlockSpec((tm, D_MODEL), lambda i: (i, 0)), _const((1, D_MODEL)),
                          _const((D_MODEL, 2 * POOL_WIDTH)), _const((POOL_GROUPS, POOL_GROUP_DIM, POOL_GROUP_DIM)),
                          _const((1, POOL_WIDTH)), _const((1, POOL_WIDTH)), _const((POOL_WIDTH, D_MODEL))],
                         rider, "in_specs"),
        out_specs=_extend([pl.BlockSpec((tm, D_MODEL), lambda i: (i, 0)),
                           pl.BlockSpec((tm, 2 * POOL_WIDTH), lambda i: (i, 0))], rider, "out_specs"),
        out_shape=_extend([jax.ShapeDtypeStruct((seq, D_MODEL), F32),
                           jax.ShapeDtypeStruct((seq, 2 * POOL_WIDTH), F32)], rider, "out_shape"),
        scratch_shapes=_extend([pltpu.VMEM((POOL_HALO, POOL_WIDTH), F32)], rider, "scratch"),
        compiler_params=_params(),
    )(x, nw, w_in, gw, gb, sc, w_out, *_extend([], rider, "arrays"))


def _pool_bwd_call(dh, p, gw, gb, sc, w_out, rider=None):
    seq = dh.shape[0]
    tm = min(MATMUL_ROW_TILE, seq)
    nt = seq // tm
    halo_blocks = tm // POOL_HALO

    def main(dh_ref, p_ref, pprev_ref, gw_ref, gb_ref, sc_ref, wout_ref,
             dp_ref, dwout_hbm, dgw_hbm, dgb_ref, dsc_ref, carry_ref, dwout_acc, dgw_acc, dwout_stage, dgw_stage):
        i = pl.program_id(0)
        t = nt - 1 - i

        @pl.when(i == 0)
        def _():
            carry_ref[...] = jnp.zeros_like(carry_ref)
            dwout_acc[...] = jnp.zeros_like(dwout_acc)
            dgw_acc[...] = jnp.zeros_like(dgw_acc)
            dgb_ref[...] = jnp.zeros_like(dgb_ref)
            dsc_ref[...] = jnp.zeros_like(dsc_ref)

        dhb = dh_ref[...].astype(BF16)
        dy = _dot_nt(dhb, wout_ref[...])
        p = p_ref[...]
        u = p[:, :POOL_WIDTH]
        gate = p[:, POOL_WIDTH:]
        u_prev = jnp.where(t > 0, pprev_ref[:, :POOL_WIDTH], 0.0)
        pooled, mixed = _pool_mix(u, u_prev, t * tm, gw_ref, gb_ref[...])
        sg = _sigmoid(gate)
        silu = gate * sg
        sc = sc_ref[...]
        y = (mixed * sc * silu).astype(BF16)
        dwout_acc[...] += _dot_tn(y, dhb)
        dmixed = dy * sc * silu
        dsc_ref[...] += jnp.sum(dy * mixed * silu, axis=0, keepdims=True)
        dgate = dy * mixed * sc * (sg * (1.0 + gate * (1.0 - sg)))
        dgb_ref[...] += jnp.sum(dmixed, axis=0, keepdims=True)
        inv = _inv_count(t * tm, tm)
        dpooled, scaled = [], []
        for g in range(POOL_GROUPS):
            dmg = dmixed[:, g * POOL_GROUP_DIM:(g + 1) * POOL_GROUP_DIM].astype(BF16)
            dgw_acc[g] += _dot_tn(pooled[g], dmg)
            dpg = _dot_nt(dmg, gw_ref[g])
            dpooled.append(dpg)
            scaled.append(dpg * inv[g])
        r = jnp.concatenate(scaled, axis=1)
        sums = _window_sums(jnp.concatenate([r, carry_ref[...]], axis=0), False)
        carry_ref[...] = r[:POOL_HALO, :]
        du = jnp.concatenate([sums[g] - dpooled[g] for g in range(POOL_GROUPS)], axis=1)
        dp_ref[...] = jnp.concatenate([du, dgate], axis=1).astype(BF16)

        @pl.when(i == nt - 1)
        def _():
            dwout_stage[...] = dwout_acc[...].astype(BF16)
            dgw_stage[...] = dgw_acc[...].astype(BF16)
            pltpu.sync_copy(dwout_stage, dwout_hbm)
            pltpu.sync_copy(dgw_stage, dgw_hbm)

    def body(*refs):
        own, comm = _split_refs(refs, 7, 5, 5, rider)
        _ride_before(comm, pl.program_id(0), nt)
        main(*own)
        _ride_after(comm, pl.program_id(0), nt)

    rev = lambda i: (nt - 1 - i, 0)
    return pl.pallas_call(
        body, name="pool_bwd", grid=(nt,),
        in_specs=_extend([pl.BlockSpec((tm, D_MODEL), rev), pl.BlockSpec((tm, 2 * POOL_WIDTH), rev),
                          pl.BlockSpec((POOL_HALO, 2 * POOL_WIDTH),
                                       lambda i: (jnp.maximum((nt - 1 - i) * halo_blocks - 1, 0), 0)),
                          _const((POOL_GROUPS, POOL_GROUP_DIM, POOL_GROUP_DIM)), _const((1, POOL_WIDTH)),
                          _const((1, POOL_WIDTH)), _const((POOL_WIDTH, D_MODEL))], rider, "in_specs"),
        out_specs=_extend([pl.BlockSpec((tm, 2 * POOL_WIDTH), rev), pl.BlockSpec(memory_space=pl.ANY),
                           pl.BlockSpec(memory_space=pl.ANY), _full((1, POOL_WIDTH)), _full((1, POOL_WIDTH))],
                          rider, "out_specs"),
        out_shape=_extend([jax.ShapeDtypeStruct((seq, 2 * POOL_WIDTH), BF16),
                           jax.ShapeDtypeStruct((POOL_WIDTH, D_MODEL), BF16),
                           jax.ShapeDtypeStruct((POOL_GROUPS, POOL_GROUP_DIM, POOL_GROUP_DIM), BF16),
                           jax.ShapeDtypeStruct((1, POOL_WIDTH), F32), jax.ShapeDtypeStruct((1, POOL_WIDTH), F32)],
                          rider, "out_shape"),
        scratch_shapes=_extend([pltpu.VMEM((POOL_HALO, POOL_WIDTH), F32), pltpu.VMEM((POOL_WIDTH, D_MODEL), F32),
                                pltpu.VMEM((POOL_GROUPS, POOL_GROUP_DIM, POOL_GROUP_DIM), F32),
                                pltpu.VMEM((POOL_WIDTH, D_MODEL), BF16),
                                pltpu.VMEM((POOL_GROUPS, POOL_GROUP_DIM, POOL_GROUP_DIM), BF16)], rider, "scratch"),
        compiler_params=_params(),
    )(dh, p, p, gw, gb, sc, w_out, *_extend([], rider, "arrays"))


def _rows_then_zeros(ref, lo, hi, rows):
    part = ref[lo:hi, :]
    return jnp.concatenate([part, jnp.zeros((rows - (hi - lo), part.shape[1]), part.dtype)], axis=0)


def _inproj_bwd_call(name, dproj, h_in, nw, w_in, dres, rider=None, transposed=False):
    seq = h_in.shape[0]
    width = dproj.shape[1]
    w_shape = tuple(w_in.shape)
    acc_shape = (width, D_MODEL) if transposed else w_shape
    whole = w_shape[0] // LANES * LANES
    tm = min(MATMUL_ROW_TILE, seq)
    nt = seq // tm

    def main(dproj_ref, h_ref, nw_ref, win_ref, dres_ref, dh_ref, dw_hbm, dnw_ref, dw_acc, dw_stage):
        i = pl.program_id(0)

        @pl.when(i == 0)
        def _():
            dw_acc[...] = jnp.zeros_like(dw_acc)
            dnw_ref[...] = jnp.zeros_like(dnw_ref)

        dpb = dproj_ref[...]
        if transposed:
            dn = _dot_nn(dpb[:, :whole], win_ref[0:whole, :])
            if whole < w_shape[0]:
                dn = dn + _dot_nn(dpb[:, whole:], _rows_then_zeros(win_ref, whole, w_shape[0], width - whole))
        else:
            dn = _dot_nt(dpb, win_ref[...])
        xhat, rstd = _rms(h_ref[...])
        nw_row = nw_ref[...]
        n = (xhat * nw_row).astype(BF16)
        dw_acc[...] += _dot_tn(dpb, n) if transposed else _dot_tn(n, dpb)
        dnw_ref[...] += jnp.sum(dn * xhat, axis=0, keepdims=True)
        dh_ref[...] = _rms_bwd(dn * nw_row, xhat, rstd) + dres_ref[...]

        @pl.when(i == nt - 1)
        def _():
            dw_stage[...] = dw_acc[...].astype(BF16)
            pltpu.sync_copy(dw_stage.at[pl.ds(0, w_shape[0])], dw_hbm)

    def body(*refs):
        own, comm = _split_refs(refs, 5, 3, 2, rider)
        _ride_before(comm, pl.program_id(0), nt)
        main(*own)
        _ride_after(comm, pl.program_id(0), nt)

    row = lambda i: (i, 0)
    return pl.pallas_call(
        body, name=name, grid=(nt,),
        in_specs=_extend([pl.BlockSpec((tm, width), row), pl.BlockSpec((tm, D_MODEL), row), _const((1, D_MODEL)),
                          _const(w_shape), pl.BlockSpec((tm, D_MODEL), row)], rider, "in_specs"),
        out_specs=_extend([pl.BlockSpec((tm, D_MODEL), row), pl.BlockSpec(memory_space=pl.ANY),
                           _full((1, D_MODEL))], rider, "out_specs"),
        out_shape=_extend([jax.ShapeDtypeStruct((seq, D_MODEL), F32), jax.ShapeDtypeStruct(w_shape, BF16),
                           jax.ShapeDtypeStruct((1, D_MODEL), F32)], rider, "out_shape"),
        scratch_shapes=_extend([pltpu.VMEM(acc_shape, F32), pltpu.VMEM(acc_shape, BF16)], rider, "scratch"),
        compiler_params=_params(),
    )(dproj, h_in, nw, w_in, dres, *_extend([], rider, "arrays"))


def _chunk_scan(x, reverse):
    n = x.shape[0]
    pos = lax.broadcasted_iota(jnp.int32, (n, 1), 0) & (CHUNK - 1)
    k = 1
    while k < CHUNK:
        if reverse:
            x = x + jnp.where(pos < CHUNK - k, pltpu.roll(x, n - k, axis=0), 0.0)
        else:
            x = x + jnp.where(pos >= k, pltpu.roll(x, k, axis=0), 0.0)
        k *= 2
    return x


def _chunk_rows(j):
    return slice(j * CHUNK, (j + 1) * CHUNK)


def _kcols(h):
    return slice(h * GLA_HEAD_K, (h + 1) * GLA_HEAD_K)


def _vcols(h):
    return slice(h * GLA_HEAD_V, (h + 1) * GLA_HEAD_V)


def _chunk_masks(tm):
    idx_t = lax.broadcasted_iota(jnp.int32, (tm, tm), 0)
    idx_s = lax.broadcasted_iota(jnp.int32, (tm, tm), 1)
    same_chunk = (idx_t ^ idx_s) < CHUNK
    return same_chunk & (idx_t >= idx_s), same_chunk & (idx_t < idx_s)


class _GlaTerms:
    def __init__(self, kc, q, k, v, low_b, gkw_ref, gkb_ref, masks):
        tm = q.shape[0]
        self.q = q * Q_SCALE
        self.k = k
        self.z = _dot_nn(low_b, gkw_ref[:, kc]) + gkb_ref[:, kc]
        log_g = (jnp.minimum(self.z, 0.0) - jnp.log(1.0 + jnp.exp(-jnp.abs(self.z)))) / GATE_NORMALIZER
        self.c = _chunk_scan(log_g, False)
        is_last = lax.broadcasted_iota(jnp.int32, (CHUNK, 1), 0) == CHUNK - 1
        self.c_last = [jnp.sum(jnp.where(is_last, self.c[_chunk_rows(j), :], 0.0), axis=0, keepdims=True)
                       for j in range(tm // CHUNK)]
        c_last_rows = jnp.concatenate([jnp.broadcast_to(r, (CHUNK, r.shape[1])) for r in self.c_last], axis=0)
        self.e_pos = jnp.exp(self.c)
        self.e_neg = jnp.exp(-self.c)
        self.e_rest = jnp.exp(c_last_rows - self.c)
        self.a_b = (self.q * self.e_pos).astype(BF16)
        self.b_b = (self.k * self.e_neg).astype(BF16)
        self.cn_b = (self.q * self.e_neg).astype(BF16)
        self.dp_b = (self.k * self.e_pos).astype(BF16)
        self.kd_b = (self.k * self.e_rest).astype(BF16)
        self.v_b = v.astype(BF16)
        self.lower, self.upper = masks

    def scores(self, kc=slice(None)):
        fwd = _dot_nt(self.a_b[:, kc], self.b_b[:, kc])
        bwd = _dot_nt(self.cn_b[:, kc], self.dp_b[:, kc])
        return jnp.where(self.lower, fwd, jnp.where(self.upper, bwd, 0.0)).astype(BF16)


def _gla_fwd_call(h1, nw, w_in, gkw, gkb, hw, w_out, wf, target):
    seq = h1.shape[0]
    tm = ROW_TILE
    nt = seq // tm
    cpt = tm // CHUNK
    n_chunks = seq // CHUNK

    def body(h_ref, nw_ref, win_ref, gkw_ref, gkb_ref, hw_ref, wout_ref, wf_ref, tgt_ref,
             dh2_ref, proj_ref, o_ref, st_ref, loss_ref, dwf_ref, state_ref):
        i = pl.program_id(0)

        @pl.when(i == 0)
        def _():
            state_ref[...] = jnp.zeros_like(state_ref)
            loss_ref[...] = jnp.zeros_like(loss_ref)
            dwf_ref[...] = jnp.zeros_like(dwf_ref)

        ht = h_ref[...]
        xhat, _ = _rms(ht)
        n = (xhat * nw_ref[...]).astype(BF16)
        sections = {}
        for name, lo, hi in (("low", GLA_QKVG_WIDTH, GLA_IN_PAD), ("qk", 0, 2 * GLA_KEY_WIDTH),
                             ("v", 2 * GLA_KEY_WIDTH, GLA_QKVG_WIDTH - GLA_VALUE_WIDTH),
                             ("gate", GLA_QKVG_WIDTH - GLA_VALUE_WIDTH, GLA_QKVG_WIDTH)):
            rows = (win_ref[lo:hi, :] if hi <= GLA_IN_WIDTH
                    else _rows_then_zeros(win_ref, lo, GLA_IN_WIDTH, hi - lo))
            sections[name] = _dot_nt(n, rows)
            proj_ref[:, lo:hi] = sections[name]
        low_b = sections["low"].astype(BF16)
        masks = _chunk_masks(tm)
        on_heads = []
        for h in range(GLA_HEADS):
            kc, vc = _kcols(h), _vcols(h)
            g = _GlaTerms(kc, sections["qk"][:, kc], sections["qk"][:, GLA_KEY_WIDTH:][:, kc], sections["v"][:, vc],
                          low_b, gkw_ref, gkb_ref, masks)
            srows = slice(h * GLA_HEAD_V, (h + 1) * GLA_HEAD_V)
            o_intra = _dot_nn(g.scores(), g.v_b)
            state = state_ref[srows, :]
            o_rows = []
            for j in range(cpt):
                r = _chunk_rows(j)
                st_ref[j, srows, :] = state
                o_rows.append(o_intra[r] + _dot_nt(g.a_b[r], state.astype(BF16)))
                decay = jnp.exp(g.c_last[j])
                state = state * decay + _dot_tn(g.v_b[r], g.kd_b[r])
            state_ref[srows, :] = state
            o_head = jnp.concatenate(o_rows, axis=0)
            o_ref[:, vc] = o_head
            on_heads.append(_rms(o_head)[0])
        gate = sections["gate"]
        on = jnp.concatenate(on_heads, axis=1) * hw_ref[...]
        y = (on * (gate * _sigmoid(gate))).astype(BF16)
        h2 = ht + _dot_nn(y, wout_ref[...])
        xhat2, rstd2 = _rms(h2)
        wf_row = wf_ref[...]
        err = xhat2 * wf_row - tgt_ref[...]
        loss_ref[...] += 0.5 * jnp.sum(err * err) / D_MODEL
        dout = err * (1.0 / D_MODEL)
        dwf_ref[...] += jnp.sum(dout * xhat2, axis=0, keepdims=True)
        dh2_ref[...] = _rms_bwd(dout * wf_row, xhat2, rstd2)

    row = lambda i: (i, 0)
    return pl.pallas_call(
        body, name="gla_fwd", grid=(nt,),
        in_specs=[pl.BlockSpec((tm, D_MODEL), row), _const((1, D_MODEL)), _const((GLA_IN_WIDTH, D_MODEL)),
                  _const((GLA_LOW_PAD, GLA_KEY_WIDTH)), _const((1, GLA_KEY_WIDTH)), _const((1, GLA_VALUE_WIDTH)),
                  _const((GLA_VALUE_WIDTH, D_MODEL)), _const((1, D_MODEL)), pl.BlockSpec((tm, D_MODEL), row)],
        out_specs=[pl.BlockSpec((tm, D_MODEL), row), pl.BlockSpec((tm, GLA_IN_PAD), row),
                   pl.BlockSpec((tm, GLA_VALUE_WIDTH), row),
                   pl.BlockSpec((cpt, GLA_VALUE_WIDTH, GLA_HEAD_K), lambda i: (i, 0, 0)),
                   _full((8, LANES)), _full((1, D_MODEL))],
        out_shape=[jax.ShapeDtypeStruct((seq, D_MODEL), F32), jax.ShapeDtypeStruct((seq, GLA_IN_PAD), F32),
                   jax.ShapeDtypeStruct((seq, GLA_VALUE_WIDTH), F32),
                   jax.ShapeDtypeStruct((n_chunks, GLA_VALUE_WIDTH, GLA_HEAD_K), F32),
                   jax.ShapeDtypeStruct((8, LANES), F32), jax.ShapeDtypeStruct((1, D_MODEL), F32)],
        scratch_shapes=[pltpu.VMEM((GLA_VALUE_WIDTH, GLA_HEAD_K), F32)],
        compiler_params=_params(),
    )(h1, nw, w_in, gkw, gkb, hw, w_out, wf, target)


def _gla_bwd_call(dh2, proj, o, states, gkw, gkb, hw, w_out):
    seq = dh2.shape[0]
    tm = ROW_TILE
    nt = seq // tm
    cpt = tm // CHUNK

    def body(dh_ref, proj_ref, o_ref, st_ref, gkw_ref, gkb_ref, hw_ref, wout_ref,
             dproj_ref, dwout_hbm, dhw_ref, dgkw_ref, dgkb_ref, dstate_ref, dwout_acc, dwout_stage):
        i = pl.program_id(0)

        @pl.when(i == 0)
        def _():
            dstate_ref[...] = jnp.zeros_like(dstate_ref)
            dwout_acc[...] = jnp.zeros_like(dwout_acc)
            dhw_ref[...] = jnp.zeros_like(dhw_ref)
            dgkw_ref[...] = jnp.zeros_like(dgkw_ref)
            dgkb_ref[...] = jnp.zeros_like(dgkb_ref)

        dhb = dh_ref[...].astype(BF16)
        dy = _dot_nt(dhb, wout_ref[...])
        v0, g0 = 2 * GLA_KEY_WIDTH, GLA_QKVG_WIDTH - GLA_VALUE_WIDTH
        gate = proj_ref[:, g0:GLA_QKVG_WIDTH]
        low_b = proj_ref[:, GLA_QKVG_WIDTH:].astype(BF16)
        o = o_ref[...]
        hw_row = hw_ref[...]
        sg = _sigmoid(gate)
        silu = gate * sg
        don = dy * silu
        on_parts, do_parts, dhw_parts = [], [], []
        for h in range(GLA_HEADS):
            vc = _vcols(h)
            xh, rs = _rms(o[:, vc])
            on_parts.append(xh * hw_row[:, vc])
            dhw_parts.append(jnp.sum(don[:, vc] * xh, axis=0, keepdims=True))
            do_parts.append(_rms_bwd(don[:, vc] * hw_row[:, vc], xh, rs).astype(BF16))
        on = jnp.concatenate(on_parts, axis=1)
        dwout_acc[...] += _dot_tn((on * silu).astype(BF16), dhb)
        dhw_ref[...] += jnp.concatenate(dhw_parts, axis=1)
        dproj_ref[:, g0:GLA_QKVG_WIDTH] = (dy * on * (sg * (1.0 + gate * (1.0 - sg)))).astype(BF16)

        last_row = lax.broadcasted_iota(jnp.int32, (CHUNK, 1), 0) == CHUNK - 1
        g = _GlaTerms(slice(0, GLA_KEY_WIDTH), proj_ref[:, :GLA_KEY_WIDTH], proj_ref[:, GLA_KEY_WIDTH:v0],
                      proj_ref[:, v0:g0], low_b, gkw_ref, gkb_ref, _chunk_masks(tm))
        dc_h = []
        for h in range(GLA_HEADS):
            kc, vc = _kcols(h), _vcols(h)
            k_cols = slice(GLA_KEY_WIDTH + kc.start, GLA_KEY_WIDTH + kc.stop)
            v_cols = slice(v0 + vc.start, v0 + vc.stop)
            do_h = do_parts[h]
            srows = slice(h * GLA_HEAD_V, (h + 1) * GLA_HEAD_V)
            scores = g.scores(kc)
            dscores = _dot_nt(do_h, g.v_b[:, vc])
            dfwd = jnp.where(g.lower, dscores, 0.0).astype(BF16)
            dbwd = jnp.where(g.upper, dscores, 0.0).astype(BF16)
            dv_intra = _dot_tn(scores, do_h)
            da_intra = _dot_nn(dfwd, g.b_b[:, kc])
            db = _dot_tn(dfwd, g.a_b[:, kc])
            dcn = _dot_nn(dbwd, g.dp_b[:, kc])
            ddp = _dot_tn(dbwd, g.cn_b[:, kc])
            dstate = dstate_ref[srows, :]
            da_rows, dkd_rows, dv_rows, dcl_rows = [None] * cpt, [None] * cpt, [None] * cpt, [None] * cpt
            for j in reversed(range(cpt)):
                r = _chunk_rows(j)
                state = st_ref[j, srows, :]
                dstate_b = dstate.astype(BF16)
                do_c = do_h[r]
                dv_rows[j] = dv_intra[r] + _dot_nt(g.kd_b[r, kc], dstate_b)
                da_rows[j] = da_intra[r] + _dot_nn(do_c, state.astype(BF16))
                dkd = _dot_nn(g.v_b[r, vc], dstate_b) * g.e_rest[r, kc]
                dkd_rows[j] = dkd
                decay = jnp.exp(g.c_last[j][:, kc])
                dc_last = (jnp.sum(dkd * g.k[r, kc], axis=0, keepdims=True)
                           + decay * jnp.sum(state * dstate, axis=0, keepdims=True))
                dcl_rows[j] = jnp.where(last_row, dc_last, 0.0)
                dstate = _dot_tn(do_c, g.a_b[r, kc]) + dstate * decay
            dstate_ref[srows, :] = dstate
            da = jnp.concatenate(da_rows, axis=0)
            dkd = jnp.concatenate(dkd_rows, axis=0)
            dproj_ref[:, v_cols] = jnp.concatenate(dv_rows, axis=0).astype(BF16)
            q_up, q_down = da * g.e_pos[:, kc], dcn * g.e_neg[:, kc]
            k_up, k_down = ddp * g.e_pos[:, kc], db * g.e_neg[:, kc] + dkd
            dproj_ref[:, kc] = (Q_SCALE * (q_up + q_down)).astype(BF16)
            dproj_ref[:, k_cols] = (k_up + k_down).astype(BF16)
            dc_h.append(g.q[:, kc] * (q_up - q_down) + g.k[:, kc] * (k_up - k_down)
                        + jnp.concatenate(dcl_rows, axis=0))
        dz = _chunk_scan(jnp.concatenate(dc_h, axis=1), True) * (1.0 / GATE_NORMALIZER) * (1.0 - _sigmoid(g.z))
        dzb = dz.astype(BF16)
        dgkb_ref[...] += jnp.sum(dz, axis=0, keepdims=True)
        dgkw_ref[...] += _dot_tn(low_b, dzb)
        dproj_ref[:, GLA_QKVG_WIDTH:] = _dot_nt(dzb, gkw_ref[...]).astype(BF16)

        @pl.when(i == nt - 1)
        def _():
            dwout_stage[...] = dwout_acc[...].astype(BF16)
            pltpu.sync_copy(dwout_stage, dwout_hbm)

    rev = lambda i: (nt - 1 - i, 0)
    return pl.pallas_call(
        body, name="gla_bwd", grid=(nt,),
        in_specs=[pl.BlockSpec((tm, D_MODEL), rev), pl.BlockSpec((tm, GLA_IN_PAD), rev),
                  pl.BlockSpec((tm, GLA_VALUE_WIDTH), rev),
                  pl.BlockSpec((cpt, GLA_VALUE_WIDTH, GLA_HEAD_K), lambda i: (nt - 1 - i, 0, 0)),
                  _const((GLA_LOW_PAD, GLA_KEY_WIDTH)), _const((1, GLA_KEY_WIDTH)), _const((1, GLA_VALUE_WIDTH)),
                  _const((GLA_VALUE_WIDTH, D_MODEL))],
        out_specs=[pl.BlockSpec((tm, GLA_IN_PAD), rev), pl.BlockSpec(memory_space=pl.ANY),
                   _full((1, GLA_VALUE_WIDTH)), _full((GLA_LOW_PAD, GLA_KEY_WIDTH)), _full((1, GLA_KEY_WIDTH))],
        out_shape=[jax.ShapeDtypeStruct((seq, GLA_IN_PAD), BF16), jax.ShapeDtypeStruct((GLA_VALUE_WIDTH, D_MODEL), BF16),
                   jax.ShapeDtypeStruct((1, GLA_VALUE_WIDTH), F32), jax.ShapeDtypeStruct((GLA_LOW_PAD, GLA_KEY_WIDTH), F32),
                   jax.ShapeDtypeStruct((1, GLA_KEY_WIDTH), F32)],
        scratch_shapes=[pltpu.VMEM((GLA_VALUE_WIDTH, GLA_HEAD_K), F32), pltpu.VMEM((GLA_VALUE_WIDTH, D_MODEL), F32),
                        pltpu.VMEM((GLA_VALUE_WIDTH, D_MODEL), BF16)],
        compiler_params=_params(),
    )(dh2, proj, o, states, gkw, gkb, hw, w_out)


def _position():
    return lax.axis_index("x"), lax.axis_index("y"), lax.axis_index("c")


def _lead_slot(ref, d):
    return ref.at[d]


def _row_slot(rows):
    return lambda ref, d: ref.at[pl.ds(pl.multiple_of(d * rows, rows), rows)]


def _dim1_slot(size):
    return lambda ref, d: ref.at[:, pl.ds(pl.multiple_of(d * size, size), size)]


class _Gather:
    def __init__(self, in_refs, out_refs, slots, send_sems, recv_sems, local_sems):
        self.in_refs, self.out_refs, self.slots = in_refs, out_refs, slots
        self.send_sems, self.recv_sems, self.local_sems = send_sems, recv_sems, local_sems
        self.n = len(in_refs)
        x, y, c = _position()
        self.c = c
        self.me, self.sibling = (x, y, c), (x, y, 1 - c)
        self.near = [(1 - x, y), (x, 1 - y)]
        self.diagonal = (1 - x, 1 - y)
        self.relay_from = (x ^ c, y ^ (1 - c))
        self.relay_to = (x ^ (1 - c), y ^ c)

    def _copy(self, a, k, block, to, from_input=False):
        part = self.slots[a](self.out_refs[a], 4 * block[0] + 2 * block[1] + block[2])
        return pltpu.make_async_remote_copy(
            src_ref=self.in_refs[a] if from_input else part, dst_ref=part,
            send_sem=self.send_sems.at[a, k], recv_sem=self.recv_sems.at[a, k], device_id=to, device_id_type=MESH)

    def _mine(self):
        return [pltpu.make_async_copy(self.in_refs[a], self.slots[a](self.out_refs[a], 4 * self.me[0] + 2 * self.me[1]
                                                                    + self.me[2]), self.local_sems.at[a])
                for a in range(self.n)]

    def _first(self):
        first = [self._copy(a, 0, self.me, self.sibling, True) for a in range(self.n)]
        return first + [self._copy(a, 1 + j, self.me, (*chip, self.c), True)
                        for j, chip in enumerate(self.near) for a in range(self.n)]

    def _relayed(self):
        return [self._copy(a, 3, (*self.relay_from, self.c), (*self.relay_to, self.c)) for a in range(self.n)]

    def _passed(self, j):
        chip = self.near[j] if j < 2 else self.diagonal
        return [self._copy(a, 4 + j, (*chip, self.c), self.sibling) for a in range(self.n)]

    def start(self):
        for cp in self._mine() + self._first():
            cp.start()

    def forward(self):
        for j, chip in enumerate(self.near):
            for a in range(self.n):
                self._copy(a, 1 + j, (*chip, self.c), self.me).wait_recv()
        for cp in self._relayed() + self._passed(0) + self._passed(1):
            cp.start()

    def relay(self):
        pass

    def finish(self):
        for a in range(self.n):
            self._copy(a, 3, (*self.diagonal, self.c), self.me).wait_recv()
        for cp in self._passed(2):
            cp.start()
        for a in range(self.n):
            self._copy(a, 0, self.sibling, self.me).wait_recv()
        for j, chip in enumerate(self.near + [self.diagonal]):
            for a in range(self.n):
                self._copy(a, 4 + j, (*chip, 1 - self.c), self.me).wait_recv()
        for cp in self._first() + self._relayed() + self._passed(0) + self._passed(1) + self._passed(2):
            cp.wait_send()
        for cp in self._mine():
            cp.wait()


class _Exchange:
    def __init__(self, in_refs, out_refs, slots, send_sems, recv_sems, local_sems):
        self.in_refs, self.out_refs, self.slots = in_refs, out_refs, slots
        self.send_sems, self.recv_sems, self.local_sems = send_sems, recv_sems, local_sems
        self.n = len(in_refs)
        self.pos = _position()

    def _copies(self):
        x, y, c = self.pos
        me = 4 * x + 2 * y + c
        mine = [pltpu.make_async_copy(self.slots[a](self.in_refs[a], me), self.out_refs[a].at[me],
                                      self.local_sems.at[a]) for a in range(self.n)]
        remote = []
        for k in range(1, N_DEV):
            px, py, pc = x ^ (k >> 2), y ^ ((k >> 1) & 1), c ^ (k & 1)
            for a in range(self.n):
                remote.append(pltpu.make_async_remote_copy(
                    src_ref=self.slots[a](self.in_refs[a], 4 * px + 2 * py + pc), dst_ref=self.out_refs[a].at[me],
                    send_sem=self.send_sems.at[a, k - 1], recv_sem=self.recv_sems.at[a, k - 1],
                    device_id=(px, py, pc), device_id_type=MESH))
        return mine, remote

    def start(self):
        mine, remote = self._copies()
        for cp in mine + remote:
            cp.start()

    def forward(self):
        pass

    def relay(self):
        pass

    def finish(self):
        mine, remote = self._copies()
        for cp in remote:
            cp.wait_recv()
        for cp in remote:
            cp.wait_send()
        for cp in mine:
            cp.wait()


class _Rider:
    def __init__(self, kind, arrays, out_shapes, slots, scratch=None, forward_step=None):
        self.kind, self.arrays, self.slots = kind, list(arrays), slots
        self.n = len(self.arrays)
        hbm = pl.BlockSpec(memory_space=pl.ANY)
        self.in_specs = [hbm] * self.n
        self.out_specs = [hbm] * self.n
        self.out_shape = [jax.ShapeDtypeStruct(tuple(s), a.dtype) for s, a in zip(out_shapes, self.arrays)]
        self.scratch = scratch if scratch is not None else [
            pltpu.SemaphoreType.DMA((self.n, 7)), pltpu.SemaphoreType.DMA((self.n, 7)),
            pltpu.SemaphoreType.DMA((self.n,))]
        self.forward_step = forward_step
        self.relay_step = None

    def bind(self, in_refs, out_refs, scratch):
        return self.kind(in_refs, out_refs, self.slots, *scratch)


def _gather_rider(shards, full_shapes, slots, forward_step=None):
    return _Rider(_Gather, shards, full_shapes, slots, None, forward_step)


def _exchange_rider(sends, part_shapes, slots):
    return _Rider(_Exchange, sends, [(N_DEV,) + tuple(s) for s in part_shapes], slots)


def _split_refs(refs, n_in, n_out, n_scratch, rider):
    k = rider.n if rider is not None else 0
    ins, r_ins = refs[:n_in], refs[n_in:n_in + k]
    outs, r_outs = refs[n_in + k:n_in + k + n_out], refs[n_in + k + n_out:n_in + 2 * k + n_out]
    rest = refs[n_in + 2 * k + n_out:]
    scratch, r_scratch = rest[:n_scratch], rest[n_scratch:]
    comm = rider.bind(r_ins, r_outs, r_scratch) if rider is not None else None
    if comm is not None:
        comm.forward_step, comm.relay_step = rider.forward_step, rider.relay_step
    return ins + outs + scratch, comm


def _ride_before(comm, i, nt):
    if comm is not None:
        pl.when(i == 0)(comm.start)
        pl.when(i == (nt - 1 if comm.forward_step is None else min(comm.forward_step, nt - 1)))(comm.forward)
        pl.when(i == (nt - 1 if comm.relay_step is None else min(comm.relay_step, nt - 1)))(comm.relay)


def _ride_after(comm, i, nt):
    if comm is not None:
        pl.when(i == nt - 1)(comm.finish)


def _extend(specs, rider, field):
    return list(specs) + (getattr(rider, field) if rider is not None else [])


def _comm_call(name, rider):
    def body(*refs):
        _, comm = _split_refs(refs, 0, 0, 0, rider)
        comm.start()
        comm.forward()
        comm.relay()
        comm.finish()

    return pl.pallas_call(body, name=name, in_specs=rider.in_specs, out_specs=rider.out_specs,
                          out_shape=rider.out_shape, scratch_shapes=rider.scratch,
                          compiler_params=pltpu.CompilerParams(vmem_limit_bytes=VMEM_LIMIT))(*rider.arrays)


N_CHIPS = 4


class _TwoLevel:
    def __init__(self, in_refs, out_refs, slots, *scratch):
        self.in_refs, self.out_refs, self.slots = in_refs, out_refs, slots
        self.n = n = len(in_refs)
        self.own_bufs, self.recv_bufs, self.relay_bufs = scratch[:n], scratch[n:2 * n], scratch[2 * n:3 * n]
        self.swap_send, self.swap_recv, self.local_sems, self.chip_send, self.chip_recv = scratch[3 * n:]
        x, y, c = self.pos = _position()
        self.first = (x ^ (1 - c), y ^ c)
        self.second = (x ^ c, y ^ (1 - c))
        self.chip_index = lambda chip: 2 * chip[0] + chip[1]

    def _swap(self):
        x, y, c = self.pos
        return [pltpu.make_async_remote_copy(
            src_ref=self.slots[a](self.in_refs[a], 2 * q + 1 - c), dst_ref=self.recv_bufs[a].at[q],
            send_sem=self.swap_send.at[a, q], recv_sem=self.swap_recv.at[a, q],
            device_id=(x, y, 1 - c), device_id_type=MESH) for a in range(self.n) for q in range(N_CHIPS)]

    def _mine(self):
        c = self.pos[2]
        return [pltpu.make_async_copy(self.slots[a](self.in_refs[a], 2 * q + c), self.own_bufs[a].at[q],
                                      self.local_sems.at[a, q]) for a in range(self.n) for q in range(N_CHIPS)]

    def _to_chip(self, a, k, src, dst, chip):
        return pltpu.make_async_remote_copy(
            src_ref=src, dst_ref=dst, send_sem=self.chip_send.at[a, k], recv_sem=self.chip_recv.at[a, k],
            device_id=(*chip, self.pos[2]), device_id_type=MESH)

    def _first_wave(self):
        x, y, _ = self.pos
        diagonal = self.chip_index((1 - x, 1 - y))
        passed_on = [self._to_chip(a, 1, self.own_bufs[a].at[diagonal], self.relay_bufs[a], self.first)
                     for a in range(self.n)]
        return passed_on + [self._to_chip(a, 0, self.own_bufs[a].at[self.chip_index(self.first)],
                                          self.out_refs[a].at[1], self.first) for a in range(self.n)]

    def _second_wave(self):
        return [self._to_chip(a, 2, self.own_bufs[a].at[self.chip_index(self.second)], self.out_refs[a].at[2],
                              self.second) for a in range(self.n)]

    def _own(self):
        x, y, _ = self.pos
        return [pltpu.make_async_copy(self.own_bufs[a].at[2 * x + y], self.out_refs[a].at[0],
                                      self.local_sems.at[a, N_CHIPS]) for a in range(self.n)]

    def start(self):
        for cp in self._swap() + self._mine():
            cp.start()

    def forward(self):
        swap, mine = self._swap(), self._mine()
        for a in range(self.n):
            for q in range(N_CHIPS):
                mine[a * N_CHIPS + q].wait()
                swap[a * N_CHIPS + q].wait_recv()
                self.own_bufs[a][q] = (self.own_bufs[a][q].astype(F32)
                                       + self.recv_bufs[a][q].astype(F32)).astype(BF16)
        for cp in self._first_wave() + self._own():
            cp.start()

    def relay(self):
        second = self.chip_index(self.second)
        for a in range(self.n):
            self._to_chip(a, 1, self.relay_bufs[a], self.relay_bufs[a], self.first).wait_recv()
            self.own_bufs[a][second] = (self.own_bufs[a][second].astype(F32)
                                        + self.relay_bufs[a][...].astype(F32)).astype(BF16)
        for cp in self._second_wave():
            cp.start()

    def finish(self):
        for a in range(self.n):
            self._to_chip(a, 0, self.out_refs[a].at[1], self.out_refs[a].at[1], self.first).wait_recv()
            self._to_chip(a, 2, self.out_refs[a].at[2], self.out_refs[a].at[2], self.second).wait_recv()
        for cp in self._first_wave() + self._second_wave() + self._swap():
            cp.wait_send()
        for cp in self._own():
            cp.wait()


def _two_level_rider(sends, part_shapes, slots, forward_step=None, relay_step=None):
    n = len(sends)
    bufs = [pltpu.VMEM((N_CHIPS,) + tuple(s), a.dtype) for s, a in zip(part_shapes, sends)]
    relay_bufs = [pltpu.VMEM(tuple(s), a.dtype) for s, a in zip(part_shapes, sends)]
    scratch = bufs + bufs + relay_bufs + [
        pltpu.SemaphoreType.DMA((n, N_CHIPS)), pltpu.SemaphoreType.DMA((n, N_CHIPS)),
        pltpu.SemaphoreType.DMA((n, N_CHIPS + 1)), pltpu.SemaphoreType.DMA((n, 3)), pltpu.SemaphoreType.DMA((n, 3))]
    rider = _Rider(_TwoLevel, sends, [(3,) + tuple(s) for s in part_shapes], slots, scratch, forward_step)
    rider.relay_step = relay_step
    return rider


class _Joined:
    def __init__(self, first, second):
        self.first, self.second = first, second

    def start(self):
        self.first.start()
        self.second.start()

    def forward(self):
        self.first.forward()
        self.second.forward()

    def relay(self):
        self.first.relay()
        self.second.relay()

    def finish(self):
        self.first.finish()
        self.second.finish()


class _JoinedRider:
    def __init__(self, first, second):
        self.first, self.second = first, second
        self.n = first.n + second.n
        self.arrays = first.arrays + second.arrays
        self.in_specs = first.in_specs + second.in_specs
        self.out_specs = first.out_specs + second.out_specs
        self.out_shape = first.out_shape + second.out_shape
        self.scratch = first.scratch + second.scratch
        self.forward_step = first.forward_step
        self.relay_step = first.relay_step

    def bind(self, in_refs, out_refs, scratch):
        k, s = self.first.n, len(self.first.scratch)
        return _Joined(self.first.bind(in_refs[:k], out_refs[:k], scratch[:s]),
                       self.second.bind(in_refs[k:], out_refs[k:], scratch[s:]))


def _adamw(w, g, m, v):
    m = ADAM_B1 * m + (1.0 - ADAM_B1) * g
    v = ADAM_B2 * v + (1.0 - ADAM_B2) * (g * g)
    m_hat = m / (1.0 - ADAM_B1 ** ADAM_STEP)
    v_hat = v / (1.0 - ADAM_B2 ** ADAM_STEP)
    delta = -ADAM_LR * (m_hat / (jnp.sqrt(v_hat) + ADAM_EPS) + ADAM_WD * w)
    return delta, m, v


def _sum_parts(parts_ref, index=()):
    g = parts_ref[(0,) + index].astype(F32)
    for s in range(1, parts_ref.shape[0]):
        g = g + parts_ref[(s,) + index].astype(F32)
    return g


def _adamw_group_call(name, groups):
    k = len(groups)

    def body(*refs):
        ins, outs = refs[:4 * k], refs[4 * k:]
        for i in range(k):
            parts_ref, w_ref, m_ref, v_ref = ins[4 * i:4 * i + 4]
            g = _sum_parts(parts_ref)
            delta, m_new, v_new = _adamw(w_ref[...], g, m_ref[...], v_ref[...])
            for out_ref, value in zip(outs[4 * i:4 * i + 4], (g, delta, m_new, v_new)):
                out_ref[...] = value

    vmem = pl.BlockSpec(memory_space=pltpu.VMEM)
    res = pl.pallas_call(
        body, name=name, in_specs=[vmem] * (4 * k), out_specs=[vmem] * (4 * k),
        out_shape=[jax.ShapeDtypeStruct(grp[1].shape, F32) for grp in groups for _ in range(4)],
        compiler_params=pltpu.CompilerParams(vmem_limit_bytes=VMEM_LIMIT),
    )(*[a for grp in groups for a in grp])
    return [res[4 * i:4 * i + 4] for i in range(k)]


def _adamw_slabs_call(name, parts, w, m, v, rider=None):
    def main(parts_ref, w_ref, m_ref, v_ref, g_ref, delta_ref, m_out, v_out):
        g = _sum_parts(parts_ref)
        delta, m_new, v_new = _adamw(w_ref[...], g, m_ref[...], v_ref[...])
        g_ref[...] = g
        delta_ref[...] = delta
        m_out[...] = m_new
        v_out[...] = v_new

    def body(*refs):
        own, comm = _split_refs(refs, 4, 4, 0, rider)
        if comm is not None:
            comm.start()
        main(*own)
        if comm is not None:
            comm.forward()
            comm.relay()
            comm.finish()

    vmem = pl.BlockSpec(memory_space=pltpu.VMEM)
    return pl.pallas_call(
        body, name=name, in_specs=_extend([vmem] * 4, rider, "in_specs"),
        out_specs=_extend([vmem] * 4, rider, "out_specs"),
        out_shape=_extend([jax.ShapeDtypeStruct(w.shape, F32)] * 4, rider, "out_shape"),
        scratch_shapes=_extend([], rider, "scratch"),
        compiler_params=pltpu.CompilerParams(vmem_limit_bytes=VMEM_LIMIT),
    )(parts, w, m, v, *_extend([], rider, "arrays"))


WIDE_ROWS = 8
NARROW_ROWS = 40
NARROW_GKW_ROW = 8
NARROW_GKB_ROW = 24
NARROW_HW_ROW = 32
GROUP_SHARD = POOL_GROUP_DIM // N_DEV
KEY_SHARD = GLA_KEY_WIDTH // N_DEV
HEAD_V_SHARD = GLA_HEAD_V // N_DEV


def _small_adamw_call(wide, narrow, w, m, v):
    names = ("norm_w", "pool_scale", "final_norm_w", "pool_group_b", "gla_gk_w", "gla_gk_b", "gla_head_norm_w")
    where = {
        "norm_w": (0, slice(0, 2), slice(None)),
        "pool_scale": (0, slice(2, 3), slice(None)),
        "final_norm_w": (0, slice(3, 4), slice(None)),
        "pool_group_b": (1, slice(0, POOL_GROUPS), slice(0, GROUP_SHARD)),
        "gla_gk_w": (1, slice(NARROW_GKW_ROW, NARROW_GKW_ROW + GLA_GATE_RANK), slice(0, KEY_SHARD)),
        "gla_gk_b": (1, slice(NARROW_GKB_ROW, NARROW_GKB_ROW + 1), slice(0, KEY_SHARD)),
        "gla_head_norm_w": (1, slice(NARROW_HW_ROW, NARROW_HW_ROW + 1), slice(0, HEAD_V_SHARD)),
    }
    k = len(names)

    def body(*refs):
        parts = refs[0:2]
        w_refs, m_refs, v_refs = refs[2:2 + k], refs[2 + k:2 + 2 * k], refs[2 + 2 * k:2 + 3 * k]
        outs = refs[2 + 3 * k:]
        loss_ref = outs[0]
        loss_ref[...] = _sum_parts(parts[0], (slice(4, 5), slice(0, 1)))
        for i, name in enumerate(names):
            buf, rows, cols = where[name]
            g = _sum_parts(parts[buf], (rows, cols))
            delta, m_new, v_new = _adamw(w_refs[i][...], g, m_refs[i][...], v_refs[i][...])
            outs[1 + i][...] = g
            outs[1 + k + i][...] = delta
            outs[1 + 2 * k + i][...] = m_new
            outs[1 + 3 * k + i][...] = v_new

    vmem = pl.BlockSpec(memory_space=pltpu.VMEM)
    shapes = [jax.ShapeDtypeStruct(w[n].shape, F32) for n in names]
    res = pl.pallas_call(
        body, name="adamw_small", in_specs=[vmem] * (2 + 3 * k), out_specs=[vmem] * (1 + 4 * k),
        out_shape=[jax.ShapeDtypeStruct((1, 1), F32)] + shapes * 4,
    )(wide, narrow, *[w[n] for n in names], *[m[n] for n in names], *[v[n] for n in names])
    unzip = lambda j: dict(zip(names, res[1 + j * k:1 + (j + 1) * k]))
    return res[0], unzip(0), unzip(1), unzip(2), unzip(3)


def kernel(x, norm_w, pool_in_w, pool_group_w, pool_group_b, pool_scale, pool_out_w, gla_in_w, gla_gk_w, gla_gk_b, gla_head_norm_w, gla_out_w, final_norm_w, loss_target, m_norm_w, m_pool_in_w, m_pool_group_w, m_pool_group_b, m_pool_scale, m_pool_out_w, m_gla_in_w, m_gla_gk_w, m_gla_gk_b, m_gla_head_norm_w, m_gla_out_w, m_final_norm_w, v_norm_w, v_pool_in_w, v_pool_group_w, v_pool_group_b, v_pool_scale, v_pool_out_w, v_gla_in_w, v_gla_gk_w, v_gla_gk_b, v_gla_head_norm_w, v_gla_out_w, v_final_norm_w):
    w = dict(norm_w=norm_w, pool_in_w=pool_in_w, pool_group_w=pool_group_w, pool_group_b=pool_group_b,
             pool_scale=pool_scale, pool_out_w=pool_out_w, gla_in_w=gla_in_w, gla_gk_w=gla_gk_w, gla_gk_b=gla_gk_b,
             gla_head_norm_w=gla_head_norm_w, gla_out_w=gla_out_w, final_norm_w=final_norm_w)
    m = dict(norm_w=m_norm_w, pool_in_w=m_pool_in_w, pool_group_w=m_pool_group_w, pool_group_b=m_pool_group_b,
             pool_scale=m_pool_scale, pool_out_w=m_pool_out_w, gla_in_w=m_gla_in_w, gla_gk_w=m_gla_gk_w,
             gla_gk_b=m_gla_gk_b, gla_head_norm_w=m_gla_head_norm_w, gla_out_w=m_gla_out_w,
             final_norm_w=m_final_norm_w)
    v = dict(norm_w=v_norm_w, pool_in_w=v_pool_in_w, pool_group_w=v_pool_group_w, pool_group_b=v_pool_group_b,
             pool_scale=v_pool_scale, pool_out_w=v_pool_out_w, gla_in_w=v_gla_in_w, gla_gk_w=v_gla_gk_w,
             gla_gk_b=v_gla_gk_b, gla_head_norm_w=v_gla_head_norm_w, gla_out_w=v_gla_out_w,
             final_norm_w=v_final_norm_w)
    col_shard = GLA_IN_WIDTH // N_DEV
    row_shard = D_MODEL // N_DEV

    def lanes(a):
        return jnp.pad(a, [(0, 0)] * (a.ndim - 1) + [(0, LANES - a.shape[-1])])

    small_in = jnp.concatenate([lanes(pool_group_b[0]), lanes(gla_gk_b), lanes(gla_head_norm_w),
                                jnp.zeros((2, LANES), F32)], axis=0)
    in_cols = 2 * POOL_WIDTH // N_DEV
    pool_in, pool_gw, pool_out, small_all = _comm_call("pool_weights_all_gather", _gather_rider(
        [pool_in_w[0].astype(BF16), pool_group_w[0].astype(BF16), pool_out_w[0].astype(BF16), small_in],
        [(D_MODEL, 2 * POOL_WIDTH), (POOL_GROUPS, POOL_GROUP_DIM, POOL_GROUP_DIM), (POOL_WIDTH, D_MODEL),
         (N_DEV, 8, LANES)],
        [_dim1_slot(in_cols), _dim1_slot(GROUP_SHARD), _row_slot(row_shard), _lead_slot]))
    pool_gb = jnp.transpose(small_all[:, 0:POOL_GROUPS, :GROUP_SHARD], (1, 0, 2)).reshape(1, POOL_WIDTH)
    gla_gkb = small_all[:, POOL_GROUPS, :KEY_SHARD].reshape(1, GLA_KEY_WIDTH)
    gla_hw = jnp.tile(small_all[:, POOL_GROUPS + 1, :HEAD_V_SHARD].reshape(1, GLA_HEAD_V), (1, GLA_HEADS))
    nw0, nw1, wf = norm_w[0:1], norm_w[1:2], final_norm_w.reshape(1, D_MODEL)
    xs, target = x[0], loss_target[0]

    h1, p, gla_in_parts, gkw_parts, gla_out = _pool_fwd_call(
        xs, nw0, pool_in, pool_gw, pool_gb, pool_scale, pool_out, _gather_rider(
            [jnp.transpose(gla_in_w[0]).astype(BF16), gla_gk_w[0].astype(BF16), gla_out_w[0].astype(BF16)],
            [(N_DEV, col_shard, D_MODEL), (N_DEV, GLA_GATE_RANK, KEY_SHARD), (GLA_VALUE_WIDTH, D_MODEL)],
            [_lead_slot, _lead_slot, _row_slot(row_shard)], GATHER_RELAY_STEP))
    gla_in = gla_in_parts.reshape(GLA_IN_WIDTH, D_MODEL)
    gla_gkw = jnp.pad(jnp.transpose(gkw_parts, (1, 0, 2)).reshape(GLA_GATE_RANK, GLA_KEY_WIDTH),
                      ((0, GLA_LOW_PAD - GLA_GATE_RANK), (0, 0)))
    dh2, proj, o, states, loss_part, dwf = _gla_fwd_call(h1, nw1, gla_in, gla_gkw, gla_gkb, gla_hw, gla_out, wf, target)

    dproj, d_gla_out, dhw, dgkw, dgkb = _gla_bwd_call(dh2, proj, o, states, gla_gkw, gla_gkb, gla_hw, gla_out)
    dh1, d_gla_in, dnw1, landed_gla_out = _inproj_bwd_call(
        "gla_in_bwd", dproj, h1, nw1, gla_in, dh2,
        _exchange_rider([d_gla_out], [(row_shard, D_MODEL)], [_row_slot(row_shard)]), transposed=True)
    slabs = col_shard * D_MODEL // (BF16_ROWS * LANES)
    gla_in_send = d_gla_in.reshape(N_DEV, slabs, BF16_ROWS, LANES)
    dp, d_pool_out, dgw, dgb, dsc, landed_gla_in = _pool_bwd_call(
        dh1, p, pool_gw, pool_gb, pool_scale, pool_out,
        _two_level_rider([gla_in_send], [(slabs, BF16_ROWS, LANES)], [_lead_slot], TWO_LEVEL_ADD_STEP,
                         TWO_LEVEL_RELAY_STEP))
    grad_x, d_pool_in, dnw0 = _inproj_bwd_call("pool_in_bwd", dp, xs, nw0, pool_in, dh1)

    wide = jnp.concatenate([
        dnw0, dnw1, dsc, dwf, jnp.pad(loss_part[0:1, 0:1], ((0, 0), (0, D_MODEL - 1))),
        jnp.zeros((WIDE_ROWS - 5, D_MODEL), F32)], axis=0)

    def rows8(a):
        return jnp.pad(lanes(a), ((0, 0), (0, -a.shape[1] % 8), (0, 0)))

    narrow = jnp.concatenate([
        rows8(jnp.transpose(dgb.reshape(POOL_GROUPS, N_DEV, GROUP_SHARD), (1, 0, 2))),
        rows8(jnp.transpose(dgkw[:GLA_GATE_RANK].reshape(GLA_GATE_RANK, N_DEV, KEY_SHARD), (1, 0, 2))),
        rows8(dgkb.reshape(N_DEV, 1, KEY_SHARD)),
        rows8(dhw.reshape(GLA_HEADS, GLA_HEAD_V).sum(axis=0).reshape(N_DEV, 1, HEAD_V_SHARD)),
    ], axis=1)
    last_exchange = _JoinedRider(
        _two_level_rider([d_pool_in, d_pool_out, dgw],
                         [(D_MODEL, in_cols), (row_shard, D_MODEL), (POOL_GROUPS, GROUP_SHARD, POOL_GROUP_DIM)],
                         [_dim1_slot(in_cols), _row_slot(row_shard), _dim1_slot(GROUP_SHARD)]),
        _exchange_rider([wide, narrow], [(WIDE_ROWS, D_MODEL), (NARROW_ROWS, LANES)],
                        [lambda ref, d: ref, _lead_slot]))

    res = {}
    as_slabs = lambda t: jnp.transpose(t[0]).reshape(slabs, BF16_ROWS, LANES)
    *outs, landed_pool_in, landed_pool_out, landed_gw, landed_wide, landed_narrow = _adamw_slabs_call(
        "adamw_gla_in_w", landed_gla_in, as_slabs(gla_in_w), as_slabs(m_gla_in_w), as_slabs(v_gla_in_w),
        last_exchange)
    res["gla_in_w"] = [jnp.transpose(t.reshape(col_shard, D_MODEL))[None] for t in outs]
    rest = [("pool_in_w", landed_pool_in, (D_MODEL, in_cols)),
            ("pool_group_w", landed_gw, (POOL_GROUPS * GROUP_SHARD, POOL_GROUP_DIM)),
            ("pool_out_w", landed_pool_out, (row_shard, D_MODEL)), ("gla_out_w", landed_gla_out, (row_shard, D_MODEL))]
    updates = _adamw_group_call("adamw_matrices", [
        (parts.reshape((parts.shape[0],) + shape), w[name].reshape(shape), m[name].reshape(shape),
         v[name].reshape(shape)) for name, parts, shape in rest])
    for (name, _, _), outs in zip(rest, updates):
        res[name] = [t.reshape(w[name].shape) for t in outs]
    small_shapes ={"norm_w": (2, D_MODEL), "pool_scale": (1, D_MODEL), "final_norm_w": (1, D_MODEL),
                    "pool_group_b": (POOL_GROUPS, GROUP_SHARD), "gla_gk_w": (GLA_GATE_RANK, KEY_SHARD),
                    "gla_gk_b": (1, KEY_SHARD), "gla_head_norm_w": (1, HEAD_V_SHARD)}
    as_small = lambda t: {n: t[n].reshape(s) for n, s in small_shapes.items()}
    loss, *small_outs = _small_adamw_call(landed_wide, landed_narrow, as_small(w), as_small(m), as_small(v))
    for name in small_shapes:
        res[name] = [t[name].reshape(w[name].shape) for t in small_outs]
    order = ("norm_w", "pool_in_w", "pool_group_w", "pool_group_b", "pool_scale", "pool_out_w", "gla_in_w",
             "gla_gk_w", "gla_gk_b", "gla_head_norm_w", "gla_out_w", "final_norm_w")
    return (loss.reshape(()), grad_x[None], *[res[n][0] for n in order], *[res[n][1] for n in order],
            *[res[n][2] for n in order], *[res[n][3] for n in order])
```

```python
import jax
import jax.numpy as jnp
from jax import lax
from jax.experimental import pallas as pl
from jax.experimental.pallas import tpu as pltpu

F32 = jnp.float32
BF16 = jnp.bfloat16
MESH = pl.DeviceIdType.MESH

N_DEV = 8
D_MODEL = 1024
POOL_WIDTH = 1024
POOL_GROUPS = 4
POOL_GROUP_DIM = 256
POOL_HALO = 16
GLA_HEADS = 4
GLA_HEAD_K = 128
GLA_HEAD_V = 256
GLA_KEY_WIDTH = 512
GLA_VALUE_WIDTH = 1024
GLA_GATE_RANK = 16
GLA_IN_WIDTH = 3088
GLA_IN_PAD = 3200
GLA_LOW_PAD = 128
GLA_QKVG_WIDTH = 3072
CHUNK = 64
GATE_NORMALIZER = 16.0
RMS_EPS = 1e-6
Q_SCALE = GLA_HEAD_K ** -0.5

ADAM_LR = 0.001
ADAM_B1 = 0.9
ADAM_B2 = 0.999
ADAM_EPS = 1e-08
ADAM_WD = 0.01
ADAM_STEP = 10

LANES = 128
BF16_ROWS = 16
VMEM_LIMIT = 56 * 1024 * 1024
ROW_TILE = 256
MATMUL_ROW_TILE = 512
GATHER_RELAY_STEP = 4
TWO_LEVEL_ADD_STEP = 1
TWO_LEVEL_RELAY_STEP = 4


def _dot_nn(a, b):
    return lax.dot_general(a, b, (((1,), (0,)), ((), ())), preferred_element_type=F32)


def _dot_nt(a, b):
    return lax.dot_general(a, b, (((1,), (1,)), ((), ())), preferred_element_type=F32)


def _dot_tn(a, b):
    return lax.dot_general(a, b, (((0,), (0,)), ((), ())), preferred_element_type=F32)


def _rms(x):
    rstd = lax.rsqrt(jnp.mean(x * x, axis=-1, keepdims=True) + RMS_EPS)
    return x * rstd, rstd


def _rms_bwd(dxhat, xhat, rstd):
    return rstd * (dxhat - xhat * jnp.mean(dxhat * xhat, axis=-1, keepdims=True))


def _sigmoid(x):
    return 1.0 / (1.0 + jnp.exp(-x))


def _params(sem=("arbitrary",)):
    return pltpu.CompilerParams(dimension_semantics=sem, vmem_limit_bytes=VMEM_LIMIT)


def _full(shape):
    return pl.BlockSpec(shape, lambda i: (0,) * len(shape))


def _const(shape):
    return pl.BlockSpec(shape, lambda i: (0,) * len(shape), pipeline_mode=pl.Buffered(1))


def _window_sums(ext, forward):
    n = ext.shape[0]
    outs = []
    for g in range(POOL_GROUPS):
        s = ext[:, g * POOL_GROUP_DIM:(g + 1) * POOL_GROUP_DIM]
        for k in range(g + 1):
            shift = (1 << k) if forward else n - (1 << k)
            s = s + pltpu.roll(s, shift, axis=0)
        outs.append(s[:n - POOL_HALO])
    return outs


def _inv_count(row0, tm):
    row = row0 + lax.broadcasted_iota(jnp.int32, (tm, 1), 0)
    return [1.0 / jnp.minimum(row + 1, 2 << g).astype(F32) for g in range(POOL_GROUPS)]


def _pool_mix(u, u_prev, row0, gw_ref, gb):
    tm = u.shape[0]
    sums = _window_sums(jnp.concatenate([u, u_prev], axis=0), True)
    inv = _inv_count(row0, tm)
    pooled, mixed = [], []
    for g in range(POOL_GROUPS):
        ug = u[:, g * POOL_GROUP_DIM:(g + 1) * POOL_GROUP_DIM]
        pg = (sums[g] * inv[g] - ug).astype(BF16)
        pooled.append(pg)
        mixed.append(_dot_nn(pg, gw_ref[g]))
    return pooled, jnp.concatenate(mixed, axis=1) + gb


def _pool_fwd_call(x, nw, sc, own_weights, next_weights):
    seq = x.shape[0]
    tm = min(MATMUL_ROW_TILE, seq)
    nt = seq // tm
    riders = _JoinedRider(own_weights, next_weights)

    def main(x_ref, nw_ref, sc_ref, h_ref, p_ref, halo_ref, win_ref, gw_ref, wout_ref, gb_ref):
        i = pl.program_id(0)
        xt = x_ref[...]
        xhat, _ = _rms(xt)
        n = (xhat * nw_ref[...]).astype(BF16)
        p = _dot_nn(n, win_ref[...])
        p_ref[...] = p
        u = p[:, :POOL_WIDTH]
        gate = p[:, POOL_WIDTH:]
        _, mixed = _pool_mix(u, halo_ref[...], i * tm, gw_ref, gb_ref[...])
        halo_ref[...] = u[tm - POOL_HALO:, :]
        y = (mixed * sc_ref[...] * (gate * _sigmoid(gate))).astype(BF16)
        h_ref[...] = xt + _dot_nn(y, wout_ref[...])

    def body(*refs):
        own, comm = _split_refs(refs, 3, 2, 5, riders)
        halo_ref, win_ref, gw_ref, wout_ref, gb_ref = own[5:]
        i = pl.program_id(0)

        @pl.when(i == 0)
        def _():
            halo_ref[...] = jnp.zeros_like(halo_ref)
            comm.first.start()
            comm.first.forward()
            comm.second.start()
            comm.first.finish()
            gathered = comm.first.out_refs
            pltpu.sync_copy(gathered[0], win_ref)
            pltpu.sync_copy(gathered[1], gw_ref)
            pltpu.sync_copy(gathered[2], wout_ref)
            pltpu.sync_copy(gathered[3], gb_ref)

        pl.when(i == min(GATHER_RELAY_STEP, nt - 1))(comm.second.forward)
        main(*own[:5], halo_ref, win_ref, gw_ref, wout_ref, _RowSum(gb_ref, SMALL_ROW_GROUP_B))
        pl.when(i == nt - 1)(comm.second.finish)

    return pl.pallas_call(
        body, name="pool_fwd", grid=(nt,),
        in_specs=_extend([pl.BlockSpec((tm, D_MODEL), lambda i: (i, 0)), _const((1, D_MODEL)),
                          _const((1, POOL_WIDTH))], riders, "in_specs"),
        out_specs=_extend([pl.BlockSpec((tm, D_MODEL), lambda i: (i, 0)),
                           pl.BlockSpec((tm, 2 * POOL_WIDTH), lambda i: (i, 0))], riders, "out_specs"),
        out_shape=_extend([jax.ShapeDtypeStruct((seq, D_MODEL), F32),
                           jax.ShapeDtypeStruct((seq, 2 * POOL_WIDTH), F32)], riders, "out_shape"),
        scratch_shapes=_extend([pltpu.VMEM((POOL_HALO, POOL_WIDTH), F32), pltpu.VMEM((D_MODEL, 2 * POOL_WIDTH), BF16),
                                pltpu.VMEM((POOL_GROUPS, POOL_GROUP_DIM, POOL_GROUP_DIM), BF16),
                                pltpu.VMEM((POOL_WIDTH, D_MODEL), BF16),
                                pltpu.VMEM((N_DEV, SMALL_ROWS, D_MODEL), F32)], riders, "scratch"),
        compiler_params=_params(),
    )(x, nw, sc, *riders.arrays)


SMALL_ROWS = 8
SMALL_ROW_GROUP_B, SMALL_ROW_GK_B, SMALL_ROW_HEAD_NORM = 0, 1, 2


class _RowSum:
    def __init__(self, ref, row):
        self.ref, self.row = ref, row

    def __getitem__(self, _):
        total = self.ref[0, self.row:self.row + 1, :]
        for d in range(1, N_DEV):
            total = total + self.ref[d, self.row:self.row + 1, :]
        return total


def _pool_bwd_call(dh, p, gw, gb, sc, w_out, rider=None):
    seq = dh.shape[0]
    tm = min(MATMUL_ROW_TILE, seq)
    nt = seq // tm
    halo_blocks = tm // POOL_HALO

    def main(dh_ref, p_ref, pprev_ref, gw_ref, gb_ref, sc_ref, wout_ref,
             dp_ref, dwout_hbm, dgw_hbm, dgb_ref, dsc_ref, carry_ref, dwout_acc, dgw_acc, dwout_stage, dgw_stage):
        i = pl.program_id(0)
        t = nt - 1 - i

        @pl.when(i == 0)
        def _():
            carry_ref[...] = jnp.zeros_like(carry_ref)
            dwout_acc[...] = jnp.zeros_like(dwout_acc)
            dgw_acc[...] = jnp.zeros_like(dgw_acc)
            dgb_ref[...] = jnp.zeros_like(dgb_ref)
            dsc_ref[...] = jnp.zeros_like(dsc_ref)

        dhb = dh_ref[...].astype(BF16)
        dy = _dot_nt(dhb, wout_ref[...])
        p = p_ref[...]
        u = p[:, :POOL_WIDTH]
        gate = p[:, POOL_WIDTH:]
        u_prev = jnp.where(t > 0, pprev_ref[:, :POOL_WIDTH], 0.0)
        pooled, mixed = _pool_mix(u, u_prev, t * tm, gw_ref, gb_ref[...])
        sg = _sigmoid(gate)
        silu = gate * sg
        sc = sc_ref[...]
        y = (mixed * sc * silu).astype(BF16)
        dwout_acc[...] += _dot_tn(y, dhb)
        dmixed = dy * sc * silu
        dsc_ref[...] += jnp.sum(dy * mixed * silu, axis=0, keepdims=True)
        dgate = dy * mixed * sc * (sg * (1.0 + gate * (1.0 - sg)))
        dgb_ref[...] += jnp.sum(dmixed, axis=0, keepdims=True)
        inv = _inv_count(t * tm, tm)
        dpooled, scaled = [], []
        for g in range(POOL_GROUPS):
            dmg = dmixed[:, g * POOL_GROUP_DIM:(g + 1) * POOL_GROUP_DIM].astype(BF16)
            dgw_acc[g] += _dot_tn(pooled[g], dmg)
            dpg = _dot_nt(dmg, gw_ref[g])
            dpooled.append(dpg)
            scaled.append(dpg * inv[g])
        r = jnp.concatenate(scaled, axis=1)
        sums = _window_sums(jnp.concatenate([r, carry_ref[...]], axis=0), False)
        carry_ref[...] = r[:POOL_HALO, :]
        du = jnp.concatenate([sums[g] - dpooled[g] for g in range(POOL_GROUPS)], axis=1)
        dp_ref[...] = jnp.concatenate([du, dgate], axis=1).astype(BF16)

        @pl.when(i == nt - 1)
        def _():
            dwout_stage[...] = dwout_acc[...].astype(BF16)
            dgw_stage[...] = dgw_acc[...].astype(BF16)
            pltpu.sync_copy(dwout_stage, dwout_hbm)
            pltpu.sync_copy(dgw_stage, dgw_hbm)

    def body(*refs):
        own, comm = _split_refs(refs, 7, 5, 5, rider)
        _ride_before(comm, pl.program_id(0), nt)
        main(*own)
        _ride_after(comm, pl.program_id(0), nt)

    rev = lambda i: (nt - 1 - i, 0)
    return pl.pallas_call(
        body, name="pool_bwd", grid=(nt,),
        in_specs=_extend([pl.BlockSpec((tm, D_MODEL), rev), pl.BlockSpec((tm, 2 * POOL_WIDTH), rev),
                          pl.BlockSpec((POOL_HALO, 2 * POOL_WIDTH),
                                       lambda i: (jnp.maximum((nt - 1 - i) * halo_blocks - 1, 0), 0)),
                          _const((POOL_GROUPS, POOL_GROUP_DIM, POOL_GROUP_DIM)), _const((1, POOL_WIDTH)),
                          _const((1, POOL_WIDTH)), _const((POOL_WIDTH, D_MODEL))], rider, "in_specs"),
        out_specs=_extend([pl.BlockSpec((tm, 2 * POOL_WIDTH), rev), pl.BlockSpec(memory_space=pl.ANY),
                           pl.BlockSpec(memory_space=pl.ANY), _full((1, POOL_WIDTH)), _full((1, POOL_WIDTH))],
                          rider, "out_specs"),
        out_shape=_extend([jax.ShapeDtypeStruct((seq, 2 * POOL_WIDTH), BF16),
                           jax.ShapeDtypeStruct((POOL_WIDTH, D_MODEL), BF16),
                           jax.ShapeDtypeStruct((POOL_GROUPS, POOL_GROUP_DIM, POOL_GROUP_DIM), BF16),
                           jax.ShapeDtypeStruct((1, POOL_WIDTH), F32), jax.ShapeDtypeStruct((1, POOL_WIDTH), F32)],
                          rider, "out_shape"),
        scratch_shapes=_extend([pltpu.VMEM((POOL_HALO, POOL_WIDTH), F32), pltpu.VMEM((POOL_WIDTH, D_MODEL), F32),
                                pltpu.VMEM((POOL_GROUPS, POOL_GROUP_DIM, POOL_GROUP_DIM), F32),
                                pltpu.VMEM((POOL_WIDTH, D_MODEL), BF16),
                                pltpu.VMEM((POOL_GROUPS, POOL_GROUP_DIM, POOL_GROUP_DIM), BF16)], rider, "scratch"),
        compiler_params=_params(),
    )(dh, p, p, gw, gb, sc, w_out, *_extend([], rider, "arrays"))


def _rows_then_zeros(ref, lo, hi, rows):
    part = ref[lo:hi, :]
    return jnp.concatenate([part, jnp.zeros((rows - (hi - lo), part.shape[1]), part.dtype)], axis=0)


def _inproj_bwd_call(name, dproj, h_in, nw, w_in, dres, rider=None, transposed=False):
    seq = h_in.shape[0]
    width = dproj.shape[1]
    w_shape = tuple(w_in.shape)
    acc_shape = (width, D_MODEL) if transposed else w_shape
    whole = w_shape[0] // LANES * LANES
    tm = min(MATMUL_ROW_TILE, seq)
    nt = seq // tm

    def main(dproj_ref, h_ref, nw_ref, win_ref, dres_ref, dh_ref, dw_hbm, dnw_ref, dw_acc, dw_stage):
        i = pl.program_id(0)

        @pl.when(i == 0)
        def _():
            dw_acc[...] = jnp.zeros_like(dw_acc)
            dnw_ref[...] = jnp.zeros_like(dnw_ref)

        dpb = dproj_ref[...]
        if transposed:
            dn = _dot_nn(dpb[:, :whole], win_ref[0:whole, :])
            if whole < w_shape[0]:
                dn = dn + _dot_nn(dpb[:, whole:], _rows_then_zeros(win_ref, whole, w_shape[0], width - whole))
        else:
            dn = _dot_nt(dpb, win_ref[...])
        xhat, rstd = _rms(h_ref[...])
        nw_row = nw_ref[...]
        n = (xhat * nw_row).astype(BF16)
        dw_acc[...] += _dot_tn(dpb, n) if transposed else _dot_tn(n, dpb)
        dnw_ref[...] += jnp.sum(dn * xhat, axis=0, keepdims=True)
        dh_ref[...] = _rms_bwd(dn * nw_row, xhat, rstd) + dres_ref[...]

        @pl.when(i == nt - 1)
        def _():
            dw_stage[...] = dw_acc[...].astype(BF16)
            pltpu.sync_copy(dw_stage.at[pl.ds(0, w_shape[0])], dw_hbm)

    def body(*refs):
        own, comm = _split_refs(refs, 5, 3, 2, rider)
        _ride_before(comm, pl.program_id(0), nt)
        main(*own)
        _ride_after(comm, pl.program_id(0), nt)

    row = lambda i: (i, 0)
    return pl.pallas_call(
        body, name=name, grid=(nt,),
        in_specs=_extend([pl.BlockSpec((tm, width), row), pl.BlockSpec((tm, D_MODEL), row), _const((1, D_MODEL)),
                          _const(w_shape), pl.BlockSpec((tm, D_MODEL), row)], rider, "in_specs"),
        out_specs=_extend([pl.BlockSpec((tm, D_MODEL), row), pl.BlockSpec(memory_space=pl.ANY),
                           _full((1, D_MODEL))], rider, "out_specs"),
        out_shape=_extend([jax.ShapeDtypeStruct((seq, D_MODEL), F32), jax.ShapeDtypeStruct(w_shape, BF16),
                           jax.ShapeDtypeStruct((1, D_MODEL), F32)], rider, "out_shape"),
        scratch_shapes=_extend([pltpu.VMEM(acc_shape, F32), pltpu.VMEM(acc_shape, BF16)], rider, "scratch"),
        compiler_params=_params(),
    )(dproj, h_in, nw, w_in, dres, *_extend([], rider, "arrays"))


def _chunk_scan(x, reverse):
    n = x.shape[0]
    pos = lax.broadcasted_iota(jnp.int32, (n, 1), 0) & (CHUNK - 1)
    k = 1
    while k < CHUNK:
        if reverse:
            x = x + jnp.where(pos < CHUNK - k, pltpu.roll(x, n - k, axis=0), 0.0)
        else:
            x = x + jnp.where(pos >= k, pltpu.roll(x, k, axis=0), 0.0)
        k *= 2
    return x


def _chunk_rows(j):
    return slice(j * CHUNK, (j + 1) * CHUNK)


def _kcols(h):
    return slice(h * GLA_HEAD_K, (h + 1) * GLA_HEAD_K)


def _vcols(h):
    return slice(h * GLA_HEAD_V, (h + 1) * GLA_HEAD_V)


def _chunk_masks(tm):
    idx_t = lax.broadcasted_iota(jnp.int32, (tm, tm), 0)
    idx_s = lax.broadcasted_iota(jnp.int32, (tm, tm), 1)
    same_chunk = (idx_t ^ idx_s) < CHUNK
    return same_chunk & (idx_t >= idx_s), same_chunk & (idx_t < idx_s)


class _GlaTerms:
    def __init__(self, kc, q, k, v, low_b, gkw_ref, gkb_ref, masks):
        tm = q.shape[0]
        self.q = q * Q_SCALE
        self.k = k
        self.z = _dot_nn(low_b, gkw_ref[:, kc]) + gkb_ref[:, kc]
        log_g = (jnp.minimum(self.z, 0.0) - jnp.log(1.0 + jnp.exp(-jnp.abs(self.z)))) / GATE_NORMALIZER
        self.c = _chunk_scan(log_g, False)
        is_last = lax.broadcasted_iota(jnp.int32, (CHUNK, 1), 0) == CHUNK - 1
        self.c_last = [jnp.sum(jnp.where(is_last, self.c[_chunk_rows(j), :], 0.0), axis=0, keepdims=True)
                       for j in range(tm // CHUNK)]
        c_last_rows = jnp.concatenate([jnp.broadcast_to(r, (CHUNK, r.shape[1])) for r in self.c_last], axis=0)
        self.e_pos = jnp.exp(self.c)
        self.e_neg = jnp.exp(-self.c)
        self.e_rest = jnp.exp(c_last_rows - self.c)
        self.a_b = (self.q * self.e_pos).astype(BF16)
        self.b_b = (self.k * self.e_neg).astype(BF16)
        self.cn_b = (self.q * self.e_neg).astype(BF16)
        self.dp_b = (self.k * self.e_pos).astype(BF16)
        self.kd_b = (self.k * self.e_rest).astype(BF16)
        self.v_b = v.astype(BF16)
        self.lower, self.upper = masks

    def scores(self, kc=slice(None)):
        fwd = _dot_nt(self.a_b[:, kc], self.b_b[:, kc])
        bwd = _dot_nt(self.cn_b[:, kc], self.dp_b[:, kc])
        return jnp.where(self.lower, fwd, jnp.where(self.upper, bwd, 0.0)).astype(BF16)


def _gla_fwd_call(h1, nw, w_in, gkw, gkb, hw, w_out, wf, target):
    seq = h1.shape[0]
    tm = ROW_TILE
    nt = seq // tm
    cpt = tm // CHUNK
    n_chunks = seq // CHUNK

    def body(h_ref, nw_ref, win_ref, gkw_ref, gkb_ref, hw_ref, wout_ref, wf_ref, tgt_ref,
             dh2_ref, proj_ref, o_ref, st_ref, loss_ref, dwf_ref, state_ref):
        i = pl.program_id(0)

        @pl.when(i == 0)
        def _():
            state_ref[...] = jnp.zeros_like(state_ref)
            loss_ref[...] = jnp.zeros_like(loss_ref)
            dwf_ref[...] = jnp.zeros_like(dwf_ref)

        ht = h_ref[...]
        xhat, _ = _rms(ht)
        n = (xhat * nw_ref[...]).astype(BF16)
        sections = {}
        for name, lo, hi in (("low", GLA_QKVG_WIDTH, GLA_IN_PAD), ("qk", 0, 2 * GLA_KEY_WIDTH),
                             ("v", 2 * GLA_KEY_WIDTH, GLA_QKVG_WIDTH - GLA_VALUE_WIDTH),
                             ("gate", GLA_QKVG_WIDTH - GLA_VALUE_WIDTH, GLA_QKVG_WIDTH)):
            rows = (win_ref[lo:hi, :] if hi <= GLA_IN_WIDTH
                    else _rows_then_zeros(win_ref, lo, GLA_IN_WIDTH, hi - lo))
            sections[name] = _dot_nt(n, rows)
            proj_ref[:, lo:hi] = sections[name]
        low_b = sections["low"].astype(BF16)
        masks = _chunk_masks(tm)
        on_heads = []
        for h in range(GLA_HEADS):
            kc, vc = _kcols(h), _vcols(h)
            g = _GlaTerms(kc, sections["qk"][:, kc], sections["qk"][:, GLA_KEY_WIDTH:][:, kc], sections["v"][:, vc],
                          low_b, gkw_ref, gkb_ref, masks)
            srows = slice(h * GLA_HEAD_V, (h + 1) * GLA_HEAD_V)
            o_intra = _dot_nn(g.scores(), g.v_b)
            state = state_ref[srows, :]
            o_rows = []
            for j in range(cpt):
                r = _chunk_rows(j)
                st_ref[j, srows, :] = state
                o_rows.append(o_intra[r] + _dot_nt(g.a_b[r], state.astype(BF16)))
                decay = jnp.exp(g.c_last[j])
                state = state * decay + _dot_tn(g.v_b[r], g.kd_b[r])
            state_ref[srows, :] = state
            o_head = jnp.concatenate(o_rows, axis=0)
            o_ref[:, vc] = o_head
            on_heads.append(_rms(o_head)[0])
        gate = sections["gate"]
        on = jnp.concatenate(on_heads, axis=1) * hw_ref[...]
        y = (on * (gate * _sigmoid(gate))).astype(BF16)
        h2 = ht + _dot_nn(y, wout_ref[...])
        xhat2, rstd2 = _rms(h2)
        wf_row = wf_ref[...]
        err = xhat2 * wf_row - tgt_ref[...]
        loss_ref[...] += 0.5 * jnp.sum(err * err) / D_MODEL
        dout = err * (1.0 / D_MODEL)
        dwf_ref[...] += jnp.sum(dout * xhat2, axis=0, keepdims=True)
        dh2_ref[...] = _rms_bwd(dout * wf_row, xhat2, rstd2)

    row = lambda i: (i, 0)
    return pl.pallas_call(
        body, name="gla_fwd", grid=(nt,),
        in_specs=[pl.BlockSpec((tm, D_MODEL), row), _const((1, D_MODEL)), _const((GLA_IN_WIDTH, D_MODEL)),
                  _const((GLA_LOW_PAD, GLA_KEY_WIDTH)), _const((1, GLA_KEY_WIDTH)), _const((1, GLA_VALUE_WIDTH)),
                  _const((GLA_VALUE_WIDTH, D_MODEL)), _const((1, D_MODEL)), pl.BlockSpec((tm, D_MODEL), row)],
        out_specs=[pl.BlockSpec((tm, D_MODEL), row), pl.BlockSpec((tm, GLA_IN_PAD), row),
                   pl.BlockSpec((tm, GLA_VALUE_WIDTH), row),
                   pl.BlockSpec((cpt, GLA_VALUE_WIDTH, GLA_HEAD_K), lambda i: (i, 0, 0)),
                   _full((8, LANES)), _full((1, D_MODEL))],
        out_shape=[jax.ShapeDtypeStruct((seq, D_MODEL), F32), jax.ShapeDtypeStruct((seq, GLA_IN_PAD), F32),
                   jax.ShapeDtypeStruct((seq, GLA_VALUE_WIDTH), F32),
                   jax.ShapeDtypeStruct((n_chunks, GLA_VALUE_WIDTH, GLA_HEAD_K), F32),
                   jax.ShapeDtypeStruct((8, LANES), F32), jax.ShapeDtypeStruct((1, D_MODEL), F32)],
        scratch_shapes=[pltpu.VMEM((GLA_VALUE_WIDTH, GLA_HEAD_K), F32)],
        compiler_params=_params(),
    )(h1, nw, w_in, gkw, gkb, hw, w_out, wf, target)


def _gla_bwd_call(dh2, proj, o, states, gkw, gkb, hw, w_out):
    seq = dh2.shape[0]
    tm = ROW_TILE
    nt = seq // tm
    cpt = tm // CHUNK

    def body(dh_ref, proj_ref, o_ref, st_ref, gkw_ref, gkb_ref, hw_ref, wout_ref,
             dproj_ref, dwout_hbm, dhw_ref, dgkw_ref, dgkb_ref, dstate_ref, dwout_acc, dwout_stage):
        i = pl.program_id(0)

        @pl.when(i == 0)
        def _():
            dstate_ref[...] = jnp.zeros_like(dstate_ref)
            dwout_acc[...] = jnp.zeros_like(dwout_acc)
            dhw_ref[...] = jnp.zeros_like(dhw_ref)
            dgkw_ref[...] = jnp.zeros_like(dgkw_ref)
            dgkb_ref[...] = jnp.zeros_like(dgkb_ref)

        dhb = dh_ref[...].astype(BF16)
        dy = _dot_nt(dhb, wout_ref[...])
        v0, g0 = 2 * GLA_KEY_WIDTH, GLA_QKVG_WIDTH - GLA_VALUE_WIDTH
        gate = proj_ref[:, g0:GLA_QKVG_WIDTH]
        low_b = proj_ref[:, GLA_QKVG_WIDTH:].astype(BF16)
        o = o_ref[...]
        hw_row = hw_ref[...]
        sg = _sigmoid(gate)
        silu = gate * sg
        don = dy * silu
        on_parts, do_parts, dhw_parts = [], [], []
        for h in range(GLA_HEADS):
            vc = _vcols(h)
            xh, rs = _rms(o[:, vc])
            on_parts.append(xh * hw_row[:, vc])
            dhw_parts.append(jnp.sum(don[:, vc] * xh, axis=0, keepdims=True))
            do_parts.append(_rms_bwd(don[:, vc] * hw_row[:, vc], xh, rs).astype(BF16))
        on = jnp.concatenate(on_parts, axis=1)
        dwout_acc[...] += _dot_tn((on * silu).astype(BF16), dhb)
        dhw_ref[...] += jnp.concatenate(dhw_parts, axis=1)
        dproj_ref[:, g0:GLA_QKVG_WIDTH] = (dy * on * (sg * (1.0 + gate * (1.0 - sg)))).astype(BF16)

        last_row = lax.broadcasted_iota(jnp.int32, (CHUNK, 1), 0) == CHUNK - 1
        g = _GlaTerms(slice(0, GLA_KEY_WIDTH), proj_ref[:, :GLA_KEY_WIDTH], proj_ref[:, GLA_KEY_WIDTH:v0],
                      proj_ref[:, v0:g0], low_b, gkw_ref, gkb_ref, _chunk_masks(tm))
        dc_h = []
        for h in range(GLA_HEADS):
            kc, vc = _kcols(h), _vcols(h)
            k_cols = slice(GLA_KEY_WIDTH + kc.start, GLA_KEY_WIDTH + kc.stop)
            v_cols = slice(v0 + vc.start, v0 + vc.stop)
            do_h = do_parts[h]
            srows = slice(h * GLA_HEAD_V, (h + 1) * GLA_HEAD_V)
            scores = g.scores(kc)
            dscores = _dot_nt(do_h, g.v_b[:, vc])
            dfwd = jnp.where(g.lower, dscores, 0.0).astype(BF16)
            dbwd = jnp.where(g.upper, dscores, 0.0).astype(BF16)
            dv_intra = _dot_tn(scores, do_h)
            da_intra = _dot_nn(dfwd, g.b_b[:, kc])
            db = _dot_tn(dfwd, g.a_b[:, kc])
            dcn = _dot_nn(dbwd, g.dp_b[:, kc])
            ddp = _dot_tn(dbwd, g.cn_b[:, kc])
            dstate = dstate_ref[srows, :]
            da_rows, dkd_rows, dv_rows, dcl_rows = [None] * cpt, [None] * cpt, [None] * cpt, [None] * cpt
            for j in reversed(range(cpt)):
                r = _chunk_rows(j)
                state = st_ref[j, srows, :]
                dstate_b = dstate.astype(BF16)
                do_c = do_h[r]
                dv_rows[j] = dv_intra[r] + _dot_nt(g.kd_b[r, kc], dstate_b)
                da_rows[j] = da_intra[r] + _dot_nn(do_c, state.astype(BF16))
                dkd = _dot_nn(g.v_b[r, vc], dstate_b) * g.e_rest[r, kc]
                dkd_rows[j] = dkd
                decay = jnp.exp(g.c_last[j][:, kc])
                dc_last = (jnp.sum(dkd * g.k[r, kc], axis=0, keepdims=True)
                           + decay * jnp.sum(state * dstate, axis=0, keepdims=True))
                dcl_rows[j] = jnp.where(last_row, dc_last, 0.0)
                dstate = _dot_tn(do_c, g.a_b[r, kc]) + dstate * decay
            dstate_ref[srows, :] = dstate
            da = jnp.concatenate(da_rows, axis=0)
            dkd = jnp.concatenate(dkd_rows, axis=0)
            dproj_ref[:, v_cols] = jnp.concatenate(dv_rows, axis=0).astype(BF16)
            q_up, q_down = da * g.e_pos[:, kc], dcn * g.e_neg[:, kc]
            k_up, k_down = ddp * g.e_pos[:, kc], db * g.e_neg[:, kc] + dkd
            dproj_ref[:, kc] = (Q_SCALE * (q_up + q_down)).astype(BF16)
            dproj_ref[:, k_cols] = (k_up + k_down).astype(BF16)
            dc_h.append(g.q[:, kc] * (q_up - q_down) + g.k[:, kc] * (k_up - k_down)
                        + jnp.concatenate(dcl_rows, axis=0))
        dz = _chunk_scan(jnp.concatenate(dc_h, axis=1), True) * (1.0 / GATE_NORMALIZER) * (1.0 - _sigmoid(g.z))
        dzb = dz.astype(BF16)
        dgkb_ref[...] += jnp.sum(dz, axis=0, keepdims=True)
        dgkw_ref[...] += _dot_tn(low_b, dzb)
        dproj_ref[:, GLA_QKVG_WIDTH:] = _dot_nt(dzb, gkw_ref[...]).astype(BF16)

        @pl.when(i == nt - 1)
        def _():
            dwout_stage[...] = dwout_acc[...].astype(BF16)
            pltpu.sync_copy(dwout_stage, dwout_hbm)

    rev = lambda i: (nt - 1 - i, 0)
    return pl.pallas_call(
        body, name="gla_bwd", grid=(nt,),
        in_specs=[pl.BlockSpec((tm, D_MODEL), rev), pl.BlockSpec((tm, GLA_IN_PAD), rev),
                  pl.BlockSpec((tm, GLA_VALUE_WIDTH), rev),
                  pl.BlockSpec((cpt, GLA_VALUE_WIDTH, GLA_HEAD_K), lambda i: (nt - 1 - i, 0, 0)),
                  _const((GLA_LOW_PAD, GLA_KEY_WIDTH)), _const((1, GLA_KEY_WIDTH)), _const((1, GLA_VALUE_WIDTH)),
                  _const((GLA_VALUE_WIDTH, D_MODEL))],
        out_specs=[pl.BlockSpec((tm, GLA_IN_PAD), rev), pl.BlockSpec(memory_space=pl.ANY),
                   _full((1, GLA_VALUE_WIDTH)), _full((GLA_LOW_PAD, GLA_KEY_WIDTH)), _full((1, GLA_KEY_WIDTH))],
        out_shape=[jax.ShapeDtypeStruct((seq, GLA_IN_PAD), BF16), jax.ShapeDtypeStruct((GLA_VALUE_WIDTH, D_MODEL), BF16),
                   jax.ShapeDtypeStruct((1, GLA_VALUE_WIDTH), F32), jax.ShapeDtypeStruct((GLA_LOW_PAD, GLA_KEY_WIDTH), F32),
                   jax.ShapeDtypeStruct((1, GLA_KEY_WIDTH), F32)],
        scratch_shapes=[pltpu.VMEM((GLA_VALUE_WIDTH, GLA_HEAD_K), F32), pltpu.VMEM((GLA_VALUE_WIDTH, D_MODEL), F32),
                        pltpu.VMEM((GLA_VALUE_WIDTH, D_MODEL), BF16)],
        compiler_params=_params(),
    )(dh2, proj, o, states, gkw, gkb, hw, w_out)


def _position():
    return lax.axis_index("x"), lax.axis_index("y"), lax.axis_index("c")


def _lead_slot(ref, d):
    return ref.at[d]


def _row_slot(rows):
    return lambda ref, d: ref.at[pl.ds(pl.multiple_of(d * rows, rows), rows)]


def _dim1_slot(size):
    return lambda ref, d: ref.at[:, pl.ds(pl.multiple_of(d * size, size), size)]


class _Gather:
    def __init__(self, in_refs, out_refs, slots, send_sems, recv_sems, local_sems):
        self.in_refs, self.out_refs, self.slots = in_refs, out_refs, slots
        self.send_sems, self.recv_sems, self.local_sems = send_sems, recv_sems, local_sems
        self.n = len(in_refs)
        x, y, c = _position()
        self.c = c
        self.me, self.sibling = (x, y, c), (x, y, 1 - c)
        self.near = [(1 - x, y), (x, 1 - y)]
        self.diagonal = (1 - x, 1 - y)
        self.relay_from = (x ^ c, y ^ (1 - c))
        self.relay_to = (x ^ (1 - c), y ^ c)

    def _copy(self, a, k, block, to, from_input=False):
        part = self.slots[a](self.out_refs[a], 4 * block[0] + 2 * block[1] + block[2])
        return pltpu.make_async_remote_copy(
            src_ref=self.in_refs[a] if from_input else part, dst_ref=part,
            send_sem=self.send_sems.at[a, k], recv_sem=self.recv_sems.at[a, k], device_id=to, device_id_type=MESH)

    def _mine(self):
        return [pltpu.make_async_copy(self.in_refs[a], self.slots[a](self.out_refs[a], 4 * self.me[0] + 2 * self.me[1]
                                                                    + self.me[2]), self.local_sems.at[a])
                for a in range(self.n)]

    def _first(self):
        first = [self._copy(a, 0, self.me, self.sibling, True) for a in range(self.n)]
        return first + [self._copy(a, 1 + j, self.me, (*chip, self.c), True)
                        for j, chip in enumerate(self.near) for a in range(self.n)]

    def _relayed(self):
        return [self._copy(a, 3, (*self.relay_from, self.c), (*self.relay_to, self.c)) for a in range(self.n)]

    def _passed(self, j):
        chip = self.near[j] if j < 2 else self.diagonal
        return [self._copy(a, 4 + j, (*chip, self.c), self.sibling) for a in range(self.n)]

    def start(self):
        for cp in self._mine() + self._first():
            cp.start()

    def forward(self):
        for j, chip in enumerate(self.near):
            for a in range(self.n):
                self._copy(a, 1 + j, (*chip, self.c), self.me).wait_recv()
        for cp in self._relayed() + self._passed(0) + self._passed(1):
            cp.start()

    def relay(self):
        pass

    def finish(self):
        for a in range(self.n):
            self._copy(a, 3, (*self.diagonal, self.c), self.me).wait_recv()
        for cp in self._passed(2):
            cp.start()
        for a in range(self.n):
            self._copy(a, 0, self.sibling, self.me).wait_recv()
        for j, chip in enumerate(self.near + [self.diagonal]):
            for a in range(self.n):
                self._copy(a, 4 + j, (*chip, 1 - self.c), self.me).wait_recv()
        for cp in self._first() + self._relayed() + self._passed(0) + self._passed(1) + self._passed(2):
            cp.wait_send()
        for cp in self._mine():
            cp.wait()


class _Exchange:
    def __init__(self, in_refs, out_refs, slots, send_sems, recv_sems, local_sems):
        self.in_refs, self.out_refs, self.slots = in_refs, out_refs, slots
        self.send_sems, self.recv_sems, self.local_sems = send_sems, recv_sems, local_sems
        self.n = len(in_refs)
        self.pos = _position()

    def _copies(self):
        x, y, c = self.pos
        me = 4 * x + 2 * y + c
        mine = [pltpu.make_async_copy(self.slots[a](self.in_refs[a], me), self.out_refs[a].at[me],
                                      self.local_sems.at[a]) for a in range(self.n)]
        remote = []
        for k in range(1, N_DEV):
            px, py, pc = x ^ (k >> 2), y ^ ((k >> 1) & 1), c ^ (k & 1)
            for a in range(self.n):
                remote.append(pltpu.make_async_remote_copy(
                    src_ref=self.slots[a](self.in_refs[a], 4 * px + 2 * py + pc), dst_ref=self.out_refs[a].at[me],
                    send_sem=self.send_sems.at[a, k - 1], recv_sem=self.recv_sems.at[a, k - 1],
                    device_id=(px, py, pc), device_id_type=MESH))
        return mine, remote

    def start(self):
        mine, remote = self._copies()
        for cp in mine + remote:
            cp.start()

    def forward(self):
        pass

    def relay(self):
        pass

    def finish(self):
        mine, remote = self._copies()
        for cp in remote:
            cp.wait_recv()
        for cp in remote:
            cp.wait_send()
        for cp in mine:
            cp.wait()


class _Rider:
    def __init__(self, kind, arrays, out_shapes, slots, scratch=None, forward_step=None):
        self.kind, self.arrays, self.slots = kind, list(arrays), slots
        self.n = len(self.arrays)
        hbm = pl.BlockSpec(memory_space=pl.ANY)
        self.in_specs = [hbm] * self.n
        self.out_specs = [hbm] * self.n
        self.out_shape = [jax.ShapeDtypeStruct(tuple(s), a.dtype) for s, a in zip(out_shapes, self.arrays)]
        self.scratch = scratch if scratch is not None else [
            pltpu.SemaphoreType.DMA((self.n, 7)), pltpu.SemaphoreType.DMA((self.n, 7)),
            pltpu.SemaphoreType.DMA((self.n,))]
        self.forward_step = forward_step
        self.relay_step = None

    def bind(self, in_refs, out_refs, scratch):
        return self.kind(in_refs, out_refs, self.slots, *scratch)


def _gather_rider(shards, full_shapes, slots, forward_step=None):
    return _Rider(_Gather, shards, full_shapes, slots, None, forward_step)


def _exchange_rider(sends, part_shapes, slots):
    return _Rider(_Exchange, sends, [(N_DEV,) + tuple(s) for s in part_shapes], slots)


def _split_refs(refs, n_in, n_out, n_scratch, rider):
    k = rider.n if rider is not None else 0
    ins, r_ins = refs[:n_in], refs[n_in:n_in + k]
    outs, r_outs = refs[n_in + k:n_in + k + n_out], refs[n_in + k + n_out:n_in + 2 * k + n_out]
    rest = refs[n_in + 2 * k + n_out:]
    scratch, r_scratch = rest[:n_scratch], rest[n_scratch:]
    comm = rider.bind(r_ins, r_outs, r_scratch) if rider is not None else None
    if comm is not None:
        comm.forward_step, comm.relay_step = rider.forward_step, rider.relay_step
    return ins + outs + scratch, comm


def _ride_before(comm, i, nt):
    if comm is not None:
        pl.when(i == 0)(comm.start)
        pl.when(i == (nt - 1 if comm.forward_step is None else min(comm.forward_step, nt - 1)))(comm.forward)
        pl.when(i == (nt - 1 if comm.relay_step is None else min(comm.relay_step, nt - 1)))(comm.relay)


def _ride_after(comm, i, nt):
    if comm is not None:
        pl.when(i == nt - 1)(comm.finish)


def _extend(specs, rider, field):
    return list(specs) + (getattr(rider, field) if rider is not None else [])


N_CHIPS = 4


class _TwoLevel:
    def __init__(self, in_refs, out_refs, slots, *scratch):
        self.in_refs, self.out_refs, self.slots = in_refs, out_refs, slots
        self.n = n = len(in_refs)
        self.own_bufs, self.recv_bufs, self.relay_bufs = scratch[:n], scratch[n:2 * n], scratch[2 * n:3 * n]
        self.swap_send, self.swap_recv, self.local_sems, self.chip_send, self.chip_recv = scratch[3 * n:]
        x, y, c = self.pos = _position()
        self.first = (x ^ (1 - c), y ^ c)
        self.second = (x ^ c, y ^ (1 - c))
        self.chip_index = lambda chip: 2 * chip[0] + chip[1]

    def _swap(self):
        x, y, c = self.pos
        return [pltpu.make_async_remote_copy(
            src_ref=self.slots[a](self.in_refs[a], 2 * q + 1 - c), dst_ref=self.recv_bufs[a].at[q],
            send_sem=self.swap_send.at[a, q], recv_sem=self.swap_recv.at[a, q],
            device_id=(x, y, 1 - c), device_id_type=MESH) for a in range(self.n) for q in range(N_CHIPS)]

    def _mine(self):
        c = self.pos[2]
        return [pltpu.make_async_copy(self.slots[a](self.in_refs[a], 2 * q + c), self.own_bufs[a].at[q],
                                      self.local_sems.at[a, q]) for a in range(self.n) for q in range(N_CHIPS)]

    def _to_chip(self, a, k, src, dst, chip):
        return pltpu.make_async_remote_copy(
            src_ref=src, dst_ref=dst, send_sem=self.chip_send.at[a, k], recv_sem=self.chip_recv.at[a, k],
            device_id=(*chip, self.pos[2]), device_id_type=MESH)

    def _first_wave(self):
        x, y, _ = self.pos
        diagonal = self.chip_index((1 - x, 1 - y))
        passed_on = [self._to_chip(a, 1, self.own_bufs[a].at[diagonal], self.relay_bufs[a], self.first)
                     for a in range(self.n)]
        return passed_on + [self._to_chip(a, 0, self.own_bufs[a].at[self.chip_index(self.first)],
                                          self.out_refs[a].at[1], self.first) for a in range(self.n)]

    def _second_wave(self):
        return [self._to_chip(a, 2, self.own_bufs[a].at[self.chip_index(self.second)], self.out_refs[a].at[2],
                              self.second) for a in range(self.n)]

    def _own(self):
        x, y, _ = self.pos
        return [pltpu.make_async_copy(self.own_bufs[a].at[2 * x + y], self.out_refs[a].at[0],
                                      self.local_sems.at[a, N_CHIPS]) for a in range(self.n)]

    def start(self):
        for cp in self._swap() + self._mine():
            cp.start()

    def forward(self):
        swap, mine = self._swap(), self._mine()
        for a in range(self.n):
            for q in range(N_CHIPS):
                mine[a * N_CHIPS + q].wait()
                swap[a * N_CHIPS + q].wait_recv()
                self.own_bufs[a][q] = (self.own_bufs[a][q].astype(F32)
                                       + self.recv_bufs[a][q].astype(F32)).astype(BF16)
        for cp in self._first_wave() + self._own():
            cp.start()

    def relay(self):
        second = self.chip_index(self.second)
        for a in range(self.n):
            self._to_chip(a, 1, self.relay_bufs[a], self.relay_bufs[a], self.first).wait_recv()
            self.own_bufs[a][second] = (self.own_bufs[a][second].astype(F32)
                                        + self.relay_bufs[a][...].astype(F32)).astype(BF16)
        for cp in self._second_wave():
            cp.start()

    def finish(self):
        for a in range(self.n):
            self._to_chip(a, 0, self.out_refs[a].at[1], self.out_refs[a].at[1], self.first).wait_recv()
            self._to_chip(a, 2, self.out_refs[a].at[2], self.out_refs[a].at[2], self.second).wait_recv()
        for cp in self._first_wave() + self._second_wave() + self._swap():
            cp.wait_send()
        for cp in self._own():
            cp.wait()


def _two_level_rider(sends, part_shapes, slots, forward_step=None, relay_step=None):
    n = len(sends)
    bufs = [pltpu.VMEM((N_CHIPS,) + tuple(s), a.dtype) for s, a in zip(part_shapes, sends)]
    relay_bufs = [pltpu.VMEM(tuple(s), a.dtype) for s, a in zip(part_shapes, sends)]
    scratch = bufs + bufs + relay_bufs + [
        pltpu.SemaphoreType.DMA((n, N_CHIPS)), pltpu.SemaphoreType.DMA((n, N_CHIPS)),
        pltpu.SemaphoreType.DMA((n, N_CHIPS + 1)), pltpu.SemaphoreType.DMA((n, 3)), pltpu.SemaphoreType.DMA((n, 3))]
    rider = _Rider(_TwoLevel, sends, [(3,) + tuple(s) for s in part_shapes], slots, scratch, forward_step)
    rider.relay_step = relay_step
    return rider


class _Joined:
    def __init__(self, first, second):
        self.first, self.second = first, second

    def start(self):
        self.first.start()
        self.second.start()

    def forward(self):
        self.first.forward()
        self.second.forward()

    def relay(self):
        self.first.relay()
        self.second.relay()

    def finish(self):
        self.first.finish()
        self.second.finish()


class _JoinedRider:
    def __init__(self, first, second):
        self.first, self.second = first, second
        self.n = first.n + second.n
        self.arrays = first.arrays + second.arrays
        self.in_specs = first.in_specs + second.in_specs
        self.out_specs = first.out_specs + second.out_specs
        self.out_shape = first.out_shape + second.out_shape
        self.scratch = first.scratch + second.scratch
        self.forward_step = first.forward_step
        self.relay_step = first.relay_step

    def bind(self, in_refs, out_refs, scratch):
        k, s = self.first.n, len(self.first.scratch)
        return _Joined(self.first.bind(in_refs[:k], out_refs[:k], scratch[:s]),
                       self.second.bind(in_refs[k:], out_refs[k:], scratch[s:]))


def _adamw(w, g, m, v):
    m = ADAM_B1 * m + (1.0 - ADAM_B1) * g
    v = ADAM_B2 * v + (1.0 - ADAM_B2) * (g * g)
    m_hat = m / (1.0 - ADAM_B1 ** ADAM_STEP)
    v_hat = v / (1.0 - ADAM_B2 ** ADAM_STEP)
    delta = -ADAM_LR * (m_hat / (jnp.sqrt(v_hat) + ADAM_EPS) + ADAM_WD * w)
    return delta, m, v


def _sum_parts(parts_ref, index=()):
    g = parts_ref[(0,) + index].astype(F32)
    for s in range(1, parts_ref.shape[0]):
        g = g + parts_ref[(s,) + index].astype(F32)
    return g


def _adamw_group_call(name, groups):
    k = len(groups)

    def body(*refs):
        ins, outs = refs[:4 * k], refs[4 * k:]
        for i in range(k):
            parts_ref, w_ref, m_ref, v_ref = ins[4 * i:4 * i + 4]
            g = _sum_parts(parts_ref)
            delta, m_new, v_new = _adamw(w_ref[...], g, m_ref[...], v_ref[...])
            for out_ref, value in zip(outs[4 * i:4 * i + 4], (g, delta, m_new, v_new)):
                out_ref[...] = value

    vmem = pl.BlockSpec(memory_space=pltpu.VMEM)
    res = pl.pallas_call(
        body, name=name, in_specs=[vmem] * (4 * k), out_specs=[vmem] * (4 * k),
        out_shape=[jax.ShapeDtypeStruct(grp[1].shape, F32) for grp in groups for _ in range(4)],
        compiler_params=pltpu.CompilerParams(vmem_limit_bytes=VMEM_LIMIT),
    )(*[a for grp in groups for a in grp])
    return [res[4 * i:4 * i + 4] for i in range(k)]


def _adamw_slabs_call(name, parts, w, m, v, rider=None):
    def main(parts_ref, w_ref, m_ref, v_ref, g_ref, delta_ref, m_out, v_out):
        g = _sum_parts(parts_ref)
        delta, m_new, v_new = _adamw(w_ref[...], g, m_ref[...], v_ref[...])
        g_ref[...] = g
        delta_ref[...] = delta
        m_out[...] = m_new
        v_out[...] = v_new

    def body(*refs):
        own, comm = _split_refs(refs, 4, 4, 0, rider)
        if comm is not None:
            comm.start()
        main(*own)
        if comm is not None:
            comm.forward()
            comm.relay()
            comm.finish()

    vmem = pl.BlockSpec(memory_space=pltpu.VMEM)
    return pl.pallas_call(
        body, name=name, in_specs=_extend([vmem] * 4, rider, "in_specs"),
        out_specs=_extend([vmem] * 4, rider, "out_specs"),
        out_shape=_extend([jax.ShapeDtypeStruct(w.shape, F32)] * 4, rider, "out_shape"),
        scratch_shapes=_extend([], rider, "scratch"),
        compiler_params=pltpu.CompilerParams(vmem_limit_bytes=VMEM_LIMIT),
    )(parts, w, m, v, *_extend([], rider, "arrays"))


WIDE_ROWS = 8
NARROW_ROWS = 40
NARROW_GKW_ROW = 8
NARROW_GKB_ROW = 24
NARROW_HW_ROW = 32
GROUP_SHARD = POOL_GROUP_DIM // N_DEV
KEY_SHARD = GLA_KEY_WIDTH // N_DEV
HEAD_V_SHARD = GLA_HEAD_V // N_DEV


def _small_adamw_call(wide, narrow, w, m, v):
    names = ("norm_w", "pool_scale", "final_norm_w", "pool_group_b", "gla_gk_w", "gla_gk_b", "gla_head_norm_w")
    where = {
        "norm_w": (0, slice(0, 2), slice(None)),
        "pool_scale": (0, slice(2, 3), slice(None)),
        "final_norm_w": (0, slice(3, 4), slice(None)),
        "pool_group_b": (1, slice(0, POOL_GROUPS), slice(0, GROUP_SHARD)),
        "gla_gk_w": (1, slice(NARROW_GKW_ROW, NARROW_GKW_ROW + GLA_GATE_RANK), slice(0, KEY_SHARD)),
        "gla_gk_b": (1, slice(NARROW_GKB_ROW, NARROW_GKB_ROW + 1), slice(0, KEY_SHARD)),
        "gla_head_norm_w": (1, slice(NARROW_HW_ROW, NARROW_HW_ROW + 1), slice(0, HEAD_V_SHARD)),
    }
    k = len(names)

    def body(*refs):
        parts = refs[0:2]
        w_refs, m_refs, v_refs = refs[2:2 + k], refs[2 + k:2 + 2 * k], refs[2 + 2 * k:2 + 3 * k]
        outs = refs[2 + 3 * k:]
        loss_ref = outs[0]
        loss_ref[...] = _sum_parts(parts[0], (slice(4, 5), slice(0, 1)))
        for i, name in enumerate(names):
            buf, rows, cols = where[name]
            g = _sum_parts(parts[buf], (rows, cols))
            delta, m_new, v_new = _adamw(w_refs[i][...], g, m_refs[i][...], v_refs[i][...])
            outs[1 + i][...] = g
            outs[1 + k + i][...] = delta
            outs[1 + 2 * k + i][...] = m_new
            outs[1 + 3 * k + i][...] = v_new

    vmem = pl.BlockSpec(memory_space=pltpu.VMEM)
    shapes = [jax.ShapeDtypeStruct(w[n].shape, F32) for n in names]
    res = pl.pallas_call(
        body, name="adamw_small", in_specs=[vmem] * (2 + 3 * k), out_specs=[vmem] * (1 + 4 * k),
        out_shape=[jax.ShapeDtypeStruct((1, 1), F32)] + shapes * 4,
    )(wide, narrow, *[w[n] for n in names], *[m[n] for n in names], *[v[n] for n in names])
    unzip = lambda j: dict(zip(names, res[1 + j * k:1 + (j + 1) * k]))
    return res[0], unzip(0), unzip(1), unzip(2), unzip(3)


def kernel(x, norm_w, pool_in_w, pool_group_w, pool_group_b, pool_scale, pool_out_w, gla_in_w, gla_gk_w, gla_gk_b, gla_head_norm_w, gla_out_w, final_norm_w, loss_target, m_norm_w, m_pool_in_w, m_pool_group_w, m_pool_group_b, m_pool_scale, m_pool_out_w, m_gla_in_w, m_gla_gk_w, m_gla_gk_b, m_gla_head_norm_w, m_gla_out_w, m_final_norm_w, v_norm_w, v_pool_in_w, v_pool_group_w, v_pool_group_b, v_pool_scale, v_pool_out_w, v_gla_in_w, v_gla_gk_w, v_gla_gk_b, v_gla_head_norm_w, v_gla_out_w, v_final_norm_w):
    w = dict(norm_w=norm_w, pool_in_w=pool_in_w, pool_group_w=pool_group_w, pool_group_b=pool_group_b,
             pool_scale=pool_scale, pool_out_w=pool_out_w, gla_in_w=gla_in_w, gla_gk_w=gla_gk_w, gla_gk_b=gla_gk_b,
             gla_head_norm_w=gla_head_norm_w, gla_out_w=gla_out_w, final_norm_w=final_norm_w)
    m = dict(norm_w=m_norm_w, pool_in_w=m_pool_in_w, pool_group_w=m_pool_group_w, pool_group_b=m_pool_group_b,
             pool_scale=m_pool_scale, pool_out_w=m_pool_out_w, gla_in_w=m_gla_in_w, gla_gk_w=m_gla_gk_w,
             gla_gk_b=m_gla_gk_b, gla_head_norm_w=m_gla_head_norm_w, gla_out_w=m_gla_out_w,
             final_norm_w=m_final_norm_w)
    v = dict(norm_w=v_norm_w, pool_in_w=v_pool_in_w, pool_group_w=v_pool_group_w, pool_group_b=v_pool_group_b,
             pool_scale=v_pool_scale, pool_out_w=v_pool_out_w, gla_in_w=v_gla_in_w, gla_gk_w=v_gla_gk_w,
             gla_gk_b=v_gla_gk_b, gla_head_norm_w=v_gla_head_norm_w, gla_out_w=v_gla_out_w,
             final_norm_w=v_final_norm_w)
    col_shard = GLA_IN_WIDTH // N_DEV
    row_shard = D_MODEL // N_DEV

    def lanes(a):
        return jnp.pad(a, [(0, 0)] * (a.ndim - 1) + [(0, LANES - a.shape[-1])])

    me = 4 * lax.axis_index("x") + 2 * lax.axis_index("y") + lax.axis_index("c")
    mine = (lax.broadcasted_iota(jnp.int32, (1, N_DEV, 1), 1) == me)

    def own_lanes(vec, repeats, shard):
        return jnp.where(mine, vec.reshape(repeats, 1, shard), 0.0).reshape(1, repeats * N_DEV * shard)

    small_in = jnp.concatenate([
        own_lanes(pool_group_b[0], POOL_GROUPS, GROUP_SHARD),
        jnp.pad(own_lanes(gla_gk_b, 1, KEY_SHARD), ((0, 0), (0, D_MODEL - GLA_KEY_WIDTH))),
        own_lanes(jnp.tile(gla_head_norm_w, (GLA_HEADS, 1)), GLA_HEADS, HEAD_V_SHARD),
        jnp.zeros((SMALL_ROWS - 3, D_MODEL), F32)], axis=0)
    in_cols = 2 * POOL_WIDTH // N_DEV
    nw0, nw1, wf = norm_w[0:1], norm_w[1:2], final_norm_w.reshape(1, D_MODEL)
    xs, target = x[0], loss_target[0]

    h1, p, pool_in, pool_gw, pool_out, small_all, gla_in_parts, gkw_parts, gla_out = _pool_fwd_call(
        xs, nw0, pool_scale,
        _gather_rider(
            [pool_in_w[0].astype(BF16), pool_group_w[0].astype(BF16), pool_out_w[0].astype(BF16), small_in],
            [(D_MODEL, 2 * POOL_WIDTH), (POOL_GROUPS, POOL_GROUP_DIM, POOL_GROUP_DIM), (POOL_WIDTH, D_MODEL),
             (N_DEV, SMALL_ROWS, D_MODEL)],
            [_dim1_slot(in_cols), _dim1_slot(GROUP_SHARD), _row_slot(row_shard), _lead_slot]),
        _gather_rider(
            [jnp.transpose(gla_in_w[0]).astype(BF16), gla_gk_w[0].astype(BF16), gla_out_w[0].astype(BF16)],
            [(N_DEV, col_shard, D_MODEL), (N_DEV, GLA_GATE_RANK, KEY_SHARD), (GLA_VALUE_WIDTH, D_MODEL)],
            [_lead_slot, _lead_slot, _row_slot(row_shard)]))
    small_sum = small_all.sum(axis=0)
    pool_gb = small_sum[SMALL_ROW_GROUP_B:SMALL_ROW_GROUP_B + 1]
    gla_gkb = small_sum[SMALL_ROW_GK_B:SMALL_ROW_GK_B + 1, :GLA_KEY_WIDTH]
    gla_hw = small_sum[SMALL_ROW_HEAD_NORM:SMALL_ROW_HEAD_NORM + 1]
    gla_in = gla_in_parts.reshape(GLA_IN_WIDTH, D_MODEL)
    gla_gkw = jnp.pad(jnp.transpose(gkw_parts, (1, 0, 2)).reshape(GLA_GATE_RANK, GLA_KEY_WIDTH),
                      ((0, GLA_LOW_PAD - GLA_GATE_RANK), (0, 0)))
    dh2, proj, o, states, loss_part, dwf = _gla_fwd_call(h1, nw1, gla_in, gla_gkw, gla_gkb, gla_hw, gla_out, wf, target)

    dproj, d_gla_out, dhw, dgkw, dgkb = _gla_bwd_call(dh2, proj, o, states, gla_gkw, gla_gkb, gla_hw, gla_out)
    dh1, d_gla_in, dnw1, landed_gla_out = _inproj_bwd_call(
        "gla_in_bwd", dproj, h1, nw1, gla_in, dh2,
        _exchange_rider([d_gla_out], [(row_shard, D_MODEL)], [_row_slot(row_shard)]), transposed=True)
    slabs = col_shard * D_MODEL // (BF16_ROWS * LANES)
    gla_in_send = d_gla_in.reshape(N_DEV, slabs, BF16_ROWS, LANES)
    dp, d_pool_out, dgw, dgb, dsc, landed_gla_in = _pool_bwd_call(
        dh1, p, pool_gw, pool_gb, pool_scale, pool_out,
        _two_level_rider([gla_in_send], [(slabs, BF16_ROWS, LANES)], [_lead_slot], TWO_LEVEL_ADD_STEP,
                         TWO_LEVEL_RELAY_STEP))
    grad_x, d_pool_in, dnw0 = _inproj_bwd_call("pool_in_bwd", dp, xs, nw0, pool_in, dh1)

    wide = jnp.concatenate([
        dnw0, dnw1, dsc, dwf, jnp.pad(loss_part[0:1, 0:1], ((0, 0), (0, D_MODEL - 1))),
        jnp.zeros((WIDE_ROWS - 5, D_MODEL), F32)], axis=0)

    def rows8(a):
        return jnp.pad(lanes(a), ((0, 0), (0, -a.shape[1] % 8), (0, 0)))

    narrow = jnp.concatenate([
        rows8(jnp.transpose(dgb.reshape(POOL_GROUPS, N_DEV, GROUP_SHARD), (1, 0, 2))),
        rows8(jnp.transpose(dgkw[:GLA_GATE_RANK].reshape(GLA_GATE_RANK, N_DEV, KEY_SHARD), (1, 0, 2))),
        rows8(dgkb.reshape(N_DEV, 1, KEY_SHARD)),
        rows8(dhw.reshape(GLA_HEADS, GLA_HEAD_V).sum(axis=0).reshape(N_DEV, 1, HEAD_V_SHARD)),
    ], axis=1)
    last_exchange = _JoinedRider(
        _two_level_rider([d_pool_in, d_pool_out, dgw],
                         [(D_MODEL, in_cols), (row_shard, D_MODEL), (POOL_GROUPS, GROUP_SHARD, POOL_GROUP_DIM)],
                         [_dim1_slot(in_cols), _row_slot(row_shard), _dim1_slot(GROUP_SHARD)]),
        _exchange_rider([wide, narrow], [(WIDE_ROWS, D_MODEL), (NARROW_ROWS, LANES)],
                        [lambda ref, d: ref, _lead_slot]))

    res = {}
    as_slabs = lambda t: jnp.transpose(t[0]).reshape(slabs, BF16_ROWS, LANES)
    *outs, landed_pool_in, landed_pool_out, landed_gw, landed_wide, landed_narrow = _adamw_slabs_call(
        "adamw_gla_in_w", landed_gla_in, as_slabs(gla_in_w), as_slabs(m_gla_in_w), as_slabs(v_gla_in_w),
        last_exchange)
    res["gla_in_w"] = [jnp.transpose(t.reshape(col_shard, D_MODEL))[None] for t in outs]
    rest = [("pool_in_w", landed_pool_in, (D_MODEL, in_cols)),
            ("pool_group_w", landed_gw, (POOL_GROUPS * GROUP_SHARD, POOL_GROUP_DIM)),
            ("pool_out_w", landed_pool_out, (row_shard, D_MODEL)), ("gla_out_w", landed_gla_out, (row_shard, D_MODEL))]
    updates = _adamw_group_call("adamw_matrices", [
        (parts.reshape((parts.shape[0],) + shape), w[name].reshape(shape), m[name].reshape(shape),
         v[name].reshape(shape)) for name, parts, shape in rest])
    for (name, _, _), outs in zip(rest, updates):
        res[name] = [t.reshape(w[name].shape) for t in outs]
    small_shapes ={"norm_w": (2, D_MODEL), "pool_scale": (1, D_MODEL), "final_norm_w": (1, D_MODEL),
                    "pool_group_b": (POOL_GROUPS, GROUP_SHARD), "gla_gk_w": (GLA_GATE_RANK, KEY_SHARD),
                    "gla_gk_b": (1, KEY_SHARD), "gla_head_norm_w": (1, HEAD_V_SHARD)}
    as_small = lambda t: {n: t[n].reshape(s) for n, s in small_shapes.items()}
    loss, *small_outs = _small_adamw_call(landed_wide, landed_narrow, as_small(w), as_small(m), as_small(v))
    for name in small_shapes:
        res[name] = [t[name].reshape(w[name].shape) for t in small_outs]
    order = ("norm_w", "pool_in_w", "pool_group_w", "pool_group_b", "pool_scale", "pool_out_w", "gla_in_w",
             "gla_gk_w", "gla_gk_b", "gla_head_norm_w", "gla_out_w", "final_norm_w")
    return (loss.reshape(()), grad_x[None], *[res[n][0] for n in order], *[res[n][1] for n in order],
            *[res[n][2] for n in order], *[res[n][3] for n in order])
```

```python
import jax
import jax.numpy as jnp
from jax import lax
from jax.experimental import pallas as pl
from jax.experimental.pallas import tpu as pltpu

F32 = jnp.float32
BF16 = jnp.bfloat16
MESH = pl.DeviceIdType.MESH

N_DEV = 8
D_MODEL = 1024
POOL_WIDTH = 1024
POOL_GROUPS = 4
POOL_GROUP_DIM = 256
POOL_HALO = 16
GLA_HEADS = 4
GLA_HEAD_K = 128
GLA_HEAD_V = 256
GLA_KEY_WIDTH = 512
GLA_VALUE_WIDTH = 1024
GLA_GATE_RANK = 16
GLA_IN_WIDTH = 3088
GLA_IN_PAD = 3200
GLA_SAVED_Z = GLA_IN_PAD
GLA_SAVED_C = GLA_SAVED_Z + 512
GLA_SAVED_WIDTH = GLA_SAVED_C + 512
GLA_LOW_PAD = 128
GLA_QKVG_WIDTH = 3072
CHUNK = 64
GATE_NORMALIZER = 16.0
RMS_EPS = 1e-6
Q_SCALE = GLA_HEAD_K ** -0.5

ADAM_LR = 0.001
ADAM_B1 = 0.9
ADAM_B2 = 0.999
ADAM_EPS = 1e-08
ADAM_WD = 0.01
ADAM_STEP = 10

LANES = 128
BF16_ROWS = 16
VMEM_LIMIT = 56 * 1024 * 1024
ROW_TILE = 256
GLA_FWD_ROW_TILE = 512
MATMUL_ROW_TILE = 512
GATHER_RELAY_STEP = 5
TWO_LEVEL_ADD_STEP = 1
TWO_LEVEL_RELAY_STEP = 4


def _dot_nn(a, b):
    return lax.dot_general(a, b, (((1,), (0,)), ((), ())), preferred_element_type=F32)


def _dot_nt(a, b):
    return lax.dot_general(a, b, (((1,), (1,)), ((), ())), preferred_element_type=F32)


def _dot_tn(a, b):
    return lax.dot_general(a, b, (((0,), (0,)), ((), ())), preferred_element_type=F32)


def _rms(x):
    rstd = lax.rsqrt(jnp.mean(x * x, axis=-1, keepdims=True) + RMS_EPS)
    return x * rstd, rstd


def _rms_bwd(dxhat, xhat, rstd):
    return rstd * (dxhat - xhat * jnp.mean(dxhat * xhat, axis=-1, keepdims=True))


def _sigmoid(x):
    return 1.0 / (1.0 + jnp.exp(-x))


def _params(sem=("arbitrary",)):
    return pltpu.CompilerParams(dimension_semantics=sem, vmem_limit_bytes=VMEM_LIMIT)


def _full(shape):
    return pl.BlockSpec(shape, lambda i: (0,) * len(shape))


def _const(shape):
    return pl.BlockSpec(shape, lambda i: (0,) * len(shape), pipeline_mode=pl.Buffered(1))


def _window_sums(ext, forward):
    n = ext.shape[0]
    outs = []
    for g in range(POOL_GROUPS):
        s = ext[:, g * POOL_GROUP_DIM:(g + 1) * POOL_GROUP_DIM]
        for k in range(g + 1):
            shift = (1 << k) if forward else n - (1 << k)
            s = s + pltpu.roll(s, shift, axis=0)
        outs.append(s[:n - POOL_HALO])
    return outs


def _inv_count(row0, tm):
    row = row0 + lax.broadcasted_iota(jnp.int32, (tm, 1), 0)
    return [1.0 / jnp.minimum(row + 1, 2 << g).astype(F32) for g in range(POOL_GROUPS)]


def _pool_mix(u, u_prev, row0, gw_ref, gb):
    tm = u.shape[0]
    sums = _window_sums(jnp.concatenate([u, u_prev], axis=0), True)
    inv = _inv_count(row0, tm)
    pooled, mixed = [], []
    for g in range(POOL_GROUPS):
        ug = u[:, g * POOL_GROUP_DIM:(g + 1) * POOL_GROUP_DIM]
        pg = (sums[g] * inv[g] - ug).astype(BF16)
        pooled.append(pg)
        mixed.append(_dot_nn(pg, gw_ref[g]))
    return pooled, jnp.concatenate(mixed, axis=1) + gb


def _pool_fwd_call(x, nw, w_in, gw, gb, sc, w_out, rider=None):
    seq = x.shape[0]
    tm = min(MATMUL_ROW_TILE, seq)
    nt = seq // tm

    def main(x_ref, nw_ref, win_ref, gw_ref, gb_ref, sc_ref, wout_ref, h_ref, p_ref, halo_ref):
        i = pl.program_id(0)

        @pl.when(i == 0)
        def _():
            halo_ref[...] = jnp.zeros_like(halo_ref)

        xt = x_ref[...]
        xhat, _ = _rms(xt)
        n = (xhat * nw_ref[...]).astype(BF16)
        p = _dot_nn(n, win_ref[...])
        p_ref[...] = p
        u = p[:, :POOL_WIDTH]
        gate = p[:, POOL_WIDTH:]
        _, mixed = _pool_mix(u, halo_ref[...], i * tm, gw_ref, gb_ref[...])
        halo_ref[...] = u[tm - POOL_HALO:, :]
        y = (mixed * sc_ref[...] * (gate * _sigmoid(gate))).astype(BF16)
        h_ref[...] = xt + _dot_nn(y, wout_ref[...])

    def body(*refs):
        own, comm = _split_refs(refs, 7, 2, 1, rider)
        _ride_before(comm, pl.program_id(0), nt)
        main(*own)
        _ride_after(comm, pl.program_id(0), nt)

    return pl.pallas_call(
        body, name="pool_fwd", grid=(nt,),
        in_specs=_extend([pl.BlockSpec((tm, D_MODEL), lambda i: (i, 0)), _const((1, D_MODEL)),
                          _const((D_MODEL, 2 * POOL_WIDTH)), _const((POOL_GROUPS, POOL_GROUP_DIM, POOL_GROUP_DIM)),
                          _const((1, POOL_WIDTH)), _const((1, POOL_WIDTH)), _const((POOL_WIDTH, D_MODEL))],
                         rider, "in_specs"),
        out_specs=_extend([pl.BlockSpec((tm, D_MODEL), lambda i: (i, 0)),
                           pl.BlockSpec((tm, 2 * POOL_WIDTH), lambda i: (i, 0))], rider, "out_specs"),
        out_shape=_extend([jax.ShapeDtypeStruct((seq, D_MODEL), F32),
                           jax.ShapeDtypeStruct((seq, 2 * POOL_WIDTH), F32)], rider, "out_shape"),
        scratch_shapes=_extend([pltpu.VMEM((POOL_HALO, POOL_WIDTH), F32)], rider, "scratch"),
        compiler_params=_params(),
    )(x, nw, w_in, gw, gb, sc, w_out, *_extend([], rider, "arrays"))


def _pool_bwd_call(dh, p, gw, gb, sc, w_out, rider=None):
    seq = dh.shape[0]
    tm = min(MATMUL_ROW_TILE, seq)
    nt = seq // tm
    halo_blocks = tm // POOL_HALO

    def main(dh_ref, p_ref, pprev_ref, gw_ref, gb_ref, sc_ref, wout_ref,
             dp_ref, dwout_hbm, dgw_hbm, dgb_ref, dsc_ref, carry_ref, dwout_acc, dgw_acc, dwout_stage, dgw_stage):
        i = pl.program_id(0)
        t = nt - 1 - i

        @pl.when(i == 0)
        def _():
            carry_ref[...] = jnp.zeros_like(carry_ref)
            dwout_acc[...] = jnp.zeros_like(dwout_acc)
            dgw_acc[...] = jnp.zeros_like(dgw_acc)
            dgb_ref[...] = jnp.zeros_like(dgb_ref)
            dsc_ref[...] = jnp.zeros_like(dsc_ref)

        p = p_ref[...]
        u = p[:, :POOL_WIDTH]
        gate = p[:, POOL_WIDTH:]
        u_prev = jnp.where(t > 0, pprev_ref[:, :POOL_WIDTH], 0.0)
        pooled, mixed = _pool_mix(u, u_prev, t * tm, gw_ref, gb_ref[...])
        sg = _sigmoid(gate)
        silu = gate * sg
        sc = sc_ref[...]
        dhb = dh_ref[...].astype(BF16)
        y = (mixed * sc * silu).astype(BF16)
        dwout_acc[...] += _dot_tn(y, dhb)
        dy = _dot_nt(dhb, wout_ref[...])
        dmixed = dy * sc * silu
        dsc_ref[...] += jnp.sum(dy * mixed * silu, axis=0, keepdims=True)
        dgate = dy * mixed * sc * (sg * (1.0 + gate * (1.0 - sg)))
        dgb_ref[...] += jnp.sum(dmixed, axis=0, keepdims=True)
        inv = _inv_count(t * tm, tm)
        dpooled, scaled = [], []
        for g in range(POOL_GROUPS):
            dmg = dmixed[:, g * POOL_GROUP_DIM:(g + 1) * POOL_GROUP_DIM].astype(BF16)
            dgw_acc[g] += _dot_tn(pooled[g], dmg)
            dpg = _dot_nt(dmg, gw_ref[g])
            dpooled.append(dpg)
            scaled.append(dpg * inv[g])
        r = jnp.concatenate(scaled, axis=1)
        sums = _window_sums(jnp.concatenate([r, carry_ref[...]], axis=0), False)
        carry_ref[...] = r[:POOL_HALO, :]
        du = jnp.concatenate([sums[g] - dpooled[g] for g in range(POOL_GROUPS)], axis=1)
        dp_ref[...] = jnp.concatenate([du, dgate], axis=1).astype(BF16)

        @pl.when(i == nt - 1)
        def _():
            dwout_stage[...] = dwout_acc[...].astype(BF16)
            dgw_stage[...] = dgw_acc[...].astype(BF16)
            pltpu.sync_copy(dwout_stage, dwout_hbm)
            pltpu.sync_copy(dgw_stage, dgw_hbm)

    def body(*refs):
        own, comm = _split_refs(refs, 7, 5, 5, rider)
        _ride_before(comm, pl.program_id(0), nt)
        main(*own)
        _ride_after(comm, pl.program_id(0), nt)

    rev = lambda i: (nt - 1 - i, 0)
    return pl.pallas_call(
        body, name="pool_bwd", grid=(nt,),
        in_specs=_extend([pl.BlockSpec((tm, D_MODEL), rev), pl.BlockSpec((tm, 2 * POOL_WIDTH), rev),
                          pl.BlockSpec((POOL_HALO, 2 * POOL_WIDTH),
                                       lambda i: (jnp.maximum((nt - 1 - i) * halo_blocks - 1, 0), 0)),
                          _const((POOL_GROUPS, POOL_GROUP_DIM, POOL_GROUP_DIM)), _const((1, POOL_WIDTH)),
                          _const((1, POOL_WIDTH)), _const((POOL_WIDTH, D_MODEL))], rider, "in_specs"),
        out_specs=_extend([pl.BlockSpec((tm, 2 * POOL_WIDTH), rev), pl.BlockSpec(memory_space=pl.ANY),
                           pl.BlockSpec(memory_space=pl.ANY), _full((1, POOL_WIDTH)), _full((1, POOL_WIDTH))],
                          rider, "out_specs"),
        out_shape=_extend([jax.ShapeDtypeStruct((seq, 2 * POOL_WIDTH), BF16),
                           jax.ShapeDtypeStruct((POOL_WIDTH, D_MODEL), BF16),
                           jax.ShapeDtypeStruct((POOL_GROUPS, POOL_GROUP_DIM, POOL_GROUP_DIM), BF16),
                           jax.ShapeDtypeStruct((1, POOL_WIDTH), F32), jax.ShapeDtypeStruct((1, POOL_WIDTH), F32)],
                          rider, "out_shape"),
        scratch_shapes=_extend([pltpu.VMEM((POOL_HALO, POOL_WIDTH), F32), pltpu.VMEM((POOL_WIDTH, D_MODEL), F32),
                                pltpu.VMEM((POOL_GROUPS, POOL_GROUP_DIM, POOL_GROUP_DIM), F32),
                                pltpu.VMEM((POOL_WIDTH, D_MODEL), BF16),
                                pltpu.VMEM((POOL_GROUPS, POOL_GROUP_DIM, POOL_GROUP_DIM), BF16)], rider, "scratch"),
        compiler_params=_params(),
    )(dh, p, p, gw, gb, sc, w_out, *_extend([], rider, "arrays"))


def _rows_then_zeros(ref, lo, hi, rows):
    part = ref[lo:hi, :]
    return jnp.concatenate([part, jnp.zeros((rows - (hi - lo), part.shape[1]), part.dtype)], axis=0)


def _inproj_bwd_call(name, dproj, h_in, nw, w_in, dres, rider=None, transposed=False):
    seq = h_in.shape[0]
    width = dproj.shape[1]
    w_shape = tuple(w_in.shape)
    acc_shape = (width, D_MODEL) if transposed else w_shape
    whole = w_shape[0] // LANES * LANES
    tm = min(MATMUL_ROW_TILE, seq)
    nt = seq // tm

    def main(dproj_ref, h_ref, nw_ref, win_ref, dres_ref, dh_ref, dw_hbm, dnw_ref, dw_acc, dw_stage):
        i = pl.program_id(0)

        @pl.when(i == 0)
        def _():
            dw_acc[...] = jnp.zeros_like(dw_acc)
            dnw_ref[...] = jnp.zeros_like(dnw_ref)

        dpb = dproj_ref[...]
        if transposed:
            dn = _dot_nn(dpb[:, :whole], win_ref[0:whole, :])
            if whole < w_shape[0]:
                dn = dn + _dot_nn(dpb[:, whole:], _rows_then_zeros(win_ref, whole, w_shape[0], width - whole))
        else:
            dn = _dot_nt(dpb, win_ref[...])
        xhat, rstd = _rms(h_ref[...])
        nw_row = nw_ref[...]
        n = (xhat * nw_row).astype(BF16)
        dw_acc[...] += _dot_tn(dpb, n) if transposed else _dot_tn(n, dpb)
        dnw_ref[...] += jnp.sum(dn * xhat, axis=0, keepdims=True)
        dh_ref[...] = _rms_bwd(dn * nw_row, xhat, rstd) + dres_ref[...]

        @pl.when(i == nt - 1)
        def _():
            dw_stage[...] = dw_acc[...].astype(BF16)
            pltpu.sync_copy(dw_stage.at[pl.ds(0, w_shape[0])], dw_hbm)

    def body(*refs):
        own, comm = _split_refs(refs, 5, 3, 2, rider)
        _ride_before(comm, pl.program_id(0), nt)
        main(*own)
        _ride_after(comm, pl.program_id(0), nt)

    row = lambda i: (i, 0)
    return pl.pallas_call(
        body, name=name, grid=(nt,),
        in_specs=_extend([pl.BlockSpec((tm, width), row), pl.BlockSpec((tm, D_MODEL), row), _const((1, D_MODEL)),
                          _const(w_shape), pl.BlockSpec((tm, D_MODEL), row)], rider, "in_specs"),
        out_specs=_extend([pl.BlockSpec((tm, D_MODEL), row), pl.BlockSpec(memory_space=pl.ANY),
                           _full((1, D_MODEL))], rider, "out_specs"),
        out_shape=_extend([jax.ShapeDtypeStruct((seq, D_MODEL), F32), jax.ShapeDtypeStruct(w_shape, BF16),
                           jax.ShapeDtypeStruct((1, D_MODEL), F32)], rider, "out_shape"),
        scratch_shapes=_extend([pltpu.VMEM(acc_shape, F32), pltpu.VMEM(acc_shape, BF16)], rider, "scratch"),
        compiler_params=_params(),
    )(dproj, h_in, nw, w_in, dres, *_extend([], rider, "arrays"))


def _chunk_scan(x, reverse):
    n = x.shape[0]
    pos = lax.broadcasted_iota(jnp.int32, (n, 1), 0) & (CHUNK - 1)
    k = 1
    while k < CHUNK:
        if reverse:
            x = x + jnp.where(pos < CHUNK - k, pltpu.roll(x, n - k, axis=0), 0.0)
        else:
            x = x + jnp.where(pos >= k, pltpu.roll(x, k, axis=0), 0.0)
        k *= 2
    return x


def _chunk_rows(j):
    return slice(j * CHUNK, (j + 1) * CHUNK)


def _kcols(h):
    return slice(h * GLA_HEAD_K, (h + 1) * GLA_HEAD_K)


def _vcols(h):
    return slice(h * GLA_HEAD_V, (h + 1) * GLA_HEAD_V)


def _chunk_masks(tm):
    idx_t = lax.broadcasted_iota(jnp.int32, (tm, tm), 0)
    idx_s = lax.broadcasted_iota(jnp.int32, (tm, tm), 1)
    same_chunk = (idx_t ^ idx_s) < CHUNK
    return same_chunk & (idx_t >= idx_s), same_chunk & (idx_t < idx_s)


class _GlaTerms:
    def __init__(self, kc, q, k, v, low_b, gkw_ref, gkb_ref, masks, saved=None):
        tm = q.shape[0]
        self.q = q * Q_SCALE
        self.k = k
        if saved is None:
            self.z = _dot_nn(low_b, gkw_ref[:, kc]) + gkb_ref[:, kc]
            log_g = (jnp.minimum(self.z, 0.0) - jnp.log(1.0 + jnp.exp(-jnp.abs(self.z)))) / GATE_NORMALIZER
            self.c = _chunk_scan(log_g, False)
        else:
            self.z, self.c = saved
        is_last = lax.broadcasted_iota(jnp.int32, (CHUNK, 1), 0) == CHUNK - 1
        self.c_last = [jnp.sum(jnp.where(is_last, self.c[_chunk_rows(j), :], 0.0), axis=0, keepdims=True)
                       for j in range(tm // CHUNK)]
        c_last_rows = jnp.concatenate([jnp.broadcast_to(r, (CHUNK, r.shape[1])) for r in self.c_last], axis=0)
        self.e_pos = jnp.exp(self.c)
        self.e_neg = jnp.exp(-self.c)
        self.e_rest = jnp.exp(c_last_rows - self.c)
        self.a_b = (self.q * self.e_pos).astype(BF16)
        self.b_b = (self.k * self.e_neg).astype(BF16)
        self.cn_b = (self.q * self.e_neg).astype(BF16)
        self.dp_b = (self.k * self.e_pos).astype(BF16)
        self.kd_b = (self.k * self.e_rest).astype(BF16)
        self.v_b = v.astype(BF16)
        self.lower, self.upper = masks

    def scores(self, kc=slice(None)):
        fwd = _dot_nt(self.a_b[:, kc], self.b_b[:, kc])
        bwd = _dot_nt(self.cn_b[:, kc], self.dp_b[:, kc])
        return jnp.where(self.lower, fwd, jnp.where(self.upper, bwd, 0.0)).astype(BF16)


def _gla_fwd_call(h1, nw, w_in, gkw, gkb, hw, w_out, wf, target):
    seq = h1.shape[0]
    tm = min(GLA_FWD_ROW_TILE, seq)
    nt = seq // tm
    cpt = tm // CHUNK
    n_chunks = seq // CHUNK

    def body(h_ref, nw_ref, win_ref, gkw_ref, gkb_ref, hw_ref, wout_ref, wf_ref, tgt_ref,
             dh2_ref, proj_ref, o_ref, st_ref, loss_ref, dwf_ref, state_ref):
        i = pl.program_id(0)

        @pl.when(i == 0)
        def _():
            state_ref[...] = jnp.zeros_like(state_ref)
            loss_ref[...] = jnp.zeros_like(loss_ref)
            dwf_ref[...] = jnp.zeros_like(dwf_ref)

        ht = h_ref[...]
        xhat, _ = _rms(ht)
        n = (xhat * nw_ref[...]).astype(BF16)
        sections = {}
        for name, lo, hi in (("low", GLA_QKVG_WIDTH, GLA_IN_PAD), ("qk", 0, 2 * GLA_KEY_WIDTH),
                             ("v", 2 * GLA_KEY_WIDTH, GLA_QKVG_WIDTH - GLA_VALUE_WIDTH),
                             ("gate", GLA_QKVG_WIDTH - GLA_VALUE_WIDTH, GLA_QKVG_WIDTH)):
            rows = (win_ref[lo:hi, :] if hi <= GLA_IN_WIDTH
                    else _rows_then_zeros(win_ref, lo, GLA_IN_WIDTH, hi - lo))
            sections[name] = _dot_nt(n, rows)
            proj_ref[:, lo:hi] = sections[name]
        low_b = sections["low"].astype(BF16)
        masks = _chunk_masks(tm)
        on_heads = []
        for h in range(GLA_HEADS):
            kc, vc = _kcols(h), _vcols(h)
            g = _GlaTerms(kc, sections["qk"][:, kc], sections["qk"][:, GLA_KEY_WIDTH:][:, kc], sections["v"][:, vc],
                          low_b, gkw_ref, gkb_ref, masks)
            srows = slice(h * GLA_HEAD_V, (h + 1) * GLA_HEAD_V)
            o_intra = _dot_nn(g.scores(), g.v_b)
            state = state_ref[srows, :]
            o_rows = []
            for j in range(cpt):
                r = _chunk_rows(j)
                st_ref[j, srows, :] = state
                o_rows.append(o_intra[r] + _dot_nt(g.a_b[r], state.astype(BF16)))
                decay = jnp.exp(g.c_last[j])
                state = state * decay + _dot_tn(g.v_b[r], g.kd_b[r])
            state_ref[srows, :] = state
            o_head = jnp.concatenate(o_rows, axis=0)
            o_ref[:, vc] = o_head
            proj_ref[:, GLA_SAVED_Z + kc.start:GLA_SAVED_Z + kc.stop] = g.z
            proj_ref[:, GLA_SAVED_C + kc.start:GLA_SAVED_C + kc.stop] = g.c
            on_heads.append(_rms(o_head)[0])
        gate = sections["gate"]
        on = jnp.concatenate(on_heads, axis=1) * hw_ref[...]
        y = (on * (gate * _sigmoid(gate))).astype(BF16)
        h2 = ht + _dot_nn(y, wout_ref[...])
        xhat2, rstd2 = _rms(h2)
        wf_row = wf_ref[...]
        err = xhat2 * wf_row - tgt_ref[...]
        loss_ref[...] += 0.5 * jnp.sum(err * err) / D_MODEL
        dout = err * (1.0 / D_MODEL)
        dwf_ref[...] += jnp.sum(dout * xhat2, axis=0, keepdims=True)
        dh2_ref[...] = _rms_bwd(dout * wf_row, xhat2, rstd2)

    row = lambda i: (i, 0)
    return pl.pallas_call(
        body, name="gla_fwd", grid=(nt,),
        in_specs=[pl.BlockSpec((tm, D_MODEL), row), _const((1, D_MODEL)), _const((GLA_IN_WIDTH, D_MODEL)),
                  _const((GLA_LOW_PAD, GLA_KEY_WIDTH)), _const((1, GLA_KEY_WIDTH)), _const((1, GLA_VALUE_WIDTH)),
                  _const((GLA_VALUE_WIDTH, D_MODEL)), _const((1, D_MODEL)), pl.BlockSpec((tm, D_MODEL), row)],
        out_specs=[pl.BlockSpec((tm, D_MODEL), row), pl.BlockSpec((tm, GLA_SAVED_WIDTH), row),
                   pl.BlockSpec((tm, GLA_VALUE_WIDTH), row),
                   pl.BlockSpec((cpt, GLA_VALUE_WIDTH, GLA_HEAD_K), lambda i: (i, 0, 0)),
                   _full((8, LANES)), _full((1, D_MODEL))],
        out_shape=[jax.ShapeDtypeStruct((seq, D_MODEL), F32), jax.ShapeDtypeStruct((seq, GLA_SAVED_WIDTH), F32),
                   jax.ShapeDtypeStruct((seq, GLA_VALUE_WIDTH), F32),
                   jax.ShapeDtypeStruct((n_chunks, GLA_VALUE_WIDTH, GLA_HEAD_K), F32),
                   jax.ShapeDtypeStruct((8, LANES), F32), jax.ShapeDtypeStruct((1, D_MODEL), F32)],
        scratch_shapes=[pltpu.VMEM((GLA_VALUE_WIDTH, GLA_HEAD_K), F32)],
        compiler_params=_params(),
    )(h1, nw, w_in, gkw, gkb, hw, w_out, wf, target)


def _gla_bwd_call(dh2, proj, o, states, gkw, gkb, hw, w_out):
    seq = dh2.shape[0]
    tm = ROW_TILE
    nt = seq // tm
    cpt = tm // CHUNK

    def body(dh_ref, proj_ref, o_ref, st_ref, gkw_ref, gkb_ref, hw_ref, wout_ref,
             dproj_ref, dwout_hbm, dhw_ref, dgkw_ref, dgkb_ref, dstate_ref, dwout_acc, dwout_stage):
        i = pl.program_id(0)

        @pl.when(i == 0)
        def _():
            dstate_ref[...] = jnp.zeros_like(dstate_ref)
            dwout_acc[...] = jnp.zeros_like(dwout_acc)
            dhw_ref[...] = jnp.zeros_like(dhw_ref)
            dgkw_ref[...] = jnp.zeros_like(dgkw_ref)
            dgkb_ref[...] = jnp.zeros_like(dgkb_ref)

        dhb = dh_ref[...].astype(BF16)
        dy = _dot_nt(dhb, wout_ref[...])
        v0, g0 = 2 * GLA_KEY_WIDTH, GLA_QKVG_WIDTH - GLA_VALUE_WIDTH
        gate = proj_ref[:, g0:GLA_QKVG_WIDTH]
        low_b = proj_ref[:, GLA_QKVG_WIDTH:GLA_IN_PAD].astype(BF16)
        o = o_ref[...]
        hw_row = hw_ref[...]
        sg = _sigmoid(gate)
        silu = gate * sg
        don = dy * silu
        on_parts, do_parts, dhw_parts = [], [], []
        for h in range(GLA_HEADS):
            vc = _vcols(h)
            xh, rs = _rms(o[:, vc])
            on_parts.append(xh * hw_row[:, vc])
            dhw_parts.append(jnp.sum(don[:, vc] * xh, axis=0, keepdims=True))
            do_parts.append(_rms_bwd(don[:, vc] * hw_row[:, vc], xh, rs).astype(BF16))
        on = jnp.concatenate(on_parts, axis=1)
        dwout_acc[...] += _dot_tn((on * silu).astype(BF16), dhb)
        dhw_ref[...] += jnp.concatenate(dhw_parts, axis=1)
        dproj_ref[:, g0:GLA_QKVG_WIDTH] = (dy * on * (sg * (1.0 + gate * (1.0 - sg)))).astype(BF16)

        last_row = lax.broadcasted_iota(jnp.int32, (CHUNK, 1), 0) == CHUNK - 1
        g = _GlaTerms(slice(0, GLA_KEY_WIDTH), proj_ref[:, :GLA_KEY_WIDTH], proj_ref[:, GLA_KEY_WIDTH:v0],
                      proj_ref[:, v0:g0], low_b, gkw_ref, gkb_ref, _chunk_masks(tm),
                      saved=(proj_ref[:, GLA_SAVED_Z:GLA_SAVED_C], proj_ref[:, GLA_SAVED_C:GLA_SAVED_WIDTH]))
        dc_h = []
        for h in range(GLA_HEADS):
            kc, vc = _kcols(h), _vcols(h)
            k_cols = slice(GLA_KEY_WIDTH + kc.start, GLA_KEY_WIDTH + kc.stop)
            v_cols = slice(v0 + vc.start, v0 + vc.stop)
            do_h = do_parts[h]
            srows = slice(h * GLA_HEAD_V, (h + 1) * GLA_HEAD_V)
            scores = g.scores(kc)
            dscores = _dot_nt(do_h, g.v_b[:, vc])
            dfwd = jnp.where(g.lower, dscores, 0.0).astype(BF16)
            dbwd = jnp.where(g.upper, dscores, 0.0).astype(BF16)
            dv_intra = _dot_tn(scores, do_h)
            da_intra = _dot_nn(dfwd, g.b_b[:, kc])
            db = _dot_tn(dfwd, g.a_b[:, kc])
            dcn = _dot_nn(dbwd, g.dp_b[:, kc])
            ddp = _dot_tn(dbwd, g.cn_b[:, kc])
            dstate = dstate_ref[srows, :]
            da_rows, dkd_rows, dv_rows, dcl_rows = [None] * cpt, [None] * cpt, [None] * cpt, [None] * cpt
            for j in reversed(range(cpt)):
                r = _chunk_rows(j)
                state = st_ref[j, srows, :]
                dstate_b = dstate.astype(BF16)
                do_c = do_h[r]
                dv_rows[j] = dv_intra[r] + _dot_nt(g.kd_b[r, kc], dstate_b)
                da_rows[j] = da_intra[r] + _dot_nn(do_c, state.astype(BF16))
                dkd = _dot_nn(g.v_b[r, vc], dstate_b) * g.e_rest[r, kc]
                dkd_rows[j] = dkd
                decay = jnp.exp(g.c_last[j][:, kc])
                dc_last = (jnp.sum(dkd * g.k[r, kc], axis=0, keepdims=True)
                           + decay * jnp.sum(state * dstate, axis=0, keepdims=True))
                dcl_rows[j] = jnp.where(last_row, dc_last, 0.0)
                dstate = _dot_tn(do_c, g.a_b[r, kc]) + dstate * decay
            dstate_ref[srows, :] = dstate
            da = jnp.concatenate(da_rows, axis=0)
            dkd = jnp.concatenate(dkd_rows, axis=0)
            dproj_ref[:, v_cols] = jnp.concatenate(dv_rows, axis=0).astype(BF16)
            q_up, q_down = da * g.e_pos[:, kc], dcn * g.e_neg[:, kc]
            k_up, k_down = ddp * g.e_pos[:, kc], db * g.e_neg[:, kc] + dkd
            dproj_ref[:, kc] = (Q_SCALE * (q_up + q_down)).astype(BF16)
            dproj_ref[:, k_cols] = (k_up + k_down).astype(BF16)
            dc_h.append(g.q[:, kc] * (q_up - q_down) + g.k[:, kc] * (k_up - k_down)
                        + jnp.concatenate(dcl_rows, axis=0))
        dz = _chunk_scan(jnp.concatenate(dc_h, axis=1), True) * (1.0 / GATE_NORMALIZER) * (1.0 - _sigmoid(g.z))
        dzb = dz.astype(BF16)
        dgkb_ref[...] += jnp.sum(dz, axis=0, keepdims=True)
        dgkw_ref[...] += _dot_tn(low_b, dzb)
        dproj_ref[:, GLA_QKVG_WIDTH:] = _dot_nt(dzb, gkw_ref[...]).astype(BF16)

        @pl.when(i == nt - 1)
        def _():
            dwout_stage[...] = dwout_acc[...].astype(BF16)
            pltpu.sync_copy(dwout_stage, dwout_hbm)

    rev = lambda i: (nt - 1 - i, 0)
    return pl.pallas_call(
        body, name="gla_bwd", grid=(nt,),
        in_specs=[pl.BlockSpec((tm, D_MODEL), rev), pl.BlockSpec((tm, GLA_SAVED_WIDTH), rev),
                  pl.BlockSpec((tm, GLA_VALUE_WIDTH), rev),
                  pl.BlockSpec((cpt, GLA_VALUE_WIDTH, GLA_HEAD_K), lambda i: (nt - 1 - i, 0, 0)),
                  _const((GLA_LOW_PAD, GLA_KEY_WIDTH)), _const((1, GLA_KEY_WIDTH)), _const((1, GLA_VALUE_WIDTH)),
                  _const((GLA_VALUE_WIDTH, D_MODEL))],
        out_specs=[pl.BlockSpec((tm, GLA_IN_PAD), rev), pl.BlockSpec(memory_space=pl.ANY),
                   _full((1, GLA_VALUE_WIDTH)), _full((GLA_LOW_PAD, GLA_KEY_WIDTH)), _full((1, GLA_KEY_WIDTH))],
        out_shape=[jax.ShapeDtypeStruct((seq, GLA_IN_PAD), BF16), jax.ShapeDtypeStruct((GLA_VALUE_WIDTH, D_MODEL), BF16),
                   jax.ShapeDtypeStruct((1, GLA_VALUE_WIDTH), F32), jax.ShapeDtypeStruct((GLA_LOW_PAD, GLA_KEY_WIDTH), F32),
                   jax.ShapeDtypeStruct((1, GLA_KEY_WIDTH), F32)],
        scratch_shapes=[pltpu.VMEM((GLA_VALUE_WIDTH, GLA_HEAD_K), F32), pltpu.VMEM((GLA_VALUE_WIDTH, D_MODEL), F32),
                        pltpu.VMEM((GLA_VALUE_WIDTH, D_MODEL), BF16)],
        compiler_params=_params(),
    )(dh2, proj, o, states, gkw, gkb, hw, w_out)


def _position():
    return lax.axis_index("x"), lax.axis_index("y"), lax.axis_index("c")


def _lead_slot(ref, d):
    return ref.at[d]


def _row_slot(rows):
    return lambda ref, d: ref.at[pl.ds(pl.multiple_of(d * rows, rows), rows)]


def _dim1_slot(size):
    return lambda ref, d: ref.at[:, pl.ds(pl.multiple_of(d * size, size), size)]


class _Gather:
    def __init__(self, in_refs, out_refs, slots, send_sems, recv_sems, local_sems):
        self.in_refs, self.out_refs, self.slots = in_refs, out_refs, slots
        self.send_sems, self.recv_sems, self.local_sems = send_sems, recv_sems, local_sems
        self.n = len(in_refs)
        x, y, c = _position()
        self.c = c
        self.me, self.sibling = (x, y, c), (x, y, 1 - c)
        self.near = [(1 - x, y), (x, 1 - y)]
        self.diagonal = (1 - x, 1 - y)
        self.relay_from = (x ^ c, y ^ (1 - c))
        self.relay_to = (x ^ (1 - c), y ^ c)

    def _copy(self, a, k, block, to, from_input=False):
        part = self.slots[a](self.out_refs[a], 4 * block[0] + 2 * block[1] + block[2])
        return pltpu.make_async_remote_copy(
            src_ref=self.in_refs[a] if from_input else part, dst_ref=part,
            send_sem=self.send_sems.at[a, k], recv_sem=self.recv_sems.at[a, k], device_id=to, device_id_type=MESH)

    def _mine(self):
        return [pltpu.make_async_copy(self.in_refs[a], self.slots[a](self.out_refs[a], 4 * self.me[0] + 2 * self.me[1]
                                                                    + self.me[2]), self.local_sems.at[a])
                for a in range(self.n)]

    def _first(self):
        first = [self._copy(a, 0, self.me, self.sibling, True) for a in range(self.n)]
        return first + [self._copy(a, 1 + j, self.me, (*chip, self.c), True)
                        for j, chip in enumerate(self.near) for a in range(self.n)]

    def _relayed(self):
        return [self._copy(a, 3, (*self.relay_from, self.c), (*self.relay_to, self.c)) for a in range(self.n)]

    def _passed(self, j):
        chip = self.near[j] if j < 2 else self.diagonal
        return [self._copy(a, 4 + j, (*chip, self.c), self.sibling) for a in range(self.n)]

    def start(self):
        for cp in self._mine() + self._first():
            cp.start()

    def forward(self):
        for j, chip in enumerate(self.near):
            for a in range(self.n):
                self._copy(a, 1 + j, (*chip, self.c), self.me).wait_recv()
        for cp in self._relayed() + self._passed(0) + self._passed(1):
            cp.start()

    def relay(self):
        pass

    def finish(self):
        for a in range(self.n):
            self._copy(a, 3, (*self.diagonal, self.c), self.me).wait_recv()
        for cp in self._passed(2):
            cp.start()
        for a in range(self.n):
            self._copy(a, 0, self.sibling, self.me).wait_recv()
        for j, chip in enumerate(self.near + [self.diagonal]):
            for a in range(self.n):
                self._copy(a, 4 + j, (*chip, 1 - self.c), self.me).wait_recv()
        for cp in self._first() + self._relayed() + self._passed(0) + self._passed(1) + self._passed(2):
            cp.wait_send()
        for cp in self._mine():
            cp.wait()


class _Exchange:
    def __init__(self, in_refs, out_refs, slots, send_sems, recv_sems, local_sems):
        self.in_refs, self.out_refs, self.slots = in_refs, out_refs, slots
        self.send_sems, self.recv_sems, self.local_sems = send_sems, recv_sems, local_sems
        self.n = len(in_refs)
        self.pos = _position()

    def _copies(self):
        x, y, c = self.pos
        me = 4 * x + 2 * y + c
        mine = [pltpu.make_async_copy(self.slots[a](self.in_refs[a], me), self.out_refs[a].at[me],
                                      self.local_sems.at[a]) for a in range(self.n)]
        remote = []
        for k in range(1, N_DEV):
            px, py, pc = x ^ (k >> 2), y ^ ((k >> 1) & 1), c ^ (k & 1)
            for a in range(self.n):
                remote.append(pltpu.make_async_remote_copy(
                    src_ref=self.slots[a](self.in_refs[a], 4 * px + 2 * py + pc), dst_ref=self.out_refs[a].at[me],
                    send_sem=self.send_sems.at[a, k - 1], recv_sem=self.recv_sems.at[a, k - 1],
                    device_id=(px, py, pc), device_id_type=MESH))
        return mine, remote

    def start(self):
        mine, remote = self._copies()
        for cp in mine + remote:
            cp.start()

    def forward(self):
        pass

    def relay(self):
        pass

    def finish(self):
        mine, remote = self._copies()
        for cp in remote:
            cp.wait_recv()
        for cp in remote:
            cp.wait_send()
        for cp in mine:
            cp.wait()


class _Rider:
    def __init__(self, kind, arrays, out_shapes, slots, scratch=None, forward_step=None):
        self.kind, self.arrays, self.slots = kind, list(arrays), slots
        self.n = len(self.arrays)
        hbm = pl.BlockSpec(memory_space=pl.ANY)
        self.in_specs = [hbm] * self.n
        self.out_specs = [hbm] * self.n
        self.out_shape = [jax.ShapeDtypeStruct(tuple(s), a.dtype) for s, a in zip(out_shapes, self.arrays)]
        self.scratch = scratch if scratch is not None else [
            pltpu.SemaphoreType.DMA((self.n, 7)), pltpu.SemaphoreType.DMA((self.n, 7)),
            pltpu.SemaphoreType.DMA((self.n,))]
        self.forward_step = forward_step
        self.relay_step = None

    def bind(self, in_refs, out_refs, scratch):
        return self.kind(in_refs, out_refs, self.slots, *scratch)


def _gather_rider(shards, full_shapes, slots, forward_step=None):
    return _Rider(_Gather, shards, full_shapes, slots, None, forward_step)


def _exchange_rider(sends, part_shapes, slots):
    return _Rider(_Exchange, sends, [(N_DEV,) + tuple(s) for s in part_shapes], slots)


def _split_refs(refs, n_in, n_out, n_scratch, rider):
    k = rider.n if rider is not None else 0
    ins, r_ins = refs[:n_in], refs[n_in:n_in + k]
    outs, r_outs = refs[n_in + k:n_in + k + n_out], refs[n_in + k + n_out:n_in + 2 * k + n_out]
    rest = refs[n_in + 2 * k + n_out:]
    scratch, r_scratch = rest[:n_scratch], rest[n_scratch:]
    comm = rider.bind(r_ins, r_outs, r_scratch) if rider is not None else None
    if comm is not None:
        comm.forward_step, comm.relay_step = rider.forward_step, rider.relay_step
    return ins + outs + scratch, comm


def _ride_before(comm, i, nt):
    if comm is not None:
        pl.when(i == 0)(comm.start)
        pl.when(i == (nt - 1 if comm.forward_step is None else min(comm.forward_step, nt - 1)))(comm.forward)
        pl.when(i == (nt - 1 if comm.relay_step is None else min(comm.relay_step, nt - 1)))(comm.relay)


def _ride_after(comm, i, nt):
    if comm is not None:
        pl.when(i == nt - 1)(comm.finish)


def _extend(specs, rider, field):
    return list(specs) + (getattr(rider, field) if rider is not None else [])


def _comm_call(name, rider):
    def body(*refs):
        _, comm = _split_refs(refs, 0, 0, 0, rider)
        comm.start()
        comm.forward()
        comm.relay()
        comm.finish()

    return pl.pallas_call(body, name=name, in_specs=rider.in_specs, out_specs=rider.out_specs,
                          out_shape=rider.out_shape, scratch_shapes=rider.scratch,
                          compiler_params=pltpu.CompilerParams(vmem_limit_bytes=VMEM_LIMIT))(*rider.arrays)


N_CHIPS = 4


class _TwoLevel:
    def __init__(self, in_refs, out_refs, slots, *scratch):
        self.in_refs, self.out_refs, self.slots = in_refs, out_refs, slots
        self.n = n = len(in_refs)
        self.own_bufs, self.recv_bufs, self.relay_bufs = scratch[:n], scratch[n:2 * n], scratch[2 * n:3 * n]
        self.swap_send, self.swap_recv, self.local_sems, self.chip_send, self.chip_recv = scratch[3 * n:]
        x, y, c = self.pos = _position()
        self.first = (x ^ (1 - c), y ^ c)
        self.second = (x ^ c, y ^ (1 - c))
        self.chip_index = lambda chip: 2 * chip[0] + chip[1]

    def _swap(self):
        x, y, c = self.pos
        return [pltpu.make_async_remote_copy(
            src_ref=self.slots[a](self.in_refs[a], 2 * q + 1 - c), dst_ref=self.recv_bufs[a].at[q],
            send_sem=self.swap_send.at[a, q], recv_sem=self.swap_recv.at[a, q],
            device_id=(x, y, 1 - c), device_id_type=MESH) for a in range(self.n) for q in range(N_CHIPS)]

    def _mine(self):
        c = self.pos[2]
        return [pltpu.make_async_copy(self.slots[a](self.in_refs[a], 2 * q + c), self.own_bufs[a].at[q],
                                      self.local_sems.at[a, q]) for a in range(self.n) for q in range(N_CHIPS)]

    def _to_chip(self, a, k, src, dst, chip):
        return pltpu.make_async_remote_copy(
            src_ref=src, dst_ref=dst, send_sem=self.chip_send.at[a, k], recv_sem=self.chip_recv.at[a, k],
            device_id=(*chip, self.pos[2]), device_id_type=MESH)

    def _first_wave(self):
        x, y, _ = self.pos
        diagonal = self.chip_index((1 - x, 1 - y))
        passed_on = [self._to_chip(a, 1, self.own_bufs[a].at[diagonal], self.relay_bufs[a], self.first)
                     for a in range(self.n)]
        return passed_on + [self._to_chip(a, 0, self.own_bufs[a].at[self.chip_index(self.first)],
                                          self.out_refs[a].at[1], self.first) for a in range(self.n)]

    def _second_wave(self):
        return [self._to_chip(a, 2, self.own_bufs[a].at[self.chip_index(self.second)], self.out_refs[a].at[2],
                              self.second) for a in range(self.n)]

    def _own(self):
        x, y, _ = self.pos
        return [pltpu.make_async_copy(self.own_bufs[a].at[2 * x + y], self.out_refs[a].at[0],
                                      self.local_sems.at[a, N_CHIPS]) for a in range(self.n)]

    def start(self):
        for cp in self._swap() + self._mine():
            cp.start()

    def forward(self):
        swap, mine = self._swap(), self._mine()
        for a in range(self.n):
            for q in range(N_CHIPS):
                mine[a * N_CHIPS + q].wait()
                swap[a * N_CHIPS + q].wait_recv()
                self.own_bufs[a][q] = (self.own_bufs[a][q].astype(F32)
                                       + self.recv_bufs[a][q].astype(F32)).astype(BF16)
        for cp in self._first_wave() + self._own():
            cp.start()

    def relay(self):
        second = self.chip_index(self.second)
        for a in range(self.n):
            self._to_chip(a, 1, self.relay_bufs[a], self.relay_bufs[a], self.first).wait_recv()
            self.own_bufs[a][second] = (self.own_bufs[a][second].astype(F32)
                                        + self.relay_bufs[a][...].astype(F32)).astype(BF16)
        for cp in self._second_wave():
            cp.start()

    def finish(self):
        for a in range(self.n):
            self._to_chip(a, 0, self.out_refs[a].at[1], self.out_refs[a].at[1], self.first).wait_recv()
            self._to_chip(a, 2, self.out_refs[a].at[2], self.out_refs[a].at[2], self.second).wait_recv()
        for cp in self._first_wave() + self._second_wave() + self._swap():
            cp.wait_send()
        for cp in self._own():
            cp.wait()


def _two_level_rider(sends, part_shapes, slots, forward_step=None, relay_step=None):
    n = len(sends)
    bufs = [pltpu.VMEM((N_CHIPS,) + tuple(s), a.dtype) for s, a in zip(part_shapes, sends)]
    relay_bufs = [pltpu.VMEM(tuple(s), a.dtype) for s, a in zip(part_shapes, sends)]
    scratch = bufs + bufs + relay_bufs + [
        pltpu.SemaphoreType.DMA((n, N_CHIPS)), pltpu.SemaphoreType.DMA((n, N_CHIPS)),
        pltpu.SemaphoreType.DMA((n, N_CHIPS + 1)), pltpu.SemaphoreType.DMA((n, 3)), pltpu.SemaphoreType.DMA((n, 3))]
    rider = _Rider(_TwoLevel, sends, [(3,) + tuple(s) for s in part_shapes], slots, scratch, forward_step)
    rider.relay_step = relay_step
    return rider


class _Joined:
    def __init__(self, first, second):
        self.first, self.second = first, second

    def start(self):
        self.first.start()
        self.second.start()

    def forward(self):
        self.first.forward()
        self.second.forward()

    def relay(self):
        self.first.relay()
        self.second.relay()

    def finish(self):
        self.first.finish()
        self.second.finish()


class _JoinedRider:
    def __init__(self, first, second):
        self.first, self.second = first, second
        self.n = first.n + second.n
        self.arrays = first.arrays + second.arrays
        self.in_specs = first.in_specs + second.in_specs
        self.out_specs = first.out_specs + second.out_specs
        self.out_shape = first.out_shape + second.out_shape
        self.scratch = first.scratch + second.scratch
        self.forward_step = first.forward_step
        self.relay_step = first.relay_step

    def bind(self, in_refs, out_refs, scratch):
        k, s = self.first.n, len(self.first.scratch)
        return _Joined(self.first.bind(in_refs[:k], out_refs[:k], scratch[:s]),
                       self.second.bind(in_refs[k:], out_refs[k:], scratch[s:]))


def _adamw(w, g, m, v):
    m = ADAM_B1 * m + (1.0 - ADAM_B1) * g
    v = ADAM_B2 * v + (1.0 - ADAM_B2) * (g * g)
    m_hat = m / (1.0 - ADAM_B1 ** ADAM_STEP)
    v_hat = v / (1.0 - ADAM_B2 ** ADAM_STEP)
    delta = -ADAM_LR * (m_hat / (jnp.sqrt(v_hat) + ADAM_EPS) + ADAM_WD * w)
    return delta, m, v


def _sum_parts(parts_ref, index=()):
    g = parts_ref[(0,) + index].astype(F32)
    for s in range(1, parts_ref.shape[0]):
        g = g + parts_ref[(s,) + index].astype(F32)
    return g


def _adamw_group_call(name, groups):
    k = len(groups)

    def body(*refs):
        ins, outs = refs[:4 * k], refs[4 * k:]
        for i in range(k):
            parts_ref, w_ref, m_ref, v_ref = ins[4 * i:4 * i + 4]
            g = _sum_parts(parts_ref)
            delta, m_new, v_new = _adamw(w_ref[...], g, m_ref[...], v_ref[...])
            for out_ref, value in zip(outs[4 * i:4 * i + 4], (g, delta, m_new, v_new)):
                out_ref[...] = value

    vmem = pl.BlockSpec(memory_space=pltpu.VMEM)
    res = pl.pallas_call(
        body, name=name, in_specs=[vmem] * (4 * k), out_specs=[vmem] * (4 * k),
        out_shape=[jax.ShapeDtypeStruct(grp[1].shape, F32) for grp in groups for _ in range(4)],
        compiler_params=pltpu.CompilerParams(vmem_limit_bytes=VMEM_LIMIT),
    )(*[a for grp in groups for a in grp])
    return [res[4 * i:4 * i + 4] for i in range(k)]


def _adamw_slabs_call(name, parts, w, m, v, rider=None):
    def main(parts_ref, w_ref, m_ref, v_ref, g_ref, delta_ref, m_out, v_out):
        g = _sum_parts(parts_ref)
        delta, m_new, v_new = _adamw(w_ref[...], g, m_ref[...], v_ref[...])
        g_ref[...] = g
        delta_ref[...] = delta
        m_out[...] = m_new
        v_out[...] = v_new

    def body(*refs):
        own, comm = _split_refs(refs, 4, 4, 0, rider)
        if comm is not None:
            comm.start()
        main(*own)
        if comm is not None:
            comm.forward()
            comm.relay()
            comm.finish()

    vmem = pl.BlockSpec(memory_space=pltpu.VMEM)
    return pl.pallas_call(
        body, name=name, in_specs=_extend([vmem] * 4, rider, "in_specs"),
        out_specs=_extend([vmem] * 4, rider, "out_specs"),
        out_shape=_extend([jax.ShapeDtypeStruct(w.shape, F32)] * 4, rider, "out_shape"),
        scratch_shapes=_extend([], rider, "scratch"),
        compiler_params=pltpu.CompilerParams(vmem_limit_bytes=VMEM_LIMIT),
    )(parts, w, m, v, *_extend([], rider, "arrays"))


WIDE_ROWS = 8
NARROW_ROWS = 40
NARROW_GKW_ROW = 8
NARROW_GKB_ROW = 24
NARROW_HW_ROW = 32
GROUP_SHARD = POOL_GROUP_DIM // N_DEV
KEY_SHARD = GLA_KEY_WIDTH // N_DEV
HEAD_V_SHARD = GLA_HEAD_V // N_DEV


def _small_adamw_call(wide, narrow, w, m, v):
    names = ("norm_w", "pool_scale", "final_norm_w", "pool_group_b", "gla_gk_w", "gla_gk_b", "gla_head_norm_w")
    where = {
        "norm_w": (0, slice(0, 2), slice(None)),
        "pool_scale": (0, slice(2, 3), slice(None)),
        "final_norm_w": (0, slice(3, 4), slice(None)),
        "pool_group_b": (1, slice(0, POOL_GROUPS), slice(0, GROUP_SHARD)),
        "gla_gk_w": (1, slice(NARROW_GKW_ROW, NARROW_GKW_ROW + GLA_GATE_RANK), slice(0, KEY_SHARD)),
        "gla_gk_b": (1, slice(NARROW_GKB_ROW, NARROW_GKB_ROW + 1), slice(0, KEY_SHARD)),
        "gla_head_norm_w": (1, slice(NARROW_HW_ROW, NARROW_HW_ROW + 1), slice(0, HEAD_V_SHARD)),
    }
    k = len(names)

    def body(*refs):
        parts = refs[0:2]
        w_refs, m_refs, v_refs = refs[2:2 + k], refs[2 + k:2 + 2 * k], refs[2 + 2 * k:2 + 3 * k]
        outs = refs[2 + 3 * k:]
        loss_ref = outs[0]
        loss_ref[...] = _sum_parts(parts[0], (slice(4, 5), slice(0, 1)))
        for i, name in enumerate(names):
            buf, rows, cols = where[name]
            g = _sum_parts(parts[buf], (rows, cols))
            delta, m_new, v_new = _adamw(w_refs[i][...], g, m_refs[i][...], v_refs[i][...])
            outs[1 + i][...] = g
            outs[1 + k + i][...] = delta
            outs[1 + 2 * k + i][...] = m_new
            outs[1 + 3 * k + i][...] = v_new

    vmem = pl.BlockSpec(memory_space=pltpu.VMEM)
    shapes = [jax.ShapeDtypeStruct(w[n].shape, F32) for n in names]
    res = pl.pallas_call(
        body, name="adamw_small", in_specs=[vmem] * (2 + 3 * k), out_specs=[vmem] * (1 + 4 * k),
        out_shape=[jax.ShapeDtypeStruct((1, 1), F32)] + shapes * 4,
    )(wide, narrow, *[w[n] for n in names], *[m[n] for n in names], *[v[n] for n in names])
    unzip = lambda j: dict(zip(names, res[1 + j * k:1 + (j + 1) * k]))
    return res[0], unzip(0), unzip(1), unzip(2), unzip(3)


def kernel(x, norm_w, pool_in_w, pool_group_w, pool_group_b, pool_scale, pool_out_w, gla_in_w, gla_gk_w, gla_gk_b, gla_head_norm_w, gla_out_w, final_norm_w, loss_target, m_norm_w, m_pool_in_w, m_pool_group_w, m_pool_group_b, m_pool_scale, m_pool_out_w, m_gla_in_w, m_gla_gk_w, m_gla_gk_b, m_gla_head_norm_w, m_gla_out_w, m_final_norm_w, v_norm_w, v_pool_in_w, v_pool_group_w, v_pool_group_b, v_pool_scale, v_pool_out_w, v_gla_in_w, v_gla_gk_w, v_gla_gk_b, v_gla_head_norm_w, v_gla_out_w, v_final_norm_w):
    w = dict(norm_w=norm_w, pool_in_w=pool_in_w, pool_group_w=pool_group_w, pool_group_b=pool_group_b,
             pool_scale=pool_scale, pool_out_w=pool_out_w, gla_in_w=gla_in_w, gla_gk_w=gla_gk_w, gla_gk_b=gla_gk_b,
             gla_head_norm_w=gla_head_norm_w, gla_out_w=gla_out_w, final_norm_w=final_norm_w)
    m = dict(norm_w=m_norm_w, pool_in_w=m_pool_in_w, pool_group_w=m_pool_group_w, pool_group_b=m_pool_group_b,
             pool_scale=m_pool_scale, pool_out_w=m_pool_out_w, gla_in_w=m_gla_in_w, gla_gk_w=m_gla_gk_w,
             gla_gk_b=m_gla_gk_b, gla_head_norm_w=m_gla_head_norm_w, gla_out_w=m_gla_out_w,
             final_norm_w=m_final_norm_w)
    v = dict(norm_w=v_norm_w, pool_in_w=v_pool_in_w, pool_group_w=v_pool_group_w, pool_group_b=v_pool_group_b,
             pool_scale=v_pool_scale, pool_out_w=v_pool_out_w, gla_in_w=v_gla_in_w, gla_gk_w=v_gla_gk_w,
             gla_gk_b=v_gla_gk_b, gla_head_norm_w=v_gla_head_norm_w, gla_out_w=v_gla_out_w,
             final_norm_w=v_final_norm_w)
    col_shard = GLA_IN_WIDTH // N_DEV
    row_shard = D_MODEL // N_DEV

    def lanes(a):
        return jnp.pad(a, [(0, 0)] * (a.ndim - 1) + [(0, LANES - a.shape[-1])])

    small_in = jnp.concatenate([lanes(pool_group_b[0]), lanes(gla_gk_b), lanes(gla_head_norm_w),
                                jnp.zeros((2, LANES), F32)], axis=0)
    in_cols = 2 * POOL_WIDTH // N_DEV
    pool_in, pool_gw, pool_out, small_all = _comm_call("pool_weights_all_gather", _gather_rider(
        [pool_in_w[0].astype(BF16), pool_group_w[0].astype(BF16), pool_out_w[0].astype(BF16), small_in],
        [(D_MODEL, 2 * POOL_WIDTH), (POOL_GROUPS, POOL_GROUP_DIM, POOL_GROUP_DIM), (POOL_WIDTH, D_MODEL),
         (N_DEV, 8, LANES)],
        [_dim1_slot(in_cols), _dim1_slot(GROUP_SHARD), _row_slot(row_shard), _lead_slot]))
    pool_gb = jnp.transpose(small_all[:, 0:POOL_GROUPS, :GROUP_SHARD], (1, 0, 2)).reshape(1, POOL_WIDTH)
    gla_gkb = small_all[:, POOL_GROUPS, :KEY_SHARD].reshape(1, GLA_KEY_WIDTH)
    gla_hw = jnp.tile(small_all[:, POOL_GROUPS + 1, :HEAD_V_SHARD].reshape(1, GLA_HEAD_V), (1, GLA_HEADS))
    nw0, nw1, wf = norm_w[0:1], norm_w[1:2], final_norm_w.reshape(1, D_MODEL)
    xs, target = x[0], loss_target[0]

    h1, p, gla_in_parts, gkw_parts, gla_out = _pool_fwd_call(
        xs, nw0, pool_in, pool_gw, pool_gb, pool_scale, pool_out, _gather_rider(
            [jnp.transpose(gla_in_w[0]).astype(BF16), gla_gk_w[0].astype(BF16), gla_out_w[0].astype(BF16)],
            [(N_DEV, col_shard, D_MODEL), (N_DEV, GLA_GATE_RANK, KEY_SHARD), (GLA_VALUE_WIDTH, D_MODEL)],
            [_lead_slot, _lead_slot, _row_slot(row_shard)], GATHER_RELAY_STEP))
    gla_in = gla_in_parts.reshape(GLA_IN_WIDTH, D_MODEL)
    gla_gkw = jnp.pad(jnp.transpose(gkw_parts, (1, 0, 2)).reshape(GLA_GATE_RANK, GLA_KEY_WIDTH),
                      ((0, GLA_LOW_PAD - GLA_GATE_RANK), (0, 0)))
    dh2, proj, o, states, loss_part, dwf = _gla_fwd_call(h1, nw1, gla_in, gla_gkw, gla_gkb, gla_hw, gla_out, wf, target)

    dproj, d_gla_out, dhw, dgkw, dgkb = _gla_bwd_call(dh2, proj, o, states, gla_gkw, gla_gkb, gla_hw, gla_out)
    dh1, d_gla_in, dnw1, landed_gla_out = _inproj_bwd_call(
        "gla_in_bwd", dproj, h1, nw1, gla_in, dh2,
        _exchange_rider([d_gla_out], [(row_shard, D_MODEL)], [_row_slot(row_shard)]), transposed=True)
    slabs = col_shard * D_MODEL // (BF16_ROWS * LANES)
    gla_in_send = d_gla_in.reshape(N_DEV, slabs, BF16_ROWS, LANES)
    dp, d_pool_out, dgw, dgb, dsc, landed_gla_in = _pool_bwd_call(
        dh1, p, pool_gw, pool_gb, pool_scale, pool_out,
        _two_level_rider([gla_in_send], [(slabs, BF16_ROWS, LANES)], [_lead_slot], TWO_LEVEL_ADD_STEP,
                         TWO_LEVEL_RELAY_STEP))
    grad_x, d_pool_in, dnw0 = _inproj_bwd_call("pool_in_bwd", dp, xs, nw0, pool_in, dh1)

    wide = jnp.concatenate([
        dnw0, dnw1, dsc, dwf, jnp.pad(loss_part[0:1, 0:1], ((0, 0), (0, D_MODEL - 1))),
        jnp.zeros((WIDE_ROWS - 5, D_MODEL), F32)], axis=0)

    def rows8(a):
        return jnp.pad(lanes(a), ((0, 0), (0, -a.shape[1] % 8), (0, 0)))

    narrow = jnp.concatenate([
        rows8(jnp.transpose(dgb.reshape(POOL_GROUPS, N_DEV, GROUP_SHARD), (1, 0, 2))),
        rows8(jnp.transpose(dgkw[:GLA_GATE_RANK].reshape(GLA_GATE_RANK, N_DEV, KEY_SHARD), (1, 0, 2))),
        rows8(dgkb.reshape(N_DEV, 1, KEY_SHARD)),
        rows8(dhw.reshape(GLA_HEADS, GLA_HEAD_V).sum(axis=0).reshape(N_DEV, 1, HEAD_V_SHARD)),
    ], axis=1)
    last_exchange = _JoinedRider(
        _two_level_rider([d_pool_in, d_pool_out, dgw],
                         [(D_MODEL, in_cols), (row_shard, D_MODEL), (POOL_GROUPS, GROUP_SHARD, POOL_GROUP_DIM)],
                         [_dim1_slot(in_cols), _row_slot(row_shard), _dim1_slot(GROUP_SHARD)]),
        _exchange_rider([wide, narrow], [(WIDE_ROWS, D_MODEL), (NARROW_ROWS, LANES)],
                        [lambda ref, d: ref, _lead_slot]))

    res = {}
    as_slabs = lambda t: jnp.transpose(t[0]).reshape(slabs, BF16_ROWS, LANES)
    *outs, landed_pool_in, landed_pool_out, landed_gw, landed_wide, landed_narrow = _adamw_slabs_call(
        "adamw_gla_in_w", landed_gla_in, as_slabs(gla_in_w), as_slabs(m_gla_in_w), as_slabs(v_gla_in_w),
        last_exchange)
    res["gla_in_w"] = [jnp.transpose(t.reshape(col_shard, D_MODEL))[None] for t in outs]
    rest = [("pool_in_w", landed_pool_in, (D_MODEL, in_cols)),
            ("pool_group_w", landed_gw, (POOL_GROUPS * GROUP_SHARD, POOL_GROUP_DIM)),
            ("pool_out_w", landed_pool_out, (row_shard, D_MODEL)), ("gla_out_w", landed_gla_out, (row_shard, D_MODEL))]
    updates = _adamw_group_call("adamw_matrices", [
        (parts.reshape((parts.shape[0],) + shape), w[name].reshape(shape), m[name].reshape(shape),
         v[name].reshape(shape)) for name, parts, shape in rest])
    for (name, _, _), outs in zip(rest, updates):
        res[name] = [t.reshape(w[name].shape) for t in outs]
    small_shapes ={"norm_w": (2, D_MODEL), "pool_scale": (1, D_MODEL), "final_norm_w": (1, D_MODEL),
                    "pool_group_b": (POOL_GROUPS, GROUP_SHARD), "gla_gk_w": (GLA_GATE_RANK, KEY_SHARD),
                    "gla_gk_b": (1, KEY_SHARD), "gla_head_norm_w": (1, HEAD_V_SHARD)}
    as_small = lambda t: {n: t[n].reshape(s) for n, s in small_shapes.items()}
    loss, *small_outs = _small_adamw_call(landed_wide, landed_narrow, as_small(w), as_small(m), as_small(v))
    for name in small_shapes:
        res[name] = [t[name].reshape(w[name].shape) for t in small_outs]
    order = ("norm_w", "pool_in_w", "pool_group_w", "pool_group_b", "pool_scale", "pool_out_w", "gla_in_w",
             "gla_gk_w", "gla_gk_b", "gla_head_norm_w", "gla_out_w", "final_norm_w")
    return (loss.reshape(()), grad_x[None], *[res[n][0] for n in order], *[res[n][1] for n in order],
            *[res[n][2] for n in order], *[res[n][3] for n in order])
```

```python
import jax
import jax.numpy as jnp
from jax import lax
from jax.experimental import pallas as pl
from jax.experimental.pallas import tpu as pltpu

F32 = jnp.float32
BF16 = jnp.bfloat16
MESH = pl.DeviceIdType.MESH

N_DEV = 8
D_MODEL = 1024
POOL_WIDTH = 1024
POOL_GROUPS = 4
POOL_GROUP_DIM = 256
POOL_HALO = 16
GLA_HEADS = 4
GLA_HEAD_K = 128
GLA_HEAD_V = 256
GLA_KEY_WIDTH = 512
GLA_VALUE_WIDTH = 1024
GLA_GATE_RANK = 16
GLA_IN_WIDTH = 3088
GLA_IN_PAD = 3200
GLA_SAVED_Z = GLA_IN_PAD
GLA_SAVED_C = GLA_SAVED_Z + 512
GLA_SAVED_WIDTH = GLA_SAVED_C + 512
GLA_LOW_PAD = 128
GLA_QKVG_WIDTH = 3072
CHUNK = 64
GATE_NORMALIZER = 16.0
RMS_EPS = 1e-6
Q_SCALE = GLA_HEAD_K ** -0.5

ADAM_LR = 0.001
ADAM_B1 = 0.9
ADAM_B2 = 0.999
ADAM_EPS = 1e-08
ADAM_WD = 0.01
ADAM_STEP = 10

LANES = 128
BF16_ROWS = 16
VMEM_LIMIT = 56 * 1024 * 1024
ROW_TILE = 256
GLA_FWD_ROW_TILE = 512
MATMUL_ROW_TILE = 512
GATHER_RELAY_STEP = 5
TWO_LEVEL_ADD_STEP = 1
TWO_LEVEL_RELAY_STEP = 4


def _dot_nn(a, b):
    return lax.dot_general(a, b, (((1,), (0,)), ((), ())), preferred_element_type=F32)


def _dot_nt(a, b):
    return lax.dot_general(a, b, (((1,), (1,)), ((), ())), preferred_element_type=F32)


def _dot_tn(a, b):
    return lax.dot_general(a, b, (((0,), (0,)), ((), ())), preferred_element_type=F32)


def _rms(x):
    rstd = lax.rsqrt(jnp.mean(x * x, axis=-1, keepdims=True) + RMS_EPS)
    return x * rstd, rstd


def _rms_bwd(dxhat, xhat, rstd):
    return rstd * (dxhat - xhat * jnp.mean(dxhat * xhat, axis=-1, keepdims=True))


def _sigmoid(x):
    return 1.0 / (1.0 + jnp.exp(-x))


def _params(sem=("arbitrary",)):
    return pltpu.CompilerParams(dimension_semantics=sem, vmem_limit_bytes=VMEM_LIMIT)


def _full(shape):
    return pl.BlockSpec(shape, lambda i: (0,) * len(shape))


def _const(shape):
    return pl.BlockSpec(shape, lambda i: (0,) * len(shape), pipeline_mode=pl.Buffered(1))


def _window_sums(ext, forward):
    n = ext.shape[0]
    outs = []
    for g in range(POOL_GROUPS):
        s = ext[:, g * POOL_GROUP_DIM:(g + 1) * POOL_GROUP_DIM]
        for k in range(g + 1):
            shift = (1 << k) if forward else n - (1 << k)
            s = s + pltpu.roll(s, shift, axis=0)
        outs.append(s[:n - POOL_HALO])
    return outs


def _inv_count(row0, tm):
    row = row0 + lax.broadcasted_iota(jnp.int32, (tm, 1), 0)
    return [1.0 / jnp.minimum(row + 1, 2 << g).astype(F32) for g in range(POOL_GROUPS)]


def _pool_mix(u, u_prev, row0, gw_ref, gb):
    tm = u.shape[0]
    sums = _window_sums(jnp.concatenate([u, u_prev], axis=0), True)
    inv = _inv_count(row0, tm)
    pooled, mixed = [], []
    for g in range(POOL_GROUPS):
        ug = u[:, g * POOL_GROUP_DIM:(g + 1) * POOL_GROUP_DIM]
        pg = (sums[g] * inv[g] - ug).astype(BF16)
        pooled.append(pg)
        mixed.append(_dot_nn(pg, gw_ref[g]))
    return pooled, jnp.concatenate(mixed, axis=1) + gb


def _pool_fwd_call(x, nw, w_in, gw, gb, sc, w_out, rider=None):
    seq = x.shape[0]
    tm = min(MATMUL_ROW_TILE, seq)
    nt = seq // tm

    def main(x_ref, nw_ref, win_ref, gw_ref, gb_ref, sc_ref, wout_ref, h_ref, gate_ref, pooled_ref, mixed_ref,
             halo_ref):
        i = pl.program_id(0)

        @pl.when(i == 0)
        def _():
            halo_ref[...] = jnp.zeros_like(halo_ref)

        xt = x_ref[...]
        xhat, _ = _rms(xt)
        n = (xhat * nw_ref[...]).astype(BF16)
        p = _dot_nn(n, win_ref[...])
        u = p[:, :POOL_WIDTH]
        gate = p[:, POOL_WIDTH:]
        gate_ref[...] = gate
        pooled, mixed = _pool_mix(u, halo_ref[...], i * tm, gw_ref, gb_ref[...])
        pooled_ref[...] = jnp.concatenate(pooled, axis=1)
        mixed_ref[...] = mixed
        halo_ref[...] = u[tm - POOL_HALO:, :]
        y = (mixed * sc_ref[...] * (gate * _sigmoid(gate))).astype(BF16)
        h_ref[...] = xt + _dot_nn(y, wout_ref[...])

    def body(*refs):
        own, comm = _split_refs(refs, 7, 4, 1, rider)
        _ride_before(comm, pl.program_id(0), nt)
        main(*own)
        _ride_after(comm, pl.program_id(0), nt)

    return pl.pallas_call(
        body, name="pool_fwd", grid=(nt,),
        in_specs=_extend([pl.BlockSpec((tm, D_MODEL), lambda i: (i, 0)), _const((1, D_MODEL)),
                          _const((D_MODEL, 2 * POOL_WIDTH)), _const((POOL_GROUPS, POOL_GROUP_DIM, POOL_GROUP_DIM)),
                          _const((1, POOL_WIDTH)), _const((1, POOL_WIDTH)), _const((POOL_WIDTH, D_MODEL))],
                         rider, "in_specs"),
        out_specs=_extend([pl.BlockSpec((tm, D_MODEL), lambda i: (i, 0))] * 4, rider, "out_specs"),
        out_shape=_extend([jax.ShapeDtypeStruct((seq, D_MODEL), F32), jax.ShapeDtypeStruct((seq, POOL_WIDTH), F32),
                           jax.ShapeDtypeStruct((seq, POOL_WIDTH), BF16),
                           jax.ShapeDtypeStruct((seq, POOL_WIDTH), F32)], rider, "out_shape"),
        scratch_shapes=_extend([pltpu.VMEM((POOL_HALO, POOL_WIDTH), F32)], rider, "scratch"),
        compiler_params=_params(),
    )(x, nw, w_in, gw, gb, sc, w_out, *_extend([], rider, "arrays"))


def _pool_bwd_call(dh, gate, pooled, mixed, gw, sc, w_out, rider=None):
    seq = dh.shape[0]
    tm = min(MATMUL_ROW_TILE, seq)
    nt = seq // tm

    def main(dh_ref, gate_ref, pooled_ref, mixed_ref, gw_ref, sc_ref, wout_ref,
             dp_ref, dwout_hbm, dgw_hbm, dgb_ref, dsc_ref, carry_ref, dwout_acc, dgw_acc, dwout_stage, dgw_stage):
        i = pl.program_id(0)
        t = nt - 1 - i

        @pl.when(i == 0)
        def _():
            carry_ref[...] = jnp.zeros_like(carry_ref)
            dwout_acc[...] = jnp.zeros_like(dwout_acc)
            dgw_acc[...] = jnp.zeros_like(dgw_acc)
            dgb_ref[...] = jnp.zeros_like(dgb_ref)
            dsc_ref[...] = jnp.zeros_like(dsc_ref)

        gate = gate_ref[...]
        mixed = mixed_ref[...]
        pooled = [pooled_ref[:, g * POOL_GROUP_DIM:(g + 1) * POOL_GROUP_DIM] for g in range(POOL_GROUPS)]
        sg = _sigmoid(gate)
        silu = gate * sg
        sc = sc_ref[...]
        dhb = dh_ref[...].astype(BF16)
        y = (mixed * sc * silu).astype(BF16)
        dwout_acc[...] += _dot_tn(y, dhb)
        dy = _dot_nt(dhb, wout_ref[...])
        dmixed = dy * sc * silu
        dsc_ref[...] += jnp.sum(dy * mixed * silu, axis=0, keepdims=True)
        dgate = dy * mixed * sc * (sg * (1.0 + gate * (1.0 - sg)))
        dgb_ref[...] += jnp.sum(dmixed, axis=0, keepdims=True)
        inv = _inv_count(t * tm, tm)
        dpooled, scaled = [], []
        for g in range(POOL_GROUPS):
            dmg = dmixed[:, g * POOL_GROUP_DIM:(g + 1) * POOL_GROUP_DIM].astype(BF16)
            dgw_acc[g] += _dot_tn(pooled[g], dmg)
            dpg = _dot_nt(dmg, gw_ref[g])
            dpooled.append(dpg)
            scaled.append(dpg * inv[g])
        r = jnp.concatenate(scaled, axis=1)
        sums = _window_sums(jnp.concatenate([r, carry_ref[...]], axis=0), False)
        carry_ref[...] = r[:POOL_HALO, :]
        du = jnp.concatenate([sums[g] - dpooled[g] for g in range(POOL_GROUPS)], axis=1)
        dp_ref[...] = jnp.concatenate([du, dgate], axis=1).astype(BF16)

        @pl.when(i == nt - 1)
        def _():
            dwout_stage[...] = dwout_acc[...].astype(BF16)
            dgw_stage[...] = dgw_acc[...].astype(BF16)
            pltpu.sync_copy(dwout_stage, dwout_hbm)
            pltpu.sync_copy(dgw_stage, dgw_hbm)

    def body(*refs):
        own, comm = _split_refs(refs, 7, 5, 5, rider)
        _ride_before(comm, pl.program_id(0), nt)
        main(*own)
        _ride_after(comm, pl.program_id(0), nt)

    rev = lambda i: (nt - 1 - i, 0)
    return pl.pallas_call(
        body, name="pool_bwd", grid=(nt,),
        in_specs=_extend([pl.BlockSpec((tm, D_MODEL), rev)] * 4
                         + [_const((POOL_GROUPS, POOL_GROUP_DIM, POOL_GROUP_DIM)), _const((1, POOL_WIDTH)),
                            _const((POOL_WIDTH, D_MODEL))], rider, "in_specs"),
        out_specs=_extend([pl.BlockSpec((tm, 2 * POOL_WIDTH), rev), pl.BlockSpec(memory_space=pl.ANY),
                           pl.BlockSpec(memory_space=pl.ANY), _full((1, POOL_WIDTH)), _full((1, POOL_WIDTH))],
                          rider, "out_specs"),
        out_shape=_extend([jax.ShapeDtypeStruct((seq, 2 * POOL_WIDTH), BF16),
                           jax.ShapeDtypeStruct((POOL_WIDTH, D_MODEL), BF16),
                           jax.ShapeDtypeStruct((POOL_GROUPS, POOL_GROUP_DIM, POOL_GROUP_DIM), BF16),
                           jax.ShapeDtypeStruct((1, POOL_WIDTH), F32), jax.ShapeDtypeStruct((1, POOL_WIDTH), F32)],
                          rider, "out_shape"),
        scratch_shapes=_extend([pltpu.VMEM((POOL_HALO, POOL_WIDTH), F32), pltpu.VMEM((POOL_WIDTH, D_MODEL), F32),
                                pltpu.VMEM((POOL_GROUPS, POOL_GROUP_DIM, POOL_GROUP_DIM), F32),
                                pltpu.VMEM((POOL_WIDTH, D_MODEL), BF16),
                                pltpu.VMEM((POOL_GROUPS, POOL_GROUP_DIM, POOL_GROUP_DIM), BF16)], rider, "scratch"),
        compiler_params=_params(),
    )(dh, gate, pooled, mixed, gw, sc, w_out, *_extend([], rider, "arrays"))


def _rows_then_zeros(ref, lo, hi, rows):
    part = ref[lo:hi, :]
    return jnp.concatenate([part, jnp.zeros((rows - (hi - lo), part.shape[1]), part.dtype)], axis=0)


def _inproj_bwd_call(name, dproj, h_in, nw, w_in, dres, rider=None, transposed=False):
    seq = h_in.shape[0]
    width = dproj.shape[1]
    w_shape = tuple(w_in.shape)
    acc_shape = (width, D_MODEL) if transposed else w_shape
    whole = w_shape[0] // LANES * LANES
    tm = min(MATMUL_ROW_TILE, seq)
    nt = seq // tm

    def main(dproj_ref, h_ref, nw_ref, win_ref, dres_ref, dh_ref, dw_hbm, dnw_ref, dw_acc, dw_stage):
        i = pl.program_id(0)

        @pl.when(i == 0)
        def _():
            dw_acc[...] = jnp.zeros_like(dw_acc)
            dnw_ref[...] = jnp.zeros_like(dnw_ref)

        dpb = dproj_ref[...]
        if transposed:
            dn = _dot_nn(dpb[:, :whole], win_ref[0:whole, :])
            if whole < w_shape[0]:
                dn = dn + _dot_nn(dpb[:, whole:], _rows_then_zeros(win_ref, whole, w_shape[0], width - whole))
        else:
            dn = _dot_nt(dpb, win_ref[...])
        xhat, rstd = _rms(h_ref[...])
        nw_row = nw_ref[...]
        n = (xhat * nw_row).astype(BF16)
        dw_acc[...] += _dot_tn(dpb, n) if transposed else _dot_tn(n, dpb)
        dnw_ref[...] += jnp.sum(dn * xhat, axis=0, keepdims=True)
        dh_ref[...] = _rms_bwd(dn * nw_row, xhat, rstd) + dres_ref[...]

        @pl.when(i == nt - 1)
        def _():
            dw_stage[...] = dw_acc[...].astype(BF16)
            pltpu.sync_copy(dw_stage.at[pl.ds(0, w_shape[0])], dw_hbm)

    def body(*refs):
        own, comm = _split_refs(refs, 5, 3, 2, rider)
        _ride_before(comm, pl.program_id(0), nt)
        main(*own)
        _ride_after(comm, pl.program_id(0), nt)

    row = lambda i: (i, 0)
    return pl.pallas_call(
        body, name=name, grid=(nt,),
        in_specs=_extend([pl.BlockSpec((tm, width), row), pl.BlockSpec((tm, D_MODEL), row), _const((1, D_MODEL)),
                          _const(w_shape), pl.BlockSpec((tm, D_MODEL), row)], rider, "in_specs"),
        out_specs=_extend([pl.BlockSpec((tm, D_MODEL), row), pl.BlockSpec(memory_space=pl.ANY),
                           _full((1, D_MODEL))], rider, "out_specs"),
        out_shape=_extend([jax.ShapeDtypeStruct((seq, D_MODEL), F32), jax.ShapeDtypeStruct(w_shape, BF16),
                           jax.ShapeDtypeStruct((1, D_MODEL), F32)], rider, "out_shape"),
        scratch_shapes=_extend([pltpu.VMEM(acc_shape, F32), pltpu.VMEM(acc_shape, BF16)], rider, "scratch"),
        compiler_params=_params(),
    )(dproj, h_in, nw, w_in, dres, *_extend([], rider, "arrays"))


def _chunk_scan(x, reverse):
    n = x.shape[0]
    pos = lax.broadcasted_iota(jnp.int32, (n, 1), 0) & (CHUNK - 1)
    k = 1
    while k < CHUNK:
        if reverse:
            x = x + jnp.where(pos < CHUNK - k, pltpu.roll(x, n - k, axis=0), 0.0)
        else:
            x = x + jnp.where(pos >= k, pltpu.roll(x, k, axis=0), 0.0)
        k *= 2
    return x


def _chunk_rows(j):
    return slice(j * CHUNK, (j + 1) * CHUNK)


def _kcols(h):
    return slice(h * GLA_HEAD_K, (h + 1) * GLA_HEAD_K)


def _vcols(h):
    return slice(h * GLA_HEAD_V, (h + 1) * GLA_HEAD_V)


def _chunk_masks(tm):
    idx_t = lax.broadcasted_iota(jnp.int32, (tm, tm), 0)
    idx_s = lax.broadcasted_iota(jnp.int32, (tm, tm), 1)
    same_chunk = (idx_t ^ idx_s) < CHUNK
    return same_chunk & (idx_t >= idx_s), same_chunk & (idx_t < idx_s)


class _GlaTerms:
    def __init__(self, kc, q, k, v, low_b, gkw_ref, gkb_ref, masks, saved=None):
        tm = q.shape[0]
        self.q = q * Q_SCALE
        self.k = k
        if saved is None:
            self.z = _dot_nn(low_b, gkw_ref[:, kc]) + gkb_ref[:, kc]
            log_g = (jnp.minimum(self.z, 0.0) - jnp.log(1.0 + jnp.exp(-jnp.abs(self.z)))) / GATE_NORMALIZER
            self.c = _chunk_scan(log_g, False)
        else:
            self.z, self.c = saved
        is_last = lax.broadcasted_iota(jnp.int32, (CHUNK, 1), 0) == CHUNK - 1
        self.c_last = [jnp.sum(jnp.where(is_last, self.c[_chunk_rows(j), :], 0.0), axis=0, keepdims=True)
                       for j in range(tm // CHUNK)]
        c_last_rows = jnp.concatenate([jnp.broadcast_to(r, (CHUNK, r.shape[1])) for r in self.c_last], axis=0)
        self.e_pos = jnp.exp(self.c)
        self.e_neg = jnp.exp(-self.c)
        self.e_rest = jnp.exp(c_last_rows - self.c)
        self.a_b = (self.q * self.e_pos).astype(BF16)
        self.b_b = (self.k * self.e_neg).astype(BF16)
        self.cn_b = (self.q * self.e_neg).astype(BF16)
        self.dp_b = (self.k * self.e_pos).astype(BF16)
        self.kd_b = (self.k * self.e_rest).astype(BF16)
        self.v_b = v.astype(BF16)
        self.lower, self.upper = masks

    def scores(self, kc=slice(None)):
        fwd = _dot_nt(self.a_b[:, kc], self.b_b[:, kc])
        bwd = _dot_nt(self.cn_b[:, kc], self.dp_b[:, kc])
        return jnp.where(self.lower, fwd, jnp.where(self.upper, bwd, 0.0)).astype(BF16)


def _gla_fwd_call(h1, nw, w_in, gkw, gkb, hw, w_out, wf, target):
    seq = h1.shape[0]
    tm = min(GLA_FWD_ROW_TILE, seq)
    nt = seq // tm
    cpt = tm // CHUNK
    n_chunks = seq // CHUNK

    def body(h_ref, nw_ref, win_ref, gkw_ref, gkb_ref, hw_ref, wout_ref, wf_ref, tgt_ref,
             dh2_ref, proj_ref, o_ref, st_ref, scores_ref, loss_ref, dwf_ref, state_ref):
        i = pl.program_id(0)

        @pl.when(i == 0)
        def _():
            state_ref[...] = jnp.zeros_like(state_ref)
            loss_ref[...] = jnp.zeros_like(loss_ref)
            dwf_ref[...] = jnp.zeros_like(dwf_ref)

        ht = h_ref[...]
        xhat, _ = _rms(ht)
        n = (xhat * nw_ref[...]).astype(BF16)
        sections = {}
        for name, lo, hi in (("low", GLA_QKVG_WIDTH, GLA_IN_PAD), ("qk", 0, 2 * GLA_KEY_WIDTH),
                             ("v", 2 * GLA_KEY_WIDTH, GLA_QKVG_WIDTH - GLA_VALUE_WIDTH),
                             ("gate", GLA_QKVG_WIDTH - GLA_VALUE_WIDTH, GLA_QKVG_WIDTH)):
            rows = (win_ref[lo:hi, :] if hi <= GLA_IN_WIDTH
                    else _rows_then_zeros(win_ref, lo, GLA_IN_WIDTH, hi - lo))
            sections[name] = _dot_nt(n, rows)
            proj_ref[:, lo:hi] = sections[name]
        low_b = sections["low"].astype(BF16)
        masks = _chunk_masks(tm)
        on_heads = []
        for h in range(GLA_HEADS):
            kc, vc = _kcols(h), _vcols(h)
            g = _GlaTerms(kc, sections["qk"][:, kc], sections["qk"][:, GLA_KEY_WIDTH:][:, kc], sections["v"][:, vc],
                          low_b, gkw_ref, gkb_ref, masks)
            srows = slice(h * GLA_HEAD_V, (h + 1) * GLA_HEAD_V)
            scores = g.scores()
            for b in range(tm // ROW_TILE):
                part = slice(b * ROW_TILE, (b + 1) * ROW_TILE)
                scores_ref[part, h * ROW_TILE:(h + 1) * ROW_TILE] = scores[part, part]
            o_intra = _dot_nn(scores, g.v_b)
            state = state_ref[srows, :]
            o_rows = []
            for j in range(cpt):
                r = _chunk_rows(j)
                st_ref[j, srows, :] = state
                o_rows.append(o_intra[r] + _dot_nt(g.a_b[r], state.astype(BF16)))
                decay = jnp.exp(g.c_last[j])
                state = state * decay + _dot_tn(g.v_b[r], g.kd_b[r])
            state_ref[srows, :] = state
            o_head = jnp.concatenate(o_rows, axis=0)
            o_ref[:, vc] = o_head
            proj_ref[:, GLA_SAVED_Z + kc.start:GLA_SAVED_Z + kc.stop] = g.z
            proj_ref[:, GLA_SAVED_C + kc.start:GLA_SAVED_C + kc.stop] = g.c
            on_heads.append(_rms(o_head)[0])
        gate = sections["gate"]
        on = jnp.concatenate(on_heads, axis=1) * hw_ref[...]
        y = (on * (gate * _sigmoid(gate))).astype(BF16)
        h2 = ht + _dot_nn(y, wout_ref[...])
        xhat2, rstd2 = _rms(h2)
        wf_row = wf_ref[...]
        err = xhat2 * wf_row - tgt_ref[...]
        loss_ref[...] += 0.5 * jnp.sum(err * err) / D_MODEL
        dout = err * (1.0 / D_MODEL)
        dwf_ref[...] += jnp.sum(dout * xhat2, axis=0, keepdims=True)
        dh2_ref[...] = _rms_bwd(dout * wf_row, xhat2, rstd2)

    row = lambda i: (i, 0)
    return pl.pallas_call(
        body, name="gla_fwd", grid=(nt,),
        in_specs=[pl.BlockSpec((tm, D_MODEL), row), _const((1, D_MODEL)), _const((GLA_IN_WIDTH, D_MODEL)),
                  _const((GLA_LOW_PAD, GLA_KEY_WIDTH)), _const((1, GLA_KEY_WIDTH)), _const((1, GLA_VALUE_WIDTH)),
                  _const((GLA_VALUE_WIDTH, D_MODEL)), _const((1, D_MODEL)), pl.BlockSpec((tm, D_MODEL), row)],
        out_specs=[pl.BlockSpec((tm, D_MODEL), row), pl.BlockSpec((tm, GLA_SAVED_WIDTH), row),
                   pl.BlockSpec((tm, GLA_VALUE_WIDTH), row),
                   pl.BlockSpec((cpt, GLA_VALUE_WIDTH, GLA_HEAD_K), lambda i: (i, 0, 0)),
                   pl.BlockSpec((tm, GLA_HEADS * ROW_TILE), row), _full((8, LANES)), _full((1, D_MODEL))],
        out_shape=[jax.ShapeDtypeStruct((seq, D_MODEL), F32), jax.ShapeDtypeStruct((seq, GLA_SAVED_WIDTH), F32),
                   jax.ShapeDtypeStruct((seq, GLA_VALUE_WIDTH), F32),
                   jax.ShapeDtypeStruct((n_chunks, GLA_VALUE_WIDTH, GLA_HEAD_K), F32),
                   jax.ShapeDtypeStruct((seq, GLA_HEADS * ROW_TILE), BF16),
                   jax.ShapeDtypeStruct((8, LANES), F32), jax.ShapeDtypeStruct((1, D_MODEL), F32)],
        scratch_shapes=[pltpu.VMEM((GLA_VALUE_WIDTH, GLA_HEAD_K), F32)],
        compiler_params=_params(),
    )(h1, nw, w_in, gkw, gkb, hw, w_out, wf, target)


def _gla_bwd_call(dh2, proj, o, states, scores, gkw, gkb, hw, w_out):
    seq = dh2.shape[0]
    tm = ROW_TILE
    nt = seq // tm
    cpt = tm // CHUNK

    def body(dh_ref, proj_ref, o_ref, st_ref, scores_ref, gkw_ref, gkb_ref, hw_ref, wout_ref,
             dproj_ref, dwout_hbm, dhw_ref, dgkw_ref, dgkb_ref, dstate_ref, dwout_acc, dwout_stage):
        i = pl.program_id(0)

        @pl.when(i == 0)
        def _():
            dstate_ref[...] = jnp.zeros_like(dstate_ref)
            dwout_acc[...] = jnp.zeros_like(dwout_acc)
            dhw_ref[...] = jnp.zeros_like(dhw_ref)
            dgkw_ref[...] = jnp.zeros_like(dgkw_ref)
            dgkb_ref[...] = jnp.zeros_like(dgkb_ref)

        dhb = dh_ref[...].astype(BF16)
        dy = _dot_nt(dhb, wout_ref[...])
        v0, g0 = 2 * GLA_KEY_WIDTH, GLA_QKVG_WIDTH - GLA_VALUE_WIDTH
        gate = proj_ref[:, g0:GLA_QKVG_WIDTH]
        low_b = proj_ref[:, GLA_QKVG_WIDTH:GLA_IN_PAD].astype(BF16)
        o = o_ref[...]
        hw_row = hw_ref[...]
        sg = _sigmoid(gate)
        silu = gate * sg
        don = dy * silu
        on_parts, do_parts, dhw_parts = [], [], []
        for h in range(GLA_HEADS):
            vc = _vcols(h)
            xh, rs = _rms(o[:, vc])
            on_parts.append(xh * hw_row[:, vc])
            dhw_parts.append(jnp.sum(don[:, vc] * xh, axis=0, keepdims=True))
            do_parts.append(_rms_bwd(don[:, vc] * hw_row[:, vc], xh, rs).astype(BF16))
        on = jnp.concatenate(on_parts, axis=1)
        dwout_acc[...] += _dot_tn((on * silu).astype(BF16), dhb)
        dhw_ref[...] += jnp.concatenate(dhw_parts, axis=1)
        dproj_ref[:, g0:GLA_QKVG_WIDTH] = (dy * on * (sg * (1.0 + gate * (1.0 - sg)))).astype(BF16)

        last_row = lax.broadcasted_iota(jnp.int32, (CHUNK, 1), 0) == CHUNK - 1
        g = _GlaTerms(slice(0, GLA_KEY_WIDTH), proj_ref[:, :GLA_KEY_WIDTH], proj_ref[:, GLA_KEY_WIDTH:v0],
                      proj_ref[:, v0:g0], low_b, gkw_ref, gkb_ref, _chunk_masks(tm),
                      saved=(proj_ref[:, GLA_SAVED_Z:GLA_SAVED_C], proj_ref[:, GLA_SAVED_C:GLA_SAVED_WIDTH]))
        dc_h = []
        for h in range(GLA_HEADS):
            kc, vc = _kcols(h), _vcols(h)
            k_cols = slice(GLA_KEY_WIDTH + kc.start, GLA_KEY_WIDTH + kc.stop)
            v_cols = slice(v0 + vc.start, v0 + vc.stop)
            do_h = do_parts[h]
            srows = slice(h * GLA_HEAD_V, (h + 1) * GLA_HEAD_V)
            scores = scores_ref[:, h * ROW_TILE:(h + 1) * ROW_TILE]
            dscores = _dot_nt(do_h, g.v_b[:, vc])
            dfwd = jnp.where(g.lower, dscores, 0.0).astype(BF16)
            dbwd = jnp.where(g.upper, dscores, 0.0).astype(BF16)
            dv_intra = _dot_tn(scores, do_h)
            da_intra = _dot_nn(dfwd, g.b_b[:, kc])
            db = _dot_tn(dfwd, g.a_b[:, kc])
            dcn = _dot_nn(dbwd, g.dp_b[:, kc])
            ddp = _dot_tn(dbwd, g.cn_b[:, kc])
            dstate = dstate_ref[srows, :]
            da_rows, dkd_rows, dv_rows, dcl_rows = [None] * cpt, [None] * cpt, [None] * cpt, [None] * cpt
            for j in reversed(range(cpt)):
                r = _chunk_rows(j)
                state = st_ref[j, srows, :]
                dstate_b = dstate.astype(BF16)
                do_c = do_h[r]
                dv_rows[j] = dv_intra[r] + _dot_nt(g.kd_b[r, kc], dstate_b)
                da_rows[j] = da_intra[r] + _dot_nn(do_c, state.astype(BF16))
                dkd = _dot_nn(g.v_b[r, vc], dstate_b) * g.e_rest[r, kc]
                dkd_rows[j] = dkd
                decay = jnp.exp(g.c_last[j][:, kc])
                dc_last = (jnp.sum(dkd * g.k[r, kc], axis=0, keepdims=True)
                           + decay * jnp.sum(state * dstate, axis=0, keepdims=True))
                dcl_rows[j] = jnp.where(last_row, dc_last, 0.0)
                dstate = _dot_tn(do_c, g.a_b[r, kc]) + dstate * decay
            dstate_ref[srows, :] = dstate
            da = jnp.concatenate(da_rows, axis=0)
            dkd = jnp.concatenate(dkd_rows, axis=0)
            dproj_ref[:, v_cols] = jnp.concatenate(dv_rows, axis=0).astype(BF16)
            q_up, q_down = da * g.e_pos[:, kc], dcn * g.e_neg[:, kc]
            k_up, k_down = ddp * g.e_pos[:, kc], db * g.e_neg[:, kc] + dkd
            dproj_ref[:, kc] = (Q_SCALE * (q_up + q_down)).astype(BF16)
            dproj_ref[:, k_cols] = (k_up + k_down).astype(BF16)
            dc_h.append(g.q[:, kc] * (q_up - q_down) + g.k[:, kc] * (k_up - k_down)
                        + jnp.concatenate(dcl_rows, axis=0))
        dz = _chunk_scan(jnp.concatenate(dc_h, axis=1), True) * (1.0 / GATE_NORMALIZER) * (1.0 - _sigmoid(g.z))
        dzb = dz.astype(BF16)
        dgkb_ref[...] += jnp.sum(dz, axis=0, keepdims=True)
        dgkw_ref[...] += _dot_tn(low_b, dzb)
        dproj_ref[:, GLA_QKVG_WIDTH:] = _dot_nt(dzb, gkw_ref[...]).astype(BF16)

        @pl.when(i == nt - 1)
        def _():
            dwout_stage[...] = dwout_acc[...].astype(BF16)
            pltpu.sync_copy(dwout_stage, dwout_hbm)

    rev = lambda i: (nt - 1 - i, 0)
    return pl.pallas_call(
        body, name="gla_bwd", grid=(nt,),
        in_specs=[pl.BlockSpec((tm, D_MODEL), rev), pl.BlockSpec((tm, GLA_SAVED_WIDTH), rev),
                  pl.BlockSpec((tm, GLA_VALUE_WIDTH), rev),
                  pl.BlockSpec((cpt, GLA_VALUE_WIDTH, GLA_HEAD_K), lambda i: (nt - 1 - i, 0, 0)),
                  pl.BlockSpec((tm, GLA_HEADS * ROW_TILE), rev),
                  _const((GLA_LOW_PAD, GLA_KEY_WIDTH)), _const((1, GLA_KEY_WIDTH)), _const((1, GLA_VALUE_WIDTH)),
                  _const((GLA_VALUE_WIDTH, D_MODEL))],
        out_specs=[pl.BlockSpec((tm, GLA_IN_PAD), rev), pl.BlockSpec(memory_space=pl.ANY),
                   _full((1, GLA_VALUE_WIDTH)), _full((GLA_LOW_PAD, GLA_KEY_WIDTH)), _full((1, GLA_KEY_WIDTH))],
        out_shape=[jax.ShapeDtypeStruct((seq, GLA_IN_PAD), BF16), jax.ShapeDtypeStruct((GLA_VALUE_WIDTH, D_MODEL), BF16),
                   jax.ShapeDtypeStruct((1, GLA_VALUE_WIDTH), F32), jax.ShapeDtypeStruct((GLA_LOW_PAD, GLA_KEY_WIDTH), F32),
                   jax.ShapeDtypeStruct((1, GLA_KEY_WIDTH), F32)],
        scratch_shapes=[pltpu.VMEM((GLA_VALUE_WIDTH, GLA_HEAD_K), F32), pltpu.VMEM((GLA_VALUE_WIDTH, D_MODEL), F32),
                        pltpu.VMEM((GLA_VALUE_WIDTH, D_MODEL), BF16)],
        compiler_params=_params(),
    )(dh2, proj, o, states, scores, gkw, gkb, hw, w_out)


def _position():
    return lax.axis_index("x"), lax.axis_index("y"), lax.axis_index("c")


def _lead_slot(ref, d):
    return ref.at[d]


def _row_slot(rows):
    return lambda ref, d: ref.at[pl.ds(pl.multiple_of(d * rows, rows), rows)]


def _dim1_slot(size):
    return lambda ref, d: ref.at[:, pl.ds(pl.multiple_of(d * size, size), size)]


class _Gather:
    def __init__(self, in_refs, out_refs, slots, send_sems, recv_sems, local_sems):
        self.in_refs, self.out_refs, self.slots = in_refs, out_refs, slots
        self.send_sems, self.recv_sems, self.local_sems = send_sems, recv_sems, local_sems
        self.n = len(in_refs)
        x, y, c = _position()
        self.c = c
        self.me, self.sibling = (x, y, c), (x, y, 1 - c)
        self.near = [(1 - x, y), (x, 1 - y)]
        self.diagonal = (1 - x, 1 - y)
        self.relay_from = (x ^ c, y ^ (1 - c))
        self.relay_to = (x ^ (1 - c), y ^ c)

    def _copy(self, a, k, block, to, from_input=False):
        part = self.slots[a](self.out_refs[a], 4 * block[0] + 2 * block[1] + block[2])
        return pltpu.make_async_remote_copy(
            src_ref=self.in_refs[a] if from_input else part, dst_ref=part,
            send_sem=self.send_sems.at[a, k], recv_sem=self.recv_sems.at[a, k], device_id=to, device_id_type=MESH)

    def _mine(self):
        return [pltpu.make_async_copy(self.in_refs[a], self.slots[a](self.out_refs[a], 4 * self.me[0] + 2 * self.me[1]
                                                                    + self.me[2]), self.local_sems.at[a])
                for a in range(self.n)]

    def _first(self):
        first = [self._copy(a, 0, self.me, self.sibling, True) for a in range(self.n)]
        return first + [self._copy(a, 1 + j, self.me, (*chip, self.c), True)
                        for j, chip in enumerate(self.near) for a in range(self.n)]

    def _relayed(self):
        return [self._copy(a, 3, (*self.relay_from, self.c), (*self.relay_to, self.c)) for a in range(self.n)]

    def _passed(self, j):
        chip = self.near[j] if j < 2 else self.diagonal
        return [self._copy(a, 4 + j, (*chip, self.c), self.sibling) for a in range(self.n)]

    def start(self):
        for cp in self._mine() + self._first():
            cp.start()

    def forward(self):
        for j, chip in enumerate(self.near):
            for a in range(self.n):
                self._copy(a, 1 + j, (*chip, self.c), self.me).wait_recv()
        for cp in self._relayed() + self._passed(0) + self._passed(1):
            cp.start()

    def relay(self):
        pass

    def finish(self):
        for a in range(self.n):
            self._copy(a, 3, (*self.diagonal, self.c), self.me).wait_recv()
        for cp in self._passed(2):
            cp.start()
        for a in range(self.n):
            self._copy(a, 0, self.sibling, self.me).wait_recv()
        for j, chip in enumerate(self.near + [self.diagonal]):
            for a in range(self.n):
                self._copy(a, 4 + j, (*chip, 1 - self.c), self.me).wait_recv()
        for cp in self._first() + self._relayed() + self._passed(0) + self._passed(1) + self._passed(2):
            cp.wait_send()
        for cp in self._mine():
            cp.wait()


class _Exchange:
    def __init__(self, in_refs, out_refs, slots, send_sems, recv_sems, local_sems):
        self.in_refs, self.out_refs, self.slots = in_refs, out_refs, slots
        self.send_sems, self.recv_sems, self.local_sems = send_sems, recv_sems, local_sems
        self.n = len(in_refs)
        self.pos = _position()

    def _copies(self):
        x, y, c = self.pos
        me = 4 * x + 2 * y + c
        mine = [pltpu.make_async_copy(self.slots[a](self.in_refs[a], me), self.out_refs[a].at[me],
                                      self.local_sems.at[a]) for a in range(self.n)]
        remote = []
        for k in range(1, N_DEV):
            px, py, pc = x ^ (k >> 2), y ^ ((k >> 1) & 1), c ^ (k & 1)
            for a in range(self.n):
                remote.append(pltpu.make_async_remote_copy(
                    src_ref=self.slots[a](self.in_refs[a], 4 * px + 2 * py + pc), dst_ref=self.out_refs[a].at[me],
                    send_sem=self.send_sems.at[a, k - 1], recv_sem=self.recv_sems.at[a, k - 1],
                    device_id=(px, py, pc), device_id_type=MESH))
        return mine, remote

    def start(self):
        mine, remote = self._copies()
        for cp in mine + remote:
            cp.start()

    def forward(self):
        pass

    def relay(self):
        pass

    def finish(self):
        mine, remote = self._copies()
        for cp in remote:
            cp.wait_recv()
        for cp in remote:
            cp.wait_send()
        for cp in mine:
            cp.wait()


class _Rider:
    def __init__(self, kind, arrays, out_shapes, slots, scratch=None, forward_step=None):
        self.kind, self.arrays, self.slots = kind, list(arrays), slots
        self.n = len(self.arrays)
        hbm = pl.BlockSpec(memory_space=pl.ANY)
        self.in_specs = [hbm] * self.n
        self.out_specs = [hbm] * self.n
        self.out_shape = [jax.ShapeDtypeStruct(tuple(s), a.dtype) for s, a in zip(out_shapes, self.arrays)]
        self.scratch = scratch if scratch is not None else [
            pltpu.SemaphoreType.DMA((self.n, 7)), pltpu.SemaphoreType.DMA((self.n, 7)),
            pltpu.SemaphoreType.DMA((self.n,))]
        self.forward_step = forward_step
        self.relay_step = None

    def bind(self, in_refs, out_refs, scratch):
        return self.kind(in_refs, out_refs, self.slots, *scratch)


def _gather_rider(shards, full_shapes, slots, forward_step=None):
    return _Rider(_Gather, shards, full_shapes, slots, None, forward_step)


def _exchange_rider(sends, part_shapes, slots):
    return _Rider(_Exchange, sends, [(N_DEV,) + tuple(s) for s in part_shapes], slots)


def _split_refs(refs, n_in, n_out, n_scratch, rider):
    k = rider.n if rider is not None else 0
    ins, r_ins = refs[:n_in], refs[n_in:n_in + k]
    outs, r_outs = refs[n_in + k:n_in + k + n_out], refs[n_in + k + n_out:n_in + 2 * k + n_out]
    rest = refs[n_in + 2 * k + n_out:]
    scratch, r_scratch = rest[:n_scratch], rest[n_scratch:]
    comm = rider.bind(r_ins, r_outs, r_scratch) if rider is not None else None
    if comm is not None:
        comm.forward_step, comm.relay_step = rider.forward_step, rider.relay_step
    return ins + outs + scratch, comm


def _ride_before(comm, i, nt):
    if comm is not None:
        pl.when(i == 0)(comm.start)
        pl.when(i == (nt - 1 if comm.forward_step is None else min(comm.forward_step, nt - 1)))(comm.forward)
        pl.when(i == (nt - 1 if comm.relay_step is None else min(comm.relay_step, nt - 1)))(comm.relay)


def _ride_after(comm, i, nt):
    if comm is not None:
        pl.when(i == nt - 1)(comm.finish)


def _extend(specs, rider, field):
    return list(specs) + (getattr(rider, field) if rider is not None else [])


def _comm_call(name, rider):
    def body(*refs):
        _, comm = _split_refs(refs, 0, 0, 0, rider)
        comm.start()
        comm.forward()
        comm.relay()
        comm.finish()

    return pl.pallas_call(body, name=name, in_specs=rider.in_specs, out_specs=rider.out_specs,
                          out_shape=rider.out_shape, scratch_shapes=rider.scratch,
                          compiler_params=pltpu.CompilerParams(vmem_limit_bytes=VMEM_LIMIT))(*rider.arrays)


N_CHIPS = 4


class _TwoLevel:
    def __init__(self, in_refs, out_refs, slots, *scratch):
        self.in_refs, self.out_refs, self.slots = in_refs, out_refs, slots
        self.n = n = len(in_refs)
        self.own_bufs, self.recv_bufs, self.relay_bufs = scratch[:n], scratch[n:2 * n], scratch[2 * n:3 * n]
        self.swap_send, self.swap_recv, self.local_sems, self.chip_send, self.chip_recv = scratch[3 * n:]
        x, y, c = self.pos = _position()
        self.first = (x ^ (1 - c), y ^ c)
        self.second = (x ^ c, y ^ (1 - c))
        self.chip_index = lambda chip: 2 * chip[0] + chip[1]

    def _swap(self):
        x, y, c = self.pos
        return [pltpu.make_async_remote_copy(
            src_ref=self.slots[a](self.in_refs[a], 2 * q + 1 - c), dst_ref=self.recv_bufs[a].at[q],
            send_sem=self.swap_send.at[a, q], recv_sem=self.swap_recv.at[a, q],
            device_id=(x, y, 1 - c), device_id_type=MESH) for a in range(self.n) for q in range(N_CHIPS)]

    def _mine(self):
        c = self.pos[2]
        return [pltpu.make_async_copy(self.slots[a](self.in_refs[a], 2 * q + c), self.own_bufs[a].at[q],
                                      self.local_sems.at[a, q]) for a in range(self.n) for q in range(N_CHIPS)]

    def _to_chip(self, a, k, src, dst, chip):
        return pltpu.make_async_remote_copy(
            src_ref=src, dst_ref=dst, send_sem=self.chip_send.at[a, k], recv_sem=self.chip_recv.at[a, k],
            device_id=(*chip, self.pos[2]), device_id_type=MESH)

    def _first_wave(self):
        x, y, _ = self.pos
        diagonal = self.chip_index((1 - x, 1 - y))
        passed_on = [self._to_chip(a, 1, self.own_bufs[a].at[diagonal], self.relay_bufs[a], self.first)
                     for a in range(self.n)]
        return passed_on + [self._to_chip(a, 0, self.own_bufs[a].at[self.chip_index(self.first)],
                                          self.out_refs[a].at[1], self.first) for a in range(self.n)]

    def _second_wave(self):
        return [self._to_chip(a, 2, self.own_bufs[a].at[self.chip_index(self.second)], self.out_refs[a].at[2],
                              self.second) for a in range(self.n)]

    def _own(self):
        x, y, _ = self.pos
        return [pltpu.make_async_copy(self.own_bufs[a].at[2 * x + y], self.out_refs[a].at[0],
                                      self.local_sems.at[a, N_CHIPS]) for a in range(self.n)]

    def start(self):
        for cp in self._swap() + self._mine():
            cp.start()

    def forward(self):
        swap, mine = self._swap(), self._mine()
        for a in range(self.n):
            for q in range(N_CHIPS):
                mine[a * N_CHIPS + q].wait()
                swap[a * N_CHIPS + q].wait_recv()
                self.own_bufs[a][q] = (self.own_bufs[a][q].astype(F32)
                                       + self.recv_bufs[a][q].astype(F32)).astype(BF16)
        for cp in self._first_wave() + self._own():
            cp.start()

    def relay(self):
        second = self.chip_index(self.second)
        for a in range(self.n):
            self._to_chip(a, 1, self.relay_bufs[a], self.relay_bufs[a], self.first).wait_recv()
            self.own_bufs[a][second] = (self.own_bufs[a][second].astype(F32)
                                        + self.relay_bufs[a][...].astype(F32)).astype(BF16)
        for cp in self._second_wave():
            cp.start()

    def finish(self):
        for a in range(self.n):
            self._to_chip(a, 0, self.out_refs[a].at[1], self.out_refs[a].at[1], self.first).wait_recv()
            self._to_chip(a, 2, self.out_refs[a].at[2], self.out_refs[a].at[2], self.second).wait_recv()
        for cp in self._first_wave() + self._second_wave() + self._swap():
            cp.wait_send()
        for cp in self._own():
            cp.wait()


def _two_level_rider(sends, part_shapes, slots, forward_step=None, relay_step=None):
    n = len(sends)
    bufs = [pltpu.VMEM((N_CHIPS,) + tuple(s), a.dtype) for s, a in zip(part_shapes, sends)]
    relay_bufs = [pltpu.VMEM(tuple(s), a.dtype) for s, a in zip(part_shapes, sends)]
    scratch = bufs + bufs + relay_bufs + [
        pltpu.SemaphoreType.DMA((n, N_CHIPS)), pltpu.SemaphoreType.DMA((n, N_CHIPS)),
        pltpu.SemaphoreType.DMA((n, N_CHIPS + 1)), pltpu.SemaphoreType.DMA((n, 3)), pltpu.SemaphoreType.DMA((n, 3))]
    rider = _Rider(_TwoLevel, sends, [(3,) + tuple(s) for s in part_shapes], slots, scratch, forward_step)
    rider.relay_step = relay_step
    return rider


class _Joined:
    def __init__(self, first, second):
        self.first, self.second = first, second

    def start(self):
        self.first.start()
        self.second.start()

    def forward(self):
        self.first.forward()
        self.second.forward()

    def relay(self):
        self.first.relay()
        self.second.relay()

    def finish(self):
        self.first.finish()
        self.second.finish()


class _JoinedRider:
    def __init__(self, first, second):
        self.first, self.second = first, second
        self.n = first.n + second.n
        self.arrays = first.arrays + second.arrays
        self.in_specs = first.in_specs + second.in_specs
        self.out_specs = first.out_specs + second.out_specs
        self.out_shape = first.out_shape + second.out_shape
        self.scratch = first.scratch + second.scratch
        self.forward_step = first.forward_step
        self.relay_step = first.relay_step

    def bind(self, in_refs, out_refs, scratch):
        k, s = self.first.n, len(self.first.scratch)
        return _Joined(self.first.bind(in_refs[:k], out_refs[:k], scratch[:s]),
                       self.second.bind(in_refs[k:], out_refs[k:], scratch[s:]))


def _adamw(w, g, m, v):
    m = ADAM_B1 * m + (1.0 - ADAM_B1) * g
    v = ADAM_B2 * v + (1.0 - ADAM_B2) * (g * g)
    m_hat = m / (1.0 - ADAM_B1 ** ADAM_STEP)
    v_hat = v / (1.0 - ADAM_B2 ** ADAM_STEP)
    delta = -ADAM_LR * (m_hat / (jnp.sqrt(v_hat) + ADAM_EPS) + ADAM_WD * w)
    return delta, m, v


def _sum_parts(parts_ref, index=()):
    g = parts_ref[(0,) + index].astype(F32)
    for s in range(1, parts_ref.shape[0]):
        g = g + parts_ref[(s,) + index].astype(F32)
    return g


def _adamw_group_call(name, groups):
    k = len(groups)

    def body(*refs):
        ins, outs = refs[:4 * k], refs[4 * k:]
        for i in range(k):
            parts_ref, w_ref, m_ref, v_ref = ins[4 * i:4 * i + 4]
            g = _sum_parts(parts_ref)
            delta, m_new, v_new = _adamw(w_ref[...], g, m_ref[...], v_ref[...])
            for out_ref, value in zip(outs[4 * i:4 * i + 4], (g, delta, m_new, v_new)):
                out_ref[...] = value

    vmem = pl.BlockSpec(memory_space=pltpu.VMEM)
    res = pl.pallas_call(
        body, name=name, in_specs=[vmem] * (4 * k), out_specs=[vmem] * (4 * k),
        out_shape=[jax.ShapeDtypeStruct(grp[1].shape, F32) for grp in groups for _ in range(4)],
        compiler_params=pltpu.CompilerParams(vmem_limit_bytes=VMEM_LIMIT),
    )(*[a for grp in groups for a in grp])
    return [res[4 * i:4 * i + 4] for i in range(k)]


def _adamw_slabs_call(name, parts, w, m, v, rider=None):
    def main(parts_ref, w_ref, m_ref, v_ref, g_ref, delta_ref, m_out, v_out):
        g = _sum_parts(parts_ref)
        delta, m_new, v_new = _adamw(w_ref[...], g, m_ref[...], v_ref[...])
        g_ref[...] = g
        delta_ref[...] = delta
        m_out[...] = m_new
        v_out[...] = v_new

    def body(*refs):
        own, comm = _split_refs(refs, 4, 4, 0, rider)
        if comm is not None:
            comm.start()
        main(*own)
        if comm is not None:
            comm.forward()
            comm.relay()
            comm.finish()

    vmem = pl.BlockSpec(memory_space=pltpu.VMEM)
    return pl.pallas_call(
        body, name=name, in_specs=_extend([vmem] * 4, rider, "in_specs"),
        out_specs=_extend([vmem] * 4, rider, "out_specs"),
        out_shape=_extend([jax.ShapeDtypeStruct(w.shape, F32)] * 4, rider, "out_shape"),
        scratch_shapes=_extend([], rider, "scratch"),
        compiler_params=pltpu.CompilerParams(vmem_limit_bytes=VMEM_LIMIT),
    )(parts, w, m, v, *_extend([], rider, "arrays"))


WIDE_ROWS = 8
NARROW_ROWS = 40
NARROW_GKW_ROW = 8
NARROW_GKB_ROW = 24
NARROW_HW_ROW = 32
GROUP_SHARD = POOL_GROUP_DIM // N_DEV
KEY_SHARD = GLA_KEY_WIDTH // N_DEV
HEAD_V_SHARD = GLA_HEAD_V // N_DEV


def _small_adamw_call(wide, narrow, w, m, v):
    names = ("norm_w", "pool_scale", "final_norm_w", "pool_group_b", "gla_gk_w", "gla_gk_b", "gla_head_norm_w")
    where = {
        "norm_w": (0, slice(0, 2), slice(None)),
        "pool_scale": (0, slice(2, 3), slice(None)),
        "final_norm_w": (0, slice(3, 4), slice(None)),
        "pool_group_b": (1, slice(0, POOL_GROUPS), slice(0, GROUP_SHARD)),
        "gla_gk_w": (1, slice(NARROW_GKW_ROW, NARROW_GKW_ROW + GLA_GATE_RANK), slice(0, KEY_SHARD)),
        "gla_gk_b": (1, slice(NARROW_GKB_ROW, NARROW_GKB_ROW + 1), slice(0, KEY_SHARD)),
        "gla_head_norm_w": (1, slice(NARROW_HW_ROW, NARROW_HW_ROW + 1), slice(0, HEAD_V_SHARD)),
    }
    k = len(names)

    def body(*refs):
        parts = refs[0:2]
        w_refs, m_refs, v_refs = refs[2:2 + k], refs[2 + k:2 + 2 * k], refs[2 + 2 * k:2 + 3 * k]
        outs = refs[2 + 3 * k:]
        loss_ref = outs[0]
        loss_ref[...] = _sum_parts(parts[0], (slice(4, 5), slice(0, 1)))
        for i, name in enumerate(names):
            buf, rows, cols = where[name]
            g = _sum_parts(parts[buf], (rows, cols))
            delta, m_new, v_new = _adamw(w_refs[i][...], g, m_refs[i][...], v_refs[i][...])
            outs[1 + i][...] = g
            outs[1 + k + i][...] = delta
            outs[1 + 2 * k + i][...] = m_new
            outs[1 + 3 * k + i][...] = v_new

    vmem = pl.BlockSpec(memory_space=pltpu.VMEM)
    shapes = [jax.ShapeDtypeStruct(w[n].shape, F32) for n in names]
    res = pl.pallas_call(
        body, name="adamw_small", in_specs=[vmem] * (2 + 3 * k), out_specs=[vmem] * (1 + 4 * k),
        out_shape=[jax.ShapeDtypeStruct((1, 1), F32)] + shapes * 4,
    )(wide, narrow, *[w[n] for n in names], *[m[n] for n in names], *[v[n] for n in names])
    unzip = lambda j: dict(zip(names, res[1 + j * k:1 + (j + 1) * k]))
    return res[0], unzip(0), unzip(1), unzip(2), unzip(3)


def kernel(x, norm_w, pool_in_w, pool_group_w, pool_group_b, pool_scale, pool_out_w, gla_in_w, gla_gk_w, gla_gk_b, gla_head_norm_w, gla_out_w, final_norm_w, loss_target, m_norm_w, m_pool_in_w, m_pool_group_w, m_pool_group_b, m_pool_scale, m_pool_out_w, m_gla_in_w, m_gla_gk_w, m_gla_gk_b, m_gla_head_norm_w, m_gla_out_w, m_final_norm_w, v_norm_w, v_pool_in_w, v_pool_group_w, v_pool_group_b, v_pool_scale, v_pool_out_w, v_gla_in_w, v_gla_gk_w, v_gla_gk_b, v_gla_head_norm_w, v_gla_out_w, v_final_norm_w):
    w = dict(norm_w=norm_w, pool_in_w=pool_in_w, pool_group_w=pool_group_w, pool_group_b=pool_group_b,
             pool_scale=pool_scale, pool_out_w=pool_out_w, gla_in_w=gla_in_w, gla_gk_w=gla_gk_w, gla_gk_b=gla_gk_b,
             gla_head_norm_w=gla_head_norm_w, gla_out_w=gla_out_w, final_norm_w=final_norm_w)
    m = dict(norm_w=m_norm_w, pool_in_w=m_pool_in_w, pool_group_w=m_pool_group_w, pool_group_b=m_pool_group_b,
             pool_scale=m_pool_scale, pool_out_w=m_pool_out_w, gla_in_w=m_gla_in_w, gla_gk_w=m_gla_gk_w,
             gla_gk_b=m_gla_gk_b, gla_head_norm_w=m_gla_head_norm_w, gla_out_w=m_gla_out_w,
             final_norm_w=m_final_norm_w)
    v = dict(norm_w=v_norm_w, pool_in_w=v_pool_in_w, pool_group_w=v_pool_group_w, pool_group_b=v_pool_group_b,
             pool_scale=v_pool_scale, pool_out_w=v_pool_out_w, gla_in_w=v_gla_in_w, gla_gk_w=v_gla_gk_w,
             gla_gk_b=v_gla_gk_b, gla_head_norm_w=v_gla_head_norm_w, gla_out_w=v_gla_out_w,
             final_norm_w=v_final_norm_w)
    col_shard = GLA_IN_WIDTH // N_DEV
    row_shard = D_MODEL // N_DEV

    def lanes(a):
        return jnp.pad(a, [(0, 0)] * (a.ndim - 1) + [(0, LANES - a.shape[-1])])

    small_in = jnp.concatenate([lanes(pool_group_b[0]), lanes(gla_gk_b), lanes(gla_head_norm_w),
                                jnp.zeros((2, LANES), F32)], axis=0)
    in_cols = 2 * POOL_WIDTH // N_DEV
    pool_in, pool_gw, pool_out, small_all = _comm_call("pool_weights_all_gather", _gather_rider(
        [pool_in_w[0].astype(BF16), pool_group_w[0].astype(BF16), pool_out_w[0].astype(BF16), small_in],
        [(D_MODEL, 2 * POOL_WIDTH), (POOL_GROUPS, POOL_GROUP_DIM, POOL_GROUP_DIM), (POOL_WIDTH, D_MODEL),
         (N_DEV, 8, LANES)],
        [_dim1_slot(in_cols), _dim1_slot(GROUP_SHARD), _row_slot(row_shard), _lead_slot]))
    pool_gb = jnp.transpose(small_all[:, 0:POOL_GROUPS, :GROUP_SHARD], (1, 0, 2)).reshape(1, POOL_WIDTH)
    gla_gkb = small_all[:, POOL_GROUPS, :KEY_SHARD].reshape(1, GLA_KEY_WIDTH)
    gla_hw = jnp.tile(small_all[:, POOL_GROUPS + 1, :HEAD_V_SHARD].reshape(1, GLA_HEAD_V), (1, GLA_HEADS))
    nw0, nw1, wf = norm_w[0:1], norm_w[1:2], final_norm_w.reshape(1, D_MODEL)
    xs, target = x[0], loss_target[0]

    h1, pool_gate, pooled, mixed, gla_in_parts, gkw_parts, gla_out = _pool_fwd_call(
        xs, nw0, pool_in, pool_gw, pool_gb, pool_scale, pool_out, _gather_rider(
            [jnp.transpose(gla_in_w[0]).astype(BF16), gla_gk_w[0].astype(BF16), gla_out_w[0].astype(BF16)],
            [(N_DEV, col_shard, D_MODEL), (N_DEV, GLA_GATE_RANK, KEY_SHARD), (GLA_VALUE_WIDTH, D_MODEL)],
            [_lead_slot, _lead_slot, _row_slot(row_shard)], GATHER_RELAY_STEP))
    gla_in = gla_in_parts.reshape(GLA_IN_WIDTH, D_MODEL)
    gla_gkw = jnp.pad(jnp.transpose(gkw_parts, (1, 0, 2)).reshape(GLA_GATE_RANK, GLA_KEY_WIDTH),
                      ((0, GLA_LOW_PAD - GLA_GATE_RANK), (0, 0)))
    dh2, proj, o, states, scores, loss_part, dwf = _gla_fwd_call(h1, nw1, gla_in, gla_gkw, gla_gkb, gla_hw, gla_out,
                                                                 wf, target)

    dproj, d_gla_out, dhw, dgkw, dgkb = _gla_bwd_call(dh2, proj, o, states, scores, gla_gkw, gla_gkb, gla_hw,
                                                      gla_out)
    dh1, d_gla_in, dnw1, landed_gla_out = _inproj_bwd_call(
        "gla_in_bwd", dproj, h1, nw1, gla_in, dh2,
        _exchange_rider([d_gla_out], [(row_shard, D_MODEL)], [_row_slot(row_shard)]), transposed=True)
    slabs = col_shard * D_MODEL // (BF16_ROWS * LANES)
    gla_in_send = d_gla_in.reshape(N_DEV, slabs, BF16_ROWS, LANES)
    dp, d_pool_out, dgw, dgb, dsc, landed_gla_in = _pool_bwd_call(
        dh1, pool_gate, pooled, mixed, pool_gw, pool_scale, pool_out,
        _two_level_rider([gla_in_send], [(slabs, BF16_ROWS, LANES)], [_lead_slot], TWO_LEVEL_ADD_STEP,
                         TWO_LEVEL_RELAY_STEP))
    grad_x, d_pool_in, dnw0 = _inproj_bwd_call("pool_in_bwd", dp, xs, nw0, pool_in, dh1)

    wide = jnp.concatenate([
        dnw0, dnw1, dsc, dwf, jnp.pad(loss_part[0:1, 0:1], ((0, 0), (0, D_MODEL - 1))),
        jnp.zeros((WIDE_ROWS - 5, D_MODEL), F32)], axis=0)

    def rows8(a):
        return jnp.pad(lanes(a), ((0, 0), (0, -a.shape[1] % 8), (0, 0)))

    narrow = jnp.concatenate([
        rows8(jnp.transpose(dgb.reshape(POOL_GROUPS, N_DEV, GROUP_SHARD), (1, 0, 2))),
        rows8(jnp.transpose(dgkw[:GLA_GATE_RANK].reshape(GLA_GATE_RANK, N_DEV, KEY_SHARD), (1, 0, 2))),
        rows8(dgkb.reshape(N_DEV, 1, KEY_SHARD)),
        rows8(dhw.reshape(GLA_HEADS, GLA_HEAD_V).sum(axis=0).reshape(N_DEV, 1, HEAD_V_SHARD)),
    ], axis=1)
    last_exchange = _JoinedRider(
        _two_level_rider([d_pool_in, d_pool_out, dgw],
                         [(D_MODEL, in_cols), (row_shard, D_MODEL), (POOL_GROUPS, GROUP_SHARD, POOL_GROUP_DIM)],
                         [_dim1_slot(in_cols), _row_slot(row_shard), _dim1_slot(GROUP_SHARD)]),
        _exchange_rider([wide, narrow], [(WIDE_ROWS, D_MODEL), (NARROW_ROWS, LANES)],
                        [lambda ref, d: ref, _lead_slot]))

    res = {}
    as_slabs = lambda t: jnp.transpose(t[0]).reshape(slabs, BF16_ROWS, LANES)
    *outs, landed_pool_in, landed_pool_out, landed_gw, landed_wide, landed_narrow = _adamw_slabs_call(
        "adamw_gla_in_w", landed_gla_in, as_slabs(gla_in_w), as_slabs(m_gla_in_w), as_slabs(v_gla_in_w),
        last_exchange)
    res["gla_in_w"] = [jnp.transpose(t.reshape(col_shard, D_MODEL))[None] for t in outs]
    rest = [("pool_in_w", landed_pool_in, (D_MODEL, in_cols)),
            ("pool_group_w", landed_gw, (POOL_GROUPS * GROUP_SHARD, POOL_GROUP_DIM)),
            ("pool_out_w", landed_pool_out, (row_shard, D_MODEL)), ("gla_out_w", landed_gla_out, (row_shard, D_MODEL))]
    updates = _adamw_group_call("adamw_matrices", [
        (parts.reshape((parts.shape[0],) + shape), w[name].reshape(shape), m[name].reshape(shape),
         v[name].reshape(shape)) for name, parts, shape in rest])
    for (name, _, _), outs in zip(rest, updates):
        res[name] = [t.reshape(w[name].shape) for t in outs]
    small_shapes ={"norm_w": (2, D_MODEL), "pool_scale": (1, D_MODEL), "final_norm_w": (1, D_MODEL),
                    "pool_group_b": (POOL_GROUPS, GROUP_SHARD), "gla_gk_w": (GLA_GATE_RANK, KEY_SHARD),
                    "gla_gk_b": (1, KEY_SHARD), "gla_head_norm_w": (1, HEAD_V_SHARD)}
    as_small = lambda t: {n: t[n].reshape(s) for n, s in small_shapes.items()}
    loss, *small_outs = _small_adamw_call(landed_wide, landed_narrow, as_small(w), as_small(m), as_small(v))
    for name in small_shapes:
        res[name] = [t[name].reshape(w[name].shape) for t in small_outs]
    order = ("norm_w", "pool_in_w", "pool_group_w", "pool_group_b", "pool_scale", "pool_out_w", "gla_in_w",
             "gla_gk_w", "gla_gk_b", "gla_head_norm_w", "gla_out_w", "final_norm_w")
    return (loss.reshape(()), grad_x[None], *[res[n][0] for n in order], *[res[n][1] for n in order],
            *[res[n][2] for n in order], *[res[n][3] for n in order])
```

```python
import jax
import jax.numpy as jnp
from jax import lax
from jax.experimental import pallas as pl
from jax.experimental.pallas import tpu as pltpu

F32 = jnp.float32
BF16 = jnp.bfloat16
MESH = pl.DeviceIdType.MESH

N_DEV = 8
D_MODEL = 1024
POOL_WIDTH = 1024
POOL_GROUPS = 4
POOL_GROUP_DIM = 256
POOL_HALO = 16
GLA_HEADS = 4
GLA_HEAD_K = 128
GLA_HEAD_V = 256
GLA_KEY_WIDTH = 512
GLA_VALUE_WIDTH = 1024
GLA_GATE_RANK = 16
GLA_IN_WIDTH = 3088
GLA_IN_PAD = 3200
GLA_SAVED_Z = GLA_IN_PAD
GLA_SAVED_C = GLA_SAVED_Z + 512
GLA_SAVED_WIDTH = GLA_SAVED_C + 512
GLA_LOW_PAD = 128
GLA_QKVG_WIDTH = 3072
CHUNK = 64
GATE_NORMALIZER = 16.0
RMS_EPS = 1e-6
Q_SCALE = GLA_HEAD_K ** -0.5

ADAM_LR = 0.001
ADAM_B1 = 0.9
ADAM_B2 = 0.999
ADAM_EPS = 1e-08
ADAM_WD = 0.01
ADAM_STEP = 10

LANES = 128
BF16_ROWS = 16
VMEM_LIMIT = 56 * 1024 * 1024
ROW_TILE = 256
GLA_FWD_ROW_TILE = 512
MATMUL_ROW_TILE = 512
GATHER_RELAY_STEP = 5
TWO_LEVEL_ADD_STEP = 1
TWO_LEVEL_RELAY_STEP = 4


def _dot_nn(a, b):
    return lax.dot_general(a, b, (((1,), (0,)), ((), ())), preferred_element_type=F32)


def _dot_nt(a, b):
    return lax.dot_general(a, b, (((1,), (1,)), ((), ())), preferred_element_type=F32)


def _dot_tn(a, b):
    return lax.dot_general(a, b, (((0,), (0,)), ((), ())), preferred_element_type=F32)


def _rms(x):
    rstd = lax.rsqrt(jnp.mean(x * x, axis=-1, keepdims=True) + RMS_EPS)
    return x * rstd, rstd


def _rms_bwd(dxhat, xhat, rstd):
    return rstd * (dxhat - xhat * jnp.mean(dxhat * xhat, axis=-1, keepdims=True))


def _sigmoid(x):
    return 1.0 / (1.0 + jnp.exp(-x))


def _params(sem=("arbitrary",)):
    return pltpu.CompilerParams(dimension_semantics=sem, vmem_limit_bytes=VMEM_LIMIT)


def _full(shape):
    return pl.BlockSpec(shape, lambda i: (0,) * len(shape))


def _const(shape):
    return pl.BlockSpec(shape, lambda i: (0,) * len(shape), pipeline_mode=pl.Buffered(1))


def _window_sums(ext, forward):
    n = ext.shape[0]
    outs = []
    for g in range(POOL_GROUPS):
        s = ext[:, g * POOL_GROUP_DIM:(g + 1) * POOL_GROUP_DIM]
        for k in range(g + 1):
            shift = (1 << k) if forward else n - (1 << k)
            s = s + pltpu.roll(s, shift, axis=0)
        outs.append(s[:n - POOL_HALO])
    return outs


def _inv_count(row0, tm):
    row = row0 + lax.broadcasted_iota(jnp.int32, (tm, 1), 0)
    return [1.0 / jnp.minimum(row + 1, 2 << g).astype(F32) for g in range(POOL_GROUPS)]


def _pool_mix(u, u_prev, row0, gw_ref, gb):
    tm = u.shape[0]
    sums = _window_sums(jnp.concatenate([u, u_prev], axis=0), True)
    inv = _inv_count(row0, tm)
    pooled, mixed = [], []
    for g in range(POOL_GROUPS):
        ug = u[:, g * POOL_GROUP_DIM:(g + 1) * POOL_GROUP_DIM]
        pg = (sums[g] * inv[g] - ug).astype(BF16)
        pooled.append(pg)
        mixed.append(_dot_nn(pg, gw_ref[g]))
    return pooled, jnp.concatenate(mixed, axis=1) + gb


def _pool_fwd_call(x, nw, w_in, gw, gb, sc, w_out, rider=None):
    seq = x.shape[0]
    tm = min(MATMUL_ROW_TILE, seq)
    nt = seq // tm

    def main(x_ref, nw_ref, win_ref, gw_ref, gb_ref, sc_ref, wout_ref, h_ref, y_ref, silu_ref, dsilu_ref,
             pooled_ref, mixed_ref, halo_ref):
        i = pl.program_id(0)

        @pl.when(i == 0)
        def _():
            halo_ref[...] = jnp.zeros_like(halo_ref)

        xt = x_ref[...]
        xhat, _ = _rms(xt)
        n = (xhat * nw_ref[...]).astype(BF16)
        p = _dot_nn(n, win_ref[...])
        u = p[:, :POOL_WIDTH]
        gate = p[:, POOL_WIDTH:]
        sg = _sigmoid(gate)
        silu = gate * sg
        silu_ref[...] = silu
        dsilu_ref[...] = sg * (1.0 + gate * (1.0 - sg))
        pooled, mixed = _pool_mix(u, halo_ref[...], i * tm, gw_ref, gb_ref[...])
        pooled_ref[...] = jnp.concatenate(pooled, axis=1)
        mixed_ref[...] = mixed
        halo_ref[...] = u[tm - POOL_HALO:, :]
        y = (mixed * sc_ref[...] * silu).astype(BF16)
        y_ref[...] = y
        h_ref[...] = xt + _dot_nn(y, wout_ref[...])

    def body(*refs):
        own, comm = _split_refs(refs, 7, 6, 1, rider)
        _ride_before(comm, pl.program_id(0), nt)
        main(*own)
        _ride_after(comm, pl.program_id(0), nt)

    return pl.pallas_call(
        body, name="pool_fwd", grid=(nt,),
        in_specs=_extend([pl.BlockSpec((tm, D_MODEL), lambda i: (i, 0)), _const((1, D_MODEL)),
                          _const((D_MODEL, 2 * POOL_WIDTH)), _const((POOL_GROUPS, POOL_GROUP_DIM, POOL_GROUP_DIM)),
                          _const((1, POOL_WIDTH)), _const((1, POOL_WIDTH)), _const((POOL_WIDTH, D_MODEL))],
                         rider, "in_specs"),
        out_specs=_extend([pl.BlockSpec((tm, D_MODEL), lambda i: (i, 0))] * 6, rider, "out_specs"),
        out_shape=_extend([jax.ShapeDtypeStruct((seq, D_MODEL), F32), jax.ShapeDtypeStruct((seq, POOL_WIDTH), BF16),
                           jax.ShapeDtypeStruct((seq, POOL_WIDTH), F32), jax.ShapeDtypeStruct((seq, POOL_WIDTH), F32),
                           jax.ShapeDtypeStruct((seq, POOL_WIDTH), BF16),
                           jax.ShapeDtypeStruct((seq, POOL_WIDTH), F32)], rider, "out_shape"),
        scratch_shapes=_extend([pltpu.VMEM((POOL_HALO, POOL_WIDTH), F32)], rider, "scratch"),
        compiler_params=_params(),
    )(x, nw, w_in, gw, gb, sc, w_out, *_extend([], rider, "arrays"))


def _pool_bwd_call(dh, y, silu, dsilu, pooled, mixed, gw, sc, w_out, rider=None):
    seq = dh.shape[0]
    tm = min(MATMUL_ROW_TILE, seq)
    nt = seq // tm

    def main(dh_ref, y_ref, silu_ref, dsilu_ref, pooled_ref, mixed_ref, gw_ref, sc_ref, wout_ref,
             dp_ref, dwout_hbm, dgw_hbm, dgb_ref, dsc_ref, carry_ref, dwout_acc, dgw_acc, dwout_stage, dgw_stage):
        i = pl.program_id(0)
        t = nt - 1 - i

        @pl.when(i == 0)
        def _():
            carry_ref[...] = jnp.zeros_like(carry_ref)
            dwout_acc[...] = jnp.zeros_like(dwout_acc)
            dgw_acc[...] = jnp.zeros_like(dgw_acc)
            dgb_ref[...] = jnp.zeros_like(dgb_ref)
            dsc_ref[...] = jnp.zeros_like(dsc_ref)

        silu = silu_ref[...]
        pooled = [pooled_ref[:, g * POOL_GROUP_DIM:(g + 1) * POOL_GROUP_DIM] for g in range(POOL_GROUPS)]
        sc = sc_ref[...]
        dhb = dh_ref[...].astype(BF16)
        dwout_acc[...] += _dot_tn(y_ref[...], dhb)
        dy = _dot_nt(dhb, wout_ref[...])
        dmixed = dy * sc * silu
        dy_mixed = dy * mixed_ref[...]
        dsc_ref[...] += jnp.sum(dy_mixed * silu, axis=0, keepdims=True)
        dgate = dy_mixed * sc * dsilu_ref[...]
        dgb_ref[...] += jnp.sum(dmixed, axis=0, keepdims=True)
        inv = _inv_count(t * tm, tm)
        dpooled, scaled = [], []
        for g in range(POOL_GROUPS):
            dmg = dmixed[:, g * POOL_GROUP_DIM:(g + 1) * POOL_GROUP_DIM].astype(BF16)
            dgw_acc[g] += _dot_tn(pooled[g], dmg)
            dpg = _dot_nt(dmg, gw_ref[g])
            dpooled.append(dpg)
            scaled.append(dpg * inv[g])
        r = jnp.concatenate(scaled, axis=1)
        sums = _window_sums(jnp.concatenate([r, carry_ref[...]], axis=0), False)
        carry_ref[...] = r[:POOL_HALO, :]
        du = jnp.concatenate([sums[g] - dpooled[g] for g in range(POOL_GROUPS)], axis=1)
        dp_ref[...] = jnp.concatenate([du, dgate], axis=1).astype(BF16)

        @pl.when(i == nt - 1)
        def _():
            dwout_stage[...] = dwout_acc[...].astype(BF16)
            dgw_stage[...] = dgw_acc[...].astype(BF16)
            pltpu.sync_copy(dwout_stage, dwout_hbm)
            pltpu.sync_copy(dgw_stage, dgw_hbm)

    def body(*refs):
        own, comm = _split_refs(refs, 9, 5, 5, rider)
        _ride_before(comm, pl.program_id(0), nt)
        main(*own)
        _ride_after(comm, pl.program_id(0), nt)

    rev = lambda i: (nt - 1 - i, 0)
    return pl.pallas_call(
        body, name="pool_bwd", grid=(nt,),
        in_specs=_extend([pl.BlockSpec((tm, D_MODEL), rev)] * 6
                         + [_const((POOL_GROUPS, POOL_GROUP_DIM, POOL_GROUP_DIM)), _const((1, POOL_WIDTH)),
                            _const((POOL_WIDTH, D_MODEL))], rider, "in_specs"),
        out_specs=_extend([pl.BlockSpec((tm, 2 * POOL_WIDTH), rev), pl.BlockSpec(memory_space=pl.ANY),
                           pl.BlockSpec(memory_space=pl.ANY), _full((1, POOL_WIDTH)), _full((1, POOL_WIDTH))],
                          rider, "out_specs"),
        out_shape=_extend([jax.ShapeDtypeStruct((seq, 2 * POOL_WIDTH), BF16),
                           jax.ShapeDtypeStruct((POOL_WIDTH, D_MODEL), BF16),
                           jax.ShapeDtypeStruct((POOL_GROUPS, POOL_GROUP_DIM, POOL_GROUP_DIM), BF16),
                           jax.ShapeDtypeStruct((1, POOL_WIDTH), F32), jax.ShapeDtypeStruct((1, POOL_WIDTH), F32)],
                          rider, "out_shape"),
        scratch_shapes=_extend([pltpu.VMEM((POOL_HALO, POOL_WIDTH), F32), pltpu.VMEM((POOL_WIDTH, D_MODEL), F32),
                                pltpu.VMEM((POOL_GROUPS, POOL_GROUP_DIM, POOL_GROUP_DIM), F32),
                                pltpu.VMEM((POOL_WIDTH, D_MODEL), BF16),
                                pltpu.VMEM((POOL_GROUPS, POOL_GROUP_DIM, POOL_GROUP_DIM), BF16)], rider, "scratch"),
        compiler_params=_params(),
    )(dh, y, silu, dsilu, pooled, mixed, gw, sc, w_out, *_extend([], rider, "arrays"))


def _rows_then_zeros(ref, lo, hi, rows):
    part = ref[lo:hi, :]
    return jnp.concatenate([part, jnp.zeros((rows - (hi - lo), part.shape[1]), part.dtype)], axis=0)


def _inproj_bwd_call(name, dproj, h_in, nw, w_in, dres, rider=None, transposed=False):
    seq = h_in.shape[0]
    width = dproj.shape[1]
    w_shape = tuple(w_in.shape)
    acc_shape = (width, D_MODEL) if transposed else w_shape
    whole = w_shape[0] // LANES * LANES
    tm = min(MATMUL_ROW_TILE, seq)
    nt = seq // tm

    def main(dproj_ref, h_ref, nw_ref, win_ref, dres_ref, dh_ref, dw_hbm, dnw_ref, dw_acc, dw_stage):
        i = pl.program_id(0)

        @pl.when(i == 0)
        def _():
            dw_acc[...] = jnp.zeros_like(dw_acc)
            dnw_ref[...] = jnp.zeros_like(dnw_ref)

        dpb = dproj_ref[...]
        if transposed:
            dn = _dot_nn(dpb[:, :whole], win_ref[0:whole, :])
            if whole < w_shape[0]:
                dn = dn + _dot_nn(dpb[:, whole:], _rows_then_zeros(win_ref, whole, w_shape[0], width - whole))
        else:
            dn = _dot_nt(dpb, win_ref[...])
        xhat, rstd = _rms(h_ref[...])
        nw_row = nw_ref[...]
        n = (xhat * nw_row).astype(BF16)
        dw_acc[...] += _dot_tn(dpb, n) if transposed else _dot_tn(n, dpb)
        dnw_ref[...] += jnp.sum(dn * xhat, axis=0, keepdims=True)
        dh_ref[...] = _rms_bwd(dn * nw_row, xhat, rstd) + dres_ref[...]

        @pl.when(i == nt - 1)
        def _():
            dw_stage[...] = dw_acc[...].astype(BF16)
            pltpu.sync_copy(dw_stage.at[pl.ds(0, w_shape[0])], dw_hbm)

    def body(*refs):
        own, comm = _split_refs(refs, 5, 3, 2, rider)
        _ride_before(comm, pl.program_id(0), nt)
        main(*own)
        _ride_after(comm, pl.program_id(0), nt)

    row = lambda i: (i, 0)
    return pl.pallas_call(
        body, name=name, grid=(nt,),
        in_specs=_extend([pl.BlockSpec((tm, width), row), pl.BlockSpec((tm, D_MODEL), row), _const((1, D_MODEL)),
                          _const(w_shape), pl.BlockSpec((tm, D_MODEL), row)], rider, "in_specs"),
        out_specs=_extend([pl.BlockSpec((tm, D_MODEL), row), pl.BlockSpec(memory_space=pl.ANY),
                           _full((1, D_MODEL))], rider, "out_specs"),
        out_shape=_extend([jax.ShapeDtypeStruct((seq, D_MODEL), F32), jax.ShapeDtypeStruct(w_shape, BF16),
                           jax.ShapeDtypeStruct((1, D_MODEL), F32)], rider, "out_shape"),
        scratch_shapes=_extend([pltpu.VMEM(acc_shape, F32), pltpu.VMEM(acc_shape, BF16)], rider, "scratch"),
        compiler_params=_params(),
    )(dproj, h_in, nw, w_in, dres, *_extend([], rider, "arrays"))


def _chunk_scan(x, reverse):
    n = x.shape[0]
    pos = lax.broadcasted_iota(jnp.int32, (n, 1), 0) & (CHUNK - 1)
    k = 1
    while k < CHUNK:
        if reverse:
            x = x + jnp.where(pos < CHUNK - k, pltpu.roll(x, n - k, axis=0), 0.0)
        else:
            x = x + jnp.where(pos >= k, pltpu.roll(x, k, axis=0), 0.0)
        k *= 2
    return x


def _chunk_rows(j):
    return slice(j * CHUNK, (j + 1) * CHUNK)


def _kcols(h):
    return slice(h * GLA_HEAD_K, (h + 1) * GLA_HEAD_K)


def _vcols(h):
    return slice(h * GLA_HEAD_V, (h + 1) * GLA_HEAD_V)


def _chunk_masks(tm):
    idx_t = lax.broadcasted_iota(jnp.int32, (tm, tm), 0)
    idx_s = lax.broadcasted_iota(jnp.int32, (tm, tm), 1)
    same_chunk = (idx_t ^ idx_s) < CHUNK
    return same_chunk & (idx_t >= idx_s), same_chunk & (idx_t < idx_s)


class _GlaTerms:
    def __init__(self, kc, q, k, v, low_b, gkw_ref, gkb_ref, masks, saved=None):
        tm = q.shape[0]
        self.q = q * Q_SCALE
        self.k = k
        if saved is None:
            self.z = _dot_nn(low_b, gkw_ref[:, kc]) + gkb_ref[:, kc]
            log_g = (jnp.minimum(self.z, 0.0) - jnp.log(1.0 + jnp.exp(-jnp.abs(self.z)))) / GATE_NORMALIZER
            self.c = _chunk_scan(log_g, False)
        else:
            self.z, self.c = saved
        is_last = lax.broadcasted_iota(jnp.int32, (CHUNK, 1), 0) == CHUNK - 1
        self.c_last = [jnp.sum(jnp.where(is_last, self.c[_chunk_rows(j), :], 0.0), axis=0, keepdims=True)
                       for j in range(tm // CHUNK)]
        c_last_rows = jnp.concatenate([jnp.broadcast_to(r, (CHUNK, r.shape[1])) for r in self.c_last], axis=0)
        self.e_pos = jnp.exp(self.c)
        self.e_neg = jnp.exp(-self.c)
        self.e_rest = jnp.exp(c_last_rows - self.c)
        self.a_b = (self.q * self.e_pos).astype(BF16)
        self.b_b = (self.k * self.e_neg).astype(BF16)
        self.cn_b = (self.q * self.e_neg).astype(BF16)
        self.dp_b = (self.k * self.e_pos).astype(BF16)
        self.kd_b = (self.k * self.e_rest).astype(BF16)
        self.v_b = v.astype(BF16)
        self.lower, self.upper = masks

    def scores(self, kc=slice(None)):
        fwd = _dot_nt(self.a_b[:, kc], self.b_b[:, kc])
        bwd = _dot_nt(self.cn_b[:, kc], self.dp_b[:, kc])
        return jnp.where(self.lower, fwd, jnp.where(self.upper, bwd, 0.0)).astype(BF16)


def _gla_fwd_call(h1, nw, w_in, gkw, gkb, hw, w_out, wf, target):
    seq = h1.shape[0]
    tm = min(GLA_FWD_ROW_TILE, seq)
    nt = seq // tm
    cpt = tm // CHUNK
    n_chunks = seq // CHUNK

    def body(h_ref, nw_ref, win_ref, gkw_ref, gkb_ref, hw_ref, wout_ref, wf_ref, tgt_ref,
             dh2_ref, proj_ref, o_ref, st_ref, scores_ref, loss_ref, dwf_ref, state_ref):
        i = pl.program_id(0)

        @pl.when(i == 0)
        def _():
            state_ref[...] = jnp.zeros_like(state_ref)
            loss_ref[...] = jnp.zeros_like(loss_ref)
            dwf_ref[...] = jnp.zeros_like(dwf_ref)

        ht = h_ref[...]
        xhat, _ = _rms(ht)
        n = (xhat * nw_ref[...]).astype(BF16)
        sections = {}
        for name, lo, hi in (("low", GLA_QKVG_WIDTH, GLA_IN_PAD), ("qk", 0, 2 * GLA_KEY_WIDTH),
                             ("v", 2 * GLA_KEY_WIDTH, GLA_QKVG_WIDTH - GLA_VALUE_WIDTH),
                             ("gate", GLA_QKVG_WIDTH - GLA_VALUE_WIDTH, GLA_QKVG_WIDTH)):
            rows = (win_ref[lo:hi, :] if hi <= GLA_IN_WIDTH
                    else _rows_then_zeros(win_ref, lo, GLA_IN_WIDTH, hi - lo))
            sections[name] = _dot_nt(n, rows)
            proj_ref[:, lo:hi] = sections[name]
        low_b = sections["low"].astype(BF16)
        masks = _chunk_masks(tm)
        on_heads = []
        for h in range(GLA_HEADS):
            kc, vc = _kcols(h), _vcols(h)
            g = _GlaTerms(kc, sections["qk"][:, kc], sections["qk"][:, GLA_KEY_WIDTH:][:, kc], sections["v"][:, vc],
                          low_b, gkw_ref, gkb_ref, masks)
            srows = slice(h * GLA_HEAD_V, (h + 1) * GLA_HEAD_V)
            scores = g.scores()
            for b in range(tm // ROW_TILE):
                part = slice(b * ROW_TILE, (b + 1) * ROW_TILE)
                scores_ref[part, h * ROW_TILE:(h + 1) * ROW_TILE] = scores[part, part]
            o_intra = _dot_nn(scores, g.v_b)
            state = state_ref[srows, :]
            o_rows = []
            for j in range(cpt):
                r = _chunk_rows(j)
                st_ref[j, srows, :] = state
                o_rows.append(o_intra[r] + _dot_nt(g.a_b[r], state.astype(BF16)))
                decay = jnp.exp(g.c_last[j])
                state = state * decay + _dot_tn(g.v_b[r], g.kd_b[r])
            state_ref[srows, :] = state
            o_head = jnp.concatenate(o_rows, axis=0)
            o_ref[:, vc] = o_head
            proj_ref[:, GLA_SAVED_Z + kc.start:GLA_SAVED_Z + kc.stop] = g.z
            proj_ref[:, GLA_SAVED_C + kc.start:GLA_SAVED_C + kc.stop] = g.c
            on_heads.append(_rms(o_head)[0])
        gate = sections["gate"]
        on = jnp.concatenate(on_heads, axis=1) * hw_ref[...]
        y = (on * (gate * _sigmoid(gate))).astype(BF16)
        h2 = ht + _dot_nn(y, wout_ref[...])
        xhat2, rstd2 = _rms(h2)
        wf_row = wf_ref[...]
        err = xhat2 * wf_row - tgt_ref[...]
        loss_ref[...] += 0.5 * jnp.sum(err * err) / D_MODEL
        dout = err * (1.0 / D_MODEL)
        dwf_ref[...] += jnp.sum(dout * xhat2, axis=0, keepdims=True)
        dh2_ref[...] = _rms_bwd(dout * wf_row, xhat2, rstd2)

    row = lambda i: (i, 0)
    return pl.pallas_call(
        body, name="gla_fwd", grid=(nt,),
        in_specs=[pl.BlockSpec((tm, D_MODEL), row), _const((1, D_MODEL)), _const((GLA_IN_WIDTH, D_MODEL)),
                  _const((GLA_LOW_PAD, GLA_KEY_WIDTH)), _const((1, GLA_KEY_WIDTH)), _const((1, GLA_VALUE_WIDTH)),
                  _const((GLA_VALUE_WIDTH, D_MODEL)), _const((1, D_MODEL)), pl.BlockSpec((tm, D_MODEL), row)],
        out_specs=[pl.BlockSpec((tm, D_MODEL), row), pl.BlockSpec((tm, GLA_SAVED_WIDTH), row),
                   pl.BlockSpec((tm, GLA_VALUE_WIDTH), row),
                   pl.BlockSpec((cpt, GLA_VALUE_WIDTH, GLA_HEAD_K), lambda i: (i, 0, 0)),
                   pl.BlockSpec((tm, GLA_HEADS * ROW_TILE), row), _full((8, LANES)), _full((1, D_MODEL))],
        out_shape=[jax.ShapeDtypeStruct((seq, D_MODEL), F32), jax.ShapeDtypeStruct((seq, GLA_SAVED_WIDTH), F32),
                   jax.ShapeDtypeStruct((seq, GLA_VALUE_WIDTH), F32),
                   jax.ShapeDtypeStruct((n_chunks, GLA_VALUE_WIDTH, GLA_HEAD_K), F32),
                   jax.ShapeDtypeStruct((seq, GLA_HEADS * ROW_TILE), BF16),
                   jax.ShapeDtypeStruct((8, LANES), F32), jax.ShapeDtypeStruct((1, D_MODEL), F32)],
        scratch_shapes=[pltpu.VMEM((GLA_VALUE_WIDTH, GLA_HEAD_K), F32)],
        compiler_params=_params(),
    )(h1, nw, w_in, gkw, gkb, hw, w_out, wf, target)


def _gla_bwd_call(dh2, proj, o, states, scores, gkw, gkb, hw, w_out):
    seq = dh2.shape[0]
    tm = ROW_TILE
    nt = seq // tm
    cpt = tm // CHUNK

    def body(dh_ref, proj_ref, o_ref, st_ref, scores_ref, gkw_ref, gkb_ref, hw_ref, wout_ref,
             dproj_ref, dwout_hbm, dhw_ref, dgkw_ref, dgkb_ref, dstate_ref, dwout_acc, dwout_stage):
        i = pl.program_id(0)

        @pl.when(i == 0)
        def _():
            dstate_ref[...] = jnp.zeros_like(dstate_ref)
            dwout_acc[...] = jnp.zeros_like(dwout_acc)
            dhw_ref[...] = jnp.zeros_like(dhw_ref)
            dgkw_ref[...] = jnp.zeros_like(dgkw_ref)
            dgkb_ref[...] = jnp.zeros_like(dgkb_ref)

        dhb = dh_ref[...].astype(BF16)
        dy = _dot_nt(dhb, wout_ref[...])
        v0, g0 = 2 * GLA_KEY_WIDTH, GLA_QKVG_WIDTH - GLA_VALUE_WIDTH
        gate = proj_ref[:, g0:GLA_QKVG_WIDTH]
        low_b = proj_ref[:, GLA_QKVG_WIDTH:GLA_IN_PAD].astype(BF16)
        o = o_ref[...]
        hw_row = hw_ref[...]
        sg = _sigmoid(gate)
        silu = gate * sg
        don = dy * silu
        on_parts, do_parts, dhw_parts = [], [], []
        for h in range(GLA_HEADS):
            vc = _vcols(h)
            xh, rs = _rms(o[:, vc])
            on_parts.append(xh * hw_row[:, vc])
            dhw_parts.append(jnp.sum(don[:, vc] * xh, axis=0, keepdims=True))
            do_parts.append(_rms_bwd(don[:, vc] * hw_row[:, vc], xh, rs).astype(BF16))
        on = jnp.concatenate(on_parts, axis=1)
        dwout_acc[...] += _dot_tn((on * silu).astype(BF16), dhb)
        dhw_ref[...] += jnp.concatenate(dhw_parts, axis=1)
        dproj_ref[:, g0:GLA_QKVG_WIDTH] = (dy * on * (sg * (1.0 + gate * (1.0 - sg)))).astype(BF16)

        last_row = lax.broadcasted_iota(jnp.int32, (CHUNK, 1), 0) == CHUNK - 1
        g = _GlaTerms(slice(0, GLA_KEY_WIDTH), proj_ref[:, :GLA_KEY_WIDTH], proj_ref[:, GLA_KEY_WIDTH:v0],
                      proj_ref[:, v0:g0], low_b, gkw_ref, gkb_ref, _chunk_masks(tm),
                      saved=(proj_ref[:, GLA_SAVED_Z:GLA_SAVED_C], proj_ref[:, GLA_SAVED_C:GLA_SAVED_WIDTH]))
        dc_h = []
        for h in range(GLA_HEADS):
            kc, vc = _kcols(h), _vcols(h)
            k_cols = slice(GLA_KEY_WIDTH + kc.start, GLA_KEY_WIDTH + kc.stop)
            v_cols = slice(v0 + vc.start, v0 + vc.stop)
            do_h = do_parts[h]
            srows = slice(h * GLA_HEAD_V, (h + 1) * GLA_HEAD_V)
            scores = scores_ref[:, h * ROW_TILE:(h + 1) * ROW_TILE]
            dscores = _dot_nt(do_h, g.v_b[:, vc])
            dfwd = jnp.where(g.lower, dscores, 0.0).astype(BF16)
            dbwd = jnp.where(g.upper, dscores, 0.0).astype(BF16)
            dv_intra = _dot_tn(scores, do_h)
            da_intra = _dot_nn(dfwd, g.b_b[:, kc])
            db = _dot_tn(dfwd, g.a_b[:, kc])
            dcn = _dot_nn(dbwd, g.dp_b[:, kc])
            ddp = _dot_tn(dbwd, g.cn_b[:, kc])
            dstate = dstate_ref[srows, :]
            da_rows, dkd_rows, dv_rows, dcl_rows = [None] * cpt, [None] * cpt, [None] * cpt, [None] * cpt
            for j in reversed(range(cpt)):
                r = _chunk_rows(j)
                state = st_ref[j, srows, :]
                dstate_b = dstate.astype(BF16)
                do_c = do_h[r]
                dv_rows[j] = dv_intra[r] + _dot_nt(g.kd_b[r, kc], dstate_b)
                da_rows[j] = da_intra[r] + _dot_nn(do_c, state.astype(BF16))
                dkd = _dot_nn(g.v_b[r, vc], dstate_b) * g.e_rest[r, kc]
                dkd_rows[j] = dkd
                decay = jnp.exp(g.c_last[j][:, kc])
                dc_last = (jnp.sum(dkd * g.k[r, kc], axis=0, keepdims=True)
                           + decay * jnp.sum(state * dstate, axis=0, keepdims=True))
                dcl_rows[j] = jnp.where(last_row, dc_last, 0.0)
                dstate = _dot_tn(do_c, g.a_b[r, kc]) + dstate * decay
            dstate_ref[srows, :] = dstate
            da = jnp.concatenate(da_rows, axis=0)
            dkd = jnp.concatenate(dkd_rows, axis=0)
            dproj_ref[:, v_cols] = jnp.concatenate(dv_rows, axis=0).astype(BF16)
            q_up, q_down = da * g.e_pos[:, kc], dcn * g.e_neg[:, kc]
            k_up, k_down = ddp * g.e_pos[:, kc], db * g.e_neg[:, kc] + dkd
            dproj_ref[:, kc] = (Q_SCALE * (q_up + q_down)).astype(BF16)
            dproj_ref[:, k_cols] = (k_up + k_down).astype(BF16)
            dc_h.append(g.q[:, kc] * (q_up - q_down) + g.k[:, kc] * (k_up - k_down)
                        + jnp.concatenate(dcl_rows, axis=0))
        dz = _chunk_scan(jnp.concatenate(dc_h, axis=1), True) * (1.0 / GATE_NORMALIZER) * (1.0 - _sigmoid(g.z))
        dzb = dz.astype(BF16)
        dgkb_ref[...] += jnp.sum(dz, axis=0, keepdims=True)
        dgkw_ref[...] += _dot_tn(low_b, dzb)
        dproj_ref[:, GLA_QKVG_WIDTH:] = _dot_nt(dzb, gkw_ref[...]).astype(BF16)

        @pl.when(i == nt - 1)
        def _():
            dwout_stage[...] = dwout_acc[...].astype(BF16)
            pltpu.sync_copy(dwout_stage, dwout_hbm)

    rev = lambda i: (nt - 1 - i, 0)
    return pl.pallas_call(
        body, name="gla_bwd", grid=(nt,),
        in_specs=[pl.BlockSpec((tm, D_MODEL), rev), pl.BlockSpec((tm, GLA_SAVED_WIDTH), rev),
                  pl.BlockSpec((tm, GLA_VALUE_WIDTH), rev),
                  pl.BlockSpec((cpt, GLA_VALUE_WIDTH, GLA_HEAD_K), lambda i: (nt - 1 - i, 0, 0)),
                  pl.BlockSpec((tm, GLA_HEADS * ROW_TILE), rev),
                  _const((GLA_LOW_PAD, GLA_KEY_WIDTH)), _const((1, GLA_KEY_WIDTH)), _const((1, GLA_VALUE_WIDTH)),
                  _const((GLA_VALUE_WIDTH, D_MODEL))],
        out_specs=[pl.BlockSpec((tm, GLA_IN_PAD), rev), pl.BlockSpec(memory_space=pl.ANY),
                   _full((1, GLA_VALUE_WIDTH)), _full((GLA_LOW_PAD, GLA_KEY_WIDTH)), _full((1, GLA_KEY_WIDTH))],
        out_shape=[jax.ShapeDtypeStruct((seq, GLA_IN_PAD), BF16), jax.ShapeDtypeStruct((GLA_VALUE_WIDTH, D_MODEL), BF16),
                   jax.ShapeDtypeStruct((1, GLA_VALUE_WIDTH), F32), jax.ShapeDtypeStruct((GLA_LOW_PAD, GLA_KEY_WIDTH), F32),
                   jax.ShapeDtypeStruct((1, GLA_KEY_WIDTH), F32)],
        scratch_shapes=[pltpu.VMEM((GLA_VALUE_WIDTH, GLA_HEAD_K), F32), pltpu.VMEM((GLA_VALUE_WIDTH, D_MODEL), F32),
                        pltpu.VMEM((GLA_VALUE_WIDTH, D_MODEL), BF16)],
        compiler_params=_params(),
    )(dh2, proj, o, states, scores, gkw, gkb, hw, w_out)


def _position():
    return lax.axis_index("x"), lax.axis_index("y"), lax.axis_index("c")


def _lead_slot(ref, d):
    return ref.at[d]


def _row_slot(rows):
    return lambda ref, d: ref.at[pl.ds(pl.multiple_of(d * rows, rows), rows)]


def _dim1_slot(size):
    return lambda ref, d: ref.at[:, pl.ds(pl.multiple_of(d * size, size), size)]


class _Gather:
    def __init__(self, in_refs, out_refs, slots, send_sems, recv_sems, local_sems):
        self.in_refs, self.out_refs, self.slots = in_refs, out_refs, slots
        self.send_sems, self.recv_sems, self.local_sems = send_sems, recv_sems, local_sems
        self.n = len(in_refs)
        x, y, c = _position()
        self.c = c
        self.me, self.sibling = (x, y, c), (x, y, 1 - c)
        self.near = [(1 - x, y), (x, 1 - y)]
        self.diagonal = (1 - x, 1 - y)
        self.relay_from = (x ^ c, y ^ (1 - c))
        self.relay_to = (x ^ (1 - c), y ^ c)

    def _copy(self, a, k, block, to, from_input=False):
        part = self.slots[a](self.out_refs[a], 4 * block[0] + 2 * block[1] + block[2])
        return pltpu.make_async_remote_copy(
            src_ref=self.in_refs[a] if from_input else part, dst_ref=part,
            send_sem=self.send_sems.at[a, k], recv_sem=self.recv_sems.at[a, k], device_id=to, device_id_type=MESH)

    def _mine(self):
        return [pltpu.make_async_copy(self.in_refs[a], self.slots[a](self.out_refs[a], 4 * self.me[0] + 2 * self.me[1]
                                                                    + self.me[2]), self.local_sems.at[a])
                for a in range(self.n)]

    def _first(self):
        first = [self._copy(a, 0, self.me, self.sibling, True) for a in range(self.n)]
        return first + [self._copy(a, 1 + j, self.me, (*chip, self.c), True)
                        for j, chip in enumerate(self.near) for a in range(self.n)]

    def _relayed(self):
        return [self._copy(a, 3, (*self.relay_from, self.c), (*self.relay_to, self.c)) for a in range(self.n)]

    def _passed(self, j):
        chip = self.near[j] if j < 2 else self.diagonal
        return [self._copy(a, 4 + j, (*chip, self.c), self.sibling) for a in range(self.n)]

    def start(self):
        for cp in self._mine() + self._first():
            cp.start()

    def forward(self):
        for j, chip in enumerate(self.near):
            for a in range(self.n):
                self._copy(a, 1 + j, (*chip, self.c), self.me).wait_recv()
        for cp in self._relayed() + self._passed(0) + self._passed(1):
            cp.start()

    def relay(self):
        pass

    def finish(self):
        for a in range(self.n):
            self._copy(a, 3, (*self.diagonal, self.c), self.me).wait_recv()
        for cp in self._passed(2):
            cp.start()
        for a in range(self.n):
            self._copy(a, 0, self.sibling, self.me).wait_recv()
        for j, chip in enumerate(self.near + [self.diagonal]):
            for a in range(self.n):
                self._copy(a, 4 + j, (*chip, 1 - self.c), self.me).wait_recv()
        for cp in self._first() + self._relayed() + self._passed(0) + self._passed(1) + self._passed(2):
            cp.wait_send()
        for cp in self._mine():
            cp.wait()


class _Exchange:
    def __init__(self, in_refs, out_refs, slots, send_sems, recv_sems, local_sems):
        self.in_refs, self.out_refs, self.slots = in_refs, out_refs, slots
        self.send_sems, self.recv_sems, self.local_sems = send_sems, recv_sems, local_sems
        self.n = len(in_refs)
        self.pos = _position()

    def _copies(self):
        x, y, c = self.pos
        me = 4 * x + 2 * y + c
        mine = [pltpu.make_async_copy(self.slots[a](self.in_refs[a], me), self.out_refs[a].at[me],
                                      self.local_sems.at[a]) for a in range(self.n)]
        remote = []
        for k in range(1, N_DEV):
            px, py, pc = x ^ (k >> 2), y ^ ((k >> 1) & 1), c ^ (k & 1)
            for a in range(self.n):
                remote.append(pltpu.make_async_remote_copy(
                    src_ref=self.slots[a](self.in_refs[a], 4 * px + 2 * py + pc), dst_ref=self.out_refs[a].at[me],
                    send_sem=self.send_sems.at[a, k - 1], recv_sem=self.recv_sems.at[a, k - 1],
                    device_id=(px, py, pc), device_id_type=MESH))
        return mine, remote

    def start(self):
        mine, remote = self._copies()
        for cp in mine + remote:
            cp.start()

    def forward(self):
        pass

    def relay(self):
        pass

    def finish(self):
        mine, remote = self._copies()
        for cp in remote:
            cp.wait_recv()
        for cp in remote:
            cp.wait_send()
        for cp in mine:
            cp.wait()


class _Rider:
    def __init__(self, kind, arrays, out_shapes, slots, scratch=None, forward_step=None):
        self.kind, self.arrays, self.slots = kind, list(arrays), slots
        self.n = len(self.arrays)
        hbm = pl.BlockSpec(memory_space=pl.ANY)
        self.in_specs = [hbm] * self.n
        self.out_specs = [hbm] * self.n
        self.out_shape = [jax.ShapeDtypeStruct(tuple(s), a.dtype) for s, a in zip(out_shapes, self.arrays)]
        self.scratch = scratch if scratch is not None else [
            pltpu.SemaphoreType.DMA((self.n, 7)), pltpu.SemaphoreType.DMA((self.n, 7)),
            pltpu.SemaphoreType.DMA((self.n,))]
        self.forward_step = forward_step
        self.relay_step = None

    def bind(self, in_refs, out_refs, scratch):
        return self.kind(in_refs, out_refs, self.slots, *scratch)


def _gather_rider(shards, full_shapes, slots, forward_step=None):
    return _Rider(_Gather, shards, full_shapes, slots, None, forward_step)


def _exchange_rider(sends, part_shapes, slots):
    return _Rider(_Exchange, sends, [(N_DEV,) + tuple(s) for s in part_shapes], slots)


def _split_refs(refs, n_in, n_out, n_scratch, rider):
    k = rider.n if rider is not None else 0
    ins, r_ins = refs[:n_in], refs[n_in:n_in + k]
    outs, r_outs = refs[n_in + k:n_in + k + n_out], refs[n_in + k + n_out:n_in + 2 * k + n_out]
    rest = refs[n_in + 2 * k + n_out:]
    scratch, r_scratch = rest[:n_scratch], rest[n_scratch:]
    comm = rider.bind(r_ins, r_outs, r_scratch) if rider is not None else None
    if comm is not None:
        comm.forward_step, comm.relay_step = rider.forward_step, rider.relay_step
    return ins + outs + scratch, comm


def _ride_before(comm, i, nt):
    if comm is not None:
        pl.when(i == 0)(comm.start)
        pl.when(i == (nt - 1 if comm.forward_step is None else min(comm.forward_step, nt - 1)))(comm.forward)
        pl.when(i == (nt - 1 if comm.relay_step is None else min(comm.relay_step, nt - 1)))(comm.relay)


def _ride_after(comm, i, nt):
    if comm is not None:
        pl.when(i == nt - 1)(comm.finish)


def _extend(specs, rider, field):
    return list(specs) + (getattr(rider, field) if rider is not None else [])


def _comm_call(name, rider):
    def body(*refs):
        _, comm = _split_refs(refs, 0, 0, 0, rider)
        comm.start()
        comm.forward()
        comm.relay()
        comm.finish()

    return pl.pallas_call(body, name=name, in_specs=rider.in_specs, out_specs=rider.out_specs,
                          out_shape=rider.out_shape, scratch_shapes=rider.scratch,
                          compiler_params=pltpu.CompilerParams(vmem_limit_bytes=VMEM_LIMIT))(*rider.arrays)


N_CHIPS = 4


class _TwoLevel:
    def __init__(self, in_refs, out_refs, slots, *scratch):
        self.in_refs, self.out_refs, self.slots = in_refs, out_refs, slots
        self.n = n = len(in_refs)
        self.own_bufs, self.recv_bufs, self.relay_bufs = scratch[:n], scratch[n:2 * n], scratch[2 * n:3 * n]
        self.swap_send, self.swap_recv, self.local_sems, self.chip_send, self.chip_recv = scratch[3 * n:]
        x, y, c = self.pos = _position()
        self.first = (x ^ (1 - c), y ^ c)
        self.second = (x ^ c, y ^ (1 - c))
        self.chip_index = lambda chip: 2 * chip[0] + chip[1]

    def _swap(self):
        x, y, c = self.pos
        return [pltpu.make_async_remote_copy(
            src_ref=self.slots[a](self.in_refs[a], 2 * q + 1 - c), dst_ref=self.recv_bufs[a].at[q],
            send_sem=self.swap_send.at[a, q], recv_sem=self.swap_recv.at[a, q],
            device_id=(x, y, 1 - c), device_id_type=MESH) for a in range(self.n) for q in range(N_CHIPS)]

    def _mine(self):
        c = self.pos[2]
        return [pltpu.make_async_copy(self.slots[a](self.in_refs[a], 2 * q + c), self.own_bufs[a].at[q],
                                      self.local_sems.at[a, q]) for a in range(self.n) for q in range(N_CHIPS)]

    def _to_chip(self, a, k, src, dst, chip):
        return pltpu.make_async_remote_copy(
            src_ref=src, dst_ref=dst, send_sem=self.chip_send.at[a, k], recv_sem=self.chip_recv.at[a, k],
            device_id=(*chip, self.pos[2]), device_id_type=MESH)

    def _first_wave(self):
        x, y, _ = self.pos
        diagonal = self.chip_index((1 - x, 1 - y))
        passed_on = [self._to_chip(a, 1, self.own_bufs[a].at[diagonal], self.relay_bufs[a], self.first)
                     for a in range(self.n)]
        return passed_on + [self._to_chip(a, 0, self.own_bufs[a].at[self.chip_index(self.first)],
                                          self.out_refs[a].at[1], self.first) for a in range(self.n)]

    def _second_wave(self):
        return [self._to_chip(a, 2, self.own_bufs[a].at[self.chip_index(self.second)], self.out_refs[a].at[2],
                              self.second) for a in range(self.n)]

    def _own(self):
        x, y, _ = self.pos
        return [pltpu.make_async_copy(self.own_bufs[a].at[2 * x + y], self.out_refs[a].at[0],
                                      self.local_sems.at[a, N_CHIPS]) for a in range(self.n)]

    def start(self):
        for cp in self._swap() + self._mine():
            cp.start()

    def forward(self):
        swap, mine = self._swap(), self._mine()
        for a in range(self.n):
            for q in range(N_CHIPS):
                mine[a * N_CHIPS + q].wait()
                swap[a * N_CHIPS + q].wait_recv()
                self.own_bufs[a][q] = (self.own_bufs[a][q].astype(F32)
                                       + self.recv_bufs[a][q].astype(F32)).astype(BF16)
        for cp in self._first_wave() + self._own():
            cp.start()

    def relay(self):
        second = self.chip_index(self.second)
        for a in range(self.n):
            self._to_chip(a, 1, self.relay_bufs[a], self.relay_bufs[a], self.first).wait_recv()
            self.own_bufs[a][second] = (self.own_bufs[a][second].astype(F32)
                                        + self.relay_bufs[a][...].astype(F32)).astype(BF16)
        for cp in self._second_wave():
            cp.start()

    def finish(self):
        for a in range(self.n):
            self._to_chip(a, 0, self.out_refs[a].at[1], self.out_refs[a].at[1], self.first).wait_recv()
            self._to_chip(a, 2, self.out_refs[a].at[2], self.out_refs[a].at[2], self.second).wait_recv()
        for cp in self._first_wave() + self._second_wave() + self._swap():
            cp.wait_send()
        for cp in self._own():
            cp.wait()


def _two_level_rider(sends, part_shapes, slots, forward_step=None, relay_step=None):
    n = len(sends)
    bufs = [pltpu.VMEM((N_CHIPS,) + tuple(s), a.dtype) for s, a in zip(part_shapes, sends)]
    relay_bufs = [pltpu.VMEM(tuple(s), a.dtype) for s, a in zip(part_shapes, sends)]
    scratch = bufs + bufs + relay_bufs + [
        pltpu.SemaphoreType.DMA((n, N_CHIPS)), pltpu.SemaphoreType.DMA((n, N_CHIPS)),
        pltpu.SemaphoreType.DMA((n, N_CHIPS + 1)), pltpu.SemaphoreType.DMA((n, 3)), pltpu.SemaphoreType.DMA((n, 3))]
    rider = _Rider(_TwoLevel, sends, [(3,) + tuple(s) for s in part_shapes], slots, scratch, forward_step)
    rider.relay_step = relay_step
    return rider


class _Joined:
    def __init__(self, first, second):
        self.first, self.second = first, second

    def start(self):
        self.first.start()
        self.second.start()

    def forward(self):
        self.first.forward()
        self.second.forward()

    def relay(self):
        self.first.relay()
        self.second.relay()

    def finish(self):
        self.first.finish()
        self.second.finish()


class _JoinedRider:
    def __init__(self, first, second):
        self.first, self.second = first, second
        self.n = first.n + second.n
        self.arrays = first.arrays + second.arrays
        self.in_specs = first.in_specs + second.in_specs
        self.out_specs = first.out_specs + second.out_specs
        self.out_shape = first.out_shape + second.out_shape
        self.scratch = first.scratch + second.scratch
        self.forward_step = first.forward_step
        self.relay_step = first.relay_step

    def bind(self, in_refs, out_refs, scratch):
        k, s = self.first.n, len(self.first.scratch)
        return _Joined(self.first.bind(in_refs[:k], out_refs[:k], scratch[:s]),
                       self.second.bind(in_refs[k:], out_refs[k:], scratch[s:]))


def _adamw(w, g, m, v):
    m = ADAM_B1 * m + (1.0 - ADAM_B1) * g
    v = ADAM_B2 * v + (1.0 - ADAM_B2) * (g * g)
    m_hat = m / (1.0 - ADAM_B1 ** ADAM_STEP)
    v_hat = v / (1.0 - ADAM_B2 ** ADAM_STEP)
    delta = -ADAM_LR * (m_hat / (jnp.sqrt(v_hat) + ADAM_EPS) + ADAM_WD * w)
    return delta, m, v


def _sum_parts(parts_ref, index=()):
    g = parts_ref[(0,) + index].astype(F32)
    for s in range(1, parts_ref.shape[0]):
        g = g + parts_ref[(s,) + index].astype(F32)
    return g


def _adamw_group_call(name, groups):
    k = len(groups)

    def body(*refs):
        ins, outs = refs[:4 * k], refs[4 * k:]
        for i in range(k):
            parts_ref, w_ref, m_ref, v_ref = ins[4 * i:4 * i + 4]
            g = _sum_parts(parts_ref)
            delta, m_new, v_new = _adamw(w_ref[...], g, m_ref[...], v_ref[...])
            for out_ref, value in zip(outs[4 * i:4 * i + 4], (g, delta, m_new, v_new)):
                out_ref[...] = value

    vmem = pl.BlockSpec(memory_space=pltpu.VMEM)
    res = pl.pallas_call(
        body, name=name, in_specs=[vmem] * (4 * k), out_specs=[vmem] * (4 * k),
        out_shape=[jax.ShapeDtypeStruct(grp[1].shape, F32) for grp in groups for _ in range(4)],
        compiler_params=pltpu.CompilerParams(vmem_limit_bytes=VMEM_LIMIT),
    )(*[a for grp in groups for a in grp])
    return [res[4 * i:4 * i + 4] for i in range(k)]


def _adamw_slabs_call(name, parts, w, m, v, rider=None):
    def main(parts_ref, w_ref, m_ref, v_ref, g_ref, delta_ref, m_out, v_out):
        g = _sum_parts(parts_ref)
        delta, m_new, v_new = _adamw(w_ref[...], g, m_ref[...], v_ref[...])
        g_ref[...] = g
        delta_ref[...] = delta
        m_out[...] = m_new
        v_out[...] = v_new

    def body(*refs):
        own, comm = _split_refs(refs, 4, 4, 0, rider)
        if comm is not None:
            comm.start()
        main(*own)
        if comm is not None:
            comm.forward()
            comm.relay()
            comm.finish()

    vmem = pl.BlockSpec(memory_space=pltpu.VMEM)
    return pl.pallas_call(
        body, name=name, in_specs=_extend([vmem] * 4, rider, "in_specs"),
        out_specs=_extend([vmem] * 4, rider, "out_specs"),
        out_shape=_extend([jax.ShapeDtypeStruct(w.shape, F32)] * 4, rider, "out_shape"),
        scratch_shapes=_extend([], rider, "scratch"),
        compiler_params=pltpu.CompilerParams(vmem_limit_bytes=VMEM_LIMIT),
    )(parts, w, m, v, *_extend([], rider, "arrays"))


WIDE_ROWS = 8
NARROW_ROWS = 40
NARROW_GKW_ROW = 8
NARROW_GKB_ROW = 24
NARROW_HW_ROW = 32
GROUP_SHARD = POOL_GROUP_DIM // N_DEV
KEY_SHARD = GLA_KEY_WIDTH // N_DEV
HEAD_V_SHARD = GLA_HEAD_V // N_DEV


def _small_adamw_call(wide, narrow, w, m, v):
    names = ("norm_w", "pool_scale", "final_norm_w", "pool_group_b", "gla_gk_w", "gla_gk_b", "gla_head_norm_w")
    where = {
        "norm_w": (0, slice(0, 2), slice(None)),
        "pool_scale": (0, slice(2, 3), slice(None)),
        "final_norm_w": (0, slice(3, 4), slice(None)),
        "pool_group_b": (1, slice(0, POOL_GROUPS), slice(0, GROUP_SHARD)),
        "gla_gk_w": (1, slice(NARROW_GKW_ROW, NARROW_GKW_ROW + GLA_GATE_RANK), slice(0, KEY_SHARD)),
        "gla_gk_b": (1, slice(NARROW_GKB_ROW, NARROW_GKB_ROW + 1), slice(0, KEY_SHARD)),
        "gla_head_norm_w": (1, slice(NARROW_HW_ROW, NARROW_HW_ROW + 1), slice(0, HEAD_V_SHARD)),
    }
    k = len(names)

    def body(*refs):
        parts = refs[0:2]
        w_refs, m_refs, v_refs = refs[2:2 + k], refs[2 + k:2 + 2 * k], refs[2 + 2 * k:2 + 3 * k]
        outs = refs[2 + 3 * k:]
        loss_ref = outs[0]
        loss_ref[...] = _sum_parts(parts[0], (slice(4, 5), slice(0, 1)))
        for i, name in enumerate(names):
            buf, rows, cols = where[name]
            g = _sum_parts(parts[buf], (rows, cols))
            delta, m_new, v_new = _adamw(w_refs[i][...], g, m_refs[i][...], v_refs[i][...])
            outs[1 + i][...] = g
            outs[1 + k + i][...] = delta
            outs[1 + 2 * k + i][...] = m_new
            outs[1 + 3 * k + i][...] = v_new

    vmem = pl.BlockSpec(memory_space=pltpu.VMEM)
    shapes = [jax.ShapeDtypeStruct(w[n].shape, F32) for n in names]
    res = pl.pallas_call(
        body, name="adamw_small", in_specs=[vmem] * (2 + 3 * k), out_specs=[vmem] * (1 + 4 * k),
        out_shape=[jax.ShapeDtypeStruct((1, 1), F32)] + shapes * 4,
    )(wide, narrow, *[w[n] for n in names], *[m[n] for n in names], *[v[n] for n in names])
    unzip = lambda j: dict(zip(names, res[1 + j * k:1 + (j + 1) * k]))
    return res[0], unzip(0), unzip(1), unzip(2), unzip(3)


def kernel(x, norm_w, pool_in_w, pool_group_w, pool_group_b, pool_scale, pool_out_w, gla_in_w, gla_gk_w, gla_gk_b, gla_head_norm_w, gla_out_w, final_norm_w, loss_target, m_norm_w, m_pool_in_w, m_pool_group_w, m_pool_group_b, m_pool_scale, m_pool_out_w, m_gla_in_w, m_gla_gk_w, m_gla_gk_b, m_gla_head_norm_w, m_gla_out_w, m_final_norm_w, v_norm_w, v_pool_in_w, v_pool_group_w, v_pool_group_b, v_pool_scale, v_pool_out_w, v_gla_in_w, v_gla_gk_w, v_gla_gk_b, v_gla_head_norm_w, v_gla_out_w, v_final_norm_w):
    w = dict(norm_w=norm_w, pool_in_w=pool_in_w, pool_group_w=pool_group_w, pool_group_b=pool_group_b,
             pool_scale=pool_scale, pool_out_w=pool_out_w, gla_in_w=gla_in_w, gla_gk_w=gla_gk_w, gla_gk_b=gla_gk_b,
             gla_head_norm_w=gla_head_norm_w, gla_out_w=gla_out_w, final_norm_w=final_norm_w)
    m = dict(norm_w=m_norm_w, pool_in_w=m_pool_in_w, pool_group_w=m_pool_group_w, pool_group_b=m_pool_group_b,
             pool_scale=m_pool_scale, pool_out_w=m_pool_out_w, gla_in_w=m_gla_in_w, gla_gk_w=m_gla_gk_w,
             gla_gk_b=m_gla_gk_b, gla_head_norm_w=m_gla_head_norm_w, gla_out_w=m_gla_out_w,
             final_norm_w=m_final_norm_w)
    v = dict(norm_w=v_norm_w, pool_in_w=v_pool_in_w, pool_group_w=v_pool_group_w, pool_group_b=v_pool_group_b,
             pool_scale=v_pool_scale, pool_out_w=v_pool_out_w, gla_in_w=v_gla_in_w, gla_gk_w=v_gla_gk_w,
             gla_gk_b=v_gla_gk_b, gla_head_norm_w=v_gla_head_norm_w, gla_out_w=v_gla_out_w,
             final_norm_w=v_final_norm_w)
    col_shard = GLA_IN_WIDTH // N_DEV
    row_shard = D_MODEL // N_DEV

    def lanes(a):
        return jnp.pad(a, [(0, 0)] * (a.ndim - 1) + [(0, LANES - a.shape[-1])])

    small_in = jnp.concatenate([lanes(pool_group_b[0]), lanes(gla_gk_b), lanes(gla_head_norm_w),
                                jnp.zeros((2, LANES), F32)], axis=0)
    in_cols = 2 * POOL_WIDTH // N_DEV
    pool_in, pool_gw, pool_out, small_all = _comm_call("pool_weights_all_gather", _gather_rider(
        [pool_in_w[0].astype(BF16), pool_group_w[0].astype(BF16), pool_out_w[0].astype(BF16), small_in],
        [(D_MODEL, 2 * POOL_WIDTH), (POOL_GROUPS, POOL_GROUP_DIM, POOL_GROUP_DIM), (POOL_WIDTH, D_MODEL),
         (N_DEV, 8, LANES)],
        [_dim1_slot(in_cols), _dim1_slot(GROUP_SHARD), _row_slot(row_shard), _lead_slot]))
    pool_gb = jnp.transpose(small_all[:, 0:POOL_GROUPS, :GROUP_SHARD], (1, 0, 2)).reshape(1, POOL_WIDTH)
    gla_gkb = small_all[:, POOL_GROUPS, :KEY_SHARD].reshape(1, GLA_KEY_WIDTH)
    gla_hw = jnp.tile(small_all[:, POOL_GROUPS + 1, :HEAD_V_SHARD].reshape(1, GLA_HEAD_V), (1, GLA_HEADS))
    nw0, nw1, wf = norm_w[0:1], norm_w[1:2], final_norm_w.reshape(1, D_MODEL)
    xs, target = x[0], loss_target[0]

    h1, pool_y, pool_silu, pool_dsilu, pooled, mixed, gla_in_parts, gkw_parts, gla_out = _pool_fwd_call(
        xs, nw0, pool_in, pool_gw, pool_gb, pool_scale, pool_out, _gather_rider(
            [jnp.transpose(gla_in_w[0]).astype(BF16), gla_gk_w[0].astype(BF16), gla_out_w[0].astype(BF16)],
            [(N_DEV, col_shard, D_MODEL), (N_DEV, GLA_GATE_RANK, KEY_SHARD), (GLA_VALUE_WIDTH, D_MODEL)],
            [_lead_slot, _lead_slot, _row_slot(row_shard)], GATHER_RELAY_STEP))
    gla_in = gla_in_parts.reshape(GLA_IN_WIDTH, D_MODEL)
    gla_gkw = jnp.pad(jnp.transpose(gkw_parts, (1, 0, 2)).reshape(GLA_GATE_RANK, GLA_KEY_WIDTH),
                      ((0, GLA_LOW_PAD - GLA_GATE_RANK), (0, 0)))
    dh2, proj, o, states, scores, loss_part, dwf = _gla_fwd_call(h1, nw1, gla_in, gla_gkw, gla_gkb, gla_hw, gla_out,
                                                                 wf, target)

    dproj, d_gla_out, dhw, dgkw, dgkb = _gla_bwd_call(dh2, proj, o, states, scores, gla_gkw, gla_gkb, gla_hw,
                                                      gla_out)
    dh1, d_gla_in, dnw1, landed_gla_out = _inproj_bwd_call(
        "gla_in_bwd", dproj, h1, nw1, gla_in, dh2,
        _exchange_rider([d_gla_out], [(row_shard, D_MODEL)], [_row_slot(row_shard)]), transposed=True)
    slabs = col_shard * D_MODEL // (BF16_ROWS * LANES)
    gla_in_send = d_gla_in.reshape(N_DEV, slabs, BF16_ROWS, LANES)
    dp, d_pool_out, dgw, dgb, dsc, landed_gla_in = _pool_bwd_call(
        dh1, pool_y, pool_silu, pool_dsilu, pooled, mixed, pool_gw, pool_scale, pool_out,
        _two_level_rider([gla_in_send], [(slabs, BF16_ROWS, LANES)], [_lead_slot], TWO_LEVEL_ADD_STEP,
                         TWO_LEVEL_RELAY_STEP))
    grad_x, d_pool_in, dnw0 = _inproj_bwd_call("pool_in_bwd", dp, xs, nw0, pool_in, dh1)

    wide = jnp.concatenate([
        dnw0, dnw1, dsc, dwf, jnp.pad(loss_part[0:1, 0:1], ((0, 0), (0, D_MODEL - 1))),
        jnp.zeros((WIDE_ROWS - 5, D_MODEL), F32)], axis=0)

    def rows8(a):
        return jnp.pad(lanes(a), ((0, 0), (0, -a.shape[1] % 8), (0, 0)))

    narrow = jnp.concatenate([
        rows8(jnp.transpose(dgb.reshape(POOL_GROUPS, N_DEV, GROUP_SHARD), (1, 0, 2))),
        rows8(jnp.transpose(dgkw[:GLA_GATE_RANK].reshape(GLA_GATE_RANK, N_DEV, KEY_SHARD), (1, 0, 2))),
        rows8(dgkb.reshape(N_DEV, 1, KEY_SHARD)),
        rows8(dhw.reshape(GLA_HEADS, GLA_HEAD_V).sum(axis=0).reshape(N_DEV, 1, HEAD_V_SHARD)),
    ], axis=1)
    last_exchange = _JoinedRider(
        _two_level_rider([d_pool_in, d_pool_out, dgw],
                         [(D_MODEL, in_cols), (row_shard, D_MODEL), (POOL_GROUPS, GROUP_SHARD, POOL_GROUP_DIM)],
                         [_dim1_slot(in_cols), _row_slot(row_shard), _dim1_slot(GROUP_SHARD)]),
        _exchange_rider([wide, narrow], [(WIDE_ROWS, D_MODEL), (NARROW_ROWS, LANES)],
                        [lambda ref, d: ref, _lead_slot]))

    res = {}
    as_slabs = lambda t: jnp.transpose(t[0]).reshape(slabs, BF16_ROWS, LANES)
    *outs, landed_pool_in, landed_pool_out, landed_gw, landed_wide, landed_narrow = _adamw_slabs_call(
        "adamw_gla_in_w", landed_gla_in, as_slabs(gla_in_w), as_slabs(m_gla_in_w), as_slabs(v_gla_in_w),
        last_exchange)
    res["gla_in_w"] = [jnp.transpose(t.reshape(col_shard, D_MODEL))[None] for t in outs]
    rest = [("pool_in_w", landed_pool_in, (D_MODEL, in_cols)),
            ("pool_group_w", landed_gw, (POOL_GROUPS * GROUP_SHARD, POOL_GROUP_DIM)),
            ("pool_out_w", landed_pool_out, (row_shard, D_MODEL)), ("gla_out_w", landed_gla_out, (row_shard, D_MODEL))]
    updates = _adamw_group_call("adamw_matrices", [
        (parts.reshape((parts.shape[0],) + shape), w[name].reshape(shape), m[name].reshape(shape),
         v[name].reshape(shape)) for name, parts, shape in rest])
    for (name, _, _), outs in zip(rest, updates):
        res[name] = [t.reshape(w[name].shape) for t in outs]
    small_shapes ={"norm_w": (2, D_MODEL), "pool_scale": (1, D_MODEL), "final_norm_w": (1, D_MODEL),
                    "pool_group_b": (POOL_GROUPS, GROUP_SHARD), "gla_gk_w": (GLA_GATE_RANK, KEY_SHARD),
                    "gla_gk_b": (1, KEY_SHARD), "gla_head_norm_w": (1, HEAD_V_SHARD)}
    as_small = lambda t: {n: t[n].reshape(s) for n, s in small_shapes.items()}
    loss, *small_outs = _small_adamw_call(landed_wide, landed_narrow, as_small(w), as_small(m), as_small(v))
    for name in small_shapes:
        res[name] = [t[name].reshape(w[name].shape) for t in small_outs]
    order = ("norm_w", "pool_in_w", "pool_group_w", "pool_group_b", "pool_scale", "pool_out_w", "gla_in_w",
             "gla_gk_w", "gla_gk_b", "gla_head_norm_w", "gla_out_w", "final_norm_w")
    return (loss.reshape(()), grad_x[None], *[res[n][0] for n in order], *[res[n][1] for n in order],
            *[res[n][2] for n in order], *[res[n][3] for n in order])
```

```python
import jax
import jax.numpy as jnp
from jax import lax
from jax.experimental import pallas as pl
from jax.experimental.pallas import tpu as pltpu

F32 = jnp.float32
BF16 = jnp.bfloat16
MESH = pl.DeviceIdType.MESH

N_DEV = 8
D_MODEL = 1024
POOL_WIDTH = 1024
POOL_GROUPS = 4
POOL_GROUP_DIM = 256
POOL_HALO = 16
GLA_HEADS = 4
GLA_HEAD_K = 128
GLA_HEAD_V = 256
GLA_KEY_WIDTH = 512
GLA_VALUE_WIDTH = 1024
GLA_GATE_RANK = 16
GLA_IN_WIDTH = 3088
GLA_IN_PAD = 3200
GLA_SAVED_Z = GLA_IN_PAD
GLA_SAVED_C = GLA_SAVED_Z + 512
GLA_SAVED_WIDTH = GLA_SAVED_C + 512
GLA_LOW_PAD = 128
GLA_QKVG_WIDTH = 3072
CHUNK = 64
GATE_NORMALIZER = 16.0
RMS_EPS = 1e-6
Q_SCALE = GLA_HEAD_K ** -0.5

ADAM_LR = 0.001
ADAM_B1 = 0.9
ADAM_B2 = 0.999
ADAM_EPS = 1e-08
ADAM_WD = 0.01
ADAM_STEP = 10

LANES = 128
BF16_ROWS = 16
VMEM_LIMIT = 56 * 1024 * 1024
ROW_TILE = 256
GLA_FWD_ROW_TILE = 512
MATMUL_ROW_TILE = 512
GATHER_RELAY_STEP = 5
TWO_LEVEL_ADD_STEP = 1
TWO_LEVEL_RELAY_STEP = 4


def _dot_nn(a, b):
    return lax.dot_general(a, b, (((1,), (0,)), ((), ())), preferred_element_type=F32)


def _dot_nt(a, b):
    return lax.dot_general(a, b, (((1,), (1,)), ((), ())), preferred_element_type=F32)


def _dot_tn(a, b):
    return lax.dot_general(a, b, (((0,), (0,)), ((), ())), preferred_element_type=F32)


def _rms(x):
    rstd = lax.rsqrt(jnp.mean(x * x, axis=-1, keepdims=True) + RMS_EPS)
    return x * rstd, rstd


def _rms_bwd(dxhat, xhat, rstd):
    return rstd * (dxhat - xhat * jnp.mean(dxhat * xhat, axis=-1, keepdims=True))


def _sigmoid(x):
    return 1.0 / (1.0 + jnp.exp(-x))


def _params(sem=("arbitrary",)):
    return pltpu.CompilerParams(dimension_semantics=sem, vmem_limit_bytes=VMEM_LIMIT)


def _full(shape):
    return pl.BlockSpec(shape, lambda i: (0,) * len(shape))


def _const(shape):
    return pl.BlockSpec(shape, lambda i: (0,) * len(shape), pipeline_mode=pl.Buffered(1))


def _window_sums(ext, forward):
    n = ext.shape[0]
    outs = []
    for g in range(POOL_GROUPS):
        s = ext[:, g * POOL_GROUP_DIM:(g + 1) * POOL_GROUP_DIM]
        for k in range(g + 1):
            shift = (1 << k) if forward else n - (1 << k)
            s = s + pltpu.roll(s, shift, axis=0)
        outs.append(s[:n - POOL_HALO])
    return outs


def _inv_count(row0, tm):
    row = row0 + lax.broadcasted_iota(jnp.int32, (tm, 1), 0)
    return [1.0 / jnp.minimum(row + 1, 2 << g).astype(F32) for g in range(POOL_GROUPS)]


def _pool_mix(u, u_prev, row0, gw_ref, gb):
    tm = u.shape[0]
    sums = _window_sums(jnp.concatenate([u, u_prev], axis=0), True)
    inv = _inv_count(row0, tm)
    pooled, mixed = [], []
    for g in range(POOL_GROUPS):
        ug = u[:, g * POOL_GROUP_DIM:(g + 1) * POOL_GROUP_DIM]
        pg = (sums[g] * inv[g] - ug).astype(BF16)
        pooled.append(pg)
        mixed.append(_dot_nn(pg, gw_ref[g]))
    return pooled, jnp.concatenate(mixed, axis=1) + gb


def _pool_fwd_call(x, nw, w_in, gw, gb, sc, w_out, rider=None):
    seq = x.shape[0]
    tm = min(MATMUL_ROW_TILE, seq)
    nt = seq // tm

    def main(x_ref, nw_ref, win_ref, gw_ref, gb_ref, sc_ref, wout_ref, h_ref, y_ref, silu_ref, dsilu_ref,
             pooled_ref, mixed_ref, halo_ref):
        i = pl.program_id(0)

        @pl.when(i == 0)
        def _():
            halo_ref[...] = jnp.zeros_like(halo_ref)

        xt = x_ref[...]
        xhat, _ = _rms(xt)
        n = (xhat * nw_ref[...]).astype(BF16)
        p = _dot_nn(n, win_ref[...])
        u = p[:, :POOL_WIDTH]
        gate = p[:, POOL_WIDTH:]
        sg = _sigmoid(gate)
        silu = gate * sg
        silu_ref[...] = silu
        dsilu_ref[...] = sg * (1.0 + gate * (1.0 - sg))
        pooled, mixed = _pool_mix(u, halo_ref[...], i * tm, gw_ref, gb_ref[...])
        pooled_ref[...] = jnp.concatenate(pooled, axis=1)
        mixed_ref[...] = mixed
        halo_ref[...] = u[tm - POOL_HALO:, :]
        y = (mixed * sc_ref[...] * silu).astype(BF16)
        y_ref[...] = y
        h_ref[...] = xt + _dot_nn(y, wout_ref[...])

    def body(*refs):
        own, comm = _split_refs(refs, 7, 6, 1, rider)
        _ride_before(comm, pl.program_id(0), nt)
        main(*own)
        _ride_after(comm, pl.program_id(0), nt)

    return pl.pallas_call(
        body, name="pool_fwd", grid=(nt,),
        in_specs=_extend([pl.BlockSpec((tm, D_MODEL), lambda i: (i, 0)), _const((1, D_MODEL)),
                          _const((D_MODEL, 2 * POOL_WIDTH)), _const((POOL_GROUPS, POOL_GROUP_DIM, POOL_GROUP_DIM)),
                          _const((1, POOL_WIDTH)), _const((1, POOL_WIDTH)), _const((POOL_WIDTH, D_MODEL))],
                         rider, "in_specs"),
        out_specs=_extend([pl.BlockSpec((tm, D_MODEL), lambda i: (i, 0))] * 6, rider, "out_specs"),
        out_shape=_extend([jax.ShapeDtypeStruct((seq, D_MODEL), F32), jax.ShapeDtypeStruct((seq, POOL_WIDTH), BF16),
                           jax.ShapeDtypeStruct((seq, POOL_WIDTH), F32), jax.ShapeDtypeStruct((seq, POOL_WIDTH), F32),
                           jax.ShapeDtypeStruct((seq, POOL_WIDTH), BF16),
                           jax.ShapeDtypeStruct((seq, POOL_WIDTH), F32)], rider, "out_shape"),
        scratch_shapes=_extend([pltpu.VMEM((POOL_HALO, POOL_WIDTH), F32)], rider, "scratch"),
        compiler_params=_params(),
    )(x, nw, w_in, gw, gb, sc, w_out, *_extend([], rider, "arrays"))


def _pool_bwd_call(dh, y, silu, dsilu, pooled, mixed, gw, sc, w_out, rider=None):
    seq = dh.shape[0]
    tm = min(MATMUL_ROW_TILE, seq)
    nt = seq // tm

    def main(dh_ref, y_ref, silu_ref, dsilu_ref, pooled_ref, mixed_ref, gw_ref, sc_ref, wout_ref,
             dp_ref, dwout_hbm, dgw_hbm, dgb_ref, dsc_ref, carry_ref, dwout_acc, dgw_acc, dwout_stage, dgw_stage):
        i = pl.program_id(0)
        t = nt - 1 - i

        @pl.when(i == 0)
        def _():
            carry_ref[...] = jnp.zeros_like(carry_ref)
            dwout_acc[...] = jnp.zeros_like(dwout_acc)
            dgw_acc[...] = jnp.zeros_like(dgw_acc)
            dgb_ref[...] = jnp.zeros_like(dgb_ref)
            dsc_ref[...] = jnp.zeros_like(dsc_ref)

        silu = silu_ref[...]
        pooled = [pooled_ref[:, g * POOL_GROUP_DIM:(g + 1) * POOL_GROUP_DIM] for g in range(POOL_GROUPS)]
        sc = sc_ref[...]
        dhb = dh_ref[...].astype(BF16)
        dwout_acc[...] += _dot_tn(y_ref[...], dhb)
        dy = _dot_nt(dhb, wout_ref[...])
        dmixed = dy * sc * silu
        dy_mixed = dy * mixed_ref[...]
        dsc_ref[...] += jnp.sum(dy_mixed * silu, axis=0, keepdims=True)
        dgate = dy_mixed * sc * dsilu_ref[...]
        dgb_ref[...] += jnp.sum(dmixed, axis=0, keepdims=True)
        inv = _inv_count(t * tm, tm)
        dpooled, scaled = [], []
        for g in range(POOL_GROUPS):
            dmg = dmixed[:, g * POOL_GROUP_DIM:(g + 1) * POOL_GROUP_DIM].astype(BF16)
            dgw_acc[g] += _dot_tn(pooled[g], dmg)
            dpg = _dot_nt(dmg, gw_ref[g])
            dpooled.append(dpg)
            scaled.append(dpg * inv[g])
        r = jnp.concatenate(scaled, axis=1)
        sums = _window_sums(jnp.concatenate([r, carry_ref[...]], axis=0), False)
        carry_ref[...] = r[:POOL_HALO, :]
        du = jnp.concatenate([sums[g] - dpooled[g] for g in range(POOL_GROUPS)], axis=1)
        dp_ref[...] = jnp.concatenate([du, dgate], axis=1).astype(BF16)

        @pl.when(i == nt - 1)
        def _():
            dwout_stage[...] = dwout_acc[...].astype(BF16)
            dgw_stage[...] = dgw_acc[...].astype(BF16)
            pltpu.sync_copy(dwout_stage, dwout_hbm)
            pltpu.sync_copy(dgw_stage, dgw_hbm)

    def body(*refs):
        own, comm = _split_refs(refs, 9, 5, 5, rider)
        _ride_before(comm, pl.program_id(0), nt)
        main(*own)
        _ride_after(comm, pl.program_id(0), nt)

    rev = lambda i: (nt - 1 - i, 0)
    return pl.pallas_call(
        body, name="pool_bwd", grid=(nt,),
        in_specs=_extend([pl.BlockSpec((tm, D_MODEL), rev)] * 6
                         + [_const((POOL_GROUPS, POOL_GROUP_DIM, POOL_GROUP_DIM)), _const((1, POOL_WIDTH)),
                            _const((POOL_WIDTH, D_MODEL))], rider, "in_specs"),
        out_specs=_extend([pl.BlockSpec((tm, 2 * POOL_WIDTH), rev), pl.BlockSpec(memory_space=pl.ANY),
                           pl.BlockSpec(memory_space=pl.ANY), _full((1, POOL_WIDTH)), _full((1, POOL_WIDTH))],
                          rider, "out_specs"),
        out_shape=_extend([jax.ShapeDtypeStruct((seq, 2 * POOL_WIDTH), BF16),
                           jax.ShapeDtypeStruct((POOL_WIDTH, D_MODEL), BF16),
                           jax.ShapeDtypeStruct((POOL_GROUPS, POOL_GROUP_DIM, POOL_GROUP_DIM), BF16),
                           jax.ShapeDtypeStruct((1, POOL_WIDTH), F32), jax.ShapeDtypeStruct((1, POOL_WIDTH), F32)],
                          rider, "out_shape"),
        scratch_shapes=_extend([pltpu.VMEM((POOL_HALO, POOL_WIDTH), F32), pltpu.VMEM((POOL_WIDTH, D_MODEL), F32),
                                pltpu.VMEM((POOL_GROUPS, POOL_GROUP_DIM, POOL_GROUP_DIM), F32),
                                pltpu.VMEM((POOL_WIDTH, D_MODEL), BF16),
                                pltpu.VMEM((POOL_GROUPS, POOL_GROUP_DIM, POOL_GROUP_DIM), BF16)], rider, "scratch"),
        compiler_params=_params(),
    )(dh, y, silu, dsilu, pooled, mixed, gw, sc, w_out, *_extend([], rider, "arrays"))


def _rows_then_zeros(ref, lo, hi, rows):
    part = ref[lo:hi, :]
    return jnp.concatenate([part, jnp.zeros((rows - (hi - lo), part.shape[1]), part.dtype)], axis=0)


def _inproj_bwd_call(name, dproj, h_in, nw, w_in, dres, rider=None, transposed=False):
    seq = h_in.shape[0]
    width = dproj.shape[1]
    w_shape = tuple(w_in.shape)
    acc_shape = (width, D_MODEL) if transposed else w_shape
    whole = w_shape[0] // LANES * LANES
    tm = min(MATMUL_ROW_TILE, seq)
    nt = seq // tm

    def main(dproj_ref, h_ref, nw_ref, win_ref, dres_ref, dh_ref, dw_hbm, dnw_ref, dw_acc, dw_stage):
        i = pl.program_id(0)

        @pl.when(i == 0)
        def _():
            dw_acc[...] = jnp.zeros_like(dw_acc)
            dnw_ref[...] = jnp.zeros_like(dnw_ref)

        dpb = dproj_ref[...]
        if transposed:
            dn = _dot_nn(dpb[:, :whole], win_ref[0:whole, :])
            if whole < w_shape[0]:
                dn = dn + _dot_nn(dpb[:, whole:], _rows_then_zeros(win_ref, whole, w_shape[0], width - whole))
        else:
            dn = _dot_nt(dpb, win_ref[...])
        xhat, rstd = _rms(h_ref[...])
        nw_row = nw_ref[...]
        n = (xhat * nw_row).astype(BF16)
        dw_acc[...] += _dot_tn(dpb, n) if transposed else _dot_tn(n, dpb)
        dnw_ref[...] += jnp.sum(dn * xhat, axis=0, keepdims=True)
        dh_ref[...] = _rms_bwd(dn * nw_row, xhat, rstd) + dres_ref[...]

        @pl.when(i == nt - 1)
        def _():
            dw_stage[...] = dw_acc[...].astype(BF16)
            pltpu.sync_copy(dw_stage.at[pl.ds(0, w_shape[0])], dw_hbm)

    def body(*refs):
        own, comm = _split_refs(refs, 5, 3, 2, rider)
        _ride_before(comm, pl.program_id(0), nt)
        main(*own)
        _ride_after(comm, pl.program_id(0), nt)

    row = lambda i: (i, 0)
    return pl.pallas_call(
        body, name=name, grid=(nt,),
        in_specs=_extend([pl.BlockSpec((tm, width), row), pl.BlockSpec((tm, D_MODEL), row), _const((1, D_MODEL)),
                          _const(w_shape), pl.BlockSpec((tm, D_MODEL), row)], rider, "in_specs"),
        out_specs=_extend([pl.BlockSpec((tm, D_MODEL), row), pl.BlockSpec(memory_space=pl.ANY),
                           _full((1, D_MODEL))], rider, "out_specs"),
        out_shape=_extend([jax.ShapeDtypeStruct((seq, D_MODEL), F32), jax.ShapeDtypeStruct(w_shape, BF16),
                           jax.ShapeDtypeStruct((1, D_MODEL), F32)], rider, "out_shape"),
        scratch_shapes=_extend([pltpu.VMEM(acc_shape, F32), pltpu.VMEM(acc_shape, BF16)], rider, "scratch"),
        compiler_params=_params(),
    )(dproj, h_in, nw, w_in, dres, *_extend([], rider, "arrays"))


def _chunk_scan(x, reverse):
    n = x.shape[0]
    pos = lax.broadcasted_iota(jnp.int32, (n, 1), 0) & (CHUNK - 1)
    k = 1
    while k < CHUNK:
        if reverse:
            x = x + jnp.where(pos < CHUNK - k, pltpu.roll(x, n - k, axis=0), 0.0)
        else:
            x = x + jnp.where(pos >= k, pltpu.roll(x, k, axis=0), 0.0)
        k *= 2
    return x


def _chunk_rows(j):
    return slice(j * CHUNK, (j + 1) * CHUNK)


def _kcols(h):
    return slice(h * GLA_HEAD_K, (h + 1) * GLA_HEAD_K)


def _vcols(h):
    return slice(h * GLA_HEAD_V, (h + 1) * GLA_HEAD_V)


def _chunk_masks(tm):
    idx_t = lax.broadcasted_iota(jnp.int32, (tm, tm), 0)
    idx_s = lax.broadcasted_iota(jnp.int32, (tm, tm), 1)
    same_chunk = (idx_t ^ idx_s) < CHUNK
    return same_chunk & (idx_t >= idx_s), same_chunk & (idx_t < idx_s)


class _GlaTerms:
    def __init__(self, kc, q, k, v, low_b, gkw_ref, gkb_ref, masks, saved=None):
        tm = q.shape[0]
        self.q = q * Q_SCALE
        self.k = k
        if saved is None:
            self.z = _dot_nn(low_b, gkw_ref[:, kc]) + gkb_ref[:, kc]
            log_g = (jnp.minimum(self.z, 0.0) - jnp.log(1.0 + jnp.exp(-jnp.abs(self.z)))) / GATE_NORMALIZER
            self.c = _chunk_scan(log_g, False)
        else:
            self.z, self.c = saved
        is_last = lax.broadcasted_iota(jnp.int32, (CHUNK, 1), 0) == CHUNK - 1
        self.c_last = [jnp.sum(jnp.where(is_last, self.c[_chunk_rows(j), :], 0.0), axis=0, keepdims=True)
                       for j in range(tm // CHUNK)]
        c_last_rows = jnp.concatenate([jnp.broadcast_to(r, (CHUNK, r.shape[1])) for r in self.c_last], axis=0)
        self.e_pos = jnp.exp(self.c)
        self.e_neg = jnp.exp(-self.c)
        self.e_rest = jnp.exp(c_last_rows - self.c)
        self.a_b = (self.q * self.e_pos).astype(BF16)
        self.b_b = (self.k * self.e_neg).astype(BF16)
        self.cn_b = (self.q * self.e_neg).astype(BF16)
        self.dp_b = (self.k * self.e_pos).astype(BF16)
        self.kd_b = (self.k * self.e_rest).astype(BF16)
        self.v_b = v.astype(BF16)
        self.lower, self.upper = masks

    def scores(self, kc=slice(None)):
        fwd = _dot_nt(self.a_b[:, kc], self.b_b[:, kc])
        bwd = _dot_nt(self.cn_b[:, kc], self.dp_b[:, kc])
        return jnp.where(self.lower, fwd, jnp.where(self.upper, bwd, 0.0)).astype(BF16)


def _gla_fwd_call(h1, nw, w_in, gkw, gkb, hw, w_out, wf, target):
    seq = h1.shape[0]
    tm = min(GLA_FWD_ROW_TILE, seq)
    nt = seq // tm
    cpt = tm // CHUNK
    n_chunks = seq // CHUNK

    def body(h_ref, nw_ref, win_ref, gkw_ref, gkb_ref, hw_ref, wout_ref, wf_ref, tgt_ref,
             dh2_ref, proj_ref, o_ref, st_ref, scores_ref, loss_ref, dwf_ref, state_ref):
        i = pl.program_id(0)

        @pl.when(i == 0)
        def _():
            state_ref[...] = jnp.zeros_like(state_ref)
            loss_ref[...] = jnp.zeros_like(loss_ref)
            dwf_ref[...] = jnp.zeros_like(dwf_ref)

        ht = h_ref[...]
        xhat, _ = _rms(ht)
        n = (xhat * nw_ref[...]).astype(BF16)
        sections = {}
        for name, lo, hi in (("low", GLA_QKVG_WIDTH, GLA_IN_PAD), ("qk", 0, 2 * GLA_KEY_WIDTH),
                             ("v", 2 * GLA_KEY_WIDTH, GLA_QKVG_WIDTH - GLA_VALUE_WIDTH),
                             ("gate", GLA_QKVG_WIDTH - GLA_VALUE_WIDTH, GLA_QKVG_WIDTH)):
            rows = (win_ref[lo:hi, :] if hi <= GLA_IN_WIDTH
                    else _rows_then_zeros(win_ref, lo, GLA_IN_WIDTH, hi - lo))
            sections[name] = _dot_nt(n, rows)
            proj_ref[:, lo:hi] = sections[name]
        low_b = sections["low"].astype(BF16)
        masks = _chunk_masks(tm)
        on_heads = []
        for h in range(GLA_HEADS):
            kc, vc = _kcols(h), _vcols(h)
            g = _GlaTerms(kc, sections["qk"][:, kc], sections["qk"][:, GLA_KEY_WIDTH:][:, kc], sections["v"][:, vc],
                          low_b, gkw_ref, gkb_ref, masks)
            srows = slice(h * GLA_HEAD_V, (h + 1) * GLA_HEAD_V)
            scores = g.scores()
            for b in range(tm // ROW_TILE):
                part = slice(b * ROW_TILE, (b + 1) * ROW_TILE)
                scores_ref[part, h * ROW_TILE:(h + 1) * ROW_TILE] = scores[part, part]
            o_intra = _dot_nn(scores, g.v_b)
            state = state_ref[srows, :]
            o_rows = []
            for j in range(cpt):
                r = _chunk_rows(j)
                st_ref[j, srows, :] = state
                o_rows.append(o_intra[r] + _dot_nt(g.a_b[r], state.astype(BF16)))
                decay = jnp.exp(g.c_last[j])
                state = state * decay + _dot_tn(g.v_b[r], g.kd_b[r])
            state_ref[srows, :] = state
            o_head = jnp.concatenate(o_rows, axis=0)
            o_ref[:, vc] = o_head
            proj_ref[:, GLA_SAVED_Z + kc.start:GLA_SAVED_Z + kc.stop] = g.z
            proj_ref[:, GLA_SAVED_C + kc.start:GLA_SAVED_C + kc.stop] = g.c
            on_heads.append(_rms(o_head)[0])
        gate = sections["gate"]
        on = jnp.concatenate(on_heads, axis=1) * hw_ref[...]
        y = (on * (gate * _sigmoid(gate))).astype(BF16)
        h2 = ht + _dot_nn(y, wout_ref[...])
        xhat2, rstd2 = _rms(h2)
        wf_row = wf_ref[...]
        err = xhat2 * wf_row - tgt_ref[...]
        loss_ref[...] += 0.5 * jnp.sum(err * err) / D_MODEL
        dout = err * (1.0 / D_MODEL)
        dwf_ref[...] += jnp.sum(dout * xhat2, axis=0, keepdims=True)
        dh2_ref[...] = _rms_bwd(dout * wf_row, xhat2, rstd2)

    row = lambda i: (i, 0)
    return pl.pallas_call(
        body, name="gla_fwd", grid=(nt,),
        in_specs=[pl.BlockSpec((tm, D_MODEL), row), _const((1, D_MODEL)), _const((GLA_IN_WIDTH, D_MODEL)),
                  _const((GLA_LOW_PAD, GLA_KEY_WIDTH)), _const((1, GLA_KEY_WIDTH)), _const((1, GLA_VALUE_WIDTH)),
                  _const((GLA_VALUE_WIDTH, D_MODEL)), _const((1, D_MODEL)), pl.BlockSpec((tm, D_MODEL), row)],
        out_specs=[pl.BlockSpec((tm, D_MODEL), row), pl.BlockSpec((tm, GLA_SAVED_WIDTH), row),
                   pl.BlockSpec((tm, GLA_VALUE_WIDTH), row),
                   pl.BlockSpec((cpt, GLA_VALUE_WIDTH, GLA_HEAD_K), lambda i: (i, 0, 0)),
                   pl.BlockSpec((tm, GLA_HEADS * ROW_TILE), row), _full((8, LANES)), _full((1, D_MODEL))],
        out_shape=[jax.ShapeDtypeStruct((seq, D_MODEL), F32), jax.ShapeDtypeStruct((seq, GLA_SAVED_WIDTH), F32),
                   jax.ShapeDtypeStruct((seq, GLA_VALUE_WIDTH), F32),
                   jax.ShapeDtypeStruct((n_chunks, GLA_VALUE_WIDTH, GLA_HEAD_K), F32),
                   jax.ShapeDtypeStruct((seq, GLA_HEADS * ROW_TILE), BF16),
                   jax.ShapeDtypeStruct((8, LANES), F32), jax.ShapeDtypeStruct((1, D_MODEL), F32)],
        scratch_shapes=[pltpu.VMEM((GLA_VALUE_WIDTH, GLA_HEAD_K), F32)],
        compiler_params=_params(),
    )(h1, nw, w_in, gkw, gkb, hw, w_out, wf, target)


def _gla_bwd_call(dh2, proj, o, states, scores, gkw, gkb, hw, w_out):
    seq = dh2.shape[0]
    tm = ROW_TILE
    nt = seq // tm
    cpt = tm // CHUNK

    def body(dh_ref, proj_ref, o_ref, st_ref, scores_ref, gkw_ref, gkb_ref, hw_ref, wout_ref,
             dproj_ref, dwout_hbm, dhw_ref, dgkw_ref, dgkb_ref, dstate_ref, dwout_acc, dwout_stage):
        i = pl.program_id(0)

        @pl.when(i == 0)
        def _():
            dstate_ref[...] = jnp.zeros_like(dstate_ref)
            dwout_acc[...] = jnp.zeros_like(dwout_acc)
            dhw_ref[...] = jnp.zeros_like(dhw_ref)
            dgkw_ref[...] = jnp.zeros_like(dgkw_ref)
            dgkb_ref[...] = jnp.zeros_like(dgkb_ref)

        dhb = dh_ref[...].astype(BF16)
        dy = _dot_nt(dhb, wout_ref[...])
        v0, g0 = 2 * GLA_KEY_WIDTH, GLA_QKVG_WIDTH - GLA_VALUE_WIDTH
        gate = proj_ref[:, g0:GLA_QKVG_WIDTH]
        low_b = proj_ref[:, GLA_QKVG_WIDTH:GLA_IN_PAD].astype(BF16)
        o = o_ref[...]
        hw_row = hw_ref[...]
        sg = _sigmoid(gate)
        silu = gate * sg
        don = dy * silu
        on_parts, do_parts, dhw_parts = [], [], []
        for h in range(GLA_HEADS):
            vc = _vcols(h)
            xh, rs = _rms(o[:, vc])
            on_parts.append(xh * hw_row[:, vc])
            dhw_parts.append(jnp.sum(don[:, vc] * xh, axis=0, keepdims=True))
            do_parts.append(_rms_bwd(don[:, vc] * hw_row[:, vc], xh, rs).astype(BF16))
        on = jnp.concatenate(on_parts, axis=1)
        dwout_acc[...] += _dot_tn((on * silu).astype(BF16), dhb)
        dhw_ref[...] += jnp.concatenate(dhw_parts, axis=1)
        dproj_ref[:, g0:GLA_QKVG_WIDTH] = (dy * on * (sg * (1.0 + gate * (1.0 - sg)))).astype(BF16)

        last_row = lax.broadcasted_iota(jnp.int32, (CHUNK, 1), 0) == CHUNK - 1
        g = _GlaTerms(slice(0, GLA_KEY_WIDTH), proj_ref[:, :GLA_KEY_WIDTH], proj_ref[:, GLA_KEY_WIDTH:v0],
                      proj_ref[:, v0:g0], low_b, gkw_ref, gkb_ref, _chunk_masks(tm),
                      saved=(proj_ref[:, GLA_SAVED_Z:GLA_SAVED_C], proj_ref[:, GLA_SAVED_C:GLA_SAVED_WIDTH]))
        dc_h = []
        for h in range(GLA_HEADS):
            kc, vc = _kcols(h), _vcols(h)
            k_cols = slice(GLA_KEY_WIDTH + kc.start, GLA_KEY_WIDTH + kc.stop)
            v_cols = slice(v0 + vc.start, v0 + vc.stop)
            do_h = do_parts[h]
            srows = slice(h * GLA_HEAD_V, (h + 1) * GLA_HEAD_V)
            scores = scores_ref[:, h * ROW_TILE:(h + 1) * ROW_TILE]
            dscores = _dot_nt(do_h, g.v_b[:, vc])
            dfwd = jnp.where(g.lower, dscores, 0.0).astype(BF16)
            dbwd = jnp.where(g.upper, dscores, 0.0).astype(BF16)
            dv_intra = _dot_tn(scores, do_h)
            da_intra = _dot_nn(dfwd, g.b_b[:, kc])
            db = _dot_tn(dfwd, g.a_b[:, kc])
            dcn = _dot_nn(dbwd, g.dp_b[:, kc])
            ddp = _dot_tn(dbwd, g.cn_b[:, kc])
            dstate = dstate_ref[srows, :]
            da_rows, dkd_rows, dv_rows, dcl_rows = [None] * cpt, [None] * cpt, [None] * cpt, [None] * cpt
            for j in reversed(range(cpt)):
                r = _chunk_rows(j)
                state = st_ref[j, srows, :]
                dstate_b = dstate.astype(BF16)
                do_c = do_h[r]
                dv_rows[j] = dv_intra[r] + _dot_nt(g.kd_b[r, kc], dstate_b)
                da_rows[j] = da_intra[r] + _dot_nn(do_c, state.astype(BF16))
                dkd = _dot_nn(g.v_b[r, vc], dstate_b) * g.e_rest[r, kc]
                dkd_rows[j] = dkd
                decay = jnp.exp(g.c_last[j][:, kc])
                dc_last = (jnp.sum(dkd * g.k[r, kc], axis=0, keepdims=True)
                           + decay * jnp.sum(state * dstate, axis=0, keepdims=True))
                dcl_rows[j] = jnp.where(last_row, dc_last, 0.0)
                dstate = _dot_tn(do_c, g.a_b[r, kc]) + dstate * decay
            dstate_ref[srows, :] = dstate
            da = jnp.concatenate(da_rows, axis=0)
            dkd = jnp.concatenate(dkd_rows, axis=0)
            dproj_ref[:, v_cols] = jnp.concatenate(dv_rows, axis=0).astype(BF16)
            q_up, q_down = da * g.e_pos[:, kc], dcn * g.e_neg[:, kc]
            k_up, k_down = ddp * g.e_pos[:, kc], db * g.e_neg[:, kc] + dkd
            dproj_ref[:, kc] = (Q_SCALE * (q_up + q_down)).astype(BF16)
            dproj_ref[:, k_cols] = (k_up + k_down).astype(BF16)
            dc_h.append(g.q[:, kc] * (q_up - q_down) + g.k[:, kc] * (k_up - k_down)
                        + jnp.concatenate(dcl_rows, axis=0))
        dz = _chunk_scan(jnp.concatenate(dc_h, axis=1), True) * (1.0 / GATE_NORMALIZER) * (1.0 - _sigmoid(g.z))
        dzb = dz.astype(BF16)
        dgkb_ref[...] += jnp.sum(dz, axis=0, keepdims=True)
        dgkw_ref[...] += _dot_tn(low_b, dzb)
        dproj_ref[:, GLA_QKVG_WIDTH:] = _dot_nt(dzb, gkw_ref[...]).astype(BF16)

        @pl.when(i == nt - 1)
        def _():
            dwout_stage[...] = dwout_acc[...].astype(BF16)
            pltpu.sync_copy(dwout_stage, dwout_hbm)

    rev = lambda i: (nt - 1 - i, 0)
    return pl.pallas_call(
        body, name="gla_bwd", grid=(nt,),
        in_specs=[pl.BlockSpec((tm, D_MODEL), rev), pl.BlockSpec((tm, GLA_SAVED_WIDTH), rev),
                  pl.BlockSpec((tm, GLA_VALUE_WIDTH), rev),
                  pl.BlockSpec((cpt, GLA_VALUE_WIDTH, GLA_HEAD_K), lambda i: (nt - 1 - i, 0, 0)),
                  pl.BlockSpec((tm, GLA_HEADS * ROW_TILE), rev),
                  _const((GLA_LOW_PAD, GLA_KEY_WIDTH)), _const((1, GLA_KEY_WIDTH)), _const((1, GLA_VALUE_WIDTH)),
                  _const((GLA_VALUE_WIDTH, D_MODEL))],
        out_specs=[pl.BlockSpec((tm, GLA_IN_PAD), rev), pl.BlockSpec(memory_space=pl.ANY),
                   _full((1, GLA_VALUE_WIDTH)), _full((GLA_LOW_PAD, GLA_KEY_WIDTH)), _full((1, GLA_KEY_WIDTH))],
        out_shape=[jax.ShapeDtypeStruct((seq, GLA_IN_PAD), BF16), jax.ShapeDtypeStruct((GLA_VALUE_WIDTH, D_MODEL), BF16),
                   jax.ShapeDtypeStruct((1, GLA_VALUE_WIDTH), F32), jax.ShapeDtypeStruct((GLA_LOW_PAD, GLA_KEY_WIDTH), F32),
                   jax.ShapeDtypeStruct((1, GLA_KEY_WIDTH), F32)],
        scratch_shapes=[pltpu.VMEM((GLA_VALUE_WIDTH, GLA_HEAD_K), F32), pltpu.VMEM((GLA_VALUE_WIDTH, D_MODEL), F32),
                        pltpu.VMEM((GLA_VALUE_WIDTH, D_MODEL), BF16)],
        compiler_params=_params(),
    )(dh2, proj, o, states, scores, gkw, gkb, hw, w_out)


def _position():
    return lax.axis_index("x"), lax.axis_index("y"), lax.axis_index("c")


def _lead_slot(ref, d):
    return ref.at[d]


def _row_slot(rows):
    return lambda ref, d: ref.at[pl.ds(pl.multiple_of(d * rows, rows), rows)]


def _dim1_slot(size):
    return lambda ref, d: ref.at[:, pl.ds(pl.multiple_of(d * size, size), size)]


class _Gather:
    def __init__(self, in_refs, out_refs, slots, send_sems, recv_sems, local_sems):
        self.in_refs, self.out_refs, self.slots = in_refs, out_refs, slots
        self.send_sems, self.recv_sems, self.local_sems = send_sems, recv_sems, local_sems
        self.n = len(in_refs)
        x, y, c = _position()
        self.c = c
        self.me, self.sibling = (x, y, c), (x, y, 1 - c)
        self.near = [(1 - x, y), (x, 1 - y)]
        self.diagonal = (1 - x, 1 - y)
        self.relay_from = (x ^ c, y ^ (1 - c))
        self.relay_to = (x ^ (1 - c), y ^ c)

    def _copy(self, a, k, block, to, from_input=False):
        part = self.slots[a](self.out_refs[a], 4 * block[0] + 2 * block[1] + block[2])
        return pltpu.make_async_remote_copy(
            src_ref=self.in_refs[a] if from_input else part, dst_ref=part,
            send_sem=self.send_sems.at[a, k], recv_sem=self.recv_sems.at[a, k], device_id=to, device_id_type=MESH)

    def _mine(self):
        return [pltpu.make_async_copy(self.in_refs[a], self.slots[a](self.out_refs[a], 4 * self.me[0] + 2 * self.me[1]
                                                                    + self.me[2]), self.local_sems.at[a])
                for a in range(self.n)]

    def _first(self):
        first = [self._copy(a, 0, self.me, self.sibling, True) for a in range(self.n)]
        return first + [self._copy(a, 1 + j, self.me, (*chip, self.c), True)
                        for j, chip in enumerate(self.near) for a in range(self.n)]

    def _relayed(self):
        return [self._copy(a, 3, (*self.relay_from, self.c), (*self.relay_to, self.c)) for a in range(self.n)]

    def _passed(self, j):
        chip = self.near[j] if j < 2 else self.diagonal
        return [self._copy(a, 4 + j, (*chip, self.c), self.sibling) for a in range(self.n)]

    def start(self):
        for cp in self._mine() + self._first():
            cp.start()

    def forward(self):
        for j, chip in enumerate(self.near):
            for a in range(self.n):
                self._copy(a, 1 + j, (*chip, self.c), self.me).wait_recv()
        for cp in self._relayed() + self._passed(0) + self._passed(1):
            cp.start()

    def relay(self):
        pass

    def finish(self):
        for a in range(self.n):
            self._copy(a, 3, (*self.diagonal, self.c), self.me).wait_recv()
        for cp in self._passed(2):
            cp.start()
        for a in range(self.n):
            self._copy(a, 0, self.sibling, self.me).wait_recv()
        for j, chip in enumerate(self.near + [self.diagonal]):
            for a in range(self.n):
                self._copy(a, 4 + j, (*chip, 1 - self.c), self.me).wait_recv()
        for cp in self._first() + self._relayed() + self._passed(0) + self._passed(1) + self._passed(2):
            cp.wait_send()
        for cp in self._mine():
            cp.wait()


class _Exchange:
    def __init__(self, in_refs, out_refs, slots, send_sems, recv_sems, local_sems):
        self.in_refs, self.out_refs, self.slots = in_refs, out_refs, slots
        self.send_sems, self.recv_sems, self.local_sems = send_sems, recv_sems, local_sems
        self.n = len(in_refs)
        self.pos = _position()

    def _copies(self):
        x, y, c = self.pos
        me = 4 * x + 2 * y + c
        mine = [pltpu.make_async_copy(self.slots[a](self.in_refs[a], me), self.out_refs[a].at[me],
                                      self.local_sems.at[a]) for a in range(self.n)]
        remote = []
        for k in range(1, N_DEV):
            px, py, pc = x ^ (k >> 2), y ^ ((k >> 1) & 1), c ^ (k & 1)
            for a in range(self.n):
                remote.append(pltpu.make_async_remote_copy(
                    src_ref=self.slots[a](self.in_refs[a], 4 * px + 2 * py + pc), dst_ref=self.out_refs[a].at[me],
                    send_sem=self.send_sems.at[a, k - 1], recv_sem=self.recv_sems.at[a, k - 1],
                    device_id=(px, py, pc), device_id_type=MESH))
        return mine, remote

    def start(self):
        mine, remote = self._copies()
        for cp in mine + remote:
            cp.start()

    def forward(self):
        pass

    def relay(self):
        pass

    def finish(self):
        mine, remote = self._copies()
        for cp in remote:
            cp.wait_recv()
        for cp in remote:
            cp.wait_send()
        for cp in mine:
            cp.wait()


class _Rider:
    def __init__(self, kind, arrays, out_shapes, slots, scratch=None, forward_step=None):
        self.kind, self.arrays, self.slots = kind, list(arrays), slots
        self.n = len(self.arrays)
        hbm = pl.BlockSpec(memory_space=pl.ANY)
        self.in_specs = [hbm] * self.n
        self.out_specs = [hbm] * self.n
        self.out_shape = [jax.ShapeDtypeStruct(tuple(s), a.dtype) for s, a in zip(out_shapes, self.arrays)]
        self.scratch = scratch if scratch is not None else [
            pltpu.SemaphoreType.DMA((self.n, 7)), pltpu.SemaphoreType.DMA((self.n, 7)),
            pltpu.SemaphoreType.DMA((self.n,))]
        self.forward_step = forward_step
        self.relay_step = None

    def bind(self, in_refs, out_refs, scratch):
        return self.kind(in_refs, out_refs, self.slots, *scratch)


def _gather_rider(shards, full_shapes, slots, forward_step=None):
    return _Rider(_Gather, shards, full_shapes, slots, None, forward_step)


def _exchange_rider(sends, part_shapes, slots):
    return _Rider(_Exchange, sends, [(N_DEV,) + tuple(s) for s in part_shapes], slots)


def _split_refs(refs, n_in, n_out, n_scratch, rider):
    k = rider.n if rider is not None else 0
    ins, r_ins = refs[:n_in], refs[n_in:n_in + k]
    outs, r_outs = refs[n_in + k:n_in + k + n_out], refs[n_in + k + n_out:n_in + 2 * k + n_out]
    rest = refs[n_in + 2 * k + n_out:]
    scratch, r_scratch = rest[:n_scratch], rest[n_scratch:]
    comm = rider.bind(r_ins, r_outs, r_scratch) if rider is not None else None
    if comm is not None:
        comm.forward_step, comm.relay_step = rider.forward_step, rider.relay_step
    return ins + outs + scratch, comm


def _ride_before(comm, i, nt):
    if comm is not None:
        pl.when(i == 0)(comm.start)
        pl.when(i == (nt - 1 if comm.forward_step is None else min(comm.forward_step, nt - 1)))(comm.forward)
        pl.when(i == (nt - 1 if comm.relay_step is None else min(comm.relay_step, nt - 1)))(comm.relay)


def _ride_after(comm, i, nt):
    if comm is not None:
        pl.when(i == nt - 1)(comm.finish)


def _extend(specs, rider, field):
    return list(specs) + (getattr(rider, field) if rider is not None else [])


def _comm_call(name, rider):
    def body(*refs):
        _, comm = _split_refs(refs, 0, 0, 0, rider)
        comm.start()
        comm.forward()
        comm.relay()
        comm.finish()

    return pl.pallas_call(body, name=name, in_specs=rider.in_specs, out_specs=rider.out_specs,
                          out_shape=rider.out_shape, scratch_shapes=rider.scratch,
                          compiler_params=pltpu.CompilerParams(vmem_limit_bytes=VMEM_LIMIT))(*rider.arrays)


N_CHIPS = 4


class _TwoLevel:
    def __init__(self, in_refs, out_refs, slots, *scratch):
        self.in_refs, self.out_refs, self.slots = in_refs, out_refs, slots
        self.n = n = len(in_refs)
        self.own_bufs, self.recv_bufs, self.relay_bufs = scratch[:n], scratch[n:2 * n], scratch[2 * n:3 * n]
        self.swap_send, self.swap_recv, self.local_sems, self.chip_send, self.chip_recv = scratch[3 * n:]
        x, y, c = self.pos = _position()
        self.first = (x ^ (1 - c), y ^ c)
        self.second = (x ^ c, y ^ (1 - c))
        self.chip_index = lambda chip: 2 * chip[0] + chip[1]

    def _swap(self):
        x, y, c = self.pos
        return [pltpu.make_async_remote_copy(
            src_ref=self.slots[a](self.in_refs[a], 2 * q + 1 - c), dst_ref=self.recv_bufs[a].at[q],
            send_sem=self.swap_send.at[a, q], recv_sem=self.swap_recv.at[a, q],
            device_id=(x, y, 1 - c), device_id_type=MESH) for a in range(self.n) for q in range(N_CHIPS)]

    def _mine(self):
        c = self.pos[2]
        return [pltpu.make_async_copy(self.slots[a](self.in_refs[a], 2 * q + c), self.own_bufs[a].at[q],
                                      self.local_sems.at[a, q]) for a in range(self.n) for q in range(N_CHIPS)]

    def _to_chip(self, a, k, src, dst, chip):
        return pltpu.make_async_remote_copy(
            src_ref=src, dst_ref=dst, send_sem=self.chip_send.at[a, k], recv_sem=self.chip_recv.at[a, k],
            device_id=(*chip, self.pos[2]), device_id_type=MESH)

    def _first_wave(self):
        x, y, _ = self.pos
        diagonal = self.chip_index((1 - x, 1 - y))
        passed_on = [self._to_chip(a, 1, self.own_bufs[a].at[diagonal], self.relay_bufs[a], self.first)
                     for a in range(self.n)]
        return passed_on + [self._to_chip(a, 0, self.own_bufs[a].at[self.chip_index(self.first)],
                                          self.out_refs[a].at[1], self.first) for a in range(self.n)]

    def _second_wave(self):
        return [self._to_chip(a, 2, self.own_bufs[a].at[self.chip_index(self.second)], self.out_refs[a].at[2],
                              self.second) for a in range(self.n)]

    def _own(self):
        x, y, _ = self.pos
        return [pltpu.make_async_copy(self.own_bufs[a].at[2 * x + y], self.out_refs[a].at[0],
                                      self.local_sems.at[a, N_CHIPS]) for a in range(self.n)]

    def start(self):
        for cp in self._swap() + self._mine():
            cp.start()

    def forward(self):
        swap, mine = self._swap(), self._mine()
        for a in range(self.n):
            for q in range(N_CHIPS):
                mine[a * N_CHIPS + q].wait()
                swap[a * N_CHIPS + q].wait_recv()
                self.own_bufs[a][q] = (self.own_bufs[a][q].astype(F32)
                                       + self.recv_bufs[a][q].astype(F32)).astype(BF16)
        for cp in self._first_wave() + self._own():
            cp.start()

    def relay(self):
        second = self.chip_index(self.second)
        for a in range(self.n):
            self._to_chip(a, 1, self.relay_bufs[a], self.relay_bufs[a], self.first).wait_recv()
            self.own_bufs[a][second] = (self.own_bufs[a][second].astype(F32)
                                        + self.relay_bufs[a][...].astype(F32)).astype(BF16)
        for cp in self._second_wave():
            cp.start()

    def finish(self):
        for a in range(self.n):
            self._to_chip(a, 0, self.out_refs[a].at[1], self.out_refs[a].at[1], self.first).wait_recv()
            self._to_chip(a, 2, self.out_refs[a].at[2], self.out_refs[a].at[2], self.second).wait_recv()
        for cp in self._first_wave() + self._second_wave() + self._swap():
            cp.wait_send()
        for cp in self._own():
            cp.wait()


def _two_level_rider(sends, part_shapes, slots, forward_step=None, relay_step=None):
    n = len(sends)
    bufs = [pltpu.VMEM((N_CHIPS,) + tuple(s), a.dtype) for s, a in zip(part_shapes, sends)]
    relay_bufs = [pltpu.VMEM(tuple(s), a.dtype) for s, a in zip(part_shapes, sends)]
    scratch = bufs + bufs + relay_bufs + [
        pltpu.SemaphoreType.DMA((n, N_CHIPS)), pltpu.SemaphoreType.DMA((n, N_CHIPS)),
        pltpu.SemaphoreType.DMA((n, N_CHIPS + 1)), pltpu.SemaphoreType.DMA((n, 3)), pltpu.SemaphoreType.DMA((n, 3))]
    rider = _Rider(_TwoLevel, sends, [(3,) + tuple(s) for s in part_shapes], slots, scratch, forward_step)
    rider.relay_step = relay_step
    return rider


class _Joined:
    def __init__(self, first, second):
        self.first, self.second = first, second

    def start(self):
        self.first.start()
        self.second.start()

    def forward(self):
        self.first.forward()
        self.second.forward()

    def relay(self):
        self.first.relay()
        self.second.relay()

    def finish(self):
        self.first.finish()
        self.second.finish()


class _JoinedRider:
    def __init__(self, first, second):
        self.first, self.second = first, second
        self.n = first.n + second.n
        self.arrays = first.arrays + second.arrays
        self.in_specs = first.in_specs + second.in_specs
        self.out_specs = first.out_specs + second.out_specs
        self.out_shape = first.out_shape + second.out_shape
        self.scratch = first.scratch + second.scratch
        self.forward_step = first.forward_step
        self.relay_step = first.relay_step

    def bind(self, in_refs, out_refs, scratch):
        k, s = self.first.n, len(self.first.scratch)
        return _Joined(self.first.bind(in_refs[:k], out_refs[:k], scratch[:s]),
                       self.second.bind(in_refs[k:], out_refs[k:], scratch[s:]))


def _adamw(w, g, m, v):
    m = ADAM_B1 * m + (1.0 - ADAM_B1) * g
    v = ADAM_B2 * v + (1.0 - ADAM_B2) * (g * g)
    m_hat = m / (1.0 - ADAM_B1 ** ADAM_STEP)
    v_hat = v / (1.0 - ADAM_B2 ** ADAM_STEP)
    delta = -ADAM_LR * (m_hat / (jnp.sqrt(v_hat) + ADAM_EPS) + ADAM_WD * w)
    return delta, m, v


def _sum_parts(parts_ref, index=()):
    g = parts_ref[(0,) + index].astype(F32)
    for s in range(1, parts_ref.shape[0]):
        g = g + parts_ref[(s,) + index].astype(F32)
    return g


def _adamw_group_call(name, groups):
    k = len(groups)

    def body(*refs):
        ins, outs = refs[:4 * k], refs[4 * k:]
        for i in range(k):
            parts_ref, w_ref, m_ref, v_ref = ins[4 * i:4 * i + 4]
            g = _sum_parts(parts_ref)
            delta, m_new, v_new = _adamw(w_ref[...], g, m_ref[...], v_ref[...])
            for out_ref, value in zip(outs[4 * i:4 * i + 4], (g, delta, m_new, v_new)):
                out_ref[...] = value

    vmem = pl.BlockSpec(memory_space=pltpu.VMEM)
    res = pl.pallas_call(
        body, name=name, in_specs=[vmem] * (4 * k), out_specs=[vmem] * (4 * k),
        out_shape=[jax.ShapeDtypeStruct(grp[1].shape, F32) for grp in groups for _ in range(4)],
        compiler_params=pltpu.CompilerParams(vmem_limit_bytes=VMEM_LIMIT),
    )(*[a for grp in groups for a in grp])
    return [res[4 * i:4 * i + 4] for i in range(k)]


def _adamw_slabs_call(name, parts, w, m, v, rider=None):
    def main(parts_ref, w_ref, m_ref, v_ref, g_ref, delta_ref, m_out, v_out):
        g = _sum_parts(parts_ref)
        delta, m_new, v_new = _adamw(w_ref[...], g, m_ref[...], v_ref[...])
        g_ref[...] = g
        delta_ref[...] = delta
        m_out[...] = m_new
        v_out[...] = v_new

    def body(*refs):
        own, comm = _split_refs(refs, 4, 4, 0, rider)
        if comm is not None:
            comm.start()
        main(*own)
        if comm is not None:
            comm.forward()
            comm.relay()
            comm.finish()

    vmem = pl.BlockSpec(memory_space=pltpu.VMEM)
    return pl.pallas_call(
        body, name=name, in_specs=_extend([vmem] * 4, rider, "in_specs"),
        out_specs=_extend([vmem] * 4, rider, "out_specs"),
        out_shape=_extend([jax.ShapeDtypeStruct(w.shape, F32)] * 4, rider, "out_shape"),
        scratch_shapes=_extend([], rider, "scratch"),
        compiler_params=pltpu.CompilerParams(vmem_limit_bytes=VMEM_LIMIT),
    )(parts, w, m, v, *_extend([], rider, "arrays"))


WIDE_ROWS = 8
NARROW_ROWS = 40
NARROW_GKW_ROW = 8
NARROW_GKB_ROW = 24
NARROW_HW_ROW = 32
GROUP_SHARD = POOL_GROUP_DIM // N_DEV
KEY_SHARD = GLA_KEY_WIDTH // N_DEV
HEAD_V_SHARD = GLA_HEAD_V // N_DEV


def _small_adamw_call(wide, narrow, w, m, v):
    names = ("norm_w", "pool_scale", "final_norm_w", "pool_group_b", "gla_gk_w", "gla_gk_b", "gla_head_norm_w")
    where = {
        "norm_w": (0, slice(0, 2), slice(None)),
        "pool_scale": (0, slice(2, 3), slice(None)),
        "final_norm_w": (0, slice(3, 4), slice(None)),
        "pool_group_b": (1, slice(0, POOL_GROUPS), slice(0, GROUP_SHARD)),
        "gla_gk_w": (1, slice(NARROW_GKW_ROW, NARROW_GKW_ROW + GLA_GATE_RANK), slice(0, KEY_SHARD)),
        "gla_gk_b": (1, slice(NARROW_GKB_ROW, NARROW_GKB_ROW + 1), slice(0, KEY_SHARD)),
        "gla_head_norm_w": (1, slice(NARROW_HW_ROW, NARROW_HW_ROW + 1), slice(0, HEAD_V_SHARD)),
    }
    k = len(names)

    def body(*refs):
        parts = refs[0:2]
        w_refs, m_refs, v_refs = refs[2:2 + k], refs[2 + k:2 + 2 * k], refs[2 + 2 * k:2 + 3 * k]
        outs = refs[2 + 3 * k:]
        loss_ref = outs[0]
        loss_ref[...] = _sum_parts(parts[0], (slice(4, 5), slice(0, 1)))
        for i, name in enumerate(names):
            buf, rows, cols = where[name]
            g = _sum_parts(parts[buf], (rows, cols))
            delta, m_new, v_new = _adamw(w_refs[i][...], g, m_refs[i][...], v_refs[i][...])
            outs[1 + i][...] = g
            outs[1 + k + i][...] = delta
            outs[1 + 2 * k + i][...] = m_new
            outs[1 + 3 * k + i][...] = v_new

    vmem = pl.BlockSpec(memory_space=pltpu.VMEM)
    shapes = [jax.ShapeDtypeStruct(w[n].shape, F32) for n in names]
    res = pl.pallas_call(
        body, name="adamw_small", in_specs=[vmem] * (2 + 3 * k), out_specs=[vmem] * (1 + 4 * k),
        out_shape=[jax.ShapeDtypeStruct((1, 1), F32)] + shapes * 4,
    )(wide, narrow, *[w[n] for n in names], *[m[n] for n in names], *[v[n] for n in names])
    unzip = lambda j: dict(zip(names, res[1 + j * k:1 + (j + 1) * k]))
    return res[0], unzip(0), unzip(1), unzip(2), unzip(3)


def kernel(x, norm_w, pool_in_w, pool_group_w, pool_group_b, pool_scale, pool_out_w, gla_in_w, gla_gk_w, gla_gk_b, gla_head_norm_w, gla_out_w, final_norm_w, loss_target, m_norm_w, m_pool_in_w, m_pool_group_w, m_pool_group_b, m_pool_scale, m_pool_out_w, m_gla_in_w, m_gla_gk_w, m_gla_gk_b, m_gla_head_norm_w, m_gla_out_w, m_final_norm_w, v_norm_w, v_pool_in_w, v_pool_group_w, v_pool_group_b, v_pool_scale, v_pool_out_w, v_gla_in_w, v_gla_gk_w, v_gla_gk_b, v_gla_head_norm_w, v_gla_out_w, v_final_norm_w):
    w = dict(norm_w=norm_w, pool_in_w=pool_in_w, pool_group_w=pool_group_w, pool_group_b=pool_group_b,
             pool_scale=pool_scale, pool_out_w=pool_out_w, gla_in_w=gla_in_w, gla_gk_w=gla_gk_w, gla_gk_b=gla_gk_b,
             gla_head_norm_w=gla_head_norm_w, gla_out_w=gla_out_w, final_norm_w=final_norm_w)
    m = dict(norm_w=m_norm_w, pool_in_w=m_pool_in_w, pool_group_w=m_pool_group_w, pool_group_b=m_pool_group_b,
             pool_scale=m_pool_scale, pool_out_w=m_pool_out_w, gla_in_w=m_gla_in_w, gla_gk_w=m_gla_gk_w,
             gla_gk_b=m_gla_gk_b, gla_head_norm_w=m_gla_head_norm_w, gla_out_w=m_gla_out_w,
             final_norm_w=m_final_norm_w)
    v = dict(norm_w=v_norm_w, pool_in_w=v_pool_in_w, pool_group_w=v_pool_group_w, pool_group_b=v_pool_group_b,
             pool_scale=v_pool_scale, pool_out_w=v_pool_out_w, gla_in_w=v_gla_in_w, gla_gk_w=v_gla_gk_w,
             gla_gk_b=v_gla_gk_b, gla_head_norm_w=v_gla_head_norm_w, gla_out_w=v_gla_out_w,
             final_norm_w=v_final_norm_w)
    col_shard = GLA_IN_WIDTH // N_DEV
    row_shard = D_MODEL // N_DEV

    def lanes(a):
        return jnp.pad(a, [(0, 0)] * (a.ndim - 1) + [(0, LANES - a.shape[-1])])

    small_in = jnp.concatenate([lanes(pool_group_b[0]), lanes(gla_gk_b), lanes(gla_head_norm_w),
                                jnp.zeros((2, LANES), F32)], axis=0)
    in_cols = 2 * POOL_WIDTH // N_DEV
    pool_in, pool_gw, pool_out, small_all = _comm_call("pool_weights_all_gather", _gather_rider(
        [pool_in_w[0].astype(BF16), pool_group_w[0].astype(BF16), pool_out_w[0].astype(BF16), small_in],
        [(D_MODEL, 2 * POOL_WIDTH), (POOL_GROUPS, POOL_GROUP_DIM, POOL_GROUP_DIM), (POOL_WIDTH, D_MODEL),
         (N_DEV, 8, LANES)],
        [_dim1_slot(in_cols), _dim1_slot(GROUP_SHARD), _row_slot(row_shard), _lead_slot]))
    pool_gb = jnp.transpose(small_all[:, 0:POOL_GROUPS, :GROUP_SHARD], (1, 0, 2)).reshape(1, POOL_WIDTH)
    gla_gkb = small_all[:, POOL_GROUPS, :KEY_SHARD].reshape(1, GLA_KEY_WIDTH)
    gla_hw = jnp.tile(small_all[:, POOL_GROUPS + 1, :HEAD_V_SHARD].reshape(1, GLA_HEAD_V), (1, GLA_HEADS))
    nw0, nw1, wf = norm_w[0:1], norm_w[1:2], final_norm_w.reshape(1, D_MODEL)
    xs, target = x[0], loss_target[0]

    h1, pool_y, pool_silu, pool_dsilu, pooled, mixed, gla_in_parts, gkw_parts, gla_out = _pool_fwd_call(
        xs, nw0, pool_in, pool_gw, pool_gb, pool_scale, pool_out, _gather_rider(
            [jnp.transpose(gla_in_w[0]).astype(BF16), gla_gk_w[0].astype(BF16), gla_out_w[0].astype(BF16)],
            [(N_DEV, col_shard, D_MODEL), (N_DEV, GLA_GATE_RANK, KEY_SHARD), (GLA_VALUE_WIDTH, D_MODEL)],
            [_lead_slot, _lead_slot, _row_slot(row_shard)], GATHER_RELAY_STEP))
    gla_in = gla_in_parts.reshape(GLA_IN_WIDTH, D_MODEL)
    gla_gkw = jnp.pad(jnp.transpose(gkw_parts, (1, 0, 2)).reshape(GLA_GATE_RANK, GLA_KEY_WIDTH),
                      ((0, GLA_LOW_PAD - GLA_GATE_RANK), (0, 0)))
    dh2, proj, o, states, scores, loss_part, dwf = _gla_fwd_call(h1, nw1, gla_in, gla_gkw, gla_gkb, gla_hw, gla_out,
                                                                 wf, target)

    dproj, d_gla_out, dhw, dgkw, dgkb = _gla_bwd_call(dh2, proj, o, states, scores, gla_gkw, gla_gkb, gla_hw,
                                                      gla_out)
    dh1, d_gla_in, dnw1 = _inproj_bwd_call("gla_in_bwd", dproj, h1, nw1, gla_in, dh2, transposed=True)
    slabs = col_shard * D_MODEL // (BF16_ROWS * LANES)
    gla_in_send = d_gla_in.reshape(N_DEV, slabs, BF16_ROWS, LANES)
    dp, d_pool_out, dgw, dgb, dsc, landed_gla_in, landed_gla_out = _pool_bwd_call(
        dh1, pool_y, pool_silu, pool_dsilu, pooled, mixed, pool_gw, pool_scale, pool_out,
        _two_level_rider([gla_in_send, d_gla_out], [(slabs, BF16_ROWS, LANES), (row_shard, D_MODEL)],
                         [_lead_slot, _row_slot(row_shard)], TWO_LEVEL_ADD_STEP, TWO_LEVEL_RELAY_STEP))
    grad_x, d_pool_in, dnw0 = _inproj_bwd_call("pool_in_bwd", dp, xs, nw0, pool_in, dh1)

    wide = jnp.concatenate([
        dnw0, dnw1, dsc, dwf, jnp.pad(loss_part[0:1, 0:1], ((0, 0), (0, D_MODEL - 1))),
        jnp.zeros((WIDE_ROWS - 5, D_MODEL), F32)], axis=0)

    def rows8(a):
        return jnp.pad(lanes(a), ((0, 0), (0, -a.shape[1] % 8), (0, 0)))

    narrow = jnp.concatenate([
        rows8(jnp.transpose(dgb.reshape(POOL_GROUPS, N_DEV, GROUP_SHARD), (1, 0, 2))),
        rows8(jnp.transpose(dgkw[:GLA_GATE_RANK].reshape(GLA_GATE_RANK, N_DEV, KEY_SHARD), (1, 0, 2))),
        rows8(dgkb.reshape(N_DEV, 1, KEY_SHARD)),
        rows8(dhw.reshape(GLA_HEADS, GLA_HEAD_V).sum(axis=0).reshape(N_DEV, 1, HEAD_V_SHARD)),
    ], axis=1)
    last_exchange = _JoinedRider(
        _two_level_rider([d_pool_in, d_pool_out, dgw],
                         [(D_MODEL, in_cols), (row_shard, D_MODEL), (POOL_GROUPS, GROUP_SHARD, POOL_GROUP_DIM)],
                         [_dim1_slot(in_cols), _row_slot(row_shard), _dim1_slot(GROUP_SHARD)]),
        _exchange_rider([wide, narrow], [(WIDE_ROWS, D_MODEL), (NARROW_ROWS, LANES)],
                        [lambda ref, d: ref, _lead_slot]))

    res = {}
    as_slabs = lambda t: jnp.transpose(t[0]).reshape(slabs, BF16_ROWS, LANES)
    *outs, landed_pool_in, landed_pool_out, landed_gw, landed_wide, landed_narrow = _adamw_slabs_call(
        "adamw_gla_in_w", landed_gla_in, as_slabs(gla_in_w), as_slabs(m_gla_in_w), as_slabs(v_gla_in_w),
        last_exchange)
    res["gla_in_w"] = [jnp.transpose(t.reshape(col_shard, D_MODEL))[None] for t in outs]
    rest = [("pool_in_w", landed_pool_in, (D_MODEL, in_cols)),
            ("pool_group_w", landed_gw, (POOL_GROUPS * GROUP_SHARD, POOL_GROUP_DIM)),
            ("pool_out_w", landed_pool_out, (row_shard, D_MODEL)), ("gla_out_w", landed_gla_out, (row_shard, D_MODEL))]
    updates = _adamw_group_call("adamw_matrices", [
        (parts.reshape((parts.shape[0],) + shape), w[name].reshape(shape), m[name].reshape(shape),
         v[name].reshape(shape)) for name, parts, shape in rest])
    for (name, _, _), outs in zip(rest, updates):
        res[name] = [t.reshape(w[name].shape) for t in outs]
    small_shapes ={"norm_w": (2, D_MODEL), "pool_scale": (1, D_MODEL), "final_norm_w": (1, D_MODEL),
                    "pool_group_b": (POOL_GROUPS, GROUP_SHARD), "gla_gk_w": (GLA_GATE_RANK, KEY_SHARD),
                    "gla_gk_b": (1, KEY_SHARD), "gla_head_norm_w": (1, HEAD_V_SHARD)}
    as_small = lambda t: {n: t[n].reshape(s) for n, s in small_shapes.items()}
    loss, *small_outs = _small_adamw_call(landed_wide, landed_narrow, as_small(w), as_small(m), as_small(v))
    for name in small_shapes:
        res[name] = [t[name].reshape(w[name].shape) for t in small_outs]
    order = ("norm_w", "pool_in_w", "pool_group_w", "pool_group_b", "pool_scale", "pool_out_w", "gla_in_w",
             "gla_gk_w", "gla_gk_b", "gla_head_norm_w", "gla_out_w", "final_norm_w")
    return (loss.reshape(()), grad_x[None], *[res[n][0] for n in order], *[res[n][1] for n in order],
            *[res[n][2] for n in order], *[res[n][3] for n in order])
```

```python
import jax
import jax.numpy as jnp
from jax import lax
from jax.experimental import pallas as pl
from jax.experimental.pallas import tpu as pltpu

F32 = jnp.float32
BF16 = jnp.bfloat16
MESH = pl.DeviceIdType.MESH

N_DEV = 8
D_MODEL = 1024
POOL_WIDTH = 1024
POOL_GROUPS = 4
POOL_GROUP_DIM = 256
POOL_HALO = 16
GLA_HEADS = 4
GLA_HEAD_K = 128
GLA_HEAD_V = 256
GLA_KEY_WIDTH = 512
GLA_VALUE_WIDTH = 1024
GLA_GATE_RANK = 16
GLA_IN_WIDTH = 3088
GLA_IN_PAD = 3200
GLA_SAVED_Z = GLA_IN_PAD
GLA_SAVED_C = GLA_SAVED_Z + 512
GLA_SAVED_WIDTH = GLA_SAVED_C + 512
GLA_LOW_PAD = 128
GLA_QKVG_WIDTH = 3072
CHUNK = 64
GATE_NORMALIZER = 16.0
RMS_EPS = 1e-6
Q_SCALE = GLA_HEAD_K ** -0.5

ADAM_LR = 0.001
ADAM_B1 = 0.9
ADAM_B2 = 0.999
ADAM_EPS = 1e-08
ADAM_WD = 0.01
ADAM_STEP = 10

LANES = 128
BF16_ROWS = 16
VMEM_LIMIT = 56 * 1024 * 1024
ROW_TILE = 256
GLA_FWD_ROW_TILE = 512
MATMUL_ROW_TILE = 512
OUT_GATHER_RELAY_STEP = 2
TWO_LEVEL_ADD_STEP = 1
TWO_LEVEL_RELAY_STEP = 4


def _dot_nn(a, b):
    return lax.dot_general(a, b, (((1,), (0,)), ((), ())), preferred_element_type=F32)


def _dot_nt(a, b):
    return lax.dot_general(a, b, (((1,), (1,)), ((), ())), preferred_element_type=F32)


def _dot_tn(a, b):
    return lax.dot_general(a, b, (((0,), (0,)), ((), ())), preferred_element_type=F32)


def _rms(x):
    rstd = lax.rsqrt(jnp.mean(x * x, axis=-1, keepdims=True) + RMS_EPS)
    return x * rstd, rstd


def _rms_bwd(dxhat, xhat, rstd):
    return rstd * (dxhat - xhat * jnp.mean(dxhat * xhat, axis=-1, keepdims=True))


def _sigmoid(x):
    return 1.0 / (1.0 + jnp.exp(-x))


def _params(sem=("arbitrary",)):
    return pltpu.CompilerParams(dimension_semantics=sem, vmem_limit_bytes=VMEM_LIMIT)


def _full(shape):
    return pl.BlockSpec(shape, lambda i: (0,) * len(shape))


def _const(shape):
    return pl.BlockSpec(shape, lambda i: (0,) * len(shape), pipeline_mode=pl.Buffered(1))


def _window_sums(ext, forward):
    n = ext.shape[0]
    outs = []
    for g in range(POOL_GROUPS):
        s = ext[:, g * POOL_GROUP_DIM:(g + 1) * POOL_GROUP_DIM]
        for k in range(g + 1):
            shift = (1 << k) if forward else n - (1 << k)
            s = s + pltpu.roll(s, shift, axis=0)
        outs.append(s[:n - POOL_HALO])
    return outs


def _inv_count(row0, tm):
    row = row0 + lax.broadcasted_iota(jnp.int32, (tm, 1), 0)
    return [1.0 / jnp.minimum(row + 1, 2 << g).astype(F32) for g in range(POOL_GROUPS)]


def _pool_mix(u, u_prev, row0, gw_ref, gb):
    tm = u.shape[0]
    sums = _window_sums(jnp.concatenate([u, u_prev], axis=0), True)
    inv = _inv_count(row0, tm)
    pooled, mixed = [], []
    for g in range(POOL_GROUPS):
        ug = u[:, g * POOL_GROUP_DIM:(g + 1) * POOL_GROUP_DIM]
        pg = (sums[g] * inv[g] - ug).astype(BF16)
        pooled.append(pg)
        mixed.append(_dot_nn(pg, gw_ref[g]))
    return pooled, jnp.concatenate(mixed, axis=1) + gb


def _pool_fwd_call(x, nw, w_in, gw, gb, sc, out_gather, next_gather):
    seq = x.shape[0]
    tm = min(MATMUL_ROW_TILE, seq)
    nt = seq // tm
    riders = _JoinedRider(out_gather, next_gather)

    def first_pass(t, x_ref, nw_ref, win_ref, gw_ref, gb_ref, sc_ref, y_ref, silu_ref, dsilu_ref, pooled_ref,
                   mixed_ref, halo_ref, y_all):
        xhat, _ = _rms(x_ref[...])
        n = (xhat * nw_ref[...]).astype(BF16)
        p = _dot_nn(n, win_ref[...])
        u = p[:, :POOL_WIDTH]
        gate = p[:, POOL_WIDTH:]
        sg = _sigmoid(gate)
        silu = gate * sg
        silu_ref[...] = silu
        dsilu_ref[...] = sg * (1.0 + gate * (1.0 - sg))
        pooled, mixed = _pool_mix(u, halo_ref[...], t * tm, gw_ref, gb_ref[...])
        pooled_ref[...] = jnp.concatenate(pooled, axis=1)
        mixed_ref[...] = mixed
        halo_ref[...] = u[tm - POOL_HALO:, :]
        y = (mixed * sc_ref[...] * silu).astype(BF16)
        y_ref[...] = y
        y_all[pl.ds(pl.multiple_of(t * tm, tm), tm), :] = y

    def body(*refs):
        own, comm = _split_refs(refs, 6, 6, 3, riders)
        x_ref, nw_ref, win_ref, gw_ref, gb_ref, sc_ref = own[:6]
        h_ref, saved = own[6], own[7:12]
        halo_ref, y_all, wout_ref = own[12:]
        i = pl.program_id(0)

        @pl.when(i == 0)
        def _():
            halo_ref[...] = jnp.zeros_like(halo_ref)
            comm.first.start()

        @pl.when(i == min(OUT_GATHER_RELAY_STEP, nt - 1))
        def _():
            comm.first.forward()
            comm.second.start()

        @pl.when(i < nt)
        def _():
            first_pass(i, x_ref, nw_ref, win_ref, gw_ref, gb_ref, sc_ref, *saved, halo_ref, y_all)

        @pl.when(i == nt - 1)
        def _():
            comm.first.finish()
            pltpu.sync_copy(comm.first.out_refs[0], wout_ref)

        pl.when(i == nt)(comm.second.forward)

        @pl.when(i >= nt)
        def _():
            rows = pl.ds(pl.multiple_of((i - nt) * tm, tm), tm)
            h_ref[...] = x_ref[...] + _dot_nn(y_all[rows, :], wout_ref[...])

        pl.when(i == 2 * nt - 1)(comm.second.finish)

    first = pl.BlockSpec((tm, D_MODEL), lambda i: (jnp.minimum(i, nt - 1), 0))
    return pl.pallas_call(
        body, name="pool_fwd", grid=(2 * nt,),
        in_specs=_extend([pl.BlockSpec((tm, D_MODEL), lambda i: (i % nt, 0)), _const((1, D_MODEL)),
                          _const((D_MODEL, 2 * POOL_WIDTH)), _const((POOL_GROUPS, POOL_GROUP_DIM, POOL_GROUP_DIM)),
                          _const((1, POOL_WIDTH)), _const((1, POOL_WIDTH))], riders, "in_specs"),
        out_specs=_extend([pl.BlockSpec((tm, D_MODEL), lambda i: (jnp.maximum(i - nt, 0), 0))] + [first] * 5,
                          riders, "out_specs"),
        out_shape=_extend([jax.ShapeDtypeStruct((seq, D_MODEL), F32), jax.ShapeDtypeStruct((seq, POOL_WIDTH), BF16),
                           jax.ShapeDtypeStruct((seq, POOL_WIDTH), F32), jax.ShapeDtypeStruct((seq, POOL_WIDTH), F32),
                           jax.ShapeDtypeStruct((seq, POOL_WIDTH), BF16),
                           jax.ShapeDtypeStruct((seq, POOL_WIDTH), F32)], riders, "out_shape"),
        scratch_shapes=_extend([pltpu.VMEM((POOL_HALO, POOL_WIDTH), F32), pltpu.VMEM((seq, POOL_WIDTH), BF16),
                                pltpu.VMEM((POOL_WIDTH, D_MODEL), BF16)], riders, "scratch"),
        compiler_params=_params(),
    )(x, nw, w_in, gw, gb, sc, *riders.arrays)


def _pool_bwd_call(dh, y, silu, dsilu, pooled, mixed, gw, sc, w_out, rider=None):
    seq = dh.shape[0]
    tm = min(MATMUL_ROW_TILE, seq)
    nt = seq // tm

    def main(dh_ref, y_ref, silu_ref, dsilu_ref, pooled_ref, mixed_ref, gw_ref, sc_ref, wout_ref,
             dp_ref, dwout_hbm, dgw_hbm, dgb_ref, dsc_ref, carry_ref, dwout_acc, dgw_acc, dwout_stage, dgw_stage):
        i = pl.program_id(0)
        t = nt - 1 - i

        @pl.when(i == 0)
        def _():
            carry_ref[...] = jnp.zeros_like(carry_ref)
            dwout_acc[...] = jnp.zeros_like(dwout_acc)
            dgw_acc[...] = jnp.zeros_like(dgw_acc)
            dgb_ref[...] = jnp.zeros_like(dgb_ref)
            dsc_ref[...] = jnp.zeros_like(dsc_ref)

        silu = silu_ref[...]
        pooled = [pooled_ref[:, g * POOL_GROUP_DIM:(g + 1) * POOL_GROUP_DIM] for g in range(POOL_GROUPS)]
        sc = sc_ref[...]
        dhb = dh_ref[...].astype(BF16)
        dwout_acc[...] += _dot_tn(y_ref[...], dhb)
        dy = _dot_nt(dhb, wout_ref[...])
        dmixed = dy * sc * silu
        dy_mixed = dy * mixed_ref[...]
        dsc_ref[...] += jnp.sum(dy_mixed * silu, axis=0, keepdims=True)
        dgate = dy_mixed * sc * dsilu_ref[...]
        dgb_ref[...] += jnp.sum(dmixed, axis=0, keepdims=True)
        inv = _inv_count(t * tm, tm)
        dpooled, scaled = [], []
        for g in range(POOL_GROUPS):
            dmg = dmixed[:, g * POOL_GROUP_DIM:(g + 1) * POOL_GROUP_DIM].astype(BF16)
            dgw_acc[g] += _dot_tn(pooled[g], dmg)
            dpg = _dot_nt(dmg, gw_ref[g])
            dpooled.append(dpg)
            scaled.append(dpg * inv[g])
        r = jnp.concatenate(scaled, axis=1)
        sums = _window_sums(jnp.concatenate([r, carry_ref[...]], axis=0), False)
        carry_ref[...] = r[:POOL_HALO, :]
        du = jnp.concatenate([sums[g] - dpooled[g] for g in range(POOL_GROUPS)], axis=1)
        dp_ref[...] = jnp.concatenate([du, dgate], axis=1).astype(BF16)

        @pl.when(i == nt - 1)
        def _():
            dwout_stage[...] = dwout_acc[...].astype(BF16)
            dgw_stage[...] = dgw_acc[...].astype(BF16)
            pltpu.sync_copy(dwout_stage, dwout_hbm)
            pltpu.sync_copy(dgw_stage, dgw_hbm)

    def body(*refs):
        own, comm = _split_refs(refs, 9, 5, 5, rider)
        _ride_before(comm, pl.program_id(0), nt)
        main(*own)
        _ride_after(comm, pl.program_id(0), nt)

    rev = lambda i: (nt - 1 - i, 0)
    return pl.pallas_call(
        body, name="pool_bwd", grid=(nt,),
        in_specs=_extend([pl.BlockSpec((tm, D_MODEL), rev)] * 6
                         + [_const((POOL_GROUPS, POOL_GROUP_DIM, POOL_GROUP_DIM)), _const((1, POOL_WIDTH)),
                            _const((POOL_WIDTH, D_MODEL))], rider, "in_specs"),
        out_specs=_extend([pl.BlockSpec((tm, 2 * POOL_WIDTH), rev), pl.BlockSpec(memory_space=pl.ANY),
                           pl.BlockSpec(memory_space=pl.ANY), _full((1, POOL_WIDTH)), _full((1, POOL_WIDTH))],
                          rider, "out_specs"),
        out_shape=_extend([jax.ShapeDtypeStruct((seq, 2 * POOL_WIDTH), BF16),
                           jax.ShapeDtypeStruct((POOL_WIDTH, D_MODEL), BF16),
                           jax.ShapeDtypeStruct((POOL_GROUPS, POOL_GROUP_DIM, POOL_GROUP_DIM), BF16),
                           jax.ShapeDtypeStruct((1, POOL_WIDTH), F32), jax.ShapeDtypeStruct((1, POOL_WIDTH), F32)],
                          rider, "out_shape"),
        scratch_shapes=_extend([pltpu.VMEM((POOL_HALO, POOL_WIDTH), F32), pltpu.VMEM((POOL_WIDTH, D_MODEL), F32),
                                pltpu.VMEM((POOL_GROUPS, POOL_GROUP_DIM, POOL_GROUP_DIM), F32),
                                pltpu.VMEM((POOL_WIDTH, D_MODEL), BF16),
                                pltpu.VMEM((POOL_GROUPS, POOL_GROUP_DIM, POOL_GROUP_DIM), BF16)], rider, "scratch"),
        compiler_params=_params(),
    )(dh, y, silu, dsilu, pooled, mixed, gw, sc, w_out, *_extend([], rider, "arrays"))


def _rows_then_zeros(ref, lo, hi, rows):
    part = ref[lo:hi, :]
    return jnp.concatenate([part, jnp.zeros((rows - (hi - lo), part.shape[1]), part.dtype)], axis=0)


def _inproj_bwd_call(name, dproj, h_in, nw, w_in, dres, rider=None, transposed=False):
    seq = h_in.shape[0]
    width = dproj.shape[1]
    w_shape = tuple(w_in.shape)
    acc_shape = (width, D_MODEL) if transposed else w_shape
    whole = w_shape[0] // LANES * LANES
    tm = min(MATMUL_ROW_TILE, seq)
    nt = seq // tm

    def main(dproj_ref, h_ref, nw_ref, win_ref, dres_ref, dh_ref, dw_hbm, dnw_ref, dw_acc, dw_stage):
        i = pl.program_id(0)

        @pl.when(i == 0)
        def _():
            dw_acc[...] = jnp.zeros_like(dw_acc)
            dnw_ref[...] = jnp.zeros_like(dnw_ref)

        dpb = dproj_ref[...]
        if transposed:
            dn = _dot_nn(dpb[:, :whole], win_ref[0:whole, :])
            if whole < w_shape[0]:
                dn = dn + _dot_nn(dpb[:, whole:], _rows_then_zeros(win_ref, whole, w_shape[0], width - whole))
        else:
            dn = _dot_nt(dpb, win_ref[...])
        xhat, rstd = _rms(h_ref[...])
        nw_row = nw_ref[...]
        n = (xhat * nw_row).astype(BF16)
        dw_acc[...] += _dot_tn(dpb, n) if transposed else _dot_tn(n, dpb)
        dnw_ref[...] += jnp.sum(dn * xhat, axis=0, keepdims=True)
        dh_ref[...] = _rms_bwd(dn * nw_row, xhat, rstd) + dres_ref[...]

        @pl.when(i == nt - 1)
        def _():
            dw_stage[...] = dw_acc[...].astype(BF16)
            pltpu.sync_copy(dw_stage.at[pl.ds(0, w_shape[0])], dw_hbm)

    def body(*refs):
        own, comm = _split_refs(refs, 5, 3, 2, rider)
        _ride_before(comm, pl.program_id(0), nt)
        main(*own)
        _ride_after(comm, pl.program_id(0), nt)

    row = lambda i: (i, 0)
    return pl.pallas_call(
        body, name=name, grid=(nt,),
        in_specs=_extend([pl.BlockSpec((tm, width), row), pl.BlockSpec((tm, D_MODEL), row), _const((1, D_MODEL)),
                          _const(w_shape), pl.BlockSpec((tm, D_MODEL), row)], rider, "in_specs"),
        out_specs=_extend([pl.BlockSpec((tm, D_MODEL), row), pl.BlockSpec(memory_space=pl.ANY),
                           _full((1, D_MODEL))], rider, "out_specs"),
        out_shape=_extend([jax.ShapeDtypeStruct((seq, D_MODEL), F32), jax.ShapeDtypeStruct(w_shape, BF16),
                           jax.ShapeDtypeStruct((1, D_MODEL), F32)], rider, "out_shape"),
        scratch_shapes=_extend([pltpu.VMEM(acc_shape, F32), pltpu.VMEM(acc_shape, BF16)], rider, "scratch"),
        compiler_params=_params(),
    )(dproj, h_in, nw, w_in, dres, *_extend([], rider, "arrays"))


def _chunk_scan(x, reverse):
    n = x.shape[0]
    pos = lax.broadcasted_iota(jnp.int32, (n, 1), 0) & (CHUNK - 1)
    k = 1
    while k < CHUNK:
        if reverse:
            x = x + jnp.where(pos < CHUNK - k, pltpu.roll(x, n - k, axis=0), 0.0)
        else:
            x = x + jnp.where(pos >= k, pltpu.roll(x, k, axis=0), 0.0)
        k *= 2
    return x


def _chunk_rows(j):
    return slice(j * CHUNK, (j + 1) * CHUNK)


def _kcols(h):
    return slice(h * GLA_HEAD_K, (h + 1) * GLA_HEAD_K)


def _vcols(h):
    return slice(h * GLA_HEAD_V, (h + 1) * GLA_HEAD_V)


def _chunk_masks(tm):
    idx_t = lax.broadcasted_iota(jnp.int32, (tm, tm), 0)
    idx_s = lax.broadcasted_iota(jnp.int32, (tm, tm), 1)
    same_chunk = (idx_t ^ idx_s) < CHUNK
    return same_chunk & (idx_t >= idx_s), same_chunk & (idx_t < idx_s)


class _GlaTerms:
    def __init__(self, kc, q, k, v, low_b, gkw_ref, gkb_ref, masks, saved=None):
        tm = q.shape[0]
        self.q = q * Q_SCALE
        self.k = k
        if saved is None:
            self.z = _dot_nn(low_b, gkw_ref[:, kc]) + gkb_ref[:, kc]
            log_g = (jnp.minimum(self.z, 0.0) - jnp.log(1.0 + jnp.exp(-jnp.abs(self.z)))) / GATE_NORMALIZER
            self.c = _chunk_scan(log_g, False)
        else:
            self.z, self.c = saved
        is_last = lax.broadcasted_iota(jnp.int32, (CHUNK, 1), 0) == CHUNK - 1
        self.c_last = [jnp.sum(jnp.where(is_last, self.c[_chunk_rows(j), :], 0.0), axis=0, keepdims=True)
                       for j in range(tm // CHUNK)]
        c_last_rows = jnp.concatenate([jnp.broadcast_to(r, (CHUNK, r.shape[1])) for r in self.c_last], axis=0)
        self.e_pos = jnp.exp(self.c)
        self.e_neg = jnp.exp(-self.c)
        self.e_rest = jnp.exp(c_last_rows - self.c)
        self.a_b = (self.q * self.e_pos).astype(BF16)
        self.b_b = (self.k * self.e_neg).astype(BF16)
        self.cn_b = (self.q * self.e_neg).astype(BF16)
        self.dp_b = (self.k * self.e_pos).astype(BF16)
        self.kd_b = (self.k * self.e_rest).astype(BF16)
        self.v_b = v.astype(BF16)
        self.lower, self.upper = masks

    def scores(self, kc=slice(None)):
        fwd = _dot_nt(self.a_b[:, kc], self.b_b[:, kc])
        bwd = _dot_nt(self.cn_b[:, kc], self.dp_b[:, kc])
        return jnp.where(self.lower, fwd, jnp.where(self.upper, bwd, 0.0)).astype(BF16)


def _gla_fwd_call(h1, nw, w_in, gkw, gkb, hw, w_out, wf, target):
    seq = h1.shape[0]
    tm = min(GLA_FWD_ROW_TILE, seq)
    nt = seq // tm
    cpt = tm // CHUNK
    n_chunks = seq // CHUNK

    def body(h_ref, nw_ref, win_ref, gkw_ref, gkb_ref, hw_ref, wout_ref, wf_ref, tgt_ref,
             dh2_ref, proj_ref, o_ref, st_ref, scores_ref, loss_ref, dwf_ref, state_ref):
        i = pl.program_id(0)

        @pl.when(i == 0)
        def _():
            state_ref[...] = jnp.zeros_like(state_ref)
            loss_ref[...] = jnp.zeros_like(loss_ref)
            dwf_ref[...] = jnp.zeros_like(dwf_ref)

        ht = h_ref[...]
        xhat, _ = _rms(ht)
        n = (xhat * nw_ref[...]).astype(BF16)
        sections = {}
        for name, lo, hi in (("low", GLA_QKVG_WIDTH, GLA_IN_PAD), ("qk", 0, 2 * GLA_KEY_WIDTH),
                             ("v", 2 * GLA_KEY_WIDTH, GLA_QKVG_WIDTH - GLA_VALUE_WIDTH),
                             ("gate", GLA_QKVG_WIDTH - GLA_VALUE_WIDTH, GLA_QKVG_WIDTH)):
            rows = (win_ref[lo:hi, :] if hi <= GLA_IN_WIDTH
                    else _rows_then_zeros(win_ref, lo, GLA_IN_WIDTH, hi - lo))
            sections[name] = _dot_nt(n, rows)
            proj_ref[:, lo:hi] = sections[name]
        low_b = sections["low"].astype(BF16)
        masks = _chunk_masks(tm)
        on_heads = []
        for h in range(GLA_HEADS):
            kc, vc = _kcols(h), _vcols(h)
            g = _GlaTerms(kc, sections["qk"][:, kc], sections["qk"][:, GLA_KEY_WIDTH:][:, kc], sections["v"][:, vc],
                          low_b, gkw_ref, gkb_ref, masks)
            srows = slice(h * GLA_HEAD_V, (h + 1) * GLA_HEAD_V)
            scores = g.scores()
            for b in range(tm // ROW_TILE):
                part = slice(b * ROW_TILE, (b + 1) * ROW_TILE)
                scores_ref[part, h * ROW_TILE:(h + 1) * ROW_TILE] = scores[part, part]
            o_intra = _dot_nn(scores, g.v_b)
            state = state_ref[srows, :]
            o_rows = []
            for j in range(cpt):
                r = _chunk_rows(j)
                st_ref[j, srows, :] = state
                o_rows.append(o_intra[r] + _dot_nt(g.a_b[r], state.astype(BF16)))
                decay = jnp.exp(g.c_last[j])
                state = state * decay + _dot_tn(g.v_b[r], g.kd_b[r])
            state_ref[srows, :] = state
            o_head = jnp.concatenate(o_rows, axis=0)
            o_ref[:, vc] = o_head
            proj_ref[:, GLA_SAVED_Z + kc.start:GLA_SAVED_Z + kc.stop] = g.z
            proj_ref[:, GLA_SAVED_C + kc.start:GLA_SAVED_C + kc.stop] = g.c
            on_heads.append(_rms(o_head)[0])
        gate = sections["gate"]
        on = jnp.concatenate(on_heads, axis=1) * hw_ref[...]
        y = (on * (gate * _sigmoid(gate))).astype(BF16)
        h2 = ht + _dot_nn(y, wout_ref[...])
        xhat2, rstd2 = _rms(h2)
        wf_row = wf_ref[...]
        err = xhat2 * wf_row - tgt_ref[...]
        loss_ref[...] += 0.5 * jnp.sum(err * err) / D_MODEL
        dout = err * (1.0 / D_MODEL)
        dwf_ref[...] += jnp.sum(dout * xhat2, axis=0, keepdims=True)
        dh2_ref[...] = _rms_bwd(dout * wf_row, xhat2, rstd2)

    row = lambda i: (i, 0)
    return pl.pallas_call(
        body, name="gla_fwd", grid=(nt,),
        in_specs=[pl.BlockSpec((tm, D_MODEL), row), _const((1, D_MODEL)), _const((GLA_IN_WIDTH, D_MODEL)),
                  _const((GLA_LOW_PAD, GLA_KEY_WIDTH)), _const((1, GLA_KEY_WIDTH)), _const((1, GLA_VALUE_WIDTH)),
                  _const((GLA_VALUE_WIDTH, D_MODEL)), _const((1, D_MODEL)), pl.BlockSpec((tm, D_MODEL), row)],
        out_specs=[pl.BlockSpec((tm, D_MODEL), row), pl.BlockSpec((tm, GLA_SAVED_WIDTH), row),
                   pl.BlockSpec((tm, GLA_VALUE_WIDTH), row),
                   pl.BlockSpec((cpt, GLA_VALUE_WIDTH, GLA_HEAD_K), lambda i: (i, 0, 0)),
                   pl.BlockSpec((tm, GLA_HEADS * ROW_TILE), row), _full((8, LANES)), _full((1, D_MODEL))],
        out_shape=[jax.ShapeDtypeStruct((seq, D_MODEL), F32), jax.ShapeDtypeStruct((seq, GLA_SAVED_WIDTH), F32),
                   jax.ShapeDtypeStruct((seq, GLA_VALUE_WIDTH), F32),
                   jax.ShapeDtypeStruct((n_chunks, GLA_VALUE_WIDTH, GLA_HEAD_K), F32),
                   jax.ShapeDtypeStruct((seq, GLA_HEADS * ROW_TILE), BF16),
                   jax.ShapeDtypeStruct((8, LANES), F32), jax.ShapeDtypeStruct((1, D_MODEL), F32)],
        scratch_shapes=[pltpu.VMEM((GLA_VALUE_WIDTH, GLA_HEAD_K), F32)],
        compiler_params=_params(),
    )(h1, nw, w_in, gkw, gkb, hw, w_out, wf, target)


def _gla_bwd_call(dh2, proj, o, states, scores, gkw, gkb, hw, w_out):
    seq = dh2.shape[0]
    tm = ROW_TILE
    nt = seq // tm
    cpt = tm // CHUNK

    def body(dh_ref, proj_ref, o_ref, st_ref, scores_ref, gkw_ref, gkb_ref, hw_ref, wout_ref,
             dproj_ref, dwout_hbm, dhw_ref, dgkw_ref, dgkb_ref, dstate_ref, dwout_acc, dwout_stage):
        i = pl.program_id(0)

        @pl.when(i == 0)
        def _():
            dstate_ref[...] = jnp.zeros_like(dstate_ref)
            dwout_acc[...] = jnp.zeros_like(dwout_acc)
            dhw_ref[...] = jnp.zeros_like(dhw_ref)
            dgkw_ref[...] = jnp.zeros_like(dgkw_ref)
            dgkb_ref[...] = jnp.zeros_like(dgkb_ref)

        dhb = dh_ref[...].astype(BF16)
        dy = _dot_nt(dhb, wout_ref[...])
        v0, g0 = 2 * GLA_KEY_WIDTH, GLA_QKVG_WIDTH - GLA_VALUE_WIDTH
        gate = proj_ref[:, g0:GLA_QKVG_WIDTH]
        low_b = proj_ref[:, GLA_QKVG_WIDTH:GLA_IN_PAD].astype(BF16)
        o = o_ref[...]
        hw_row = hw_ref[...]
        sg = _sigmoid(gate)
        silu = gate * sg
        don = dy * silu
        on_parts, do_parts, dhw_parts = [], [], []
        for h in range(GLA_HEADS):
            vc = _vcols(h)
            xh, rs = _rms(o[:, vc])
            on_parts.append(xh * hw_row[:, vc])
            dhw_parts.append(jnp.sum(don[:, vc] * xh, axis=0, keepdims=True))
            do_parts.append(_rms_bwd(don[:, vc] * hw_row[:, vc], xh, rs).astype(BF16))
        on = jnp.concatenate(on_parts, axis=1)
        dwout_acc[...] += _dot_tn((on * silu).astype(BF16), dhb)
        dhw_ref[...] += jnp.concatenate(dhw_parts, axis=1)
        dproj_ref[:, g0:GLA_QKVG_WIDTH] = (dy * on * (sg * (1.0 + gate * (1.0 - sg)))).astype(BF16)

        last_row = lax.broadcasted_iota(jnp.int32, (CHUNK, 1), 0) == CHUNK - 1
        g = _GlaTerms(slice(0, GLA_KEY_WIDTH), proj_ref[:, :GLA_KEY_WIDTH], proj_ref[:, GLA_KEY_WIDTH:v0],
                      proj_ref[:, v0:g0], low_b, gkw_ref, gkb_ref, _chunk_masks(tm),
                      saved=(proj_ref[:, GLA_SAVED_Z:GLA_SAVED_C], proj_ref[:, GLA_SAVED_C:GLA_SAVED_WIDTH]))
        dc_h = []
        for h in range(GLA_HEADS):
            kc, vc = _kcols(h), _vcols(h)
            k_cols = slice(GLA_KEY_WIDTH + kc.start, GLA_KEY_WIDTH + kc.stop)
            v_cols = slice(v0 + vc.start, v0 + vc.stop)
            do_h = do_parts[h]
            srows = slice(h * GLA_HEAD_V, (h + 1) * GLA_HEAD_V)
            scores = scores_ref[:, h * ROW_TILE:(h + 1) * ROW_TILE]
            dscores = _dot_nt(do_h, g.v_b[:, vc])
            dfwd = jnp.where(g.lower, dscores, 0.0).astype(BF16)
            dbwd = jnp.where(g.upper, dscores, 0.0).astype(BF16)
            dv_intra = _dot_tn(scores, do_h)
            da_intra = _dot_nn(dfwd, g.b_b[:, kc])
            db = _dot_tn(dfwd, g.a_b[:, kc])
            dcn = _dot_nn(dbwd, g.dp_b[:, kc])
            ddp = _dot_tn(dbwd, g.cn_b[:, kc])
            dstate = dstate_ref[srows, :]
            da_rows, dkd_rows, dv_rows, dcl_rows = [None] * cpt, [None] * cpt, [None] * cpt, [None] * cpt
            for j in reversed(range(cpt)):
                r = _chunk_rows(j)
                state = st_ref[j, srows, :]
                dstate_b = dstate.astype(BF16)
                do_c = do_h[r]
                dv_rows[j] = dv_intra[r] + _dot_nt(g.kd_b[r, kc], dstate_b)
                da_rows[j] = da_intra[r] + _dot_nn(do_c, state.astype(BF16))
                dkd = _dot_nn(g.v_b[r, vc], dstate_b) * g.e_rest[r, kc]
                dkd_rows[j] = dkd
                decay = jnp.exp(g.c_last[j][:, kc])
                dc_last = (jnp.sum(dkd * g.k[r, kc], axis=0, keepdims=True)
                           + decay * jnp.sum(state * dstate, axis=0, keepdims=True))
                dcl_rows[j] = jnp.where(last_row, dc_last, 0.0)
                dstate = _dot_tn(do_c, g.a_b[r, kc]) + dstate * decay
            dstate_ref[srows, :] = dstate
            da = jnp.concatenate(da_rows, axis=0)
            dkd = jnp.concatenate(dkd_rows, axis=0)
            dproj_ref[:, v_cols] = jnp.concatenate(dv_rows, axis=0).astype(BF16)
            q_up, q_down = da * g.e_pos[:, kc], dcn * g.e_neg[:, kc]
            k_up, k_down = ddp * g.e_pos[:, kc], db * g.e_neg[:, kc] + dkd
            dproj_ref[:, kc] = (Q_SCALE * (q_up + q_down)).astype(BF16)
            dproj_ref[:, k_cols] = (k_up + k_down).astype(BF16)
            dc_h.append(g.q[:, kc] * (q_up - q_down) + g.k[:, kc] * (k_up - k_down)
                        + jnp.concatenate(dcl_rows, axis=0))
        dz = _chunk_scan(jnp.concatenate(dc_h, axis=1), True) * (1.0 / GATE_NORMALIZER) * (1.0 - _sigmoid(g.z))
        dzb = dz.astype(BF16)
        dgkb_ref[...] += jnp.sum(dz, axis=0, keepdims=True)
        dgkw_ref[...] += _dot_tn(low_b, dzb)
        dproj_ref[:, GLA_QKVG_WIDTH:] = _dot_nt(dzb, gkw_ref[...]).astype(BF16)

        @pl.when(i == nt - 1)
        def _():
            dwout_stage[...] = dwout_acc[...].astype(BF16)
            pltpu.sync_copy(dwout_stage, dwout_hbm)

    rev = lambda i: (nt - 1 - i, 0)
    return pl.pallas_call(
        body, name="gla_bwd", grid=(nt,),
        in_specs=[pl.BlockSpec((tm, D_MODEL), rev), pl.BlockSpec((tm, GLA_SAVED_WIDTH), rev),
                  pl.BlockSpec((tm, GLA_VALUE_WIDTH), rev),
                  pl.BlockSpec((cpt, GLA_VALUE_WIDTH, GLA_HEAD_K), lambda i: (nt - 1 - i, 0, 0)),
                  pl.BlockSpec((tm, GLA_HEADS * ROW_TILE), rev),
                  _const((GLA_LOW_PAD, GLA_KEY_WIDTH)), _const((1, GLA_KEY_WIDTH)), _const((1, GLA_VALUE_WIDTH)),
                  _const((GLA_VALUE_WIDTH, D_MODEL))],
        out_specs=[pl.BlockSpec((tm, GLA_IN_PAD), rev), pl.BlockSpec(memory_space=pl.ANY),
                   _full((1, GLA_VALUE_WIDTH)), _full((GLA_LOW_PAD, GLA_KEY_WIDTH)), _full((1, GLA_KEY_WIDTH))],
        out_shape=[jax.ShapeDtypeStruct((seq, GLA_IN_PAD), BF16), jax.ShapeDtypeStruct((GLA_VALUE_WIDTH, D_MODEL), BF16),
                   jax.ShapeDtypeStruct((1, GLA_VALUE_WIDTH), F32), jax.ShapeDtypeStruct((GLA_LOW_PAD, GLA_KEY_WIDTH), F32),
                   jax.ShapeDtypeStruct((1, GLA_KEY_WIDTH), F32)],
        scratch_shapes=[pltpu.VMEM((GLA_VALUE_WIDTH, GLA_HEAD_K), F32), pltpu.VMEM((GLA_VALUE_WIDTH, D_MODEL), F32),
                        pltpu.VMEM((GLA_VALUE_WIDTH, D_MODEL), BF16)],
        compiler_params=_params(),
    )(dh2, proj, o, states, scores, gkw, gkb, hw, w_out)


def _position():
    return lax.axis_index("x"), lax.axis_index("y"), lax.axis_index("c")


def _lead_slot(ref, d):
    return ref.at[d]


def _row_slot(rows):
    return lambda ref, d: ref.at[pl.ds(pl.multiple_of(d * rows, rows), rows)]


def _dim1_slot(size):
    return lambda ref, d: ref.at[:, pl.ds(pl.multiple_of(d * size, size), size)]


class _Gather:
    def __init__(self, in_refs, out_refs, slots, send_sems, recv_sems, local_sems):
        self.in_refs, self.out_refs, self.slots = in_refs, out_refs, slots
        self.send_sems, self.recv_sems, self.local_sems = send_sems, recv_sems, local_sems
        self.n = len(in_refs)
        x, y, c = _position()
        self.c = c
        self.me, self.sibling = (x, y, c), (x, y, 1 - c)
        self.near = [(1 - x, y), (x, 1 - y)]
        self.diagonal = (1 - x, 1 - y)
        self.relay_from = (x ^ c, y ^ (1 - c))
        self.relay_to = (x ^ (1 - c), y ^ c)

    def _copy(self, a, k, block, to, from_input=False):
        part = self.slots[a](self.out_refs[a], 4 * block[0] + 2 * block[1] + block[2])
        return pltpu.make_async_remote_copy(
            src_ref=self.in_refs[a] if from_input else part, dst_ref=part,
            send_sem=self.send_sems.at[a, k], recv_sem=self.recv_sems.at[a, k], device_id=to, device_id_type=MESH)

    def _mine(self):
        return [pltpu.make_async_copy(self.in_refs[a], self.slots[a](self.out_refs[a], 4 * self.me[0] + 2 * self.me[1]
                                                                    + self.me[2]), self.local_sems.at[a])
                for a in range(self.n)]

    def _first(self):
        first = [self._copy(a, 0, self.me, self.sibling, True) for a in range(self.n)]
        return first + [self._copy(a, 1 + j, self.me, (*chip, self.c), True)
                        for j, chip in enumerate(self.near) for a in range(self.n)]

    def _relayed(self):
        return [self._copy(a, 3, (*self.relay_from, self.c), (*self.relay_to, self.c)) for a in range(self.n)]

    def _passed(self, j):
        chip = self.near[j] if j < 2 else self.diagonal
        return [self._copy(a, 4 + j, (*chip, self.c), self.sibling) for a in range(self.n)]

    def start(self):
        for cp in self._mine() + self._first():
            cp.start()

    def forward(self):
        for j, chip in enumerate(self.near):
            for a in range(self.n):
                self._copy(a, 1 + j, (*chip, self.c), self.me).wait_recv()
        for cp in self._relayed() + self._passed(0) + self._passed(1):
            cp.start()

    def relay(self):
        pass

    def finish(self):
        for a in range(self.n):
            self._copy(a, 3, (*self.diagonal, self.c), self.me).wait_recv()
        for cp in self._passed(2):
            cp.start()
        for a in range(self.n):
            self._copy(a, 0, self.sibling, self.me).wait_recv()
        for j, chip in enumerate(self.near + [self.diagonal]):
            for a in range(self.n):
                self._copy(a, 4 + j, (*chip, 1 - self.c), self.me).wait_recv()
        for cp in self._first() + self._relayed() + self._passed(0) + self._passed(1) + self._passed(2):
            cp.wait_send()
        for cp in self._mine():
            cp.wait()


class _Exchange:
    def __init__(self, in_refs, out_refs, slots, send_sems, recv_sems, local_sems):
        self.in_refs, self.out_refs, self.slots = in_refs, out_refs, slots
        self.send_sems, self.recv_sems, self.local_sems = send_sems, recv_sems, local_sems
        self.n = len(in_refs)
        self.pos = _position()

    def _copies(self):
        x, y, c = self.pos
        me = 4 * x + 2 * y + c
        mine = [pltpu.make_async_copy(self.slots[a](self.in_refs[a], me), self.out_refs[a].at[me],
                                      self.local_sems.at[a]) for a in range(self.n)]
        remote = []
        for k in range(1, N_DEV):
            px, py, pc = x ^ (k >> 2), y ^ ((k >> 1) & 1), c ^ (k & 1)
            for a in range(self.n):
                remote.append(pltpu.make_async_remote_copy(
                    src_ref=self.slots[a](self.in_refs[a], 4 * px + 2 * py + pc), dst_ref=self.out_refs[a].at[me],
                    send_sem=self.send_sems.at[a, k - 1], recv_sem=self.recv_sems.at[a, k - 1],
                    device_id=(px, py, pc), device_id_type=MESH))
        return mine, remote

    def start(self):
        mine, remote = self._copies()
        for cp in mine + remote:
            cp.start()

    def forward(self):
        pass

    def relay(self):
        pass

    def finish(self):
        mine, remote = self._copies()
        for cp in remote:
            cp.wait_recv()
        for cp in remote:
            cp.wait_send()
        for cp in mine:
            cp.wait()


class _Rider:
    def __init__(self, kind, arrays, out_shapes, slots, scratch=None, forward_step=None):
        self.kind, self.arrays, self.slots = kind, list(arrays), slots
        self.n = len(self.arrays)
        hbm = pl.BlockSpec(memory_space=pl.ANY)
        self.in_specs = [hbm] * self.n
        self.out_specs = [hbm] * self.n
        self.out_shape = [jax.ShapeDtypeStruct(tuple(s), a.dtype) for s, a in zip(out_shapes, self.arrays)]
        self.scratch = scratch if scratch is not None else [
            pltpu.SemaphoreType.DMA((self.n, 7)), pltpu.SemaphoreType.DMA((self.n, 7)),
            pltpu.SemaphoreType.DMA((self.n,))]
        self.forward_step = forward_step
        self.relay_step = None

    def bind(self, in_refs, out_refs, scratch):
        return self.kind(in_refs, out_refs, self.slots, *scratch)


def _gather_rider(shards, full_shapes, slots, forward_step=None):
    return _Rider(_Gather, shards, full_shapes, slots, None, forward_step)


def _exchange_rider(sends, part_shapes, slots):
    return _Rider(_Exchange, sends, [(N_DEV,) + tuple(s) for s in part_shapes], slots)


def _split_refs(refs, n_in, n_out, n_scratch, rider):
    k = rider.n if rider is not None else 0
    ins, r_ins = refs[:n_in], refs[n_in:n_in + k]
    outs, r_outs = refs[n_in + k:n_in + k + n_out], refs[n_in + k + n_out:n_in + 2 * k + n_out]
    rest = refs[n_in + 2 * k + n_out:]
    scratch, r_scratch = rest[:n_scratch], rest[n_scratch:]
    comm = rider.bind(r_ins, r_outs, r_scratch) if rider is not None else None
    if comm is not None:
        comm.forward_step, comm.relay_step = rider.forward_step, rider.relay_step
    return ins + outs + scratch, comm


def _ride_before(comm, i, nt):
    if comm is not None:
        pl.when(i == 0)(comm.start)
        pl.when(i == (nt - 1 if comm.forward_step is None else min(comm.forward_step, nt - 1)))(comm.forward)
        pl.when(i == (nt - 1 if comm.relay_step is None else min(comm.relay_step, nt - 1)))(comm.relay)


def _ride_after(comm, i, nt):
    if comm is not None:
        pl.when(i == nt - 1)(comm.finish)


def _extend(specs, rider, field):
    return list(specs) + (getattr(rider, field) if rider is not None else [])


def _comm_call(name, rider):
    def body(*refs):
        _, comm = _split_refs(refs, 0, 0, 0, rider)
        comm.start()
        comm.forward()
        comm.relay()
        comm.finish()

    return pl.pallas_call(body, name=name, in_specs=rider.in_specs, out_specs=rider.out_specs,
                          out_shape=rider.out_shape, scratch_shapes=rider.scratch,
                          compiler_params=pltpu.CompilerParams(vmem_limit_bytes=VMEM_LIMIT))(*rider.arrays)


N_CHIPS = 4


class _TwoLevel:
    def __init__(self, in_refs, out_refs, slots, *scratch):
        self.in_refs, self.out_refs, self.slots = in_refs, out_refs, slots
        self.n = n = len(in_refs)
        self.own_bufs, self.recv_bufs, self.relay_bufs = scratch[:n], scratch[n:2 * n], scratch[2 * n:3 * n]
        self.swap_send, self.swap_recv, self.local_sems, self.chip_send, self.chip_recv = scratch[3 * n:]
        x, y, c = self.pos = _position()
        self.first = (x ^ (1 - c), y ^ c)
        self.second = (x ^ c, y ^ (1 - c))
        self.chip_index = lambda chip: 2 * chip[0] + chip[1]

    def _swap(self):
        x, y, c = self.pos
        return [pltpu.make_async_remote_copy(
            src_ref=self.slots[a](self.in_refs[a], 2 * q + 1 - c), dst_ref=self.recv_bufs[a].at[q],
            send_sem=self.swap_send.at[a, q], recv_sem=self.swap_recv.at[a, q],
            device_id=(x, y, 1 - c), device_id_type=MESH) for a in range(self.n) for q in range(N_CHIPS)]

    def _mine(self):
        c = self.pos[2]
        return [pltpu.make_async_copy(self.slots[a](self.in_refs[a], 2 * q + c), self.own_bufs[a].at[q],
                                      self.local_sems.at[a, q]) for a in range(self.n) for q in range(N_CHIPS)]

    def _to_chip(self, a, k, src, dst, chip):
        return pltpu.make_async_remote_copy(
            src_ref=src, dst_ref=dst, send_sem=self.chip_send.at[a, k], recv_sem=self.chip_recv.at[a, k],
            device_id=(*chip, self.pos[2]), device_id_type=MESH)

    def _first_wave(self):
        x, y, _ = self.pos
        diagonal = self.chip_index((1 - x, 1 - y))
        passed_on = [self._to_chip(a, 1, self.own_bufs[a].at[diagonal], self.relay_bufs[a], self.first)
                     for a in range(self.n)]
        return passed_on + [self._to_chip(a, 0, self.own_bufs[a].at[self.chip_index(self.first)],
                                          self.out_refs[a].at[1], self.first) for a in range(self.n)]

    def _second_wave(self):
        return [self._to_chip(a, 2, self.own_bufs[a].at[self.chip_index(self.second)], self.out_refs[a].at[2],
                              self.second) for a in range(self.n)]

    def _own(self):
        x, y, _ = self.pos
        return [pltpu.make_async_copy(self.own_bufs[a].at[2 * x + y], self.out_refs[a].at[0],
                                      self.local_sems.at[a, N_CHIPS]) for a in range(self.n)]

    def start(self):
        for cp in self._swap() + self._mine():
            cp.start()

    def forward(self):
        swap, mine = self._swap(), self._mine()
        for a in range(self.n):
            for q in range(N_CHIPS):
                mine[a * N_CHIPS + q].wait()
                swap[a * N_CHIPS + q].wait_recv()
                self.own_bufs[a][q] = (self.own_bufs[a][q].astype(F32)
                                       + self.recv_bufs[a][q].astype(F32)).astype(BF16)
        for cp in self._first_wave() + self._own():
            cp.start()

    def relay(self):
        second = self.chip_index(self.second)
        for a in range(self.n):
            self._to_chip(a, 1, self.relay_bufs[a], self.relay_bufs[a], self.first).wait_recv()
            self.own_bufs[a][second] = (self.own_bufs[a][second].astype(F32)
                                        + self.relay_bufs[a][...].astype(F32)).astype(BF16)
        for cp in self._second_wave():
            cp.start()

    def finish(self):
        for a in range(self.n):
            self._to_chip(a, 0, self.out_refs[a].at[1], self.out_refs[a].at[1], self.first).wait_recv()
            self._to_chip(a, 2, self.out_refs[a].at[2], self.out_refs[a].at[2], self.second).wait_recv()
        for cp in self._first_wave() + self._second_wave() + self._swap():
            cp.wait_send()
        for cp in self._own():
            cp.wait()


def _two_level_rider(sends, part_shapes, slots, forward_step=None, relay_step=None):
    n = len(sends)
    bufs = [pltpu.VMEM((N_CHIPS,) + tuple(s), a.dtype) for s, a in zip(part_shapes, sends)]
    relay_bufs = [pltpu.VMEM(tuple(s), a.dtype) for s, a in zip(part_shapes, sends)]
    scratch = bufs + bufs + relay_bufs + [
        pltpu.SemaphoreType.DMA((n, N_CHIPS)), pltpu.SemaphoreType.DMA((n, N_CHIPS)),
        pltpu.SemaphoreType.DMA((n, N_CHIPS + 1)), pltpu.SemaphoreType.DMA((n, 3)), pltpu.SemaphoreType.DMA((n, 3))]
    rider = _Rider(_TwoLevel, sends, [(3,) + tuple(s) for s in part_shapes], slots, scratch, forward_step)
    rider.relay_step = relay_step
    return rider


class _Joined:
    def __init__(self, first, second):
        self.first, self.second = first, second

    def start(self):
        self.first.start()
        self.second.start()

    def forward(self):
        self.first.forward()
        self.second.forward()

    def relay(self):
        self.first.relay()
        self.second.relay()

    def finish(self):
        self.first.finish()
        self.second.finish()


class _JoinedRider:
    def __init__(self, first, second):
        self.first, self.second = first, second
        self.n = first.n + second.n
        self.arrays = first.arrays + second.arrays
        self.in_specs = first.in_specs + second.in_specs
        self.out_specs = first.out_specs + second.out_specs
        self.out_shape = first.out_shape + second.out_shape
        self.scratch = first.scratch + second.scratch
        self.forward_step = first.forward_step
        self.relay_step = first.relay_step

    def bind(self, in_refs, out_refs, scratch):
        k, s = self.first.n, len(self.first.scratch)
        return _Joined(self.first.bind(in_refs[:k], out_refs[:k], scratch[:s]),
                       self.second.bind(in_refs[k:], out_refs[k:], scratch[s:]))


def _adamw(w, g, m, v):
    m = ADAM_B1 * m + (1.0 - ADAM_B1) * g
    v = ADAM_B2 * v + (1.0 - ADAM_B2) * (g * g)
    m_hat = m / (1.0 - ADAM_B1 ** ADAM_STEP)
    v_hat = v / (1.0 - ADAM_B2 ** ADAM_STEP)
    delta = -ADAM_LR * (m_hat / (jnp.sqrt(v_hat) + ADAM_EPS) + ADAM_WD * w)
    return delta, m, v


def _sum_parts(parts_ref, index=()):
    g = parts_ref[(0,) + index].astype(F32)
    for s in range(1, parts_ref.shape[0]):
        g = g + parts_ref[(s,) + index].astype(F32)
    return g


def _adamw_group_call(name, groups):
    k = len(groups)

    def body(*refs):
        ins, outs = refs[:4 * k], refs[4 * k:]
        for i in range(k):
            parts_ref, w_ref, m_ref, v_ref = ins[4 * i:4 * i + 4]
            g = _sum_parts(parts_ref)
            delta, m_new, v_new = _adamw(w_ref[...], g, m_ref[...], v_ref[...])
            for out_ref, value in zip(outs[4 * i:4 * i + 4], (g, delta, m_new, v_new)):
                out_ref[...] = value

    vmem = pl.BlockSpec(memory_space=pltpu.VMEM)
    res = pl.pallas_call(
        body, name=name, in_specs=[vmem] * (4 * k), out_specs=[vmem] * (4 * k),
        out_shape=[jax.ShapeDtypeStruct(grp[1].shape, F32) for grp in groups for _ in range(4)],
        compiler_params=pltpu.CompilerParams(vmem_limit_bytes=VMEM_LIMIT),
    )(*[a for grp in groups for a in grp])
    return [res[4 * i:4 * i + 4] for i in range(k)]


def _adamw_slabs_call(name, parts, w, m, v, rider=None):
    def main(parts_ref, w_ref, m_ref, v_ref, g_ref, delta_ref, m_out, v_out):
        g = _sum_parts(parts_ref)
        delta, m_new, v_new = _adamw(w_ref[...], g, m_ref[...], v_ref[...])
        g_ref[...] = g
        delta_ref[...] = delta
        m_out[...] = m_new
        v_out[...] = v_new

    def body(*refs):
        own, comm = _split_refs(refs, 4, 4, 0, rider)
        if comm is not None:
            comm.start()
        main(*own)
        if comm is not None:
            comm.forward()
            comm.relay()
            comm.finish()

    vmem = pl.BlockSpec(memory_space=pltpu.VMEM)
    return pl.pallas_call(
        body, name=name, in_specs=_extend([vmem] * 4, rider, "in_specs"),
        out_specs=_extend([vmem] * 4, rider, "out_specs"),
        out_shape=_extend([jax.ShapeDtypeStruct(w.shape, F32)] * 4, rider, "out_shape"),
        scratch_shapes=_extend([], rider, "scratch"),
        compiler_params=pltpu.CompilerParams(vmem_limit_bytes=VMEM_LIMIT),
    )(parts, w, m, v, *_extend([], rider, "arrays"))


WIDE_ROWS = 8
NARROW_ROWS = 40
NARROW_GKW_ROW = 8
NARROW_GKB_ROW = 24
NARROW_HW_ROW = 32
GROUP_SHARD = POOL_GROUP_DIM // N_DEV
KEY_SHARD = GLA_KEY_WIDTH // N_DEV
HEAD_V_SHARD = GLA_HEAD_V // N_DEV


def _small_adamw_call(wide, narrow, w, m, v):
    names = ("norm_w", "pool_scale", "final_norm_w", "pool_group_b", "gla_gk_w", "gla_gk_b", "gla_head_norm_w")
    where = {
        "norm_w": (0, slice(0, 2), slice(None)),
        "pool_scale": (0, slice(2, 3), slice(None)),
        "final_norm_w": (0, slice(3, 4), slice(None)),
        "pool_group_b": (1, slice(0, POOL_GROUPS), slice(0, GROUP_SHARD)),
        "gla_gk_w": (1, slice(NARROW_GKW_ROW, NARROW_GKW_ROW + GLA_GATE_RANK), slice(0, KEY_SHARD)),
        "gla_gk_b": (1, slice(NARROW_GKB_ROW, NARROW_GKB_ROW + 1), slice(0, KEY_SHARD)),
        "gla_head_norm_w": (1, slice(NARROW_HW_ROW, NARROW_HW_ROW + 1), slice(0, HEAD_V_SHARD)),
    }
    k = len(names)

    def body(*refs):
        parts = refs[0:2]
        w_refs, m_refs, v_refs = refs[2:2 + k], refs[2 + k:2 + 2 * k], refs[2 + 2 * k:2 + 3 * k]
        outs = refs[2 + 3 * k:]
        loss_ref = outs[0]
        loss_ref[...] = _sum_parts(parts[0], (slice(4, 5), slice(0, 1)))
        for i, name in enumerate(names):
            buf, rows, cols = where[name]
            g = _sum_parts(parts[buf], (rows, cols))
            delta, m_new, v_new = _adamw(w_refs[i][...], g, m_refs[i][...], v_refs[i][...])
            outs[1 + i][...] = g
            outs[1 + k + i][...] = delta
            outs[1 + 2 * k + i][...] = m_new
            outs[1 + 3 * k + i][...] = v_new

    vmem = pl.BlockSpec(memory_space=pltpu.VMEM)
    shapes = [jax.ShapeDtypeStruct(w[n].shape, F32) for n in names]
    res = pl.pallas_call(
        body, name="adamw_small", in_specs=[vmem] * (2 + 3 * k), out_specs=[vmem] * (1 + 4 * k),
        out_shape=[jax.ShapeDtypeStruct((1, 1), F32)] + shapes * 4,
    )(wide, narrow, *[w[n] for n in names], *[m[n] for n in names], *[v[n] for n in names])
    unzip = lambda j: dict(zip(names, res[1 + j * k:1 + (j + 1) * k]))
    return res[0], unzip(0), unzip(1), unzip(2), unzip(3)


def kernel(x, norm_w, pool_in_w, pool_group_w, pool_group_b, pool_scale, pool_out_w, gla_in_w, gla_gk_w, gla_gk_b, gla_head_norm_w, gla_out_w, final_norm_w, loss_target, m_norm_w, m_pool_in_w, m_pool_group_w, m_pool_group_b, m_pool_scale, m_pool_out_w, m_gla_in_w, m_gla_gk_w, m_gla_gk_b, m_gla_head_norm_w, m_gla_out_w, m_final_norm_w, v_norm_w, v_pool_in_w, v_pool_group_w, v_pool_group_b, v_pool_scale, v_pool_out_w, v_gla_in_w, v_gla_gk_w, v_gla_gk_b, v_gla_head_norm_w, v_gla_out_w, v_final_norm_w):
    w = dict(norm_w=norm_w, pool_in_w=pool_in_w, pool_group_w=pool_group_w, pool_group_b=pool_group_b,
             pool_scale=pool_scale, pool_out_w=pool_out_w, gla_in_w=gla_in_w, gla_gk_w=gla_gk_w, gla_gk_b=gla_gk_b,
             gla_head_norm_w=gla_head_norm_w, gla_out_w=gla_out_w, final_norm_w=final_norm_w)
    m = dict(norm_w=m_norm_w, pool_in_w=m_pool_in_w, pool_group_w=m_pool_group_w, pool_group_b=m_pool_group_b,
             pool_scale=m_pool_scale, pool_out_w=m_pool_out_w, gla_in_w=m_gla_in_w, gla_gk_w=m_gla_gk_w,
             gla_gk_b=m_gla_gk_b, gla_head_norm_w=m_gla_head_norm_w, gla_out_w=m_gla_out_w,
             final_norm_w=m_final_norm_w)
    v = dict(norm_w=v_norm_w, pool_in_w=v_pool_in_w, pool_group_w=v_pool_group_w, pool_group_b=v_pool_group_b,
             pool_scale=v_pool_scale, pool_out_w=v_pool_out_w, gla_in_w=v_gla_in_w, gla_gk_w=v_gla_gk_w,
             gla_gk_b=v_gla_gk_b, gla_head_norm_w=v_gla_head_norm_w, gla_out_w=v_gla_out_w,
             final_norm_w=v_final_norm_w)
    col_shard = GLA_IN_WIDTH // N_DEV
    row_shard = D_MODEL // N_DEV

    def lanes(a):
        return jnp.pad(a, [(0, 0)] * (a.ndim - 1) + [(0, LANES - a.shape[-1])])

    small_in = jnp.concatenate([lanes(pool_group_b[0]), lanes(gla_gk_b), lanes(gla_head_norm_w),
                                jnp.zeros((2, LANES), F32)], axis=0)
    in_cols = 2 * POOL_WIDTH // N_DEV
    pool_in, pool_gw, small_all = _comm_call("pool_weights_all_gather", _gather_rider(
        [pool_in_w[0].astype(BF16), pool_group_w[0].astype(BF16), small_in],
        [(D_MODEL, 2 * POOL_WIDTH), (POOL_GROUPS, POOL_GROUP_DIM, POOL_GROUP_DIM), (N_DEV, 8, LANES)],
        [_dim1_slot(in_cols), _dim1_slot(GROUP_SHARD), _lead_slot]))
    pool_gb = jnp.transpose(small_all[:, 0:POOL_GROUPS, :GROUP_SHARD], (1, 0, 2)).reshape(1, POOL_WIDTH)
    gla_gkb = small_all[:, POOL_GROUPS, :KEY_SHARD].reshape(1, GLA_KEY_WIDTH)
    gla_hw = jnp.tile(small_all[:, POOL_GROUPS + 1, :HEAD_V_SHARD].reshape(1, GLA_HEAD_V), (1, GLA_HEADS))
    nw0, nw1, wf = norm_w[0:1], norm_w[1:2], final_norm_w.reshape(1, D_MODEL)
    xs, target = x[0], loss_target[0]

    (h1, pool_y, pool_silu, pool_dsilu, pooled, mixed, pool_out, gla_in_parts, gkw_parts,
     gla_out) = _pool_fwd_call(
        xs, nw0, pool_in, pool_gw, pool_gb, pool_scale,
        _gather_rider([pool_out_w[0].astype(BF16)], [(POOL_WIDTH, D_MODEL)], [_row_slot(row_shard)]),
        _gather_rider(
            [jnp.transpose(gla_in_w[0]).astype(BF16), gla_gk_w[0].astype(BF16), gla_out_w[0].astype(BF16)],
            [(N_DEV, col_shard, D_MODEL), (N_DEV, GLA_GATE_RANK, KEY_SHARD), (GLA_VALUE_WIDTH, D_MODEL)],
            [_lead_slot, _lead_slot, _row_slot(row_shard)]))
    gla_in = gla_in_parts.reshape(GLA_IN_WIDTH, D_MODEL)
    gla_gkw = jnp.pad(jnp.transpose(gkw_parts, (1, 0, 2)).reshape(GLA_GATE_RANK, GLA_KEY_WIDTH),
                      ((0, GLA_LOW_PAD - GLA_GATE_RANK), (0, 0)))
    dh2, proj, o, states, scores, loss_part, dwf = _gla_fwd_call(h1, nw1, gla_in, gla_gkw, gla_gkb, gla_hw, gla_out,
                                                                 wf, target)

    dproj, d_gla_out, dhw, dgkw, dgkb = _gla_bwd_call(dh2, proj, o, states, scores, gla_gkw, gla_gkb, gla_hw,
                                                      gla_out)
    dh1, d_gla_in, dnw1 = _inproj_bwd_call("gla_in_bwd", dproj, h1, nw1, gla_in, dh2, transposed=True)
    slabs = col_shard * D_MODEL // (BF16_ROWS * LANES)
    gla_in_send = d_gla_in.reshape(N_DEV, slabs, BF16_ROWS, LANES)
    dp, d_pool_out, dgw, dgb, dsc, landed_gla_in, landed_gla_out = _pool_bwd_call(
        dh1, pool_y, pool_silu, pool_dsilu, pooled, mixed, pool_gw, pool_scale, pool_out,
        _two_level_rider([gla_in_send, d_gla_out], [(slabs, BF16_ROWS, LANES), (row_shard, D_MODEL)],
                         [_lead_slot, _row_slot(row_shard)], TWO_LEVEL_ADD_STEP, TWO_LEVEL_RELAY_STEP))
    grad_x, d_pool_in, dnw0 = _inproj_bwd_call("pool_in_bwd", dp, xs, nw0, pool_in, dh1)

    wide = jnp.concatenate([
        dnw0, dnw1, dsc, dwf, jnp.pad(loss_part[0:1, 0:1], ((0, 0), (0, D_MODEL - 1))),
        jnp.zeros((WIDE_ROWS - 5, D_MODEL), F32)], axis=0)

    def rows8(a):
        return jnp.pad(lanes(a), ((0, 0), (0, -a.shape[1] % 8), (0, 0)))

    narrow = jnp.concatenate([
        rows8(jnp.transpose(dgb.reshape(POOL_GROUPS, N_DEV, GROUP_SHARD), (1, 0, 2))),
        rows8(jnp.transpose(dgkw[:GLA_GATE_RANK].reshape(GLA_GATE_RANK, N_DEV, KEY_SHARD), (1, 0, 2))),
        rows8(dgkb.reshape(N_DEV, 1, KEY_SHARD)),
        rows8(dhw.reshape(GLA_HEADS, GLA_HEAD_V).sum(axis=0).reshape(N_DEV, 1, HEAD_V_SHARD)),
    ], axis=1)
    last_exchange = _JoinedRider(
        _two_level_rider([d_pool_in, d_pool_out, dgw],
                         [(D_MODEL, in_cols), (row_shard, D_MODEL), (POOL_GROUPS, GROUP_SHARD, POOL_GROUP_DIM)],
                         [_dim1_slot(in_cols), _row_slot(row_shard), _dim1_slot(GROUP_SHARD)]),
        _exchange_rider([wide, narrow], [(WIDE_ROWS, D_MODEL), (NARROW_ROWS, LANES)],
                        [lambda ref, d: ref, _lead_slot]))

    res = {}
    as_slabs = lambda t: jnp.transpose(t[0]).reshape(slabs, BF16_ROWS, LANES)
    *outs, landed_pool_in, landed_pool_out, landed_gw, landed_wide, landed_narrow = _adamw_slabs_call(
        "adamw_gla_in_w", landed_gla_in, as_slabs(gla_in_w), as_slabs(m_gla_in_w), as_slabs(v_gla_in_w),
        last_exchange)
    res["gla_in_w"] = [jnp.transpose(t.reshape(col_shard, D_MODEL))[None] for t in outs]
    rest = [("pool_in_w", landed_pool_in, (D_MODEL, in_cols)),
            ("pool_group_w", landed_gw, (POOL_GROUPS * GROUP_SHARD, POOL_GROUP_DIM)),
            ("pool_out_w", landed_pool_out, (row_shard, D_MODEL)), ("gla_out_w", landed_gla_out, (row_shard, D_MODEL))]
    updates = _adamw_group_call("adamw_matrices", [
        (parts.reshape((parts.shape[0],) + shape), w[name].reshape(shape), m[name].reshape(shape),
         v[name].reshape(shape)) for name, parts, shape in rest])
    for (name, _, _), outs in zip(rest, updates):
        res[name] = [t.reshape(w[name].shape) for t in outs]
    small_shapes ={"norm_w": (2, D_MODEL), "pool_scale": (1, D_MODEL), "final_norm_w": (1, D_MODEL),
                    "pool_group_b": (POOL_GROUPS, GROUP_SHARD), "gla_gk_w": (GLA_GATE_RANK, KEY_SHARD),
                    "gla_gk_b": (1, KEY_SHARD), "gla_head_norm_w": (1, HEAD_V_SHARD)}
    as_small = lambda t: {n: t[n].reshape(s) for n, s in small_shapes.items()}
    loss, *small_outs = _small_adamw_call(landed_wide, landed_narrow, as_small(w), as_small(m), as_small(v))
    for name in small_shapes:
        res[name] = [t[name].reshape(w[name].shape) for t in small_outs]
    order = ("norm_w", "pool_in_w", "pool_group_w", "pool_group_b", "pool_scale", "pool_out_w", "gla_in_w",
             "gla_gk_w", "gla_gk_b", "gla_head_norm_w", "gla_out_w", "final_norm_w")
    return (loss.reshape(()), grad_x[None], *[res[n][0] for n in order], *[res[n][1] for n in order],
            *[res[n][2] for n in order], *[res[n][3] for n in order])
```

```python
import jax
import jax.numpy as jnp
from jax import lax
from jax.experimental import pallas as pl
from jax.experimental.pallas import tpu as pltpu

F32 = jnp.float32
BF16 = jnp.bfloat16
MESH = pl.DeviceIdType.MESH

N_DEV = 8
D_MODEL = 1024
POOL_WIDTH = 1024
POOL_GROUPS = 4
POOL_GROUP_DIM = 256
POOL_HALO = 16
GLA_HEADS = 4
GLA_HEAD_K = 128
GLA_HEAD_V = 256
GLA_KEY_WIDTH = 512
GLA_VALUE_WIDTH = 1024
GLA_GATE_RANK = 16
GLA_IN_WIDTH = 3088
GLA_IN_PAD = 3200
GLA_SAVED_Z = GLA_IN_PAD
GLA_SAVED_C = GLA_SAVED_Z + 512
GLA_SAVED_WIDTH = GLA_SAVED_C + 512
GLA_LOW_PAD = 128
GLA_QKVG_WIDTH = 3072
CHUNK = 64
GATE_NORMALIZER = 16.0
RMS_EPS = 1e-6
Q_SCALE = GLA_HEAD_K ** -0.5

ADAM_LR = 0.001
ADAM_B1 = 0.9
ADAM_B2 = 0.999
ADAM_EPS = 1e-08
ADAM_WD = 0.01
ADAM_STEP = 10

LANES = 128
BF16_ROWS = 16
VMEM_LIMIT = 56 * 1024 * 1024
ROW_TILE = 256
GLA_FWD_ROW_TILE = 512
MATMUL_ROW_TILE = 512
DW_STAGE_CHUNKS = 4
DW_PAIR_MAX_WIDTH = 2048
GATHER_RELAY_STEP = 5
TWO_LEVEL_ADD_STEP = 1
TWO_LEVEL_RELAY_STEP = 4


def _dot_nn(a, b):
    return lax.dot_general(a, b, (((1,), (0,)), ((), ())), preferred_element_type=F32)


def _dot_nt(a, b):
    return lax.dot_general(a, b, (((1,), (1,)), ((), ())), preferred_element_type=F32)


def _dot_tn(a, b):
    return lax.dot_general(a, b, (((0,), (0,)), ((), ())), preferred_element_type=F32)


def _rms(x):
    rstd = lax.rsqrt(jnp.mean(x * x, axis=-1, keepdims=True) + RMS_EPS)
    return x * rstd, rstd


def _rms_bwd(dxhat, xhat, rstd):
    return rstd * (dxhat - xhat * jnp.mean(dxhat * xhat, axis=-1, keepdims=True))


def _sigmoid(x):
    return 1.0 / (1.0 + jnp.exp(-x))


def _params(sem=("arbitrary",)):
    return pltpu.CompilerParams(dimension_semantics=sem, vmem_limit_bytes=VMEM_LIMIT)


def _full(shape):
    return pl.BlockSpec(shape, lambda i: (0,) * len(shape))


def _const(shape):
    return pl.BlockSpec(shape, lambda i: (0,) * len(shape), pipeline_mode=pl.Buffered(1))


def _window_sums(ext, forward):
    n = ext.shape[0]
    outs = []
    for g in range(POOL_GROUPS):
        s = ext[:, g * POOL_GROUP_DIM:(g + 1) * POOL_GROUP_DIM]
        for k in range(g + 1):
            shift = (1 << k) if forward else n - (1 << k)
            s = s + pltpu.roll(s, shift, axis=0)
        outs.append(s[:n - POOL_HALO])
    return outs


def _inv_count(row0, tm):
    row = row0 + lax.broadcasted_iota(jnp.int32, (tm, 1), 0)
    return [1.0 / jnp.minimum(row + 1, 2 << g).astype(F32) for g in range(POOL_GROUPS)]


def _pool_mix(u, u_prev, row0, gw_ref, gb):
    tm = u.shape[0]
    sums = _window_sums(jnp.concatenate([u, u_prev], axis=0), True)
    inv = _inv_count(row0, tm)
    pooled, mixed = [], []
    for g in range(POOL_GROUPS):
        ug = u[:, g * POOL_GROUP_DIM:(g + 1) * POOL_GROUP_DIM]
        pg = (sums[g] * inv[g] - ug).astype(BF16)
        pooled.append(pg)
        mixed.append(_dot_nn(pg, gw_ref[g]))
    return pooled, jnp.concatenate(mixed, axis=1) + gb


def _pool_fwd_call(x, nw, w_in, gw, gb, sc, w_out, rider=None):
    seq = x.shape[0]
    tm = min(MATMUL_ROW_TILE, seq)
    nt = seq // tm

    def main(x_ref, nw_ref, win_ref, gw_ref, gb_ref, sc_ref, wout_ref, h_ref, y_ref, silu_ref, dsilu_ref,
             pooled_ref, mixed_ref, halo_ref):
        i = pl.program_id(0)

        @pl.when(i == 0)
        def _():
            halo_ref[...] = jnp.zeros_like(halo_ref)

        xt = x_ref[...]
        xhat, _ = _rms(xt)
        n = (xhat * nw_ref[...]).astype(BF16)
        p = _dot_nn(n, win_ref[...])
        u = p[:, :POOL_WIDTH]
        gate = p[:, POOL_WIDTH:]
        sg = _sigmoid(gate)
        silu = gate * sg
        silu_ref[...] = silu
        dsilu_ref[...] = sg * (1.0 + gate * (1.0 - sg))
        pooled, mixed = _pool_mix(u, halo_ref[...], i * tm, gw_ref, gb_ref[...])
        pooled_ref[...] = jnp.concatenate(pooled, axis=1)
        mixed_ref[...] = mixed
        halo_ref[...] = u[tm - POOL_HALO:, :]
        y = (mixed * sc_ref[...] * silu).astype(BF16)
        y_ref[...] = y
        h_ref[...] = xt + _dot_nn(y, wout_ref[...])

    def body(*refs):
        own, comm = _split_refs(refs, 7, 6, 1, rider)
        _ride_before(comm, pl.program_id(0), nt)
        main(*own)
        _ride_after(comm, pl.program_id(0), nt)

    return pl.pallas_call(
        body, name="pool_fwd", grid=(nt,),
        in_specs=_extend([pl.BlockSpec((tm, D_MODEL), lambda i: (i, 0)), _const((1, D_MODEL)),
                          _const((D_MODEL, 2 * POOL_WIDTH)), _const((POOL_GROUPS, POOL_GROUP_DIM, POOL_GROUP_DIM)),
                          _const((1, POOL_WIDTH)), _const((1, POOL_WIDTH)), _const((POOL_WIDTH, D_MODEL))],
                         rider, "in_specs"),
        out_specs=_extend([pl.BlockSpec((tm, D_MODEL), lambda i: (i, 0))] * 6, rider, "out_specs"),
        out_shape=_extend([jax.ShapeDtypeStruct((seq, D_MODEL), F32), jax.ShapeDtypeStruct((seq, POOL_WIDTH), BF16),
                           jax.ShapeDtypeStruct((seq, POOL_WIDTH), F32), jax.ShapeDtypeStruct((seq, POOL_WIDTH), F32),
                           jax.ShapeDtypeStruct((seq, POOL_WIDTH), BF16),
                           jax.ShapeDtypeStruct((seq, POOL_WIDTH), F32)], rider, "out_shape"),
        scratch_shapes=_extend([pltpu.VMEM((POOL_HALO, POOL_WIDTH), F32)], rider, "scratch"),
        compiler_params=_params(),
    )(x, nw, w_in, gw, gb, sc, w_out, *_extend([], rider, "arrays"))


def _pool_bwd_call(dh, y, silu, dsilu, pooled, mixed, gw, sc, w_out, rider=None):
    seq = dh.shape[0]
    tm = min(MATMUL_ROW_TILE, seq)
    nt = seq // tm

    def main(dh_ref, y_ref, silu_ref, dsilu_ref, pooled_ref, mixed_ref, gw_ref, sc_ref, wout_ref,
             dp_ref, dwout_hbm, dgw_hbm, dgb_ref, dsc_ref, carry_ref, dwout_acc, dgw_acc, dwout_stage, dgw_stage):
        i = pl.program_id(0)
        t = nt - 1 - i

        @pl.when(i == 0)
        def _():
            carry_ref[...] = jnp.zeros_like(carry_ref)
            dwout_acc[...] = jnp.zeros_like(dwout_acc)
            dgw_acc[...] = jnp.zeros_like(dgw_acc)
            dgb_ref[...] = jnp.zeros_like(dgb_ref)
            dsc_ref[...] = jnp.zeros_like(dsc_ref)

        silu = silu_ref[...]
        pooled = [pooled_ref[:, g * POOL_GROUP_DIM:(g + 1) * POOL_GROUP_DIM] for g in range(POOL_GROUPS)]
        sc = sc_ref[...]
        dhb = dh_ref[...].astype(BF16)
        dwout_acc[...] += _dot_tn(y_ref[...], dhb)
        dy = _dot_nt(dhb, wout_ref[...])
        dmixed = dy * sc * silu
        dy_mixed = dy * mixed_ref[...]
        dsc_ref[...] += jnp.sum(dy_mixed * silu, axis=0, keepdims=True)
        dgate = dy_mixed * sc * dsilu_ref[...]
        dgb_ref[...] += jnp.sum(dmixed, axis=0, keepdims=True)
        inv = _inv_count(t * tm, tm)
        dpooled, scaled = [], []
        for g in range(POOL_GROUPS):
            dmg = dmixed[:, g * POOL_GROUP_DIM:(g + 1) * POOL_GROUP_DIM].astype(BF16)
            dgw_acc[g] += _dot_tn(pooled[g], dmg)
            dpg = _dot_nt(dmg, gw_ref[g])
            dpooled.append(dpg)
            scaled.append(dpg * inv[g])
        r = jnp.concatenate(scaled, axis=1)
        sums = _window_sums(jnp.concatenate([r, carry_ref[...]], axis=0), False)
        carry_ref[...] = r[:POOL_HALO, :]
        du = jnp.concatenate([sums[g] - dpooled[g] for g in range(POOL_GROUPS)], axis=1)
        dp_ref[...] = jnp.concatenate([du, dgate], axis=1).astype(BF16)

        @pl.when(i == nt - 1)
        def _():
            dwout_stage[...] = dwout_acc[...].astype(BF16)
            dgw_stage[...] = dgw_acc[...].astype(BF16)
            pltpu.sync_copy(dwout_stage, dwout_hbm)
            pltpu.sync_copy(dgw_stage, dgw_hbm)

    def body(*refs):
        own, comm = _split_refs(refs, 9, 5, 5, rider)
        _ride_before(comm, pl.program_id(0), nt)
        main(*own)
        _ride_after(comm, pl.program_id(0), nt)

    rev = lambda i: (nt - 1 - i, 0)
    return pl.pallas_call(
        body, name="pool_bwd", grid=(nt,),
        in_specs=_extend([pl.BlockSpec((tm, D_MODEL), rev)] * 6
                         + [_const((POOL_GROUPS, POOL_GROUP_DIM, POOL_GROUP_DIM)), _const((1, POOL_WIDTH)),
                            _const((POOL_WIDTH, D_MODEL))], rider, "in_specs"),
        out_specs=_extend([pl.BlockSpec((tm, 2 * POOL_WIDTH), rev), pl.BlockSpec(memory_space=pl.ANY),
                           pl.BlockSpec(memory_space=pl.ANY), _full((1, POOL_WIDTH)), _full((1, POOL_WIDTH))],
                          rider, "out_specs"),
        out_shape=_extend([jax.ShapeDtypeStruct((seq, 2 * POOL_WIDTH), BF16),
                           jax.ShapeDtypeStruct((POOL_WIDTH, D_MODEL), BF16),
                           jax.ShapeDtypeStruct((POOL_GROUPS, POOL_GROUP_DIM, POOL_GROUP_DIM), BF16),
                           jax.ShapeDtypeStruct((1, POOL_WIDTH), F32), jax.ShapeDtypeStruct((1, POOL_WIDTH), F32)],
                          rider, "out_shape"),
        scratch_shapes=_extend([pltpu.VMEM((POOL_HALO, POOL_WIDTH), F32), pltpu.VMEM((POOL_WIDTH, D_MODEL), F32),
                                pltpu.VMEM((POOL_GROUPS, POOL_GROUP_DIM, POOL_GROUP_DIM), F32),
                                pltpu.VMEM((POOL_WIDTH, D_MODEL), BF16),
                                pltpu.VMEM((POOL_GROUPS, POOL_GROUP_DIM, POOL_GROUP_DIM), BF16)], rider, "scratch"),
        compiler_params=_params(),
    )(dh, y, silu, dsilu, pooled, mixed, gw, sc, w_out, *_extend([], rider, "arrays"))


def _rows_then_zeros(ref, lo, hi, rows):
    part = ref[lo:hi, :]
    return jnp.concatenate([part, jnp.zeros((rows - (hi - lo), part.shape[1]), part.dtype)], axis=0)


def _inproj_bwd_call(name, dproj, h_in, nw, w_in, dres, rider=None, transposed=False):
    seq = h_in.shape[0]
    width = dproj.shape[1]
    w_shape = tuple(w_in.shape)
    acc_shape = (width, D_MODEL) if transposed else w_shape
    whole = w_shape[0] // LANES * LANES
    tm = min(MATMUL_ROW_TILE, seq)
    nt = seq // tm

    stage_rows = acc_shape[0] // DW_STAGE_CHUNKS
    pair_tiles = nt % 2 == 0 and width <= DW_PAIR_MAX_WIDTH
    pair_rows = 2 * tm if pair_tiles else BF16_ROWS

    def main(dproj_ref, h_ref, nw_ref, win_ref, dres_ref, dh_ref, dw_hbm, dnw_ref, dw_acc, dw_stage, dp_pair, n_pair):
        i = pl.program_id(0)

        @pl.when(i == 0)
        def _():
            dw_acc[...] = jnp.zeros_like(dw_acc)
            dnw_ref[...] = jnp.zeros_like(dnw_ref)

        def add_dw(dp_rows, n_rows):
            dw_acc[...] += _dot_tn(dp_rows, n_rows) if transposed else _dot_tn(n_rows, dp_rows)

        dpb = dproj_ref[...]
        if transposed:
            dn = _dot_nn(dpb[:, :whole], win_ref[0:whole, :])
            if whole < w_shape[0]:
                dn = dn + _dot_nn(dpb[:, whole:], _rows_then_zeros(win_ref, whole, w_shape[0], width - whole))
        else:
            dn = _dot_nt(dpb, win_ref[...])
        xhat, rstd = _rms(h_ref[...])
        nw_row = nw_ref[...]
        n = (xhat * nw_row).astype(BF16)
        if pair_tiles:
            half = pl.ds(pl.multiple_of((i % 2) * tm, tm), tm)
            dp_pair[half, :] = dpb
            n_pair[half, :] = n
            pl.when(i % 2 == 1)(lambda: add_dw(dp_pair[...], n_pair[...]))
        else:
            add_dw(dpb, n)
        dnw_ref[...] += jnp.sum(dn * xhat, axis=0, keepdims=True)
        dh_ref[...] = _rms_bwd(dn * nw_row, xhat, rstd) + dres_ref[...]

        @pl.when(i == nt - 1)
        def _():
            for chunk in range(DW_STAGE_CHUNKS):
                lo = chunk * stage_rows
                rows = min(stage_rows, w_shape[0] - lo)
                dw_stage[...] = dw_acc[lo:lo + stage_rows, :].astype(BF16)
                pltpu.sync_copy(dw_stage.at[pl.ds(0, rows)], dw_hbm.at[pl.ds(lo, rows)])

    def body(*refs):
        own, comm = _split_refs(refs, 5, 3, 4, rider)
        _ride_before(comm, pl.program_id(0), nt)
        main(*own)
        _ride_after(comm, pl.program_id(0), nt)

    row = lambda i: (i, 0)
    return pl.pallas_call(
        body, name=name, grid=(nt,),
        in_specs=_extend([pl.BlockSpec((tm, width), row), pl.BlockSpec((tm, D_MODEL), row), _const((1, D_MODEL)),
                          _const(w_shape), pl.BlockSpec((tm, D_MODEL), row)], rider, "in_specs"),
        out_specs=_extend([pl.BlockSpec((tm, D_MODEL), row), pl.BlockSpec(memory_space=pl.ANY),
                           _full((1, D_MODEL))], rider, "out_specs"),
        out_shape=_extend([jax.ShapeDtypeStruct((seq, D_MODEL), F32), jax.ShapeDtypeStruct(w_shape, BF16),
                           jax.ShapeDtypeStruct((1, D_MODEL), F32)], rider, "out_shape"),
        scratch_shapes=_extend([pltpu.VMEM(acc_shape, F32), pltpu.VMEM((stage_rows, acc_shape[1]), BF16),
                                pltpu.VMEM((pair_rows, width), BF16), pltpu.VMEM((pair_rows, D_MODEL), BF16)],
                               rider, "scratch"),
        compiler_params=_params(),
    )(dproj, h_in, nw, w_in, dres, *_extend([], rider, "arrays"))


def _chunk_scan(x, reverse):
    n = x.shape[0]
    pos = lax.broadcasted_iota(jnp.int32, (n, 1), 0) & (CHUNK - 1)
    k = 1
    while k < CHUNK:
        if reverse:
            x = x + jnp.where(pos < CHUNK - k, pltpu.roll(x, n - k, axis=0), 0.0)
        else:
            x = x + jnp.where(pos >= k, pltpu.roll(x, k, axis=0), 0.0)
        k *= 2
    return x


def _chunk_rows(j):
    return slice(j * CHUNK, (j + 1) * CHUNK)


def _kcols(h):
    return slice(h * GLA_HEAD_K, (h + 1) * GLA_HEAD_K)


def _vcols(h):
    return slice(h * GLA_HEAD_V, (h + 1) * GLA_HEAD_V)


def _chunk_masks(tm):
    idx_t = lax.broadcasted_iota(jnp.int32, (tm, tm), 0)
    idx_s = lax.broadcasted_iota(jnp.int32, (tm, tm), 1)
    same_chunk = (idx_t ^ idx_s) < CHUNK
    return same_chunk & (idx_t >= idx_s), same_chunk & (idx_t < idx_s)


class _GlaTerms:
    def __init__(self, kc, q, k, v, low_b, gkw_ref, gkb_ref, masks, saved=None):
        tm = q.shape[0]
        self.q = q * Q_SCALE
        self.k = k
        if saved is None:
            self.z = _dot_nn(low_b, gkw_ref[:, kc]) + gkb_ref[:, kc]
            log_g = (jnp.minimum(self.z, 0.0) - jnp.log(1.0 + jnp.exp(-jnp.abs(self.z)))) / GATE_NORMALIZER
            self.c = _chunk_scan(log_g, False)
        else:
            self.z, self.c = saved
        is_last = lax.broadcasted_iota(jnp.int32, (CHUNK, 1), 0) == CHUNK - 1
        self.c_last = [jnp.sum(jnp.where(is_last, self.c[_chunk_rows(j), :], 0.0), axis=0, keepdims=True)
                       for j in range(tm // CHUNK)]
        c_last_rows = jnp.concatenate([jnp.broadcast_to(r, (CHUNK, r.shape[1])) for r in self.c_last], axis=0)
        self.e_pos = jnp.exp(self.c)
        self.e_neg = jnp.exp(-self.c)
        self.e_rest = jnp.exp(c_last_rows - self.c)
        self.a_b = (self.q * self.e_pos).astype(BF16)
        self.b_b = (self.k * self.e_neg).astype(BF16)
        self.cn_b = (self.q * self.e_neg).astype(BF16)
        self.dp_b = (self.k * self.e_pos).astype(BF16)
        self.kd_b = (self.k * self.e_rest).astype(BF16)
        self.v_b = v.astype(BF16)
        self.lower, self.upper = masks

    def scores(self, kc=slice(None)):
        fwd = _dot_nt(self.a_b[:, kc], self.b_b[:, kc])
        bwd = _dot_nt(self.cn_b[:, kc], self.dp_b[:, kc])
        return jnp.where(self.lower, fwd, jnp.where(self.upper, bwd, 0.0)).astype(BF16)


def _gla_fwd_call(h1, nw, w_in, gkw, gkb, hw, w_out, wf, target):
    seq = h1.shape[0]
    tm = min(GLA_FWD_ROW_TILE, seq)
    nt = seq // tm
    cpt = tm // CHUNK
    n_chunks = seq // CHUNK

    def body(h_ref, nw_ref, win_ref, gkw_ref, gkb_ref, hw_ref, wout_ref, wf_ref, tgt_ref,
             dh2_ref, proj_ref, o_ref, st_ref, scores_ref, loss_ref, dwf_ref, state_ref):
        i = pl.program_id(0)

        @pl.when(i == 0)
        def _():
            state_ref[...] = jnp.zeros_like(state_ref)
            loss_ref[...] = jnp.zeros_like(loss_ref)
            dwf_ref[...] = jnp.zeros_like(dwf_ref)

        ht = h_ref[...]
        xhat, _ = _rms(ht)
        n = (xhat * nw_ref[...]).astype(BF16)
        sections = {}
        for name, lo, hi in (("low", GLA_QKVG_WIDTH, GLA_IN_PAD), ("qk", 0, 2 * GLA_KEY_WIDTH),
                             ("v", 2 * GLA_KEY_WIDTH, GLA_QKVG_WIDTH - GLA_VALUE_WIDTH),
                             ("gate", GLA_QKVG_WIDTH - GLA_VALUE_WIDTH, GLA_QKVG_WIDTH)):
            rows = (win_ref[lo:hi, :] if hi <= GLA_IN_WIDTH
                    else _rows_then_zeros(win_ref, lo, GLA_IN_WIDTH, hi - lo))
            sections[name] = _dot_nt(n, rows)
            proj_ref[:, lo:hi] = sections[name]
        low_b = sections["low"].astype(BF16)
        masks = _chunk_masks(tm)
        on_heads = []
        for h in range(GLA_HEADS):
            kc, vc = _kcols(h), _vcols(h)
            g = _GlaTerms(kc, sections["qk"][:, kc], sections["qk"][:, GLA_KEY_WIDTH:][:, kc], sections["v"][:, vc],
                          low_b, gkw_ref, gkb_ref, masks)
            srows = slice(h * GLA_HEAD_V, (h + 1) * GLA_HEAD_V)
            scores = g.scores()
            for b in range(tm // ROW_TILE):
                part = slice(b * ROW_TILE, (b + 1) * ROW_TILE)
                scores_ref[part, h * ROW_TILE:(h + 1) * ROW_TILE] = scores[part, part]
            o_intra = _dot_nn(scores, g.v_b)
            state = state_ref[srows, :]
            o_rows = []
            for j in range(cpt):
                r = _chunk_rows(j)
                st_ref[j, srows, :] = state
                o_rows.append(o_intra[r] + _dot_nt(g.a_b[r], state.astype(BF16)))
                decay = jnp.exp(g.c_last[j])
                state = state * decay + _dot_tn(g.v_b[r], g.kd_b[r])
            state_ref[srows, :] = state
            o_head = jnp.concatenate(o_rows, axis=0)
            o_ref[:, vc] = o_head
            proj_ref[:, GLA_SAVED_Z + kc.start:GLA_SAVED_Z + kc.stop] = g.z
            proj_ref[:, GLA_SAVED_C + kc.start:GLA_SAVED_C + kc.stop] = g.c
            on_heads.append(_rms(o_head)[0])
        gate = sections["gate"]
        on = jnp.concatenate(on_heads, axis=1) * hw_ref[...]
        y = (on * (gate * _sigmoid(gate))).astype(BF16)
        h2 = ht + _dot_nn(y, wout_ref[...])
        xhat2, rstd2 = _rms(h2)
        wf_row = wf_ref[...]
        err = xhat2 * wf_row - tgt_ref[...]
        loss_ref[...] += 0.5 * jnp.sum(err * err) / D_MODEL
        dout = err * (1.0 / D_MODEL)
        dwf_ref[...] += jnp.sum(dout * xhat2, axis=0, keepdims=True)
        dh2_ref[...] = _rms_bwd(dout * wf_row, xhat2, rstd2)

    row = lambda i: (i, 0)
    return pl.pallas_call(
        body, name="gla_fwd", grid=(nt,),
        in_specs=[pl.BlockSpec((tm, D_MODEL), row), _const((1, D_MODEL)), _const((GLA_IN_WIDTH, D_MODEL)),
                  _const((GLA_LOW_PAD, GLA_KEY_WIDTH)), _const((1, GLA_KEY_WIDTH)), _const((1, GLA_VALUE_WIDTH)),
                  _const((GLA_VALUE_WIDTH, D_MODEL)), _const((1, D_MODEL)), pl.BlockSpec((tm, D_MODEL), row)],
        out_specs=[pl.BlockSpec((tm, D_MODEL), row), pl.BlockSpec((tm, GLA_SAVED_WIDTH), row),
                   pl.BlockSpec((tm, GLA_VALUE_WIDTH), row),
                   pl.BlockSpec((cpt, GLA_VALUE_WIDTH, GLA_HEAD_K), lambda i: (i, 0, 0)),
                   pl.BlockSpec((tm, GLA_HEADS * ROW_TILE), row), _full((8, LANES)), _full((1, D_MODEL))],
        out_shape=[jax.ShapeDtypeStruct((seq, D_MODEL), F32), jax.ShapeDtypeStruct((seq, GLA_SAVED_WIDTH), F32),
                   jax.ShapeDtypeStruct((seq, GLA_VALUE_WIDTH), F32),
                   jax.ShapeDtypeStruct((n_chunks, GLA_VALUE_WIDTH, GLA_HEAD_K), F32),
                   jax.ShapeDtypeStruct((seq, GLA_HEADS * ROW_TILE), BF16),
                   jax.ShapeDtypeStruct((8, LANES), F32), jax.ShapeDtypeStruct((1, D_MODEL), F32)],
        scratch_shapes=[pltpu.VMEM((GLA_VALUE_WIDTH, GLA_HEAD_K), F32)],
        compiler_params=_params(),
    )(h1, nw, w_in, gkw, gkb, hw, w_out, wf, target)


def _gla_bwd_call(dh2, proj, o, states, scores, gkw, gkb, hw, w_out):
    seq = dh2.shape[0]
    tm = ROW_TILE
    nt = seq // tm
    cpt = tm // CHUNK

    def body(dh_ref, proj_ref, o_ref, st_ref, scores_ref, gkw_ref, gkb_ref, hw_ref, wout_ref,
             dproj_ref, dwout_hbm, dhw_ref, dgkw_ref, dgkb_ref, dstate_ref, dwout_acc, dwout_stage):
        i = pl.program_id(0)

        @pl.when(i == 0)
        def _():
            dstate_ref[...] = jnp.zeros_like(dstate_ref)
            dwout_acc[...] = jnp.zeros_like(dwout_acc)
            dhw_ref[...] = jnp.zeros_like(dhw_ref)
            dgkw_ref[...] = jnp.zeros_like(dgkw_ref)
            dgkb_ref[...] = jnp.zeros_like(dgkb_ref)

        dhb = dh_ref[...].astype(BF16)
        dy = _dot_nt(dhb, wout_ref[...])
        v0, g0 = 2 * GLA_KEY_WIDTH, GLA_QKVG_WIDTH - GLA_VALUE_WIDTH
        gate = proj_ref[:, g0:GLA_QKVG_WIDTH]
        low_b = proj_ref[:, GLA_QKVG_WIDTH:GLA_IN_PAD].astype(BF16)
        o = o_ref[...]
        hw_row = hw_ref[...]
        sg = _sigmoid(gate)
        silu = gate * sg
        don = dy * silu
        on_parts, do_parts, dhw_parts = [], [], []
        for h in range(GLA_HEADS):
            vc = _vcols(h)
            xh, rs = _rms(o[:, vc])
            on_parts.append(xh * hw_row[:, vc])
            dhw_parts.append(jnp.sum(don[:, vc] * xh, axis=0, keepdims=True))
            do_parts.append(_rms_bwd(don[:, vc] * hw_row[:, vc], xh, rs).astype(BF16))
        on = jnp.concatenate(on_parts, axis=1)
        dwout_acc[...] += _dot_tn((on * silu).astype(BF16), dhb)
        dhw_ref[...] += jnp.concatenate(dhw_parts, axis=1)
        dproj_ref[:, g0:GLA_QKVG_WIDTH] = (dy * on * (sg * (1.0 + gate * (1.0 - sg)))).astype(BF16)

        last_row = lax.broadcasted_iota(jnp.int32, (CHUNK, 1), 0) == CHUNK - 1
        g = _GlaTerms(slice(0, GLA_KEY_WIDTH), proj_ref[:, :GLA_KEY_WIDTH], proj_ref[:, GLA_KEY_WIDTH:v0],
                      proj_ref[:, v0:g0], low_b, gkw_ref, gkb_ref, _chunk_masks(tm),
                      saved=(proj_ref[:, GLA_SAVED_Z:GLA_SAVED_C], proj_ref[:, GLA_SAVED_C:GLA_SAVED_WIDTH]))
        dc_h = []
        for h in range(GLA_HEADS):
            kc, vc = _kcols(h), _vcols(h)
            k_cols = slice(GLA_KEY_WIDTH + kc.start, GLA_KEY_WIDTH + kc.stop)
            v_cols = slice(v0 + vc.start, v0 + vc.stop)
            do_h = do_parts[h]
            srows = slice(h * GLA_HEAD_V, (h + 1) * GLA_HEAD_V)
            scores = scores_ref[:, h * ROW_TILE:(h + 1) * ROW_TILE]
            dscores = _dot_nt(do_h, g.v_b[:, vc])
            dfwd = jnp.where(g.lower, dscores, 0.0).astype(BF16)
            dbwd = jnp.where(g.upper, dscores, 0.0).astype(BF16)
            dv_intra = _dot_tn(scores, do_h)
            da_intra = _dot_nn(dfwd, g.b_b[:, kc])
            db = _dot_tn(dfwd, g.a_b[:, kc])
            dcn = _dot_nn(dbwd, g.dp_b[:, kc])
            ddp = _dot_tn(dbwd, g.cn_b[:, kc])
            dstate = dstate_ref[srows, :]
            da_rows, dkd_rows, dv_rows, dcl_rows = [None] * cpt, [None] * cpt, [None] * cpt, [None] * cpt
            for j in reversed(range(cpt)):
                r = _chunk_rows(j)
                state = st_ref[j, srows, :]
                dstate_b = dstate.astype(BF16)
                do_c = do_h[r]
                dv_rows[j] = dv_intra[r] + _dot_nt(g.kd_b[r, kc], dstate_b)
                da_rows[j] = da_intra[r] + _dot_nn(do_c, state.astype(BF16))
                dkd = _dot_nn(g.v_b[r, vc], dstate_b) * g.e_rest[r, kc]
                dkd_rows[j] = dkd
                decay = jnp.exp(g.c_last[j][:, kc])
                dc_last = (jnp.sum(dkd * g.k[r, kc], axis=0, keepdims=True)
                           + decay * jnp.sum(state * dstate, axis=0, keepdims=True))
                dcl_rows[j] = jnp.where(last_row, dc_last, 0.0)
                dstate = _dot_tn(do_c, g.a_b[r, kc]) + dstate * decay
            dstate_ref[srows, :] = dstate
            da = jnp.concatenate(da_rows, axis=0)
            dkd = jnp.concatenate(dkd_rows, axis=0)
            dproj_ref[:, v_cols] = jnp.concatenate(dv_rows, axis=0).astype(BF16)
            q_up, q_down = da * g.e_pos[:, kc], dcn * g.e_neg[:, kc]
            k_up, k_down = ddp * g.e_pos[:, kc], db * g.e_neg[:, kc] + dkd
            dproj_ref[:, kc] = (Q_SCALE * (q_up + q_down)).astype(BF16)
            dproj_ref[:, k_cols] = (k_up + k_down).astype(BF16)
            dc_h.append(g.q[:, kc] * (q_up - q_down) + g.k[:, kc] * (k_up - k_down)
                        + jnp.concatenate(dcl_rows, axis=0))
        dz = _chunk_scan(jnp.concatenate(dc_h, axis=1), True) * (1.0 / GATE_NORMALIZER) * (1.0 - _sigmoid(g.z))
        dzb = dz.astype(BF16)
        dgkb_ref[...] += jnp.sum(dz, axis=0, keepdims=True)
        dgkw_ref[...] += _dot_tn(low_b, dzb)
        dproj_ref[:, GLA_QKVG_WIDTH:] = _dot_nt(dzb, gkw_ref[...]).astype(BF16)

        @pl.when(i == nt - 1)
        def _():
            dwout_stage[...] = dwout_acc[...].astype(BF16)
            pltpu.sync_copy(dwout_stage, dwout_hbm)

    rev = lambda i: (nt - 1 - i, 0)
    return pl.pallas_call(
        body, name="gla_bwd", grid=(nt,),
        in_specs=[pl.BlockSpec((tm, D_MODEL), rev), pl.BlockSpec((tm, GLA_SAVED_WIDTH), rev),
                  pl.BlockSpec((tm, GLA_VALUE_WIDTH), rev),
                  pl.BlockSpec((cpt, GLA_VALUE_WIDTH, GLA_HEAD_K), lambda i: (nt - 1 - i, 0, 0)),
                  pl.BlockSpec((tm, GLA_HEADS * ROW_TILE), rev),
                  _const((GLA_LOW_PAD, GLA_KEY_WIDTH)), _const((1, GLA_KEY_WIDTH)), _const((1, GLA_VALUE_WIDTH)),
                  _const((GLA_VALUE_WIDTH, D_MODEL))],
        out_specs=[pl.BlockSpec((tm, GLA_IN_PAD), rev), pl.BlockSpec(memory_space=pl.ANY),
                   _full((1, GLA_VALUE_WIDTH)), _full((GLA_LOW_PAD, GLA_KEY_WIDTH)), _full((1, GLA_KEY_WIDTH))],
        out_shape=[jax.ShapeDtypeStruct((seq, GLA_IN_PAD), BF16), jax.ShapeDtypeStruct((GLA_VALUE_WIDTH, D_MODEL), BF16),
                   jax.ShapeDtypeStruct((1, GLA_VALUE_WIDTH), F32), jax.ShapeDtypeStruct((GLA_LOW_PAD, GLA_KEY_WIDTH), F32),
                   jax.ShapeDtypeStruct((1, GLA_KEY_WIDTH), F32)],
        scratch_shapes=[pltpu.VMEM((GLA_VALUE_WIDTH, GLA_HEAD_K), F32), pltpu.VMEM((GLA_VALUE_WIDTH, D_MODEL), F32),
                        pltpu.VMEM((GLA_VALUE_WIDTH, D_MODEL), BF16)],
        compiler_params=_params(),
    )(dh2, proj, o, states, scores, gkw, gkb, hw, w_out)


def _position():
    return lax.axis_index("x"), lax.axis_index("y"), lax.axis_index("c")


def _lead_slot(ref, d):
    return ref.at[d]


def _row_slot(rows):
    return lambda ref, d: ref.at[pl.ds(pl.multiple_of(d * rows, rows), rows)]


def _dim1_slot(size):
    return lambda ref, d: ref.at[:, pl.ds(pl.multiple_of(d * size, size), size)]


class _Gather:
    def __init__(self, in_refs, out_refs, slots, send_sems, recv_sems, local_sems):
        self.in_refs, self.out_refs, self.slots = in_refs, out_refs, slots
        self.send_sems, self.recv_sems, self.local_sems = send_sems, recv_sems, local_sems
        self.n = len(in_refs)
        x, y, c = _position()
        self.c = c
        self.me, self.sibling = (x, y, c), (x, y, 1 - c)
        self.near = [(1 - x, y), (x, 1 - y)]
        self.diagonal = (1 - x, 1 - y)
        self.relay_from = (x ^ c, y ^ (1 - c))
        self.relay_to = (x ^ (1 - c), y ^ c)

    def _copy(self, a, k, block, to, from_input=False):
        part = self.slots[a](self.out_refs[a], 4 * block[0] + 2 * block[1] + block[2])
        return pltpu.make_async_remote_copy(
            src_ref=self.in_refs[a] if from_input else part, dst_ref=part,
            send_sem=self.send_sems.at[a, k], recv_sem=self.recv_sems.at[a, k], device_id=to, device_id_type=MESH)

    def _mine(self):
        return [pltpu.make_async_copy(self.in_refs[a], self.slots[a](self.out_refs[a], 4 * self.me[0] + 2 * self.me[1]
                                                                    + self.me[2]), self.local_sems.at[a])
                for a in range(self.n)]

    def _first(self):
        first = [self._copy(a, 0, self.me, self.sibling, True) for a in range(self.n)]
        return first + [self._copy(a, 1 + j, self.me, (*chip, self.c), True)
                        for j, chip in enumerate(self.near) for a in range(self.n)]

    def _relayed(self):
        return [self._copy(a, 3, (*self.relay_from, self.c), (*self.relay_to, self.c)) for a in range(self.n)]

    def _passed(self, j):
        chip = self.near[j] if j < 2 else self.diagonal
        return [self._copy(a, 4 + j, (*chip, self.c), self.sibling) for a in range(self.n)]

    def start(self):
        for cp in self._mine() + self._first():
            cp.start()

    def forward(self):
        for j, chip in enumerate(self.near):
            for a in range(self.n):
                self._copy(a, 1 + j, (*chip, self.c), self.me).wait_recv()
        for cp in self._relayed() + self._passed(0) + self._passed(1):
            cp.start()

    def relay(self):
        pass

    def finish(self):
        for a in range(self.n):
            self._copy(a, 3, (*self.diagonal, self.c), self.me).wait_recv()
        for cp in self._passed(2):
            cp.start()
        for a in range(self.n):
            self._copy(a, 0, self.sibling, self.me).wait_recv()
        for j, chip in enumerate(self.near + [self.diagonal]):
            for a in range(self.n):
                self._copy(a, 4 + j, (*chip, 1 - self.c), self.me).wait_recv()
        for cp in self._first() + self._relayed() + self._passed(0) + self._passed(1) + self._passed(2):
            cp.wait_send()
        for cp in self._mine():
            cp.wait()


class _Exchange:
    def __init__(self, in_refs, out_refs, slots, send_sems, recv_sems, local_sems):
        self.in_refs, self.out_refs, self.slots = in_refs, out_refs, slots
        self.send_sems, self.recv_sems, self.local_sems = send_sems, recv_sems, local_sems
        self.n = len(in_refs)
        self.pos = _position()

    def _copies(self):
        x, y, c = self.pos
        me = 4 * x + 2 * y + c
        mine = [pltpu.make_async_copy(self.slots[a](self.in_refs[a], me), self.out_refs[a].at[me],
                                      self.local_sems.at[a]) for a in range(self.n)]
        remote = []
        for k in range(1, N_DEV):
            px, py, pc = x ^ (k >> 2), y ^ ((k >> 1) & 1), c ^ (k & 1)
            for a in range(self.n):
                remote.append(pltpu.make_async_remote_copy(
                    src_ref=self.slots[a](self.in_refs[a], 4 * px + 2 * py + pc), dst_ref=self.out_refs[a].at[me],
                    send_sem=self.send_sems.at[a, k - 1], recv_sem=self.recv_sems.at[a, k - 1],
                    device_id=(px, py, pc), device_id_type=MESH))
        return mine, remote

    def start(self):
        mine, remote = self._copies()
        for cp in mine + remote:
            cp.start()

    def forward(self):
        pass

    def relay(self):
        pass

    def finish(self):
        mine, remote = self._copies()
        for cp in remote:
            cp.wait_recv()
        for cp in remote:
            cp.wait_send()
        for cp in mine:
            cp.wait()


class _Rider:
    def __init__(self, kind, arrays, out_shapes, slots, scratch=None, forward_step=None):
        self.kind, self.arrays, self.slots = kind, list(arrays), slots
        self.n = len(self.arrays)
        hbm = pl.BlockSpec(memory_space=pl.ANY)
        self.in_specs = [hbm] * self.n
        self.out_specs = [hbm] * self.n
        self.out_shape = [jax.ShapeDtypeStruct(tuple(s), a.dtype) for s, a in zip(out_shapes, self.arrays)]
        self.scratch = scratch if scratch is not None else [
            pltpu.SemaphoreType.DMA((self.n, 7)), pltpu.SemaphoreType.DMA((self.n, 7)),
            pltpu.SemaphoreType.DMA((self.n,))]
        self.forward_step = forward_step
        self.relay_step = None

    def bind(self, in_refs, out_refs, scratch):
        return self.kind(in_refs, out_refs, self.slots, *scratch)


def _gather_rider(shards, full_shapes, slots, forward_step=None):
    return _Rider(_Gather, shards, full_shapes, slots, None, forward_step)


def _exchange_rider(sends, part_shapes, slots):
    return _Rider(_Exchange, sends, [(N_DEV,) + tuple(s) for s in part_shapes], slots)


def _split_refs(refs, n_in, n_out, n_scratch, rider):
    k = rider.n if rider is not None else 0
    ins, r_ins = refs[:n_in], refs[n_in:n_in + k]
    outs, r_outs = refs[n_in + k:n_in + k + n_out], refs[n_in + k + n_out:n_in + 2 * k + n_out]
    rest = refs[n_in + 2 * k + n_out:]
    scratch, r_scratch = rest[:n_scratch], rest[n_scratch:]
    comm = rider.bind(r_ins, r_outs, r_scratch) if rider is not None else None
    if comm is not None:
        comm.forward_step, comm.relay_step = rider.forward_step, rider.relay_step
    return ins + outs + scratch, comm


def _ride_before(comm, i, nt):
    if comm is not None:
        pl.when(i == 0)(comm.start)
        pl.when(i == (nt - 1 if comm.forward_step is None else min(comm.forward_step, nt - 1)))(comm.forward)
        pl.when(i == (nt - 1 if comm.relay_step is None else min(comm.relay_step, nt - 1)))(comm.relay)


def _ride_after(comm, i, nt):
    if comm is not None:
        pl.when(i == nt - 1)(comm.finish)


def _extend(specs, rider, field):
    return list(specs) + (getattr(rider, field) if rider is not None else [])


def _comm_call(name, rider):
    def body(*refs):
        _, comm = _split_refs(refs, 0, 0, 0, rider)
        comm.start()
        comm.forward()
        comm.relay()
        comm.finish()

    return pl.pallas_call(body, name=name, in_specs=rider.in_specs, out_specs=rider.out_specs,
                          out_shape=rider.out_shape, scratch_shapes=rider.scratch,
                          compiler_params=pltpu.CompilerParams(vmem_limit_bytes=VMEM_LIMIT))(*rider.arrays)


N_CHIPS = 4


class _TwoLevel:
    def __init__(self, in_refs, out_refs, slots, *scratch):
        self.in_refs, self.out_refs, self.slots = in_refs, out_refs, slots
        self.n = n = len(in_refs)
        self.own_bufs, self.recv_bufs, self.relay_bufs = scratch[:n], scratch[n:2 * n], scratch[2 * n:3 * n]
        self.swap_send, self.swap_recv, self.local_sems, self.chip_send, self.chip_recv = scratch[3 * n:]
        x, y, c = self.pos = _position()
        self.first = (x ^ (1 - c), y ^ c)
        self.second = (x ^ c, y ^ (1 - c))
        self.chip_index = lambda chip: 2 * chip[0] + chip[1]

    def _swap(self):
        x, y, c = self.pos
        return [pltpu.make_async_remote_copy(
            src_ref=self.slots[a](self.in_refs[a], 2 * q + 1 - c), dst_ref=self.recv_bufs[a].at[q],
            send_sem=self.swap_send.at[a, q], recv_sem=self.swap_recv.at[a, q],
            device_id=(x, y, 1 - c), device_id_type=MESH) for a in range(self.n) for q in range(N_CHIPS)]

    def _mine(self):
        c = self.pos[2]
        return [pltpu.make_async_copy(self.slots[a](self.in_refs[a], 2 * q + c), self.own_bufs[a].at[q],
                                      self.local_sems.at[a, q]) for a in range(self.n) for q in range(N_CHIPS)]

    def _to_chip(self, a, k, src, dst, chip):
        return pltpu.make_async_remote_copy(
            src_ref=src, dst_ref=dst, send_sem=self.chip_send.at[a, k], recv_sem=self.chip_recv.at[a, k],
            device_id=(*chip, self.pos[2]), device_id_type=MESH)

    def _first_wave(self):
        x, y, _ = self.pos
        diagonal = self.chip_index((1 - x, 1 - y))
        passed_on = [self._to_chip(a, 1, self.own_bufs[a].at[diagonal], self.relay_bufs[a], self.first)
                     for a in range(self.n)]
        return passed_on + [self._to_chip(a, 0, self.own_bufs[a].at[self.chip_index(self.first)],
                                          self.out_refs[a].at[1], self.first) for a in range(self.n)]

    def _second_wave(self):
        return [self._to_chip(a, 2, self.own_bufs[a].at[self.chip_index(self.second)], self.out_refs[a].at[2],
                              self.second) for a in range(self.n)]

    def _own(self):
        x, y, _ = self.pos
        return [pltpu.make_async_copy(self.own_bufs[a].at[2 * x + y], self.out_refs[a].at[0],
                                      self.local_sems.at[a, N_CHIPS]) for a in range(self.n)]

    def start(self):
        for cp in self._swap() + self._mine():
            cp.start()

    def forward(self):
        swap, mine = self._swap(), self._mine()
        for a in range(self.n):
            for q in range(N_CHIPS):
                mine[a * N_CHIPS + q].wait()
                swap[a * N_CHIPS + q].wait_recv()
                self.own_bufs[a][q] = (self.own_bufs[a][q].astype(F32)
                                       + self.recv_bufs[a][q].astype(F32)).astype(BF16)
        for cp in self._first_wave() + self._own():
            cp.start()

    def relay(self):
        second = self.chip_index(self.second)
        for a in range(self.n):
            self._to_chip(a, 1, self.relay_bufs[a], self.relay_bufs[a], self.first).wait_recv()
            self.own_bufs[a][second] = (self.own_bufs[a][second].astype(F32)
                                        + self.relay_bufs[a][...].astype(F32)).astype(BF16)
        for cp in self._second_wave():
            cp.start()

    def finish(self):
        for a in range(self.n):
            self._to_chip(a, 0, self.out_refs[a].at[1], self.out_refs[a].at[1], self.first).wait_recv()
            self._to_chip(a, 2, self.out_refs[a].at[2], self.out_refs[a].at[2], self.second).wait_recv()
        for cp in self._first_wave() + self._second_wave() + self._swap():
            cp.wait_send()
        for cp in self._own():
            cp.wait()


def _two_level_rider(sends, part_shapes, slots, forward_step=None, relay_step=None):
    n = len(sends)
    bufs = [pltpu.VMEM((N_CHIPS,) + tuple(s), a.dtype) for s, a in zip(part_shapes, sends)]
    relay_bufs = [pltpu.VMEM(tuple(s), a.dtype) for s, a in zip(part_shapes, sends)]
    scratch = bufs + bufs + relay_bufs + [
        pltpu.SemaphoreType.DMA((n, N_CHIPS)), pltpu.SemaphoreType.DMA((n, N_CHIPS)),
        pltpu.SemaphoreType.DMA((n, N_CHIPS + 1)), pltpu.SemaphoreType.DMA((n, 3)), pltpu.SemaphoreType.DMA((n, 3))]
    rider = _Rider(_TwoLevel, sends, [(3,) + tuple(s) for s in part_shapes], slots, scratch, forward_step)
    rider.relay_step = relay_step
    return rider


class _Joined:
    def __init__(self, first, second):
        self.first, self.second = first, second

    def start(self):
        self.first.start()
        self.second.start()

    def forward(self):
        self.first.forward()
        self.second.forward()

    def relay(self):
        self.first.relay()
        self.second.relay()

    def finish(self):
        self.first.finish()
        self.second.finish()


class _JoinedRider:
    def __init__(self, first, second):
        self.first, self.second = first, second
        self.n = first.n + second.n
        self.arrays = first.arrays + second.arrays
        self.in_specs = first.in_specs + second.in_specs
        self.out_specs = first.out_specs + second.out_specs
        self.out_shape = first.out_shape + second.out_shape
        self.scratch = first.scratch + second.scratch
        self.forward_step = first.forward_step
        self.relay_step = first.relay_step

    def bind(self, in_refs, out_refs, scratch):
        k, s = self.first.n, len(self.first.scratch)
        return _Joined(self.first.bind(in_refs[:k], out_refs[:k], scratch[:s]),
                       self.second.bind(in_refs[k:], out_refs[k:], scratch[s:]))


def _adamw(w, g, m, v):
    m = ADAM_B1 * m + (1.0 - ADAM_B1) * g
    v = ADAM_B2 * v + (1.0 - ADAM_B2) * (g * g)
    m_hat = m / (1.0 - ADAM_B1 ** ADAM_STEP)
    v_hat = v / (1.0 - ADAM_B2 ** ADAM_STEP)
    delta = -ADAM_LR * (m_hat / (jnp.sqrt(v_hat) + ADAM_EPS) + ADAM_WD * w)
    return delta, m, v


def _sum_parts(parts_ref, index=()):
    g = parts_ref[(0,) + index].astype(F32)
    for s in range(1, parts_ref.shape[0]):
        g = g + parts_ref[(s,) + index].astype(F32)
    return g


def _adamw_group_call(name, groups):
    k = len(groups)

    def body(*refs):
        ins, outs = refs[:4 * k], refs[4 * k:]
        for i in range(k):
            parts_ref, w_ref, m_ref, v_ref = ins[4 * i:4 * i + 4]
            g = _sum_parts(parts_ref)
            delta, m_new, v_new = _adamw(w_ref[...], g, m_ref[...], v_ref[...])
            for out_ref, value in zip(outs[4 * i:4 * i + 4], (g, delta, m_new, v_new)):
                out_ref[...] = value

    vmem = pl.BlockSpec(memory_space=pltpu.VMEM)
    res = pl.pallas_call(
        body, name=name, in_specs=[vmem] * (4 * k), out_specs=[vmem] * (4 * k),
        out_shape=[jax.ShapeDtypeStruct(grp[1].shape, F32) for grp in groups for _ in range(4)],
        compiler_params=pltpu.CompilerParams(vmem_limit_bytes=VMEM_LIMIT),
    )(*[a for grp in groups for a in grp])
    return [res[4 * i:4 * i + 4] for i in range(k)]


def _adamw_slabs_call(name, parts, w, m, v, rider=None):
    def main(parts_ref, w_ref, m_ref, v_ref, g_ref, delta_ref, m_out, v_out):
        g = _sum_parts(parts_ref)
        delta, m_new, v_new = _adamw(w_ref[...], g, m_ref[...], v_ref[...])
        g_ref[...] = g
        delta_ref[...] = delta
        m_out[...] = m_new
        v_out[...] = v_new

    def body(*refs):
        own, comm = _split_refs(refs, 4, 4, 0, rider)
        if comm is not None:
            comm.start()
        main(*own)
        if comm is not None:
            comm.forward()
            comm.relay()
            comm.finish()

    vmem = pl.BlockSpec(memory_space=pltpu.VMEM)
    return pl.pallas_call(
        body, name=name, in_specs=_extend([vmem] * 4, rider, "in_specs"),
        out_specs=_extend([vmem] * 4, rider, "out_specs"),
        out_shape=_extend([jax.ShapeDtypeStruct(w.shape, F32)] * 4, rider, "out_shape"),
        scratch_shapes=_extend([], rider, "scratch"),
        compiler_params=pltpu.CompilerParams(vmem_limit_bytes=VMEM_LIMIT),
    )(parts, w, m, v, *_extend([], rider, "arrays"))


WIDE_ROWS = 8
NARROW_ROWS = 40
NARROW_GKW_ROW = 8
NARROW_GKB_ROW = 24
NARROW_HW_ROW = 32
GROUP_SHARD = POOL_GROUP_DIM // N_DEV
KEY_SHARD = GLA_KEY_WIDTH // N_DEV
HEAD_V_SHARD = GLA_HEAD_V // N_DEV


def _small_adamw_call(wide, narrow, w, m, v):
    names = ("norm_w", "pool_scale", "final_norm_w", "pool_group_b", "gla_gk_w", "gla_gk_b", "gla_head_norm_w")
    where = {
        "norm_w": (0, slice(0, 2), slice(None)),
        "pool_scale": (0, slice(2, 3), slice(None)),
        "final_norm_w": (0, slice(3, 4), slice(None)),
        "pool_group_b": (1, slice(0, POOL_GROUPS), slice(0, GROUP_SHARD)),
        "gla_gk_w": (1, slice(NARROW_GKW_ROW, NARROW_GKW_ROW + GLA_GATE_RANK), slice(0, KEY_SHARD)),
        "gla_gk_b": (1, slice(NARROW_GKB_ROW, NARROW_GKB_ROW + 1), slice(0, KEY_SHARD)),
        "gla_head_norm_w": (1, slice(NARROW_HW_ROW, NARROW_HW_ROW + 1), slice(0, HEAD_V_SHARD)),
    }
    k = len(names)

    def body(*refs):
        parts = refs[0:2]
        w_refs, m_refs, v_refs = refs[2:2 + k], refs[2 + k:2 + 2 * k], refs[2 + 2 * k:2 + 3 * k]
        outs = refs[2 + 3 * k:]
        loss_ref = outs[0]
        loss_ref[...] = _sum_parts(parts[0], (slice(4, 5), slice(0, 1)))
        for i, name in enumerate(names):
            buf, rows, cols = where[name]
            g = _sum_parts(parts[buf], (rows, cols))
            delta, m_new, v_new = _adamw(w_refs[i][...], g, m_refs[i][...], v_refs[i][...])
            outs[1 + i][...] = g
            outs[1 + k + i][...] = delta
            outs[1 + 2 * k + i][...] = m_new
            outs[1 + 3 * k + i][...] = v_new

    vmem = pl.BlockSpec(memory_space=pltpu.VMEM)
    shapes = [jax.ShapeDtypeStruct(w[n].shape, F32) for n in names]
    res = pl.pallas_call(
        body, name="adamw_small", in_specs=[vmem] * (2 + 3 * k), out_specs=[vmem] * (1 + 4 * k),
        out_shape=[jax.ShapeDtypeStruct((1, 1), F32)] + shapes * 4,
    )(wide, narrow, *[w[n] for n in names], *[m[n] for n in names], *[v[n] for n in names])
    unzip = lambda j: dict(zip(names, res[1 + j * k:1 + (j + 1) * k]))
    return res[0], unzip(0), unzip(1), unzip(2), unzip(3)


def kernel(x, norm_w, pool_in_w, pool_group_w, pool_group_b, pool_scale, pool_out_w, gla_in_w, gla_gk_w, gla_gk_b, gla_head_norm_w, gla_out_w, final_norm_w, loss_target, m_norm_w, m_pool_in_w, m_pool_group_w, m_pool_group_b, m_pool_scale, m_pool_out_w, m_gla_in_w, m_gla_gk_w, m_gla_gk_b, m_gla_head_norm_w, m_gla_out_w, m_final_norm_w, v_norm_w, v_pool_in_w, v_pool_group_w, v_pool_group_b, v_pool_scale, v_pool_out_w, v_gla_in_w, v_gla_gk_w, v_gla_gk_b, v_gla_head_norm_w, v_gla_out_w, v_final_norm_w):
    w = dict(norm_w=norm_w, pool_in_w=pool_in_w, pool_group_w=pool_group_w, pool_group_b=pool_group_b,
             pool_scale=pool_scale, pool_out_w=pool_out_w, gla_in_w=gla_in_w, gla_gk_w=gla_gk_w, gla_gk_b=gla_gk_b,
             gla_head_norm_w=gla_head_norm_w, gla_out_w=gla_out_w, final_norm_w=final_norm_w)
    m = dict(norm_w=m_norm_w, pool_in_w=m_pool_in_w, pool_group_w=m_pool_group_w, pool_group_b=m_pool_group_b,
             pool_scale=m_pool_scale, pool_out_w=m_pool_out_w, gla_in_w=m_gla_in_w, gla_gk_w=m_gla_gk_w,
             gla_gk_b=m_gla_gk_b, gla_head_norm_w=m_gla_head_norm_w, gla_out_w=m_gla_out_w,
             final_norm_w=m_final_norm_w)
    v = dict(norm_w=v_norm_w, pool_in_w=v_pool_in_w, pool_group_w=v_pool_group_w, pool_group_b=v_pool_group_b,
             pool_scale=v_pool_scale, pool_out_w=v_pool_out_w, gla_in_w=v_gla_in_w, gla_gk_w=v_gla_gk_w,
             gla_gk_b=v_gla_gk_b, gla_head_norm_w=v_gla_head_norm_w, gla_out_w=v_gla_out_w,
             final_norm_w=v_final_norm_w)
    col_shard = GLA_IN_WIDTH // N_DEV
    row_shard = D_MODEL // N_DEV

    def lanes(a):
        return jnp.pad(a, [(0, 0)] * (a.ndim - 1) + [(0, LANES - a.shape[-1])])

    small_in = jnp.concatenate([lanes(pool_group_b[0]), lanes(gla_gk_b), lanes(gla_head_norm_w),
                                jnp.zeros((2, LANES), F32)], axis=0)
    in_cols = 2 * POOL_WIDTH // N_DEV
    pool_in, pool_gw, pool_out, small_all = _comm_call("pool_weights_all_gather", _gather_rider(
        [pool_in_w[0].astype(BF16), pool_group_w[0].astype(BF16), pool_out_w[0].astype(BF16), small_in],
        [(D_MODEL, 2 * POOL_WIDTH), (POOL_GROUPS, POOL_GROUP_DIM, POOL_GROUP_DIM), (POOL_WIDTH, D_MODEL),
         (N_DEV, 8, LANES)],
        [_dim1_slot(in_cols), _dim1_slot(GROUP_SHARD), _row_slot(row_shard), _lead_slot]))
    pool_gb = jnp.transpose(small_all[:, 0:POOL_GROUPS, :GROUP_SHARD], (1, 0, 2)).reshape(1, POOL_WIDTH)
    gla_gkb = small_all[:, POOL_GROUPS, :KEY_SHARD].reshape(1, GLA_KEY_WIDTH)
    gla_hw = jnp.tile(small_all[:, POOL_GROUPS + 1, :HEAD_V_SHARD].reshape(1, GLA_HEAD_V), (1, GLA_HEADS))
    nw0, nw1, wf = norm_w[0:1], norm_w[1:2], final_norm_w.reshape(1, D_MODEL)
    xs, target = x[0], loss_target[0]

    h1, pool_y, pool_silu, pool_dsilu, pooled, mixed, gla_in_parts, gkw_parts, gla_out = _pool_fwd_call(
        xs, nw0, pool_in, pool_gw, pool_gb, pool_scale, pool_out, _gather_rider(
            [jnp.transpose(gla_in_w[0]).astype(BF16), gla_gk_w[0].astype(BF16), gla_out_w[0].astype(BF16)],
            [(N_DEV, col_shard, D_MODEL), (N_DEV, GLA_GATE_RANK, KEY_SHARD), (GLA_VALUE_WIDTH, D_MODEL)],
            [_lead_slot, _lead_slot, _row_slot(row_shard)], GATHER_RELAY_STEP))
    gla_in = gla_in_parts.reshape(GLA_IN_WIDTH, D_MODEL)
    gla_gkw = jnp.pad(jnp.transpose(gkw_parts, (1, 0, 2)).reshape(GLA_GATE_RANK, GLA_KEY_WIDTH),
                      ((0, GLA_LOW_PAD - GLA_GATE_RANK), (0, 0)))
    dh2, proj, o, states, scores, loss_part, dwf = _gla_fwd_call(h1, nw1, gla_in, gla_gkw, gla_gkb, gla_hw, gla_out,
                                                                 wf, target)

    dproj, d_gla_out, dhw, dgkw, dgkb = _gla_bwd_call(dh2, proj, o, states, scores, gla_gkw, gla_gkb, gla_hw,
                                                      gla_out)
    dh1, d_gla_in, dnw1 = _inproj_bwd_call("gla_in_bwd", dproj, h1, nw1, gla_in, dh2, transposed=True)
    slabs = col_shard * D_MODEL // (BF16_ROWS * LANES)
    gla_in_send = d_gla_in.reshape(N_DEV, slabs, BF16_ROWS, LANES)
    dp, d_pool_out, dgw, dgb, dsc, landed_gla_in, landed_gla_out = _pool_bwd_call(
        dh1, pool_y, pool_silu, pool_dsilu, pooled, mixed, pool_gw, pool_scale, pool_out,
        _two_level_rider([gla_in_send, d_gla_out], [(slabs, BF16_ROWS, LANES), (row_shard, D_MODEL)],
                         [_lead_slot, _row_slot(row_shard)], TWO_LEVEL_ADD_STEP, TWO_LEVEL_RELAY_STEP))
    grad_x, d_pool_in, dnw0 = _inproj_bwd_call("pool_in_bwd", dp, xs, nw0, pool_in, dh1)

    wide = jnp.concatenate([
        dnw0, dnw1, dsc, dwf, jnp.pad(loss_part[0:1, 0:1], ((0, 0), (0, D_MODEL - 1))),
        jnp.zeros((WIDE_ROWS - 5, D_MODEL), F32)], axis=0)

    def rows8(a):
        return jnp.pad(lanes(a), ((0, 0), (0, -a.shape[1] % 8), (0, 0)))

    narrow = jnp.concatenate([
        rows8(jnp.transpose(dgb.reshape(POOL_GROUPS, N_DEV, GROUP_SHARD), (1, 0, 2))),
        rows8(jnp.transpose(dgkw[:GLA_GATE_RANK].reshape(GLA_GATE_RANK, N_DEV, KEY_SHARD), (1, 0, 2))),
        rows8(dgkb.reshape(N_DEV, 1, KEY_SHARD)),
        rows8(dhw.reshape(GLA_HEADS, GLA_HEAD_V).sum(axis=0).reshape(N_DEV, 1, HEAD_V_SHARD)),
    ], axis=1)
    last_exchange = _JoinedRider(
        _two_level_rider([d_pool_in, d_pool_out, dgw],
                         [(D_MODEL, in_cols), (row_shard, D_MODEL), (POOL_GROUPS, GROUP_SHARD, POOL_GROUP_DIM)],
                         [_dim1_slot(in_cols), _row_slot(row_shard), _dim1_slot(GROUP_SHARD)]),
        _exchange_rider([wide, narrow], [(WIDE_ROWS, D_MODEL), (NARROW_ROWS, LANES)],
                        [lambda ref, d: ref, _lead_slot]))

    res = {}
    as_slabs = lambda t: jnp.transpose(t[0]).reshape(slabs, BF16_ROWS, LANES)
    *outs, landed_pool_in, landed_pool_out, landed_gw, landed_wide, landed_narrow = _adamw_slabs_call(
        "adamw_gla_in_w", landed_gla_in, as_slabs(gla_in_w), as_slabs(m_gla_in_w), as_slabs(v_gla_in_w),
        last_exchange)
    res["gla_in_w"] = [jnp.transpose(t.reshape(col_shard, D_MODEL))[None] for t in outs]
    rest = [("pool_in_w", landed_pool_in, (D_MODEL, in_cols)),
            ("pool_group_w", landed_gw, (POOL_GROUPS * GROUP_SHARD, POOL_GROUP_DIM)),
            ("pool_out_w", landed_pool_out, (row_shard, D_MODEL)), ("gla_out_w", landed_gla_out, (row_shard, D_MODEL))]
    updates = _adamw_group_call("adamw_matrices", [
        (parts.reshape((parts.shape[0],) + shape), w[name].reshape(shape), m[name].reshape(shape),
         v[name].reshape(shape)) for name, parts, shape in rest])
    for (name, _, _), outs in zip(rest, updates):
        res[name] = [t.reshape(w[name].shape) for t in outs]
    small_shapes ={"norm_w": (2, D_MODEL), "pool_scale": (1, D_MODEL), "final_norm_w": (1, D_MODEL),
                    "pool_group_b": (POOL_GROUPS, GROUP_SHARD), "gla_gk_w": (GLA_GATE_RANK, KEY_SHARD),
                    "gla_gk_b": (1, KEY_SHARD), "gla_head_norm_w": (1, HEAD_V_SHARD)}
    as_small = lambda t: {n: t[n].reshape(s) for n, s in small_shapes.items()}
    loss, *small_outs = _small_adamw_call(landed_wide, landed_narrow, as_small(w), as_small(m), as_small(v))
    for name in small_shapes:
        res[name] = [t[name].reshape(w[name].shape) for t in small_outs]
    order = ("norm_w", "pool_in_w", "pool_group_w", "pool_group_b", "pool_scale", "pool_out_w", "gla_in_w",
             "gla_gk_w", "gla_gk_b", "gla_head_norm_w", "gla_out_w", "final_norm_w")
    return (loss.reshape(()), grad_x[None], *[res[n][0] for n in order], *[res[n][1] for n in order],
            *[res[n][2] for n in order], *[res[n][3] for n in order])
```

```python
import jax
import jax.numpy as jnp
from jax import lax
from jax.experimental import pallas as pl
from jax.experimental.pallas import tpu as pltpu

F32 = jnp.float32
BF16 = jnp.bfloat16
MESH = pl.DeviceIdType.MESH

N_DEV = 8
D_MODEL = 1024
POOL_WIDTH = 1024
POOL_GROUPS = 4
POOL_GROUP_DIM = 256
POOL_HALO = 16
GLA_HEADS = 4
GLA_HEAD_K = 128
GLA_HEAD_V = 256
GLA_KEY_WIDTH = 512
GLA_VALUE_WIDTH = 1024
GLA_GATE_RANK = 16
GLA_IN_WIDTH = 3088
GLA_IN_PAD = 3200
GLA_SAVED_Z = GLA_IN_PAD
GLA_SAVED_C = GLA_SAVED_Z + 512
GLA_SAVED_WIDTH = GLA_SAVED_C + 512
GLA_LOW_PAD = 128
GLA_QKVG_WIDTH = 3072
CHUNK = 64
GATE_NORMALIZER = 16.0
RMS_EPS = 1e-6
Q_SCALE = GLA_HEAD_K ** -0.5

ADAM_LR = 0.001
ADAM_B1 = 0.9
ADAM_B2 = 0.999
ADAM_EPS = 1e-08
ADAM_WD = 0.01
ADAM_STEP = 10

LANES = 128
BF16_ROWS = 16
VMEM_LIMIT = 56 * 1024 * 1024
ROW_TILE = 256
GLA_FWD_ROW_TILE = 512
MATMUL_ROW_TILE = 512
ROW_MAJOR_PIECE = 776
GATHER_RELAY_STEP = 5
TWO_LEVEL_ADD_STEP = 1
TWO_LEVEL_RELAY_STEP = 4


def _dot_nn(a, b):
    return lax.dot_general(a, b, (((1,), (0,)), ((), ())), preferred_element_type=F32)


def _dot_nt(a, b):
    return lax.dot_general(a, b, (((1,), (1,)), ((), ())), preferred_element_type=F32)


def _dot_tn(a, b):
    return lax.dot_general(a, b, (((0,), (0,)), ((), ())), preferred_element_type=F32)


def _rms(x):
    rstd = lax.rsqrt(jnp.mean(x * x, axis=-1, keepdims=True) + RMS_EPS)
    return x * rstd, rstd


def _rms_bwd(dxhat, xhat, rstd):
    return rstd * (dxhat - xhat * jnp.mean(dxhat * xhat, axis=-1, keepdims=True))


def _sigmoid(x):
    return 1.0 / (1.0 + jnp.exp(-x))


def _params(sem=("arbitrary",)):
    return pltpu.CompilerParams(dimension_semantics=sem, vmem_limit_bytes=VMEM_LIMIT)


def _full(shape):
    return pl.BlockSpec(shape, lambda i: (0,) * len(shape))


def _const(shape):
    return pl.BlockSpec(shape, lambda i: (0,) * len(shape), pipeline_mode=pl.Buffered(1))


def _window_sums(ext, forward):
    n = ext.shape[0]
    outs = []
    for g in range(POOL_GROUPS):
        s = ext[:, g * POOL_GROUP_DIM:(g + 1) * POOL_GROUP_DIM]
        for k in range(g + 1):
            shift = (1 << k) if forward else n - (1 << k)
            s = s + pltpu.roll(s, shift, axis=0)
        outs.append(s[:n - POOL_HALO])
    return outs


def _inv_count(row0, tm):
    row = row0 + lax.broadcasted_iota(jnp.int32, (tm, 1), 0)
    return [1.0 / jnp.minimum(row + 1, 2 << g).astype(F32) for g in range(POOL_GROUPS)]


def _pool_mix(u, u_prev, row0, gw_ref, gb):
    tm = u.shape[0]
    sums = _window_sums(jnp.concatenate([u, u_prev], axis=0), True)
    inv = _inv_count(row0, tm)
    pooled, mixed = [], []
    for g in range(POOL_GROUPS):
        ug = u[:, g * POOL_GROUP_DIM:(g + 1) * POOL_GROUP_DIM]
        pg = (sums[g] * inv[g] - ug).astype(BF16)
        pooled.append(pg)
        mixed.append(_dot_nn(pg, gw_ref[g]))
    return pooled, jnp.concatenate(mixed, axis=1) + gb


def _pool_fwd_call(x, nw, w_in, gw, gb, sc, w_out, rider=None):
    seq = x.shape[0]
    tm = min(MATMUL_ROW_TILE, seq)
    nt = seq // tm

    def main(x_ref, nw_ref, win_ref, gw_ref, gb_ref, sc_ref, wout_ref, h_ref, y_ref, silu_ref, dsilu_ref,
             pooled_ref, mixed_ref, halo_ref):
        i = pl.program_id(0)

        @pl.when(i == 0)
        def _():
            halo_ref[...] = jnp.zeros_like(halo_ref)

        xt = x_ref[...]
        xhat, _ = _rms(xt)
        n = (xhat * nw_ref[...]).astype(BF16)
        p = _dot_nn(n, win_ref[...])
        u = p[:, :POOL_WIDTH]
        gate = p[:, POOL_WIDTH:]
        sg = _sigmoid(gate)
        silu = gate * sg
        silu_ref[...] = silu
        dsilu_ref[...] = sg * (1.0 + gate * (1.0 - sg))
        pooled, mixed = _pool_mix(u, halo_ref[...], i * tm, gw_ref, gb_ref[...])
        pooled_ref[...] = jnp.concatenate(pooled, axis=1)
        mixed_ref[...] = mixed
        halo_ref[...] = u[tm - POOL_HALO:, :]
        y = (mixed * sc_ref[...] * silu).astype(BF16)
        y_ref[...] = y
        h_ref[...] = xt + _dot_nn(y, wout_ref[...])

    def body(*refs):
        own, comm = _split_refs(refs, 7, 6, 1, rider)
        _ride_before(comm, pl.program_id(0), nt)
        main(*own)
        _ride_after(comm, pl.program_id(0), nt)

    return pl.pallas_call(
        body, name="pool_fwd", grid=(nt,),
        in_specs=_extend([pl.BlockSpec((tm, D_MODEL), lambda i: (i, 0)), _const((1, D_MODEL)),
                          _const((D_MODEL, 2 * POOL_WIDTH)), _const((POOL_GROUPS, POOL_GROUP_DIM, POOL_GROUP_DIM)),
                          _const((1, POOL_WIDTH)), _const((1, POOL_WIDTH)), _const((POOL_WIDTH, D_MODEL))],
                         rider, "in_specs"),
        out_specs=_extend([pl.BlockSpec((tm, D_MODEL), lambda i: (i, 0))] * 6, rider, "out_specs"),
        out_shape=_extend([jax.ShapeDtypeStruct((seq, D_MODEL), F32), jax.ShapeDtypeStruct((seq, POOL_WIDTH), BF16),
                           jax.ShapeDtypeStruct((seq, POOL_WIDTH), F32), jax.ShapeDtypeStruct((seq, POOL_WIDTH), F32),
                           jax.ShapeDtypeStruct((seq, POOL_WIDTH), BF16),
                           jax.ShapeDtypeStruct((seq, POOL_WIDTH), F32)], rider, "out_shape"),
        scratch_shapes=_extend([pltpu.VMEM((POOL_HALO, POOL_WIDTH), F32)], rider, "scratch"),
        compiler_params=_params(),
    )(x, nw, w_in, gw, gb, sc, w_out, *_extend([], rider, "arrays"))


def _pool_bwd_call(dh, y, silu, dsilu, pooled, mixed, gw, sc, w_out, rider=None):
    seq = dh.shape[0]
    tm = min(MATMUL_ROW_TILE, seq)
    nt = seq // tm

    def main(dh_ref, y_ref, silu_ref, dsilu_ref, pooled_ref, mixed_ref, gw_ref, sc_ref, wout_ref,
             dp_ref, dwout_hbm, dgw_hbm, dgb_ref, dsc_ref, carry_ref, dwout_acc, dgw_acc, dwout_stage, dgw_stage):
        i = pl.program_id(0)
        t = nt - 1 - i

        @pl.when(i == 0)
        def _():
            carry_ref[...] = jnp.zeros_like(carry_ref)
            dwout_acc[...] = jnp.zeros_like(dwout_acc)
            dgw_acc[...] = jnp.zeros_like(dgw_acc)
            dgb_ref[...] = jnp.zeros_like(dgb_ref)
            dsc_ref[...] = jnp.zeros_like(dsc_ref)

        silu = silu_ref[...]
        pooled = [pooled_ref[:, g * POOL_GROUP_DIM:(g + 1) * POOL_GROUP_DIM] for g in range(POOL_GROUPS)]
        sc = sc_ref[...]
        dhb = dh_ref[...].astype(BF16)
        dwout_acc[...] += _dot_tn(y_ref[...], dhb)
        dy = _dot_nt(dhb, wout_ref[...])
        dmixed = dy * sc * silu
        dy_mixed = dy * mixed_ref[...]
        dsc_ref[...] += jnp.sum(dy_mixed * silu, axis=0, keepdims=True)
        dgate = dy_mixed * sc * dsilu_ref[...]
        dgb_ref[...] += jnp.sum(dmixed, axis=0, keepdims=True)
        inv = _inv_count(t * tm, tm)
        dpooled, scaled = [], []
        for g in range(POOL_GROUPS):
            dmg = dmixed[:, g * POOL_GROUP_DIM:(g + 1) * POOL_GROUP_DIM].astype(BF16)
            dgw_acc[g] += _dot_tn(pooled[g], dmg)
            dpg = _dot_nt(dmg, gw_ref[g])
            dpooled.append(dpg)
            scaled.append(dpg * inv[g])
        r = jnp.concatenate(scaled, axis=1)
        sums = _window_sums(jnp.concatenate([r, carry_ref[...]], axis=0), False)
        carry_ref[...] = r[:POOL_HALO, :]
        du = jnp.concatenate([sums[g] - dpooled[g] for g in range(POOL_GROUPS)], axis=1)
        dp_ref[...] = jnp.concatenate([du, dgate], axis=1).astype(BF16)

        @pl.when(i == nt - 1)
        def _():
            dwout_stage[...] = dwout_acc[...].astype(BF16)
            dgw_stage[...] = dgw_acc[...].astype(BF16)
            pltpu.sync_copy(dwout_stage, dwout_hbm)
            pltpu.sync_copy(dgw_stage, dgw_hbm)

    def body(*refs):
        own, comm = _split_refs(refs, 9, 5, 5, rider)
        _ride_before(comm, pl.program_id(0), nt)
        main(*own)
        _ride_after(comm, pl.program_id(0), nt)

    rev = lambda i: (nt - 1 - i, 0)
    return pl.pallas_call(
        body, name="pool_bwd", grid=(nt,),
        in_specs=_extend([pl.BlockSpec((tm, D_MODEL), rev)] * 6
                         + [_const((POOL_GROUPS, POOL_GROUP_DIM, POOL_GROUP_DIM)), _const((1, POOL_WIDTH)),
                            _const((POOL_WIDTH, D_MODEL))], rider, "in_specs"),
        out_specs=_extend([pl.BlockSpec((tm, 2 * POOL_WIDTH), rev), pl.BlockSpec(memory_space=pl.ANY),
                           pl.BlockSpec(memory_space=pl.ANY), _full((1, POOL_WIDTH)), _full((1, POOL_WIDTH))],
                          rider, "out_specs"),
        out_shape=_extend([jax.ShapeDtypeStruct((seq, 2 * POOL_WIDTH), BF16),
                           jax.ShapeDtypeStruct((POOL_WIDTH, D_MODEL), BF16),
                           jax.ShapeDtypeStruct((POOL_GROUPS, POOL_GROUP_DIM, POOL_GROUP_DIM), BF16),
                           jax.ShapeDtypeStruct((1, POOL_WIDTH), F32), jax.ShapeDtypeStruct((1, POOL_WIDTH), F32)],
                          rider, "out_shape"),
        scratch_shapes=_extend([pltpu.VMEM((POOL_HALO, POOL_WIDTH), F32), pltpu.VMEM((POOL_WIDTH, D_MODEL), F32),
                                pltpu.VMEM((POOL_GROUPS, POOL_GROUP_DIM, POOL_GROUP_DIM), F32),
                                pltpu.VMEM((POOL_WIDTH, D_MODEL), BF16),
                                pltpu.VMEM((POOL_GROUPS, POOL_GROUP_DIM, POOL_GROUP_DIM), BF16)], rider, "scratch"),
        compiler_params=_params(),
    )(dh, y, silu, dsilu, pooled, mixed, gw, sc, w_out, *_extend([], rider, "arrays"))


def _rows_then_zeros(ref, lo, hi, rows):
    part = ref[lo:hi, :]
    return jnp.concatenate([part, jnp.zeros((rows - (hi - lo), part.shape[1]), part.dtype)], axis=0)


def _inproj_bwd_call(name, dproj, h_in, nw, w_in, dres, rider=None, transposed=False):
    seq = h_in.shape[0]
    width = dproj.shape[1]
    w_shape = tuple(w_in.shape)
    acc_shape = (width, D_MODEL) if transposed else w_shape
    whole = w_shape[0] // LANES * LANES
    tm = min(MATMUL_ROW_TILE, seq)
    nt = seq // tm
    lane_tiles = D_MODEL // LANES
    dw_shape = (w_shape[0] * lane_tiles, LANES) if transposed else w_shape
    pieces = [(lo, min(lo + ROW_MAJOR_PIECE, w_shape[0])) for lo in range(0, w_shape[0], ROW_MAJOR_PIECE)]

    def to_row_major(dw_acc, dw_stage, dw_lines):
        for lo, hi in pieces:
            for j in range(lane_tiles):
                dw_lines[pl.ds(j, hi - lo, stride=lane_tiles), :] = dw_acc[lo:hi, j * LANES:(j + 1) * LANES]
            dw_stage[lo * lane_tiles:hi * lane_tiles, :] = dw_lines[0:(hi - lo) * lane_tiles, :].astype(BF16)

    def main(dproj_ref, h_ref, nw_ref, win_ref, dres_ref, dh_ref, dw_hbm, dnw_ref, dw_acc, dw_stage, *dw_lines):
        i = pl.program_id(0)

        @pl.when(i == 0)
        def _():
            dw_acc[...] = jnp.zeros_like(dw_acc)
            dnw_ref[...] = jnp.zeros_like(dnw_ref)

        dpb = dproj_ref[...]
        if transposed:
            dn = _dot_nn(dpb[:, :whole], win_ref[0:whole, :])
            if whole < w_shape[0]:
                dn = dn + _dot_nn(dpb[:, whole:], _rows_then_zeros(win_ref, whole, w_shape[0], width - whole))
        else:
            dn = _dot_nt(dpb, win_ref[...])
        xhat, rstd = _rms(h_ref[...])
        nw_row = nw_ref[...]
        n = (xhat * nw_row).astype(BF16)
        dw_acc[...] += _dot_tn(dpb, n) if transposed else _dot_tn(n, dpb)
        dnw_ref[...] += jnp.sum(dn * xhat, axis=0, keepdims=True)
        dh_ref[...] = _rms_bwd(dn * nw_row, xhat, rstd) + dres_ref[...]

        @pl.when(i == nt - 1)
        def _():
            if transposed:
                to_row_major(dw_acc, dw_stage, *dw_lines)
            else:
                dw_stage[...] = dw_acc[...].astype(BF16)
            pltpu.sync_copy(dw_stage, dw_hbm)

    scratch = [pltpu.VMEM(acc_shape, F32), pltpu.VMEM(dw_shape, BF16)]
    if transposed:
        scratch.append(pltpu.VMEM((ROW_MAJOR_PIECE * lane_tiles, LANES), F32))

    def body(*refs):
        own, comm = _split_refs(refs, 5, 3, len(scratch), rider)
        _ride_before(comm, pl.program_id(0), nt)
        main(*own)
        _ride_after(comm, pl.program_id(0), nt)

    row = lambda i: (i, 0)
    return pl.pallas_call(
        body, name=name, grid=(nt,),
        in_specs=_extend([pl.BlockSpec((tm, width), row), pl.BlockSpec((tm, D_MODEL), row), _const((1, D_MODEL)),
                          _const(w_shape), pl.BlockSpec((tm, D_MODEL), row)], rider, "in_specs"),
        out_specs=_extend([pl.BlockSpec((tm, D_MODEL), row), pl.BlockSpec(memory_space=pl.ANY),
                           _full((1, D_MODEL))], rider, "out_specs"),
        out_shape=_extend([jax.ShapeDtypeStruct((seq, D_MODEL), F32), jax.ShapeDtypeStruct(dw_shape, BF16),
                           jax.ShapeDtypeStruct((1, D_MODEL), F32)], rider, "out_shape"),
        scratch_shapes=_extend(scratch, rider, "scratch"),
        compiler_params=_params(),
    )(dproj, h_in, nw, w_in, dres, *_extend([], rider, "arrays"))


def _chunk_scan(x, reverse):
    n = x.shape[0]
    pos = lax.broadcasted_iota(jnp.int32, (n, 1), 0) & (CHUNK - 1)
    k = 1
    while k < CHUNK:
        if reverse:
            x = x + jnp.where(pos < CHUNK - k, pltpu.roll(x, n - k, axis=0), 0.0)
        else:
            x = x + jnp.where(pos >= k, pltpu.roll(x, k, axis=0), 0.0)
        k *= 2
    return x


def _chunk_rows(j):
    return slice(j * CHUNK, (j + 1) * CHUNK)


def _kcols(h):
    return slice(h * GLA_HEAD_K, (h + 1) * GLA_HEAD_K)


def _vcols(h):
    return slice(h * GLA_HEAD_V, (h + 1) * GLA_HEAD_V)


def _chunk_masks(tm):
    idx_t = lax.broadcasted_iota(jnp.int32, (tm, tm), 0)
    idx_s = lax.broadcasted_iota(jnp.int32, (tm, tm), 1)
    same_chunk = (idx_t ^ idx_s) < CHUNK
    return same_chunk & (idx_t >= idx_s), same_chunk & (idx_t < idx_s)


class _GlaTerms:
    def __init__(self, kc, q, k, v, low_b, gkw_ref, gkb_ref, masks, saved=None):
        tm = q.shape[0]
        self.q = q * Q_SCALE
        self.k = k
        if saved is None:
            self.z = _dot_nn(low_b, gkw_ref[:, kc]) + gkb_ref[:, kc]
            log_g = (jnp.minimum(self.z, 0.0) - jnp.log(1.0 + jnp.exp(-jnp.abs(self.z)))) / GATE_NORMALIZER
            self.c = _chunk_scan(log_g, False)
        else:
            self.z, self.c = saved
        is_last = lax.broadcasted_iota(jnp.int32, (CHUNK, 1), 0) == CHUNK - 1
        self.c_last = [jnp.sum(jnp.where(is_last, self.c[_chunk_rows(j), :], 0.0), axis=0, keepdims=True)
                       for j in range(tm // CHUNK)]
        c_last_rows = jnp.concatenate([jnp.broadcast_to(r, (CHUNK, r.shape[1])) for r in self.c_last], axis=0)
        self.e_pos = jnp.exp(self.c)
        self.e_neg = jnp.exp(-self.c)
        self.e_rest = jnp.exp(c_last_rows - self.c)
        self.a_b = (self.q * self.e_pos).astype(BF16)
        self.b_b = (self.k * self.e_neg).astype(BF16)
        self.cn_b = (self.q * self.e_neg).astype(BF16)
        self.dp_b = (self.k * self.e_pos).astype(BF16)
        self.kd_b = (self.k * self.e_rest).astype(BF16)
        self.v_b = v.astype(BF16)
        self.lower, self.upper = masks

    def scores(self, kc=slice(None)):
        fwd = _dot_nt(self.a_b[:, kc], self.b_b[:, kc])
        bwd = _dot_nt(self.cn_b[:, kc], self.dp_b[:, kc])
        return jnp.where(self.lower, fwd, jnp.where(self.upper, bwd, 0.0)).astype(BF16)


def _gla_fwd_call(h1, nw, w_in, gkw, gkb, hw, w_out, wf, target):
    seq = h1.shape[0]
    tm = min(GLA_FWD_ROW_TILE, seq)
    nt = seq // tm
    cpt = tm // CHUNK
    n_chunks = seq // CHUNK

    def body(h_ref, nw_ref, win_ref, gkw_ref, gkb_ref, hw_ref, wout_ref, wf_ref, tgt_ref,
             dh2_ref, proj_ref, o_ref, st_ref, scores_ref, loss_ref, dwf_ref, state_ref):
        i = pl.program_id(0)

        @pl.when(i == 0)
        def _():
            state_ref[...] = jnp.zeros_like(state_ref)
            loss_ref[...] = jnp.zeros_like(loss_ref)
            dwf_ref[...] = jnp.zeros_like(dwf_ref)

        ht = h_ref[...]
        xhat, _ = _rms(ht)
        n = (xhat * nw_ref[...]).astype(BF16)
        sections = {}
        for name, lo, hi in (("low", GLA_QKVG_WIDTH, GLA_IN_PAD), ("qk", 0, 2 * GLA_KEY_WIDTH),
                             ("v", 2 * GLA_KEY_WIDTH, GLA_QKVG_WIDTH - GLA_VALUE_WIDTH),
                             ("gate", GLA_QKVG_WIDTH - GLA_VALUE_WIDTH, GLA_QKVG_WIDTH)):
            rows = (win_ref[lo:hi, :] if hi <= GLA_IN_WIDTH
                    else _rows_then_zeros(win_ref, lo, GLA_IN_WIDTH, hi - lo))
            sections[name] = _dot_nt(n, rows)
            proj_ref[:, lo:hi] = sections[name]
        low_b = sections["low"].astype(BF16)
        masks = _chunk_masks(tm)
        on_heads = []
        for h in range(GLA_HEADS):
            kc, vc = _kcols(h), _vcols(h)
            g = _GlaTerms(kc, sections["qk"][:, kc], sections["qk"][:, GLA_KEY_WIDTH:][:, kc], sections["v"][:, vc],
                          low_b, gkw_ref, gkb_ref, masks)
            srows = slice(h * GLA_HEAD_V, (h + 1) * GLA_HEAD_V)
            scores = g.scores()
            for b in range(tm // ROW_TILE):
                part = slice(b * ROW_TILE, (b + 1) * ROW_TILE)
                scores_ref[part, h * ROW_TILE:(h + 1) * ROW_TILE] = scores[part, part]
            o_intra = _dot_nn(scores, g.v_b)
            state = state_ref[srows, :]
            o_rows = []
            for j in range(cpt):
                r = _chunk_rows(j)
                st_ref[j, srows, :] = state
                o_rows.append(o_intra[r] + _dot_nt(g.a_b[r], state.astype(BF16)))
                decay = jnp.exp(g.c_last[j])
                state = state * decay + _dot_tn(g.v_b[r], g.kd_b[r])
            state_ref[srows, :] = state
            o_head = jnp.concatenate(o_rows, axis=0)
            o_ref[:, vc] = o_head
            proj_ref[:, GLA_SAVED_Z + kc.start:GLA_SAVED_Z + kc.stop] = g.z
            proj_ref[:, GLA_SAVED_C + kc.start:GLA_SAVED_C + kc.stop] = g.c
            on_heads.append(_rms(o_head)[0])
        gate = sections["gate"]
        on = jnp.concatenate(on_heads, axis=1) * hw_ref[...]
        y = (on * (gate * _sigmoid(gate))).astype(BF16)
        h2 = ht + _dot_nn(y, wout_ref[...])
        xhat2, rstd2 = _rms(h2)
        wf_row = wf_ref[...]
        err = xhat2 * wf_row - tgt_ref[...]
        loss_ref[...] += 0.5 * jnp.sum(err * err) / D_MODEL
        dout = err * (1.0 / D_MODEL)
        dwf_ref[...] += jnp.sum(dout * xhat2, axis=0, keepdims=True)
        dh2_ref[...] = _rms_bwd(dout * wf_row, xhat2, rstd2)

    row = lambda i: (i, 0)
    return pl.pallas_call(
        body, name="gla_fwd", grid=(nt,),
        in_specs=[pl.BlockSpec((tm, D_MODEL), row), _const((1, D_MODEL)), _const((GLA_IN_WIDTH, D_MODEL)),
                  _const((GLA_LOW_PAD, GLA_KEY_WIDTH)), _const((1, GLA_KEY_WIDTH)), _const((1, GLA_VALUE_WIDTH)),
                  _const((GLA_VALUE_WIDTH, D_MODEL)), _const((1, D_MODEL)), pl.BlockSpec((tm, D_MODEL), row)],
        out_specs=[pl.BlockSpec((tm, D_MODEL), row), pl.BlockSpec((tm, GLA_SAVED_WIDTH), row),
                   pl.BlockSpec((tm, GLA_VALUE_WIDTH), row),
                   pl.BlockSpec((cpt, GLA_VALUE_WIDTH, GLA_HEAD_K), lambda i: (i, 0, 0)),
                   pl.BlockSpec((tm, GLA_HEADS * ROW_TILE), row), _full((8, LANES)), _full((1, D_MODEL))],
        out_shape=[jax.ShapeDtypeStruct((seq, D_MODEL), F32), jax.ShapeDtypeStruct((seq, GLA_SAVED_WIDTH), F32),
                   jax.ShapeDtypeStruct((seq, GLA_VALUE_WIDTH), F32),
                   jax.ShapeDtypeStruct((n_chunks, GLA_VALUE_WIDTH, GLA_HEAD_K), F32),
                   jax.ShapeDtypeStruct((seq, GLA_HEADS * ROW_TILE), BF16),
                   jax.ShapeDtypeStruct((8, LANES), F32), jax.ShapeDtypeStruct((1, D_MODEL), F32)],
        scratch_shapes=[pltpu.VMEM((GLA_VALUE_WIDTH, GLA_HEAD_K), F32)],
        compiler_params=_params(),
    )(h1, nw, w_in, gkw, gkb, hw, w_out, wf, target)


def _gla_bwd_call(dh2, proj, o, states, scores, gkw, gkb, hw, w_out):
    seq = dh2.shape[0]
    tm = ROW_TILE
    nt = seq // tm
    cpt = tm // CHUNK

    def body(dh_ref, proj_ref, o_ref, st_ref, scores_ref, gkw_ref, gkb_ref, hw_ref, wout_ref,
             dproj_ref, dwout_hbm, dhw_ref, dgkw_ref, dgkb_ref, dstate_ref, dwout_acc, dwout_stage):
        i = pl.program_id(0)

        @pl.when(i == 0)
        def _():
            dstate_ref[...] = jnp.zeros_like(dstate_ref)
            dwout_acc[...] = jnp.zeros_like(dwout_acc)
            dhw_ref[...] = jnp.zeros_like(dhw_ref)
            dgkw_ref[...] = jnp.zeros_like(dgkw_ref)
            dgkb_ref[...] = jnp.zeros_like(dgkb_ref)

        dhb = dh_ref[...].astype(BF16)
        dy = _dot_nt(dhb, wout_ref[...])
        v0, g0 = 2 * GLA_KEY_WIDTH, GLA_QKVG_WIDTH - GLA_VALUE_WIDTH
        gate = proj_ref[:, g0:GLA_QKVG_WIDTH]
        low_b = proj_ref[:, GLA_QKVG_WIDTH:GLA_IN_PAD].astype(BF16)
        o = o_ref[...]
        hw_row = hw_ref[...]
        sg = _sigmoid(gate)
        silu = gate * sg
        don = dy * silu
        on_parts, do_parts, dhw_parts = [], [], []
        for h in range(GLA_HEADS):
            vc = _vcols(h)
            xh, rs = _rms(o[:, vc])
            on_parts.append(xh * hw_row[:, vc])
            dhw_parts.append(jnp.sum(don[:, vc] * xh, axis=0, keepdims=True))
            do_parts.append(_rms_bwd(don[:, vc] * hw_row[:, vc], xh, rs).astype(BF16))
        on = jnp.concatenate(on_parts, axis=1)
        dwout_acc[...] += _dot_tn((on * silu).astype(BF16), dhb)
        dhw_ref[...] += jnp.concatenate(dhw_parts, axis=1)
        dproj_ref[:, g0:GLA_QKVG_WIDTH] = (dy * on * (sg * (1.0 + gate * (1.0 - sg)))).astype(BF16)

        last_row = lax.broadcasted_iota(jnp.int32, (CHUNK, 1), 0) == CHUNK - 1
        g = _GlaTerms(slice(0, GLA_KEY_WIDTH), proj_ref[:, :GLA_KEY_WIDTH], proj_ref[:, GLA_KEY_WIDTH:v0],
                      proj_ref[:, v0:g0], low_b, gkw_ref, gkb_ref, _chunk_masks(tm),
                      saved=(proj_ref[:, GLA_SAVED_Z:GLA_SAVED_C], proj_ref[:, GLA_SAVED_C:GLA_SAVED_WIDTH]))
        dc_h = []
        for h in range(GLA_HEADS):
            kc, vc = _kcols(h), _vcols(h)
            k_cols = slice(GLA_KEY_WIDTH + kc.start, GLA_KEY_WIDTH + kc.stop)
            v_cols = slice(v0 + vc.start, v0 + vc.stop)
            do_h = do_parts[h]
            srows = slice(h * GLA_HEAD_V, (h + 1) * GLA_HEAD_V)
            scores = scores_ref[:, h * ROW_TILE:(h + 1) * ROW_TILE]
            dscores = _dot_nt(do_h, g.v_b[:, vc])
            dfwd = jnp.where(g.lower, dscores, 0.0).astype(BF16)
            dbwd = jnp.where(g.upper, dscores, 0.0).astype(BF16)
            dv_intra = _dot_tn(scores, do_h)
            da_intra = _dot_nn(dfwd, g.b_b[:, kc])
            db = _dot_tn(dfwd, g.a_b[:, kc])
            dcn = _dot_nn(dbwd, g.dp_b[:, kc])
            ddp = _dot_tn(dbwd, g.cn_b[:, kc])
            dstate = dstate_ref[srows, :]
            da_rows, dkd_rows, dv_rows, dcl_rows = [None] * cpt, [None] * cpt, [None] * cpt, [None] * cpt
            for j in reversed(range(cpt)):
                r = _chunk_rows(j)
                state = st_ref[j, srows, :]
                dstate_b = dstate.astype(BF16)
                do_c = do_h[r]
                dv_rows[j] = dv_intra[r] + _dot_nt(g.kd_b[r, kc], dstate_b)
                da_rows[j] = da_intra[r] + _dot_nn(do_c, state.astype(BF16))
                dkd = _dot_nn(g.v_b[r, vc], dstate_b) * g.e_rest[r, kc]
                dkd_rows[j] = dkd
                decay = jnp.exp(g.c_last[j][:, kc])
                dc_last = (jnp.sum(dkd * g.k[r, kc], axis=0, keepdims=True)
                           + decay * jnp.sum(state * dstate, axis=0, keepdims=True))
                dcl_rows[j] = jnp.where(last_row, dc_last, 0.0)
                dstate = _dot_tn(do_c, g.a_b[r, kc]) + dstate * decay
            dstate_ref[srows, :] = dstate
            da = jnp.concatenate(da_rows, axis=0)
            dkd = jnp.concatenate(dkd_rows, axis=0)
            dproj_ref[:, v_cols] = jnp.concatenate(dv_rows, axis=0).astype(BF16)
            q_up, q_down = da * g.e_pos[:, kc], dcn * g.e_neg[:, kc]
            k_up, k_down = ddp * g.e_pos[:, kc], db * g.e_neg[:, kc] + dkd
            dproj_ref[:, kc] = (Q_SCALE * (q_up + q_down)).astype(BF16)
            dproj_ref[:, k_cols] = (k_up + k_down).astype(BF16)
            dc_h.append(g.q[:, kc] * (q_up - q_down) + g.k[:, kc] * (k_up - k_down)
                        + jnp.concatenate(dcl_rows, axis=0))
        dz = _chunk_scan(jnp.concatenate(dc_h, axis=1), True) * (1.0 / GATE_NORMALIZER) * (1.0 - _sigmoid(g.z))
        dzb = dz.astype(BF16)
        dgkb_ref[...] += jnp.sum(dz, axis=0, keepdims=True)
        dgkw_ref[...] += _dot_tn(low_b, dzb)
        dproj_ref[:, GLA_QKVG_WIDTH:] = _dot_nt(dzb, gkw_ref[...]).astype(BF16)

        @pl.when(i == nt - 1)
        def _():
            dwout_stage[...] = dwout_acc[...].astype(BF16)
            pltpu.sync_copy(dwout_stage, dwout_hbm)

    rev = lambda i: (nt - 1 - i, 0)
    return pl.pallas_call(
        body, name="gla_bwd", grid=(nt,),
        in_specs=[pl.BlockSpec((tm, D_MODEL), rev), pl.BlockSpec((tm, GLA_SAVED_WIDTH), rev),
                  pl.BlockSpec((tm, GLA_VALUE_WIDTH), rev),
                  pl.BlockSpec((cpt, GLA_VALUE_WIDTH, GLA_HEAD_K), lambda i: (nt - 1 - i, 0, 0)),
                  pl.BlockSpec((tm, GLA_HEADS * ROW_TILE), rev),
                  _const((GLA_LOW_PAD, GLA_KEY_WIDTH)), _const((1, GLA_KEY_WIDTH)), _const((1, GLA_VALUE_WIDTH)),
                  _const((GLA_VALUE_WIDTH, D_MODEL))],
        out_specs=[pl.BlockSpec((tm, GLA_IN_PAD), rev), pl.BlockSpec(memory_space=pl.ANY),
                   _full((1, GLA_VALUE_WIDTH)), _full((GLA_LOW_PAD, GLA_KEY_WIDTH)), _full((1, GLA_KEY_WIDTH))],
        out_shape=[jax.ShapeDtypeStruct((seq, GLA_IN_PAD), BF16), jax.ShapeDtypeStruct((GLA_VALUE_WIDTH, D_MODEL), BF16),
                   jax.ShapeDtypeStruct((1, GLA_VALUE_WIDTH), F32), jax.ShapeDtypeStruct((GLA_LOW_PAD, GLA_KEY_WIDTH), F32),
                   jax.ShapeDtypeStruct((1, GLA_KEY_WIDTH), F32)],
        scratch_shapes=[pltpu.VMEM((GLA_VALUE_WIDTH, GLA_HEAD_K), F32), pltpu.VMEM((GLA_VALUE_WIDTH, D_MODEL), F32),
                        pltpu.VMEM((GLA_VALUE_WIDTH, D_MODEL), BF16)],
        compiler_params=_params(),
    )(dh2, proj, o, states, scores, gkw, gkb, hw, w_out)


def _position():
    return lax.axis_index("x"), lax.axis_index("y"), lax.axis_index("c")


def _lead_slot(ref, d):
    return ref.at[d]


def _row_slot(rows):
    return lambda ref, d: ref.at[pl.ds(pl.multiple_of(d * rows, rows), rows)]


def _dim1_slot(size):
    return lambda ref, d: ref.at[:, pl.ds(pl.multiple_of(d * size, size), size)]


class _Gather:
    def __init__(self, in_refs, out_refs, slots, send_sems, recv_sems, local_sems):
        self.in_refs, self.out_refs, self.slots = in_refs, out_refs, slots
        self.send_sems, self.recv_sems, self.local_sems = send_sems, recv_sems, local_sems
        self.n = len(in_refs)
        x, y, c = _position()
        self.c = c
        self.me, self.sibling = (x, y, c), (x, y, 1 - c)
        self.near = [(1 - x, y), (x, 1 - y)]
        self.diagonal = (1 - x, 1 - y)
        self.relay_from = (x ^ c, y ^ (1 - c))
        self.relay_to = (x ^ (1 - c), y ^ c)

    def _copy(self, a, k, block, to, from_input=False):
        part = self.slots[a](self.out_refs[a], 4 * block[0] + 2 * block[1] + block[2])
        return pltpu.make_async_remote_copy(
            src_ref=self.in_refs[a] if from_input else part, dst_ref=part,
            send_sem=self.send_sems.at[a, k], recv_sem=self.recv_sems.at[a, k], device_id=to, device_id_type=MESH)

    def _mine(self):
        return [pltpu.make_async_copy(self.in_refs[a], self.slots[a](self.out_refs[a], 4 * self.me[0] + 2 * self.me[1]
                                                                    + self.me[2]), self.local_sems.at[a])
                for a in range(self.n)]

    def _first(self):
        first = [self._copy(a, 0, self.me, self.sibling, True) for a in range(self.n)]
        return first + [self._copy(a, 1 + j, self.me, (*chip, self.c), True)
                        for j, chip in enumerate(self.near) for a in range(self.n)]

    def _relayed(self):
        return [self._copy(a, 3, (*self.relay_from, self.c), (*self.relay_to, self.c)) for a in range(self.n)]

    def _passed(self, j):
        chip = self.near[j] if j < 2 else self.diagonal
        return [self._copy(a, 4 + j, (*chip, self.c), self.sibling) for a in range(self.n)]

    def start(self):
        for cp in self._mine() + self._first():
            cp.start()

    def forward(self):
        for j, chip in enumerate(self.near):
            for a in range(self.n):
                self._copy(a, 1 + j, (*chip, self.c), self.me).wait_recv()
        for cp in self._relayed() + self._passed(0) + self._passed(1):
            cp.start()

    def relay(self):
        pass

    def finish(self):
        for a in range(self.n):
            self._copy(a, 3, (*self.diagonal, self.c), self.me).wait_recv()
        for cp in self._passed(2):
            cp.start()
        for a in range(self.n):
            self._copy(a, 0, self.sibling, self.me).wait_recv()
        for j, chip in enumerate(self.near + [self.diagonal]):
            for a in range(self.n):
                self._copy(a, 4 + j, (*chip, 1 - self.c), self.me).wait_recv()
        for cp in self._first() + self._relayed() + self._passed(0) + self._passed(1) + self._passed(2):
            cp.wait_send()
        for cp in self._mine():
            cp.wait()


class _Exchange:
    def __init__(self, in_refs, out_refs, slots, send_sems, recv_sems, local_sems):
        self.in_refs, self.out_refs, self.slots = in_refs, out_refs, slots
        self.send_sems, self.recv_sems, self.local_sems = send_sems, recv_sems, local_sems
        self.n = len(in_refs)
        self.pos = _position()

    def _copies(self):
        x, y, c = self.pos
        me = 4 * x + 2 * y + c
        mine = [pltpu.make_async_copy(self.slots[a](self.in_refs[a], me), self.out_refs[a].at[me],
                                      self.local_sems.at[a]) for a in range(self.n)]
        remote = []
        for k in range(1, N_DEV):
            px, py, pc = x ^ (k >> 2), y ^ ((k >> 1) & 1), c ^ (k & 1)
            for a in range(self.n):
                remote.append(pltpu.make_async_remote_copy(
                    src_ref=self.slots[a](self.in_refs[a], 4 * px + 2 * py + pc), dst_ref=self.out_refs[a].at[me],
                    send_sem=self.send_sems.at[a, k - 1], recv_sem=self.recv_sems.at[a, k - 1],
                    device_id=(px, py, pc), device_id_type=MESH))
        return mine, remote

    def start(self):
        mine, remote = self._copies()
        for cp in mine + remote:
            cp.start()

    def forward(self):
        pass

    def relay(self):
        pass

    def finish(self):
        mine, remote = self._copies()
        for cp in remote:
            cp.wait_recv()
        for cp in remote:
            cp.wait_send()
        for cp in mine:
            cp.wait()


class _Rider:
    def __init__(self, kind, arrays, out_shapes, slots, scratch=None, forward_step=None):
        self.kind, self.arrays, self.slots = kind, list(arrays), slots
        self.n = len(self.arrays)
        hbm = pl.BlockSpec(memory_space=pl.ANY)
        self.in_specs = [hbm] * self.n
        self.out_specs = [hbm] * self.n
        self.out_shape = [jax.ShapeDtypeStruct(tuple(s), a.dtype) for s, a in zip(out_shapes, self.arrays)]
        self.scratch = scratch if scratch is not None else [
            pltpu.SemaphoreType.DMA((self.n, 7)), pltpu.SemaphoreType.DMA((self.n, 7)),
            pltpu.SemaphoreType.DMA((self.n,))]
        self.forward_step = forward_step
        self.relay_step = None

    def bind(self, in_refs, out_refs, scratch):
        return self.kind(in_refs, out_refs, self.slots, *scratch)


def _gather_rider(shards, full_shapes, slots, forward_step=None):
    return _Rider(_Gather, shards, full_shapes, slots, None, forward_step)


def _exchange_rider(sends, part_shapes, slots):
    return _Rider(_Exchange, sends, [(N_DEV,) + tuple(s) for s in part_shapes], slots)


def _split_refs(refs, n_in, n_out, n_scratch, rider):
    k = rider.n if rider is not None else 0
    ins, r_ins = refs[:n_in], refs[n_in:n_in + k]
    outs, r_outs = refs[n_in + k:n_in + k + n_out], refs[n_in + k + n_out:n_in + 2 * k + n_out]
    rest = refs[n_in + 2 * k + n_out:]
    scratch, r_scratch = rest[:n_scratch], rest[n_scratch:]
    comm = rider.bind(r_ins, r_outs, r_scratch) if rider is not None else None
    if comm is not None:
        comm.forward_step, comm.relay_step = rider.forward_step, rider.relay_step
    return ins + outs + scratch, comm


def _ride_before(comm, i, nt):
    if comm is not None:
        pl.when(i == 0)(comm.start)
        pl.when(i == (nt - 1 if comm.forward_step is None else min(comm.forward_step, nt - 1)))(comm.forward)
        pl.when(i == (nt - 1 if comm.relay_step is None else min(comm.relay_step, nt - 1)))(comm.relay)


def _ride_after(comm, i, nt):
    if comm is not None:
        pl.when(i == nt - 1)(comm.finish)


def _extend(specs, rider, field):
    return list(specs) + (getattr(rider, field) if rider is not None else [])


def _comm_call(name, rider):
    def body(*refs):
        _, comm = _split_refs(refs, 0, 0, 0, rider)
        comm.start()
        comm.forward()
        comm.relay()
        comm.finish()

    return pl.pallas_call(body, name=name, in_specs=rider.in_specs, out_specs=rider.out_specs,
                          out_shape=rider.out_shape, scratch_shapes=rider.scratch,
                          compiler_params=pltpu.CompilerParams(vmem_limit_bytes=VMEM_LIMIT))(*rider.arrays)


N_CHIPS = 4


class _TwoLevel:
    def __init__(self, in_refs, out_refs, slots, *scratch):
        self.in_refs, self.out_refs, self.slots = in_refs, out_refs, slots
        self.n = n = len(in_refs)
        self.own_bufs, self.recv_bufs, self.relay_bufs = scratch[:n], scratch[n:2 * n], scratch[2 * n:3 * n]
        self.swap_send, self.swap_recv, self.local_sems, self.chip_send, self.chip_recv = scratch[3 * n:]
        x, y, c = self.pos = _position()
        self.first = (x ^ (1 - c), y ^ c)
        self.second = (x ^ c, y ^ (1 - c))
        self.chip_index = lambda chip: 2 * chip[0] + chip[1]

    def _swap(self):
        x, y, c = self.pos
        return [pltpu.make_async_remote_copy(
            src_ref=self.slots[a](self.in_refs[a], 2 * q + 1 - c), dst_ref=self.recv_bufs[a].at[q],
            send_sem=self.swap_send.at[a, q], recv_sem=self.swap_recv.at[a, q],
            device_id=(x, y, 1 - c), device_id_type=MESH) for a in range(self.n) for q in range(N_CHIPS)]

    def _mine(self):
        c = self.pos[2]
        return [pltpu.make_async_copy(self.slots[a](self.in_refs[a], 2 * q + c), self.own_bufs[a].at[q],
                                      self.local_sems.at[a, q]) for a in range(self.n) for q in range(N_CHIPS)]

    def _to_chip(self, a, k, src, dst, chip):
        return pltpu.make_async_remote_copy(
            src_ref=src, dst_ref=dst, send_sem=self.chip_send.at[a, k], recv_sem=self.chip_recv.at[a, k],
            device_id=(*chip, self.pos[2]), device_id_type=MESH)

    def _first_wave(self):
        x, y, _ = self.pos
        diagonal = self.chip_index((1 - x, 1 - y))
        passed_on = [self._to_chip(a, 1, self.own_bufs[a].at[diagonal], self.relay_bufs[a], self.first)
                     for a in range(self.n)]
        return passed_on + [self._to_chip(a, 0, self.own_bufs[a].at[self.chip_index(self.first)],
                                          self.out_refs[a].at[1], self.first) for a in range(self.n)]

    def _second_wave(self):
        return [self._to_chip(a, 2, self.own_bufs[a].at[self.chip_index(self.second)], self.out_refs[a].at[2],
                              self.second) for a in range(self.n)]

    def _own(self):
        x, y, _ = self.pos
        return [pltpu.make_async_copy(self.own_bufs[a].at[2 * x + y], self.out_refs[a].at[0],
                                      self.local_sems.at[a, N_CHIPS]) for a in range(self.n)]

    def start(self):
        for cp in self._swap() + self._mine():
            cp.start()

    def forward(self):
        swap, mine = self._swap(), self._mine()
        for a in range(self.n):
            for q in range(N_CHIPS):
                mine[a * N_CHIPS + q].wait()
                swap[a * N_CHIPS + q].wait_recv()
                self.own_bufs[a][q] = (self.own_bufs[a][q].astype(F32)
                                       + self.recv_bufs[a][q].astype(F32)).astype(BF16)
        for cp in self._first_wave() + self._own():
            cp.start()

    def relay(self):
        second = self.chip_index(self.second)
        for a in range(self.n):
            self._to_chip(a, 1, self.relay_bufs[a], self.relay_bufs[a], self.first).wait_recv()
            self.own_bufs[a][second] = (self.own_bufs[a][second].astype(F32)
                                        + self.relay_bufs[a][...].astype(F32)).astype(BF16)
        for cp in self._second_wave():
            cp.start()

    def finish(self):
        for a in range(self.n):
            self._to_chip(a, 0, self.out_refs[a].at[1], self.out_refs[a].at[1], self.first).wait_recv()
            self._to_chip(a, 2, self.out_refs[a].at[2], self.out_refs[a].at[2], self.second).wait_recv()
        for cp in self._first_wave() + self._second_wave() + self._swap():
            cp.wait_send()
        for cp in self._own():
            cp.wait()


def _two_level_rider(sends, part_shapes, slots, forward_step=None, relay_step=None):
    n = len(sends)
    bufs = [pltpu.VMEM((N_CHIPS,) + tuple(s), a.dtype) for s, a in zip(part_shapes, sends)]
    relay_bufs = [pltpu.VMEM(tuple(s), a.dtype) for s, a in zip(part_shapes, sends)]
    scratch = bufs + bufs + relay_bufs + [
        pltpu.SemaphoreType.DMA((n, N_CHIPS)), pltpu.SemaphoreType.DMA((n, N_CHIPS)),
        pltpu.SemaphoreType.DMA((n, N_CHIPS + 1)), pltpu.SemaphoreType.DMA((n, 3)), pltpu.SemaphoreType.DMA((n, 3))]
    rider = _Rider(_TwoLevel, sends, [(3,) + tuple(s) for s in part_shapes], slots, scratch, forward_step)
    rider.relay_step = relay_step
    return rider


class _Joined:
    def __init__(self, first, second):
        self.first, self.second = first, second

    def start(self):
        self.first.start()
        self.second.start()

    def forward(self):
        self.first.forward()
        self.second.forward()

    def relay(self):
        self.first.relay()
        self.second.relay()

    def finish(self):
        self.first.finish()
        self.second.finish()


class _JoinedRider:
    def __init__(self, first, second):
        self.first, self.second = first, second
        self.n = first.n + second.n
        self.arrays = first.arrays + second.arrays
        self.in_specs = first.in_specs + second.in_specs
        self.out_specs = first.out_specs + second.out_specs
        self.out_shape = first.out_shape + second.out_shape
        self.scratch = first.scratch + second.scratch
        self.forward_step = first.forward_step
        self.relay_step = first.relay_step

    def bind(self, in_refs, out_refs, scratch):
        k, s = self.first.n, len(self.first.scratch)
        return _Joined(self.first.bind(in_refs[:k], out_refs[:k], scratch[:s]),
                       self.second.bind(in_refs[k:], out_refs[k:], scratch[s:]))


def _adamw(w, g, m, v):
    m = ADAM_B1 * m + (1.0 - ADAM_B1) * g
    v = ADAM_B2 * v + (1.0 - ADAM_B2) * (g * g)
    m_hat = m / (1.0 - ADAM_B1 ** ADAM_STEP)
    v_hat = v / (1.0 - ADAM_B2 ** ADAM_STEP)
    delta = -ADAM_LR * (m_hat / (jnp.sqrt(v_hat) + ADAM_EPS) + ADAM_WD * w)
    return delta, m, v


def _sum_parts(parts_ref, index=()):
    g = parts_ref[(0,) + index].astype(F32)
    for s in range(1, parts_ref.shape[0]):
        g = g + parts_ref[(s,) + index].astype(F32)
    return g


def _adamw_group_call(name, groups):
    k = len(groups)

    def body(*refs):
        ins, outs = refs[:4 * k], refs[4 * k:]
        for i in range(k):
            parts_ref, w_ref, m_ref, v_ref = ins[4 * i:4 * i + 4]
            g = _sum_parts(parts_ref)
            delta, m_new, v_new = _adamw(w_ref[...], g, m_ref[...], v_ref[...])
            for out_ref, value in zip(outs[4 * i:4 * i + 4], (g, delta, m_new, v_new)):
                out_ref[...] = value

    vmem = pl.BlockSpec(memory_space=pltpu.VMEM)
    res = pl.pallas_call(
        body, name=name, in_specs=[vmem] * (4 * k), out_specs=[vmem] * (4 * k),
        out_shape=[jax.ShapeDtypeStruct(grp[1].shape, F32) for grp in groups for _ in range(4)],
        compiler_params=pltpu.CompilerParams(vmem_limit_bytes=VMEM_LIMIT),
    )(*[a for grp in groups for a in grp])
    return [res[4 * i:4 * i + 4] for i in range(k)]


def _adamw_slabs_call(name, parts, w, m, v, rider=None):
    def main(parts_ref, w_ref, m_ref, v_ref, g_ref, delta_ref, m_out, v_out):
        g = _sum_parts(parts_ref)
        delta, m_new, v_new = _adamw(w_ref[...], g, m_ref[...], v_ref[...])
        g_ref[...] = g
        delta_ref[...] = delta
        m_out[...] = m_new
        v_out[...] = v_new

    def body(*refs):
        own, comm = _split_refs(refs, 4, 4, 0, rider)
        if comm is not None:
            comm.start()
        main(*own)
        if comm is not None:
            comm.forward()
            comm.relay()
            comm.finish()

    vmem = pl.BlockSpec(memory_space=pltpu.VMEM)
    return pl.pallas_call(
        body, name=name, in_specs=_extend([vmem] * 4, rider, "in_specs"),
        out_specs=_extend([vmem] * 4, rider, "out_specs"),
        out_shape=_extend([jax.ShapeDtypeStruct(w.shape, F32)] * 4, rider, "out_shape"),
        scratch_shapes=_extend([], rider, "scratch"),
        compiler_params=pltpu.CompilerParams(vmem_limit_bytes=VMEM_LIMIT),
    )(parts, w, m, v, *_extend([], rider, "arrays"))


WIDE_ROWS = 8
NARROW_ROWS = 40
NARROW_GKW_ROW = 8
NARROW_GKB_ROW = 24
NARROW_HW_ROW = 32
GROUP_SHARD = POOL_GROUP_DIM // N_DEV
KEY_SHARD = GLA_KEY_WIDTH // N_DEV
HEAD_V_SHARD = GLA_HEAD_V // N_DEV


def _small_adamw_call(wide, narrow, w, m, v):
    names = ("norm_w", "pool_scale", "final_norm_w", "pool_group_b", "gla_gk_w", "gla_gk_b", "gla_head_norm_w")
    where = {
        "norm_w": (0, slice(0, 2), slice(None)),
        "pool_scale": (0, slice(2, 3), slice(None)),
        "final_norm_w": (0, slice(3, 4), slice(None)),
        "pool_group_b": (1, slice(0, POOL_GROUPS), slice(0, GROUP_SHARD)),
        "gla_gk_w": (1, slice(NARROW_GKW_ROW, NARROW_GKW_ROW + GLA_GATE_RANK), slice(0, KEY_SHARD)),
        "gla_gk_b": (1, slice(NARROW_GKB_ROW, NARROW_GKB_ROW + 1), slice(0, KEY_SHARD)),
        "gla_head_norm_w": (1, slice(NARROW_HW_ROW, NARROW_HW_ROW + 1), slice(0, HEAD_V_SHARD)),
    }
    k = len(names)

    def body(*refs):
        parts = refs[0:2]
        w_refs, m_refs, v_refs = refs[2:2 + k], refs[2 + k:2 + 2 * k], refs[2 + 2 * k:2 + 3 * k]
        outs = refs[2 + 3 * k:]
        loss_ref = outs[0]
        loss_ref[...] = _sum_parts(parts[0], (slice(4, 5), slice(0, 1)))
        for i, name in enumerate(names):
            buf, rows, cols = where[name]
            g = _sum_parts(parts[buf], (rows, cols))
            delta, m_new, v_new = _adamw(w_refs[i][...], g, m_refs[i][...], v_refs[i][...])
            outs[1 + i][...] = g
            outs[1 + k + i][...] = delta
            outs[1 + 2 * k + i][...] = m_new
            outs[1 + 3 * k + i][...] = v_new

    vmem = pl.BlockSpec(memory_space=pltpu.VMEM)
    shapes = [jax.ShapeDtypeStruct(w[n].shape, F32) for n in names]
    res = pl.pallas_call(
        body, name="adamw_small", in_specs=[vmem] * (2 + 3 * k), out_specs=[vmem] * (1 + 4 * k),
        out_shape=[jax.ShapeDtypeStruct((1, 1), F32)] + shapes * 4,
    )(wide, narrow, *[w[n] for n in names], *[m[n] for n in names], *[v[n] for n in names])
    unzip = lambda j: dict(zip(names, res[1 + j * k:1 + (j + 1) * k]))
    return res[0], unzip(0), unzip(1), unzip(2), unzip(3)


def kernel(x, norm_w, pool_in_w, pool_group_w, pool_group_b, pool_scale, pool_out_w, gla_in_w, gla_gk_w, gla_gk_b, gla_head_norm_w, gla_out_w, final_norm_w, loss_target, m_norm_w, m_pool_in_w, m_pool_group_w, m_pool_group_b, m_pool_scale, m_pool_out_w, m_gla_in_w, m_gla_gk_w, m_gla_gk_b, m_gla_head_norm_w, m_gla_out_w, m_final_norm_w, v_norm_w, v_pool_in_w, v_pool_group_w, v_pool_group_b, v_pool_scale, v_pool_out_w, v_gla_in_w, v_gla_gk_w, v_gla_gk_b, v_gla_head_norm_w, v_gla_out_w, v_final_norm_w):
    w = dict(norm_w=norm_w, pool_in_w=pool_in_w, pool_group_w=pool_group_w, pool_group_b=pool_group_b,
             pool_scale=pool_scale, pool_out_w=pool_out_w, gla_in_w=gla_in_w, gla_gk_w=gla_gk_w, gla_gk_b=gla_gk_b,
             gla_head_norm_w=gla_head_norm_w, gla_out_w=gla_out_w, final_norm_w=final_norm_w)
    m = dict(norm_w=m_norm_w, pool_in_w=m_pool_in_w, pool_group_w=m_pool_group_w, pool_group_b=m_pool_group_b,
             pool_scale=m_pool_scale, pool_out_w=m_pool_out_w, gla_in_w=m_gla_in_w, gla_gk_w=m_gla_gk_w,
             gla_gk_b=m_gla_gk_b, gla_head_norm_w=m_gla_head_norm_w, gla_out_w=m_gla_out_w,
             final_norm_w=m_final_norm_w)
    v = dict(norm_w=v_norm_w, pool_in_w=v_pool_in_w, pool_group_w=v_pool_group_w, pool_group_b=v_pool_group_b,
             pool_scale=v_pool_scale, pool_out_w=v_pool_out_w, gla_in_w=v_gla_in_w, gla_gk_w=v_gla_gk_w,
             gla_gk_b=v_gla_gk_b, gla_head_norm_w=v_gla_head_norm_w, gla_out_w=v_gla_out_w,
             final_norm_w=v_final_norm_w)
    col_shard = GLA_IN_WIDTH // N_DEV
    row_shard = D_MODEL // N_DEV

    def lanes(a):
        return jnp.pad(a, [(0, 0)] * (a.ndim - 1) + [(0, LANES - a.shape[-1])])

    small_in = jnp.concatenate([lanes(pool_group_b[0]), lanes(gla_gk_b), lanes(gla_head_norm_w),
                                jnp.zeros((2, LANES), F32)], axis=0)
    in_cols = 2 * POOL_WIDTH // N_DEV
    pool_in, pool_gw, pool_out, small_all = _comm_call("pool_weights_all_gather", _gather_rider(
        [pool_in_w[0].astype(BF16), pool_group_w[0].astype(BF16), pool_out_w[0].astype(BF16), small_in],
        [(D_MODEL, 2 * POOL_WIDTH), (POOL_GROUPS, POOL_GROUP_DIM, POOL_GROUP_DIM), (POOL_WIDTH, D_MODEL),
         (N_DEV, 8, LANES)],
        [_dim1_slot(in_cols), _dim1_slot(GROUP_SHARD), _row_slot(row_shard), _lead_slot]))
    pool_gb = jnp.transpose(small_all[:, 0:POOL_GROUPS, :GROUP_SHARD], (1, 0, 2)).reshape(1, POOL_WIDTH)
    gla_gkb = small_all[:, POOL_GROUPS, :KEY_SHARD].reshape(1, GLA_KEY_WIDTH)
    gla_hw = jnp.tile(small_all[:, POOL_GROUPS + 1, :HEAD_V_SHARD].reshape(1, GLA_HEAD_V), (1, GLA_HEADS))
    nw0, nw1, wf = norm_w[0:1], norm_w[1:2], final_norm_w.reshape(1, D_MODEL)
    xs, target = x[0], loss_target[0]

    h1, pool_y, pool_silu, pool_dsilu, pooled, mixed, gla_in_parts, gkw_parts, gla_out = _pool_fwd_call(
        xs, nw0, pool_in, pool_gw, pool_gb, pool_scale, pool_out, _gather_rider(
            [jnp.transpose(gla_in_w[0]).astype(BF16), gla_gk_w[0].astype(BF16), gla_out_w[0].astype(BF16)],
            [(N_DEV, col_shard, D_MODEL), (N_DEV, GLA_GATE_RANK, KEY_SHARD), (GLA_VALUE_WIDTH, D_MODEL)],
            [_lead_slot, _lead_slot, _row_slot(row_shard)], GATHER_RELAY_STEP))
    gla_in = gla_in_parts.reshape(GLA_IN_WIDTH, D_MODEL)
    gla_gkw = jnp.pad(jnp.transpose(gkw_parts, (1, 0, 2)).reshape(GLA_GATE_RANK, GLA_KEY_WIDTH),
                      ((0, GLA_LOW_PAD - GLA_GATE_RANK), (0, 0)))
    dh2, proj, o, states, scores, loss_part, dwf = _gla_fwd_call(h1, nw1, gla_in, gla_gkw, gla_gkb, gla_hw, gla_out,
                                                                 wf, target)

    dproj, d_gla_out, dhw, dgkw, dgkb = _gla_bwd_call(dh2, proj, o, states, scores, gla_gkw, gla_gkb, gla_hw,
                                                      gla_out)
    dh1, d_gla_in, dnw1, landed_gla_out = _inproj_bwd_call(
        "gla_in_bwd", dproj, h1, nw1, gla_in, dh2,
        _exchange_rider([d_gla_out], [(row_shard, D_MODEL)], [_row_slot(row_shard)]), transposed=True)
    slabs = col_shard * D_MODEL // (BF16_ROWS * LANES)
    gla_in_send = d_gla_in.reshape(N_DEV, slabs, BF16_ROWS, LANES)
    dp, d_pool_out, dgw, dgb, dsc, landed_gla_in = _pool_bwd_call(
        dh1, pool_y, pool_silu, pool_dsilu, pooled, mixed, pool_gw, pool_scale, pool_out,
        _two_level_rider([gla_in_send], [(slabs, BF16_ROWS, LANES)], [_lead_slot], TWO_LEVEL_ADD_STEP,
                         TWO_LEVEL_RELAY_STEP))
    grad_x, d_pool_in, dnw0 = _inproj_bwd_call("pool_in_bwd", dp, xs, nw0, pool_in, dh1)

    wide = jnp.concatenate([
        dnw0, dnw1, dsc, dwf, jnp.pad(loss_part[0:1, 0:1], ((0, 0), (0, D_MODEL - 1))),
        jnp.zeros((WIDE_ROWS - 5, D_MODEL), F32)], axis=0)

    def rows8(a):
        return jnp.pad(lanes(a), ((0, 0), (0, -a.shape[1] % 8), (0, 0)))

    narrow = jnp.concatenate([
        rows8(jnp.transpose(dgb.reshape(POOL_GROUPS, N_DEV, GROUP_SHARD), (1, 0, 2))),
        rows8(jnp.transpose(dgkw[:GLA_GATE_RANK].reshape(GLA_GATE_RANK, N_DEV, KEY_SHARD), (1, 0, 2))),
        rows8(dgkb.reshape(N_DEV, 1, KEY_SHARD)),
        rows8(dhw.reshape(GLA_HEADS, GLA_HEAD_V).sum(axis=0).reshape(N_DEV, 1, HEAD_V_SHARD)),
    ], axis=1)
    last_exchange = _JoinedRider(
        _two_level_rider([d_pool_in, d_pool_out, dgw],
                         [(D_MODEL, in_cols), (row_shard, D_MODEL), (POOL_GROUPS, GROUP_SHARD, POOL_GROUP_DIM)],
                         [_dim1_slot(in_cols), _row_slot(row_shard), _dim1_slot(GROUP_SHARD)]),
        _exchange_rider([wide, narrow], [(WIDE_ROWS, D_MODEL), (NARROW_ROWS, LANES)],
                        [lambda ref, d: ref, _lead_slot]))

    res = {}
    as_slabs = lambda t: jnp.transpose(t[0]).reshape(slabs, BF16_ROWS, LANES)
    *outs, landed_pool_in, landed_pool_out, landed_gw, landed_wide, landed_narrow = _adamw_slabs_call(
        "adamw_gla_in_w", landed_gla_in, as_slabs(gla_in_w), as_slabs(m_gla_in_w), as_slabs(v_gla_in_w),
        last_exchange)
    res["gla_in_w"] = [jnp.transpose(t.reshape(col_shard, D_MODEL))[None] for t in outs]
    rest = [("pool_in_w", landed_pool_in, (D_MODEL, in_cols)),
            ("pool_group_w", landed_gw, (POOL_GROUPS * GROUP_SHARD, POOL_GROUP_DIM)),
            ("pool_out_w", landed_pool_out, (row_shard, D_MODEL)), ("gla_out_w", landed_gla_out, (row_shard, D_MODEL))]
    updates = _adamw_group_call("adamw_matrices", [
        (parts.reshape((parts.shape[0],) + shape), w[name].reshape(shape), m[name].reshape(shape),
         v[name].reshape(shape)) for name, parts, shape in rest])
    for (name, _, _), outs in zip(rest, updates):
        res[name] = [t.reshape(w[name].shape) for t in outs]
    small_shapes ={"norm_w": (2, D_MODEL), "pool_scale": (1, D_MODEL), "final_norm_w": (1, D_MODEL),
                    "pool_group_b": (POOL_GROUPS, GROUP_SHARD), "gla_gk_w": (GLA_GATE_RANK, KEY_SHARD),
                    "gla_gk_b": (1, KEY_SHARD), "gla_head_norm_w": (1, HEAD_V_SHARD)}
    as_small = lambda t: {n: t[n].reshape(s) for n, s in small_shapes.items()}
    loss, *small_outs = _small_adamw_call(landed_wide, landed_narrow, as_small(w), as_small(m), as_small(v))
    for name in small_shapes:
        res[name] = [t[name].reshape(w[name].shape) for t in small_outs]
    order = ("norm_w", "pool_in_w", "pool_group_w", "pool_group_b", "pool_scale", "pool_out_w", "gla_in_w",
             "gla_gk_w", "gla_gk_b", "gla_head_norm_w", "gla_out_w", "final_norm_w")
    return (loss.reshape(()), grad_x[None], *[res[n][0] for n in order], *[res[n][1] for n in order],
            *[res[n][2] for n in order], *[res[n][3] for n in order])
```

```python
import jax
import jax.numpy as jnp
from jax import lax
from jax.experimental import pallas as pl
from jax.experimental.pallas import tpu as pltpu

F32 = jnp.float32
BF16 = jnp.bfloat16
MESH = pl.DeviceIdType.MESH

N_DEV = 8
D_MODEL = 1024
POOL_WIDTH = 1024
POOL_GROUPS = 4
POOL_GROUP_DIM = 256
POOL_HALO = 16
GLA_HEADS = 4
GLA_HEAD_K = 128
GLA_HEAD_V = 256
GLA_KEY_WIDTH = 512
GLA_VALUE_WIDTH = 1024
GLA_GATE_RANK = 16
GLA_IN_WIDTH = 3088
GLA_IN_PAD = 3200
GLA_SAVED_Z = GLA_IN_PAD
GLA_SAVED_C = GLA_SAVED_Z + 512
GLA_SAVED_WIDTH = GLA_SAVED_C + 512
GLA_LOW_PAD = 128
GLA_QKVG_WIDTH = 3072
CHUNK = 64
GATE_NORMALIZER = 16.0
RMS_EPS = 1e-6
Q_SCALE = GLA_HEAD_K ** -0.5

ADAM_LR = 0.001
ADAM_B1 = 0.9
ADAM_B2 = 0.999
ADAM_EPS = 1e-08
ADAM_WD = 0.01
ADAM_STEP = 10

LANES = 128
BF16_ROWS = 16
VMEM_LIMIT = 60 * 1024 * 1024
ROW_TILE = 256
GLA_FWD_ROW_TILE = 512
MATMUL_ROW_TILE = 512
ROW_MAJOR_PIECE = 776
WEIGHT_ROWS_PIECE = 208
GATHER_RELAY_STEP = 5
TWO_LEVEL_ADD_STEP = 1
TWO_LEVEL_RELAY_STEP = 4


def _dot_nn(a, b):
    return lax.dot_general(a, b, (((1,), (0,)), ((), ())), preferred_element_type=F32)


def _dot_nt(a, b):
    return lax.dot_general(a, b, (((1,), (1,)), ((), ())), preferred_element_type=F32)


def _dot_tn(a, b):
    return lax.dot_general(a, b, (((0,), (0,)), ((), ())), preferred_element_type=F32)


def _rms(x):
    rstd = lax.rsqrt(jnp.mean(x * x, axis=-1, keepdims=True) + RMS_EPS)
    return x * rstd, rstd


def _rms_bwd(dxhat, xhat, rstd):
    return rstd * (dxhat - xhat * jnp.mean(dxhat * xhat, axis=-1, keepdims=True))


def _sigmoid(x):
    return 1.0 / (1.0 + jnp.exp(-x))


def _params(sem=("arbitrary",)):
    return pltpu.CompilerParams(dimension_semantics=sem, vmem_limit_bytes=VMEM_LIMIT)


def _full(shape):
    return pl.BlockSpec(shape, lambda i: (0,) * len(shape))


def _const(shape):
    return pl.BlockSpec(shape, lambda i: (0,) * len(shape), pipeline_mode=pl.Buffered(1))


def _window_sums(ext, forward):
    n = ext.shape[0]
    outs = []
    for g in range(POOL_GROUPS):
        s = ext[:, g * POOL_GROUP_DIM:(g + 1) * POOL_GROUP_DIM]
        for k in range(g + 1):
            shift = (1 << k) if forward else n - (1 << k)
            s = s + pltpu.roll(s, shift, axis=0)
        outs.append(s[:n - POOL_HALO])
    return outs


def _inv_count(row0, tm):
    row = row0 + lax.broadcasted_iota(jnp.int32, (tm, 1), 0)
    return [1.0 / jnp.minimum(row + 1, 2 << g).astype(F32) for g in range(POOL_GROUPS)]


def _pool_mix(u, u_prev, row0, gw_ref, gb):
    tm = u.shape[0]
    sums = _window_sums(jnp.concatenate([u, u_prev], axis=0), True)
    inv = _inv_count(row0, tm)
    pooled, mixed = [], []
    for g in range(POOL_GROUPS):
        ug = u[:, g * POOL_GROUP_DIM:(g + 1) * POOL_GROUP_DIM]
        pg = (sums[g] * inv[g] - ug).astype(BF16)
        pooled.append(pg)
        mixed.append(_dot_nn(pg, gw_ref[g]))
    return pooled, jnp.concatenate(mixed, axis=1) + gb


def _pool_fwd_call(x, nw, w_in, gw, gb, sc, w_out, rider=None):
    seq = x.shape[0]
    tm = min(MATMUL_ROW_TILE, seq)
    nt = seq // tm

    def main(x_ref, nw_ref, win_ref, gw_ref, gb_ref, sc_ref, wout_ref, h_ref, y_ref, silu_ref, dsilu_ref,
             pooled_ref, mixed_ref, halo_ref):
        i = pl.program_id(0)

        @pl.when(i == 0)
        def _():
            halo_ref[...] = jnp.zeros_like(halo_ref)

        xt = x_ref[...]
        xhat, _ = _rms(xt)
        n = (xhat * nw_ref[...]).astype(BF16)
        p = _dot_nn(n, win_ref[...])
        u = p[:, :POOL_WIDTH]
        gate = p[:, POOL_WIDTH:]
        sg = _sigmoid(gate)
        silu = gate * sg
        silu_ref[...] = silu
        dsilu_ref[...] = sg * (1.0 + gate * (1.0 - sg))
        pooled, mixed = _pool_mix(u, halo_ref[...], i * tm, gw_ref, gb_ref[...])
        pooled_ref[...] = jnp.concatenate(pooled, axis=1)
        mixed_ref[...] = mixed
        halo_ref[...] = u[tm - POOL_HALO:, :]
        y = (mixed * sc_ref[...] * silu).astype(BF16)
        y_ref[...] = y
        h_ref[...] = xt + _dot_nn(y, wout_ref[...])

    def body(*refs):
        own, comm = _split_refs(refs, 7, 6, 1, rider)
        _ride_before(comm, pl.program_id(0), nt)
        main(*own)
        _ride_after(comm, pl.program_id(0), nt)

    return pl.pallas_call(
        body, name="pool_fwd", grid=(nt,),
        in_specs=_extend([pl.BlockSpec((tm, D_MODEL), lambda i: (i, 0)), _const((1, D_MODEL)),
                          _const((D_MODEL, 2 * POOL_WIDTH)), _const((POOL_GROUPS, POOL_GROUP_DIM, POOL_GROUP_DIM)),
                          _const((1, POOL_WIDTH)), _const((1, POOL_WIDTH)), _const((POOL_WIDTH, D_MODEL))],
                         rider, "in_specs"),
        out_specs=_extend([pl.BlockSpec((tm, D_MODEL), lambda i: (i, 0))] * 6, rider, "out_specs"),
        out_shape=_extend([jax.ShapeDtypeStruct((seq, D_MODEL), F32), jax.ShapeDtypeStruct((seq, POOL_WIDTH), BF16),
                           jax.ShapeDtypeStruct((seq, POOL_WIDTH), F32), jax.ShapeDtypeStruct((seq, POOL_WIDTH), F32),
                           jax.ShapeDtypeStruct((seq, POOL_WIDTH), BF16),
                           jax.ShapeDtypeStruct((seq, POOL_WIDTH), F32)], rider, "out_shape"),
        scratch_shapes=_extend([pltpu.VMEM((POOL_HALO, POOL_WIDTH), F32)], rider, "scratch"),
        compiler_params=_params(),
    )(x, nw, w_in, gw, gb, sc, w_out, *_extend([], rider, "arrays"))


def _pool_bwd_call(dh, y, silu, dsilu, pooled, mixed, gw, sc, w_out, rider=None):
    seq = dh.shape[0]
    tm = min(MATMUL_ROW_TILE, seq)
    nt = seq // tm

    def main(dh_ref, y_ref, silu_ref, dsilu_ref, pooled_ref, mixed_ref, gw_ref, sc_ref, wout_ref,
             dp_ref, dwout_hbm, dgw_hbm, dgb_ref, dsc_ref, carry_ref, dwout_acc, dgw_acc, dwout_stage, dgw_stage):
        i = pl.program_id(0)
        t = nt - 1 - i

        @pl.when(i == 0)
        def _():
            carry_ref[...] = jnp.zeros_like(carry_ref)
            dwout_acc[...] = jnp.zeros_like(dwout_acc)
            dgw_acc[...] = jnp.zeros_like(dgw_acc)
            dgb_ref[...] = jnp.zeros_like(dgb_ref)
            dsc_ref[...] = jnp.zeros_like(dsc_ref)

        silu = silu_ref[...]
        pooled = [pooled_ref[:, g * POOL_GROUP_DIM:(g + 1) * POOL_GROUP_DIM] for g in range(POOL_GROUPS)]
        sc = sc_ref[...]
        dhb = dh_ref[...].astype(BF16)
        dwout_acc[...] += _dot_tn(y_ref[...], dhb)
        dy = _dot_nt(dhb, wout_ref[...])
        dmixed = dy * sc * silu
        dy_mixed = dy * mixed_ref[...]
        dsc_ref[...] += jnp.sum(dy_mixed * silu, axis=0, keepdims=True)
        dgate = dy_mixed * sc * dsilu_ref[...]
        dgb_ref[...] += jnp.sum(dmixed, axis=0, keepdims=True)
        inv = _inv_count(t * tm, tm)
        dpooled, scaled = [], []
        for g in range(POOL_GROUPS):
            dmg = dmixed[:, g * POOL_GROUP_DIM:(g + 1) * POOL_GROUP_DIM].astype(BF16)
            dgw_acc[g] += _dot_tn(pooled[g], dmg)
            dpg = _dot_nt(dmg, gw_ref[g])
            dpooled.append(dpg)
            scaled.append(dpg * inv[g])
        r = jnp.concatenate(scaled, axis=1)
        sums = _window_sums(jnp.concatenate([r, carry_ref[...]], axis=0), False)
        carry_ref[...] = r[:POOL_HALO, :]
        du = jnp.concatenate([sums[g] - dpooled[g] for g in range(POOL_GROUPS)], axis=1)
        dp_ref[...] = jnp.concatenate([du, dgate], axis=1).astype(BF16)

        @pl.when(i == nt - 1)
        def _():
            dwout_stage[...] = dwout_acc[...].astype(BF16)
            dgw_stage[...] = dgw_acc[...].astype(BF16)
            pltpu.sync_copy(dwout_stage, dwout_hbm)
            pltpu.sync_copy(dgw_stage, dgw_hbm)

    def body(*refs):
        own, comm = _split_refs(refs, 9, 5, 5, rider)
        _ride_before(comm, pl.program_id(0), nt)
        main(*own)
        _ride_after(comm, pl.program_id(0), nt)

    rev = lambda i: (nt - 1 - i, 0)
    return pl.pallas_call(
        body, name="pool_bwd", grid=(nt,),
        in_specs=_extend([pl.BlockSpec((tm, D_MODEL), rev)] * 6
                         + [_const((POOL_GROUPS, POOL_GROUP_DIM, POOL_GROUP_DIM)), _const((1, POOL_WIDTH)),
                            _const((POOL_WIDTH, D_MODEL))], rider, "in_specs"),
        out_specs=_extend([pl.BlockSpec((tm, 2 * POOL_WIDTH), rev), pl.BlockSpec(memory_space=pl.ANY),
                           pl.BlockSpec(memory_space=pl.ANY), _full((1, POOL_WIDTH)), _full((1, POOL_WIDTH))],
                          rider, "out_specs"),
        out_shape=_extend([jax.ShapeDtypeStruct((seq, 2 * POOL_WIDTH), BF16),
                           jax.ShapeDtypeStruct((POOL_WIDTH, D_MODEL), BF16),
                           jax.ShapeDtypeStruct((POOL_GROUPS, POOL_GROUP_DIM, POOL_GROUP_DIM), BF16),
                           jax.ShapeDtypeStruct((1, POOL_WIDTH), F32), jax.ShapeDtypeStruct((1, POOL_WIDTH), F32)],
                          rider, "out_shape"),
        scratch_shapes=_extend([pltpu.VMEM((POOL_HALO, POOL_WIDTH), F32), pltpu.VMEM((POOL_WIDTH, D_MODEL), F32),
                                pltpu.VMEM((POOL_GROUPS, POOL_GROUP_DIM, POOL_GROUP_DIM), F32),
                                pltpu.VMEM((POOL_WIDTH, D_MODEL), BF16),
                                pltpu.VMEM((POOL_GROUPS, POOL_GROUP_DIM, POOL_GROUP_DIM), BF16)], rider, "scratch"),
        compiler_params=_params(),
    )(dh, y, silu, dsilu, pooled, mixed, gw, sc, w_out, *_extend([], rider, "arrays"))


def _rows_then_zeros(ref, lo, hi, rows):
    part = ref[lo:hi, :]
    return jnp.concatenate([part, jnp.zeros((rows - (hi - lo), part.shape[1]), part.dtype)], axis=0)


def _inproj_bwd_call(name, dproj, h_in, nw, w_in, dres, rider=None, transposed=False):
    seq = h_in.shape[0]
    width = dproj.shape[1]
    w_shape = tuple(w_in.shape)
    acc_shape = (width, D_MODEL) if transposed else w_shape
    whole = w_shape[0] // LANES * LANES
    tm = min(MATMUL_ROW_TILE, seq)
    nt = seq // tm
    lane_tiles = D_MODEL // LANES
    dw_shape = (w_shape[0] * lane_tiles, LANES) if transposed else w_shape
    pieces = [(lo, min(lo + ROW_MAJOR_PIECE, w_shape[0])) for lo in range(0, w_shape[0], ROW_MAJOR_PIECE)]

    def to_row_major(dw_acc, dw_stage, dw_lines):
        for lo, hi in pieces:
            for j in range(lane_tiles):
                dw_lines[pl.ds(j, hi - lo, stride=lane_tiles), :] = dw_acc[lo:hi, j * LANES:(j + 1) * LANES]
            dw_stage[lo * lane_tiles:hi * lane_tiles, :] = dw_lines[0:(hi - lo) * lane_tiles, :].astype(BF16)

    def main(dproj_ref, h_ref, nw_ref, win_ref, dres_ref, dh_ref, dw_hbm, dnw_ref, dw_acc, dw_stage, *dw_lines):
        i = pl.program_id(0)

        @pl.when(i == 0)
        def _():
            dw_acc[...] = jnp.zeros_like(dw_acc)
            dnw_ref[...] = jnp.zeros_like(dnw_ref)

        dpb = dproj_ref[...]
        if transposed:
            dn = _dot_nn(dpb[:, :whole], win_ref[0:whole, :])
            if whole < w_shape[0]:
                dn = dn + _dot_nn(dpb[:, whole:], _rows_then_zeros(win_ref, whole, w_shape[0], width - whole))
        else:
            dn = _dot_nt(dpb, win_ref[...])
        xhat, rstd = _rms(h_ref[...])
        nw_row = nw_ref[...]
        n = (xhat * nw_row).astype(BF16)
        dw_acc[...] += _dot_tn(dpb, n) if transposed else _dot_tn(n, dpb)
        dnw_ref[...] += jnp.sum(dn * xhat, axis=0, keepdims=True)
        dh_ref[...] = _rms_bwd(dn * nw_row, xhat, rstd) + dres_ref[...]

        @pl.when(i == nt - 1)
        def _():
            if transposed:
                to_row_major(dw_acc, dw_stage, *dw_lines)
            else:
                dw_stage[...] = dw_acc[...].astype(BF16)
            pltpu.sync_copy(dw_stage, dw_hbm)

    scratch = [pltpu.VMEM(acc_shape, F32), pltpu.VMEM(dw_shape, BF16)]
    if transposed:
        scratch.append(pltpu.VMEM((ROW_MAJOR_PIECE * lane_tiles, LANES), F32))

    def body(*refs):
        own, comm = _split_refs(refs, 5, 3, len(scratch), rider)
        _ride_before(comm, pl.program_id(0), nt)
        main(*own)
        _ride_after(comm, pl.program_id(0), nt)

    row = lambda i: (i, 0)
    return pl.pallas_call(
        body, name=name, grid=(nt,),
        in_specs=_extend([pl.BlockSpec((tm, width), row), pl.BlockSpec((tm, D_MODEL), row), _const((1, D_MODEL)),
                          _const(w_shape), pl.BlockSpec((tm, D_MODEL), row)], rider, "in_specs"),
        out_specs=_extend([pl.BlockSpec((tm, D_MODEL), row), pl.BlockSpec(memory_space=pl.ANY),
                           _full((1, D_MODEL))], rider, "out_specs"),
        out_shape=_extend([jax.ShapeDtypeStruct((seq, D_MODEL), F32), jax.ShapeDtypeStruct(dw_shape, BF16),
                           jax.ShapeDtypeStruct((1, D_MODEL), F32)], rider, "out_shape"),
        scratch_shapes=_extend(scratch, rider, "scratch"),
        compiler_params=_params(),
    )(dproj, h_in, nw, w_in, dres, *_extend([], rider, "arrays"))


def _chunk_scan(x, reverse):
    n = x.shape[0]
    pos = lax.broadcasted_iota(jnp.int32, (n, 1), 0) & (CHUNK - 1)
    k = 1
    while k < CHUNK:
        if reverse:
            x = x + jnp.where(pos < CHUNK - k, pltpu.roll(x, n - k, axis=0), 0.0)
        else:
            x = x + jnp.where(pos >= k, pltpu.roll(x, k, axis=0), 0.0)
        k *= 2
    return x


def _chunk_rows(j):
    return slice(j * CHUNK, (j + 1) * CHUNK)


def _kcols(h):
    return slice(h * GLA_HEAD_K, (h + 1) * GLA_HEAD_K)


def _vcols(h):
    return slice(h * GLA_HEAD_V, (h + 1) * GLA_HEAD_V)


def _chunk_masks(tm):
    idx_t = lax.broadcasted_iota(jnp.int32, (tm, tm), 0)
    idx_s = lax.broadcasted_iota(jnp.int32, (tm, tm), 1)
    same_chunk = (idx_t ^ idx_s) < CHUNK
    return same_chunk & (idx_t >= idx_s), same_chunk & (idx_t < idx_s)


class _GlaTerms:
    def __init__(self, kc, q, k, v, low_b, gkw_ref, gkb_ref, masks, saved=None):
        tm = q.shape[0]
        self.q = q * Q_SCALE
        self.k = k
        if saved is None:
            self.z = _dot_nn(low_b, gkw_ref[:, kc]) + gkb_ref[:, kc]
            log_g = (jnp.minimum(self.z, 0.0) - jnp.log(1.0 + jnp.exp(-jnp.abs(self.z)))) / GATE_NORMALIZER
            self.c = _chunk_scan(log_g, False)
        else:
            self.z, self.c = saved
        is_last = lax.broadcasted_iota(jnp.int32, (CHUNK, 1), 0) == CHUNK - 1
        self.c_last = [jnp.sum(jnp.where(is_last, self.c[_chunk_rows(j), :], 0.0), axis=0, keepdims=True)
                       for j in range(tm // CHUNK)]
        c_last_rows = jnp.concatenate([jnp.broadcast_to(r, (CHUNK, r.shape[1])) for r in self.c_last], axis=0)
        self.e_pos = jnp.exp(self.c)
        self.e_neg = jnp.exp(-self.c)
        self.e_rest = jnp.exp(c_last_rows - self.c)
        self.a_b = (self.q * self.e_pos).astype(BF16)
        self.b_b = (self.k * self.e_neg).astype(BF16)
        self.cn_b = (self.q * self.e_neg).astype(BF16)
        self.dp_b = (self.k * self.e_pos).astype(BF16)
        self.kd_b = (self.k * self.e_rest).astype(BF16)
        self.v_b = v.astype(BF16)
        self.lower, self.upper = masks

    def scores(self, kc=slice(None)):
        fwd = _dot_nt(self.a_b[:, kc], self.b_b[:, kc])
        bwd = _dot_nt(self.cn_b[:, kc], self.dp_b[:, kc])
        return jnp.where(self.lower, fwd, jnp.where(self.upper, bwd, 0.0)).astype(BF16)


def _gla_fwd_call(h1, nw, w_lines, gkw, gkb, hw, w_out, wf, target):
    seq = h1.shape[0]
    tm = min(GLA_FWD_ROW_TILE, seq)
    nt = seq // tm
    cpt = tm // CHUNK
    n_chunks = seq // CHUNK

    lane_tiles = D_MODEL // LANES
    pieces = [(lo, min(lo + WEIGHT_ROWS_PIECE, GLA_IN_WIDTH)) for lo in range(0, GLA_IN_WIDTH, WEIGHT_ROWS_PIECE)]

    def body(h_ref, nw_ref, lines_hbm, gkw_ref, gkb_ref, hw_ref, wout_ref, wf_ref, tgt_ref,
             dh2_ref, proj_ref, o_ref, st_ref, scores_ref, loss_ref, dwf_ref, win_hbm,
             state_ref, win_ref, piece_ref, lines_ref, win_sem):
        i = pl.program_id(0)
        win_copy = pltpu.make_async_copy(win_ref, win_hbm, win_sem)

        @pl.when(i == 0)
        def _():
            state_ref[...] = jnp.zeros_like(state_ref)
            loss_ref[...] = jnp.zeros_like(loss_ref)
            dwf_ref[...] = jnp.zeros_like(dwf_ref)
            for lo, hi in pieces:
                n_lines = (hi - lo) * lane_tiles
                pltpu.sync_copy(lines_hbm.at[pl.ds(lo * lane_tiles, n_lines)], piece_ref.at[pl.ds(0, n_lines)])
                lines_ref[0:n_lines, :] = piece_ref[0:n_lines, :].astype(F32)
                for j in range(lane_tiles):
                    win_ref[lo:hi, j * LANES:(j + 1) * LANES] = lines_ref[pl.ds(j, hi - lo, stride=lane_tiles),
                                                                          :].astype(BF16)
            win_copy.start()

        pl.when(i == nt - 1)(win_copy.wait)

        ht = h_ref[...]
        xhat, _ = _rms(ht)
        n = (xhat * nw_ref[...]).astype(BF16)
        sections = {}
        for name, lo, hi in (("low", GLA_QKVG_WIDTH, GLA_IN_PAD), ("qk", 0, 2 * GLA_KEY_WIDTH),
                             ("v", 2 * GLA_KEY_WIDTH, GLA_QKVG_WIDTH - GLA_VALUE_WIDTH),
                             ("gate", GLA_QKVG_WIDTH - GLA_VALUE_WIDTH, GLA_QKVG_WIDTH)):
            rows = (win_ref[lo:hi, :] if hi <= GLA_IN_WIDTH
                    else _rows_then_zeros(win_ref, lo, GLA_IN_WIDTH, hi - lo))
            sections[name] = _dot_nt(n, rows)
            proj_ref[:, lo:hi] = sections[name]
        low_b = sections["low"].astype(BF16)
        masks = _chunk_masks(tm)
        on_heads = []
        for h in range(GLA_HEADS):
            kc, vc = _kcols(h), _vcols(h)
            g = _GlaTerms(kc, sections["qk"][:, kc], sections["qk"][:, GLA_KEY_WIDTH:][:, kc], sections["v"][:, vc],
                          low_b, gkw_ref, gkb_ref, masks)
            srows = slice(h * GLA_HEAD_V, (h + 1) * GLA_HEAD_V)
            scores = g.scores()
            for b in range(tm // ROW_TILE):
                part = slice(b * ROW_TILE, (b + 1) * ROW_TILE)
                scores_ref[part, h * ROW_TILE:(h + 1) * ROW_TILE] = scores[part, part]
            o_intra = _dot_nn(scores, g.v_b)
            state = state_ref[srows, :]
            o_rows = []
            for j in range(cpt):
                r = _chunk_rows(j)
                st_ref[j, srows, :] = state
                o_rows.append(o_intra[r] + _dot_nt(g.a_b[r], state.astype(BF16)))
                decay = jnp.exp(g.c_last[j])
                state = state * decay + _dot_tn(g.v_b[r], g.kd_b[r])
            state_ref[srows, :] = state
            o_head = jnp.concatenate(o_rows, axis=0)
            o_ref[:, vc] = o_head
            proj_ref[:, GLA_SAVED_Z + kc.start:GLA_SAVED_Z + kc.stop] = g.z
            proj_ref[:, GLA_SAVED_C + kc.start:GLA_SAVED_C + kc.stop] = g.c
            on_heads.append(_rms(o_head)[0])
        gate = sections["gate"]
        on = jnp.concatenate(on_heads, axis=1) * hw_ref[...]
        y = (on * (gate * _sigmoid(gate))).astype(BF16)
        h2 = ht + _dot_nn(y, wout_ref[...])
        xhat2, rstd2 = _rms(h2)
        wf_row = wf_ref[...]
        err = xhat2 * wf_row - tgt_ref[...]
        loss_ref[...] += 0.5 * jnp.sum(err * err) / D_MODEL
        dout = err * (1.0 / D_MODEL)
        dwf_ref[...] += jnp.sum(dout * xhat2, axis=0, keepdims=True)
        dh2_ref[...] = _rms_bwd(dout * wf_row, xhat2, rstd2)

    row = lambda i: (i, 0)
    return pl.pallas_call(
        body, name="gla_fwd", grid=(nt,),
        in_specs=[pl.BlockSpec((tm, D_MODEL), row), _const((1, D_MODEL)), pl.BlockSpec(memory_space=pl.ANY),
                  _const((GLA_LOW_PAD, GLA_KEY_WIDTH)), _const((1, GLA_KEY_WIDTH)), _const((1, GLA_VALUE_WIDTH)),
                  _const((GLA_VALUE_WIDTH, D_MODEL)), _const((1, D_MODEL)), pl.BlockSpec((tm, D_MODEL), row)],
        out_specs=[pl.BlockSpec((tm, D_MODEL), row), pl.BlockSpec((tm, GLA_SAVED_WIDTH), row),
                   pl.BlockSpec((tm, GLA_VALUE_WIDTH), row),
                   pl.BlockSpec((cpt, GLA_VALUE_WIDTH, GLA_HEAD_K), lambda i: (i, 0, 0)),
                   pl.BlockSpec((tm, GLA_HEADS * ROW_TILE), row), _full((8, LANES)), _full((1, D_MODEL)),
                   pl.BlockSpec(memory_space=pl.ANY)],
        out_shape=[jax.ShapeDtypeStruct((seq, D_MODEL), F32), jax.ShapeDtypeStruct((seq, GLA_SAVED_WIDTH), F32),
                   jax.ShapeDtypeStruct((seq, GLA_VALUE_WIDTH), F32),
                   jax.ShapeDtypeStruct((n_chunks, GLA_VALUE_WIDTH, GLA_HEAD_K), F32),
                   jax.ShapeDtypeStruct((seq, GLA_HEADS * ROW_TILE), BF16),
                   jax.ShapeDtypeStruct((8, LANES), F32), jax.ShapeDtypeStruct((1, D_MODEL), F32),
                   jax.ShapeDtypeStruct((GLA_IN_WIDTH, D_MODEL), BF16)],
        scratch_shapes=[pltpu.VMEM((GLA_VALUE_WIDTH, GLA_HEAD_K), F32), pltpu.VMEM((GLA_IN_WIDTH, D_MODEL), BF16),
                        pltpu.VMEM((WEIGHT_ROWS_PIECE * lane_tiles, LANES), BF16),
                        pltpu.VMEM((WEIGHT_ROWS_PIECE * lane_tiles, LANES), F32), pltpu.SemaphoreType.DMA(())],
        compiler_params=_params(),
    )(h1, nw, w_lines, gkw, gkb, hw, w_out, wf, target)


def _gla_bwd_call(dh2, proj, o, states, scores, gkw, gkb, hw, w_out):
    seq = dh2.shape[0]
    tm = ROW_TILE
    nt = seq // tm
    cpt = tm // CHUNK

    def body(dh_ref, proj_ref, o_ref, st_ref, scores_ref, gkw_ref, gkb_ref, hw_ref, wout_ref,
             dproj_ref, dwout_hbm, dhw_ref, dgkw_ref, dgkb_ref, dstate_ref, dwout_acc, dwout_stage):
        i = pl.program_id(0)

        @pl.when(i == 0)
        def _():
            dstate_ref[...] = jnp.zeros_like(dstate_ref)
            dwout_acc[...] = jnp.zeros_like(dwout_acc)
            dhw_ref[...] = jnp.zeros_like(dhw_ref)
            dgkw_ref[...] = jnp.zeros_like(dgkw_ref)
            dgkb_ref[...] = jnp.zeros_like(dgkb_ref)

        dhb = dh_ref[...].astype(BF16)
        dy = _dot_nt(dhb, wout_ref[...])
        v0, g0 = 2 * GLA_KEY_WIDTH, GLA_QKVG_WIDTH - GLA_VALUE_WIDTH
        gate = proj_ref[:, g0:GLA_QKVG_WIDTH]
        low_b = proj_ref[:, GLA_QKVG_WIDTH:GLA_IN_PAD].astype(BF16)
        o = o_ref[...]
        hw_row = hw_ref[...]
        sg = _sigmoid(gate)
        silu = gate * sg
        don = dy * silu
        on_parts, do_parts, dhw_parts = [], [], []
        for h in range(GLA_HEADS):
            vc = _vcols(h)
            xh, rs = _rms(o[:, vc])
            on_parts.append(xh * hw_row[:, vc])
            dhw_parts.append(jnp.sum(don[:, vc] * xh, axis=0, keepdims=True))
            do_parts.append(_rms_bwd(don[:, vc] * hw_row[:, vc], xh, rs).astype(BF16))
        on = jnp.concatenate(on_parts, axis=1)
        dwout_acc[...] += _dot_tn((on * silu).astype(BF16), dhb)
        dhw_ref[...] += jnp.concatenate(dhw_parts, axis=1)
        dproj_ref[:, g0:GLA_QKVG_WIDTH] = (dy * on * (sg * (1.0 + gate * (1.0 - sg)))).astype(BF16)

        last_row = lax.broadcasted_iota(jnp.int32, (CHUNK, 1), 0) == CHUNK - 1
        g = _GlaTerms(slice(0, GLA_KEY_WIDTH), proj_ref[:, :GLA_KEY_WIDTH], proj_ref[:, GLA_KEY_WIDTH:v0],
                      proj_ref[:, v0:g0], low_b, gkw_ref, gkb_ref, _chunk_masks(tm),
                      saved=(proj_ref[:, GLA_SAVED_Z:GLA_SAVED_C], proj_ref[:, GLA_SAVED_C:GLA_SAVED_WIDTH]))
        dc_h = []
        for h in range(GLA_HEADS):
            kc, vc = _kcols(h), _vcols(h)
            k_cols = slice(GLA_KEY_WIDTH + kc.start, GLA_KEY_WIDTH + kc.stop)
            v_cols = slice(v0 + vc.start, v0 + vc.stop)
            do_h = do_parts[h]
            srows = slice(h * GLA_HEAD_V, (h + 1) * GLA_HEAD_V)
            scores = scores_ref[:, h * ROW_TILE:(h + 1) * ROW_TILE]
            dscores = _dot_nt(do_h, g.v_b[:, vc])
            dfwd = jnp.where(g.lower, dscores, 0.0).astype(BF16)
            dbwd = jnp.where(g.upper, dscores, 0.0).astype(BF16)
            dv_intra = _dot_tn(scores, do_h)
            da_intra = _dot_nn(dfwd, g.b_b[:, kc])
            db = _dot_tn(dfwd, g.a_b[:, kc])
            dcn = _dot_nn(dbwd, g.dp_b[:, kc])
            ddp = _dot_tn(dbwd, g.cn_b[:, kc])
            dstate = dstate_ref[srows, :]
            da_rows, dkd_rows, dv_rows, dcl_rows = [None] * cpt, [None] * cpt, [None] * cpt, [None] * cpt
            for j in reversed(range(cpt)):
                r = _chunk_rows(j)
                state = st_ref[j, srows, :]
                dstate_b = dstate.astype(BF16)
                do_c = do_h[r]
                dv_rows[j] = dv_intra[r] + _dot_nt(g.kd_b[r, kc], dstate_b)
                da_rows[j] = da_intra[r] + _dot_nn(do_c, state.astype(BF16))
                dkd = _dot_nn(g.v_b[r, vc], dstate_b) * g.e_rest[r, kc]
                dkd_rows[j] = dkd
                decay = jnp.exp(g.c_last[j][:, kc])
                dc_last = (jnp.sum(dkd * g.k[r, kc], axis=0, keepdims=True)
                           + decay * jnp.sum(state * dstate, axis=0, keepdims=True))
                dcl_rows[j] = jnp.where(last_row, dc_last, 0.0)
                dstate = _dot_tn(do_c, g.a_b[r, kc]) + dstate * decay
            dstate_ref[srows, :] = dstate
            da = jnp.concatenate(da_rows, axis=0)
            dkd = jnp.concatenate(dkd_rows, axis=0)
            dproj_ref[:, v_cols] = jnp.concatenate(dv_rows, axis=0).astype(BF16)
            q_up, q_down = da * g.e_pos[:, kc], dcn * g.e_neg[:, kc]
            k_up, k_down = ddp * g.e_pos[:, kc], db * g.e_neg[:, kc] + dkd
            dproj_ref[:, kc] = (Q_SCALE * (q_up + q_down)).astype(BF16)
            dproj_ref[:, k_cols] = (k_up + k_down).astype(BF16)
            dc_h.append(g.q[:, kc] * (q_up - q_down) + g.k[:, kc] * (k_up - k_down)
                        + jnp.concatenate(dcl_rows, axis=0))
        dz = _chunk_scan(jnp.concatenate(dc_h, axis=1), True) * (1.0 / GATE_NORMALIZER) * (1.0 - _sigmoid(g.z))
        dzb = dz.astype(BF16)
        dgkb_ref[...] += jnp.sum(dz, axis=0, keepdims=True)
        dgkw_ref[...] += _dot_tn(low_b, dzb)
        dproj_ref[:, GLA_QKVG_WIDTH:] = _dot_nt(dzb, gkw_ref[...]).astype(BF16)

        @pl.when(i == nt - 1)
        def _():
            dwout_stage[...] = dwout_acc[...].astype(BF16)
            pltpu.sync_copy(dwout_stage, dwout_hbm)

    rev = lambda i: (nt - 1 - i, 0)
    return pl.pallas_call(
        body, name="gla_bwd", grid=(nt,),
        in_specs=[pl.BlockSpec((tm, D_MODEL), rev), pl.BlockSpec((tm, GLA_SAVED_WIDTH), rev),
                  pl.BlockSpec((tm, GLA_VALUE_WIDTH), rev),
                  pl.BlockSpec((cpt, GLA_VALUE_WIDTH, GLA_HEAD_K), lambda i: (nt - 1 - i, 0, 0)),
                  pl.BlockSpec((tm, GLA_HEADS * ROW_TILE), rev),
                  _const((GLA_LOW_PAD, GLA_KEY_WIDTH)), _const((1, GLA_KEY_WIDTH)), _const((1, GLA_VALUE_WIDTH)),
                  _const((GLA_VALUE_WIDTH, D_MODEL))],
        out_specs=[pl.BlockSpec((tm, GLA_IN_PAD), rev), pl.BlockSpec(memory_space=pl.ANY),
                   _full((1, GLA_VALUE_WIDTH)), _full((GLA_LOW_PAD, GLA_KEY_WIDTH)), _full((1, GLA_KEY_WIDTH))],
        out_shape=[jax.ShapeDtypeStruct((seq, GLA_IN_PAD), BF16), jax.ShapeDtypeStruct((GLA_VALUE_WIDTH, D_MODEL), BF16),
                   jax.ShapeDtypeStruct((1, GLA_VALUE_WIDTH), F32), jax.ShapeDtypeStruct((GLA_LOW_PAD, GLA_KEY_WIDTH), F32),
                   jax.ShapeDtypeStruct((1, GLA_KEY_WIDTH), F32)],
        scratch_shapes=[pltpu.VMEM((GLA_VALUE_WIDTH, GLA_HEAD_K), F32), pltpu.VMEM((GLA_VALUE_WIDTH, D_MODEL), F32),
                        pltpu.VMEM((GLA_VALUE_WIDTH, D_MODEL), BF16)],
        compiler_params=_params(),
    )(dh2, proj, o, states, scores, gkw, gkb, hw, w_out)


def _position():
    return lax.axis_index("x"), lax.axis_index("y"), lax.axis_index("c")


def _lead_slot(ref, d):
    return ref.at[d]


def _row_slot(rows):
    return lambda ref, d: ref.at[pl.ds(pl.multiple_of(d * rows, rows), rows)]


def _dim1_slot(size):
    return lambda ref, d: ref.at[:, pl.ds(pl.multiple_of(d * size, size), size)]


class _Gather:
    def __init__(self, in_refs, out_refs, slots, send_sems, recv_sems, local_sems):
        self.in_refs, self.out_refs, self.slots = in_refs, out_refs, slots
        self.send_sems, self.recv_sems, self.local_sems = send_sems, recv_sems, local_sems
        self.n = len(in_refs)
        x, y, c = _position()
        self.c = c
        self.me, self.sibling = (x, y, c), (x, y, 1 - c)
        self.near = [(1 - x, y), (x, 1 - y)]
        self.diagonal = (1 - x, 1 - y)
        self.relay_from = (x ^ c, y ^ (1 - c))
        self.relay_to = (x ^ (1 - c), y ^ c)

    def _copy(self, a, k, block, to, from_input=False):
        part = self.slots[a](self.out_refs[a], 4 * block[0] + 2 * block[1] + block[2])
        return pltpu.make_async_remote_copy(
            src_ref=self.in_refs[a] if from_input else part, dst_ref=part,
            send_sem=self.send_sems.at[a, k], recv_sem=self.recv_sems.at[a, k], device_id=to, device_id_type=MESH)

    def _mine(self):
        return [pltpu.make_async_copy(self.in_refs[a], self.slots[a](self.out_refs[a], 4 * self.me[0] + 2 * self.me[1]
                                                                    + self.me[2]), self.local_sems.at[a])
                for a in range(self.n)]

    def _first(self):
        first = [self._copy(a, 0, self.me, self.sibling, True) for a in range(self.n)]
        return first + [self._copy(a, 1 + j, self.me, (*chip, self.c), True)
                        for j, chip in enumerate(self.near) for a in range(self.n)]

    def _relayed(self):
        return [self._copy(a, 3, (*self.relay_from, self.c), (*self.relay_to, self.c)) for a in range(self.n)]

    def _passed(self, j):
        chip = self.near[j] if j < 2 else self.diagonal
        return [self._copy(a, 4 + j, (*chip, self.c), self.sibling) for a in range(self.n)]

    def start(self):
        for cp in self._mine() + self._first():
            cp.start()

    def forward(self):
        for j, chip in enumerate(self.near):
            for a in range(self.n):
                self._copy(a, 1 + j, (*chip, self.c), self.me).wait_recv()
        for cp in self._relayed() + self._passed(0) + self._passed(1):
            cp.start()

    def relay(self):
        pass

    def finish(self):
        for a in range(self.n):
            self._copy(a, 3, (*self.diagonal, self.c), self.me).wait_recv()
        for cp in self._passed(2):
            cp.start()
        for a in range(self.n):
            self._copy(a, 0, self.sibling, self.me).wait_recv()
        for j, chip in enumerate(self.near + [self.diagonal]):
            for a in range(self.n):
                self._copy(a, 4 + j, (*chip, 1 - self.c), self.me).wait_recv()
        for cp in self._first() + self._relayed() + self._passed(0) + self._passed(1) + self._passed(2):
            cp.wait_send()
        for cp in self._mine():
            cp.wait()


class _Exchange:
    def __init__(self, in_refs, out_refs, slots, send_sems, recv_sems, local_sems):
        self.in_refs, self.out_refs, self.slots = in_refs, out_refs, slots
        self.send_sems, self.recv_sems, self.local_sems = send_sems, recv_sems, local_sems
        self.n = len(in_refs)
        self.pos = _position()

    def _copies(self):
        x, y, c = self.pos
        me = 4 * x + 2 * y + c
        mine = [pltpu.make_async_copy(self.slots[a](self.in_refs[a], me), self.out_refs[a].at[me],
                                      self.local_sems.at[a]) for a in range(self.n)]
        remote = []
        for k in range(1, N_DEV):
            px, py, pc = x ^ (k >> 2), y ^ ((k >> 1) & 1), c ^ (k & 1)
            for a in range(self.n):
                remote.append(pltpu.make_async_remote_copy(
                    src_ref=self.slots[a](self.in_refs[a], 4 * px + 2 * py + pc), dst_ref=self.out_refs[a].at[me],
                    send_sem=self.send_sems.at[a, k - 1], recv_sem=self.recv_sems.at[a, k - 1],
                    device_id=(px, py, pc), device_id_type=MESH))
        return mine, remote

    def start(self):
        mine, remote = self._copies()
        for cp in mine + remote:
            cp.start()

    def forward(self):
        pass

    def relay(self):
        pass

    def finish(self):
        mine, remote = self._copies()
        for cp in remote:
            cp.wait_recv()
        for cp in remote:
            cp.wait_send()
        for cp in mine:
            cp.wait()


class _Rider:
    def __init__(self, kind, arrays, out_shapes, slots, scratch=None, forward_step=None):
        self.kind, self.arrays, self.slots = kind, list(arrays), slots
        self.n = len(self.arrays)
        hbm = pl.BlockSpec(memory_space=pl.ANY)
        self.in_specs = [hbm] * self.n
        self.out_specs = [hbm] * self.n
        self.out_shape = [jax.ShapeDtypeStruct(tuple(s), a.dtype) for s, a in zip(out_shapes, self.arrays)]
        self.scratch = scratch if scratch is not None else [
            pltpu.SemaphoreType.DMA((self.n, 7)), pltpu.SemaphoreType.DMA((self.n, 7)),
            pltpu.SemaphoreType.DMA((self.n,))]
        self.forward_step = forward_step
        self.relay_step = None

    def bind(self, in_refs, out_refs, scratch):
        return self.kind(in_refs, out_refs, self.slots, *scratch)


def _gather_rider(shards, full_shapes, slots, forward_step=None):
    return _Rider(_Gather, shards, full_shapes, slots, None, forward_step)


def _exchange_rider(sends, part_shapes, slots):
    return _Rider(_Exchange, sends, [(N_DEV,) + tuple(s) for s in part_shapes], slots)


def _split_refs(refs, n_in, n_out, n_scratch, rider):
    k = rider.n if rider is not None else 0
    ins, r_ins = refs[:n_in], refs[n_in:n_in + k]
    outs, r_outs = refs[n_in + k:n_in + k + n_out], refs[n_in + k + n_out:n_in + 2 * k + n_out]
    rest = refs[n_in + 2 * k + n_out:]
    scratch, r_scratch = rest[:n_scratch], rest[n_scratch:]
    comm = rider.bind(r_ins, r_outs, r_scratch) if rider is not None else None
    if comm is not None:
        comm.forward_step, comm.relay_step = rider.forward_step, rider.relay_step
    return ins + outs + scratch, comm


def _ride_before(comm, i, nt):
    if comm is not None:
        pl.when(i == 0)(comm.start)
        pl.when(i == (nt - 1 if comm.forward_step is None else min(comm.forward_step, nt - 1)))(comm.forward)
        pl.when(i == (nt - 1 if comm.relay_step is None else min(comm.relay_step, nt - 1)))(comm.relay)


def _ride_after(comm, i, nt):
    if comm is not None:
        pl.when(i == nt - 1)(comm.finish)


def _extend(specs, rider, field):
    return list(specs) + (getattr(rider, field) if rider is not None else [])


def _comm_call(name, rider):
    def body(*refs):
        _, comm = _split_refs(refs, 0, 0, 0, rider)
        comm.start()
        comm.forward()
        comm.relay()
        comm.finish()

    return pl.pallas_call(body, name=name, in_specs=rider.in_specs, out_specs=rider.out_specs,
                          out_shape=rider.out_shape, scratch_shapes=rider.scratch,
                          compiler_params=pltpu.CompilerParams(vmem_limit_bytes=VMEM_LIMIT))(*rider.arrays)


N_CHIPS = 4


class _TwoLevel:
    def __init__(self, in_refs, out_refs, slots, *scratch):
        self.in_refs, self.out_refs, self.slots = in_refs, out_refs, slots
        self.n = n = len(in_refs)
        self.own_bufs, self.recv_bufs, self.relay_bufs = scratch[:n], scratch[n:2 * n], scratch[2 * n:3 * n]
        self.swap_send, self.swap_recv, self.local_sems, self.chip_send, self.chip_recv = scratch[3 * n:]
        x, y, c = self.pos = _position()
        self.first = (x ^ (1 - c), y ^ c)
        self.second = (x ^ c, y ^ (1 - c))
        self.chip_index = lambda chip: 2 * chip[0] + chip[1]

    def _swap(self):
        x, y, c = self.pos
        return [pltpu.make_async_remote_copy(
            src_ref=self.slots[a](self.in_refs[a], 2 * q + 1 - c), dst_ref=self.recv_bufs[a].at[q],
            send_sem=self.swap_send.at[a, q], recv_sem=self.swap_recv.at[a, q],
            device_id=(x, y, 1 - c), device_id_type=MESH) for a in range(self.n) for q in range(N_CHIPS)]

    def _mine(self):
        c = self.pos[2]
        return [pltpu.make_async_copy(self.slots[a](self.in_refs[a], 2 * q + c), self.own_bufs[a].at[q],
                                      self.local_sems.at[a, q]) for a in range(self.n) for q in range(N_CHIPS)]

    def _to_chip(self, a, k, src, dst, chip):
        return pltpu.make_async_remote_copy(
            src_ref=src, dst_ref=dst, send_sem=self.chip_send.at[a, k], recv_sem=self.chip_recv.at[a, k],
            device_id=(*chip, self.pos[2]), device_id_type=MESH)

    def _first_wave(self):
        x, y, _ = self.pos
        diagonal = self.chip_index((1 - x, 1 - y))
        passed_on = [self._to_chip(a, 1, self.own_bufs[a].at[diagonal], self.relay_bufs[a], self.first)
                     for a in range(self.n)]
        return passed_on + [self._to_chip(a, 0, self.own_bufs[a].at[self.chip_index(self.first)],
                                          self.out_refs[a].at[1], self.first) for a in range(self.n)]

    def _second_wave(self):
        return [self._to_chip(a, 2, self.own_bufs[a].at[self.chip_index(self.second)], self.out_refs[a].at[2],
                              self.second) for a in range(self.n)]

    def _own(self):
        x, y, _ = self.pos
        return [pltpu.make_async_copy(self.own_bufs[a].at[2 * x + y], self.out_refs[a].at[0],
                                      self.local_sems.at[a, N_CHIPS]) for a in range(self.n)]

    def start(self):
        for cp in self._swap() + self._mine():
            cp.start()

    def forward(self):
        swap, mine = self._swap(), self._mine()
        for a in range(self.n):
            for q in range(N_CHIPS):
                mine[a * N_CHIPS + q].wait()
                swap[a * N_CHIPS + q].wait_recv()
                self.own_bufs[a][q] = (self.own_bufs[a][q].astype(F32)
                                       + self.recv_bufs[a][q].astype(F32)).astype(BF16)
        for cp in self._first_wave() + self._own():
            cp.start()

    def relay(self):
        second = self.chip_index(self.second)
        for a in range(self.n):
            self._to_chip(a, 1, self.relay_bufs[a], self.relay_bufs[a], self.first).wait_recv()
            self.own_bufs[a][second] = (self.own_bufs[a][second].astype(F32)
                                        + self.relay_bufs[a][...].astype(F32)).astype(BF16)
        for cp in self._second_wave():
            cp.start()

    def finish(self):
        for a in range(self.n):
            self._to_chip(a, 0, self.out_refs[a].at[1], self.out_refs[a].at[1], self.first).wait_recv()
            self._to_chip(a, 2, self.out_refs[a].at[2], self.out_refs[a].at[2], self.second).wait_recv()
        for cp in self._first_wave() + self._second_wave() + self._swap():
            cp.wait_send()
        for cp in self._own():
            cp.wait()


def _two_level_rider(sends, part_shapes, slots, forward_step=None, relay_step=None):
    n = len(sends)
    bufs = [pltpu.VMEM((N_CHIPS,) + tuple(s), a.dtype) for s, a in zip(part_shapes, sends)]
    relay_bufs = [pltpu.VMEM(tuple(s), a.dtype) for s, a in zip(part_shapes, sends)]
    scratch = bufs + bufs + relay_bufs + [
        pltpu.SemaphoreType.DMA((n, N_CHIPS)), pltpu.SemaphoreType.DMA((n, N_CHIPS)),
        pltpu.SemaphoreType.DMA((n, N_CHIPS + 1)), pltpu.SemaphoreType.DMA((n, 3)), pltpu.SemaphoreType.DMA((n, 3))]
    rider = _Rider(_TwoLevel, sends, [(3,) + tuple(s) for s in part_shapes], slots, scratch, forward_step)
    rider.relay_step = relay_step
    return rider


class _Joined:
    def __init__(self, first, second):
        self.first, self.second = first, second

    def start(self):
        self.first.start()
        self.second.start()

    def forward(self):
        self.first.forward()
        self.second.forward()

    def relay(self):
        self.first.relay()
        self.second.relay()

    def finish(self):
        self.first.finish()
        self.second.finish()


class _JoinedRider:
    def __init__(self, first, second):
        self.first, self.second = first, second
        self.n = first.n + second.n
        self.arrays = first.arrays + second.arrays
        self.in_specs = first.in_specs + second.in_specs
        self.out_specs = first.out_specs + second.out_specs
        self.out_shape = first.out_shape + second.out_shape
        self.scratch = first.scratch + second.scratch
        self.forward_step = first.forward_step
        self.relay_step = first.relay_step

    def bind(self, in_refs, out_refs, scratch):
        k, s = self.first.n, len(self.first.scratch)
        return _Joined(self.first.bind(in_refs[:k], out_refs[:k], scratch[:s]),
                       self.second.bind(in_refs[k:], out_refs[k:], scratch[s:]))


def _adamw(w, g, m, v):
    m = ADAM_B1 * m + (1.0 - ADAM_B1) * g
    v = ADAM_B2 * v + (1.0 - ADAM_B2) * (g * g)
    m_hat = m / (1.0 - ADAM_B1 ** ADAM_STEP)
    v_hat = v / (1.0 - ADAM_B2 ** ADAM_STEP)
    delta = -ADAM_LR * (m_hat / (jnp.sqrt(v_hat) + ADAM_EPS) + ADAM_WD * w)
    return delta, m, v


def _sum_parts(parts_ref, index=()):
    g = parts_ref[(0,) + index].astype(F32)
    for s in range(1, parts_ref.shape[0]):
        g = g + parts_ref[(s,) + index].astype(F32)
    return g


def _adamw_group_call(name, groups):
    k = len(groups)

    def body(*refs):
        ins, outs = refs[:4 * k], refs[4 * k:]
        for i in range(k):
            parts_ref, w_ref, m_ref, v_ref = ins[4 * i:4 * i + 4]
            g = _sum_parts(parts_ref)
            delta, m_new, v_new = _adamw(w_ref[...], g, m_ref[...], v_ref[...])
            for out_ref, value in zip(outs[4 * i:4 * i + 4], (g, delta, m_new, v_new)):
                out_ref[...] = value

    vmem = pl.BlockSpec(memory_space=pltpu.VMEM)
    res = pl.pallas_call(
        body, name=name, in_specs=[vmem] * (4 * k), out_specs=[vmem] * (4 * k),
        out_shape=[jax.ShapeDtypeStruct(grp[1].shape, F32) for grp in groups for _ in range(4)],
        compiler_params=pltpu.CompilerParams(vmem_limit_bytes=VMEM_LIMIT),
    )(*[a for grp in groups for a in grp])
    return [res[4 * i:4 * i + 4] for i in range(k)]


def _adamw_slabs_call(name, parts, w, m, v, rider=None):
    def main(parts_ref, w_ref, m_ref, v_ref, g_ref, delta_ref, m_out, v_out):
        g = _sum_parts(parts_ref)
        delta, m_new, v_new = _adamw(w_ref[...], g, m_ref[...], v_ref[...])
        g_ref[...] = g
        delta_ref[...] = delta
        m_out[...] = m_new
        v_out[...] = v_new

    def body(*refs):
        own, comm = _split_refs(refs, 4, 4, 0, rider)
        if comm is not None:
            comm.start()
        main(*own)
        if comm is not None:
            comm.forward()
            comm.relay()
            comm.finish()

    vmem = pl.BlockSpec(memory_space=pltpu.VMEM)
    return pl.pallas_call(
        body, name=name, in_specs=_extend([vmem] * 4, rider, "in_specs"),
        out_specs=_extend([vmem] * 4, rider, "out_specs"),
        out_shape=_extend([jax.ShapeDtypeStruct(w.shape, F32)] * 4, rider, "out_shape"),
        scratch_shapes=_extend([], rider, "scratch"),
        compiler_params=pltpu.CompilerParams(vmem_limit_bytes=VMEM_LIMIT),
    )(parts, w, m, v, *_extend([], rider, "arrays"))


WIDE_ROWS = 8
NARROW_ROWS = 40
NARROW_GKW_ROW = 8
NARROW_GKB_ROW = 24
NARROW_HW_ROW = 32
GROUP_SHARD = POOL_GROUP_DIM // N_DEV
KEY_SHARD = GLA_KEY_WIDTH // N_DEV
HEAD_V_SHARD = GLA_HEAD_V // N_DEV


def _small_adamw_call(wide, narrow, w, m, v):
    names = ("norm_w", "pool_scale", "final_norm_w", "pool_group_b", "gla_gk_w", "gla_gk_b", "gla_head_norm_w")
    where = {
        "norm_w": (0, slice(0, 2), slice(None)),
        "pool_scale": (0, slice(2, 3), slice(None)),
        "final_norm_w": (0, slice(3, 4), slice(None)),
        "pool_group_b": (1, slice(0, POOL_GROUPS), slice(0, GROUP_SHARD)),
        "gla_gk_w": (1, slice(NARROW_GKW_ROW, NARROW_GKW_ROW + GLA_GATE_RANK), slice(0, KEY_SHARD)),
        "gla_gk_b": (1, slice(NARROW_GKB_ROW, NARROW_GKB_ROW + 1), slice(0, KEY_SHARD)),
        "gla_head_norm_w": (1, slice(NARROW_HW_ROW, NARROW_HW_ROW + 1), slice(0, HEAD_V_SHARD)),
    }
    k = len(names)

    def body(*refs):
        parts = refs[0:2]
        w_refs, m_refs, v_refs = refs[2:2 + k], refs[2 + k:2 + 2 * k], refs[2 + 2 * k:2 + 3 * k]
        outs = refs[2 + 3 * k:]
        loss_ref = outs[0]
        loss_ref[...] = _sum_parts(parts[0], (slice(4, 5), slice(0, 1)))
        for i, name in enumerate(names):
            buf, rows, cols = where[name]
            g = _sum_parts(parts[buf], (rows, cols))
            delta, m_new, v_new = _adamw(w_refs[i][...], g, m_refs[i][...], v_refs[i][...])
            outs[1 + i][...] = g
            outs[1 + k + i][...] = delta
            outs[1 + 2 * k + i][...] = m_new
            outs[1 + 3 * k + i][...] = v_new

    vmem = pl.BlockSpec(memory_space=pltpu.VMEM)
    shapes = [jax.ShapeDtypeStruct(w[n].shape, F32) for n in names]
    res = pl.pallas_call(
        body, name="adamw_small", in_specs=[vmem] * (2 + 3 * k), out_specs=[vmem] * (1 + 4 * k),
        out_shape=[jax.ShapeDtypeStruct((1, 1), F32)] + shapes * 4,
    )(wide, narrow, *[w[n] for n in names], *[m[n] for n in names], *[v[n] for n in names])
    unzip = lambda j: dict(zip(names, res[1 + j * k:1 + (j + 1) * k]))
    return res[0], unzip(0), unzip(1), unzip(2), unzip(3)


def kernel(x, norm_w, pool_in_w, pool_group_w, pool_group_b, pool_scale, pool_out_w, gla_in_w, gla_gk_w, gla_gk_b, gla_head_norm_w, gla_out_w, final_norm_w, loss_target, m_norm_w, m_pool_in_w, m_pool_group_w, m_pool_group_b, m_pool_scale, m_pool_out_w, m_gla_in_w, m_gla_gk_w, m_gla_gk_b, m_gla_head_norm_w, m_gla_out_w, m_final_norm_w, v_norm_w, v_pool_in_w, v_pool_group_w, v_pool_group_b, v_pool_scale, v_pool_out_w, v_gla_in_w, v_gla_gk_w, v_gla_gk_b, v_gla_head_norm_w, v_gla_out_w, v_final_norm_w):
    w = dict(norm_w=norm_w, pool_in_w=pool_in_w, pool_group_w=pool_group_w, pool_group_b=pool_group_b,
             pool_scale=pool_scale, pool_out_w=pool_out_w, gla_in_w=gla_in_w, gla_gk_w=gla_gk_w, gla_gk_b=gla_gk_b,
             gla_head_norm_w=gla_head_norm_w, gla_out_w=gla_out_w, final_norm_w=final_norm_w)
    m = dict(norm_w=m_norm_w, pool_in_w=m_pool_in_w, pool_group_w=m_pool_group_w, pool_group_b=m_pool_group_b,
             pool_scale=m_pool_scale, pool_out_w=m_pool_out_w, gla_in_w=m_gla_in_w, gla_gk_w=m_gla_gk_w,
             gla_gk_b=m_gla_gk_b, gla_head_norm_w=m_gla_head_norm_w, gla_out_w=m_gla_out_w,
             final_norm_w=m_final_norm_w)
    v = dict(norm_w=v_norm_w, pool_in_w=v_pool_in_w, pool_group_w=v_pool_group_w, pool_group_b=v_pool_group_b,
             pool_scale=v_pool_scale, pool_out_w=v_pool_out_w, gla_in_w=v_gla_in_w, gla_gk_w=v_gla_gk_w,
             gla_gk_b=v_gla_gk_b, gla_head_norm_w=v_gla_head_norm_w, gla_out_w=v_gla_out_w,
             final_norm_w=v_final_norm_w)
    col_shard = GLA_IN_WIDTH // N_DEV
    row_shard = D_MODEL // N_DEV

    def lanes(a):
        return jnp.pad(a, [(0, 0)] * (a.ndim - 1) + [(0, LANES - a.shape[-1])])

    small_in = jnp.concatenate([lanes(pool_group_b[0]), lanes(gla_gk_b), lanes(gla_head_norm_w),
                                jnp.zeros((2, LANES), F32)], axis=0)
    in_cols = 2 * POOL_WIDTH // N_DEV
    pool_in, pool_gw, pool_out, small_all = _comm_call("pool_weights_all_gather", _gather_rider(
        [pool_in_w[0].astype(BF16), pool_group_w[0].astype(BF16), pool_out_w[0].astype(BF16), small_in],
        [(D_MODEL, 2 * POOL_WIDTH), (POOL_GROUPS, POOL_GROUP_DIM, POOL_GROUP_DIM), (POOL_WIDTH, D_MODEL),
         (N_DEV, 8, LANES)],
        [_dim1_slot(in_cols), _dim1_slot(GROUP_SHARD), _row_slot(row_shard), _lead_slot]))
    pool_gb = jnp.transpose(small_all[:, 0:POOL_GROUPS, :GROUP_SHARD], (1, 0, 2)).reshape(1, POOL_WIDTH)
    gla_gkb = small_all[:, POOL_GROUPS, :KEY_SHARD].reshape(1, GLA_KEY_WIDTH)
    gla_hw = jnp.tile(small_all[:, POOL_GROUPS + 1, :HEAD_V_SHARD].reshape(1, GLA_HEAD_V), (1, GLA_HEADS))
    nw0, nw1, wf = norm_w[0:1], norm_w[1:2], final_norm_w.reshape(1, D_MODEL)
    xs, target = x[0], loss_target[0]

    slabs = col_shard * D_MODEL // (BF16_ROWS * LANES)
    as_slabs = lambda t: jnp.transpose(t[0]).reshape(slabs, BF16_ROWS, LANES)
    h1, pool_y, pool_silu, pool_dsilu, pooled, mixed, gla_in_parts, gkw_parts, gla_out = _pool_fwd_call(
        xs, nw0, pool_in, pool_gw, pool_gb, pool_scale, pool_out, _gather_rider(
            [as_slabs(gla_in_w).astype(BF16), gla_gk_w[0].astype(BF16), gla_out_w[0].astype(BF16)],
            [(N_DEV, slabs, BF16_ROWS, LANES), (N_DEV, GLA_GATE_RANK, KEY_SHARD), (GLA_VALUE_WIDTH, D_MODEL)],
            [_lead_slot, _lead_slot, _row_slot(row_shard)], GATHER_RELAY_STEP))
    gla_gkw = jnp.pad(jnp.transpose(gkw_parts, (1, 0, 2)).reshape(GLA_GATE_RANK, GLA_KEY_WIDTH),
                      ((0, GLA_LOW_PAD - GLA_GATE_RANK), (0, 0)))
    dh2, proj, o, states, scores, loss_part, dwf, gla_in = _gla_fwd_call(
        h1, nw1, gla_in_parts.reshape(GLA_IN_WIDTH * D_MODEL // LANES, LANES), gla_gkw, gla_gkb, gla_hw, gla_out,
        wf, target)

    dproj, d_gla_out, dhw, dgkw, dgkb = _gla_bwd_call(dh2, proj, o, states, scores, gla_gkw, gla_gkb, gla_hw,
                                                      gla_out)
    dh1, d_gla_in, dnw1, landed_gla_out = _inproj_bwd_call(
        "gla_in_bwd", dproj, h1, nw1, gla_in, dh2,
        _exchange_rider([d_gla_out], [(row_shard, D_MODEL)], [_row_slot(row_shard)]), transposed=True)
    gla_in_send = d_gla_in.reshape(N_DEV, slabs, BF16_ROWS, LANES)
    dp, d_pool_out, dgw, dgb, dsc, landed_gla_in = _pool_bwd_call(
        dh1, pool_y, pool_silu, pool_dsilu, pooled, mixed, pool_gw, pool_scale, pool_out,
        _two_level_rider([gla_in_send], [(slabs, BF16_ROWS, LANES)], [_lead_slot], TWO_LEVEL_ADD_STEP,
                         TWO_LEVEL_RELAY_STEP))
    grad_x, d_pool_in, dnw0 = _inproj_bwd_call("pool_in_bwd", dp, xs, nw0, pool_in, dh1)

    wide = jnp.concatenate([
        dnw0, dnw1, dsc, dwf, jnp.pad(loss_part[0:1, 0:1], ((0, 0), (0, D_MODEL - 1))),
        jnp.zeros((WIDE_ROWS - 5, D_MODEL), F32)], axis=0)

    def rows8(a):
        return jnp.pad(lanes(a), ((0, 0), (0, -a.shape[1] % 8), (0, 0)))

    narrow = jnp.concatenate([
        rows8(jnp.transpose(dgb.reshape(POOL_GROUPS, N_DEV, GROUP_SHARD), (1, 0, 2))),
        rows8(jnp.transpose(dgkw[:GLA_GATE_RANK].reshape(GLA_GATE_RANK, N_DEV, KEY_SHARD), (1, 0, 2))),
        rows8(dgkb.reshape(N_DEV, 1, KEY_SHARD)),
        rows8(dhw.reshape(GLA_HEADS, GLA_HEAD_V).sum(axis=0).reshape(N_DEV, 1, HEAD_V_SHARD)),
    ], axis=1)
    last_exchange = _JoinedRider(
        _two_level_rider([d_pool_in, d_pool_out, dgw],
                         [(D_MODEL, in_cols), (row_shard, D_MODEL), (POOL_GROUPS, GROUP_SHARD, POOL_GROUP_DIM)],
                         [_dim1_slot(in_cols), _row_slot(row_shard), _dim1_slot(GROUP_SHARD)]),
        _exchange_rider([wide, narrow], [(WIDE_ROWS, D_MODEL), (NARROW_ROWS, LANES)],
                        [lambda ref, d: ref, _lead_slot]))

    res = {}
    *outs, landed_pool_in, landed_pool_out, landed_gw, landed_wide, landed_narrow = _adamw_slabs_call(
        "adamw_gla_in_w", landed_gla_in, as_slabs(gla_in_w), as_slabs(m_gla_in_w), as_slabs(v_gla_in_w),
        last_exchange)
    res["gla_in_w"] = [jnp.transpose(t.reshape(col_shard, D_MODEL))[None] for t in outs]
    rest = [("pool_in_w", landed_pool_in, (D_MODEL, in_cols)),
            ("pool_group_w", landed_gw, (POOL_GROUPS * GROUP_SHARD, POOL_GROUP_DIM)),
            ("pool_out_w", landed_pool_out, (row_shard, D_MODEL)), ("gla_out_w", landed_gla_out, (row_shard, D_MODEL))]
    updates = _adamw_group_call("adamw_matrices", [
        (parts.reshape((parts.shape[0],) + shape), w[name].reshape(shape), m[name].reshape(shape),
         v[name].reshape(shape)) for name, parts, shape in rest])
    for (name, _, _), outs in zip(rest, updates):
        res[name] = [t.reshape(w[name].shape) for t in outs]
    small_shapes ={"norm_w": (2, D_MODEL), "pool_scale": (1, D_MODEL), "final_norm_w": (1, D_MODEL),
                    "pool_group_b": (POOL_GROUPS, GROUP_SHARD), "gla_gk_w": (GLA_GATE_RANK, KEY_SHARD),
                    "gla_gk_b": (1, KEY_SHARD), "gla_head_norm_w": (1, HEAD_V_SHARD)}
    as_small = lambda t: {n: t[n].reshape(s) for n, s in small_shapes.items()}
    loss, *small_outs = _small_adamw_call(landed_wide, landed_narrow, as_small(w), as_small(m), as_small(v))
    for name in small_shapes:
        res[name] = [t[name].reshape(w[name].shape) for t in small_outs]
    order = ("norm_w", "pool_in_w", "pool_group_w", "pool_group_b", "pool_scale", "pool_out_w", "gla_in_w",
             "gla_gk_w", "gla_gk_b", "gla_head_norm_w", "gla_out_w", "final_norm_w")
    return (loss.reshape(()), grad_x[None], *[res[n][0] for n in order], *[res[n][1] for n in order],
            *[res[n][2] for n in order], *[res[n][3] for n in order])
```

```python
import jax
import jax.numpy as jnp
from jax import lax
from jax.experimental import pallas as pl
from jax.experimental.pallas import tpu as pltpu

F32 = jnp.float32
BF16 = jnp.bfloat16
MESH = pl.DeviceIdType.MESH

N_DEV = 8
D_MODEL = 1024
POOL_WIDTH = 1024
POOL_GROUPS = 4
POOL_GROUP_DIM = 256
POOL_HALO = 16
GLA_HEADS = 4
GLA_HEAD_K = 128
GLA_HEAD_V = 256
GLA_KEY_WIDTH = 512
GLA_VALUE_WIDTH = 1024
GLA_GATE_RANK = 16
GLA_IN_WIDTH = 3088
GLA_IN_PAD = 3200
GLA_SAVED_Z = GLA_IN_PAD
GLA_SAVED_C = GLA_SAVED_Z + 512
GLA_SAVED_WIDTH = GLA_SAVED_C + 512
GLA_LOW_PAD = 128
GLA_QKVG_WIDTH = 3072
CHUNK = 64
GATE_NORMALIZER = 16.0
RMS_EPS = 1e-6
Q_SCALE = GLA_HEAD_K ** -0.5

ADAM_LR = 0.001
ADAM_B1 = 0.9
ADAM_B2 = 0.999
ADAM_EPS = 1e-08
ADAM_WD = 0.01
ADAM_STEP = 10

LANES = 128
BF16_ROWS = 16
VMEM_LIMIT = 56 * 1024 * 1024
ROW_TILE = 256
GLA_FWD_ROW_TILE = 512
MATMUL_ROW_TILE = 512
ROW_MAJOR_PIECE = 776
GATHER_RELAY_STEP = 5
TWO_LEVEL_ADD_STEP = 1
TWO_LEVEL_RELAY_STEP = 4


def _dot_nn(a, b):
    return lax.dot_general(a, b, (((1,), (0,)), ((), ())), preferred_element_type=F32)


def _dot_nt(a, b):
    return lax.dot_general(a, b, (((1,), (1,)), ((), ())), preferred_element_type=F32)


def _dot_tn(a, b):
    return lax.dot_general(a, b, (((0,), (0,)), ((), ())), preferred_element_type=F32)


def _rms(x):
    rstd = lax.rsqrt(jnp.mean(x * x, axis=-1, keepdims=True) + RMS_EPS)
    return x * rstd, rstd


def _rms_bwd(dxhat, xhat, rstd):
    return rstd * (dxhat - xhat * jnp.mean(dxhat * xhat, axis=-1, keepdims=True))


def _sigmoid(x):
    return 1.0 / (1.0 + jnp.exp(-x))


def _params(sem=("arbitrary",)):
    return pltpu.CompilerParams(dimension_semantics=sem, vmem_limit_bytes=VMEM_LIMIT)


def _full(shape):
    return pl.BlockSpec(shape, lambda i: (0,) * len(shape))


def _const(shape):
    return pl.BlockSpec(shape, lambda i: (0,) * len(shape), pipeline_mode=pl.Buffered(1))


def _window_sums(ext, forward):
    n = ext.shape[0]
    outs = []
    for g in range(POOL_GROUPS):
        s = ext[:, g * POOL_GROUP_DIM:(g + 1) * POOL_GROUP_DIM]
        for k in range(g + 1):
            shift = (1 << k) if forward else n - (1 << k)
            s = s + pltpu.roll(s, shift, axis=0)
        outs.append(s[:n - POOL_HALO])
    return outs


def _inv_count(row0, tm):
    row = row0 + lax.broadcasted_iota(jnp.int32, (tm, 1), 0)
    return [1.0 / jnp.minimum(row + 1, 2 << g).astype(F32) for g in range(POOL_GROUPS)]


def _pool_mix(u, u_prev, row0, gw_ref, gb):
    tm = u.shape[0]
    sums = _window_sums(jnp.concatenate([u, u_prev], axis=0), True)
    inv = _inv_count(row0, tm)
    pooled, mixed = [], []
    for g in range(POOL_GROUPS):
        ug = u[:, g * POOL_GROUP_DIM:(g + 1) * POOL_GROUP_DIM]
        pg = (sums[g] * inv[g] - ug).astype(BF16)
        pooled.append(pg)
        mixed.append(_dot_nn(pg, gw_ref[g]))
    return pooled, jnp.concatenate(mixed, axis=1) + gb


def _pool_fwd_call(x, nw, w_in, gw, gb, sc, w_out, rider=None):
    seq = x.shape[0]
    tm = min(MATMUL_ROW_TILE, seq)
    nt = seq // tm

    def main(x_ref, nw_ref, win_ref, gw_ref, gb_ref, sc_ref, wout_ref, h_ref, y_ref, silu_ref, dsilu_ref,
             pooled_ref, mixed_ref, halo_ref):
        i = pl.program_id(0)

        @pl.when(i == 0)
        def _():
            halo_ref[...] = jnp.zeros_like(halo_ref)

        xt = x_ref[...]
        xhat, _ = _rms(xt)
        n = (xhat * nw_ref[...]).astype(BF16)
        p = _dot_nn(n, win_ref[...])
        u = p[:, :POOL_WIDTH]
        gate = p[:, POOL_WIDTH:]
        sg = _sigmoid(gate)
        silu = gate * sg
        silu_ref[...] = silu
        dsilu_ref[...] = sg * (1.0 + gate * (1.0 - sg))
        pooled, mixed = _pool_mix(u, halo_ref[...], i * tm, gw_ref, gb_ref[...])
        pooled_ref[...] = jnp.concatenate(pooled, axis=1)
        mixed_ref[...] = mixed
        halo_ref[...] = u[tm - POOL_HALO:, :]
        y = (mixed * sc_ref[...] * silu).astype(BF16)
        y_ref[...] = y
        h_ref[...] = xt + _dot_nn(y, wout_ref[...])

    def body(*refs):
        own, comm = _split_refs(refs, 7, 6, 1, rider)
        _ride_before(comm, pl.program_id(0), nt)
        main(*own)
        _ride_after(comm, pl.program_id(0), nt)

    return pl.pallas_call(
        body, name="pool_fwd", grid=(nt,),
        in_specs=_extend([pl.BlockSpec((tm, D_MODEL), lambda i: (i, 0)), _const((1, D_MODEL)),
                          _const((D_MODEL, 2 * POOL_WIDTH)), _const((POOL_GROUPS, POOL_GROUP_DIM, POOL_GROUP_DIM)),
                          _const((1, POOL_WIDTH)), _const((1, POOL_WIDTH)), _const((POOL_WIDTH, D_MODEL))],
                         rider, "in_specs"),
        out_specs=_extend([pl.BlockSpec((tm, D_MODEL), lambda i: (i, 0))] * 6, rider, "out_specs"),
        out_shape=_extend([jax.ShapeDtypeStruct((seq, D_MODEL), F32), jax.ShapeDtypeStruct((seq, POOL_WIDTH), BF16),
                           jax.ShapeDtypeStruct((seq, POOL_WIDTH), F32), jax.ShapeDtypeStruct((seq, POOL_WIDTH), F32),
                           jax.ShapeDtypeStruct((seq, POOL_WIDTH), BF16),
                           jax.ShapeDtypeStruct((seq, POOL_WIDTH), F32)], rider, "out_shape"),
        scratch_shapes=_extend([pltpu.VMEM((POOL_HALO, POOL_WIDTH), F32)], rider, "scratch"),
        compiler_params=_params(),
    )(x, nw, w_in, gw, gb, sc, w_out, *_extend([], rider, "arrays"))


def _pool_bwd_call(dh, y, silu, dsilu, pooled, mixed, gw, sc, w_out, rider=None):
    seq = dh.shape[0]
    tm = min(MATMUL_ROW_TILE, seq)
    nt = seq // tm

    def main(dh_ref, y_ref, silu_ref, dsilu_ref, pooled_ref, mixed_ref, gw_ref, sc_ref, wout_ref,
             dp_ref, dwout_hbm, dgw_hbm, dgb_ref, dsc_ref, carry_ref, dwout_acc, dgw_acc, dwout_stage, dgw_stage):
        i = pl.program_id(0)
        t = nt - 1 - i

        @pl.when(i == 0)
        def _():
            carry_ref[...] = jnp.zeros_like(carry_ref)
            dwout_acc[...] = jnp.zeros_like(dwout_acc)
            dgw_acc[...] = jnp.zeros_like(dgw_acc)
            dgb_ref[...] = jnp.zeros_like(dgb_ref)
            dsc_ref[...] = jnp.zeros_like(dsc_ref)

        silu = silu_ref[...]
        pooled = [pooled_ref[:, g * POOL_GROUP_DIM:(g + 1) * POOL_GROUP_DIM] for g in range(POOL_GROUPS)]
        sc = sc_ref[...]
        dhb = dh_ref[...].astype(BF16)
        dwout_acc[...] += _dot_tn(y_ref[...], dhb)
        dy = _dot_nt(dhb, wout_ref[...])
        dmixed = dy * sc * silu
        dy_mixed = dy * mixed_ref[...]
        dsc_ref[...] += jnp.sum(dy_mixed * silu, axis=0, keepdims=True)
        dgate = dy_mixed * sc * dsilu_ref[...]
        dgb_ref[...] += jnp.sum(dmixed, axis=0, keepdims=True)
        inv = _inv_count(t * tm, tm)
        dpooled, scaled = [], []
        for g in range(POOL_GROUPS):
            dmg = dmixed[:, g * POOL_GROUP_DIM:(g + 1) * POOL_GROUP_DIM].astype(BF16)
            dgw_acc[g] += _dot_tn(pooled[g], dmg)
            dpg = _dot_nt(dmg, gw_ref[g])
            dpooled.append(dpg)
            scaled.append(dpg * inv[g])
        r = jnp.concatenate(scaled, axis=1)
        sums = _window_sums(jnp.concatenate([r, carry_ref[...]], axis=0), False)
        carry_ref[...] = r[:POOL_HALO, :]
        du = jnp.concatenate([sums[g] - dpooled[g] for g in range(POOL_GROUPS)], axis=1)
        dp_ref[...] = jnp.concatenate([du, dgate], axis=1).astype(BF16)

        @pl.when(i == nt - 1)
        def _():
            dwout_stage[...] = dwout_acc[...].astype(BF16)
            dgw_stage[...] = dgw_acc[...].astype(BF16)
            pltpu.sync_copy(dwout_stage, dwout_hbm)
            pltpu.sync_copy(dgw_stage, dgw_hbm)

    def body(*refs):
        own, comm = _split_refs(refs, 9, 5, 5, rider)
        _ride_before(comm, pl.program_id(0), nt)
        main(*own)
        _ride_after(comm, pl.program_id(0), nt)

    rev = lambda i: (nt - 1 - i, 0)
    return pl.pallas_call(
        body, name="pool_bwd", grid=(nt,),
        in_specs=_extend([pl.BlockSpec((tm, D_MODEL), rev)] * 6
                         + [_const((POOL_GROUPS, POOL_GROUP_DIM, POOL_GROUP_DIM)), _const((1, POOL_WIDTH)),
                            _const((POOL_WIDTH, D_MODEL))], rider, "in_specs"),
        out_specs=_extend([pl.BlockSpec((tm, 2 * POOL_WIDTH), rev), pl.BlockSpec(memory_space=pl.ANY),
                           pl.BlockSpec(memory_space=pl.ANY), _full((1, POOL_WIDTH)), _full((1, POOL_WIDTH))],
                          rider, "out_specs"),
        out_shape=_extend([jax.ShapeDtypeStruct((seq, 2 * POOL_WIDTH), BF16),
                           jax.ShapeDtypeStruct((POOL_WIDTH, D_MODEL), BF16),
                           jax.ShapeDtypeStruct((POOL_GROUPS, POOL_GROUP_DIM, POOL_GROUP_DIM), BF16),
                           jax.ShapeDtypeStruct((1, POOL_WIDTH), F32), jax.ShapeDtypeStruct((1, POOL_WIDTH), F32)],
                          rider, "out_shape"),
        scratch_shapes=_extend([pltpu.VMEM((POOL_HALO, POOL_WIDTH), F32), pltpu.VMEM((POOL_WIDTH, D_MODEL), F32),
                                pltpu.VMEM((POOL_GROUPS, POOL_GROUP_DIM, POOL_GROUP_DIM), F32),
                                pltpu.VMEM((POOL_WIDTH, D_MODEL), BF16),
                                pltpu.VMEM((POOL_GROUPS, POOL_GROUP_DIM, POOL_GROUP_DIM), BF16)], rider, "scratch"),
        compiler_params=_params(),
    )(dh, y, silu, dsilu, pooled, mixed, gw, sc, w_out, *_extend([], rider, "arrays"))


def _rows_then_zeros(ref, lo, hi, rows):
    part = ref[lo:hi, :]
    return jnp.concatenate([part, jnp.zeros((rows - (hi - lo), part.shape[1]), part.dtype)], axis=0)


def _inproj_bwd_call(name, dproj, h_in, nw, w_in, dres, rider=None, transposed=False):
    seq = h_in.shape[0]
    width = dproj.shape[1]
    w_shape = tuple(w_in.shape)
    acc_shape = (width, D_MODEL) if transposed else w_shape
    whole = w_shape[0] // LANES * LANES
    tm = min(MATMUL_ROW_TILE, seq)
    nt = seq // tm
    lane_tiles = D_MODEL // LANES
    dw_shape = (w_shape[0] * lane_tiles, LANES) if transposed else w_shape
    pieces = [(lo, min(lo + ROW_MAJOR_PIECE, w_shape[0])) for lo in range(0, w_shape[0], ROW_MAJOR_PIECE)]

    def to_row_major(dw_acc, dw_stage, dw_lines):
        for lo, hi in pieces:
            for j in range(lane_tiles):
                dw_lines[pl.ds(j, hi - lo, stride=lane_tiles), :] = dw_acc[lo:hi, j * LANES:(j + 1) * LANES]
            dw_stage[lo * lane_tiles:hi * lane_tiles, :] = dw_lines[0:(hi - lo) * lane_tiles, :].astype(BF16)

    def main(dproj_ref, h_ref, nw_ref, win_ref, dres_ref, dh_ref, dw_hbm, dnw_ref, dw_acc, dw_stage, *dw_lines):
        i = pl.program_id(0)

        @pl.when(i == 0)
        def _():
            dw_acc[...] = jnp.zeros_like(dw_acc)
            dnw_ref[...] = jnp.zeros_like(dnw_ref)

        dpb = dproj_ref[...]
        if transposed:
            dn = _dot_nn(dpb[:, :whole], win_ref[0:whole, :])
            if whole < w_shape[0]:
                dn = dn + _dot_nn(dpb[:, whole:], _rows_then_zeros(win_ref, whole, w_shape[0], width - whole))
        else:
            dn = _dot_nt(dpb, win_ref[...])
        xhat, rstd = _rms(h_ref[...])
        nw_row = nw_ref[...]
        n = (xhat * nw_row).astype(BF16)
        dw_acc[...] += _dot_tn(dpb, n) if transposed else _dot_tn(n, dpb)
        dnw_ref[...] += jnp.sum(dn * xhat, axis=0, keepdims=True)
        dh_ref[...] = _rms_bwd(dn * nw_row, xhat, rstd) + dres_ref[...]

        @pl.when(i == nt - 1)
        def _():
            if transposed:
                to_row_major(dw_acc, dw_stage, *dw_lines)
            else:
                dw_stage[...] = dw_acc[...].astype(BF16)
            pltpu.sync_copy(dw_stage, dw_hbm)

    scratch = [pltpu.VMEM(acc_shape, F32), pltpu.VMEM(dw_shape, BF16)]
    if transposed:
        scratch.append(pltpu.VMEM((ROW_MAJOR_PIECE * lane_tiles, LANES), F32))

    def body(*refs):
        own, comm = _split_refs(refs, 5, 3, len(scratch), rider)
        _ride_before(comm, pl.program_id(0), nt)
        main(*own)
        _ride_after(comm, pl.program_id(0), nt)

    row = lambda i: (i, 0)
    return pl.pallas_call(
        body, name=name, grid=(nt,),
        in_specs=_extend([pl.BlockSpec((tm, width), row), pl.BlockSpec((tm, D_MODEL), row), _const((1, D_MODEL)),
                          _const(w_shape), pl.BlockSpec((tm, D_MODEL), row)], rider, "in_specs"),
        out_specs=_extend([pl.BlockSpec((tm, D_MODEL), row), pl.BlockSpec(memory_space=pl.ANY),
                           _full((1, D_MODEL))], rider, "out_specs"),
        out_shape=_extend([jax.ShapeDtypeStruct((seq, D_MODEL), F32), jax.ShapeDtypeStruct(dw_shape, BF16),
                           jax.ShapeDtypeStruct((1, D_MODEL), F32)], rider, "out_shape"),
        scratch_shapes=_extend(scratch, rider, "scratch"),
        compiler_params=_params(),
    )(dproj, h_in, nw, w_in, dres, *_extend([], rider, "arrays"))


def _chunk_scan(x, reverse):
    n = x.shape[0]
    pos = lax.broadcasted_iota(jnp.int32, (n, 1), 0) & (CHUNK - 1)
    k = 1
    while k < CHUNK:
        if reverse:
            x = x + jnp.where(pos < CHUNK - k, pltpu.roll(x, n - k, axis=0), 0.0)
        else:
            x = x + jnp.where(pos >= k, pltpu.roll(x, k, axis=0), 0.0)
        k *= 2
    return x


def _chunk_rows(j):
    return slice(j * CHUNK, (j + 1) * CHUNK)


def _kcols(h):
    return slice(h * GLA_HEAD_K, (h + 1) * GLA_HEAD_K)


def _vcols(h):
    return slice(h * GLA_HEAD_V, (h + 1) * GLA_HEAD_V)


def _chunk_masks(tm):
    idx_t = lax.broadcasted_iota(jnp.int32, (tm, tm), 0)
    idx_s = lax.broadcasted_iota(jnp.int32, (tm, tm), 1)
    same_chunk = (idx_t ^ idx_s) < CHUNK
    return same_chunk & (idx_t >= idx_s), same_chunk & (idx_t < idx_s)


class _GlaTerms:
    def __init__(self, kc, q, k, v, low_b, gkw_ref, gkb_ref, masks, saved=None):
        tm = q.shape[0]
        self.q = q * Q_SCALE
        self.k = k
        if saved is None:
            self.z = _dot_nn(low_b, gkw_ref[:, kc]) + gkb_ref[:, kc]
            log_g = (jnp.minimum(self.z, 0.0) - jnp.log(1.0 + jnp.exp(-jnp.abs(self.z)))) / GATE_NORMALIZER
            self.c = _chunk_scan(log_g, False)
        else:
            self.z, self.c = saved
        is_last = lax.broadcasted_iota(jnp.int32, (CHUNK, 1), 0) == CHUNK - 1
        self.c_last = [jnp.sum(jnp.where(is_last, self.c[_chunk_rows(j), :], 0.0), axis=0, keepdims=True)
                       for j in range(tm // CHUNK)]
        c_last_rows = jnp.concatenate([jnp.broadcast_to(r, (CHUNK, r.shape[1])) for r in self.c_last], axis=0)
        self.e_pos = jnp.exp(self.c)
        self.e_neg = jnp.exp(-self.c)
        self.e_rest = jnp.exp(c_last_rows - self.c)
        self.a_b = (self.q * self.e_pos).astype(BF16)
        self.b_b = (self.k * self.e_neg).astype(BF16)
        self.cn_b = (self.q * self.e_neg).astype(BF16)
        self.dp_b = (self.k * self.e_pos).astype(BF16)
        self.kd_b = (self.k * self.e_rest).astype(BF16)
        self.v_b = v.astype(BF16)
        self.lower, self.upper = masks

    def scores(self, kc=slice(None)):
        fwd = _dot_nt(self.a_b[:, kc], self.b_b[:, kc])
        bwd = _dot_nt(self.cn_b[:, kc], self.dp_b[:, kc])
        return jnp.where(self.lower, fwd, jnp.where(self.upper, bwd, 0.0)).astype(BF16)


def _gla_fwd_call(h1, nw, w_in, gkw, gkb, hw, w_out, wf, target):
    seq = h1.shape[0]
    tm = min(GLA_FWD_ROW_TILE, seq)
    nt = seq // tm
    cpt = tm // CHUNK
    n_chunks = seq // CHUNK

    def body(h_ref, nw_ref, win_ref, gkw_ref, gkb_ref, hw_ref, wout_ref, wf_ref, tgt_ref,
             dh2_ref, proj_ref, o_ref, st_ref, scores_ref, loss_ref, dwf_ref, state_ref):
        i = pl.program_id(0)

        @pl.when(i == 0)
        def _():
            state_ref[...] = jnp.zeros_like(state_ref)
            loss_ref[...] = jnp.zeros_like(loss_ref)
            dwf_ref[...] = jnp.zeros_like(dwf_ref)

        ht = h_ref[...]
        xhat, _ = _rms(ht)
        n = (xhat * nw_ref[...]).astype(BF16)
        sections = {}
        for name, lo, hi in (("low", GLA_QKVG_WIDTH, GLA_IN_PAD), ("qk", 0, 2 * GLA_KEY_WIDTH),
                             ("v", 2 * GLA_KEY_WIDTH, GLA_QKVG_WIDTH - GLA_VALUE_WIDTH),
                             ("gate", GLA_QKVG_WIDTH - GLA_VALUE_WIDTH, GLA_QKVG_WIDTH)):
            rows = (win_ref[lo:hi, :] if hi <= GLA_IN_WIDTH
                    else _rows_then_zeros(win_ref, lo, GLA_IN_WIDTH, hi - lo))
            sections[name] = _dot_nt(n, rows)
            proj_ref[:, lo:hi] = sections[name]
        low_b = sections["low"].astype(BF16)
        masks = _chunk_masks(tm)
        on_heads = []
        for h in range(GLA_HEADS):
            kc, vc = _kcols(h), _vcols(h)
            g = _GlaTerms(kc, sections["qk"][:, kc], sections["qk"][:, GLA_KEY_WIDTH:][:, kc], sections["v"][:, vc],
                          low_b, gkw_ref, gkb_ref, masks)
            srows = slice(h * GLA_HEAD_V, (h + 1) * GLA_HEAD_V)
            scores = g.scores()
            for b in range(tm // ROW_TILE):
                part = slice(b * ROW_TILE, (b + 1) * ROW_TILE)
                scores_ref[part, h * ROW_TILE:(h + 1) * ROW_TILE] = scores[part, part]
            o_intra = _dot_nn(scores, g.v_b)
            state = state_ref[srows, :]
            o_rows = []
            for j in range(cpt):
                r = _chunk_rows(j)
                st_ref[j, srows, :] = state
                o_rows.append(o_intra[r] + _dot_nt(g.a_b[r], state.astype(BF16)))
                decay = jnp.exp(g.c_last[j])
                state = state * decay + _dot_tn(g.v_b[r], g.kd_b[r])
            state_ref[srows, :] = state
            o_head = jnp.concatenate(o_rows, axis=0)
            o_ref[:, vc] = o_head
            proj_ref[:, GLA_SAVED_Z + kc.start:GLA_SAVED_Z + kc.stop] = g.z
            proj_ref[:, GLA_SAVED_C + kc.start:GLA_SAVED_C + kc.stop] = g.c
            on_heads.append(_rms(o_head)[0])
        gate = sections["gate"]
        on = jnp.concatenate(on_heads, axis=1) * hw_ref[...]
        y = (on * (gate * _sigmoid(gate))).astype(BF16)
        h2 = ht + _dot_nn(y, wout_ref[...])
        xhat2, rstd2 = _rms(h2)
        wf_row = wf_ref[...]
        err = xhat2 * wf_row - tgt_ref[...]
        loss_ref[...] += 0.5 * jnp.sum(err * err) / D_MODEL
        dout = err * (1.0 / D_MODEL)
        dwf_ref[...] += jnp.sum(dout * xhat2, axis=0, keepdims=True)
        dh2_ref[...] = _rms_bwd(dout * wf_row, xhat2, rstd2)

    row = lambda i: (i, 0)
    return pl.pallas_call(
        body, name="gla_fwd", grid=(nt,),
        in_specs=[pl.BlockSpec((tm, D_MODEL), row), _const((1, D_MODEL)), _const((GLA_IN_WIDTH, D_MODEL)),
                  _const((GLA_LOW_PAD, GLA_KEY_WIDTH)), _const((1, GLA_KEY_WIDTH)), _const((1, GLA_VALUE_WIDTH)),
                  _const((GLA_VALUE_WIDTH, D_MODEL)), _const((1, D_MODEL)), pl.BlockSpec((tm, D_MODEL), row)],
        out_specs=[pl.BlockSpec((tm, D_MODEL), row), pl.BlockSpec((tm, GLA_SAVED_WIDTH), row),
                   pl.BlockSpec((tm, GLA_VALUE_WIDTH), row),
                   pl.BlockSpec((cpt, GLA_VALUE_WIDTH, GLA_HEAD_K), lambda i: (i, 0, 0)),
                   pl.BlockSpec((tm, GLA_HEADS * ROW_TILE), row), _full((8, LANES)), _full((1, D_MODEL))],
        out_shape=[jax.ShapeDtypeStruct((seq, D_MODEL), F32), jax.ShapeDtypeStruct((seq, GLA_SAVED_WIDTH), F32),
                   jax.ShapeDtypeStruct((seq, GLA_VALUE_WIDTH), F32),
                   jax.ShapeDtypeStruct((n_chunks, GLA_VALUE_WIDTH, GLA_HEAD_K), F32),
                   jax.ShapeDtypeStruct((seq, GLA_HEADS * ROW_TILE), BF16),
                   jax.ShapeDtypeStruct((8, LANES), F32), jax.ShapeDtypeStruct((1, D_MODEL), F32)],
        scratch_shapes=[pltpu.VMEM((GLA_VALUE_WIDTH, GLA_HEAD_K), F32)],
        compiler_params=_params(),
    )(h1, nw, w_in, gkw, gkb, hw, w_out, wf, target)


def _gla_bwd_call(dh2, proj, o, states, scores, gkw, gkb, hw, w_out):
    seq = dh2.shape[0]
    tm = ROW_TILE
    nt = seq // tm
    cpt = tm // CHUNK

    def body(dh_ref, proj_ref, o_ref, st_ref, scores_ref, gkw_ref, gkb_ref, hw_ref, wout_ref,
             dproj_ref, dwout_hbm, dhw_ref, dgkw_ref, dgkb_ref, dstate_ref, dwout_acc, dwout_stage):
        i = pl.program_id(0)

        @pl.when(i == 0)
        def _():
            dstate_ref[...] = jnp.zeros_like(dstate_ref)
            dwout_acc[...] = jnp.zeros_like(dwout_acc)
            dhw_ref[...] = jnp.zeros_like(dhw_ref)
            dgkw_ref[...] = jnp.zeros_like(dgkw_ref)
            dgkb_ref[...] = jnp.zeros_like(dgkb_ref)

        dhb = dh_ref[...].astype(BF16)
        dy = _dot_nt(dhb, wout_ref[...])
        v0, g0 = 2 * GLA_KEY_WIDTH, GLA_QKVG_WIDTH - GLA_VALUE_WIDTH
        gate = proj_ref[:, g0:GLA_QKVG_WIDTH]
        low_b = proj_ref[:, GLA_QKVG_WIDTH:GLA_IN_PAD].astype(BF16)
        o = o_ref[...]
        hw_row = hw_ref[...]
        sg = _sigmoid(gate)
        silu = gate * sg
        don = dy * silu
        on_parts, do_parts, dhw_parts = [], [], []
        for h in range(GLA_HEADS):
            vc = _vcols(h)
            xh, rs = _rms(o[:, vc])
            on_parts.append(xh * hw_row[:, vc])
            dhw_parts.append(jnp.sum(don[:, vc] * xh, axis=0, keepdims=True))
            do_parts.append(_rms_bwd(don[:, vc] * hw_row[:, vc], xh, rs).astype(BF16))
        on = jnp.concatenate(on_parts, axis=1)
        dwout_acc[...] += _dot_tn((on * silu).astype(BF16), dhb)
        dhw_ref[...] += jnp.concatenate(dhw_parts, axis=1)
        dproj_ref[:, g0:GLA_QKVG_WIDTH] = (dy * on * (sg * (1.0 + gate * (1.0 - sg)))).astype(BF16)

        last_row = lax.broadcasted_iota(jnp.int32, (CHUNK, 1), 0) == CHUNK - 1
        g = _GlaTerms(slice(0, GLA_KEY_WIDTH), proj_ref[:, :GLA_KEY_WIDTH], proj_ref[:, GLA_KEY_WIDTH:v0],
                      proj_ref[:, v0:g0], low_b, gkw_ref, gkb_ref, _chunk_masks(tm),
                      saved=(proj_ref[:, GLA_SAVED_Z:GLA_SAVED_C], proj_ref[:, GLA_SAVED_C:GLA_SAVED_WIDTH]))
        dc_h = []
        for h in range(GLA_HEADS):
            kc, vc = _kcols(h), _vcols(h)
            k_cols = slice(GLA_KEY_WIDTH + kc.start, GLA_KEY_WIDTH + kc.stop)
            v_cols = slice(v0 + vc.start, v0 + vc.stop)
            do_h = do_parts[h]
            srows = slice(h * GLA_HEAD_V, (h + 1) * GLA_HEAD_V)
            scores = scores_ref[:, h * ROW_TILE:(h + 1) * ROW_TILE]
            dscores = _dot_nt(do_h, g.v_b[:, vc])
            dfwd = jnp.where(g.lower, dscores, 0.0).astype(BF16)
            dbwd = jnp.where(g.upper, dscores, 0.0).astype(BF16)
            dv_intra = _dot_tn(scores, do_h)
            da_intra = _dot_nn(dfwd, g.b_b[:, kc])
            db = _dot_tn(dfwd, g.a_b[:, kc])
            dcn = _dot_nn(dbwd, g.dp_b[:, kc])
            ddp = _dot_tn(dbwd, g.cn_b[:, kc])
            dstate = dstate_ref[srows, :]
            da_rows, dkd_rows, dv_rows, dcl_rows = [None] * cpt, [None] * cpt, [None] * cpt, [None] * cpt
            for j in reversed(range(cpt)):
                r = _chunk_rows(j)
                state = st_ref[j, srows, :]
                dstate_b = dstate.astype(BF16)
                do_c = do_h[r]
                dv_rows[j] = dv_intra[r] + _dot_nt(g.kd_b[r, kc], dstate_b)
                da_rows[j] = da_intra[r] + _dot_nn(do_c, state.astype(BF16))
                dkd = _dot_nn(g.v_b[r, vc], dstate_b) * g.e_rest[r, kc]
                dkd_rows[j] = dkd
                decay = jnp.exp(g.c_last[j][:, kc])
                dc_last = (jnp.sum(dkd * g.k[r, kc], axis=0, keepdims=True)
                           + decay * jnp.sum(state * dstate, axis=0, keepdims=True))
                dcl_rows[j] = jnp.where(last_row, dc_last, 0.0)
                dstate = _dot_tn(do_c, g.a_b[r, kc]) + dstate * decay
            dstate_ref[srows, :] = dstate
            da = jnp.concatenate(da_rows, axis=0)
            dkd = jnp.concatenate(dkd_rows, axis=0)
            dproj_ref[:, v_cols] = jnp.concatenate(dv_rows, axis=0).astype(BF16)
            q_up, q_down = da * g.e_pos[:, kc], dcn * g.e_neg[:, kc]
            k_up, k_down = ddp * g.e_pos[:, kc], db * g.e_neg[:, kc] + dkd
            dproj_ref[:, kc] = (Q_SCALE * (q_up + q_down)).astype(BF16)
            dproj_ref[:, k_cols] = (k_up + k_down).astype(BF16)
            dc_h.append(g.q[:, kc] * (q_up - q_down) + g.k[:, kc] * (k_up - k_down)
                        + jnp.concatenate(dcl_rows, axis=0))
        dz = _chunk_scan(jnp.concatenate(dc_h, axis=1), True) * (1.0 / GATE_NORMALIZER) * (1.0 - _sigmoid(g.z))
        dzb = dz.astype(BF16)
        dgkb_ref[...] += jnp.sum(dz, axis=0, keepdims=True)
        dgkw_ref[...] += _dot_tn(low_b, dzb)
        dproj_ref[:, GLA_QKVG_WIDTH:] = _dot_nt(dzb, gkw_ref[...]).astype(BF16)

        @pl.when(i == nt - 1)
        def _():
            dwout_stage[...] = dwout_acc[...].astype(BF16)
            pltpu.sync_copy(dwout_stage, dwout_hbm)

    rev = lambda i: (nt - 1 - i, 0)
    return pl.pallas_call(
        body, name="gla_bwd", grid=(nt,),
        in_specs=[pl.BlockSpec((tm, D_MODEL), rev), pl.BlockSpec((tm, GLA_SAVED_WIDTH), rev),
                  pl.BlockSpec((tm, GLA_VALUE_WIDTH), rev),
                  pl.BlockSpec((cpt, GLA_VALUE_WIDTH, GLA_HEAD_K), lambda i: (nt - 1 - i, 0, 0)),
                  pl.BlockSpec((tm, GLA_HEADS * ROW_TILE), rev),
                  _const((GLA_LOW_PAD, GLA_KEY_WIDTH)), _const((1, GLA_KEY_WIDTH)), _const((1, GLA_VALUE_WIDTH)),
                  _const((GLA_VALUE_WIDTH, D_MODEL))],
        out_specs=[pl.BlockSpec((tm, GLA_IN_PAD), rev), pl.BlockSpec(memory_space=pl.ANY),
                   _full((1, GLA_VALUE_WIDTH)), _full((GLA_LOW_PAD, GLA_KEY_WIDTH)), _full((1, GLA_KEY_WIDTH))],
        out_shape=[jax.ShapeDtypeStruct((seq, GLA_IN_PAD), BF16), jax.ShapeDtypeStruct((GLA_VALUE_WIDTH, D_MODEL), BF16),
                   jax.ShapeDtypeStruct((1, GLA_VALUE_WIDTH), F32), jax.ShapeDtypeStruct((GLA_LOW_PAD, GLA_KEY_WIDTH), F32),
                   jax.ShapeDtypeStruct((1, GLA_KEY_WIDTH), F32)],
        scratch_shapes=[pltpu.VMEM((GLA_VALUE_WIDTH, GLA_HEAD_K), F32), pltpu.VMEM((GLA_VALUE_WIDTH, D_MODEL), F32),
                        pltpu.VMEM((GLA_VALUE_WIDTH, D_MODEL), BF16)],
        compiler_params=_params(),
    )(dh2, proj, o, states, scores, gkw, gkb, hw, w_out)


def _position():
    return lax.axis_index("x"), lax.axis_index("y"), lax.axis_index("c")


def _lead_slot(ref, d):
    return ref.at[d]


def _row_slot(rows):
    return lambda ref, d: ref.at[pl.ds(pl.multiple_of(d * rows, rows), rows)]


def _dim1_slot(size):
    return lambda ref, d: ref.at[:, pl.ds(pl.multiple_of(d * size, size), size)]


class _Gather:
    def __init__(self, in_refs, out_refs, slots, send_sems, recv_sems, local_sems):
        self.in_refs, self.out_refs, self.slots = in_refs, out_refs, slots
        self.send_sems, self.recv_sems, self.local_sems = send_sems, recv_sems, local_sems
        self.n = len(in_refs)
        x, y, c = _position()
        self.c = c
        self.me, self.sibling = (x, y, c), (x, y, 1 - c)
        self.near = [(1 - x, y), (x, 1 - y)]
        self.diagonal = (1 - x, 1 - y)
        self.relay_from = (x ^ c, y ^ (1 - c))
        self.relay_to = (x ^ (1 - c), y ^ c)

    def _copy(self, a, k, block, to, from_input=False):
        part = self.slots[a](self.out_refs[a], 4 * block[0] + 2 * block[1] + block[2])
        return pltpu.make_async_remote_copy(
            src_ref=self.in_refs[a] if from_input else part, dst_ref=part,
            send_sem=self.send_sems.at[a, k], recv_sem=self.recv_sems.at[a, k], device_id=to, device_id_type=MESH)

    def _mine(self):
        return [pltpu.make_async_copy(self.in_refs[a], self.slots[a](self.out_refs[a], 4 * self.me[0] + 2 * self.me[1]
                                                                    + self.me[2]), self.local_sems.at[a])
                for a in range(self.n)]

    def _first(self):
        first = [self._copy(a, 0, self.me, self.sibling, True) for a in range(self.n)]
        return first + [self._copy(a, 1 + j, self.me, (*chip, self.c), True)
                        for j, chip in enumerate(self.near) for a in range(self.n)]

    def _relayed(self):
        return [self._copy(a, 3, (*self.relay_from, self.c), (*self.relay_to, self.c)) for a in range(self.n)]

    def _passed(self, j):
        chip = self.near[j] if j < 2 else self.diagonal
        return [self._copy(a, 4 + j, (*chip, self.c), self.sibling) for a in range(self.n)]

    def start(self):
        for cp in self._mine() + self._first():
            cp.start()

    def forward(self):
        for j, chip in enumerate(self.near):
            for a in range(self.n):
                self._copy(a, 1 + j, (*chip, self.c), self.me).wait_recv()
        for cp in self._relayed() + self._passed(0) + self._passed(1):
            cp.start()

    def relay(self):
        pass

    def finish(self):
        for a in range(self.n):
            self._copy(a, 3, (*self.diagonal, self.c), self.me).wait_recv()
        for cp in self._passed(2):
            cp.start()
        for a in range(self.n):
            self._copy(a, 0, self.sibling, self.me).wait_recv()
        for j, chip in enumerate(self.near + [self.diagonal]):
            for a in range(self.n):
                self._copy(a, 4 + j, (*chip, 1 - self.c), self.me).wait_recv()
        for cp in self._first() + self._relayed() + self._passed(0) + self._passed(1) + self._passed(2):
            cp.wait_send()
        for cp in self._mine():
            cp.wait()


class _Exchange:
    def __init__(self, in_refs, out_refs, slots, send_sems, recv_sems, local_sems):
        self.in_refs, self.out_refs, self.slots = in_refs, out_refs, slots
        self.send_sems, self.recv_sems, self.local_sems = send_sems, recv_sems, local_sems
        self.n = len(in_refs)
        self.pos = _position()

    def _copies(self):
        x, y, c = self.pos
        me = 4 * x + 2 * y + c
        mine = [pltpu.make_async_copy(self.slots[a](self.in_refs[a], me), self.out_refs[a].at[me],
                                      self.local_sems.at[a]) for a in range(self.n)]
        remote = []
        for k in range(1, N_DEV):
            px, py, pc = x ^ (k >> 2), y ^ ((k >> 1) & 1), c ^ (k & 1)
            for a in range(self.n):
                remote.append(pltpu.make_async_remote_copy(
                    src_ref=self.slots[a](self.in_refs[a], 4 * px + 2 * py + pc), dst_ref=self.out_refs[a].at[me],
                    send_sem=self.send_sems.at[a, k - 1], recv_sem=self.recv_sems.at[a, k - 1],
                    device_id=(px, py, pc), device_id_type=MESH))
        return mine, remote

    def start(self):
        mine, remote = self._copies()
        for cp in mine + remote:
            cp.start()

    def forward(self):
        pass

    def relay(self):
        pass

    def finish(self):
        mine, remote = self._copies()
        for cp in remote:
            cp.wait_recv()
        for cp in remote:
            cp.wait_send()
        for cp in mine:
            cp.wait()


class _Rider:
    def __init__(self, kind, arrays, out_shapes, slots, scratch=None, forward_step=None):
        self.kind, self.arrays, self.slots = kind, list(arrays), slots
        self.n = len(self.arrays)
        hbm = pl.BlockSpec(memory_space=pl.ANY)
        self.in_specs = [hbm] * self.n
        self.out_specs = [hbm] * self.n
        self.out_shape = [jax.ShapeDtypeStruct(tuple(s), a.dtype) for s, a in zip(out_shapes, self.arrays)]
        self.scratch = scratch if scratch is not None else [
            pltpu.SemaphoreType.DMA((self.n, 7)), pltpu.SemaphoreType.DMA((self.n, 7)),
            pltpu.SemaphoreType.DMA((self.n,))]
        self.forward_step = forward_step
        self.relay_step = None

    def bind(self, in_refs, out_refs, scratch):
        return self.kind(in_refs, out_refs, self.slots, *scratch)


def _gather_rider(shards, full_shapes, slots, forward_step=None):
    return _Rider(_Gather, shards, full_shapes, slots, None, forward_step)


def _exchange_rider(sends, part_shapes, slots):
    return _Rider(_Exchange, sends, [(N_DEV,) + tuple(s) for s in part_shapes], slots)


def _split_refs(refs, n_in, n_out, n_scratch, rider):
    k = rider.n if rider is not None else 0
    ins, r_ins = refs[:n_in], refs[n_in:n_in + k]
    outs, r_outs = refs[n_in + k:n_in + k + n_out], refs[n_in + k + n_out:n_in + 2 * k + n_out]
    rest = refs[n_in + 2 * k + n_out:]
    scratch, r_scratch = rest[:n_scratch], rest[n_scratch:]
    comm = rider.bind(r_ins, r_outs, r_scratch) if rider is not None else None
    if comm is not None:
        comm.forward_step, comm.relay_step = rider.forward_step, rider.relay_step
    return ins + outs + scratch, comm


def _ride_before(comm, i, nt):
    if comm is not None:
        pl.when(i == 0)(comm.start)
        pl.when(i == (nt - 1 if comm.forward_step is None else min(comm.forward_step, nt - 1)))(comm.forward)
        pl.when(i == (nt - 1 if comm.relay_step is None else min(comm.relay_step, nt - 1)))(comm.relay)


def _ride_after(comm, i, nt):
    if comm is not None:
        pl.when(i == nt - 1)(comm.finish)


def _extend(specs, rider, field):
    return list(specs) + (getattr(rider, field) if rider is not None else [])


def _comm_call(name, rider, cast_from):
    order = sorted(cast_from)

    def body(*refs):
        n = rider.n
        ins, outs, rest = list(refs[:n]), refs[n:2 * n], refs[2 * n:]
        for stage, index in zip(rest[:len(order)], order):
            stage[...] = ins[index][...].astype(BF16)
            ins[index] = stage
        comm = rider.bind(ins, outs, rest[len(order):])
        comm.start()
        comm.forward()
        comm.relay()
        comm.finish()

    vmem = pl.BlockSpec(memory_space=pltpu.VMEM)
    return pl.pallas_call(
        body, name=name, in_specs=[vmem if i in cast_from else spec for i, spec in enumerate(rider.in_specs)],
        out_specs=rider.out_specs, out_shape=rider.out_shape,
        scratch_shapes=[pltpu.VMEM(cast_from[i].shape, BF16) for i in order] + rider.scratch,
        compiler_params=pltpu.CompilerParams(vmem_limit_bytes=VMEM_LIMIT),
    )(*[cast_from.get(i, a) for i, a in enumerate(rider.arrays)])


N_CHIPS = 4


class _TwoLevel:
    def __init__(self, in_refs, out_refs, slots, *scratch):
        self.in_refs, self.out_refs, self.slots = in_refs, out_refs, slots
        self.n = n = len(in_refs)
        self.own_bufs, self.recv_bufs, self.relay_bufs = scratch[:n], scratch[n:2 * n], scratch[2 * n:3 * n]
        self.swap_send, self.swap_recv, self.local_sems, self.chip_send, self.chip_recv = scratch[3 * n:]
        x, y, c = self.pos = _position()
        self.first = (x ^ (1 - c), y ^ c)
        self.second = (x ^ c, y ^ (1 - c))
        self.chip_index = lambda chip: 2 * chip[0] + chip[1]

    def _swap(self):
        x, y, c = self.pos
        return [pltpu.make_async_remote_copy(
            src_ref=self.slots[a](self.in_refs[a], 2 * q + 1 - c), dst_ref=self.recv_bufs[a].at[q],
            send_sem=self.swap_send.at[a, q], recv_sem=self.swap_recv.at[a, q],
            device_id=(x, y, 1 - c), device_id_type=MESH) for a in range(self.n) for q in range(N_CHIPS)]

    def _mine(self):
        c = self.pos[2]
        return [pltpu.make_async_copy(self.slots[a](self.in_refs[a], 2 * q + c), self.own_bufs[a].at[q],
                                      self.local_sems.at[a, q]) for a in range(self.n) for q in range(N_CHIPS)]

    def _to_chip(self, a, k, src, dst, chip):
        return pltpu.make_async_remote_copy(
            src_ref=src, dst_ref=dst, send_sem=self.chip_send.at[a, k], recv_sem=self.chip_recv.at[a, k],
            device_id=(*chip, self.pos[2]), device_id_type=MESH)

    def _first_wave(self):
        x, y, _ = self.pos
        diagonal = self.chip_index((1 - x, 1 - y))
        passed_on = [self._to_chip(a, 1, self.own_bufs[a].at[diagonal], self.relay_bufs[a], self.first)
                     for a in range(self.n)]
        return passed_on + [self._to_chip(a, 0, self.own_bufs[a].at[self.chip_index(self.first)],
                                          self.out_refs[a].at[1], self.first) for a in range(self.n)]

    def _second_wave(self):
        return [self._to_chip(a, 2, self.own_bufs[a].at[self.chip_index(self.second)], self.out_refs[a].at[2],
                              self.second) for a in range(self.n)]

    def _own(self):
        x, y, _ = self.pos
        return [pltpu.make_async_copy(self.own_bufs[a].at[2 * x + y], self.out_refs[a].at[0],
                                      self.local_sems.at[a, N_CHIPS]) for a in range(self.n)]

    def start(self):
        for cp in self._swap() + self._mine():
            cp.start()

    def forward(self):
        swap, mine = self._swap(), self._mine()
        for a in range(self.n):
            for q in range(N_CHIPS):
                mine[a * N_CHIPS + q].wait()
                swap[a * N_CHIPS + q].wait_recv()
                self.own_bufs[a][q] = (self.own_bufs[a][q].astype(F32)
                                       + self.recv_bufs[a][q].astype(F32)).astype(BF16)
        for cp in self._first_wave() + self._own():
            cp.start()

    def relay(self):
        second = self.chip_index(self.second)
        for a in range(self.n):
            self._to_chip(a, 1, self.relay_bufs[a], self.relay_bufs[a], self.first).wait_recv()
            self.own_bufs[a][second] = (self.own_bufs[a][second].astype(F32)
                                        + self.relay_bufs[a][...].astype(F32)).astype(BF16)
        for cp in self._second_wave():
            cp.start()

    def finish(self):
        for a in range(self.n):
            self._to_chip(a, 0, self.out_refs[a].at[1], self.out_refs[a].at[1], self.first).wait_recv()
            self._to_chip(a, 2, self.out_refs[a].at[2], self.out_refs[a].at[2], self.second).wait_recv()
        for cp in self._first_wave() + self._second_wave() + self._swap():
            cp.wait_send()
        for cp in self._own():
            cp.wait()


def _two_level_rider(sends, part_shapes, slots, forward_step=None, relay_step=None):
    n = len(sends)
    bufs = [pltpu.VMEM((N_CHIPS,) + tuple(s), a.dtype) for s, a in zip(part_shapes, sends)]
    relay_bufs = [pltpu.VMEM(tuple(s), a.dtype) for s, a in zip(part_shapes, sends)]
    scratch = bufs + bufs + relay_bufs + [
        pltpu.SemaphoreType.DMA((n, N_CHIPS)), pltpu.SemaphoreType.DMA((n, N_CHIPS)),
        pltpu.SemaphoreType.DMA((n, N_CHIPS + 1)), pltpu.SemaphoreType.DMA((n, 3)), pltpu.SemaphoreType.DMA((n, 3))]
    rider = _Rider(_TwoLevel, sends, [(3,) + tuple(s) for s in part_shapes], slots, scratch, forward_step)
    rider.relay_step = relay_step
    return rider


class _Joined:
    def __init__(self, first, second):
        self.first, self.second = first, second

    def start(self):
        self.first.start()
        self.second.start()

    def forward(self):
        self.first.forward()
        self.second.forward()

    def relay(self):
        self.first.relay()
        self.second.relay()

    def finish(self):
        self.first.finish()
        self.second.finish()


class _JoinedRider:
    def __init__(self, first, second):
        self.first, self.second = first, second
        self.n = first.n + second.n
        self.arrays = first.arrays + second.arrays
        self.in_specs = first.in_specs + second.in_specs
        self.out_specs = first.out_specs + second.out_specs
        self.out_shape = first.out_shape + second.out_shape
        self.scratch = first.scratch + second.scratch
        self.forward_step = first.forward_step
        self.relay_step = first.relay_step

    def bind(self, in_refs, out_refs, scratch):
        k, s = self.first.n, len(self.first.scratch)
        return _Joined(self.first.bind(in_refs[:k], out_refs[:k], scratch[:s]),
                       self.second.bind(in_refs[k:], out_refs[k:], scratch[s:]))


def _adamw(w, g, m, v):
    m = ADAM_B1 * m + (1.0 - ADAM_B1) * g
    v = ADAM_B2 * v + (1.0 - ADAM_B2) * (g * g)
    m_hat = m / (1.0 - ADAM_B1 ** ADAM_STEP)
    v_hat = v / (1.0 - ADAM_B2 ** ADAM_STEP)
    delta = -ADAM_LR * (m_hat / (jnp.sqrt(v_hat) + ADAM_EPS) + ADAM_WD * w)
    return delta, m, v


def _sum_parts(parts_ref, index=()):
    g = parts_ref[(0,) + index].astype(F32)
    for s in range(1, parts_ref.shape[0]):
        g = g + parts_ref[(s,) + index].astype(F32)
    return g


def _adamw_group_call(name, groups):
    k = len(groups)

    def body(*refs):
        ins, outs = refs[:4 * k], refs[4 * k:]
        for i in range(k):
            parts_ref, w_ref, m_ref, v_ref = ins[4 * i:4 * i + 4]
            g = _sum_parts(parts_ref)
            delta, m_new, v_new = _adamw(w_ref[...], g, m_ref[...], v_ref[...])
            for out_ref, value in zip(outs[4 * i:4 * i + 4], (g, delta, m_new, v_new)):
                out_ref[...] = value

    vmem = pl.BlockSpec(memory_space=pltpu.VMEM)
    res = pl.pallas_call(
        body, name=name, in_specs=[vmem] * (4 * k), out_specs=[vmem] * (4 * k),
        out_shape=[jax.ShapeDtypeStruct(grp[1].shape, F32) for grp in groups for _ in range(4)],
        compiler_params=pltpu.CompilerParams(vmem_limit_bytes=VMEM_LIMIT),
    )(*[a for grp in groups for a in grp])
    return [res[4 * i:4 * i + 4] for i in range(k)]


def _adamw_slabs_call(name, parts, w, m, v, rider=None):
    def main(parts_ref, w_ref, m_ref, v_ref, g_ref, delta_ref, m_out, v_out):
        g = _sum_parts(parts_ref)
        delta, m_new, v_new = _adamw(w_ref[...], g, m_ref[...], v_ref[...])
        g_ref[...] = g
        delta_ref[...] = delta
        m_out[...] = m_new
        v_out[...] = v_new

    def body(*refs):
        own, comm = _split_refs(refs, 4, 4, 0, rider)
        if comm is not None:
            comm.start()
        main(*own)
        if comm is not None:
            comm.forward()
            comm.relay()
            comm.finish()

    vmem = pl.BlockSpec(memory_space=pltpu.VMEM)
    return pl.pallas_call(
        body, name=name, in_specs=_extend([vmem] * 4, rider, "in_specs"),
        out_specs=_extend([vmem] * 4, rider, "out_specs"),
        out_shape=_extend([jax.ShapeDtypeStruct(w.shape, F32)] * 4, rider, "out_shape"),
        scratch_shapes=_extend([], rider, "scratch"),
        compiler_params=pltpu.CompilerParams(vmem_limit_bytes=VMEM_LIMIT),
    )(parts, w, m, v, *_extend([], rider, "arrays"))


WIDE_ROWS = 8
NARROW_ROWS = 40
NARROW_GKW_ROW = 8
NARROW_GKB_ROW = 24
NARROW_HW_ROW = 32
GROUP_SHARD = POOL_GROUP_DIM // N_DEV
KEY_SHARD = GLA_KEY_WIDTH // N_DEV
HEAD_V_SHARD = GLA_HEAD_V // N_DEV


def _small_adamw_call(wide, narrow, w, m, v):
    names = ("norm_w", "pool_scale", "final_norm_w", "pool_group_b", "gla_gk_w", "gla_gk_b", "gla_head_norm_w")
    where = {
        "norm_w": (0, slice(0, 2), slice(None)),
        "pool_scale": (0, slice(2, 3), slice(None)),
        "final_norm_w": (0, slice(3, 4), slice(None)),
        "pool_group_b": (1, slice(0, POOL_GROUPS), slice(0, GROUP_SHARD)),
        "gla_gk_w": (1, slice(NARROW_GKW_ROW, NARROW_GKW_ROW + GLA_GATE_RANK), slice(0, KEY_SHARD)),
        "gla_gk_b": (1, slice(NARROW_GKB_ROW, NARROW_GKB_ROW + 1), slice(0, KEY_SHARD)),
        "gla_head_norm_w": (1, slice(NARROW_HW_ROW, NARROW_HW_ROW + 1), slice(0, HEAD_V_SHARD)),
    }
    k = len(names)

    def body(*refs):
        parts = refs[0:2]
        w_refs, m_refs, v_refs = refs[2:2 + k], refs[2 + k:2 + 2 * k], refs[2 + 2 * k:2 + 3 * k]
        outs = refs[2 + 3 * k:]
        loss_ref = outs[0]
        loss_ref[...] = _sum_parts(parts[0], (slice(4, 5), slice(0, 1)))
        for i, name in enumerate(names):
            buf, rows, cols = where[name]
            g = _sum_parts(parts[buf], (rows, cols))
            delta, m_new, v_new = _adamw(w_refs[i][...], g, m_refs[i][...], v_refs[i][...])
            outs[1 + i][...] = g
            outs[1 + k + i][...] = delta
            outs[1 + 2 * k + i][...] = m_new
            outs[1 + 3 * k + i][...] = v_new

    vmem = pl.BlockSpec(memory_space=pltpu.VMEM)
    shapes = [jax.ShapeDtypeStruct(w[n].shape, F32) for n in names]
    res = pl.pallas_call(
        body, name="adamw_small", in_specs=[vmem] * (2 + 3 * k), out_specs=[vmem] * (1 + 4 * k),
        out_shape=[jax.ShapeDtypeStruct((1, 1), F32)] + shapes * 4,
    )(wide, narrow, *[w[n] for n in names], *[m[n] for n in names], *[v[n] for n in names])
    unzip = lambda j: dict(zip(names, res[1 + j * k:1 + (j + 1) * k]))
    return res[0], unzip(0), unzip(1), unzip(2), unzip(3)


def kernel(x, norm_w, pool_in_w, pool_group_w, pool_group_b, pool_scale, pool_out_w, gla_in_w, gla_gk_w, gla_gk_b, gla_head_norm_w, gla_out_w, final_norm_w, loss_target, m_norm_w, m_pool_in_w, m_pool_group_w, m_pool_group_b, m_pool_scale, m_pool_out_w, m_gla_in_w, m_gla_gk_w, m_gla_gk_b, m_gla_head_norm_w, m_gla_out_w, m_final_norm_w, v_norm_w, v_pool_in_w, v_pool_group_w, v_pool_group_b, v_pool_scale, v_pool_out_w, v_gla_in_w, v_gla_gk_w, v_gla_gk_b, v_gla_head_norm_w, v_gla_out_w, v_final_norm_w):
    w = dict(norm_w=norm_w, pool_in_w=pool_in_w, pool_group_w=pool_group_w, pool_group_b=pool_group_b,
             pool_scale=pool_scale, pool_out_w=pool_out_w, gla_in_w=gla_in_w, gla_gk_w=gla_gk_w, gla_gk_b=gla_gk_b,
             gla_head_norm_w=gla_head_norm_w, gla_out_w=gla_out_w, final_norm_w=final_norm_w)
    m = dict(norm_w=m_norm_w, pool_in_w=m_pool_in_w, pool_group_w=m_pool_group_w, pool_group_b=m_pool_group_b,
             pool_scale=m_pool_scale, pool_out_w=m_pool_out_w, gla_in_w=m_gla_in_w, gla_gk_w=m_gla_gk_w,
             gla_gk_b=m_gla_gk_b, gla_head_norm_w=m_gla_head_norm_w, gla_out_w=m_gla_out_w,
             final_norm_w=m_final_norm_w)
    v = dict(norm_w=v_norm_w, pool_in_w=v_pool_in_w, pool_group_w=v_pool_group_w, pool_group_b=v_pool_group_b,
             pool_scale=v_pool_scale, pool_out_w=v_pool_out_w, gla_in_w=v_gla_in_w, gla_gk_w=v_gla_gk_w,
             gla_gk_b=v_gla_gk_b, gla_head_norm_w=v_gla_head_norm_w, gla_out_w=v_gla_out_w,
             final_norm_w=v_final_norm_w)
    col_shard = GLA_IN_WIDTH // N_DEV
    row_shard = D_MODEL // N_DEV

    def lanes(a):
        return jnp.pad(a, [(0, 0)] * (a.ndim - 1) + [(0, LANES - a.shape[-1])])

    small_in = jnp.concatenate([lanes(pool_group_b[0]), lanes(gla_gk_b), lanes(gla_head_norm_w),
                                jnp.zeros((2, LANES), F32)], axis=0)
    in_cols = 2 * POOL_WIDTH // N_DEV
    pool_f32 = [pool_in_w[0], pool_group_w[0], pool_out_w[0]]
    pool_in, pool_gw, pool_out, small_all = _comm_call("pool_weights_all_gather", _gather_rider(
        [jax.ShapeDtypeStruct(a.shape, BF16) for a in pool_f32] + [small_in],
        [(D_MODEL, 2 * POOL_WIDTH), (POOL_GROUPS, POOL_GROUP_DIM, POOL_GROUP_DIM), (POOL_WIDTH, D_MODEL),
         (N_DEV, 8, LANES)],
        [_dim1_slot(in_cols), _dim1_slot(GROUP_SHARD), _row_slot(row_shard), _lead_slot]), dict(enumerate(pool_f32)))
    pool_gb = jnp.transpose(small_all[:, 0:POOL_GROUPS, :GROUP_SHARD], (1, 0, 2)).reshape(1, POOL_WIDTH)
    gla_gkb = small_all[:, POOL_GROUPS, :KEY_SHARD].reshape(1, GLA_KEY_WIDTH)
    gla_hw = jnp.tile(small_all[:, POOL_GROUPS + 1, :HEAD_V_SHARD].reshape(1, GLA_HEAD_V), (1, GLA_HEADS))
    nw0, nw1, wf = norm_w[0:1], norm_w[1:2], final_norm_w.reshape(1, D_MODEL)
    xs, target = x[0], loss_target[0]

    h1, pool_y, pool_silu, pool_dsilu, pooled, mixed, gla_in_parts, gkw_parts, gla_out = _pool_fwd_call(
        xs, nw0, pool_in, pool_gw, pool_gb, pool_scale, pool_out, _gather_rider(
            [jnp.transpose(gla_in_w[0]).astype(BF16), gla_gk_w[0].astype(BF16), gla_out_w[0].astype(BF16)],
            [(N_DEV, col_shard, D_MODEL), (N_DEV, GLA_GATE_RANK, KEY_SHARD), (GLA_VALUE_WIDTH, D_MODEL)],
            [_lead_slot, _lead_slot, _row_slot(row_shard)], GATHER_RELAY_STEP))
    gla_in = gla_in_parts.reshape(GLA_IN_WIDTH, D_MODEL)
    gla_gkw = jnp.pad(jnp.transpose(gkw_parts, (1, 0, 2)).reshape(GLA_GATE_RANK, GLA_KEY_WIDTH),
                      ((0, GLA_LOW_PAD - GLA_GATE_RANK), (0, 0)))
    dh2, proj, o, states, scores, loss_part, dwf = _gla_fwd_call(h1, nw1, gla_in, gla_gkw, gla_gkb, gla_hw, gla_out,
                                                                 wf, target)

    dproj, d_gla_out, dhw, dgkw, dgkb = _gla_bwd_call(dh2, proj, o, states, scores, gla_gkw, gla_gkb, gla_hw,
                                                      gla_out)
    dh1, d_gla_in, dnw1, landed_gla_out = _inproj_bwd_call(
        "gla_in_bwd", dproj, h1, nw1, gla_in, dh2,
        _exchange_rider([d_gla_out], [(row_shard, D_MODEL)], [_row_slot(row_shard)]), transposed=True)
    slabs = col_shard * D_MODEL // (BF16_ROWS * LANES)
    gla_in_send = d_gla_in.reshape(N_DEV, slabs, BF16_ROWS, LANES)
    dp, d_pool_out, dgw, dgb, dsc, landed_gla_in = _pool_bwd_call(
        dh1, pool_y, pool_silu, pool_dsilu, pooled, mixed, pool_gw, pool_scale, pool_out,
        _two_level_rider([gla_in_send], [(slabs, BF16_ROWS, LANES)], [_lead_slot], TWO_LEVEL_ADD_STEP,
                         TWO_LEVEL_RELAY_STEP))
    grad_x, d_pool_in, dnw0 = _inproj_bwd_call("pool_in_bwd", dp, xs, nw0, pool_in, dh1)

    wide = jnp.concatenate([
        dnw0, dnw1, dsc, dwf, jnp.pad(loss_part[0:1, 0:1], ((0, 0), (0, D_MODEL - 1))),
        jnp.zeros((WIDE_ROWS - 5, D_MODEL), F32)], axis=0)

    def rows8(a):
        return jnp.pad(lanes(a), ((0, 0), (0, -a.shape[1] % 8), (0, 0)))

    narrow = jnp.concatenate([
        rows8(jnp.transpose(dgb.reshape(POOL_GROUPS, N_DEV, GROUP_SHARD), (1, 0, 2))),
        rows8(jnp.transpose(dgkw[:GLA_GATE_RANK].reshape(GLA_GATE_RANK, N_DEV, KEY_SHARD), (1, 0, 2))),
        rows8(dgkb.reshape(N_DEV, 1, KEY_SHARD)),
        rows8(dhw.reshape(GLA_HEADS, GLA_HEAD_V).sum(axis=0).reshape(N_DEV, 1, HEAD_V_SHARD)),
    ], axis=1)
    last_exchange = _JoinedRider(
        _two_level_rider([d_pool_in, d_pool_out, dgw],
                         [(D_MODEL, in_cols), (row_shard, D_MODEL), (POOL_GROUPS, GROUP_SHARD, POOL_GROUP_DIM)],
                         [_dim1_slot(in_cols), _row_slot(row_shard), _dim1_slot(GROUP_SHARD)]),
        _exchange_rider([wide, narrow], [(WIDE_ROWS, D_MODEL), (NARROW_ROWS, LANES)],
                        [lambda ref, d: ref, _lead_slot]))

    res = {}
    as_slabs = lambda t: jnp.transpose(t[0]).reshape(slabs, BF16_ROWS, LANES)
    *outs, landed_pool_in, landed_pool_out, landed_gw, landed_wide, landed_narrow = _adamw_slabs_call(
        "adamw_gla_in_w", landed_gla_in, as_slabs(gla_in_w), as_slabs(m_gla_in_w), as_slabs(v_gla_in_w),
        last_exchange)
    res["gla_in_w"] = [jnp.transpose(t.reshape(col_shard, D_MODEL))[None] for t in outs]
    rest = [("pool_in_w", landed_pool_in, (D_MODEL, in_cols)),
            ("pool_group_w", landed_gw, (POOL_GROUPS * GROUP_SHARD, POOL_GROUP_DIM)),
            ("pool_out_w", landed_pool_out, (row_shard, D_MODEL)), ("gla_out_w", landed_gla_out, (row_shard, D_MODEL))]
    updates = _adamw_group_call("adamw_matrices", [
        (parts.reshape((parts.shape[0],) + shape), w[name].reshape(shape), m[name].reshape(shape),
         v[name].reshape(shape)) for name, parts, shape in rest])
    for (name, _, _), outs in zip(rest, updates):
        res[name] = [t.reshape(w[name].shape) for t in outs]
    small_shapes ={"norm_w": (2, D_MODEL), "pool_scale": (1, D_MODEL), "final_norm_w": (1, D_MODEL),
                    "pool_group_b": (POOL_GROUPS, GROUP_SHARD), "gla_gk_w": (GLA_GATE_RANK, KEY_SHARD),
                    "gla_gk_b": (1, KEY_SHARD), "gla_head_norm_w": (1, HEAD_V_SHARD)}
    as_small = lambda t: {n: t[n].reshape(s) for n, s in small_shapes.items()}
    loss, *small_outs = _small_adamw_call(landed_wide, landed_narrow, as_small(w), as_small(m), as_small(v))
    for name in small_shapes:
        res[name] = [t[name].reshape(w[name].shape) for t in small_outs]
    order = ("norm_w", "pool_in_w", "pool_group_w", "pool_group_b", "pool_scale", "pool_out_w", "gla_in_w",
             "gla_gk_w", "gla_gk_b", "gla_head_norm_w", "gla_out_w", "final_norm_w")
    return (loss.reshape(()), grad_x[None], *[res[n][0] for n in order], *[res[n][1] for n in order],
            *[res[n][2] for n in order], *[res[n][3] for n in order])
```

```python
import jax
import jax.numpy as jnp
from jax import lax
from jax.experimental import pallas as pl
from jax.experimental.pallas import tpu as pltpu

F32 = jnp.float32
BF16 = jnp.bfloat16
MESH = pl.DeviceIdType.MESH

N_DEV = 8
D_MODEL = 1024
POOL_WIDTH = 1024
POOL_GROUPS = 4
POOL_GROUP_DIM = 256
POOL_HALO = 16
GLA_HEADS = 4
GLA_HEAD_K = 128
GLA_HEAD_V = 256
GLA_KEY_WIDTH = 512
GLA_VALUE_WIDTH = 1024
GLA_GATE_RANK = 16
GLA_IN_WIDTH = 3088
GLA_IN_PAD = 3200
GLA_SAVED_Z = GLA_IN_PAD
GLA_SAVED_C = GLA_SAVED_Z + 512
GLA_SAVED_WIDTH = GLA_SAVED_C + 512
GLA_LOW_PAD = 128
GLA_QKVG_WIDTH = 3072
CHUNK = 64
GATE_NORMALIZER = 16.0
RMS_EPS = 1e-6
Q_SCALE = GLA_HEAD_K ** -0.5

ADAM_LR = 0.001
ADAM_B1 = 0.9
ADAM_B2 = 0.999
ADAM_EPS = 1e-08
ADAM_WD = 0.01
ADAM_STEP = 10

LANES = 128
BF16_ROWS = 16
VMEM_LIMIT = 60 * 1024 * 1024
ROW_TILE = 256
GLA_FWD_ROW_TILE = 512
MATMUL_ROW_TILE = 512
ROW_MAJOR_PIECE = 776
WEIGHT_ROWS_PIECE = 208
GATHER_RELAY_STEP = 5
TWO_LEVEL_ADD_STEP = 1
TWO_LEVEL_RELAY_STEP = 4


def _dot_nn(a, b):
    return lax.dot_general(a, b, (((1,), (0,)), ((), ())), preferred_element_type=F32)


def _dot_nt(a, b):
    return lax.dot_general(a, b, (((1,), (1,)), ((), ())), preferred_element_type=F32)


def _dot_tn(a, b):
    return lax.dot_general(a, b, (((0,), (0,)), ((), ())), preferred_element_type=F32)


def _rms(x):
    rstd = lax.rsqrt(jnp.mean(x * x, axis=-1, keepdims=True) + RMS_EPS)
    return x * rstd, rstd


def _rms_bwd(dxhat, xhat, rstd):
    return rstd * (dxhat - xhat * jnp.mean(dxhat * xhat, axis=-1, keepdims=True))


def _sigmoid(x):
    return 1.0 / (1.0 + jnp.exp(-x))


def _params(sem=("arbitrary",)):
    return pltpu.CompilerParams(dimension_semantics=sem, vmem_limit_bytes=VMEM_LIMIT)


def _full(shape):
    return pl.BlockSpec(shape, lambda i: (0,) * len(shape))


def _const(shape):
    return pl.BlockSpec(shape, lambda i: (0,) * len(shape), pipeline_mode=pl.Buffered(1))


def _window_sums(ext, forward):
    n = ext.shape[0]
    outs = []
    for g in range(POOL_GROUPS):
        s = ext[:, g * POOL_GROUP_DIM:(g + 1) * POOL_GROUP_DIM]
        for k in range(g + 1):
            shift = (1 << k) if forward else n - (1 << k)
            s = s + pltpu.roll(s, shift, axis=0)
        outs.append(s[:n - POOL_HALO])
    return outs


def _inv_count(row0, tm):
    row = row0 + lax.broadcasted_iota(jnp.int32, (tm, 1), 0)
    return [1.0 / jnp.minimum(row + 1, 2 << g).astype(F32) for g in range(POOL_GROUPS)]


def _pool_mix(u, u_prev, row0, gw_ref, gb):
    tm = u.shape[0]
    sums = _window_sums(jnp.concatenate([u, u_prev], axis=0), True)
    inv = _inv_count(row0, tm)
    pooled, mixed = [], []
    for g in range(POOL_GROUPS):
        ug = u[:, g * POOL_GROUP_DIM:(g + 1) * POOL_GROUP_DIM]
        pg = (sums[g] * inv[g] - ug).astype(BF16)
        pooled.append(pg)
        mixed.append(_dot_nn(pg, gw_ref[g]))
    return pooled, jnp.concatenate(mixed, axis=1) + gb


def _pool_fwd_call(x, nw, w_in, gw, gb, sc, w_out, rider=None):
    seq = x.shape[0]
    tm = min(MATMUL_ROW_TILE, seq)
    nt = seq // tm

    def main(x_ref, nw_ref, win_ref, gw_ref, gb_ref, sc_ref, wout_ref, h_ref, y_ref, silu_ref, dsilu_ref,
             pooled_ref, mixed_ref, halo_ref):
        i = pl.program_id(0)

        @pl.when(i == 0)
        def _():
            halo_ref[...] = jnp.zeros_like(halo_ref)

        xt = x_ref[...]
        xhat, _ = _rms(xt)
        n = (xhat * nw_ref[...]).astype(BF16)
        p = _dot_nn(n, win_ref[...])
        u = p[:, :POOL_WIDTH]
        gate = p[:, POOL_WIDTH:]
        sg = _sigmoid(gate)
        silu = gate * sg
        silu_ref[...] = silu
        dsilu_ref[...] = sg * (1.0 + gate * (1.0 - sg))
        pooled, mixed = _pool_mix(u, halo_ref[...], i * tm, gw_ref, gb_ref[...])
        pooled_ref[...] = jnp.concatenate(pooled, axis=1)
        mixed_ref[...] = mixed
        halo_ref[...] = u[tm - POOL_HALO:, :]
        y = (mixed * sc_ref[...] * silu).astype(BF16)
        y_ref[...] = y
        h_ref[...] = xt + _dot_nn(y, wout_ref[...])

    def body(*refs):
        own, comm = _split_refs(refs, 7, 6, 1, rider)
        _ride_before(comm, pl.program_id(0), nt)
        main(*own)
        _ride_after(comm, pl.program_id(0), nt)

    return pl.pallas_call(
        body, name="pool_fwd", grid=(nt,),
        in_specs=_extend([pl.BlockSpec((tm, D_MODEL), lambda i: (i, 0)), _const((1, D_MODEL)),
                          _const((D_MODEL, 2 * POOL_WIDTH)), _const((POOL_GROUPS, POOL_GROUP_DIM, POOL_GROUP_DIM)),
                          _const((1, POOL_WIDTH)), _const((1, POOL_WIDTH)), _const((POOL_WIDTH, D_MODEL))],
                         rider, "in_specs"),
        out_specs=_extend([pl.BlockSpec((tm, D_MODEL), lambda i: (i, 0))] * 6, rider, "out_specs"),
        out_shape=_extend([jax.ShapeDtypeStruct((seq, D_MODEL), F32), jax.ShapeDtypeStruct((seq, POOL_WIDTH), BF16),
                           jax.ShapeDtypeStruct((seq, POOL_WIDTH), F32), jax.ShapeDtypeStruct((seq, POOL_WIDTH), F32),
                           jax.ShapeDtypeStruct((seq, POOL_WIDTH), BF16),
                           jax.ShapeDtypeStruct((seq, POOL_WIDTH), F32)], rider, "out_shape"),
        scratch_shapes=_extend([pltpu.VMEM((POOL_HALO, POOL_WIDTH), F32)], rider, "scratch"),
        compiler_params=_params(),
    )(x, nw, w_in, gw, gb, sc, w_out, *_extend([], rider, "arrays"))


def _pool_bwd_call(dh, y, silu, dsilu, pooled, mixed, gw, sc, w_out, rider=None):
    seq = dh.shape[0]
    tm = min(MATMUL_ROW_TILE, seq)
    nt = seq // tm

    def main(dh_ref, y_ref, silu_ref, dsilu_ref, pooled_ref, mixed_ref, gw_ref, sc_ref, wout_ref,
             dp_ref, dwout_hbm, dgw_hbm, dgb_ref, dsc_ref, carry_ref, dwout_acc, dgw_acc, dwout_stage, dgw_stage):
        i = pl.program_id(0)
        t = nt - 1 - i

        @pl.when(i == 0)
        def _():
            carry_ref[...] = jnp.zeros_like(carry_ref)
            dwout_acc[...] = jnp.zeros_like(dwout_acc)
            dgw_acc[...] = jnp.zeros_like(dgw_acc)
            dgb_ref[...] = jnp.zeros_like(dgb_ref)
            dsc_ref[...] = jnp.zeros_like(dsc_ref)

        silu = silu_ref[...]
        pooled = [pooled_ref[:, g * POOL_GROUP_DIM:(g + 1) * POOL_GROUP_DIM] for g in range(POOL_GROUPS)]
        sc = sc_ref[...]
        dhb = dh_ref[...].astype(BF16)
        dwout_acc[...] += _dot_tn(y_ref[...], dhb)
        dy = _dot_nt(dhb, wout_ref[...])
        dmixed = dy * sc * silu
        dy_mixed = dy * mixed_ref[...]
        dsc_ref[...] += jnp.sum(dy_mixed * silu, axis=0, keepdims=True)
        dgate = dy_mixed * sc * dsilu_ref[...]
        dgb_ref[...] += jnp.sum(dmixed, axis=0, keepdims=True)
        inv = _inv_count(t * tm, tm)
        dpooled, scaled = [], []
        for g in range(POOL_GROUPS):
            dmg = dmixed[:, g * POOL_GROUP_DIM:(g + 1) * POOL_GROUP_DIM].astype(BF16)
            dgw_acc[g] += _dot_tn(pooled[g], dmg)
            dpg = _dot_nt(dmg, gw_ref[g])
            dpooled.append(dpg)
            scaled.append(dpg * inv[g])
        r = jnp.concatenate(scaled, axis=1)
        sums = _window_sums(jnp.concatenate([r, carry_ref[...]], axis=0), False)
        carry_ref[...] = r[:POOL_HALO, :]
        du = jnp.concatenate([sums[g] - dpooled[g] for g in range(POOL_GROUPS)], axis=1)
        dp_ref[...] = jnp.concatenate([du, dgate], axis=1).astype(BF16)

        @pl.when(i == nt - 1)
        def _():
            dwout_stage[...] = dwout_acc[...].astype(BF16)
            dgw_stage[...] = dgw_acc[...].astype(BF16)
            pltpu.sync_copy(dwout_stage, dwout_hbm)
            pltpu.sync_copy(dgw_stage, dgw_hbm)

    def body(*refs):
        own, comm = _split_refs(refs, 9, 5, 5, rider)
        _ride_before(comm, pl.program_id(0), nt)
        main(*own)
        _ride_after(comm, pl.program_id(0), nt)

    rev = lambda i: (nt - 1 - i, 0)
    return pl.pallas_call(
        body, name="pool_bwd", grid=(nt,),
        in_specs=_extend([pl.BlockSpec((tm, D_MODEL), rev)] * 6
                         + [_const((POOL_GROUPS, POOL_GROUP_DIM, POOL_GROUP_DIM)), _const((1, POOL_WIDTH)),
                            _const((POOL_WIDTH, D_MODEL))], rider, "in_specs"),
        out_specs=_extend([pl.BlockSpec((tm, 2 * POOL_WIDTH), rev), pl.BlockSpec(memory_space=pl.ANY),
                           pl.BlockSpec(memory_space=pl.ANY), _full((1, POOL_WIDTH)), _full((1, POOL_WIDTH))],
                          rider, "out_specs"),
        out_shape=_extend([jax.ShapeDtypeStruct((seq, 2 * POOL_WIDTH), BF16),
                           jax.ShapeDtypeStruct((POOL_WIDTH, D_MODEL), BF16),
                           jax.ShapeDtypeStruct((POOL_GROUPS, POOL_GROUP_DIM, POOL_GROUP_DIM), BF16),
                           jax.ShapeDtypeStruct((1, POOL_WIDTH), F32), jax.ShapeDtypeStruct((1, POOL_WIDTH), F32)],
                          rider, "out_shape"),
        scratch_shapes=_extend([pltpu.VMEM((POOL_HALO, POOL_WIDTH), F32), pltpu.VMEM((POOL_WIDTH, D_MODEL), F32),
                                pltpu.VMEM((POOL_GROUPS, POOL_GROUP_DIM, POOL_GROUP_DIM), F32),
                                pltpu.VMEM((POOL_WIDTH, D_MODEL), BF16),
                                pltpu.VMEM((POOL_GROUPS, POOL_GROUP_DIM, POOL_GROUP_DIM), BF16)], rider, "scratch"),
        compiler_params=_params(),
    )(dh, y, silu, dsilu, pooled, mixed, gw, sc, w_out, *_extend([], rider, "arrays"))


def _rows_then_zeros(ref, lo, hi, rows):
    part = ref[lo:hi, :]
    return jnp.concatenate([part, jnp.zeros((rows - (hi - lo), part.shape[1]), part.dtype)], axis=0)


def _inproj_bwd_call(name, dproj, h_in, nw, w_in, dres, rider=None, transposed=False):
    seq = h_in.shape[0]
    width = dproj.shape[1]
    w_shape = tuple(w_in.shape)
    acc_shape = (width, D_MODEL) if transposed else w_shape
    whole = w_shape[0] // LANES * LANES
    tm = min(MATMUL_ROW_TILE, seq)
    nt = seq // tm
    lane_tiles = D_MODEL // LANES
    dw_shape = (w_shape[0] * lane_tiles, LANES) if transposed else w_shape
    pieces = [(lo, min(lo + ROW_MAJOR_PIECE, w_shape[0])) for lo in range(0, w_shape[0], ROW_MAJOR_PIECE)]

    def to_row_major(dw_acc, dw_stage, dw_lines):
        for lo, hi in pieces:
            for j in range(lane_tiles):
                dw_lines[pl.ds(j, hi - lo, stride=lane_tiles), :] = dw_acc[lo:hi, j * LANES:(j + 1) * LANES]
            dw_stage[lo * lane_tiles:hi * lane_tiles, :] = dw_lines[0:(hi - lo) * lane_tiles, :].astype(BF16)

    def main(dproj_ref, h_ref, nw_ref, win_ref, dres_ref, dh_ref, dw_hbm, dnw_ref, dw_acc, dw_stage, *dw_lines):
        i = pl.program_id(0)

        @pl.when(i == 0)
        def _():
            dw_acc[...] = jnp.zeros_like(dw_acc)
            dnw_ref[...] = jnp.zeros_like(dnw_ref)

        dpb = dproj_ref[...]
        if transposed:
            dn = _dot_nn(dpb[:, :whole], win_ref[0:whole, :])
            if whole < w_shape[0]:
                dn = dn + _dot_nn(dpb[:, whole:], _rows_then_zeros(win_ref, whole, w_shape[0], width - whole))
        else:
            dn = _dot_nt(dpb, win_ref[...])
        xhat, rstd = _rms(h_ref[...])
        nw_row = nw_ref[...]
        n = (xhat * nw_row).astype(BF16)
        dw_acc[...] += _dot_tn(dpb, n) if transposed else _dot_tn(n, dpb)
        dnw_ref[...] += jnp.sum(dn * xhat, axis=0, keepdims=True)
        dh_ref[...] = _rms_bwd(dn * nw_row, xhat, rstd) + dres_ref[...]

        @pl.when(i == nt - 1)
        def _():
            if transposed:
                to_row_major(dw_acc, dw_stage, *dw_lines)
            else:
                dw_stage[...] = dw_acc[...].astype(BF16)
            pltpu.sync_copy(dw_stage, dw_hbm)

    scratch = [pltpu.VMEM(acc_shape, F32), pltpu.VMEM(dw_shape, BF16)]
    if transposed:
        scratch.append(pltpu.VMEM((ROW_MAJOR_PIECE * lane_tiles, LANES), F32))

    def body(*refs):
        own, comm = _split_refs(refs, 5, 3, len(scratch), rider)
        _ride_before(comm, pl.program_id(0), nt)
        main(*own)
        _ride_after(comm, pl.program_id(0), nt)

    row = lambda i: (i, 0)
    return pl.pallas_call(
        body, name=name, grid=(nt,),
        in_specs=_extend([pl.BlockSpec((tm, width), row), pl.BlockSpec((tm, D_MODEL), row), _const((1, D_MODEL)),
                          _const(w_shape), pl.BlockSpec((tm, D_MODEL), row)], rider, "in_specs"),
        out_specs=_extend([pl.BlockSpec((tm, D_MODEL), row), pl.BlockSpec(memory_space=pl.ANY),
                           _full((1, D_MODEL))], rider, "out_specs"),
        out_shape=_extend([jax.ShapeDtypeStruct((seq, D_MODEL), F32), jax.ShapeDtypeStruct(dw_shape, BF16),
                           jax.ShapeDtypeStruct((1, D_MODEL), F32)], rider, "out_shape"),
        scratch_shapes=_extend(scratch, rider, "scratch"),
        compiler_params=_params(),
    )(dproj, h_in, nw, w_in, dres, *_extend([], rider, "arrays"))


def _chunk_scan(x, reverse):
    n = x.shape[0]
    pos = lax.broadcasted_iota(jnp.int32, (n, 1), 0) & (CHUNK - 1)
    k = 1
    while k < CHUNK:
        if reverse:
            x = x + jnp.where(pos < CHUNK - k, pltpu.roll(x, n - k, axis=0), 0.0)
        else:
            x = x + jnp.where(pos >= k, pltpu.roll(x, k, axis=0), 0.0)
        k *= 2
    return x


def _chunk_rows(j):
    return slice(j * CHUNK, (j + 1) * CHUNK)


def _kcols(h):
    return slice(h * GLA_HEAD_K, (h + 1) * GLA_HEAD_K)


def _vcols(h):
    return slice(h * GLA_HEAD_V, (h + 1) * GLA_HEAD_V)


def _chunk_masks(tm):
    idx_t = lax.broadcasted_iota(jnp.int32, (tm, tm), 0)
    idx_s = lax.broadcasted_iota(jnp.int32, (tm, tm), 1)
    same_chunk = (idx_t ^ idx_s) < CHUNK
    return same_chunk & (idx_t >= idx_s), same_chunk & (idx_t < idx_s)


class _GlaTerms:
    def __init__(self, kc, q, k, v, low_b, gkw_ref, gkb_ref, masks, saved=None):
        tm = q.shape[0]
        self.q = q * Q_SCALE
        self.k = k
        if saved is None:
            self.z = _dot_nn(low_b, gkw_ref[:, kc]) + gkb_ref[:, kc]
            log_g = (jnp.minimum(self.z, 0.0) - jnp.log(1.0 + jnp.exp(-jnp.abs(self.z)))) / GATE_NORMALIZER
            self.c = _chunk_scan(log_g, False)
        else:
            self.z, self.c = saved
        is_last = lax.broadcasted_iota(jnp.int32, (CHUNK, 1), 0) == CHUNK - 1
        self.c_last = [jnp.sum(jnp.where(is_last, self.c[_chunk_rows(j), :], 0.0), axis=0, keepdims=True)
                       for j in range(tm // CHUNK)]
        c_last_rows = jnp.concatenate([jnp.broadcast_to(r, (CHUNK, r.shape[1])) for r in self.c_last], axis=0)
        self.e_pos = jnp.exp(self.c)
        self.e_neg = jnp.exp(-self.c)
        self.e_rest = jnp.exp(c_last_rows - self.c)
        self.a_b = (self.q * self.e_pos).astype(BF16)
        self.b_b = (self.k * self.e_neg).astype(BF16)
        self.cn_b = (self.q * self.e_neg).astype(BF16)
        self.dp_b = (self.k * self.e_pos).astype(BF16)
        self.kd_b = (self.k * self.e_rest).astype(BF16)
        self.v_b = v.astype(BF16)
        self.lower, self.upper = masks

    def scores(self, kc=slice(None)):
        fwd = _dot_nt(self.a_b[:, kc], self.b_b[:, kc])
        bwd = _dot_nt(self.cn_b[:, kc], self.dp_b[:, kc])
        return jnp.where(self.lower, fwd, jnp.where(self.upper, bwd, 0.0)).astype(BF16)


def _gla_fwd_call(h1, nw, w_lines, gkw, gkb, hw, w_out, wf, target):
    seq = h1.shape[0]
    tm = min(GLA_FWD_ROW_TILE, seq)
    nt = seq // tm
    cpt = tm // CHUNK
    n_chunks = seq // CHUNK

    lane_tiles = D_MODEL // LANES
    pieces = [(lo, min(lo + WEIGHT_ROWS_PIECE, GLA_IN_WIDTH)) for lo in range(0, GLA_IN_WIDTH, WEIGHT_ROWS_PIECE)]

    def body(h_ref, nw_ref, lines_hbm, gkw_ref, gkb_ref, hw_ref, wout_ref, wf_ref, tgt_ref,
             dh2_ref, proj_ref, o_ref, st_ref, scores_ref, loss_ref, dwf_ref, win_hbm,
             state_ref, win_ref, piece_ref, lines_ref, piece_sem, win_sem):
        i = pl.program_id(0)
        win_copy = pltpu.make_async_copy(win_ref, win_hbm, win_sem)

        def piece_copy(k):
            lo, hi = pieces[k]
            n_lines = (hi - lo) * lane_tiles
            return pltpu.make_async_copy(lines_hbm.at[pl.ds(lo * lane_tiles, n_lines)],
                                         piece_ref.at[k % 2, pl.ds(0, n_lines)], piece_sem.at[k % 2])

        @pl.when(i == 0)
        def _():
            state_ref[...] = jnp.zeros_like(state_ref)
            loss_ref[...] = jnp.zeros_like(loss_ref)
            dwf_ref[...] = jnp.zeros_like(dwf_ref)
            piece_copy(0).start()
            for k, (lo, hi) in enumerate(pieces):
                if k + 1 < len(pieces):
                    piece_copy(k + 1).start()
                piece_copy(k).wait()
                n_lines = (hi - lo) * lane_tiles
                lines_ref[0:n_lines, :] = piece_ref[k % 2, 0:n_lines, :].astype(F32)
                for j in range(lane_tiles):
                    win_ref[lo:hi, j * LANES:(j + 1) * LANES] = lines_ref[pl.ds(j, hi - lo, stride=lane_tiles),
                                                                          :].astype(BF16)
            win_copy.start()

        pl.when(i == nt - 1)(win_copy.wait)

        ht = h_ref[...]
        xhat, _ = _rms(ht)
        n = (xhat * nw_ref[...]).astype(BF16)
        sections = {}
        for name, lo, hi in (("low", GLA_QKVG_WIDTH, GLA_IN_PAD), ("qk", 0, 2 * GLA_KEY_WIDTH),
                             ("v", 2 * GLA_KEY_WIDTH, GLA_QKVG_WIDTH - GLA_VALUE_WIDTH),
                             ("gate", GLA_QKVG_WIDTH - GLA_VALUE_WIDTH, GLA_QKVG_WIDTH)):
            rows = (win_ref[lo:hi, :] if hi <= GLA_IN_WIDTH
                    else _rows_then_zeros(win_ref, lo, GLA_IN_WIDTH, hi - lo))
            sections[name] = _dot_nt(n, rows)
            proj_ref[:, lo:hi] = sections[name]
        low_b = sections["low"].astype(BF16)
        masks = _chunk_masks(tm)
        on_heads = []
        for h in range(GLA_HEADS):
            kc, vc = _kcols(h), _vcols(h)
            g = _GlaTerms(kc, sections["qk"][:, kc], sections["qk"][:, GLA_KEY_WIDTH:][:, kc], sections["v"][:, vc],
                          low_b, gkw_ref, gkb_ref, masks)
            srows = slice(h * GLA_HEAD_V, (h + 1) * GLA_HEAD_V)
            scores = g.scores()
            for b in range(tm // ROW_TILE):
                part = slice(b * ROW_TILE, (b + 1) * ROW_TILE)
                scores_ref[part, h * ROW_TILE:(h + 1) * ROW_TILE] = scores[part, part]
            o_intra = _dot_nn(scores, g.v_b)
            state = state_ref[srows, :]
            o_rows = []
            for j in range(cpt):
                r = _chunk_rows(j)
                st_ref[j, srows, :] = state
                o_rows.append(o_intra[r] + _dot_nt(g.a_b[r], state.astype(BF16)))
                decay = jnp.exp(g.c_last[j])
                state = state * decay + _dot_tn(g.v_b[r], g.kd_b[r])
            state_ref[srows, :] = state
            o_head = jnp.concatenate(o_rows, axis=0)
            o_ref[:, vc] = o_head
            proj_ref[:, GLA_SAVED_Z + kc.start:GLA_SAVED_Z + kc.stop] = g.z
            proj_ref[:, GLA_SAVED_C + kc.start:GLA_SAVED_C + kc.stop] = g.c
            on_heads.append(_rms(o_head)[0])
        gate = sections["gate"]
        on = jnp.concatenate(on_heads, axis=1) * hw_ref[...]
        y = (on * (gate * _sigmoid(gate))).astype(BF16)
        h2 = ht + _dot_nn(y, wout_ref[...])
        xhat2, rstd2 = _rms(h2)
        wf_row = wf_ref[...]
        err = xhat2 * wf_row - tgt_ref[...]
        loss_ref[...] += 0.5 * jnp.sum(err * err) / D_MODEL
        dout = err * (1.0 / D_MODEL)
        dwf_ref[...] += jnp.sum(dout * xhat2, axis=0, keepdims=True)
        dh2_ref[...] = _rms_bwd(dout * wf_row, xhat2, rstd2)

    row = lambda i: (i, 0)
    return pl.pallas_call(
        body, name="gla_fwd", grid=(nt,),
        in_specs=[pl.BlockSpec((tm, D_MODEL), row), _const((1, D_MODEL)), pl.BlockSpec(memory_space=pl.ANY),
                  _const((GLA_LOW_PAD, GLA_KEY_WIDTH)), _const((1, GLA_KEY_WIDTH)), _const((1, GLA_VALUE_WIDTH)),
                  _const((GLA_VALUE_WIDTH, D_MODEL)), _const((1, D_MODEL)), pl.BlockSpec((tm, D_MODEL), row)],
        out_specs=[pl.BlockSpec((tm, D_MODEL), row), pl.BlockSpec((tm, GLA_SAVED_WIDTH), row),
                   pl.BlockSpec((tm, GLA_VALUE_WIDTH), row),
                   pl.BlockSpec((cpt, GLA_VALUE_WIDTH, GLA_HEAD_K), lambda i: (i, 0, 0)),
                   pl.BlockSpec((tm, GLA_HEADS * ROW_TILE), row), _full((8, LANES)), _full((1, D_MODEL)),
                   pl.BlockSpec(memory_space=pl.ANY)],
        out_shape=[jax.ShapeDtypeStruct((seq, D_MODEL), F32), jax.ShapeDtypeStruct((seq, GLA_SAVED_WIDTH), F32),
                   jax.ShapeDtypeStruct((seq, GLA_VALUE_WIDTH), F32),
                   jax.ShapeDtypeStruct((n_chunks, GLA_VALUE_WIDTH, GLA_HEAD_K), F32),
                   jax.ShapeDtypeStruct((seq, GLA_HEADS * ROW_TILE), BF16),
                   jax.ShapeDtypeStruct((8, LANES), F32), jax.ShapeDtypeStruct((1, D_MODEL), F32),
                   jax.ShapeDtypeStruct((GLA_IN_WIDTH, D_MODEL), BF16)],
        scratch_shapes=[pltpu.VMEM((GLA_VALUE_WIDTH, GLA_HEAD_K), F32), pltpu.VMEM((GLA_IN_WIDTH, D_MODEL), BF16),
                        pltpu.VMEM((2, WEIGHT_ROWS_PIECE * lane_tiles, LANES), BF16),
                        pltpu.VMEM((WEIGHT_ROWS_PIECE * lane_tiles, LANES), F32), pltpu.SemaphoreType.DMA((2,)),
                        pltpu.SemaphoreType.DMA(())],
        compiler_params=_params(),
    )(h1, nw, w_lines, gkw, gkb, hw, w_out, wf, target)


def _gla_bwd_call(dh2, proj, o, states, scores, gkw, gkb, hw, w_out):
    seq = dh2.shape[0]
    tm = ROW_TILE
    nt = seq // tm
    cpt = tm // CHUNK

    def body(dh_ref, proj_ref, o_ref, st_ref, scores_ref, gkw_ref, gkb_ref, hw_ref, wout_ref,
             dproj_ref, dwout_hbm, dhw_ref, dgkw_ref, dgkb_ref, dstate_ref, dwout_acc, dwout_stage):
        i = pl.program_id(0)

        @pl.when(i == 0)
        def _():
            dstate_ref[...] = jnp.zeros_like(dstate_ref)
            dwout_acc[...] = jnp.zeros_like(dwout_acc)
            dhw_ref[...] = jnp.zeros_like(dhw_ref)
            dgkw_ref[...] = jnp.zeros_like(dgkw_ref)
            dgkb_ref[...] = jnp.zeros_like(dgkb_ref)

        dhb = dh_ref[...].astype(BF16)
        dy = _dot_nt(dhb, wout_ref[...])
        v0, g0 = 2 * GLA_KEY_WIDTH, GLA_QKVG_WIDTH - GLA_VALUE_WIDTH
        gate = proj_ref[:, g0:GLA_QKVG_WIDTH]
        low_b = proj_ref[:, GLA_QKVG_WIDTH:GLA_IN_PAD].astype(BF16)
        o = o_ref[...]
        hw_row = hw_ref[...]
        sg = _sigmoid(gate)
        silu = gate * sg
        don = dy * silu
        on_parts, do_parts, dhw_parts = [], [], []
        for h in range(GLA_HEADS):
            vc = _vcols(h)
            xh, rs = _rms(o[:, vc])
            on_parts.append(xh * hw_row[:, vc])
            dhw_parts.append(jnp.sum(don[:, vc] * xh, axis=0, keepdims=True))
            do_parts.append(_rms_bwd(don[:, vc] * hw_row[:, vc], xh, rs).astype(BF16))
        on = jnp.concatenate(on_parts, axis=1)
        dwout_acc[...] += _dot_tn((on * silu).astype(BF16), dhb)
        dhw_ref[...] += jnp.concatenate(dhw_parts, axis=1)
        dproj_ref[:, g0:GLA_QKVG_WIDTH] = (dy * on * (sg * (1.0 + gate * (1.0 - sg)))).astype(BF16)

        last_row = lax.broadcasted_iota(jnp.int32, (CHUNK, 1), 0) == CHUNK - 1
        g = _GlaTerms(slice(0, GLA_KEY_WIDTH), proj_ref[:, :GLA_KEY_WIDTH], proj_ref[:, GLA_KEY_WIDTH:v0],
                      proj_ref[:, v0:g0], low_b, gkw_ref, gkb_ref, _chunk_masks(tm),
                      saved=(proj_ref[:, GLA_SAVED_Z:GLA_SAVED_C], proj_ref[:, GLA_SAVED_C:GLA_SAVED_WIDTH]))
        dc_h = []
        for h in range(GLA_HEADS):
            kc, vc = _kcols(h), _vcols(h)
            k_cols = slice(GLA_KEY_WIDTH + kc.start, GLA_KEY_WIDTH + kc.stop)
            v_cols = slice(v0 + vc.start, v0 + vc.stop)
            do_h = do_parts[h]
            srows = slice(h * GLA_HEAD_V, (h + 1) * GLA_HEAD_V)
            scores = scores_ref[:, h * ROW_TILE:(h + 1) * ROW_TILE]
            dscores = _dot_nt(do_h, g.v_b[:, vc])
            dfwd = jnp.where(g.lower, dscores, 0.0).astype(BF16)
            dbwd = jnp.where(g.upper, dscores, 0.0).astype(BF16)
            dv_intra = _dot_tn(scores, do_h)
            da_intra = _dot_nn(dfwd, g.b_b[:, kc])
            db = _dot_tn(dfwd, g.a_b[:, kc])
            dcn = _dot_nn(dbwd, g.dp_b[:, kc])
            ddp = _dot_tn(dbwd, g.cn_b[:, kc])
            dstate = dstate_ref[srows, :]
            da_rows, dkd_rows, dv_rows, dcl_rows = [None] * cpt, [None] * cpt, [None] * cpt, [None] * cpt
            for j in reversed(range(cpt)):
                r = _chunk_rows(j)
                state = st_ref[j, srows, :]
                dstate_b = dstate.astype(BF16)
                do_c = do_h[r]
                dv_rows[j] = dv_intra[r] + _dot_nt(g.kd_b[r, kc], dstate_b)
                da_rows[j] = da_intra[r] + _dot_nn(do_c, state.astype(BF16))
                dkd = _dot_nn(g.v_b[r, vc], dstate_b) * g.e_rest[r, kc]
                dkd_rows[j] = dkd
                decay = jnp.exp(g.c_last[j][:, kc])
                dc_last = (jnp.sum(dkd * g.k[r, kc], axis=0, keepdims=True)
                           + decay * jnp.sum(state * dstate, axis=0, keepdims=True))
                dcl_rows[j] = jnp.where(last_row, dc_last, 0.0)
                dstate = _dot_tn(do_c, g.a_b[r, kc]) + dstate * decay
            dstate_ref[srows, :] = dstate
            da = jnp.concatenate(da_rows, axis=0)
            dkd = jnp.concatenate(dkd_rows, axis=0)
            dproj_ref[:, v_cols] = jnp.concatenate(dv_rows, axis=0).astype(BF16)
            q_up, q_down = da * g.e_pos[:, kc], dcn * g.e_neg[:, kc]
            k_up, k_down = ddp * g.e_pos[:, kc], db * g.e_neg[:, kc] + dkd
            dproj_ref[:, kc] = (Q_SCALE * (q_up + q_down)).astype(BF16)
            dproj_ref[:, k_cols] = (k_up + k_down).astype(BF16)
            dc_h.append(g.q[:, kc] * (q_up - q_down) + g.k[:, kc] * (k_up - k_down)
                        + jnp.concatenate(dcl_rows, axis=0))
        dz = _chunk_scan(jnp.concatenate(dc_h, axis=1), True) * (1.0 / GATE_NORMALIZER) * (1.0 - _sigmoid(g.z))
        dzb = dz.astype(BF16)
        dgkb_ref[...] += jnp.sum(dz, axis=0, keepdims=True)
        dgkw_ref[...] += _dot_tn(low_b, dzb)
        dproj_ref[:, GLA_QKVG_WIDTH:] = _dot_nt(dzb, gkw_ref[...]).astype(BF16)

        @pl.when(i == nt - 1)
        def _():
            dwout_stage[...] = dwout_acc[...].astype(BF16)
            pltpu.sync_copy(dwout_stage, dwout_hbm)

    rev = lambda i: (nt - 1 - i, 0)
    return pl.pallas_call(
        body, name="gla_bwd", grid=(nt,),
        in_specs=[pl.BlockSpec((tm, D_MODEL), rev), pl.BlockSpec((tm, GLA_SAVED_WIDTH), rev),
                  pl.BlockSpec((tm, GLA_VALUE_WIDTH), rev),
                  pl.BlockSpec((cpt, GLA_VALUE_WIDTH, GLA_HEAD_K), lambda i: (nt - 1 - i, 0, 0)),
                  pl.BlockSpec((tm, GLA_HEADS * ROW_TILE), rev),
                  _const((GLA_LOW_PAD, GLA_KEY_WIDTH)), _const((1, GLA_KEY_WIDTH)), _const((1, GLA_VALUE_WIDTH)),
                  _const((GLA_VALUE_WIDTH, D_MODEL))],
        out_specs=[pl.BlockSpec((tm, GLA_IN_PAD), rev), pl.BlockSpec(memory_space=pl.ANY),
                   _full((1, GLA_VALUE_WIDTH)), _full((GLA_LOW_PAD, GLA_KEY_WIDTH)), _full((1, GLA_KEY_WIDTH))],
        out_shape=[jax.ShapeDtypeStruct((seq, GLA_IN_PAD), BF16), jax.ShapeDtypeStruct((GLA_VALUE_WIDTH, D_MODEL), BF16),
                   jax.ShapeDtypeStruct((1, GLA_VALUE_WIDTH), F32), jax.ShapeDtypeStruct((GLA_LOW_PAD, GLA_KEY_WIDTH), F32),
                   jax.ShapeDtypeStruct((1, GLA_KEY_WIDTH), F32)],
        scratch_shapes=[pltpu.VMEM((GLA_VALUE_WIDTH, GLA_HEAD_K), F32), pltpu.VMEM((GLA_VALUE_WIDTH, D_MODEL), F32),
                        pltpu.VMEM((GLA_VALUE_WIDTH, D_MODEL), BF16)],
        compiler_params=_params(),
    )(dh2, proj, o, states, scores, gkw, gkb, hw, w_out)


def _position():
    return lax.axis_index("x"), lax.axis_index("y"), lax.axis_index("c")


def _lead_slot(ref, d):
    return ref.at[d]


def _row_slot(rows):
    return lambda ref, d: ref.at[pl.ds(pl.multiple_of(d * rows, rows), rows)]


def _dim1_slot(size):
    return lambda ref, d: ref.at[:, pl.ds(pl.multiple_of(d * size, size), size)]


class _Gather:
    def __init__(self, in_refs, out_refs, slots, send_sems, recv_sems, local_sems):
        self.in_refs, self.out_refs, self.slots = in_refs, out_refs, slots
        self.send_sems, self.recv_sems, self.local_sems = send_sems, recv_sems, local_sems
        self.n = len(in_refs)
        x, y, c = _position()
        self.c = c
        self.me, self.sibling = (x, y, c), (x, y, 1 - c)
        self.near = [(1 - x, y), (x, 1 - y)]
        self.diagonal = (1 - x, 1 - y)
        self.relay_from = (x ^ c, y ^ (1 - c))
        self.relay_to = (x ^ (1 - c), y ^ c)

    def _copy(self, a, k, block, to, from_input=False):
        part = self.slots[a](self.out_refs[a], 4 * block[0] + 2 * block[1] + block[2])
        return pltpu.make_async_remote_copy(
            src_ref=self.in_refs[a] if from_input else part, dst_ref=part,
            send_sem=self.send_sems.at[a, k], recv_sem=self.recv_sems.at[a, k], device_id=to, device_id_type=MESH)

    def _mine(self):
        return [pltpu.make_async_copy(self.in_refs[a], self.slots[a](self.out_refs[a], 4 * self.me[0] + 2 * self.me[1]
                                                                    + self.me[2]), self.local_sems.at[a])
                for a in range(self.n)]

    def _first(self):
        first = [self._copy(a, 0, self.me, self.sibling, True) for a in range(self.n)]
        return first + [self._copy(a, 1 + j, self.me, (*chip, self.c), True)
                        for j, chip in enumerate(self.near) for a in range(self.n)]

    def _relayed(self):
        return [self._copy(a, 3, (*self.relay_from, self.c), (*self.relay_to, self.c)) for a in range(self.n)]

    def _passed(self, j):
        chip = self.near[j] if j < 2 else self.diagonal
        return [self._copy(a, 4 + j, (*chip, self.c), self.sibling) for a in range(self.n)]

    def start(self):
        for cp in self._mine() + self._first():
            cp.start()

    def forward(self):
        for j, chip in enumerate(self.near):
            for a in range(self.n):
                self._copy(a, 1 + j, (*chip, self.c), self.me).wait_recv()
        for cp in self._relayed() + self._passed(0) + self._passed(1):
            cp.start()

    def relay(self):
        pass

    def finish(self):
        for a in range(self.n):
            self._copy(a, 3, (*self.diagonal, self.c), self.me).wait_recv()
        for cp in self._passed(2):
            cp.start()
        for a in range(self.n):
            self._copy(a, 0, self.sibling, self.me).wait_recv()
        for j, chip in enumerate(self.near + [self.diagonal]):
            for a in range(self.n):
                self._copy(a, 4 + j, (*chip, 1 - self.c), self.me).wait_recv()
        for cp in self._first() + self._relayed() + self._passed(0) + self._passed(1) + self._passed(2):
            cp.wait_send()
        for cp in self._mine():
            cp.wait()


class _Exchange:
    def __init__(self, in_refs, out_refs, slots, send_sems, recv_sems, local_sems):
        self.in_refs, self.out_refs, self.slots = in_refs, out_refs, slots
        self.send_sems, self.recv_sems, self.local_sems = send_sems, recv_sems, local_sems
        self.n = len(in_refs)
        self.pos = _position()

    def _copies(self):
        x, y, c = self.pos
        me = 4 * x + 2 * y + c
        mine = [pltpu.make_async_copy(self.slots[a](self.in_refs[a], me), self.out_refs[a].at[me],
                                      self.local_sems.at[a]) for a in range(self.n)]
        remote = []
        for k in range(1, N_DEV):
            px, py, pc = x ^ (k >> 2), y ^ ((k >> 1) & 1), c ^ (k & 1)
            for a in range(self.n):
                remote.append(pltpu.make_async_remote_copy(
                    src_ref=self.slots[a](self.in_refs[a], 4 * px + 2 * py + pc), dst_ref=self.out_refs[a].at[me],
                    send_sem=self.send_sems.at[a, k - 1], recv_sem=self.recv_sems.at[a, k - 1],
                    device_id=(px, py, pc), device_id_type=MESH))
        return mine, remote

    def start(self):
        mine, remote = self._copies()
        for cp in mine + remote:
            cp.start()

    def forward(self):
        pass

    def relay(self):
        pass

    def finish(self):
        mine, remote = self._copies()
        for cp in remote:
            cp.wait_recv()
        for cp in remote:
            cp.wait_send()
        for cp in mine:
            cp.wait()


class _Rider:
    def __init__(self, kind, arrays, out_shapes, slots, scratch=None, forward_step=None):
        self.kind, self.arrays, self.slots = kind, list(arrays), slots
        self.n = len(self.arrays)
        hbm = pl.BlockSpec(memory_space=pl.ANY)
        self.in_specs = [hbm] * self.n
        self.out_specs = [hbm] * self.n
        self.out_shape = [jax.ShapeDtypeStruct(tuple(s), a.dtype) for s, a in zip(out_shapes, self.arrays)]
        self.scratch = scratch if scratch is not None else [
            pltpu.SemaphoreType.DMA((self.n, 7)), pltpu.SemaphoreType.DMA((self.n, 7)),
            pltpu.SemaphoreType.DMA((self.n,))]
        self.forward_step = forward_step
        self.relay_step = None

    def bind(self, in_refs, out_refs, scratch):
        return self.kind(in_refs, out_refs, self.slots, *scratch)


def _gather_rider(shards, full_shapes, slots, forward_step=None):
    return _Rider(_Gather, shards, full_shapes, slots, None, forward_step)


def _exchange_rider(sends, part_shapes, slots):
    return _Rider(_Exchange, sends, [(N_DEV,) + tuple(s) for s in part_shapes], slots)


def _split_refs(refs, n_in, n_out, n_scratch, rider):
    k = rider.n if rider is not None else 0
    ins, r_ins = refs[:n_in], refs[n_in:n_in + k]
    outs, r_outs = refs[n_in + k:n_in + k + n_out], refs[n_in + k + n_out:n_in + 2 * k + n_out]
    rest = refs[n_in + 2 * k + n_out:]
    scratch, r_scratch = rest[:n_scratch], rest[n_scratch:]
    comm = rider.bind(r_ins, r_outs, r_scratch) if rider is not None else None
    if comm is not None:
        comm.forward_step, comm.relay_step = rider.forward_step, rider.relay_step
    return ins + outs + scratch, comm


def _ride_before(comm, i, nt):
    if comm is not None:
        pl.when(i == 0)(comm.start)
        pl.when(i == (nt - 1 if comm.forward_step is None else min(comm.forward_step, nt - 1)))(comm.forward)
        pl.when(i == (nt - 1 if comm.relay_step is None else min(comm.relay_step, nt - 1)))(comm.relay)


def _ride_after(comm, i, nt):
    if comm is not None:
        pl.when(i == nt - 1)(comm.finish)


def _extend(specs, rider, field):
    return list(specs) + (getattr(rider, field) if rider is not None else [])


def _comm_call(name, rider, cast_from):
    order = sorted(cast_from)

    def body(*refs):
        n = rider.n
        ins, outs, rest = list(refs[:n]), refs[n:2 * n], refs[2 * n:]
        for stage, index in zip(rest[:len(order)], order):
            stage[...] = ins[index][...].astype(BF16)
            ins[index] = stage
        comm = rider.bind(ins, outs, rest[len(order):])
        comm.start()
        comm.forward()
        comm.relay()
        comm.finish()

    vmem = pl.BlockSpec(memory_space=pltpu.VMEM)
    return pl.pallas_call(
        body, name=name, in_specs=[vmem if i in cast_from else spec for i, spec in enumerate(rider.in_specs)],
        out_specs=rider.out_specs, out_shape=rider.out_shape,
        scratch_shapes=[pltpu.VMEM(cast_from[i].shape, BF16) for i in order] + rider.scratch,
        compiler_params=pltpu.CompilerParams(vmem_limit_bytes=VMEM_LIMIT),
    )(*[cast_from.get(i, a) for i, a in enumerate(rider.arrays)])


N_CHIPS = 4


class _TwoLevel:
    def __init__(self, in_refs, out_refs, slots, *scratch):
        self.in_refs, self.out_refs, self.slots = in_refs, out_refs, slots
        self.n = n = len(in_refs)
        self.own_bufs, self.recv_bufs, self.relay_bufs = scratch[:n], scratch[n:2 * n], scratch[2 * n:3 * n]
        self.swap_send, self.swap_recv, self.local_sems, self.chip_send, self.chip_recv = scratch[3 * n:]
        x, y, c = self.pos = _position()
        self.first = (x ^ (1 - c), y ^ c)
        self.second = (x ^ c, y ^ (1 - c))
        self.chip_index = lambda chip: 2 * chip[0] + chip[1]

    def _swap(self):
        x, y, c = self.pos
        return [pltpu.make_async_remote_copy(
            src_ref=self.slots[a](self.in_refs[a], 2 * q + 1 - c), dst_ref=self.recv_bufs[a].at[q],
            send_sem=self.swap_send.at[a, q], recv_sem=self.swap_recv.at[a, q],
            device_id=(x, y, 1 - c), device_id_type=MESH) for a in range(self.n) for q in range(N_CHIPS)]

    def _mine(self):
        c = self.pos[2]
        return [pltpu.make_async_copy(self.slots[a](self.in_refs[a], 2 * q + c), self.own_bufs[a].at[q],
                                      self.local_sems.at[a, q]) for a in range(self.n) for q in range(N_CHIPS)]

    def _to_chip(self, a, k, src, dst, chip):
        return pltpu.make_async_remote_copy(
            src_ref=src, dst_ref=dst, send_sem=self.chip_send.at[a, k], recv_sem=self.chip_recv.at[a, k],
            device_id=(*chip, self.pos[2]), device_id_type=MESH)

    def _first_wave(self):
        x, y, _ = self.pos
        diagonal = self.chip_index((1 - x, 1 - y))
        passed_on = [self._to_chip(a, 1, self.own_bufs[a].at[diagonal], self.relay_bufs[a], self.first)
                     for a in range(self.n)]
        return passed_on + [self._to_chip(a, 0, self.own_bufs[a].at[self.chip_index(self.first)],
                                          self.out_refs[a].at[1], self.first) for a in range(self.n)]

    def _second_wave(self):
        return [self._to_chip(a, 2, self.own_bufs[a].at[self.chip_index(self.second)], self.out_refs[a].at[2],
                              self.second) for a in range(self.n)]

    def _own(self):
        x, y, _ = self.pos
        return [pltpu.make_async_copy(self.own_bufs[a].at[2 * x + y], self.out_refs[a].at[0],
                                      self.local_sems.at[a, N_CHIPS]) for a in range(self.n)]

    def start(self):
        for cp in self._swap() + self._mine():
            cp.start()

    def forward(self):
        swap, mine = self._swap(), self._mine()
        for a in range(self.n):
            for q in range(N_CHIPS):
                mine[a * N_CHIPS + q].wait()
                swap[a * N_CHIPS + q].wait_recv()
                self.own_bufs[a][q] = (self.own_bufs[a][q].astype(F32)
                                       + self.recv_bufs[a][q].astype(F32)).astype(BF16)
        for cp in self._first_wave() + self._own():
            cp.start()

    def relay(self):
        second = self.chip_index(self.second)
        for a in range(self.n):
            self._to_chip(a, 1, self.relay_bufs[a], self.relay_bufs[a], self.first).wait_recv()
            self.own_bufs[a][second] = (self.own_bufs[a][second].astype(F32)
                                        + self.relay_bufs[a][...].astype(F32)).astype(BF16)
        for cp in self._second_wave():
            cp.start()

    def finish(self):
        for a in range(self.n):
            self._to_chip(a, 0, self.out_refs[a].at[1], self.out_refs[a].at[1], self.first).wait_recv()
            self._to_chip(a, 2, self.out_refs[a].at[2], self.out_refs[a].at[2], self.second).wait_recv()
        for cp in self._first_wave() + self._second_wave() + self._swap():
            cp.wait_send()
        for cp in self._own():
            cp.wait()


def _two_level_rider(sends, part_shapes, slots, forward_step=None, relay_step=None):
    n = len(sends)
    bufs = [pltpu.VMEM((N_CHIPS,) + tuple(s), a.dtype) for s, a in zip(part_shapes, sends)]
    relay_bufs = [pltpu.VMEM(tuple(s), a.dtype) for s, a in zip(part_shapes, sends)]
    scratch = bufs + bufs + relay_bufs + [
        pltpu.SemaphoreType.DMA((n, N_CHIPS)), pltpu.SemaphoreType.DMA((n, N_CHIPS)),
        pltpu.SemaphoreType.DMA((n, N_CHIPS + 1)), pltpu.SemaphoreType.DMA((n, 3)), pltpu.SemaphoreType.DMA((n, 3))]
    rider = _Rider(_TwoLevel, sends, [(3,) + tuple(s) for s in part_shapes], slots, scratch, forward_step)
    rider.relay_step = relay_step
    return rider


class _Joined:
    def __init__(self, first, second):
        self.first, self.second = first, second

    def start(self):
        self.first.start()
        self.second.start()

    def forward(self):
        self.first.forward()
        self.second.forward()

    def relay(self):
        self.first.relay()
        self.second.relay()

    def finish(self):
        self.first.finish()
        self.second.finish()


class _JoinedRider:
    def __init__(self, first, second):
        self.first, self.second = first, second
        self.n = first.n + second.n
        self.arrays = first.arrays + second.arrays
        self.in_specs = first.in_specs + second.in_specs
        self.out_specs = first.out_specs + second.out_specs
        self.out_shape = first.out_shape + second.out_shape
        self.scratch = first.scratch + second.scratch
        self.forward_step = first.forward_step
        self.relay_step = first.relay_step

    def bind(self, in_refs, out_refs, scratch):
        k, s = self.first.n, len(self.first.scratch)
        return _Joined(self.first.bind(in_refs[:k], out_refs[:k], scratch[:s]),
                       self.second.bind(in_refs[k:], out_refs[k:], scratch[s:]))


def _adamw(w, g, m, v):
    m = ADAM_B1 * m + (1.0 - ADAM_B1) * g
    v = ADAM_B2 * v + (1.0 - ADAM_B2) * (g * g)
    m_hat = m / (1.0 - ADAM_B1 ** ADAM_STEP)
    v_hat = v / (1.0 - ADAM_B2 ** ADAM_STEP)
    delta = -ADAM_LR * (m_hat / (jnp.sqrt(v_hat) + ADAM_EPS) + ADAM_WD * w)
    return delta, m, v


def _sum_parts(parts_ref, index=()):
    g = parts_ref[(0,) + index].astype(F32)
    for s in range(1, parts_ref.shape[0]):
        g = g + parts_ref[(s,) + index].astype(F32)
    return g


def _adamw_group_call(name, groups):
    k = len(groups)

    def body(*refs):
        ins, outs = refs[:4 * k], refs[4 * k:]
        for i in range(k):
            parts_ref, w_ref, m_ref, v_ref = ins[4 * i:4 * i + 4]
            g = _sum_parts(parts_ref)
            delta, m_new, v_new = _adamw(w_ref[...], g, m_ref[...], v_ref[...])
            for out_ref, value in zip(outs[4 * i:4 * i + 4], (g, delta, m_new, v_new)):
                out_ref[...] = value

    vmem = pl.BlockSpec(memory_space=pltpu.VMEM)
    res = pl.pallas_call(
        body, name=name, in_specs=[vmem] * (4 * k), out_specs=[vmem] * (4 * k),
        out_shape=[jax.ShapeDtypeStruct(grp[1].shape, F32) for grp in groups for _ in range(4)],
        compiler_params=pltpu.CompilerParams(vmem_limit_bytes=VMEM_LIMIT),
    )(*[a for grp in groups for a in grp])
    return [res[4 * i:4 * i + 4] for i in range(k)]


def _adamw_slabs_call(name, parts, w, m, v, rider=None):
    def main(parts_ref, w_ref, m_ref, v_ref, g_ref, delta_ref, m_out, v_out):
        g = _sum_parts(parts_ref)
        delta, m_new, v_new = _adamw(w_ref[...], g, m_ref[...], v_ref[...])
        g_ref[...] = g
        delta_ref[...] = delta
        m_out[...] = m_new
        v_out[...] = v_new

    def body(*refs):
        own, comm = _split_refs(refs, 4, 4, 0, rider)
        if comm is not None:
            comm.start()
        main(*own)
        if comm is not None:
            comm.forward()
            comm.relay()
            comm.finish()

    vmem = pl.BlockSpec(memory_space=pltpu.VMEM)
    return pl.pallas_call(
        body, name=name, in_specs=_extend([vmem] * 4, rider, "in_specs"),
        out_specs=_extend([vmem] * 4, rider, "out_specs"),
        out_shape=_extend([jax.ShapeDtypeStruct(w.shape, F32)] * 4, rider, "out_shape"),
        scratch_shapes=_extend([], rider, "scratch"),
        compiler_params=pltpu.CompilerParams(vmem_limit_bytes=VMEM_LIMIT),
    )(parts, w, m, v, *_extend([], rider, "arrays"))


WIDE_ROWS = 8
NARROW_ROWS = 40
NARROW_GKW_ROW = 8
NARROW_GKB_ROW = 24
NARROW_HW_ROW = 32
GROUP_SHARD = POOL_GROUP_DIM // N_DEV
KEY_SHARD = GLA_KEY_WIDTH // N_DEV
HEAD_V_SHARD = GLA_HEAD_V // N_DEV


def _small_adamw_call(wide, narrow, w, m, v):
    names = ("norm_w", "pool_scale", "final_norm_w", "pool_group_b", "gla_gk_w", "gla_gk_b", "gla_head_norm_w")
    where = {
        "norm_w": (0, slice(0, 2), slice(None)),
        "pool_scale": (0, slice(2, 3), slice(None)),
        "final_norm_w": (0, slice(3, 4), slice(None)),
        "pool_group_b": (1, slice(0, POOL_GROUPS), slice(0, GROUP_SHARD)),
        "gla_gk_w": (1, slice(NARROW_GKW_ROW, NARROW_GKW_ROW + GLA_GATE_RANK), slice(0, KEY_SHARD)),
        "gla_gk_b": (1, slice(NARROW_GKB_ROW, NARROW_GKB_ROW + 1), slice(0, KEY_SHARD)),
        "gla_head_norm_w": (1, slice(NARROW_HW_ROW, NARROW_HW_ROW + 1), slice(0, HEAD_V_SHARD)),
    }
    k = len(names)

    def body(*refs):
        parts = refs[0:2]
        w_refs, m_refs, v_refs = refs[2:2 + k], refs[2 + k:2 + 2 * k], refs[2 + 2 * k:2 + 3 * k]
        outs = refs[2 + 3 * k:]
        loss_ref = outs[0]
        loss_ref[...] = _sum_parts(parts[0], (slice(4, 5), slice(0, 1)))
        for i, name in enumerate(names):
            buf, rows, cols = where[name]
            g = _sum_parts(parts[buf], (rows, cols))
            delta, m_new, v_new = _adamw(w_refs[i][...], g, m_refs[i][...], v_refs[i][...])
            outs[1 + i][...] = g
            outs[1 + k + i][...] = delta
            outs[1 + 2 * k + i][...] = m_new
            outs[1 + 3 * k + i][...] = v_new

    vmem = pl.BlockSpec(memory_space=pltpu.VMEM)
    shapes = [jax.ShapeDtypeStruct(w[n].shape, F32) for n in names]
    res = pl.pallas_call(
        body, name="adamw_small", in_specs=[vmem] * (2 + 3 * k), out_specs=[vmem] * (1 + 4 * k),
        out_shape=[jax.ShapeDtypeStruct((1, 1), F32)] + shapes * 4,
    )(wide, narrow, *[w[n] for n in names], *[m[n] for n in names], *[v[n] for n in names])
    unzip = lambda j: dict(zip(names, res[1 + j * k:1 + (j + 1) * k]))
    return res[0], unzip(0), unzip(1), unzip(2), unzip(3)


def kernel(x, norm_w, pool_in_w, pool_group_w, pool_group_b, pool_scale, pool_out_w, gla_in_w, gla_gk_w, gla_gk_b, gla_head_norm_w, gla_out_w, final_norm_w, loss_target, m_norm_w, m_pool_in_w, m_pool_group_w, m_pool_group_b, m_pool_scale, m_pool_out_w, m_gla_in_w, m_gla_gk_w, m_gla_gk_b, m_gla_head_norm_w, m_gla_out_w, m_final_norm_w, v_norm_w, v_pool_in_w, v_pool_group_w, v_pool_group_b, v_pool_scale, v_pool_out_w, v_gla_in_w, v_gla_gk_w, v_gla_gk_b, v_gla_head_norm_w, v_gla_out_w, v_final_norm_w):
    w = dict(norm_w=norm_w, pool_in_w=pool_in_w, pool_group_w=pool_group_w, pool_group_b=pool_group_b,
             pool_scale=pool_scale, pool_out_w=pool_out_w, gla_in_w=gla_in_w, gla_gk_w=gla_gk_w, gla_gk_b=gla_gk_b,
             gla_head_norm_w=gla_head_norm_w, gla_out_w=gla_out_w, final_norm_w=final_norm_w)
    m = dict(norm_w=m_norm_w, pool_in_w=m_pool_in_w, pool_group_w=m_pool_group_w, pool_group_b=m_pool_group_b,
             pool_scale=m_pool_scale, pool_out_w=m_pool_out_w, gla_in_w=m_gla_in_w, gla_gk_w=m_gla_gk_w,
             gla_gk_b=m_gla_gk_b, gla_head_norm_w=m_gla_head_norm_w, gla_out_w=m_gla_out_w,
             final_norm_w=m_final_norm_w)
    v = dict(norm_w=v_norm_w, pool_in_w=v_pool_in_w, pool_group_w=v_pool_group_w, pool_group_b=v_pool_group_b,
             pool_scale=v_pool_scale, pool_out_w=v_pool_out_w, gla_in_w=v_gla_in_w, gla_gk_w=v_gla_gk_w,
             gla_gk_b=v_gla_gk_b, gla_head_norm_w=v_gla_head_norm_w, gla_out_w=v_gla_out_w,
             final_norm_w=v_final_norm_w)
    col_shard = GLA_IN_WIDTH // N_DEV
    row_shard = D_MODEL // N_DEV

    def lanes(a):
        return jnp.pad(a, [(0, 0)] * (a.ndim - 1) + [(0, LANES - a.shape[-1])])

    small_in = jnp.concatenate([lanes(pool_group_b[0]), lanes(gla_gk_b), lanes(gla_head_norm_w),
                                jnp.zeros((2, LANES), F32)], axis=0)
    in_cols = 2 * POOL_WIDTH // N_DEV
    pool_f32 = [pool_in_w[0], pool_group_w[0], pool_out_w[0]]
    pool_in, pool_gw, pool_out, small_all = _comm_call("pool_weights_all_gather", _gather_rider(
        [jax.ShapeDtypeStruct(a.shape, BF16) for a in pool_f32] + [small_in],
        [(D_MODEL, 2 * POOL_WIDTH), (POOL_GROUPS, POOL_GROUP_DIM, POOL_GROUP_DIM), (POOL_WIDTH, D_MODEL),
         (N_DEV, 8, LANES)],
        [_dim1_slot(in_cols), _dim1_slot(GROUP_SHARD), _row_slot(row_shard), _lead_slot]), dict(enumerate(pool_f32)))
    pool_gb = jnp.transpose(small_all[:, 0:POOL_GROUPS, :GROUP_SHARD], (1, 0, 2)).reshape(1, POOL_WIDTH)
    gla_gkb = small_all[:, POOL_GROUPS, :KEY_SHARD].reshape(1, GLA_KEY_WIDTH)
    gla_hw = jnp.tile(small_all[:, POOL_GROUPS + 1, :HEAD_V_SHARD].reshape(1, GLA_HEAD_V), (1, GLA_HEADS))
    nw0, nw1, wf = norm_w[0:1], norm_w[1:2], final_norm_w.reshape(1, D_MODEL)
    xs, target = x[0], loss_target[0]

    slabs = col_shard * D_MODEL // (BF16_ROWS * LANES)
    as_slabs = lambda t: jnp.transpose(t[0]).reshape(slabs, BF16_ROWS, LANES)
    h1, pool_y, pool_silu, pool_dsilu, pooled, mixed, gla_in_parts, gkw_parts, gla_out = _pool_fwd_call(
        xs, nw0, pool_in, pool_gw, pool_gb, pool_scale, pool_out, _gather_rider(
            [as_slabs(gla_in_w).astype(BF16), gla_gk_w[0].astype(BF16), gla_out_w[0].astype(BF16)],
            [(N_DEV, slabs, BF16_ROWS, LANES), (N_DEV, GLA_GATE_RANK, KEY_SHARD), (GLA_VALUE_WIDTH, D_MODEL)],
            [_lead_slot, _lead_slot, _row_slot(row_shard)], GATHER_RELAY_STEP))
    gla_gkw = jnp.pad(jnp.transpose(gkw_parts, (1, 0, 2)).reshape(GLA_GATE_RANK, GLA_KEY_WIDTH),
                      ((0, GLA_LOW_PAD - GLA_GATE_RANK), (0, 0)))
    dh2, proj, o, states, scores, loss_part, dwf, gla_in = _gla_fwd_call(
        h1, nw1, gla_in_parts.reshape(GLA_IN_WIDTH * D_MODEL // LANES, LANES), gla_gkw, gla_gkb, gla_hw, gla_out,
        wf, target)

    dproj, d_gla_out, dhw, dgkw, dgkb = _gla_bwd_call(dh2, proj, o, states, scores, gla_gkw, gla_gkb, gla_hw,
                                                      gla_out)
    dh1, d_gla_in, dnw1, landed_gla_out = _inproj_bwd_call(
        "gla_in_bwd", dproj, h1, nw1, gla_in, dh2,
        _exchange_rider([d_gla_out], [(row_shard, D_MODEL)], [_row_slot(row_shard)]), transposed=True)
    gla_in_send = d_gla_in.reshape(N_DEV, slabs, BF16_ROWS, LANES)
    dp, d_pool_out, dgw, dgb, dsc, landed_gla_in = _pool_bwd_call(
        dh1, pool_y, pool_silu, pool_dsilu, pooled, mixed, pool_gw, pool_scale, pool_out,
        _two_level_rider([gla_in_send], [(slabs, BF16_ROWS, LANES)], [_lead_slot], TWO_LEVEL_ADD_STEP,
                         TWO_LEVEL_RELAY_STEP))
    grad_x, d_pool_in, dnw0 = _inproj_bwd_call("pool_in_bwd", dp, xs, nw0, pool_in, dh1)

    wide = jnp.concatenate([
        dnw0, dnw1, dsc, dwf, jnp.pad(loss_part[0:1, 0:1], ((0, 0), (0, D_MODEL - 1))),
        jnp.zeros((WIDE_ROWS - 5, D_MODEL), F32)], axis=0)

    def rows8(a):
        return jnp.pad(lanes(a), ((0, 0), (0, -a.shape[1] % 8), (0, 0)))

    narrow = jnp.concatenate([
        rows8(jnp.transpose(dgb.reshape(POOL_GROUPS, N_DEV, GROUP_SHARD), (1, 0, 2))),
        rows8(jnp.transpose(dgkw[:GLA_GATE_RANK].reshape(GLA_GATE_RANK, N_DEV, KEY_SHARD), (1, 0, 2))),
        rows8(dgkb.reshape(N_DEV, 1, KEY_SHARD)),
        rows8(dhw.reshape(GLA_HEADS, GLA_HEAD_V).sum(axis=0).reshape(N_DEV, 1, HEAD_V_SHARD)),
    ], axis=1)
    last_exchange = _JoinedRider(
        _two_level_rider([d_pool_in, d_pool_out, dgw],
                         [(D_MODEL, in_cols), (row_shard, D_MODEL), (POOL_GROUPS, GROUP_SHARD, POOL_GROUP_DIM)],
                         [_dim1_slot(in_cols), _row_slot(row_shard), _dim1_slot(GROUP_SHARD)]),
        _exchange_rider([wide, narrow], [(WIDE_ROWS, D_MODEL), (NARROW_ROWS, LANES)],
                        [lambda ref, d: ref, _lead_slot]))

    res = {}
    *outs, landed_pool_in, landed_pool_out, landed_gw, landed_wide, landed_narrow = _adamw_slabs_call(
        "adamw_gla_in_w", landed_gla_in, as_slabs(gla_in_w), as_slabs(m_gla_in_w), as_slabs(v_gla_in_w),
        last_exchange)
    res["gla_in_w"] = [jnp.transpose(t.reshape(col_shard, D_MODEL))[None] for t in outs]
    rest = [("pool_in_w", landed_pool_in, (D_MODEL, in_cols)),
            ("pool_group_w", landed_gw, (POOL_GROUPS * GROUP_SHARD, POOL_GROUP_DIM)),
            ("pool_out_w", landed_pool_out, (row_shard, D_MODEL)), ("gla_out_w", landed_gla_out, (row_shard, D_MODEL))]
    updates = _adamw_group_call("adamw_matrices", [
        (parts.reshape((parts.shape[0],) + shape), w[name].reshape(shape), m[name].reshape(shape),
         v[name].reshape(shape)) for name, parts, shape in rest])
    for (name, _, _), outs in zip(rest, updates):
        res[name] = [t.reshape(w[name].shape) for t in outs]
    small_shapes ={"norm_w": (2, D_MODEL), "pool_scale": (1, D_MODEL), "final_norm_w": (1, D_MODEL),
                    "pool_group_b": (POOL_GROUPS, GROUP_SHARD), "gla_gk_w": (GLA_GATE_RANK, KEY_SHARD),
                    "gla_gk_b": (1, KEY_SHARD), "gla_head_norm_w": (1, HEAD_V_SHARD)}
    as_small = lambda t: {n: t[n].reshape(s) for n, s in small_shapes.items()}
    loss, *small_outs = _small_adamw_call(landed_wide, landed_narrow, as_small(w), as_small(m), as_small(v))
    for name in small_shapes:
        res[name] = [t[name].reshape(w[name].shape) for t in small_outs]
    order = ("norm_w", "pool_in_w", "pool_group_w", "pool_group_b", "pool_scale", "pool_out_w", "gla_in_w",
             "gla_gk_w", "gla_gk_b", "gla_head_norm_w", "gla_out_w", "final_norm_w")
    return (loss.reshape(()), grad_x[None], *[res[n][0] for n in order], *[res[n][1] for n in order],
            *[res[n][2] for n in order], *[res[n][3] for n in order])
```

```python
import jax
import jax.numpy as jnp
from jax import lax
from jax.experimental import pallas as pl
from jax.experimental.pallas import tpu as pltpu

F32 = jnp.float32
BF16 = jnp.bfloat16
MESH = pl.DeviceIdType.MESH

N_DEV = 8
D_MODEL = 1024
POOL_WIDTH = 1024
POOL_GROUPS = 4
POOL_GROUP_DIM = 256
POOL_HALO = 16
GLA_HEADS = 4
GLA_HEAD_K = 128
GLA_HEAD_V = 256
GLA_KEY_WIDTH = 512
GLA_VALUE_WIDTH = 1024
GLA_GATE_RANK = 16
GLA_IN_WIDTH = 3088
GLA_IN_PAD = 3200
GLA_SAVED_Z = GLA_IN_PAD
GLA_SAVED_C = GLA_SAVED_Z + 512
GLA_SAVED_WIDTH = GLA_SAVED_C + 512
GLA_LOW_PAD = 128
GLA_QKVG_WIDTH = 3072
CHUNK = 64
GATE_NORMALIZER = 16.0
RMS_EPS = 1e-6
Q_SCALE = GLA_HEAD_K ** -0.5

ADAM_LR = 0.001
ADAM_B1 = 0.9
ADAM_B2 = 0.999
ADAM_EPS = 1e-08
ADAM_WD = 0.01
ADAM_STEP = 10

LANES = 128
BF16_ROWS = 16
VMEM_LIMIT = 60 * 1024 * 1024
ROW_TILE = 256
GLA_FWD_ROW_TILE = 512
MATMUL_ROW_TILE = 512
DW_COLUMN_BLOCK = 512
ROW_MAJOR_PIECE = 776
WEIGHT_ROWS_PIECE = 208
GATHER_RELAY_STEP = 5
TWO_LEVEL_ADD_STEP = 1
TWO_LEVEL_RELAY_STEP = 4


def _dot_nn(a, b):
    return lax.dot_general(a, b, (((1,), (0,)), ((), ())), preferred_element_type=F32)


def _dot_nt(a, b):
    return lax.dot_general(a, b, (((1,), (1,)), ((), ())), preferred_element_type=F32)


def _dot_tn(a, b):
    return lax.dot_general(a, b, (((0,), (0,)), ((), ())), preferred_element_type=F32)


def _rms(x):
    rstd = lax.rsqrt(jnp.mean(x * x, axis=-1, keepdims=True) + RMS_EPS)
    return x * rstd, rstd


def _rms_bwd(dxhat, xhat, rstd):
    return rstd * (dxhat - xhat * jnp.mean(dxhat * xhat, axis=-1, keepdims=True))


def _sigmoid(x):
    return 1.0 / (1.0 + jnp.exp(-x))


def _params(sem=("arbitrary",)):
    return pltpu.CompilerParams(dimension_semantics=sem, vmem_limit_bytes=VMEM_LIMIT)


def _full(shape):
    return pl.BlockSpec(shape, lambda i: (0,) * len(shape))


def _const(shape):
    return pl.BlockSpec(shape, lambda i: (0,) * len(shape), pipeline_mode=pl.Buffered(1))


def _window_sums(ext, forward):
    n = ext.shape[0]
    outs = []
    for g in range(POOL_GROUPS):
        s = ext[:, g * POOL_GROUP_DIM:(g + 1) * POOL_GROUP_DIM]
        for k in range(g + 1):
            shift = (1 << k) if forward else n - (1 << k)
            s = s + pltpu.roll(s, shift, axis=0)
        outs.append(s[:n - POOL_HALO])
    return outs


def _inv_count(row0, tm):
    row = row0 + lax.broadcasted_iota(jnp.int32, (tm, 1), 0)
    return [1.0 / jnp.minimum(row + 1, 2 << g).astype(F32) for g in range(POOL_GROUPS)]


def _pool_mix(u, u_prev, row0, gw_ref, gb):
    tm = u.shape[0]
    sums = _window_sums(jnp.concatenate([u, u_prev], axis=0), True)
    inv = _inv_count(row0, tm)
    pooled, mixed = [], []
    for g in range(POOL_GROUPS):
        ug = u[:, g * POOL_GROUP_DIM:(g + 1) * POOL_GROUP_DIM]
        pg = (sums[g] * inv[g] - ug).astype(BF16)
        pooled.append(pg)
        mixed.append(_dot_nn(pg, gw_ref[g]))
    return pooled, jnp.concatenate(mixed, axis=1) + gb


def _pool_fwd_call(x, nw, w_in, gw, gb, sc, w_out, rider=None):
    seq = x.shape[0]
    tm = min(MATMUL_ROW_TILE, seq)
    nt = seq // tm

    def main(x_ref, nw_ref, win_ref, gw_ref, gb_ref, sc_ref, wout_ref, h_ref, y_ref, silu_ref, dsilu_ref,
             pooled_ref, mixed_ref, halo_ref):
        i = pl.program_id(0)

        @pl.when(i == 0)
        def _():
            halo_ref[...] = jnp.zeros_like(halo_ref)

        xt = x_ref[...]
        xhat, _ = _rms(xt)
        n = (xhat * nw_ref[...]).astype(BF16)
        p = _dot_nn(n, win_ref[...])
        u = p[:, :POOL_WIDTH]
        gate = p[:, POOL_WIDTH:]
        sg = _sigmoid(gate)
        silu = gate * sg
        silu_ref[...] = silu
        dsilu_ref[...] = sg * (1.0 + gate * (1.0 - sg))
        pooled, mixed = _pool_mix(u, halo_ref[...], i * tm, gw_ref, gb_ref[...])
        pooled_ref[...] = jnp.concatenate(pooled, axis=1)
        mixed_ref[...] = mixed
        halo_ref[...] = u[tm - POOL_HALO:, :]
        y = (mixed * sc_ref[...] * silu).astype(BF16)
        y_ref[...] = y
        h_ref[...] = xt + _dot_nn(y, wout_ref[...])

    def body(*refs):
        own, comm = _split_refs(refs, 7, 6, 1, rider)
        _ride_before(comm, pl.program_id(0), nt)
        main(*own)
        _ride_after(comm, pl.program_id(0), nt)

    return pl.pallas_call(
        body, name="pool_fwd", grid=(nt,),
        in_specs=_extend([pl.BlockSpec((tm, D_MODEL), lambda i: (i, 0)), _const((1, D_MODEL)),
                          _const((D_MODEL, 2 * POOL_WIDTH)), _const((POOL_GROUPS, POOL_GROUP_DIM, POOL_GROUP_DIM)),
                          _const((1, POOL_WIDTH)), _const((1, POOL_WIDTH)), _const((POOL_WIDTH, D_MODEL))],
                         rider, "in_specs"),
        out_specs=_extend([pl.BlockSpec((tm, D_MODEL), lambda i: (i, 0))] * 6, rider, "out_specs"),
        out_shape=_extend([jax.ShapeDtypeStruct((seq, D_MODEL), F32), jax.ShapeDtypeStruct((seq, POOL_WIDTH), BF16),
                           jax.ShapeDtypeStruct((seq, POOL_WIDTH), F32), jax.ShapeDtypeStruct((seq, POOL_WIDTH), F32),
                           jax.ShapeDtypeStruct((seq, POOL_WIDTH), BF16),
                           jax.ShapeDtypeStruct((seq, POOL_WIDTH), F32)], rider, "out_shape"),
        scratch_shapes=_extend([pltpu.VMEM((POOL_HALO, POOL_WIDTH), F32)], rider, "scratch"),
        compiler_params=_params(),
    )(x, nw, w_in, gw, gb, sc, w_out, *_extend([], rider, "arrays"))


def _pool_bwd_call(dh, y, silu, dsilu, pooled, mixed, gw, sc, w_out, rider=None):
    seq = dh.shape[0]
    tm = min(MATMUL_ROW_TILE, seq)
    nt = seq // tm

    def main(dh_ref, y_ref, silu_ref, dsilu_ref, pooled_ref, mixed_ref, gw_ref, sc_ref, wout_ref,
             dp_ref, dwout_hbm, dgw_hbm, dgb_ref, dsc_ref, carry_ref, dwout_acc, dgw_acc, dwout_stage, dgw_stage):
        i = pl.program_id(0)
        t = nt - 1 - i

        @pl.when(i == 0)
        def _():
            carry_ref[...] = jnp.zeros_like(carry_ref)
            dwout_acc[...] = jnp.zeros_like(dwout_acc)
            dgw_acc[...] = jnp.zeros_like(dgw_acc)
            dgb_ref[...] = jnp.zeros_like(dgb_ref)
            dsc_ref[...] = jnp.zeros_like(dsc_ref)

        silu = silu_ref[...]
        pooled = [pooled_ref[:, g * POOL_GROUP_DIM:(g + 1) * POOL_GROUP_DIM] for g in range(POOL_GROUPS)]
        sc = sc_ref[...]
        dhb = dh_ref[...].astype(BF16)
        dwout_acc[...] += _dot_tn(y_ref[...], dhb)
        dy = _dot_nt(dhb, wout_ref[...])
        dmixed = dy * sc * silu
        dy_mixed = dy * mixed_ref[...]
        dsc_ref[...] += jnp.sum(dy_mixed * silu, axis=0, keepdims=True)
        dgate = dy_mixed * sc * dsilu_ref[...]
        dgb_ref[...] += jnp.sum(dmixed, axis=0, keepdims=True)
        inv = _inv_count(t * tm, tm)
        dpooled, scaled = [], []
        for g in range(POOL_GROUPS):
            dmg = dmixed[:, g * POOL_GROUP_DIM:(g + 1) * POOL_GROUP_DIM].astype(BF16)
            dgw_acc[g] += _dot_tn(pooled[g], dmg)
            dpg = _dot_nt(dmg, gw_ref[g])
            dpooled.append(dpg)
            scaled.append(dpg * inv[g])
        r = jnp.concatenate(scaled, axis=1)
        sums = _window_sums(jnp.concatenate([r, carry_ref[...]], axis=0), False)
        carry_ref[...] = r[:POOL_HALO, :]
        du = jnp.concatenate([sums[g] - dpooled[g] for g in range(POOL_GROUPS)], axis=1)
        dp_ref[...] = jnp.concatenate([du, dgate], axis=1).astype(BF16)

        @pl.when(i == nt - 1)
        def _():
            dwout_stage[...] = dwout_acc[...].astype(BF16)
            dgw_stage[...] = dgw_acc[...].astype(BF16)
            pltpu.sync_copy(dwout_stage, dwout_hbm)
            pltpu.sync_copy(dgw_stage, dgw_hbm)

    def body(*refs):
        own, comm = _split_refs(refs, 9, 5, 5, rider)
        _ride_before(comm, pl.program_id(0), nt)
        main(*own)
        _ride_after(comm, pl.program_id(0), nt)

    rev = lambda i: (nt - 1 - i, 0)
    return pl.pallas_call(
        body, name="pool_bwd", grid=(nt,),
        in_specs=_extend([pl.BlockSpec((tm, D_MODEL), rev)] * 6
                         + [_const((POOL_GROUPS, POOL_GROUP_DIM, POOL_GROUP_DIM)), _const((1, POOL_WIDTH)),
                            _const((POOL_WIDTH, D_MODEL))], rider, "in_specs"),
        out_specs=_extend([pl.BlockSpec((tm, 2 * POOL_WIDTH), rev), pl.BlockSpec(memory_space=pl.ANY),
                           pl.BlockSpec(memory_space=pl.ANY), _full((1, POOL_WIDTH)), _full((1, POOL_WIDTH))],
                          rider, "out_specs"),
        out_shape=_extend([jax.ShapeDtypeStruct((seq, 2 * POOL_WIDTH), BF16),
                           jax.ShapeDtypeStruct((POOL_WIDTH, D_MODEL), BF16),
                           jax.ShapeDtypeStruct((POOL_GROUPS, POOL_GROUP_DIM, POOL_GROUP_DIM), BF16),
                           jax.ShapeDtypeStruct((1, POOL_WIDTH), F32), jax.ShapeDtypeStruct((1, POOL_WIDTH), F32)],
                          rider, "out_shape"),
        scratch_shapes=_extend([pltpu.VMEM((POOL_HALO, POOL_WIDTH), F32), pltpu.VMEM((POOL_WIDTH, D_MODEL), F32),
                                pltpu.VMEM((POOL_GROUPS, POOL_GROUP_DIM, POOL_GROUP_DIM), F32),
                                pltpu.VMEM((POOL_WIDTH, D_MODEL), BF16),
                                pltpu.VMEM((POOL_GROUPS, POOL_GROUP_DIM, POOL_GROUP_DIM), BF16)], rider, "scratch"),
        compiler_params=_params(),
    )(dh, y, silu, dsilu, pooled, mixed, gw, sc, w_out, *_extend([], rider, "arrays"))


def _rows_then_zeros(ref, lo, hi, rows):
    part = ref[lo:hi, :]
    return jnp.concatenate([part, jnp.zeros((rows - (hi - lo), part.shape[1]), part.dtype)], axis=0)


def _inproj_bwd_call(name, dproj, h_in, nw, w_in, dres, rider=None, transposed=False):
    seq = h_in.shape[0]
    width = dproj.shape[1]
    w_shape = tuple(w_in.shape)
    acc_shape = (width, D_MODEL) if transposed else w_shape
    whole = w_shape[0] // LANES * LANES
    tm = min(MATMUL_ROW_TILE, seq)
    nt = seq // tm
    lane_tiles = D_MODEL // LANES
    dw_shape = (w_shape[0] * lane_tiles, LANES) if transposed else w_shape
    pieces = [(lo, min(lo + ROW_MAJOR_PIECE, w_shape[0])) for lo in range(0, w_shape[0], ROW_MAJOR_PIECE)]

    def to_row_major(dw_acc, dw_stage, dw_lines):
        for lo, hi in pieces:
            for j in range(lane_tiles):
                dw_lines[pl.ds(j, hi - lo, stride=lane_tiles), :] = dw_acc[lo:hi, j * LANES:(j + 1) * LANES]
            dw_stage[lo * lane_tiles:hi * lane_tiles, :] = dw_lines[0:(hi - lo) * lane_tiles, :].astype(BF16)

    def main(dproj_ref, h_ref, nw_ref, win_ref, dres_ref, dh_ref, dw_hbm, dnw_ref, dw_acc, dw_stage, *dw_lines):
        i = pl.program_id(0)

        @pl.when(i == 0)
        def _():
            dw_acc[...] = jnp.zeros_like(dw_acc)
            dnw_ref[...] = jnp.zeros_like(dnw_ref)

        dpb = dproj_ref[...]
        if transposed:
            dn = _dot_nn(dpb[:, :whole], win_ref[0:whole, :])
            if whole < w_shape[0]:
                dn = dn + _dot_nn(dpb[:, whole:], _rows_then_zeros(win_ref, whole, w_shape[0], width - whole))
        else:
            dn = _dot_nt(dpb, win_ref[...])
        xhat, rstd = _rms(h_ref[...])
        nw_row = nw_ref[...]
        n = (xhat * nw_row).astype(BF16)
        dw_acc[...] += _dot_tn(dpb, n) if transposed else _dot_tn(n, dpb)
        dnw_ref[...] += jnp.sum(dn * xhat, axis=0, keepdims=True)
        dh_ref[...] = _rms_bwd(dn * nw_row, xhat, rstd) + dres_ref[...]

        @pl.when(i == nt - 1)
        def _():
            if transposed:
                to_row_major(dw_acc, dw_stage, *dw_lines)
            else:
                dw_stage[...] = dw_acc[...].astype(BF16)
            pltpu.sync_copy(dw_stage, dw_hbm)

    scratch = [pltpu.VMEM(acc_shape, F32), pltpu.VMEM(dw_shape, BF16)]
    if transposed:
        scratch.append(pltpu.VMEM((ROW_MAJOR_PIECE * lane_tiles, LANES), F32))

    def body(*refs):
        own, comm = _split_refs(refs, 5, 3, len(scratch), rider)
        _ride_before(comm, pl.program_id(0), nt)
        main(*own)
        _ride_after(comm, pl.program_id(0), nt)

    row = lambda i: (i, 0)
    return pl.pallas_call(
        body, name=name, grid=(nt,),
        in_specs=_extend([pl.BlockSpec((tm, width), row), pl.BlockSpec((tm, D_MODEL), row), _const((1, D_MODEL)),
                          _const(w_shape), pl.BlockSpec((tm, D_MODEL), row)], rider, "in_specs"),
        out_specs=_extend([pl.BlockSpec((tm, D_MODEL), row), pl.BlockSpec(memory_space=pl.ANY),
                           _full((1, D_MODEL))], rider, "out_specs"),
        out_shape=_extend([jax.ShapeDtypeStruct((seq, D_MODEL), F32), jax.ShapeDtypeStruct(dw_shape, BF16),
                           jax.ShapeDtypeStruct((1, D_MODEL), F32)], rider, "out_shape"),
        scratch_shapes=_extend(scratch, rider, "scratch"),
        compiler_params=_params(),
    )(dproj, h_in, nw, w_in, dres, *_extend([], rider, "arrays"))


def _inproj_bwd_full_depth_call(name, dproj, h_in, nw, w_in, dres):
    seq = h_in.shape[0]
    width = dproj.shape[1]
    tm = min(MATMUL_ROW_TILE, seq)
    nt = seq // tm
    nb = width // DW_COLUMN_BLOCK

    def body(dproj_ref, h_ref, nw_ref, win_ref, dres_ref, dcols_ref, dh_ref, dw_ref, dnw_ref, n_t):
        i = pl.program_id(0)

        @pl.when(i == 0)
        def _():
            dnw_ref[...] = jnp.zeros_like(dnw_ref)

        @pl.when(i < nt)
        def _():
            dn = _dot_nt(dproj_ref[...], win_ref[...])
            xhat, rstd = _rms(h_ref[...])
            nw_row = nw_ref[...]
            n_t[:, pl.ds(pl.multiple_of(i * tm, tm), tm)] = jnp.transpose(xhat * nw_row).astype(BF16)
            dnw_ref[...] += jnp.sum(dn * xhat, axis=0, keepdims=True)
            dh_ref[...] = _rms_bwd(dn * nw_row, xhat, rstd) + dres_ref[...]

        @pl.when(i >= nt)
        def _():
            dw_ref[...] = _dot_nn(n_t[...], dcols_ref[...]).astype(BF16)

    rows = lambda i: (jnp.minimum(i, nt - 1), 0)
    cols = lambda i: (0, jnp.maximum(i - nt, 0))
    return pl.pallas_call(
        body, name=name, grid=(nt + nb,),
        in_specs=[pl.BlockSpec((tm, width), rows), pl.BlockSpec((tm, D_MODEL), rows), _const((1, D_MODEL)),
                  _const(tuple(w_in.shape)), pl.BlockSpec((tm, D_MODEL), rows),
                  pl.BlockSpec((seq, DW_COLUMN_BLOCK), cols)],
        out_specs=[pl.BlockSpec((tm, D_MODEL), rows), pl.BlockSpec((D_MODEL, DW_COLUMN_BLOCK), cols),
                   _full((1, D_MODEL))],
        out_shape=[jax.ShapeDtypeStruct((seq, D_MODEL), F32), jax.ShapeDtypeStruct(tuple(w_in.shape), BF16),
                   jax.ShapeDtypeStruct((1, D_MODEL), F32)],
        scratch_shapes=[pltpu.VMEM((D_MODEL, seq), BF16)],
        compiler_params=_params(),
    )(dproj, h_in, nw, w_in, dres, dproj)


def _chunk_scan(x, reverse):
    n = x.shape[0]
    pos = lax.broadcasted_iota(jnp.int32, (n, 1), 0) & (CHUNK - 1)
    k = 1
    while k < CHUNK:
        if reverse:
            x = x + jnp.where(pos < CHUNK - k, pltpu.roll(x, n - k, axis=0), 0.0)
        else:
            x = x + jnp.where(pos >= k, pltpu.roll(x, k, axis=0), 0.0)
        k *= 2
    return x


def _chunk_rows(j):
    return slice(j * CHUNK, (j + 1) * CHUNK)


def _kcols(h):
    return slice(h * GLA_HEAD_K, (h + 1) * GLA_HEAD_K)


def _vcols(h):
    return slice(h * GLA_HEAD_V, (h + 1) * GLA_HEAD_V)


def _chunk_masks(tm):
    idx_t = lax.broadcasted_iota(jnp.int32, (tm, tm), 0)
    idx_s = lax.broadcasted_iota(jnp.int32, (tm, tm), 1)
    same_chunk = (idx_t ^ idx_s) < CHUNK
    return same_chunk & (idx_t >= idx_s), same_chunk & (idx_t < idx_s)


class _GlaTerms:
    def __init__(self, kc, q, k, v, low_b, gkw_ref, gkb_ref, masks, saved=None):
        tm = q.shape[0]
        self.q = q * Q_SCALE
        self.k = k
        if saved is None:
            self.z = _dot_nn(low_b, gkw_ref[:, kc]) + gkb_ref[:, kc]
            log_g = (jnp.minimum(self.z, 0.0) - jnp.log(1.0 + jnp.exp(-jnp.abs(self.z)))) / GATE_NORMALIZER
            self.c = _chunk_scan(log_g, False)
        else:
            self.z, self.c = saved
        is_last = lax.broadcasted_iota(jnp.int32, (CHUNK, 1), 0) == CHUNK - 1
        self.c_last = [jnp.sum(jnp.where(is_last, self.c[_chunk_rows(j), :], 0.0), axis=0, keepdims=True)
                       for j in range(tm // CHUNK)]
        c_last_rows = jnp.concatenate([jnp.broadcast_to(r, (CHUNK, r.shape[1])) for r in self.c_last], axis=0)
        self.e_pos = jnp.exp(self.c)
        self.e_neg = jnp.exp(-self.c)
        self.e_rest = jnp.exp(c_last_rows - self.c)
        self.a_b = (self.q * self.e_pos).astype(BF16)
        self.b_b = (self.k * self.e_neg).astype(BF16)
        self.cn_b = (self.q * self.e_neg).astype(BF16)
        self.dp_b = (self.k * self.e_pos).astype(BF16)
        self.kd_b = (self.k * self.e_rest).astype(BF16)
        self.v_b = v.astype(BF16)
        self.lower, self.upper = masks

    def scores(self, kc=slice(None)):
        fwd = _dot_nt(self.a_b[:, kc], self.b_b[:, kc])
        bwd = _dot_nt(self.cn_b[:, kc], self.dp_b[:, kc])
        return jnp.where(self.lower, fwd, jnp.where(self.upper, bwd, 0.0)).astype(BF16)


def _gla_fwd_call(h1, nw, w_lines, gkw, gkb, hw, w_out, wf, target):
    seq = h1.shape[0]
    tm = min(GLA_FWD_ROW_TILE, seq)
    nt = seq // tm
    cpt = tm // CHUNK
    n_chunks = seq // CHUNK

    lane_tiles = D_MODEL // LANES
    pieces = [(lo, min(lo + WEIGHT_ROWS_PIECE, GLA_IN_WIDTH)) for lo in range(0, GLA_IN_WIDTH, WEIGHT_ROWS_PIECE)]

    def body(h_ref, nw_ref, lines_hbm, gkw_ref, gkb_ref, hw_ref, wout_ref, wf_ref, tgt_ref,
             dh2_ref, proj_ref, o_ref, st_ref, scores_ref, loss_ref, dwf_ref, win_hbm,
             state_ref, win_ref, piece_ref, lines_ref, piece_sem, win_sem):
        i = pl.program_id(0)
        win_copy = pltpu.make_async_copy(win_ref, win_hbm, win_sem)

        def piece_copy(k):
            lo, hi = pieces[k]
            n_lines = (hi - lo) * lane_tiles
            return pltpu.make_async_copy(lines_hbm.at[pl.ds(lo * lane_tiles, n_lines)],
                                         piece_ref.at[k % 2, pl.ds(0, n_lines)], piece_sem.at[k % 2])

        @pl.when(i == 0)
        def _():
            state_ref[...] = jnp.zeros_like(state_ref)
            loss_ref[...] = jnp.zeros_like(loss_ref)
            dwf_ref[...] = jnp.zeros_like(dwf_ref)
            piece_copy(0).start()
            for k, (lo, hi) in enumerate(pieces):
                if k + 1 < len(pieces):
                    piece_copy(k + 1).start()
                piece_copy(k).wait()
                n_lines = (hi - lo) * lane_tiles
                lines_ref[0:n_lines, :] = piece_ref[k % 2, 0:n_lines, :].astype(F32)
                for j in range(lane_tiles):
                    win_ref[lo:hi, j * LANES:(j + 1) * LANES] = lines_ref[pl.ds(j, hi - lo, stride=lane_tiles),
                                                                          :].astype(BF16)
            win_copy.start()

        pl.when(i == nt - 1)(win_copy.wait)

        ht = h_ref[...]
        xhat, _ = _rms(ht)
        n = (xhat * nw_ref[...]).astype(BF16)
        sections = {}
        for name, lo, hi in (("low", GLA_QKVG_WIDTH, GLA_IN_PAD), ("qk", 0, 2 * GLA_KEY_WIDTH),
                             ("v", 2 * GLA_KEY_WIDTH, GLA_QKVG_WIDTH - GLA_VALUE_WIDTH),
                             ("gate", GLA_QKVG_WIDTH - GLA_VALUE_WIDTH, GLA_QKVG_WIDTH)):
            rows = (win_ref[lo:hi, :] if hi <= GLA_IN_WIDTH
                    else _rows_then_zeros(win_ref, lo, GLA_IN_WIDTH, hi - lo))
            sections[name] = _dot_nt(n, rows)
            proj_ref[:, lo:hi] = sections[name]
        low_b = sections["low"].astype(BF16)
        masks = _chunk_masks(tm)
        on_heads = []
        for h in range(GLA_HEADS):
            kc, vc = _kcols(h), _vcols(h)
            g = _GlaTerms(kc, sections["qk"][:, kc], sections["qk"][:, GLA_KEY_WIDTH:][:, kc], sections["v"][:, vc],
                          low_b, gkw_ref, gkb_ref, masks)
            srows = slice(h * GLA_HEAD_V, (h + 1) * GLA_HEAD_V)
            scores = g.scores()
            for b in range(tm // ROW_TILE):
                part = slice(b * ROW_TILE, (b + 1) * ROW_TILE)
                scores_ref[part, h * ROW_TILE:(h + 1) * ROW_TILE] = scores[part, part]
            o_intra = _dot_nn(scores, g.v_b)
            state = state_ref[srows, :]
            o_rows = []
            for j in range(cpt):
                r = _chunk_rows(j)
                st_ref[j, srows, :] = state
                o_rows.append(o_intra[r] + _dot_nt(g.a_b[r], state.astype(BF16)))
                decay = jnp.exp(g.c_last[j])
                state = state * decay + _dot_tn(g.v_b[r], g.kd_b[r])
            state_ref[srows, :] = state
            o_head = jnp.concatenate(o_rows, axis=0)
            o_ref[:, vc] = o_head
            proj_ref[:, GLA_SAVED_Z + kc.start:GLA_SAVED_Z + kc.stop] = g.z
            proj_ref[:, GLA_SAVED_C + kc.start:GLA_SAVED_C + kc.stop] = g.c
            on_heads.append(_rms(o_head)[0])
        gate = sections["gate"]
        on = jnp.concatenate(on_heads, axis=1) * hw_ref[...]
        y = (on * (gate * _sigmoid(gate))).astype(BF16)
        h2 = ht + _dot_nn(y, wout_ref[...])
        xhat2, rstd2 = _rms(h2)
        wf_row = wf_ref[...]
        err = xhat2 * wf_row - tgt_ref[...]
        loss_ref[...] += 0.5 * jnp.sum(err * err) / D_MODEL
        dout = err * (1.0 / D_MODEL)
        dwf_ref[...] += jnp.sum(dout * xhat2, axis=0, keepdims=True)
        dh2_ref[...] = _rms_bwd(dout * wf_row, xhat2, rstd2)

    row = lambda i: (i, 0)
    return pl.pallas_call(
        body, name="gla_fwd", grid=(nt,),
        in_specs=[pl.BlockSpec((tm, D_MODEL), row), _const((1, D_MODEL)), pl.BlockSpec(memory_space=pl.ANY),
                  _const((GLA_LOW_PAD, GLA_KEY_WIDTH)), _const((1, GLA_KEY_WIDTH)), _const((1, GLA_VALUE_WIDTH)),
                  _const((GLA_VALUE_WIDTH, D_MODEL)), _const((1, D_MODEL)), pl.BlockSpec((tm, D_MODEL), row)],
        out_specs=[pl.BlockSpec((tm, D_MODEL), row), pl.BlockSpec((tm, GLA_SAVED_WIDTH), row),
                   pl.BlockSpec((tm, GLA_VALUE_WIDTH), row),
                   pl.BlockSpec((cpt, GLA_VALUE_WIDTH, GLA_HEAD_K), lambda i: (i, 0, 0)),
                   pl.BlockSpec((tm, GLA_HEADS * ROW_TILE), row), _full((8, LANES)), _full((1, D_MODEL)),
                   pl.BlockSpec(memory_space=pl.ANY)],
        out_shape=[jax.ShapeDtypeStruct((seq, D_MODEL), F32), jax.ShapeDtypeStruct((seq, GLA_SAVED_WIDTH), F32),
                   jax.ShapeDtypeStruct((seq, GLA_VALUE_WIDTH), F32),
                   jax.ShapeDtypeStruct((n_chunks, GLA_VALUE_WIDTH, GLA_HEAD_K), F32),
                   jax.ShapeDtypeStruct((seq, GLA_HEADS * ROW_TILE), BF16),
                   jax.ShapeDtypeStruct((8, LANES), F32), jax.ShapeDtypeStruct((1, D_MODEL), F32),
                   jax.ShapeDtypeStruct((GLA_IN_WIDTH, D_MODEL), BF16)],
        scratch_shapes=[pltpu.VMEM((GLA_VALUE_WIDTH, GLA_HEAD_K), F32), pltpu.VMEM((GLA_IN_WIDTH, D_MODEL), BF16),
                        pltpu.VMEM((2, WEIGHT_ROWS_PIECE * lane_tiles, LANES), BF16),
                        pltpu.VMEM((WEIGHT_ROWS_PIECE * lane_tiles, LANES), F32), pltpu.SemaphoreType.DMA((2,)),
                        pltpu.SemaphoreType.DMA(())],
        compiler_params=_params(),
    )(h1, nw, w_lines, gkw, gkb, hw, w_out, wf, target)


def _gla_bwd_call(dh2, proj, o, states, scores, gkw, gkb, hw, w_out):
    seq = dh2.shape[0]
    tm = ROW_TILE
    nt = seq // tm
    cpt = tm // CHUNK

    def body(dh_ref, proj_ref, o_ref, st_ref, scores_ref, gkw_ref, gkb_ref, hw_ref, wout_ref,
             dproj_ref, dwout_hbm, dhw_ref, dgkw_ref, dgkb_ref, dstate_ref, dwout_acc, dwout_stage):
        i = pl.program_id(0)

        @pl.when(i == 0)
        def _():
            dstate_ref[...] = jnp.zeros_like(dstate_ref)
            dwout_acc[...] = jnp.zeros_like(dwout_acc)
            dhw_ref[...] = jnp.zeros_like(dhw_ref)
            dgkw_ref[...] = jnp.zeros_like(dgkw_ref)
            dgkb_ref[...] = jnp.zeros_like(dgkb_ref)

        dhb = dh_ref[...].astype(BF16)
        dy = _dot_nt(dhb, wout_ref[...])
        v0, g0 = 2 * GLA_KEY_WIDTH, GLA_QKVG_WIDTH - GLA_VALUE_WIDTH
        gate = proj_ref[:, g0:GLA_QKVG_WIDTH]
        low_b = proj_ref[:, GLA_QKVG_WIDTH:GLA_IN_PAD].astype(BF16)
        o = o_ref[...]
        hw_row = hw_ref[...]
        sg = _sigmoid(gate)
        silu = gate * sg
        don = dy * silu
        on_parts, do_parts, dhw_parts = [], [], []
        for h in range(GLA_HEADS):
            vc = _vcols(h)
            xh, rs = _rms(o[:, vc])
            on_parts.append(xh * hw_row[:, vc])
            dhw_parts.append(jnp.sum(don[:, vc] * xh, axis=0, keepdims=True))
            do_parts.append(_rms_bwd(don[:, vc] * hw_row[:, vc], xh, rs).astype(BF16))
        on = jnp.concatenate(on_parts, axis=1)
        dwout_acc[...] += _dot_tn((on * silu).astype(BF16), dhb)
        dhw_ref[...] += jnp.concatenate(dhw_parts, axis=1)
        dproj_ref[:, g0:GLA_QKVG_WIDTH] = (dy * on * (sg * (1.0 + gate * (1.0 - sg)))).astype(BF16)

        last_row = lax.broadcasted_iota(jnp.int32, (CHUNK, 1), 0) == CHUNK - 1
        g = _GlaTerms(slice(0, GLA_KEY_WIDTH), proj_ref[:, :GLA_KEY_WIDTH], proj_ref[:, GLA_KEY_WIDTH:v0],
                      proj_ref[:, v0:g0], low_b, gkw_ref, gkb_ref, _chunk_masks(tm),
                      saved=(proj_ref[:, GLA_SAVED_Z:GLA_SAVED_C], proj_ref[:, GLA_SAVED_C:GLA_SAVED_WIDTH]))
        dc_h = []
        for h in range(GLA_HEADS):
            kc, vc = _kcols(h), _vcols(h)
            k_cols = slice(GLA_KEY_WIDTH + kc.start, GLA_KEY_WIDTH + kc.stop)
            v_cols = slice(v0 + vc.start, v0 + vc.stop)
            do_h = do_parts[h]
            srows = slice(h * GLA_HEAD_V, (h + 1) * GLA_HEAD_V)
            scores = scores_ref[:, h * ROW_TILE:(h + 1) * ROW_TILE]
            dscores = _dot_nt(do_h, g.v_b[:, vc])
            dfwd = jnp.where(g.lower, dscores, 0.0).astype(BF16)
            dbwd = jnp.where(g.upper, dscores, 0.0).astype(BF16)
            dv_intra = _dot_tn(scores, do_h)
            da_intra = _dot_nn(dfwd, g.b_b[:, kc])
            db = _dot_tn(dfwd, g.a_b[:, kc])
            dcn = _dot_nn(dbwd, g.dp_b[:, kc])
            ddp = _dot_tn(dbwd, g.cn_b[:, kc])
            dstate = dstate_ref[srows, :]
            da_rows, dkd_rows, dv_rows, dcl_rows = [None] * cpt, [None] * cpt, [None] * cpt, [None] * cpt
            for j in reversed(range(cpt)):
                r = _chunk_rows(j)
                state = st_ref[j, srows, :]
                dstate_b = dstate.astype(BF16)
                do_c = do_h[r]
                dv_rows[j] = dv_intra[r] + _dot_nt(g.kd_b[r, kc], dstate_b)
                da_rows[j] = da_intra[r] + _dot_nn(do_c, state.astype(BF16))
                dkd = _dot_nn(g.v_b[r, vc], dstate_b) * g.e_rest[r, kc]
                dkd_rows[j] = dkd
                decay = jnp.exp(g.c_last[j][:, kc])
                dc_last = (jnp.sum(dkd * g.k[r, kc], axis=0, keepdims=True)
                           + decay * jnp.sum(state * dstate, axis=0, keepdims=True))
                dcl_rows[j] = jnp.where(last_row, dc_last, 0.0)
                dstate = _dot_tn(do_c, g.a_b[r, kc]) + dstate * decay
            dstate_ref[srows, :] = dstate
            da = jnp.concatenate(da_rows, axis=0)
            dkd = jnp.concatenate(dkd_rows, axis=0)
            dproj_ref[:, v_cols] = jnp.concatenate(dv_rows, axis=0).astype(BF16)
            q_up, q_down = da * g.e_pos[:, kc], dcn * g.e_neg[:, kc]
            k_up, k_down = ddp * g.e_pos[:, kc], db * g.e_neg[:, kc] + dkd
            dproj_ref[:, kc] = (Q_SCALE * (q_up + q_down)).astype(BF16)
            dproj_ref[:, k_cols] = (k_up + k_down).astype(BF16)
            dc_h.append(g.q[:, kc] * (q_up - q_down) + g.k[:, kc] * (k_up - k_down)
                        + jnp.concatenate(dcl_rows, axis=0))
        dz = _chunk_scan(jnp.concatenate(dc_h, axis=1), True) * (1.0 / GATE_NORMALIZER) * (1.0 - _sigmoid(g.z))
        dzb = dz.astype(BF16)
        dgkb_ref[...] += jnp.sum(dz, axis=0, keepdims=True)
        dgkw_ref[...] += _dot_tn(low_b, dzb)
        dproj_ref[:, GLA_QKVG_WIDTH:] = _dot_nt(dzb, gkw_ref[...]).astype(BF16)

        @pl.when(i == nt - 1)
        def _():
            dwout_stage[...] = dwout_acc[...].astype(BF16)
            pltpu.sync_copy(dwout_stage, dwout_hbm)

    rev = lambda i: (nt - 1 - i, 0)
    return pl.pallas_call(
        body, name="gla_bwd", grid=(nt,),
        in_specs=[pl.BlockSpec((tm, D_MODEL), rev), pl.BlockSpec((tm, GLA_SAVED_WIDTH), rev),
                  pl.BlockSpec((tm, GLA_VALUE_WIDTH), rev),
                  pl.BlockSpec((cpt, GLA_VALUE_WIDTH, GLA_HEAD_K), lambda i: (nt - 1 - i, 0, 0)),
                  pl.BlockSpec((tm, GLA_HEADS * ROW_TILE), rev),
                  _const((GLA_LOW_PAD, GLA_KEY_WIDTH)), _const((1, GLA_KEY_WIDTH)), _const((1, GLA_VALUE_WIDTH)),
                  _const((GLA_VALUE_WIDTH, D_MODEL))],
        out_specs=[pl.BlockSpec((tm, GLA_IN_PAD), rev), pl.BlockSpec(memory_space=pl.ANY),
                   _full((1, GLA_VALUE_WIDTH)), _full((GLA_LOW_PAD, GLA_KEY_WIDTH)), _full((1, GLA_KEY_WIDTH))],
        out_shape=[jax.ShapeDtypeStruct((seq, GLA_IN_PAD), BF16), jax.ShapeDtypeStruct((GLA_VALUE_WIDTH, D_MODEL), BF16),
                   jax.ShapeDtypeStruct((1, GLA_VALUE_WIDTH), F32), jax.ShapeDtypeStruct((GLA_LOW_PAD, GLA_KEY_WIDTH), F32),
                   jax.ShapeDtypeStruct((1, GLA_KEY_WIDTH), F32)],
        scratch_shapes=[pltpu.VMEM((GLA_VALUE_WIDTH, GLA_HEAD_K), F32), pltpu.VMEM((GLA_VALUE_WIDTH, D_MODEL), F32),
                        pltpu.VMEM((GLA_VALUE_WIDTH, D_MODEL), BF16)],
        compiler_params=_params(),
    )(dh2, proj, o, states, scores, gkw, gkb, hw, w_out)


def _position():
    return lax.axis_index("x"), lax.axis_index("y"), lax.axis_index("c")


def _lead_slot(ref, d):
    return ref.at[d]


def _row_slot(rows):
    return lambda ref, d: ref.at[pl.ds(pl.multiple_of(d * rows, rows), rows)]


def _dim1_slot(size):
    return lambda ref, d: ref.at[:, pl.ds(pl.multiple_of(d * size, size), size)]


class _Gather:
    def __init__(self, in_refs, out_refs, slots, send_sems, recv_sems, local_sems):
        self.in_refs, self.out_refs, self.slots = in_refs, out_refs, slots
        self.send_sems, self.recv_sems, self.local_sems = send_sems, recv_sems, local_sems
        self.n = len(in_refs)
        x, y, c = _position()
        self.c = c
        self.me, self.sibling = (x, y, c), (x, y, 1 - c)
        self.near = [(1 - x, y), (x, 1 - y)]
        self.diagonal = (1 - x, 1 - y)
        self.relay_from = (x ^ c, y ^ (1 - c))
        self.relay_to = (x ^ (1 - c), y ^ c)

    def _copy(self, a, k, block, to, from_input=False):
        part = self.slots[a](self.out_refs[a], 4 * block[0] + 2 * block[1] + block[2])
        return pltpu.make_async_remote_copy(
            src_ref=self.in_refs[a] if from_input else part, dst_ref=part,
            send_sem=self.send_sems.at[a, k], recv_sem=self.recv_sems.at[a, k], device_id=to, device_id_type=MESH)

    def _mine(self):
        return [pltpu.make_async_copy(self.in_refs[a], self.slots[a](self.out_refs[a], 4 * self.me[0] + 2 * self.me[1]
                                                                    + self.me[2]), self.local_sems.at[a])
                for a in range(self.n)]

    def _first(self):
        first = [self._copy(a, 0, self.me, self.sibling, True) for a in range(self.n)]
        return first + [self._copy(a, 1 + j, self.me, (*chip, self.c), True)
                        for j, chip in enumerate(self.near) for a in range(self.n)]

    def _relayed(self):
        return [self._copy(a, 3, (*self.relay_from, self.c), (*self.relay_to, self.c)) for a in range(self.n)]

    def _passed(self, j):
        chip = self.near[j] if j < 2 else self.diagonal
        return [self._copy(a, 4 + j, (*chip, self.c), self.sibling) for a in range(self.n)]

    def start(self):
        for cp in self._mine() + self._first():
            cp.start()

    def forward(self):
        for j, chip in enumerate(self.near):
            for a in range(self.n):
                self._copy(a, 1 + j, (*chip, self.c), self.me).wait_recv()
        for cp in self._relayed() + self._passed(0) + self._passed(1):
            cp.start()

    def relay(self):
        pass

    def finish(self):
        for a in range(self.n):
            self._copy(a, 3, (*self.diagonal, self.c), self.me).wait_recv()
        for cp in self._passed(2):
            cp.start()
        for a in range(self.n):
            self._copy(a, 0, self.sibling, self.me).wait_recv()
        for j, chip in enumerate(self.near + [self.diagonal]):
            for a in range(self.n):
                self._copy(a, 4 + j, (*chip, 1 - self.c), self.me).wait_recv()
        for cp in self._first() + self._relayed() + self._passed(0) + self._passed(1) + self._passed(2):
            cp.wait_send()
        for cp in self._mine():
            cp.wait()


class _Exchange:
    def __init__(self, in_refs, out_refs, slots, send_sems, recv_sems, local_sems):
        self.in_refs, self.out_refs, self.slots = in_refs, out_refs, slots
        self.send_sems, self.recv_sems, self.local_sems = send_sems, recv_sems, local_sems
        self.n = len(in_refs)
        self.pos = _position()

    def _copies(self):
        x, y, c = self.pos
        me = 4 * x + 2 * y + c
        mine = [pltpu.make_async_copy(self.slots[a](self.in_refs[a], me), self.out_refs[a].at[me],
                                      self.local_sems.at[a]) for a in range(self.n)]
        remote = []
        for k in range(1, N_DEV):
            px, py, pc = x ^ (k >> 2), y ^ ((k >> 1) & 1), c ^ (k & 1)
            for a in range(self.n):
                remote.append(pltpu.make_async_remote_copy(
                    src_ref=self.slots[a](self.in_refs[a], 4 * px + 2 * py + pc), dst_ref=self.out_refs[a].at[me],
                    send_sem=self.send_sems.at[a, k - 1], recv_sem=self.recv_sems.at[a, k - 1],
                    device_id=(px, py, pc), device_id_type=MESH))
        return mine, remote

    def start(self):
        mine, remote = self._copies()
        for cp in mine + remote:
            cp.start()

    def forward(self):
        pass

    def relay(self):
        pass

    def finish(self):
        mine, remote = self._copies()
        for cp in remote:
            cp.wait_recv()
        for cp in remote:
            cp.wait_send()
        for cp in mine:
            cp.wait()


class _Rider:
    def __init__(self, kind, arrays, out_shapes, slots, scratch=None, forward_step=None):
        self.kind, self.arrays, self.slots = kind, list(arrays), slots
        self.n = len(self.arrays)
        hbm = pl.BlockSpec(memory_space=pl.ANY)
        self.in_specs = [hbm] * self.n
        self.out_specs = [hbm] * self.n
        self.out_shape = [jax.ShapeDtypeStruct(tuple(s), a.dtype) for s, a in zip(out_shapes, self.arrays)]
        self.scratch = scratch if scratch is not None else [
            pltpu.SemaphoreType.DMA((self.n, 7)), pltpu.SemaphoreType.DMA((self.n, 7)),
            pltpu.SemaphoreType.DMA((self.n,))]
        self.forward_step = forward_step
        self.relay_step = None

    def bind(self, in_refs, out_refs, scratch):
        return self.kind(in_refs, out_refs, self.slots, *scratch)


def _gather_rider(shards, full_shapes, slots, forward_step=None):
    return _Rider(_Gather, shards, full_shapes, slots, None, forward_step)


def _exchange_rider(sends, part_shapes, slots):
    return _Rider(_Exchange, sends, [(N_DEV,) + tuple(s) for s in part_shapes], slots)


def _split_refs(refs, n_in, n_out, n_scratch, rider):
    k = rider.n if rider is not None else 0
    ins, r_ins = refs[:n_in], refs[n_in:n_in + k]
    outs, r_outs = refs[n_in + k:n_in + k + n_out], refs[n_in + k + n_out:n_in + 2 * k + n_out]
    rest = refs[n_in + 2 * k + n_out:]
    scratch, r_scratch = rest[:n_scratch], rest[n_scratch:]
    comm = rider.bind(r_ins, r_outs, r_scratch) if rider is not None else None
    if comm is not None:
        comm.forward_step, comm.relay_step = rider.forward_step, rider.relay_step
    return ins + outs + scratch, comm


def _ride_before(comm, i, nt):
    if comm is not None:
        pl.when(i == 0)(comm.start)
        pl.when(i == (nt - 1 if comm.forward_step is None else min(comm.forward_step, nt - 1)))(comm.forward)
        pl.when(i == (nt - 1 if comm.relay_step is None else min(comm.relay_step, nt - 1)))(comm.relay)


def _ride_after(comm, i, nt):
    if comm is not None:
        pl.when(i == nt - 1)(comm.finish)


def _extend(specs, rider, field):
    return list(specs) + (getattr(rider, field) if rider is not None else [])


def _comm_call(name, rider, cast_from):
    order = sorted(cast_from)

    def body(*refs):
        n = rider.n
        ins, outs, rest = list(refs[:n]), refs[n:2 * n], refs[2 * n:]
        for stage, index in zip(rest[:len(order)], order):
            stage[...] = ins[index][...].astype(BF16)
            ins[index] = stage
        comm = rider.bind(ins, outs, rest[len(order):])
        comm.start()
        comm.forward()
        comm.relay()
        comm.finish()

    vmem = pl.BlockSpec(memory_space=pltpu.VMEM)
    return pl.pallas_call(
        body, name=name, in_specs=[vmem if i in cast_from else spec for i, spec in enumerate(rider.in_specs)],
        out_specs=rider.out_specs, out_shape=rider.out_shape,
        scratch_shapes=[pltpu.VMEM(cast_from[i].shape, BF16) for i in order] + rider.scratch,
        compiler_params=pltpu.CompilerParams(vmem_limit_bytes=VMEM_LIMIT),
    )(*[cast_from.get(i, a) for i, a in enumerate(rider.arrays)])


N_CHIPS = 4


class _TwoLevel:
    def __init__(self, in_refs, out_refs, slots, *scratch):
        self.in_refs, self.out_refs, self.slots = in_refs, out_refs, slots
        self.n = n = len(in_refs)
        self.own_bufs, self.recv_bufs, self.relay_bufs = scratch[:n], scratch[n:2 * n], scratch[2 * n:3 * n]
        self.swap_send, self.swap_recv, self.local_sems, self.chip_send, self.chip_recv = scratch[3 * n:]
        x, y, c = self.pos = _position()
        self.first = (x ^ (1 - c), y ^ c)
        self.second = (x ^ c, y ^ (1 - c))
        self.chip_index = lambda chip: 2 * chip[0] + chip[1]

    def _swap(self):
        x, y, c = self.pos
        return [pltpu.make_async_remote_copy(
            src_ref=self.slots[a](self.in_refs[a], 2 * q + 1 - c), dst_ref=self.recv_bufs[a].at[q],
            send_sem=self.swap_send.at[a, q], recv_sem=self.swap_recv.at[a, q],
            device_id=(x, y, 1 - c), device_id_type=MESH) for a in range(self.n) for q in range(N_CHIPS)]

    def _mine(self):
        c = self.pos[2]
        return [pltpu.make_async_copy(self.slots[a](self.in_refs[a], 2 * q + c), self.own_bufs[a].at[q],
                                      self.local_sems.at[a, q]) for a in range(self.n) for q in range(N_CHIPS)]

    def _to_chip(self, a, k, src, dst, chip):
        return pltpu.make_async_remote_copy(
            src_ref=src, dst_ref=dst, send_sem=self.chip_send.at[a, k], recv_sem=self.chip_recv.at[a, k],
            device_id=(*chip, self.pos[2]), device_id_type=MESH)

    def _first_wave(self):
        x, y, _ = self.pos
        diagonal = self.chip_index((1 - x, 1 - y))
        passed_on = [self._to_chip(a, 1, self.own_bufs[a].at[diagonal], self.relay_bufs[a], self.first)
                     for a in range(self.n)]
        return passed_on + [self._to_chip(a, 0, self.own_bufs[a].at[self.chip_index(self.first)],
                                          self.out_refs[a].at[1], self.first) for a in range(self.n)]

    def _second_wave(self):
        return [self._to_chip(a, 2, self.own_bufs[a].at[self.chip_index(self.second)], self.out_refs[a].at[2],
                              self.second) for a in range(self.n)]

    def _own(self):
        x, y, _ = self.pos
        return [pltpu.make_async_copy(self.own_bufs[a].at[2 * x + y], self.out_refs[a].at[0],
                                      self.local_sems.at[a, N_CHIPS]) for a in range(self.n)]

    def start(self):
        for cp in self._swap() + self._mine():
            cp.start()

    def forward(self):
        swap, mine = self._swap(), self._mine()
        for a in range(self.n):
            for q in range(N_CHIPS):
                mine[a * N_CHIPS + q].wait()
                swap[a * N_CHIPS + q].wait_recv()
                self.own_bufs[a][q] = (self.own_bufs[a][q].astype(F32)
                                       + self.recv_bufs[a][q].astype(F32)).astype(BF16)
        for cp in self._first_wave() + self._own():
            cp.start()

    def relay(self):
        second = self.chip_index(self.second)
        for a in range(self.n):
            self._to_chip(a, 1, self.relay_bufs[a], self.relay_bufs[a], self.first).wait_recv()
            self.own_bufs[a][second] = (self.own_bufs[a][second].astype(F32)
                                        + self.relay_bufs[a][...].astype(F32)).astype(BF16)
        for cp in self._second_wave():
            cp.start()

    def finish(self):
        for a in range(self.n):
            self._to_chip(a, 0, self.out_refs[a].at[1], self.out_refs[a].at[1], self.first).wait_recv()
            self._to_chip(a, 2, self.out_refs[a].at[2], self.out_refs[a].at[2], self.second).wait_recv()
        for cp in self._first_wave() + self._second_wave() + self._swap():
            cp.wait_send()
        for cp in self._own():
            cp.wait()


def _two_level_rider(sends, part_shapes, slots, forward_step=None, relay_step=None):
    n = len(sends)
    bufs = [pltpu.VMEM((N_CHIPS,) + tuple(s), a.dtype) for s, a in zip(part_shapes, sends)]
    relay_bufs = [pltpu.VMEM(tuple(s), a.dtype) for s, a in zip(part_shapes, sends)]
    scratch = bufs + bufs + relay_bufs + [
        pltpu.SemaphoreType.DMA((n, N_CHIPS)), pltpu.SemaphoreType.DMA((n, N_CHIPS)),
        pltpu.SemaphoreType.DMA((n, N_CHIPS + 1)), pltpu.SemaphoreType.DMA((n, 3)), pltpu.SemaphoreType.DMA((n, 3))]
    rider = _Rider(_TwoLevel, sends, [(3,) + tuple(s) for s in part_shapes], slots, scratch, forward_step)
    rider.relay_step = relay_step
    return rider


class _Joined:
    def __init__(self, first, second):
        self.first, self.second = first, second

    def start(self):
        self.first.start()
        self.second.start()

    def forward(self):
        self.first.forward()
        self.second.forward()

    def relay(self):
        self.first.relay()
        self.second.relay()

    def finish(self):
        self.first.finish()
        self.second.finish()


class _JoinedRider:
    def __init__(self, first, second):
        self.first, self.second = first, second
        self.n = first.n + second.n
        self.arrays = first.arrays + second.arrays
        self.in_specs = first.in_specs + second.in_specs
        self.out_specs = first.out_specs + second.out_specs
        self.out_shape = first.out_shape + second.out_shape
        self.scratch = first.scratch + second.scratch
        self.forward_step = first.forward_step
        self.relay_step = first.relay_step

    def bind(self, in_refs, out_refs, scratch):
        k, s = self.first.n, len(self.first.scratch)
        return _Joined(self.first.bind(in_refs[:k], out_refs[:k], scratch[:s]),
                       self.second.bind(in_refs[k:], out_refs[k:], scratch[s:]))


def _adamw(w, g, m, v):
    m = ADAM_B1 * m + (1.0 - ADAM_B1) * g
    v = ADAM_B2 * v + (1.0 - ADAM_B2) * (g * g)
    m_hat = m / (1.0 - ADAM_B1 ** ADAM_STEP)
    v_hat = v / (1.0 - ADAM_B2 ** ADAM_STEP)
    delta = -ADAM_LR * (m_hat / (jnp.sqrt(v_hat) + ADAM_EPS) + ADAM_WD * w)
    return delta, m, v


def _sum_parts(parts_ref, index=()):
    g = parts_ref[(0,) + index].astype(F32)
    for s in range(1, parts_ref.shape[0]):
        g = g + parts_ref[(s,) + index].astype(F32)
    return g


def _adamw_group_call(name, groups):
    k = len(groups)

    def body(*refs):
        ins, outs = refs[:4 * k], refs[4 * k:]
        for i in range(k):
            parts_ref, w_ref, m_ref, v_ref = ins[4 * i:4 * i + 4]
            g = _sum_parts(parts_ref)
            delta, m_new, v_new = _adamw(w_ref[...], g, m_ref[...], v_ref[...])
            for out_ref, value in zip(outs[4 * i:4 * i + 4], (g, delta, m_new, v_new)):
                out_ref[...] = value

    vmem = pl.BlockSpec(memory_space=pltpu.VMEM)
    res = pl.pallas_call(
        body, name=name, in_specs=[vmem] * (4 * k), out_specs=[vmem] * (4 * k),
        out_shape=[jax.ShapeDtypeStruct(grp[1].shape, F32) for grp in groups for _ in range(4)],
        compiler_params=pltpu.CompilerParams(vmem_limit_bytes=VMEM_LIMIT),
    )(*[a for grp in groups for a in grp])
    return [res[4 * i:4 * i + 4] for i in range(k)]


def _adamw_slabs_call(name, parts, w, m, v, rider=None):
    def main(parts_ref, w_ref, m_ref, v_ref, g_ref, delta_ref, m_out, v_out):
        g = _sum_parts(parts_ref)
        delta, m_new, v_new = _adamw(w_ref[...], g, m_ref[...], v_ref[...])
        g_ref[...] = g
        delta_ref[...] = delta
        m_out[...] = m_new
        v_out[...] = v_new

    def body(*refs):
        own, comm = _split_refs(refs, 4, 4, 0, rider)
        if comm is not None:
            comm.start()
        main(*own)
        if comm is not None:
            comm.forward()
            comm.relay()
            comm.finish()

    vmem = pl.BlockSpec(memory_space=pltpu.VMEM)
    return pl.pallas_call(
        body, name=name, in_specs=_extend([vmem] * 4, rider, "in_specs"),
        out_specs=_extend([vmem] * 4, rider, "out_specs"),
        out_shape=_extend([jax.ShapeDtypeStruct(w.shape, F32)] * 4, rider, "out_shape"),
        scratch_shapes=_extend([], rider, "scratch"),
        compiler_params=pltpu.CompilerParams(vmem_limit_bytes=VMEM_LIMIT),
    )(parts, w, m, v, *_extend([], rider, "arrays"))


WIDE_ROWS = 8
NARROW_ROWS = 40
NARROW_GKW_ROW = 8
NARROW_GKB_ROW = 24
NARROW_HW_ROW = 32
GROUP_SHARD = POOL_GROUP_DIM // N_DEV
KEY_SHARD = GLA_KEY_WIDTH // N_DEV
HEAD_V_SHARD = GLA_HEAD_V // N_DEV


def _small_adamw_call(wide, narrow, w, m, v):
    names = ("norm_w", "pool_scale", "final_norm_w", "pool_group_b", "gla_gk_w", "gla_gk_b", "gla_head_norm_w")
    where = {
        "norm_w": (0, slice(0, 2), slice(None)),
        "pool_scale": (0, slice(2, 3), slice(None)),
        "final_norm_w": (0, slice(3, 4), slice(None)),
        "pool_group_b": (1, slice(0, POOL_GROUPS), slice(0, GROUP_SHARD)),
        "gla_gk_w": (1, slice(NARROW_GKW_ROW, NARROW_GKW_ROW + GLA_GATE_RANK), slice(0, KEY_SHARD)),
        "gla_gk_b": (1, slice(NARROW_GKB_ROW, NARROW_GKB_ROW + 1), slice(0, KEY_SHARD)),
        "gla_head_norm_w": (1, slice(NARROW_HW_ROW, NARROW_HW_ROW + 1), slice(0, HEAD_V_SHARD)),
    }
    k = len(names)

    def body(*refs):
        parts = refs[0:2]
        w_refs, m_refs, v_refs = refs[2:2 + k], refs[2 + k:2 + 2 * k], refs[2 + 2 * k:2 + 3 * k]
        outs = refs[2 + 3 * k:]
        loss_ref = outs[0]
        loss_ref[...] = _sum_parts(parts[0], (slice(4, 5), slice(0, 1)))
        for i, name in enumerate(names):
            buf, rows, cols = where[name]
            g = _sum_parts(parts[buf], (rows, cols))
            delta, m_new, v_new = _adamw(w_refs[i][...], g, m_refs[i][...], v_refs[i][...])
            outs[1 + i][...] = g
            outs[1 + k + i][...] = delta
            outs[1 + 2 * k + i][...] = m_new
            outs[1 + 3 * k + i][...] = v_new

    vmem = pl.BlockSpec(memory_space=pltpu.VMEM)
    shapes = [jax.ShapeDtypeStruct(w[n].shape, F32) for n in names]
    res = pl.pallas_call(
        body, name="adamw_small", in_specs=[vmem] * (2 + 3 * k), out_specs=[vmem] * (1 + 4 * k),
        out_shape=[jax.ShapeDtypeStruct((1, 1), F32)] + shapes * 4,
    )(wide, narrow, *[w[n] for n in names], *[m[n] for n in names], *[v[n] for n in names])
    unzip = lambda j: dict(zip(names, res[1 + j * k:1 + (j + 1) * k]))
    return res[0], unzip(0), unzip(1), unzip(2), unzip(3)


def kernel(x, norm_w, pool_in_w, pool_group_w, pool_group_b, pool_scale, pool_out_w, gla_in_w, gla_gk_w, gla_gk_b, gla_head_norm_w, gla_out_w, final_norm_w, loss_target, m_norm_w, m_pool_in_w, m_pool_group_w, m_pool_group_b, m_pool_scale, m_pool_out_w, m_gla_in_w, m_gla_gk_w, m_gla_gk_b, m_gla_head_norm_w, m_gla_out_w, m_final_norm_w, v_norm_w, v_pool_in_w, v_pool_group_w, v_pool_group_b, v_pool_scale, v_pool_out_w, v_gla_in_w, v_gla_gk_w, v_gla_gk_b, v_gla_head_norm_w, v_gla_out_w, v_final_norm_w):
    w = dict(norm_w=norm_w, pool_in_w=pool_in_w, pool_group_w=pool_group_w, pool_group_b=pool_group_b,
             pool_scale=pool_scale, pool_out_w=pool_out_w, gla_in_w=gla_in_w, gla_gk_w=gla_gk_w, gla_gk_b=gla_gk_b,
             gla_head_norm_w=gla_head_norm_w, gla_out_w=gla_out_w, final_norm_w=final_norm_w)
    m = dict(norm_w=m_norm_w, pool_in_w=m_pool_in_w, pool_group_w=m_pool_group_w, pool_group_b=m_pool_group_b,
             pool_scale=m_pool_scale, pool_out_w=m_pool_out_w, gla_in_w=m_gla_in_w, gla_gk_w=m_gla_gk_w,
             gla_gk_b=m_gla_gk_b, gla_head_norm_w=m_gla_head_norm_w, gla_out_w=m_gla_out_w,
             final_norm_w=m_final_norm_w)
    v = dict(norm_w=v_norm_w, pool_in_w=v_pool_in_w, pool_group_w=v_pool_group_w, pool_group_b=v_pool_group_b,
             pool_scale=v_pool_scale, pool_out_w=v_pool_out_w, gla_in_w=v_gla_in_w, gla_gk_w=v_gla_gk_w,
             gla_gk_b=v_gla_gk_b, gla_head_norm_w=v_gla_head_norm_w, gla_out_w=v_gla_out_w,
             final_norm_w=v_final_norm_w)
    col_shard = GLA_IN_WIDTH // N_DEV
    row_shard = D_MODEL // N_DEV

    def lanes(a):
        return jnp.pad(a, [(0, 0)] * (a.ndim - 1) + [(0, LANES - a.shape[-1])])

    small_in = jnp.concatenate([lanes(pool_group_b[0]), lanes(gla_gk_b), lanes(gla_head_norm_w),
                                jnp.zeros((2, LANES), F32)], axis=0)
    in_cols = 2 * POOL_WIDTH // N_DEV
    pool_f32 = [pool_in_w[0], pool_group_w[0], pool_out_w[0]]
    pool_in, pool_gw, pool_out, small_all = _comm_call("pool_weights_all_gather", _gather_rider(
        [jax.ShapeDtypeStruct(a.shape, BF16) for a in pool_f32] + [small_in],
        [(D_MODEL, 2 * POOL_WIDTH), (POOL_GROUPS, POOL_GROUP_DIM, POOL_GROUP_DIM), (POOL_WIDTH, D_MODEL),
         (N_DEV, 8, LANES)],
        [_dim1_slot(in_cols), _dim1_slot(GROUP_SHARD), _row_slot(row_shard), _lead_slot]), dict(enumerate(pool_f32)))
    pool_gb = jnp.transpose(small_all[:, 0:POOL_GROUPS, :GROUP_SHARD], (1, 0, 2)).reshape(1, POOL_WIDTH)
    gla_gkb = small_all[:, POOL_GROUPS, :KEY_SHARD].reshape(1, GLA_KEY_WIDTH)
    gla_hw = jnp.tile(small_all[:, POOL_GROUPS + 1, :HEAD_V_SHARD].reshape(1, GLA_HEAD_V), (1, GLA_HEADS))
    nw0, nw1, wf = norm_w[0:1], norm_w[1:2], final_norm_w.reshape(1, D_MODEL)
    xs, target = x[0], loss_target[0]

    slabs = col_shard * D_MODEL // (BF16_ROWS * LANES)
    as_slabs = lambda t: jnp.transpose(t[0]).reshape(slabs, BF16_ROWS, LANES)
    h1, pool_y, pool_silu, pool_dsilu, pooled, mixed, gla_in_parts, gkw_parts, gla_out = _pool_fwd_call(
        xs, nw0, pool_in, pool_gw, pool_gb, pool_scale, pool_out, _gather_rider(
            [as_slabs(gla_in_w).astype(BF16), gla_gk_w[0].astype(BF16), gla_out_w[0].astype(BF16)],
            [(N_DEV, slabs, BF16_ROWS, LANES), (N_DEV, GLA_GATE_RANK, KEY_SHARD), (GLA_VALUE_WIDTH, D_MODEL)],
            [_lead_slot, _lead_slot, _row_slot(row_shard)], GATHER_RELAY_STEP))
    gla_gkw = jnp.pad(jnp.transpose(gkw_parts, (1, 0, 2)).reshape(GLA_GATE_RANK, GLA_KEY_WIDTH),
                      ((0, GLA_LOW_PAD - GLA_GATE_RANK), (0, 0)))
    dh2, proj, o, states, scores, loss_part, dwf, gla_in = _gla_fwd_call(
        h1, nw1, gla_in_parts.reshape(GLA_IN_WIDTH * D_MODEL // LANES, LANES), gla_gkw, gla_gkb, gla_hw, gla_out,
        wf, target)

    dproj, d_gla_out, dhw, dgkw, dgkb = _gla_bwd_call(dh2, proj, o, states, scores, gla_gkw, gla_gkb, gla_hw,
                                                      gla_out)
    dh1, d_gla_in, dnw1, landed_gla_out = _inproj_bwd_call(
        "gla_in_bwd", dproj, h1, nw1, gla_in, dh2,
        _exchange_rider([d_gla_out], [(row_shard, D_MODEL)], [_row_slot(row_shard)]), transposed=True)
    gla_in_send = d_gla_in.reshape(N_DEV, slabs, BF16_ROWS, LANES)
    dp, d_pool_out, dgw, dgb, dsc, landed_gla_in = _pool_bwd_call(
        dh1, pool_y, pool_silu, pool_dsilu, pooled, mixed, pool_gw, pool_scale, pool_out,
        _two_level_rider([gla_in_send], [(slabs, BF16_ROWS, LANES)], [_lead_slot], TWO_LEVEL_ADD_STEP,
                         TWO_LEVEL_RELAY_STEP))
    grad_x, d_pool_in, dnw0 = _inproj_bwd_full_depth_call("pool_in_bwd", dp, xs, nw0, pool_in, dh1)

    wide = jnp.concatenate([
        dnw0, dnw1, dsc, dwf, jnp.pad(loss_part[0:1, 0:1], ((0, 0), (0, D_MODEL - 1))),
        jnp.zeros((WIDE_ROWS - 5, D_MODEL), F32)], axis=0)

    def rows8(a):
        return jnp.pad(lanes(a), ((0, 0), (0, -a.shape[1] % 8), (0, 0)))

    narrow = jnp.concatenate([
        rows8(jnp.transpose(dgb.reshape(POOL_GROUPS, N_DEV, GROUP_SHARD), (1, 0, 2))),
        rows8(jnp.transpose(dgkw[:GLA_GATE_RANK].reshape(GLA_GATE_RANK, N_DEV, KEY_SHARD), (1, 0, 2))),
        rows8(dgkb.reshape(N_DEV, 1, KEY_SHARD)),
        rows8(dhw.reshape(GLA_HEADS, GLA_HEAD_V).sum(axis=0).reshape(N_DEV, 1, HEAD_V_SHARD)),
    ], axis=1)
    last_exchange = _JoinedRider(
        _two_level_rider([d_pool_in, d_pool_out, dgw],
                         [(D_MODEL, in_cols), (row_shard, D_MODEL), (POOL_GROUPS, GROUP_SHARD, POOL_GROUP_DIM)],
                         [_dim1_slot(in_cols), _row_slot(row_shard), _dim1_slot(GROUP_SHARD)]),
        _exchange_rider([wide, narrow], [(WIDE_ROWS, D_MODEL), (NARROW_ROWS, LANES)],
                        [lambda ref, d: ref, _lead_slot]))

    res = {}
    *outs, landed_pool_in, landed_pool_out, landed_gw, landed_wide, landed_narrow = _adamw_slabs_call(
        "adamw_gla_in_w", landed_gla_in, as_slabs(gla_in_w), as_slabs(m_gla_in_w), as_slabs(v_gla_in_w),
        last_exchange)
    res["gla_in_w"] = [jnp.transpose(t.reshape(col_shard, D_MODEL))[None] for t in outs]
    rest = [("pool_in_w", landed_pool_in, (D_MODEL, in_cols)),
            ("pool_group_w", landed_gw, (POOL_GROUPS * GROUP_SHARD, POOL_GROUP_DIM)),
            ("pool_out_w", landed_pool_out, (row_shard, D_MODEL)), ("gla_out_w", landed_gla_out, (row_shard, D_MODEL))]
    updates = _adamw_group_call("adamw_matrices", [
        (parts.reshape((parts.shape[0],) + shape), w[name].reshape(shape), m[name].reshape(shape),
         v[name].reshape(shape)) for name, parts, shape in rest])
    for (name, _, _), outs in zip(rest, updates):
        res[name] = [t.reshape(w[name].shape) for t in outs]
    small_shapes ={"norm_w": (2, D_MODEL), "pool_scale": (1, D_MODEL), "final_norm_w": (1, D_MODEL),
                    "pool_group_b": (POOL_GROUPS, GROUP_SHARD), "gla_gk_w": (GLA_GATE_RANK, KEY_SHARD),
                    "gla_gk_b": (1, KEY_SHARD), "gla_head_norm_w": (1, HEAD_V_SHARD)}
    as_small = lambda t: {n: t[n].reshape(s) for n, s in small_shapes.items()}
    loss, *small_outs = _small_adamw_call(landed_wide, landed_narrow, as_small(w), as_small(m), as_small(v))
    for name in small_shapes:
        res[name] = [t[name].reshape(w[name].shape) for t in small_outs]
    order = ("norm_w", "pool_in_w", "pool_group_w", "pool_group_b", "pool_scale", "pool_out_w", "gla_in_w",
             "gla_gk_w", "gla_gk_b", "gla_head_norm_w", "gla_out_w", "final_norm_w")
    return (loss.reshape(()), grad_x[None], *[res[n][0] for n in order], *[res[n][1] for n in order],
            *[res[n][2] for n in order], *[res[n][3] for n in order])
```

```python
import jax
import jax.numpy as jnp
from jax import lax
from jax.experimental import pallas as pl
from jax.experimental.pallas import tpu as pltpu

F32 = jnp.float32
BF16 = jnp.bfloat16
MESH = pl.DeviceIdType.MESH

N_DEV = 8
D_MODEL = 1024
POOL_WIDTH = 1024
POOL_GROUPS = 4
POOL_GROUP_DIM = 256
POOL_HALO = 16
GLA_HEADS = 4
GLA_HEAD_K = 128
GLA_HEAD_V = 256
GLA_KEY_WIDTH = 512
GLA_VALUE_WIDTH = 1024
GLA_GATE_RANK = 16
GLA_IN_WIDTH = 3088
GLA_IN_PAD = 3200
GLA_SAVED_Z = GLA_IN_PAD
GLA_SAVED_C = GLA_SAVED_Z + 512
GLA_SAVED_WIDTH = GLA_SAVED_C + 512
GLA_LOW_PAD = 128
GLA_QKVG_WIDTH = 3072
CHUNK = 64
GATE_NORMALIZER = 16.0
RMS_EPS = 1e-6
Q_SCALE = GLA_HEAD_K ** -0.5

ADAM_LR = 0.001
ADAM_B1 = 0.9
ADAM_B2 = 0.999
ADAM_EPS = 1e-08
ADAM_WD = 0.01
ADAM_STEP = 10

LANES = 128
BF16_ROWS = 16
VMEM_LIMIT = 60 * 1024 * 1024
ROW_TILE = 256
GLA_FWD_ROW_TILE = 512
MATMUL_ROW_TILE = 512
ROW_MAJOR_PIECE = 776
WEIGHT_ROWS_PIECE = 208
GATHER_RELAY_STEP = 5
TWO_LEVEL_ADD_STEP = 1
TWO_LEVEL_RELAY_STEP = 4


def _dot_nn(a, b):
    return lax.dot_general(a, b, (((1,), (0,)), ((), ())), preferred_element_type=F32)


def _dot_nt(a, b):
    return lax.dot_general(a, b, (((1,), (1,)), ((), ())), preferred_element_type=F32)


def _dot_tn(a, b):
    return lax.dot_general(a, b, (((0,), (0,)), ((), ())), preferred_element_type=F32)


def _rms(x):
    rstd = lax.rsqrt(jnp.mean(x * x, axis=-1, keepdims=True) + RMS_EPS)
    return x * rstd, rstd


def _rms_bwd(dxhat, xhat, rstd):
    return rstd * (dxhat - xhat * jnp.mean(dxhat * xhat, axis=-1, keepdims=True))


def _sigmoid(x):
    return 1.0 / (1.0 + jnp.exp(-x))


def _params(sem=("arbitrary",)):
    return pltpu.CompilerParams(dimension_semantics=sem, vmem_limit_bytes=VMEM_LIMIT)


def _full(shape):
    return pl.BlockSpec(shape, lambda i: (0,) * len(shape))


def _const(shape):
    return pl.BlockSpec(shape, lambda i: (0,) * len(shape), pipeline_mode=pl.Buffered(1))


def _window_sums(ext, forward):
    n = ext.shape[0]
    outs = []
    for g in range(POOL_GROUPS):
        s = ext[:, g * POOL_GROUP_DIM:(g + 1) * POOL_GROUP_DIM]
        for k in range(g + 1):
            shift = (1 << k) if forward else n - (1 << k)
            s = s + pltpu.roll(s, shift, axis=0)
        outs.append(s[:n - POOL_HALO])
    return outs


def _inv_count(row0, tm):
    row = row0 + lax.broadcasted_iota(jnp.int32, (tm, 1), 0)
    return [1.0 / jnp.minimum(row + 1, 2 << g).astype(F32) for g in range(POOL_GROUPS)]


def _pool_mix(u, u_prev, row0, gw_ref, gb):
    tm = u.shape[0]
    sums = _window_sums(jnp.concatenate([u, u_prev], axis=0), True)
    inv = _inv_count(row0, tm)
    pooled, mixed = [], []
    for g in range(POOL_GROUPS):
        ug = u[:, g * POOL_GROUP_DIM:(g + 1) * POOL_GROUP_DIM]
        pg = (sums[g] * inv[g] - ug).astype(BF16)
        pooled.append(pg)
        mixed.append(_dot_nn(pg, gw_ref[g]))
    return pooled, jnp.concatenate(mixed, axis=1) + gb


def _pool_fwd_call(x, nw, w_in, gw, gb, sc, w_out, rider=None):
    seq = x.shape[0]
    tm = min(MATMUL_ROW_TILE, seq)
    nt = seq // tm

    def main(x_ref, nw_ref, win_ref, gw_ref, gb_ref, sc_ref, wout_ref, h_ref, y_ref, silu_ref, dsilu_ref,
             pooled_ref, mixed_ref, halo_ref):
        i = pl.program_id(0)

        @pl.when(i == 0)
        def _():
            halo_ref[...] = jnp.zeros_like(halo_ref)

        xt = x_ref[...]
        xhat, _ = _rms(xt)
        n = (xhat * nw_ref[...]).astype(BF16)
        p = _dot_nn(n, win_ref[...])
        u = p[:, :POOL_WIDTH]
        gate = p[:, POOL_WIDTH:]
        sg = _sigmoid(gate)
        silu = gate * sg
        silu_ref[...] = silu
        dsilu_ref[...] = sg * (1.0 + gate * (1.0 - sg))
        pooled, mixed = _pool_mix(u, halo_ref[...], i * tm, gw_ref, gb_ref[...])
        pooled_ref[...] = jnp.concatenate(pooled, axis=1)
        mixed_ref[...] = mixed
        halo_ref[...] = u[tm - POOL_HALO:, :]
        y = (mixed * sc_ref[...] * silu).astype(BF16)
        y_ref[...] = y
        h_ref[...] = xt + _dot_nn(y, wout_ref[...])

    def body(*refs):
        own, comm = _split_refs(refs, 7, 6, 1, rider)
        _ride_before(comm, pl.program_id(0), nt)
        main(*own)
        _ride_after(comm, pl.program_id(0), nt)

    return pl.pallas_call(
        body, name="pool_fwd", grid=(nt,),
        in_specs=_extend([pl.BlockSpec((tm, D_MODEL), lambda i: (i, 0)), _const((1, D_MODEL)),
                          _const((D_MODEL, 2 * POOL_WIDTH)), _const((POOL_GROUPS, POOL_GROUP_DIM, POOL_GROUP_DIM)),
                          _const((1, POOL_WIDTH)), _const((1, POOL_WIDTH)), _const((POOL_WIDTH, D_MODEL))],
                         rider, "in_specs"),
        out_specs=_extend([pl.BlockSpec((tm, D_MODEL), lambda i: (i, 0))] * 6, rider, "out_specs"),
        out_shape=_extend([jax.ShapeDtypeStruct((seq, D_MODEL), F32), jax.ShapeDtypeStruct((seq, POOL_WIDTH), BF16),
                           jax.ShapeDtypeStruct((seq, POOL_WIDTH), F32), jax.ShapeDtypeStruct((seq, POOL_WIDTH), F32),
                           jax.ShapeDtypeStruct((seq, POOL_WIDTH), BF16),
                           jax.ShapeDtypeStruct((seq, POOL_WIDTH), F32)], rider, "out_shape"),
        scratch_shapes=_extend([pltpu.VMEM((POOL_HALO, POOL_WIDTH), F32)], rider, "scratch"),
        compiler_params=_params(),
    )(x, nw, w_in, gw, gb, sc, w_out, *_extend([], rider, "arrays"))


def _pool_bwd_call(dh, y, silu, dsilu, pooled, mixed, gw, sc, w_out, rider=None):
    seq = dh.shape[0]
    tm = min(MATMUL_ROW_TILE, seq)
    nt = seq // tm

    def main(dh_ref, y_ref, silu_ref, dsilu_ref, pooled_ref, mixed_ref, gw_ref, sc_ref, wout_ref,
             dp_ref, dwout_hbm, dgw_hbm, dgb_ref, dsc_ref, carry_ref, dwout_acc, dgw_acc, dwout_stage, dgw_stage):
        i = pl.program_id(0)
        t = nt - 1 - i

        @pl.when(i == 0)
        def _():
            carry_ref[...] = jnp.zeros_like(carry_ref)
            dwout_acc[...] = jnp.zeros_like(dwout_acc)
            dgw_acc[...] = jnp.zeros_like(dgw_acc)
            dgb_ref[...] = jnp.zeros_like(dgb_ref)
            dsc_ref[...] = jnp.zeros_like(dsc_ref)

        silu = silu_ref[...]
        pooled = [pooled_ref[:, g * POOL_GROUP_DIM:(g + 1) * POOL_GROUP_DIM] for g in range(POOL_GROUPS)]
        sc = sc_ref[...]
        dhb = dh_ref[...].astype(BF16)
        dwout_acc[...] += _dot_tn(y_ref[...], dhb)
        dy = _dot_nt(dhb, wout_ref[...])
        dmixed = dy * sc * silu
        dy_mixed = dy * mixed_ref[...]
        dsc_ref[...] += jnp.sum(dy_mixed * silu, axis=0, keepdims=True)
        dgate = dy_mixed * sc * dsilu_ref[...]
        dgb_ref[...] += jnp.sum(dmixed, axis=0, keepdims=True)
        inv = _inv_count(t * tm, tm)
        dpooled, scaled = [], []
        for g in range(POOL_GROUPS):
            dmg = dmixed[:, g * POOL_GROUP_DIM:(g + 1) * POOL_GROUP_DIM].astype(BF16)
            dgw_acc[g] += _dot_tn(pooled[g], dmg)
            dpg = _dot_nt(dmg, gw_ref[g])
            dpooled.append(dpg)
            scaled.append(dpg * inv[g])
        r = jnp.concatenate(scaled, axis=1)
        sums = _window_sums(jnp.concatenate([r, carry_ref[...]], axis=0), False)
        carry_ref[...] = r[:POOL_HALO, :]
        du = jnp.concatenate([sums[g] - dpooled[g] for g in range(POOL_GROUPS)], axis=1)
        dp_ref[...] = jnp.concatenate([du, dgate], axis=1).astype(BF16)

        @pl.when(i == nt - 1)
        def _():
            dwout_stage[...] = dwout_acc[...].astype(BF16)
            dgw_stage[...] = dgw_acc[...].astype(BF16)
            pltpu.sync_copy(dwout_stage, dwout_hbm)
            pltpu.sync_copy(dgw_stage, dgw_hbm)

    def body(*refs):
        own, comm = _split_refs(refs, 9, 5, 5, rider)
        _ride_before(comm, pl.program_id(0), nt)
        main(*own)
        _ride_after(comm, pl.program_id(0), nt)

    rev = lambda i: (nt - 1 - i, 0)
    return pl.pallas_call(
        body, name="pool_bwd", grid=(nt,),
        in_specs=_extend([pl.BlockSpec((tm, D_MODEL), rev)] * 6
                         + [_const((POOL_GROUPS, POOL_GROUP_DIM, POOL_GROUP_DIM)), _const((1, POOL_WIDTH)),
                            _const((POOL_WIDTH, D_MODEL))], rider, "in_specs"),
        out_specs=_extend([pl.BlockSpec((tm, 2 * POOL_WIDTH), rev), pl.BlockSpec(memory_space=pl.ANY),
                           pl.BlockSpec(memory_space=pl.ANY), _full((1, POOL_WIDTH)), _full((1, POOL_WIDTH))],
                          rider, "out_specs"),
        out_shape=_extend([jax.ShapeDtypeStruct((seq, 2 * POOL_WIDTH), BF16),
                           jax.ShapeDtypeStruct((POOL_WIDTH, D_MODEL), BF16),
                           jax.ShapeDtypeStruct((POOL_GROUPS, POOL_GROUP_DIM, POOL_GROUP_DIM), BF16),
                           jax.ShapeDtypeStruct((1, POOL_WIDTH), F32), jax.ShapeDtypeStruct((1, POOL_WIDTH), F32)],
                          rider, "out_shape"),
        scratch_shapes=_extend([pltpu.VMEM((POOL_HALO, POOL_WIDTH), F32), pltpu.VMEM((POOL_WIDTH, D_MODEL), F32),
                                pltpu.VMEM((POOL_GROUPS, POOL_GROUP_DIM, POOL_GROUP_DIM), F32),
                                pltpu.VMEM((POOL_WIDTH, D_MODEL), BF16),
                                pltpu.VMEM((POOL_GROUPS, POOL_GROUP_DIM, POOL_GROUP_DIM), BF16)], rider, "scratch"),
        compiler_params=_params(),
    )(dh, y, silu, dsilu, pooled, mixed, gw, sc, w_out, *_extend([], rider, "arrays"))


def _rows_then_zeros(ref, lo, hi, rows):
    part = ref[lo:hi, :]
    return jnp.concatenate([part, jnp.zeros((rows - (hi - lo), part.shape[1]), part.dtype)], axis=0)


def _inproj_bwd_call(name, dproj, h_in, nw, w_in, dres, rider=None, transposed=False):
    seq = h_in.shape[0]
    width = dproj.shape[1]
    w_shape = tuple(w_in.shape)
    acc_shape = (width, D_MODEL) if transposed else w_shape
    whole = w_shape[0] // LANES * LANES
    tm = min(MATMUL_ROW_TILE, seq)
    nt = seq // tm
    lane_tiles = D_MODEL // LANES
    dw_shape = (w_shape[0] * lane_tiles, LANES) if transposed else w_shape
    pieces = [(lo, min(lo + ROW_MAJOR_PIECE, w_shape[0])) for lo in range(0, w_shape[0], ROW_MAJOR_PIECE)]

    def to_row_major(dw_acc, dw_stage, dw_lines):
        for lo, hi in pieces:
            for j in range(lane_tiles):
                dw_lines[pl.ds(j, hi - lo, stride=lane_tiles), :] = dw_acc[lo:hi, j * LANES:(j + 1) * LANES]
            dw_stage[lo * lane_tiles:hi * lane_tiles, :] = dw_lines[0:(hi - lo) * lane_tiles, :].astype(BF16)

    def main(dproj_ref, h_ref, nw_ref, win_ref, dres_ref, dh_ref, dw_hbm, dnw_ref, dw_acc, dw_stage, *dw_lines):
        i = pl.program_id(0)

        @pl.when(i == 0)
        def _():
            dw_acc[...] = jnp.zeros_like(dw_acc)
            dnw_ref[...] = jnp.zeros_like(dnw_ref)

        dpb = dproj_ref[...]
        if transposed:
            dn = _dot_nn(dpb[:, :whole], win_ref[0:whole, :])
            if whole < w_shape[0]:
                dn = dn + _dot_nn(dpb[:, whole:], _rows_then_zeros(win_ref, whole, w_shape[0], width - whole))
        else:
            dn = _dot_nt(dpb, win_ref[...])
        xhat, rstd = _rms(h_ref[...])
        nw_row = nw_ref[...]
        n = (xhat * nw_row).astype(BF16)
        dw_acc[...] += _dot_tn(dpb, n) if transposed else _dot_tn(n, dpb)
        dnw_ref[...] += jnp.sum(dn * xhat, axis=0, keepdims=True)
        dh_ref[...] = _rms_bwd(dn * nw_row, xhat, rstd) + dres_ref[...]

        @pl.when(i == nt - 1)
        def _():
            if transposed:
                to_row_major(dw_acc, dw_stage, *dw_lines)
            else:
                dw_stage[...] = dw_acc[...].astype(BF16)
            pltpu.sync_copy(dw_stage, dw_hbm)

    scratch = [pltpu.VMEM(acc_shape, F32), pltpu.VMEM(dw_shape, BF16)]
    if transposed:
        scratch.append(pltpu.VMEM((ROW_MAJOR_PIECE * lane_tiles, LANES), F32))

    def body(*refs):
        own, comm = _split_refs(refs, 5, 3, len(scratch), rider)
        _ride_before(comm, pl.program_id(0), nt)
        main(*own)
        _ride_after(comm, pl.program_id(0), nt)

    row = lambda i: (i, 0)
    return pl.pallas_call(
        body, name=name, grid=(nt,),
        in_specs=_extend([pl.BlockSpec((tm, width), row), pl.BlockSpec((tm, D_MODEL), row), _const((1, D_MODEL)),
                          _const(w_shape), pl.BlockSpec((tm, D_MODEL), row)], rider, "in_specs"),
        out_specs=_extend([pl.BlockSpec((tm, D_MODEL), row), pl.BlockSpec(memory_space=pl.ANY),
                           _full((1, D_MODEL))], rider, "out_specs"),
        out_shape=_extend([jax.ShapeDtypeStruct((seq, D_MODEL), F32), jax.ShapeDtypeStruct(dw_shape, BF16),
                           jax.ShapeDtypeStruct((1, D_MODEL), F32)], rider, "out_shape"),
        scratch_shapes=_extend(scratch, rider, "scratch"),
        compiler_params=_params(),
    )(dproj, h_in, nw, w_in, dres, *_extend([], rider, "arrays"))


def _chunk_scan(x, reverse):
    n = x.shape[0]
    pos = lax.broadcasted_iota(jnp.int32, (n, 1), 0) & (CHUNK - 1)
    k = 1
    while k < CHUNK:
        if reverse:
            x = x + jnp.where(pos < CHUNK - k, pltpu.roll(x, n - k, axis=0), 0.0)
        else:
            x = x + jnp.where(pos >= k, pltpu.roll(x, k, axis=0), 0.0)
        k *= 2
    return x


def _chunk_rows(j):
    return slice(j * CHUNK, (j + 1) * CHUNK)


def _kcols(h):
    return slice(h * GLA_HEAD_K, (h + 1) * GLA_HEAD_K)


def _vcols(h):
    return slice(h * GLA_HEAD_V, (h + 1) * GLA_HEAD_V)


def _chunk_masks(tm):
    idx_t = lax.broadcasted_iota(jnp.int32, (tm, tm), 0)
    idx_s = lax.broadcasted_iota(jnp.int32, (tm, tm), 1)
    same_chunk = (idx_t ^ idx_s) < CHUNK
    return same_chunk & (idx_t >= idx_s), same_chunk & (idx_t < idx_s)


class _GlaTerms:
    def __init__(self, kc, q, k, v, low_b, gkw_ref, gkb_ref, masks, saved=None):
        tm = q.shape[0]
        self.q = q * Q_SCALE
        self.k = k
        if saved is None:
            self.z = _dot_nn(low_b, gkw_ref[:, kc]) + gkb_ref[:, kc]
            log_g = (jnp.minimum(self.z, 0.0) - jnp.log(1.0 + jnp.exp(-jnp.abs(self.z)))) / GATE_NORMALIZER
            self.c = _chunk_scan(log_g, False)
        else:
            self.z, self.c = saved
        is_last = lax.broadcasted_iota(jnp.int32, (CHUNK, 1), 0) == CHUNK - 1
        self.c_last = [jnp.sum(jnp.where(is_last, self.c[_chunk_rows(j), :], 0.0), axis=0, keepdims=True)
                       for j in range(tm // CHUNK)]
        c_last_rows = jnp.concatenate([jnp.broadcast_to(r, (CHUNK, r.shape[1])) for r in self.c_last], axis=0)
        self.e_pos = jnp.exp(self.c)
        self.e_neg = jnp.exp(-self.c)
        self.e_rest = jnp.exp(c_last_rows - self.c)
        self.a_b = (self.q * self.e_pos).astype(BF16)
        self.b_b = (self.k * self.e_neg).astype(BF16)
        self.cn_b = (self.q * self.e_neg).astype(BF16)
        self.dp_b = (self.k * self.e_pos).astype(BF16)
        self.kd_b = (self.k * self.e_rest).astype(BF16)
        self.v_b = v.astype(BF16)
        self.lower, self.upper = masks

    def scores(self, kc=slice(None)):
        fwd = _dot_nt(self.a_b[:, kc], self.b_b[:, kc])
        bwd = _dot_nt(self.cn_b[:, kc], self.dp_b[:, kc])
        return jnp.where(self.lower, fwd, jnp.where(self.upper, bwd, 0.0)).astype(BF16)


def _gla_fwd_call(h1, nw, w_lines, gkw, gkb, hw, w_out, wf, target):
    seq = h1.shape[0]
    tm = min(GLA_FWD_ROW_TILE, seq)
    nt = seq // tm
    cpt = tm // CHUNK
    n_chunks = seq // CHUNK

    lane_tiles = D_MODEL // LANES
    pieces = [(lo, min(lo + WEIGHT_ROWS_PIECE, GLA_IN_WIDTH)) for lo in range(0, GLA_IN_WIDTH, WEIGHT_ROWS_PIECE)]

    def body(h_ref, nw_ref, lines_hbm, gkw_ref, gkb_ref, hw_ref, wout_ref, wf_ref, tgt_ref,
             dh2_ref, proj_ref, o_ref, st_ref, scores_ref, loss_ref, dwf_ref, win_hbm,
             state_ref, win_ref, piece_ref, lines_ref, piece_sem, win_sem):
        i = pl.program_id(0)
        win_copy = pltpu.make_async_copy(win_ref, win_hbm, win_sem)

        def piece_copy(k):
            lo, hi = pieces[k]
            n_lines = (hi - lo) * lane_tiles
            return pltpu.make_async_copy(lines_hbm.at[pl.ds(lo * lane_tiles, n_lines)],
                                         piece_ref.at[k % 2, pl.ds(0, n_lines)], piece_sem.at[k % 2])

        @pl.when(i == 0)
        def _():
            state_ref[...] = jnp.zeros_like(state_ref)
            loss_ref[...] = jnp.zeros_like(loss_ref)
            dwf_ref[...] = jnp.zeros_like(dwf_ref)
            piece_copy(0).start()
            for k, (lo, hi) in enumerate(pieces):
                if k + 1 < len(pieces):
                    piece_copy(k + 1).start()
                piece_copy(k).wait()
                n_lines = (hi - lo) * lane_tiles
                lines_ref[0:n_lines, :] = piece_ref[k % 2, 0:n_lines, :].astype(F32)
                for j in range(lane_tiles):
                    win_ref[lo:hi, j * LANES:(j + 1) * LANES] = lines_ref[pl.ds(j, hi - lo, stride=lane_tiles),
                                                                          :].astype(BF16)
            win_copy.start()

        pl.when(i == nt - 1)(win_copy.wait)

        ht = h_ref[...]
        xhat, _ = _rms(ht)
        n = (xhat * nw_ref[...]).astype(BF16)
        sections = {}
        for name, lo, hi in (("low", GLA_QKVG_WIDTH, GLA_IN_PAD), ("qk", 0, 2 * GLA_KEY_WIDTH),
                             ("v", 2 * GLA_KEY_WIDTH, GLA_QKVG_WIDTH - GLA_VALUE_WIDTH),
                             ("gate", GLA_QKVG_WIDTH - GLA_VALUE_WIDTH, GLA_QKVG_WIDTH)):
            rows = (win_ref[lo:hi, :] if hi <= GLA_IN_WIDTH
                    else _rows_then_zeros(win_ref, lo, GLA_IN_WIDTH, hi - lo))
            sections[name] = _dot_nt(n, rows)
            proj_ref[:, lo:hi] = sections[name]
        low_b = sections["low"].astype(BF16)
        masks = _chunk_masks(tm)
        on_heads = []
        for h in range(GLA_HEADS):
            kc, vc = _kcols(h), _vcols(h)
            g = _GlaTerms(kc, sections["qk"][:, kc], sections["qk"][:, GLA_KEY_WIDTH:][:, kc], sections["v"][:, vc],
                          low_b, gkw_ref, gkb_ref, masks)
            srows = slice(h * GLA_HEAD_V, (h + 1) * GLA_HEAD_V)
            scores = g.scores()
            for b in range(tm // ROW_TILE):
                part = slice(b * ROW_TILE, (b + 1) * ROW_TILE)
                scores_ref[part, h * ROW_TILE:(h + 1) * ROW_TILE] = scores[part, part]
            o_intra = _dot_nn(scores, g.v_b)
            state = state_ref[srows, :]
            o_rows = []
            for j in range(cpt):
                r = _chunk_rows(j)
                st_ref[j, srows, :] = state
                o_rows.append(o_intra[r] + _dot_nt(g.a_b[r], state.astype(BF16)))
                decay = jnp.exp(g.c_last[j])
                state = state * decay + _dot_tn(g.v_b[r], g.kd_b[r])
            state_ref[srows, :] = state
            o_head = jnp.concatenate(o_rows, axis=0)
            o_ref[:, vc] = o_head
            proj_ref[:, GLA_SAVED_Z + kc.start:GLA_SAVED_Z + kc.stop] = g.z
            proj_ref[:, GLA_SAVED_C + kc.start:GLA_SAVED_C + kc.stop] = g.c
            on_heads.append(_rms(o_head)[0])
        gate = sections["gate"]
        on = jnp.concatenate(on_heads, axis=1) * hw_ref[...]
        y = (on * (gate * _sigmoid(gate))).astype(BF16)
        h2 = ht + _dot_nn(y, wout_ref[...])
        xhat2, rstd2 = _rms(h2)
        wf_row = wf_ref[...]
        err = xhat2 * wf_row - tgt_ref[...]
        loss_ref[...] += 0.5 * jnp.sum(err * err) / D_MODEL
        dout = err * (1.0 / D_MODEL)
        dwf_ref[...] += jnp.sum(dout * xhat2, axis=0, keepdims=True)
        dh2_ref[...] = _rms_bwd(dout * wf_row, xhat2, rstd2)

    row = lambda i: (i, 0)
    return pl.pallas_call(
        body, name="gla_fwd", grid=(nt,),
        in_specs=[pl.BlockSpec((tm, D_MODEL), row), _const((1, D_MODEL)), pl.BlockSpec(memory_space=pl.ANY),
                  _const((GLA_LOW_PAD, GLA_KEY_WIDTH)), _const((1, GLA_KEY_WIDTH)), _const((1, GLA_VALUE_WIDTH)),
                  _const((GLA_VALUE_WIDTH, D_MODEL)), _const((1, D_MODEL)), pl.BlockSpec((tm, D_MODEL), row)],
        out_specs=[pl.BlockSpec((tm, D_MODEL), row), pl.BlockSpec((tm, GLA_SAVED_WIDTH), row),
                   pl.BlockSpec((tm, GLA_VALUE_WIDTH), row),
                   pl.BlockSpec((cpt, GLA_VALUE_WIDTH, GLA_HEAD_K), lambda i: (i, 0, 0)),
                   pl.BlockSpec((tm, GLA_HEADS * ROW_TILE), row), _full((8, LANES)), _full((1, D_MODEL)),
                   pl.BlockSpec(memory_space=pl.ANY)],
        out_shape=[jax.ShapeDtypeStruct((seq, D_MODEL), F32), jax.ShapeDtypeStruct((seq, GLA_SAVED_WIDTH), F32),
                   jax.ShapeDtypeStruct((seq, GLA_VALUE_WIDTH), F32),
                   jax.ShapeDtypeStruct((n_chunks, GLA_VALUE_WIDTH, GLA_HEAD_K), F32),
                   jax.ShapeDtypeStruct((seq, GLA_HEADS * ROW_TILE), BF16),
                   jax.ShapeDtypeStruct((8, LANES), F32), jax.ShapeDtypeStruct((1, D_MODEL), F32),
                   jax.ShapeDtypeStruct((GLA_IN_WIDTH, D_MODEL), BF16)],
        scratch_shapes=[pltpu.VMEM((GLA_VALUE_WIDTH, GLA_HEAD_K), F32), pltpu.VMEM((GLA_IN_WIDTH, D_MODEL), BF16),
                        pltpu.VMEM((2, WEIGHT_ROWS_PIECE * lane_tiles, LANES), BF16),
                        pltpu.VMEM((WEIGHT_ROWS_PIECE * lane_tiles, LANES), F32), pltpu.SemaphoreType.DMA((2,)),
                        pltpu.SemaphoreType.DMA(())],
        compiler_params=_params(),
    )(h1, nw, w_lines, gkw, gkb, hw, w_out, wf, target)


def _gla_bwd_call(dh2, proj, o, states, scores, gkw, gkb, hw, w_out):
    seq = dh2.shape[0]
    tm = ROW_TILE
    nt = seq // tm
    cpt = tm // CHUNK

    def body(dh_ref, proj_ref, o_ref, st_ref, scores_ref, gkw_ref, gkb_ref, hw_ref, wout_ref,
             dproj_ref, dwout_hbm, dhw_ref, dgkw_ref, dgkb_ref, dstate_ref, dwout_acc, dwout_stage):
        i = pl.program_id(0)

        @pl.when(i == 0)
        def _():
            dstate_ref[...] = jnp.zeros_like(dstate_ref)
            dwout_acc[...] = jnp.zeros_like(dwout_acc)
            dhw_ref[...] = jnp.zeros_like(dhw_ref)
            dgkw_ref[...] = jnp.zeros_like(dgkw_ref)
            dgkb_ref[...] = jnp.zeros_like(dgkb_ref)

        dhb = dh_ref[...].astype(BF16)
        dy = _dot_nt(dhb, wout_ref[...])
        v0, g0 = 2 * GLA_KEY_WIDTH, GLA_QKVG_WIDTH - GLA_VALUE_WIDTH
        gate = proj_ref[:, g0:GLA_QKVG_WIDTH]
        low_b = proj_ref[:, GLA_QKVG_WIDTH:GLA_IN_PAD].astype(BF16)
        o = o_ref[...]
        hw_row = hw_ref[...]
        sg = _sigmoid(gate)
        silu = gate * sg
        don = dy * silu
        on_parts, do_parts, dhw_parts = [], [], []
        for h in range(GLA_HEADS):
            vc = _vcols(h)
            xh, rs = _rms(o[:, vc])
            on_parts.append(xh * hw_row[:, vc])
            dhw_parts.append(jnp.sum(don[:, vc] * xh, axis=0, keepdims=True))
            do_parts.append(_rms_bwd(don[:, vc] * hw_row[:, vc], xh, rs).astype(BF16))
        on = jnp.concatenate(on_parts, axis=1)
        dwout_acc[...] += _dot_tn((on * silu).astype(BF16), dhb)
        dhw_ref[...] += jnp.concatenate(dhw_parts, axis=1)
        dproj_ref[:, g0:GLA_QKVG_WIDTH] = (dy * on * (sg * (1.0 + gate * (1.0 - sg)))).astype(BF16)

        last_row = lax.broadcasted_iota(jnp.int32, (CHUNK, 1), 0) == CHUNK - 1
        g = _GlaTerms(slice(0, GLA_KEY_WIDTH), proj_ref[:, :GLA_KEY_WIDTH], proj_ref[:, GLA_KEY_WIDTH:v0],
                      proj_ref[:, v0:g0], low_b, gkw_ref, gkb_ref, _chunk_masks(tm),
                      saved=(proj_ref[:, GLA_SAVED_Z:GLA_SAVED_C], proj_ref[:, GLA_SAVED_C:GLA_SAVED_WIDTH]))
        dc_h = []
        for h in range(GLA_HEADS):
            kc, vc = _kcols(h), _vcols(h)
            k_cols = slice(GLA_KEY_WIDTH + kc.start, GLA_KEY_WIDTH + kc.stop)
            v_cols = slice(v0 + vc.start, v0 + vc.stop)
            do_h = do_parts[h]
            srows = slice(h * GLA_HEAD_V, (h + 1) * GLA_HEAD_V)
            scores = scores_ref[:, h * ROW_TILE:(h + 1) * ROW_TILE]
            dscores = _dot_nt(do_h, g.v_b[:, vc])
            dfwd = jnp.where(g.lower, dscores, 0.0).astype(BF16)
            dbwd = jnp.where(g.upper, dscores, 0.0).astype(BF16)
            dv_intra = _dot_tn(scores, do_h)
            da_intra = _dot_nn(dfwd, g.b_b[:, kc])
            db = _dot_tn(dfwd, g.a_b[:, kc])
            dcn = _dot_nn(dbwd, g.dp_b[:, kc])
            ddp = _dot_tn(dbwd, g.cn_b[:, kc])
            dstate = dstate_ref[srows, :]
            da_rows, dkd_rows, dv_rows, dcl_rows = [None] * cpt, [None] * cpt, [None] * cpt, [None] * cpt
            for j in reversed(range(cpt)):
                r = _chunk_rows(j)
                state = st_ref[j, srows, :]
                dstate_b = dstate.astype(BF16)
                do_c = do_h[r]
                dv_rows[j] = dv_intra[r] + _dot_nt(g.kd_b[r, kc], dstate_b)
                da_rows[j] = da_intra[r] + _dot_nn(do_c, state.astype(BF16))
                dkd = _dot_nn(g.v_b[r, vc], dstate_b) * g.e_rest[r, kc]
                dkd_rows[j] = dkd
                decay = jnp.exp(g.c_last[j][:, kc])
                dc_last = (jnp.sum(dkd * g.k[r, kc], axis=0, keepdims=True)
                           + decay * jnp.sum(state * dstate, axis=0, keepdims=True))
                dcl_rows[j] = jnp.where(last_row, dc_last, 0.0)
                dstate = _dot_tn(do_c, g.a_b[r, kc]) + dstate * decay
            dstate_ref[srows, :] = dstate
            da = jnp.concatenate(da_rows, axis=0)
            dkd = jnp.concatenate(dkd_rows, axis=0)
            dproj_ref[:, v_cols] = jnp.concatenate(dv_rows, axis=0).astype(BF16)
            q_up, q_down = da * g.e_pos[:, kc], dcn * g.e_neg[:, kc]
            k_up, k_down = ddp * g.e_pos[:, kc], db * g.e_neg[:, kc] + dkd
            dproj_ref[:, kc] = (Q_SCALE * (q_up + q_down)).astype(BF16)
            dproj_ref[:, k_cols] = (k_up + k_down).astype(BF16)
            dc_h.append(g.q[:, kc] * (q_up - q_down) + g.k[:, kc] * (k_up - k_down)
                        + jnp.concatenate(dcl_rows, axis=0))
        dz = _chunk_scan(jnp.concatenate(dc_h, axis=1), True) * (1.0 / GATE_NORMALIZER) * (1.0 - _sigmoid(g.z))
        dzb = dz.astype(BF16)
        dgkb_ref[...] += jnp.sum(dz, axis=0, keepdims=True)
        dgkw_ref[...] += _dot_tn(low_b, dzb)
        dproj_ref[:, GLA_QKVG_WIDTH:] = _dot_nt(dzb, gkw_ref[...]).astype(BF16)

        @pl.when(i == nt - 1)
        def _():
            dwout_stage[...] = dwout_acc[...].astype(BF16)
            pltpu.sync_copy(dwout_stage, dwout_hbm)

    rev = lambda i: (nt - 1 - i, 0)
    return pl.pallas_call(
        body, name="gla_bwd", grid=(nt,),
        in_specs=[pl.BlockSpec((tm, D_MODEL), rev), pl.BlockSpec((tm, GLA_SAVED_WIDTH), rev),
                  pl.BlockSpec((tm, GLA_VALUE_WIDTH), rev),
                  pl.BlockSpec((cpt, GLA_VALUE_WIDTH, GLA_HEAD_K), lambda i: (nt - 1 - i, 0, 0)),
                  pl.BlockSpec((tm, GLA_HEADS * ROW_TILE), rev),
                  _const((GLA_LOW_PAD, GLA_KEY_WIDTH)), _const((1, GLA_KEY_WIDTH)), _const((1, GLA_VALUE_WIDTH)),
                  _const((GLA_VALUE_WIDTH, D_MODEL))],
        out_specs=[pl.BlockSpec((tm, GLA_IN_PAD), rev), pl.BlockSpec(memory_space=pl.ANY),
                   _full((1, GLA_VALUE_WIDTH)), _full((GLA_LOW_PAD, GLA_KEY_WIDTH)), _full((1, GLA_KEY_WIDTH))],
        out_shape=[jax.ShapeDtypeStruct((seq, GLA_IN_PAD), BF16), jax.ShapeDtypeStruct((GLA_VALUE_WIDTH, D_MODEL), BF16),
                   jax.ShapeDtypeStruct((1, GLA_VALUE_WIDTH), F32), jax.ShapeDtypeStruct((GLA_LOW_PAD, GLA_KEY_WIDTH), F32),
                   jax.ShapeDtypeStruct((1, GLA_KEY_WIDTH), F32)],
        scratch_shapes=[pltpu.VMEM((GLA_VALUE_WIDTH, GLA_HEAD_K), F32), pltpu.VMEM((GLA_VALUE_WIDTH, D_MODEL), F32),
                        pltpu.VMEM((GLA_VALUE_WIDTH, D_MODEL), BF16)],
        compiler_params=_params(),
    )(dh2, proj, o, states, scores, gkw, gkb, hw, w_out)


def _position():
    return lax.axis_index("x"), lax.axis_index("y"), lax.axis_index("c")


def _lead_slot(ref, d):
    return ref.at[d]


def _row_slot(rows):
    return lambda ref, d: ref.at[pl.ds(pl.multiple_of(d * rows, rows), rows)]


def _dim1_slot(size):
    return lambda ref, d: ref.at[:, pl.ds(pl.multiple_of(d * size, size), size)]


class _Gather:
    def __init__(self, in_refs, out_refs, slots, send_sems, recv_sems, local_sems):
        self.in_refs, self.out_refs, self.slots = in_refs, out_refs, slots
        self.send_sems, self.recv_sems, self.local_sems = send_sems, recv_sems, local_sems
        self.n = len(in_refs)
        x, y, c = _position()
        self.c = c
        self.me, self.sibling = (x, y, c), (x, y, 1 - c)
        self.near = [(1 - x, y), (x, 1 - y)]
        self.diagonal = (1 - x, 1 - y)
        self.relay_from = (x ^ c, y ^ (1 - c))
        self.relay_to = (x ^ (1 - c), y ^ c)

    def _copy(self, a, k, block, to, from_input=False):
        part = self.slots[a](self.out_refs[a], 4 * block[0] + 2 * block[1] + block[2])
        return pltpu.make_async_remote_copy(
            src_ref=self.in_refs[a] if from_input else part, dst_ref=part,
            send_sem=self.send_sems.at[a, k], recv_sem=self.recv_sems.at[a, k], device_id=to, device_id_type=MESH)

    def _mine(self):
        return [pltpu.make_async_copy(self.in_refs[a], self.slots[a](self.out_refs[a], 4 * self.me[0] + 2 * self.me[1]
                                                                    + self.me[2]), self.local_sems.at[a])
                for a in range(self.n)]

    def _first(self):
        first = [self._copy(a, 0, self.me, self.sibling, True) for a in range(self.n)]
        return first + [self._copy(a, 1 + j, self.me, (*chip, self.c), True)
                        for j, chip in enumerate(self.near) for a in range(self.n)]

    def _relayed(self):
        return [self._copy(a, 3, (*self.relay_from, self.c), (*self.relay_to, self.c)) for a in range(self.n)]

    def _passed(self, j):
        chip = self.near[j] if j < 2 else self.diagonal
        return [self._copy(a, 4 + j, (*chip, self.c), self.sibling) for a in range(self.n)]

    def start(self):
        for cp in self._mine() + self._first():
            cp.start()

    def forward(self):
        for j, chip in enumerate(self.near):
            for a in range(self.n):
                self._copy(a, 1 + j, (*chip, self.c), self.me).wait_recv()
        for cp in self._relayed() + self._passed(0) + self._passed(1):
            cp.start()

    def relay(self):
        pass

    def finish(self):
        for a in range(self.n):
            self._copy(a, 3, (*self.diagonal, self.c), self.me).wait_recv()
        for cp in self._passed(2):
            cp.start()
        for a in range(self.n):
            self._copy(a, 0, self.sibling, self.me).wait_recv()
        for j, chip in enumerate(self.near + [self.diagonal]):
            for a in range(self.n):
                self._copy(a, 4 + j, (*chip, 1 - self.c), self.me).wait_recv()
        for cp in self._first() + self._relayed() + self._passed(0) + self._passed(1) + self._passed(2):
            cp.wait_send()
        for cp in self._mine():
            cp.wait()


class _Exchange:
    def __init__(self, in_refs, out_refs, slots, send_sems, recv_sems, local_sems):
        self.in_refs, self.out_refs, self.slots = in_refs, out_refs, slots
        self.send_sems, self.recv_sems, self.local_sems = send_sems, recv_sems, local_sems
        self.n = len(in_refs)
        self.pos = _position()

    def _copies(self):
        x, y, c = self.pos
        me = 4 * x + 2 * y + c
        mine = [pltpu.make_async_copy(self.slots[a](self.in_refs[a], me), self.out_refs[a].at[me],
                                      self.local_sems.at[a]) for a in range(self.n)]
        remote = []
        for k in range(1, N_DEV):
            px, py, pc = x ^ (k >> 2), y ^ ((k >> 1) & 1), c ^ (k & 1)
            for a in range(self.n):
                remote.append(pltpu.make_async_remote_copy(
                    src_ref=self.slots[a](self.in_refs[a], 4 * px + 2 * py + pc), dst_ref=self.out_refs[a].at[me],
                    send_sem=self.send_sems.at[a, k - 1], recv_sem=self.recv_sems.at[a, k - 1],
                    device_id=(px, py, pc), device_id_type=MESH))
        return mine, remote

    def start(self):
        mine, remote = self._copies()
        for cp in mine + remote:
            cp.start()

    def forward(self):
        pass

    def relay(self):
        pass

    def finish(self):
        mine, remote = self._copies()
        for cp in remote:
            cp.wait_recv()
        for cp in remote:
            cp.wait_send()
        for cp in mine:
            cp.wait()


class _Rider:
    def __init__(self, kind, arrays, out_shapes, slots, scratch=None, forward_step=None):
        self.kind, self.arrays, self.slots = kind, list(arrays), slots
        self.n = len(self.arrays)
        hbm = pl.BlockSpec(memory_space=pl.ANY)
        self.in_specs = [hbm] * self.n
        self.out_specs = [hbm] * self.n
        self.out_shape = [jax.ShapeDtypeStruct(tuple(s), a.dtype) for s, a in zip(out_shapes, self.arrays)]
        self.scratch = scratch if scratch is not None else [
            pltpu.SemaphoreType.DMA((self.n, 7)), pltpu.SemaphoreType.DMA((self.n, 7)),
            pltpu.SemaphoreType.DMA((self.n,))]
        self.forward_step = forward_step
        self.relay_step = None

    def bind(self, in_refs, out_refs, scratch):
        return self.kind(in_refs, out_refs, self.slots, *scratch)


def _gather_rider(shards, full_shapes, slots, forward_step=None):
    return _Rider(_Gather, shards, full_shapes, slots, None, forward_step)


def _exchange_rider(sends, part_shapes, slots):
    return _Rider(_Exchange, sends, [(N_DEV,) + tuple(s) for s in part_shapes], slots)


def _split_refs(refs, n_in, n_out, n_scratch, rider):
    k = rider.n if rider is not None else 0
    ins, r_ins = refs[:n_in], refs[n_in:n_in + k]
    outs, r_outs = refs[n_in + k:n_in + k + n_out], refs[n_in + k + n_out:n_in + 2 * k + n_out]
    rest = refs[n_in + 2 * k + n_out:]
    scratch, r_scratch = rest[:n_scratch], rest[n_scratch:]
    comm = rider.bind(r_ins, r_outs, r_scratch) if rider is not None else None
    if comm is not None:
        comm.forward_step, comm.relay_step = rider.forward_step, rider.relay_step
    return ins + outs + scratch, comm


def _ride_before(comm, i, nt):
    if comm is not None:
        pl.when(i == 0)(comm.start)
        pl.when(i == (nt - 1 if comm.forward_step is None else min(comm.forward_step, nt - 1)))(comm.forward)
        pl.when(i == (nt - 1 if comm.relay_step is None else min(comm.relay_step, nt - 1)))(comm.relay)


def _ride_after(comm, i, nt):
    if comm is not None:
        pl.when(i == nt - 1)(comm.finish)


def _extend(specs, rider, field):
    return list(specs) + (getattr(rider, field) if rider is not None else [])


def _comm_call(name, rider, cast_from):
    order = sorted(cast_from)

    def body(*refs):
        n = rider.n
        ins, outs, rest = list(refs[:n]), refs[n:2 * n], refs[2 * n:]
        for stage, index in zip(rest[:len(order)], order):
            stage[...] = ins[index][...].astype(BF16)
            ins[index] = stage
        comm = rider.bind(ins, outs, rest[len(order):])
        comm.start()
        comm.forward()
        comm.relay()
        comm.finish()

    vmem = pl.BlockSpec(memory_space=pltpu.VMEM)
    return pl.pallas_call(
        body, name=name, in_specs=[vmem if i in cast_from else spec for i, spec in enumerate(rider.in_specs)],
        out_specs=rider.out_specs, out_shape=rider.out_shape,
        scratch_shapes=[pltpu.VMEM(cast_from[i].shape, BF16) for i in order] + rider.scratch,
        compiler_params=pltpu.CompilerParams(vmem_limit_bytes=VMEM_LIMIT),
    )(*[cast_from.get(i, a) for i, a in enumerate(rider.arrays)])


N_CHIPS = 4


class _TwoLevel:
    def __init__(self, in_refs, out_refs, slots, *scratch):
        self.in_refs, self.out_refs, self.slots = in_refs, out_refs, slots
        self.n = n = len(in_refs)
        self.own_bufs, self.recv_bufs, self.relay_bufs = scratch[:n], scratch[n:2 * n], scratch[2 * n:3 * n]
        self.swap_send, self.swap_recv, self.local_sems, self.chip_send, self.chip_recv = scratch[3 * n:]
        x, y, c = self.pos = _position()
        self.first = (x ^ (1 - c), y ^ c)
        self.second = (x ^ c, y ^ (1 - c))
        self.chip_index = lambda chip: 2 * chip[0] + chip[1]

    def _swap(self):
        x, y, c = self.pos
        return [pltpu.make_async_remote_copy(
            src_ref=self.slots[a](self.in_refs[a], 2 * q + 1 - c), dst_ref=self.recv_bufs[a].at[q],
            send_sem=self.swap_send.at[a, q], recv_sem=self.swap_recv.at[a, q],
            device_id=(x, y, 1 - c), device_id_type=MESH) for a in range(self.n) for q in range(N_CHIPS)]

    def _mine(self):
        c = self.pos[2]
        return [pltpu.make_async_copy(self.slots[a](self.in_refs[a], 2 * q + c), self.own_bufs[a].at[q],
                                      self.local_sems.at[a, q]) for a in range(self.n) for q in range(N_CHIPS)]

    def _to_chip(self, a, k, src, dst, chip):
        return pltpu.make_async_remote_copy(
            src_ref=src, dst_ref=dst, send_sem=self.chip_send.at[a, k], recv_sem=self.chip_recv.at[a, k],
            device_id=(*chip, self.pos[2]), device_id_type=MESH)

    def _first_wave(self):
        x, y, _ = self.pos
        diagonal = self.chip_index((1 - x, 1 - y))
        passed_on = [self._to_chip(a, 1, self.own_bufs[a].at[diagonal], self.relay_bufs[a], self.first)
                     for a in range(self.n)]
        return passed_on + [self._to_chip(a, 0, self.own_bufs[a].at[self.chip_index(self.first)],
                                          self.out_refs[a].at[1], self.first) for a in range(self.n)]

    def _second_wave(self):
        return [self._to_chip(a, 2, self.own_bufs[a].at[self.chip_index(self.second)], self.out_refs[a].at[2],
                              self.second) for a in range(self.n)]

    def _own(self):
        x, y, _ = self.pos
        return [pltpu.make_async_copy(self.own_bufs[a].at[2 * x + y], self.out_refs[a].at[0],
                                      self.local_sems.at[a, N_CHIPS]) for a in range(self.n)]

    def start(self):
        for cp in self._swap() + self._mine():
            cp.start()

    def forward(self):
        swap, mine = self._swap(), self._mine()
        for a in range(self.n):
            for q in range(N_CHIPS):
                mine[a * N_CHIPS + q].wait()
                swap[a * N_CHIPS + q].wait_recv()
                self.own_bufs[a][q] = (self.own_bufs[a][q].astype(F32)
                                       + self.recv_bufs[a][q].astype(F32)).astype(BF16)
        for cp in self._first_wave() + self._own():
            cp.start()

    def relay(self):
        second = self.chip_index(self.second)
        for a in range(self.n):
            self._to_chip(a, 1, self.relay_bufs[a], self.relay_bufs[a], self.first).wait_recv()
            self.own_bufs[a][second] = (self.own_bufs[a][second].astype(F32)
                                        + self.relay_bufs[a][...].astype(F32)).astype(BF16)
        for cp in self._second_wave():
            cp.start()

    def finish(self):
        for a in range(self.n):
            self._to_chip(a, 0, self.out_refs[a].at[1], self.out_refs[a].at[1], self.first).wait_recv()
            self._to_chip(a, 2, self.out_refs[a].at[2], self.out_refs[a].at[2], self.second).wait_recv()
        for cp in self._first_wave() + self._second_wave() + self._swap():
            cp.wait_send()
        for cp in self._own():
            cp.wait()


def _two_level_rider(sends, part_shapes, slots, forward_step=None, relay_step=None):
    n = len(sends)
    bufs = [pltpu.VMEM((N_CHIPS,) + tuple(s), a.dtype) for s, a in zip(part_shapes, sends)]
    relay_bufs = [pltpu.VMEM(tuple(s), a.dtype) for s, a in zip(part_shapes, sends)]
    scratch = bufs + bufs + relay_bufs + [
        pltpu.SemaphoreType.DMA((n, N_CHIPS)), pltpu.SemaphoreType.DMA((n, N_CHIPS)),
        pltpu.SemaphoreType.DMA((n, N_CHIPS + 1)), pltpu.SemaphoreType.DMA((n, 3)), pltpu.SemaphoreType.DMA((n, 3))]
    rider = _Rider(_TwoLevel, sends, [(3,) + tuple(s) for s in part_shapes], slots, scratch, forward_step)
    rider.relay_step = relay_step
    return rider


class _Joined:
    def __init__(self, first, second):
        self.first, self.second = first, second

    def start(self):
        self.first.start()
        self.second.start()

    def forward(self):
        self.first.forward()
        self.second.forward()

    def relay(self):
        self.first.relay()
        self.second.relay()

    def finish(self):
        self.first.finish()
        self.second.finish()


class _JoinedRider:
    def __init__(self, first, second):
        self.first, self.second = first, second
        self.n = first.n + second.n
        self.arrays = first.arrays + second.arrays
        self.in_specs = first.in_specs + second.in_specs
        self.out_specs = first.out_specs + second.out_specs
        self.out_shape = first.out_shape + second.out_shape
        self.scratch = first.scratch + second.scratch
        self.forward_step = first.forward_step
        self.relay_step = first.relay_step

    def bind(self, in_refs, out_refs, scratch):
        k, s = self.first.n, len(self.first.scratch)
        return _Joined(self.first.bind(in_refs[:k], out_refs[:k], scratch[:s]),
                       self.second.bind(in_refs[k:], out_refs[k:], scratch[s:]))


def _adamw(w, g, m, v):
    m = ADAM_B1 * m + (1.0 - ADAM_B1) * g
    v = ADAM_B2 * v + (1.0 - ADAM_B2) * (g * g)
    m_hat = m / (1.0 - ADAM_B1 ** ADAM_STEP)
    v_hat = v / (1.0 - ADAM_B2 ** ADAM_STEP)
    delta = -ADAM_LR * (m_hat / (jnp.sqrt(v_hat) + ADAM_EPS) + ADAM_WD * w)
    return delta, m, v


def _sum_parts(parts_ref, index=()):
    g = parts_ref[(0,) + index].astype(F32)
    for s in range(1, parts_ref.shape[0]):
        g = g + parts_ref[(s,) + index].astype(F32)
    return g


def _adamw_group_call(name, groups):
    k = len(groups)

    def body(*refs):
        ins, outs = refs[:4 * k], refs[4 * k:]
        for i in range(k):
            parts_ref, w_ref, m_ref, v_ref = ins[4 * i:4 * i + 4]
            g = _sum_parts(parts_ref)
            delta, m_new, v_new = _adamw(w_ref[...], g, m_ref[...], v_ref[...])
            for out_ref, value in zip(outs[4 * i:4 * i + 4], (g, delta, m_new, v_new)):
                out_ref[...] = value

    vmem = pl.BlockSpec(memory_space=pltpu.VMEM)
    res = pl.pallas_call(
        body, name=name, in_specs=[vmem] * (4 * k), out_specs=[vmem] * (4 * k),
        out_shape=[jax.ShapeDtypeStruct(grp[1].shape, F32) for grp in groups for _ in range(4)],
        compiler_params=pltpu.CompilerParams(vmem_limit_bytes=VMEM_LIMIT),
    )(*[a for grp in groups for a in grp])
    return [res[4 * i:4 * i + 4] for i in range(k)]


def _adamw_slabs_call(name, parts, w, m, v, rider=None):
    n_lines = w.shape[0]
    lines = pl.BlockSpec((n_lines, None, LANES), lambda: (0, 0, 0))

    def main(parts_ref, w_ref, m_ref, v_ref, g_ref, delta_ref, m_out, v_out):
        g = _sum_parts(parts_ref).reshape(n_lines, LANES)
        delta, m_new, v_new = _adamw(w_ref[...], g, m_ref[...], v_ref[...])
        g_ref[...] = g
        delta_ref[...] = delta
        m_out[...] = m_new
        v_out[...] = v_new

    def body(*refs):
        own, comm = _split_refs(refs, 4, 4, 0, rider)
        if comm is not None:
            comm.start()
        main(*own)
        if comm is not None:
            comm.forward()
            comm.relay()
            comm.finish()

    vmem = pl.BlockSpec(memory_space=pltpu.VMEM)
    return pl.pallas_call(
        body, name=name, in_specs=_extend([vmem] + [lines] * 3, rider, "in_specs"),
        out_specs=_extend([lines] * 4, rider, "out_specs"),
        out_shape=_extend([jax.ShapeDtypeStruct(w.shape, F32)] * 4, rider, "out_shape"),
        scratch_shapes=_extend([], rider, "scratch"),
        compiler_params=pltpu.CompilerParams(vmem_limit_bytes=VMEM_LIMIT),
    )(parts, w, m, v, *_extend([], rider, "arrays"))


WIDE_ROWS = 8
NARROW_ROWS = 40
NARROW_GKW_ROW = 8
NARROW_GKB_ROW = 24
NARROW_HW_ROW = 32
GROUP_SHARD = POOL_GROUP_DIM // N_DEV
KEY_SHARD = GLA_KEY_WIDTH // N_DEV
HEAD_V_SHARD = GLA_HEAD_V // N_DEV


def _small_adamw_call(wide, narrow, w, m, v):
    names = ("norm_w", "pool_scale", "final_norm_w", "pool_group_b", "gla_gk_w", "gla_gk_b", "gla_head_norm_w")
    where = {
        "norm_w": (0, slice(0, 2), slice(None)),
        "pool_scale": (0, slice(2, 3), slice(None)),
        "final_norm_w": (0, slice(3, 4), slice(None)),
        "pool_group_b": (1, slice(0, POOL_GROUPS), slice(0, GROUP_SHARD)),
        "gla_gk_w": (1, slice(NARROW_GKW_ROW, NARROW_GKW_ROW + GLA_GATE_RANK), slice(0, KEY_SHARD)),
        "gla_gk_b": (1, slice(NARROW_GKB_ROW, NARROW_GKB_ROW + 1), slice(0, KEY_SHARD)),
        "gla_head_norm_w": (1, slice(NARROW_HW_ROW, NARROW_HW_ROW + 1), slice(0, HEAD_V_SHARD)),
    }
    k = len(names)

    def body(*refs):
        parts = refs[0:2]
        w_refs, m_refs, v_refs = refs[2:2 + k], refs[2 + k:2 + 2 * k], refs[2 + 2 * k:2 + 3 * k]
        outs = refs[2 + 3 * k:]
        loss_ref = outs[0]
        loss_ref[...] = _sum_parts(parts[0], (slice(4, 5), slice(0, 1)))
        for i, name in enumerate(names):
            buf, rows, cols = where[name]
            g = _sum_parts(parts[buf], (rows, cols))
            delta, m_new, v_new = _adamw(w_refs[i][...], g, m_refs[i][...], v_refs[i][...])
            outs[1 + i][...] = g
            outs[1 + k + i][...] = delta
            outs[1 + 2 * k + i][...] = m_new
            outs[1 + 3 * k + i][...] = v_new

    vmem = pl.BlockSpec(memory_space=pltpu.VMEM)
    shapes = [jax.ShapeDtypeStruct(w[n].shape, F32) for n in names]
    res = pl.pallas_call(
        body, name="adamw_small", in_specs=[vmem] * (2 + 3 * k), out_specs=[vmem] * (1 + 4 * k),
        out_shape=[jax.ShapeDtypeStruct((1, 1), F32)] + shapes * 4,
    )(wide, narrow, *[w[n] for n in names], *[m[n] for n in names], *[v[n] for n in names])
    unzip = lambda j: dict(zip(names, res[1 + j * k:1 + (j + 1) * k]))
    return res[0], unzip(0), unzip(1), unzip(2), unzip(3)


def kernel(x, norm_w, pool_in_w, pool_group_w, pool_group_b, pool_scale, pool_out_w, gla_in_w, gla_gk_w, gla_gk_b, gla_head_norm_w, gla_out_w, final_norm_w, loss_target, m_norm_w, m_pool_in_w, m_pool_group_w, m_pool_group_b, m_pool_scale, m_pool_out_w, m_gla_in_w, m_gla_gk_w, m_gla_gk_b, m_gla_head_norm_w, m_gla_out_w, m_final_norm_w, v_norm_w, v_pool_in_w, v_pool_group_w, v_pool_group_b, v_pool_scale, v_pool_out_w, v_gla_in_w, v_gla_gk_w, v_gla_gk_b, v_gla_head_norm_w, v_gla_out_w, v_final_norm_w):
    w = dict(norm_w=norm_w, pool_in_w=pool_in_w, pool_group_w=pool_group_w, pool_group_b=pool_group_b,
             pool_scale=pool_scale, pool_out_w=pool_out_w, gla_in_w=gla_in_w, gla_gk_w=gla_gk_w, gla_gk_b=gla_gk_b,
             gla_head_norm_w=gla_head_norm_w, gla_out_w=gla_out_w, final_norm_w=final_norm_w)
    m = dict(norm_w=m_norm_w, pool_in_w=m_pool_in_w, pool_group_w=m_pool_group_w, pool_group_b=m_pool_group_b,
             pool_scale=m_pool_scale, pool_out_w=m_pool_out_w, gla_in_w=m_gla_in_w, gla_gk_w=m_gla_gk_w,
             gla_gk_b=m_gla_gk_b, gla_head_norm_w=m_gla_head_norm_w, gla_out_w=m_gla_out_w,
             final_norm_w=m_final_norm_w)
    v = dict(norm_w=v_norm_w, pool_in_w=v_pool_in_w, pool_group_w=v_pool_group_w, pool_group_b=v_pool_group_b,
             pool_scale=v_pool_scale, pool_out_w=v_pool_out_w, gla_in_w=v_gla_in_w, gla_gk_w=v_gla_gk_w,
             gla_gk_b=v_gla_gk_b, gla_head_norm_w=v_gla_head_norm_w, gla_out_w=v_gla_out_w,
             final_norm_w=v_final_norm_w)
    col_shard = GLA_IN_WIDTH // N_DEV
    row_shard = D_MODEL // N_DEV

    def lanes(a):
        return jnp.pad(a, [(0, 0)] * (a.ndim - 1) + [(0, LANES - a.shape[-1])])

    small_in = jnp.concatenate([lanes(pool_group_b[0]), lanes(gla_gk_b), lanes(gla_head_norm_w),
                                jnp.zeros((2, LANES), F32)], axis=0)
    in_cols = 2 * POOL_WIDTH // N_DEV
    pool_f32 = [pool_in_w[0], pool_group_w[0], pool_out_w[0]]
    pool_in, pool_gw, pool_out, small_all = _comm_call("pool_weights_all_gather", _gather_rider(
        [jax.ShapeDtypeStruct(a.shape, BF16) for a in pool_f32] + [small_in],
        [(D_MODEL, 2 * POOL_WIDTH), (POOL_GROUPS, POOL_GROUP_DIM, POOL_GROUP_DIM), (POOL_WIDTH, D_MODEL),
         (N_DEV, 8, LANES)],
        [_dim1_slot(in_cols), _dim1_slot(GROUP_SHARD), _row_slot(row_shard), _lead_slot]), dict(enumerate(pool_f32)))
    pool_gb = jnp.transpose(small_all[:, 0:POOL_GROUPS, :GROUP_SHARD], (1, 0, 2)).reshape(1, POOL_WIDTH)
    gla_gkb = small_all[:, POOL_GROUPS, :KEY_SHARD].reshape(1, GLA_KEY_WIDTH)
    gla_hw = jnp.tile(small_all[:, POOL_GROUPS + 1, :HEAD_V_SHARD].reshape(1, GLA_HEAD_V), (1, GLA_HEADS))
    nw0, nw1, wf = norm_w[0:1], norm_w[1:2], final_norm_w.reshape(1, D_MODEL)
    xs, target = x[0], loss_target[0]

    slabs = col_shard * D_MODEL // (BF16_ROWS * LANES)
    as_slabs = lambda t: jnp.transpose(t[0]).reshape(slabs, BF16_ROWS, LANES)
    h1, pool_y, pool_silu, pool_dsilu, pooled, mixed, gla_in_parts, gkw_parts, gla_out = _pool_fwd_call(
        xs, nw0, pool_in, pool_gw, pool_gb, pool_scale, pool_out, _gather_rider(
            [as_slabs(gla_in_w).astype(BF16), gla_gk_w[0].astype(BF16), gla_out_w[0].astype(BF16)],
            [(N_DEV, slabs, BF16_ROWS, LANES), (N_DEV, GLA_GATE_RANK, KEY_SHARD), (GLA_VALUE_WIDTH, D_MODEL)],
            [_lead_slot, _lead_slot, _row_slot(row_shard)], GATHER_RELAY_STEP))
    gla_gkw = jnp.pad(jnp.transpose(gkw_parts, (1, 0, 2)).reshape(GLA_GATE_RANK, GLA_KEY_WIDTH),
                      ((0, GLA_LOW_PAD - GLA_GATE_RANK), (0, 0)))
    dh2, proj, o, states, scores, loss_part, dwf, gla_in = _gla_fwd_call(
        h1, nw1, gla_in_parts.reshape(GLA_IN_WIDTH * D_MODEL // LANES, LANES), gla_gkw, gla_gkb, gla_hw, gla_out,
        wf, target)

    dproj, d_gla_out, dhw, dgkw, dgkb = _gla_bwd_call(dh2, proj, o, states, scores, gla_gkw, gla_gkb, gla_hw,
                                                      gla_out)
    dh1, d_gla_in, dnw1, landed_gla_out = _inproj_bwd_call(
        "gla_in_bwd", dproj, h1, nw1, gla_in, dh2,
        _exchange_rider([d_gla_out], [(row_shard, D_MODEL)], [_row_slot(row_shard)]), transposed=True)
    gla_in_send = d_gla_in.reshape(N_DEV, slabs, BF16_ROWS, LANES)
    dp, d_pool_out, dgw, dgb, dsc, landed_gla_in = _pool_bwd_call(
        dh1, pool_y, pool_silu, pool_dsilu, pooled, mixed, pool_gw, pool_scale, pool_out,
        _two_level_rider([gla_in_send], [(slabs, BF16_ROWS, LANES)], [_lead_slot], TWO_LEVEL_ADD_STEP,
                         TWO_LEVEL_RELAY_STEP))
    grad_x, d_pool_in, dnw0 = _inproj_bwd_call("pool_in_bwd", dp, xs, nw0, pool_in, dh1)

    wide = jnp.concatenate([
        dnw0, dnw1, dsc, dwf, jnp.pad(loss_part[0:1, 0:1], ((0, 0), (0, D_MODEL - 1))),
        jnp.zeros((WIDE_ROWS - 5, D_MODEL), F32)], axis=0)

    def rows8(a):
        return jnp.pad(lanes(a), ((0, 0), (0, -a.shape[1] % 8), (0, 0)))

    narrow = jnp.concatenate([
        rows8(jnp.transpose(dgb.reshape(POOL_GROUPS, N_DEV, GROUP_SHARD), (1, 0, 2))),
        rows8(jnp.transpose(dgkw[:GLA_GATE_RANK].reshape(GLA_GATE_RANK, N_DEV, KEY_SHARD), (1, 0, 2))),
        rows8(dgkb.reshape(N_DEV, 1, KEY_SHARD)),
        rows8(dhw.reshape(GLA_HEADS, GLA_HEAD_V).sum(axis=0).reshape(N_DEV, 1, HEAD_V_SHARD)),
    ], axis=1)
    last_exchange = _JoinedRider(
        _two_level_rider([d_pool_in, d_pool_out, dgw],
                         [(D_MODEL, in_cols), (row_shard, D_MODEL), (POOL_GROUPS, GROUP_SHARD, POOL_GROUP_DIM)],
                         [_dim1_slot(in_cols), _row_slot(row_shard), _dim1_slot(GROUP_SHARD)]),
        _exchange_rider([wide, narrow], [(WIDE_ROWS, D_MODEL), (NARROW_ROWS, LANES)],
                        [lambda ref, d: ref, _lead_slot]))

    res = {}
    as_lines = lambda t: jnp.transpose(t[0]).reshape(col_shard * D_MODEL // LANES, 1, LANES)
    *outs, landed_pool_in, landed_pool_out, landed_gw, landed_wide, landed_narrow = _adamw_slabs_call(
        "adamw_gla_in_w", landed_gla_in, as_lines(gla_in_w), as_lines(m_gla_in_w), as_lines(v_gla_in_w),
        last_exchange)
    res["gla_in_w"] = [jnp.transpose(t.reshape(col_shard, D_MODEL))[None] for t in outs]
    rest = [("pool_in_w", landed_pool_in, (D_MODEL, in_cols)),
            ("pool_group_w", landed_gw, (POOL_GROUPS * GROUP_SHARD, POOL_GROUP_DIM)),
            ("pool_out_w", landed_pool_out, (row_shard, D_MODEL)), ("gla_out_w", landed_gla_out, (row_shard, D_MODEL))]
    updates = _adamw_group_call("adamw_matrices", [
        (parts.reshape((parts.shape[0],) + shape), w[name].reshape(shape), m[name].reshape(shape),
         v[name].reshape(shape)) for name, parts, shape in rest])
    for (name, _, _), outs in zip(rest, updates):
        res[name] = [t.reshape(w[name].shape) for t in outs]
    small_shapes ={"norm_w": (2, D_MODEL), "pool_scale": (1, D_MODEL), "final_norm_w": (1, D_MODEL),
                    "pool_group_b": (POOL_GROUPS, GROUP_SHARD), "gla_gk_w": (GLA_GATE_RANK, KEY_SHARD),
                    "gla_gk_b": (1, KEY_SHARD), "gla_head_norm_w": (1, HEAD_V_SHARD)}
    as_small = lambda t: {n: t[n].reshape(s) for n, s in small_shapes.items()}
    loss, *small_outs = _small_adamw_call(landed_wide, landed_narrow, as_small(w), as_small(m), as_small(v))
    for name in small_shapes:
        res[name] = [t[name].reshape(w[name].shape) for t in small_outs]
    order = ("norm_w", "pool_in_w", "pool_group_w", "pool_group_b", "pool_scale", "pool_out_w", "gla_in_w",
             "gla_gk_w", "gla_gk_b", "gla_head_norm_w", "gla_out_w", "final_norm_w")
    return (loss.reshape(()), grad_x[None], *[res[n][0] for n in order], *[res[n][1] for n in order],
            *[res[n][2] for n in order], *[res[n][3] for n in order])
```

```python
import jax
import jax.numpy as jnp
from jax import lax
from jax.experimental import pallas as pl
from jax.experimental.pallas import tpu as pltpu

F32 = jnp.float32
BF16 = jnp.bfloat16
MESH = pl.DeviceIdType.MESH

N_DEV = 8
D_MODEL = 1024
POOL_WIDTH = 1024
POOL_GROUPS = 4
POOL_GROUP_DIM = 256
POOL_HALO = 16
GLA_HEADS = 4
GLA_HEAD_K = 128
GLA_HEAD_V = 256
GLA_KEY_WIDTH = 512
GLA_VALUE_WIDTH = 1024
GLA_GATE_RANK = 16
GLA_IN_WIDTH = 3088
GLA_IN_PAD = 3200
GLA_SAVED_Z = GLA_IN_PAD
GLA_SAVED_C = GLA_SAVED_Z + 512
GLA_SAVED_WIDTH = GLA_SAVED_C + 512
GLA_LOW_PAD = 128
GLA_QKVG_WIDTH = 3072
CHUNK = 64
GATE_NORMALIZER = 16.0
RMS_EPS = 1e-6
Q_SCALE = GLA_HEAD_K ** -0.5

ADAM_LR = 0.001
ADAM_B1 = 0.9
ADAM_B2 = 0.999
ADAM_EPS = 1e-08
ADAM_WD = 0.01
ADAM_STEP = 10

LANES = 128
BF16_ROWS = 16
VMEM_LIMIT = 60 * 1024 * 1024
ROW_TILE = 256
GLA_FWD_ROW_TILE = 512
MATMUL_ROW_TILE = 512
ROW_MAJOR_PIECE = 776
WEIGHT_ROWS_PIECE = 208
GATHER_RELAY_STEP = 5
TWO_LEVEL_ADD_STEP = 1
TWO_LEVEL_RELAY_STEP = 4


def _dot_nn(a, b):
    return lax.dot_general(a, b, (((1,), (0,)), ((), ())), preferred_element_type=F32)


def _dot_nt(a, b):
    return lax.dot_general(a, b, (((1,), (1,)), ((), ())), preferred_element_type=F32)


def _dot_tn(a, b):
    return lax.dot_general(a, b, (((0,), (0,)), ((), ())), preferred_element_type=F32)


def _rms(x):
    rstd = lax.rsqrt(jnp.mean(x * x, axis=-1, keepdims=True) + RMS_EPS)
    return x * rstd, rstd


def _rms_bwd(dxhat, xhat, rstd):
    return rstd * (dxhat - xhat * jnp.mean(dxhat * xhat, axis=-1, keepdims=True))


def _sigmoid(x):
    return 1.0 / (1.0 + jnp.exp(-x))


def _params(sem=("arbitrary",)):
    return pltpu.CompilerParams(dimension_semantics=sem, vmem_limit_bytes=VMEM_LIMIT)


def _full(shape):
    return pl.BlockSpec(shape, lambda i: (0,) * len(shape))


def _const(shape):
    return pl.BlockSpec(shape, lambda i: (0,) * len(shape), pipeline_mode=pl.Buffered(1))


def _window_sums(ext, forward):
    n = ext.shape[0]
    outs = []
    for g in range(POOL_GROUPS):
        s = ext[:, g * POOL_GROUP_DIM:(g + 1) * POOL_GROUP_DIM]
        for k in range(g + 1):
            shift = (1 << k) if forward else n - (1 << k)
            s = s + pltpu.roll(s, shift, axis=0)
        outs.append(s[:n - POOL_HALO])
    return outs


def _inv_count(row0, tm):
    row = row0 + lax.broadcasted_iota(jnp.int32, (tm, 1), 0)
    return [1.0 / jnp.minimum(row + 1, 2 << g).astype(F32) for g in range(POOL_GROUPS)]


def _pool_mix(u, u_prev, row0, gw_ref, gb):
    tm = u.shape[0]
    sums = _window_sums(jnp.concatenate([u, u_prev], axis=0), True)
    inv = _inv_count(row0, tm)
    pooled, mixed = [], []
    for g in range(POOL_GROUPS):
        ug = u[:, g * POOL_GROUP_DIM:(g + 1) * POOL_GROUP_DIM]
        pg = (sums[g] * inv[g] - ug).astype(BF16)
        pooled.append(pg)
        mixed.append(_dot_nn(pg, gw_ref[g]))
    return pooled, jnp.concatenate(mixed, axis=1) + gb


def _pool_fwd_call(x, nw, w_in, gw, gb, sc, w_out, rider=None):
    seq = x.shape[0]
    tm = min(MATMUL_ROW_TILE, seq)
    nt = seq // tm

    def main(x_ref, nw_ref, win_ref, gw_ref, gb_ref, sc_ref, wout_ref, h_ref, y_ref, silu_ref, dsilu_ref,
             pooled_ref, mixed_ref, halo_ref):
        i = pl.program_id(0)

        @pl.when(i == 0)
        def _():
            halo_ref[...] = jnp.zeros_like(halo_ref)

        xt = x_ref[...]
        xhat, _ = _rms(xt)
        n = (xhat * nw_ref[...]).astype(BF16)
        p = _dot_nn(n, win_ref[...])
        u = p[:, :POOL_WIDTH]
        gate = p[:, POOL_WIDTH:]
        sg = _sigmoid(gate)
        silu = gate * sg
        silu_ref[...] = silu
        dsilu_ref[...] = sg * (1.0 + gate * (1.0 - sg))
        pooled, mixed = _pool_mix(u, halo_ref[...], i * tm, gw_ref, gb_ref[...])
        pooled_ref[...] = jnp.concatenate(pooled, axis=1)
        mixed_ref[...] = mixed
        halo_ref[...] = u[tm - POOL_HALO:, :]
        y = (mixed * sc_ref[...] * silu).astype(BF16)
        y_ref[...] = y
        h_ref[...] = xt + _dot_nn(y, wout_ref[...])

    def body(*refs):
        own, comm = _split_refs(refs, 7, 6, 1, rider)
        _ride_before(comm, pl.program_id(0), nt)
        main(*own)
        _ride_after(comm, pl.program_id(0), nt)

    return pl.pallas_call(
        body, name="pool_fwd", grid=(nt,),
        in_specs=_extend([pl.BlockSpec((tm, D_MODEL), lambda i: (i, 0)), _const((1, D_MODEL)),
                          _const((D_MODEL, 2 * POOL_WIDTH)), _const((POOL_GROUPS, POOL_GROUP_DIM, POOL_GROUP_DIM)),
                          _const((1, POOL_WIDTH)), _const((1, POOL_WIDTH)), _const((POOL_WIDTH, D_MODEL))],
                         rider, "in_specs"),
        out_specs=_extend([pl.BlockSpec((tm, D_MODEL), lambda i: (i, 0))] * 6, rider, "out_specs"),
        out_shape=_extend([jax.ShapeDtypeStruct((seq, D_MODEL), F32), jax.ShapeDtypeStruct((seq, POOL_WIDTH), BF16),
                           jax.ShapeDtypeStruct((seq, POOL_WIDTH), F32), jax.ShapeDtypeStruct((seq, POOL_WIDTH), F32),
                           jax.ShapeDtypeStruct((seq, POOL_WIDTH), BF16),
                           jax.ShapeDtypeStruct((seq, POOL_WIDTH), F32)], rider, "out_shape"),
        scratch_shapes=_extend([pltpu.VMEM((POOL_HALO, POOL_WIDTH), F32)], rider, "scratch"),
        compiler_params=_params(),
    )(x, nw, w_in, gw, gb, sc, w_out, *_extend([], rider, "arrays"))


def _pool_bwd_call(dh, y, silu, dsilu, pooled, mixed, gw, sc, w_out, rider=None):
    seq = dh.shape[0]
    tm = min(MATMUL_ROW_TILE, seq)
    nt = seq // tm

    def main(dh_ref, y_ref, silu_ref, dsilu_ref, pooled_ref, mixed_ref, gw_ref, sc_ref, wout_ref,
             dp_ref, dwout_hbm, dgw_hbm, dgb_ref, dsc_ref, carry_ref, dwout_acc, dgw_acc, dwout_stage, dgw_stage):
        i = pl.program_id(0)
        t = nt - 1 - i

        @pl.when(i == 0)
        def _():
            carry_ref[...] = jnp.zeros_like(carry_ref)
            dwout_acc[...] = jnp.zeros_like(dwout_acc)
            dgw_acc[...] = jnp.zeros_like(dgw_acc)
            dgb_ref[...] = jnp.zeros_like(dgb_ref)
            dsc_ref[...] = jnp.zeros_like(dsc_ref)

        silu = silu_ref[...]
        pooled = [pooled_ref[:, g * POOL_GROUP_DIM:(g + 1) * POOL_GROUP_DIM] for g in range(POOL_GROUPS)]
        sc = sc_ref[...]
        dhb = dh_ref[...].astype(BF16)
        dwout_acc[...] += _dot_tn(y_ref[...], dhb)
        dy = _dot_nt(dhb, wout_ref[...])
        dmixed = dy * sc * silu
        dy_mixed = dy * mixed_ref[...]
        dsc_ref[...] += jnp.sum(dy_mixed * silu, axis=0, keepdims=True)
        dgate = dy_mixed * sc * dsilu_ref[...]
        dgb_ref[...] += jnp.sum(dmixed, axis=0, keepdims=True)
        inv = _inv_count(t * tm, tm)
        dpooled, scaled = [], []
        for g in range(POOL_GROUPS):
            dmg = dmixed[:, g * POOL_GROUP_DIM:(g + 1) * POOL_GROUP_DIM].astype(BF16)
            dgw_acc[g] += _dot_tn(pooled[g], dmg)
            dpg = _dot_nt(dmg, gw_ref[g])
            dpooled.append(dpg)
            scaled.append(dpg * inv[g])
        r = jnp.concatenate(scaled, axis=1)
        sums = _window_sums(jnp.concatenate([r, carry_ref[...]], axis=0), False)
        carry_ref[...] = r[:POOL_HALO, :]
        du = jnp.concatenate([sums[g] - dpooled[g] for g in range(POOL_GROUPS)], axis=1)
        dp_ref[...] = jnp.concatenate([du, dgate], axis=1).astype(BF16)

        @pl.when(i == nt - 1)
        def _():
            dwout_stage[...] = dwout_acc[...].astype(BF16)
            dgw_stage[...] = dgw_acc[...].astype(BF16)
            pltpu.sync_copy(dwout_stage, dwout_hbm)
            pltpu.sync_copy(dgw_stage, dgw_hbm)

    def body(*refs):
        own, comm = _split_refs(refs, 9, 5, 5, rider)
        _ride_before(comm, pl.program_id(0), nt)
        main(*own)
        _ride_after(comm, pl.program_id(0), nt)

    rev = lambda i: (nt - 1 - i, 0)
    return pl.pallas_call(
        body, name="pool_bwd", grid=(nt,),
        in_specs=_extend([pl.BlockSpec((tm, D_MODEL), rev)] * 6
                         + [_const((POOL_GROUPS, POOL_GROUP_DIM, POOL_GROUP_DIM)), _const((1, POOL_WIDTH)),
                            _const((POOL_WIDTH, D_MODEL))], rider, "in_specs"),
        out_specs=_extend([pl.BlockSpec((tm, 2 * POOL_WIDTH), rev), pl.BlockSpec(memory_space=pl.ANY),
                           pl.BlockSpec(memory_space=pl.ANY), _full((1, POOL_WIDTH)), _full((1, POOL_WIDTH))],
                          rider, "out_specs"),
        out_shape=_extend([jax.ShapeDtypeStruct((seq, 2 * POOL_WIDTH), BF16),
                           jax.ShapeDtypeStruct((POOL_WIDTH, D_MODEL), BF16),
                           jax.ShapeDtypeStruct((POOL_GROUPS, POOL_GROUP_DIM, POOL_GROUP_DIM), BF16),
                           jax.ShapeDtypeStruct((1, POOL_WIDTH), F32), jax.ShapeDtypeStruct((1, POOL_WIDTH), F32)],
                          rider, "out_shape"),
        scratch_shapes=_extend([pltpu.VMEM((POOL_HALO, POOL_WIDTH), F32), pltpu.VMEM((POOL_WIDTH, D_MODEL), F32),
                                pltpu.VMEM((POOL_GROUPS, POOL_GROUP_DIM, POOL_GROUP_DIM), F32),
                                pltpu.VMEM((POOL_WIDTH, D_MODEL), BF16),
                                pltpu.VMEM((POOL_GROUPS, POOL_GROUP_DIM, POOL_GROUP_DIM), BF16)], rider, "scratch"),
        compiler_params=_params(),
    )(dh, y, silu, dsilu, pooled, mixed, gw, sc, w_out, *_extend([], rider, "arrays"))


def _rows_then_zeros(ref, lo, hi, rows):
    part = ref[lo:hi, :]
    return jnp.concatenate([part, jnp.zeros((rows - (hi - lo), part.shape[1]), part.dtype)], axis=0)


def _inproj_bwd_call(name, dproj, h_in, nw, w_in, dres, rider=None, transposed=False):
    seq = h_in.shape[0]
    width = dproj.shape[1]
    w_shape = tuple(w_in.shape)
    acc_shape = (width, D_MODEL) if transposed else w_shape
    whole = w_shape[0] // LANES * LANES
    tm = min(MATMUL_ROW_TILE, seq)
    nt = seq // tm
    lane_tiles = D_MODEL // LANES
    dw_shape = (w_shape[0] * lane_tiles, LANES) if transposed else w_shape
    pieces = [(lo, min(lo + ROW_MAJOR_PIECE, w_shape[0])) for lo in range(0, w_shape[0], ROW_MAJOR_PIECE)]

    def to_row_major(dw_acc, dw_stage, dw_hbm, dw_lines, sems):
        copies = []
        for k, (lo, hi) in enumerate(pieces):
            for j in range(lane_tiles):
                dw_lines[pl.ds(j, hi - lo, stride=lane_tiles), :] = dw_acc[lo:hi, j * LANES:(j + 1) * LANES]
            rows = pl.ds(lo * lane_tiles, (hi - lo) * lane_tiles)
            dw_stage[lo * lane_tiles:hi * lane_tiles, :] = dw_lines[0:(hi - lo) * lane_tiles, :].astype(BF16)
            copies.append(pltpu.make_async_copy(dw_stage.at[rows], dw_hbm.at[rows], sems.at[k]))
            copies[-1].start()
        for copy in copies:
            copy.wait()

    def main(dproj_ref, h_ref, nw_ref, win_ref, dres_ref, dh_ref, dw_hbm, dnw_ref, dw_acc, dw_stage, *dw_lines):
        i = pl.program_id(0)

        @pl.when(i == 0)
        def _():
            dw_acc[...] = jnp.zeros_like(dw_acc)
            dnw_ref[...] = jnp.zeros_like(dnw_ref)

        dpb = dproj_ref[...]
        if transposed:
            dn = _dot_nn(dpb[:, :whole], win_ref[0:whole, :])
            if whole < w_shape[0]:
                dn = dn + _dot_nn(dpb[:, whole:], _rows_then_zeros(win_ref, whole, w_shape[0], width - whole))
        else:
            dn = _dot_nt(dpb, win_ref[...])
        xhat, rstd = _rms(h_ref[...])
        nw_row = nw_ref[...]
        n = (xhat * nw_row).astype(BF16)
        dw_acc[...] += _dot_tn(dpb, n) if transposed else _dot_tn(n, dpb)
        dnw_ref[...] += jnp.sum(dn * xhat, axis=0, keepdims=True)
        dh_ref[...] = _rms_bwd(dn * nw_row, xhat, rstd) + dres_ref[...]

        @pl.when(i == nt - 1)
        def _():
            if transposed:
                to_row_major(dw_acc, dw_stage, dw_hbm, *dw_lines)
            else:
                dw_stage[...] = dw_acc[...].astype(BF16)
                pltpu.sync_copy(dw_stage, dw_hbm)

    scratch = [pltpu.VMEM(acc_shape, F32), pltpu.VMEM(dw_shape, BF16)]
    if transposed:
        scratch += [pltpu.VMEM((ROW_MAJOR_PIECE * lane_tiles, LANES), F32), pltpu.SemaphoreType.DMA((len(pieces),))]

    def body(*refs):
        own, comm = _split_refs(refs, 5, 3, len(scratch), rider)
        _ride_before(comm, pl.program_id(0), nt)
        main(*own)
        _ride_after(comm, pl.program_id(0), nt)

    row = lambda i: (i, 0)
    return pl.pallas_call(
        body, name=name, grid=(nt,),
        in_specs=_extend([pl.BlockSpec((tm, width), row), pl.BlockSpec((tm, D_MODEL), row), _const((1, D_MODEL)),
                          _const(w_shape), pl.BlockSpec((tm, D_MODEL), row)], rider, "in_specs"),
        out_specs=_extend([pl.BlockSpec((tm, D_MODEL), row), pl.BlockSpec(memory_space=pl.ANY),
                           _full((1, D_MODEL))], rider, "out_specs"),
        out_shape=_extend([jax.ShapeDtypeStruct((seq, D_MODEL), F32), jax.ShapeDtypeStruct(dw_shape, BF16),
                           jax.ShapeDtypeStruct((1, D_MODEL), F32)], rider, "out_shape"),
        scratch_shapes=_extend(scratch, rider, "scratch"),
        compiler_params=_params(),
    )(dproj, h_in, nw, w_in, dres, *_extend([], rider, "arrays"))


def _chunk_scan(x, reverse):
    n = x.shape[0]
    pos = lax.broadcasted_iota(jnp.int32, (n, 1), 0) & (CHUNK - 1)
    k = 1
    while k < CHUNK:
        if reverse:
            x = x + jnp.where(pos < CHUNK - k, pltpu.roll(x, n - k, axis=0), 0.0)
        else:
            x = x + jnp.where(pos >= k, pltpu.roll(x, k, axis=0), 0.0)
        k *= 2
    return x


def _chunk_rows(j):
    return slice(j * CHUNK, (j + 1) * CHUNK)


def _kcols(h):
    return slice(h * GLA_HEAD_K, (h + 1) * GLA_HEAD_K)


def _vcols(h):
    return slice(h * GLA_HEAD_V, (h + 1) * GLA_HEAD_V)


def _chunk_masks(tm):
    idx_t = lax.broadcasted_iota(jnp.int32, (tm, tm), 0)
    idx_s = lax.broadcasted_iota(jnp.int32, (tm, tm), 1)
    same_chunk = (idx_t ^ idx_s) < CHUNK
    return same_chunk & (idx_t >= idx_s), same_chunk & (idx_t < idx_s)


class _GlaTerms:
    def __init__(self, kc, q, k, v, low_b, gkw_ref, gkb_ref, masks, saved=None):
        tm = q.shape[0]
        self.q = q * Q_SCALE
        self.k = k
        if saved is None:
            self.z = _dot_nn(low_b, gkw_ref[:, kc]) + gkb_ref[:, kc]
            log_g = (jnp.minimum(self.z, 0.0) - jnp.log(1.0 + jnp.exp(-jnp.abs(self.z)))) / GATE_NORMALIZER
            self.c = _chunk_scan(log_g, False)
        else:
            self.z, self.c = saved
        is_last = lax.broadcasted_iota(jnp.int32, (CHUNK, 1), 0) == CHUNK - 1
        self.c_last = [jnp.sum(jnp.where(is_last, self.c[_chunk_rows(j), :], 0.0), axis=0, keepdims=True)
                       for j in range(tm // CHUNK)]
        c_last_rows = jnp.concatenate([jnp.broadcast_to(r, (CHUNK, r.shape[1])) for r in self.c_last], axis=0)
        self.e_pos = jnp.exp(self.c)
        self.e_neg = jnp.exp(-self.c)
        self.e_rest = jnp.exp(c_last_rows - self.c)
        self.a_b = (self.q * self.e_pos).astype(BF16)
        self.b_b = (self.k * self.e_neg).astype(BF16)
        self.cn_b = (self.q * self.e_neg).astype(BF16)
        self.dp_b = (self.k * self.e_pos).astype(BF16)
        self.kd_b = (self.k * self.e_rest).astype(BF16)
        self.v_b = v.astype(BF16)
        self.lower, self.upper = masks

    def scores(self, kc=slice(None)):
        fwd = _dot_nt(self.a_b[:, kc], self.b_b[:, kc])
        bwd = _dot_nt(self.cn_b[:, kc], self.dp_b[:, kc])
        return jnp.where(self.lower, fwd, jnp.where(self.upper, bwd, 0.0)).astype(BF16)


def _gla_fwd_call(h1, nw, w_lines, gkw, gkb, hw, w_out, wf, target):
    seq = h1.shape[0]
    tm = min(GLA_FWD_ROW_TILE, seq)
    nt = seq // tm
    cpt = tm // CHUNK
    n_chunks = seq // CHUNK

    lane_tiles = D_MODEL // LANES
    pieces = [(lo, min(lo + WEIGHT_ROWS_PIECE, GLA_IN_WIDTH)) for lo in range(0, GLA_IN_WIDTH, WEIGHT_ROWS_PIECE)]

    def body(h_ref, nw_ref, lines_hbm, gkw_ref, gkb_ref, hw_ref, wout_ref, wf_ref, tgt_ref,
             dh2_ref, proj_ref, o_ref, st_ref, scores_ref, loss_ref, dwf_ref, win_hbm,
             state_ref, win_ref, piece_ref, lines_ref, piece_sem, win_sem):
        i = pl.program_id(0)
        win_copy = pltpu.make_async_copy(win_ref, win_hbm, win_sem)

        def piece_copy(k):
            lo, hi = pieces[k]
            n_lines = (hi - lo) * lane_tiles
            return pltpu.make_async_copy(lines_hbm.at[pl.ds(lo * lane_tiles, n_lines)],
                                         piece_ref.at[k % 2, pl.ds(0, n_lines)], piece_sem.at[k % 2])

        @pl.when(i == 0)
        def _():
            state_ref[...] = jnp.zeros_like(state_ref)
            loss_ref[...] = jnp.zeros_like(loss_ref)
            dwf_ref[...] = jnp.zeros_like(dwf_ref)
            piece_copy(0).start()
            for k, (lo, hi) in enumerate(pieces):
                if k + 1 < len(pieces):
                    piece_copy(k + 1).start()
                piece_copy(k).wait()
                n_lines = (hi - lo) * lane_tiles
                lines_ref[0:n_lines, :] = piece_ref[k % 2, 0:n_lines, :].astype(F32)
                for j in range(lane_tiles):
                    win_ref[lo:hi, j * LANES:(j + 1) * LANES] = lines_ref[pl.ds(j, hi - lo, stride=lane_tiles),
                                                                          :].astype(BF16)
            win_copy.start()

        pl.when(i == nt - 1)(win_copy.wait)

        ht = h_ref[...]
        xhat, _ = _rms(ht)
        n = (xhat * nw_ref[...]).astype(BF16)
        sections = {}
        for name, lo, hi in (("low", GLA_QKVG_WIDTH, GLA_IN_PAD), ("qk", 0, 2 * GLA_KEY_WIDTH),
                             ("v", 2 * GLA_KEY_WIDTH, GLA_QKVG_WIDTH - GLA_VALUE_WIDTH),
                             ("gate", GLA_QKVG_WIDTH - GLA_VALUE_WIDTH, GLA_QKVG_WIDTH)):
            rows = (win_ref[lo:hi, :] if hi <= GLA_IN_WIDTH
                    else _rows_then_zeros(win_ref, lo, GLA_IN_WIDTH, hi - lo))
            sections[name] = _dot_nt(n, rows)
            proj_ref[:, lo:hi] = sections[name]
        low_b = sections["low"].astype(BF16)
        masks = _chunk_masks(tm)
        on_heads = []
        for h in range(GLA_HEADS):
            kc, vc = _kcols(h), _vcols(h)
            g = _GlaTerms(kc, sections["qk"][:, kc], sections["qk"][:, GLA_KEY_WIDTH:][:, kc], sections["v"][:, vc],
                          low_b, gkw_ref, gkb_ref, masks)
            srows = slice(h * GLA_HEAD_V, (h + 1) * GLA_HEAD_V)
            scores = g.scores()
            for b in range(tm // ROW_TILE):
                part = slice(b * ROW_TILE, (b + 1) * ROW_TILE)
                scores_ref[part, h * ROW_TILE:(h + 1) * ROW_TILE] = scores[part, part]
            o_intra = _dot_nn(scores, g.v_b)
            state = state_ref[srows, :]
            o_rows = []
            for j in range(cpt):
                r = _chunk_rows(j)
                st_ref[j, srows, :] = state
                o_rows.append(o_intra[r] + _dot_nt(g.a_b[r], state.astype(BF16)))
                decay = jnp.exp(g.c_last[j])
                state = state * decay + _dot_tn(g.v_b[r], g.kd_b[r])
            state_ref[srows, :] = state
            o_head = jnp.concatenate(o_rows, axis=0)
            o_ref[:, vc] = o_head
            proj_ref[:, GLA_SAVED_Z + kc.start:GLA_SAVED_Z + kc.stop] = g.z
            proj_ref[:, GLA_SAVED_C + kc.start:GLA_SAVED_C + kc.stop] = g.c
            on_heads.append(_rms(o_head)[0])
        gate = sections["gate"]
        on = jnp.concatenate(on_heads, axis=1) * hw_ref[...]
        y = (on * (gate * _sigmoid(gate))).astype(BF16)
        h2 = ht + _dot_nn(y, wout_ref[...])
        xhat2, rstd2 = _rms(h2)
        wf_row = wf_ref[...]
        err = xhat2 * wf_row - tgt_ref[...]
        loss_ref[...] += 0.5 * jnp.sum(err * err) / D_MODEL
        dout = err * (1.0 / D_MODEL)
        dwf_ref[...] += jnp.sum(dout * xhat2, axis=0, keepdims=True)
        dh2_ref[...] = _rms_bwd(dout * wf_row, xhat2, rstd2)

    row = lambda i: (i, 0)
    return pl.pallas_call(
        body, name="gla_fwd", grid=(nt,),
        in_specs=[pl.BlockSpec((tm, D_MODEL), row), _const((1, D_MODEL)), pl.BlockSpec(memory_space=pl.ANY),
                  _const((GLA_LOW_PAD, GLA_KEY_WIDTH)), _const((1, GLA_KEY_WIDTH)), _const((1, GLA_VALUE_WIDTH)),
                  _const((GLA_VALUE_WIDTH, D_MODEL)), _const((1, D_MODEL)), pl.BlockSpec((tm, D_MODEL), row)],
        out_specs=[pl.BlockSpec((tm, D_MODEL), row), pl.BlockSpec((tm, GLA_SAVED_WIDTH), row),
                   pl.BlockSpec((tm, GLA_VALUE_WIDTH), row),
                   pl.BlockSpec((cpt, GLA_VALUE_WIDTH, GLA_HEAD_K), lambda i: (i, 0, 0)),
                   pl.BlockSpec((tm, GLA_HEADS * ROW_TILE), row), _full((8, LANES)), _full((1, D_MODEL)),
                   pl.BlockSpec(memory_space=pl.ANY)],
        out_shape=[jax.ShapeDtypeStruct((seq, D_MODEL), F32), jax.ShapeDtypeStruct((seq, GLA_SAVED_WIDTH), F32),
                   jax.ShapeDtypeStruct((seq, GLA_VALUE_WIDTH), F32),
                   jax.ShapeDtypeStruct((n_chunks, GLA_VALUE_WIDTH, GLA_HEAD_K), F32),
                   jax.ShapeDtypeStruct((seq, GLA_HEADS * ROW_TILE), BF16),
                   jax.ShapeDtypeStruct((8, LANES), F32), jax.ShapeDtypeStruct((1, D_MODEL), F32),
                   jax.ShapeDtypeStruct((GLA_IN_WIDTH, D_MODEL), BF16)],
        scratch_shapes=[pltpu.VMEM((GLA_VALUE_WIDTH, GLA_HEAD_K), F32), pltpu.VMEM((GLA_IN_WIDTH, D_MODEL), BF16),
                        pltpu.VMEM((2, WEIGHT_ROWS_PIECE * lane_tiles, LANES), BF16),
                        pltpu.VMEM((WEIGHT_ROWS_PIECE * lane_tiles, LANES), F32), pltpu.SemaphoreType.DMA((2,)),
                        pltpu.SemaphoreType.DMA(())],
        compiler_params=_params(),
    )(h1, nw, w_lines, gkw, gkb, hw, w_out, wf, target)


def _gla_bwd_call(dh2, proj, o, states, scores, gkw, gkb, hw, w_out):
    seq = dh2.shape[0]
    tm = ROW_TILE
    nt = seq // tm
    cpt = tm // CHUNK

    def body(dh_ref, proj_ref, o_ref, st_ref, scores_ref, gkw_ref, gkb_ref, hw_ref, wout_ref,
             dproj_ref, dwout_hbm, dhw_ref, dgkw_ref, dgkb_ref, dstate_ref, dwout_acc, dwout_stage):
        i = pl.program_id(0)

        @pl.when(i == 0)
        def _():
            dstate_ref[...] = jnp.zeros_like(dstate_ref)
            dwout_acc[...] = jnp.zeros_like(dwout_acc)
            dhw_ref[...] = jnp.zeros_like(dhw_ref)
            dgkw_ref[...] = jnp.zeros_like(dgkw_ref)
            dgkb_ref[...] = jnp.zeros_like(dgkb_ref)

        dhb = dh_ref[...].astype(BF16)
        dy = _dot_nt(dhb, wout_ref[...])
        v0, g0 = 2 * GLA_KEY_WIDTH, GLA_QKVG_WIDTH - GLA_VALUE_WIDTH
        gate = proj_ref[:, g0:GLA_QKVG_WIDTH]
        low_b = proj_ref[:, GLA_QKVG_WIDTH:GLA_IN_PAD].astype(BF16)
        o = o_ref[...]
        hw_row = hw_ref[...]
        sg = _sigmoid(gate)
        silu = gate * sg
        don = dy * silu
        on_parts, do_parts, dhw_parts = [], [], []
        for h in range(GLA_HEADS):
            vc = _vcols(h)
            xh, rs = _rms(o[:, vc])
            on_parts.append(xh * hw_row[:, vc])
            dhw_parts.append(jnp.sum(don[:, vc] * xh, axis=0, keepdims=True))
            do_parts.append(_rms_bwd(don[:, vc] * hw_row[:, vc], xh, rs).astype(BF16))
        on = jnp.concatenate(on_parts, axis=1)
        dwout_acc[...] += _dot_tn((on * silu).astype(BF16), dhb)
        dhw_ref[...] += jnp.concatenate(dhw_parts, axis=1)
        dproj_ref[:, g0:GLA_QKVG_WIDTH] = (dy * on * (sg * (1.0 + gate * (1.0 - sg)))).astype(BF16)

        last_row = lax.broadcasted_iota(jnp.int32, (CHUNK, 1), 0) == CHUNK - 1
        g = _GlaTerms(slice(0, GLA_KEY_WIDTH), proj_ref[:, :GLA_KEY_WIDTH], proj_ref[:, GLA_KEY_WIDTH:v0],
                      proj_ref[:, v0:g0], low_b, gkw_ref, gkb_ref, _chunk_masks(tm),
                      saved=(proj_ref[:, GLA_SAVED_Z:GLA_SAVED_C], proj_ref[:, GLA_SAVED_C:GLA_SAVED_WIDTH]))
        dc_h = []
        for h in range(GLA_HEADS):
            kc, vc = _kcols(h), _vcols(h)
            k_cols = slice(GLA_KEY_WIDTH + kc.start, GLA_KEY_WIDTH + kc.stop)
            v_cols = slice(v0 + vc.start, v0 + vc.stop)
            do_h = do_parts[h]
            srows = slice(h * GLA_HEAD_V, (h + 1) * GLA_HEAD_V)
            scores = scores_ref[:, h * ROW_TILE:(h + 1) * ROW_TILE]
            dscores = _dot_nt(do_h, g.v_b[:, vc])
            dfwd = jnp.where(g.lower, dscores, 0.0).astype(BF16)
            dbwd = jnp.where(g.upper, dscores, 0.0).astype(BF16)
            dv_intra = _dot_tn(scores, do_h)
            da_intra = _dot_nn(dfwd, g.b_b[:, kc])
            db = _dot_tn(dfwd, g.a_b[:, kc])
            dcn = _dot_nn(dbwd, g.dp_b[:, kc])
            ddp = _dot_tn(dbwd, g.cn_b[:, kc])
            dstate = dstate_ref[srows, :]
            da_rows, dkd_rows, dv_rows, dcl_rows = [None] * cpt, [None] * cpt, [None] * cpt, [None] * cpt
            for j in reversed(range(cpt)):
                r = _chunk_rows(j)
                state = st_ref[j, srows, :]
                dstate_b = dstate.astype(BF16)
                do_c = do_h[r]
                dv_rows[j] = dv_intra[r] + _dot_nt(g.kd_b[r, kc], dstate_b)
                da_rows[j] = da_intra[r] + _dot_nn(do_c, state.astype(BF16))
                dkd = _dot_nn(g.v_b[r, vc], dstate_b) * g.e_rest[r, kc]
                dkd_rows[j] = dkd
                decay = jnp.exp(g.c_last[j][:, kc])
                dc_last = (jnp.sum(dkd * g.k[r, kc], axis=0, keepdims=True)
                           + decay * jnp.sum(state * dstate, axis=0, keepdims=True))
                dcl_rows[j] = jnp.where(last_row, dc_last, 0.0)
                dstate = _dot_tn(do_c, g.a_b[r, kc]) + dstate * decay
            dstate_ref[srows, :] = dstate
            da = jnp.concatenate(da_rows, axis=0)
            dkd = jnp.concatenate(dkd_rows, axis=0)
            dproj_ref[:, v_cols] = jnp.concatenate(dv_rows, axis=0).astype(BF16)
            q_up, q_down = da * g.e_pos[:, kc], dcn * g.e_neg[:, kc]
            k_up, k_down = ddp * g.e_pos[:, kc], db * g.e_neg[:, kc] + dkd
            dproj_ref[:, kc] = (Q_SCALE * (q_up + q_down)).astype(BF16)
            dproj_ref[:, k_cols] = (k_up + k_down).astype(BF16)
            dc_h.append(g.q[:, kc] * (q_up - q_down) + g.k[:, kc] * (k_up - k_down)
                        + jnp.concatenate(dcl_rows, axis=0))
        dz = _chunk_scan(jnp.concatenate(dc_h, axis=1), True) * (1.0 / GATE_NORMALIZER) * (1.0 - _sigmoid(g.z))
        dzb = dz.astype(BF16)
        dgkb_ref[...] += jnp.sum(dz, axis=0, keepdims=True)
        dgkw_ref[...] += _dot_tn(low_b, dzb)
        dproj_ref[:, GLA_QKVG_WIDTH:] = _dot_nt(dzb, gkw_ref[...]).astype(BF16)

        @pl.when(i == nt - 1)
        def _():
            dwout_stage[...] = dwout_acc[...].astype(BF16)
            pltpu.sync_copy(dwout_stage, dwout_hbm)

    rev = lambda i: (nt - 1 - i, 0)
    return pl.pallas_call(
        body, name="gla_bwd", grid=(nt,),
        in_specs=[pl.BlockSpec((tm, D_MODEL), rev), pl.BlockSpec((tm, GLA_SAVED_WIDTH), rev),
                  pl.BlockSpec((tm, GLA_VALUE_WIDTH), rev),
                  pl.BlockSpec((cpt, GLA_VALUE_WIDTH, GLA_HEAD_K), lambda i: (nt - 1 - i, 0, 0)),
                  pl.BlockSpec((tm, GLA_HEADS * ROW_TILE), rev),
                  _const((GLA_LOW_PAD, GLA_KEY_WIDTH)), _const((1, GLA_KEY_WIDTH)), _const((1, GLA_VALUE_WIDTH)),
                  _const((GLA_VALUE_WIDTH, D_MODEL))],
        out_specs=[pl.BlockSpec((tm, GLA_IN_PAD), rev), pl.BlockSpec(memory_space=pl.ANY),
                   _full((1, GLA_VALUE_WIDTH)), _full((GLA_LOW_PAD, GLA_KEY_WIDTH)), _full((1, GLA_KEY_WIDTH))],
        out_shape=[jax.ShapeDtypeStruct((seq, GLA_IN_PAD), BF16), jax.ShapeDtypeStruct((GLA_VALUE_WIDTH, D_MODEL), BF16),
                   jax.ShapeDtypeStruct((1, GLA_VALUE_WIDTH), F32), jax.ShapeDtypeStruct((GLA_LOW_PAD, GLA_KEY_WIDTH), F32),
                   jax.ShapeDtypeStruct((1, GLA_KEY_WIDTH), F32)],
        scratch_shapes=[pltpu.VMEM((GLA_VALUE_WIDTH, GLA_HEAD_K), F32), pltpu.VMEM((GLA_VALUE_WIDTH, D_MODEL), F32),
                        pltpu.VMEM((GLA_VALUE_WIDTH, D_MODEL), BF16)],
        compiler_params=_params(),
    )(dh2, proj, o, states, scores, gkw, gkb, hw, w_out)


def _position():
    return lax.axis_index("x"), lax.axis_index("y"), lax.axis_index("c")


def _lead_slot(ref, d):
    return ref.at[d]


def _row_slot(rows):
    return lambda ref, d: ref.at[pl.ds(pl.multiple_of(d * rows, rows), rows)]


def _dim1_slot(size):
    return lambda ref, d: ref.at[:, pl.ds(pl.multiple_of(d * size, size), size)]


class _Gather:
    def __init__(self, in_refs, out_refs, slots, send_sems, recv_sems, local_sems):
        self.in_refs, self.out_refs, self.slots = in_refs, out_refs, slots
        self.send_sems, self.recv_sems, self.local_sems = send_sems, recv_sems, local_sems
        self.n = len(in_refs)
        x, y, c = _position()
        self.c = c
        self.me, self.sibling = (x, y, c), (x, y, 1 - c)
        self.near = [(1 - x, y), (x, 1 - y)]
        self.diagonal = (1 - x, 1 - y)
        self.relay_from = (x ^ c, y ^ (1 - c))
        self.relay_to = (x ^ (1 - c), y ^ c)

    def _copy(self, a, k, block, to, from_input=False):
        part = self.slots[a](self.out_refs[a], 4 * block[0] + 2 * block[1] + block[2])
        return pltpu.make_async_remote_copy(
            src_ref=self.in_refs[a] if from_input else part, dst_ref=part,
            send_sem=self.send_sems.at[a, k], recv_sem=self.recv_sems.at[a, k], device_id=to, device_id_type=MESH)

    def _mine(self):
        return [pltpu.make_async_copy(self.in_refs[a], self.slots[a](self.out_refs[a], 4 * self.me[0] + 2 * self.me[1]
                                                                    + self.me[2]), self.local_sems.at[a])
                for a in range(self.n)]

    def _first(self):
        first = [self._copy(a, 0, self.me, self.sibling, True) for a in range(self.n)]
        return first + [self._copy(a, 1 + j, self.me, (*chip, self.c), True)
                        for j, chip in enumerate(self.near) for a in range(self.n)]

    def _relayed(self):
        return [self._copy(a, 3, (*self.relay_from, self.c), (*self.relay_to, self.c)) for a in range(self.n)]

    def _passed(self, j):
        chip = self.near[j] if j < 2 else self.diagonal
        return [self._copy(a, 4 + j, (*chip, self.c), self.sibling) for a in range(self.n)]

    def start(self):
        for cp in self._mine() + self._first():
            cp.start()

    def forward(self):
        for j, chip in enumerate(self.near):
            for a in range(self.n):
                self._copy(a, 1 + j, (*chip, self.c), self.me).wait_recv()
        for cp in self._relayed() + self._passed(0) + self._passed(1):
            cp.start()

    def relay(self):
        pass

    def finish(self):
        for a in range(self.n):
            self._copy(a, 3, (*self.diagonal, self.c), self.me).wait_recv()
        for cp in self._passed(2):
            cp.start()
        for a in range(self.n):
            self._copy(a, 0, self.sibling, self.me).wait_recv()
        for j, chip in enumerate(self.near + [self.diagonal]):
            for a in range(self.n):
                self._copy(a, 4 + j, (*chip, 1 - self.c), self.me).wait_recv()
        for cp in self._first() + self._relayed() + self._passed(0) + self._passed(1) + self._passed(2):
            cp.wait_send()
        for cp in self._mine():
            cp.wait()


class _Exchange:
    def __init__(self, in_refs, out_refs, slots, send_sems, recv_sems, local_sems):
        self.in_refs, self.out_refs, self.slots = in_refs, out_refs, slots
        self.send_sems, self.recv_sems, self.local_sems = send_sems, recv_sems, local_sems
        self.n = len(in_refs)
        self.pos = _position()

    def _copies(self):
        x, y, c = self.pos
        me = 4 * x + 2 * y + c
        mine = [pltpu.make_async_copy(self.slots[a](self.in_refs[a], me), self.out_refs[a].at[me],
                                      self.local_sems.at[a]) for a in range(self.n)]
        remote = []
        for k in range(1, N_DEV):
            px, py, pc = x ^ (k >> 2), y ^ ((k >> 1) & 1), c ^ (k & 1)
            for a in range(self.n):
                remote.append(pltpu.make_async_remote_copy(
                    src_ref=self.slots[a](self.in_refs[a], 4 * px + 2 * py + pc), dst_ref=self.out_refs[a].at[me],
                    send_sem=self.send_sems.at[a, k - 1], recv_sem=self.recv_sems.at[a, k - 1],
                    device_id=(px, py, pc), device_id_type=MESH))
        return mine, remote

    def start(self):
        mine, remote = self._copies()
        for cp in mine + remote:
            cp.start()

    def forward(self):
        pass

    def relay(self):
        pass

    def finish(self):
        mine, remote = self._copies()
        for cp in remote:
            cp.wait_recv()
        for cp in remote:
            cp.wait_send()
        for cp in mine:
            cp.wait()


class _Rider:
    def __init__(self, kind, arrays, out_shapes, slots, scratch=None, forward_step=None):
        self.kind, self.arrays, self.slots = kind, list(arrays), slots
        self.n = len(self.arrays)
        hbm = pl.BlockSpec(memory_space=pl.ANY)
        self.in_specs = [hbm] * self.n
        self.out_specs = [hbm] * self.n
        self.out_shape = [jax.ShapeDtypeStruct(tuple(s), a.dtype) for s, a in zip(out_shapes, self.arrays)]
        self.scratch = scratch if scratch is not None else [
            pltpu.SemaphoreType.DMA((self.n, 7)), pltpu.SemaphoreType.DMA((self.n, 7)),
            pltpu.SemaphoreType.DMA((self.n,))]
        self.forward_step = forward_step
        self.relay_step = None

    def bind(self, in_refs, out_refs, scratch):
        return self.kind(in_refs, out_refs, self.slots, *scratch)


def _gather_rider(shards, full_shapes, slots, forward_step=None):
    return _Rider(_Gather, shards, full_shapes, slots, None, forward_step)


def _exchange_rider(sends, part_shapes, slots):
    return _Rider(_Exchange, sends, [(N_DEV,) + tuple(s) for s in part_shapes], slots)


def _split_refs(refs, n_in, n_out, n_scratch, rider):
    k = rider.n if rider is not None else 0
    ins, r_ins = refs[:n_in], refs[n_in:n_in + k]
    outs, r_outs = refs[n_in + k:n_in + k + n_out], refs[n_in + k + n_out:n_in + 2 * k + n_out]
    rest = refs[n_in + 2 * k + n_out:]
    scratch, r_scratch = rest[:n_scratch], rest[n_scratch:]
    comm = rider.bind(r_ins, r_outs, r_scratch) if rider is not None else None
    if comm is not None:
        comm.forward_step, comm.relay_step = rider.forward_step, rider.relay_step
    return ins + outs + scratch, comm


def _ride_before(comm, i, nt):
    if comm is not None:
        pl.when(i == 0)(comm.start)
        pl.when(i == (nt - 1 if comm.forward_step is None else min(comm.forward_step, nt - 1)))(comm.forward)
        pl.when(i == (nt - 1 if comm.relay_step is None else min(comm.relay_step, nt - 1)))(comm.relay)


def _ride_after(comm, i, nt):
    if comm is not None:
        pl.when(i == nt - 1)(comm.finish)


def _extend(specs, rider, field):
    return list(specs) + (getattr(rider, field) if rider is not None else [])


def _comm_call(name, rider, cast_from):
    order = sorted(cast_from)

    def body(*refs):
        n = rider.n
        ins, outs, rest = list(refs[:n]), refs[n:2 * n], refs[2 * n:]
        for stage, index in zip(rest[:len(order)], order):
            stage[...] = ins[index][...].astype(BF16)
            ins[index] = stage
        comm = rider.bind(ins, outs, rest[len(order):])
        comm.start()
        comm.forward()
        comm.relay()
        comm.finish()

    vmem = pl.BlockSpec(memory_space=pltpu.VMEM)
    return pl.pallas_call(
        body, name=name, in_specs=[vmem if i in cast_from else spec for i, spec in enumerate(rider.in_specs)],
        out_specs=rider.out_specs, out_shape=rider.out_shape,
        scratch_shapes=[pltpu.VMEM(cast_from[i].shape, BF16) for i in order] + rider.scratch,
        compiler_params=pltpu.CompilerParams(vmem_limit_bytes=VMEM_LIMIT),
    )(*[cast_from.get(i, a) for i, a in enumerate(rider.arrays)])


N_CHIPS = 4


class _TwoLevel:
    def __init__(self, in_refs, out_refs, slots, *scratch):
        self.in_refs, self.out_refs, self.slots = in_refs, out_refs, slots
        self.n = n = len(in_refs)
        self.own_bufs, self.recv_bufs, self.relay_bufs = scratch[:n], scratch[n:2 * n], scratch[2 * n:3 * n]
        self.swap_send, self.swap_recv, self.local_sems, self.chip_send, self.chip_recv = scratch[3 * n:]
        x, y, c = self.pos = _position()
        self.first = (x ^ (1 - c), y ^ c)
        self.second = (x ^ c, y ^ (1 - c))
        self.chip_index = lambda chip: 2 * chip[0] + chip[1]

    def _swap(self):
        x, y, c = self.pos
        return [pltpu.make_async_remote_copy(
            src_ref=self.slots[a](self.in_refs[a], 2 * q + 1 - c), dst_ref=self.recv_bufs[a].at[q],
            send_sem=self.swap_send.at[a, q], recv_sem=self.swap_recv.at[a, q],
            device_id=(x, y, 1 - c), device_id_type=MESH) for a in range(self.n) for q in range(N_CHIPS)]

    def _mine(self):
        c = self.pos[2]
        return [pltpu.make_async_copy(self.slots[a](self.in_refs[a], 2 * q + c), self.own_bufs[a].at[q],
                                      self.local_sems.at[a, q]) for a in range(self.n) for q in range(N_CHIPS)]

    def _to_chip(self, a, k, src, dst, chip):
        return pltpu.make_async_remote_copy(
            src_ref=src, dst_ref=dst, send_sem=self.chip_send.at[a, k], recv_sem=self.chip_recv.at[a, k],
            device_id=(*chip, self.pos[2]), device_id_type=MESH)

    def _first_wave(self):
        x, y, _ = self.pos
        diagonal = self.chip_index((1 - x, 1 - y))
        passed_on = [self._to_chip(a, 1, self.own_bufs[a].at[diagonal], self.relay_bufs[a], self.first)
                     for a in range(self.n)]
        return passed_on + [self._to_chip(a, 0, self.own_bufs[a].at[self.chip_index(self.first)],
                                          self.out_refs[a].at[1], self.first) for a in range(self.n)]

    def _second_wave(self):
        return [self._to_chip(a, 2, self.own_bufs[a].at[self.chip_index(self.second)], self.out_refs[a].at[2],
                              self.second) for a in range(self.n)]

    def _own(self):
        x, y, _ = self.pos
        return [pltpu.make_async_copy(self.own_bufs[a].at[2 * x + y], self.out_refs[a].at[0],
                                      self.local_sems.at[a, N_CHIPS]) for a in range(self.n)]

    def start(self):
        for cp in self._swap() + self._mine():
            cp.start()

    def forward(self):
        swap, mine = self._swap(), self._mine()
        for a in range(self.n):
            for q in range(N_CHIPS):
                mine[a * N_CHIPS + q].wait()
                swap[a * N_CHIPS + q].wait_recv()
                self.own_bufs[a][q] = (self.own_bufs[a][q].astype(F32)
                                       + self.recv_bufs[a][q].astype(F32)).astype(BF16)
        for cp in self._first_wave() + self._own():
            cp.start()

    def relay(self):
        second = self.chip_index(self.second)
        for a in range(self.n):
            self._to_chip(a, 1, self.relay_bufs[a], self.relay_bufs[a], self.first).wait_recv()
            self.own_bufs[a][second] = (self.own_bufs[a][second].astype(F32)
                                        + self.relay_bufs[a][...].astype(F32)).astype(BF16)
        for cp in self._second_wave():
            cp.start()

    def finish(self):
        for a in range(self.n):
            self._to_chip(a, 0, self.out_refs[a].at[1], self.out_refs[a].at[1], self.first).wait_recv()
            self._to_chip(a, 2, self.out_refs[a].at[2], self.out_refs[a].at[2], self.second).wait_recv()
        for cp in self._first_wave() + self._second_wave() + self._swap():
            cp.wait_send()
        for cp in self._own():
            cp.wait()


def _two_level_rider(sends, part_shapes, slots, forward_step=None, relay_step=None):
    n = len(sends)
    bufs = [pltpu.VMEM((N_CHIPS,) + tuple(s), a.dtype) for s, a in zip(part_shapes, sends)]
    relay_bufs = [pltpu.VMEM(tuple(s), a.dtype) for s, a in zip(part_shapes, sends)]
    scratch = bufs + bufs + relay_bufs + [
        pltpu.SemaphoreType.DMA((n, N_CHIPS)), pltpu.SemaphoreType.DMA((n, N_CHIPS)),
        pltpu.SemaphoreType.DMA((n, N_CHIPS + 1)), pltpu.SemaphoreType.DMA((n, 3)), pltpu.SemaphoreType.DMA((n, 3))]
    rider = _Rider(_TwoLevel, sends, [(3,) + tuple(s) for s in part_shapes], slots, scratch, forward_step)
    rider.relay_step = relay_step
    return rider


class _Joined:
    def __init__(self, first, second):
        self.first, self.second = first, second

    def start(self):
        self.first.start()
        self.second.start()

    def forward(self):
        self.first.forward()
        self.second.forward()

    def relay(self):
        self.first.relay()
        self.second.relay()

    def finish(self):
        self.first.finish()
        self.second.finish()


class _JoinedRider:
    def __init__(self, first, second):
        self.first, self.second = first, second
        self.n = first.n + second.n
        self.arrays = first.arrays + second.arrays
        self.in_specs = first.in_specs + second.in_specs
        self.out_specs = first.out_specs + second.out_specs
        self.out_shape = first.out_shape + second.out_shape
        self.scratch = first.scratch + second.scratch
        self.forward_step = first.forward_step
        self.relay_step = first.relay_step

    def bind(self, in_refs, out_refs, scratch):
        k, s = self.first.n, len(self.first.scratch)
        return _Joined(self.first.bind(in_refs[:k], out_refs[:k], scratch[:s]),
                       self.second.bind(in_refs[k:], out_refs[k:], scratch[s:]))


def _adamw(w, g, m, v):
    m = ADAM_B1 * m + (1.0 - ADAM_B1) * g
    v = ADAM_B2 * v + (1.0 - ADAM_B2) * (g * g)
    m_hat = m / (1.0 - ADAM_B1 ** ADAM_STEP)
    v_hat = v / (1.0 - ADAM_B2 ** ADAM_STEP)
    delta = -ADAM_LR * (m_hat / (jnp.sqrt(v_hat) + ADAM_EPS) + ADAM_WD * w)
    return delta, m, v


def _sum_parts(parts_ref, index=()):
    g = parts_ref[(0,) + index].astype(F32)
    for s in range(1, parts_ref.shape[0]):
        g = g + parts_ref[(s,) + index].astype(F32)
    return g


def _adamw_group_call(name, groups):
    k = len(groups)

    def body(*refs):
        ins, outs = refs[:4 * k], refs[4 * k:]
        for i in range(k):
            parts_ref, w_ref, m_ref, v_ref = ins[4 * i:4 * i + 4]
            g = _sum_parts(parts_ref)
            delta, m_new, v_new = _adamw(w_ref[...], g, m_ref[...], v_ref[...])
            for out_ref, value in zip(outs[4 * i:4 * i + 4], (g, delta, m_new, v_new)):
                out_ref[...] = value

    vmem = pl.BlockSpec(memory_space=pltpu.VMEM)
    res = pl.pallas_call(
        body, name=name, in_specs=[vmem] * (4 * k), out_specs=[vmem] * (4 * k),
        out_shape=[jax.ShapeDtypeStruct(grp[1].shape, F32) for grp in groups for _ in range(4)],
        compiler_params=pltpu.CompilerParams(vmem_limit_bytes=VMEM_LIMIT),
    )(*[a for grp in groups for a in grp])
    return [res[4 * i:4 * i + 4] for i in range(k)]


def _adamw_slabs_call(name, parts, w, m, v, rider=None):
    def main(parts_ref, w_ref, m_ref, v_ref, g_ref, delta_ref, m_out, v_out):
        g = _sum_parts(parts_ref)
        delta, m_new, v_new = _adamw(w_ref[...], g, m_ref[...], v_ref[...])
        g_ref[...] = g
        delta_ref[...] = delta
        m_out[...] = m_new
        v_out[...] = v_new

    def body(*refs):
        own, comm = _split_refs(refs, 4, 4, 0, rider)
        if comm is not None:
            comm.start()
        main(*own)
        if comm is not None:
            comm.forward()
            comm.relay()
            comm.finish()

    vmem = pl.BlockSpec(memory_space=pltpu.VMEM)
    return pl.pallas_call(
        body, name=name, in_specs=_extend([vmem] * 4, rider, "in_specs"),
        out_specs=_extend([vmem] * 4, rider, "out_specs"),
        out_shape=_extend([jax.ShapeDtypeStruct(w.shape, F32)] * 4, rider, "out_shape"),
        scratch_shapes=_extend([], rider, "scratch"),
        compiler_params=pltpu.CompilerParams(vmem_limit_bytes=VMEM_LIMIT),
    )(parts, w, m, v, *_extend([], rider, "arrays"))


WIDE_ROWS = 8
NARROW_ROWS = 40
NARROW_GKW_ROW = 8
NARROW_GKB_ROW = 24
NARROW_HW_ROW = 32
GROUP_SHARD = POOL_GROUP_DIM // N_DEV
KEY_SHARD = GLA_KEY_WIDTH // N_DEV
HEAD_V_SHARD = GLA_HEAD_V // N_DEV


def _small_adamw_call(wide, narrow, w, m, v):
    names = ("norm_w", "pool_scale", "final_norm_w", "pool_group_b", "gla_gk_w", "gla_gk_b", "gla_head_norm_w")
    where = {
        "norm_w": (0, slice(0, 2), slice(None)),
        "pool_scale": (0, slice(2, 3), slice(None)),
        "final_norm_w": (0, slice(3, 4), slice(None)),
        "pool_group_b": (1, slice(0, POOL_GROUPS), slice(0, GROUP_SHARD)),
        "gla_gk_w": (1, slice(NARROW_GKW_ROW, NARROW_GKW_ROW + GLA_GATE_RANK), slice(0, KEY_SHARD)),
        "gla_gk_b": (1, slice(NARROW_GKB_ROW, NARROW_GKB_ROW + 1), slice(0, KEY_SHARD)),
        "gla_head_norm_w": (1, slice(NARROW_HW_ROW, NARROW_HW_ROW + 1), slice(0, HEAD_V_SHARD)),
    }
    k = len(names)

    def body(*refs):
        parts = refs[0:2]
        w_refs, m_refs, v_refs = refs[2:2 + k], refs[2 + k:2 + 2 * k], refs[2 + 2 * k:2 + 3 * k]
        outs = refs[2 + 3 * k:]
        loss_ref = outs[0]
        loss_ref[...] = _sum_parts(parts[0], (slice(4, 5), slice(0, 1)))
        for i, name in enumerate(names):
            buf, rows, cols = where[name]
            g = _sum_parts(parts[buf], (rows, cols))
            delta, m_new, v_new = _adamw(w_refs[i][...], g, m_refs[i][...], v_refs[i][...])
            outs[1 + i][...] = g
            outs[1 + k + i][...] = delta
            outs[1 + 2 * k + i][...] = m_new
            outs[1 + 3 * k + i][...] = v_new

    vmem = pl.BlockSpec(memory_space=pltpu.VMEM)
    shapes = [jax.ShapeDtypeStruct(w[n].shape, F32) for n in names]
    res = pl.pallas_call(
        body, name="adamw_small", in_specs=[vmem] * (2 + 3 * k), out_specs=[vmem] * (1 + 4 * k),
        out_shape=[jax.ShapeDtypeStruct((1, 1), F32)] + shapes * 4,
    )(wide, narrow, *[w[n] for n in names], *[m[n] for n in names], *[v[n] for n in names])
    unzip = lambda j: dict(zip(names, res[1 + j * k:1 + (j + 1) * k]))
    return res[0], unzip(0), unzip(1), unzip(2), unzip(3)


def kernel(x, norm_w, pool_in_w, pool_group_w, pool_group_b, pool_scale, pool_out_w, gla_in_w, gla_gk_w, gla_gk_b, gla_head_norm_w, gla_out_w, final_norm_w, loss_target, m_norm_w, m_pool_in_w, m_pool_group_w, m_pool_group_b, m_pool_scale, m_pool_out_w, m_gla_in_w, m_gla_gk_w, m_gla_gk_b, m_gla_head_norm_w, m_gla_out_w, m_final_norm_w, v_norm_w, v_pool_in_w, v_pool_group_w, v_pool_group_b, v_pool_scale, v_pool_out_w, v_gla_in_w, v_gla_gk_w, v_gla_gk_b, v_gla_head_norm_w, v_gla_out_w, v_final_norm_w):
    w = dict(norm_w=norm_w, pool_in_w=pool_in_w, pool_group_w=pool_group_w, pool_group_b=pool_group_b,
             pool_scale=pool_scale, pool_out_w=pool_out_w, gla_in_w=gla_in_w, gla_gk_w=gla_gk_w, gla_gk_b=gla_gk_b,
             gla_head_norm_w=gla_head_norm_w, gla_out_w=gla_out_w, final_norm_w=final_norm_w)
    m = dict(norm_w=m_norm_w, pool_in_w=m_pool_in_w, pool_group_w=m_pool_group_w, pool_group_b=m_pool_group_b,
             pool_scale=m_pool_scale, pool_out_w=m_pool_out_w, gla_in_w=m_gla_in_w, gla_gk_w=m_gla_gk_w,
             gla_gk_b=m_gla_gk_b, gla_head_norm_w=m_gla_head_norm_w, gla_out_w=m_gla_out_w,
             final_norm_w=m_final_norm_w)
    v = dict(norm_w=v_norm_w, pool_in_w=v_pool_in_w, pool_group_w=v_pool_group_w, pool_group_b=v_pool_group_b,
             pool_scale=v_pool_scale, pool_out_w=v_pool_out_w, gla_in_w=v_gla_in_w, gla_gk_w=v_gla_gk_w,
             gla_gk_b=v_gla_gk_b, gla_head_norm_w=v_gla_head_norm_w, gla_out_w=v_gla_out_w,
             final_norm_w=v_final_norm_w)
    col_shard = GLA_IN_WIDTH // N_DEV
    row_shard = D_MODEL // N_DEV

    def lanes(a):
        return jnp.pad(a, [(0, 0)] * (a.ndim - 1) + [(0, LANES - a.shape[-1])])

    small_in = jnp.concatenate([lanes(pool_group_b[0]), lanes(gla_gk_b), lanes(gla_head_norm_w),
                                jnp.zeros((2, LANES), F32)], axis=0)
    in_cols = 2 * POOL_WIDTH // N_DEV
    pool_f32 = [pool_in_w[0], pool_group_w[0], pool_out_w[0]]
    pool_in, pool_gw, pool_out, small_all = _comm_call("pool_weights_all_gather", _gather_rider(
        [jax.ShapeDtypeStruct(a.shape, BF16) for a in pool_f32] + [small_in],
        [(D_MODEL, 2 * POOL_WIDTH), (POOL_GROUPS, POOL_GROUP_DIM, POOL_GROUP_DIM), (POOL_WIDTH, D_MODEL),
         (N_DEV, 8, LANES)],
        [_dim1_slot(in_cols), _dim1_slot(GROUP_SHARD), _row_slot(row_shard), _lead_slot]), dict(enumerate(pool_f32)))
    pool_gb = jnp.transpose(small_all[:, 0:POOL_GROUPS, :GROUP_SHARD], (1, 0, 2)).reshape(1, POOL_WIDTH)
    gla_gkb = small_all[:, POOL_GROUPS, :KEY_SHARD].reshape(1, GLA_KEY_WIDTH)
    gla_hw = jnp.tile(small_all[:, POOL_GROUPS + 1, :HEAD_V_SHARD].reshape(1, GLA_HEAD_V), (1, GLA_HEADS))
    nw0, nw1, wf = norm_w[0:1], norm_w[1:2], final_norm_w.reshape(1, D_MODEL)
    xs, target = x[0], loss_target[0]

    slabs = col_shard * D_MODEL // (BF16_ROWS * LANES)
    as_slabs = lambda t: jnp.transpose(t[0]).reshape(slabs, BF16_ROWS, LANES)
    h1, pool_y, pool_silu, pool_dsilu, pooled, mixed, gla_in_parts, gkw_parts, gla_out = _pool_fwd_call(
        xs, nw0, pool_in, pool_gw, pool_gb, pool_scale, pool_out, _gather_rider(
            [as_slabs(gla_in_w).astype(BF16), gla_gk_w[0].astype(BF16), gla_out_w[0].astype(BF16)],
            [(N_DEV, slabs, BF16_ROWS, LANES), (N_DEV, GLA_GATE_RANK, KEY_SHARD), (GLA_VALUE_WIDTH, D_MODEL)],
            [_lead_slot, _lead_slot, _row_slot(row_shard)], GATHER_RELAY_STEP))
    gla_gkw = jnp.pad(jnp.transpose(gkw_parts, (1, 0, 2)).reshape(GLA_GATE_RANK, GLA_KEY_WIDTH),
                      ((0, GLA_LOW_PAD - GLA_GATE_RANK), (0, 0)))
    dh2, proj, o, states, scores, loss_part, dwf, gla_in = _gla_fwd_call(
        h1, nw1, gla_in_parts.reshape(GLA_IN_WIDTH * D_MODEL // LANES, LANES), gla_gkw, gla_gkb, gla_hw, gla_out,
        wf, target)

    dproj, d_gla_out, dhw, dgkw, dgkb = _gla_bwd_call(dh2, proj, o, states, scores, gla_gkw, gla_gkb, gla_hw,
                                                      gla_out)
    dh1, d_gla_in, dnw1, landed_gla_out = _inproj_bwd_call(
        "gla_in_bwd", dproj, h1, nw1, gla_in, dh2,
        _exchange_rider([d_gla_out], [(row_shard, D_MODEL)], [_row_slot(row_shard)]), transposed=True)
    gla_in_send = d_gla_in.reshape(N_DEV, slabs, BF16_ROWS, LANES)
    dp, d_pool_out, dgw, dgb, dsc, landed_gla_in = _pool_bwd_call(
        dh1, pool_y, pool_silu, pool_dsilu, pooled, mixed, pool_gw, pool_scale, pool_out,
        _two_level_rider([gla_in_send], [(slabs, BF16_ROWS, LANES)], [_lead_slot], TWO_LEVEL_ADD_STEP,
                         TWO_LEVEL_RELAY_STEP))
    grad_x, d_pool_in, dnw0 = _inproj_bwd_call("pool_in_bwd", dp, xs, nw0, pool_in, dh1)

    wide = jnp.concatenate([
        dnw0, dnw1, dsc, dwf, jnp.pad(loss_part[0:1, 0:1], ((0, 0), (0, D_MODEL - 1))),
        jnp.zeros((WIDE_ROWS - 5, D_MODEL), F32)], axis=0)

    def rows8(a):
        return jnp.pad(lanes(a), ((0, 0), (0, -a.shape[1] % 8), (0, 0)))

    narrow = jnp.concatenate([
        rows8(jnp.transpose(dgb.reshape(POOL_GROUPS, N_DEV, GROUP_SHARD), (1, 0, 2))),
        rows8(jnp.transpose(dgkw[:GLA_GATE_RANK].reshape(GLA_GATE_RANK, N_DEV, KEY_SHARD), (1, 0, 2))),
        rows8(dgkb.reshape(N_DEV, 1, KEY_SHARD)),
        rows8(dhw.reshape(GLA_HEADS, GLA_HEAD_V).sum(axis=0).reshape(N_DEV, 1, HEAD_V_SHARD)),
    ], axis=1)
    last_exchange = _JoinedRider(
        _two_level_rider([d_pool_in, d_pool_out, dgw],
                         [(D_MODEL, in_cols), (row_shard, D_MODEL), (POOL_GROUPS, GROUP_SHARD, POOL_GROUP_DIM)],
                         [_dim1_slot(in_cols), _row_slot(row_shard), _dim1_slot(GROUP_SHARD)]),
        _exchange_rider([wide, narrow], [(WIDE_ROWS, D_MODEL), (NARROW_ROWS, LANES)],
                        [lambda ref, d: ref, _lead_slot]))

    res = {}
    *outs, landed_pool_in, landed_pool_out, landed_gw, landed_wide, landed_narrow = _adamw_slabs_call(
        "adamw_gla_in_w", landed_gla_in, as_slabs(gla_in_w), as_slabs(m_gla_in_w), as_slabs(v_gla_in_w),
        last_exchange)
    res["gla_in_w"] = [jnp.transpose(t.reshape(col_shard, D_MODEL))[None] for t in outs]
    rest = [("pool_in_w", landed_pool_in, (D_MODEL, in_cols)),
            ("pool_group_w", landed_gw, (POOL_GROUPS * GROUP_SHARD, POOL_GROUP_DIM)),
            ("pool_out_w", landed_pool_out, (row_shard, D_MODEL)), ("gla_out_w", landed_gla_out, (row_shard, D_MODEL))]
    updates = _adamw_group_call("adamw_matrices", [
        (parts.reshape((parts.shape[0],) + shape), w[name].reshape(shape), m[name].reshape(shape),
         v[name].reshape(shape)) for name, parts, shape in rest])
    for (name, _, _), outs in zip(rest, updates):
        res[name] = [t.reshape(w[name].shape) for t in outs]
    small_shapes ={"norm_w": (2, D_MODEL), "pool_scale": (1, D_MODEL), "final_norm_w": (1, D_MODEL),
                    "pool_group_b": (POOL_GROUPS, GROUP_SHARD), "gla_gk_w": (GLA_GATE_RANK, KEY_SHARD),
                    "gla_gk_b": (1, KEY_SHARD), "gla_head_norm_w": (1, HEAD_V_SHARD)}
    as_small = lambda t: {n: t[n].reshape(s) for n, s in small_shapes.items()}
    loss, *small_outs = _small_adamw_call(landed_wide, landed_narrow, as_small(w), as_small(m), as_small(v))
    for name in small_shapes:
        res[name] = [t[name].reshape(w[name].shape) for t in small_outs]
    order = ("norm_w", "pool_in_w", "pool_group_w", "pool_group_b", "pool_scale", "pool_out_w", "gla_in_w",
             "gla_gk_w", "gla_gk_b", "gla_head_norm_w", "gla_out_w", "final_norm_w")
    return (loss.reshape(()), grad_x[None], *[res[n][0] for n in order], *[res[n][1] for n in order],
            *[res[n][2] for n in order], *[res[n][3] for n in order])
```

```python
import jax
import jax.numpy as jnp
from jax import lax
from jax.experimental import pallas as pl
from jax.experimental.pallas import tpu as pltpu

F32 = jnp.float32
BF16 = jnp.bfloat16
MESH = pl.DeviceIdType.MESH

N_DEV = 8
D_MODEL = 1024
POOL_WIDTH = 1024
POOL_GROUPS = 4
POOL_GROUP_DIM = 256
POOL_HALO = 16
GLA_HEADS = 4
GLA_HEAD_K = 128
GLA_HEAD_V = 256
GLA_KEY_WIDTH = 512
GLA_VALUE_WIDTH = 1024
GLA_GATE_RANK = 16
GLA_IN_WIDTH = 3088
GLA_IN_PAD = 3200
GLA_SAVED_Z = GLA_IN_PAD
GLA_SAVED_C = GLA_SAVED_Z + 512
GLA_SAVED_WIDTH = GLA_SAVED_C + 512
GLA_LOW_PAD = 128
GLA_QKVG_WIDTH = 3072
CHUNK = 64
GATE_NORMALIZER = 16.0
RMS_EPS = 1e-6
Q_SCALE = GLA_HEAD_K ** -0.5

ADAM_LR = 0.001
ADAM_B1 = 0.9
ADAM_B2 = 0.999
ADAM_EPS = 1e-08
ADAM_WD = 0.01
ADAM_STEP = 10

LANES = 128
BF16_ROWS = 16
VMEM_LIMIT = 60 * 1024 * 1024
ROW_TILE = 256
GLA_FWD_ROW_TILE = 512
MATMUL_ROW_TILE = 512
DW_OUT_PIECES = 4
ROW_MAJOR_PIECE = 776
WEIGHT_ROWS_PIECE = 208
GATHER_RELAY_STEP = 5
TWO_LEVEL_ADD_STEP = 1
TWO_LEVEL_RELAY_STEP = 4


def _dot_nn(a, b):
    return lax.dot_general(a, b, (((1,), (0,)), ((), ())), preferred_element_type=F32)


def _dot_nt(a, b):
    return lax.dot_general(a, b, (((1,), (1,)), ((), ())), preferred_element_type=F32)


def _dot_tn(a, b):
    return lax.dot_general(a, b, (((0,), (0,)), ((), ())), preferred_element_type=F32)


def _rms(x):
    rstd = lax.rsqrt(jnp.mean(x * x, axis=-1, keepdims=True) + RMS_EPS)
    return x * rstd, rstd


def _rms_bwd(dxhat, xhat, rstd):
    return rstd * (dxhat - xhat * jnp.mean(dxhat * xhat, axis=-1, keepdims=True))


def _sigmoid(x):
    return 1.0 / (1.0 + jnp.exp(-x))


def _params(sem=("arbitrary",)):
    return pltpu.CompilerParams(dimension_semantics=sem, vmem_limit_bytes=VMEM_LIMIT)


def _full(shape):
    return pl.BlockSpec(shape, lambda i: (0,) * len(shape))


def _const(shape):
    return pl.BlockSpec(shape, lambda i: (0,) * len(shape), pipeline_mode=pl.Buffered(1))


def _window_sums(ext, forward):
    n = ext.shape[0]
    outs = []
    for g in range(POOL_GROUPS):
        s = ext[:, g * POOL_GROUP_DIM:(g + 1) * POOL_GROUP_DIM]
        for k in range(g + 1):
            shift = (1 << k) if forward else n - (1 << k)
            s = s + pltpu.roll(s, shift, axis=0)
        outs.append(s[:n - POOL_HALO])
    return outs


def _inv_count(row0, tm):
    row = row0 + lax.broadcasted_iota(jnp.int32, (tm, 1), 0)
    return [1.0 / jnp.minimum(row + 1, 2 << g).astype(F32) for g in range(POOL_GROUPS)]


def _pool_mix(u, u_prev, row0, gw_ref, gb):
    tm = u.shape[0]
    sums = _window_sums(jnp.concatenate([u, u_prev], axis=0), True)
    inv = _inv_count(row0, tm)
    pooled, mixed = [], []
    for g in range(POOL_GROUPS):
        ug = u[:, g * POOL_GROUP_DIM:(g + 1) * POOL_GROUP_DIM]
        pg = (sums[g] * inv[g] - ug).astype(BF16)
        pooled.append(pg)
        mixed.append(_dot_nn(pg, gw_ref[g]))
    return pooled, jnp.concatenate(mixed, axis=1) + gb


def _pool_fwd_call(x, nw, w_in, gw, gb, sc, w_out, rider=None):
    seq = x.shape[0]
    tm = min(MATMUL_ROW_TILE, seq)
    nt = seq // tm

    def main(x_ref, nw_ref, win_ref, gw_ref, gb_ref, sc_ref, wout_ref, h_ref, y_ref, silu_ref, dsilu_ref,
             pooled_ref, mixed_ref, halo_ref):
        i = pl.program_id(0)

        @pl.when(i == 0)
        def _():
            halo_ref[...] = jnp.zeros_like(halo_ref)

        xt = x_ref[...]
        xhat, _ = _rms(xt)
        n = (xhat * nw_ref[...]).astype(BF16)
        p = _dot_nn(n, win_ref[...])
        u = p[:, :POOL_WIDTH]
        gate = p[:, POOL_WIDTH:]
        sg = _sigmoid(gate)
        silu = gate * sg
        silu_ref[...] = silu
        dsilu_ref[...] = sg * (1.0 + gate * (1.0 - sg))
        pooled, mixed = _pool_mix(u, halo_ref[...], i * tm, gw_ref, gb_ref[...])
        pooled_ref[...] = jnp.concatenate(pooled, axis=1)
        mixed_ref[...] = mixed
        halo_ref[...] = u[tm - POOL_HALO:, :]
        y = (mixed * sc_ref[...] * silu).astype(BF16)
        y_ref[...] = y
        h_ref[...] = xt + _dot_nn(y, wout_ref[...])

    def body(*refs):
        own, comm = _split_refs(refs, 7, 6, 1, rider)
        _ride_before(comm, pl.program_id(0), nt)
        main(*own)
        _ride_after(comm, pl.program_id(0), nt)

    return pl.pallas_call(
        body, name="pool_fwd", grid=(nt,),
        in_specs=_extend([pl.BlockSpec((tm, D_MODEL), lambda i: (i, 0)), _const((1, D_MODEL)),
                          _const((D_MODEL, 2 * POOL_WIDTH)), _const((POOL_GROUPS, POOL_GROUP_DIM, POOL_GROUP_DIM)),
                          _const((1, POOL_WIDTH)), _const((1, POOL_WIDTH)), _const((POOL_WIDTH, D_MODEL))],
                         rider, "in_specs"),
        out_specs=_extend([pl.BlockSpec((tm, D_MODEL), lambda i: (i, 0))] * 6, rider, "out_specs"),
        out_shape=_extend([jax.ShapeDtypeStruct((seq, D_MODEL), F32), jax.ShapeDtypeStruct((seq, POOL_WIDTH), BF16),
                           jax.ShapeDtypeStruct((seq, POOL_WIDTH), F32), jax.ShapeDtypeStruct((seq, POOL_WIDTH), F32),
                           jax.ShapeDtypeStruct((seq, POOL_WIDTH), BF16),
                           jax.ShapeDtypeStruct((seq, POOL_WIDTH), F32)], rider, "out_shape"),
        scratch_shapes=_extend([pltpu.VMEM((POOL_HALO, POOL_WIDTH), F32)], rider, "scratch"),
        compiler_params=_params(),
    )(x, nw, w_in, gw, gb, sc, w_out, *_extend([], rider, "arrays"))


def _pool_bwd_call(dh, y, silu, dsilu, pooled, mixed, gw, sc, w_out, rider=None):
    seq = dh.shape[0]
    tm = min(MATMUL_ROW_TILE, seq)
    nt = seq // tm

    def main(dh_ref, y_ref, silu_ref, dsilu_ref, pooled_ref, mixed_ref, gw_ref, sc_ref, wout_ref,
             dp_ref, dwout_hbm, dgw_hbm, dgb_ref, dsc_ref, carry_ref, dwout_acc, dgw_acc, dwout_stage, dgw_stage):
        i = pl.program_id(0)
        t = nt - 1 - i

        @pl.when(i == 0)
        def _():
            carry_ref[...] = jnp.zeros_like(carry_ref)
            dwout_acc[...] = jnp.zeros_like(dwout_acc)
            dgw_acc[...] = jnp.zeros_like(dgw_acc)
            dgb_ref[...] = jnp.zeros_like(dgb_ref)
            dsc_ref[...] = jnp.zeros_like(dsc_ref)

        silu = silu_ref[...]
        pooled = [pooled_ref[:, g * POOL_GROUP_DIM:(g + 1) * POOL_GROUP_DIM] for g in range(POOL_GROUPS)]
        sc = sc_ref[...]
        dhb = dh_ref[...].astype(BF16)
        dwout_acc[...] += _dot_tn(y_ref[...], dhb)
        dy = _dot_nt(dhb, wout_ref[...])
        dmixed = dy * sc * silu
        dy_mixed = dy * mixed_ref[...]
        dsc_ref[...] += jnp.sum(dy_mixed * silu, axis=0, keepdims=True)
        dgate = dy_mixed * sc * dsilu_ref[...]
        dgb_ref[...] += jnp.sum(dmixed, axis=0, keepdims=True)
        inv = _inv_count(t * tm, tm)
        dpooled, scaled = [], []
        for g in range(POOL_GROUPS):
            dmg = dmixed[:, g * POOL_GROUP_DIM:(g + 1) * POOL_GROUP_DIM].astype(BF16)
            dgw_acc[g] += _dot_tn(pooled[g], dmg)
            dpg = _dot_nt(dmg, gw_ref[g])
            dpooled.append(dpg)
            scaled.append(dpg * inv[g])
        r = jnp.concatenate(scaled, axis=1)
        sums = _window_sums(jnp.concatenate([r, carry_ref[...]], axis=0), False)
        carry_ref[...] = r[:POOL_HALO, :]
        du = jnp.concatenate([sums[g] - dpooled[g] for g in range(POOL_GROUPS)], axis=1)
        dp_ref[...] = jnp.concatenate([du, dgate], axis=1).astype(BF16)

        @pl.when(i == nt - 1)
        def _():
            dwout_stage[...] = dwout_acc[...].astype(BF16)
            dgw_stage[...] = dgw_acc[...].astype(BF16)
            pltpu.sync_copy(dwout_stage, dwout_hbm)
            pltpu.sync_copy(dgw_stage, dgw_hbm)

    def body(*refs):
        own, comm = _split_refs(refs, 9, 5, 5, rider)
        _ride_before(comm, pl.program_id(0), nt)
        main(*own)
        _ride_after(comm, pl.program_id(0), nt)

    rev = lambda i: (nt - 1 - i, 0)
    return pl.pallas_call(
        body, name="pool_bwd", grid=(nt,),
        in_specs=_extend([pl.BlockSpec((tm, D_MODEL), rev)] * 6
                         + [_const((POOL_GROUPS, POOL_GROUP_DIM, POOL_GROUP_DIM)), _const((1, POOL_WIDTH)),
                            _const((POOL_WIDTH, D_MODEL))], rider, "in_specs"),
        out_specs=_extend([pl.BlockSpec((tm, 2 * POOL_WIDTH), rev), pl.BlockSpec(memory_space=pl.ANY),
                           pl.BlockSpec(memory_space=pl.ANY), _full((1, POOL_WIDTH)), _full((1, POOL_WIDTH))],
                          rider, "out_specs"),
        out_shape=_extend([jax.ShapeDtypeStruct((seq, 2 * POOL_WIDTH), BF16),
                           jax.ShapeDtypeStruct((POOL_WIDTH, D_MODEL), BF16),
                           jax.ShapeDtypeStruct((POOL_GROUPS, POOL_GROUP_DIM, POOL_GROUP_DIM), BF16),
                           jax.ShapeDtypeStruct((1, POOL_WIDTH), F32), jax.ShapeDtypeStruct((1, POOL_WIDTH), F32)],
                          rider, "out_shape"),
        scratch_shapes=_extend([pltpu.VMEM((POOL_HALO, POOL_WIDTH), F32), pltpu.VMEM((POOL_WIDTH, D_MODEL), F32),
                                pltpu.VMEM((POOL_GROUPS, POOL_GROUP_DIM, POOL_GROUP_DIM), F32),
                                pltpu.VMEM((POOL_WIDTH, D_MODEL), BF16),
                                pltpu.VMEM((POOL_GROUPS, POOL_GROUP_DIM, POOL_GROUP_DIM), BF16)], rider, "scratch"),
        compiler_params=_params(),
    )(dh, y, silu, dsilu, pooled, mixed, gw, sc, w_out, *_extend([], rider, "arrays"))


def _rows_then_zeros(ref, lo, hi, rows):
    part = ref[lo:hi, :]
    return jnp.concatenate([part, jnp.zeros((rows - (hi - lo), part.shape[1]), part.dtype)], axis=0)


def _inproj_bwd_call(name, dproj, h_in, nw, w_in, dres, rider=None, transposed=False):
    seq = h_in.shape[0]
    width = dproj.shape[1]
    w_shape = tuple(w_in.shape)
    acc_shape = (width, D_MODEL) if transposed else w_shape
    whole = w_shape[0] // LANES * LANES
    tm = min(MATMUL_ROW_TILE, seq)
    nt = seq // tm
    lane_tiles = D_MODEL // LANES
    dw_shape = (w_shape[0] * lane_tiles, LANES) if transposed else w_shape
    pieces = [(lo, min(lo + ROW_MAJOR_PIECE, w_shape[0])) for lo in range(0, w_shape[0], ROW_MAJOR_PIECE)]

    def to_row_major(dw_acc, dw_stage, dw_hbm, dw_lines, sems):
        copies = []
        for k, (lo, hi) in enumerate(pieces):
            for j in range(lane_tiles):
                dw_lines[pl.ds(j, hi - lo, stride=lane_tiles), :] = dw_acc[lo:hi, j * LANES:(j + 1) * LANES]
            rows = pl.ds(lo * lane_tiles, (hi - lo) * lane_tiles)
            dw_stage[lo * lane_tiles:hi * lane_tiles, :] = dw_lines[0:(hi - lo) * lane_tiles, :].astype(BF16)
            copies.append(pltpu.make_async_copy(dw_stage.at[rows], dw_hbm.at[rows], sems.at[k]))
            copies[-1].start()
        for copy in copies:
            copy.wait()

    def main(dproj_ref, h_ref, nw_ref, win_ref, dres_ref, dh_ref, dw_hbm, dnw_ref, dw_acc, dw_stage, *dw_lines):
        i = pl.program_id(0)

        @pl.when(i == 0)
        def _():
            dw_acc[...] = jnp.zeros_like(dw_acc)
            dnw_ref[...] = jnp.zeros_like(dnw_ref)

        dpb = dproj_ref[...]
        if transposed:
            dn = _dot_nn(dpb[:, :whole], win_ref[0:whole, :])
            if whole < w_shape[0]:
                dn = dn + _dot_nn(dpb[:, whole:], _rows_then_zeros(win_ref, whole, w_shape[0], width - whole))
        else:
            dn = _dot_nt(dpb, win_ref[...])
        xhat, rstd = _rms(h_ref[...])
        nw_row = nw_ref[...]
        n = (xhat * nw_row).astype(BF16)
        dw_acc[...] += _dot_tn(dpb, n) if transposed else _dot_tn(n, dpb)
        dnw_ref[...] += jnp.sum(dn * xhat, axis=0, keepdims=True)
        dh_ref[...] = _rms_bwd(dn * nw_row, xhat, rstd) + dres_ref[...]

        @pl.when(i == nt - 1)
        def _():
            if transposed:
                to_row_major(dw_acc, dw_stage, dw_hbm, *dw_lines)
            else:
                copies = []
                for k in range(DW_OUT_PIECES):
                    rows = pl.ds(k * w_shape[0] // DW_OUT_PIECES, w_shape[0] // DW_OUT_PIECES)
                    dw_stage[rows, :] = dw_acc[rows, :].astype(BF16)
                    copies.append(pltpu.make_async_copy(dw_stage.at[rows], dw_hbm.at[rows], dw_lines[0].at[k]))
                    copies[-1].start()
                for copy in copies:
                    copy.wait()

    scratch = [pltpu.VMEM(acc_shape, F32), pltpu.VMEM(dw_shape, BF16)]
    if transposed:
        scratch += [pltpu.VMEM((ROW_MAJOR_PIECE * lane_tiles, LANES), F32), pltpu.SemaphoreType.DMA((len(pieces),))]
    else:
        scratch += [pltpu.SemaphoreType.DMA((DW_OUT_PIECES,))]

    def body(*refs):
        own, comm = _split_refs(refs, 5, 3, len(scratch), rider)
        _ride_before(comm, pl.program_id(0), nt)
        main(*own)
        _ride_after(comm, pl.program_id(0), nt)

    row = lambda i: (i, 0)
    return pl.pallas_call(
        body, name=name, grid=(nt,),
        in_specs=_extend([pl.BlockSpec((tm, width), row), pl.BlockSpec((tm, D_MODEL), row), _const((1, D_MODEL)),
                          _const(w_shape), pl.BlockSpec((tm, D_MODEL), row)], rider, "in_specs"),
        out_specs=_extend([pl.BlockSpec((tm, D_MODEL), row), pl.BlockSpec(memory_space=pl.ANY),
                           _full((1, D_MODEL))], rider, "out_specs"),
        out_shape=_extend([jax.ShapeDtypeStruct((seq, D_MODEL), F32), jax.ShapeDtypeStruct(dw_shape, BF16),
                           jax.ShapeDtypeStruct((1, D_MODEL), F32)], rider, "out_shape"),
        scratch_shapes=_extend(scratch, rider, "scratch"),
        compiler_params=_params(),
    )(dproj, h_in, nw, w_in, dres, *_extend([], rider, "arrays"))


def _chunk_scan(x, reverse):
    n = x.shape[0]
    pos = lax.broadcasted_iota(jnp.int32, (n, 1), 0) & (CHUNK - 1)
    k = 1
    while k < CHUNK:
        if reverse:
            x = x + jnp.where(pos < CHUNK - k, pltpu.roll(x, n - k, axis=0), 0.0)
        else:
            x = x + jnp.where(pos >= k, pltpu.roll(x, k, axis=0), 0.0)
        k *= 2
    return x


def _chunk_rows(j):
    return slice(j * CHUNK, (j + 1) * CHUNK)


def _kcols(h):
    return slice(h * GLA_HEAD_K, (h + 1) * GLA_HEAD_K)


def _vcols(h):
    return slice(h * GLA_HEAD_V, (h + 1) * GLA_HEAD_V)


def _chunk_masks(tm):
    idx_t = lax.broadcasted_iota(jnp.int32, (tm, tm), 0)
    idx_s = lax.broadcasted_iota(jnp.int32, (tm, tm), 1)
    same_chunk = (idx_t ^ idx_s) < CHUNK
    return same_chunk & (idx_t >= idx_s), same_chunk & (idx_t < idx_s)


class _GlaTerms:
    def __init__(self, kc, q, k, v, low_b, gkw_ref, gkb_ref, masks, saved=None):
        tm = q.shape[0]
        self.q = q * Q_SCALE
        self.k = k
        if saved is None:
            self.z = _dot_nn(low_b, gkw_ref[:, kc]) + gkb_ref[:, kc]
            log_g = (jnp.minimum(self.z, 0.0) - jnp.log(1.0 + jnp.exp(-jnp.abs(self.z)))) / GATE_NORMALIZER
            self.c = _chunk_scan(log_g, False)
        else:
            self.z, self.c = saved
        is_last = lax.broadcasted_iota(jnp.int32, (CHUNK, 1), 0) == CHUNK - 1
        self.c_last = [jnp.sum(jnp.where(is_last, self.c[_chunk_rows(j), :], 0.0), axis=0, keepdims=True)
                       for j in range(tm // CHUNK)]
        c_last_rows = jnp.concatenate([jnp.broadcast_to(r, (CHUNK, r.shape[1])) for r in self.c_last], axis=0)
        self.e_pos = jnp.exp(self.c)
        self.e_neg = jnp.exp(-self.c)
        self.e_rest = jnp.exp(c_last_rows - self.c)
        self.a_b = (self.q * self.e_pos).astype(BF16)
        self.b_b = (self.k * self.e_neg).astype(BF16)
        self.cn_b = (self.q * self.e_neg).astype(BF16)
        self.dp_b = (self.k * self.e_pos).astype(BF16)
        self.kd_b = (self.k * self.e_rest).astype(BF16)
        self.v_b = v.astype(BF16)
        self.lower, self.upper = masks

    def scores(self, kc=slice(None)):
        fwd = _dot_nt(self.a_b[:, kc], self.b_b[:, kc])
        bwd = _dot_nt(self.cn_b[:, kc], self.dp_b[:, kc])
        return jnp.where(self.lower, fwd, jnp.where(self.upper, bwd, 0.0)).astype(BF16)


def _gla_fwd_call(h1, nw, w_lines, gkw, gkb, hw, w_out, wf, target):
    seq = h1.shape[0]
    tm = min(GLA_FWD_ROW_TILE, seq)
    nt = seq // tm
    cpt = tm // CHUNK
    n_chunks = seq // CHUNK

    lane_tiles = D_MODEL // LANES
    pieces = [(lo, min(lo + WEIGHT_ROWS_PIECE, GLA_IN_WIDTH)) for lo in range(0, GLA_IN_WIDTH, WEIGHT_ROWS_PIECE)]

    def body(h_ref, nw_ref, lines_hbm, gkw_ref, gkb_ref, hw_ref, wout_ref, wf_ref, tgt_ref,
             dh2_ref, proj_ref, o_ref, st_ref, scores_ref, loss_ref, dwf_ref, win_hbm,
             state_ref, win_ref, piece_ref, lines_ref, piece_sem, win_sem):
        i = pl.program_id(0)
        win_copy = pltpu.make_async_copy(win_ref, win_hbm, win_sem)

        def piece_copy(k):
            lo, hi = pieces[k]
            n_lines = (hi - lo) * lane_tiles
            return pltpu.make_async_copy(lines_hbm.at[pl.ds(lo * lane_tiles, n_lines)],
                                         piece_ref.at[k % 2, pl.ds(0, n_lines)], piece_sem.at[k % 2])

        @pl.when(i == 0)
        def _():
            state_ref[...] = jnp.zeros_like(state_ref)
            loss_ref[...] = jnp.zeros_like(loss_ref)
            dwf_ref[...] = jnp.zeros_like(dwf_ref)
            piece_copy(0).start()
            for k, (lo, hi) in enumerate(pieces):
                if k + 1 < len(pieces):
                    piece_copy(k + 1).start()
                piece_copy(k).wait()
                n_lines = (hi - lo) * lane_tiles
                lines_ref[0:n_lines, :] = piece_ref[k % 2, 0:n_lines, :].astype(F32)
                for j in range(lane_tiles):
                    win_ref[lo:hi, j * LANES:(j + 1) * LANES] = lines_ref[pl.ds(j, hi - lo, stride=lane_tiles),
                                                                          :].astype(BF16)
            win_copy.start()

        pl.when(i == nt - 1)(win_copy.wait)

        ht = h_ref[...]
        xhat, _ = _rms(ht)
        n = (xhat * nw_ref[...]).astype(BF16)
        sections = {}
        for name, lo, hi in (("low", GLA_QKVG_WIDTH, GLA_IN_PAD), ("qk", 0, 2 * GLA_KEY_WIDTH),
                             ("v", 2 * GLA_KEY_WIDTH, GLA_QKVG_WIDTH - GLA_VALUE_WIDTH),
                             ("gate", GLA_QKVG_WIDTH - GLA_VALUE_WIDTH, GLA_QKVG_WIDTH)):
            rows = (win_ref[lo:hi, :] if hi <= GLA_IN_WIDTH
                    else _rows_then_zeros(win_ref, lo, GLA_IN_WIDTH, hi - lo))
            sections[name] = _dot_nt(n, rows)
            proj_ref[:, lo:hi] = sections[name]
        low_b = sections["low"].astype(BF16)
        masks = _chunk_masks(tm)
        on_heads = []
        for h in range(GLA_HEADS):
            kc, vc = _kcols(h), _vcols(h)
            g = _GlaTerms(kc, sections["qk"][:, kc], sections["qk"][:, GLA_KEY_WIDTH:][:, kc], sections["v"][:, vc],
                          low_b, gkw_ref, gkb_ref, masks)
            srows = slice(h * GLA_HEAD_V, (h + 1) * GLA_HEAD_V)
            scores = g.scores()
            for b in range(tm // ROW_TILE):
                part = slice(b * ROW_TILE, (b + 1) * ROW_TILE)
                scores_ref[part, h * ROW_TILE:(h + 1) * ROW_TILE] = scores[part, part]
            o_intra = _dot_nn(scores, g.v_b)
            state = state_ref[srows, :]
            o_rows = []
            for j in range(cpt):
                r = _chunk_rows(j)
                st_ref[j, srows, :] = state
                o_rows.append(o_intra[r] + _dot_nt(g.a_b[r], state.astype(BF16)))
                decay = jnp.exp(g.c_last[j])
                state = state * decay + _dot_tn(g.v_b[r], g.kd_b[r])
            state_ref[srows, :] = state
            o_head = jnp.concatenate(o_rows, axis=0)
            o_ref[:, vc] = o_head
            proj_ref[:, GLA_SAVED_Z + kc.start:GLA_SAVED_Z + kc.stop] = g.z
            proj_ref[:, GLA_SAVED_C + kc.start:GLA_SAVED_C + kc.stop] = g.c
            on_heads.append(_rms(o_head)[0])
        gate = sections["gate"]
        on = jnp.concatenate(on_heads, axis=1) * hw_ref[...]
        y = (on * (gate * _sigmoid(gate))).astype(BF16)
        h2 = ht + _dot_nn(y, wout_ref[...])
        xhat2, rstd2 = _rms(h2)
        wf_row = wf_ref[...]
        err = xhat2 * wf_row - tgt_ref[...]
        loss_ref[...] += 0.5 * jnp.sum(err * err) / D_MODEL
        dout = err * (1.0 / D_MODEL)
        dwf_ref[...] += jnp.sum(dout * xhat2, axis=0, keepdims=True)
        dh2_ref[...] = _rms_bwd(dout * wf_row, xhat2, rstd2)

    row = lambda i: (i, 0)
    return pl.pallas_call(
        body, name="gla_fwd", grid=(nt,),
        in_specs=[pl.BlockSpec((tm, D_MODEL), row), _const((1, D_MODEL)), pl.BlockSpec(memory_space=pl.ANY),
                  _const((GLA_LOW_PAD, GLA_KEY_WIDTH)), _const((1, GLA_KEY_WIDTH)), _const((1, GLA_VALUE_WIDTH)),
                  _const((GLA_VALUE_WIDTH, D_MODEL)), _const((1, D_MODEL)), pl.BlockSpec((tm, D_MODEL), row)],
        out_specs=[pl.BlockSpec((tm, D_MODEL), row), pl.BlockSpec((tm, GLA_SAVED_WIDTH), row),
                   pl.BlockSpec((tm, GLA_VALUE_WIDTH), row),
                   pl.BlockSpec((cpt, GLA_VALUE_WIDTH, GLA_HEAD_K), lambda i: (i, 0, 0)),
                   pl.BlockSpec((tm, GLA_HEADS * ROW_TILE), row), _full((8, LANES)), _full((1, D_MODEL)),
                   pl.BlockSpec(memory_space=pl.ANY)],
        out_shape=[jax.ShapeDtypeStruct((seq, D_MODEL), F32), jax.ShapeDtypeStruct((seq, GLA_SAVED_WIDTH), F32),
                   jax.ShapeDtypeStruct((seq, GLA_VALUE_WIDTH), F32),
                   jax.ShapeDtypeStruct((n_chunks, GLA_VALUE_WIDTH, GLA_HEAD_K), F32),
                   jax.ShapeDtypeStruct((seq, GLA_HEADS * ROW_TILE), BF16),
                   jax.ShapeDtypeStruct((8, LANES), F32), jax.ShapeDtypeStruct((1, D_MODEL), F32),
                   jax.ShapeDtypeStruct((GLA_IN_WIDTH, D_MODEL), BF16)],
        scratch_shapes=[pltpu.VMEM((GLA_VALUE_WIDTH, GLA_HEAD_K), F32), pltpu.VMEM((GLA_IN_WIDTH, D_MODEL), BF16),
                        pltpu.VMEM((2, WEIGHT_ROWS_PIECE * lane_tiles, LANES), BF16),
                        pltpu.VMEM((WEIGHT_ROWS_PIECE * lane_tiles, LANES), F32), pltpu.SemaphoreType.DMA((2,)),
                        pltpu.SemaphoreType.DMA(())],
        compiler_params=_params(),
    )(h1, nw, w_lines, gkw, gkb, hw, w_out, wf, target)


def _gla_bwd_call(dh2, proj, o, states, scores, gkw, gkb, hw, w_out):
    seq = dh2.shape[0]
    tm = ROW_TILE
    nt = seq // tm
    cpt = tm // CHUNK

    def body(dh_ref, proj_ref, o_ref, st_ref, scores_ref, gkw_ref, gkb_ref, hw_ref, wout_ref,
             dproj_ref, dwout_hbm, dhw_ref, dgkw_ref, dgkb_ref, dstate_ref, dwout_acc, dwout_stage):
        i = pl.program_id(0)

        @pl.when(i == 0)
        def _():
            dstate_ref[...] = jnp.zeros_like(dstate_ref)
            dwout_acc[...] = jnp.zeros_like(dwout_acc)
            dhw_ref[...] = jnp.zeros_like(dhw_ref)
            dgkw_ref[...] = jnp.zeros_like(dgkw_ref)
            dgkb_ref[...] = jnp.zeros_like(dgkb_ref)

        dhb = dh_ref[...].astype(BF16)
        dy = _dot_nt(dhb, wout_ref[...])
        v0, g0 = 2 * GLA_KEY_WIDTH, GLA_QKVG_WIDTH - GLA_VALUE_WIDTH
        gate = proj_ref[:, g0:GLA_QKVG_WIDTH]
        low_b = proj_ref[:, GLA_QKVG_WIDTH:GLA_IN_PAD].astype(BF16)
        o = o_ref[...]
        hw_row = hw_ref[...]
        sg = _sigmoid(gate)
        silu = gate * sg
        don = dy * silu
        on_parts, do_parts, dhw_parts = [], [], []
        for h in range(GLA_HEADS):
            vc = _vcols(h)
            xh, rs = _rms(o[:, vc])
            on_parts.append(xh * hw_row[:, vc])
            dhw_parts.append(jnp.sum(don[:, vc] * xh, axis=0, keepdims=True))
            do_parts.append(_rms_bwd(don[:, vc] * hw_row[:, vc], xh, rs).astype(BF16))
        on = jnp.concatenate(on_parts, axis=1)
        dwout_acc[...] += _dot_tn((on * silu).astype(BF16), dhb)
        dhw_ref[...] += jnp.concatenate(dhw_parts, axis=1)
        dproj_ref[:, g0:GLA_QKVG_WIDTH] = (dy * on * (sg * (1.0 + gate * (1.0 - sg)))).astype(BF16)

        last_row = lax.broadcasted_iota(jnp.int32, (CHUNK, 1), 0) == CHUNK - 1
        g = _GlaTerms(slice(0, GLA_KEY_WIDTH), proj_ref[:, :GLA_KEY_WIDTH], proj_ref[:, GLA_KEY_WIDTH:v0],
                      proj_ref[:, v0:g0], low_b, gkw_ref, gkb_ref, _chunk_masks(tm),
                      saved=(proj_ref[:, GLA_SAVED_Z:GLA_SAVED_C], proj_ref[:, GLA_SAVED_C:GLA_SAVED_WIDTH]))
        dc_h = []
        for h in range(GLA_HEADS):
            kc, vc = _kcols(h), _vcols(h)
            k_cols = slice(GLA_KEY_WIDTH + kc.start, GLA_KEY_WIDTH + kc.stop)
            v_cols = slice(v0 + vc.start, v0 + vc.stop)
            do_h = do_parts[h]
            srows = slice(h * GLA_HEAD_V, (h + 1) * GLA_HEAD_V)
            scores = scores_ref[:, h * ROW_TILE:(h + 1) * ROW_TILE]
            dscores = _dot_nt(do_h, g.v_b[:, vc])
            dfwd = jnp.where(g.lower, dscores, 0.0).astype(BF16)
            dbwd = jnp.where(g.upper, dscores, 0.0).astype(BF16)
            dv_intra = _dot_tn(scores, do_h)
            da_intra = _dot_nn(dfwd, g.b_b[:, kc])
            db = _dot_tn(dfwd, g.a_b[:, kc])
            dcn = _dot_nn(dbwd, g.dp_b[:, kc])
            ddp = _dot_tn(dbwd, g.cn_b[:, kc])
            dstate = dstate_ref[srows, :]
            da_rows, dkd_rows, dv_rows, dcl_rows = [None] * cpt, [None] * cpt, [None] * cpt, [None] * cpt
            for j in reversed(range(cpt)):
                r = _chunk_rows(j)
                state = st_ref[j, srows, :]
                dstate_b = dstate.astype(BF16)
                do_c = do_h[r]
                dv_rows[j] = dv_intra[r] + _dot_nt(g.kd_b[r, kc], dstate_b)
                da_rows[j] = da_intra[r] + _dot_nn(do_c, state.astype(BF16))
                dkd = _dot_nn(g.v_b[r, vc], dstate_b) * g.e_rest[r, kc]
                dkd_rows[j] = dkd
                decay = jnp.exp(g.c_last[j][:, kc])
                dc_last = (jnp.sum(dkd * g.k[r, kc], axis=0, keepdims=True)
                           + decay * jnp.sum(state * dstate, axis=0, keepdims=True))
                dcl_rows[j] = jnp.where(last_row, dc_last, 0.0)
                dstate = _dot_tn(do_c, g.a_b[r, kc]) + dstate * decay
            dstate_ref[srows, :] = dstate
            da = jnp.concatenate(da_rows, axis=0)
            dkd = jnp.concatenate(dkd_rows, axis=0)
            dproj_ref[:, v_cols] = jnp.concatenate(dv_rows, axis=0).astype(BF16)
            q_up, q_down = da * g.e_pos[:, kc], dcn * g.e_neg[:, kc]
            k_up, k_down = ddp * g.e_pos[:, kc], db * g.e_neg[:, kc] + dkd
            dproj_ref[:, kc] = (Q_SCALE * (q_up + q_down)).astype(BF16)
            dproj_ref[:, k_cols] = (k_up + k_down).astype(BF16)
            dc_h.append(g.q[:, kc] * (q_up - q_down) + g.k[:, kc] * (k_up - k_down)
                        + jnp.concatenate(dcl_rows, axis=0))
        dz = _chunk_scan(jnp.concatenate(dc_h, axis=1), True) * (1.0 / GATE_NORMALIZER) * (1.0 - _sigmoid(g.z))
        dzb = dz.astype(BF16)
        dgkb_ref[...] += jnp.sum(dz, axis=0, keepdims=True)
        dgkw_ref[...] += _dot_tn(low_b, dzb)
        dproj_ref[:, GLA_QKVG_WIDTH:] = _dot_nt(dzb, gkw_ref[...]).astype(BF16)

        @pl.when(i == nt - 1)
        def _():
            dwout_stage[...] = dwout_acc[...].astype(BF16)
            pltpu.sync_copy(dwout_stage, dwout_hbm)

    rev = lambda i: (nt - 1 - i, 0)
    return pl.pallas_call(
        body, name="gla_bwd", grid=(nt,),
        in_specs=[pl.BlockSpec((tm, D_MODEL), rev), pl.BlockSpec((tm, GLA_SAVED_WIDTH), rev),
                  pl.BlockSpec((tm, GLA_VALUE_WIDTH), rev),
                  pl.BlockSpec((cpt, GLA_VALUE_WIDTH, GLA_HEAD_K), lambda i: (nt - 1 - i, 0, 0)),
                  pl.BlockSpec((tm, GLA_HEADS * ROW_TILE), rev),
                  _const((GLA_LOW_PAD, GLA_KEY_WIDTH)), _const((1, GLA_KEY_WIDTH)), _const((1, GLA_VALUE_WIDTH)),
                  _const((GLA_VALUE_WIDTH, D_MODEL))],
        out_specs=[pl.BlockSpec((tm, GLA_IN_PAD), rev), pl.BlockSpec(memory_space=pl.ANY),
                   _full((1, GLA_VALUE_WIDTH)), _full((GLA_LOW_PAD, GLA_KEY_WIDTH)), _full((1, GLA_KEY_WIDTH))],
        out_shape=[jax.ShapeDtypeStruct((seq, GLA_IN_PAD), BF16), jax.ShapeDtypeStruct((GLA_VALUE_WIDTH, D_MODEL), BF16),
                   jax.ShapeDtypeStruct((1, GLA_VALUE_WIDTH), F32), jax.ShapeDtypeStruct((GLA_LOW_PAD, GLA_KEY_WIDTH), F32),
                   jax.ShapeDtypeStruct((1, GLA_KEY_WIDTH), F32)],
        scratch_shapes=[pltpu.VMEM((GLA_VALUE_WIDTH, GLA_HEAD_K), F32), pltpu.VMEM((GLA_VALUE_WIDTH, D_MODEL), F32),
                        pltpu.VMEM((GLA_VALUE_WIDTH, D_MODEL), BF16)],
        compiler_params=_params(),
    )(dh2, proj, o, states, scores, gkw, gkb, hw, w_out)


def _position():
    return lax.axis_index("x"), lax.axis_index("y"), lax.axis_index("c")


def _lead_slot(ref, d):
    return ref.at[d]


def _row_slot(rows):
    return lambda ref, d: ref.at[pl.ds(pl.multiple_of(d * rows, rows), rows)]


def _dim1_slot(size):
    return lambda ref, d: ref.at[:, pl.ds(pl.multiple_of(d * size, size), size)]


class _Gather:
    def __init__(self, in_refs, out_refs, slots, send_sems, recv_sems, local_sems):
        self.in_refs, self.out_refs, self.slots = in_refs, out_refs, slots
        self.send_sems, self.recv_sems, self.local_sems = send_sems, recv_sems, local_sems
        self.n = len(in_refs)
        x, y, c = _position()
        self.c = c
        self.me, self.sibling = (x, y, c), (x, y, 1 - c)
        self.near = [(1 - x, y), (x, 1 - y)]
        self.diagonal = (1 - x, 1 - y)
        self.relay_from = (x ^ c, y ^ (1 - c))
        self.relay_to = (x ^ (1 - c), y ^ c)

    def _copy(self, a, k, block, to, from_input=False):
        part = self.slots[a](self.out_refs[a], 4 * block[0] + 2 * block[1] + block[2])
        return pltpu.make_async_remote_copy(
            src_ref=self.in_refs[a] if from_input else part, dst_ref=part,
            send_sem=self.send_sems.at[a, k], recv_sem=self.recv_sems.at[a, k], device_id=to, device_id_type=MESH)

    def _mine(self):
        return [pltpu.make_async_copy(self.in_refs[a], self.slots[a](self.out_refs[a], 4 * self.me[0] + 2 * self.me[1]
                                                                    + self.me[2]), self.local_sems.at[a])
                for a in range(self.n)]

    def _first(self):
        first = [self._copy(a, 0, self.me, self.sibling, True) for a in range(self.n)]
        return first + [self._copy(a, 1 + j, self.me, (*chip, self.c), True)
                        for j, chip in enumerate(self.near) for a in range(self.n)]

    def _relayed(self):
        return [self._copy(a, 3, (*self.relay_from, self.c), (*self.relay_to, self.c)) for a in range(self.n)]

    def _passed(self, j):
        chip = self.near[j] if j < 2 else self.diagonal
        return [self._copy(a, 4 + j, (*chip, self.c), self.sibling) for a in range(self.n)]

    def start(self):
        for cp in self._mine() + self._first():
            cp.start()

    def forward(self):
        for j, chip in enumerate(self.near):
            for a in range(self.n):
                self._copy(a, 1 + j, (*chip, self.c), self.me).wait_recv()
        for cp in self._relayed() + self._passed(0) + self._passed(1):
            cp.start()

    def relay(self):
        pass

    def finish(self):
        for a in range(self.n):
            self._copy(a, 3, (*self.diagonal, self.c), self.me).wait_recv()
        for cp in self._passed(2):
            cp.start()
        for a in range(self.n):
            self._copy(a, 0, self.sibling, self.me).wait_recv()
        for j, chip in enumerate(self.near + [self.diagonal]):
            for a in range(self.n):
                self._copy(a, 4 + j, (*chip, 1 - self.c), self.me).wait_recv()
        for cp in self._first() + self._relayed() + self._passed(0) + self._passed(1) + self._passed(2):
            cp.wait_send()
        for cp in self._mine():
            cp.wait()


class _Exchange:
    def __init__(self, in_refs, out_refs, slots, send_sems, recv_sems, local_sems):
        self.in_refs, self.out_refs, self.slots = in_refs, out_refs, slots
        self.send_sems, self.recv_sems, self.local_sems = send_sems, recv_sems, local_sems
        self.n = len(in_refs)
        self.pos = _position()

    def _copies(self):
        x, y, c = self.pos
        me = 4 * x + 2 * y + c
        mine = [pltpu.make_async_copy(self.slots[a](self.in_refs[a], me), self.out_refs[a].at[me],
                                      self.local_sems.at[a]) for a in range(self.n)]
        remote = []
        for k in range(1, N_DEV):
            px, py, pc = x ^ (k >> 2), y ^ ((k >> 1) & 1), c ^ (k & 1)
            for a in range(self.n):
                remote.append(pltpu.make_async_remote_copy(
                    src_ref=self.slots[a](self.in_refs[a], 4 * px + 2 * py + pc), dst_ref=self.out_refs[a].at[me],
                    send_sem=self.send_sems.at[a, k - 1], recv_sem=self.recv_sems.at[a, k - 1],
                    device_id=(px, py, pc), device_id_type=MESH))
        return mine, remote

    def start(self):
        mine, remote = self._copies()
        for cp in mine + remote:
            cp.start()

    def forward(self):
        pass

    def relay(self):
        pass

    def finish(self):
        mine, remote = self._copies()
        for cp in remote:
            cp.wait_recv()
        for cp in remote:
            cp.wait_send()
        for cp in mine:
            cp.wait()


class _Rider:
    def __init__(self, kind, arrays, out_shapes, slots, scratch=None, forward_step=None):
        self.kind, self.arrays, self.slots = kind, list(arrays), slots
        self.n = len(self.arrays)
        hbm = pl.BlockSpec(memory_space=pl.ANY)
        self.in_specs = [hbm] * self.n
        self.out_specs = [hbm] * self.n
        self.out_shape = [jax.ShapeDtypeStruct(tuple(s), a.dtype) for s, a in zip(out_shapes, self.arrays)]
        self.scratch = scratch if scratch is not None else [
            pltpu.SemaphoreType.DMA((self.n, 7)), pltpu.SemaphoreType.DMA((self.n, 7)),
            pltpu.SemaphoreType.DMA((self.n,))]
        self.forward_step = forward_step
        self.relay_step = None

    def bind(self, in_refs, out_refs, scratch):
        return self.kind(in_refs, out_refs, self.slots, *scratch)


def _gather_rider(shards, full_shapes, slots, forward_step=None):
    return _Rider(_Gather, shards, full_shapes, slots, None, forward_step)


def _exchange_rider(sends, part_shapes, slots):
    return _Rider(_Exchange, sends, [(N_DEV,) + tuple(s) for s in part_shapes], slots)


def _split_refs(refs, n_in, n_out, n_scratch, rider):
    k = rider.n if rider is not None else 0
    ins, r_ins = refs[:n_in], refs[n_in:n_in + k]
    outs, r_outs = refs[n_in + k:n_in + k + n_out], refs[n_in + k + n_out:n_in + 2 * k + n_out]
    rest = refs[n_in + 2 * k + n_out:]
    scratch, r_scratch = rest[:n_scratch], rest[n_scratch:]
    comm = rider.bind(r_ins, r_outs, r_scratch) if rider is not None else None
    if comm is not None:
        comm.forward_step, comm.relay_step = rider.forward_step, rider.relay_step
    return ins + outs + scratch, comm


def _ride_before(comm, i, nt):
    if comm is not None:
        pl.when(i == 0)(comm.start)
        pl.when(i == (nt - 1 if comm.forward_step is None else min(comm.forward_step, nt - 1)))(comm.forward)
        pl.when(i == (nt - 1 if comm.relay_step is None else min(comm.relay_step, nt - 1)))(comm.relay)


def _ride_after(comm, i, nt):
    if comm is not None:
        pl.when(i == nt - 1)(comm.finish)


def _extend(specs, rider, field):
    return list(specs) + (getattr(rider, field) if rider is not None else [])


def _comm_call(name, rider, cast_from):
    order = sorted(cast_from)

    def body(*refs):
        n = rider.n
        ins, outs, rest = list(refs[:n]), refs[n:2 * n], refs[2 * n:]
        for stage, index in zip(rest[:len(order)], order):
            stage[...] = ins[index][...].astype(BF16)
            ins[index] = stage
        comm = rider.bind(ins, outs, rest[len(order):])
        comm.start()
        comm.forward()
        comm.relay()
        comm.finish()

    vmem = pl.BlockSpec(memory_space=pltpu.VMEM)
    return pl.pallas_call(
        body, name=name, in_specs=[vmem if i in cast_from else spec for i, spec in enumerate(rider.in_specs)],
        out_specs=rider.out_specs, out_shape=rider.out_shape,
        scratch_shapes=[pltpu.VMEM(cast_from[i].shape, BF16) for i in order] + rider.scratch,
        compiler_params=pltpu.CompilerParams(vmem_limit_bytes=VMEM_LIMIT),
    )(*[cast_from.get(i, a) for i, a in enumerate(rider.arrays)])


N_CHIPS = 4


class _TwoLevel:
    def __init__(self, in_refs, out_refs, slots, *scratch):
        self.in_refs, self.out_refs, self.slots = in_refs, out_refs, slots
        self.n = n = len(in_refs)
        self.own_bufs, self.recv_bufs, self.relay_bufs = scratch[:n], scratch[n:2 * n], scratch[2 * n:3 * n]
        self.swap_send, self.swap_recv, self.local_sems, self.chip_send, self.chip_recv = scratch[3 * n:]
        x, y, c = self.pos = _position()
        self.first = (x ^ (1 - c), y ^ c)
        self.second = (x ^ c, y ^ (1 - c))
        self.chip_index = lambda chip: 2 * chip[0] + chip[1]

    def _swap(self):
        x, y, c = self.pos
        return [pltpu.make_async_remote_copy(
            src_ref=self.slots[a](self.in_refs[a], 2 * q + 1 - c), dst_ref=self.recv_bufs[a].at[q],
            send_sem=self.swap_send.at[a, q], recv_sem=self.swap_recv.at[a, q],
            device_id=(x, y, 1 - c), device_id_type=MESH) for a in range(self.n) for q in range(N_CHIPS)]

    def _mine(self):
        c = self.pos[2]
        return [pltpu.make_async_copy(self.slots[a](self.in_refs[a], 2 * q + c), self.own_bufs[a].at[q],
                                      self.local_sems.at[a, q]) for a in range(self.n) for q in range(N_CHIPS)]

    def _to_chip(self, a, k, src, dst, chip):
        return pltpu.make_async_remote_copy(
            src_ref=src, dst_ref=dst, send_sem=self.chip_send.at[a, k], recv_sem=self.chip_recv.at[a, k],
            device_id=(*chip, self.pos[2]), device_id_type=MESH)

    def _first_wave(self):
        x, y, _ = self.pos
        diagonal = self.chip_index((1 - x, 1 - y))
        passed_on = [self._to_chip(a, 1, self.own_bufs[a].at[diagonal], self.relay_bufs[a], self.first)
                     for a in range(self.n)]
        return passed_on + [self._to_chip(a, 0, self.own_bufs[a].at[self.chip_index(self.first)],
                                          self.out_refs[a].at[1], self.first) for a in range(self.n)]

    def _second_wave(self):
        return [self._to_chip(a, 2, self.own_bufs[a].at[self.chip_index(self.second)], self.out_refs[a].at[2],
                              self.second) for a in range(self.n)]

    def _own(self):
        x, y, _ = self.pos
        return [pltpu.make_async_copy(self.own_bufs[a].at[2 * x + y], self.out_refs[a].at[0],
                                      self.local_sems.at[a, N_CHIPS]) for a in range(self.n)]

    def start(self):
        for cp in self._swap() + self._mine():
            cp.start()

    def forward(self):
        swap, mine = self._swap(), self._mine()
        for a in range(self.n):
            for q in range(N_CHIPS):
                mine[a * N_CHIPS + q].wait()
                swap[a * N_CHIPS + q].wait_recv()
                self.own_bufs[a][q] = (self.own_bufs[a][q].astype(F32)
                                       + self.recv_bufs[a][q].astype(F32)).astype(BF16)
        for cp in self._first_wave() + self._own():
            cp.start()

    def relay(self):
        second = self.chip_index(self.second)
        for a in range(self.n):
            self._to_chip(a, 1, self.relay_bufs[a], self.relay_bufs[a], self.first).wait_recv()
            self.own_bufs[a][second] = (self.own_bufs[a][second].astype(F32)
                                        + self.relay_bufs[a][...].astype(F32)).astype(BF16)
        for cp in self._second_wave():
            cp.start()

    def finish(self):
        for a in range(self.n):
            self._to_chip(a, 0, self.out_refs[a].at[1], self.out_refs[a].at[1], self.first).wait_recv()
            self._to_chip(a, 2, self.out_refs[a].at[2], self.out_refs[a].at[2], self.second).wait_recv()
        for cp in self._first_wave() + self._second_wave() + self._swap():
            cp.wait_send()
        for cp in self._own():
            cp.wait()


def _two_level_rider(sends, part_shapes, slots, forward_step=None, relay_step=None):
    n = len(sends)
    bufs = [pltpu.VMEM((N_CHIPS,) + tuple(s), a.dtype) for s, a in zip(part_shapes, sends)]
    relay_bufs = [pltpu.VMEM(tuple(s), a.dtype) for s, a in zip(part_shapes, sends)]
    scratch = bufs + bufs + relay_bufs + [
        pltpu.SemaphoreType.DMA((n, N_CHIPS)), pltpu.SemaphoreType.DMA((n, N_CHIPS)),
        pltpu.SemaphoreType.DMA((n, N_CHIPS + 1)), pltpu.SemaphoreType.DMA((n, 3)), pltpu.SemaphoreType.DMA((n, 3))]
    rider = _Rider(_TwoLevel, sends, [(3,) + tuple(s) for s in part_shapes], slots, scratch, forward_step)
    rider.relay_step = relay_step
    return rider


class _Joined:
    def __init__(self, first, second):
        self.first, self.second = first, second

    def start(self):
        self.first.start()
        self.second.start()

    def forward(self):
        self.first.forward()
        self.second.forward()

    def relay(self):
        self.first.relay()
        self.second.relay()

    def finish(self):
        self.first.finish()
        self.second.finish()


class _JoinedRider:
    def __init__(self, first, second):
        self.first, self.second = first, second
        self.n = first.n + second.n
        self.arrays = first.arrays + second.arrays
        self.in_specs = first.in_specs + second.in_specs
        self.out_specs = first.out_specs + second.out_specs
        self.out_shape = first.out_shape + second.out_shape
        self.scratch = first.scratch + second.scratch
        self.forward_step = first.forward_step
        self.relay_step = first.relay_step

    def bind(self, in_refs, out_refs, scratch):
        k, s = self.first.n, len(self.first.scratch)
        return _Joined(self.first.bind(in_refs[:k], out_refs[:k], scratch[:s]),
                       self.second.bind(in_refs[k:], out_refs[k:], scratch[s:]))


def _adamw(w, g, m, v):
    m = ADAM_B1 * m + (1.0 - ADAM_B1) * g
    v = ADAM_B2 * v + (1.0 - ADAM_B2) * (g * g)
    m_hat = m / (1.0 - ADAM_B1 ** ADAM_STEP)
    v_hat = v / (1.0 - ADAM_B2 ** ADAM_STEP)
    delta = -ADAM_LR * (m_hat / (jnp.sqrt(v_hat) + ADAM_EPS) + ADAM_WD * w)
    return delta, m, v


def _sum_parts(parts_ref, index=()):
    g = parts_ref[(0,) + index].astype(F32)
    for s in range(1, parts_ref.shape[0]):
        g = g + parts_ref[(s,) + index].astype(F32)
    return g


def _adamw_group_call(name, groups):
    k = len(groups)

    def body(*refs):
        ins, outs = refs[:4 * k], refs[4 * k:]
        for i in range(k):
            parts_ref, w_ref, m_ref, v_ref = ins[4 * i:4 * i + 4]
            g = _sum_parts(parts_ref)
            delta, m_new, v_new = _adamw(w_ref[...], g, m_ref[...], v_ref[...])
            for out_ref, value in zip(outs[4 * i:4 * i + 4], (g, delta, m_new, v_new)):
                out_ref[...] = value

    vmem = pl.BlockSpec(memory_space=pltpu.VMEM)
    res = pl.pallas_call(
        body, name=name, in_specs=[vmem] * (4 * k), out_specs=[vmem] * (4 * k),
        out_shape=[jax.ShapeDtypeStruct(grp[1].shape, F32) for grp in groups for _ in range(4)],
        compiler_params=pltpu.CompilerParams(vmem_limit_bytes=VMEM_LIMIT),
    )(*[a for grp in groups for a in grp])
    return [res[4 * i:4 * i + 4] for i in range(k)]


def _adamw_slabs_call(name, parts, w, m, v, rider=None):
    def main(parts_ref, w_ref, m_ref, v_ref, g_ref, delta_ref, m_out, v_out):
        g = _sum_parts(parts_ref)
        delta, m_new, v_new = _adamw(w_ref[...], g, m_ref[...], v_ref[...])
        g_ref[...] = g
        delta_ref[...] = delta
        m_out[...] = m_new
        v_out[...] = v_new

    def body(*refs):
        own, comm = _split_refs(refs, 4, 4, 0, rider)
        if comm is not None:
            comm.start()
        main(*own)
        if comm is not None:
            comm.forward()
            comm.relay()
            comm.finish()

    vmem = pl.BlockSpec(memory_space=pltpu.VMEM)
    return pl.pallas_call(
        body, name=name, in_specs=_extend([vmem] * 4, rider, "in_specs"),
        out_specs=_extend([vmem] * 4, rider, "out_specs"),
        out_shape=_extend([jax.ShapeDtypeStruct(w.shape, F32)] * 4, rider, "out_shape"),
        scratch_shapes=_extend([], rider, "scratch"),
        compiler_params=pltpu.CompilerParams(vmem_limit_bytes=VMEM_LIMIT),
    )(parts, w, m, v, *_extend([], rider, "arrays"))


WIDE_ROWS = 8
NARROW_ROWS = 40
NARROW_GKW_ROW = 8
NARROW_GKB_ROW = 24
NARROW_HW_ROW = 32
GROUP_SHARD = POOL_GROUP_DIM // N_DEV
KEY_SHARD = GLA_KEY_WIDTH // N_DEV
HEAD_V_SHARD = GLA_HEAD_V // N_DEV


def _small_adamw_call(wide, narrow, w, m, v):
    names = ("norm_w", "pool_scale", "final_norm_w", "pool_group_b", "gla_gk_w", "gla_gk_b", "gla_head_norm_w")
    where = {
        "norm_w": (0, slice(0, 2), slice(None)),
        "pool_scale": (0, slice(2, 3), slice(None)),
        "final_norm_w": (0, slice(3, 4), slice(None)),
        "pool_group_b": (1, slice(0, POOL_GROUPS), slice(0, GROUP_SHARD)),
        "gla_gk_w": (1, slice(NARROW_GKW_ROW, NARROW_GKW_ROW + GLA_GATE_RANK), slice(0, KEY_SHARD)),
        "gla_gk_b": (1, slice(NARROW_GKB_ROW, NARROW_GKB_ROW + 1), slice(0, KEY_SHARD)),
        "gla_head_norm_w": (1, slice(NARROW_HW_ROW, NARROW_HW_ROW + 1), slice(0, HEAD_V_SHARD)),
    }
    k = len(names)

    def body(*refs):
        parts = refs[0:2]
        w_refs, m_refs, v_refs = refs[2:2 + k], refs[2 + k:2 + 2 * k], refs[2 + 2 * k:2 + 3 * k]
        outs = refs[2 + 3 * k:]
        loss_ref = outs[0]
        loss_ref[...] = _sum_parts(parts[0], (slice(4, 5), slice(0, 1)))
        for i, name in enumerate(names):
            buf, rows, cols = where[name]
            g = _sum_parts(parts[buf], (rows, cols))
            delta, m_new, v_new = _adamw(w_refs[i][...], g, m_refs[i][...], v_refs[i][...])
            outs[1 + i][...] = g
            outs[1 + k + i][...] = delta
            outs[1 + 2 * k + i][...] = m_new
            outs[1 + 3 * k + i][...] = v_new

    vmem = pl.BlockSpec(memory_space=pltpu.VMEM)
    shapes = [jax.ShapeDtypeStruct(w[n].shape, F32) for n in names]
    res = pl.pallas_call(
        body, name="adamw_small", in_specs=[vmem] * (2 + 3 * k), out_specs=[vmem] * (1 + 4 * k),
        out_shape=[jax.ShapeDtypeStruct((1, 1), F32)] + shapes * 4,
    )(wide, narrow, *[w[n] for n in names], *[m[n] for n in names], *[v[n] for n in names])
    unzip = lambda j: dict(zip(names, res[1 + j * k:1 + (j + 1) * k]))
    return res[0], unzip(0), unzip(1), unzip(2), unzip(3)


def kernel(x, norm_w, pool_in_w, pool_group_w, pool_group_b, pool_scale, pool_out_w, gla_in_w, gla_gk_w, gla_gk_b, gla_head_norm_w, gla_out_w, final_norm_w, loss_target, m_norm_w, m_pool_in_w, m_pool_group_w, m_pool_group_b, m_pool_scale, m_pool_out_w, m_gla_in_w, m_gla_gk_w, m_gla_gk_b, m_gla_head_norm_w, m_gla_out_w, m_final_norm_w, v_norm_w, v_pool_in_w, v_pool_group_w, v_pool_group_b, v_pool_scale, v_pool_out_w, v_gla_in_w, v_gla_gk_w, v_gla_gk_b, v_gla_head_norm_w, v_gla_out_w, v_final_norm_w):
    w = dict(norm_w=norm_w, pool_in_w=pool_in_w, pool_group_w=pool_group_w, pool_group_b=pool_group_b,
             pool_scale=pool_scale, pool_out_w=pool_out_w, gla_in_w=gla_in_w, gla_gk_w=gla_gk_w, gla_gk_b=gla_gk_b,
             gla_head_norm_w=gla_head_norm_w, gla_out_w=gla_out_w, final_norm_w=final_norm_w)
    m = dict(norm_w=m_norm_w, pool_in_w=m_pool_in_w, pool_group_w=m_pool_group_w, pool_group_b=m_pool_group_b,
             pool_scale=m_pool_scale, pool_out_w=m_pool_out_w, gla_in_w=m_gla_in_w, gla_gk_w=m_gla_gk_w,
             gla_gk_b=m_gla_gk_b, gla_head_norm_w=m_gla_head_norm_w, gla_out_w=m_gla_out_w,
             final_norm_w=m_final_norm_w)
    v = dict(norm_w=v_norm_w, pool_in_w=v_pool_in_w, pool_group_w=v_pool_group_w, pool_group_b=v_pool_group_b,
             pool_scale=v_pool_scale, pool_out_w=v_pool_out_w, gla_in_w=v_gla_in_w, gla_gk_w=v_gla_gk_w,
             gla_gk_b=v_gla_gk_b, gla_head_norm_w=v_gla_head_norm_w, gla_out_w=v_gla_out_w,
             final_norm_w=v_final_norm_w)
    col_shard = GLA_IN_WIDTH // N_DEV
    row_shard = D_MODEL // N_DEV

    def lanes(a):
        return jnp.pad(a, [(0, 0)] * (a.ndim - 1) + [(0, LANES - a.shape[-1])])

    small_in = jnp.concatenate([lanes(pool_group_b[0]), lanes(gla_gk_b), lanes(gla_head_norm_w),
                                jnp.zeros((2, LANES), F32)], axis=0)
    in_cols = 2 * POOL_WIDTH // N_DEV
    pool_f32 = [pool_in_w[0], pool_group_w[0], pool_out_w[0]]
    pool_in, pool_gw, pool_out, small_all = _comm_call("pool_weights_all_gather", _gather_rider(
        [jax.ShapeDtypeStruct(a.shape, BF16) for a in pool_f32] + [small_in],
        [(D_MODEL, 2 * POOL_WIDTH), (POOL_GROUPS, POOL_GROUP_DIM, POOL_GROUP_DIM), (POOL_WIDTH, D_MODEL),
         (N_DEV, 8, LANES)],
        [_dim1_slot(in_cols), _dim1_slot(GROUP_SHARD), _row_slot(row_shard), _lead_slot]), dict(enumerate(pool_f32)))
    pool_gb = jnp.transpose(small_all[:, 0:POOL_GROUPS, :GROUP_SHARD], (1, 0, 2)).reshape(1, POOL_WIDTH)
    gla_gkb = small_all[:, POOL_GROUPS, :KEY_SHARD].reshape(1, GLA_KEY_WIDTH)
    gla_hw = jnp.tile(small_all[:, POOL_GROUPS + 1, :HEAD_V_SHARD].reshape(1, GLA_HEAD_V), (1, GLA_HEADS))
    nw0, nw1, wf = norm_w[0:1], norm_w[1:2], final_norm_w.reshape(1, D_MODEL)
    xs, target = x[0], loss_target[0]

    slabs = col_shard * D_MODEL // (BF16_ROWS * LANES)
    as_slabs = lambda t: jnp.transpose(t[0]).reshape(slabs, BF16_ROWS, LANES)
    h1, pool_y, pool_silu, pool_dsilu, pooled, mixed, gla_in_parts, gkw_parts, gla_out = _pool_fwd_call(
        xs, nw0, pool_in, pool_gw, pool_gb, pool_scale, pool_out, _gather_rider(
            [as_slabs(gla_in_w).astype(BF16), gla_gk_w[0].astype(BF16), gla_out_w[0].astype(BF16)],
            [(N_DEV, slabs, BF16_ROWS, LANES), (N_DEV, GLA_GATE_RANK, KEY_SHARD), (GLA_VALUE_WIDTH, D_MODEL)],
            [_lead_slot, _lead_slot, _row_slot(row_shard)], GATHER_RELAY_STEP))
    gla_gkw = jnp.pad(jnp.transpose(gkw_parts, (1, 0, 2)).reshape(GLA_GATE_RANK, GLA_KEY_WIDTH),
                      ((0, GLA_LOW_PAD - GLA_GATE_RANK), (0, 0)))
    dh2, proj, o, states, scores, loss_part, dwf, gla_in = _gla_fwd_call(
        h1, nw1, gla_in_parts.reshape(GLA_IN_WIDTH * D_MODEL // LANES, LANES), gla_gkw, gla_gkb, gla_hw, gla_out,
        wf, target)

    dproj, d_gla_out, dhw, dgkw, dgkb = _gla_bwd_call(dh2, proj, o, states, scores, gla_gkw, gla_gkb, gla_hw,
                                                      gla_out)
    dh1, d_gla_in, dnw1, landed_gla_out = _inproj_bwd_call(
        "gla_in_bwd", dproj, h1, nw1, gla_in, dh2,
        _exchange_rider([d_gla_out], [(row_shard, D_MODEL)], [_row_slot(row_shard)]), transposed=True)
    gla_in_send = d_gla_in.reshape(N_DEV, slabs, BF16_ROWS, LANES)
    dp, d_pool_out, dgw, dgb, dsc, landed_gla_in = _pool_bwd_call(
        dh1, pool_y, pool_silu, pool_dsilu, pooled, mixed, pool_gw, pool_scale, pool_out,
        _two_level_rider([gla_in_send], [(slabs, BF16_ROWS, LANES)], [_lead_slot], TWO_LEVEL_ADD_STEP,
                         TWO_LEVEL_RELAY_STEP))
    grad_x, d_pool_in, dnw0 = _inproj_bwd_call("pool_in_bwd", dp, xs, nw0, pool_in, dh1)

    wide = jnp.concatenate([
        dnw0, dnw1, dsc, dwf, jnp.pad(loss_part[0:1, 0:1], ((0, 0), (0, D_MODEL - 1))),
        jnp.zeros((WIDE_ROWS - 5, D_MODEL), F32)], axis=0)

    def rows8(a):
        return jnp.pad(lanes(a), ((0, 0), (0, -a.shape[1] % 8), (0, 0)))

    narrow = jnp.concatenate([
        rows8(jnp.transpose(dgb.reshape(POOL_GROUPS, N_DEV, GROUP_SHARD), (1, 0, 2))),
        rows8(jnp.transpose(dgkw[:GLA_GATE_RANK].reshape(GLA_GATE_RANK, N_DEV, KEY_SHARD), (1, 0, 2))),
        rows8(dgkb.reshape(N_DEV, 1, KEY_SHARD)),
        rows8(dhw.reshape(GLA_HEADS, GLA_HEAD_V).sum(axis=0).reshape(N_DEV, 1, HEAD_V_SHARD)),
    ], axis=1)
    last_exchange = _JoinedRider(
        _two_level_rider([d_pool_in, d_pool_out, dgw],
                         [(D_MODEL, in_cols), (row_shard, D_MODEL), (POOL_GROUPS, GROUP_SHARD, POOL_GROUP_DIM)],
                         [_dim1_slot(in_cols), _row_slot(row_shard), _dim1_slot(GROUP_SHARD)]),
        _exchange_rider([wide, narrow], [(WIDE_ROWS, D_MODEL), (NARROW_ROWS, LANES)],
                        [lambda ref, d: ref, _lead_slot]))

    res = {}
    *outs, landed_pool_in, landed_pool_out, landed_gw, landed_wide, landed_narrow = _adamw_slabs_call(
        "adamw_gla_in_w", landed_gla_in, as_slabs(gla_in_w), as_slabs(m_gla_in_w), as_slabs(v_gla_in_w),
        last_exchange)
    res["gla_in_w"] = [jnp.transpose(t.reshape(col_shard, D_MODEL))[None] for t in outs]
    rest = [("pool_in_w", landed_pool_in, (D_MODEL, in_cols)),
            ("pool_group_w", landed_gw, (POOL_GROUPS * GROUP_SHARD, POOL_GROUP_DIM)),
            ("pool_out_w", landed_pool_out, (row_shard, D_MODEL)), ("gla_out_w", landed_gla_out, (row_shard, D_MODEL))]
    updates = _adamw_group_call("adamw_matrices", [
        (parts.reshape((parts.shape[0],) + shape), w[name].reshape(shape), m[name].reshape(shape),
         v[name].reshape(shape)) for name, parts, shape in rest])
    for (name, _, _), outs in zip(rest, updates):
        res[name] = [t.reshape(w[name].shape) for t in outs]
    small_shapes ={"norm_w": (2, D_MODEL), "pool_scale": (1, D_MODEL), "final_norm_w": (1, D_MODEL),
                    "pool_group_b": (POOL_GROUPS, GROUP_SHARD), "gla_gk_w": (GLA_GATE_RANK, KEY_SHARD),
                    "gla_gk_b": (1, KEY_SHARD), "gla_head_norm_w": (1, HEAD_V_SHARD)}
    as_small = lambda t: {n: t[n].reshape(s) for n, s in small_shapes.items()}
    loss, *small_outs = _small_adamw_call(landed_wide, landed_narrow, as_small(w), as_small(m), as_small(v))
    for name in small_shapes:
        res[name] = [t[name].reshape(w[name].shape) for t in small_outs]
    order = ("norm_w", "pool_in_w", "pool_group_w", "pool_group_b", "pool_scale", "pool_out_w", "gla_in_w",
             "gla_gk_w", "gla_gk_b", "gla_head_norm_w", "gla_out_w", "final_norm_w")
    return (loss.reshape(()), grad_x[None], *[res[n][0] for n in order], *[res[n][1] for n in order],
            *[res[n][2] for n in order], *[res[n][3] for n in order])
```
